```python
import jax, jax.numpy as jnp
from jax import lax
import numpy as np

D_MODEL = 1024
BATCH = 8
SEQ = 4096
DEPTH = 2

CHUNK = 64
N_MIXERS = 2
N_A = (DEPTH + 1) // 2
N_B = DEPTH // 2
SHORT_CONV_WIDTH = 3
CONFORMER_CONV_WIDTH = 31
D_FF = ((8 * D_MODEL // 3 + 255) // 256) * 256
RMS_EPS = 1e-6
LN_EPS = 1e-5

kernel_name = "hybrid_shortconv_conformer_conv_encoder"


def rms_norm(x, g):
    xf = x.astype(jnp.float32)
    y = xf * lax.rsqrt(jnp.mean(xf * xf, axis=-1, keepdims=True) + RMS_EPS)
    return (y * g.astype(jnp.float32)).astype(x.dtype)


def layer_norm(x, g, b):
    xf = x.astype(jnp.float32)
    mu = jnp.mean(xf, axis=-1, keepdims=True)
    var = jnp.mean(jnp.square(xf - mu), axis=-1, keepdims=True)
    y = (xf - mu) * lax.rsqrt(var + LN_EPS)
    return (y * g.astype(jnp.float32) + b.astype(jnp.float32)).astype(x.dtype)


def causal_depthwise_conv(x, w):
    k = w.shape[0]
    return lax.conv_general_dilated(
        x, w[:, None, :].astype(x.dtype), window_strides=(1,),
        padding=[(k - 1, 0)], dimension_numbers=("NWC", "WIO", "NWC"),
        feature_group_count=x.shape[-1])


def short_gated_conv(h, w_in, w_conv, w_out):
    bcv = jnp.einsum("bsd,de->bse", h, w_in)
    gate_b, gate_c, v = jnp.split(bcv, 3, axis=-1)
    y = gate_b * causal_depthwise_conv(gate_c * v, w_conv)
    return jnp.einsum("bsd,de->bse", y, w_out)


def conformer_conv_module(h, w_pw1, b_pw1, w_dw, b_dw, ln_g, ln_b, w_pw2, b_pw2):
    u = jnp.einsum("bsd,de->bse", h, w_pw1) + b_pw1
    a, g = jnp.split(u, 2, axis=-1)
    u = a * jax.nn.sigmoid(g)
    u = causal_depthwise_conv(u, w_dw) + b_dw
    u = jax.nn.silu(layer_norm(u, ln_g, ln_b))
    return jnp.einsum("bsd,de->bse", u, w_pw2) + b_pw2


def swiglu(h, w_gate, w_up, w_down):
    gu = jax.nn.silu(jnp.einsum("bsd,df->bsf", h, w_gate)) * jnp.einsum("bsd,df->bsf", h, w_up)
    return jnp.einsum("bsf,fd->bsd", gu, w_down)


def _fwd_setup_inputs(seed: int = 0) -> dict:
    key = jax.random.key(seed)
    ks = jax.random.split(key, 24)
    D, F = D_MODEL, D_FF
    nrm = lambda k, shape, fan_in: jax.random.normal(k, shape, jnp.float32) * (fan_in ** -0.5)
    gain = lambda k, shape: 1.0 + 0.02 * jax.random.normal(k, shape, jnp.float32)
    small = lambda k, shape: 0.02 * jax.random.normal(k, shape, jnp.float32)
    return {
        "x": jax.random.normal(ks[0], (BATCH, SEQ, D), jnp.float32),
        "a_norm": gain(ks[1], (N_A, D)),
        "a_w_in": nrm(ks[2], (N_A, D, 3 * D), D),
        "a_conv": nrm(ks[3], (N_A, SHORT_CONV_WIDTH, D), SHORT_CONV_WIDTH),
        "a_w_out": nrm(ks[4], (N_A, D, D), D),
        "b_norm": gain(ks[5], (N_B, D)),
        "b_w_pw1": nrm(ks[6], (N_B, D, 2 * D), D),
        "b_b_pw1": small(ks[7], (N_B, 2 * D)),
        "b_conv": nrm(ks[8], (N_B, CONFORMER_CONV_WIDTH, D), CONFORMER_CONV_WIDTH),
        "b_b_conv": small(ks[9], (N_B, D)),
        "b_ln_g": gain(ks[10], (N_B, D)),
        "b_ln_b": small(ks[11], (N_B, D)),
        "b_w_pw2": nrm(ks[12], (N_B, D, D), D),
        "b_b_pw2": small(ks[13], (N_B, D)),
        "ffn_norm": gain(ks[14], (DEPTH, D)),
        "ffn_w_gate": nrm(ks[15], (DEPTH, D, F), D),
        "ffn_w_up": nrm(ks[16], (DEPTH, D, F), D),
        "ffn_w_down": nrm(ks[17], (DEPTH, F, D), F),
        "final_norm": gain(ks[18], (D,)),
    }


def _fwd_reference(x, a_norm, a_w_in, a_conv, a_w_out,
              b_norm, b_w_pw1, b_b_pw1, b_conv, b_b_conv, b_ln_g, b_ln_b, b_w_pw2, b_b_pw2,
              ffn_norm, ffn_w_gate, ffn_w_up, ffn_w_down, final_norm):
    h = x
    for i in range(DEPTH):
        j = i // N_MIXERS
        if i % N_MIXERS == 0:
            h = h + short_gated_conv(rms_norm(h, a_norm[j]), a_w_in[j], a_conv[j], a_w_out[j])
        else:
            h = h + conformer_conv_module(
                rms_norm(h, b_norm[j]), b_w_pw1[j], b_b_pw1[j], b_conv[j], b_b_conv[j],
                b_ln_g[j], b_ln_b[j], b_w_pw2[j], b_b_pw2[j])
        h = h + swiglu(rms_norm(h, ffn_norm[i]), ffn_w_gate[i], ffn_w_up[i], ffn_w_down[i])
    return rms_norm(h, final_norm)


import jax as _jax
import jax.numpy as _jnp

TWIN_FORMAT = 'train_step'
FWD_PARAMS = ['x', 'a_norm', 'a_w_in', 'a_conv', 'a_w_out', 'b_norm', 'b_w_pw1', 'b_b_pw1', 'b_conv', 'b_b_conv', 'b_ln_g', 'b_ln_b', 'b_w_pw2', 'b_b_pw2', 'ffn_norm', 'ffn_w_gate', 'ffn_w_up', 'ffn_w_down', 'final_norm']
TWIN_WEIGHTS = ['a_norm', 'a_w_in', 'a_conv', 'a_w_out', 'b_norm', 'b_w_pw1', 'b_b_pw1', 'b_conv', 'b_b_conv', 'b_ln_g', 'b_ln_b', 'b_w_pw2', 'b_b_pw2', 'ffn_norm', 'ffn_w_gate', 'ffn_w_up', 'ffn_w_down', 'final_norm']
TWIN_DIFF_INPUT = 'x'
TWIN_INPUTS = ['x', 'a_norm', 'a_w_in', 'a_conv', 'a_w_out', 'b_norm', 'b_w_pw1', 'b_b_pw1', 'b_conv', 'b_b_conv', 'b_ln_g', 'b_ln_b', 'b_w_pw2', 'b_b_pw2', 'ffn_norm', 'ffn_w_gate', 'ffn_w_up', 'ffn_w_down', 'final_norm', 'loss_target', 'm_a_norm', 'm_a_w_in', 'm_a_conv', 'm_a_w_out', 'm_b_norm', 'm_b_w_pw1', 'm_b_b_pw1', 'm_b_conv', 'm_b_b_conv', 'm_b_ln_g', 'm_b_ln_b', 'm_b_w_pw2', 'm_b_b_pw2', 'm_ffn_norm', 'm_ffn_w_gate', 'm_ffn_w_up', 'm_ffn_w_down', 'm_final_norm', 'v_a_norm', 'v_a_w_in', 'v_a_conv', 'v_a_w_out', 'v_b_norm', 'v_b_w_pw1', 'v_b_b_pw1', 'v_b_conv', 'v_b_b_conv', 'v_b_ln_g', 'v_b_ln_b', 'v_b_w_pw2', 'v_b_b_pw2', 'v_ffn_norm', 'v_ffn_w_gate', 'v_ffn_w_up', 'v_ffn_w_down', 'v_final_norm']
TWIN_OUTPUTS = ['loss', 'grad_x', 'grad_a_norm', 'grad_a_w_in', 'grad_a_conv', 'grad_a_w_out', 'grad_b_norm', 'grad_b_w_pw1', 'grad_b_b_pw1', 'grad_b_conv', 'grad_b_b_conv', 'grad_b_ln_g', 'grad_b_ln_b', 'grad_b_w_pw2', 'grad_b_b_pw2', 'grad_ffn_norm', 'grad_ffn_w_gate', 'grad_ffn_w_up', 'grad_ffn_w_down', 'grad_final_norm', 'delta_a_norm', 'delta_a_w_in', 'delta_a_conv', 'delta_a_w_out', 'delta_b_norm', 'delta_b_w_pw1', 'delta_b_b_pw1', 'delta_b_conv', 'delta_b_b_conv', 'delta_b_ln_g', 'delta_b_ln_b', 'delta_b_w_pw2', 'delta_b_b_pw2', 'delta_ffn_norm', 'delta_ffn_w_gate', 'delta_ffn_w_up', 'delta_ffn_w_down', 'delta_final_norm', 'new_m_a_norm', 'new_m_a_w_in', 'new_m_a_conv', 'new_m_a_w_out', 'new_m_b_norm', 'new_m_b_w_pw1', 'new_m_b_b_pw1', 'new_m_b_conv', 'new_m_b_b_conv', 'new_m_b_ln_g', 'new_m_b_ln_b', 'new_m_b_w_pw2', 'new_m_b_b_pw2', 'new_m_ffn_norm', 'new_m_ffn_w_gate', 'new_m_ffn_w_up', 'new_m_ffn_w_down', 'new_m_final_norm', 'new_v_a_norm', 'new_v_a_w_in', 'new_v_a_conv', 'new_v_a_w_out', 'new_v_b_norm', 'new_v_b_w_pw1', 'new_v_b_b_pw1', 'new_v_b_conv', 'new_v_b_b_conv', 'new_v_b_ln_g', 'new_v_b_ln_b', 'new_v_b_w_pw2', 'new_v_b_b_pw2', 'new_v_ffn_norm', 'new_v_ffn_w_gate', 'new_v_ffn_w_up', 'new_v_ffn_w_down', 'new_v_final_norm']
TWIN_LEAF_KINDS = {'loss': 'loss', 'grad_x': 'grad_x', 'grad_a_norm': 'grad_w', 'grad_a_w_in': 'grad_w', 'grad_a_conv': 'grad_w', 'grad_a_w_out': 'grad_w', 'grad_b_norm': 'grad_w', 'grad_b_w_pw1': 'grad_w', 'grad_b_b_pw1': 'grad_w', 'grad_b_conv': 'grad_w', 'grad_b_b_conv': 'grad_w', 'grad_b_ln_g': 'grad_w', 'grad_b_ln_b': 'grad_w', 'grad_b_w_pw2': 'grad_w', 'grad_b_b_pw2': 'grad_w', 'grad_ffn_norm': 'grad_w', 'grad_ffn_w_gate': 'grad_w', 'grad_ffn_w_up': 'grad_w', 'grad_ffn_w_down': 'grad_w', 'grad_final_norm': 'grad_w', 'delta_a_norm': 'delta_w', 'delta_a_w_in': 'delta_w', 'delta_a_conv': 'delta_w', 'delta_a_w_out': 'delta_w', 'delta_b_norm': 'delta_w', 'delta_b_w_pw1': 'delta_w', 'delta_b_b_pw1': 'delta_w', 'delta_b_conv': 'delta_w', 'delta_b_b_conv': 'delta_w', 'delta_b_ln_g': 'delta_w', 'delta_b_ln_b': 'delta_w', 'delta_b_w_pw2': 'delta_w', 'delta_b_b_pw2': 'delta_w', 'delta_ffn_norm': 'delta_w', 'delta_ffn_w_gate': 'delta_w', 'delta_ffn_w_up': 'delta_w', 'delta_ffn_w_down': 'delta_w', 'delta_final_norm': 'delta_w', 'new_m_a_norm': 'new_m', 'new_m_a_w_in': 'new_m', 'new_m_a_conv': 'new_m', 'new_m_a_w_out': 'new_m', 'new_m_b_norm': 'new_m', 'new_m_b_w_pw1': 'new_m', 'new_m_b_b_pw1': 'new_m', 'new_m_b_conv': 'new_m', 'new_m_b_b_conv': 'new_m', 'new_m_b_ln_g': 'new_m', 'new_m_b_ln_b': 'new_m', 'new_m_b_w_pw2': 'new_m', 'new_m_b_b_pw2': 'new_m', 'new_m_ffn_norm': 'new_m', 'new_m_ffn_w_gate': 'new_m', 'new_m_ffn_w_up': 'new_m', 'new_m_ffn_w_down': 'new_m', 'new_m_final_norm': 'new_m', 'new_v_a_norm': 'new_v', 'new_v_a_w_in': 'new_v', 'new_v_a_conv': 'new_v', 'new_v_a_w_out': 'new_v', 'new_v_b_norm': 'new_v', 'new_v_b_w_pw1': 'new_v', 'new_v_b_b_pw1': 'new_v', 'new_v_b_conv': 'new_v', 'new_v_b_b_conv': 'new_v', 'new_v_b_ln_g': 'new_v', 'new_v_b_ln_b': 'new_v', 'new_v_b_w_pw2': 'new_v', 'new_v_b_b_pw2': 'new_v', 'new_v_ffn_norm': 'new_v', 'new_v_ffn_w_gate': 'new_v', 'new_v_ffn_w_up': 'new_v', 'new_v_ffn_w_down': 'new_v', 'new_v_final_norm': 'new_v'}


def _forward(args):
    return _fwd_reference(*[args[k] for k in FWD_PARAMS])


def _output_shape():
    out = _jax.eval_shape(lambda: _forward(_fwd_setup_inputs(0)))
    return out.shape, out.dtype

N_MICROBATCH = 1
ADAM_LR = 0.001
ADAM_B1 = 0.9
ADAM_B2 = 0.999
ADAM_EPS = 1e-08
ADAM_WD = 0.01
ADAM_STEP = 10
PER_EXAMPLE_BATCH_AXIS = {'x': 0, 'loss_target': 0}
SHARED_INPUTS = []
_WEIGHT_DTYPES = {'a_norm': _jnp.float32, 'a_w_in': _jnp.float32, 'a_conv': _jnp.float32, 'a_w_out': _jnp.float32, 'b_norm': _jnp.float32, 'b_w_pw1': _jnp.float32, 'b_b_pw1': _jnp.float32, 'b_conv': _jnp.float32, 'b_b_conv': _jnp.float32, 'b_ln_g': _jnp.float32, 'b_ln_b': _jnp.float32, 'b_w_pw2': _jnp.float32, 'b_b_pw2': _jnp.float32, 'ffn_norm': _jnp.float32, 'ffn_w_gate': _jnp.float32, 'ffn_w_up': _jnp.float32, 'ffn_w_down': _jnp.float32, 'final_norm': _jnp.float32}
MOMENT_SCALE = {'a_norm': 2.682000e-01, 'a_w_in': 1.446571e-01, 'a_conv': 1.509712e-01, 'a_w_out': 1.444591e-01, 'b_norm': 7.773838e-02, 'b_w_pw1': 5.438086e-02, 'b_b_pw1': 5.719230e-02, 'b_conv': 7.193784e-02, 'b_b_conv': 1.568332e-01, 'b_ln_g': 8.438982e-02, 'b_ln_b': 7.187203e-02, 'b_w_pw2': 6.976892e-02, 'b_b_pw2': 1.250777e-01, 'ffn_norm': 1.034968e-01, 'ffn_w_gate': 4.254013e-02, 'ffn_w_up': 4.121685e-02, 'ffn_w_down': 6.840249e-02, 'final_norm': 3.194128e+01}


def _to_microbatches(a, axis):
    t = _jnp.moveaxis(a, axis, 0)
    t = t.reshape((N_MICROBATCH, t.shape[0] // N_MICROBATCH) + t.shape[1:])
    return _jnp.moveaxis(t, 1, axis + 1)


def setup_inputs(seed: int = 0) -> dict:
    inp = _fwd_setup_inputs(seed)
    key = _jax.random.fold_in(_jax.random.key(seed), 7919)
    shape, _ = _output_shape()
    out = dict(inp)
    out["loss_target"] = _jax.random.normal(_jax.random.fold_in(key, 0), shape, _jnp.float32)
    for i, name in enumerate(TWIN_WEIGHTS):
        w = inp[name].astype(_jnp.float32)
        if MOMENT_SCALE is None:
            s = _jnp.sqrt(_jnp.mean(_jnp.square(w)) + 1e-30)
        else:
            s = MOMENT_SCALE[name]
        km, kv = _jax.random.split(_jax.random.fold_in(key, i + 1))
        out[name] = w
        out["m_" + name] = s * _jax.random.normal(km, w.shape, _jnp.float32)
        out["v_" + name] = (s * s) * _jax.random.uniform(kv, w.shape, _jnp.float32, 0.5, 1.5)
    if N_MICROBATCH > 1:
        for name, axis in PER_EXAMPLE_BATCH_AXIS.items():
            out[name] = _to_microbatches(out[name], axis)
    return {'x': out['x'], 'a_norm': out['a_norm'], 'a_w_in': out['a_w_in'], 'a_conv': out['a_conv'], 'a_w_out': out['a_w_out'], 'b_norm': out['b_norm'], 'b_w_pw1': out['b_w_pw1'], 'b_b_pw1': out['b_b_pw1'], 'b_conv': out['b_conv'], 'b_b_conv': out['b_b_conv'], 'b_ln_g': out['b_ln_g'], 'b_ln_b': out['b_ln_b'], 'b_w_pw2': out['b_w_pw2'], 'b_b_pw2': out['b_b_pw2'], 'ffn_norm': out['ffn_norm'], 'ffn_w_gate': out['ffn_w_gate'], 'ffn_w_up': out['ffn_w_up'], 'ffn_w_down': out['ffn_w_down'], 'final_norm': out['final_norm'], 'loss_target': out['loss_target'], 'm_a_norm': out['m_a_norm'], 'm_a_w_in': out['m_a_w_in'], 'm_a_conv': out['m_a_conv'], 'm_a_w_out': out['m_a_w_out'], 'm_b_norm': out['m_b_norm'], 'm_b_w_pw1': out['m_b_w_pw1'], 'm_b_b_pw1': out['m_b_b_pw1'], 'm_b_conv': out['m_b_conv'], 'm_b_b_conv': out['m_b_b_conv'], 'm_b_ln_g': out['m_b_ln_g'], 'm_b_ln_b': out['m_b_ln_b'], 'm_b_w_pw2': out['m_b_w_pw2'], 'm_b_b_pw2': out['m_b_b_pw2'], 'm_ffn_norm': out['m_ffn_norm'], 'm_ffn_w_gate': out['m_ffn_w_gate'], 'm_ffn_w_up': out['m_ffn_w_up'], 'm_ffn_w_down': out['m_ffn_w_down'], 'm_final_norm': out['m_final_norm'], 'v_a_norm': out['v_a_norm'], 'v_a_w_in': out['v_a_w_in'], 'v_a_conv': out['v_a_conv'], 'v_a_w_out': out['v_a_w_out'], 'v_b_norm': out['v_b_norm'], 'v_b_w_pw1': out['v_b_w_pw1'], 'v_b_b_pw1': out['v_b_b_pw1'], 'v_b_conv': out['v_b_conv'], 'v_b_b_conv': out['v_b_b_conv'], 'v_b_ln_g': out['v_b_ln_g'], 'v_b_ln_b': out['v_b_ln_b'], 'v_b_w_pw2': out['v_b_w_pw2'], 'v_b_b_pw2': out['v_b_b_pw2'], 'v_ffn_norm': out['v_ffn_norm'], 'v_ffn_w_gate': out['v_ffn_w_gate'], 'v_ffn_w_up': out['v_ffn_w_up'], 'v_ffn_w_down': out['v_ffn_w_down'], 'v_final_norm': out['v_final_norm']}


def _loss(weights, diff, rest, loss_target):
    with _jax.named_scope("forward"):
        args = {**rest, TWIN_DIFF_INPUT: diff, **{k: w.astype(_WEIGHT_DTYPES[k]) for k, w in weights.items()}}
        y = _forward(args)
    with _jax.named_scope("loss_head"):
        err = _jnp.square(y.astype(_jnp.float32) - loss_target)
        return 0.5 * _jnp.sum(_jnp.mean(err, axis=-1)) if err.ndim else 0.5 * err


def _adamw(w, g, m, v):
    m = ADAM_B1 * m + (1.0 - ADAM_B1) * g
    v = ADAM_B2 * v + (1.0 - ADAM_B2) * _jnp.square(g)
    m_hat = m / (1.0 - ADAM_B1 ** ADAM_STEP)
    v_hat = v / (1.0 - ADAM_B2 ** ADAM_STEP)
    delta = -ADAM_LR * (m_hat / (_jnp.sqrt(v_hat) + ADAM_EPS) + ADAM_WD * w)
    return delta, m, v


def reference(x, a_norm, a_w_in, a_conv, a_w_out, b_norm, b_w_pw1, b_b_pw1, b_conv, b_b_conv, b_ln_g, b_ln_b, b_w_pw2, b_b_pw2, ffn_norm, ffn_w_gate, ffn_w_up, ffn_w_down, final_norm, loss_target, m_a_norm, m_a_w_in, m_a_conv, m_a_w_out, m_b_norm, m_b_w_pw1, m_b_b_pw1, m_b_conv, m_b_b_conv, m_b_ln_g, m_b_ln_b, m_b_w_pw2, m_b_b_pw2, m_ffn_norm, m_ffn_w_gate, m_ffn_w_up, m_ffn_w_down, m_final_norm, v_a_norm, v_a_w_in, v_a_conv, v_a_w_out, v_b_norm, v_b_w_pw1, v_b_b_pw1, v_b_conv, v_b_b_conv, v_b_ln_g, v_b_ln_b, v_b_w_pw2, v_b_b_pw2, v_ffn_norm, v_ffn_w_gate, v_ffn_w_up, v_ffn_w_down, v_final_norm):
    given = dict(x=x, a_norm=a_norm, a_w_in=a_w_in, a_conv=a_conv, a_w_out=a_w_out, b_norm=b_norm, b_w_pw1=b_w_pw1, b_b_pw1=b_b_pw1, b_conv=b_conv, b_b_conv=b_b_conv, b_ln_g=b_ln_g, b_ln_b=b_ln_b, b_w_pw2=b_w_pw2, b_b_pw2=b_b_pw2, ffn_norm=ffn_norm, ffn_w_gate=ffn_w_gate, ffn_w_up=ffn_w_up, ffn_w_down=ffn_w_down, final_norm=final_norm, loss_target=loss_target, m_a_norm=m_a_norm, m_a_w_in=m_a_w_in, m_a_conv=m_a_conv, m_a_w_out=m_a_w_out, m_b_norm=m_b_norm, m_b_w_pw1=m_b_w_pw1, m_b_b_pw1=m_b_b_pw1, m_b_conv=m_b_conv, m_b_b_conv=m_b_b_conv, m_b_ln_g=m_b_ln_g, m_b_ln_b=m_b_ln_b, m_b_w_pw2=m_b_w_pw2, m_b_b_pw2=m_b_b_pw2, m_ffn_norm=m_ffn_norm, m_ffn_w_gate=m_ffn_w_gate, m_ffn_w_up=m_ffn_w_up, m_ffn_w_down=m_ffn_w_down, m_final_norm=m_final_norm, v_a_norm=v_a_norm, v_a_w_in=v_a_w_in, v_a_conv=v_a_conv, v_a_w_out=v_a_w_out, v_b_norm=v_b_norm, v_b_w_pw1=v_b_w_pw1, v_b_b_pw1=v_b_b_pw1, v_b_conv=v_b_conv, v_b_b_conv=v_b_b_conv, v_b_ln_g=v_b_ln_g, v_b_ln_b=v_b_ln_b, v_b_w_pw2=v_b_w_pw2, v_b_b_pw2=v_b_b_pw2, v_ffn_norm=v_ffn_norm, v_ffn_w_gate=v_ffn_w_gate, v_ffn_w_up=v_ffn_w_up, v_ffn_w_down=v_ffn_w_down, v_final_norm=v_final_norm)
    weights = {n: given[n] for n in TWIN_WEIGHTS}
    shared = {n: given[n] for n in SHARED_INPUTS}
    per_example = {n: given[n] for n in ['x']}
    grad_fn = _jax.value_and_grad(_loss, argnums=(0, 1))

    def one_microbatch(ex, loss_target):
        ex = dict(ex)
        diff = ex.pop(TWIN_DIFF_INPUT)
        return grad_fn(weights, diff, {**shared, **ex}, loss_target)

    if N_MICROBATCH == 1:
        loss, (grad_w, grad_x) = one_microbatch(per_example, given["loss_target"])
    else:
        def body(carry, xs):
            loss_sum, grad_sum = carry
            l_k, (gw_k, gx_k) = one_microbatch(xs[0], xs[1])
            with _jax.named_scope("update"):
                return (loss_sum + l_k, _jax.tree.map(_jnp.add, grad_sum, gw_k)), gx_k

        init = (_jnp.zeros((), _jnp.float32), _jax.tree.map(_jnp.zeros_like, weights))
        (loss, grad_w), grad_x = _jax.lax.scan(body, init, (per_example, given["loss_target"]))
    with _jax.named_scope("update"):
        delta_w, new_m, new_v = {}, {}, {}
        for n in TWIN_WEIGHTS:
            delta_w[n], new_m[n], new_v[n] = _adamw(weights[n], grad_w[n], given["m_" + n], given["v_" + n])
    return (loss, grad_x, *[grad_w[n] for n in TWIN_WEIGHTS], *[delta_w[n] for n in TWIN_WEIGHTS],
            *[new_m[n] for n in TWIN_WEIGHTS], *[new_v[n] for n in TWIN_WEIGHTS])
```

```python
import functools

import jax
import jax.numpy as jnp
from jax import lax
from jax.experimental import pallas as pl
from jax.experimental.pallas import tpu as pltpu

RMS_EPS = 1e-6
LN_EPS = 1e-5
ADAM_LR = 0.001
ADAM_B1 = 0.9
ADAM_B2 = 0.999
ADAM_EPS = 1e-08
ADAM_WD = 0.01
ADAM_STEP = 10

N_CHIPS = 4
N_DEV = 8
LANES = 128
SUBLANES = 8
HALO = 32
CONV_ROWS = 64
TOKEN_TILE = 512
ROW_TILE = 256
VMEM_LIMIT = 48 * 1024 * 1024
MESH = pl.DeviceIdType.MESH
BF16 = jnp.bfloat16
F32 = jnp.float32


def _tile(n, pref):
    t = min(n, pref)
    while n % t:
        t -= SUBLANES
    return t


def _params(sem):
    return pltpu.CompilerParams(dimension_semantics=sem, vmem_limit_bytes=VMEM_LIMIT)


def _sigmoid(x):
    return jax.nn.sigmoid(x)


def rms_fwd(h, gain, name):
    T, D = h.shape
    tm = _tile(T, TOKEN_TILE)

    def body(h_ref, g_ref, o_ref):
        x = h_ref[...]
        r = lax.rsqrt(jnp.mean(x * x, axis=-1, keepdims=True) + RMS_EPS)
        o_ref[...] = (x * r * g_ref[...]).astype(o_ref.dtype)

    return pl.pallas_call(
        body, name=name, grid=(T // tm,),
        in_specs=[pl.BlockSpec((tm, D), lambda i: (i, 0)), pl.BlockSpec((1, D), lambda i: (0, 0))],
        out_specs=pl.BlockSpec((tm, D), lambda i: (i, 0)),
        out_shape=jax.ShapeDtypeStruct((T, D), BF16),
        compiler_params=_params(("parallel",)),
    )(h, gain)


def rms_bwd(dn, h, gain, dres, name):
    T, D = h.shape
    tm = _tile(T, TOKEN_TILE)

    def body(dn_ref, h_ref, g_ref, dres_ref, dh_ref, dg_ref):
        i = pl.program_id(0)
        x = h_ref[...]
        dn_ = dn_ref[...]
        r = lax.rsqrt(jnp.mean(x * x, axis=-1, keepdims=True) + RMS_EPS)
        xhat = x * r
        dxhat = dn_ * g_ref[...]
        dx = r * (dxhat - xhat * jnp.mean(dxhat * xhat, axis=-1, keepdims=True))
        dh_ref[...] = dres_ref[...] + dx
        part = jnp.sum(dn_ * xhat, axis=0, keepdims=True)

        @pl.when(i == 0)
        def _():
            dg_ref[...] = part

        @pl.when(i > 0)
        def _():
            dg_ref[...] += part

    row = pl.BlockSpec((tm, D), lambda i: (i, 0))
    vec = pl.BlockSpec((1, D), lambda i: (0, 0))
    return pl.pallas_call(
        body, name=name, grid=(T // tm,),
        in_specs=[row, row, vec, row], out_specs=[row, vec],
        out_shape=[jax.ShapeDtypeStruct((T, D), F32), jax.ShapeDtypeStruct((1, D), F32)],
        compiler_params=_params(("arbitrary",)),
    )(dn, h, gain, dres)


def loss_head(h, gain, tgt, name):
    T, D = h.shape
    tm = _tile(T, TOKEN_TILE)

    def body(h_ref, g_ref, t_ref, loss_ref, dh_ref, dg_ref):
        i = pl.program_id(0)
        x = h_ref[...]
        g = g_ref[...]
        r = lax.rsqrt(jnp.mean(x * x, axis=-1, keepdims=True) + RMS_EPS)
        xhat = x * r
        diff = xhat * g - t_ref[...]
        part_loss = 0.5 * jnp.sum(jnp.mean(diff * diff, axis=-1, keepdims=True), axis=0, keepdims=True)
        dy = diff * (1.0 / D)
        dxhat = dy * g
        dh_ref[...] = r * (dxhat - xhat * jnp.mean(dxhat * xhat, axis=-1, keepdims=True))
        part = jnp.sum(dy * xhat, axis=0, keepdims=True)

        @pl.when(i == 0)
        def _():
            dg_ref[...] = part
            loss_ref[...] = part_loss

        @pl.when(i > 0)
        def _():
            dg_ref[...] += part
            loss_ref[...] += part_loss

    row = pl.BlockSpec((tm, D), lambda i: (i, 0))
    vec = pl.BlockSpec((1, D), lambda i: (0, 0))
    return pl.pallas_call(
        body, name=name, grid=(T // tm,),
        in_specs=[row, vec, row],
        out_specs=[pl.BlockSpec((1, 1), lambda i: (0, 0)), row, vec],
        out_shape=[jax.ShapeDtypeStruct((1, 1), F32), jax.ShapeDtypeStruct((T, D), F32),
                   jax.ShapeDtypeStruct((1, D), F32)],
        compiler_params=_params(("arbitrary",)),
    )(h, gain, tgt)


def _prev_halo_spec(tm, width):
    return pl.BlockSpec((HALO, width), lambda i: (jnp.maximum(i * (tm // HALO) - 1, 0), 0))


def _next_halo_spec(tm, width, T):
    return pl.BlockSpec((HALO, width), lambda i: (jnp.minimum((i + 1) * (tm // HALO), T // HALO - 1), 0))


def _shifted(win, off, rows):
    if off % SUBLANES == 0:
        return win[off:off + rows]
    n = win.shape[0]
    return pltpu.roll(win, (n - off) % n, 0)[:rows]


def _rowsum8(x):
    acc = x[0:SUBLANES]
    for q in range(1, x.shape[0] // SUBLANES):
        acc = acc + x[q * SUBLANES:(q + 1) * SUBLANES]
    return acc


def _conv_loops(tm, D, per_block):
    def chunk(r, carry):
        t0 = pl.multiple_of(r * CONV_ROWS, CONV_ROWS)
        for lb in range(D // LANES):
            per_block(t0, slice(lb * LANES, (lb + 1) * LANES))
        return carry

    lax.fori_loop(0, tm // CONV_ROWS, chunk, 0)


def gateconv_fwd(bcv, w, name):
    T, D3 = bcv.shape
    D = D3 // 3
    K = w.shape[0]
    tm = _tile(T, TOKEN_TILE)

    def body(x_ref, halo_ref, w_ref, y_ref, pad_ref):
        i = pl.program_id(0)
        pad_ref[HALO:, :] = x_ref[:, D:2 * D] * x_ref[:, 2 * D:]
        pad_ref[:HALO, :] = jnp.where(i > 0, halo_ref[:, D:2 * D] * halo_ref[:, 2 * D:], 0.0)

        def block(t0, ls):
            win = pad_ref[pl.ds(t0, CONV_ROWS + HALO), ls]
            acc = jnp.zeros((CONV_ROWS, LANES), F32)
            for k in range(K):
                acc = acc + w_ref[k:k + 1, ls] * _shifted(win, HALO - (K - 1) + k, CONV_ROWS)
            y_ref[pl.ds(t0, CONV_ROWS), ls] = (x_ref[pl.ds(t0, CONV_ROWS), ls] * acc).astype(y_ref.dtype)

        _conv_loops(tm, D, block)

    return pl.pallas_call(
        body, name=name, grid=(T // tm,),
        in_specs=[pl.BlockSpec((tm, D3), lambda i: (i, 0)), _prev_halo_spec(tm, D3),
                  pl.BlockSpec((K, D), lambda i: (0, 0))],
        out_specs=pl.BlockSpec((tm, D), lambda i: (i, 0)),
        out_shape=jax.ShapeDtypeStruct((T, D), BF16),
        scratch_shapes=[pltpu.VMEM((tm + HALO, D), F32)],
        compiler_params=_params(("parallel",)),
    )(bcv, bcv, w)


def gateconv_bwd(dy, bcv, w, name):
    T, D3 = bcv.shape
    D = D3 // 3
    K = w.shape[0]
    tm = _tile(T, TOKEN_TILE)
    nt = T // tm

    def body(dy_ref, dyn_ref, x_ref, xp_ref, xn_ref, w_ref, o_ref, dw_ref, cv_ref, dc_ref, wacc_ref):
        i = pl.program_id(0)
        cv_ref[HALO:, :] = x_ref[:, D:2 * D] * x_ref[:, 2 * D:]
        cv_ref[:HALO, :] = jnp.where(i > 0, xp_ref[:, D:2 * D] * xp_ref[:, 2 * D:], 0.0)
        dc_ref[:tm, :] = dy_ref[...] * x_ref[:, :D]
        dc_ref[tm:, :] = jnp.where(i < nt - 1, dyn_ref[...] * xn_ref[:, :D], 0.0)

        @pl.when(i == 0)
        def _():
            wacc_ref[...] = jnp.zeros_like(wacc_ref)

        def block(t0, ls):
            cwin = cv_ref[pl.ds(t0, CONV_ROWS + HALO), ls]
            dwin = dc_ref[pl.ds(t0, CONV_ROWS + HALO), ls]
            dcon = dwin[:CONV_ROWS]
            conv = jnp.zeros((CONV_ROWS, LANES), F32)
            dcv = jnp.zeros((CONV_ROWS, LANES), F32)
            for k in range(K):
                wk = w_ref[k:k + 1, ls]
                cs = _shifted(cwin, HALO - (K - 1) + k, CONV_ROWS)
                conv = conv + wk * cs
                dcv = dcv + wk * _shifted(dwin, (K - 1) - k, CONV_ROWS)
                wacc_ref[k * SUBLANES:(k + 1) * SUBLANES, ls] += _rowsum8(dcon * cs)
            rows = pl.ds(t0, CONV_ROWS)
            o_ref[rows, ls] = (dy_ref[rows, ls] * conv).astype(o_ref.dtype)
            o_ref[rows, pl.ds(D + ls.start, LANES)] = (dcv * x_ref[rows, pl.ds(2 * D + ls.start, LANES)]).astype(o_ref.dtype)
            o_ref[rows, pl.ds(2 * D + ls.start, LANES)] = (dcv * x_ref[rows, pl.ds(D + ls.start, LANES)]).astype(o_ref.dtype)

        _conv_loops(tm, D, block)

        @pl.when(i == nt - 1)
        def _():
            for k in range(K):
                dw_ref[k:k + 1, :] = jnp.sum(wacc_ref[k * SUBLANES:(k + 1) * SUBLANES, :], axis=0, keepdims=True)

    return pl.pallas_call(
        body, name=name, grid=(nt,),
        in_specs=[pl.BlockSpec((tm, D), lambda i: (i, 0)), _next_halo_spec(tm, D, T),
                  pl.BlockSpec((tm, D3), lambda i: (i, 0)), _prev_halo_spec(tm, D3), _next_halo_spec(tm, D3, T),
                  pl.BlockSpec((K, D), lambda i: (0, 0))],
        out_specs=[pl.BlockSpec((tm, D3), lambda i: (i, 0)), pl.BlockSpec((K, D), lambda i: (0, 0))],
        out_shape=[jax.ShapeDtypeStruct((T, D3), BF16), jax.ShapeDtypeStruct((K, D), F32)],
        scratch_shapes=[pltpu.VMEM((tm + HALO, D), F32), pltpu.VMEM((tm + HALO, D), F32),
                        pltpu.VMEM((K * SUBLANES, D), F32)],
        compiler_params=_params(("arbitrary",)),
    )(dy, dy, bcv, bcv, bcv, w)


def bconv_fwd(u, w, b_conv, ln_g, ln_b, name):
    T, D2 = u.shape
    D = D2 // 2
    K = w.shape[0]
    tm = _tile(T, TOKEN_TILE)

    def body(u_ref, halo_ref, w_ref, bc_ref, g_ref, b_ref, cu_ref, s_ref, pad_ref):
        i = pl.program_id(0)
        pad_ref[HALO:, :] = u_ref[:, :D] * _sigmoid(u_ref[:, D:])
        pad_ref[:HALO, :] = jnp.where(i > 0, halo_ref[:, :D] * _sigmoid(halo_ref[:, D:]), 0.0)

        def block(t0, ls):
            win = pad_ref[pl.ds(t0, CONV_ROWS + HALO), ls]
            acc = jnp.zeros((CONV_ROWS, LANES), F32)
            for k in range(K):
                acc = acc + w_ref[k:k + 1, ls] * _shifted(win, HALO - (K - 1) + k, CONV_ROWS)
            cu_ref[pl.ds(t0, CONV_ROWS), ls] = acc + bc_ref[:, ls]

        _conv_loops(tm, D, block)
        cu = cu_ref[...]
        mu = jnp.mean(cu, axis=-1, keepdims=True)
        xc = cu - mu
        rstd = lax.rsqrt(jnp.mean(xc * xc, axis=-1, keepdims=True) + LN_EPS)
        ln = xc * rstd * g_ref[...] + b_ref[...]
        s_ref[...] = (ln * _sigmoid(ln)).astype(s_ref.dtype)

    vec = pl.BlockSpec((1, D), lambda i: (0, 0))
    row = pl.BlockSpec((tm, D), lambda i: (i, 0))
    return pl.pallas_call(
        body, name=name, grid=(T // tm,),
        in_specs=[pl.BlockSpec((tm, D2), lambda i: (i, 0)), _prev_halo_spec(tm, D2),
                  pl.BlockSpec((K, D), lambda i: (0, 0)), vec, vec, vec],
        out_specs=[row, row],
        out_shape=[jax.ShapeDtypeStruct((T, D), F32), jax.ShapeDtypeStruct((T, D), BF16)],
        scratch_shapes=[pltpu.VMEM((tm + HALO, D), F32)],
        compiler_params=_params(("parallel",)),
    )(u, u, w, b_conv, ln_g, ln_b)


def ln_silu_bwd(ds, cu, ln_g, ln_b, name):
    T, D = cu.shape
    tm = _tile(T, TOKEN_TILE)

    def body(ds_ref, cu_ref, g_ref, b_ref, dcu_ref, dg_ref, db_ref, dbc_ref):
        i = pl.program_id(0)
        cu_ = cu_ref[...]
        mu = jnp.mean(cu_, axis=-1, keepdims=True)
        xc = cu_ - mu
        rstd = lax.rsqrt(jnp.mean(xc * xc, axis=-1, keepdims=True) + LN_EPS)
        xh = xc * rstd
        ln = xh * g_ref[...] + b_ref[...]
        sg = _sigmoid(ln)
        dl = ds_ref[...] * (sg * (1.0 + ln * (1.0 - sg)))
        dxh = dl * g_ref[...]
        dcu = rstd * (dxh - jnp.mean(dxh, axis=-1, keepdims=True) - xh * jnp.mean(dxh * xh, axis=-1, keepdims=True))
        dcu_ref[...] = dcu
        pg = jnp.sum(dl * xh, axis=0, keepdims=True)
        pb = jnp.sum(dl, axis=0, keepdims=True)
        pc = jnp.sum(dcu, axis=0, keepdims=True)

        @pl.when(i == 0)
        def _():
            dg_ref[...] = pg
            db_ref[...] = pb
            dbc_ref[...] = pc

        @pl.when(i > 0)
        def _():
            dg_ref[...] += pg
            db_ref[...] += pb
            dbc_ref[...] += pc

    vec = pl.BlockSpec((1, D), lambda i: (0, 0))
    row = pl.BlockSpec((tm, D), lambda i: (i, 0))
    vshape = jax.ShapeDtypeStruct((1, D), F32)
    return pl.pallas_call(
        body, name=name, grid=(T // tm,),
        in_specs=[row, row, vec, vec], out_specs=[row, vec, vec, vec],
        out_shape=[jax.ShapeDtypeStruct((T, D), F32), vshape, vshape, vshape],
        compiler_params=_params(("arbitrary",)),
    )(ds, cu, ln_g, ln_b)


def bconv_bwd(dcu, u, w, name):
    T, D2 = u.shape
    D = D2 // 2
    K = w.shape[0]
    tm = _tile(T, TOKEN_TILE)
    nt = T // tm

    def body(dc_ref, dcn_ref, u_ref, up_ref, w_ref, du_ref, dw_ref, db_ref, glu_ref, dpad_ref, dglu_ref, wacc_ref):
        i = pl.program_id(0)
        glu_ref[HALO:, :] = u_ref[:, :D] * _sigmoid(u_ref[:, D:])
        glu_ref[:HALO, :] = jnp.where(i > 0, up_ref[:, :D] * _sigmoid(up_ref[:, D:]), 0.0)
        dpad_ref[:tm, :] = dc_ref[...]
        dpad_ref[tm:, :] = jnp.where(i < nt - 1, dcn_ref[...], 0.0)

        @pl.when(i == 0)
        def _():
            wacc_ref[...] = jnp.zeros_like(wacc_ref)

        def block(t0, ls):
            gwin = glu_ref[pl.ds(t0, CONV_ROWS + HALO), ls]
            dwin = dpad_ref[pl.ds(t0, CONV_ROWS + HALO), ls]
            dcur = dwin[:CONV_ROWS]
            dglu = jnp.zeros((CONV_ROWS, LANES), F32)
            for k in range(K):
                dglu = dglu + w_ref[k:k + 1, ls] * _shifted(dwin, (K - 1) - k, CONV_ROWS)
                gs = _shifted(gwin, HALO - (K - 1) + k, CONV_ROWS)
                wacc_ref[k * SUBLANES:(k + 1) * SUBLANES, ls] += _rowsum8(dcur * gs)
            dglu_ref[pl.ds(t0, CONV_ROWS), ls] = dglu

        _conv_loops(tm, D, block)
        dglu = dglu_ref[...]
        a = u_ref[:, :D]
        sg = _sigmoid(u_ref[:, D:])
        da = dglu * sg
        dg = dglu * a * (sg * (1.0 - sg))
        du_ref[:, :D] = da.astype(du_ref.dtype)
        du_ref[:, D:] = dg.astype(du_ref.dtype)
        pa = jnp.sum(da, axis=0, keepdims=True)
        pg = jnp.sum(dg, axis=0, keepdims=True)

        @pl.when(i == 0)
        def _():
            db_ref[:, :D] = pa
            db_ref[:, D:] = pg

        @pl.when(i > 0)
        def _():
            db_ref[:, :D] += pa
            db_ref[:, D:] += pg

        @pl.when(i == nt - 1)
        def _():
            for k in range(K):
                dw_ref[k:k + 1, :] = jnp.sum(wacc_ref[k * SUBLANES:(k + 1) * SUBLANES, :], axis=0, keepdims=True)

    return pl.pallas_call(
        body, name=name, grid=(nt,),
        in_specs=[pl.BlockSpec((tm, D), lambda i: (i, 0)), _next_halo_spec(tm, D, T),
                  pl.BlockSpec((tm, D2), lambda i: (i, 0)), _prev_halo_spec(tm, D2),
                  pl.BlockSpec((K, D), lambda i: (0, 0))],
        out_specs=[pl.BlockSpec((tm, D2), lambda i: (i, 0)), pl.BlockSpec((K, D), lambda i: (0, 0)),
                   pl.BlockSpec((1, D2), lambda i: (0, 0))],
        out_shape=[jax.ShapeDtypeStruct((T, D2), BF16), jax.ShapeDtypeStruct((K, D), F32),
                   jax.ShapeDtypeStruct((1, D2), F32)],
        scratch_shapes=[pltpu.VMEM((tm + HALO, D), F32), pltpu.VMEM((tm + HALO, D), F32),
                        pltpu.VMEM((tm, D), F32), pltpu.VMEM((K * SUBLANES, D), F32)],
        compiler_params=_params(("arbitrary",)),
    )(dcu, dcu, u, u, w)


def mm_cols(a, w, lyr, bias, name):
    T, K = a.shape
    _, S, _, n = w.shape
    tm = _tile(T, TOKEN_TILE)

    def body(*refs):
        a_ref, w_ref = refs[:2]
        o_ref = refs[-1]
        acc = jnp.dot(a_ref[...], w_ref[...], preferred_element_type=F32)
        if bias is not None:
            acc = acc + refs[2][...]
        o_ref[...] = acc

    in_specs = [pl.BlockSpec((tm, K), lambda i, s: (i, 0)), pl.BlockSpec((None, None, K, n), lambda i, s: (lyr, s, 0, 0))]
    args = [a, w]
    if bias is not None:
        in_specs.append(pl.BlockSpec((1, n), lambda i, s: (0, s)))
        args.append(bias)
    return pl.pallas_call(
        body, name=name, grid=(T // tm, S), in_specs=in_specs,
        out_specs=pl.BlockSpec((tm, n), lambda i, s: (i, s)),
        out_shape=jax.ShapeDtypeStruct((T, S * n), F32),
        compiler_params=_params(("parallel", "parallel")),
    )(*args)


def ffn_up(a, wg, wu, lyr, name):
    T, K = a.shape
    _, S, _, f = wg.shape
    tm = _tile(T, TOKEN_TILE)

    def body(a_ref, wg_ref, wu_ref, g_ref, u_ref, gu_ref):
        av = a_ref[...]
        g = jnp.dot(av, wg_ref[...], preferred_element_type=F32)
        u = jnp.dot(av, wu_ref[...], preferred_element_type=F32)
        g_ref[...] = g
        u_ref[...] = u
        gu_ref[...] = (g * _sigmoid(g) * u).astype(gu_ref.dtype)

    wspec = pl.BlockSpec((None, None, K, f), lambda i, s: (lyr, s, 0, 0))
    seg = pl.BlockSpec((None, tm, f), lambda i, s: (s, i, 0))
    return pl.pallas_call(
        body, name=name, grid=(T // tm, S),
        in_specs=[pl.BlockSpec((tm, K), lambda i, s: (i, 0)), wspec, wspec],
        out_specs=[seg, seg, seg],
        out_shape=[jax.ShapeDtypeStruct((S, T, f), F32), jax.ShapeDtypeStruct((S, T, f), F32),
                   jax.ShapeDtypeStruct((S, T, f), BF16)],
        compiler_params=_params(("parallel", "parallel")),
    )(a, wg, wu)


def mm_rows(a, w, lyr, res, bias, name):
    S, T, k = a.shape
    N = w.shape[-1]
    tm = _tile(T, TOKEN_TILE)

    def body(*refs):
        a_ref, w_ref, r_ref = refs[:3]
        o_ref = refs[-1]
        s = pl.program_id(1)
        acc = jnp.dot(a_ref[...], w_ref[...], preferred_element_type=F32)

        @pl.when(s == 0)
        def _():
            base = r_ref[...]
            if bias is not None:
                base = base + refs[3][...]
            o_ref[...] = base + acc

        @pl.when(s > 0)
        def _():
            o_ref[...] += acc

    in_specs = [pl.BlockSpec((None, tm, k), lambda i, s: (s, i, 0)),
                pl.BlockSpec((None, None, k, N), lambda i, s: (lyr, s, 0, 0)),
                pl.BlockSpec((tm, N), lambda i, s: (i, 0))]
    args = [a, w, res]
    if bias is not None:
        in_specs.append(pl.BlockSpec((1, N), lambda i, s: (0, 0)))
        args.append(bias)
    return pl.pallas_call(
        body, name=name, grid=(T // tm, S), in_specs=in_specs,
        out_specs=pl.BlockSpec((tm, N), lambda i, s: (i, 0)),
        out_shape=jax.ShapeDtypeStruct((T, N), F32),
        compiler_params=_params(("parallel", "arbitrary")),
    )(*args)


_NT = (((1,), (1,)), ((), ()))
_TN = (((0,), (0,)), ((), ()))


def nt_rows(dy, w, lyr, want_colsum, name):
    T, N = dy.shape
    _, S, k, _ = w.shape
    tm = _tile(T, TOKEN_TILE)

    def body(dy_ref, w_ref, o_ref, *rest):
        i, s = pl.program_id(0), pl.program_id(1)
        d = dy_ref[...]
        o_ref[...] = lax.dot_general(d.astype(BF16), w_ref[...], _NT, preferred_element_type=F32)
        if want_colsum:
            cs_ref = rest[0]
            part = jnp.sum(d, axis=0, keepdims=True)

            @pl.when((i == 0) & (s == 0))
            def _():
                cs_ref[...] = part

            @pl.when((i > 0) & (s == 0))
            def _():
                cs_ref[...] += part

    out_specs = [pl.BlockSpec((None, tm, k), lambda i, s: (s, i, 0))]
    out_shape = [jax.ShapeDtypeStruct((S, T, k), F32)]
    if want_colsum:
        out_specs.append(pl.BlockSpec((1, N), lambda i, s: (0, 0)))
        out_shape.append(jax.ShapeDtypeStruct((1, N), F32))
    return pl.pallas_call(
        body, name=name, grid=(T // tm, S),
        in_specs=[pl.BlockSpec((tm, N), lambda i, s: (i, 0)), pl.BlockSpec((None, None, k, N), lambda i, s: (lyr, s, 0, 0))],
        out_specs=out_specs, out_shape=out_shape,
        compiler_params=_params(("arbitrary", "arbitrary")),
    )(dy, w)


def ffn_nt(dy, wd, lyr, g, u, name):
    T, N = dy.shape
    _, S, f, _ = wd.shape
    tm = _tile(T, TOKEN_TILE)

    def body(dy_ref, w_ref, g_ref, u_ref, dg_ref, du_ref):
        dgu = lax.dot_general(dy_ref[...].astype(BF16), w_ref[...], _NT, preferred_element_type=F32)
        gv = g_ref[...]
        sg = _sigmoid(gv)
        dg_ref[...] = (dgu * u_ref[...] * (sg * (1.0 + gv * (1.0 - sg)))).astype(dg_ref.dtype)
        du_ref[...] = (dgu * (gv * sg)).astype(du_ref.dtype)

    seg = pl.BlockSpec((None, tm, f), lambda i, s: (s, i, 0))
    return pl.pallas_call(
        body, name=name, grid=(T // tm, S),
        in_specs=[pl.BlockSpec((tm, N), lambda i, s: (i, 0)), pl.BlockSpec((None, None, f, N), lambda i, s: (lyr, s, 0, 0)), seg, seg],
        out_specs=[seg, seg],
        out_shape=[jax.ShapeDtypeStruct((S, T, f), BF16), jax.ShapeDtypeStruct((S, T, f), BF16)],
        compiler_params=_params(("parallel", "parallel")),
    )(dy, wd, g, u)


def _seg_spec(x, S, tm):
    if x.ndim == 2:
        return pl.BlockSpec((tm, x.shape[1] // S), lambda i, s: (i, s))
    return pl.BlockSpec((None, tm, x.shape[2]), lambda i, s: (s, i, 0))


def nt_cols(pairs, lyr, T, name):
    S, K = pairs[0][1].shape[1], pairs[0][1].shape[2]
    tm = _tile(T, TOKEN_TILE)
    np_ = len(pairs)

    def body(*refs):
        o_ref = refs[-1]
        s = pl.program_id(1)
        acc = None
        for p in range(np_):
            part = lax.dot_general(refs[2 * p][...], refs[2 * p + 1][...], _NT, preferred_element_type=F32)
            acc = part if acc is None else acc + part

        @pl.when(s == 0)
        def _():
            o_ref[...] = acc

        @pl.when(s > 0)
        def _():
            o_ref[...] += acc

    in_specs, args = [], []
    for dy, w in pairs:
        in_specs += [_seg_spec(dy, S, tm), pl.BlockSpec((None, None, K, w.shape[3]), lambda i, s: (lyr, s, 0, 0))]
        args += [dy, w]
    return pl.pallas_call(
        body, name=name, grid=(T // tm, S), in_specs=in_specs,
        out_specs=pl.BlockSpec((tm, K), lambda i, s: (i, 0)),
        out_shape=jax.ShapeDtypeStruct((T, K), F32),
        compiler_params=_params(("parallel", "arbitrary")),
    )(*args)


def tn_grad(a, dy, S, a_by_seg, name):
    T = dy.shape[0] if dy.ndim == 2 else dy.shape[1]
    tt = _tile(T, TOKEN_TILE)
    if a_by_seg:
        R = a.shape[1] // S if a.ndim == 2 else a.shape[2]
        C = dy.shape[1]
        a_spec = pl.BlockSpec((tt, R), lambda s, t: (t, s)) if a.ndim == 2 else pl.BlockSpec((None, tt, R), lambda s, t: (s, t, 0))
        b_spec = pl.BlockSpec((tt, C), lambda s, t: (t, 0))
    else:
        R = a.shape[1]
        C = dy.shape[1] // S if dy.ndim == 2 else dy.shape[2]
        a_spec = pl.BlockSpec((tt, R), lambda s, t: (t, 0))
        b_spec = pl.BlockSpec((tt, C), lambda s, t: (t, s)) if dy.ndim == 2 else pl.BlockSpec((None, tt, C), lambda s, t: (s, t, 0))
    Rh = R // 2
    nt = T // tt

    def body(a_ref, b_ref, o_ref, acc_ref):
        t = pl.program_id(1)
        part = lax.dot_general(a_ref[...], b_ref[...].astype(BF16), _TN, preferred_element_type=F32)

        @pl.when(t == 0)
        def _():
            acc_ref[...] = part

        @pl.when(t > 0)
        def _():
            acc_ref[...] += part

        @pl.when(t == nt - 1)
        def _():
            o_ref[0] = acc_ref[:Rh, :].astype(o_ref.dtype)
            o_ref[1] = acc_ref[Rh:, :].astype(o_ref.dtype)

    return pl.pallas_call(
        body, name=name, grid=(S, nt), in_specs=[a_spec, b_spec],
        out_specs=pl.BlockSpec((2, None, Rh, C), lambda s, t: (0, s, 0, 0)),
        out_shape=jax.ShapeDtypeStruct((2, S, Rh, C), BF16),
        scratch_shapes=[pltpu.VMEM((R, C), F32)],
        compiler_params=_params(("parallel", "arbitrary")),
    )(a, dy)


def _place():
    x, y, c = lax.axis_index("x"), lax.axis_index("y"), lax.axis_index("c")
    chips = [(1 - x, y), (x, 1 - y), (1 - x, 1 - y)]
    return x, y, c, chips


def _any_specs(n):
    return [pl.BlockSpec(memory_space=pl.ANY)] * n


def _remote(src, dst, send_sem, recv_sem, dev):
    return pltpu.make_async_remote_copy(src_ref=src, dst_ref=dst, send_sem=send_sem, recv_sem=recv_sem,
                                        device_id=dev, device_id_type=MESH)


def small_allgather(v, name):
    rows, W = v.shape

    def body(v_ref, o_ref, send_sems, recv_sems, loc_sem):
        x, y, c, _ = _place()
        mine = pltpu.make_async_copy(v_ref, o_ref.at[4 * x + 2 * y + c], loc_sem)
        mine.start()

        def peer_of(m):
            return ((1 - x) if m & 4 else x, (1 - y) if m & 2 else y, (1 - c) if m & 1 else c)

        sends = []
        for m in range(1, N_DEV):
            cp = _remote(v_ref, o_ref.at[4 * x + 2 * y + c], send_sems.at[m - 1], recv_sems.at[m - 1], peer_of(m))
            cp.start()
            sends.append(cp)
        for m in range(1, N_DEV):
            px, py, pc = peer_of(m)
            blk = o_ref.at[4 * px + 2 * py + pc]
            _remote(blk, blk, send_sems.at[m - 1], recv_sems.at[m - 1], (px, py, pc)).wait_recv()
        for cp in sends:
            cp.wait_send()
        mine.wait()

    return pl.pallas_call(
        body, name=name,
        in_specs=[pl.BlockSpec(memory_space=pltpu.VMEM)], out_specs=pl.BlockSpec(memory_space=pltpu.VMEM),
        out_shape=jax.ShapeDtypeStruct((N_DEV, rows, W), F32),
        scratch_shapes=[pltpu.SemaphoreType.DMA((N_DEV - 1,)), pltpu.SemaphoreType.DMA((N_DEV - 1,)), pltpu.SemaphoreType.DMA],
    )(v)


def gather_weights(shards, name):
    units = [(w, l) for w in range(len(shards)) for l in range(shards[w].shape[0])]
    nu = len(units)
    n = len(shards)

    def body(*refs):
        ins, outs = refs[:n], refs[n:2 * n]
        send_sems, recv_sems, loc_sems = refs[2 * n:]
        x, y, c, chips = _place()
        me = 2 * x + y

        def half(w, l, chip, hc):
            rh = ins[w].shape[1] // 2
            return outs[w].at[l, chip, pl.ds(hc * rh, rh)]

        local, sends = [], []
        for k, (w, l) in enumerate(units):
            rh = ins[w].shape[1] // 2
            cp = pltpu.make_async_copy(ins[w].at[l], outs[w].at[l, me], loc_sems.at[k])
            cp.start()
            local.append(cp)
            for j, (px, py) in enumerate(chips):
                cp = _remote(ins[w].at[l, pl.ds(c * rh, rh)], half(w, l, me, c), send_sems.at[k, j], recv_sems.at[k, j], (px, py, c))
                cp.start()
                sends.append(cp)
        for k, (w, l) in enumerate(units):
            for j, (px, py) in enumerate(chips):
                blk = half(w, l, 2 * px + py, c)
                _remote(blk, blk, send_sems.at[k, j], recv_sems.at[k, j], (px, py, c)).wait_recv()
                cp = _remote(blk, blk, send_sems.at[k, 3 + j], recv_sems.at[k, 3 + j], (x, y, 1 - c))
                cp.start()
                sends.append(cp)
        for k, (w, l) in enumerate(units):
            for j, (px, py) in enumerate(chips):
                blk = half(w, l, 2 * px + py, 1 - c)
                _remote(blk, blk, send_sems.at[k, 3 + j], recv_sems.at[k, 3 + j], (x, y, 1 - c)).wait_recv()
        for cp in sends:
            cp.wait_send()
        for cp in local:
            cp.wait()

    return pl.pallas_call(
        body, name=name, in_specs=_any_specs(n), out_specs=_any_specs(n),
        out_shape=[jax.ShapeDtypeStruct((s.shape[0], N_CHIPS) + s.shape[1:], s.dtype) for s in shards],
        scratch_shapes=[pltpu.SemaphoreType.DMA((nu, 6)), pltpu.SemaphoreType.DMA((nu, 6)), pltpu.SemaphoreType.DMA((nu,))],
    )(*shards)


def sibling_halves(grads, name):
    n = len(grads)

    def body(*refs):
        ins, outs = refs[:n], refs[n:2 * n]
        send_sems, recv_sems = refs[2 * n:]
        x, y, c, _ = _place()
        cps = []
        for k in range(n):
            cp = _remote(ins[k].at[1 - c], outs[k], send_sems.at[k], recv_sems.at[k], (x, y, 1 - c))
            cp.start()
            cps.append(cp)
        for cp in cps:
            cp.wait_recv()
        for cp in cps:
            cp.wait_send()

    return pl.pallas_call(
        body, name=name, in_specs=_any_specs(n), out_specs=_any_specs(n),
        out_shape=[jax.ShapeDtypeStruct(g.shape[1:], g.dtype) for g in grads],
        scratch_shapes=[pltpu.SemaphoreType.DMA((n,)), pltpu.SemaphoreType.DMA((n,))],
    )(*grads)


def pair_sum(gh, recv, cidx, name):
    _, S, Rh, C = gh.shape

    def body(c_ref, a_ref, b_ref, o_ref):
        o_ref[...] = (a_ref[...].astype(F32) + b_ref[...].astype(F32)).astype(o_ref.dtype)

    return pl.pallas_call(
        body, name=name, out_shape=jax.ShapeDtypeStruct((S, Rh, C), BF16),
        grid_spec=pltpu.PrefetchScalarGridSpec(
            num_scalar_prefetch=1, grid=(S,),
            in_specs=[pl.BlockSpec((None, None, Rh, C), lambda s, c_ref: (c_ref[0], s, 0, 0)),
                      pl.BlockSpec((None, Rh, C), lambda s, c_ref: (s, 0, 0))],
            out_specs=pl.BlockSpec((None, Rh, C), lambda s, c_ref: (s, 0, 0))),
        compiler_params=_params(("parallel",)),
    )(cidx, gh, recv)


def scatter_to_owners(parts, name):
    n = len(parts)

    def body(*refs):
        ins, outs = refs[:n], refs[n:2 * n]
        send_sems, recv_sems, loc_sems = refs[2 * n:]
        x, y, c, chips = _place()
        me = 2 * x + y
        local, sends = [], []
        for k in range(n):
            cp = pltpu.make_async_copy(ins[k].at[me], outs[k].at[me], loc_sems.at[k])
            cp.start()
            local.append(cp)
            for j, (px, py) in enumerate(chips):
                cp = _remote(ins[k].at[2 * px + py], outs[k].at[me], send_sems.at[k, j], recv_sems.at[k, j], (px, py, c))
                cp.start()
                sends.append(cp)
        for k in range(n):
            for j, (px, py) in enumerate(chips):
                blk = outs[k].at[2 * px + py]
                _remote(blk, blk, send_sems.at[k, j], recv_sems.at[k, j], (px, py, c)).wait_recv()
        for cp in sends:
            cp.wait_send()
        for cp in local:
            cp.wait()

    return pl.pallas_call(
        body, name=name, in_specs=_any_specs(n), out_specs=_any_specs(n),
        out_shape=[jax.ShapeDtypeStruct(p.shape, p.dtype) for p in parts],
        scratch_shapes=[pltpu.SemaphoreType.DMA((n, 3)), pltpu.SemaphoreType.DMA((n, 3)), pltpu.SemaphoreType.DMA((n,))],
    )(*parts)


def chip_sum(buf, name):
    S, Rh, C = buf.shape
    rb = _tile(Rh, ROW_TILE) if Rh % 16 == 0 and _tile(Rh, ROW_TILE) % 16 == 0 else Rh

    def body(b_ref, o_ref):
        acc = b_ref[0].astype(F32)
        for p in range(1, S):
            acc = acc + b_ref[p].astype(F32)
        o_ref[...] = acc

    return pl.pallas_call(
        body, name=name, grid=(Rh // rb,),
        in_specs=[pl.BlockSpec((S, rb, C), lambda i: (0, i, 0))],
        out_specs=pl.BlockSpec((rb, C), lambda i: (i, 0)),
        out_shape=jax.ShapeDtypeStruct((Rh, C), F32),
        compiler_params=_params(("parallel",)),
    )(buf)


def join_halves(halves, groups, name):
    n = len(halves)
    ng = len(groups)
    where = {}
    for g, idxs in enumerate(groups):
        for l, k in enumerate(idxs):
            where[k] = (g, l)

    def body(*refs):
        ins, outs = refs[:n], refs[n:n + ng]
        send_sems, recv_sems, loc_sems = refs[n + ng:]
        x, y, c, _ = _place()
        local, sends = [], []
        for k in range(n):
            g, l = where[k]
            rh = ins[k].shape[0]
            dst = outs[g].at[l, pl.ds(c * rh, rh)]
            cp = pltpu.make_async_copy(ins[k], dst, loc_sems.at[k])
            cp.start()
            local.append(cp)
            cp = _remote(ins[k], dst, send_sems.at[k], recv_sems.at[k], (x, y, 1 - c))
            cp.start()
            sends.append(cp)
        for k in range(n):
            g, l = where[k]
            rh = ins[k].shape[0]
            blk = outs[g].at[l, pl.ds((1 - c) * rh, rh)]
            _remote(blk, blk, send_sems.at[k], recv_sems.at[k], (x, y, 1 - c)).wait_recv()
        for cp in sends:
            cp.wait_send()
        for cp in local:
            cp.wait()

    out_shape = [jax.ShapeDtypeStruct((len(idxs), 2 * halves[idxs[0]].shape[0], halves[idxs[0]].shape[1]), F32) for idxs in groups]
    return pl.pallas_call(
        body, name=name, in_specs=_any_specs(n), out_specs=_any_specs(ng), out_shape=out_shape,
        scratch_shapes=[pltpu.SemaphoreType.DMA((n,)), pltpu.SemaphoreType.DMA((n,)), pltpu.SemaphoreType.DMA((n,))],
    )(*halves)


def _adamw_math(w, g, m, v):
    m = ADAM_B1 * m + (1.0 - ADAM_B1) * g
    v = ADAM_B2 * v + (1.0 - ADAM_B2) * (g * g)
    m_hat = m / (1.0 - ADAM_B1 ** ADAM_STEP)
    v_hat = v / (1.0 - ADAM_B2 ** ADAM_STEP)
    delta = -ADAM_LR * (m_hat / (jnp.sqrt(v_hat) + ADAM_EPS) + ADAM_WD * w)
    return delta, m, v


def adamw_big(w, g, m, v, name):
    L, R, C = w.shape
    rb = _tile(R, ROW_TILE)

    def body(w_ref, g_ref, m_ref, v_ref, go_ref, d_ref, mo_ref, vo_ref):
        gv = g_ref[...]
        d, mn, vn = _adamw_math(w_ref[...], gv, m_ref[...], v_ref[...])
        go_ref[...] = gv
        d_ref[...] = d
        mo_ref[...] = mn
        vo_ref[...] = vn

    blk = pl.BlockSpec((None, rb, C), lambda l, i: (l, i, 0))
    shp = jax.ShapeDtypeStruct((L, R, C), F32)
    return pl.pallas_call(
        body, name=name, grid=(L, R // rb), in_specs=[blk] * 4, out_specs=[blk] * 4, out_shape=[shp] * 4,
        compiler_params=_params(("parallel", "parallel")),
    )(w, g, m, v)


def small_update(gall, chip, entries, name):
    ne = len(entries)
    D = gall.shape[2]

    def body(chip_ref, gall_ref, *refs):
        ins, outs = refs[:3 * ne], refs[3 * ne:]
        ch = chip_ref[0]
        for e, (row0, kind, w, _, _) in enumerate(entries):
            r, width = w.shape

            def gsum(rs, cs):
                acc = gall_ref[0, rs, cs]
                for d in range(1, N_DEV):
                    acc = acc + gall_ref[d, rs, cs]
                return acc

            if kind == "full":
                g = gsum(slice(row0, row0 + r), slice(0, D))
            elif kind == "cols":
                g = gsum(slice(row0, row0 + r), slice(0, width))
                for q in range(1, N_CHIPS):
                    g = jnp.where(ch == q, gsum(slice(row0, row0 + r), slice(q * width, (q + 1) * width)), g)
            else:
                per_row = D // width
                g = gsum(slice(row0, row0 + 1), slice(0, width))
                for q in range(1, N_CHIPS):
                    rr = row0 + q // per_row
                    cc = (q % per_row) * width
                    g = jnp.where(ch == q, gsum(slice(rr, rr + 1), slice(cc, cc + width)), g)
            d, mn, vn = _adamw_math(ins[3 * e][...], g, ins[3 * e + 1][...], ins[3 * e + 2][...])
            outs[4 * e][...] = g
            outs[4 * e + 1][...] = d
            outs[4 * e + 2][...] = mn
            outs[4 * e + 3][...] = vn

    vm = pl.BlockSpec(memory_space=pltpu.VMEM)
    args, out_shape = [], []
    for _, _, w, m, v in entries:
        args += [w, m, v]
        out_shape += [jax.ShapeDtypeStruct(w.shape, F32)] * 4
    return pl.pallas_call(
        body, name=name,
        in_specs=[pl.BlockSpec(memory_space=pltpu.SMEM), vm] + [vm] * (3 * ne),
        out_specs=[vm] * (4 * ne), out_shape=out_shape,
        compiler_params=pltpu.CompilerParams(vmem_limit_bytes=VMEM_LIMIT),
    )(chip, gall, *args)


def _pack_rows(items, width):
    rows, starts, at = [], [], 0
    for it in items:
        r = it.shape[0]
        pad = (-r) % SUBLANES
        starts.append(at)
        rows.append(it)
        if pad:
            rows.append(jnp.zeros((pad, width), F32))
        at += r + pad
    return jnp.concatenate(rows, axis=0), starts


def kernel(x, a_norm, a_w_in, a_conv, a_w_out, b_norm, b_w_pw1, b_b_pw1, b_conv, b_b_conv, b_ln_g, b_ln_b, b_w_pw2, b_b_pw2, ffn_norm, ffn_w_gate, ffn_w_up, ffn_w_down, final_norm, loss_target, m_a_norm, m_a_w_in, m_a_conv, m_a_w_out, m_b_norm, m_b_w_pw1, m_b_b_pw1, m_b_conv, m_b_b_conv, m_b_ln_g, m_b_ln_b, m_b_w_pw2, m_b_b_pw2, m_ffn_norm, m_ffn_w_gate, m_ffn_w_up, m_ffn_w_down, m_final_norm, v_a_norm, v_a_w_in, v_a_conv, v_a_w_out, v_b_norm, v_b_w_pw1, v_b_b_pw1, v_b_conv, v_b_b_conv, v_b_ln_g, v_b_ln_b, v_b_w_pw2, v_b_b_pw2, v_ffn_norm, v_ffn_w_gate, v_ffn_w_up, v_ffn_w_down, v_final_norm):
    T, D = x.shape[1], x.shape[2]
    Dq = D // N_CHIPS
    cx, cy, cc = lax.axis_index("x"), lax.axis_index("y"), lax.axis_index("c")
    chip = (2 * cx + cy).astype(jnp.int32).reshape(1)
    cidx = cc.astype(jnp.int32).reshape(1)
    h0 = x.reshape(T, D)
    tgt = loss_target.reshape(T, D)

    small_shards = [a_conv[0], b_norm, b_b_pw1.reshape(2, Dq), b_conv[0], b_b_conv, b_ln_g, b_ln_b, b_b_pw2]
    packed, st = _pack_rows(small_shards, Dq)
    sw = small_allgather(packed, "gather_small")[0::2]

    def whole(k, r):
        return jnp.transpose(sw[:, st[k]:st[k] + r, :], (1, 0, 2)).reshape(r, D)

    a_conv_f, b_norm_f = whole(0, 3), whole(1, 1)
    b_b_pw1_f = sw[:, st[2]:st[2] + 2, :].reshape(1, 2 * D)
    b_conv_f, b_b_conv_f, b_ln_g_f, b_ln_b_f, b_b_pw2_f = whole(3, b_conv.shape[1]), whole(4, 1), whole(5, 1), whole(6, 1), whole(7, 1)

    big = [a_w_in, a_w_out, b_w_pw1, b_w_pw2, ffn_w_gate, ffn_w_up, ffn_w_down]
    g_in, g_out, g_pw1, g_pw2, g_gate, g_up, g_down = gather_weights([w.astype(BF16) for w in big], "gather_weights")
    g_out = g_out.reshape(1, 1, D, D)
    g_pw2 = g_pw2.reshape(1, 1, D, D)

    n0 = rms_fwd(h0, a_norm, "rms_a")
    bcv = mm_cols(n0, g_in, 0, None, "mm_w_in")
    ya = gateconv_fwd(bcv, a_conv_f, "gateconv_fwd")
    h1 = mm_rows(ya[None], g_out, 0, h0, None, "mm_w_out")
    n1 = rms_fwd(h1, ffn_norm[0:1], "rms_f0")
    fg0, fu0, gu0 = ffn_up(n1, g_gate, g_up, 0, "ffn_up0")
    h2 = mm_rows(gu0, g_down, 0, h1, None, "ffn_down0")
    n2 = rms_fwd(h2, b_norm_f, "rms_b")
    ub = mm_cols(n2, g_pw1, 0, b_b_pw1_f, "mm_pw1")
    cu, sb = bconv_fwd(ub, b_conv_f, b_b_conv_f, b_ln_g_f, b_ln_b_f, "bconv_fwd")
    h3 = mm_rows(sb[None], g_pw2, 0, h2, b_b_pw2_f, "mm_pw2")
    n3 = rms_fwd(h3, ffn_norm[1:2], "rms_f1")
    fg1, fu1, gu1 = ffn_up(n3, g_gate, g_up, 1, "ffn_up1")
    h4 = mm_rows(gu1, g_down, 1, h3, None, "ffn_down1")
    loss_part, dh4, d_final = loss_head(h4, final_norm.reshape(1, D), tgt, "loss_head")

    def ffn_bwd(dh, h_in, n_in, fg, fu, gu, lyr, tag):
        dg, du = ffn_nt(dh, g_down, lyr, fg, fu, "ffn_nt" + tag)
        gd = tn_grad(gu, dh, N_CHIPS, True, "tn_down" + tag)
        dn = nt_cols([(dg, g_gate), (du, g_up)], lyr, T, "nt_gateup" + tag)
        gg = tn_grad(n_in, dg, N_CHIPS, False, "tn_gate" + tag)
        gu_ = tn_grad(n_in, du, N_CHIPS, False, "tn_up" + tag)
        dh_in, dnorm = rms_bwd(dn, h_in, ffn_norm[lyr:lyr + 1], dh, "rms_bwd_f" + tag)
        return dh_in, dnorm, gg, gu_, gd

    dh3, d_fn1, gh_gate1, gh_up1, gh_down1 = ffn_bwd(dh4, h3, n3, fg1, fu1, gu1, 1, "1")

    ds, d_b_pw2 = nt_rows(dh3, g_pw2, 0, True, "nt_pw2")
    gh_pw2 = tn_grad(sb, dh3, N_CHIPS, True, "tn_pw2")
    dcu, d_ln_g, d_ln_b, d_b_conv = ln_silu_bwd(ds[0], cu, b_ln_g_f, b_ln_b_f, "ln_silu_bwd")
    dub, d_bconv_w, d_b_pw1 = bconv_bwd(dcu, ub, b_conv_f, "bconv_bwd")
    dn2 = nt_cols([(dub, g_pw1)], 0, T, "nt_pw1")
    gh_pw1 = tn_grad(n2, dub, N_CHIPS, False, "tn_pw1")
    dh2, d_b_norm = rms_bwd(dn2, h2, b_norm_f, dh3, "rms_bwd_b")

    dh1, d_fn0, gh_gate0, gh_up0, gh_down0 = ffn_bwd(dh2, h1, n1, fg0, fu0, gu0, 0, "0")

    dya = nt_rows(dh1, g_out, 0, False, "nt_w_out")[0]
    gh_out = tn_grad(ya, dh1, N_CHIPS, True, "tn_w_out")
    dbcv, d_aconv_w = gateconv_bwd(dya[0], bcv, a_conv_f, "gateconv_bwd")
    dn0 = nt_cols([(dbcv, g_in)], 0, T, "nt_w_in")
    gh_in = tn_grad(n0, dbcv, N_CHIPS, False, "tn_w_in")
    grad_x, d_a_norm = rms_bwd(dn0, h0, a_norm, dh1, "rms_bwd_a")

    ghs = [gh_in, gh_out, gh_pw1, gh_pw2, gh_gate0, gh_gate1, gh_up0, gh_up1, gh_down0, gh_down1]
    tags = ["in", "out", "pw1", "pw2", "gate0", "gate1", "up0", "up1", "down0", "down1"]
    from_sib = sibling_halves(ghs, "reduce_sibling")
    parts = [pair_sum(g, r, cidx, "pair_sum_" + t) for g, r, t in zip(ghs, from_sib, tags)]
    at_owner = scatter_to_owners(parts, "reduce_scatter")
    halves = [chip_sum(b, "chip_sum_" + t) for b, t in zip(at_owner, tags)]
    G_in, G_out, G_pw1, G_pw2, G_gate, G_up, G_down = join_halves(
        halves, [[0], [1], [2], [3], [4, 5], [6, 7], [8, 9]], "reduce_join")

    def upd(w, g, m, v, tag):
        go, d, mn, vn = adamw_big(w, g, m, v, "adamw_" + tag)
        return go, d, mn, vn

    r_in = upd(a_w_in, G_in, m_a_w_in, v_a_w_in, "w_in")
    r_out = upd(a_w_out, G_out, m_a_w_out, v_a_w_out, "w_out")
    r_pw1 = upd(b_w_pw1, G_pw1, m_b_w_pw1, v_b_w_pw1, "pw1")
    r_pw2 = upd(b_w_pw2, G_pw2, m_b_w_pw2, v_b_w_pw2, "pw2")
    r_gate = upd(ffn_w_gate, G_gate, m_ffn_w_gate, v_ffn_w_gate, "gate")
    r_up = upd(ffn_w_up, G_up, m_ffn_w_up, v_ffn_w_up, "up")
    r_down = upd(ffn_w_down, G_down, m_ffn_w_down, v_ffn_w_down, "down")

    d_ffn_norm = jnp.concatenate([d_fn0, d_fn1], axis=0)
    small_grads = [d_a_norm, d_aconv_w, d_b_norm, d_b_pw1.reshape(2, D), d_bconv_w, d_b_conv, d_ln_g, d_ln_b, d_b_pw2,
                   d_ffn_norm, d_final]
    gpacked, gs = _pack_rows(small_grads, D)
    gall = small_allgather(gpacked, "gather_small_grads")
    entries = [
        (gs[0], "full", a_norm, m_a_norm, v_a_norm),
        (gs[1], "cols", a_conv[0], m_a_conv[0], v_a_conv[0]),
        (gs[2], "cols", b_norm, m_b_norm, v_b_norm),
        (gs[3], "flat2", b_b_pw1, m_b_b_pw1, v_b_b_pw1),
        (gs[4], "cols", b_conv[0], m_b_conv[0], v_b_conv[0]),
        (gs[5], "cols", b_b_conv, m_b_b_conv, v_b_b_conv),
        (gs[6], "cols", b_ln_g, m_b_ln_g, v_b_ln_g),
        (gs[7], "cols", b_ln_b, m_b_ln_b, v_b_ln_b),
        (gs[8], "cols", b_b_pw2, m_b_b_pw2, v_b_b_pw2),
        (gs[9], "full", ffn_norm, m_ffn_norm, v_ffn_norm),
        (gs[10], "full", final_norm.reshape(1, D), m_final_norm.reshape(1, D), v_final_norm.reshape(1, D)),
    ]
    so = small_update(gall, chip, entries, "small_update")
    sm = [so[4 * e:4 * e + 4] for e in range(len(entries))]

    def shaped(e, like):
        return [t.reshape(like.shape) for t in sm[e]]

    r_a_norm, r_a_conv, r_b_norm, r_b_b_pw1 = shaped(0, a_norm), shaped(1, a_conv), shaped(2, b_norm), shaped(3, b_b_pw1)
    r_b_conv, r_b_b_conv, r_b_ln_g, r_b_ln_b = shaped(4, b_conv), shaped(5, b_b_conv), shaped(6, b_ln_g), shaped(7, b_ln_b)
    r_b_b_pw2, r_ffn_norm, r_final = shaped(8, b_b_pw2), shaped(9, ffn_norm), shaped(10, final_norm)

    loss = lax.psum(loss_part[0, 0], ("x", "y", "c"))
    order = [r_a_norm, r_in, r_a_conv, r_out, r_b_norm, r_pw1, r_b_b_pw1, r_b_conv, r_b_b_conv, r_b_ln_g, r_b_ln_b,
             r_pw2, r_b_b_pw2, r_ffn_norm, r_gate, r_up, r_down, r_final]
    outs = [loss, grad_x.reshape(x.shape)]
    for field in range(4):
        outs += [r[field] for r in order]
    return tuple(outs)
```

```python
import functools

import jax
import jax.numpy as jnp
from jax import lax
from jax.experimental import pallas as pl
from jax.experimental.pallas import tpu as pltpu

RMS_EPS = 1e-6
LN_EPS = 1e-5
ADAM_LR = 0.001
ADAM_B1 = 0.9
ADAM_B2 = 0.999
ADAM_EPS = 1e-08
ADAM_WD = 0.01
ADAM_STEP = 10

N_CHIPS = 4
N_DEV = 8
LANES = 128
SUBLANES = 8
HALO = 32
CONV_ROWS = 64
TOKEN_TILE = 512
WIDE_TOKEN_TILE = 1024
GRAD_TOKEN_TILE = 2048
ROW_TILE = 256
VMEM_LIMIT = 56 * 1024 * 1024
MESH = pl.DeviceIdType.MESH
BF16 = jnp.bfloat16
F32 = jnp.float32


def _tile(n, pref, mult=SUBLANES):
    t = min(n, pref) // mult * mult
    while n % t:
        t -= mult
    return t


def _params(sem):
    return pltpu.CompilerParams(dimension_semantics=sem, vmem_limit_bytes=VMEM_LIMIT)


def _sigmoid(x):
    return jax.nn.sigmoid(x)


def rms_fwd(h, gain, name):
    T, D = h.shape
    tm = _tile(T, TOKEN_TILE)

    def body(h_ref, g_ref, o_ref):
        x = h_ref[...]
        r = lax.rsqrt(jnp.mean(x * x, axis=-1, keepdims=True) + RMS_EPS)
        o_ref[...] = (x * r * g_ref[...]).astype(o_ref.dtype)

    return pl.pallas_call(
        body, name=name, grid=(T // tm,),
        in_specs=[pl.BlockSpec((tm, D), lambda i: (i, 0)), pl.BlockSpec((1, D), lambda i: (0, 0))],
        out_specs=pl.BlockSpec((tm, D), lambda i: (i, 0)),
        out_shape=jax.ShapeDtypeStruct((T, D), BF16),
        compiler_params=_params(("parallel",)),
    )(h, gain)


def loss_head(h, gain, tgt, name):
    T, D = h.shape
    tm = _tile(T, TOKEN_TILE)

    def body(h_ref, g_ref, t_ref, loss_ref, dh_ref, dg_ref):
        i = pl.program_id(0)
        x = h_ref[...]
        g = g_ref[...]
        r = lax.rsqrt(jnp.mean(x * x, axis=-1, keepdims=True) + RMS_EPS)
        xhat = x * r
        diff = xhat * g - t_ref[...]
        part_loss = 0.5 * jnp.sum(jnp.mean(diff * diff, axis=-1, keepdims=True), axis=0, keepdims=True)
        dy = diff * (1.0 / D)
        dxhat = dy * g
        dh_ref[...] = r * (dxhat - xhat * jnp.mean(dxhat * xhat, axis=-1, keepdims=True))
        part = jnp.sum(dy * xhat, axis=0, keepdims=True)

        @pl.when(i == 0)
        def _():
            dg_ref[...] = part
            loss_ref[...] = part_loss

        @pl.when(i > 0)
        def _():
            dg_ref[...] += part
            loss_ref[...] += part_loss

    row = pl.BlockSpec((tm, D), lambda i: (i, 0))
    vec = pl.BlockSpec((1, D), lambda i: (0, 0))
    return pl.pallas_call(
        body, name=name, grid=(T // tm,),
        in_specs=[row, vec, row],
        out_specs=[pl.BlockSpec((1, 1), lambda i: (0, 0)), row, vec],
        out_shape=[jax.ShapeDtypeStruct((1, 1), F32), jax.ShapeDtypeStruct((T, D), F32),
                   jax.ShapeDtypeStruct((1, D), F32)],
        compiler_params=_params(("arbitrary",)),
    )(h, gain, tgt)


def _prev_halo_spec(tm, width):
    return pl.BlockSpec((HALO, width), lambda i: (jnp.maximum(i * (tm // HALO) - 1, 0), 0))


def _next_halo_spec(tm, width, T):
    return pl.BlockSpec((HALO, width), lambda i: (jnp.minimum((i + 1) * (tm // HALO), T // HALO - 1), 0))


def _shifted(win, off, rows):
    if off % SUBLANES == 0:
        return win[off:off + rows]
    n = win.shape[0]
    return pltpu.roll(win, (n - off) % n, 0)[:rows]


def _rowsum8(x):
    acc = x[0:SUBLANES]
    for q in range(1, x.shape[0] // SUBLANES):
        acc = acc + x[q * SUBLANES:(q + 1) * SUBLANES]
    return acc


def _conv_loops(tm, D, per_block):
    def chunk(r, carry):
        t0 = pl.multiple_of(r * CONV_ROWS, CONV_ROWS)
        for lb in range(D // LANES):
            per_block(t0, slice(lb * LANES, (lb + 1) * LANES))
        return carry

    lax.fori_loop(0, tm // CONV_ROWS, chunk, 0)


def gateconv_fwd(bcv, w, name):
    T, D3 = bcv.shape
    D = D3 // 3
    K = w.shape[0]
    tm = _tile(T, TOKEN_TILE)

    def body(x_ref, halo_ref, w_ref, y_ref, pad_ref):
        i = pl.program_id(0)
        pad_ref[HALO:, :] = x_ref[:, D:2 * D] * x_ref[:, 2 * D:]
        pad_ref[:HALO, :] = jnp.where(i > 0, halo_ref[:, D:2 * D] * halo_ref[:, 2 * D:], 0.0)

        def block(t0, ls):
            win = pad_ref[pl.ds(t0, CONV_ROWS + HALO), ls]
            acc = jnp.zeros((CONV_ROWS, LANES), F32)
            for k in range(K):
                acc = acc + w_ref[k:k + 1, ls] * _shifted(win, HALO - (K - 1) + k, CONV_ROWS)
            y_ref[pl.ds(t0, CONV_ROWS), ls] = (x_ref[pl.ds(t0, CONV_ROWS), ls] * acc).astype(y_ref.dtype)

        _conv_loops(tm, D, block)

    return pl.pallas_call(
        body, name=name, grid=(T // tm,),
        in_specs=[pl.BlockSpec((tm, D3), lambda i: (i, 0)), _prev_halo_spec(tm, D3),
                  pl.BlockSpec((K, D), lambda i: (0, 0))],
        out_specs=pl.BlockSpec((tm, D), lambda i: (i, 0)),
        out_shape=jax.ShapeDtypeStruct((T, D), BF16),
        scratch_shapes=[pltpu.VMEM((tm + HALO, D), F32)],
        compiler_params=_params(("parallel",)),
    )(bcv, bcv, w)


def gateconv_bwd(dy, bcv, w, name):
    T, D3 = bcv.shape
    D = D3 // 3
    K = w.shape[0]
    tm = _tile(T, TOKEN_TILE)
    nt = T // tm

    def body(dy_ref, dyn_ref, x_ref, xp_ref, xn_ref, w_ref, o_ref, dw_ref, cv_ref, dc_ref, wacc_ref):
        i = pl.program_id(0)
        cv_ref[HALO:, :] = x_ref[:, D:2 * D] * x_ref[:, 2 * D:]
        cv_ref[:HALO, :] = jnp.where(i > 0, xp_ref[:, D:2 * D] * xp_ref[:, 2 * D:], 0.0)
        dc_ref[:tm, :] = dy_ref[...] * x_ref[:, :D]
        dc_ref[tm:, :] = jnp.where(i < nt - 1, dyn_ref[...] * xn_ref[:, :D], 0.0)

        @pl.when(i == 0)
        def _():
            wacc_ref[...] = jnp.zeros_like(wacc_ref)

        def block(t0, ls):
            cwin = cv_ref[pl.ds(t0, CONV_ROWS + HALO), ls]
            dwin = dc_ref[pl.ds(t0, CONV_ROWS + HALO), ls]
            dcon = dwin[:CONV_ROWS]
            conv = jnp.zeros((CONV_ROWS, LANES), F32)
            dcv = jnp.zeros((CONV_ROWS, LANES), F32)
            for k in range(K):
                wk = w_ref[k:k + 1, ls]
                cs = _shifted(cwin, HALO - (K - 1) + k, CONV_ROWS)
                conv = conv + wk * cs
                dcv = dcv + wk * _shifted(dwin, (K - 1) - k, CONV_ROWS)
                wacc_ref[k * SUBLANES:(k + 1) * SUBLANES, ls] += _rowsum8(dcon * cs)
            rows = pl.ds(t0, CONV_ROWS)
            o_ref[rows, ls] = (dy_ref[rows, ls] * conv).astype(o_ref.dtype)
            o_ref[rows, pl.ds(D + ls.start, LANES)] = (dcv * x_ref[rows, pl.ds(2 * D + ls.start, LANES)]).astype(o_ref.dtype)
            o_ref[rows, pl.ds(2 * D + ls.start, LANES)] = (dcv * x_ref[rows, pl.ds(D + ls.start, LANES)]).astype(o_ref.dtype)

        _conv_loops(tm, D, block)

        @pl.when(i == nt - 1)
        def _():
            for k in range(K):
                dw_ref[k:k + 1, :] = jnp.sum(wacc_ref[k * SUBLANES:(k + 1) * SUBLANES, :], axis=0, keepdims=True)

    return pl.pallas_call(
        body, name=name, grid=(nt,),
        in_specs=[pl.BlockSpec((tm, D), lambda i: (i, 0)), _next_halo_spec(tm, D, T),
                  pl.BlockSpec((tm, D3), lambda i: (i, 0)), _prev_halo_spec(tm, D3), _next_halo_spec(tm, D3, T),
                  pl.BlockSpec((K, D), lambda i: (0, 0))],
        out_specs=[pl.BlockSpec((tm, D3), lambda i: (i, 0)), pl.BlockSpec((K, D), lambda i: (0, 0))],
        out_shape=[jax.ShapeDtypeStruct((T, D3), BF16), jax.ShapeDtypeStruct((K, D), F32)],
        scratch_shapes=[pltpu.VMEM((tm + HALO, D), F32), pltpu.VMEM((tm + HALO, D), F32),
                        pltpu.VMEM((K * SUBLANES, D), F32)],
        compiler_params=_params(("arbitrary",)),
    )(dy, dy, bcv, bcv, bcv, w)


def bconv_fwd(u, w, b_conv, ln_g, ln_b, name):
    T, D2 = u.shape
    D = D2 // 2
    K = w.shape[0]
    tm = _tile(T, TOKEN_TILE)

    def body(u_ref, halo_ref, w_ref, bc_ref, g_ref, b_ref, cu_ref, s_ref, pad_ref):
        i = pl.program_id(0)
        pad_ref[HALO:, :] = u_ref[:, :D] * _sigmoid(u_ref[:, D:])
        pad_ref[:HALO, :] = jnp.where(i > 0, halo_ref[:, :D] * _sigmoid(halo_ref[:, D:]), 0.0)

        def block(t0, ls):
            win = pad_ref[pl.ds(t0, CONV_ROWS + HALO), ls]
            acc = jnp.zeros((CONV_ROWS, LANES), F32)
            for k in range(K):
                acc = acc + w_ref[k:k + 1, ls] * _shifted(win, HALO - (K - 1) + k, CONV_ROWS)
            cu_ref[pl.ds(t0, CONV_ROWS), ls] = acc + bc_ref[:, ls]

        _conv_loops(tm, D, block)
        cu = cu_ref[...]
        mu = jnp.mean(cu, axis=-1, keepdims=True)
        xc = cu - mu
        rstd = lax.rsqrt(jnp.mean(xc * xc, axis=-1, keepdims=True) + LN_EPS)
        ln = xc * rstd * g_ref[...] + b_ref[...]
        s_ref[...] = (ln * _sigmoid(ln)).astype(s_ref.dtype)

    vec = pl.BlockSpec((1, D), lambda i: (0, 0))
    row = pl.BlockSpec((tm, D), lambda i: (i, 0))
    return pl.pallas_call(
        body, name=name, grid=(T // tm,),
        in_specs=[pl.BlockSpec((tm, D2), lambda i: (i, 0)), _prev_halo_spec(tm, D2),
                  pl.BlockSpec((K, D), lambda i: (0, 0)), vec, vec, vec],
        out_specs=[row, row],
        out_shape=[jax.ShapeDtypeStruct((T, D), F32), jax.ShapeDtypeStruct((T, D), BF16)],
        scratch_shapes=[pltpu.VMEM((tm + HALO, D), F32)],
        compiler_params=_params(("parallel",)),
    )(u, u, w, b_conv, ln_g, ln_b)


def ln_silu_bwd(ds, cu, ln_g, ln_b, name):
    T, D = cu.shape
    tm = _tile(T, TOKEN_TILE)

    def body(ds_ref, cu_ref, g_ref, b_ref, dcu_ref, dg_ref, db_ref, dbc_ref):
        i = pl.program_id(0)
        cu_ = cu_ref[...]
        mu = jnp.mean(cu_, axis=-1, keepdims=True)
        xc = cu_ - mu
        rstd = lax.rsqrt(jnp.mean(xc * xc, axis=-1, keepdims=True) + LN_EPS)
        xh = xc * rstd
        ln = xh * g_ref[...] + b_ref[...]
        sg = _sigmoid(ln)
        dl = ds_ref[...] * (sg * (1.0 + ln * (1.0 - sg)))
        dxh = dl * g_ref[...]
        dcu = rstd * (dxh - jnp.mean(dxh, axis=-1, keepdims=True) - xh * jnp.mean(dxh * xh, axis=-1, keepdims=True))
        dcu_ref[...] = dcu
        pg = jnp.sum(dl * xh, axis=0, keepdims=True)
        pb = jnp.sum(dl, axis=0, keepdims=True)
        pc = jnp.sum(dcu, axis=0, keepdims=True)

        @pl.when(i == 0)
        def _():
            dg_ref[...] = pg
            db_ref[...] = pb
            dbc_ref[...] = pc

        @pl.when(i > 0)
        def _():
            dg_ref[...] += pg
            db_ref[...] += pb
            dbc_ref[...] += pc

    vec = pl.BlockSpec((1, D), lambda i: (0, 0))
    row = pl.BlockSpec((tm, D), lambda i: (i, 0))
    vshape = jax.ShapeDtypeStruct((1, D), F32)
    return pl.pallas_call(
        body, name=name, grid=(T // tm,),
        in_specs=[row, row, vec, vec], out_specs=[row, vec, vec, vec],
        out_shape=[jax.ShapeDtypeStruct((T, D), F32), vshape, vshape, vshape],
        compiler_params=_params(("arbitrary",)),
    )(ds, cu, ln_g, ln_b)


def bconv_bwd(dcu, u, w, name):
    T, D2 = u.shape
    D = D2 // 2
    K = w.shape[0]
    tm = _tile(T, TOKEN_TILE)
    nt = T // tm

    def body(dc_ref, dcn_ref, u_ref, up_ref, w_ref, du_ref, dw_ref, db_ref, glu_ref, dpad_ref, dglu_ref, wacc_ref):
        i = pl.program_id(0)
        glu_ref[HALO:, :] = u_ref[:, :D] * _sigmoid(u_ref[:, D:])
        glu_ref[:HALO, :] = jnp.where(i > 0, up_ref[:, :D] * _sigmoid(up_ref[:, D:]), 0.0)
        dpad_ref[:tm, :] = dc_ref[...]
        dpad_ref[tm:, :] = jnp.where(i < nt - 1, dcn_ref[...], 0.0)

        @pl.when(i == 0)
        def _():
            wacc_ref[...] = jnp.zeros_like(wacc_ref)

        def block(t0, ls):
            gwin = glu_ref[pl.ds(t0, CONV_ROWS + HALO), ls]
            dwin = dpad_ref[pl.ds(t0, CONV_ROWS + HALO), ls]
            dcur = dwin[:CONV_ROWS]
            dglu = jnp.zeros((CONV_ROWS, LANES), F32)
            for k in range(K):
                dglu = dglu + w_ref[k:k + 1, ls] * _shifted(dwin, (K - 1) - k, CONV_ROWS)
                gs = _shifted(gwin, HALO - (K - 1) + k, CONV_ROWS)
                wacc_ref[k * SUBLANES:(k + 1) * SUBLANES, ls] += _rowsum8(dcur * gs)
            dglu_ref[pl.ds(t0, CONV_ROWS), ls] = dglu

        _conv_loops(tm, D, block)
        dglu = dglu_ref[...]
        a = u_ref[:, :D]
        sg = _sigmoid(u_ref[:, D:])
        da = dglu * sg
        dg = dglu * a * (sg * (1.0 - sg))
        du_ref[:, :D] = da.astype(du_ref.dtype)
        du_ref[:, D:] = dg.astype(du_ref.dtype)
        pa = jnp.sum(da, axis=0, keepdims=True)
        pg = jnp.sum(dg, axis=0, keepdims=True)

        @pl.when(i == 0)
        def _():
            db_ref[:, :D] = pa
            db_ref[:, D:] = pg

        @pl.when(i > 0)
        def _():
            db_ref[:, :D] += pa
            db_ref[:, D:] += pg

        @pl.when(i == nt - 1)
        def _():
            for k in range(K):
                dw_ref[k:k + 1, :] = jnp.sum(wacc_ref[k * SUBLANES:(k + 1) * SUBLANES, :], axis=0, keepdims=True)

    return pl.pallas_call(
        body, name=name, grid=(nt,),
        in_specs=[pl.BlockSpec((tm, D), lambda i: (i, 0)), _next_halo_spec(tm, D, T),
                  pl.BlockSpec((tm, D2), lambda i: (i, 0)), _prev_halo_spec(tm, D2),
                  pl.BlockSpec((K, D), lambda i: (0, 0))],
        out_specs=[pl.BlockSpec((tm, D2), lambda i: (i, 0)), pl.BlockSpec((K, D), lambda i: (0, 0)),
                   pl.BlockSpec((1, D2), lambda i: (0, 0))],
        out_shape=[jax.ShapeDtypeStruct((T, D2), BF16), jax.ShapeDtypeStruct((K, D), F32),
                   jax.ShapeDtypeStruct((1, D2), F32)],
        scratch_shapes=[pltpu.VMEM((tm + HALO, D), F32), pltpu.VMEM((tm + HALO, D), F32),
                        pltpu.VMEM((tm, D), F32), pltpu.VMEM((K * SUBLANES, D), F32)],
        compiler_params=_params(("arbitrary",)),
    )(dcu, dcu, u, u, w)


def mm_cols(a, w, lyr, bias, name):
    T, K = a.shape
    _, S, _, n = w.shape
    tm = _tile(T, WIDE_TOKEN_TILE)

    def body(*refs):
        a_ref, w_ref = refs[:2]
        o_ref = refs[-1]
        acc = jnp.dot(a_ref[...], w_ref[...], preferred_element_type=F32)
        if bias is not None:
            acc = acc + refs[2][...]
        o_ref[...] = acc

    in_specs = [pl.BlockSpec((tm, K), lambda s, i: (i, 0)), pl.BlockSpec((None, None, K, n), lambda s, i: (lyr, s, 0, 0))]
    args = [a, w]
    if bias is not None:
        in_specs.append(pl.BlockSpec((1, n), lambda s, i: (0, s)))
        args.append(bias)
    return pl.pallas_call(
        body, name=name, grid=(S, T // tm), in_specs=in_specs,
        out_specs=pl.BlockSpec((tm, n), lambda s, i: (i, s)),
        out_shape=jax.ShapeDtypeStruct((T, S * n), F32),
        compiler_params=_params(("parallel", "parallel")),
    )(*args)


def _load_once(pairs, sems):
    cps = [pltpu.make_async_copy(src, dst, sems.at[k]) for k, (src, dst) in enumerate(pairs)]
    for cp in cps:
        cp.start()
    for cp in cps:
        cp.wait()


def ffn_fwd(h, gain, wg, wu, wd, lyr, name):
    T, D = h.shape
    _, S, _, f = wg.shape
    tm = _tile(T, TOKEN_TILE)

    def body(h_ref, gain_ref, wg_hbm, wu_hbm, wd_hbm, n_ref, g_ref, u_ref, gu_ref, o_ref, wg_v, wu_v, wd_v, sems):
        i, s = pl.program_id(0), pl.program_id(1)

        @pl.when((i == 0) & (s == 0))
        def _():
            _load_once([(wg_hbm.at[lyr], wg_v), (wu_hbm.at[lyr], wu_v), (wd_hbm.at[lyr], wd_v)], sems)

        @pl.when(s == 0)
        def _():
            x = h_ref[...]
            r = lax.rsqrt(jnp.mean(x * x, axis=-1, keepdims=True) + RMS_EPS)
            n_ref[...] = (x * r * gain_ref[...]).astype(n_ref.dtype)

        a = n_ref[...]
        g = jnp.dot(a, wg_v[s], preferred_element_type=F32)
        u = jnp.dot(a, wu_v[s], preferred_element_type=F32)
        gu = (g * _sigmoid(g) * u).astype(gu_ref.dtype)
        g_ref[...] = g.astype(g_ref.dtype)
        u_ref[...] = u.astype(u_ref.dtype)
        gu_ref[...] = gu
        part = jnp.dot(gu, wd_v[s], preferred_element_type=F32)

        @pl.when(s == 0)
        def _():
            o_ref[...] = h_ref[...] + part

        @pl.when(s > 0)
        def _():
            o_ref[...] += part

    row = pl.BlockSpec((tm, D), lambda i, s: (i, 0))
    seg = pl.BlockSpec((None, tm, f), lambda i, s: (s, i, 0))
    hbm = pl.BlockSpec(memory_space=pl.ANY)
    segs = jax.ShapeDtypeStruct((S, T, f), BF16)
    return pl.pallas_call(
        body, name=name, grid=(T // tm, S),
        in_specs=[row, pl.BlockSpec((1, D), lambda i, s: (0, 0)), hbm, hbm, hbm],
        out_specs=[row, seg, seg, seg, row],
        out_shape=[jax.ShapeDtypeStruct((T, D), BF16), segs, segs, segs, jax.ShapeDtypeStruct((T, D), F32)],
        scratch_shapes=[pltpu.VMEM((S, D, f), BF16), pltpu.VMEM((S, D, f), BF16), pltpu.VMEM((S, f, D), BF16),
                        pltpu.SemaphoreType.DMA((3,))],
        compiler_params=_params(("arbitrary", "arbitrary")),
    )(h, gain, wg, wu, wd)


def ffn_bwd(dy, h, gain, g, u, wd, wg, wu, lyr, name):
    T, D = h.shape
    _, S, _, f = wg.shape
    tm = _tile(T, TOKEN_TILE)
    nt = T // tm

    def body(dy_ref, h_ref, gain_ref, g_ref, u_ref, wd_hbm, wg_hbm, wu_hbm, dg_ref, du_ref, dh_ref, dgain_ref,
             wd_v, wg_v, wu_v, dyb_ref, sems):
        i, s = pl.program_id(0), pl.program_id(1)

        @pl.when((i == 0) & (s == 0))
        def _():
            _load_once([(wd_hbm.at[lyr], wd_v), (wg_hbm.at[lyr], wg_v), (wu_hbm.at[lyr], wu_v)], sems)

        @pl.when(s == 0)
        def _():
            dyb_ref[...] = dy_ref[...].astype(dyb_ref.dtype)

        dgu = lax.dot_general(dyb_ref[...], wd_v[s], _NT, preferred_element_type=F32)
        gv = g_ref[...].astype(F32)
        sg = _sigmoid(gv)
        dg = (dgu * u_ref[...].astype(F32) * (sg * (1.0 + gv * (1.0 - sg)))).astype(dg_ref.dtype)
        du = (dgu * (gv * sg)).astype(du_ref.dtype)
        dg_ref[...] = dg
        du_ref[...] = du
        part = (lax.dot_general(dg, wg_v[s], _NT, preferred_element_type=F32)
                + lax.dot_general(du, wu_v[s], _NT, preferred_element_type=F32))

        @pl.when(s == 0)
        def _():
            dh_ref[...] = part

        @pl.when(s > 0)
        def _():
            dh_ref[...] += part

        @pl.when(s == S - 1)
        def _():
            dn = dh_ref[...]
            x = h_ref[...]
            r = lax.rsqrt(jnp.mean(x * x, axis=-1, keepdims=True) + RMS_EPS)
            xhat = x * r
            dxhat = dn * gain_ref[...]
            dh_ref[...] = dy_ref[...] + r * (dxhat - xhat * jnp.mean(dxhat * xhat, axis=-1, keepdims=True))
            pg = jnp.sum(dn * xhat, axis=0, keepdims=True)

            @pl.when(i == 0)
            def _():
                dgain_ref[...] = pg

            @pl.when(i > 0)
            def _():
                dgain_ref[...] += pg

    row = pl.BlockSpec((tm, D), lambda i, s: (i, 0))
    vec = pl.BlockSpec((1, D), lambda i, s: (0, 0))
    seg = pl.BlockSpec((None, tm, f), lambda i, s: (s, i, 0))
    hbm = pl.BlockSpec(memory_space=pl.ANY)
    segs = jax.ShapeDtypeStruct((S, T, f), BF16)
    return pl.pallas_call(
        body, name=name, grid=(nt, S),
        in_specs=[row, row, vec, seg, seg, hbm, hbm, hbm],
        out_specs=[seg, seg, row, vec],
        out_shape=[segs, segs, jax.ShapeDtypeStruct((T, D), F32), jax.ShapeDtypeStruct((1, D), F32)],
        scratch_shapes=[pltpu.VMEM((S, f, D), BF16), pltpu.VMEM((S, D, f), BF16), pltpu.VMEM((S, D, f), BF16),
                        pltpu.VMEM((tm, D), BF16), pltpu.SemaphoreType.DMA((3,))],
        compiler_params=_params(("arbitrary", "arbitrary")),
    )(dy, h, gain, g, u, wd, wg, wu)


def mm_rows(a, w, lyr, res, bias, name):
    S, T, k = a.shape
    N = w.shape[-1]
    tm = _tile(T, TOKEN_TILE)

    def body(*refs):
        a_ref, w_ref, r_ref = refs[:3]
        o_ref = refs[-1]
        s = pl.program_id(1)
        acc = jnp.dot(a_ref[...], w_ref[...], preferred_element_type=F32)

        @pl.when(s == 0)
        def _():
            base = r_ref[...]
            if bias is not None:
                base = base + refs[3][...]
            o_ref[...] = base + acc

        @pl.when(s > 0)
        def _():
            o_ref[...] += acc

    in_specs = [pl.BlockSpec((None, tm, k), lambda i, s: (s, i, 0)),
                pl.BlockSpec((None, None, k, N), lambda i, s: (lyr, s, 0, 0)),
                pl.BlockSpec((tm, N), lambda i, s: (i, 0))]
    args = [a, w, res]
    if bias is not None:
        in_specs.append(pl.BlockSpec((1, N), lambda i, s: (0, 0)))
        args.append(bias)
    return pl.pallas_call(
        body, name=name, grid=(T // tm, S), in_specs=in_specs,
        out_specs=pl.BlockSpec((tm, N), lambda i, s: (i, 0)),
        out_shape=jax.ShapeDtypeStruct((T, N), F32),
        compiler_params=_params(("parallel", "arbitrary")),
    )(*args)


_NT = (((1,), (1,)), ((), ()))
_TN = (((0,), (0,)), ((), ()))


def nt_rows(dy, w, lyr, want_colsum, name):
    T, N = dy.shape
    _, S, k, _ = w.shape
    tm = _tile(T, TOKEN_TILE)

    def body(dy_ref, w_ref, o_ref, *rest):
        i, s = pl.program_id(0), pl.program_id(1)
        d = dy_ref[...]
        o_ref[...] = lax.dot_general(d.astype(BF16), w_ref[...], _NT, preferred_element_type=F32)
        if want_colsum:
            cs_ref = rest[0]
            part = jnp.sum(d, axis=0, keepdims=True)

            @pl.when((i == 0) & (s == 0))
            def _():
                cs_ref[...] = part

            @pl.when((i > 0) & (s == 0))
            def _():
                cs_ref[...] += part

    out_specs = [pl.BlockSpec((None, tm, k), lambda i, s: (s, i, 0))]
    out_shape = [jax.ShapeDtypeStruct((S, T, k), F32)]
    if want_colsum:
        out_specs.append(pl.BlockSpec((1, N), lambda i, s: (0, 0)))
        out_shape.append(jax.ShapeDtypeStruct((1, N), F32))
    return pl.pallas_call(
        body, name=name, grid=(T // tm, S),
        in_specs=[pl.BlockSpec((tm, N), lambda i, s: (i, 0)), pl.BlockSpec((None, None, k, N), lambda i, s: (lyr, s, 0, 0))],
        out_specs=out_specs, out_shape=out_shape,
        compiler_params=_params(("arbitrary", "arbitrary")),
    )(dy, w)


def nt_cols_rms(dy, w, lyr, h, gain, dres, name):
    T, K = h.shape
    _, S, _, n = w.shape
    tm = _tile(T, TOKEN_TILE)

    def body(dy_ref, w_ref, h_ref, gain_ref, dres_ref, dh_ref, dgain_ref):
        i = pl.program_id(0)
        dn = None
        for s in range(S):
            part = lax.dot_general(dy_ref[:, s * n:(s + 1) * n], w_ref[s], _NT, preferred_element_type=F32)
            dn = part if dn is None else dn + part
        x = h_ref[...]
        r = lax.rsqrt(jnp.mean(x * x, axis=-1, keepdims=True) + RMS_EPS)
        xhat = x * r
        dxhat = dn * gain_ref[...]
        dh_ref[...] = dres_ref[...] + r * (dxhat - xhat * jnp.mean(dxhat * xhat, axis=-1, keepdims=True))
        pg = jnp.sum(dn * xhat, axis=0, keepdims=True)

        @pl.when(i == 0)
        def _():
            dgain_ref[...] = pg

        @pl.when(i > 0)
        def _():
            dgain_ref[...] += pg

    row = pl.BlockSpec((tm, K), lambda i: (i, 0))
    vec = pl.BlockSpec((1, K), lambda i: (0, 0))
    return pl.pallas_call(
        body, name=name, grid=(T // tm,),
        in_specs=[pl.BlockSpec((tm, S * n), lambda i: (i, 0)), pl.BlockSpec((None, S, K, n), lambda i: (lyr, 0, 0, 0)),
                  row, vec, row],
        out_specs=[row, vec],
        out_shape=[jax.ShapeDtypeStruct((T, K), F32), jax.ShapeDtypeStruct((1, K), F32)],
        compiler_params=_params(("arbitrary",)),
    )(dy, w, h, gain, dres)


def tn_grad(a, dy, S, a_by_seg, name):
    T = dy.shape[0] if dy.ndim == 2 else dy.shape[1]
    tt = _tile(T, GRAD_TOKEN_TILE)
    if a_by_seg:
        R = a.shape[1] // S if a.ndim == 2 else a.shape[2]
        C = dy.shape[1]
        a_spec = pl.BlockSpec((tt, R), lambda s, t: (t, s)) if a.ndim == 2 else pl.BlockSpec((None, tt, R), lambda s, t: (s, t, 0))
        b_spec = pl.BlockSpec((tt, C), lambda s, t: (t, 0))
    else:
        R = a.shape[1]
        C = dy.shape[1] // S if dy.ndim == 2 else dy.shape[2]
        a_spec = pl.BlockSpec((tt, R), lambda s, t: (t, 0))
        b_spec = pl.BlockSpec((tt, C), lambda s, t: (t, s)) if dy.ndim == 2 else pl.BlockSpec((None, tt, C), lambda s, t: (s, t, 0))
    Rh = R // 2
    nt = T // tt

    def body(a_ref, b_ref, o_ref, acc_ref):
        t = pl.program_id(1)
        part = lax.dot_general(a_ref[...], b_ref[...].astype(BF16), _TN, preferred_element_type=F32)

        @pl.when(t == 0)
        def _():
            acc_ref[...] = part

        @pl.when(t > 0)
        def _():
            acc_ref[...] += part

        @pl.when(t == nt - 1)
        def _():
            o_ref[0] = acc_ref[:Rh, :].astype(o_ref.dtype)
            o_ref[1] = acc_ref[Rh:, :].astype(o_ref.dtype)

    return pl.pallas_call(
        body, name=name, grid=(S, nt), in_specs=[a_spec, b_spec],
        out_specs=pl.BlockSpec((2, None, Rh, C), lambda s, t: (0, s, 0, 0)),
        out_shape=jax.ShapeDtypeStruct((2, S, Rh, C), BF16),
        scratch_shapes=[pltpu.VMEM((R, C), F32)],
        compiler_params=_params(("parallel", "arbitrary")),
    )(a, dy)


def _place():
    x, y, c = lax.axis_index("x"), lax.axis_index("y"), lax.axis_index("c")
    chips = [(1 - x, y), (x, 1 - y), (1 - x, 1 - y)]
    return x, y, c, chips


def _any_specs(n):
    return [pl.BlockSpec(memory_space=pl.ANY)] * n


def _remote(src, dst, send_sem, recv_sem, dev):
    return pltpu.make_async_remote_copy(src_ref=src, dst_ref=dst, send_sem=send_sem, recv_sem=recv_sem,
                                        device_id=dev, device_id_type=MESH)


def small_allgather(v, name):
    rows, W = v.shape

    def body(v_ref, o_ref, send_sems, recv_sems, loc_sem):
        x, y, c, _ = _place()
        mine = pltpu.make_async_copy(v_ref, o_ref.at[4 * x + 2 * y + c], loc_sem)
        mine.start()

        def peer_of(m):
            return ((1 - x) if m & 4 else x, (1 - y) if m & 2 else y, (1 - c) if m & 1 else c)

        sends = []
        for m in range(1, N_DEV):
            cp = _remote(v_ref, o_ref.at[4 * x + 2 * y + c], send_sems.at[m - 1], recv_sems.at[m - 1], peer_of(m))
            cp.start()
            sends.append(cp)
        for m in range(1, N_DEV):
            px, py, pc = peer_of(m)
            blk = o_ref.at[4 * px + 2 * py + pc]
            _remote(blk, blk, send_sems.at[m - 1], recv_sems.at[m - 1], (px, py, pc)).wait_recv()
        for cp in sends:
            cp.wait_send()
        mine.wait()

    return pl.pallas_call(
        body, name=name,
        in_specs=[pl.BlockSpec(memory_space=pltpu.VMEM)], out_specs=pl.BlockSpec(memory_space=pltpu.VMEM),
        out_shape=jax.ShapeDtypeStruct((N_DEV, rows, W), F32),
        scratch_shapes=[pltpu.SemaphoreType.DMA((N_DEV - 1,)), pltpu.SemaphoreType.DMA((N_DEV - 1,)), pltpu.SemaphoreType.DMA],
    )(v)


def gather_weights(shards, name):
    units = [(w, l) for w in range(len(shards)) for l in range(shards[w].shape[0])]
    nu = len(units)
    n = len(shards)

    def body(*refs):
        ins, outs = refs[:n], refs[n:2 * n]
        send_sems, recv_sems = refs[2 * n:]
        x, y, c, chips = _place()
        me = 2 * x + y
        sib = (x, y, 1 - c)

        def half(w, l, chip, hc):
            rh = ins[w].shape[1] // 2
            return outs[w].at[l, chip, pl.ds(hc * rh, rh)]

        sends = []
        for k, (w, l) in enumerate(units):
            rh = ins[w].shape[1] // 2
            cp = _remote(ins[w].at[l], outs[w].at[l, me], send_sems.at[k, 6], recv_sems.at[k, 6], sib)
            cp.start()
            sends.append(cp)
            for j, (px, py) in enumerate(chips):
                cp = _remote(ins[w].at[l, pl.ds(c * rh, rh)], half(w, l, me, c), send_sems.at[k, j], recv_sems.at[k, j], (px, py, c))
                cp.start()
                sends.append(cp)
        for k, (w, l) in enumerate(units):
            for j, (px, py) in enumerate(chips):
                blk = half(w, l, 2 * px + py, c)
                _remote(blk, blk, send_sems.at[k, j], recv_sems.at[k, j], (px, py, c)).wait_recv()
                cp = _remote(blk, blk, send_sems.at[k, 3 + j], recv_sems.at[k, 3 + j], sib)
                cp.start()
                sends.append(cp)
        for k, (w, l) in enumerate(units):
            own = outs[w].at[l, me]
            _remote(own, own, send_sems.at[k, 6], recv_sems.at[k, 6], sib).wait_recv()
            for j, (px, py) in enumerate(chips):
                blk = half(w, l, 2 * px + py, 1 - c)
                _remote(blk, blk, send_sems.at[k, 3 + j], recv_sems.at[k, 3 + j], sib).wait_recv()
        for cp in sends:
            cp.wait_send()

    return pl.pallas_call(
        body, name=name, in_specs=_any_specs(n), out_specs=_any_specs(n),
        out_shape=[jax.ShapeDtypeStruct((s.shape[0], N_CHIPS) + s.shape[1:], s.dtype) for s in shards],
        scratch_shapes=[pltpu.SemaphoreType.DMA((nu, 7)), pltpu.SemaphoreType.DMA((nu, 7))],
    )(*shards)


def sibling_halves(grads, name):
    n = len(grads)

    def body(*refs):
        ins, outs = refs[:n], refs[n:2 * n]
        send_sems, recv_sems = refs[2 * n:]
        x, y, c, _ = _place()
        cps = []
        for k in range(n):
            cp = _remote(ins[k].at[1 - c], outs[k], send_sems.at[k], recv_sems.at[k], (x, y, 1 - c))
            cp.start()
            cps.append(cp)
        for cp in cps:
            cp.wait_recv()
        for cp in cps:
            cp.wait_send()

    return pl.pallas_call(
        body, name=name, in_specs=_any_specs(n), out_specs=_any_specs(n),
        out_shape=[jax.ShapeDtypeStruct(g.shape[1:], g.dtype) for g in grads],
        scratch_shapes=[pltpu.SemaphoreType.DMA((n,)), pltpu.SemaphoreType.DMA((n,))],
    )(*grads)


def pair_sum(gh, recv, cidx, name):
    _, S, Rh, C = gh.shape

    def body(c_ref, a_ref, b_ref, o_ref):
        o_ref[...] = (a_ref[...].astype(F32) + b_ref[...].astype(F32)).astype(o_ref.dtype)

    return pl.pallas_call(
        body, name=name, out_shape=jax.ShapeDtypeStruct((S, Rh, C), BF16),
        grid_spec=pltpu.PrefetchScalarGridSpec(
            num_scalar_prefetch=1, grid=(S,),
            in_specs=[pl.BlockSpec((None, None, Rh, C), lambda s, c_ref: (c_ref[0], s, 0, 0)),
                      pl.BlockSpec((None, Rh, C), lambda s, c_ref: (s, 0, 0))],
            out_specs=pl.BlockSpec((None, Rh, C), lambda s, c_ref: (s, 0, 0))),
        compiler_params=_params(("parallel",)),
    )(cidx, gh, recv)


def scatter_to_owners(parts, name):
    n = len(parts)

    def body(*refs):
        ins, outs = refs[:n], refs[n:2 * n]
        send_sems, recv_sems = refs[2 * n:]
        x, y, c, chips = _place()
        me = 2 * x + y
        sib = (x, y, 1 - c)
        sends = []
        for k in range(n):
            cp = _remote(ins[k].at[me], outs[k].at[me, c], send_sems.at[k, 6], recv_sems.at[k, 6], sib)
            cp.start()
            sends.append(cp)
            for j, (px, py) in enumerate(chips):
                cp = _remote(ins[k].at[2 * px + py], outs[k].at[me, c], send_sems.at[k, j], recv_sems.at[k, j], (px, py, c))
                cp.start()
                sends.append(cp)
        for k in range(n):
            for j, (px, py) in enumerate(chips):
                blk = outs[k].at[2 * px + py, c]
                _remote(blk, blk, send_sems.at[k, j], recv_sems.at[k, j], (px, py, c)).wait_recv()
                cp = _remote(blk, blk, send_sems.at[k, 3 + j], recv_sems.at[k, 3 + j], sib)
                cp.start()
                sends.append(cp)
        for k in range(n):
            own = outs[k].at[me, 1 - c]
            _remote(own, own, send_sems.at[k, 6], recv_sems.at[k, 6], sib).wait_recv()
            for j, (px, py) in enumerate(chips):
                blk = outs[k].at[2 * px + py, 1 - c]
                _remote(blk, blk, send_sems.at[k, 3 + j], recv_sems.at[k, 3 + j], sib).wait_recv()
        for cp in sends:
            cp.wait_send()

    return pl.pallas_call(
        body, name=name, in_specs=_any_specs(n), out_specs=_any_specs(n),
        out_shape=[jax.ShapeDtypeStruct((p.shape[0], 2) + p.shape[1:], p.dtype) for p in parts],
        scratch_shapes=[pltpu.SemaphoreType.DMA((n, 7)), pltpu.SemaphoreType.DMA((n, 7))],
    )(*parts)


def _adamw_math(w, g, m, v):
    m = ADAM_B1 * m + (1.0 - ADAM_B1) * g
    v = ADAM_B2 * v + (1.0 - ADAM_B2) * (g * g)
    m_hat = m / (1.0 - ADAM_B1 ** ADAM_STEP)
    v_hat = v / (1.0 - ADAM_B2 ** ADAM_STEP)
    delta = -ADAM_LR * (m_hat / (jnp.sqrt(v_hat) + ADAM_EPS) + ADAM_WD * w)
    return delta, m, v


def adamw_reduce(w, m, v, buf, part, place, lyr, bases, name):
    L, R, C = w.shape
    Rh = R // 2
    rb = _tile(Rh, ROW_TILE, 2 * SUBLANES)
    nb = Rh // rb

    def body(place_ref, p_ref, b0, b1, b2, b3, w_ref, m_ref, v_ref, *rest):
        go_ref, d_ref, mo_ref, vo_ref = rest[-4:]
        mine = (place_ref[1] == pl.program_id(0))
        g = None
        for p, b in enumerate((b0, b1, b2, b3)):
            val = jnp.where(mine & (place_ref[0] == p), p_ref[...], b[...]).astype(F32)
            g = val if g is None else g + val
        d, mn, vn = _adamw_math(w_ref[...], g, m_ref[...], v_ref[...])
        go_ref[...] = g
        d_ref[...] = d
        mo_ref[...] = mn
        vo_ref[...] = vn

    def buf_spec(p):
        def idx(h, i, pr):
            own = (pr[0] == p) & (pr[1] == h)
            return (p, jnp.where(own, 1 - h, h), i, 0)
        return pl.BlockSpec((None, None, rb, C), idx)

    blk = pl.BlockSpec((None, rb, C), lambda h, i, pr: (lyr, h * nb + i, 0))
    in_specs = [pl.BlockSpec((None, rb, C), lambda h, i, pr: (pr[0], i, 0))] + [buf_spec(p) for p in range(N_CHIPS)] + [blk] * 3
    args = [place, part, buf, buf, buf, buf, w, m, v]
    aliases = {}
    if bases is not None:
        in_specs += [pl.BlockSpec(memory_space=pl.ANY)] * 4
        aliases = {len(args) + k: k for k in range(4)}
        args += list(bases)
    shp = jax.ShapeDtypeStruct((L, R, C), F32)
    return pl.pallas_call(
        body, name=name, out_shape=[shp] * 4,
        grid_spec=pltpu.PrefetchScalarGridSpec(num_scalar_prefetch=1, grid=(2, nb), in_specs=in_specs, out_specs=[blk] * 4),
        input_output_aliases=aliases,
        compiler_params=_params(("parallel", "parallel")),
    )(*args)


def small_update(gall, chip, entries, name):
    ne = len(entries)
    D = gall.shape[2]

    def body(chip_ref, gall_ref, *refs):
        ins, outs = refs[:3 * ne], refs[3 * ne:]
        ch = chip_ref[0]
        for e, (row0, kind, w, _, _) in enumerate(entries):
            r, width = w.shape

            def gsum(rs, cs):
                acc = gall_ref[0, rs, cs]
                for d in range(1, N_DEV):
                    acc = acc + gall_ref[d, rs, cs]
                return acc

            if kind == "full":
                g = gsum(slice(row0, row0 + r), slice(0, D))
            elif kind == "cols":
                g = gsum(slice(row0, row0 + r), slice(0, width))
                for q in range(1, N_CHIPS):
                    g = jnp.where(ch == q, gsum(slice(row0, row0 + r), slice(q * width, (q + 1) * width)), g)
            else:
                per_row = D // width
                g = gsum(slice(row0, row0 + 1), slice(0, width))
                for q in range(1, N_CHIPS):
                    rr = row0 + q // per_row
                    cc = (q % per_row) * width
                    g = jnp.where(ch == q, gsum(slice(rr, rr + 1), slice(cc, cc + width)), g)
            d, mn, vn = _adamw_math(ins[3 * e][...], g, ins[3 * e + 1][...], ins[3 * e + 2][...])
            outs[4 * e][...] = g
            outs[4 * e + 1][...] = d
            outs[4 * e + 2][...] = mn
            outs[4 * e + 3][...] = vn

    vm = pl.BlockSpec(memory_space=pltpu.VMEM)
    args, out_shape = [], []
    for _, _, w, m, v in entries:
        args += [w, m, v]
        out_shape += [jax.ShapeDtypeStruct(w.shape, F32)] * 4
    return pl.pallas_call(
        body, name=name,
        in_specs=[pl.BlockSpec(memory_space=pltpu.SMEM), vm] + [vm] * (3 * ne),
        out_specs=[vm] * (4 * ne), out_shape=out_shape,
        compiler_params=pltpu.CompilerParams(vmem_limit_bytes=VMEM_LIMIT),
    )(chip, gall, *args)


def _pack_rows(items, width):
    rows, starts, at = [], [], 0
    for it in items:
        r = it.shape[0]
        pad = (-r) % SUBLANES
        starts.append(at)
        rows.append(it)
        if pad:
            rows.append(jnp.zeros((pad, width), F32))
        at += r + pad
    return jnp.concatenate(rows, axis=0), starts


def kernel(x, a_norm, a_w_in, a_conv, a_w_out, b_norm, b_w_pw1, b_b_pw1, b_conv, b_b_conv, b_ln_g, b_ln_b, b_w_pw2, b_b_pw2, ffn_norm, ffn_w_gate, ffn_w_up, ffn_w_down, final_norm, loss_target, m_a_norm, m_a_w_in, m_a_conv, m_a_w_out, m_b_norm, m_b_w_pw1, m_b_b_pw1, m_b_conv, m_b_b_conv, m_b_ln_g, m_b_ln_b, m_b_w_pw2, m_b_b_pw2, m_ffn_norm, m_ffn_w_gate, m_ffn_w_up, m_ffn_w_down, m_final_norm, v_a_norm, v_a_w_in, v_a_conv, v_a_w_out, v_b_norm, v_b_w_pw1, v_b_b_pw1, v_b_conv, v_b_b_conv, v_b_ln_g, v_b_ln_b, v_b_w_pw2, v_b_b_pw2, v_ffn_norm, v_ffn_w_gate, v_ffn_w_up, v_ffn_w_down, v_final_norm):
    T, D = x.shape[1], x.shape[2]
    Dq = D // N_CHIPS
    cx, cy, cc = lax.axis_index("x"), lax.axis_index("y"), lax.axis_index("c")
    chip = (2 * cx + cy).astype(jnp.int32).reshape(1)
    cidx = cc.astype(jnp.int32).reshape(1)
    h0 = x.reshape(T, D)
    tgt = loss_target.reshape(T, D)

    small_shards = [a_conv[0], b_norm, b_b_pw1.reshape(2, Dq), b_conv[0], b_b_conv, b_ln_g, b_ln_b, b_b_pw2]
    packed, st = _pack_rows(small_shards, Dq)
    sw = small_allgather(packed, "gather_small")[0::2]

    def whole(k, r):
        return jnp.transpose(sw[:, st[k]:st[k] + r, :], (1, 0, 2)).reshape(r, D)

    a_conv_f, b_norm_f = whole(0, 3), whole(1, 1)
    b_b_pw1_f = sw[:, st[2]:st[2] + 2, :].reshape(1, 2 * D)
    b_conv_f, b_b_conv_f, b_ln_g_f, b_ln_b_f, b_b_pw2_f = whole(3, b_conv.shape[1]), whole(4, 1), whole(5, 1), whole(6, 1), whole(7, 1)

    big = [a_w_in, a_w_out, b_w_pw1, b_w_pw2, ffn_w_gate, ffn_w_up, ffn_w_down]
    g_in, g_out, g_pw1, g_pw2, g_gate, g_up, g_down = gather_weights([w.astype(BF16) for w in big], "gather_weights")
    g_out = g_out.reshape(1, 1, D, D)
    g_pw2 = g_pw2.reshape(1, 1, D, D)

    n0 = rms_fwd(h0, a_norm, "rms_a")
    bcv = mm_cols(n0, g_in, 0, None, "mm_w_in")
    ya = gateconv_fwd(bcv, a_conv_f, "gateconv_fwd")
    h1 = mm_rows(ya[None], g_out, 0, h0, None, "mm_w_out")
    n1, fg0, fu0, gu0, h2 = ffn_fwd(h1, ffn_norm[0:1], g_gate, g_up, g_down, 0, "ffn_fwd0")
    n2 = rms_fwd(h2, b_norm_f, "rms_b")
    ub = mm_cols(n2, g_pw1, 0, b_b_pw1_f, "mm_pw1")
    cu, sb = bconv_fwd(ub, b_conv_f, b_b_conv_f, b_ln_g_f, b_ln_b_f, "bconv_fwd")
    h3 = mm_rows(sb[None], g_pw2, 0, h2, b_b_pw2_f, "mm_pw2")
    n3, fg1, fu1, gu1, h4 = ffn_fwd(h3, ffn_norm[1:2], g_gate, g_up, g_down, 1, "ffn_fwd1")
    loss_part, dh4, d_final = loss_head(h4, final_norm.reshape(1, D), tgt, "loss_head")

    def ffn_back(dh, h_in, n_in, fg, fu, gu, lyr, tag):
        dg, du, dh_in, dnorm = ffn_bwd(dh, h_in, ffn_norm[lyr:lyr + 1], fg, fu, g_down, g_gate, g_up, lyr, "ffn_bwd" + tag)
        gd = tn_grad(gu, dh, N_CHIPS, True, "tn_down" + tag)
        gg = tn_grad(n_in, dg, N_CHIPS, False, "tn_gate" + tag)
        gu_ = tn_grad(n_in, du, N_CHIPS, False, "tn_up" + tag)
        return dh_in, dnorm, gg, gu_, gd

    dh3, d_fn1, gh_gate1, gh_up1, gh_down1 = ffn_back(dh4, h3, n3, fg1, fu1, gu1, 1, "1")

    ds, d_b_pw2 = nt_rows(dh3, g_pw2, 0, True, "nt_pw2")
    gh_pw2 = tn_grad(sb, dh3, N_CHIPS, True, "tn_pw2")
    dcu, d_ln_g, d_ln_b, d_b_conv = ln_silu_bwd(ds[0], cu, b_ln_g_f, b_ln_b_f, "ln_silu_bwd")
    dub, d_bconv_w, d_b_pw1 = bconv_bwd(dcu, ub, b_conv_f, "bconv_bwd")
    gh_pw1 = tn_grad(n2, dub, N_CHIPS, False, "tn_pw1")
    dh2, d_b_norm = nt_cols_rms(dub, g_pw1, 0, h2, b_norm_f, dh3, "nt_pw1")

    dh1, d_fn0, gh_gate0, gh_up0, gh_down0 = ffn_back(dh2, h1, n1, fg0, fu0, gu0, 0, "0")

    dya = nt_rows(dh1, g_out, 0, False, "nt_w_out")[0]
    gh_out = tn_grad(ya, dh1, N_CHIPS, True, "tn_w_out")
    dbcv, d_aconv_w = gateconv_bwd(dya[0], bcv, a_conv_f, "gateconv_bwd")
    gh_in = tn_grad(n0, dbcv, N_CHIPS, False, "tn_w_in")
    grad_x, d_a_norm = nt_cols_rms(dbcv, g_in, 0, h0, a_norm, dh1, "nt_w_in")

    ghs = [gh_in, gh_out, gh_pw1, gh_pw2, gh_gate0, gh_gate1, gh_up0, gh_up1, gh_down0, gh_down1]
    tags = ["in", "out", "pw1", "pw2", "gate0", "gate1", "up0", "up1", "down0", "down1"]
    from_sib = sibling_halves(ghs, "reduce_sibling")
    parts = [pair_sum(g, r, cidx, "pair_sum_" + t) for g, r, t in zip(ghs, from_sib, tags)]
    at_owner = scatter_to_owners(parts, "reduce_scatter")
    place = jnp.concatenate([chip, cidx])

    def upd(w, m, v, ks, tag):
        res = None
        for lyr, k in enumerate(ks):
            res = adamw_reduce(w, m, v, at_owner[k], parts[k], place, lyr, res, "adamw_%s%d" % (tag, lyr))
        return res

    r_in = upd(a_w_in, m_a_w_in, v_a_w_in, [0], "w_in")
    r_out = upd(a_w_out, m_a_w_out, v_a_w_out, [1], "w_out")
    r_pw1 = upd(b_w_pw1, m_b_w_pw1, v_b_w_pw1, [2], "pw1")
    r_pw2 = upd(b_w_pw2, m_b_w_pw2, v_b_w_pw2, [3], "pw2")
    r_gate = upd(ffn_w_gate, m_ffn_w_gate, v_ffn_w_gate, [4, 5], "gate")
    r_up = upd(ffn_w_up, m_ffn_w_up, v_ffn_w_up, [6, 7], "up")
    r_down = upd(ffn_w_down, m_ffn_w_down, v_ffn_w_down, [8, 9], "down")

    d_ffn_norm = jnp.concatenate([d_fn0, d_fn1], axis=0)
    small_grads = [d_a_norm, d_aconv_w, d_b_norm, d_b_pw1.reshape(2, D), d_bconv_w, d_b_conv, d_ln_g, d_ln_b, d_b_pw2,
                   d_ffn_norm, d_final]
    gpacked, gs = _pack_rows(small_grads, D)
    gall = small_allgather(gpacked, "gather_small_grads")
    entries = [
        (gs[0], "full", a_norm, m_a_norm, v_a_norm),
        (gs[1], "cols", a_conv[0], m_a_conv[0], v_a_conv[0]),
        (gs[2], "cols", b_norm, m_b_norm, v_b_norm),
        (gs[3], "flat2", b_b_pw1, m_b_b_pw1, v_b_b_pw1),
        (gs[4], "cols", b_conv[0], m_b_conv[0], v_b_conv[0]),
        (gs[5], "cols", b_b_conv, m_b_b_conv, v_b_b_conv),
        (gs[6], "cols", b_ln_g, m_b_ln_g, v_b_ln_g),
        (gs[7], "cols", b_ln_b, m_b_ln_b, v_b_ln_b),
        (gs[8], "cols", b_b_pw2, m_b_b_pw2, v_b_b_pw2),
        (gs[9], "full", ffn_norm, m_ffn_norm, v_ffn_norm),
        (gs[10], "full", final_norm.reshape(1, D), m_final_norm.reshape(1, D), v_final_norm.reshape(1, D)),
    ]
    so = small_update(gall, chip, entries, "small_update")
    sm = [so[4 * e:4 * e + 4] for e in range(len(entries))]

    def shaped(e, like):
        return [t.reshape(like.shape) for t in sm[e]]

    r_a_norm, r_a_conv, r_b_norm, r_b_b_pw1 = shaped(0, a_norm), shaped(1, a_conv), shaped(2, b_norm), shaped(3, b_b_pw1)
    r_b_conv, r_b_b_conv, r_b_ln_g, r_b_ln_b = shaped(4, b_conv), shaped(5, b_b_conv), shaped(6, b_ln_g), shaped(7, b_ln_b)
    r_b_b_pw2, r_ffn_norm, r_final = shaped(8, b_b_pw2), shaped(9, ffn_norm), shaped(10, final_norm)

    loss = lax.psum(loss_part[0, 0], ("x", "y", "c"))
    order = [r_a_norm, r_in, r_a_conv, r_out, r_b_norm, r_pw1, r_b_b_pw1, r_b_conv, r_b_b_conv, r_b_ln_g, r_b_ln_b,
             r_pw2, r_b_b_pw2, r_ffn_norm, r_gate, r_up, r_down, r_final]
    outs = [loss, grad_x.reshape(x.shape)]
    for field in range(4):
        outs += [r[field] for r in order]
    return tuple(outs)
```

```python
import functools

import jax
import jax.numpy as jnp
from jax import lax
from jax.experimental import pallas as pl
from jax.experimental.pallas import tpu as pltpu

RMS_EPS = 1e-6
LN_EPS = 1e-5
ADAM_LR = 0.001
ADAM_B1 = 0.9
ADAM_B2 = 0.999
ADAM_EPS = 1e-08
ADAM_WD = 0.01
ADAM_STEP = 10

N_CHIPS = 4
N_DEV = 8
LANES = 128
SUBLANES = 8
HALO = 32
CONV_ROWS = 64
TOKEN_TILE = 512
WIDE_TOKEN_TILE = 1024
GRAD_TOKEN_TILE = 2048
ROW_TILE = 256
VMEM_LIMIT = 56 * 1024 * 1024
MESH = pl.DeviceIdType.MESH
BF16 = jnp.bfloat16
F32 = jnp.float32


def _tile(n, pref, mult=SUBLANES):
    t = min(n, pref) // mult * mult
    while n % t:
        t -= mult
    return t


def _params(sem):
    return pltpu.CompilerParams(dimension_semantics=sem, vmem_limit_bytes=VMEM_LIMIT)


def _sigmoid(x):
    return jax.nn.sigmoid(x)


def rms_fwd(h, gain, name):
    T, D = h.shape
    tm = _tile(T, TOKEN_TILE)

    def body(h_ref, g_ref, o_ref):
        x = h_ref[...]
        r = lax.rsqrt(jnp.mean(x * x, axis=-1, keepdims=True) + RMS_EPS)
        o_ref[...] = (x * r * g_ref[...]).astype(o_ref.dtype)

    return pl.pallas_call(
        body, name=name, grid=(T // tm,),
        in_specs=[pl.BlockSpec((tm, D), lambda i: (i, 0)), pl.BlockSpec((1, D), lambda i: (0, 0))],
        out_specs=pl.BlockSpec((tm, D), lambda i: (i, 0)),
        out_shape=jax.ShapeDtypeStruct((T, D), BF16),
        compiler_params=_params(("parallel",)),
    )(h, gain)


def loss_head(h, gain, tgt, name):
    T, D = h.shape
    tm = _tile(T, TOKEN_TILE)

    def body(h_ref, g_ref, t_ref, loss_ref, dh_ref, dg_ref):
        i = pl.program_id(0)
        x = h_ref[...]
        g = g_ref[...]
        r = lax.rsqrt(jnp.mean(x * x, axis=-1, keepdims=True) + RMS_EPS)
        xhat = x * r
        diff = xhat * g - t_ref[...]
        part_loss = 0.5 * jnp.sum(jnp.mean(diff * diff, axis=-1, keepdims=True), axis=0, keepdims=True)
        dy = diff * (1.0 / D)
        dxhat = dy * g
        dh_ref[...] = r * (dxhat - xhat * jnp.mean(dxhat * xhat, axis=-1, keepdims=True))
        part = jnp.sum(dy * xhat, axis=0, keepdims=True)

        @pl.when(i == 0)
        def _():
            dg_ref[...] = part
            loss_ref[...] = part_loss

        @pl.when(i > 0)
        def _():
            dg_ref[...] += part
            loss_ref[...] += part_loss

    row = pl.BlockSpec((tm, D), lambda i: (i, 0))
    vec = pl.BlockSpec((1, D), lambda i: (0, 0))
    return pl.pallas_call(
        body, name=name, grid=(T // tm,),
        in_specs=[row, vec, row],
        out_specs=[pl.BlockSpec((1, 1), lambda i: (0, 0)), row, vec],
        out_shape=[jax.ShapeDtypeStruct((1, 1), F32), jax.ShapeDtypeStruct((T, D), F32),
                   jax.ShapeDtypeStruct((1, D), F32)],
        compiler_params=_params(("arbitrary",)),
    )(h, gain, tgt)


def _prev_halo_spec(tm, width):
    return pl.BlockSpec((HALO, width), lambda i: (jnp.maximum(i * (tm // HALO) - 1, 0), 0))


def _next_halo_spec(tm, width, T):
    return pl.BlockSpec((HALO, width), lambda i: (jnp.minimum((i + 1) * (tm // HALO), T // HALO - 1), 0))


def _shifted(win, off, rows):
    if off % SUBLANES == 0:
        return win[off:off + rows]
    n = win.shape[0]
    return pltpu.roll(win, (n - off) % n, 0)[:rows]


def _rowsum8(x):
    acc = x[0:SUBLANES]
    for q in range(1, x.shape[0] // SUBLANES):
        acc = acc + x[q * SUBLANES:(q + 1) * SUBLANES]
    return acc


def _conv_loops(tm, D, per_block):
    def chunk(r, carry):
        t0 = pl.multiple_of(r * CONV_ROWS, CONV_ROWS)
        for lb in range(D // LANES):
            per_block(t0, slice(lb * LANES, (lb + 1) * LANES))
        return carry

    lax.fori_loop(0, tm // CONV_ROWS, chunk, 0)


def gateconv_fwd(bcv, w, name):
    T, D3 = bcv.shape
    D = D3 // 3
    K = w.shape[0]
    tm = _tile(T, TOKEN_TILE)

    def body(x_ref, halo_ref, w_ref, y_ref, pad_ref):
        i = pl.program_id(0)
        pad_ref[HALO:, :] = x_ref[:, D:2 * D] * x_ref[:, 2 * D:]
        pad_ref[:HALO, :] = jnp.where(i > 0, halo_ref[:, D:2 * D] * halo_ref[:, 2 * D:], 0.0)

        def block(t0, ls):
            win = pad_ref[pl.ds(t0, CONV_ROWS + HALO), ls]
            acc = jnp.zeros((CONV_ROWS, LANES), F32)
            for k in range(K):
                acc = acc + w_ref[k:k + 1, ls] * _shifted(win, HALO - (K - 1) + k, CONV_ROWS)
            y_ref[pl.ds(t0, CONV_ROWS), ls] = (x_ref[pl.ds(t0, CONV_ROWS), ls] * acc).astype(y_ref.dtype)

        _conv_loops(tm, D, block)

    return pl.pallas_call(
        body, name=name, grid=(T // tm,),
        in_specs=[pl.BlockSpec((tm, D3), lambda i: (i, 0)), _prev_halo_spec(tm, D3),
                  pl.BlockSpec((K, D), lambda i: (0, 0))],
        out_specs=pl.BlockSpec((tm, D), lambda i: (i, 0)),
        out_shape=jax.ShapeDtypeStruct((T, D), BF16),
        scratch_shapes=[pltpu.VMEM((tm + HALO, D), F32)],
        compiler_params=_params(("parallel",)),
    )(bcv, bcv, w)


def gateconv_bwd(dy, bcv, w, name):
    T, D3 = bcv.shape
    D = D3 // 3
    K = w.shape[0]
    tm = _tile(T, TOKEN_TILE)
    nt = T // tm

    def body(dy_ref, dyn_ref, x_ref, xp_ref, xn_ref, w_ref, o_ref, dw_ref, cv_ref, dc_ref, wacc_ref):
        i = pl.program_id(0)
        cv_ref[HALO:, :] = x_ref[:, D:2 * D] * x_ref[:, 2 * D:]
        cv_ref[:HALO, :] = jnp.where(i > 0, xp_ref[:, D:2 * D] * xp_ref[:, 2 * D:], 0.0)
        dc_ref[:tm, :] = dy_ref[...] * x_ref[:, :D]
        dc_ref[tm:, :] = jnp.where(i < nt - 1, dyn_ref[...] * xn_ref[:, :D], 0.0)

        @pl.when(i == 0)
        def _():
            wacc_ref[...] = jnp.zeros_like(wacc_ref)

        def block(t0, ls):
            cwin = cv_ref[pl.ds(t0, CONV_ROWS + HALO), ls]
            dwin = dc_ref[pl.ds(t0, CONV_ROWS + HALO), ls]
            dcon = dwin[:CONV_ROWS]
            conv = jnp.zeros((CONV_ROWS, LANES), F32)
            dcv = jnp.zeros((CONV_ROWS, LANES), F32)
            for k in range(K):
                wk = w_ref[k:k + 1, ls]
                cs = _shifted(cwin, HALO - (K - 1) + k, CONV_ROWS)
                conv = conv + wk * cs
                dcv = dcv + wk * _shifted(dwin, (K - 1) - k, CONV_ROWS)
                wacc_ref[k * SUBLANES:(k + 1) * SUBLANES, ls] += _rowsum8(dcon * cs)
            rows = pl.ds(t0, CONV_ROWS)
            o_ref[rows, ls] = (dy_ref[rows, ls] * conv).astype(o_ref.dtype)
            o_ref[rows, pl.ds(D + ls.start, LANES)] = (dcv * x_ref[rows, pl.ds(2 * D + ls.start, LANES)]).astype(o_ref.dtype)
            o_ref[rows, pl.ds(2 * D + ls.start, LANES)] = (dcv * x_ref[rows, pl.ds(D + ls.start, LANES)]).astype(o_ref.dtype)

        _conv_loops(tm, D, block)

        @pl.when(i == nt - 1)
        def _():
            for k in range(K):
                dw_ref[k:k + 1, :] = jnp.sum(wacc_ref[k * SUBLANES:(k + 1) * SUBLANES, :], axis=0, keepdims=True)

    return pl.pallas_call(
        body, name=name, grid=(nt,),
        in_specs=[pl.BlockSpec((tm, D), lambda i: (i, 0)), _next_halo_spec(tm, D, T),
                  pl.BlockSpec((tm, D3), lambda i: (i, 0)), _prev_halo_spec(tm, D3), _next_halo_spec(tm, D3, T),
                  pl.BlockSpec((K, D), lambda i: (0, 0))],
        out_specs=[pl.BlockSpec((tm, D3), lambda i: (i, 0)), pl.BlockSpec((K, D), lambda i: (0, 0))],
        out_shape=[jax.ShapeDtypeStruct((T, D3), BF16), jax.ShapeDtypeStruct((K, D), F32)],
        scratch_shapes=[pltpu.VMEM((tm + HALO, D), F32), pltpu.VMEM((tm + HALO, D), F32),
                        pltpu.VMEM((K * SUBLANES, D), F32)],
        compiler_params=_params(("arbitrary",)),
    )(dy, dy, bcv, bcv, bcv, w)


def bconv_fwd(u, w, b_conv, ln_g, ln_b, name):
    T, D2 = u.shape
    D = D2 // 2
    K = w.shape[0]
    tm = _tile(T, TOKEN_TILE)

    def body(u_ref, halo_ref, w_ref, bc_ref, g_ref, b_ref, cu_ref, s_ref, pad_ref):
        i = pl.program_id(0)
        pad_ref[HALO:, :] = u_ref[:, :D] * _sigmoid(u_ref[:, D:])
        pad_ref[:HALO, :] = jnp.where(i > 0, halo_ref[:, :D] * _sigmoid(halo_ref[:, D:]), 0.0)

        def block(t0, ls):
            win = pad_ref[pl.ds(t0, CONV_ROWS + HALO), ls]
            acc = jnp.zeros((CONV_ROWS, LANES), F32)
            for k in range(K):
                acc = acc + w_ref[k:k + 1, ls] * _shifted(win, HALO - (K - 1) + k, CONV_ROWS)
            cu_ref[pl.ds(t0, CONV_ROWS), ls] = acc + bc_ref[:, ls]

        _conv_loops(tm, D, block)
        cu = cu_ref[...]
        mu = jnp.mean(cu, axis=-1, keepdims=True)
        xc = cu - mu
        rstd = lax.rsqrt(jnp.mean(xc * xc, axis=-1, keepdims=True) + LN_EPS)
        ln = xc * rstd * g_ref[...] + b_ref[...]
        s_ref[...] = (ln * _sigmoid(ln)).astype(s_ref.dtype)

    vec = pl.BlockSpec((1, D), lambda i: (0, 0))
    row = pl.BlockSpec((tm, D), lambda i: (i, 0))
    return pl.pallas_call(
        body, name=name, grid=(T // tm,),
        in_specs=[pl.BlockSpec((tm, D2), lambda i: (i, 0)), _prev_halo_spec(tm, D2),
                  pl.BlockSpec((K, D), lambda i: (0, 0)), vec, vec, vec],
        out_specs=[row, row],
        out_shape=[jax.ShapeDtypeStruct((T, D), F32), jax.ShapeDtypeStruct((T, D), BF16)],
        scratch_shapes=[pltpu.VMEM((tm + HALO, D), F32)],
        compiler_params=_params(("parallel",)),
    )(u, u, w, b_conv, ln_g, ln_b)


def ln_silu_bwd(ds, cu, ln_g, ln_b, name):
    T, D = cu.shape
    tm = _tile(T, TOKEN_TILE)

    def body(ds_ref, cu_ref, g_ref, b_ref, dcu_ref, dg_ref, db_ref, dbc_ref):
        i = pl.program_id(0)
        cu_ = cu_ref[...]
        mu = jnp.mean(cu_, axis=-1, keepdims=True)
        xc = cu_ - mu
        rstd = lax.rsqrt(jnp.mean(xc * xc, axis=-1, keepdims=True) + LN_EPS)
        xh = xc * rstd
        ln = xh * g_ref[...] + b_ref[...]
        sg = _sigmoid(ln)
        dl = ds_ref[...] * (sg * (1.0 + ln * (1.0 - sg)))
        dxh = dl * g_ref[...]
        dcu = rstd * (dxh - jnp.mean(dxh, axis=-1, keepdims=True) - xh * jnp.mean(dxh * xh, axis=-1, keepdims=True))
        dcu_ref[...] = dcu
        pg = jnp.sum(dl * xh, axis=0, keepdims=True)
        pb = jnp.sum(dl, axis=0, keepdims=True)
        pc = jnp.sum(dcu, axis=0, keepdims=True)

        @pl.when(i == 0)
        def _():
            dg_ref[...] = pg
            db_ref[...] = pb
            dbc_ref[...] = pc

        @pl.when(i > 0)
        def _():
            dg_ref[...] += pg
            db_ref[...] += pb
            dbc_ref[...] += pc

    vec = pl.BlockSpec((1, D), lambda i: (0, 0))
    row = pl.BlockSpec((tm, D), lambda i: (i, 0))
    vshape = jax.ShapeDtypeStruct((1, D), F32)
    return pl.pallas_call(
        body, name=name, grid=(T // tm,),
        in_specs=[row, row, vec, vec], out_specs=[row, vec, vec, vec],
        out_shape=[jax.ShapeDtypeStruct((T, D), F32), vshape, vshape, vshape],
        compiler_params=_params(("arbitrary",)),
    )(ds, cu, ln_g, ln_b)


def bconv_bwd(dcu, u, w, name):
    T, D2 = u.shape
    D = D2 // 2
    K = w.shape[0]
    tm = _tile(T, TOKEN_TILE)
    nt = T // tm

    def body(dc_ref, dcn_ref, u_ref, up_ref, w_ref, du_ref, dw_ref, db_ref, glu_ref, dpad_ref, dglu_ref, wacc_ref):
        i = pl.program_id(0)
        glu_ref[HALO:, :] = u_ref[:, :D] * _sigmoid(u_ref[:, D:])
        glu_ref[:HALO, :] = jnp.where(i > 0, up_ref[:, :D] * _sigmoid(up_ref[:, D:]), 0.0)
        dpad_ref[:tm, :] = dc_ref[...]
        dpad_ref[tm:, :] = jnp.where(i < nt - 1, dcn_ref[...], 0.0)

        @pl.when(i == 0)
        def _():
            wacc_ref[...] = jnp.zeros_like(wacc_ref)

        def block(t0, ls):
            gwin = glu_ref[pl.ds(t0, CONV_ROWS + HALO), ls]
            dwin = dpad_ref[pl.ds(t0, CONV_ROWS + HALO), ls]
            dcur = dwin[:CONV_ROWS]
            dglu = jnp.zeros((CONV_ROWS, LANES), F32)
            for k in range(K):
                dglu = dglu + w_ref[k:k + 1, ls] * _shifted(dwin, (K - 1) - k, CONV_ROWS)
                gs = _shifted(gwin, HALO - (K - 1) + k, CONV_ROWS)
                wacc_ref[k * SUBLANES:(k + 1) * SUBLANES, ls] += _rowsum8(dcur * gs)
            dglu_ref[pl.ds(t0, CONV_ROWS), ls] = dglu

        _conv_loops(tm, D, block)
        dglu = dglu_ref[...]
        a = u_ref[:, :D]
        sg = _sigmoid(u_ref[:, D:])
        da = dglu * sg
        dg = dglu * a * (sg * (1.0 - sg))
        du_ref[:, :D] = da.astype(du_ref.dtype)
        du_ref[:, D:] = dg.astype(du_ref.dtype)
        pa = jnp.sum(da, axis=0, keepdims=True)
        pg = jnp.sum(dg, axis=0, keepdims=True)

        @pl.when(i == 0)
        def _():
            db_ref[:, :D] = pa
            db_ref[:, D:] = pg

        @pl.when(i > 0)
        def _():
            db_ref[:, :D] += pa
            db_ref[:, D:] += pg

        @pl.when(i == nt - 1)
        def _():
            for k in range(K):
                dw_ref[k:k + 1, :] = jnp.sum(wacc_ref[k * SUBLANES:(k + 1) * SUBLANES, :], axis=0, keepdims=True)

    return pl.pallas_call(
        body, name=name, grid=(nt,),
        in_specs=[pl.BlockSpec((tm, D), lambda i: (i, 0)), _next_halo_spec(tm, D, T),
                  pl.BlockSpec((tm, D2), lambda i: (i, 0)), _prev_halo_spec(tm, D2),
                  pl.BlockSpec((K, D), lambda i: (0, 0))],
        out_specs=[pl.BlockSpec((tm, D2), lambda i: (i, 0)), pl.BlockSpec((K, D), lambda i: (0, 0)),
                   pl.BlockSpec((1, D2), lambda i: (0, 0))],
        out_shape=[jax.ShapeDtypeStruct((T, D2), BF16), jax.ShapeDtypeStruct((K, D), F32),
                   jax.ShapeDtypeStruct((1, D2), F32)],
        scratch_shapes=[pltpu.VMEM((tm + HALO, D), F32), pltpu.VMEM((tm + HALO, D), F32),
                        pltpu.VMEM((tm, D), F32), pltpu.VMEM((K * SUBLANES, D), F32)],
        compiler_params=_params(("arbitrary",)),
    )(dcu, dcu, u, u, w)


def mm_cols(a, w, lyr, bias, name):
    T, K = a.shape
    _, S, _, n = w.shape
    tm = _tile(T, WIDE_TOKEN_TILE)

    def body(*refs):
        a_ref, w_ref = refs[:2]
        o_ref = refs[-1]
        acc = jnp.dot(a_ref[...], w_ref[...], preferred_element_type=F32)
        if bias is not None:
            acc = acc + refs[2][...]
        o_ref[...] = acc

    in_specs = [pl.BlockSpec((tm, K), lambda s, i: (i, 0)), pl.BlockSpec((None, None, K, n), lambda s, i: (lyr, s, 0, 0))]
    args = [a, w]
    if bias is not None:
        in_specs.append(pl.BlockSpec((1, n), lambda s, i: (0, s)))
        args.append(bias)
    return pl.pallas_call(
        body, name=name, grid=(S, T // tm), in_specs=in_specs,
        out_specs=pl.BlockSpec((tm, n), lambda s, i: (i, s)),
        out_shape=jax.ShapeDtypeStruct((T, S * n), F32),
        compiler_params=_params(("parallel", "parallel")),
    )(*args)


def _load_once(pairs, sems):
    cps = [pltpu.make_async_copy(src, dst, sems.at[k]) for k, (src, dst) in enumerate(pairs)]
    for cp in cps:
        cp.start()
    for cp in cps:
        cp.wait()


def ffn_fwd(h, gain, wg, wu, wd, lyr, name):
    T, D = h.shape
    _, S, f, _ = wg.shape
    tm = _tile(T, TOKEN_TILE)

    def body(h_ref, gain_ref, wg_hbm, wu_hbm, wd_hbm, n_ref, g_ref, u_ref, gu_ref, o_ref, wg_v, wu_v, wd_v, sems):
        i, s = pl.program_id(0), pl.program_id(1)

        @pl.when((i == 0) & (s == 0))
        def _():
            _load_once([(wg_hbm.at[lyr], wg_v), (wu_hbm.at[lyr], wu_v), (wd_hbm.at[lyr], wd_v)], sems)

        @pl.when(s == 0)
        def _():
            x = h_ref[...]
            r = lax.rsqrt(jnp.mean(x * x, axis=-1, keepdims=True) + RMS_EPS)
            n_ref[...] = (x * r * gain_ref[...]).astype(n_ref.dtype)

        a = n_ref[...]
        g = lax.dot_general(a, wg_v[s], _NT, preferred_element_type=F32)
        u = lax.dot_general(a, wu_v[s], _NT, preferred_element_type=F32)
        gu = (g * _sigmoid(g) * u).astype(gu_ref.dtype)
        g_ref[...] = g.astype(g_ref.dtype)
        u_ref[...] = u.astype(u_ref.dtype)
        gu_ref[...] = gu
        part = jnp.dot(gu, wd_v[s], preferred_element_type=F32)

        @pl.when(s == 0)
        def _():
            o_ref[...] = h_ref[...] + part

        @pl.when(s > 0)
        def _():
            o_ref[...] += part

    row = pl.BlockSpec((tm, D), lambda i, s: (i, 0))
    seg = pl.BlockSpec((None, tm, f), lambda i, s: (s, i, 0))
    hbm = pl.BlockSpec(memory_space=pl.ANY)
    segs = jax.ShapeDtypeStruct((S, T, f), BF16)
    return pl.pallas_call(
        body, name=name, grid=(T // tm, S),
        in_specs=[row, pl.BlockSpec((1, D), lambda i, s: (0, 0)), hbm, hbm, hbm],
        out_specs=[row, seg, seg, seg, row],
        out_shape=[jax.ShapeDtypeStruct((T, D), BF16), segs, segs, segs, jax.ShapeDtypeStruct((T, D), F32)],
        scratch_shapes=[pltpu.VMEM((S, f, D), BF16), pltpu.VMEM((S, f, D), BF16), pltpu.VMEM((S, f, D), BF16),
                        pltpu.SemaphoreType.DMA((3,))],
        compiler_params=_params(("arbitrary", "arbitrary")),
    )(h, gain, wg, wu, wd)


def ffn_bwd(dy, h, gain, g, u, wd, wg, wu, lyr, name):
    T, D = h.shape
    _, S, f, _ = wg.shape
    tm = _tile(T, TOKEN_TILE)
    nt = T // tm

    def body(dy_ref, h_ref, gain_ref, g_ref, u_ref, wd_hbm, wg_hbm, wu_hbm, dg_ref, du_ref, dh_ref, dgain_ref,
             wd_v, wg_v, wu_v, dyb_ref, sems):
        i, s = pl.program_id(0), pl.program_id(1)

        @pl.when((i == 0) & (s == 0))
        def _():
            _load_once([(wd_hbm.at[lyr], wd_v), (wg_hbm.at[lyr], wg_v), (wu_hbm.at[lyr], wu_v)], sems)

        @pl.when(s == 0)
        def _():
            dyb_ref[...] = dy_ref[...].astype(dyb_ref.dtype)

        dgu = lax.dot_general(dyb_ref[...], wd_v[s], _NT, preferred_element_type=F32)
        gv = g_ref[...].astype(F32)
        sg = _sigmoid(gv)
        dg = (dgu * u_ref[...].astype(F32) * (sg * (1.0 + gv * (1.0 - sg)))).astype(dg_ref.dtype)
        du = (dgu * (gv * sg)).astype(du_ref.dtype)
        dg_ref[...] = dg
        du_ref[...] = du
        part = (jnp.dot(dg, wg_v[s], preferred_element_type=F32)
                + jnp.dot(du, wu_v[s], preferred_element_type=F32))

        @pl.when(s == 0)
        def _():
            dh_ref[...] = part

        @pl.when(s > 0)
        def _():
            dh_ref[...] += part

        @pl.when(s == S - 1)
        def _():
            dn = dh_ref[...]
            x = h_ref[...]
            r = lax.rsqrt(jnp.mean(x * x, axis=-1, keepdims=True) + RMS_EPS)
            xhat = x * r
            dxhat = dn * gain_ref[...]
            dh_ref[...] = dy_ref[...] + r * (dxhat - xhat * jnp.mean(dxhat * xhat, axis=-1, keepdims=True))
            pg = jnp.sum(dn * xhat, axis=0, keepdims=True)

            @pl.when(i == 0)
            def _():
                dgain_ref[...] = pg

            @pl.when(i > 0)
            def _():
                dgain_ref[...] += pg

    row = pl.BlockSpec((tm, D), lambda i, s: (i, 0))
    vec = pl.BlockSpec((1, D), lambda i, s: (0, 0))
    seg = pl.BlockSpec((None, tm, f), lambda i, s: (s, i, 0))
    hbm = pl.BlockSpec(memory_space=pl.ANY)
    segs = jax.ShapeDtypeStruct((S, T, f), BF16)
    return pl.pallas_call(
        body, name=name, grid=(nt, S),
        in_specs=[row, row, vec, seg, seg, hbm, hbm, hbm],
        out_specs=[seg, seg, row, vec],
        out_shape=[segs, segs, jax.ShapeDtypeStruct((T, D), F32), jax.ShapeDtypeStruct((1, D), F32)],
        scratch_shapes=[pltpu.VMEM((S, f, D), BF16), pltpu.VMEM((S, f, D), BF16), pltpu.VMEM((S, f, D), BF16),
                        pltpu.VMEM((tm, D), BF16), pltpu.SemaphoreType.DMA((3,))],
        compiler_params=_params(("arbitrary", "arbitrary")),
    )(dy, h, gain, g, u, wd, wg, wu)


def mm_rows(a, w, lyr, res, bias, name):
    S, T, k = a.shape
    N = w.shape[-1]
    tm = _tile(T, TOKEN_TILE)

    def body(*refs):
        a_ref, w_ref, r_ref = refs[:3]
        o_ref = refs[-1]
        s = pl.program_id(1)
        acc = jnp.dot(a_ref[...], w_ref[...], preferred_element_type=F32)

        @pl.when(s == 0)
        def _():
            base = r_ref[...]
            if bias is not None:
                base = base + refs[3][...]
            o_ref[...] = base + acc

        @pl.when(s > 0)
        def _():
            o_ref[...] += acc

    in_specs = [pl.BlockSpec((None, tm, k), lambda i, s: (s, i, 0)),
                pl.BlockSpec((None, None, k, N), lambda i, s: (lyr, s, 0, 0)),
                pl.BlockSpec((tm, N), lambda i, s: (i, 0))]
    args = [a, w, res]
    if bias is not None:
        in_specs.append(pl.BlockSpec((1, N), lambda i, s: (0, 0)))
        args.append(bias)
    return pl.pallas_call(
        body, name=name, grid=(T // tm, S), in_specs=in_specs,
        out_specs=pl.BlockSpec((tm, N), lambda i, s: (i, 0)),
        out_shape=jax.ShapeDtypeStruct((T, N), F32),
        compiler_params=_params(("parallel", "arbitrary")),
    )(*args)


_NT = (((1,), (1,)), ((), ()))
_TN = (((0,), (0,)), ((), ()))


def nt_rows(dy, w, lyr, want_colsum, name):
    T, N = dy.shape
    _, S, k, _ = w.shape
    tm = _tile(T, TOKEN_TILE)

    def body(dy_ref, w_ref, o_ref, *rest):
        i, s = pl.program_id(0), pl.program_id(1)
        d = dy_ref[...]
        o_ref[...] = lax.dot_general(d.astype(BF16), w_ref[...], _NT, preferred_element_type=F32)
        if want_colsum:
            cs_ref = rest[0]
            part = jnp.sum(d, axis=0, keepdims=True)

            @pl.when((i == 0) & (s == 0))
            def _():
                cs_ref[...] = part

            @pl.when((i > 0) & (s == 0))
            def _():
                cs_ref[...] += part

    out_specs = [pl.BlockSpec((None, tm, k), lambda i, s: (s, i, 0))]
    out_shape = [jax.ShapeDtypeStruct((S, T, k), F32)]
    if want_colsum:
        out_specs.append(pl.BlockSpec((1, N), lambda i, s: (0, 0)))
        out_shape.append(jax.ShapeDtypeStruct((1, N), F32))
    return pl.pallas_call(
        body, name=name, grid=(T // tm, S),
        in_specs=[pl.BlockSpec((tm, N), lambda i, s: (i, 0)), pl.BlockSpec((None, None, k, N), lambda i, s: (lyr, s, 0, 0))],
        out_specs=out_specs, out_shape=out_shape,
        compiler_params=_params(("arbitrary", "arbitrary")),
    )(dy, w)


def nt_cols_rms(dy, w, lyr, h, gain, dres, name):
    T, K = h.shape
    _, S, _, n = w.shape
    tm = _tile(T, TOKEN_TILE)

    def body(dy_ref, w_ref, h_ref, gain_ref, dres_ref, dh_ref, dgain_ref):
        i = pl.program_id(0)
        dn = None
        for s in range(S):
            part = lax.dot_general(dy_ref[:, s * n:(s + 1) * n], w_ref[s], _NT, preferred_element_type=F32)
            dn = part if dn is None else dn + part
        x = h_ref[...]
        r = lax.rsqrt(jnp.mean(x * x, axis=-1, keepdims=True) + RMS_EPS)
        xhat = x * r
        dxhat = dn * gain_ref[...]
        dh_ref[...] = dres_ref[...] + r * (dxhat - xhat * jnp.mean(dxhat * xhat, axis=-1, keepdims=True))
        pg = jnp.sum(dn * xhat, axis=0, keepdims=True)

        @pl.when(i == 0)
        def _():
            dgain_ref[...] = pg

        @pl.when(i > 0)
        def _():
            dgain_ref[...] += pg

    row = pl.BlockSpec((tm, K), lambda i: (i, 0))
    vec = pl.BlockSpec((1, K), lambda i: (0, 0))
    return pl.pallas_call(
        body, name=name, grid=(T // tm,),
        in_specs=[pl.BlockSpec((tm, S * n), lambda i: (i, 0)), pl.BlockSpec((None, S, K, n), lambda i: (lyr, 0, 0, 0)),
                  row, vec, row],
        out_specs=[row, vec],
        out_shape=[jax.ShapeDtypeStruct((T, K), F32), jax.ShapeDtypeStruct((1, K), F32)],
        compiler_params=_params(("arbitrary",)),
    )(dy, w, h, gain, dres)


def tn_grad(a, dy, S, a_by_seg, name):
    T = dy.shape[0] if dy.ndim == 2 else dy.shape[1]
    tt = _tile(T, GRAD_TOKEN_TILE)
    if a_by_seg:
        R = a.shape[1] // S if a.ndim == 2 else a.shape[2]
        C = dy.shape[1]
        a_spec = pl.BlockSpec((tt, R), lambda s, t: (t, s)) if a.ndim == 2 else pl.BlockSpec((None, tt, R), lambda s, t: (s, t, 0))
        b_spec = pl.BlockSpec((tt, C), lambda s, t: (t, 0))
    else:
        R = a.shape[1]
        C = dy.shape[1] // S if dy.ndim == 2 else dy.shape[2]
        a_spec = pl.BlockSpec((tt, R), lambda s, t: (t, 0))
        b_spec = pl.BlockSpec((tt, C), lambda s, t: (t, s)) if dy.ndim == 2 else pl.BlockSpec((None, tt, C), lambda s, t: (s, t, 0))
    Rh = R // 2
    nt = T // tt

    def body(a_ref, b_ref, o_ref, acc_ref):
        t = pl.program_id(1)
        part = lax.dot_general(a_ref[...], b_ref[...].astype(BF16), _TN, preferred_element_type=F32)

        @pl.when(t == 0)
        def _():
            acc_ref[...] = part

        @pl.when(t > 0)
        def _():
            acc_ref[...] += part

        @pl.when(t == nt - 1)
        def _():
            o_ref[0] = acc_ref[:Rh, :].astype(o_ref.dtype)
            o_ref[1] = acc_ref[Rh:, :].astype(o_ref.dtype)

    return pl.pallas_call(
        body, name=name, grid=(S, nt), in_specs=[a_spec, b_spec],
        out_specs=pl.BlockSpec((2, None, Rh, C), lambda s, t: (0, s, 0, 0)),
        out_shape=jax.ShapeDtypeStruct((2, S, Rh, C), BF16),
        scratch_shapes=[pltpu.VMEM((R, C), F32)],
        compiler_params=_params(("parallel", "arbitrary")),
    )(a, dy)


def _place():
    x, y, c = lax.axis_index("x"), lax.axis_index("y"), lax.axis_index("c")
    chips = [(1 - x, y), (x, 1 - y), (1 - x, 1 - y)]
    return x, y, c, chips


def _any_specs(n):
    return [pl.BlockSpec(memory_space=pl.ANY)] * n


def _remote(src, dst, send_sem, recv_sem, dev):
    return pltpu.make_async_remote_copy(src_ref=src, dst_ref=dst, send_sem=send_sem, recv_sem=recv_sem,
                                        device_id=dev, device_id_type=MESH)


def small_allgather(v, name):
    rows, W = v.shape

    def body(v_ref, o_ref, send_sems, recv_sems, loc_sem):
        x, y, c, _ = _place()
        mine = pltpu.make_async_copy(v_ref, o_ref.at[4 * x + 2 * y + c], loc_sem)
        mine.start()

        def peer_of(m):
            return ((1 - x) if m & 4 else x, (1 - y) if m & 2 else y, (1 - c) if m & 1 else c)

        sends = []
        for m in range(1, N_DEV):
            cp = _remote(v_ref, o_ref.at[4 * x + 2 * y + c], send_sems.at[m - 1], recv_sems.at[m - 1], peer_of(m))
            cp.start()
            sends.append(cp)
        for m in range(1, N_DEV):
            px, py, pc = peer_of(m)
            blk = o_ref.at[4 * px + 2 * py + pc]
            _remote(blk, blk, send_sems.at[m - 1], recv_sems.at[m - 1], (px, py, pc)).wait_recv()
        for cp in sends:
            cp.wait_send()
        mine.wait()

    return pl.pallas_call(
        body, name=name,
        in_specs=[pl.BlockSpec(memory_space=pltpu.VMEM)], out_specs=pl.BlockSpec(memory_space=pltpu.VMEM),
        out_shape=jax.ShapeDtypeStruct((N_DEV, rows, W), F32),
        scratch_shapes=[pltpu.SemaphoreType.DMA((N_DEV - 1,)), pltpu.SemaphoreType.DMA((N_DEV - 1,)), pltpu.SemaphoreType.DMA],
    )(v)


def small_allreduce(v, name):
    rows, W = v.shape

    def body(v_ref, o_ref, sib_ref, pair_ref, chips_ref, send_sems, recv_sems):
        x, y, c, chips = _place()
        me = 2 * x + y
        swap = _remote(v_ref, sib_ref, send_sems.at[3], recv_sems.at[3], (x, y, 1 - c))
        swap.start()
        swap.wait()
        mine, other = v_ref[...], sib_ref[...]
        pair_ref[...] = jnp.where(c == 0, mine, other) + jnp.where(c == 0, other, mine)
        sends = []
        for j, (px, py) in enumerate(chips):
            cp = _remote(pair_ref, chips_ref.at[me], send_sems.at[j], recv_sems.at[j], (px, py, c))
            cp.start()
            sends.append(cp)
        chips_ref[me] = pair_ref[...]
        for j, (px, py) in enumerate(chips):
            blk = chips_ref.at[2 * px + py]
            _remote(blk, blk, send_sems.at[j], recv_sems.at[j], (px, py, c)).wait_recv()
        for cp in sends:
            cp.wait_send()
        o_ref[...] = (chips_ref[0] + chips_ref[1]) + (chips_ref[2] + chips_ref[3])

    return pl.pallas_call(
        body, name=name,
        in_specs=[pl.BlockSpec(memory_space=pltpu.VMEM)], out_specs=pl.BlockSpec(memory_space=pltpu.VMEM),
        out_shape=jax.ShapeDtypeStruct((rows, W), F32),
        scratch_shapes=[pltpu.VMEM((rows, W), F32), pltpu.VMEM((rows, W), F32), pltpu.VMEM((N_CHIPS, rows, W), F32),
                        pltpu.SemaphoreType.DMA((4,)), pltpu.SemaphoreType.DMA((4,))],
    )(v)


def gather_weights(shards, name):
    units = [(w, l) for w in range(len(shards)) for l in range(shards[w].shape[0])]
    nu = len(units)
    n = len(shards)

    def body(*refs):
        ins, outs = refs[:n], refs[n:2 * n]
        send_sems, recv_sems = refs[2 * n:]
        x, y, c, chips = _place()
        me = 2 * x + y
        sib = (x, y, 1 - c)

        def half(w, l, chip, hc):
            rh = ins[w].shape[1] // 2
            return outs[w].at[l, chip, pl.ds(hc * rh, rh)]

        sends = []
        for k, (w, l) in enumerate(units):
            rh = ins[w].shape[1] // 2
            cp = _remote(ins[w].at[l], outs[w].at[l, me], send_sems.at[k, 6], recv_sems.at[k, 6], sib)
            cp.start()
            sends.append(cp)
            for j, (px, py) in enumerate(chips):
                cp = _remote(ins[w].at[l, pl.ds(c * rh, rh)], half(w, l, me, c), send_sems.at[k, j], recv_sems.at[k, j], (px, py, c))
                cp.start()
                sends.append(cp)
        for k, (w, l) in enumerate(units):
            for j, (px, py) in enumerate(chips):
                blk = half(w, l, 2 * px + py, c)
                _remote(blk, blk, send_sems.at[k, j], recv_sems.at[k, j], (px, py, c)).wait_recv()
                cp = _remote(blk, blk, send_sems.at[k, 3 + j], recv_sems.at[k, 3 + j], sib)
                cp.start()
                sends.append(cp)
        for k, (w, l) in enumerate(units):
            own = outs[w].at[l, me]
            _remote(own, own, send_sems.at[k, 6], recv_sems.at[k, 6], sib).wait_recv()
            for j, (px, py) in enumerate(chips):
                blk = half(w, l, 2 * px + py, 1 - c)
                _remote(blk, blk, send_sems.at[k, 3 + j], recv_sems.at[k, 3 + j], sib).wait_recv()
        for cp in sends:
            cp.wait_send()

    return pl.pallas_call(
        body, name=name, in_specs=_any_specs(n), out_specs=_any_specs(n),
        out_shape=[jax.ShapeDtypeStruct((s.shape[0], N_CHIPS) + s.shape[1:], s.dtype) for s in shards],
        scratch_shapes=[pltpu.SemaphoreType.DMA((nu, 7)), pltpu.SemaphoreType.DMA((nu, 7))],
    )(*shards)


def sibling_halves(grads, name):
    n = len(grads)

    def body(*refs):
        ins, outs = refs[:n], refs[n:2 * n]
        send_sems, recv_sems = refs[2 * n:]
        x, y, c, _ = _place()
        cps = []
        for k in range(n):
            cp = _remote(ins[k].at[1 - c], outs[k], send_sems.at[k], recv_sems.at[k], (x, y, 1 - c))
            cp.start()
            cps.append(cp)
        for cp in cps:
            cp.wait_recv()
        for cp in cps:
            cp.wait_send()

    return pl.pallas_call(
        body, name=name, in_specs=_any_specs(n), out_specs=_any_specs(n),
        out_shape=[jax.ShapeDtypeStruct(g.shape[1:], g.dtype) for g in grads],
        scratch_shapes=[pltpu.SemaphoreType.DMA((n,)), pltpu.SemaphoreType.DMA((n,))],
    )(*grads)


def pair_sum(gh, recv, cidx, name):
    _, S, Rh, C = gh.shape

    def body(c_ref, a_ref, b_ref, o_ref):
        o_ref[...] = (a_ref[...].astype(F32) + b_ref[...].astype(F32)).astype(o_ref.dtype)

    return pl.pallas_call(
        body, name=name, out_shape=jax.ShapeDtypeStruct((S, Rh, C), BF16),
        grid_spec=pltpu.PrefetchScalarGridSpec(
            num_scalar_prefetch=1, grid=(S,),
            in_specs=[pl.BlockSpec((None, None, Rh, C), lambda s, c_ref: (c_ref[0], s, 0, 0)),
                      pl.BlockSpec((None, Rh, C), lambda s, c_ref: (s, 0, 0))],
            out_specs=pl.BlockSpec((None, Rh, C), lambda s, c_ref: (s, 0, 0))),
        compiler_params=_params(("parallel",)),
    )(cidx, gh, recv)


def scatter_to_owners(parts, name):
    n = len(parts)

    def body(*refs):
        ins, outs = refs[:n], refs[n:2 * n]
        send_sems, recv_sems = refs[2 * n:]
        x, y, c, chips = _place()
        me = 2 * x + y
        sib = (x, y, 1 - c)
        sends = []
        for k in range(n):
            cp = _remote(ins[k].at[me], outs[k].at[me, c], send_sems.at[k, 6], recv_sems.at[k, 6], sib)
            cp.start()
            sends.append(cp)
            for j, (px, py) in enumerate(chips):
                cp = _remote(ins[k].at[2 * px + py], outs[k].at[me, c], send_sems.at[k, j], recv_sems.at[k, j], (px, py, c))
                cp.start()
                sends.append(cp)
        for k in range(n):
            for j, (px, py) in enumerate(chips):
                blk = outs[k].at[2 * px + py, c]
                _remote(blk, blk, send_sems.at[k, j], recv_sems.at[k, j], (px, py, c)).wait_recv()
                cp = _remote(blk, blk, send_sems.at[k, 3 + j], recv_sems.at[k, 3 + j], sib)
                cp.start()
                sends.append(cp)
        for k in range(n):
            own = outs[k].at[me, 1 - c]
            _remote(own, own, send_sems.at[k, 6], recv_sems.at[k, 6], sib).wait_recv()
            for j, (px, py) in enumerate(chips):
                blk = outs[k].at[2 * px + py, 1 - c]
                _remote(blk, blk, send_sems.at[k, 3 + j], recv_sems.at[k, 3 + j], sib).wait_recv()
        for cp in sends:
            cp.wait_send()

    return pl.pallas_call(
        body, name=name, in_specs=_any_specs(n), out_specs=_any_specs(n),
        out_shape=[jax.ShapeDtypeStruct((p.shape[0], 2) + p.shape[1:], p.dtype) for p in parts],
        scratch_shapes=[pltpu.SemaphoreType.DMA((n, 7)), pltpu.SemaphoreType.DMA((n, 7))],
    )(*parts)


def _adamw_math(w, g, m, v):
    m = ADAM_B1 * m + (1.0 - ADAM_B1) * g
    v = ADAM_B2 * v + (1.0 - ADAM_B2) * (g * g)
    m_hat = m / (1.0 - ADAM_B1 ** ADAM_STEP)
    v_hat = v / (1.0 - ADAM_B2 ** ADAM_STEP)
    delta = -ADAM_LR * (m_hat / (jnp.sqrt(v_hat) + ADAM_EPS) + ADAM_WD * w)
    return delta, m, v


def adamw_reduce(w, m, v, buf, part, place, lyr, bases, name):
    L, R, C = w.shape
    Rh = R // 2
    rb = _tile(Rh, ROW_TILE, 2 * SUBLANES)
    nb = Rh // rb

    def body(place_ref, p_ref, b0, b1, b2, b3, w_ref, m_ref, v_ref, *rest):
        go_ref, d_ref, mo_ref, vo_ref = rest[-4:]
        mine = (place_ref[1] == pl.program_id(0))
        g = None
        for p, b in enumerate((b0, b1, b2, b3)):
            val = jnp.where(mine & (place_ref[0] == p), p_ref[...], b[...]).astype(F32)
            g = val if g is None else g + val
        d, mn, vn = _adamw_math(w_ref[...], g, m_ref[...], v_ref[...])
        go_ref[...] = g
        d_ref[...] = d
        mo_ref[...] = mn
        vo_ref[...] = vn

    def buf_spec(p):
        def idx(h, i, pr):
            own = (pr[0] == p) & (pr[1] == h)
            return (p, jnp.where(own, 1 - h, h), i, 0)
        return pl.BlockSpec((None, None, rb, C), idx)

    blk = pl.BlockSpec((None, rb, C), lambda h, i, pr: (lyr, h * nb + i, 0))
    in_specs = [pl.BlockSpec((None, rb, C), lambda h, i, pr: (pr[0], i, 0))] + [buf_spec(p) for p in range(N_CHIPS)] + [blk] * 3
    args = [place, part, buf, buf, buf, buf, w, m, v]
    aliases = {}
    if bases is not None:
        in_specs += [pl.BlockSpec(memory_space=pl.ANY)] * 4
        aliases = {len(args) + k: k for k in range(4)}
        args += list(bases)
    shp = jax.ShapeDtypeStruct((L, R, C), F32)
    return pl.pallas_call(
        body, name=name, out_shape=[shp] * 4,
        grid_spec=pltpu.PrefetchScalarGridSpec(num_scalar_prefetch=1, grid=(2, nb), in_specs=in_specs, out_specs=[blk] * 4),
        input_output_aliases=aliases,
        compiler_params=_params(("parallel", "parallel")),
    )(*args)


def small_update(gall, chip, entries, name):
    ne = len(entries)
    D = gall.shape[1]

    def body(chip_ref, gall_ref, *refs):
        ins, outs = refs[:3 * ne], refs[3 * ne:]
        ch = chip_ref[0]
        for e, (row0, kind, w, _, _) in enumerate(entries):
            r, width = w.shape

            def gsum(rs, cs):
                return gall_ref[rs, cs]

            if kind == "full":
                g = gsum(slice(row0, row0 + r), slice(0, D))
            elif kind == "cols":
                g = gsum(slice(row0, row0 + r), slice(0, width))
                for q in range(1, N_CHIPS):
                    g = jnp.where(ch == q, gsum(slice(row0, row0 + r), slice(q * width, (q + 1) * width)), g)
            else:
                per_row = D // width
                g = gsum(slice(row0, row0 + 1), slice(0, width))
                for q in range(1, N_CHIPS):
                    rr = row0 + q // per_row
                    cc = (q % per_row) * width
                    g = jnp.where(ch == q, gsum(slice(rr, rr + 1), slice(cc, cc + width)), g)
            d, mn, vn = _adamw_math(ins[3 * e][...], g, ins[3 * e + 1][...], ins[3 * e + 2][...])
            outs[4 * e][...] = g
            outs[4 * e + 1][...] = d
            outs[4 * e + 2][...] = mn
            outs[4 * e + 3][...] = vn

    vm = pl.BlockSpec(memory_space=pltpu.VMEM)
    args, out_shape = [], []
    for _, _, w, m, v in entries:
        args += [w, m, v]
        out_shape += [jax.ShapeDtypeStruct(w.shape, F32)] * 4
    return pl.pallas_call(
        body, name=name,
        in_specs=[pl.BlockSpec(memory_space=pltpu.SMEM), vm] + [vm] * (3 * ne),
        out_specs=[vm] * (4 * ne), out_shape=out_shape,
        compiler_params=pltpu.CompilerParams(vmem_limit_bytes=VMEM_LIMIT),
    )(chip, gall, *args)


def _pack_rows(items, width):
    rows, starts, at = [], [], 0
    for it in items:
        r = it.shape[0]
        pad = (-r) % SUBLANES
        starts.append(at)
        rows.append(it)
        if pad:
            rows.append(jnp.zeros((pad, width), F32))
        at += r + pad
    return jnp.concatenate(rows, axis=0), starts


def kernel(x, a_norm, a_w_in, a_conv, a_w_out, b_norm, b_w_pw1, b_b_pw1, b_conv, b_b_conv, b_ln_g, b_ln_b, b_w_pw2, b_b_pw2, ffn_norm, ffn_w_gate, ffn_w_up, ffn_w_down, final_norm, loss_target, m_a_norm, m_a_w_in, m_a_conv, m_a_w_out, m_b_norm, m_b_w_pw1, m_b_b_pw1, m_b_conv, m_b_b_conv, m_b_ln_g, m_b_ln_b, m_b_w_pw2, m_b_b_pw2, m_ffn_norm, m_ffn_w_gate, m_ffn_w_up, m_ffn_w_down, m_final_norm, v_a_norm, v_a_w_in, v_a_conv, v_a_w_out, v_b_norm, v_b_w_pw1, v_b_b_pw1, v_b_conv, v_b_b_conv, v_b_ln_g, v_b_ln_b, v_b_w_pw2, v_b_b_pw2, v_ffn_norm, v_ffn_w_gate, v_ffn_w_up, v_ffn_w_down, v_final_norm):
    T, D = x.shape[1], x.shape[2]
    Dq = D // N_CHIPS
    cx, cy, cc = lax.axis_index("x"), lax.axis_index("y"), lax.axis_index("c")
    chip = (2 * cx + cy).astype(jnp.int32).reshape(1)
    cidx = cc.astype(jnp.int32).reshape(1)
    h0 = x.reshape(T, D)
    tgt = loss_target.reshape(T, D)

    small_shards = [a_conv[0], b_norm, b_b_pw1.reshape(2, Dq), b_conv[0], b_b_conv, b_ln_g, b_ln_b, b_b_pw2]
    packed, st = _pack_rows(small_shards, Dq)
    sw = small_allgather(packed, "gather_small")[0::2]

    def whole(k, r):
        return jnp.transpose(sw[:, st[k]:st[k] + r, :], (1, 0, 2)).reshape(r, D)

    a_conv_f, b_norm_f = whole(0, 3), whole(1, 1)
    b_b_pw1_f = sw[:, st[2]:st[2] + 2, :].reshape(1, 2 * D)
    b_conv_f, b_b_conv_f, b_ln_g_f, b_ln_b_f, b_b_pw2_f = whole(3, b_conv.shape[1]), whole(4, 1), whole(5, 1), whole(6, 1), whole(7, 1)

    tr = lambda t: jnp.swapaxes(t, 1, 2)
    w_gate, m_gate, v_gate = tr(ffn_w_gate), tr(m_ffn_w_gate), tr(v_ffn_w_gate)
    w_up, m_up, v_up = tr(ffn_w_up), tr(m_ffn_w_up), tr(v_ffn_w_up)
    big = [a_w_in, a_w_out, b_w_pw1, b_w_pw2, w_gate, w_up, ffn_w_down]
    g_in, g_out, g_pw1, g_pw2, g_gate, g_up, g_down = gather_weights([w.astype(BF16) for w in big], "gather_weights")
    g_out = g_out.reshape(1, 1, D, D)
    g_pw2 = g_pw2.reshape(1, 1, D, D)

    n0 = rms_fwd(h0, a_norm, "rms_a")
    bcv = mm_cols(n0, g_in, 0, None, "mm_w_in")
    ya = gateconv_fwd(bcv, a_conv_f, "gateconv_fwd")
    h1 = mm_rows(ya[None], g_out, 0, h0, None, "mm_w_out")
    n1, fg0, fu0, gu0, h2 = ffn_fwd(h1, ffn_norm[0:1], g_gate, g_up, g_down, 0, "ffn_fwd0")
    n2 = rms_fwd(h2, b_norm_f, "rms_b")
    ub = mm_cols(n2, g_pw1, 0, b_b_pw1_f, "mm_pw1")
    cu, sb = bconv_fwd(ub, b_conv_f, b_b_conv_f, b_ln_g_f, b_ln_b_f, "bconv_fwd")
    h3 = mm_rows(sb[None], g_pw2, 0, h2, b_b_pw2_f, "mm_pw2")
    n3, fg1, fu1, gu1, h4 = ffn_fwd(h3, ffn_norm[1:2], g_gate, g_up, g_down, 1, "ffn_fwd1")
    loss_part, dh4, d_final = loss_head(h4, final_norm.reshape(1, D), tgt, "loss_head")

    def ffn_back(dh, h_in, n_in, fg, fu, gu, lyr, tag):
        dg, du, dh_in, dnorm = ffn_bwd(dh, h_in, ffn_norm[lyr:lyr + 1], fg, fu, g_down, g_gate, g_up, lyr, "ffn_bwd" + tag)
        gd = tn_grad(gu, dh, N_CHIPS, True, "tn_down" + tag)
        gg = tn_grad(dg, n_in, N_CHIPS, True, "tn_gate" + tag)
        gu_ = tn_grad(du, n_in, N_CHIPS, True, "tn_up" + tag)
        return dh_in, dnorm, gg, gu_, gd

    dh3, d_fn1, gh_gate1, gh_up1, gh_down1 = ffn_back(dh4, h3, n3, fg1, fu1, gu1, 1, "1")

    ds, d_b_pw2 = nt_rows(dh3, g_pw2, 0, True, "nt_pw2")
    gh_pw2 = tn_grad(sb, dh3, N_CHIPS, True, "tn_pw2")
    dcu, d_ln_g, d_ln_b, d_b_conv = ln_silu_bwd(ds[0], cu, b_ln_g_f, b_ln_b_f, "ln_silu_bwd")
    dub, d_bconv_w, d_b_pw1 = bconv_bwd(dcu, ub, b_conv_f, "bconv_bwd")
    gh_pw1 = tn_grad(n2, dub, N_CHIPS, False, "tn_pw1")
    dh2, d_b_norm = nt_cols_rms(dub, g_pw1, 0, h2, b_norm_f, dh3, "nt_pw1")

    dh1, d_fn0, gh_gate0, gh_up0, gh_down0 = ffn_back(dh2, h1, n1, fg0, fu0, gu0, 0, "0")

    dya = nt_rows(dh1, g_out, 0, False, "nt_w_out")[0]
    gh_out = tn_grad(ya, dh1, N_CHIPS, True, "tn_w_out")
    dbcv, d_aconv_w = gateconv_bwd(dya[0], bcv, a_conv_f, "gateconv_bwd")
    gh_in = tn_grad(n0, dbcv, N_CHIPS, False, "tn_w_in")
    grad_x, d_a_norm = nt_cols_rms(dbcv, g_in, 0, h0, a_norm, dh1, "nt_w_in")

    ghs = [gh_in, gh_out, gh_pw1, gh_pw2, gh_gate0, gh_gate1, gh_up0, gh_up1, gh_down0, gh_down1]
    tags = ["in", "out", "pw1", "pw2", "gate0", "gate1", "up0", "up1", "down0", "down1"]
    from_sib = sibling_halves(ghs, "reduce_sibling")
    parts = [pair_sum(g, r, cidx, "pair_sum_" + t) for g, r, t in zip(ghs, from_sib, tags)]
    at_owner = scatter_to_owners(parts, "reduce_scatter")
    place = jnp.concatenate([chip, cidx])

    def upd(w, m, v, ks, tag):
        res = None
        for lyr, k in enumerate(ks):
            res = adamw_reduce(w, m, v, at_owner[k], parts[k], place, lyr, res, "adamw_%s%d" % (tag, lyr))
        return res

    r_in = upd(a_w_in, m_a_w_in, v_a_w_in, [0], "w_in")
    r_out = upd(a_w_out, m_a_w_out, v_a_w_out, [1], "w_out")
    r_pw1 = upd(b_w_pw1, m_b_w_pw1, v_b_w_pw1, [2], "pw1")
    r_pw2 = upd(b_w_pw2, m_b_w_pw2, v_b_w_pw2, [3], "pw2")
    r_gate = [tr(t) for t in upd(w_gate, m_gate, v_gate, [4, 5], "gate")]
    r_up = [tr(t) for t in upd(w_up, m_up, v_up, [6, 7], "up")]
    r_down = upd(ffn_w_down, m_ffn_w_down, v_ffn_w_down, [8, 9], "down")

    d_ffn_norm = jnp.concatenate([d_fn0, d_fn1], axis=0)
    small_grads = [d_a_norm, d_aconv_w, d_b_norm, d_b_pw1.reshape(2, D), d_bconv_w, d_b_conv, d_ln_g, d_ln_b, d_b_pw2,
                   d_ffn_norm, d_final, jnp.broadcast_to(loss_part, (1, D))]
    gpacked, gs = _pack_rows(small_grads, D)
    gall = small_allreduce(gpacked, "allreduce_small_grads")
    entries = [
        (gs[0], "full", a_norm, m_a_norm, v_a_norm),
        (gs[1], "cols", a_conv[0], m_a_conv[0], v_a_conv[0]),
        (gs[2], "cols", b_norm, m_b_norm, v_b_norm),
        (gs[3], "flat2", b_b_pw1, m_b_b_pw1, v_b_b_pw1),
        (gs[4], "cols", b_conv[0], m_b_conv[0], v_b_conv[0]),
        (gs[5], "cols", b_b_conv, m_b_b_conv, v_b_b_conv),
        (gs[6], "cols", b_ln_g, m_b_ln_g, v_b_ln_g),
        (gs[7], "cols", b_ln_b, m_b_ln_b, v_b_ln_b),
        (gs[8], "cols", b_b_pw2, m_b_b_pw2, v_b_b_pw2),
        (gs[9], "full", ffn_norm, m_ffn_norm, v_ffn_norm),
        (gs[10], "full", final_norm.reshape(1, D), m_final_norm.reshape(1, D), v_final_norm.reshape(1, D)),
    ]
    so = small_update(gall, chip, entries, "small_update")
    sm = [so[4 * e:4 * e + 4] for e in range(len(entries))]

    def shaped(e, like):
        return [t.reshape(like.shape) for t in sm[e]]

    r_a_norm, r_a_conv, r_b_norm, r_b_b_pw1 = shaped(0, a_norm), shaped(1, a_conv), shaped(2, b_norm), shaped(3, b_b_pw1)
    r_b_conv, r_b_b_conv, r_b_ln_g, r_b_ln_b = shaped(4, b_conv), shaped(5, b_b_conv), shaped(6, b_ln_g), shaped(7, b_ln_b)
    r_b_b_pw2, r_ffn_norm, r_final = shaped(8, b_b_pw2), shaped(9, ffn_norm), shaped(10, final_norm)

    loss = gall[gs[11], 0]
    order =[r_a_norm, r_in, r_a_conv, r_out, r_b_norm, r_pw1, r_b_b_pw1, r_b_conv, r_b_b_conv, r_b_ln_g, r_b_ln_b,
             r_pw2, r_b_b_pw2, r_ffn_norm, r_gate, r_up, r_down, r_final]
    outs = [loss, grad_x.reshape(x.shape)]
    for field in range(4):
        outs += [r[field] for r in order]
    return tuple(outs)
```

```python
import functools

import jax
import jax.numpy as jnp
from jax import lax
from jax.experimental import pallas as pl
from jax.experimental.pallas import tpu as pltpu

RMS_EPS = 1e-6
LN_EPS = 1e-5
ADAM_LR = 0.001
ADAM_B1 = 0.9
ADAM_B2 = 0.999
ADAM_EPS = 1e-08
ADAM_WD = 0.01
ADAM_STEP = 10

N_CHIPS = 4
N_DEV = 8
LANES = 128
SUBLANES = 8
HALO = 32
CONV_ROWS = 64
TOKEN_TILE = 512
WIDE_TOKEN_TILE = 1024
GRAD_TOKEN_TILE = 2048
ROW_TILE = 256
VMEM_LIMIT = 56 * 1024 * 1024
MESH = pl.DeviceIdType.MESH
BF16 = jnp.bfloat16
F32 = jnp.float32


def _tile(n, pref, mult=SUBLANES):
    t = min(n, pref) // mult * mult
    while n % t:
        t -= mult
    return t


def _params(sem):
    return pltpu.CompilerParams(dimension_semantics=sem, vmem_limit_bytes=VMEM_LIMIT)


def _sigmoid(x):
    return jax.nn.sigmoid(x)


class _Exchange:
    def __init__(self, ins, outs, aliases, n_sems, copies):
        self.ins, self.outs, self.aliases, self.n_sems, self.copies = list(ins), list(outs), dict(aliases), n_sems, copies

    def start(self, xi, xo, ssem, rsem):
        for cp in self.copies(xi, xo, ssem, rsem)[0]:
            cp.start()

    def finish(self, xi, xo, ssem, rsem):
        sends, recvs = self.copies(xi, xo, ssem, rsem)
        for cp in recvs:
            cp.wait_recv()
        for cp in sends:
            cp.wait_send()


def _call(body, name, grid, in_specs, out_specs, out_shape, args, sem, scratch_shapes=(), hosted=(), prefetch=(),
          own_aliases=None):
    in_specs, out_specs, out_shape = list(in_specs), list(out_specs), list(out_shape)
    scratch_shapes, hosted, prefetch = list(scratch_shapes), list(hosted), list(prefetch)
    n_pre, n_in, n_out, n_scr = len(prefetch), len(args), len(out_shape), len(scratch_shapes)
    x_in = [a for ex in hosted for a in ex.ins]
    x_out = [o for ex in hosted for o in ex.outs]
    aliases = {n_pre + i: o for i, o in (own_aliases or {}).items()}
    at_in, at_out = n_pre + n_in, n_out
    for ex in hosted:
        for i, o in ex.aliases.items():
            aliases[at_in + i] = at_out + o
        at_in += len(ex.ins)
        at_out += len(ex.outs)
    sems = [pltpu.SemaphoreType.DMA((ex.n_sems,)) for ex in hosted for _ in range(2)]

    def wrapped(*refs):
        pre, refs = refs[:n_pre], refs[n_pre:]
        ins, xi = refs[:n_in], refs[n_in:n_in + len(x_in)]
        refs = refs[n_in + len(x_in):]
        outs, xo = refs[:n_out], refs[n_out:n_out + len(x_out)]
        refs = refs[n_out + len(x_out):]
        scr, sm = refs[:n_scr], refs[n_scr:]
        views, a, b = [], 0, 0
        for e, ex in enumerate(hosted):
            views.append((xi[a:a + len(ex.ins)], xo[b:b + len(ex.outs)], sm[2 * e], sm[2 * e + 1]))
            a += len(ex.ins)
            b += len(ex.outs)
        first = last = None
        for ax, g in enumerate(grid):
            f, l = pl.program_id(ax) == 0, pl.program_id(ax) == g - 1
            first, last = (f, l) if first is None else (first & f, last & l)

        def begin():
            for ex, v in zip(hosted, views):
                ex.start(*v)

        def end():
            for ex, v in zip(hosted, views):
                ex.finish(*v)

        if hosted and grid:
            pl.when(first)(begin)
        elif hosted:
            begin()
        body(*pre, *ins, *outs, *scr)
        if hosted and grid:
            pl.when(last)(end)
        elif hosted:
            end()

    hbm = pl.BlockSpec(memory_space=pl.ANY)
    all_in, all_out = in_specs + [hbm] * len(x_in), out_specs + [hbm] * len(x_out)
    kw = dict(name=name, out_shape=out_shape + x_out, input_output_aliases=aliases,
              compiler_params=_params(tuple("arbitrary" for _ in grid) if hosted else sem))
    if prefetch:
        kw["grid_spec"] = pltpu.PrefetchScalarGridSpec(num_scalar_prefetch=n_pre, grid=grid, in_specs=all_in,
                                                       out_specs=all_out, scratch_shapes=scratch_shapes + sems)
    else:
        kw.update(grid=grid, in_specs=all_in, out_specs=all_out, scratch_shapes=scratch_shapes + sems)
    res = pl.pallas_call(wrapped, **kw)(*prefetch, *args, *x_in)
    return list(res[:n_out]), list(res[n_out:])


def rms_fwd(h, gain, name, hosted=()):
    T, D = h.shape
    tm = _tile(T, TOKEN_TILE)

    def body(h_ref, g_ref, o_ref):
        x = h_ref[...]
        r = lax.rsqrt(jnp.mean(x * x, axis=-1, keepdims=True) + RMS_EPS)
        o_ref[...] = (x * r * g_ref[...]).astype(o_ref.dtype)

    (n,), xo = _call(
        body, name, (T // tm,),
        [pl.BlockSpec((tm, D), lambda i: (i, 0)), pl.BlockSpec((1, D), lambda i: (0, 0))],
        [pl.BlockSpec((tm, D), lambda i: (i, 0))], [jax.ShapeDtypeStruct((T, D), BF16)],
        [h, gain], ("parallel",), hosted=hosted)
    return n, xo


def loss_head(h, gain, tgt, name):
    T, D = h.shape
    tm = _tile(T, TOKEN_TILE)

    def body(h_ref, g_ref, t_ref, loss_ref, dh_ref, dg_ref):
        i = pl.program_id(0)
        x = h_ref[...]
        g = g_ref[...]
        r = lax.rsqrt(jnp.mean(x * x, axis=-1, keepdims=True) + RMS_EPS)
        xhat = x * r
        diff = xhat * g - t_ref[...]
        part_loss = 0.5 * jnp.sum(jnp.mean(diff * diff, axis=-1, keepdims=True), axis=0, keepdims=True)
        dy = diff * (1.0 / D)
        dxhat = dy * g
        dh_ref[...] = r * (dxhat - xhat * jnp.mean(dxhat * xhat, axis=-1, keepdims=True))
        part = jnp.sum(dy * xhat, axis=0, keepdims=True)

        @pl.when(i == 0)
        def _():
            dg_ref[...] = part
            loss_ref[...] = part_loss

        @pl.when(i > 0)
        def _():
            dg_ref[...] += part
            loss_ref[...] += part_loss

    row = pl.BlockSpec((tm, D), lambda i: (i, 0))
    vec = pl.BlockSpec((1, D), lambda i: (0, 0))
    return pl.pallas_call(
        body, name=name, grid=(T // tm,),
        in_specs=[row, vec, row],
        out_specs=[pl.BlockSpec((1, 1), lambda i: (0, 0)), row, vec],
        out_shape=[jax.ShapeDtypeStruct((1, 1), F32), jax.ShapeDtypeStruct((T, D), F32),
                   jax.ShapeDtypeStruct((1, D), F32)],
        compiler_params=_params(("arbitrary",)),
    )(h, gain, tgt)


def _prev_halo_spec(tm, width):
    return pl.BlockSpec((HALO, width), lambda i: (jnp.maximum(i * (tm // HALO) - 1, 0), 0))


def _next_halo_spec(tm, width, T):
    return pl.BlockSpec((HALO, width), lambda i: (jnp.minimum((i + 1) * (tm // HALO), T // HALO - 1), 0))


def _shifted(win, off, rows):
    if off % SUBLANES == 0:
        return win[off:off + rows]
    n = win.shape[0]
    return pltpu.roll(win, (n - off) % n, 0)[:rows]


def _rowsum8(x):
    acc = x[0:SUBLANES]
    for q in range(1, x.shape[0] // SUBLANES):
        acc = acc + x[q * SUBLANES:(q + 1) * SUBLANES]
    return acc


def _conv_loops(tm, D, per_block):
    def chunk(r, carry):
        t0 = pl.multiple_of(r * CONV_ROWS, CONV_ROWS)
        for lb in range(D // LANES):
            per_block(t0, slice(lb * LANES, (lb + 1) * LANES))
        return carry

    lax.fori_loop(0, tm // CONV_ROWS, chunk, 0)


def gateconv_fwd(bcv, w, name, hosted=()):
    T, D3 = bcv.shape
    D = D3 // 3
    K = w.shape[0]
    tm = _tile(T, TOKEN_TILE)

    def body(x_ref, halo_ref, w_ref, y_ref, pad_ref):
        i = pl.program_id(0)
        pad_ref[HALO:, :] = x_ref[:, D:2 * D] * x_ref[:, 2 * D:]
        pad_ref[:HALO, :] = jnp.where(i > 0, halo_ref[:, D:2 * D] * halo_ref[:, 2 * D:], 0.0)

        def block(t0, ls):
            win = pad_ref[pl.ds(t0, CONV_ROWS + HALO), ls]
            acc = jnp.zeros((CONV_ROWS, LANES), F32)
            for k in range(K):
                acc = acc + w_ref[k:k + 1, ls] * _shifted(win, HALO - (K - 1) + k, CONV_ROWS)
            y_ref[pl.ds(t0, CONV_ROWS), ls] = (x_ref[pl.ds(t0, CONV_ROWS), ls] * acc).astype(y_ref.dtype)

        _conv_loops(tm, D, block)

    (y,), xo = _call(
        body, name, (T // tm,),
        [pl.BlockSpec((tm, D3), lambda i: (i, 0)), _prev_halo_spec(tm, D3), pl.BlockSpec((K, D), lambda i: (0, 0))],
        [pl.BlockSpec((tm, D), lambda i: (i, 0))], [jax.ShapeDtypeStruct((T, D), BF16)],
        [bcv, bcv, w], ("parallel",), [pltpu.VMEM((tm + HALO, D), F32)], hosted=hosted)
    return y, xo


def gateconv_bwd(dy, bcv, w, name, hosted=()):
    T, D3 = bcv.shape
    D = D3 // 3
    K = w.shape[0]
    tm = _tile(T, TOKEN_TILE)
    nt = T // tm

    def body(dy_ref, dyn_ref, x_ref, xp_ref, xn_ref, w_ref, o_ref, dw_ref, cv_ref, dc_ref, wacc_ref):
        i = pl.program_id(0)
        cv_ref[HALO:, :] = x_ref[:, D:2 * D] * x_ref[:, 2 * D:]
        cv_ref[:HALO, :] = jnp.where(i > 0, xp_ref[:, D:2 * D] * xp_ref[:, 2 * D:], 0.0)
        dc_ref[:tm, :] = dy_ref[...] * x_ref[:, :D]
        dc_ref[tm:, :] = jnp.where(i < nt - 1, dyn_ref[...] * xn_ref[:, :D], 0.0)

        @pl.when(i == 0)
        def _():
            wacc_ref[...] = jnp.zeros_like(wacc_ref)

        def block(t0, ls):
            cwin = cv_ref[pl.ds(t0, CONV_ROWS + HALO), ls]
            dwin = dc_ref[pl.ds(t0, CONV_ROWS + HALO), ls]
            dcon = dwin[:CONV_ROWS]
            conv = jnp.zeros((CONV_ROWS, LANES), F32)
            dcv = jnp.zeros((CONV_ROWS, LANES), F32)
            for k in range(K):
                wk = w_ref[k:k + 1, ls]
                cs = _shifted(cwin, HALO - (K - 1) + k, CONV_ROWS)
                conv = conv + wk * cs
                dcv = dcv + wk * _shifted(dwin, (K - 1) - k, CONV_ROWS)
                wacc_ref[k * SUBLANES:(k + 1) * SUBLANES, ls] += _rowsum8(dcon * cs)
            rows = pl.ds(t0, CONV_ROWS)
            o_ref[rows, ls] = (dy_ref[rows, ls] * conv).astype(o_ref.dtype)
            o_ref[rows, pl.ds(D + ls.start, LANES)] = (dcv * x_ref[rows, pl.ds(2 * D + ls.start, LANES)]).astype(o_ref.dtype)
            o_ref[rows, pl.ds(2 * D + ls.start, LANES)] = (dcv * x_ref[rows, pl.ds(D + ls.start, LANES)]).astype(o_ref.dtype)

        _conv_loops(tm, D, block)

        @pl.when(i == nt - 1)
        def _():
            for k in range(K):
                dw_ref[k:k + 1, :] = jnp.sum(wacc_ref[k * SUBLANES:(k + 1) * SUBLANES, :], axis=0, keepdims=True)

    (dx, dw), xo = _call(
        body, name, (nt,),
        [pl.BlockSpec((tm, D), lambda i: (i, 0)), _next_halo_spec(tm, D, T),
         pl.BlockSpec((tm, D3), lambda i: (i, 0)), _prev_halo_spec(tm, D3), _next_halo_spec(tm, D3, T),
         pl.BlockSpec((K, D), lambda i: (0, 0))],
        [pl.BlockSpec((tm, D3), lambda i: (i, 0)), pl.BlockSpec((K, D), lambda i: (0, 0))],
        [jax.ShapeDtypeStruct((T, D3), BF16), jax.ShapeDtypeStruct((K, D), F32)],
        [dy, dy, bcv, bcv, bcv, w], ("arbitrary",),
        [pltpu.VMEM((tm + HALO, D), F32), pltpu.VMEM((tm + HALO, D), F32), pltpu.VMEM((K * SUBLANES, D), F32)],
        hosted=hosted)
    return dx, dw, xo


def bconv_fwd(u, w, b_conv, ln_g, ln_b, name, hosted=()):
    T, D2 = u.shape
    D = D2 // 2
    K = w.shape[0]
    tm = _tile(T, TOKEN_TILE)

    def body(u_ref, halo_ref, w_ref, bc_ref, g_ref, b_ref, cu_ref, s_ref, pad_ref):
        i = pl.program_id(0)
        pad_ref[HALO:, :] = u_ref[:, :D] * _sigmoid(u_ref[:, D:])
        pad_ref[:HALO, :] = jnp.where(i > 0, halo_ref[:, :D] * _sigmoid(halo_ref[:, D:]), 0.0)

        def block(t0, ls):
            win = pad_ref[pl.ds(t0, CONV_ROWS + HALO), ls]
            acc = jnp.zeros((CONV_ROWS, LANES), F32)
            for k in range(K):
                acc = acc + w_ref[k:k + 1, ls] * _shifted(win, HALO - (K - 1) + k, CONV_ROWS)
            cu_ref[pl.ds(t0, CONV_ROWS), ls] = acc + bc_ref[:, ls]

        _conv_loops(tm, D, block)
        cu = cu_ref[...]
        mu = jnp.mean(cu, axis=-1, keepdims=True)
        xc = cu - mu
        rstd = lax.rsqrt(jnp.mean(xc * xc, axis=-1, keepdims=True) + LN_EPS)
        ln = xc * rstd * g_ref[...] + b_ref[...]
        s_ref[...] = (ln * _sigmoid(ln)).astype(s_ref.dtype)

    vec = pl.BlockSpec((1, D), lambda i: (0, 0))
    row = pl.BlockSpec((tm, D), lambda i: (i, 0))
    (cu, s), xo = _call(
        body, name, (T // tm,),
        [pl.BlockSpec((tm, D2), lambda i: (i, 0)), _prev_halo_spec(tm, D2), pl.BlockSpec((K, D), lambda i: (0, 0)), vec, vec, vec],
        [row, row], [jax.ShapeDtypeStruct((T, D), F32), jax.ShapeDtypeStruct((T, D), BF16)],
        [u, u, w, b_conv, ln_g, ln_b], ("parallel",), [pltpu.VMEM((tm + HALO, D), F32)], hosted=hosted)
    return cu, s, xo


def ln_silu_bwd(ds, cu, ln_g, ln_b, name):
    T, D = cu.shape
    tm = _tile(T, TOKEN_TILE)

    def body(ds_ref, cu_ref, g_ref, b_ref, dcu_ref, dg_ref, db_ref, dbc_ref):
        i = pl.program_id(0)
        cu_ = cu_ref[...]
        mu = jnp.mean(cu_, axis=-1, keepdims=True)
        xc = cu_ - mu
        rstd = lax.rsqrt(jnp.mean(xc * xc, axis=-1, keepdims=True) + LN_EPS)
        xh = xc * rstd
        ln = xh * g_ref[...] + b_ref[...]
        sg = _sigmoid(ln)
        dl = ds_ref[...] * (sg * (1.0 + ln * (1.0 - sg)))
        dxh = dl * g_ref[...]
        dcu = rstd * (dxh - jnp.mean(dxh, axis=-1, keepdims=True) - xh * jnp.mean(dxh * xh, axis=-1, keepdims=True))
        dcu_ref[...] = dcu
        pg = jnp.sum(dl * xh, axis=0, keepdims=True)
        pb = jnp.sum(dl, axis=0, keepdims=True)
        pc = jnp.sum(dcu, axis=0, keepdims=True)

        @pl.when(i == 0)
        def _():
            dg_ref[...] = pg
            db_ref[...] = pb
            dbc_ref[...] = pc

        @pl.when(i > 0)
        def _():
            dg_ref[...] += pg
            db_ref[...] += pb
            dbc_ref[...] += pc

    vec = pl.BlockSpec((1, D), lambda i: (0, 0))
    row = pl.BlockSpec((tm, D), lambda i: (i, 0))
    vshape = jax.ShapeDtypeStruct((1, D), F32)
    return pl.pallas_call(
        body, name=name, grid=(T // tm,),
        in_specs=[row, row, vec, vec], out_specs=[row, vec, vec, vec],
        out_shape=[jax.ShapeDtypeStruct((T, D), F32), vshape, vshape, vshape],
        compiler_params=_params(("arbitrary",)),
    )(ds, cu, ln_g, ln_b)


def bconv_bwd(dcu, u, w, name, hosted=()):
    T, D2 = u.shape
    D = D2 // 2
    K = w.shape[0]
    tm = _tile(T, TOKEN_TILE)
    nt = T // tm

    def body(dc_ref, dcn_ref, u_ref, up_ref, w_ref, du_ref, dw_ref, db_ref, glu_ref, dpad_ref, dglu_ref, wacc_ref):
        i = pl.program_id(0)
        glu_ref[HALO:, :] = u_ref[:, :D] * _sigmoid(u_ref[:, D:])
        glu_ref[:HALO, :] = jnp.where(i > 0, up_ref[:, :D] * _sigmoid(up_ref[:, D:]), 0.0)
        dpad_ref[:tm, :] = dc_ref[...]
        dpad_ref[tm:, :] = jnp.where(i < nt - 1, dcn_ref[...], 0.0)

        @pl.when(i == 0)
        def _():
            wacc_ref[...] = jnp.zeros_like(wacc_ref)

        def block(t0, ls):
            gwin = glu_ref[pl.ds(t0, CONV_ROWS + HALO), ls]
            dwin = dpad_ref[pl.ds(t0, CONV_ROWS + HALO), ls]
            dcur = dwin[:CONV_ROWS]
            dglu = jnp.zeros((CONV_ROWS, LANES), F32)
            for k in range(K):
                dglu = dglu + w_ref[k:k + 1, ls] * _shifted(dwin, (K - 1) - k, CONV_ROWS)
                gs = _shifted(gwin, HALO - (K - 1) + k, CONV_ROWS)
                wacc_ref[k * SUBLANES:(k + 1) * SUBLANES, ls] += _rowsum8(dcur * gs)
            dglu_ref[pl.ds(t0, CONV_ROWS), ls] = dglu

        _conv_loops(tm, D, block)
        dglu = dglu_ref[...]
        a = u_ref[:, :D]
        sg = _sigmoid(u_ref[:, D:])
        da = dglu * sg
        dg = dglu * a * (sg * (1.0 - sg))
        du_ref[:, :D] = da.astype(du_ref.dtype)
        du_ref[:, D:] = dg.astype(du_ref.dtype)
        pa = jnp.sum(da, axis=0, keepdims=True)
        pg = jnp.sum(dg, axis=0, keepdims=True)

        @pl.when(i == 0)
        def _():
            db_ref[:, :D] = pa
            db_ref[:, D:] = pg

        @pl.when(i > 0)
        def _():
            db_ref[:, :D] += pa
            db_ref[:, D:] += pg

        @pl.when(i == nt - 1)
        def _():
            for k in range(K):
                dw_ref[k:k + 1, :] = jnp.sum(wacc_ref[k * SUBLANES:(k + 1) * SUBLANES, :], axis=0, keepdims=True)

    (du, dw, db), xo = _call(
        body, name, (nt,),
        [pl.BlockSpec((tm, D), lambda i: (i, 0)), _next_halo_spec(tm, D, T),
         pl.BlockSpec((tm, D2), lambda i: (i, 0)), _prev_halo_spec(tm, D2), pl.BlockSpec((K, D), lambda i: (0, 0))],
        [pl.BlockSpec((tm, D2), lambda i: (i, 0)), pl.BlockSpec((K, D), lambda i: (0, 0)), pl.BlockSpec((1, D2), lambda i: (0, 0))],
        [jax.ShapeDtypeStruct((T, D2), BF16), jax.ShapeDtypeStruct((K, D), F32), jax.ShapeDtypeStruct((1, D2), F32)],
        [dcu, dcu, u, u, w], ("arbitrary",),
        [pltpu.VMEM((tm + HALO, D), F32), pltpu.VMEM((tm + HALO, D), F32), pltpu.VMEM((tm, D), F32),
         pltpu.VMEM((K * SUBLANES, D), F32)], hosted=hosted)
    return du, dw, db, xo


def mm_cols(a, w, bias, name, hosted=()):
    T, K = a.shape
    S, _, n = w.shape
    tm = _tile(T, WIDE_TOKEN_TILE)

    def body(*refs):
        a_ref, w_ref = refs[:2]
        o_ref = refs[-1]
        acc = jnp.dot(a_ref[...], w_ref[...], preferred_element_type=F32)
        if bias is not None:
            acc = acc + refs[2][...]
        o_ref[...] = acc

    in_specs = [pl.BlockSpec((tm, K), lambda s, i: (i, 0)), pl.BlockSpec((None, K, n), lambda s, i: (s, 0, 0))]
    args = [a, w]
    if bias is not None:
        in_specs.append(pl.BlockSpec((1, n), lambda s, i: (0, s)))
        args.append(bias)
    (out,), xo = _call(body, name, (S, T // tm), in_specs, [pl.BlockSpec((tm, n), lambda s, i: (i, s))],
                       [jax.ShapeDtypeStruct((T, S * n), F32)], args, ("parallel", "parallel"), hosted=hosted)
    return out, xo


def _load_once(pairs, sems):
    cps = [pltpu.make_async_copy(src, dst, sems.at[k]) for k, (src, dst) in enumerate(pairs)]
    for cp in cps:
        cp.start()
    for cp in cps:
        cp.wait()


def ffn_fwd(h, gain, wg, wu, wd, name, hosted=()):
    T, D = h.shape
    S, f, _ = wg.shape
    tm = _tile(T, TOKEN_TILE)

    def body(h_ref, gain_ref, wg_hbm, wu_hbm, wd_hbm, n_ref, g_ref, u_ref, gu_ref, o_ref, wg_v, wu_v, wd_v, sems):
        i, s = pl.program_id(0), pl.program_id(1)

        @pl.when((i == 0) & (s == 0))
        def _():
            _load_once([(wg_hbm, wg_v), (wu_hbm, wu_v), (wd_hbm, wd_v)], sems)

        @pl.when(s == 0)
        def _():
            x = h_ref[...]
            r = lax.rsqrt(jnp.mean(x * x, axis=-1, keepdims=True) + RMS_EPS)
            n_ref[...] = (x * r * gain_ref[...]).astype(n_ref.dtype)

        a = n_ref[...]
        g = lax.dot_general(a, wg_v[s], _NT, preferred_element_type=F32)
        u = lax.dot_general(a, wu_v[s], _NT, preferred_element_type=F32)
        gu = (g * _sigmoid(g) * u).astype(gu_ref.dtype)
        g_ref[...] = g.astype(g_ref.dtype)
        u_ref[...] = u.astype(u_ref.dtype)
        gu_ref[...] = gu
        part = jnp.dot(gu, wd_v[s], preferred_element_type=F32)

        @pl.when(s == 0)
        def _():
            o_ref[...] = h_ref[...] + part

        @pl.when(s > 0)
        def _():
            o_ref[...] += part

    row = pl.BlockSpec((tm, D), lambda i, s: (i, 0))
    seg = pl.BlockSpec((None, tm, f), lambda i, s: (s, i, 0))
    hbm = pl.BlockSpec(memory_space=pl.ANY)
    segs = jax.ShapeDtypeStruct((S, T, f), BF16)
    outs, xo = _call(
        body, name, (T // tm, S),
        [row, pl.BlockSpec((1, D), lambda i, s: (0, 0)), hbm, hbm, hbm], [row, seg, seg, seg, row],
        [jax.ShapeDtypeStruct((T, D), BF16), segs, segs, segs, jax.ShapeDtypeStruct((T, D), F32)],
        [h, gain, wg, wu, wd], ("arbitrary", "arbitrary"),
        [pltpu.VMEM((S, f, D), BF16), pltpu.VMEM((S, f, D), BF16), pltpu.VMEM((S, f, D), BF16), pltpu.SemaphoreType.DMA((3,))],
        hosted=hosted)
    return (*outs, xo)


def ffn_bwd(dy, h, gain, g, u, wd, wg, wu, name, hosted=()):
    T, D = h.shape
    S, f, _ = wg.shape
    tm = _tile(T, TOKEN_TILE)
    nt = T // tm

    def body(dy_ref, h_ref, gain_ref, g_ref, u_ref, wd_hbm, wg_hbm, wu_hbm, dg_ref, du_ref, dh_ref, dgain_ref,
             wd_v, wg_v, wu_v, dyb_ref, sems):
        i, s = pl.program_id(0), pl.program_id(1)

        @pl.when((i == 0) & (s == 0))
        def _():
            _load_once([(wd_hbm, wd_v), (wg_hbm, wg_v), (wu_hbm, wu_v)], sems)

        @pl.when(s == 0)
        def _():
            dyb_ref[...] = dy_ref[...].astype(dyb_ref.dtype)

        dgu = lax.dot_general(dyb_ref[...], wd_v[s], _NT, preferred_element_type=F32)
        gv = g_ref[...].astype(F32)
        sg = _sigmoid(gv)
        dg = (dgu * u_ref[...].astype(F32) * (sg * (1.0 + gv * (1.0 - sg)))).astype(dg_ref.dtype)
        du = (dgu * (gv * sg)).astype(du_ref.dtype)
        dg_ref[...] = dg
        du_ref[...] = du
        part = (jnp.dot(dg, wg_v[s], preferred_element_type=F32)
                + jnp.dot(du, wu_v[s], preferred_element_type=F32))

        @pl.when(s == 0)
        def _():
            dh_ref[...] = part

        @pl.when(s > 0)
        def _():
            dh_ref[...] += part

        @pl.when(s == S - 1)
        def _():
            dn = dh_ref[...]
            x = h_ref[...]
            r = lax.rsqrt(jnp.mean(x * x, axis=-1, keepdims=True) + RMS_EPS)
            xhat = x * r
            dxhat = dn * gain_ref[...]
            dh_ref[...] = dy_ref[...] + r * (dxhat - xhat * jnp.mean(dxhat * xhat, axis=-1, keepdims=True))
            pg = jnp.sum(dn * xhat, axis=0, keepdims=True)

            @pl.when(i == 0)
            def _():
                dgain_ref[...] = pg

            @pl.when(i > 0)
            def _():
                dgain_ref[...] += pg

    row = pl.BlockSpec((tm, D), lambda i, s: (i, 0))
    vec = pl.BlockSpec((1, D), lambda i, s: (0, 0))
    seg = pl.BlockSpec((None, tm, f), lambda i, s: (s, i, 0))
    hbm = pl.BlockSpec(memory_space=pl.ANY)
    segs = jax.ShapeDtypeStruct((S, T, f), BF16)
    outs, xo = _call(
        body, name, (nt, S),
        [row, row, vec, seg, seg, hbm, hbm, hbm], [seg, seg, row, vec],
        [segs, segs, jax.ShapeDtypeStruct((T, D), F32), jax.ShapeDtypeStruct((1, D), F32)],
        [dy, h, gain, g, u, wd, wg, wu], ("arbitrary", "arbitrary"),
        [pltpu.VMEM((S, f, D), BF16), pltpu.VMEM((S, f, D), BF16), pltpu.VMEM((S, f, D), BF16),
         pltpu.VMEM((tm, D), BF16), pltpu.SemaphoreType.DMA((3,))], hosted=hosted)
    return (*outs, xo)


def mm_rows(a, w, res, bias, name, hosted=()):
    S, T, k = a.shape
    N = w.shape[-1]
    tm = _tile(T, TOKEN_TILE)

    def body(*refs):
        a_ref, w_ref, r_ref = refs[:3]
        o_ref = refs[-1]
        s = pl.program_id(1)
        acc = jnp.dot(a_ref[...], w_ref[...], preferred_element_type=F32)

        @pl.when(s == 0)
        def _():
            base = r_ref[...]
            if bias is not None:
                base = base + refs[3][...]
            o_ref[...] = base + acc

        @pl.when(s > 0)
        def _():
            o_ref[...] += acc

    in_specs = [pl.BlockSpec((None, tm, k), lambda i, s: (s, i, 0)),
                pl.BlockSpec((None, k, N), lambda i, s: (s, 0, 0)),
                pl.BlockSpec((tm, N), lambda i, s: (i, 0))]
    args = [a, w, res]
    if bias is not None:
        in_specs.append(pl.BlockSpec((1, N), lambda i, s: (0, 0)))
        args.append(bias)
    (out,), xo = _call(body, name, (T // tm, S), in_specs, [pl.BlockSpec((tm, N), lambda i, s: (i, 0))],
                       [jax.ShapeDtypeStruct((T, N), F32)], args, ("parallel", "arbitrary"), hosted=hosted)
    return out, xo


_NT = (((1,), (1,)), ((), ()))
_TN = (((0,), (0,)), ((), ()))


def nt_rows(dy, w, want_colsum, name, hosted=()):
    T, N = dy.shape
    S, k, _ = w.shape
    tm = _tile(T, TOKEN_TILE)

    def body(dy_ref, w_ref, o_ref, *rest):
        i, s = pl.program_id(0), pl.program_id(1)
        d = dy_ref[...]
        o_ref[...] = lax.dot_general(d.astype(BF16), w_ref[...], _NT, preferred_element_type=F32)
        if want_colsum:
            cs_ref = rest[0]
            part = jnp.sum(d, axis=0, keepdims=True)

            @pl.when((i == 0) & (s == 0))
            def _():
                cs_ref[...] = part

            @pl.when((i > 0) & (s == 0))
            def _():
                cs_ref[...] += part

    out_specs = [pl.BlockSpec((None, tm, k), lambda i, s: (s, i, 0))]
    out_shape = [jax.ShapeDtypeStruct((S, T, k), F32)]
    if want_colsum:
        out_specs.append(pl.BlockSpec((1, N), lambda i, s: (0, 0)))
        out_shape.append(jax.ShapeDtypeStruct((1, N), F32))
    outs, xo = _call(
        body, name, (T // tm, S),
        [pl.BlockSpec((tm, N), lambda i, s: (i, 0)), pl.BlockSpec((None, k, N), lambda i, s: (s, 0, 0))],
        out_specs, out_shape, [dy, w], ("arbitrary", "arbitrary"), hosted=hosted)
    return (*outs, xo)


def nt_cols_rms(dy, w, h, gain, dres, name, hosted=()):
    T, K = h.shape
    S, _, n = w.shape
    tm = _tile(T, TOKEN_TILE)

    def body(dy_ref, w_ref, h_ref, gain_ref, dres_ref, dh_ref, dgain_ref):
        i = pl.program_id(0)
        dn = None
        for s in range(S):
            part = lax.dot_general(dy_ref[:, s * n:(s + 1) * n], w_ref[s], _NT, preferred_element_type=F32)
            dn = part if dn is None else dn + part
        x = h_ref[...]
        r = lax.rsqrt(jnp.mean(x * x, axis=-1, keepdims=True) + RMS_EPS)
        xhat = x * r
        dxhat = dn * gain_ref[...]
        dh_ref[...] = dres_ref[...] + r * (dxhat - xhat * jnp.mean(dxhat * xhat, axis=-1, keepdims=True))
        pg = jnp.sum(dn * xhat, axis=0, keepdims=True)

        @pl.when(i == 0)
        def _():
            dgain_ref[...] = pg

        @pl.when(i > 0)
        def _():
            dgain_ref[...] += pg

    row = pl.BlockSpec((tm, K), lambda i: (i, 0))
    vec = pl.BlockSpec((1, K), lambda i: (0, 0))
    (dh, dgain), xo = _call(
        body, name, (T // tm,),
        [pl.BlockSpec((tm, S * n), lambda i: (i, 0)), pl.BlockSpec((S, K, n), lambda i: (0, 0, 0)), row, vec, row],
        [row, vec], [jax.ShapeDtypeStruct((T, K), F32), jax.ShapeDtypeStruct((1, K), F32)],
        [dy, w, h, gain, dres], ("arbitrary",), hosted=hosted)
    return dh, dgain, xo


def tn_grad(a, dy, S, a_by_seg, name, hosted=()):
    T = dy.shape[0] if dy.ndim == 2 else dy.shape[1]
    tt = _tile(T, GRAD_TOKEN_TILE)
    if a_by_seg:
        R = a.shape[1] // S if a.ndim == 2 else a.shape[2]
        C = dy.shape[1]
        a_spec = pl.BlockSpec((tt, R), lambda s, t: (t, s)) if a.ndim == 2 else pl.BlockSpec((None, tt, R), lambda s, t: (s, t, 0))
        b_spec = pl.BlockSpec((tt, C), lambda s, t: (t, 0))
    else:
        R = a.shape[1]
        C = dy.shape[1] // S if dy.ndim == 2 else dy.shape[2]
        a_spec = pl.BlockSpec((tt, R), lambda s, t: (t, 0))
        b_spec = pl.BlockSpec((tt, C), lambda s, t: (t, s)) if dy.ndim == 2 else pl.BlockSpec((None, tt, C), lambda s, t: (s, t, 0))
    Rh = R // 2
    nt = T // tt

    def body(a_ref, b_ref, o_ref, acc_ref):
        t = pl.program_id(1)
        part = lax.dot_general(a_ref[...], b_ref[...].astype(BF16), _TN, preferred_element_type=F32)

        @pl.when(t == 0)
        def _():
            acc_ref[...] = part

        @pl.when(t > 0)
        def _():
            acc_ref[...] += part

        @pl.when(t == nt - 1)
        def _():
            o_ref[0] = acc_ref[:Rh, :].astype(o_ref.dtype)
            o_ref[1] = acc_ref[Rh:, :].astype(o_ref.dtype)

    (gh,), xo = _call(
        body, name, (S, nt), [a_spec, b_spec], [pl.BlockSpec((2, None, Rh, C), lambda s, t: (0, s, 0, 0))],
        [jax.ShapeDtypeStruct((2, S, Rh, C), BF16)], [a, dy], ("parallel", "arbitrary"), [pltpu.VMEM((R, C), F32)],
        hosted=hosted)
    return gh, xo


def _place():
    x, y, c = lax.axis_index("x"), lax.axis_index("y"), lax.axis_index("c")
    chips = [(1 - x, y), (x, 1 - y), (1 - x, 1 - y)]
    return x, y, c, chips


def _any_specs(n):
    return [pl.BlockSpec(memory_space=pl.ANY)] * n


def _remote(src, dst, send_sem, recv_sem, dev):
    return pltpu.make_async_remote_copy(src_ref=src, dst_ref=dst, send_sem=send_sem, recv_sem=recv_sem,
                                        device_id=dev, device_id_type=MESH)


def small_allgather(v, name):
    rows, W = v.shape

    def body(v_ref, o_ref, send_sems, recv_sems, loc_sem):
        x, y, c, _ = _place()
        mine = pltpu.make_async_copy(v_ref, o_ref.at[4 * x + 2 * y + c], loc_sem)
        mine.start()

        def peer_of(m):
            return ((1 - x) if m & 4 else x, (1 - y) if m & 2 else y, (1 - c) if m & 1 else c)

        sends = []
        for m in range(1, N_DEV):
            cp = _remote(v_ref, o_ref.at[4 * x + 2 * y + c], send_sems.at[m - 1], recv_sems.at[m - 1], peer_of(m))
            cp.start()
            sends.append(cp)
        for m in range(1, N_DEV):
            px, py, pc = peer_of(m)
            blk = o_ref.at[4 * px + 2 * py + pc]
            _remote(blk, blk, send_sems.at[m - 1], recv_sems.at[m - 1], (px, py, pc)).wait_recv()
        for cp in sends:
            cp.wait_send()
        mine.wait()

    return pl.pallas_call(
        body, name=name,
        in_specs=[pl.BlockSpec(memory_space=pltpu.VMEM)], out_specs=pl.BlockSpec(memory_space=pltpu.VMEM),
        out_shape=jax.ShapeDtypeStruct((N_DEV, rows, W), F32),
        scratch_shapes=[pltpu.SemaphoreType.DMA((N_DEV - 1,)), pltpu.SemaphoreType.DMA((N_DEV - 1,)), pltpu.SemaphoreType.DMA],
    )(v)


def small_allreduce(v, name, hosted=()):
    rows, W = v.shape

    def body(v_ref, o_ref, sib_ref, pair_ref, chips_ref, send_sems, recv_sems):
        x, y, c, chips = _place()
        me = 2 * x + y
        swap = _remote(v_ref, sib_ref, send_sems.at[3], recv_sems.at[3], (x, y, 1 - c))
        swap.start()
        swap.wait()
        mine, other = v_ref[...], sib_ref[...]
        pair_ref[...] = jnp.where(c == 0, mine, other) + jnp.where(c == 0, other, mine)
        sends = []
        for j, (px, py) in enumerate(chips):
            cp = _remote(pair_ref, chips_ref.at[me], send_sems.at[j], recv_sems.at[j], (px, py, c))
            cp.start()
            sends.append(cp)
        chips_ref[me] = pair_ref[...]
        for j, (px, py) in enumerate(chips):
            blk = chips_ref.at[2 * px + py]
            _remote(blk, blk, send_sems.at[j], recv_sems.at[j], (px, py, c)).wait_recv()
        for cp in sends:
            cp.wait_send()
        o_ref[...] = (chips_ref[0] + chips_ref[1]) + (chips_ref[2] + chips_ref[3])

    vm = pl.BlockSpec(memory_space=pltpu.VMEM)
    (out,), xo = _call(
        body, name, (), [vm], [vm], [jax.ShapeDtypeStruct((rows, W), F32)], [v], (),
        [pltpu.VMEM((rows, W), F32), pltpu.VMEM((rows, W), F32), pltpu.VMEM((N_CHIPS, rows, W), F32),
         pltpu.SemaphoreType.DMA((4,)), pltpu.SemaphoreType.DMA((4,))], hosted=hosted)
    return out, xo


def _gather_p1_copies(srcs, bufs, ssem, rsem, base):
    x, y, c, chips = _place()
    me, sib = 2 * x + y, (x, y, 1 - c)
    sends, recvs = [], []
    for k, (src, buf) in enumerate(zip(srcs, bufs)):
        rh = src.shape[0] // 2
        s0 = base + 4 * k
        sends.append(_remote(src, buf.at[me], ssem.at[s0 + 3], rsem.at[s0 + 3], sib))
        recvs.append(_remote(buf.at[me], buf.at[me], ssem.at[s0 + 3], rsem.at[s0 + 3], sib))
        for j, (px, py) in enumerate(chips):
            sends.append(_remote(src.at[pl.ds(c * rh, rh)], buf.at[me, pl.ds(c * rh, rh)], ssem.at[s0 + j], rsem.at[s0 + j], (px, py, c)))
            blk = buf.at[2 * px + py, pl.ds(c * rh, rh)]
            recvs.append(_remote(blk, blk, ssem.at[s0 + j], rsem.at[s0 + j], (px, py, c)))
    return sends, recvs


def _gather_p2_copies(bufs, ssem, rsem, base):
    x, y, c, chips = _place()
    sib = (x, y, 1 - c)
    sends, recvs = [], []
    for k, buf in enumerate(bufs):
        rh = buf.shape[1] // 2
        for j, (px, py) in enumerate(chips):
            s0 = base + 3 * k + j
            blk = buf.at[2 * px + py, pl.ds(c * rh, rh)]
            sends.append(_remote(blk, blk, ssem.at[s0], rsem.at[s0], sib))
            got = buf.at[2 * px + py, pl.ds((1 - c) * rh, rh)]
            recvs.append(_remote(got, got, ssem.at[s0], rsem.at[s0], sib))
    return sends, recvs


def _gathered_shape(s):
    return jax.ShapeDtypeStruct((N_CHIPS,) + s.shape, s.dtype)


def gather_p1(shards):
    return _Exchange(shards, [_gathered_shape(s) for s in shards], {}, 4 * len(shards),
                     lambda xi, xo, ss, rs: _gather_p1_copies(xi, xo, ss, rs, 0))


def gather_p2(bufs):
    return _Exchange(bufs, [jax.ShapeDtypeStruct(b.shape, b.dtype) for b in bufs], {k: k for k in range(len(bufs))},
                     3 * len(bufs), lambda xi, xo, ss, rs: _gather_p2_copies(xo, ss, rs, 0))


def gather_first(whole, begun, name):
    nw, n = len(whole), len(whole) + len(begun)

    def body(*refs):
        ins, outs = refs[:n], refs[n:2 * n]
        ssem, rsem = refs[2 * n:]
        s1, r1 = _gather_p1_copies(ins, outs, ssem, rsem, 0)
        for cp in s1:
            cp.start()
        for cp in r1[:4 * nw]:
            cp.wait_recv()
        s2, r2 = _gather_p2_copies(outs[:nw], ssem, rsem, 4 * n)
        for cp in s2:
            cp.start()
        for cp in r1[4 * nw:] + r2:
            cp.wait_recv()
        for cp in s1 + s2:
            cp.wait_send()

    shards = list(whole) + list(begun)
    return pl.pallas_call(
        body, name=name, in_specs=_any_specs(n), out_specs=_any_specs(n),
        out_shape=[_gathered_shape(s) for s in shards],
        scratch_shapes=[pltpu.SemaphoreType.DMA((4 * n + 3 * nw,)), pltpu.SemaphoreType.DMA((4 * n + 3 * nw,))],
    )(*shards)


def sibling_halves(grads):
    def copies(xi, xo, ssem, rsem):
        x, y, c, _ = _place()
        sends = [_remote(xi[k].at[1 - c], xo[k], ssem.at[k], rsem.at[k], (x, y, 1 - c)) for k in range(len(grads))]
        return sends, sends

    return _Exchange(grads, [jax.ShapeDtypeStruct(g.shape[1:], g.dtype) for g in grads], {}, len(grads), copies)


def pair_sum(gh, recv, cidx, name):
    _, S, Rh, C = gh.shape

    def body(c_ref, a_ref, b_ref, o_ref):
        o_ref[...] = (a_ref[...].astype(F32) + b_ref[...].astype(F32)).astype(o_ref.dtype)

    return pl.pallas_call(
        body, name=name, out_shape=jax.ShapeDtypeStruct((S, Rh, C), BF16),
        grid_spec=pltpu.PrefetchScalarGridSpec(
            num_scalar_prefetch=1, grid=(S,),
            in_specs=[pl.BlockSpec((None, None, Rh, C), lambda s, c_ref: (c_ref[0], s, 0, 0)),
                      pl.BlockSpec((None, Rh, C), lambda s, c_ref: (s, 0, 0))],
            out_specs=pl.BlockSpec((None, Rh, C), lambda s, c_ref: (s, 0, 0))),
        compiler_params=_params(("parallel",)),
    )(cidx, gh, recv)


def scatter_p1(parts):
    def copies(xi, xo, ssem, rsem):
        x, y, c, chips = _place()
        me, sib = 2 * x + y, (x, y, 1 - c)
        sends, recvs = [], []
        for k in range(len(parts)):
            s0 = 4 * k
            sends.append(_remote(xi[k].at[me], xo[k].at[me, c], ssem.at[s0 + 3], rsem.at[s0 + 3], sib))
            own = xo[k].at[me, 1 - c]
            recvs.append(_remote(own, own, ssem.at[s0 + 3], rsem.at[s0 + 3], sib))
            for j, (px, py) in enumerate(chips):
                sends.append(_remote(xi[k].at[2 * px + py], xo[k].at[me, c], ssem.at[s0 + j], rsem.at[s0 + j], (px, py, c)))
                blk = xo[k].at[2 * px + py, c]
                recvs.append(_remote(blk, blk, ssem.at[s0 + j], rsem.at[s0 + j], (px, py, c)))
        return sends, recvs

    return _Exchange(parts, [jax.ShapeDtypeStruct((p.shape[0], 2) + p.shape[1:], p.dtype) for p in parts], {},
                     4 * len(parts), copies)


def scatter_p2(bufs):
    def copies(xi, xo, ssem, rsem):
        x, y, c, chips = _place()
        sib = (x, y, 1 - c)
        sends, recvs = [], []
        for k in range(len(bufs)):
            for j, (px, py) in enumerate(chips):
                s0 = 3 * k + j
                blk = xo[k].at[2 * px + py, c]
                sends.append(_remote(blk, blk, ssem.at[s0], rsem.at[s0], sib))
                got = xo[k].at[2 * px + py, 1 - c]
                recvs.append(_remote(got, got, ssem.at[s0], rsem.at[s0], sib))
        return sends, recvs

    return _Exchange(bufs, [jax.ShapeDtypeStruct(b.shape, b.dtype) for b in bufs], {k: k for k in range(len(bufs))},
                     3 * len(bufs), copies)


def _adamw_math(w, g, m, v):
    m = ADAM_B1 * m + (1.0 - ADAM_B1) * g
    v = ADAM_B2 * v + (1.0 - ADAM_B2) * (g * g)
    m_hat = m / (1.0 - ADAM_B1 ** ADAM_STEP)
    v_hat = v / (1.0 - ADAM_B2 ** ADAM_STEP)
    delta = -ADAM_LR * (m_hat / (jnp.sqrt(v_hat) + ADAM_EPS) + ADAM_WD * w)
    return delta, m, v


def adamw_reduce(w, m, v, buf, part, place, lyr, bases, name, hosted=()):
    L, R, C = w.shape
    Rh = R // 2
    rb = _tile(Rh, ROW_TILE, 2 * SUBLANES)
    nb = Rh // rb

    def body(place_ref, p_ref, b0, b1, b2, b3, w_ref, m_ref, v_ref, *rest):
        go_ref, d_ref, mo_ref, vo_ref = rest[-4:]
        mine = (place_ref[1] == pl.program_id(0))
        g = None
        for p, b in enumerate((b0, b1, b2, b3)):
            val = jnp.where(mine & (place_ref[0] == p), p_ref[...], b[...]).astype(F32)
            g = val if g is None else g + val
        d, mn, vn = _adamw_math(w_ref[...], g, m_ref[...], v_ref[...])
        go_ref[...] = g
        d_ref[...] = d
        mo_ref[...] = mn
        vo_ref[...] = vn

    def buf_spec(p):
        def idx(h, i, pr):
            own = (pr[0] == p) & (pr[1] == h)
            return (p, jnp.where(own, 1 - h, h), i, 0)
        return pl.BlockSpec((None, None, rb, C), idx)

    blk = pl.BlockSpec((None, rb, C), lambda h, i, pr: (lyr, h * nb + i, 0))
    in_specs = [pl.BlockSpec((None, rb, C), lambda h, i, pr: (pr[0], i, 0))] + [buf_spec(p) for p in range(N_CHIPS)] + [blk] * 3
    args = [part, buf, buf, buf, buf, w, m, v]
    aliases = {}
    if bases is not None:
        in_specs += [pl.BlockSpec(memory_space=pl.ANY)] * 4
        aliases = {len(args) + k: k for k in range(4)}
        args += list(bases)
    shp = jax.ShapeDtypeStruct((L, R, C), F32)
    return _call(body, name, (2, nb), in_specs, [blk] * 4, [shp] * 4, args, ("parallel", "parallel"),
                 hosted=hosted, prefetch=[place], own_aliases=aliases)


def small_update(gall, chip, entries, name):
    ne = len(entries)
    D = gall.shape[1]

    def body(chip_ref, gall_ref, *refs):
        ins, outs = refs[:3 * ne], refs[3 * ne:]
        ch = chip_ref[0]
        for e, (row0, kind, w, _, _) in enumerate(entries):
            r, width = w.shape

            def gsum(rs, cs):
                return gall_ref[rs, cs]

            if kind == "full":
                g = gsum(slice(row0, row0 + r), slice(0, D))
            elif kind == "cols":
                g = gsum(slice(row0, row0 + r), slice(0, width))
                for q in range(1, N_CHIPS):
                    g = jnp.where(ch == q, gsum(slice(row0, row0 + r), slice(q * width, (q + 1) * width)), g)
            else:
                per_row = D // width
                g = gsum(slice(row0, row0 + 1), slice(0, width))
                for q in range(1, N_CHIPS):
                    rr = row0 + q // per_row
                    cc = (q % per_row) * width
                    g = jnp.where(ch == q, gsum(slice(rr, rr + 1), slice(cc, cc + width)), g)
            d, mn, vn = _adamw_math(ins[3 * e][...], g, ins[3 * e + 1][...], ins[3 * e + 2][...])
            outs[4 * e][...] = g
            outs[4 * e + 1][...] = d
            outs[4 * e + 2][...] = mn
            outs[4 * e + 3][...] = vn

    vm = pl.BlockSpec(memory_space=pltpu.VMEM)
    args, out_shape = [], []
    for _, _, w, m, v in entries:
        args += [w, m, v]
        out_shape += [jax.ShapeDtypeStruct(w.shape, F32)] * 4
    return pl.pallas_call(
        body, name=name,
        in_specs=[pl.BlockSpec(memory_space=pltpu.SMEM), vm] + [vm] * (3 * ne),
        out_specs=[vm] * (4 * ne), out_shape=out_shape,
        compiler_params=pltpu.CompilerParams(vmem_limit_bytes=VMEM_LIMIT),
    )(chip, gall, *args)


def _pack_rows(items, width):
    rows, starts, at = [], [], 0
    for it in items:
        r = it.shape[0]
        pad = (-r) % SUBLANES
        starts.append(at)
        rows.append(it)
        if pad:
            rows.append(jnp.zeros((pad, width), F32))
        at += r + pad
    return jnp.concatenate(rows, axis=0), starts


def kernel(x, a_norm, a_w_in, a_conv, a_w_out, b_norm, b_w_pw1, b_b_pw1, b_conv, b_b_conv, b_ln_g, b_ln_b, b_w_pw2, b_b_pw2, ffn_norm, ffn_w_gate, ffn_w_up, ffn_w_down, final_norm, loss_target, m_a_norm, m_a_w_in, m_a_conv, m_a_w_out, m_b_norm, m_b_w_pw1, m_b_b_pw1, m_b_conv, m_b_b_conv, m_b_ln_g, m_b_ln_b, m_b_w_pw2, m_b_b_pw2, m_ffn_norm, m_ffn_w_gate, m_ffn_w_up, m_ffn_w_down, m_final_norm, v_a_norm, v_a_w_in, v_a_conv, v_a_w_out, v_b_norm, v_b_w_pw1, v_b_b_pw1, v_b_conv, v_b_b_conv, v_b_ln_g, v_b_ln_b, v_b_w_pw2, v_b_b_pw2, v_ffn_norm, v_ffn_w_gate, v_ffn_w_up, v_ffn_w_down, v_final_norm):
    T, D = x.shape[1], x.shape[2]
    Dq = D // N_CHIPS
    cx, cy, cc = lax.axis_index("x"), lax.axis_index("y"), lax.axis_index("c")
    chip = (2 * cx + cy).astype(jnp.int32).reshape(1)
    cidx = cc.astype(jnp.int32).reshape(1)
    h0 = x.reshape(T, D)
    tgt = loss_target.reshape(T, D)

    small_shards = [a_conv[0], b_norm, b_b_pw1.reshape(2, Dq), b_conv[0], b_b_conv, b_ln_g, b_ln_b, b_b_pw2]
    packed, st = _pack_rows(small_shards, Dq)
    sw = small_allgather(packed, "gather_small")[0::2]

    def whole(k, r):
        return jnp.transpose(sw[:, st[k]:st[k] + r, :], (1, 0, 2)).reshape(r, D)

    a_conv_f, b_norm_f = whole(0, 3), whole(1, 1)
    b_b_pw1_f = sw[:, st[2]:st[2] + 2, :].reshape(1, 2 * D)
    b_conv_f, b_b_conv_f, b_ln_g_f, b_ln_b_f, b_b_pw2_f = whole(3, b_conv.shape[1]), whole(4, 1), whole(5, 1), whole(6, 1), whole(7, 1)

    tr = lambda t: jnp.swapaxes(t, 1, 2)
    w_gate, m_gate, v_gate = tr(ffn_w_gate), tr(m_ffn_w_gate), tr(v_ffn_w_gate)
    w_up, m_up, v_up = tr(ffn_w_up), tr(m_ffn_w_up), tr(v_ffn_w_up)
    bf = lambda t: t.astype(BF16)
    s_in, s_out, s_pw1, s_pw2 = bf(a_w_in[0]), bf(a_w_out[0]), bf(b_w_pw1[0]), bf(b_w_pw2[0])
    s_gate, s_up, s_down = [bf(w_gate[l]) for l in (0, 1)], [bf(w_up[l]) for l in (0, 1)], [bf(ffn_w_down[l]) for l in (0, 1)]

    g_in, g_out, gate0 = gather_first([s_in, s_out], [s_gate[0]], "gather_first")
    g_out = g_out.reshape(1, D, D)
    n0, _ = rms_fwd(h0, a_norm, "rms_a")
    bcv, (up0, gate0) = mm_cols(n0, g_in, None, "mm_w_in", hosted=[gather_p1([s_up[0]]), gather_p2([gate0])])
    ya, (down0, up0) = gateconv_fwd(bcv, a_conv_f, "gateconv_fwd", hosted=[gather_p1([s_down[0]]), gather_p2([up0])])
    h1, (down0,) = mm_rows(ya[None], g_out, h0, None, "mm_w_out", hosted=[gather_p2([down0])])
    n1, fg0, fu0, gu0, h2, later = ffn_fwd(h1, ffn_norm[0:1], gate0, up0, down0, "ffn_fwd0",
                                           hosted=[gather_p1([s_pw1, s_pw2, s_gate[1], s_up[1]])])
    n2, (g_pw1, g_pw2, gate1, up1) = rms_fwd(h2, b_norm_f, "rms_b", hosted=[gather_p2(later)])
    g_pw2 = g_pw2.reshape(1, D, D)
    ub, (down1,) = mm_cols(n2, g_pw1, b_b_pw1_f, "mm_pw1", hosted=[gather_p1([s_down[1]])])
    cu, sb, (down1,) = bconv_fwd(ub, b_conv_f, b_b_conv_f, b_ln_g_f, b_ln_b_f, "bconv_fwd", hosted=[gather_p2([down1])])
    h3, _ = mm_rows(sb[None], g_pw2, h2, b_b_pw2_f, "mm_pw2")
    n3, fg1, fu1, gu1, h4, _ = ffn_fwd(h3, ffn_norm[1:2], gate1, up1, down1, "ffn_fwd1")
    loss_part, dh4, d_final = loss_head(h4, final_norm.reshape(1, D), tgt, "loss_head")

    place = jnp.concatenate([chip, cidx])

    def pair_sums(ghs, from_sib, tags):
        return [pair_sum(g, r, cidx, "pair_sum_" + t) for g, r, t in zip(ghs, from_sib, tags)]

    def upd(w, m, v, bufs, parts, tag, hosted=()):
        res, xo = None, []
        for lyr, (b, p) in enumerate(zip(bufs, parts)):
            res, xo_l = adamw_reduce(w, m, v, b, p, place, lyr, res, "adamw_%s%d" % (tag, lyr), hosted=hosted if lyr == 0 else ())
            xo += xo_l
        return res, xo

    dg1, du1, dh3, d_fn1, _ = ffn_bwd(dh4, h3, ffn_norm[1:2], fg1, fu1, down1, gate1, up1, "ffn_bwd1")
    gh_down1, _ = tn_grad(gu1, dh4, N_CHIPS, True, "tn_down1")
    gh_gate1, _ = tn_grad(dg1, n3, N_CHIPS, True, "tn_gate1")
    gh_up1, _ = tn_grad(du1, n3, N_CHIPS, True, "tn_up1")
    f1 = [gh_gate1, gh_up1, gh_down1]

    ds, d_b_pw2, sib_f1 = nt_rows(dh3, g_pw2, True, "nt_pw2", hosted=[sibling_halves(f1)])
    p_f1 = pair_sums(f1, sib_f1, ["gate1", "up1", "down1"])
    gh_pw2, _ = tn_grad(sb, dh3, N_CHIPS, True, "tn_pw2")
    dcu, d_ln_g, d_ln_b, d_b_conv = ln_silu_bwd(ds[0], cu, b_ln_g_f, b_ln_b_f, "ln_silu_bwd")
    dub, d_bconv_w, d_b_pw1, buf_f1 = bconv_bwd(dcu, ub, b_conv_f, "bconv_bwd", hosted=[scatter_p1(p_f1)])
    gh_pw1, buf_f1 = tn_grad(n2, dub, N_CHIPS, False, "tn_pw1", hosted=[scatter_p2(buf_f1)])
    b_grp = [gh_pw1, gh_pw2]
    dh2, d_b_norm, sib_b = nt_cols_rms(dub, g_pw1, h2, b_norm_f, dh3, "nt_pw1", hosted=[sibling_halves(b_grp)])
    p_b = pair_sums(b_grp, sib_b, ["pw1", "pw2"])

    dg0, du0, dh1, d_fn0, buf_b = ffn_bwd(dh2, h1, ffn_norm[0:1], fg0, fu0, down0, gate0, up0, "ffn_bwd0", hosted=[scatter_p1(p_b)])
    gh_down0, buf_b = tn_grad(gu0, dh2, N_CHIPS, True, "tn_down0", hosted=[scatter_p2(buf_b)])
    gh_gate0, _ = tn_grad(dg0, n1, N_CHIPS, True, "tn_gate0")
    gh_up0, _ = tn_grad(du0, n1, N_CHIPS, True, "tn_up0")
    f0 = [gh_gate0, gh_up0, gh_down0]

    dya, sib_f0 = nt_rows(dh1, g_out, False, "nt_w_out", hosted=[sibling_halves(f0)])
    p_f0 = pair_sums(f0, sib_f0, ["gate0", "up0", "down0"])
    gh_out, (buf_gate0,) = tn_grad(ya, dh1, N_CHIPS, True, "tn_w_out", hosted=[scatter_p1(p_f0[0:1])])
    dbcv, d_aconv_w, (buf_up0,) = gateconv_bwd(dya[0], bcv, a_conv_f, "gateconv_bwd", hosted=[scatter_p1(p_f0[1:2])])
    gh_in, (buf_down0,) = tn_grad(n0, dbcv, N_CHIPS, False, "tn_w_in", hosted=[scatter_p1(p_f0[2:3])])
    a_grp = [gh_in, gh_out]
    grad_x, d_a_norm, (buf_gate0, buf_up0, buf_down0, sib_in, sib_out) = nt_cols_rms(
        dbcv, g_in, h0, a_norm, dh1, "nt_w_in", hosted=[scatter_p2([buf_gate0, buf_up0, buf_down0]), sibling_halves(a_grp)])
    p_a = pair_sums(a_grp, [sib_in, sib_out], ["in", "out"])

    d_ffn_norm = jnp.concatenate([d_fn0, d_fn1], axis=0)
    small_grads = [d_a_norm, d_aconv_w, d_b_norm, d_b_pw1.reshape(2, D), d_bconv_w, d_b_conv, d_ln_g, d_ln_b, d_b_pw2,
                   d_ffn_norm, d_final, jnp.broadcast_to(loss_part, (1, D))]
    gpacked, gs = _pack_rows(small_grads, D)
    gall, buf_a = small_allreduce(gpacked, "allreduce_small_grads", hosted=[scatter_p1(p_a)])

    r_gate, buf_a = upd(w_gate, m_gate, v_gate, [buf_gate0, buf_f1[0]], [p_f0[0], p_f1[0]], "gate", hosted=[scatter_p2(buf_a)])
    r_up, _ = upd(w_up, m_up, v_up, [buf_up0, buf_f1[1]], [p_f0[1], p_f1[1]], "up")
    r_down, _ = upd(ffn_w_down, m_ffn_w_down, v_ffn_w_down, [buf_down0, buf_f1[2]], [p_f0[2], p_f1[2]], "down")
    r_gate, r_up = [tr(t) for t in r_gate], [tr(t) for t in r_up]
    r_pw1, _ = upd(b_w_pw1, m_b_w_pw1, v_b_w_pw1, [buf_b[0]], [p_b[0]], "pw1")
    r_pw2, _ = upd(b_w_pw2, m_b_w_pw2, v_b_w_pw2, [buf_b[1]], [p_b[1]], "pw2")
    r_in, _ = upd(a_w_in, m_a_w_in, v_a_w_in, [buf_a[0]], [p_a[0]], "w_in")
    r_out, _ = upd(a_w_out, m_a_w_out, v_a_w_out, [buf_a[1]], [p_a[1]], "w_out")
    entries = [
        (gs[0], "full", a_norm, m_a_norm, v_a_norm),
        (gs[1], "cols", a_conv[0], m_a_conv[0], v_a_conv[0]),
        (gs[2], "cols", b_norm, m_b_norm, v_b_norm),
        (gs[3], "flat2", b_b_pw1, m_b_b_pw1, v_b_b_pw1),
        (gs[4], "cols", b_conv[0], m_b_conv[0], v_b_conv[0]),
        (gs[5], "cols", b_b_conv, m_b_b_conv, v_b_b_conv),
        (gs[6], "cols", b_ln_g, m_b_ln_g, v_b_ln_g),
        (gs[7], "cols", b_ln_b, m_b_ln_b, v_b_ln_b),
        (gs[8], "cols", b_b_pw2, m_b_b_pw2, v_b_b_pw2),
        (gs[9], "full", ffn_norm, m_ffn_norm, v_ffn_norm),
        (gs[10], "full", final_norm.reshape(1, D), m_final_norm.reshape(1, D), v_final_norm.reshape(1, D)),
    ]
    so = small_update(gall, chip, entries, "small_update")
    sm = [so[4 * e:4 * e + 4] for e in range(len(entries))]

    def shaped(e, like):
        return [t.reshape(like.shape) for t in sm[e]]

    r_a_norm, r_a_conv, r_b_norm, r_b_b_pw1 = shaped(0, a_norm), shaped(1, a_conv), shaped(2, b_norm), shaped(3, b_b_pw1)
    r_b_conv, r_b_b_conv, r_b_ln_g, r_b_ln_b = shaped(4, b_conv), shaped(5, b_b_conv), shaped(6, b_ln_g), shaped(7, b_ln_b)
    r_b_b_pw2, r_ffn_norm, r_final = shaped(8, b_b_pw2), shaped(9, ffn_norm), shaped(10, final_norm)

    loss = gall[gs[11], 0]
    order =[r_a_norm, r_in, r_a_conv, r_out, r_b_norm, r_pw1, r_b_b_pw1, r_b_conv, r_b_b_conv, r_b_ln_g, r_b_ln_b,
             r_pw2, r_b_b_pw2, r_ffn_norm, r_gate, r_up, r_down, r_final]
    outs = [loss, grad_x.reshape(x.shape)]
    for field in range(4):
        outs += [r[field] for r in order]
    return tuple(outs)
```

```python
import functools

import jax
import jax.numpy as jnp
from jax import lax
from jax.experimental import pallas as pl
from jax.experimental.pallas import tpu as pltpu

RMS_EPS = 1e-6
LN_EPS = 1e-5
ADAM_LR = 0.001
ADAM_B1 = 0.9
ADAM_B2 = 0.999
ADAM_EPS = 1e-08
ADAM_WD = 0.01
ADAM_STEP = 10

N_CHIPS = 4
N_DEV = 8
LANES = 128
SUBLANES = 8
HALO = 32
CONV_ROWS = 64
TOKEN_TILE = 512
WIDE_TOKEN_TILE = 1024
GRAD_TOKEN_TILE = 2048
FFN_ROW_CHUNKS = 2
ROW_TILE = 256
VMEM_LIMIT = 56 * 1024 * 1024
MESH = pl.DeviceIdType.MESH
BF16 = jnp.bfloat16
F32 = jnp.float32


def _tile(n, pref, mult=SUBLANES):
    t = min(n, pref) // mult * mult
    while n % t:
        t -= mult
    return t


def _params(sem):
    return pltpu.CompilerParams(dimension_semantics=sem, vmem_limit_bytes=VMEM_LIMIT)


def _sigmoid(x):
    return 0.5 * jnp.tanh(0.5 * x) + 0.5


class _Exchange:
    def __init__(self, ins, outs, aliases, n_sems, copies):
        self.ins, self.outs, self.aliases, self.n_sems, self.copies = list(ins), list(outs), dict(aliases), n_sems, copies

    def start(self, xi, xo, ssem, rsem):
        for cp in self.copies(xi, xo, ssem, rsem)[0]:
            cp.start()

    def finish(self, xi, xo, ssem, rsem):
        sends, recvs = self.copies(xi, xo, ssem, rsem)
        for cp in recvs:
            cp.wait_recv()
        for cp in sends:
            cp.wait_send()


def _call(body, name, grid, in_specs, out_specs, out_shape, args, sem, scratch_shapes=(), hosted=(), prefetch=(),
          own_aliases=None):
    in_specs, out_specs, out_shape = list(in_specs), list(out_specs), list(out_shape)
    scratch_shapes, hosted, prefetch = list(scratch_shapes), list(hosted), list(prefetch)
    n_pre, n_in, n_out, n_scr = len(prefetch), len(args), len(out_shape), len(scratch_shapes)
    x_in = [a for ex in hosted for a in ex.ins]
    x_out = [o for ex in hosted for o in ex.outs]
    aliases = {n_pre + i: o for i, o in (own_aliases or {}).items()}
    at_in, at_out = n_pre + n_in, n_out
    for ex in hosted:
        for i, o in ex.aliases.items():
            aliases[at_in + i] = at_out + o
        at_in += len(ex.ins)
        at_out += len(ex.outs)
    sems = [pltpu.SemaphoreType.DMA((ex.n_sems,)) for ex in hosted for _ in range(2)]

    def wrapped(*refs):
        pre, refs = refs[:n_pre], refs[n_pre:]
        ins, xi = refs[:n_in], refs[n_in:n_in + len(x_in)]
        refs = refs[n_in + len(x_in):]
        outs, xo = refs[:n_out], refs[n_out:n_out + len(x_out)]
        refs = refs[n_out + len(x_out):]
        scr, sm = refs[:n_scr], refs[n_scr:]
        views, a, b = [], 0, 0
        for e, ex in enumerate(hosted):
            views.append((xi[a:a + len(ex.ins)], xo[b:b + len(ex.outs)], sm[2 * e], sm[2 * e + 1]))
            a += len(ex.ins)
            b += len(ex.outs)
        first = last = None
        for ax, g in enumerate(grid):
            f, l = pl.program_id(ax) == 0, pl.program_id(ax) == g - 1
            first, last = (f, l) if first is None else (first & f, last & l)

        def begin():
            for ex, v in zip(hosted, views):
                ex.start(*v)

        def end():
            for ex, v in zip(hosted, views):
                ex.finish(*v)

        if hosted and grid:
            pl.when(first)(begin)
        elif hosted:
            begin()
        body(*pre, *ins, *outs, *scr)
        if hosted and grid:
            pl.when(last)(end)
        elif hosted:
            end()

    hbm = pl.BlockSpec(memory_space=pl.ANY)
    all_in, all_out = in_specs + [hbm] * len(x_in), out_specs + [hbm] * len(x_out)
    kw = dict(name=name, out_shape=out_shape + x_out, input_output_aliases=aliases,
              compiler_params=_params(tuple("arbitrary" for _ in grid) if hosted else sem))
    if prefetch:
        kw["grid_spec"] = pltpu.PrefetchScalarGridSpec(num_scalar_prefetch=n_pre, grid=grid, in_specs=all_in,
                                                       out_specs=all_out, scratch_shapes=scratch_shapes + sems)
    else:
        kw.update(grid=grid, in_specs=all_in, out_specs=all_out, scratch_shapes=scratch_shapes + sems)
    res = pl.pallas_call(wrapped, **kw)(*prefetch, *args, *x_in)
    return list(res[:n_out]), list(res[n_out:])


def rms_fwd(h, gain, name, hosted=()):
    T, D = h.shape
    tm = _tile(T, TOKEN_TILE)

    def body(h_ref, g_ref, o_ref):
        x = h_ref[...]
        r = lax.rsqrt(jnp.mean(x * x, axis=-1, keepdims=True) + RMS_EPS)
        o_ref[...] = (x * r * g_ref[...]).astype(o_ref.dtype)

    (n,), xo = _call(
        body, name, (T // tm,),
        [pl.BlockSpec((tm, D), lambda i: (i, 0)), pl.BlockSpec((1, D), lambda i: (0, 0))],
        [pl.BlockSpec((tm, D), lambda i: (i, 0))], [jax.ShapeDtypeStruct((T, D), BF16)],
        [h, gain], ("parallel",), hosted=hosted)
    return n, xo


def loss_head(h, gain, tgt, name):
    T, D = h.shape
    tm = _tile(T, TOKEN_TILE)

    def body(h_ref, g_ref, t_ref, loss_ref, dh_ref, dg_ref):
        i = pl.program_id(0)
        x = h_ref[...]
        g = g_ref[...]
        r = lax.rsqrt(jnp.mean(x * x, axis=-1, keepdims=True) + RMS_EPS)
        xhat = x * r
        diff = xhat * g - t_ref[...]
        part_loss = 0.5 * jnp.sum(jnp.mean(diff * diff, axis=-1, keepdims=True), axis=0, keepdims=True)
        dy = diff * (1.0 / D)
        dxhat = dy * g
        dh_ref[...] = r * (dxhat - xhat * jnp.mean(dxhat * xhat, axis=-1, keepdims=True))
        part = jnp.sum(dy * xhat, axis=0, keepdims=True)

        @pl.when(i == 0)
        def _():
            dg_ref[...] = part
            loss_ref[...] = part_loss

        @pl.when(i > 0)
        def _():
            dg_ref[...] += part
            loss_ref[...] += part_loss

    row = pl.BlockSpec((tm, D), lambda i: (i, 0))
    vec = pl.BlockSpec((1, D), lambda i: (0, 0))
    return pl.pallas_call(
        body, name=name, grid=(T // tm,),
        in_specs=[row, vec, row],
        out_specs=[pl.BlockSpec((1, 1), lambda i: (0, 0)), row, vec],
        out_shape=[jax.ShapeDtypeStruct((1, 1), F32), jax.ShapeDtypeStruct((T, D), F32),
                   jax.ShapeDtypeStruct((1, D), F32)],
        compiler_params=_params(("arbitrary",)),
    )(h, gain, tgt)


def _prev_halo_spec(tm, width):
    return pl.BlockSpec((HALO, width), lambda i: (jnp.maximum(i * (tm // HALO) - 1, 0), 0))


def _next_halo_spec(tm, width, T):
    return pl.BlockSpec((HALO, width), lambda i: (jnp.minimum((i + 1) * (tm // HALO), T // HALO - 1), 0))


def _shifted(win, off, rows):
    if off % SUBLANES == 0:
        return win[off:off + rows]
    n = win.shape[0]
    return pltpu.roll(win, (n - off) % n, 0)[:rows]


def _rowsum8(x):
    acc = x[0:SUBLANES]
    for q in range(1, x.shape[0] // SUBLANES):
        acc = acc + x[q * SUBLANES:(q + 1) * SUBLANES]
    return acc


def _conv_loops(tm, D, per_block):
    def chunk(r, carry):
        t0 = pl.multiple_of(r * CONV_ROWS, CONV_ROWS)
        for lb in range(D // LANES):
            per_block(t0, slice(lb * LANES, (lb + 1) * LANES))
        return carry

    lax.fori_loop(0, tm // CONV_ROWS, chunk, 0)


def gateconv_fwd(bcv, w, name, hosted=()):
    T, D3 = bcv.shape
    D = D3 // 3
    K = w.shape[0]
    tm = _tile(T, TOKEN_TILE)

    def body(x_ref, halo_ref, w_ref, y_ref, pad_ref):
        i = pl.program_id(0)
        pad_ref[HALO:, :] = x_ref[:, D:2 * D] * x_ref[:, 2 * D:]
        pad_ref[:HALO, :] = jnp.where(i > 0, halo_ref[:, D:2 * D] * halo_ref[:, 2 * D:], 0.0)

        def block(t0, ls):
            win = pad_ref[pl.ds(t0, CONV_ROWS + HALO), ls]
            acc = jnp.zeros((CONV_ROWS, LANES), F32)
            for k in range(K):
                acc = acc + w_ref[k:k + 1, ls] * _shifted(win, HALO - (K - 1) + k, CONV_ROWS)
            y_ref[pl.ds(t0, CONV_ROWS), ls] = (x_ref[pl.ds(t0, CONV_ROWS), ls] * acc).astype(y_ref.dtype)

        _conv_loops(tm, D, block)

    (y,), xo = _call(
        body, name, (T // tm,),
        [pl.BlockSpec((tm, D3), lambda i: (i, 0)), _prev_halo_spec(tm, D3), pl.BlockSpec((K, D), lambda i: (0, 0))],
        [pl.BlockSpec((tm, D), lambda i: (i, 0))], [jax.ShapeDtypeStruct((T, D), BF16)],
        [bcv, bcv, w], ("parallel",), [pltpu.VMEM((tm + HALO, D), F32)], hosted=hosted)
    return y, xo


def gateconv_bwd(dy, bcv, w, name, hosted=()):
    T, D3 = bcv.shape
    D = D3 // 3
    K = w.shape[0]
    tm = _tile(T, TOKEN_TILE)
    nt = T // tm

    def body(dy_ref, dyn_ref, x_ref, xp_ref, xn_ref, w_ref, o_ref, dw_ref, cv_ref, dc_ref, wacc_ref):
        i = pl.program_id(0)
        cv_ref[HALO:, :] = x_ref[:, D:2 * D] * x_ref[:, 2 * D:]
        cv_ref[:HALO, :] = jnp.where(i > 0, xp_ref[:, D:2 * D] * xp_ref[:, 2 * D:], 0.0)
        dc_ref[:tm, :] = dy_ref[...] * x_ref[:, :D]
        dc_ref[tm:, :] = jnp.where(i < nt - 1, dyn_ref[...] * xn_ref[:, :D], 0.0)

        @pl.when(i == 0)
        def _():
            wacc_ref[...] = jnp.zeros_like(wacc_ref)

        def block(t0, ls):
            cwin = cv_ref[pl.ds(t0, CONV_ROWS + HALO), ls]
            dwin = dc_ref[pl.ds(t0, CONV_ROWS + HALO), ls]
            dcon = dwin[:CONV_ROWS]
            conv = jnp.zeros((CONV_ROWS, LANES), F32)
            dcv = jnp.zeros((CONV_ROWS, LANES), F32)
            for k in range(K):
                wk = w_ref[k:k + 1, ls]
                cs = _shifted(cwin, HALO - (K - 1) + k, CONV_ROWS)
                conv = conv + wk * cs
                dcv = dcv + wk * _shifted(dwin, (K - 1) - k, CONV_ROWS)
                wacc_ref[k * SUBLANES:(k + 1) * SUBLANES, ls] += _rowsum8(dcon * cs)
            rows = pl.ds(t0, CONV_ROWS)
            o_ref[rows, ls] = (dy_ref[rows, ls] * conv).astype(o_ref.dtype)
            o_ref[rows, pl.ds(D + ls.start, LANES)] = (dcv * x_ref[rows, pl.ds(2 * D + ls.start, LANES)]).astype(o_ref.dtype)
            o_ref[rows, pl.ds(2 * D + ls.start, LANES)] = (dcv * x_ref[rows, pl.ds(D + ls.start, LANES)]).astype(o_ref.dtype)

        _conv_loops(tm, D, block)

        @pl.when(i == nt - 1)
        def _():
            for k in range(K):
                dw_ref[k:k + 1, :] = jnp.sum(wacc_ref[k * SUBLANES:(k + 1) * SUBLANES, :], axis=0, keepdims=True)

    (dx, dw), xo = _call(
        body, name, (nt,),
        [pl.BlockSpec((tm, D), lambda i: (i, 0)), _next_halo_spec(tm, D, T),
         pl.BlockSpec((tm, D3), lambda i: (i, 0)), _prev_halo_spec(tm, D3), _next_halo_spec(tm, D3, T),
         pl.BlockSpec((K, D), lambda i: (0, 0))],
        [pl.BlockSpec((tm, D3), lambda i: (i, 0)), pl.BlockSpec((K, D), lambda i: (0, 0))],
        [jax.ShapeDtypeStruct((T, D3), BF16), jax.ShapeDtypeStruct((K, D), F32)],
        [dy, dy, bcv, bcv, bcv, w], ("arbitrary",),
        [pltpu.VMEM((tm + HALO, D), F32), pltpu.VMEM((tm + HALO, D), F32), pltpu.VMEM((K * SUBLANES, D), F32)],
        hosted=hosted)
    return dx, dw, xo


def bconv_fwd(u, w, b_conv, ln_g, ln_b, name, hosted=()):
    T, D2 = u.shape
    D = D2 // 2
    K = w.shape[0]
    tm = _tile(T, TOKEN_TILE)

    def body(u_ref, halo_ref, w_ref, bc_ref, g_ref, b_ref, cu_ref, s_ref, pad_ref):
        i = pl.program_id(0)
        pad_ref[HALO:, :] = u_ref[:, :D] * _sigmoid(u_ref[:, D:])
        pad_ref[:HALO, :] = jnp.where(i > 0, halo_ref[:, :D] * _sigmoid(halo_ref[:, D:]), 0.0)

        def block(t0, ls):
            win = pad_ref[pl.ds(t0, CONV_ROWS + HALO), ls]
            acc = jnp.zeros((CONV_ROWS, LANES), F32)
            for k in range(K):
                acc = acc + w_ref[k:k + 1, ls] * _shifted(win, HALO - (K - 1) + k, CONV_ROWS)
            cu_ref[pl.ds(t0, CONV_ROWS), ls] = acc + bc_ref[:, ls]

        _conv_loops(tm, D, block)
        cu = cu_ref[...]
        mu = jnp.mean(cu, axis=-1, keepdims=True)
        xc = cu - mu
        rstd = lax.rsqrt(jnp.mean(xc * xc, axis=-1, keepdims=True) + LN_EPS)
        ln = xc * rstd * g_ref[...] + b_ref[...]
        s_ref[...] = (ln * _sigmoid(ln)).astype(s_ref.dtype)

    vec = pl.BlockSpec((1, D), lambda i: (0, 0))
    row = pl.BlockSpec((tm, D), lambda i: (i, 0))
    (cu, s), xo = _call(
        body, name, (T // tm,),
        [pl.BlockSpec((tm, D2), lambda i: (i, 0)), _prev_halo_spec(tm, D2), pl.BlockSpec((K, D), lambda i: (0, 0)), vec, vec, vec],
        [row, row], [jax.ShapeDtypeStruct((T, D), F32), jax.ShapeDtypeStruct((T, D), BF16)],
        [u, u, w, b_conv, ln_g, ln_b], ("parallel",), [pltpu.VMEM((tm + HALO, D), F32)], hosted=hosted)
    return cu, s, xo


def ln_silu_bwd(ds, cu, ln_g, ln_b, name):
    T, D = cu.shape
    tm = _tile(T, TOKEN_TILE)

    def body(ds_ref, cu_ref, g_ref, b_ref, dcu_ref, dg_ref, db_ref, dbc_ref):
        i = pl.program_id(0)
        cu_ = cu_ref[...]
        mu = jnp.mean(cu_, axis=-1, keepdims=True)
        xc = cu_ - mu
        rstd = lax.rsqrt(jnp.mean(xc * xc, axis=-1, keepdims=True) + LN_EPS)
        xh = xc * rstd
        ln = xh * g_ref[...] + b_ref[...]
        sg = _sigmoid(ln)
        dl = ds_ref[...] * (sg * (1.0 + ln * (1.0 - sg)))
        dxh = dl * g_ref[...]
        dcu = rstd * (dxh - jnp.mean(dxh, axis=-1, keepdims=True) - xh * jnp.mean(dxh * xh, axis=-1, keepdims=True))
        dcu_ref[...] = dcu
        pg = jnp.sum(dl * xh, axis=0, keepdims=True)
        pb = jnp.sum(dl, axis=0, keepdims=True)
        pc = jnp.sum(dcu, axis=0, keepdims=True)

        @pl.when(i == 0)
        def _():
            dg_ref[...] = pg
            db_ref[...] = pb
            dbc_ref[...] = pc

        @pl.when(i > 0)
        def _():
            dg_ref[...] += pg
            db_ref[...] += pb
            dbc_ref[...] += pc

    vec = pl.BlockSpec((1, D), lambda i: (0, 0))
    row = pl.BlockSpec((tm, D), lambda i: (i, 0))
    vshape = jax.ShapeDtypeStruct((1, D), F32)
    return pl.pallas_call(
        body, name=name, grid=(T // tm,),
        in_specs=[row, row, vec, vec], out_specs=[row, vec, vec, vec],
        out_shape=[jax.ShapeDtypeStruct((T, D), F32), vshape, vshape, vshape],
        compiler_params=_params(("arbitrary",)),
    )(ds, cu, ln_g, ln_b)


def bconv_bwd(dcu, u, w, name, hosted=()):
    T, D2 = u.shape
    D = D2 // 2
    K = w.shape[0]
    tm = _tile(T, TOKEN_TILE)
    nt = T // tm

    def body(dc_ref, dcn_ref, u_ref, up_ref, w_ref, du_ref, dw_ref, db_ref, glu_ref, dpad_ref, dglu_ref, wacc_ref):
        i = pl.program_id(0)
        glu_ref[HALO:, :] = u_ref[:, :D] * _sigmoid(u_ref[:, D:])
        glu_ref[:HALO, :] = jnp.where(i > 0, up_ref[:, :D] * _sigmoid(up_ref[:, D:]), 0.0)
        dpad_ref[:tm, :] = dc_ref[...]
        dpad_ref[tm:, :] = jnp.where(i < nt - 1, dcn_ref[...], 0.0)

        @pl.when(i == 0)
        def _():
            wacc_ref[...] = jnp.zeros_like(wacc_ref)

        def block(t0, ls):
            gwin = glu_ref[pl.ds(t0, CONV_ROWS + HALO), ls]
            dwin = dpad_ref[pl.ds(t0, CONV_ROWS + HALO), ls]
            dcur = dwin[:CONV_ROWS]
            dglu = jnp.zeros((CONV_ROWS, LANES), F32)
            for k in range(K):
                dglu = dglu + w_ref[k:k + 1, ls] * _shifted(dwin, (K - 1) - k, CONV_ROWS)
                gs = _shifted(gwin, HALO - (K - 1) + k, CONV_ROWS)
                wacc_ref[k * SUBLANES:(k + 1) * SUBLANES, ls] += _rowsum8(dcur * gs)
            dglu_ref[pl.ds(t0, CONV_ROWS), ls] = dglu

        _conv_loops(tm, D, block)
        dglu = dglu_ref[...]
        a = u_ref[:, :D]
        sg = _sigmoid(u_ref[:, D:])
        da = dglu * sg
        dg = dglu * a * (sg * (1.0 - sg))
        du_ref[:, :D] = da.astype(du_ref.dtype)
        du_ref[:, D:] = dg.astype(du_ref.dtype)
        pa = jnp.sum(da, axis=0, keepdims=True)
        pg = jnp.sum(dg, axis=0, keepdims=True)

        @pl.when(i == 0)
        def _():
            db_ref[:, :D] = pa
            db_ref[:, D:] = pg

        @pl.when(i > 0)
        def _():
            db_ref[:, :D] += pa
            db_ref[:, D:] += pg

        @pl.when(i == nt - 1)
        def _():
            for k in range(K):
                dw_ref[k:k + 1, :] = jnp.sum(wacc_ref[k * SUBLANES:(k + 1) * SUBLANES, :], axis=0, keepdims=True)

    (du, dw, db), xo = _call(
        body, name, (nt,),
        [pl.BlockSpec((tm, D), lambda i: (i, 0)), _next_halo_spec(tm, D, T),
         pl.BlockSpec((tm, D2), lambda i: (i, 0)), _prev_halo_spec(tm, D2), pl.BlockSpec((K, D), lambda i: (0, 0))],
        [pl.BlockSpec((tm, D2), lambda i: (i, 0)), pl.BlockSpec((K, D), lambda i: (0, 0)), pl.BlockSpec((1, D2), lambda i: (0, 0))],
        [jax.ShapeDtypeStruct((T, D2), BF16), jax.ShapeDtypeStruct((K, D), F32), jax.ShapeDtypeStruct((1, D2), F32)],
        [dcu, dcu, u, u, w], ("arbitrary",),
        [pltpu.VMEM((tm + HALO, D), F32), pltpu.VMEM((tm + HALO, D), F32), pltpu.VMEM((tm, D), F32),
         pltpu.VMEM((K * SUBLANES, D), F32)], hosted=hosted)
    return du, dw, db, xo


def mm_cols(a, w, bias, name, hosted=()):
    T, K = a.shape
    S, _, n = w.shape
    tm = _tile(T, WIDE_TOKEN_TILE)

    def body(*refs):
        a_ref, w_ref = refs[:2]
        o_ref = refs[-1]
        acc = jnp.dot(a_ref[...], w_ref[...], preferred_element_type=F32)
        if bias is not None:
            acc = acc + refs[2][...]
        o_ref[...] = acc

    in_specs = [pl.BlockSpec((tm, K), lambda s, i: (i, 0)), pl.BlockSpec((None, K, n), lambda s, i: (s, 0, 0))]
    args = [a, w]
    if bias is not None:
        in_specs.append(pl.BlockSpec((1, n), lambda s, i: (0, s)))
        args.append(bias)
    (out,), xo = _call(body, name, (S, T // tm), in_specs, [pl.BlockSpec((tm, n), lambda s, i: (i, s))],
                       [jax.ShapeDtypeStruct((T, S * n), F32)], args, ("parallel", "parallel"), hosted=hosted)
    return out, xo


def _load_once(pairs, sems):
    cps = [pltpu.make_async_copy(src, dst, sems.at[k]) for k, (src, dst) in enumerate(pairs)]
    for cp in cps:
        cp.start()
    for cp in cps:
        cp.wait()


def ffn_fwd(h, gain, wg, wu, wd, name, hosted=()):
    T, D = h.shape
    S, f, _ = wg.shape
    tm = _tile(T, TOKEN_TILE)
    rc = tm // FFN_ROW_CHUNKS
    chunks = [slice(r * rc, (r + 1) * rc) for r in range(FFN_ROW_CHUNKS)]

    def body(h_ref, gain_ref, wg_hbm, wu_hbm, wd_hbm, n_ref, g_ref, u_ref, gu_ref, o_ref, wg_v, wu_v, wd_v, sems):
        i, s = pl.program_id(0), pl.program_id(1)

        @pl.when((i == 0) & (s == 0))
        def _():
            _load_once([(wg_hbm, wg_v), (wu_hbm, wu_v), (wd_hbm, wd_v)], sems)

        @pl.when(s == 0)
        def _():
            x = h_ref[...]
            r = lax.rsqrt(jnp.mean(x * x, axis=-1, keepdims=True) + RMS_EPS)
            n_ref[...] = (x * r * gain_ref[...]).astype(n_ref.dtype)

        parts = []
        for rows in chunks:
            a = n_ref[rows, :]
            g = lax.dot_general(a, wg_v[s], _NT, preferred_element_type=F32)
            u = lax.dot_general(a, wu_v[s], _NT, preferred_element_type=F32)
            gu = (g * _sigmoid(g) * u).astype(gu_ref.dtype)
            g_ref[rows, :] = g.astype(g_ref.dtype)
            u_ref[rows, :] = u.astype(u_ref.dtype)
            gu_ref[rows, :] = gu
            parts.append(jnp.dot(gu, wd_v[s], preferred_element_type=F32))

        @pl.when(s == 0)
        def _():
            for rows, part in zip(chunks, parts):
                o_ref[rows, :] = h_ref[rows, :] + part

        @pl.when(s > 0)
        def _():
            for rows, part in zip(chunks, parts):
                o_ref[rows, :] += part

    row = pl.BlockSpec((tm, D), lambda i, s: (i, 0))
    seg = pl.BlockSpec((None, tm, f), lambda i, s: (s, i, 0))
    hbm = pl.BlockSpec(memory_space=pl.ANY)
    segs = jax.ShapeDtypeStruct((S, T, f), BF16)
    outs, xo = _call(
        body, name, (T // tm, S),
        [row, pl.BlockSpec((1, D), lambda i, s: (0, 0)), hbm, hbm, hbm], [row, seg, seg, seg, row],
        [jax.ShapeDtypeStruct((T, D), BF16), segs, segs, segs, jax.ShapeDtypeStruct((T, D), F32)],
        [h, gain, wg, wu, wd], ("arbitrary", "arbitrary"),
        [pltpu.VMEM((S, f, D), BF16), pltpu.VMEM((S, f, D), BF16), pltpu.VMEM((S, f, D), BF16), pltpu.SemaphoreType.DMA((3,))],
        hosted=hosted)
    return (*outs, xo)


def ffn_bwd(dy, h, gain, g, u, wd, wg, wu, name, hosted=()):
    T, D = h.shape
    S, f, _ = wg.shape
    tm = _tile(T, TOKEN_TILE)
    nt = T // tm
    rc = tm // FFN_ROW_CHUNKS
    chunks = [slice(r * rc, (r + 1) * rc) for r in range(FFN_ROW_CHUNKS)]

    def body(dy_ref, h_ref, gain_ref, g_ref, u_ref, wd_hbm, wg_hbm, wu_hbm, dg_ref, du_ref, dh_ref, dgain_ref,
             wd_v, wg_v, wu_v, dyb_ref, sems):
        i, s = pl.program_id(0), pl.program_id(1)

        @pl.when((i == 0) & (s == 0))
        def _():
            _load_once([(wd_hbm, wd_v), (wg_hbm, wg_v), (wu_hbm, wu_v)], sems)

        @pl.when(s == 0)
        def _():
            dyb_ref[...] = dy_ref[...].astype(dyb_ref.dtype)

        parts = []
        for rows in chunks:
            dgu = lax.dot_general(dyb_ref[rows, :], wd_v[s], _NT, preferred_element_type=F32)
            gv = g_ref[rows, :].astype(F32)
            sg = _sigmoid(gv)
            dg = (dgu * u_ref[rows, :].astype(F32) * (sg * (1.0 + gv * (1.0 - sg)))).astype(dg_ref.dtype)
            du = (dgu * (gv * sg)).astype(du_ref.dtype)
            dg_ref[rows, :] = dg
            du_ref[rows, :] = du
            parts.append(jnp.dot(dg, wg_v[s], preferred_element_type=F32)
                         + jnp.dot(du, wu_v[s], preferred_element_type=F32))

        @pl.when(s == 0)
        def _():
            for rows, part in zip(chunks, parts):
                dh_ref[rows, :] = part

        @pl.when(s > 0)
        def _():
            for rows, part in zip(chunks, parts):
                dh_ref[rows, :] += part

        @pl.when(s == S - 1)
        def _():
            dn = dh_ref[...]
            x = h_ref[...]
            r = lax.rsqrt(jnp.mean(x * x, axis=-1, keepdims=True) + RMS_EPS)
            xhat = x * r
            dxhat = dn * gain_ref[...]
            dh_ref[...] = dy_ref[...] + r * (dxhat - xhat * jnp.mean(dxhat * xhat, axis=-1, keepdims=True))
            pg = jnp.sum(dn * xhat, axis=0, keepdims=True)

            @pl.when(i == 0)
            def _():
                dgain_ref[...] = pg

            @pl.when(i > 0)
            def _():
                dgain_ref[...] += pg

    row = pl.BlockSpec((tm, D), lambda i, s: (i, 0))
    vec = pl.BlockSpec((1, D), lambda i, s: (0, 0))
    seg = pl.BlockSpec((None, tm, f), lambda i, s: (s, i, 0))
    hbm = pl.BlockSpec(memory_space=pl.ANY)
    segs = jax.ShapeDtypeStruct((S, T, f), BF16)
    outs, xo = _call(
        body, name, (nt, S),
        [row, row, vec, seg, seg, hbm, hbm, hbm], [seg, seg, row, vec],
        [segs, segs, jax.ShapeDtypeStruct((T, D), F32), jax.ShapeDtypeStruct((1, D), F32)],
        [dy, h, gain, g, u, wd, wg, wu], ("arbitrary", "arbitrary"),
        [pltpu.VMEM((S, f, D), BF16), pltpu.VMEM((S, f, D), BF16), pltpu.VMEM((S, f, D), BF16),
         pltpu.VMEM((tm, D), BF16), pltpu.SemaphoreType.DMA((3,))], hosted=hosted)
    return (*outs, xo)


def mm_rows(a, w, res, bias, name, hosted=()):
    S, T, k = a.shape
    N = w.shape[-1]
    tm = _tile(T, TOKEN_TILE)

    def body(*refs):
        a_ref, w_ref, r_ref = refs[:3]
        o_ref = refs[-1]
        s = pl.program_id(1)
        acc = jnp.dot(a_ref[...], w_ref[...], preferred_element_type=F32)

        @pl.when(s == 0)
        def _():
            base = r_ref[...]
            if bias is not None:
                base = base + refs[3][...]
            o_ref[...] = base + acc

        @pl.when(s > 0)
        def _():
            o_ref[...] += acc

    in_specs = [pl.BlockSpec((None, tm, k), lambda i, s: (s, i, 0)),
                pl.BlockSpec((None, k, N), lambda i, s: (s, 0, 0)),
                pl.BlockSpec((tm, N), lambda i, s: (i, 0))]
    args = [a, w, res]
    if bias is not None:
        in_specs.append(pl.BlockSpec((1, N), lambda i, s: (0, 0)))
        args.append(bias)
    (out,), xo = _call(body, name, (T // tm, S), in_specs, [pl.BlockSpec((tm, N), lambda i, s: (i, 0))],
                       [jax.ShapeDtypeStruct((T, N), F32)], args, ("parallel", "arbitrary"), hosted=hosted)
    return out, xo


_NT = (((1,), (1,)), ((), ()))
_TN = (((0,), (0,)), ((), ()))


def nt_rows(dy, w, want_colsum, name, hosted=()):
    T, N = dy.shape
    S, k, _ = w.shape
    tm = _tile(T, TOKEN_TILE)

    def body(dy_ref, w_ref, o_ref, *rest):
        i, s = pl.program_id(0), pl.program_id(1)
        d = dy_ref[...]
        o_ref[...] = lax.dot_general(d.astype(BF16), w_ref[...], _NT, preferred_element_type=F32)
        if want_colsum:
            cs_ref = rest[0]
            part = jnp.sum(d, axis=0, keepdims=True)

            @pl.when((i == 0) & (s == 0))
            def _():
                cs_ref[...] = part

            @pl.when((i > 0) & (s == 0))
            def _():
                cs_ref[...] += part

    out_specs = [pl.BlockSpec((None, tm, k), lambda i, s: (s, i, 0))]
    out_shape = [jax.ShapeDtypeStruct((S, T, k), F32)]
    if want_colsum:
        out_specs.append(pl.BlockSpec((1, N), lambda i, s: (0, 0)))
        out_shape.append(jax.ShapeDtypeStruct((1, N), F32))
    outs, xo = _call(
        body, name, (T // tm, S),
        [pl.BlockSpec((tm, N), lambda i, s: (i, 0)), pl.BlockSpec((None, k, N), lambda i, s: (s, 0, 0))],
        out_specs, out_shape, [dy, w], ("arbitrary", "arbitrary"), hosted=hosted)
    return (*outs, xo)


def nt_cols_rms(dy, w, h, gain, dres, name, hosted=()):
    T, K = h.shape
    S, _, n = w.shape
    tm = _tile(T, TOKEN_TILE)

    def body(dy_ref, w_ref, h_ref, gain_ref, dres_ref, dh_ref, dgain_ref):
        i = pl.program_id(0)
        dn = None
        for s in range(S):
            part = lax.dot_general(dy_ref[:, s * n:(s + 1) * n], w_ref[s], _NT, preferred_element_type=F32)
            dn = part if dn is None else dn + part
        x = h_ref[...]
        r = lax.rsqrt(jnp.mean(x * x, axis=-1, keepdims=True) + RMS_EPS)
        xhat = x * r
        dxhat = dn * gain_ref[...]
        dh_ref[...] = dres_ref[...] + r * (dxhat - xhat * jnp.mean(dxhat * xhat, axis=-1, keepdims=True))
        pg = jnp.sum(dn * xhat, axis=0, keepdims=True)

        @pl.when(i == 0)
        def _():
            dgain_ref[...] = pg

        @pl.when(i > 0)
        def _():
            dgain_ref[...] += pg

    row = pl.BlockSpec((tm, K), lambda i: (i, 0))
    vec = pl.BlockSpec((1, K), lambda i: (0, 0))
    (dh, dgain), xo = _call(
        body, name, (T // tm,),
        [pl.BlockSpec((tm, S * n), lambda i: (i, 0)), pl.BlockSpec((S, K, n), lambda i: (0, 0, 0)), row, vec, row],
        [row, vec], [jax.ShapeDtypeStruct((T, K), F32), jax.ShapeDtypeStruct((1, K), F32)],
        [dy, w, h, gain, dres], ("arbitrary",), hosted=hosted)
    return dh, dgain, xo


def tn_grad(a, dy, S, a_by_seg, name, hosted=()):
    T = dy.shape[0] if dy.ndim == 2 else dy.shape[1]
    tt = _tile(T, GRAD_TOKEN_TILE)
    if a_by_seg:
        R = a.shape[1] // S if a.ndim == 2 else a.shape[2]
        C = dy.shape[1]
        a_spec = pl.BlockSpec((tt, R), lambda s, t: (t, s)) if a.ndim == 2 else pl.BlockSpec((None, tt, R), lambda s, t: (s, t, 0))
        b_spec = pl.BlockSpec((tt, C), lambda s, t: (t, 0))
    else:
        R = a.shape[1]
        C = dy.shape[1] // S if dy.ndim == 2 else dy.shape[2]
        a_spec = pl.BlockSpec((tt, R), lambda s, t: (t, 0))
        b_spec = pl.BlockSpec((tt, C), lambda s, t: (t, s)) if dy.ndim == 2 else pl.BlockSpec((None, tt, C), lambda s, t: (s, t, 0))
    Rh = R // 2
    nt = T // tt

    def body(a_ref, b_ref, o_ref, acc_ref):
        t = pl.program_id(1)
        part = lax.dot_general(a_ref[...], b_ref[...].astype(BF16), _TN, preferred_element_type=F32)

        @pl.when(t == 0)
        def _():
            acc_ref[...] = part

        @pl.when(t > 0)
        def _():
            acc_ref[...] += part

        @pl.when(t == nt - 1)
        def _():
            o_ref[0] = acc_ref[:Rh, :].astype(o_ref.dtype)
            o_ref[1] = acc_ref[Rh:, :].astype(o_ref.dtype)

    (gh,), xo = _call(
        body, name, (S, nt), [a_spec, b_spec], [pl.BlockSpec((2, None, Rh, C), lambda s, t: (0, s, 0, 0))],
        [jax.ShapeDtypeStruct((2, S, Rh, C), BF16)], [a, dy], ("parallel", "arbitrary"), [pltpu.VMEM((R, C), F32)],
        hosted=hosted)
    return gh, xo


def tn_grad_square(a, dy, S, name, hosted=()):
    T, K = a.shape
    N = dy.shape[1]
    tt = _tile(T, GRAD_TOKEN_TILE)
    nt = T // tt
    Rh = K // S // 2

    def body(a_ref, b_ref, o_ref, acc_ref):
        t = pl.program_id(0)
        part = lax.dot_general(a_ref[...], b_ref[...].astype(BF16), _TN, preferred_element_type=F32)

        @pl.when(t == 0)
        def _():
            acc_ref[...] = part

        @pl.when(t > 0)
        def _():
            acc_ref[...] += part

        @pl.when(t == nt - 1)
        def _():
            for s in range(S):
                for hf in range(2):
                    r0 = (2 * s + hf) * Rh
                    o_ref[hf, s] = acc_ref[r0:r0 + Rh, :].astype(o_ref.dtype)

    (gh,), xo = _call(
        body, name, (nt,), [pl.BlockSpec((tt, K), lambda t: (t, 0)), pl.BlockSpec((tt, N), lambda t: (t, 0))],
        [pl.BlockSpec((2, S, Rh, N), lambda t: (0, 0, 0, 0))], [jax.ShapeDtypeStruct((2, S, Rh, N), BF16)],
        [a, dy], ("arbitrary",), [pltpu.VMEM((K, N), F32)], hosted=hosted)
    return gh, xo


def _place():
    x, y, c = lax.axis_index("x"), lax.axis_index("y"), lax.axis_index("c")
    chips = [(1 - x, y), (x, 1 - y), (1 - x, 1 - y)]
    return x, y, c, chips


def _any_specs(n):
    return [pl.BlockSpec(memory_space=pl.ANY)] * n


def _remote(src, dst, send_sem, recv_sem, dev):
    return pltpu.make_async_remote_copy(src_ref=src, dst_ref=dst, send_sem=send_sem, recv_sem=recv_sem,
                                        device_id=dev, device_id_type=MESH)


def small_allreduce(v, name, hosted=()):
    rows, W = v.shape

    def body(v_ref, o_ref, sib_ref, pair_ref, chips_ref, send_sems, recv_sems):
        x, y, c, chips = _place()
        me = 2 * x + y
        swap = _remote(v_ref, sib_ref, send_sems.at[3], recv_sems.at[3], (x, y, 1 - c))
        swap.start()
        swap.wait()
        mine, other = v_ref[...], sib_ref[...]
        pair_ref[...] = jnp.where(c == 0, mine, other) + jnp.where(c == 0, other, mine)
        sends = []
        for j, (px, py) in enumerate(chips):
            cp = _remote(pair_ref, chips_ref.at[me], send_sems.at[j], recv_sems.at[j], (px, py, c))
            cp.start()
            sends.append(cp)
        chips_ref[me] = pair_ref[...]
        for j, (px, py) in enumerate(chips):
            blk = chips_ref.at[2 * px + py]
            _remote(blk, blk, send_sems.at[j], recv_sems.at[j], (px, py, c)).wait_recv()
        for cp in sends:
            cp.wait_send()
        o_ref[...] = (chips_ref[0] + chips_ref[1]) + (chips_ref[2] + chips_ref[3])

    vm = pl.BlockSpec(memory_space=pltpu.VMEM)
    (out,), xo = _call(
        body, name, (), [vm], [vm], [jax.ShapeDtypeStruct((rows, W), F32)], [v], (),
        [pltpu.VMEM((rows, W), F32), pltpu.VMEM((rows, W), F32), pltpu.VMEM((N_CHIPS, rows, W), F32),
         pltpu.SemaphoreType.DMA((4,)), pltpu.SemaphoreType.DMA((4,))], hosted=hosted)
    return out, xo


def _gather_p1_copies(srcs, bufs, ssem, rsem, base):
    x, y, c, chips = _place()
    me, sib = 2 * x + y, (x, y, 1 - c)
    sends, recvs = [], []
    for k, (src, buf) in enumerate(zip(srcs, bufs)):
        rh = src.shape[0] // 2
        s0 = base + 4 * k
        sends.append(_remote(src, buf.at[me], ssem.at[s0 + 3], rsem.at[s0 + 3], sib))
        recvs.append(_remote(buf.at[me], buf.at[me], ssem.at[s0 + 3], rsem.at[s0 + 3], sib))
        for j, (px, py) in enumerate(chips):
            sends.append(_remote(src.at[pl.ds(c * rh, rh)], buf.at[me, pl.ds(c * rh, rh)], ssem.at[s0 + j], rsem.at[s0 + j], (px, py, c)))
            blk = buf.at[2 * px + py, pl.ds(c * rh, rh)]
            recvs.append(_remote(blk, blk, ssem.at[s0 + j], rsem.at[s0 + j], (px, py, c)))
    return sends, recvs


def _gather_p2_copies(bufs, ssem, rsem, base):
    x, y, c, chips = _place()
    sib = (x, y, 1 - c)
    sends, recvs = [], []
    for k, buf in enumerate(bufs):
        rh = buf.shape[1] // 2
        for j, (px, py) in enumerate(chips):
            s0 = base + 3 * k + j
            blk = buf.at[2 * px + py, pl.ds(c * rh, rh)]
            sends.append(_remote(blk, blk, ssem.at[s0], rsem.at[s0], sib))
            got = buf.at[2 * px + py, pl.ds((1 - c) * rh, rh)]
            recvs.append(_remote(got, got, ssem.at[s0], rsem.at[s0], sib))
    return sends, recvs


def _gathered_shape(s):
    return jax.ShapeDtypeStruct((N_CHIPS,) + s.shape, s.dtype)


def gather_p1(shards):
    return _Exchange(shards, [_gathered_shape(s) for s in shards], {}, 4 * len(shards),
                     lambda xi, xo, ss, rs: _gather_p1_copies(xi, xo, ss, rs, 0))


def gather_p2(bufs):
    return _Exchange(bufs, [jax.ShapeDtypeStruct(b.shape, b.dtype) for b in bufs], {k: k for k in range(len(bufs))},
                     3 * len(bufs), lambda xi, xo, ss, rs: _gather_p2_copies(xo, ss, rs, 0))


def gather_first(whole, begun, name):
    nw, n = len(whole), len(whole) + len(begun)

    def body(*refs):
        ins, outs = refs[:n], refs[n:2 * n]
        ssem, rsem = refs[2 * n:]
        s1, r1 = _gather_p1_copies(ins, outs, ssem, rsem, 0)
        for cp in s1:
            cp.start()
        for cp in r1[:4 * nw]:
            cp.wait_recv()
        s2, r2 = _gather_p2_copies(outs[:nw], ssem, rsem, 4 * n)
        for cp in s2:
            cp.start()
        for cp in r1[4 * nw:] + r2:
            cp.wait_recv()
        for cp in s1 + s2:
            cp.wait_send()

    shards = list(whole) + list(begun)
    return pl.pallas_call(
        body, name=name, in_specs=_any_specs(n), out_specs=_any_specs(n),
        out_shape=[_gathered_shape(s) for s in shards],
        scratch_shapes=[pltpu.SemaphoreType.DMA((4 * n + 3 * nw,)), pltpu.SemaphoreType.DMA((4 * n + 3 * nw,))],
    )(*shards)


def gather_small(v):
    def copies(xi, xo, ssem, rsem):
        x, y, c, chips = _place()
        me, sib = 2 * x + y, (x, y, 1 - c)
        sends = [_remote(xi[0], xo[0].at[me], ssem.at[3], rsem.at[3], sib)]
        recvs = [_remote(xo[0].at[me], xo[0].at[me], ssem.at[3], rsem.at[3], sib)]
        for j, (px, py) in enumerate(chips):
            sends.append(_remote(xi[0], xo[0].at[me], ssem.at[j], rsem.at[j], (px, py, c)))
            blk = xo[0].at[2 * px + py]
            recvs.append(_remote(blk, blk, ssem.at[j], rsem.at[j], (px, py, c)))
        return sends, recvs

    return _Exchange([v], [_gathered_shape(v)], {}, 4, copies)


def run_exchanges(exchanges, name):
    return _call(lambda: None, name, (), [], [], [], [], (), hosted=exchanges)[1]


def sibling_halves(grads):
    def copies(xi, xo, ssem, rsem):
        x, y, c, _ = _place()
        sends = [_remote(xi[k].at[1 - c], xo[k], ssem.at[k], rsem.at[k], (x, y, 1 - c)) for k in range(len(grads))]
        return sends, sends

    return _Exchange(grads, [jax.ShapeDtypeStruct(g.shape[1:], g.dtype) for g in grads], {}, len(grads), copies)


def pair_sum(gh, recv, cidx, name):
    _, S, Rh, C = gh.shape

    def body(c_ref, a_ref, b_ref, o_ref):
        o_ref[...] = (a_ref[...].astype(F32) + b_ref[...].astype(F32)).astype(o_ref.dtype)

    return pl.pallas_call(
        body, name=name, out_shape=jax.ShapeDtypeStruct((S, Rh, C), BF16),
        grid_spec=pltpu.PrefetchScalarGridSpec(
            num_scalar_prefetch=1, grid=(S,),
            in_specs=[pl.BlockSpec((None, None, Rh, C), lambda s, c_ref: (c_ref[0], s, 0, 0)),
                      pl.BlockSpec((None, Rh, C), lambda s, c_ref: (s, 0, 0))],
            out_specs=pl.BlockSpec((None, Rh, C), lambda s, c_ref: (s, 0, 0))),
        compiler_params=_params(("parallel",)),
    )(cidx, gh, recv)


def scatter_p1(parts):
    def copies(xi, xo, ssem, rsem):
        x, y, c, chips = _place()
        me, sib = 2 * x + y, (x, y, 1 - c)
        sends, recvs = [], []
        for k in range(len(parts)):
            s0 = 4 * k
            sends.append(_remote(xi[k].at[me], xo[k].at[me, c], ssem.at[s0 + 3], rsem.at[s0 + 3], sib))
            own = xo[k].at[me, 1 - c]
            recvs.append(_remote(own, own, ssem.at[s0 + 3], rsem.at[s0 + 3], sib))
            for j, (px, py) in enumerate(chips):
                sends.append(_remote(xi[k].at[2 * px + py], xo[k].at[me, c], ssem.at[s0 + j], rsem.at[s0 + j], (px, py, c)))
                blk = xo[k].at[2 * px + py, c]
                recvs.append(_remote(blk, blk, ssem.at[s0 + j], rsem.at[s0 + j], (px, py, c)))
        return sends, recvs

    return _Exchange(parts, [jax.ShapeDtypeStruct((p.shape[0], 2) + p.shape[1:], p.dtype) for p in parts], {},
                     4 * len(parts), copies)


def scatter_p2(bufs):
    def copies(xi, xo, ssem, rsem):
        x, y, c, chips = _place()
        sib = (x, y, 1 - c)
        sends, recvs = [], []
        for k in range(len(bufs)):
            for j, (px, py) in enumerate(chips):
                s0 = 3 * k + j
                blk = xo[k].at[2 * px + py, c]
                sends.append(_remote(blk, blk, ssem.at[s0], rsem.at[s0], sib))
                got = xo[k].at[2 * px + py, 1 - c]
                recvs.append(_remote(got, got, ssem.at[s0], rsem.at[s0], sib))
        return sends, recvs

    return _Exchange(bufs, [jax.ShapeDtypeStruct(b.shape, b.dtype) for b in bufs], {k: k for k in range(len(bufs))},
                     3 * len(bufs), copies)


def _adamw_math(w, g, m, v):
    m = ADAM_B1 * m + (1.0 - ADAM_B1) * g
    v = ADAM_B2 * v + (1.0 - ADAM_B2) * (g * g)
    m_hat = m / (1.0 - ADAM_B1 ** ADAM_STEP)
    v_hat = v / (1.0 - ADAM_B2 ** ADAM_STEP)
    delta = -ADAM_LR * (m_hat / (jnp.sqrt(v_hat) + ADAM_EPS) + ADAM_WD * w)
    return delta, m, v


def adamw_reduce(w, m, v, buf, part, place, lyr, bases, name, hosted=()):
    L, R, C = w.shape
    Rh = R // 2
    rb = _tile(Rh, ROW_TILE, 2 * SUBLANES)
    nb = Rh // rb

    def body(place_ref, p_ref, b0, b1, b2, b3, w_ref, m_ref, v_ref, *rest):
        go_ref, d_ref, mo_ref, vo_ref = rest[-4:]
        mine = (place_ref[1] == pl.program_id(0))
        g = None
        for p, b in enumerate((b0, b1, b2, b3)):
            val = jnp.where(mine & (place_ref[0] == p), p_ref[...], b[...]).astype(F32)
            g = val if g is None else g + val
        d, mn, vn = _adamw_math(w_ref[...], g, m_ref[...], v_ref[...])
        go_ref[...] = g
        d_ref[...] = d
        mo_ref[...] = mn
        vo_ref[...] = vn

    def buf_spec(p):
        def idx(h, i, pr):
            own = (pr[0] == p) & (pr[1] == h)
            return (p, jnp.where(own, 1 - h, h), i, 0)
        return pl.BlockSpec((None, None, rb, C), idx)

    blk = pl.BlockSpec((None, rb, C), lambda h, i, pr: (lyr, h * nb + i, 0))
    in_specs = [pl.BlockSpec((None, rb, C), lambda h, i, pr: (pr[0], i, 0))] + [buf_spec(p) for p in range(N_CHIPS)] + [blk] * 3
    args = [part, buf, buf, buf, buf, w, m, v]
    aliases = {}
    if bases is not None:
        in_specs += [pl.BlockSpec(memory_space=pl.ANY)] * 4
        aliases = {len(args) + k: k for k in range(4)}
        args += list(bases)
    shp = jax.ShapeDtypeStruct((L, R, C), F32)
    return _call(body, name, (2, nb), in_specs, [blk] * 4, [shp] * 4, args, ("parallel", "parallel"),
                 hosted=hosted, prefetch=[place], own_aliases=aliases)


def small_update(gall, chip, entries, name):
    ne = len(entries)
    D = gall.shape[1]

    def body(chip_ref, gall_ref, *refs):
        ins, outs = refs[:3 * ne], refs[3 * ne:]
        ch = chip_ref[0]
        for e, (row0, kind, w, _, _) in enumerate(entries):
            r, width = w.shape

            def gsum(rs, cs):
                return gall_ref[rs, cs]

            if kind == "full":
                g = gsum(slice(row0, row0 + r), slice(0, D))
            elif kind == "cols":
                g = gsum(slice(row0, row0 + r), slice(0, width))
                for q in range(1, N_CHIPS):
                    g = jnp.where(ch == q, gsum(slice(row0, row0 + r), slice(q * width, (q + 1) * width)), g)
            else:
                per_row = D // width
                g = gsum(slice(row0, row0 + 1), slice(0, width))
                for q in range(1, N_CHIPS):
                    rr = row0 + q // per_row
                    cc = (q % per_row) * width
                    g = jnp.where(ch == q, gsum(slice(rr, rr + 1), slice(cc, cc + width)), g)
            d, mn, vn = _adamw_math(ins[3 * e][...], g, ins[3 * e + 1][...], ins[3 * e + 2][...])
            outs[4 * e][...] = g
            outs[4 * e + 1][...] = d
            outs[4 * e + 2][...] = mn
            outs[4 * e + 3][...] = vn

    vm = pl.BlockSpec(memory_space=pltpu.VMEM)
    args, out_shape = [], []
    for _, _, w, m, v in entries:
        args += [w, m, v]
        out_shape += [jax.ShapeDtypeStruct(w.shape, F32)] * 4
    return pl.pallas_call(
        body, name=name,
        in_specs=[pl.BlockSpec(memory_space=pltpu.SMEM), vm] + [vm] * (3 * ne),
        out_specs=[vm] * (4 * ne), out_shape=out_shape,
        compiler_params=pltpu.CompilerParams(vmem_limit_bytes=VMEM_LIMIT),
    )(chip, gall, *args)


def _pack_rows(items, width):
    rows, starts, at = [], [], 0
    for it in items:
        r = it.shape[0]
        pad = (-r) % SUBLANES
        starts.append(at)
        rows.append(it)
        if pad:
            rows.append(jnp.zeros((pad, width), F32))
        at += r + pad
    return jnp.concatenate(rows, axis=0), starts


def kernel(x, a_norm, a_w_in, a_conv, a_w_out, b_norm, b_w_pw1, b_b_pw1, b_conv, b_b_conv, b_ln_g, b_ln_b, b_w_pw2, b_b_pw2, ffn_norm, ffn_w_gate, ffn_w_up, ffn_w_down, final_norm, loss_target, m_a_norm, m_a_w_in, m_a_conv, m_a_w_out, m_b_norm, m_b_w_pw1, m_b_b_pw1, m_b_conv, m_b_b_conv, m_b_ln_g, m_b_ln_b, m_b_w_pw2, m_b_b_pw2, m_ffn_norm, m_ffn_w_gate, m_ffn_w_up, m_ffn_w_down, m_final_norm, v_a_norm, v_a_w_in, v_a_conv, v_a_w_out, v_b_norm, v_b_w_pw1, v_b_b_pw1, v_b_conv, v_b_b_conv, v_b_ln_g, v_b_ln_b, v_b_w_pw2, v_b_b_pw2, v_ffn_norm, v_ffn_w_gate, v_ffn_w_up, v_ffn_w_down, v_final_norm):
    T, D = x.shape[1], x.shape[2]
    Dq = D // N_CHIPS
    cx, cy, cc = lax.axis_index("x"), lax.axis_index("y"), lax.axis_index("c")
    chip = (2 * cx + cy).astype(jnp.int32).reshape(1)
    cidx = cc.astype(jnp.int32).reshape(1)
    h0 = x.reshape(T, D)
    tgt = loss_target.reshape(T, D)

    small_shards = [a_conv[0], b_norm, b_b_pw1.reshape(2, Dq), b_conv[0], b_b_conv, b_ln_g, b_ln_b, b_b_pw2]
    packed, st = _pack_rows(small_shards, Dq)

    tr = lambda t: jnp.swapaxes(t, 1, 2)
    w_gate, m_gate, v_gate = tr(ffn_w_gate), tr(m_ffn_w_gate), tr(v_ffn_w_gate)
    w_up, m_up, v_up = tr(ffn_w_up), tr(m_ffn_w_up), tr(v_ffn_w_up)
    bf = lambda t: t.astype(BF16)
    s_in, s_out, s_pw1, s_pw2 = bf(a_w_in[0]), bf(a_w_out[0]), bf(b_w_pw1[0]), bf(b_w_pw2[0])
    s_gate, s_up, s_down = [bf(w_gate[l]) for l in (0, 1)], [bf(w_up[l]) for l in (0, 1)], [bf(ffn_w_down[l]) for l in (0, 1)]

    g_in, g_out, gate0 = gather_first([s_in, s_out], [s_gate[0]], "gather_first")
    g_out = g_out.reshape(1, D, D)
    n0, _ = rms_fwd(h0, a_norm, "rms_a")
    bcv, (up0, gate0, sw) = mm_cols(n0, g_in, None, "mm_w_in",
                                    hosted=[gather_p1([s_up[0]]), gather_p2([gate0]), gather_small(packed)])

    def whole(k, r):
        return jnp.transpose(sw[:, st[k]:st[k] + r, :], (1, 0, 2)).reshape(r, D)

    a_conv_f, b_norm_f = whole(0, 3), whole(1, 1)
    b_b_pw1_f = sw[:, st[2]:st[2] + 2, :].reshape(1, 2 * D)
    b_conv_f, b_b_conv_f, b_ln_g_f, b_ln_b_f, b_b_pw2_f = whole(3, b_conv.shape[1]), whole(4, 1), whole(5, 1), whole(6, 1), whole(7, 1)
    ya, (down0, up0) = gateconv_fwd(bcv, a_conv_f, "gateconv_fwd", hosted=[gather_p1([s_down[0]]), gather_p2([up0])])
    h1, (down0,) = mm_rows(ya[None], g_out, h0, None, "mm_w_out", hosted=[gather_p2([down0])])
    n1, fg0, fu0, gu0, h2, later = ffn_fwd(h1, ffn_norm[0:1], gate0, up0, down0, "ffn_fwd0",
                                           hosted=[gather_p1([s_pw1, s_pw2, s_gate[1], s_up[1]])])
    n2, (g_pw1, g_pw2, gate1, up1) = rms_fwd(h2, b_norm_f, "rms_b", hosted=[gather_p2(later)])
    g_pw2 = g_pw2.reshape(1, D, D)
    ub, (down1,) = mm_cols(n2, g_pw1, b_b_pw1_f, "mm_pw1", hosted=[gather_p1([s_down[1]])])
    cu, sb, (down1,) = bconv_fwd(ub, b_conv_f, b_b_conv_f, b_ln_g_f, b_ln_b_f, "bconv_fwd", hosted=[gather_p2([down1])])
    h3, _ = mm_rows(sb[None], g_pw2, h2, b_b_pw2_f, "mm_pw2")
    n3, fg1, fu1, gu1, h4, _ = ffn_fwd(h3, ffn_norm[1:2], gate1, up1, down1, "ffn_fwd1")
    loss_part, dh4, d_final = loss_head(h4, final_norm.reshape(1, D), tgt, "loss_head")

    place = jnp.concatenate([chip, cidx])

    def pair_sums(ghs, from_sib, tags):
        return [pair_sum(g, r, cidx, "pair_sum_" + t) for g, r, t in zip(ghs, from_sib, tags)]

    def upd(w, m, v, bufs, parts, tag, hosted=()):
        res, xo = None, []
        for lyr, (b, p) in enumerate(zip(bufs, parts)):
            res, xo_l = adamw_reduce(w, m, v, b, p, place, lyr, res, "adamw_%s%d" % (tag, lyr), hosted=hosted if lyr == 0 else ())
            xo += xo_l
        return res, xo

    dg1, du1, dh3, d_fn1, _ = ffn_bwd(dh4, h3, ffn_norm[1:2], fg1, fu1, down1, gate1, up1, "ffn_bwd1")
    gh_down1, _ = tn_grad(gu1, dh4, N_CHIPS, True, "tn_down1")
    gh_gate1, _ = tn_grad(dg1, n3, N_CHIPS, True, "tn_gate1")
    gh_up1, _ = tn_grad(du1, n3, N_CHIPS, True, "tn_up1")
    f1 = [gh_gate1, gh_up1, gh_down1]

    ds, d_b_pw2, sib_f1 = nt_rows(dh3, g_pw2, True, "nt_pw2", hosted=[sibling_halves(f1)])
    p_f1 = pair_sums(f1, sib_f1, ["gate1", "up1", "down1"])
    gh_pw2, _ = tn_grad_square(sb, dh3, N_CHIPS, "tn_pw2")
    dcu, d_ln_g, d_ln_b, d_b_conv = ln_silu_bwd(ds[0], cu, b_ln_g_f, b_ln_b_f, "ln_silu_bwd")
    dub, d_bconv_w, d_b_pw1, buf_f1 = bconv_bwd(dcu, ub, b_conv_f, "bconv_bwd", hosted=[scatter_p1(p_f1)])
    gh_pw1, buf_f1 = tn_grad(n2, dub, N_CHIPS, False, "tn_pw1", hosted=[scatter_p2(buf_f1)])
    b_grp = [gh_pw1, gh_pw2]
    dh2, d_b_norm, sib_b = nt_cols_rms(dub, g_pw1, h2, b_norm_f, dh3, "nt_pw1", hosted=[sibling_halves(b_grp)])
    p_b = pair_sums(b_grp, sib_b, ["pw1", "pw2"])

    dg0, du0, dh1, d_fn0, buf_b = ffn_bwd(dh2, h1, ffn_norm[0:1], fg0, fu0, down0, gate0, up0, "ffn_bwd0", hosted=[scatter_p1(p_b)])
    gh_down0, buf_b = tn_grad(gu0, dh2, N_CHIPS, True, "tn_down0", hosted=[scatter_p2(buf_b)])
    gh_gate0, _ = tn_grad(dg0, n1, N_CHIPS, True, "tn_gate0")
    gh_up0, _ = tn_grad(du0, n1, N_CHIPS, True, "tn_up0")
    f0 = [gh_gate0, gh_up0, gh_down0]

    dya, sib_f0 = nt_rows(dh1, g_out, False, "nt_w_out", hosted=[sibling_halves(f0)])
    p_f0 = pair_sums(f0, sib_f0, ["gate0", "up0", "down0"])
    gh_out, (buf_gate0,) = tn_grad_square(ya, dh1, N_CHIPS, "tn_w_out", hosted=[scatter_p1(p_f0[0:1])])
    dbcv, d_aconv_w, (buf_up0, sib_out) = gateconv_bwd(dya[0], bcv, a_conv_f, "gateconv_bwd",
                                                       hosted=[scatter_p1(p_f0[1:2]), sibling_halves([gh_out])])
    p_out = pair_sums([gh_out], [sib_out], ["out"])
    gh_in, (buf_down0, buf_out) = tn_grad(n0, dbcv, N_CHIPS, False, "tn_w_in", hosted=[scatter_p1(p_f0[2:3] + p_out)])
    grad_x, d_a_norm, (buf_gate0, buf_up0, buf_down0, buf_out, sib_in) = nt_cols_rms(
        dbcv, g_in, h0, a_norm, dh1, "nt_w_in",
        hosted=[scatter_p2([buf_gate0, buf_up0, buf_down0, buf_out]), sibling_halves([gh_in])])
    p_in = pair_sums([gh_in], [sib_in], ["in"])

    d_ffn_norm = jnp.concatenate([d_fn0, d_fn1], axis=0)
    small_grads = [d_a_norm, d_aconv_w, d_b_norm, d_b_pw1.reshape(2, D), d_bconv_w, d_b_conv, d_ln_g, d_ln_b, d_b_pw2,
                   d_ffn_norm, d_final, jnp.broadcast_to(loss_part, (1, D))]
    gpacked, gs = _pack_rows(small_grads, D)
    gall, buf_in = small_allreduce(gpacked, "allreduce_small_grads", hosted=[scatter_p1(p_in)])
    buf_in = run_exchanges([scatter_p2(buf_in)], "reduce_last")
    buf_a, p_a = [buf_in[0], buf_out], [p_in[0], p_out[0]]

    r_gate, _ = upd(w_gate, m_gate, v_gate, [buf_gate0, buf_f1[0]], [p_f0[0], p_f1[0]], "gate")
    r_up, _ = upd(w_up, m_up, v_up, [buf_up0, buf_f1[1]], [p_f0[1], p_f1[1]], "up")
    r_down, _ = upd(ffn_w_down, m_ffn_w_down, v_ffn_w_down, [buf_down0, buf_f1[2]], [p_f0[2], p_f1[2]], "down")
    r_gate, r_up = [tr(t) for t in r_gate], [tr(t) for t in r_up]
    r_pw1, _ = upd(b_w_pw1, m_b_w_pw1, v_b_w_pw1, [buf_b[0]], [p_b[0]], "pw1")
    r_pw2, _ = upd(b_w_pw2, m_b_w_pw2, v_b_w_pw2, [buf_b[1]], [p_b[1]], "pw2")
    r_in, _ = upd(a_w_in, m_a_w_in, v_a_w_in, [buf_a[0]], [p_a[0]], "w_in")
    r_out, _ = upd(a_w_out, m_a_w_out, v_a_w_out, [buf_a[1]], [p_a[1]], "w_out")
    entries = [
        (gs[0], "full", a_norm, m_a_norm, v_a_norm),
        (gs[1], "cols", a_conv[0], m_a_conv[0], v_a_conv[0]),
        (gs[2], "cols", b_norm, m_b_norm, v_b_norm),
        (gs[3], "flat2", b_b_pw1, m_b_b_pw1, v_b_b_pw1),
        (gs[4], "cols", b_conv[0], m_b_conv[0], v_b_conv[0]),
        (gs[5], "cols", b_b_conv, m_b_b_conv, v_b_b_conv),
        (gs[6], "cols", b_ln_g, m_b_ln_g, v_b_ln_g),
        (gs[7], "cols", b_ln_b, m_b_ln_b, v_b_ln_b),
        (gs[8], "cols", b_b_pw2, m_b_b_pw2, v_b_b_pw2),
        (gs[9], "full", ffn_norm, m_ffn_norm, v_ffn_norm),
        (gs[10], "full", final_norm.reshape(1, D), m_final_norm.reshape(1, D), v_final_norm.reshape(1, D)),
    ]
    so = small_update(gall, chip, entries, "small_update")
    sm = [so[4 * e:4 * e + 4] for e in range(len(entries))]

    def shaped(e, like):
        return [t.reshape(like.shape) for t in sm[e]]

    r_a_norm, r_a_conv, r_b_norm, r_b_b_pw1 = shaped(0, a_norm), shaped(1, a_conv), shaped(2, b_norm), shaped(3, b_b_pw1)
    r_b_conv, r_b_b_conv, r_b_ln_g, r_b_ln_b = shaped(4, b_conv), shaped(5, b_b_conv), shaped(6, b_ln_g), shaped(7, b_ln_b)
    r_b_b_pw2, r_ffn_norm, r_final = shaped(8, b_b_pw2), shaped(9, ffn_norm), shaped(10, final_norm)

    loss = gall[gs[11], 0]
    order =[r_a_norm, r_in, r_a_conv, r_out, r_b_norm, r_pw1, r_b_b_pw1, r_b_conv, r_b_b_conv, r_b_ln_g, r_b_ln_b,
             r_pw2, r_b_b_pw2, r_ffn_norm, r_gate, r_up, r_down, r_final]
    outs = [loss, grad_x.reshape(x.shape)]
    for field in range(4):
        outs += [r[field] for r in order]
    return tuple(outs)
```

```python
import functools

import jax
import jax.numpy as jnp
from jax import lax
from jax.experimental import pallas as pl
from jax.experimental.pallas import tpu as pltpu

RMS_EPS = 1e-6
LN_EPS = 1e-5
ADAM_LR = 0.001
ADAM_B1 = 0.9
ADAM_B2 = 0.999
ADAM_EPS = 1e-08
ADAM_WD = 0.01
ADAM_STEP = 10

N_CHIPS = 4
N_DEV = 8
LANES = 128
SUBLANES = 8
HALO = 32
CONV_ROWS = 64
TOKEN_TILE = 512
WIDE_TOKEN_TILE = 1024
GRAD_TOKEN_TILE = 2048
FFN_ROW_CHUNKS = 2
ROW_TILE = 256
VMEM_LIMIT = 56 * 1024 * 1024
MESH = pl.DeviceIdType.MESH
BF16 = jnp.bfloat16
F32 = jnp.float32


def _tile(n, pref, mult=SUBLANES):
    t = min(n, pref) // mult * mult
    while n % t:
        t -= mult
    return t


def _params(sem):
    return pltpu.CompilerParams(dimension_semantics=sem, vmem_limit_bytes=VMEM_LIMIT)


def _sigmoid(x):
    return 0.5 * jnp.tanh(0.5 * x) + 0.5


class _Exchange:
    def __init__(self, ins, outs, aliases, n_sems, copies):
        self.ins, self.outs, self.aliases, self.n_sems, self.copies = list(ins), list(outs), dict(aliases), n_sems, copies
        self.early = False

    def awaited_first(self):
        self.early = True
        return self

    def start(self, xi, xo, ssem, rsem):
        for cp in self.copies(xi, xo, ssem, rsem)[0]:
            cp.start()

    def finish(self, xi, xo, ssem, rsem):
        sends, recvs = self.copies(xi, xo, ssem, rsem)
        for cp in recvs:
            cp.wait_recv()
        for cp in sends:
            cp.wait_send()


def _call(body, name, grid, in_specs, out_specs, out_shape, args, sem, scratch_shapes=(), hosted=(), prefetch=(),
          own_aliases=None):
    in_specs, out_specs, out_shape = list(in_specs), list(out_specs), list(out_shape)
    scratch_shapes, hosted, prefetch = list(scratch_shapes), list(hosted), list(prefetch)
    n_pre, n_in, n_out, n_scr = len(prefetch), len(args), len(out_shape), len(scratch_shapes)
    x_in = [a for ex in hosted for a in ex.ins]
    x_out = [o for ex in hosted for o in ex.outs]
    aliases = {n_pre + i: o for i, o in (own_aliases or {}).items()}
    at_in, at_out = n_pre + n_in, n_out
    for ex in hosted:
        for i, o in ex.aliases.items():
            aliases[at_in + i] = at_out + o
        at_in += len(ex.ins)
        at_out += len(ex.outs)
    sems = [pltpu.SemaphoreType.DMA((ex.n_sems,)) for ex in hosted for _ in range(2)]

    def wrapped(*refs):
        pre, refs = refs[:n_pre], refs[n_pre:]
        ins, xi = refs[:n_in], refs[n_in:n_in + len(x_in)]
        refs = refs[n_in + len(x_in):]
        outs, xo = refs[:n_out], refs[n_out:n_out + len(x_out)]
        refs = refs[n_out + len(x_out):]
        scr, sm = refs[:n_scr], refs[n_scr:]
        views, a, b = [], 0, 0
        for e, ex in enumerate(hosted):
            views.append((xi[a:a + len(ex.ins)], xo[b:b + len(ex.outs)], sm[2 * e], sm[2 * e + 1]))
            a += len(ex.ins)
            b += len(ex.outs)
        first = last = None
        for ax, g in enumerate(grid):
            f, l = pl.program_id(ax) == 0, pl.program_id(ax) == g - 1
            first, last = (f, l) if first is None else (first & f, last & l)

        def begin():
            for ex, v in zip(hosted, views):
                ex.start(*v)
            for ex, v in zip(hosted, views):
                if ex.early:
                    ex.finish(*v)

        def end():
            for ex, v in zip(hosted, views):
                if not ex.early:
                    ex.finish(*v)

        if hosted and grid:
            pl.when(first)(begin)
        elif hosted:
            begin()
        early_refs = [r for ex, v in zip(hosted, views) if ex.early for r in v[1]]
        body(*pre, *ins, *outs, *scr, *early_refs)
        if hosted and grid:
            pl.when(last)(end)
        elif hosted:
            end()

    hbm = pl.BlockSpec(memory_space=pl.ANY)
    all_in, all_out = in_specs + [hbm] * len(x_in), out_specs + [hbm] * len(x_out)
    kw = dict(name=name, out_shape=out_shape + x_out, input_output_aliases=aliases,
              compiler_params=_params(tuple("arbitrary" for _ in grid) if hosted else sem))
    if prefetch:
        kw["grid_spec"] = pltpu.PrefetchScalarGridSpec(num_scalar_prefetch=n_pre, grid=grid, in_specs=all_in,
                                                       out_specs=all_out, scratch_shapes=scratch_shapes + sems)
    else:
        kw.update(grid=grid, in_specs=all_in, out_specs=all_out, scratch_shapes=scratch_shapes + sems)
    res = pl.pallas_call(wrapped, **kw)(*prefetch, *args, *x_in)
    return list(res[:n_out]), list(res[n_out:])


def rms_fwd(h, gain, name, hosted=()):
    T, D = h.shape
    tm = _tile(T, TOKEN_TILE)

    def body(h_ref, g_ref, o_ref):
        x = h_ref[...]
        r = lax.rsqrt(jnp.mean(x * x, axis=-1, keepdims=True) + RMS_EPS)
        o_ref[...] = (x * r * g_ref[...]).astype(o_ref.dtype)

    (n,), xo = _call(
        body, name, (T // tm,),
        [pl.BlockSpec((tm, D), lambda i: (i, 0)), pl.BlockSpec((1, D), lambda i: (0, 0))],
        [pl.BlockSpec((tm, D), lambda i: (i, 0))], [jax.ShapeDtypeStruct((T, D), BF16)],
        [h, gain], ("parallel",), hosted=hosted)
    return n, xo


def loss_head(h, gain, tgt, name):
    T, D = h.shape
    tm = _tile(T, TOKEN_TILE)

    def body(h_ref, g_ref, t_ref, loss_ref, dh_ref, dg_ref):
        i = pl.program_id(0)
        x = h_ref[...]
        g = g_ref[...]
        r = lax.rsqrt(jnp.mean(x * x, axis=-1, keepdims=True) + RMS_EPS)
        xhat = x * r
        diff = xhat * g - t_ref[...]
        part_loss = 0.5 * jnp.sum(jnp.mean(diff * diff, axis=-1, keepdims=True), axis=0, keepdims=True)
        dy = diff * (1.0 / D)
        dxhat = dy * g
        dh_ref[...] = r * (dxhat - xhat * jnp.mean(dxhat * xhat, axis=-1, keepdims=True))
        part = jnp.sum(dy * xhat, axis=0, keepdims=True)

        @pl.when(i == 0)
        def _():
            dg_ref[...] = part
            loss_ref[...] = part_loss

        @pl.when(i > 0)
        def _():
            dg_ref[...] += part
            loss_ref[...] += part_loss

    row = pl.BlockSpec((tm, D), lambda i: (i, 0))
    vec = pl.BlockSpec((1, D), lambda i: (0, 0))
    return pl.pallas_call(
        body, name=name, grid=(T // tm,),
        in_specs=[row, vec, row],
        out_specs=[pl.BlockSpec((1, 1), lambda i: (0, 0)), row, vec],
        out_shape=[jax.ShapeDtypeStruct((1, 1), F32), jax.ShapeDtypeStruct((T, D), F32),
                   jax.ShapeDtypeStruct((1, D), F32)],
        compiler_params=_params(("arbitrary",)),
    )(h, gain, tgt)


def _prev_halo_spec(tm, width):
    return pl.BlockSpec((HALO, width), lambda i: (jnp.maximum(i * (tm // HALO) - 1, 0), 0))


def _next_halo_spec(tm, width, T):
    return pl.BlockSpec((HALO, width), lambda i: (jnp.minimum((i + 1) * (tm // HALO), T // HALO - 1), 0))


def _shifted(win, off, rows):
    if off % SUBLANES == 0:
        return win[off:off + rows]
    n = win.shape[0]
    return pltpu.roll(win, (n - off) % n, 0)[:rows]


def _rowsum8(x):
    acc = x[0:SUBLANES]
    for q in range(1, x.shape[0] // SUBLANES):
        acc = acc + x[q * SUBLANES:(q + 1) * SUBLANES]
    return acc


def _conv_loops(tm, D, per_block):
    def chunk(r, carry):
        t0 = pl.multiple_of(r * CONV_ROWS, CONV_ROWS)
        for lb in range(D // LANES):
            per_block(t0, slice(lb * LANES, (lb + 1) * LANES))
        return carry

    lax.fori_loop(0, tm // CONV_ROWS, chunk, 0)


def gateconv_fwd(bcv, w, name, hosted=()):
    T, D3 = bcv.shape
    D = D3 // 3
    K = w.shape[0]
    tm = _tile(T, TOKEN_TILE)

    def body(x_ref, halo_ref, w_ref, y_ref, pad_ref):
        i = pl.program_id(0)
        pad_ref[HALO:, :] = x_ref[:, D:2 * D] * x_ref[:, 2 * D:]
        pad_ref[:HALO, :] = jnp.where(i > 0, halo_ref[:, D:2 * D] * halo_ref[:, 2 * D:], 0.0)

        def block(t0, ls):
            win = pad_ref[pl.ds(t0, CONV_ROWS + HALO), ls]
            acc = jnp.zeros((CONV_ROWS, LANES), F32)
            for k in range(K):
                acc = acc + w_ref[k:k + 1, ls] * _shifted(win, HALO - (K - 1) + k, CONV_ROWS)
            y_ref[pl.ds(t0, CONV_ROWS), ls] = (x_ref[pl.ds(t0, CONV_ROWS), ls] * acc).astype(y_ref.dtype)

        _conv_loops(tm, D, block)

    (y,), xo = _call(
        body, name, (T // tm,),
        [pl.BlockSpec((tm, D3), lambda i: (i, 0)), _prev_halo_spec(tm, D3), pl.BlockSpec((K, D), lambda i: (0, 0))],
        [pl.BlockSpec((tm, D), lambda i: (i, 0))], [jax.ShapeDtypeStruct((T, D), BF16)],
        [bcv, bcv, w], ("parallel",), [pltpu.VMEM((tm + HALO, D), F32)], hosted=hosted)
    return y, xo


def gateconv_bwd(dy, bcv, w, name, hosted=()):
    T, D3 = bcv.shape
    D = D3 // 3
    K = w.shape[0]
    tm = _tile(T, TOKEN_TILE)
    nt = T // tm

    def body(dy_ref, dyn_ref, x_ref, xp_ref, xn_ref, w_ref, o_ref, dw_ref, cv_ref, dc_ref, wacc_ref):
        i = pl.program_id(0)
        cv_ref[HALO:, :] = x_ref[:, D:2 * D] * x_ref[:, 2 * D:]
        cv_ref[:HALO, :] = jnp.where(i > 0, xp_ref[:, D:2 * D] * xp_ref[:, 2 * D:], 0.0)
        dc_ref[:tm, :] = dy_ref[...] * x_ref[:, :D]
        dc_ref[tm:, :] = jnp.where(i < nt - 1, dyn_ref[...] * xn_ref[:, :D], 0.0)

        @pl.when(i == 0)
        def _():
            wacc_ref[...] = jnp.zeros_like(wacc_ref)

        def block(t0, ls):
            cwin = cv_ref[pl.ds(t0, CONV_ROWS + HALO), ls]
            dwin = dc_ref[pl.ds(t0, CONV_ROWS + HALO), ls]
            dcon = dwin[:CONV_ROWS]
            conv = jnp.zeros((CONV_ROWS, LANES), F32)
            dcv = jnp.zeros((CONV_ROWS, LANES), F32)
            for k in range(K):
                wk = w_ref[k:k + 1, ls]
                cs = _shifted(cwin, HALO - (K - 1) + k, CONV_ROWS)
                conv = conv + wk * cs
                dcv = dcv + wk * _shifted(dwin, (K - 1) - k, CONV_ROWS)
                wacc_ref[k * SUBLANES:(k + 1) * SUBLANES, ls] += _rowsum8(dcon * cs)
            rows = pl.ds(t0, CONV_ROWS)
            o_ref[rows, ls] = (dy_ref[rows, ls] * conv).astype(o_ref.dtype)
            o_ref[rows, pl.ds(D + ls.start, LANES)] = (dcv * x_ref[rows, pl.ds(2 * D + ls.start, LANES)]).astype(o_ref.dtype)
            o_ref[rows, pl.ds(2 * D + ls.start, LANES)] = (dcv * x_ref[rows, pl.ds(D + ls.start, LANES)]).astype(o_ref.dtype)

        _conv_loops(tm, D, block)

        @pl.when(i == nt - 1)
        def _():
            for k in range(K):
                dw_ref[k:k + 1, :] = jnp.sum(wacc_ref[k * SUBLANES:(k + 1) * SUBLANES, :], axis=0, keepdims=True)

    (dx, dw), xo = _call(
        body, name, (nt,),
        [pl.BlockSpec((tm, D), lambda i: (i, 0)), _next_halo_spec(tm, D, T),
         pl.BlockSpec((tm, D3), lambda i: (i, 0)), _prev_halo_spec(tm, D3), _next_halo_spec(tm, D3, T),
         pl.BlockSpec((K, D), lambda i: (0, 0))],
        [pl.BlockSpec((tm, D3), lambda i: (i, 0)), pl.BlockSpec((K, D), lambda i: (0, 0))],
        [jax.ShapeDtypeStruct((T, D3), BF16), jax.ShapeDtypeStruct((K, D), F32)],
        [dy, dy, bcv, bcv, bcv, w], ("arbitrary",),
        [pltpu.VMEM((tm + HALO, D), F32), pltpu.VMEM((tm + HALO, D), F32), pltpu.VMEM((K * SUBLANES, D), F32)],
        hosted=hosted)
    return dx, dw, xo


def bconv_fwd(u, w, b_conv, ln_g, ln_b, name, hosted=()):
    T, D2 = u.shape
    D = D2 // 2
    K = w.shape[0]
    tm = _tile(T, TOKEN_TILE)

    def body(u_ref, halo_ref, w_ref, bc_ref, g_ref, b_ref, cu_ref, s_ref, pad_ref):
        i = pl.program_id(0)
        pad_ref[HALO:, :] = u_ref[:, :D] * _sigmoid(u_ref[:, D:])
        pad_ref[:HALO, :] = jnp.where(i > 0, halo_ref[:, :D] * _sigmoid(halo_ref[:, D:]), 0.0)

        def block(t0, ls):
            win = pad_ref[pl.ds(t0, CONV_ROWS + HALO), ls]
            acc = jnp.zeros((CONV_ROWS, LANES), F32)
            for k in range(K):
                acc = acc + w_ref[k:k + 1, ls] * _shifted(win, HALO - (K - 1) + k, CONV_ROWS)
            cu_ref[pl.ds(t0, CONV_ROWS), ls] = acc + bc_ref[:, ls]

        _conv_loops(tm, D, block)
        cu = cu_ref[...]
        mu = jnp.mean(cu, axis=-1, keepdims=True)
        xc = cu - mu
        rstd = lax.rsqrt(jnp.mean(xc * xc, axis=-1, keepdims=True) + LN_EPS)
        ln = xc * rstd * g_ref[...] + b_ref[...]
        s_ref[...] = (ln * _sigmoid(ln)).astype(s_ref.dtype)

    vec = pl.BlockSpec((1, D), lambda i: (0, 0))
    row = pl.BlockSpec((tm, D), lambda i: (i, 0))
    (cu, s), xo = _call(
        body, name, (T // tm,),
        [pl.BlockSpec((tm, D2), lambda i: (i, 0)), _prev_halo_spec(tm, D2), pl.BlockSpec((K, D), lambda i: (0, 0)), vec, vec, vec],
        [row, row], [jax.ShapeDtypeStruct((T, D), F32), jax.ShapeDtypeStruct((T, D), BF16)],
        [u, u, w, b_conv, ln_g, ln_b], ("parallel",), [pltpu.VMEM((tm + HALO, D), F32)], hosted=hosted)
    return cu, s, xo


def ln_silu_bwd(ds, cu, ln_g, ln_b, name):
    T, D = cu.shape
    tm = _tile(T, TOKEN_TILE)

    def body(ds_ref, cu_ref, g_ref, b_ref, dcu_ref, dg_ref, db_ref, dbc_ref):
        i = pl.program_id(0)
        cu_ = cu_ref[...]
        mu = jnp.mean(cu_, axis=-1, keepdims=True)
        xc = cu_ - mu
        rstd = lax.rsqrt(jnp.mean(xc * xc, axis=-1, keepdims=True) + LN_EPS)
        xh = xc * rstd
        ln = xh * g_ref[...] + b_ref[...]
        sg = _sigmoid(ln)
        dl = ds_ref[...] * (sg * (1.0 + ln * (1.0 - sg)))
        dxh = dl * g_ref[...]
        dcu = rstd * (dxh - jnp.mean(dxh, axis=-1, keepdims=True) - xh * jnp.mean(dxh * xh, axis=-1, keepdims=True))
        dcu_ref[...] = dcu
        pg = jnp.sum(dl * xh, axis=0, keepdims=True)
        pb = jnp.sum(dl, axis=0, keepdims=True)
        pc = jnp.sum(dcu, axis=0, keepdims=True)

        @pl.when(i == 0)
        def _():
            dg_ref[...] = pg
            db_ref[...] = pb
            dbc_ref[...] = pc

        @pl.when(i > 0)
        def _():
            dg_ref[...] += pg
            db_ref[...] += pb
            dbc_ref[...] += pc

    vec = pl.BlockSpec((1, D), lambda i: (0, 0))
    row = pl.BlockSpec((tm, D), lambda i: (i, 0))
    vshape = jax.ShapeDtypeStruct((1, D), F32)
    return pl.pallas_call(
        body, name=name, grid=(T // tm,),
        in_specs=[row, row, vec, vec], out_specs=[row, vec, vec, vec],
        out_shape=[jax.ShapeDtypeStruct((T, D), F32), vshape, vshape, vshape],
        compiler_params=_params(("arbitrary",)),
    )(ds, cu, ln_g, ln_b)


def bconv_bwd(dcu, u, w, name, hosted=()):
    T, D2 = u.shape
    D = D2 // 2
    K = w.shape[0]
    tm = _tile(T, TOKEN_TILE)
    nt = T // tm

    def body(dc_ref, dcn_ref, u_ref, up_ref, w_ref, du_ref, dw_ref, db_ref, glu_ref, dpad_ref, dglu_ref, wacc_ref):
        i = pl.program_id(0)
        glu_ref[HALO:, :] = u_ref[:, :D] * _sigmoid(u_ref[:, D:])
        glu_ref[:HALO, :] = jnp.where(i > 0, up_ref[:, :D] * _sigmoid(up_ref[:, D:]), 0.0)
        dpad_ref[:tm, :] = dc_ref[...]
        dpad_ref[tm:, :] = jnp.where(i < nt - 1, dcn_ref[...], 0.0)

        @pl.when(i == 0)
        def _():
            wacc_ref[...] = jnp.zeros_like(wacc_ref)

        def block(t0, ls):
            gwin = glu_ref[pl.ds(t0, CONV_ROWS + HALO), ls]
            dwin = dpad_ref[pl.ds(t0, CONV_ROWS + HALO), ls]
            dcur = dwin[:CONV_ROWS]
            dglu = jnp.zeros((CONV_ROWS, LANES), F32)
            for k in range(K):
                dglu = dglu + w_ref[k:k + 1, ls] * _shifted(dwin, (K - 1) - k, CONV_ROWS)
                gs = _shifted(gwin, HALO - (K - 1) + k, CONV_ROWS)
                wacc_ref[k * SUBLANES:(k + 1) * SUBLANES, ls] += _rowsum8(dcur * gs)
            dglu_ref[pl.ds(t0, CONV_ROWS), ls] = dglu

        _conv_loops(tm, D, block)
        dglu = dglu_ref[...]
        a = u_ref[:, :D]
        sg = _sigmoid(u_ref[:, D:])
        da = dglu * sg
        dg = dglu * a * (sg * (1.0 - sg))
        du_ref[:, :D] = da.astype(du_ref.dtype)
        du_ref[:, D:] = dg.astype(du_ref.dtype)
        pa = jnp.sum(da, axis=0, keepdims=True)
        pg = jnp.sum(dg, axis=0, keepdims=True)

        @pl.when(i == 0)
        def _():
            db_ref[:, :D] = pa
            db_ref[:, D:] = pg

        @pl.when(i > 0)
        def _():
            db_ref[:, :D] += pa
            db_ref[:, D:] += pg

        @pl.when(i == nt - 1)
        def _():
            for k in range(K):
                dw_ref[k:k + 1, :] = jnp.sum(wacc_ref[k * SUBLANES:(k + 1) * SUBLANES, :], axis=0, keepdims=True)

    (du, dw, db), xo = _call(
        body, name, (nt,),
        [pl.BlockSpec((tm, D), lambda i: (i, 0)), _next_halo_spec(tm, D, T),
         pl.BlockSpec((tm, D2), lambda i: (i, 0)), _prev_halo_spec(tm, D2), pl.BlockSpec((K, D), lambda i: (0, 0))],
        [pl.BlockSpec((tm, D2), lambda i: (i, 0)), pl.BlockSpec((K, D), lambda i: (0, 0)), pl.BlockSpec((1, D2), lambda i: (0, 0))],
        [jax.ShapeDtypeStruct((T, D2), BF16), jax.ShapeDtypeStruct((K, D), F32), jax.ShapeDtypeStruct((1, D2), F32)],
        [dcu, dcu, u, u, w], ("arbitrary",),
        [pltpu.VMEM((tm + HALO, D), F32), pltpu.VMEM((tm + HALO, D), F32), pltpu.VMEM((tm, D), F32),
         pltpu.VMEM((K * SUBLANES, D), F32)], hosted=hosted)
    return du, dw, db, xo


def mm_cols(a, w, bias, name, hosted=()):
    T, K = a.shape
    S, _, n = w.shape
    tm = _tile(T, WIDE_TOKEN_TILE)

    def body(*refs):
        a_ref, w_ref = refs[:2]
        o_ref = refs[-1]
        acc = jnp.dot(a_ref[...], w_ref[...], preferred_element_type=F32)
        if bias is not None:
            acc = acc + refs[2][...]
        o_ref[...] = acc

    in_specs = [pl.BlockSpec((tm, K), lambda s, i: (i, 0)), pl.BlockSpec((None, K, n), lambda s, i: (s, 0, 0))]
    args = [a, w]
    if bias is not None:
        in_specs.append(pl.BlockSpec((1, n), lambda s, i: (0, s)))
        args.append(bias)
    (out,), xo = _call(body, name, (S, T // tm), in_specs, [pl.BlockSpec((tm, n), lambda s, i: (i, s))],
                       [jax.ShapeDtypeStruct((T, S * n), F32)], args, ("parallel", "parallel"), hosted=hosted)
    return out, xo


def _load_once(pairs, sems):
    cps = [pltpu.make_async_copy(src, dst, sems.at[k]) for k, (src, dst) in enumerate(pairs)]
    for cp in cps:
        cp.start()
    for cp in cps:
        cp.wait()


def ffn_fwd(h, gain, wg, wu, wd, name, hosted=()):
    T, D = h.shape
    S, f, _ = wg.shape
    tm = _tile(T, TOKEN_TILE)
    rc = tm // FFN_ROW_CHUNKS
    chunks = [slice(r * rc, (r + 1) * rc) for r in range(FFN_ROW_CHUNKS)]
    wd_arrives = isinstance(wd, _Exchange)
    weights = [wg, wu] if wd_arrives else [wg, wu, wd]
    hosted = ([wd] if wd_arrives else []) + list(hosted)

    def body(h_ref, gain_ref, *refs):
        nw = len(weights)
        wg_hbm, wu_hbm = refs[:2]
        wd_hbm = refs[-1] if wd_arrives else refs[2]
        n_ref, g_ref, u_ref, gu_ref, o_ref, wg_v, wu_v, wd_v, sems = refs[nw:nw + 9]
        i, s = pl.program_id(0), pl.program_id(1)

        @pl.when((i == 0) & (s == 0))
        def _():
            _load_once([(wg_hbm, wg_v), (wu_hbm, wu_v), (wd_hbm, wd_v)], sems)

        @pl.when(s == 0)
        def _():
            x = h_ref[...]
            r = lax.rsqrt(jnp.mean(x * x, axis=-1, keepdims=True) + RMS_EPS)
            n_ref[...] = (x * r * gain_ref[...]).astype(n_ref.dtype)

        parts = []
        for rows in chunks:
            a = n_ref[rows, :]
            g = lax.dot_general(a, wg_v[s], _NT, preferred_element_type=F32)
            u = lax.dot_general(a, wu_v[s], _NT, preferred_element_type=F32)
            gu = (g * _sigmoid(g) * u).astype(gu_ref.dtype)
            g_ref[rows, :] = g.astype(g_ref.dtype)
            u_ref[rows, :] = u.astype(u_ref.dtype)
            gu_ref[rows, :] = gu
            parts.append(jnp.dot(gu, wd_v[s], preferred_element_type=F32))

        @pl.when(s == 0)
        def _():
            for rows, part in zip(chunks, parts):
                o_ref[rows, :] = h_ref[rows, :] + part

        @pl.when(s > 0)
        def _():
            for rows, part in zip(chunks, parts):
                o_ref[rows, :] += part

    row = pl.BlockSpec((tm, D), lambda i, s: (i, 0))
    seg = pl.BlockSpec((None, tm, f), lambda i, s: (s, i, 0))
    hbm = pl.BlockSpec(memory_space=pl.ANY)
    segs = jax.ShapeDtypeStruct((S, T, f), BF16)
    outs, xo = _call(
        body, name, (T // tm, S),
        [row, pl.BlockSpec((1, D), lambda i, s: (0, 0))] + [hbm] * len(weights), [row, seg, seg, seg, row],
        [jax.ShapeDtypeStruct((T, D), BF16), segs, segs, segs, jax.ShapeDtypeStruct((T, D), F32)],
        [h, gain] + weights, ("arbitrary", "arbitrary"),
        [pltpu.VMEM((S, f, D), BF16), pltpu.VMEM((S, f, D), BF16), pltpu.VMEM((S, f, D), BF16), pltpu.SemaphoreType.DMA((3,))],
        hosted=hosted)
    return (*outs, xo)


def ffn_bwd(dy, h, gain, g, u, wd, wg, wu, name, hosted=()):
    T, D = h.shape
    S, f, _ = wg.shape
    tm = _tile(T, TOKEN_TILE)
    nt = T // tm
    rc = tm // FFN_ROW_CHUNKS
    chunks = [slice(r * rc, (r + 1) * rc) for r in range(FFN_ROW_CHUNKS)]

    def body(dy_ref, h_ref, gain_ref, g_ref, u_ref, wd_hbm, wg_hbm, wu_hbm, dg_ref, du_ref, dh_ref, dgain_ref,
             wd_v, wg_v, wu_v, dyb_ref, sems):
        i, s = pl.program_id(0), pl.program_id(1)

        @pl.when((i == 0) & (s == 0))
        def _():
            _load_once([(wd_hbm, wd_v), (wg_hbm, wg_v), (wu_hbm, wu_v)], sems)

        @pl.when(s == 0)
        def _():
            dyb_ref[...] = dy_ref[...].astype(dyb_ref.dtype)

        parts = []
        for rows in chunks:
            dgu = lax.dot_general(dyb_ref[rows, :], wd_v[s], _NT, preferred_element_type=F32)
            gv = g_ref[rows, :].astype(F32)
            sg = _sigmoid(gv)
            dg = (dgu * u_ref[rows, :].astype(F32) * (sg * (1.0 + gv * (1.0 - sg)))).astype(dg_ref.dtype)
            du = (dgu * (gv * sg)).astype(du_ref.dtype)
            dg_ref[rows, :] = dg
            du_ref[rows, :] = du
            parts.append(jnp.dot(dg, wg_v[s], preferred_element_type=F32)
                         + jnp.dot(du, wu_v[s], preferred_element_type=F32))

        @pl.when(s == 0)
        def _():
            for rows, part in zip(chunks, parts):
                dh_ref[rows, :] = part

        @pl.when(s > 0)
        def _():
            for rows, part in zip(chunks, parts):
                dh_ref[rows, :] += part

        @pl.when(s == S - 1)
        def _():
            dn = dh_ref[...]
            x = h_ref[...]
            r = lax.rsqrt(jnp.mean(x * x, axis=-1, keepdims=True) + RMS_EPS)
            xhat = x * r
            dxhat = dn * gain_ref[...]
            dh_ref[...] = dy_ref[...] + r * (dxhat - xhat * jnp.mean(dxhat * xhat, axis=-1, keepdims=True))
            pg = jnp.sum(dn * xhat, axis=0, keepdims=True)

            @pl.when(i == 0)
            def _():
                dgain_ref[...] = pg

            @pl.when(i > 0)
            def _():
                dgain_ref[...] += pg

    row = pl.BlockSpec((tm, D), lambda i, s: (i, 0))
    vec = pl.BlockSpec((1, D), lambda i, s: (0, 0))
    seg = pl.BlockSpec((None, tm, f), lambda i, s: (s, i, 0))
    hbm = pl.BlockSpec(memory_space=pl.ANY)
    segs = jax.ShapeDtypeStruct((S, T, f), BF16)
    outs, xo = _call(
        body, name, (nt, S),
        [row, row, vec, seg, seg, hbm, hbm, hbm], [seg, seg, row, vec],
        [segs, segs, jax.ShapeDtypeStruct((T, D), F32), jax.ShapeDtypeStruct((1, D), F32)],
        [dy, h, gain, g, u, wd, wg, wu], ("arbitrary", "arbitrary"),
        [pltpu.VMEM((S, f, D), BF16), pltpu.VMEM((S, f, D), BF16), pltpu.VMEM((S, f, D), BF16),
         pltpu.VMEM((tm, D), BF16), pltpu.SemaphoreType.DMA((3,))], hosted=hosted)
    return (*outs, xo)


def mm_rows(a, w, res, bias, name, hosted=()):
    S, T, k = a.shape
    N = w.shape[-1]
    tm = _tile(T, TOKEN_TILE)

    def body(*refs):
        a_ref, w_ref, r_ref = refs[:3]
        o_ref = refs[-1]
        s = pl.program_id(1)
        acc = jnp.dot(a_ref[...], w_ref[...], preferred_element_type=F32)

        @pl.when(s == 0)
        def _():
            base = r_ref[...]
            if bias is not None:
                base = base + refs[3][...]
            o_ref[...] = base + acc

        @pl.when(s > 0)
        def _():
            o_ref[...] += acc

    in_specs = [pl.BlockSpec((None, tm, k), lambda i, s: (s, i, 0)),
                pl.BlockSpec((None, k, N), lambda i, s: (s, 0, 0)),
                pl.BlockSpec((tm, N), lambda i, s: (i, 0))]
    args = [a, w, res]
    if bias is not None:
        in_specs.append(pl.BlockSpec((1, N), lambda i, s: (0, 0)))
        args.append(bias)
    (out,), xo = _call(body, name, (T // tm, S), in_specs, [pl.BlockSpec((tm, N), lambda i, s: (i, 0))],
                       [jax.ShapeDtypeStruct((T, N), F32)], args, ("parallel", "arbitrary"), hosted=hosted)
    return out, xo


_NT = (((1,), (1,)), ((), ()))
_TN = (((0,), (0,)), ((), ()))


def nt_rows(dy, w, want_colsum, name, hosted=()):
    T, N = dy.shape
    S, k, _ = w.shape
    tm = _tile(T, TOKEN_TILE)

    def body(dy_ref, w_ref, o_ref, *rest):
        i, s = pl.program_id(0), pl.program_id(1)
        d = dy_ref[...]
        o_ref[...] = lax.dot_general(d.astype(BF16), w_ref[...], _NT, preferred_element_type=F32)
        if want_colsum:
            cs_ref = rest[0]
            part = jnp.sum(d, axis=0, keepdims=True)

            @pl.when((i == 0) & (s == 0))
            def _():
                cs_ref[...] = part

            @pl.when((i > 0) & (s == 0))
            def _():
                cs_ref[...] += part

    out_specs = [pl.BlockSpec((None, tm, k), lambda i, s: (s, i, 0))]
    out_shape = [jax.ShapeDtypeStruct((S, T, k), F32)]
    if want_colsum:
        out_specs.append(pl.BlockSpec((1, N), lambda i, s: (0, 0)))
        out_shape.append(jax.ShapeDtypeStruct((1, N), F32))
    outs, xo = _call(
        body, name, (T // tm, S),
        [pl.BlockSpec((tm, N), lambda i, s: (i, 0)), pl.BlockSpec((None, k, N), lambda i, s: (s, 0, 0))],
        out_specs, out_shape, [dy, w], ("arbitrary", "arbitrary"), hosted=hosted)
    return (*outs, xo)


def nt_cols_rms(dy, w, h, gain, dres, name, hosted=()):
    T, K = h.shape
    S, _, n = w.shape
    tm = _tile(T, TOKEN_TILE)

    def body(dy_ref, w_ref, h_ref, gain_ref, dres_ref, dh_ref, dgain_ref):
        i = pl.program_id(0)
        dn = None
        for s in range(S):
            part = lax.dot_general(dy_ref[:, s * n:(s + 1) * n], w_ref[s], _NT, preferred_element_type=F32)
            dn = part if dn is None else dn + part
        x = h_ref[...]
        r = lax.rsqrt(jnp.mean(x * x, axis=-1, keepdims=True) + RMS_EPS)
        xhat = x * r
        dxhat = dn * gain_ref[...]
        dh_ref[...] = dres_ref[...] + r * (dxhat - xhat * jnp.mean(dxhat * xhat, axis=-1, keepdims=True))
        pg = jnp.sum(dn * xhat, axis=0, keepdims=True)

        @pl.when(i == 0)
        def _():
            dgain_ref[...] = pg

        @pl.when(i > 0)
        def _():
            dgain_ref[...] += pg

    row = pl.BlockSpec((tm, K), lambda i: (i, 0))
    vec = pl.BlockSpec((1, K), lambda i: (0, 0))
    (dh, dgain), xo = _call(
        body, name, (T // tm,),
        [pl.BlockSpec((tm, S * n), lambda i: (i, 0)), pl.BlockSpec((S, K, n), lambda i: (0, 0, 0)), row, vec, row],
        [row, vec], [jax.ShapeDtypeStruct((T, K), F32), jax.ShapeDtypeStruct((1, K), F32)],
        [dy, w, h, gain, dres], ("arbitrary",), hosted=hosted)
    return dh, dgain, xo


def tn_grad(a, dy, S, a_by_seg, name, hosted=()):
    T = dy.shape[0] if dy.ndim == 2 else dy.shape[1]
    tt = _tile(T, GRAD_TOKEN_TILE)
    if a_by_seg:
        R = a.shape[1] // S if a.ndim == 2 else a.shape[2]
        C = dy.shape[1]
        a_spec = pl.BlockSpec((tt, R), lambda s, t: (t, s)) if a.ndim == 2 else pl.BlockSpec((None, tt, R), lambda s, t: (s, t, 0))
        b_spec = pl.BlockSpec((tt, C), lambda s, t: (t, 0))
    else:
        R = a.shape[1]
        C = dy.shape[1] // S if dy.ndim == 2 else dy.shape[2]
        a_spec = pl.BlockSpec((tt, R), lambda s, t: (t, 0))
        b_spec = pl.BlockSpec((tt, C), lambda s, t: (t, s)) if dy.ndim == 2 else pl.BlockSpec((None, tt, C), lambda s, t: (s, t, 0))
    Rh = R // 2
    nt = T // tt

    def body(a_ref, b_ref, o_ref, acc_ref):
        t = pl.program_id(1)
        part = lax.dot_general(a_ref[...], b_ref[...].astype(BF16), _TN, preferred_element_type=F32)

        @pl.when(t == 0)
        def _():
            acc_ref[...] = part

        @pl.when(t > 0)
        def _():
            acc_ref[...] += part

        @pl.when(t == nt - 1)
        def _():
            o_ref[0] = acc_ref[:Rh, :].astype(o_ref.dtype)
            o_ref[1] = acc_ref[Rh:, :].astype(o_ref.dtype)

    (gh,), xo = _call(
        body, name, (S, nt), [a_spec, b_spec], [pl.BlockSpec((2, None, Rh, C), lambda s, t: (0, s, 0, 0))],
        [jax.ShapeDtypeStruct((2, S, Rh, C), BF16)], [a, dy], ("parallel", "arbitrary"), [pltpu.VMEM((R, C), F32)],
        hosted=hosted)
    return gh, xo


def tn_grad_square(a, dy, S, name, hosted=()):
    T, K = a.shape
    N = dy.shape[1]
    tt = _tile(T, GRAD_TOKEN_TILE)
    nt = T // tt
    Rh = K // S // 2

    def body(a_ref, b_ref, o_ref, acc_ref):
        t = pl.program_id(0)
        part = lax.dot_general(a_ref[...], b_ref[...].astype(BF16), _TN, preferred_element_type=F32)

        @pl.when(t == 0)
        def _():
            acc_ref[...] = part

        @pl.when(t > 0)
        def _():
            acc_ref[...] += part

        @pl.when(t == nt - 1)
        def _():
            for s in range(S):
                for hf in range(2):
                    r0 = (2 * s + hf) * Rh
                    o_ref[hf, s] = acc_ref[r0:r0 + Rh, :].astype(o_ref.dtype)

    (gh,), xo = _call(
        body, name, (nt,), [pl.BlockSpec((tt, K), lambda t: (t, 0)), pl.BlockSpec((tt, N), lambda t: (t, 0))],
        [pl.BlockSpec((2, S, Rh, N), lambda t: (0, 0, 0, 0))], [jax.ShapeDtypeStruct((2, S, Rh, N), BF16)],
        [a, dy], ("arbitrary",), [pltpu.VMEM((K, N), F32)], hosted=hosted)
    return gh, xo


def _place():
    x, y, c = lax.axis_index("x"), lax.axis_index("y"), lax.axis_index("c")
    chips = [(1 - x, y), (x, 1 - y), (1 - x, 1 - y)]
    return x, y, c, chips


def _any_specs(n):
    return [pl.BlockSpec(memory_space=pl.ANY)] * n


def _remote(src, dst, send_sem, recv_sem, dev):
    return pltpu.make_async_remote_copy(src_ref=src, dst_ref=dst, send_sem=send_sem, recv_sem=recv_sem,
                                        device_id=dev, device_id_type=MESH)


def small_allreduce(v, name, hosted=()):
    rows, W = v.shape

    def body(v_ref, o_ref, sib_ref, pair_ref, chips_ref, send_sems, recv_sems):
        x, y, c, chips = _place()
        me = 2 * x + y
        swap = _remote(v_ref, sib_ref, send_sems.at[3], recv_sems.at[3], (x, y, 1 - c))
        swap.start()
        swap.wait()
        mine, other = v_ref[...], sib_ref[...]
        pair_ref[...] = jnp.where(c == 0, mine, other) + jnp.where(c == 0, other, mine)
        sends = []
        for j, (px, py) in enumerate(chips):
            cp = _remote(pair_ref, chips_ref.at[me], send_sems.at[j], recv_sems.at[j], (px, py, c))
            cp.start()
            sends.append(cp)
        chips_ref[me] = pair_ref[...]
        for j, (px, py) in enumerate(chips):
            blk = chips_ref.at[2 * px + py]
            _remote(blk, blk, send_sems.at[j], recv_sems.at[j], (px, py, c)).wait_recv()
        for cp in sends:
            cp.wait_send()
        o_ref[...] = (chips_ref[0] + chips_ref[1]) + (chips_ref[2] + chips_ref[3])

    vm = pl.BlockSpec(memory_space=pltpu.VMEM)
    (out,), xo = _call(
        body, name, (), [vm], [vm], [jax.ShapeDtypeStruct((rows, W), F32)], [v], (),
        [pltpu.VMEM((rows, W), F32), pltpu.VMEM((rows, W), F32), pltpu.VMEM((N_CHIPS, rows, W), F32),
         pltpu.SemaphoreType.DMA((4,)), pltpu.SemaphoreType.DMA((4,))], hosted=hosted)
    return out, xo


def _gather_p1_copies(srcs, bufs, ssem, rsem, base):
    x, y, c, chips = _place()
    me, sib = 2 * x + y, (x, y, 1 - c)
    sends, recvs = [], []
    for k, (src, buf) in enumerate(zip(srcs, bufs)):
        rh = src.shape[0] // 2
        s0 = base + 4 * k
        sends.append(_remote(src, buf.at[me], ssem.at[s0 + 3], rsem.at[s0 + 3], sib))
        recvs.append(_remote(buf.at[me], buf.at[me], ssem.at[s0 + 3], rsem.at[s0 + 3], sib))
        for j, (px, py) in enumerate(chips):
            sends.append(_remote(src.at[pl.ds(c * rh, rh)], buf.at[me, pl.ds(c * rh, rh)], ssem.at[s0 + j], rsem.at[s0 + j], (px, py, c)))
            blk = buf.at[2 * px + py, pl.ds(c * rh, rh)]
            recvs.append(_remote(blk, blk, ssem.at[s0 + j], rsem.at[s0 + j], (px, py, c)))
    return sends, recvs


def _gather_p2_copies(bufs, ssem, rsem, base):
    x, y, c, chips = _place()
    sib = (x, y, 1 - c)
    sends, recvs = [], []
    for k, buf in enumerate(bufs):
        rh = buf.shape[1] // 2
        for j, (px, py) in enumerate(chips):
            s0 = base + 3 * k + j
            blk = buf.at[2 * px + py, pl.ds(c * rh, rh)]
            sends.append(_remote(blk, blk, ssem.at[s0], rsem.at[s0], sib))
            got = buf.at[2 * px + py, pl.ds((1 - c) * rh, rh)]
            recvs.append(_remote(got, got, ssem.at[s0], rsem.at[s0], sib))
    return sends, recvs


def _gathered_shape(s):
    return jax.ShapeDtypeStruct((N_CHIPS,) + s.shape, s.dtype)


def gather_p1(shards):
    return _Exchange(shards, [_gathered_shape(s) for s in shards], {}, 4 * len(shards),
                     lambda xi, xo, ss, rs: _gather_p1_copies(xi, xo, ss, rs, 0))


def gather_p2(bufs):
    return _Exchange(bufs, [jax.ShapeDtypeStruct(b.shape, b.dtype) for b in bufs], {k: k for k in range(len(bufs))},
                     3 * len(bufs), lambda xi, xo, ss, rs: _gather_p2_copies(xo, ss, rs, 0))


def gather_first(whole, begun, name):
    nw, n = len(whole), len(whole) + len(begun)

    def body(*refs):
        ins, outs = refs[:n], refs[n:2 * n]
        ssem, rsem = refs[2 * n:]
        s1, r1 = _gather_p1_copies(ins, outs, ssem, rsem, 0)
        for cp in s1:
            cp.start()
        for cp in r1[:4 * nw]:
            cp.wait_recv()
        s2, r2 = _gather_p2_copies(outs[:nw], ssem, rsem, 4 * n)
        for cp in s2:
            cp.start()
        for cp in r1[4 * nw:] + r2:
            cp.wait_recv()
        for cp in s1 + s2:
            cp.wait_send()

    shards = list(whole) + list(begun)
    return pl.pallas_call(
        body, name=name, in_specs=_any_specs(n), out_specs=_any_specs(n),
        out_shape=[_gathered_shape(s) for s in shards],
        scratch_shapes=[pltpu.SemaphoreType.DMA((4 * n + 3 * nw,)), pltpu.SemaphoreType.DMA((4 * n + 3 * nw,))],
    )(*shards)


def gather_small(v):
    def copies(xi, xo, ssem, rsem):
        x, y, c, chips = _place()
        me, sib = 2 * x + y, (x, y, 1 - c)
        sends = [_remote(xi[0], xo[0].at[me], ssem.at[3], rsem.at[3], sib)]
        recvs = [_remote(xo[0].at[me], xo[0].at[me], ssem.at[3], rsem.at[3], sib)]
        for j, (px, py) in enumerate(chips):
            sends.append(_remote(xi[0], xo[0].at[me], ssem.at[j], rsem.at[j], (px, py, c)))
            blk = xo[0].at[2 * px + py]
            recvs.append(_remote(blk, blk, ssem.at[j], rsem.at[j], (px, py, c)))
        return sends, recvs

    return _Exchange([v], [_gathered_shape(v)], {}, 4, copies)


def run_exchanges(exchanges, name):
    return _call(lambda: None, name, (), [], [], [], [], (), hosted=exchanges)[1]


def sibling_halves(grads):
    def copies(xi, xo, ssem, rsem):
        x, y, c, _ = _place()
        sends = [_remote(xi[k].at[1 - c], xo[k], ssem.at[k], rsem.at[k], (x, y, 1 - c)) for k in range(len(grads))]
        return sends, sends

    return _Exchange(grads, [jax.ShapeDtypeStruct(g.shape[1:], g.dtype) for g in grads], {}, len(grads), copies)


def pair_sum(gh, recv, cidx, name):
    _, S, Rh, C = gh.shape

    def body(c_ref, a_ref, b_ref, o_ref):
        o_ref[...] = (a_ref[...].astype(F32) + b_ref[...].astype(F32)).astype(o_ref.dtype)

    return pl.pallas_call(
        body, name=name, out_shape=jax.ShapeDtypeStruct((S, Rh, C), BF16),
        grid_spec=pltpu.PrefetchScalarGridSpec(
            num_scalar_prefetch=1, grid=(S,),
            in_specs=[pl.BlockSpec((None, None, Rh, C), lambda s, c_ref: (c_ref[0], s, 0, 0)),
                      pl.BlockSpec((None, Rh, C), lambda s, c_ref: (s, 0, 0))],
            out_specs=pl.BlockSpec((None, Rh, C), lambda s, c_ref: (s, 0, 0))),
        compiler_params=_params(("parallel",)),
    )(cidx, gh, recv)


def scatter_p1(parts):
    def copies(xi, xo, ssem, rsem):
        x, y, c, chips = _place()
        me, sib = 2 * x + y, (x, y, 1 - c)
        sends, recvs = [], []
        for k in range(len(parts)):
            s0 = 4 * k
            sends.append(_remote(xi[k].at[me], xo[k].at[me, c], ssem.at[s0 + 3], rsem.at[s0 + 3], sib))
            own = xo[k].at[me, 1 - c]
            recvs.append(_remote(own, own, ssem.at[s0 + 3], rsem.at[s0 + 3], sib))
            for j, (px, py) in enumerate(chips):
                sends.append(_remote(xi[k].at[2 * px + py], xo[k].at[me, c], ssem.at[s0 + j], rsem.at[s0 + j], (px, py, c)))
                blk = xo[k].at[2 * px + py, c]
                recvs.append(_remote(blk, blk, ssem.at[s0 + j], rsem.at[s0 + j], (px, py, c)))
        return sends, recvs

    return _Exchange(parts, [jax.ShapeDtypeStruct((p.shape[0], 2) + p.shape[1:], p.dtype) for p in parts], {},
                     4 * len(parts), copies)


def scatter_p2(bufs):
    def copies(xi, xo, ssem, rsem):
        x, y, c, chips = _place()
        sib = (x, y, 1 - c)
        sends, recvs = [], []
        for k in range(len(bufs)):
            for j, (px, py) in enumerate(chips):
                s0 = 3 * k + j
                blk = xo[k].at[2 * px + py, c]
                sends.append(_remote(blk, blk, ssem.at[s0], rsem.at[s0], sib))
                got = xo[k].at[2 * px + py, 1 - c]
                recvs.append(_remote(got, got, ssem.at[s0], rsem.at[s0], sib))
        return sends, recvs

    return _Exchange(bufs, [jax.ShapeDtypeStruct(b.shape, b.dtype) for b in bufs], {k: k for k in range(len(bufs))},
                     3 * len(bufs), copies)


def _adamw_math(w, g, m, v):
    m = ADAM_B1 * m + (1.0 - ADAM_B1) * g
    v = ADAM_B2 * v + (1.0 - ADAM_B2) * (g * g)
    m_hat = m / (1.0 - ADAM_B1 ** ADAM_STEP)
    v_hat = v / (1.0 - ADAM_B2 ** ADAM_STEP)
    delta = -ADAM_LR * (m_hat / (jnp.sqrt(v_hat) + ADAM_EPS) + ADAM_WD * w)
    return delta, m, v


def adamw_reduce(w, m, v, buf, part, place, lyr, bases, name, hosted=()):
    L, R, C = w.shape
    Rh = R // 2
    rb = _tile(Rh, ROW_TILE, 2 * SUBLANES)
    nb = Rh // rb

    def body(place_ref, p_ref, b0, b1, b2, b3, w_ref, m_ref, v_ref, *rest):
        go_ref, d_ref, mo_ref, vo_ref = rest[-4:]
        mine = (place_ref[1] == pl.program_id(0))
        g = None
        for p, b in enumerate((b0, b1, b2, b3)):
            val = jnp.where(mine & (place_ref[0] == p), p_ref[...], b[...]).astype(F32)
            g = val if g is None else g + val
        d, mn, vn = _adamw_math(w_ref[...], g, m_ref[...], v_ref[...])
        go_ref[...] = g
        d_ref[...] = d
        mo_ref[...] = mn
        vo_ref[...] = vn

    def buf_spec(p):
        def idx(h, i, pr):
            own = (pr[0] == p) & (pr[1] == h)
            return (p, jnp.where(own, 1 - h, h), i, 0)
        return pl.BlockSpec((None, None, rb, C), idx)

    blk = pl.BlockSpec((None, rb, C), lambda h, i, pr: (lyr, h * nb + i, 0))
    in_specs = [pl.BlockSpec((None, rb, C), lambda h, i, pr: (pr[0], i, 0))] + [buf_spec(p) for p in range(N_CHIPS)] + [blk] * 3
    args = [part, buf, buf, buf, buf, w, m, v]
    aliases = {}
    if bases is not None:
        in_specs += [pl.BlockSpec(memory_space=pl.ANY)] * 4
        aliases = {len(args) + k: k for k in range(4)}
        args += list(bases)
    shp = jax.ShapeDtypeStruct((L, R, C), F32)
    return _call(body, name, (2, nb), in_specs, [blk] * 4, [shp] * 4, args, ("parallel", "parallel"),
                 hosted=hosted, prefetch=[place], own_aliases=aliases)


def small_update(gall, chip, entries, name):
    ne = len(entries)
    D = gall.shape[1]

    def body(chip_ref, gall_ref, *refs):
        ins, outs = refs[:3 * ne], refs[3 * ne:]
        ch = chip_ref[0]
        for e, (row0, kind, w, _, _) in enumerate(entries):
            r, width = w.shape

            def gsum(rs, cs):
                return gall_ref[rs, cs]

            if kind == "full":
                g = gsum(slice(row0, row0 + r), slice(0, D))
            elif kind == "cols":
                g = gsum(slice(row0, row0 + r), slice(0, width))
                for q in range(1, N_CHIPS):
                    g = jnp.where(ch == q, gsum(slice(row0, row0 + r), slice(q * width, (q + 1) * width)), g)
            else:
                per_row = D // width
                g = gsum(slice(row0, row0 + 1), slice(0, width))
                for q in range(1, N_CHIPS):
                    rr = row0 + q // per_row
                    cc = (q % per_row) * width
                    g = jnp.where(ch == q, gsum(slice(rr, rr + 1), slice(cc, cc + width)), g)
            d, mn, vn = _adamw_math(ins[3 * e][...], g, ins[3 * e + 1][...], ins[3 * e + 2][...])
            outs[4 * e][...] = g
            outs[4 * e + 1][...] = d
            outs[4 * e + 2][...] = mn
            outs[4 * e + 3][...] = vn

    vm = pl.BlockSpec(memory_space=pltpu.VMEM)
    args, out_shape = [], []
    for _, _, w, m, v in entries:
        args += [w, m, v]
        out_shape += [jax.ShapeDtypeStruct(w.shape, F32)] * 4
    return pl.pallas_call(
        body, name=name,
        in_specs=[pl.BlockSpec(memory_space=pltpu.SMEM), vm] + [vm] * (3 * ne),
        out_specs=[vm] * (4 * ne), out_shape=out_shape,
        compiler_params=pltpu.CompilerParams(vmem_limit_bytes=VMEM_LIMIT),
    )(chip, gall, *args)


def _pack_rows(items, width):
    rows, starts, at = [], [], 0
    for it in items:
        r = it.shape[0]
        pad = (-r) % SUBLANES
        starts.append(at)
        rows.append(it)
        if pad:
            rows.append(jnp.zeros((pad, width), F32))
        at += r + pad
    return jnp.concatenate(rows, axis=0), starts


def kernel(x, a_norm, a_w_in, a_conv, a_w_out, b_norm, b_w_pw1, b_b_pw1, b_conv, b_b_conv, b_ln_g, b_ln_b, b_w_pw2, b_b_pw2, ffn_norm, ffn_w_gate, ffn_w_up, ffn_w_down, final_norm, loss_target, m_a_norm, m_a_w_in, m_a_conv, m_a_w_out, m_b_norm, m_b_w_pw1, m_b_b_pw1, m_b_conv, m_b_b_conv, m_b_ln_g, m_b_ln_b, m_b_w_pw2, m_b_b_pw2, m_ffn_norm, m_ffn_w_gate, m_ffn_w_up, m_ffn_w_down, m_final_norm, v_a_norm, v_a_w_in, v_a_conv, v_a_w_out, v_b_norm, v_b_w_pw1, v_b_b_pw1, v_b_conv, v_b_b_conv, v_b_ln_g, v_b_ln_b, v_b_w_pw2, v_b_b_pw2, v_ffn_norm, v_ffn_w_gate, v_ffn_w_up, v_ffn_w_down, v_final_norm):
    T, D = x.shape[1], x.shape[2]
    Dq = D // N_CHIPS
    cx, cy, cc = lax.axis_index("x"), lax.axis_index("y"), lax.axis_index("c")
    chip = (2 * cx + cy).astype(jnp.int32).reshape(1)
    cidx = cc.astype(jnp.int32).reshape(1)
    h0 = x.reshape(T, D)
    tgt = loss_target.reshape(T, D)

    small_shards = [a_conv[0], b_norm, b_b_pw1.reshape(2, Dq), b_conv[0], b_b_conv, b_ln_g, b_ln_b, b_b_pw2]
    packed, st = _pack_rows(small_shards, Dq)

    tr = lambda t: jnp.swapaxes(t, 1, 2)
    w_gate, m_gate, v_gate = tr(ffn_w_gate), tr(m_ffn_w_gate), tr(v_ffn_w_gate)
    w_up, m_up, v_up = tr(ffn_w_up), tr(m_ffn_w_up), tr(v_ffn_w_up)
    bf = lambda t: t.astype(BF16)
    s_in, s_out, s_pw1, s_pw2 = bf(a_w_in[0]), bf(a_w_out[0]), bf(b_w_pw1[0]), bf(b_w_pw2[0])
    s_gate, s_up, s_down = [bf(w_gate[l]) for l in (0, 1)], [bf(w_up[l]) for l in (0, 1)], [bf(ffn_w_down[l]) for l in (0, 1)]

    g_in, g_out = gather_first([s_in], [s_out], "gather_first")
    n0, _ = rms_fwd(h0, a_norm, "rms_a")
    bcv, (gate0, g_out, sw) = mm_cols(n0, g_in, None, "mm_w_in",
                                      hosted=[gather_p1([s_gate[0]]), gather_p2([g_out]), gather_small(packed)])
    g_out = g_out.reshape(1, D, D)

    def whole(k, r):
        return jnp.transpose(sw[:, st[k]:st[k] + r, :], (1, 0, 2)).reshape(r, D)

    a_conv_f, b_norm_f = whole(0, 3), whole(1, 1)
    b_b_pw1_f = sw[:, st[2]:st[2] + 2, :].reshape(1, 2 * D)
    b_conv_f, b_b_conv_f, b_ln_g_f, b_ln_b_f, b_b_pw2_f = whole(3, b_conv.shape[1]), whole(4, 1), whole(5, 1), whole(6, 1), whole(7, 1)
    ya, (up0, gate0) = gateconv_fwd(bcv, a_conv_f, "gateconv_fwd", hosted=[gather_p1([s_up[0]]), gather_p2([gate0])])
    h1, (down0, up0) = mm_rows(ya[None], g_out, h0, None, "mm_w_out", hosted=[gather_p1([s_down[0]]), gather_p2([up0])])
    n1, fg0, fu0, gu0, h2, (down0, *later) = ffn_fwd(h1, ffn_norm[0:1], gate0, up0, gather_p2([down0]).awaited_first(), "ffn_fwd0",
                                                     hosted=[gather_p1([s_pw1, s_pw2, s_gate[1], s_up[1]])])
    n2, (g_pw1, g_pw2, gate1, up1) = rms_fwd(h2, b_norm_f, "rms_b", hosted=[gather_p2(later)])
    g_pw2 = g_pw2.reshape(1, D, D)
    ub, (down1,) = mm_cols(n2, g_pw1, b_b_pw1_f, "mm_pw1", hosted=[gather_p1([s_down[1]])])
    cu, sb, (down1,) = bconv_fwd(ub, b_conv_f, b_b_conv_f, b_ln_g_f, b_ln_b_f, "bconv_fwd", hosted=[gather_p2([down1])])
    h3, _ = mm_rows(sb[None], g_pw2, h2, b_b_pw2_f, "mm_pw2")
    n3, fg1, fu1, gu1, h4, _ = ffn_fwd(h3, ffn_norm[1:2], gate1, up1, down1, "ffn_fwd1")
    loss_part, dh4, d_final = loss_head(h4, final_norm.reshape(1, D), tgt, "loss_head")

    place = jnp.concatenate([chip, cidx])

    def pair_sums(ghs, from_sib, tags):
        return [pair_sum(g, r, cidx, "pair_sum_" + t) for g, r, t in zip(ghs, from_sib, tags)]

    def upd(w, m, v, bufs, parts, tag, hosted=()):
        res, xo = None, []
        for lyr, (b, p) in enumerate(zip(bufs, parts)):
            res, xo_l = adamw_reduce(w, m, v, b, p, place, lyr, res, "adamw_%s%d" % (tag, lyr), hosted=hosted if lyr == 0 else ())
            xo += xo_l
        return res, xo

    dg1, du1, dh3, d_fn1, _ = ffn_bwd(dh4, h3, ffn_norm[1:2], fg1, fu1, down1, gate1, up1, "ffn_bwd1")
    gh_down1, _ = tn_grad(gu1, dh4, N_CHIPS, True, "tn_down1")
    gh_gate1, _ = tn_grad(dg1, n3, N_CHIPS, True, "tn_gate1")
    gh_up1, _ = tn_grad(du1, n3, N_CHIPS, True, "tn_up1")
    f1 = [gh_gate1, gh_up1, gh_down1]

    ds, d_b_pw2, sib_f1 = nt_rows(dh3, g_pw2, True, "nt_pw2", hosted=[sibling_halves(f1)])
    p_f1 = pair_sums(f1, sib_f1, ["gate1", "up1", "down1"])
    gh_pw2, _ = tn_grad_square(sb, dh3, N_CHIPS, "tn_pw2")
    dcu, d_ln_g, d_ln_b, d_b_conv = ln_silu_bwd(ds[0], cu, b_ln_g_f, b_ln_b_f, "ln_silu_bwd")
    dub, d_bconv_w, d_b_pw1, buf_f1 = bconv_bwd(dcu, ub, b_conv_f, "bconv_bwd", hosted=[scatter_p1(p_f1)])
    gh_pw1, buf_f1 = tn_grad(n2, dub, N_CHIPS, False, "tn_pw1", hosted=[scatter_p2(buf_f1)])
    b_grp = [gh_pw1, gh_pw2]
    dh2, d_b_norm, sib_b = nt_cols_rms(dub, g_pw1, h2, b_norm_f, dh3, "nt_pw1", hosted=[sibling_halves(b_grp)])
    p_b = pair_sums(b_grp, sib_b, ["pw1", "pw2"])

    dg0, du0, dh1, d_fn0, buf_b = ffn_bwd(dh2, h1, ffn_norm[0:1], fg0, fu0, down0, gate0, up0, "ffn_bwd0", hosted=[scatter_p1(p_b)])
    gh_down0, buf_b = tn_grad(gu0, dh2, N_CHIPS, True, "tn_down0", hosted=[scatter_p2(buf_b)])
    gh_gate0, sib_down0 = tn_grad(dg0, n1, N_CHIPS, True, "tn_gate0", hosted=[sibling_halves([gh_down0])])
    p_down0 = pair_sums([gh_down0], sib_down0, ["down0"])
    gh_up0, (buf_down0, sib_gate0) = tn_grad(du0, n1, N_CHIPS, True, "tn_up0",
                                             hosted=[scatter_p1(p_down0), sibling_halves([gh_gate0])])
    p_gate0 = pair_sums([gh_gate0], [sib_gate0], ["gate0"])
    dya, (buf_down0, sib_up0) = nt_rows(dh1, g_out, False, "nt_w_out",
                                        hosted=[scatter_p2([buf_down0]), sibling_halves([gh_up0])])
    p_up0 = pair_sums([gh_up0], [sib_up0], ["up0"])
    gh_out, _ = tn_grad_square(ya, dh1, N_CHIPS, "tn_w_out")
    dbcv, d_aconv_w, (buf_gate0, sib_out) = gateconv_bwd(dya[0], bcv, a_conv_f, "gateconv_bwd",
                                                         hosted=[scatter_p1(p_gate0), sibling_halves([gh_out])])
    p_out = pair_sums([gh_out], [sib_out], ["out"])
    gh_in, (buf_up0, buf_gate0) = tn_grad(n0, dbcv, N_CHIPS, False, "tn_w_in",
                                          hosted=[scatter_p1(p_up0), scatter_p2([buf_gate0])])
    grad_x, d_a_norm, (buf_out, buf_up0, sib_in) = nt_cols_rms(
        dbcv, g_in, h0, a_norm, dh1, "nt_w_in",
        hosted=[scatter_p1(p_out), scatter_p2([buf_up0]), sibling_halves([gh_in])])
    p_in = pair_sums([gh_in], [sib_in], ["in"])
    p_f0 = [p_gate0[0], p_up0[0], p_down0[0]]

    d_ffn_norm = jnp.concatenate([d_fn0, d_fn1], axis=0)
    small_grads = [d_a_norm, d_aconv_w, d_b_norm, d_b_pw1.reshape(2, D), d_bconv_w, d_b_conv, d_ln_g, d_ln_b, d_b_pw2,
                   d_ffn_norm, d_final, jnp.broadcast_to(loss_part, (1, D))]
    gpacked, gs = _pack_rows(small_grads, D)
    gall, buf_in = small_allreduce(gpacked, "allreduce_small_grads", hosted=[scatter_p1(p_in)])
    buf_in, buf_out = run_exchanges([scatter_p2([buf_in[0], buf_out])], "reduce_last")
    buf_a, p_a = [buf_in, buf_out], [p_in[0], p_out[0]]

    r_gate, _ = upd(w_gate, m_gate, v_gate, [buf_gate0, buf_f1[0]], [p_f0[0], p_f1[0]], "gate")
    r_up, _ = upd(w_up, m_up, v_up, [buf_up0, buf_f1[1]], [p_f0[1], p_f1[1]], "up")
    r_down, _ = upd(ffn_w_down, m_ffn_w_down, v_ffn_w_down, [buf_down0, buf_f1[2]], [p_f0[2], p_f1[2]], "down")
    r_gate, r_up = [tr(t) for t in r_gate], [tr(t) for t in r_up]
    r_pw1, _ = upd(b_w_pw1, m_b_w_pw1, v_b_w_pw1, [buf_b[0]], [p_b[0]], "pw1")
    r_pw2, _ = upd(b_w_pw2, m_b_w_pw2, v_b_w_pw2, [buf_b[1]], [p_b[1]], "pw2")
    r_in, _ = upd(a_w_in, m_a_w_in, v_a_w_in, [buf_a[0]], [p_a[0]], "w_in")
    r_out, _ = upd(a_w_out, m_a_w_out, v_a_w_out, [buf_a[1]], [p_a[1]], "w_out")
    entries = [
        (gs[0], "full", a_norm, m_a_norm, v_a_norm),
        (gs[1], "cols", a_conv[0], m_a_conv[0], v_a_conv[0]),
        (gs[2], "cols", b_norm, m_b_norm, v_b_norm),
        (gs[3], "flat2", b_b_pw1, m_b_b_pw1, v_b_b_pw1),
        (gs[4], "cols", b_conv[0], m_b_conv[0], v_b_conv[0]),
        (gs[5], "cols", b_b_conv, m_b_b_conv, v_b_b_conv),
        (gs[6], "cols", b_ln_g, m_b_ln_g, v_b_ln_g),
        (gs[7], "cols", b_ln_b, m_b_ln_b, v_b_ln_b),
        (gs[8], "cols", b_b_pw2, m_b_b_pw2, v_b_b_pw2),
        (gs[9], "full", ffn_norm, m_ffn_norm, v_ffn_norm),
        (gs[10], "full", final_norm.reshape(1, D), m_final_norm.reshape(1, D), v_final_norm.reshape(1, D)),
    ]
    so = small_update(gall, chip, entries, "small_update")
    sm = [so[4 * e:4 * e + 4] for e in range(len(entries))]

    def shaped(e, like):
        return [t.reshape(like.shape) for t in sm[e]]

    r_a_norm, r_a_conv, r_b_norm, r_b_b_pw1 = shaped(0, a_norm), shaped(1, a_conv), shaped(2, b_norm), shaped(3, b_b_pw1)
    r_b_conv, r_b_b_conv, r_b_ln_g, r_b_ln_b = shaped(4, b_conv), shaped(5, b_b_conv), shaped(6, b_ln_g), shaped(7, b_ln_b)
    r_b_b_pw2, r_ffn_norm, r_final = shaped(8, b_b_pw2), shaped(9, ffn_norm), shaped(10, final_norm)

    loss = gall[gs[11], 0]
    order =[r_a_norm, r_in, r_a_conv, r_out, r_b_norm, r_pw1, r_b_b_pw1, r_b_conv, r_b_b_conv, r_b_ln_g, r_b_ln_b,
             r_pw2, r_b_b_pw2, r_ffn_norm, r_gate, r_up, r_down, r_final]
    outs = [loss, grad_x.reshape(x.shape)]
    for field in range(4):
        outs += [r[field] for r in order]
    return tuple(outs)
```

```python
import functools

import jax
import jax.numpy as jnp
from jax import lax
from jax.experimental import pallas as pl
from jax.experimental.pallas import tpu as pltpu

RMS_EPS = 1e-6
LN_EPS = 1e-5
ADAM_LR = 0.001
ADAM_B1 = 0.9
ADAM_B2 = 0.999
ADAM_EPS = 1e-08
ADAM_WD = 0.01
ADAM_STEP = 10

N_CHIPS = 4
N_DEV = 8
LANES = 128
SUBLANES = 8
HALO = 32
CONV_ROWS = 64
TOKEN_TILE = 512
WIDE_TOKEN_TILE = 1024
GRAD_TOKEN_TILE = 2048
FFN_ROW_CHUNKS = 2
ROW_TILE = 256
VMEM_LIMIT = 56 * 1024 * 1024
MESH = pl.DeviceIdType.MESH
BF16 = jnp.bfloat16
F32 = jnp.float32


def _tile(n, pref, mult=SUBLANES):
    t = min(n, pref) // mult * mult
    while n % t:
        t -= mult
    return t


def _params(sem):
    return pltpu.CompilerParams(dimension_semantics=sem, vmem_limit_bytes=VMEM_LIMIT)


def _sigmoid(x):
    return 0.5 * jnp.tanh(0.5 * x) + 0.5


class _Exchange:
    def __init__(self, ins, outs, aliases, n_sems, copies):
        self.ins, self.outs, self.aliases, self.n_sems, self.copies = list(ins), list(outs), dict(aliases), n_sems, copies
        self.early = False

    def awaited_first(self):
        self.early = True
        return self

    def start(self, xi, xo, ssem, rsem):
        for cp in self.copies(xi, xo, ssem, rsem)[0]:
            cp.start()

    def finish(self, xi, xo, ssem, rsem):
        sends, recvs = self.copies(xi, xo, ssem, rsem)
        for cp in recvs:
            cp.wait_recv()
        for cp in sends:
            cp.wait_send()


def _call(body, name, grid, in_specs, out_specs, out_shape, args, sem, scratch_shapes=(), hosted=(), prefetch=(),
          own_aliases=None):
    in_specs, out_specs, out_shape = list(in_specs), list(out_specs), list(out_shape)
    scratch_shapes, hosted, prefetch = list(scratch_shapes), list(hosted), list(prefetch)
    n_pre, n_in, n_out, n_scr = len(prefetch), len(args), len(out_shape), len(scratch_shapes)
    x_in = [a for ex in hosted for a in ex.ins]
    x_out = [o for ex in hosted for o in ex.outs]
    aliases = {n_pre + i: o for i, o in (own_aliases or {}).items()}
    at_in, at_out = n_pre + n_in, n_out
    for ex in hosted:
        for i, o in ex.aliases.items():
            aliases[at_in + i] = at_out + o
        at_in += len(ex.ins)
        at_out += len(ex.outs)
    sems = [pltpu.SemaphoreType.DMA((ex.n_sems,)) for ex in hosted for _ in range(2)]

    def wrapped(*refs):
        pre, refs = refs[:n_pre], refs[n_pre:]
        ins, xi = refs[:n_in], refs[n_in:n_in + len(x_in)]
        refs = refs[n_in + len(x_in):]
        outs, xo = refs[:n_out], refs[n_out:n_out + len(x_out)]
        refs = refs[n_out + len(x_out):]
        scr, sm = refs[:n_scr], refs[n_scr:]
        views, a, b = [], 0, 0
        for e, ex in enumerate(hosted):
            views.append((xi[a:a + len(ex.ins)], xo[b:b + len(ex.outs)], sm[2 * e], sm[2 * e + 1]))
            a += len(ex.ins)
            b += len(ex.outs)
        first = last = None
        for ax, g in enumerate(grid):
            f, l = pl.program_id(ax) == 0, pl.program_id(ax) == g - 1
            first, last = (f, l) if first is None else (first & f, last & l)

        def begin():
            for ex, v in zip(hosted, views):
                ex.start(*v)
            for ex, v in zip(hosted, views):
                if ex.early:
                    ex.finish(*v)

        def end():
            for ex, v in zip(hosted, views):
                if not ex.early:
                    ex.finish(*v)

        if hosted and grid:
            pl.when(first)(begin)
        elif hosted:
            begin()
        early_refs = [r for ex, v in zip(hosted, views) if ex.early for r in v[1]]
        body(*pre, *ins, *outs, *scr, *early_refs)
        if hosted and grid:
            pl.when(last)(end)
        elif hosted:
            end()

    hbm = pl.BlockSpec(memory_space=pl.ANY)
    all_in, all_out = in_specs + [hbm] * len(x_in), out_specs + [hbm] * len(x_out)
    kw = dict(name=name, out_shape=out_shape + x_out, input_output_aliases=aliases,
              compiler_params=_params(tuple("arbitrary" for _ in grid) if hosted else sem))
    if prefetch:
        kw["grid_spec"] = pltpu.PrefetchScalarGridSpec(num_scalar_prefetch=n_pre, grid=grid, in_specs=all_in,
                                                       out_specs=all_out, scratch_shapes=scratch_shapes + sems)
    else:
        kw.update(grid=grid, in_specs=all_in, out_specs=all_out, scratch_shapes=scratch_shapes + sems)
    res = pl.pallas_call(wrapped, **kw)(*prefetch, *args, *x_in)
    return list(res[:n_out]), list(res[n_out:])


def rms_fwd(h, gain, name, hosted=()):
    T, D = h.shape
    tm = _tile(T, TOKEN_TILE)

    def body(h_ref, g_ref, o_ref):
        x = h_ref[...]
        r = lax.rsqrt(jnp.mean(x * x, axis=-1, keepdims=True) + RMS_EPS)
        o_ref[...] = (x * r * g_ref[...]).astype(o_ref.dtype)

    (n,), xo = _call(
        body, name, (T // tm,),
        [pl.BlockSpec((tm, D), lambda i: (i, 0)), pl.BlockSpec((1, D), lambda i: (0, 0))],
        [pl.BlockSpec((tm, D), lambda i: (i, 0))], [jax.ShapeDtypeStruct((T, D), BF16)],
        [h, gain], ("parallel",), hosted=hosted)
    return n, xo


def loss_head(h, gain, tgt, name):
    T, D = h.shape
    tm = _tile(T, TOKEN_TILE)

    def body(h_ref, g_ref, t_ref, loss_ref, dh_ref, dg_ref):
        i = pl.program_id(0)
        x = h_ref[...]
        g = g_ref[...]
        r = lax.rsqrt(jnp.mean(x * x, axis=-1, keepdims=True) + RMS_EPS)
        xhat = x * r
        diff = xhat * g - t_ref[...]
        part_loss = 0.5 * jnp.sum(jnp.mean(diff * diff, axis=-1, keepdims=True), axis=0, keepdims=True)
        dy = diff * (1.0 / D)
        dxhat = dy * g
        dh_ref[...] = r * (dxhat - xhat * jnp.mean(dxhat * xhat, axis=-1, keepdims=True))
        part = jnp.sum(dy * xhat, axis=0, keepdims=True)

        @pl.when(i == 0)
        def _():
            dg_ref[...] = part
            loss_ref[...] = part_loss

        @pl.when(i > 0)
        def _():
            dg_ref[...] += part
            loss_ref[...] += part_loss

    row = pl.BlockSpec((tm, D), lambda i: (i, 0))
    vec = pl.BlockSpec((1, D), lambda i: (0, 0))
    return pl.pallas_call(
        body, name=name, grid=(T // tm,),
        in_specs=[row, vec, row],
        out_specs=[pl.BlockSpec((1, 1), lambda i: (0, 0)), row, vec],
        out_shape=[jax.ShapeDtypeStruct((1, 1), F32), jax.ShapeDtypeStruct((T, D), F32),
                   jax.ShapeDtypeStruct((1, D), F32)],
        compiler_params=_params(("arbitrary",)),
    )(h, gain, tgt)


def _prev_halo_spec(tm, width):
    return pl.BlockSpec((HALO, width), lambda i: (jnp.maximum(i * (tm // HALO) - 1, 0), 0))


def _next_halo_spec(tm, width, T):
    return pl.BlockSpec((HALO, width), lambda i: (jnp.minimum((i + 1) * (tm // HALO), T // HALO - 1), 0))


def _shifted(win, off, rows):
    if off % SUBLANES == 0:
        return win[off:off + rows]
    n = win.shape[0]
    return pltpu.roll(win, (n - off) % n, 0)[:rows]


def _rowsum8(x):
    acc = x[0:SUBLANES]
    for q in range(1, x.shape[0] // SUBLANES):
        acc = acc + x[q * SUBLANES:(q + 1) * SUBLANES]
    return acc


def _conv_loops(tm, D, per_block):
    def chunk(r, carry):
        t0 = pl.multiple_of(r * CONV_ROWS, CONV_ROWS)
        for lb in range(D // LANES):
            per_block(t0, slice(lb * LANES, (lb + 1) * LANES))
        return carry

    lax.fori_loop(0, tm // CONV_ROWS, chunk, 0)


def gateconv_fwd(bcv, w, name, hosted=()):
    T, D3 = bcv.shape
    D = D3 // 3
    K = w.shape[0]
    tm = _tile(T, TOKEN_TILE)

    def body(x_ref, halo_ref, w_ref, y_ref, pad_ref):
        i = pl.program_id(0)
        pad_ref[HALO:, :] = x_ref[:, D:2 * D] * x_ref[:, 2 * D:]
        pad_ref[:HALO, :] = jnp.where(i > 0, halo_ref[:, D:2 * D] * halo_ref[:, 2 * D:], 0.0)

        def block(t0, ls):
            win = pad_ref[pl.ds(t0, CONV_ROWS + HALO), ls]
            acc = jnp.zeros((CONV_ROWS, LANES), F32)
            for k in range(K):
                acc = acc + w_ref[k:k + 1, ls] * _shifted(win, HALO - (K - 1) + k, CONV_ROWS)
            y_ref[pl.ds(t0, CONV_ROWS), ls] = (x_ref[pl.ds(t0, CONV_ROWS), ls] * acc).astype(y_ref.dtype)

        _conv_loops(tm, D, block)

    (y,), xo = _call(
        body, name, (T // tm,),
        [pl.BlockSpec((tm, D3), lambda i: (i, 0)), _prev_halo_spec(tm, D3), pl.BlockSpec((K, D), lambda i: (0, 0))],
        [pl.BlockSpec((tm, D), lambda i: (i, 0))], [jax.ShapeDtypeStruct((T, D), BF16)],
        [bcv, bcv, w], ("parallel",), [pltpu.VMEM((tm + HALO, D), F32)], hosted=hosted)
    return y, xo


def gateconv_bwd(dy, bcv, w, name, hosted=()):
    T, D3 = bcv.shape
    D = D3 // 3
    K = w.shape[0]
    tm = _tile(T, TOKEN_TILE)
    nt = T // tm

    def body(dy_ref, dyn_ref, x_ref, xp_ref, xn_ref, w_ref, o_ref, dw_ref, cv_ref, dc_ref, wacc_ref):
        i = pl.program_id(0)
        cv_ref[HALO:, :] = x_ref[:, D:2 * D] * x_ref[:, 2 * D:]
        cv_ref[:HALO, :] = jnp.where(i > 0, xp_ref[:, D:2 * D] * xp_ref[:, 2 * D:], 0.0)
        dc_ref[:tm, :] = dy_ref[...] * x_ref[:, :D]
        dc_ref[tm:, :] = jnp.where(i < nt - 1, dyn_ref[...] * xn_ref[:, :D], 0.0)

        @pl.when(i == 0)
        def _():
            wacc_ref[...] = jnp.zeros_like(wacc_ref)

        def block(t0, ls):
            cwin = cv_ref[pl.ds(t0, CONV_ROWS + HALO), ls]
            dwin = dc_ref[pl.ds(t0, CONV_ROWS + HALO), ls]
            dcon = dwin[:CONV_ROWS]
            conv = jnp.zeros((CONV_ROWS, LANES), F32)
            dcv = jnp.zeros((CONV_ROWS, LANES), F32)
            for k in range(K):
                wk = w_ref[k:k + 1, ls]
                cs = _shifted(cwin, HALO - (K - 1) + k, CONV_ROWS)
                conv = conv + wk * cs
                dcv = dcv + wk * _shifted(dwin, (K - 1) - k, CONV_ROWS)
                wacc_ref[k * SUBLANES:(k + 1) * SUBLANES, ls] += _rowsum8(dcon * cs)
            rows = pl.ds(t0, CONV_ROWS)
            o_ref[rows, ls] = (dy_ref[rows, ls] * conv).astype(o_ref.dtype)
            o_ref[rows, pl.ds(D + ls.start, LANES)] = (dcv * x_ref[rows, pl.ds(2 * D + ls.start, LANES)]).astype(o_ref.dtype)
            o_ref[rows, pl.ds(2 * D + ls.start, LANES)] = (dcv * x_ref[rows, pl.ds(D + ls.start, LANES)]).astype(o_ref.dtype)

        _conv_loops(tm, D, block)

        @pl.when(i == nt - 1)
        def _():
            for k in range(K):
                dw_ref[k:k + 1, :] = jnp.sum(wacc_ref[k * SUBLANES:(k + 1) * SUBLANES, :], axis=0, keepdims=True)

    (dx, dw), xo = _call(
        body, name, (nt,),
        [pl.BlockSpec((tm, D), lambda i: (i, 0)), _next_halo_spec(tm, D, T),
         pl.BlockSpec((tm, D3), lambda i: (i, 0)), _prev_halo_spec(tm, D3), _next_halo_spec(tm, D3, T),
         pl.BlockSpec((K, D), lambda i: (0, 0))],
        [pl.BlockSpec((tm, D3), lambda i: (i, 0)), pl.BlockSpec((K, D), lambda i: (0, 0))],
        [jax.ShapeDtypeStruct((T, D3), BF16), jax.ShapeDtypeStruct((K, D), F32)],
        [dy, dy, bcv, bcv, bcv, w], ("arbitrary",),
        [pltpu.VMEM((tm + HALO, D), F32), pltpu.VMEM((tm + HALO, D), F32), pltpu.VMEM((K * SUBLANES, D), F32)],
        hosted=hosted)
    return dx, dw, xo


def bconv_fwd(u, w, b_conv, ln_g, ln_b, name, hosted=()):
    T, D2 = u.shape
    D = D2 // 2
    K = w.shape[0]
    tm = _tile(T, TOKEN_TILE)

    def body(u_ref, halo_ref, w_ref, bc_ref, g_ref, b_ref, cu_ref, s_ref, pad_ref):
        i = pl.program_id(0)
        pad_ref[HALO:, :] = u_ref[:, :D] * _sigmoid(u_ref[:, D:])
        pad_ref[:HALO, :] = jnp.where(i > 0, halo_ref[:, :D] * _sigmoid(halo_ref[:, D:]), 0.0)

        def block(t0, ls):
            win = pad_ref[pl.ds(t0, CONV_ROWS + HALO), ls]
            acc = jnp.zeros((CONV_ROWS, LANES), F32)
            for k in range(K):
                acc = acc + w_ref[k:k + 1, ls] * _shifted(win, HALO - (K - 1) + k, CONV_ROWS)
            cu_ref[pl.ds(t0, CONV_ROWS), ls] = acc + bc_ref[:, ls]

        _conv_loops(tm, D, block)
        cu = cu_ref[...]
        mu = jnp.mean(cu, axis=-1, keepdims=True)
        xc = cu - mu
        rstd = lax.rsqrt(jnp.mean(xc * xc, axis=-1, keepdims=True) + LN_EPS)
        ln = xc * rstd * g_ref[...] + b_ref[...]
        s_ref[...] = (ln * _sigmoid(ln)).astype(s_ref.dtype)

    vec = pl.BlockSpec((1, D), lambda i: (0, 0))
    row = pl.BlockSpec((tm, D), lambda i: (i, 0))
    (cu, s), xo = _call(
        body, name, (T // tm,),
        [pl.BlockSpec((tm, D2), lambda i: (i, 0)), _prev_halo_spec(tm, D2), pl.BlockSpec((K, D), lambda i: (0, 0)), vec, vec, vec],
        [row, row], [jax.ShapeDtypeStruct((T, D), F32), jax.ShapeDtypeStruct((T, D), BF16)],
        [u, u, w, b_conv, ln_g, ln_b], ("parallel",), [pltpu.VMEM((tm + HALO, D), F32)], hosted=hosted)
    return cu, s, xo


def pw2_ln_bwd(dy, w, cu, ln_g, ln_b, name, hosted=()):
    T, D = cu.shape
    tm = _tile(T, TOKEN_TILE)

    def body(dy_ref, w_ref, cu_ref, g_ref, b_ref, dcu_ref, dg_ref, db_ref, dbc_ref, dbo_ref):
        i = pl.program_id(0)
        dy_ = dy_ref[...]
        ds = lax.dot_general(dy_.astype(BF16), w_ref[0], _NT, preferred_element_type=F32)
        cu_ = cu_ref[...]
        mu = jnp.mean(cu_, axis=-1, keepdims=True)
        xc = cu_ - mu
        rstd = lax.rsqrt(jnp.mean(xc * xc, axis=-1, keepdims=True) + LN_EPS)
        xh = xc * rstd
        ln = xh * g_ref[...] + b_ref[...]
        sg = _sigmoid(ln)
        dl = ds * (sg * (1.0 + ln * (1.0 - sg)))
        dxh = dl * g_ref[...]
        dcu = rstd * (dxh - jnp.mean(dxh, axis=-1, keepdims=True) - xh * jnp.mean(dxh * xh, axis=-1, keepdims=True))
        dcu_ref[...] = dcu
        pg = jnp.sum(dl * xh, axis=0, keepdims=True)
        pb = jnp.sum(dl, axis=0, keepdims=True)
        pc = jnp.sum(dcu, axis=0, keepdims=True)
        po = jnp.sum(dy_, axis=0, keepdims=True)

        @pl.when(i == 0)
        def _():
            dg_ref[...] = pg
            db_ref[...] = pb
            dbc_ref[...] = pc
            dbo_ref[...] = po

        @pl.when(i > 0)
        def _():
            dg_ref[...] += pg
            db_ref[...] += pb
            dbc_ref[...] += pc
            dbo_ref[...] += po

    vec = pl.BlockSpec((1, D), lambda i: (0, 0))
    row = pl.BlockSpec((tm, D), lambda i: (i, 0))
    vshape = jax.ShapeDtypeStruct((1, D), F32)
    outs, xo = _call(
        body, name, (T // tm,), [row, pl.BlockSpec((1, D, D), lambda i: (0, 0, 0)), row, vec, vec], [row, vec, vec, vec, vec],
        [jax.ShapeDtypeStruct((T, D), F32), vshape, vshape, vshape, vshape], [dy, w, cu, ln_g, ln_b], ("arbitrary",),
        hosted=hosted)
    return (*outs, xo)


def bconv_bwd(dcu, u, w, name, hosted=()):
    T, D2 = u.shape
    D = D2 // 2
    K = w.shape[0]
    tm = _tile(T, TOKEN_TILE)
    nt = T // tm

    def body(dc_ref, dcn_ref, u_ref, up_ref, w_ref, du_ref, dw_ref, db_ref, glu_ref, dpad_ref, dglu_ref, wacc_ref):
        i = pl.program_id(0)
        glu_ref[HALO:, :] = u_ref[:, :D] * _sigmoid(u_ref[:, D:])
        glu_ref[:HALO, :] = jnp.where(i > 0, up_ref[:, :D] * _sigmoid(up_ref[:, D:]), 0.0)
        dpad_ref[:tm, :] = dc_ref[...]
        dpad_ref[tm:, :] = jnp.where(i < nt - 1, dcn_ref[...], 0.0)

        @pl.when(i == 0)
        def _():
            wacc_ref[...] = jnp.zeros_like(wacc_ref)

        def block(t0, ls):
            gwin = glu_ref[pl.ds(t0, CONV_ROWS + HALO), ls]
            dwin = dpad_ref[pl.ds(t0, CONV_ROWS + HALO), ls]
            dcur = dwin[:CONV_ROWS]
            dglu = jnp.zeros((CONV_ROWS, LANES), F32)
            for k in range(K):
                dglu = dglu + w_ref[k:k + 1, ls] * _shifted(dwin, (K - 1) - k, CONV_ROWS)
                gs = _shifted(gwin, HALO - (K - 1) + k, CONV_ROWS)
                wacc_ref[k * SUBLANES:(k + 1) * SUBLANES, ls] += _rowsum8(dcur * gs)
            dglu_ref[pl.ds(t0, CONV_ROWS), ls] = dglu

        _conv_loops(tm, D, block)
        dglu = dglu_ref[...]
        a = u_ref[:, :D]
        sg = _sigmoid(u_ref[:, D:])
        da = dglu * sg
        dg = dglu * a * (sg * (1.0 - sg))
        du_ref[:, :D] = da.astype(du_ref.dtype)
        du_ref[:, D:] = dg.astype(du_ref.dtype)
        pa = jnp.sum(da, axis=0, keepdims=True)
        pg = jnp.sum(dg, axis=0, keepdims=True)

        @pl.when(i == 0)
        def _():
            db_ref[:, :D] = pa
            db_ref[:, D:] = pg

        @pl.when(i > 0)
        def _():
            db_ref[:, :D] += pa
            db_ref[:, D:] += pg

        @pl.when(i == nt - 1)
        def _():
            for k in range(K):
                dw_ref[k:k + 1, :] = jnp.sum(wacc_ref[k * SUBLANES:(k + 1) * SUBLANES, :], axis=0, keepdims=True)

    (du, dw, db), xo = _call(
        body, name, (nt,),
        [pl.BlockSpec((tm, D), lambda i: (i, 0)), _next_halo_spec(tm, D, T),
         pl.BlockSpec((tm, D2), lambda i: (i, 0)), _prev_halo_spec(tm, D2), pl.BlockSpec((K, D), lambda i: (0, 0))],
        [pl.BlockSpec((tm, D2), lambda i: (i, 0)), pl.BlockSpec((K, D), lambda i: (0, 0)), pl.BlockSpec((1, D2), lambda i: (0, 0))],
        [jax.ShapeDtypeStruct((T, D2), BF16), jax.ShapeDtypeStruct((K, D), F32), jax.ShapeDtypeStruct((1, D2), F32)],
        [dcu, dcu, u, u, w], ("arbitrary",),
        [pltpu.VMEM((tm + HALO, D), F32), pltpu.VMEM((tm + HALO, D), F32), pltpu.VMEM((tm, D), F32),
         pltpu.VMEM((K * SUBLANES, D), F32)], hosted=hosted)
    return du, dw, db, xo


def mm_cols(a, w, bias, name, hosted=()):
    T, K = a.shape
    S, _, n = w.shape
    tm = _tile(T, WIDE_TOKEN_TILE)

    def body(*refs):
        a_ref, w_ref = refs[:2]
        o_ref = refs[-1]
        acc = jnp.dot(a_ref[...], w_ref[...], preferred_element_type=F32)
        if bias is not None:
            acc = acc + refs[2][...]
        o_ref[...] = acc

    in_specs = [pl.BlockSpec((tm, K), lambda s, i: (i, 0)), pl.BlockSpec((None, K, n), lambda s, i: (s, 0, 0))]
    args = [a, w]
    if bias is not None:
        in_specs.append(pl.BlockSpec((1, n), lambda s, i: (0, s)))
        args.append(bias)
    (out,), xo = _call(body, name, (S, T // tm), in_specs, [pl.BlockSpec((tm, n), lambda s, i: (i, s))],
                       [jax.ShapeDtypeStruct((T, S * n), F32)], args, ("parallel", "parallel"), hosted=hosted)
    return out, xo


def _load_once(pairs, sems):
    cps = [pltpu.make_async_copy(src, dst, sems.at[k]) for k, (src, dst) in enumerate(pairs)]
    for cp in cps:
        cp.start()
    for cp in cps:
        cp.wait()


def ffn_fwd(h, gain, wg, wu, wd, name, hosted=()):
    T, D = h.shape
    S, f, _ = wg.shape
    tm = _tile(T, WIDE_TOKEN_TILE)
    rc = tm // FFN_ROW_CHUNKS
    chunks = [slice(r * rc, (r + 1) * rc) for r in range(FFN_ROW_CHUNKS)]
    wd_arrives = isinstance(wd, _Exchange)
    weights = [wg, wu] if wd_arrives else [wg, wu, wd]
    hosted = ([wd] if wd_arrives else []) + list(hosted)

    def body(h_ref, gain_ref, *refs):
        nw = len(weights)
        wg_hbm, wu_hbm = refs[:2]
        wd_hbm = refs[-1] if wd_arrives else refs[2]
        n_ref, g_ref, u_ref, gu_ref, o_ref, wg_v, wu_v, wd_v, sems = refs[nw:nw + 9]
        i, s = pl.program_id(0), pl.program_id(1)

        @pl.when((i == 0) & (s == 0))
        def _():
            _load_once([(wg_hbm, wg_v), (wu_hbm, wu_v), (wd_hbm, wd_v)], sems)

        @pl.when(s == 0)
        def _():
            x = h_ref[...]
            r = lax.rsqrt(jnp.mean(x * x, axis=-1, keepdims=True) + RMS_EPS)
            n_ref[...] = (x * r * gain_ref[...]).astype(n_ref.dtype)

        parts = []
        for rows in chunks:
            a = n_ref[rows, :]
            g = lax.dot_general(a, wg_v[s], _NT, preferred_element_type=F32)
            u = lax.dot_general(a, wu_v[s], _NT, preferred_element_type=F32)
            gu = (g * _sigmoid(g) * u).astype(gu_ref.dtype)
            g_ref[rows, :] = g.astype(g_ref.dtype)
            u_ref[rows, :] = u.astype(u_ref.dtype)
            gu_ref[rows, :] = gu
            parts.append(jnp.dot(gu, wd_v[s], preferred_element_type=F32))

        @pl.when(s == 0)
        def _():
            for rows, part in zip(chunks, parts):
                o_ref[rows, :] = h_ref[rows, :] + part

        @pl.when(s > 0)
        def _():
            for rows, part in zip(chunks, parts):
                o_ref[rows, :] += part

    row = pl.BlockSpec((tm, D), lambda i, s: (i, 0))
    seg = pl.BlockSpec((None, tm, f), lambda i, s: (s, i, 0))
    hbm = pl.BlockSpec(memory_space=pl.ANY)
    segs = jax.ShapeDtypeStruct((S, T, f), BF16)
    outs, xo = _call(
        body, name, (T // tm, S),
        [pl.BlockSpec((tm, D), lambda i, s: (i, 0), pipeline_mode=pl.Buffered(1)), pl.BlockSpec((1, D), lambda i, s: (0, 0))]
        + [hbm] * len(weights), [row, seg, seg, seg, row],
        [jax.ShapeDtypeStruct((T, D), BF16), segs, segs, segs, jax.ShapeDtypeStruct((T, D), F32)],
        [h, gain] + weights, ("arbitrary", "arbitrary"),
        [pltpu.VMEM((S, f, D), BF16), pltpu.VMEM((S, f, D), BF16), pltpu.VMEM((S, f, D), BF16), pltpu.SemaphoreType.DMA((3,))],
        hosted=hosted)
    return (*outs, xo)


def ffn_bwd(dy, h, gain, g, u, wd, wg, wu, name, hosted=()):
    T, D = h.shape
    S, f, _ = wg.shape
    tm = _tile(T, TOKEN_TILE)
    nt = T // tm
    rc = tm // FFN_ROW_CHUNKS
    chunks = [slice(r * rc, (r + 1) * rc) for r in range(FFN_ROW_CHUNKS)]

    def body(dy_ref, h_ref, gain_ref, g_ref, u_ref, wd_hbm, wg_hbm, wu_hbm, dg_ref, du_ref, dh_ref, dgain_ref,
             wd_v, wg_v, wu_v, dyb_ref, sems):
        i, s = pl.program_id(0), pl.program_id(1)

        @pl.when((i == 0) & (s == 0))
        def _():
            _load_once([(wd_hbm, wd_v), (wg_hbm, wg_v), (wu_hbm, wu_v)], sems)

        @pl.when(s == 0)
        def _():
            dyb_ref[...] = dy_ref[...].astype(dyb_ref.dtype)

        parts = []
        for rows in chunks:
            dgu = lax.dot_general(dyb_ref[rows, :], wd_v[s], _NT, preferred_element_type=F32)
            gv = g_ref[rows, :].astype(F32)
            sg = _sigmoid(gv)
            dg = (dgu * u_ref[rows, :].astype(F32) * (sg * (1.0 + gv * (1.0 - sg)))).astype(dg_ref.dtype)
            du = (dgu * (gv * sg)).astype(du_ref.dtype)
            dg_ref[rows, :] = dg
            du_ref[rows, :] = du
            parts.append(jnp.dot(dg, wg_v[s], preferred_element_type=F32)
                         + jnp.dot(du, wu_v[s], preferred_element_type=F32))

        @pl.when(s == 0)
        def _():
            for rows, part in zip(chunks, parts):
                dh_ref[rows, :] = part

        @pl.when(s > 0)
        def _():
            for rows, part in zip(chunks, parts):
                dh_ref[rows, :] += part

        @pl.when(s == S - 1)
        def _():
            dn = dh_ref[...]
            x = h_ref[...]
            r = lax.rsqrt(jnp.mean(x * x, axis=-1, keepdims=True) + RMS_EPS)
            xhat = x * r
            dxhat = dn * gain_ref[...]
            dh_ref[...] = dy_ref[...] + r * (dxhat - xhat * jnp.mean(dxhat * xhat, axis=-1, keepdims=True))
            pg = jnp.sum(dn * xhat, axis=0, keepdims=True)

            @pl.when(i == 0)
            def _():
                dgain_ref[...] = pg

            @pl.when(i > 0)
            def _():
                dgain_ref[...] += pg

    row = pl.BlockSpec((tm, D), lambda i, s: (i, 0))
    vec = pl.BlockSpec((1, D), lambda i, s: (0, 0))
    seg = pl.BlockSpec((None, tm, f), lambda i, s: (s, i, 0))
    hbm = pl.BlockSpec(memory_space=pl.ANY)
    segs = jax.ShapeDtypeStruct((S, T, f), BF16)
    outs, xo = _call(
        body, name, (nt, S),
        [row, row, vec, seg, seg, hbm, hbm, hbm], [seg, seg, row, vec],
        [segs, segs, jax.ShapeDtypeStruct((T, D), F32), jax.ShapeDtypeStruct((1, D), F32)],
        [dy, h, gain, g, u, wd, wg, wu], ("arbitrary", "arbitrary"),
        [pltpu.VMEM((S, f, D), BF16), pltpu.VMEM((S, f, D), BF16), pltpu.VMEM((S, f, D), BF16),
         pltpu.VMEM((tm, D), BF16), pltpu.SemaphoreType.DMA((3,))], hosted=hosted)
    return (*outs, xo)


def mm_rows(a, w, res, bias, name, hosted=()):
    S, T, k = a.shape
    N = w.shape[-1]
    tm = _tile(T, TOKEN_TILE)

    def body(*refs):
        a_ref, w_ref, r_ref = refs[:3]
        o_ref = refs[-1]
        s = pl.program_id(1)
        acc = jnp.dot(a_ref[...], w_ref[...], preferred_element_type=F32)

        @pl.when(s == 0)
        def _():
            base = r_ref[...]
            if bias is not None:
                base = base + refs[3][...]
            o_ref[...] = base + acc

        @pl.when(s > 0)
        def _():
            o_ref[...] += acc

    in_specs = [pl.BlockSpec((None, tm, k), lambda i, s: (s, i, 0)),
                pl.BlockSpec((None, k, N), lambda i, s: (s, 0, 0)),
                pl.BlockSpec((tm, N), lambda i, s: (i, 0))]
    args = [a, w, res]
    if bias is not None:
        in_specs.append(pl.BlockSpec((1, N), lambda i, s: (0, 0)))
        args.append(bias)
    (out,), xo = _call(body, name, (T // tm, S), in_specs, [pl.BlockSpec((tm, N), lambda i, s: (i, 0))],
                       [jax.ShapeDtypeStruct((T, N), F32)], args, ("parallel", "arbitrary"), hosted=hosted)
    return out, xo


_NT = (((1,), (1,)), ((), ()))
_TN = (((0,), (0,)), ((), ()))


def nt_rows(dy, w, want_colsum, name, hosted=()):
    T, N = dy.shape
    S, k, _ = w.shape
    tm = _tile(T, TOKEN_TILE)

    def body(dy_ref, w_ref, o_ref, *rest):
        i, s = pl.program_id(0), pl.program_id(1)
        d = dy_ref[...]
        o_ref[...] = lax.dot_general(d.astype(BF16), w_ref[...], _NT, preferred_element_type=F32)
        if want_colsum:
            cs_ref = rest[0]
            part = jnp.sum(d, axis=0, keepdims=True)

            @pl.when((i == 0) & (s == 0))
            def _():
                cs_ref[...] = part

            @pl.when((i > 0) & (s == 0))
            def _():
                cs_ref[...] += part

    out_specs = [pl.BlockSpec((None, tm, k), lambda i, s: (s, i, 0))]
    out_shape = [jax.ShapeDtypeStruct((S, T, k), F32)]
    if want_colsum:
        out_specs.append(pl.BlockSpec((1, N), lambda i, s: (0, 0)))
        out_shape.append(jax.ShapeDtypeStruct((1, N), F32))
    outs, xo = _call(
        body, name, (T // tm, S),
        [pl.BlockSpec((tm, N), lambda i, s: (i, 0)), pl.BlockSpec((None, k, N), lambda i, s: (s, 0, 0))],
        out_specs, out_shape, [dy, w], ("arbitrary", "arbitrary"), hosted=hosted)
    return (*outs, xo)


def nt_cols_rms(dy, w, h, gain, dres, name, hosted=()):
    T, K = h.shape
    S, _, n = w.shape
    tm = _tile(T, TOKEN_TILE)

    def body(dy_ref, w_ref, h_ref, gain_ref, dres_ref, dh_ref, dgain_ref):
        i = pl.program_id(0)
        dn = None
        for s in range(S):
            part = lax.dot_general(dy_ref[:, s * n:(s + 1) * n], w_ref[s], _NT, preferred_element_type=F32)
            dn = part if dn is None else dn + part
        x = h_ref[...]
        r = lax.rsqrt(jnp.mean(x * x, axis=-1, keepdims=True) + RMS_EPS)
        xhat = x * r
        dxhat = dn * gain_ref[...]
        dh_ref[...] = dres_ref[...] + r * (dxhat - xhat * jnp.mean(dxhat * xhat, axis=-1, keepdims=True))
        pg = jnp.sum(dn * xhat, axis=0, keepdims=True)

        @pl.when(i == 0)
        def _():
            dgain_ref[...] = pg

        @pl.when(i > 0)
        def _():
            dgain_ref[...] += pg

    row = pl.BlockSpec((tm, K), lambda i: (i, 0))
    vec = pl.BlockSpec((1, K), lambda i: (0, 0))
    (dh, dgain), xo = _call(
        body, name, (T // tm,),
        [pl.BlockSpec((tm, S * n), lambda i: (i, 0)), pl.BlockSpec((S, K, n), lambda i: (0, 0, 0)), row, vec, row],
        [row, vec], [jax.ShapeDtypeStruct((T, K), F32), jax.ShapeDtypeStruct((1, K), F32)],
        [dy, w, h, gain, dres], ("arbitrary",), hosted=hosted)
    return dh, dgain, xo


def tn_grad(a, dy, S, a_by_seg, name, hosted=()):
    T = dy.shape[0] if dy.ndim == 2 else dy.shape[1]
    tt = _tile(T, GRAD_TOKEN_TILE)
    if a_by_seg:
        R = a.shape[1] // S if a.ndim == 2 else a.shape[2]
        C = dy.shape[1]
        a_spec = pl.BlockSpec((tt, R), lambda s, t: (t, s)) if a.ndim == 2 else pl.BlockSpec((None, tt, R), lambda s, t: (s, t, 0))
        b_spec = pl.BlockSpec((tt, C), lambda s, t: (t, 0))
    else:
        R = a.shape[1]
        C = dy.shape[1] // S if dy.ndim == 2 else dy.shape[2]
        a_spec = pl.BlockSpec((tt, R), lambda s, t: (t, 0))
        b_spec = pl.BlockSpec((tt, C), lambda s, t: (t, s)) if dy.ndim == 2 else pl.BlockSpec((None, tt, C), lambda s, t: (s, t, 0))
    Rh = R // 2
    nt = T // tt

    def body(a_ref, b_ref, o_ref, acc_ref):
        t = pl.program_id(1)
        part = lax.dot_general(a_ref[...], b_ref[...].astype(BF16), _TN, preferred_element_type=F32)

        @pl.when(t == 0)
        def _():
            acc_ref[...] = part

        @pl.when(t > 0)
        def _():
            acc_ref[...] += part

        @pl.when(t == nt - 1)
        def _():
            o_ref[0] = acc_ref[:Rh, :].astype(o_ref.dtype)
            o_ref[1] = acc_ref[Rh:, :].astype(o_ref.dtype)

    (gh,), xo = _call(
        body, name, (S, nt), [a_spec, b_spec], [pl.BlockSpec((2, None, Rh, C), lambda s, t: (0, s, 0, 0))],
        [jax.ShapeDtypeStruct((2, S, Rh, C), BF16)], [a, dy], ("parallel", "arbitrary"), [pltpu.VMEM((R, C), F32)],
        hosted=hosted)
    return gh, xo


def tn_grad_square(a, dy, S, name, hosted=()):
    T, K = a.shape
    N = dy.shape[1]
    tt = _tile(T, GRAD_TOKEN_TILE)
    nt = T // tt
    Rh = K // S // 2

    def body(a_ref, b_ref, o_ref, acc_ref):
        t = pl.program_id(0)
        part = lax.dot_general(a_ref[...], b_ref[...].astype(BF16), _TN, preferred_element_type=F32)

        @pl.when(t == 0)
        def _():
            acc_ref[...] = part

        @pl.when(t > 0)
        def _():
            acc_ref[...] += part

        @pl.when(t == nt - 1)
        def _():
            for s in range(S):
                for hf in range(2):
                    r0 = (2 * s + hf) * Rh
                    o_ref[hf, s] = acc_ref[r0:r0 + Rh, :].astype(o_ref.dtype)

    (gh,), xo = _call(
        body, name, (nt,), [pl.BlockSpec((tt, K), lambda t: (t, 0)), pl.BlockSpec((tt, N), lambda t: (t, 0))],
        [pl.BlockSpec((2, S, Rh, N), lambda t: (0, 0, 0, 0))], [jax.ShapeDtypeStruct((2, S, Rh, N), BF16)],
        [a, dy], ("arbitrary",), [pltpu.VMEM((K, N), F32)], hosted=hosted)
    return gh, xo


def _place():
    x, y, c = lax.axis_index("x"), lax.axis_index("y"), lax.axis_index("c")
    chips = [(1 - x, y), (x, 1 - y), (1 - x, 1 - y)]
    return x, y, c, chips


def _any_specs(n):
    return [pl.BlockSpec(memory_space=pl.ANY)] * n


def _remote(src, dst, send_sem, recv_sem, dev):
    return pltpu.make_async_remote_copy(src_ref=src, dst_ref=dst, send_sem=send_sem, recv_sem=recv_sem,
                                        device_id=dev, device_id_type=MESH)


def small_allreduce(v, name, hosted=()):
    rows, W = v.shape

    def body(v_ref, o_ref, sib_ref, pair_ref, chips_ref, send_sems, recv_sems):
        x, y, c, chips = _place()
        me = 2 * x + y
        swap = _remote(v_ref, sib_ref, send_sems.at[3], recv_sems.at[3], (x, y, 1 - c))
        swap.start()
        swap.wait()
        mine, other = v_ref[...], sib_ref[...]
        pair_ref[...] = jnp.where(c == 0, mine, other) + jnp.where(c == 0, other, mine)
        sends = []
        for j, (px, py) in enumerate(chips):
            cp = _remote(pair_ref, chips_ref.at[me], send_sems.at[j], recv_sems.at[j], (px, py, c))
            cp.start()
            sends.append(cp)
        chips_ref[me] = pair_ref[...]
        for j, (px, py) in enumerate(chips):
            blk = chips_ref.at[2 * px + py]
            _remote(blk, blk, send_sems.at[j], recv_sems.at[j], (px, py, c)).wait_recv()
        for cp in sends:
            cp.wait_send()
        o_ref[...] = (chips_ref[0] + chips_ref[1]) + (chips_ref[2] + chips_ref[3])

    vm = pl.BlockSpec(memory_space=pltpu.VMEM)
    (out,), xo = _call(
        body, name, (), [vm], [vm], [jax.ShapeDtypeStruct((rows, W), F32)], [v], (),
        [pltpu.VMEM((rows, W), F32), pltpu.VMEM((rows, W), F32), pltpu.VMEM((N_CHIPS, rows, W), F32),
         pltpu.SemaphoreType.DMA((4,)), pltpu.SemaphoreType.DMA((4,))], hosted=hosted)
    return out, xo


def _gather_p1_copies(srcs, bufs, ssem, rsem, base):
    x, y, c, chips = _place()
    me, sib = 2 * x + y, (x, y, 1 - c)
    sends, recvs = [], []
    for k, (src, buf) in enumerate(zip(srcs, bufs)):
        rh = src.shape[0] // 2
        s0 = base + 4 * k
        sends.append(_remote(src, buf.at[me], ssem.at[s0 + 3], rsem.at[s0 + 3], sib))
        recvs.append(_remote(buf.at[me], buf.at[me], ssem.at[s0 + 3], rsem.at[s0 + 3], sib))
        for j, (px, py) in enumerate(chips):
            sends.append(_remote(src.at[pl.ds(c * rh, rh)], buf.at[me, pl.ds(c * rh, rh)], ssem.at[s0 + j], rsem.at[s0 + j], (px, py, c)))
            blk = buf.at[2 * px + py, pl.ds(c * rh, rh)]
            recvs.append(_remote(blk, blk, ssem.at[s0 + j], rsem.at[s0 + j], (px, py, c)))
    return sends, recvs


def _gather_p2_copies(bufs, ssem, rsem, base):
    x, y, c, chips = _place()
    sib = (x, y, 1 - c)
    sends, recvs = [], []
    for k, buf in enumerate(bufs):
        rh = buf.shape[1] // 2
        for j, (px, py) in enumerate(chips):
            s0 = base + 3 * k + j
            blk = buf.at[2 * px + py, pl.ds(c * rh, rh)]
            sends.append(_remote(blk, blk, ssem.at[s0], rsem.at[s0], sib))
            got = buf.at[2 * px + py, pl.ds((1 - c) * rh, rh)]
            recvs.append(_remote(got, got, ssem.at[s0], rsem.at[s0], sib))
    return sends, recvs


def _gathered_shape(s):
    return jax.ShapeDtypeStruct((N_CHIPS,) + s.shape, s.dtype)


def gather_p1(shards):
    return _Exchange(shards, [_gathered_shape(s) for s in shards], {}, 4 * len(shards),
                     lambda xi, xo, ss, rs: _gather_p1_copies(xi, xo, ss, rs, 0))


def gather_p2(bufs):
    return _Exchange(bufs, [jax.ShapeDtypeStruct(b.shape, b.dtype) for b in bufs], {k: k for k in range(len(bufs))},
                     3 * len(bufs), lambda xi, xo, ss, rs: _gather_p2_copies(xo, ss, rs, 0))


def gather_first(whole, begun, name):
    nw, n = len(whole), len(whole) + len(begun)

    def body(*refs):
        ins, outs = refs[:n], refs[n:2 * n]
        ssem, rsem = refs[2 * n:]
        s1, r1 = _gather_p1_copies(ins, outs, ssem, rsem, 0)
        for cp in s1:
            cp.start()
        for cp in r1[:4 * nw]:
            cp.wait_recv()
        s2, r2 = _gather_p2_copies(outs[:nw], ssem, rsem, 4 * n)
        for cp in s2:
            cp.start()
        for cp in r1[4 * nw:] + r2:
            cp.wait_recv()
        for cp in s1 + s2:
            cp.wait_send()

    shards = list(whole) + list(begun)
    return pl.pallas_call(
        body, name=name, in_specs=_any_specs(n), out_specs=_any_specs(n),
        out_shape=[_gathered_shape(s) for s in shards],
        scratch_shapes=[pltpu.SemaphoreType.DMA((4 * n + 3 * nw,)), pltpu.SemaphoreType.DMA((4 * n + 3 * nw,))],
    )(*shards)


def gather_small(v):
    def copies(xi, xo, ssem, rsem):
        x, y, c, chips = _place()
        me, sib = 2 * x + y, (x, y, 1 - c)
        sends = [_remote(xi[0], xo[0].at[me], ssem.at[3], rsem.at[3], sib)]
        recvs = [_remote(xo[0].at[me], xo[0].at[me], ssem.at[3], rsem.at[3], sib)]
        for j, (px, py) in enumerate(chips):
            sends.append(_remote(xi[0], xo[0].at[me], ssem.at[j], rsem.at[j], (px, py, c)))
            blk = xo[0].at[2 * px + py]
            recvs.append(_remote(blk, blk, ssem.at[j], rsem.at[j], (px, py, c)))
        return sends, recvs

    return _Exchange([v], [_gathered_shape(v)], {}, 4, copies)


def run_exchanges(exchanges, name):
    return _call(lambda: None, name, (), [], [], [], [], (), hosted=exchanges)[1]


def sibling_halves(grads):
    def copies(xi, xo, ssem, rsem):
        x, y, c, _ = _place()
        sends = [_remote(xi[k].at[1 - c], xo[k], ssem.at[k], rsem.at[k], (x, y, 1 - c)) for k in range(len(grads))]
        return sends, sends

    return _Exchange(grads, [jax.ShapeDtypeStruct(g.shape[1:], g.dtype) for g in grads], {}, len(grads), copies)


def pair_sum(gh, recv, cidx, name):
    _, S, Rh, C = gh.shape

    def body(c_ref, a_ref, b_ref, o_ref):
        o_ref[...] = (a_ref[...].astype(F32) + b_ref[...].astype(F32)).astype(o_ref.dtype)

    return pl.pallas_call(
        body, name=name, out_shape=jax.ShapeDtypeStruct((S, Rh, C), BF16),
        grid_spec=pltpu.PrefetchScalarGridSpec(
            num_scalar_prefetch=1, grid=(S,),
            in_specs=[pl.BlockSpec((None, None, Rh, C), lambda s, c_ref: (c_ref[0], s, 0, 0)),
                      pl.BlockSpec((None, Rh, C), lambda s, c_ref: (s, 0, 0))],
            out_specs=pl.BlockSpec((None, Rh, C), lambda s, c_ref: (s, 0, 0))),
        compiler_params=_params(("parallel",)),
    )(cidx, gh, recv)


def scatter_p1(parts):
    def copies(xi, xo, ssem, rsem):
        x, y, c, chips = _place()
        me, sib = 2 * x + y, (x, y, 1 - c)
        sends, recvs = [], []
        for k in range(len(parts)):
            s0 = 4 * k
            sends.append(_remote(xi[k].at[me], xo[k].at[me, c], ssem.at[s0 + 3], rsem.at[s0 + 3], sib))
            own = xo[k].at[me, 1 - c]
            recvs.append(_remote(own, own, ssem.at[s0 + 3], rsem.at[s0 + 3], sib))
            for j, (px, py) in enumerate(chips):
                sends.append(_remote(xi[k].at[2 * px + py], xo[k].at[me, c], ssem.at[s0 + j], rsem.at[s0 + j], (px, py, c)))
                blk = xo[k].at[2 * px + py, c]
                recvs.append(_remote(blk, blk, ssem.at[s0 + j], rsem.at[s0 + j], (px, py, c)))
        return sends, recvs

    return _Exchange(parts, [jax.ShapeDtypeStruct((p.shape[0], 2) + p.shape[1:], p.dtype) for p in parts], {},
                     4 * len(parts), copies)


def scatter_p2(bufs):
    def copies(xi, xo, ssem, rsem):
        x, y, c, chips = _place()
        sib = (x, y, 1 - c)
        sends, recvs = [], []
        for k in range(len(bufs)):
            for j, (px, py) in enumerate(chips):
                s0 = 3 * k + j
                blk = xo[k].at[2 * px + py, c]
                sends.append(_remote(blk, blk, ssem.at[s0], rsem.at[s0], sib))
                got = xo[k].at[2 * px + py, 1 - c]
                recvs.append(_remote(got, got, ssem.at[s0], rsem.at[s0], sib))
        return sends, recvs

    return _Exchange(bufs, [jax.ShapeDtypeStruct(b.shape, b.dtype) for b in bufs], {k: k for k in range(len(bufs))},
                     3 * len(bufs), copies)


def _adamw_math(w, g, m, v):
    m = ADAM_B1 * m + (1.0 - ADAM_B1) * g
    v = ADAM_B2 * v + (1.0 - ADAM_B2) * (g * g)
    m_hat = m / (1.0 - ADAM_B1 ** ADAM_STEP)
    v_hat = v / (1.0 - ADAM_B2 ** ADAM_STEP)
    delta = -ADAM_LR * (m_hat / (jnp.sqrt(v_hat) + ADAM_EPS) + ADAM_WD * w)
    return delta, m, v


def adamw_reduce(w, m, v, buf, part, place, lyr, bases, name, hosted=()):
    L, R, C = w.shape
    Rh = R // 2
    rb = _tile(Rh, ROW_TILE, 2 * SUBLANES)
    nb = Rh // rb

    def body(place_ref, p_ref, b0, b1, b2, b3, w_ref, m_ref, v_ref, *rest):
        go_ref, d_ref, mo_ref, vo_ref = rest[-4:]
        mine = (place_ref[1] == pl.program_id(0))
        g = None
        for p, b in enumerate((b0, b1, b2, b3)):
            val = jnp.where(mine & (place_ref[0] == p), p_ref[...], b[...]).astype(F32)
            g = val if g is None else g + val
        d, mn, vn = _adamw_math(w_ref[...], g, m_ref[...], v_ref[...])
        go_ref[...] = g
        d_ref[...] = d
        mo_ref[...] = mn
        vo_ref[...] = vn

    def buf_spec(p):
        def idx(h, i, pr):
            own = (pr[0] == p) & (pr[1] == h)
            return (p, jnp.where(own, 1 - h, h), i, 0)
        return pl.BlockSpec((None, None, rb, C), idx)

    blk = pl.BlockSpec((None, rb, C), lambda h, i, pr: (lyr, h * nb + i, 0))
    in_specs = [pl.BlockSpec((None, rb, C), lambda h, i, pr: (pr[0], i, 0))] + [buf_spec(p) for p in range(N_CHIPS)] + [blk] * 3
    args = [part, buf, buf, buf, buf, w, m, v]
    aliases = {}
    if bases is not None:
        in_specs += [pl.BlockSpec(memory_space=pl.ANY)] * 4
        aliases = {len(args) + k: k for k in range(4)}
        args += list(bases)
    shp = jax.ShapeDtypeStruct((L, R, C), F32)
    return _call(body, name, (2, nb), in_specs, [blk] * 4, [shp] * 4, args, ("parallel", "parallel"),
                 hosted=hosted, prefetch=[place], own_aliases=aliases)


def small_update(gall, chip, entries, name):
    ne = len(entries)
    D = gall.shape[1]

    def body(chip_ref, gall_ref, *refs):
        ins, outs = refs[:3 * ne], refs[3 * ne:]
        ch = chip_ref[0]
        for e, (row0, kind, w, _, _) in enumerate(entries):
            r, width = w.shape

            def gsum(rs, cs):
                return gall_ref[rs, cs]

            if kind == "full":
                g = gsum(slice(row0, row0 + r), slice(0, D))
            elif kind == "cols":
                g = gsum(slice(row0, row0 + r), slice(0, width))
                for q in range(1, N_CHIPS):
                    g = jnp.where(ch == q, gsum(slice(row0, row0 + r), slice(q * width, (q + 1) * width)), g)
            else:
                per_row = D // width
                g = gsum(slice(row0, row0 + 1), slice(0, width))
                for q in range(1, N_CHIPS):
                    rr = row0 + q // per_row
                    cc = (q % per_row) * width
                    g = jnp.where(ch == q, gsum(slice(rr, rr + 1), slice(cc, cc + width)), g)
            d, mn, vn = _adamw_math(ins[3 * e][...], g, ins[3 * e + 1][...], ins[3 * e + 2][...])
            outs[4 * e][...] = g
            outs[4 * e + 1][...] = d
            outs[4 * e + 2][...] = mn
            outs[4 * e + 3][...] = vn

    vm = pl.BlockSpec(memory_space=pltpu.VMEM)
    args, out_shape = [], []
    for _, _, w, m, v in entries:
        args += [w, m, v]
        out_shape += [jax.ShapeDtypeStruct(w.shape, F32)] * 4
    return pl.pallas_call(
        body, name=name,
        in_specs=[pl.BlockSpec(memory_space=pltpu.SMEM), vm] + [vm] * (3 * ne),
        out_specs=[vm] * (4 * ne), out_shape=out_shape,
        compiler_params=pltpu.CompilerParams(vmem_limit_bytes=VMEM_LIMIT),
    )(chip, gall, *args)


def _pack_rows(items, width):
    rows, starts, at = [], [], 0
    for it in items:
        r = it.shape[0]
        pad = (-r) % SUBLANES
        starts.append(at)
        rows.append(it)
        if pad:
            rows.append(jnp.zeros((pad, width), F32))
        at += r + pad
    return jnp.concatenate(rows, axis=0), starts


def kernel(x, a_norm, a_w_in, a_conv, a_w_out, b_norm, b_w_pw1, b_b_pw1, b_conv, b_b_conv, b_ln_g, b_ln_b, b_w_pw2, b_b_pw2, ffn_norm, ffn_w_gate, ffn_w_up, ffn_w_down, final_norm, loss_target, m_a_norm, m_a_w_in, m_a_conv, m_a_w_out, m_b_norm, m_b_w_pw1, m_b_b_pw1, m_b_conv, m_b_b_conv, m_b_ln_g, m_b_ln_b, m_b_w_pw2, m_b_b_pw2, m_ffn_norm, m_ffn_w_gate, m_ffn_w_up, m_ffn_w_down, m_final_norm, v_a_norm, v_a_w_in, v_a_conv, v_a_w_out, v_b_norm, v_b_w_pw1, v_b_b_pw1, v_b_conv, v_b_b_conv, v_b_ln_g, v_b_ln_b, v_b_w_pw2, v_b_b_pw2, v_ffn_norm, v_ffn_w_gate, v_ffn_w_up, v_ffn_w_down, v_final_norm):
    T, D = x.shape[1], x.shape[2]
    Dq = D // N_CHIPS
    cx, cy, cc = lax.axis_index("x"), lax.axis_index("y"), lax.axis_index("c")
    chip = (2 * cx + cy).astype(jnp.int32).reshape(1)
    cidx = cc.astype(jnp.int32).reshape(1)
    h0 = x.reshape(T, D)
    tgt = loss_target.reshape(T, D)

    small_shards = [a_conv[0], b_norm, b_b_pw1.reshape(2, Dq), b_conv[0], b_b_conv, b_ln_g, b_ln_b, b_b_pw2]
    packed, st = _pack_rows(small_shards, Dq)

    tr = lambda t: jnp.swapaxes(t, 1, 2)
    w_gate, m_gate, v_gate = tr(ffn_w_gate), tr(m_ffn_w_gate), tr(v_ffn_w_gate)
    w_up, m_up, v_up = tr(ffn_w_up), tr(m_ffn_w_up), tr(v_ffn_w_up)
    bf = lambda t: t.astype(BF16)
    s_in, s_out, s_pw1, s_pw2 = bf(a_w_in[0]), bf(a_w_out[0]), bf(b_w_pw1[0]), bf(b_w_pw2[0])
    s_gate, s_up, s_down = [bf(w_gate[l]) for l in (0, 1)], [bf(w_up[l]) for l in (0, 1)], [bf(ffn_w_down[l]) for l in (0, 1)]

    g_in, g_out = gather_first([s_in], [s_out], "gather_first")
    n0, _ = rms_fwd(h0, a_norm, "rms_a")
    bcv, (gate0, g_out, sw) = mm_cols(n0, g_in, None, "mm_w_in",
                                      hosted=[gather_p1([s_gate[0]]), gather_p2([g_out]), gather_small(packed)])
    g_out = g_out.reshape(1, D, D)

    def whole(k, r):
        return jnp.transpose(sw[:, st[k]:st[k] + r, :], (1, 0, 2)).reshape(r, D)

    a_conv_f, b_norm_f = whole(0, 3), whole(1, 1)
    b_b_pw1_f = sw[:, st[2]:st[2] + 2, :].reshape(1, 2 * D)
    b_conv_f, b_b_conv_f, b_ln_g_f, b_ln_b_f, b_b_pw2_f = whole(3, b_conv.shape[1]), whole(4, 1), whole(5, 1), whole(6, 1), whole(7, 1)
    ya, (up0, gate0) = gateconv_fwd(bcv, a_conv_f, "gateconv_fwd", hosted=[gather_p1([s_up[0]]), gather_p2([gate0])])
    h1, (down0, up0) = mm_rows(ya[None], g_out, h0, None, "mm_w_out", hosted=[gather_p1([s_down[0]]), gather_p2([up0])])
    n1, fg0, fu0, gu0, h2, (down0, *later) = ffn_fwd(h1, ffn_norm[0:1], gate0, up0, gather_p2([down0]).awaited_first(), "ffn_fwd0",
                                                     hosted=[gather_p1([s_pw1, s_pw2, s_gate[1], s_up[1]])])
    n2, (g_pw1, g_pw2, gate1, up1) = rms_fwd(h2, b_norm_f, "rms_b", hosted=[gather_p2(later)])
    g_pw2 = g_pw2.reshape(1, D, D)
    ub, (down1,) = mm_cols(n2, g_pw1, b_b_pw1_f, "mm_pw1", hosted=[gather_p1([s_down[1]])])
    cu, sb, (down1,) = bconv_fwd(ub, b_conv_f, b_b_conv_f, b_ln_g_f, b_ln_b_f, "bconv_fwd", hosted=[gather_p2([down1])])
    h3, _ = mm_rows(sb[None], g_pw2, h2, b_b_pw2_f, "mm_pw2")
    n3, fg1, fu1, gu1, h4, _ = ffn_fwd(h3, ffn_norm[1:2], gate1, up1, down1, "ffn_fwd1")
    loss_part, dh4, d_final = loss_head(h4, final_norm.reshape(1, D), tgt, "loss_head")

    place = jnp.concatenate([chip, cidx])

    def pair_sums(ghs, from_sib, tags):
        return [pair_sum(g, r, cidx, "pair_sum_" + t) for g, r, t in zip(ghs, from_sib, tags)]

    def upd(w, m, v, bufs, parts, tag, hosted=()):
        res, xo = None, []
        for lyr, (b, p) in enumerate(zip(bufs, parts)):
            res, xo_l = adamw_reduce(w, m, v, b, p, place, lyr, res, "adamw_%s%d" % (tag, lyr), hosted=hosted if lyr == 0 else ())
            xo += xo_l
        return res, xo

    dg1, du1, dh3, d_fn1, _ = ffn_bwd(dh4, h3, ffn_norm[1:2], fg1, fu1, down1, gate1, up1, "ffn_bwd1")
    gh_down1, _ = tn_grad(gu1, dh4, N_CHIPS, True, "tn_down1")
    gh_gate1, _ = tn_grad(dg1, n3, N_CHIPS, True, "tn_gate1")
    gh_up1, _ = tn_grad(du1, n3, N_CHIPS, True, "tn_up1")
    f1 = [gh_gate1, gh_up1, gh_down1]

    dcu, d_ln_g, d_ln_b, d_b_conv, d_b_pw2, sib_f1 = pw2_ln_bwd(dh3, g_pw2, cu, b_ln_g_f, b_ln_b_f, "pw2_ln_bwd",
                                                                hosted=[sibling_halves(f1)])
    p_f1 = pair_sums(f1, sib_f1, ["gate1", "up1", "down1"])
    gh_pw2, _ = tn_grad_square(sb, dh3, N_CHIPS, "tn_pw2")
    dub, d_bconv_w, d_b_pw1, buf_f1 = bconv_bwd(dcu, ub, b_conv_f, "bconv_bwd", hosted=[scatter_p1(p_f1)])
    gh_pw1, buf_f1 = tn_grad(n2, dub, N_CHIPS, False, "tn_pw1", hosted=[scatter_p2(buf_f1)])
    b_grp = [gh_pw1, gh_pw2]
    dh2, d_b_norm, sib_b = nt_cols_rms(dub, g_pw1, h2, b_norm_f, dh3, "nt_pw1", hosted=[sibling_halves(b_grp)])
    p_b = pair_sums(b_grp, sib_b, ["pw1", "pw2"])

    dg0, du0, dh1, d_fn0, buf_b = ffn_bwd(dh2, h1, ffn_norm[0:1], fg0, fu0, down0, gate0, up0, "ffn_bwd0", hosted=[scatter_p1(p_b)])
    gh_down0, buf_b = tn_grad(gu0, dh2, N_CHIPS, True, "tn_down0", hosted=[scatter_p2(buf_b)])
    gh_gate0, sib_down0 = tn_grad(dg0, n1, N_CHIPS, True, "tn_gate0", hosted=[sibling_halves([gh_down0])])
    p_down0 = pair_sums([gh_down0], sib_down0, ["down0"])
    gh_up0, (buf_down0, sib_gate0) = tn_grad(du0, n1, N_CHIPS, True, "tn_up0",
                                             hosted=[scatter_p1(p_down0), sibling_halves([gh_gate0])])
    p_gate0 = pair_sums([gh_gate0], [sib_gate0], ["gate0"])
    dya, (buf_down0, sib_up0) = nt_rows(dh1, g_out, False, "nt_w_out",
                                        hosted=[scatter_p2([buf_down0]), sibling_halves([gh_up0])])
    p_up0 = pair_sums([gh_up0], [sib_up0], ["up0"])
    gh_out, _ = tn_grad_square(ya, dh1, N_CHIPS, "tn_w_out")
    dbcv, d_aconv_w, (buf_gate0, sib_out) = gateconv_bwd(dya[0], bcv, a_conv_f, "gateconv_bwd",
                                                         hosted=[scatter_p1(p_gate0), sibling_halves([gh_out])])
    p_out = pair_sums([gh_out], [sib_out], ["out"])
    gh_in, (buf_up0, buf_gate0) = tn_grad(n0, dbcv, N_CHIPS, False, "tn_w_in",
                                          hosted=[scatter_p1(p_up0), scatter_p2([buf_gate0])])
    sib_in = run_exchanges([sibling_halves([gh_in])], "reduce_in_siblings")
    p_in = pair_sums([gh_in], sib_in, ["in"])
    grad_x, d_a_norm, (buf_in, buf_out, buf_up0) = nt_cols_rms(
        dbcv, g_in, h0, a_norm, dh1, "nt_w_in", hosted=[scatter_p1(p_in + p_out), scatter_p2([buf_up0])])
    p_f0 = [p_gate0[0], p_up0[0], p_down0[0]]

    d_ffn_norm = jnp.concatenate([d_fn0, d_fn1], axis=0)
    small_grads = [d_a_norm, d_aconv_w, d_b_norm, d_b_pw1.reshape(2, D), d_bconv_w, d_b_conv, d_ln_g, d_ln_b, d_b_pw2,
                   d_ffn_norm, d_final, jnp.broadcast_to(loss_part, (1, D))]
    gpacked, gs = _pack_rows(small_grads, D)
    gall, (buf_in, buf_out) = small_allreduce(gpacked, "allreduce_small_grads", hosted=[scatter_p2([buf_in, buf_out])])
    buf_a, p_a = [buf_in, buf_out], [p_in[0], p_out[0]]

    r_gate, _ = upd(w_gate, m_gate, v_gate, [buf_gate0, buf_f1[0]], [p_f0[0], p_f1[0]], "gate")
    r_up, _ = upd(w_up, m_up, v_up, [buf_up0, buf_f1[1]], [p_f0[1], p_f1[1]], "up")
    r_down, _ = upd(ffn_w_down, m_ffn_w_down, v_ffn_w_down, [buf_down0, buf_f1[2]], [p_f0[2], p_f1[2]], "down")
    r_gate, r_up = [tr(t) for t in r_gate], [tr(t) for t in r_up]
    r_pw1, _ = upd(b_w_pw1, m_b_w_pw1, v_b_w_pw1, [buf_b[0]], [p_b[0]], "pw1")
    r_pw2, _ = upd(b_w_pw2, m_b_w_pw2, v_b_w_pw2, [buf_b[1]], [p_b[1]], "pw2")
    r_in, _ = upd(a_w_in, m_a_w_in, v_a_w_in, [buf_a[0]], [p_a[0]], "w_in")
    r_out, _ = upd(a_w_out, m_a_w_out, v_a_w_out, [buf_a[1]], [p_a[1]], "w_out")
    entries = [
        (gs[0], "full", a_norm, m_a_norm, v_a_norm),
        (gs[1], "cols", a_conv[0], m_a_conv[0], v_a_conv[0]),
        (gs[2], "cols", b_norm, m_b_norm, v_b_norm),
        (gs[3], "flat2", b_b_pw1, m_b_b_pw1, v_b_b_pw1),
        (gs[4], "cols", b_conv[0], m_b_conv[0], v_b_conv[0]),
        (gs[5], "cols", b_b_conv, m_b_b_conv, v_b_b_conv),
        (gs[6], "cols", b_ln_g, m_b_ln_g, v_b_ln_g),
        (gs[7], "cols", b_ln_b, m_b_ln_b, v_b_ln_b),
        (gs[8], "cols", b_b_pw2, m_b_b_pw2, v_b_b_pw2),
        (gs[9], "full", ffn_norm, m_ffn_norm, v_ffn_norm),
        (gs[10], "full", final_norm.reshape(1, D), m_final_norm.reshape(1, D), v_final_norm.reshape(1, D)),
    ]
    so = small_update(gall, chip, entries, "small_update")
    sm = [so[4 * e:4 * e + 4] for e in range(len(entries))]

    def shaped(e, like):
        return [t.reshape(like.shape) for t in sm[e]]

    r_a_norm, r_a_conv, r_b_norm, r_b_b_pw1 = shaped(0, a_norm), shaped(1, a_conv), shaped(2, b_norm), shaped(3, b_b_pw1)
    r_b_conv, r_b_b_conv, r_b_ln_g, r_b_ln_b = shaped(4, b_conv), shaped(5, b_b_conv), shaped(6, b_ln_g), shaped(7, b_ln_b)
    r_b_b_pw2, r_ffn_norm, r_final = shaped(8, b_b_pw2), shaped(9, ffn_norm), shaped(10, final_norm)

    loss = gall[gs[11], 0]
    order =[r_a_norm, r_in, r_a_conv, r_out, r_b_norm, r_pw1, r_b_b_pw1, r_b_conv, r_b_b_conv, r_b_ln_g, r_b_ln_b,
             r_pw2, r_b_b_pw2, r_ffn_norm, r_gate, r_up, r_down, r_final]
    outs = [loss, grad_x.reshape(x.shape)]
    for field in range(4):
        outs += [r[field] for r in order]
    return tuple(outs)
```

```python
import functools

import jax
import jax.numpy as jnp
from jax import lax
from jax.experimental import pallas as pl
from jax.experimental.pallas import tpu as pltpu

RMS_EPS = 1e-6
LN_EPS = 1e-5
ADAM_LR = 0.001
ADAM_B1 = 0.9
ADAM_B2 = 0.999
ADAM_EPS = 1e-08
ADAM_WD = 0.01
ADAM_STEP = 10

N_CHIPS = 4
N_DEV = 8
LANES = 128
SUBLANES = 8
HALO = 32
CONV_ROWS = 64
TOKEN_TILE = 512
WIDE_TOKEN_TILE = 1024
GRAD_TOKEN_TILE = 2048
FFN_ROW_CHUNKS = 2
ROW_TILE = 256
VMEM_LIMIT = 56 * 1024 * 1024
MESH = pl.DeviceIdType.MESH
BF16 = jnp.bfloat16
F32 = jnp.float32


def _tile(n, pref, mult=SUBLANES):
    t = min(n, pref) // mult * mult
    while n % t:
        t -= mult
    return t


def _params(sem):
    return pltpu.CompilerParams(dimension_semantics=sem, vmem_limit_bytes=VMEM_LIMIT)


def _sigmoid(x):
    return 0.5 * jnp.tanh(0.5 * x) + 0.5


class _Exchange:
    def __init__(self, ins, outs, aliases, n_sems, copies, then=None):
        self.ins, self.outs, self.aliases, self.n_sems, self.copies = list(ins), list(outs), dict(aliases), n_sems, copies
        self.then = then
        self.early = False

    def awaited_first(self):
        self.early = True
        return self

    def start(self, xi, xo, ssem, rsem):
        for cp in self.copies(xi, xo, ssem, rsem)[0]:
            cp.start()

    def finish(self, xi, xo, ssem, rsem):
        sends, recvs = self.copies(xi, xo, ssem, rsem)
        for cp in recvs:
            cp.wait_recv()
        if self.then is not None:
            sends2, recvs2 = self.then(xi, xo, ssem, rsem)
            for cp in sends2:
                cp.start()
            for cp in recvs2:
                cp.wait_recv()
            sends = sends + sends2
        for cp in sends:
            cp.wait_send()


def _call(body, name, grid, in_specs, out_specs, out_shape, args, sem, scratch_shapes=(), hosted=(), prefetch=(),
          own_aliases=None):
    in_specs, out_specs, out_shape = list(in_specs), list(out_specs), list(out_shape)
    scratch_shapes, hosted, prefetch = list(scratch_shapes), list(hosted), list(prefetch)
    n_pre, n_in, n_out, n_scr = len(prefetch), len(args), len(out_shape), len(scratch_shapes)
    x_in = [a for ex in hosted for a in ex.ins]
    x_out = [o for ex in hosted for o in ex.outs]
    aliases = {n_pre + i: o for i, o in (own_aliases or {}).items()}
    at_in, at_out = n_pre + n_in, n_out
    for ex in hosted:
        for i, o in ex.aliases.items():
            aliases[at_in + i] = at_out + o
        at_in += len(ex.ins)
        at_out += len(ex.outs)
    sems = [pltpu.SemaphoreType.DMA((ex.n_sems,)) for ex in hosted for _ in range(2)]

    def wrapped(*refs):
        pre, refs = refs[:n_pre], refs[n_pre:]
        ins, xi = refs[:n_in], refs[n_in:n_in + len(x_in)]
        refs = refs[n_in + len(x_in):]
        outs, xo = refs[:n_out], refs[n_out:n_out + len(x_out)]
        refs = refs[n_out + len(x_out):]
        scr, sm = refs[:n_scr], refs[n_scr:]
        views, a, b = [], 0, 0
        for e, ex in enumerate(hosted):
            views.append((xi[a:a + len(ex.ins)], xo[b:b + len(ex.outs)], sm[2 * e], sm[2 * e + 1]))
            a += len(ex.ins)
            b += len(ex.outs)
        first = last = None
        for ax, g in enumerate(grid):
            f, l = pl.program_id(ax) == 0, pl.program_id(ax) == g - 1
            first, last = (f, l) if first is None else (first & f, last & l)

        def begin():
            for ex, v in zip(hosted, views):
                ex.start(*v)
            for ex, v in zip(hosted, views):
                if ex.early:
                    ex.finish(*v)

        def end():
            for ex, v in zip(hosted, views):
                if not ex.early:
                    ex.finish(*v)

        if hosted and grid:
            pl.when(first)(begin)
        elif hosted:
            begin()
        early_refs = [r for ex, v in zip(hosted, views) if ex.early for r in v[1]]
        body(*pre, *ins, *outs, *scr, *early_refs)
        if hosted and grid:
            pl.when(last)(end)
        elif hosted:
            end()

    hbm = pl.BlockSpec(memory_space=pl.ANY)
    all_in, all_out = in_specs + [hbm] * len(x_in), out_specs + [hbm] * len(x_out)
    kw = dict(name=name, out_shape=out_shape + x_out, input_output_aliases=aliases,
              compiler_params=_params(tuple("arbitrary" for _ in grid) if hosted else sem))
    if prefetch:
        kw["grid_spec"] = pltpu.PrefetchScalarGridSpec(num_scalar_prefetch=n_pre, grid=grid, in_specs=all_in,
                                                       out_specs=all_out, scratch_shapes=scratch_shapes + sems)
    else:
        kw.update(grid=grid, in_specs=all_in, out_specs=all_out, scratch_shapes=scratch_shapes + sems)
    res = pl.pallas_call(wrapped, **kw)(*prefetch, *args, *x_in)
    return list(res[:n_out]), list(res[n_out:])


def rms_fwd(h, gain, name, hosted=()):
    T, D = h.shape
    tm = _tile(T, TOKEN_TILE)

    def body(h_ref, g_ref, o_ref):
        x = h_ref[...]
        r = lax.rsqrt(jnp.mean(x * x, axis=-1, keepdims=True) + RMS_EPS)
        o_ref[...] = (x * r * g_ref[...]).astype(o_ref.dtype)

    (n,), xo = _call(
        body, name, (T // tm,),
        [pl.BlockSpec((tm, D), lambda i: (i, 0)), pl.BlockSpec((1, D), lambda i: (0, 0))],
        [pl.BlockSpec((tm, D), lambda i: (i, 0))], [jax.ShapeDtypeStruct((T, D), BF16)],
        [h, gain], ("parallel",), hosted=hosted)
    return n, xo


def loss_head(h, gain, tgt, name):
    T, D = h.shape
    tm = _tile(T, TOKEN_TILE)

    def body(h_ref, g_ref, t_ref, loss_ref, dh_ref, dg_ref):
        i = pl.program_id(0)
        x = h_ref[...]
        g = g_ref[...]
        r = lax.rsqrt(jnp.mean(x * x, axis=-1, keepdims=True) + RMS_EPS)
        xhat = x * r
        diff = xhat * g - t_ref[...]
        part_loss = 0.5 * jnp.sum(jnp.mean(diff * diff, axis=-1, keepdims=True), axis=0, keepdims=True)
        dy = diff * (1.0 / D)
        dxhat = dy * g
        dh_ref[...] = r * (dxhat - xhat * jnp.mean(dxhat * xhat, axis=-1, keepdims=True))
        part = jnp.sum(dy * xhat, axis=0, keepdims=True)

        @pl.when(i == 0)
        def _():
            dg_ref[...] = part
            loss_ref[...] = part_loss

        @pl.when(i > 0)
        def _():
            dg_ref[...] += part
            loss_ref[...] += part_loss

    row = pl.BlockSpec((tm, D), lambda i: (i, 0))
    vec = pl.BlockSpec((1, D), lambda i: (0, 0))
    return pl.pallas_call(
        body, name=name, grid=(T // tm,),
        in_specs=[row, vec, row],
        out_specs=[pl.BlockSpec((1, 1), lambda i: (0, 0)), row, vec],
        out_shape=[jax.ShapeDtypeStruct((1, 1), F32), jax.ShapeDtypeStruct((T, D), F32),
                   jax.ShapeDtypeStruct((1, D), F32)],
        compiler_params=_params(("arbitrary",)),
    )(h, gain, tgt)


def _prev_halo_spec(tm, width):
    return pl.BlockSpec((HALO, width), lambda i: (jnp.maximum(i * (tm // HALO) - 1, 0), 0))


def _next_halo_spec(tm, width, T):
    return pl.BlockSpec((HALO, width), lambda i: (jnp.minimum((i + 1) * (tm // HALO), T // HALO - 1), 0))


def _shifted(win, off, rows):
    if off % SUBLANES == 0:
        return win[off:off + rows]
    n = win.shape[0]
    return pltpu.roll(win, (n - off) % n, 0)[:rows]


def _rowsum8(x):
    acc = x[0:SUBLANES]
    for q in range(1, x.shape[0] // SUBLANES):
        acc = acc + x[q * SUBLANES:(q + 1) * SUBLANES]
    return acc


def _conv_loops(tm, D, per_block):
    def chunk(r, carry):
        t0 = pl.multiple_of(r * CONV_ROWS, CONV_ROWS)
        for lb in range(D // LANES):
            per_block(t0, slice(lb * LANES, (lb + 1) * LANES))
        return carry

    lax.fori_loop(0, tm // CONV_ROWS, chunk, 0)


def gateconv_fwd(bcv, w, name, hosted=()):
    T, D3 = bcv.shape
    D = D3 // 3
    K = w.shape[0]
    tm = _tile(T, TOKEN_TILE)

    def body(x_ref, halo_ref, w_ref, y_ref, pad_ref):
        i = pl.program_id(0)
        pad_ref[HALO:, :] = x_ref[:, D:2 * D] * x_ref[:, 2 * D:]
        pad_ref[:HALO, :] = jnp.where(i > 0, halo_ref[:, D:2 * D] * halo_ref[:, 2 * D:], 0.0)

        def block(t0, ls):
            win = pad_ref[pl.ds(t0, CONV_ROWS + HALO), ls]
            acc = jnp.zeros((CONV_ROWS, LANES), F32)
            for k in range(K):
                acc = acc + w_ref[k:k + 1, ls] * _shifted(win, HALO - (K - 1) + k, CONV_ROWS)
            y_ref[pl.ds(t0, CONV_ROWS), ls] = (x_ref[pl.ds(t0, CONV_ROWS), ls] * acc).astype(y_ref.dtype)

        _conv_loops(tm, D, block)

    (y,), xo = _call(
        body, name, (T // tm,),
        [pl.BlockSpec((tm, D3), lambda i: (i, 0)), _prev_halo_spec(tm, D3), pl.BlockSpec((K, D), lambda i: (0, 0))],
        [pl.BlockSpec((tm, D), lambda i: (i, 0))], [jax.ShapeDtypeStruct((T, D), BF16)],
        [bcv, bcv, w], ("parallel",), [pltpu.VMEM((tm + HALO, D), F32)], hosted=hosted)
    return y, xo


def gateconv_bwd(dy, bcv, w, name, hosted=()):
    T, D3 = bcv.shape
    D = D3 // 3
    K = w.shape[0]
    tm = _tile(T, TOKEN_TILE)
    nt = T // tm

    def body(dy_ref, dyn_ref, x_ref, xp_ref, xn_ref, w_ref, o_ref, dw_ref, cv_ref, dc_ref, wacc_ref):
        i = pl.program_id(0)
        cv_ref[HALO:, :] = x_ref[:, D:2 * D] * x_ref[:, 2 * D:]
        cv_ref[:HALO, :] = jnp.where(i > 0, xp_ref[:, D:2 * D] * xp_ref[:, 2 * D:], 0.0)
        dc_ref[:tm, :] = dy_ref[...] * x_ref[:, :D]
        dc_ref[tm:, :] = jnp.where(i < nt - 1, dyn_ref[...] * xn_ref[:, :D], 0.0)

        @pl.when(i == 0)
        def _():
            wacc_ref[...] = jnp.zeros_like(wacc_ref)

        def block(t0, ls):
            cwin = cv_ref[pl.ds(t0, CONV_ROWS + HALO), ls]
            dwin = dc_ref[pl.ds(t0, CONV_ROWS + HALO), ls]
            dcon = dwin[:CONV_ROWS]
            conv = jnp.zeros((CONV_ROWS, LANES), F32)
            dcv = jnp.zeros((CONV_ROWS, LANES), F32)
            for k in range(K):
                wk = w_ref[k:k + 1, ls]
                cs = _shifted(cwin, HALO - (K - 1) + k, CONV_ROWS)
                conv = conv + wk * cs
                dcv = dcv + wk * _shifted(dwin, (K - 1) - k, CONV_ROWS)
                wacc_ref[k * SUBLANES:(k + 1) * SUBLANES, ls] += _rowsum8(dcon * cs)
            rows = pl.ds(t0, CONV_ROWS)
            o_ref[rows, ls] = (dy_ref[rows, ls] * conv).astype(o_ref.dtype)
            o_ref[rows, pl.ds(D + ls.start, LANES)] = (dcv * x_ref[rows, pl.ds(2 * D + ls.start, LANES)]).astype(o_ref.dtype)
            o_ref[rows, pl.ds(2 * D + ls.start, LANES)] = (dcv * x_ref[rows, pl.ds(D + ls.start, LANES)]).astype(o_ref.dtype)

        _conv_loops(tm, D, block)

        @pl.when(i == nt - 1)
        def _():
            for k in range(K):
                dw_ref[k:k + 1, :] = jnp.sum(wacc_ref[k * SUBLANES:(k + 1) * SUBLANES, :], axis=0, keepdims=True)

    (dx, dw), xo = _call(
        body, name, (nt,),
        [pl.BlockSpec((tm, D), lambda i: (i, 0)), _next_halo_spec(tm, D, T),
         pl.BlockSpec((tm, D3), lambda i: (i, 0)), _prev_halo_spec(tm, D3), _next_halo_spec(tm, D3, T),
         pl.BlockSpec((K, D), lambda i: (0, 0))],
        [pl.BlockSpec((tm, D3), lambda i: (i, 0)), pl.BlockSpec((K, D), lambda i: (0, 0))],
        [jax.ShapeDtypeStruct((T, D3), BF16), jax.ShapeDtypeStruct((K, D), F32)],
        [dy, dy, bcv, bcv, bcv, w], ("arbitrary",),
        [pltpu.VMEM((tm + HALO, D), F32), pltpu.VMEM((tm + HALO, D), F32), pltpu.VMEM((K * SUBLANES, D), F32)],
        hosted=hosted)
    return dx, dw, xo


def bconv_fwd(u, w, b_conv, ln_g, ln_b, name, hosted=()):
    T, D2 = u.shape
    D = D2 // 2
    K = w.shape[0]
    tm = _tile(T, TOKEN_TILE)

    def body(u_ref, halo_ref, w_ref, bc_ref, g_ref, b_ref, cu_ref, s_ref, pad_ref):
        i = pl.program_id(0)
        pad_ref[HALO:, :] = u_ref[:, :D] * _sigmoid(u_ref[:, D:])
        pad_ref[:HALO, :] = jnp.where(i > 0, halo_ref[:, :D] * _sigmoid(halo_ref[:, D:]), 0.0)

        def block(t0, ls):
            win = pad_ref[pl.ds(t0, CONV_ROWS + HALO), ls]
            acc = jnp.zeros((CONV_ROWS, LANES), F32)
            for k in range(K):
                acc = acc + w_ref[k:k + 1, ls] * _shifted(win, HALO - (K - 1) + k, CONV_ROWS)
            cu_ref[pl.ds(t0, CONV_ROWS), ls] = acc + bc_ref[:, ls]

        _conv_loops(tm, D, block)
        cu = cu_ref[...]
        mu = jnp.mean(cu, axis=-1, keepdims=True)
        xc = cu - mu
        rstd = lax.rsqrt(jnp.mean(xc * xc, axis=-1, keepdims=True) + LN_EPS)
        ln = xc * rstd * g_ref[...] + b_ref[...]
        s_ref[...] = (ln * _sigmoid(ln)).astype(s_ref.dtype)

    vec = pl.BlockSpec((1, D), lambda i: (0, 0))
    row = pl.BlockSpec((tm, D), lambda i: (i, 0))
    (cu, s), xo = _call(
        body, name, (T // tm,),
        [pl.BlockSpec((tm, D2), lambda i: (i, 0)), _prev_halo_spec(tm, D2), pl.BlockSpec((K, D), lambda i: (0, 0)), vec, vec, vec],
        [row, row], [jax.ShapeDtypeStruct((T, D), F32), jax.ShapeDtypeStruct((T, D), BF16)],
        [u, u, w, b_conv, ln_g, ln_b], ("parallel",), [pltpu.VMEM((tm + HALO, D), F32)], hosted=hosted)
    return cu, s, xo


def pw2_ln_bwd(dy, w, cu, ln_g, ln_b, name, hosted=()):
    T, D = cu.shape
    tm = _tile(T, TOKEN_TILE)

    def body(dy_ref, w_ref, cu_ref, g_ref, b_ref, dcu_ref, dg_ref, db_ref, dbc_ref, dbo_ref):
        i = pl.program_id(0)
        dy_ = dy_ref[...]
        ds = lax.dot_general(dy_.astype(BF16), w_ref[0], _NT, preferred_element_type=F32)
        cu_ = cu_ref[...]
        mu = jnp.mean(cu_, axis=-1, keepdims=True)
        xc = cu_ - mu
        rstd = lax.rsqrt(jnp.mean(xc * xc, axis=-1, keepdims=True) + LN_EPS)
        xh = xc * rstd
        ln = xh * g_ref[...] + b_ref[...]
        sg = _sigmoid(ln)
        dl = ds * (sg * (1.0 + ln * (1.0 - sg)))
        dxh = dl * g_ref[...]
        dcu = rstd * (dxh - jnp.mean(dxh, axis=-1, keepdims=True) - xh * jnp.mean(dxh * xh, axis=-1, keepdims=True))
        dcu_ref[...] = dcu
        pg = jnp.sum(dl * xh, axis=0, keepdims=True)
        pb = jnp.sum(dl, axis=0, keepdims=True)
        pc = jnp.sum(dcu, axis=0, keepdims=True)
        po = jnp.sum(dy_, axis=0, keepdims=True)

        @pl.when(i == 0)
        def _():
            dg_ref[...] = pg
            db_ref[...] = pb
            dbc_ref[...] = pc
            dbo_ref[...] = po

        @pl.when(i > 0)
        def _():
            dg_ref[...] += pg
            db_ref[...] += pb
            dbc_ref[...] += pc
            dbo_ref[...] += po

    vec = pl.BlockSpec((1, D), lambda i: (0, 0))
    row = pl.BlockSpec((tm, D), lambda i: (i, 0))
    vshape = jax.ShapeDtypeStruct((1, D), F32)
    outs, xo = _call(
        body, name, (T // tm,), [row, pl.BlockSpec((1, D, D), lambda i: (0, 0, 0)), row, vec, vec], [row, vec, vec, vec, vec],
        [jax.ShapeDtypeStruct((T, D), F32), vshape, vshape, vshape, vshape], [dy, w, cu, ln_g, ln_b], ("arbitrary",),
        hosted=hosted)
    return (*outs, xo)


def bconv_bwd(dcu, u, w, name, hosted=()):
    T, D2 = u.shape
    D = D2 // 2
    K = w.shape[0]
    tm = _tile(T, TOKEN_TILE)
    nt = T // tm

    def body(dc_ref, dcn_ref, u_ref, up_ref, w_ref, du_ref, dw_ref, db_ref, glu_ref, dpad_ref, dglu_ref, wacc_ref):
        i = pl.program_id(0)
        glu_ref[HALO:, :] = u_ref[:, :D] * _sigmoid(u_ref[:, D:])
        glu_ref[:HALO, :] = jnp.where(i > 0, up_ref[:, :D] * _sigmoid(up_ref[:, D:]), 0.0)
        dpad_ref[:tm, :] = dc_ref[...]
        dpad_ref[tm:, :] = jnp.where(i < nt - 1, dcn_ref[...], 0.0)

        @pl.when(i == 0)
        def _():
            wacc_ref[...] = jnp.zeros_like(wacc_ref)

        def block(t0, ls):
            gwin = glu_ref[pl.ds(t0, CONV_ROWS + HALO), ls]
            dwin = dpad_ref[pl.ds(t0, CONV_ROWS + HALO), ls]
            dcur = dwin[:CONV_ROWS]
            dglu = jnp.zeros((CONV_ROWS, LANES), F32)
            for k in range(K):
                dglu = dglu + w_ref[k:k + 1, ls] * _shifted(dwin, (K - 1) - k, CONV_ROWS)
                gs = _shifted(gwin, HALO - (K - 1) + k, CONV_ROWS)
                wacc_ref[k * SUBLANES:(k + 1) * SUBLANES, ls] += _rowsum8(dcur * gs)
            dglu_ref[pl.ds(t0, CONV_ROWS), ls] = dglu

        _conv_loops(tm, D, block)
        dglu = dglu_ref[...]
        a = u_ref[:, :D]
        sg = _sigmoid(u_ref[:, D:])
        da = dglu * sg
        dg = dglu * a * (sg * (1.0 - sg))
        du_ref[:, :D] = da.astype(du_ref.dtype)
        du_ref[:, D:] = dg.astype(du_ref.dtype)
        pa = jnp.sum(da, axis=0, keepdims=True)
        pg = jnp.sum(dg, axis=0, keepdims=True)

        @pl.when(i == 0)
        def _():
            db_ref[:, :D] = pa
            db_ref[:, D:] = pg

        @pl.when(i > 0)
        def _():
            db_ref[:, :D] += pa
            db_ref[:, D:] += pg

        @pl.when(i == nt - 1)
        def _():
            for k in range(K):
                dw_ref[k:k + 1, :] = jnp.sum(wacc_ref[k * SUBLANES:(k + 1) * SUBLANES, :], axis=0, keepdims=True)

    (du, dw, db), xo = _call(
        body, name, (nt,),
        [pl.BlockSpec((tm, D), lambda i: (i, 0)), _next_halo_spec(tm, D, T),
         pl.BlockSpec((tm, D2), lambda i: (i, 0)), _prev_halo_spec(tm, D2), pl.BlockSpec((K, D), lambda i: (0, 0))],
        [pl.BlockSpec((tm, D2), lambda i: (i, 0)), pl.BlockSpec((K, D), lambda i: (0, 0)), pl.BlockSpec((1, D2), lambda i: (0, 0))],
        [jax.ShapeDtypeStruct((T, D2), BF16), jax.ShapeDtypeStruct((K, D), F32), jax.ShapeDtypeStruct((1, D2), F32)],
        [dcu, dcu, u, u, w], ("arbitrary",),
        [pltpu.VMEM((tm + HALO, D), F32), pltpu.VMEM((tm + HALO, D), F32), pltpu.VMEM((tm, D), F32),
         pltpu.VMEM((K * SUBLANES, D), F32)], hosted=hosted)
    return du, dw, db, xo


def mm_cols(a, w, bias, name, hosted=()):
    T, K = a.shape
    S, _, n = w.shape
    tm = _tile(T, WIDE_TOKEN_TILE)

    def body(*refs):
        a_ref, w_ref = refs[:2]
        o_ref = refs[-1]
        acc = jnp.dot(a_ref[...], w_ref[...], preferred_element_type=F32)
        if bias is not None:
            acc = acc + refs[2][...]
        o_ref[...] = acc

    in_specs = [pl.BlockSpec((tm, K), lambda s, i: (i, 0)), pl.BlockSpec((None, K, n), lambda s, i: (s, 0, 0))]
    args = [a, w]
    if bias is not None:
        in_specs.append(pl.BlockSpec((1, n), lambda s, i: (0, s)))
        args.append(bias)
    (out,), xo = _call(body, name, (S, T // tm), in_specs, [pl.BlockSpec((tm, n), lambda s, i: (i, s))],
                       [jax.ShapeDtypeStruct((T, S * n), F32)], args, ("parallel", "parallel"), hosted=hosted)
    return out, xo


def _load_weights(pairs, sems, S, i, s):
    def copies(seg):
        return [pltpu.make_async_copy(src.at[seg], dst.at[seg], sems.at[k, seg]) for k, (src, dst) in enumerate(pairs)]

    @pl.when((i == 0) & (s == 0))
    def _():
        for seg in range(S):
            for cp in copies(seg):
                cp.start()

    @pl.when(i == 0)
    def _():
        for cp in copies(s):
            cp.wait()


def ffn_fwd(h, gain, wg, wu, wd, name, hosted=()):
    T, D = h.shape
    S, f, _ = wg.shape
    tm = _tile(T, TOKEN_TILE)
    rc = tm // FFN_ROW_CHUNKS
    chunks = [slice(r * rc, (r + 1) * rc) for r in range(FFN_ROW_CHUNKS)]
    wd_arrives = isinstance(wd, _Exchange)
    weights = [wg, wu] if wd_arrives else [wg, wu, wd]
    hosted = ([wd] if wd_arrives else []) + list(hosted)

    def body(h_ref, gain_ref, *refs):
        nw = len(weights)
        wg_hbm, wu_hbm = refs[:2]
        wd_hbm = refs[-1] if wd_arrives else refs[2]
        n_ref, g_ref, u_ref, gu_ref, o_ref, wg_v, wu_v, wd_v, sems = refs[nw:nw + 9]
        i, s = pl.program_id(0), pl.program_id(1)
        _load_weights([(wg_hbm, wg_v), (wu_hbm, wu_v), (wd_hbm, wd_v)], sems, S, i, s)

        @pl.when(s == 0)
        def _():
            x = h_ref[...]
            r = lax.rsqrt(jnp.mean(x * x, axis=-1, keepdims=True) + RMS_EPS)
            n_ref[...] = (x * r * gain_ref[...]).astype(n_ref.dtype)

        parts = []
        for rows in chunks:
            a = n_ref[rows, :]
            g = lax.dot_general(a, wg_v[s], _NT, preferred_element_type=F32)
            u = lax.dot_general(a, wu_v[s], _NT, preferred_element_type=F32)
            gu = (g * _sigmoid(g) * u).astype(gu_ref.dtype)
            g_ref[rows, :] = g.astype(g_ref.dtype)
            u_ref[rows, :] = u.astype(u_ref.dtype)
            gu_ref[rows, :] = gu
            parts.append(jnp.dot(gu, wd_v[s], preferred_element_type=F32))

        @pl.when(s == 0)
        def _():
            for rows, part in zip(chunks, parts):
                o_ref[rows, :] = h_ref[rows, :] + part

        @pl.when(s > 0)
        def _():
            for rows, part in zip(chunks, parts):
                o_ref[rows, :] += part

    row = pl.BlockSpec((tm, D), lambda i, s: (i, 0))
    seg = pl.BlockSpec((None, tm, f), lambda i, s: (s, i, 0))
    hbm = pl.BlockSpec(memory_space=pl.ANY)
    segs = jax.ShapeDtypeStruct((S, T, f), BF16)
    outs, xo = _call(
        body, name, (T // tm, S),
        [row, pl.BlockSpec((1, D), lambda i, s: (0, 0))] + [hbm] * len(weights), [row, seg, seg, seg, row],
        [jax.ShapeDtypeStruct((T, D), BF16), segs, segs, segs, jax.ShapeDtypeStruct((T, D), F32)],
        [h, gain] + weights, ("arbitrary", "arbitrary"),
        [pltpu.VMEM((S, f, D), BF16), pltpu.VMEM((S, f, D), BF16), pltpu.VMEM((S, f, D), BF16), pltpu.SemaphoreType.DMA((3, S))],
        hosted=hosted)
    return (*outs, xo)


def ffn_bwd(dy, h, gain, g, u, wd, wg, wu, name, hosted=()):
    T, D = h.shape
    S, f, _ = wg.shape
    tm = _tile(T, TOKEN_TILE)
    nt = T // tm
    rc = tm // FFN_ROW_CHUNKS
    chunks = [slice(r * rc, (r + 1) * rc) for r in range(FFN_ROW_CHUNKS)]

    def body(dy_ref, h_ref, gain_ref, g_ref, u_ref, wd_hbm, wg_hbm, wu_hbm, dg_ref, du_ref, dh_ref, dgain_ref,
             wd_v, wg_v, wu_v, dyb_ref, sems):
        i, s = pl.program_id(0), pl.program_id(1)
        _load_weights([(wd_hbm, wd_v), (wg_hbm, wg_v), (wu_hbm, wu_v)], sems, S, i, s)

        @pl.when(s == 0)
        def _():
            dyb_ref[...] = dy_ref[...].astype(dyb_ref.dtype)

        parts = []
        for rows in chunks:
            dgu = lax.dot_general(dyb_ref[rows, :], wd_v[s], _NT, preferred_element_type=F32)
            gv = g_ref[rows, :].astype(F32)
            sg = _sigmoid(gv)
            dg = (dgu * u_ref[rows, :].astype(F32) * (sg * (1.0 + gv * (1.0 - sg)))).astype(dg_ref.dtype)
            du = (dgu * (gv * sg)).astype(du_ref.dtype)
            dg_ref[rows, :] = dg
            du_ref[rows, :] = du
            parts.append(jnp.dot(dg, wg_v[s], preferred_element_type=F32)
                         + jnp.dot(du, wu_v[s], preferred_element_type=F32))

        @pl.when(s == 0)
        def _():
            for rows, part in zip(chunks, parts):
                dh_ref[rows, :] = part

        @pl.when(s > 0)
        def _():
            for rows, part in zip(chunks, parts):
                dh_ref[rows, :] += part

        @pl.when(s == S - 1)
        def _():
            dn = dh_ref[...]
            x = h_ref[...]
            r = lax.rsqrt(jnp.mean(x * x, axis=-1, keepdims=True) + RMS_EPS)
            xhat = x * r
            dxhat = dn * gain_ref[...]
            dh_ref[...] = dy_ref[...] + r * (dxhat - xhat * jnp.mean(dxhat * xhat, axis=-1, keepdims=True))
            pg = jnp.sum(dn * xhat, axis=0, keepdims=True)

            @pl.when(i == 0)
            def _():
                dgain_ref[...] = pg

            @pl.when(i > 0)
            def _():
                dgain_ref[...] += pg

    row = pl.BlockSpec((tm, D), lambda i, s: (i, 0))
    vec = pl.BlockSpec((1, D), lambda i, s: (0, 0))
    seg = pl.BlockSpec((None, tm, f), lambda i, s: (s, i, 0))
    hbm = pl.BlockSpec(memory_space=pl.ANY)
    segs = jax.ShapeDtypeStruct((S, T, f), BF16)
    outs, xo = _call(
        body, name, (nt, S),
        [row, row, vec, seg, seg, hbm, hbm, hbm], [seg, seg, row, vec],
        [segs, segs, jax.ShapeDtypeStruct((T, D), F32), jax.ShapeDtypeStruct((1, D), F32)],
        [dy, h, gain, g, u, wd, wg, wu], ("arbitrary", "arbitrary"),
        [pltpu.VMEM((S, f, D), BF16), pltpu.VMEM((S, f, D), BF16), pltpu.VMEM((S, f, D), BF16),
         pltpu.VMEM((tm, D), BF16), pltpu.SemaphoreType.DMA((3, S))], hosted=hosted)
    return (*outs, xo)


def mm_rows(a, w, res, bias, name, hosted=()):
    S, T, k = a.shape
    N = w.shape[-1]
    tm = _tile(T, TOKEN_TILE)

    def body(*refs):
        a_ref, w_ref, r_ref = refs[:3]
        o_ref = refs[-1]
        s = pl.program_id(1)
        acc = jnp.dot(a_ref[...], w_ref[...], preferred_element_type=F32)

        @pl.when(s == 0)
        def _():
            base = r_ref[...]
            if bias is not None:
                base = base + refs[3][...]
            o_ref[...] = base + acc

        @pl.when(s > 0)
        def _():
            o_ref[...] += acc

    in_specs = [pl.BlockSpec((None, tm, k), lambda i, s: (s, i, 0)),
                pl.BlockSpec((None, k, N), lambda i, s: (s, 0, 0)),
                pl.BlockSpec((tm, N), lambda i, s: (i, 0))]
    args = [a, w, res]
    if bias is not None:
        in_specs.append(pl.BlockSpec((1, N), lambda i, s: (0, 0)))
        args.append(bias)
    (out,), xo = _call(body, name, (T // tm, S), in_specs, [pl.BlockSpec((tm, N), lambda i, s: (i, 0))],
                       [jax.ShapeDtypeStruct((T, N), F32)], args, ("parallel", "arbitrary"), hosted=hosted)
    return out, xo


_NT = (((1,), (1,)), ((), ()))
_TN = (((0,), (0,)), ((), ()))


def nt_rows(dy, w, name, hosted=()):
    T, N = dy.shape
    S, k, _ = w.shape
    tm = _tile(T, TOKEN_TILE)

    def body(dy_ref, w_ref, o_ref):
        o_ref[...] = lax.dot_general(dy_ref[...].astype(BF16), w_ref[...], _NT, preferred_element_type=F32)

    (out,), xo = _call(
        body, name, (T // tm, S),
        [pl.BlockSpec((tm, N), lambda i, s: (i, 0)), pl.BlockSpec((None, k, N), lambda i, s: (s, 0, 0))],
        [pl.BlockSpec((None, tm, k), lambda i, s: (s, i, 0))], [jax.ShapeDtypeStruct((S, T, k), F32)],
        [dy, w], ("parallel", "parallel"), hosted=hosted)
    return out, xo


def nt_cols_rms(dy, w, h, gain, dres, name, hosted=()):
    T, K = h.shape
    S, _, n = w.shape
    tm = _tile(T, TOKEN_TILE)

    def body(dy_ref, w_ref, h_ref, gain_ref, dres_ref, dh_ref, dgain_ref):
        i = pl.program_id(0)
        dn = None
        for s in range(S):
            part = lax.dot_general(dy_ref[:, s * n:(s + 1) * n], w_ref[s], _NT, preferred_element_type=F32)
            dn = part if dn is None else dn + part
        x = h_ref[...]
        r = lax.rsqrt(jnp.mean(x * x, axis=-1, keepdims=True) + RMS_EPS)
        xhat = x * r
        dxhat = dn * gain_ref[...]
        dh_ref[...] = dres_ref[...] + r * (dxhat - xhat * jnp.mean(dxhat * xhat, axis=-1, keepdims=True))
        pg = jnp.sum(dn * xhat, axis=0, keepdims=True)

        @pl.when(i == 0)
        def _():
            dgain_ref[...] = pg

        @pl.when(i > 0)
        def _():
            dgain_ref[...] += pg

    row = pl.BlockSpec((tm, K), lambda i: (i, 0))
    vec = pl.BlockSpec((1, K), lambda i: (0, 0))
    (dh, dgain), xo = _call(
        body, name, (T // tm,),
        [pl.BlockSpec((tm, S * n), lambda i: (i, 0)), pl.BlockSpec((S, K, n), lambda i: (0, 0, 0)), row, vec, row],
        [row, vec], [jax.ShapeDtypeStruct((T, K), F32), jax.ShapeDtypeStruct((1, K), F32)],
        [dy, w, h, gain, dres], ("arbitrary",), hosted=hosted)
    return dh, dgain, xo


def tn_grad(a, dy, S, a_by_seg, name, hosted=()):
    T = dy.shape[0] if dy.ndim == 2 else dy.shape[1]
    tt = _tile(T, GRAD_TOKEN_TILE)
    if a_by_seg:
        R = a.shape[1] // S if a.ndim == 2 else a.shape[2]
        C = dy.shape[1]
        a_spec = pl.BlockSpec((tt, R), lambda s, t: (t, s)) if a.ndim == 2 else pl.BlockSpec((None, tt, R), lambda s, t: (s, t, 0))
        b_spec = pl.BlockSpec((tt, C), lambda s, t: (t, 0))
    else:
        R = a.shape[1]
        C = dy.shape[1] // S if dy.ndim == 2 else dy.shape[2]
        a_spec = pl.BlockSpec((tt, R), lambda s, t: (t, 0))
        b_spec = pl.BlockSpec((tt, C), lambda s, t: (t, s)) if dy.ndim == 2 else pl.BlockSpec((None, tt, C), lambda s, t: (s, t, 0))
    Rh = R // 2
    nt = T // tt

    def body(a_ref, b_ref, o_ref, acc_ref):
        t = pl.program_id(1)
        part = lax.dot_general(a_ref[...], b_ref[...].astype(BF16), _TN, preferred_element_type=F32)

        @pl.when(t == 0)
        def _():
            acc_ref[...] = part

        @pl.when(t > 0)
        def _():
            acc_ref[...] += part

        @pl.when(t == nt - 1)
        def _():
            o_ref[0] = acc_ref[:Rh, :].astype(o_ref.dtype)
            o_ref[1] = acc_ref[Rh:, :].astype(o_ref.dtype)

    (gh,), xo = _call(
        body, name, (S, nt), [a_spec, b_spec], [pl.BlockSpec((2, None, Rh, C), lambda s, t: (0, s, 0, 0))],
        [jax.ShapeDtypeStruct((2, S, Rh, C), BF16)], [a, dy], ("parallel", "arbitrary"), [pltpu.VMEM((R, C), F32)],
        hosted=hosted)
    return gh, xo


def tn_grad_square(a, dy, S, name, hosted=()):
    T, K = a.shape
    N = dy.shape[1]
    tt = _tile(T, GRAD_TOKEN_TILE)
    nt = T // tt
    Rh = K // S // 2

    def body(a_ref, b_ref, o_ref, acc_ref):
        t = pl.program_id(0)
        part = lax.dot_general(a_ref[...], b_ref[...].astype(BF16), _TN, preferred_element_type=F32)

        @pl.when(t == 0)
        def _():
            acc_ref[...] = part

        @pl.when(t > 0)
        def _():
            acc_ref[...] += part

        @pl.when(t == nt - 1)
        def _():
            for s in range(S):
                for hf in range(2):
                    r0 = (2 * s + hf) * Rh
                    o_ref[hf, s] = acc_ref[r0:r0 + Rh, :].astype(o_ref.dtype)

    (gh,), xo = _call(
        body, name, (nt,), [pl.BlockSpec((tt, K), lambda t: (t, 0)), pl.BlockSpec((tt, N), lambda t: (t, 0))],
        [pl.BlockSpec((2, S, Rh, N), lambda t: (0, 0, 0, 0))], [jax.ShapeDtypeStruct((2, S, Rh, N), BF16)],
        [a, dy], ("arbitrary",), [pltpu.VMEM((K, N), F32)], hosted=hosted)
    return gh, xo


def _place():
    x, y, c = lax.axis_index("x"), lax.axis_index("y"), lax.axis_index("c")
    chips = [(1 - x, y), (x, 1 - y), (1 - x, 1 - y)]
    return x, y, c, chips


def _remote(src, dst, send_sem, recv_sem, dev):
    return pltpu.make_async_remote_copy(src_ref=src, dst_ref=dst, send_sem=send_sem, recv_sem=recv_sem,
                                        device_id=dev, device_id_type=MESH)


def small_allreduce(v, name, hosted=()):
    rows, W = v.shape

    def body(v_ref, o_ref, sib_ref, pair_ref, chips_ref, send_sems, recv_sems):
        x, y, c, chips = _place()
        me = 2 * x + y
        swap = _remote(v_ref, sib_ref, send_sems.at[3], recv_sems.at[3], (x, y, 1 - c))
        swap.start()
        swap.wait()
        mine, other = v_ref[...], sib_ref[...]
        pair_ref[...] = jnp.where(c == 0, mine, other) + jnp.where(c == 0, other, mine)
        sends = []
        for j, (px, py) in enumerate(chips):
            cp = _remote(pair_ref, chips_ref.at[me], send_sems.at[j], recv_sems.at[j], (px, py, c))
            cp.start()
            sends.append(cp)
        chips_ref[me] = pair_ref[...]
        for j, (px, py) in enumerate(chips):
            blk = chips_ref.at[2 * px + py]
            _remote(blk, blk, send_sems.at[j], recv_sems.at[j], (px, py, c)).wait_recv()
        for cp in sends:
            cp.wait_send()
        o_ref[...] = (chips_ref[0] + chips_ref[1]) + (chips_ref[2] + chips_ref[3])

    vm = pl.BlockSpec(memory_space=pltpu.VMEM)
    (out,), xo = _call(
        body, name, (), [vm], [vm], [jax.ShapeDtypeStruct((rows, W), F32)], [v], (),
        [pltpu.VMEM((rows, W), F32), pltpu.VMEM((rows, W), F32), pltpu.VMEM((N_CHIPS, rows, W), F32),
         pltpu.SemaphoreType.DMA((4,)), pltpu.SemaphoreType.DMA((4,))], hosted=hosted)
    return out, xo


def _gather_p1_copies(srcs, bufs, ssem, rsem, base):
    x, y, c, chips = _place()
    me, sib = 2 * x + y, (x, y, 1 - c)
    sends, recvs = [], []
    for k, (src, buf) in enumerate(zip(srcs, bufs)):
        rh = src.shape[0] // 2
        s0 = base + 4 * k
        sends.append(_remote(src, buf.at[me], ssem.at[s0 + 3], rsem.at[s0 + 3], sib))
        recvs.append(_remote(buf.at[me], buf.at[me], ssem.at[s0 + 3], rsem.at[s0 + 3], sib))
        for j, (px, py) in enumerate(chips):
            sends.append(_remote(src.at[pl.ds(c * rh, rh)], buf.at[me, pl.ds(c * rh, rh)], ssem.at[s0 + j], rsem.at[s0 + j], (px, py, c)))
            blk = buf.at[2 * px + py, pl.ds(c * rh, rh)]
            recvs.append(_remote(blk, blk, ssem.at[s0 + j], rsem.at[s0 + j], (px, py, c)))
    return sends, recvs


def _gather_p2_copies(bufs, ssem, rsem, base):
    x, y, c, chips = _place()
    sib = (x, y, 1 - c)
    sends, recvs = [], []
    for k, buf in enumerate(bufs):
        rh = buf.shape[1] // 2
        for j, (px, py) in enumerate(chips):
            s0 = base + 3 * k + j
            blk = buf.at[2 * px + py, pl.ds(c * rh, rh)]
            sends.append(_remote(blk, blk, ssem.at[s0], rsem.at[s0], sib))
            got = buf.at[2 * px + py, pl.ds((1 - c) * rh, rh)]
            recvs.append(_remote(got, got, ssem.at[s0], rsem.at[s0], sib))
    return sends, recvs


def _gathered_shape(s):
    return jax.ShapeDtypeStruct((N_CHIPS,) + s.shape, s.dtype)


def gather_p1(shards):
    return _Exchange(shards, [_gathered_shape(s) for s in shards], {}, 4 * len(shards),
                     lambda xi, xo, ss, rs: _gather_p1_copies(xi, xo, ss, rs, 0))


def gather_p2(bufs):
    return _Exchange(bufs, [jax.ShapeDtypeStruct(b.shape, b.dtype) for b in bufs], {k: k for k in range(len(bufs))},
                     3 * len(bufs), lambda xi, xo, ss, rs: _gather_p2_copies(xo, ss, rs, 0))


def gather_whole(whole, begun):
    nw, n = len(whole), len(whole) + len(begun)
    shards = list(whole) + list(begun)
    return _Exchange(shards, [_gathered_shape(s) for s in shards], {}, 4 * n + 3 * nw,
                     lambda xi, xo, ss, rs: _gather_p1_copies(xi, xo, ss, rs, 0),
                     then=lambda xi, xo, ss, rs: _gather_p2_copies(xo[:nw], ss, rs, 4 * n))


def gather_small(v):
    def copies(xi, xo, ssem, rsem):
        x, y, c, chips = _place()
        me, sib = 2 * x + y, (x, y, 1 - c)
        sends = [_remote(xi[0], xo[0].at[me], ssem.at[3], rsem.at[3], sib)]
        recvs = [_remote(xo[0].at[me], xo[0].at[me], ssem.at[3], rsem.at[3], sib)]
        for j, (px, py) in enumerate(chips):
            sends.append(_remote(xi[0], xo[0].at[me], ssem.at[j], rsem.at[j], (px, py, c)))
            blk = xo[0].at[2 * px + py]
            recvs.append(_remote(blk, blk, ssem.at[j], rsem.at[j], (px, py, c)))
        return sends, recvs

    return _Exchange([v], [_gathered_shape(v)], {}, 4, copies)


def run_exchanges(exchanges, name):
    return _call(lambda: None, name, (), [], [], [], [], (), hosted=exchanges)[1]


def sibling_halves(grads):
    def copies(xi, xo, ssem, rsem):
        x, y, c, _ = _place()
        sends = [_remote(xi[k].at[1 - c], xo[k], ssem.at[k], rsem.at[k], (x, y, 1 - c)) for k in range(len(grads))]
        return sends, sends

    return _Exchange(grads, [jax.ShapeDtypeStruct(g.shape[1:], g.dtype) for g in grads], {}, len(grads), copies)


def pair_sum(gh, recv, cidx, name):
    _, S, Rh, C = gh.shape

    def body(c_ref, a_ref, b_ref, o_ref):
        o_ref[...] = (a_ref[...].astype(F32) + b_ref[...].astype(F32)).astype(o_ref.dtype)

    return pl.pallas_call(
        body, name=name, out_shape=jax.ShapeDtypeStruct((S, Rh, C), BF16),
        grid_spec=pltpu.PrefetchScalarGridSpec(
            num_scalar_prefetch=1, grid=(S,),
            in_specs=[pl.BlockSpec((None, None, Rh, C), lambda s, c_ref: (c_ref[0], s, 0, 0)),
                      pl.BlockSpec((None, Rh, C), lambda s, c_ref: (s, 0, 0))],
            out_specs=pl.BlockSpec((None, Rh, C), lambda s, c_ref: (s, 0, 0))),
        compiler_params=_params(("parallel",)),
    )(cidx, gh, recv)


def scatter_p1(parts):
    def copies(xi, xo, ssem, rsem):
        x, y, c, chips = _place()
        me, sib = 2 * x + y, (x, y, 1 - c)
        sends, recvs = [], []
        for k in range(len(parts)):
            s0 = 4 * k
            sends.append(_remote(xi[k].at[me], xo[k].at[me, c], ssem.at[s0 + 3], rsem.at[s0 + 3], sib))
            own = xo[k].at[me, 1 - c]
            recvs.append(_remote(own, own, ssem.at[s0 + 3], rsem.at[s0 + 3], sib))
            for j, (px, py) in enumerate(chips):
                sends.append(_remote(xi[k].at[2 * px + py], xo[k].at[me, c], ssem.at[s0 + j], rsem.at[s0 + j], (px, py, c)))
                blk = xo[k].at[2 * px + py, c]
                recvs.append(_remote(blk, blk, ssem.at[s0 + j], rsem.at[s0 + j], (px, py, c)))
        return sends, recvs

    return _Exchange(parts, [jax.ShapeDtypeStruct((p.shape[0], 2) + p.shape[1:], p.dtype) for p in parts], {},
                     4 * len(parts), copies)


def scatter_p2(bufs):
    def copies(xi, xo, ssem, rsem):
        x, y, c, chips = _place()
        sib = (x, y, 1 - c)
        sends, recvs = [], []
        for k in range(len(bufs)):
            for j, (px, py) in enumerate(chips):
                s0 = 3 * k + j
                blk = xo[k].at[2 * px + py, c]
                sends.append(_remote(blk, blk, ssem.at[s0], rsem.at[s0], sib))
                got = xo[k].at[2 * px + py, 1 - c]
                recvs.append(_remote(got, got, ssem.at[s0], rsem.at[s0], sib))
        return sends, recvs

    return _Exchange(bufs, [jax.ShapeDtypeStruct(b.shape, b.dtype) for b in bufs], {k: k for k in range(len(bufs))},
                     3 * len(bufs), copies)


def _adamw_math(w, g, m, v):
    m = ADAM_B1 * m + (1.0 - ADAM_B1) * g
    v = ADAM_B2 * v + (1.0 - ADAM_B2) * (g * g)
    m_hat = m / (1.0 - ADAM_B1 ** ADAM_STEP)
    v_hat = v / (1.0 - ADAM_B2 ** ADAM_STEP)
    delta = -ADAM_LR * (m_hat / (jnp.sqrt(v_hat) + ADAM_EPS) + ADAM_WD * w)
    return delta, m, v


def adamw_reduce(w, m, v, buf, part, place, lyr, bases, name, hosted=()):
    L, R, C = w.shape
    Rh = R // 2
    rb = _tile(Rh, ROW_TILE, 2 * SUBLANES)
    nb = Rh // rb

    def body(place_ref, p_ref, b0, b1, b2, b3, w_ref, m_ref, v_ref, *rest):
        go_ref, d_ref, mo_ref, vo_ref = rest[-4:]
        mine = (place_ref[1] == pl.program_id(0))
        g = None
        for p, b in enumerate((b0, b1, b2, b3)):
            val = jnp.where(mine & (place_ref[0] == p), p_ref[...], b[...]).astype(F32)
            g = val if g is None else g + val
        d, mn, vn = _adamw_math(w_ref[...], g, m_ref[...], v_ref[...])
        go_ref[...] = g
        d_ref[...] = d
        mo_ref[...] = mn
        vo_ref[...] = vn

    def buf_spec(p):
        def idx(h, i, pr):
            own = (pr[0] == p) & (pr[1] == h)
            return (p, jnp.where(own, 1 - h, h), i, 0)
        return pl.BlockSpec((None, None, rb, C), idx)

    blk = pl.BlockSpec((None, rb, C), lambda h, i, pr: (lyr, h * nb + i, 0))
    in_specs = [pl.BlockSpec((None, rb, C), lambda h, i, pr: (pr[0], i, 0))] + [buf_spec(p) for p in range(N_CHIPS)] + [blk] * 3
    args = [part, buf, buf, buf, buf, w, m, v]
    aliases = {}
    if bases is not None:
        in_specs += [pl.BlockSpec(memory_space=pl.ANY)] * 4
        aliases = {len(args) + k: k for k in range(4)}
        args += list(bases)
    shp = jax.ShapeDtypeStruct((L, R, C), F32)
    return _call(body, name, (2, nb), in_specs, [blk] * 4, [shp] * 4, args, ("parallel", "parallel"),
                 hosted=hosted, prefetch=[place], own_aliases=aliases)


def small_update(gall, chip, entries, name):
    ne = len(entries)
    D = gall.shape[1]

    def body(chip_ref, gall_ref, *refs):
        ins, outs = refs[:3 * ne], refs[3 * ne:]
        ch = chip_ref[0]
        for e, (row0, kind, w, _, _) in enumerate(entries):
            r, width = w.shape

            def gsum(rs, cs):
                return gall_ref[rs, cs]

            if kind == "full":
                g = gsum(slice(row0, row0 + r), slice(0, D))
            elif kind == "cols":
                g = gsum(slice(row0, row0 + r), slice(0, width))
                for q in range(1, N_CHIPS):
                    g = jnp.where(ch == q, gsum(slice(row0, row0 + r), slice(q * width, (q + 1) * width)), g)
            else:
                per_row = D // width
                g = gsum(slice(row0, row0 + 1), slice(0, width))
                for q in range(1, N_CHIPS):
                    rr = row0 + q // per_row
                    cc = (q % per_row) * width
                    g = jnp.where(ch == q, gsum(slice(rr, rr + 1), slice(cc, cc + width)), g)
            d, mn, vn = _adamw_math(ins[3 * e][...], g, ins[3 * e + 1][...], ins[3 * e + 2][...])
            outs[4 * e][...] = g
            outs[4 * e + 1][...] = d
            outs[4 * e + 2][...] = mn
            outs[4 * e + 3][...] = vn

    vm = pl.BlockSpec(memory_space=pltpu.VMEM)
    args, out_shape = [], []
    for _, _, w, m, v in entries:
        args += [w, m, v]
        out_shape += [jax.ShapeDtypeStruct(w.shape, F32)] * 4
    return pl.pallas_call(
        body, name=name,
        in_specs=[pl.BlockSpec(memory_space=pltpu.SMEM), vm] + [vm] * (3 * ne),
        out_specs=[vm] * (4 * ne), out_shape=out_shape,
        compiler_params=pltpu.CompilerParams(vmem_limit_bytes=VMEM_LIMIT),
    )(chip, gall, *args)


def _pack_rows(items, width):
    rows, starts, at = [], [], 0
    for it in items:
        r = it.shape[0]
        pad = (-r) % SUBLANES
        starts.append(at)
        rows.append(it)
        if pad:
            rows.append(jnp.zeros((pad, width), F32))
        at += r + pad
    return jnp.concatenate(rows, axis=0), starts


def kernel(x, a_norm, a_w_in, a_conv, a_w_out, b_norm, b_w_pw1, b_b_pw1, b_conv, b_b_conv, b_ln_g, b_ln_b, b_w_pw2, b_b_pw2, ffn_norm, ffn_w_gate, ffn_w_up, ffn_w_down, final_norm, loss_target, m_a_norm, m_a_w_in, m_a_conv, m_a_w_out, m_b_norm, m_b_w_pw1, m_b_b_pw1, m_b_conv, m_b_b_conv, m_b_ln_g, m_b_ln_b, m_b_w_pw2, m_b_b_pw2, m_ffn_norm, m_ffn_w_gate, m_ffn_w_up, m_ffn_w_down, m_final_norm, v_a_norm, v_a_w_in, v_a_conv, v_a_w_out, v_b_norm, v_b_w_pw1, v_b_b_pw1, v_b_conv, v_b_b_conv, v_b_ln_g, v_b_ln_b, v_b_w_pw2, v_b_b_pw2, v_ffn_norm, v_ffn_w_gate, v_ffn_w_up, v_ffn_w_down, v_final_norm):
    T, D = x.shape[1], x.shape[2]
    Dq = D // N_CHIPS
    cx, cy, cc = lax.axis_index("x"), lax.axis_index("y"), lax.axis_index("c")
    chip = (2 * cx + cy).astype(jnp.int32).reshape(1)
    cidx = cc.astype(jnp.int32).reshape(1)
    h0 = x.reshape(T, D)
    tgt = loss_target.reshape(T, D)

    small_shards = [a_conv[0], b_norm, b_b_pw1.reshape(2, Dq), b_conv[0], b_b_conv, b_ln_g, b_ln_b, b_b_pw2]
    packed, st = _pack_rows(small_shards, Dq)

    tr = lambda t: jnp.swapaxes(t, 1, 2)
    w_gate, m_gate, v_gate = tr(ffn_w_gate), tr(m_ffn_w_gate), tr(v_ffn_w_gate)
    w_up, m_up, v_up = tr(ffn_w_up), tr(m_ffn_w_up), tr(v_ffn_w_up)
    bf = lambda t: t.astype(BF16)
    s_in, s_out, s_pw1, s_pw2 = bf(a_w_in[0]), bf(a_w_out[0]), bf(b_w_pw1[0]), bf(b_w_pw2[0])
    s_gate, s_up, s_down = [bf(w_gate[l]) for l in (0, 1)], [bf(w_up[l]) for l in (0, 1)], [bf(ffn_w_down[l]) for l in (0, 1)]

    n0, (g_in, g_out) = rms_fwd(h0, a_norm, "rms_a", hosted=[gather_whole([s_in], [s_out])])
    bcv, (gate0, g_out, sw) = mm_cols(n0, g_in, None, "mm_w_in",
                                      hosted=[gather_p1([s_gate[0]]), gather_p2([g_out]), gather_small(packed)])
    g_out = g_out.reshape(1, D, D)

    def whole(k, r):
        return jnp.transpose(sw[:, st[k]:st[k] + r, :], (1, 0, 2)).reshape(r, D)

    a_conv_f, b_norm_f = whole(0, 3), whole(1, 1)
    b_b_pw1_f = sw[:, st[2]:st[2] + 2, :].reshape(1, 2 * D)
    b_conv_f, b_b_conv_f, b_ln_g_f, b_ln_b_f, b_b_pw2_f = whole(3, b_conv.shape[1]), whole(4, 1), whole(5, 1), whole(6, 1), whole(7, 1)
    ya, (up0, gate0) = gateconv_fwd(bcv, a_conv_f, "gateconv_fwd", hosted=[gather_p1([s_up[0]]), gather_p2([gate0])])
    h1, (down0, up0) = mm_rows(ya[None], g_out, h0, None, "mm_w_out", hosted=[gather_p1([s_down[0]]), gather_p2([up0])])
    n1, fg0, fu0, gu0, h2, (down0, *later) = ffn_fwd(h1, ffn_norm[0:1], gate0, up0, gather_p2([down0]).awaited_first(), "ffn_fwd0",
                                                     hosted=[gather_p1([s_pw1, s_pw2, s_gate[1], s_up[1]])])
    n2, (g_pw1, g_pw2, gate1, up1) = rms_fwd(h2, b_norm_f, "rms_b", hosted=[gather_p2(later)])
    g_pw2 = g_pw2.reshape(1, D, D)
    ub, (down1,) = mm_cols(n2, g_pw1, b_b_pw1_f, "mm_pw1", hosted=[gather_p1([s_down[1]])])
    cu, sb, (down1,) = bconv_fwd(ub, b_conv_f, b_b_conv_f, b_ln_g_f, b_ln_b_f, "bconv_fwd", hosted=[gather_p2([down1])])
    h3, _ = mm_rows(sb[None], g_pw2, h2, b_b_pw2_f, "mm_pw2")
    n3, fg1, fu1, gu1, h4, _ = ffn_fwd(h3, ffn_norm[1:2], gate1, up1, down1, "ffn_fwd1")
    loss_part, dh4, d_final = loss_head(h4, final_norm.reshape(1, D), tgt, "loss_head")

    place = jnp.concatenate([chip, cidx])

    def pair_sums(ghs, from_sib, tags):
        return [pair_sum(g, r, cidx, "pair_sum_" + t) for g, r, t in zip(ghs, from_sib, tags)]

    def upd(w, m, v, bufs, parts, tag, hosted=()):
        res, xo = None, []
        for lyr, (b, p) in enumerate(zip(bufs, parts)):
            res, xo_l = adamw_reduce(w, m, v, b, p, place, lyr, res, "adamw_%s%d" % (tag, lyr), hosted=hosted if lyr == 0 else ())
            xo += xo_l
        return res, xo

    dg1, du1, dh3, d_fn1, _ = ffn_bwd(dh4, h3, ffn_norm[1:2], fg1, fu1, down1, gate1, up1, "ffn_bwd1")
    gh_down1, _ = tn_grad(gu1, dh4, N_CHIPS, True, "tn_down1")
    gh_gate1, _ = tn_grad(dg1, n3, N_CHIPS, True, "tn_gate1")
    gh_up1, _ = tn_grad(du1, n3, N_CHIPS, True, "tn_up1")
    f1 = [gh_gate1, gh_up1, gh_down1]

    dcu, d_ln_g, d_ln_b, d_b_conv, d_b_pw2, sib_f1 = pw2_ln_bwd(dh3, g_pw2, cu, b_ln_g_f, b_ln_b_f, "pw2_ln_bwd",
                                                                hosted=[sibling_halves(f1)])
    p_f1 = pair_sums(f1, sib_f1, ["gate1", "up1", "down1"])
    gh_pw2, _ = tn_grad_square(sb, dh3, N_CHIPS, "tn_pw2")
    dub, d_bconv_w, d_b_pw1, buf_f1 = bconv_bwd(dcu, ub, b_conv_f, "bconv_bwd", hosted=[scatter_p1(p_f1)])
    gh_pw1, buf_f1 = tn_grad(n2, dub, N_CHIPS, False, "tn_pw1", hosted=[scatter_p2(buf_f1)])
    b_grp = [gh_pw1, gh_pw2]
    dh2, d_b_norm, sib_b = nt_cols_rms(dub, g_pw1, h2, b_norm_f, dh3, "nt_pw1", hosted=[sibling_halves(b_grp)])
    p_b = pair_sums(b_grp, sib_b, ["pw1", "pw2"])

    dg0, du0, dh1, d_fn0, buf_b = ffn_bwd(dh2, h1, ffn_norm[0:1], fg0, fu0, down0, gate0, up0, "ffn_bwd0", hosted=[scatter_p1(p_b)])
    gh_down0, buf_b = tn_grad(gu0, dh2, N_CHIPS, True, "tn_down0", hosted=[scatter_p2(buf_b)])
    gh_gate0, sib_down0 = tn_grad(dg0, n1, N_CHIPS, True, "tn_gate0", hosted=[sibling_halves([gh_down0])])
    p_down0 = pair_sums([gh_down0], sib_down0, ["down0"])
    gh_up0, (buf_down0, sib_gate0) = tn_grad(du0, n1, N_CHIPS, True, "tn_up0",
                                             hosted=[scatter_p1(p_down0), sibling_halves([gh_gate0])])
    p_gate0 = pair_sums([gh_gate0], [sib_gate0], ["gate0"])
    dya, (buf_down0, sib_up0) = nt_rows(dh1, g_out, "nt_w_out",
                                        hosted=[scatter_p2([buf_down0]), sibling_halves([gh_up0])])
    p_up0 = pair_sums([gh_up0], [sib_up0], ["up0"])
    gh_out, _ = tn_grad_square(ya, dh1, N_CHIPS, "tn_w_out")
    dbcv, d_aconv_w, (buf_gate0, sib_out) = gateconv_bwd(dya[0], bcv, a_conv_f, "gateconv_bwd",
                                                         hosted=[scatter_p1(p_gate0), sibling_halves([gh_out])])
    p_out = pair_sums([gh_out], [sib_out], ["out"])
    gh_in, (buf_up0, buf_gate0) = tn_grad(n0, dbcv, N_CHIPS, False, "tn_w_in",
                                          hosted=[scatter_p1(p_up0), scatter_p2([buf_gate0])])
    sib_in = run_exchanges([sibling_halves([gh_in])], "reduce_in_siblings")
    p_in = pair_sums([gh_in], sib_in, ["in"])
    grad_x, d_a_norm, (buf_in, buf_out, buf_up0) = nt_cols_rms(
        dbcv, g_in, h0, a_norm, dh1, "nt_w_in", hosted=[scatter_p1(p_in + p_out), scatter_p2([buf_up0])])
    p_f0 = [p_gate0[0], p_up0[0], p_down0[0]]

    d_ffn_norm = jnp.concatenate([d_fn0, d_fn1], axis=0)
    small_grads = [d_a_norm, d_aconv_w, d_b_norm, d_b_pw1.reshape(2, D), d_bconv_w, d_b_conv, d_ln_g, d_ln_b, d_b_pw2,
                   d_ffn_norm, d_final, jnp.broadcast_to(loss_part, (1, D))]
    gpacked, gs = _pack_rows(small_grads, D)
    gall, (buf_in, buf_out) = small_allreduce(gpacked, "allreduce_small_grads", hosted=[scatter_p2([buf_in, buf_out])])
    buf_a, p_a = [buf_in, buf_out], [p_in[0], p_out[0]]

    r_gate, _ = upd(w_gate, m_gate, v_gate, [buf_gate0, buf_f1[0]], [p_f0[0], p_f1[0]], "gate")
    r_up, _ = upd(w_up, m_up, v_up, [buf_up0, buf_f1[1]], [p_f0[1], p_f1[1]], "up")
    r_down, _ = upd(ffn_w_down, m_ffn_w_down, v_ffn_w_down, [buf_down0, buf_f1[2]], [p_f0[2], p_f1[2]], "down")
    r_gate, r_up = [tr(t) for t in r_gate], [tr(t) for t in r_up]
    r_pw1, _ = upd(b_w_pw1, m_b_w_pw1, v_b_w_pw1, [buf_b[0]], [p_b[0]], "pw1")
    r_pw2, _ = upd(b_w_pw2, m_b_w_pw2, v_b_w_pw2, [buf_b[1]], [p_b[1]], "pw2")
    r_in, _ = upd(a_w_in, m_a_w_in, v_a_w_in, [buf_a[0]], [p_a[0]], "w_in")
    r_out, _ = upd(a_w_out, m_a_w_out, v_a_w_out, [buf_a[1]], [p_a[1]], "w_out")
    entries = [
        (gs[0], "full", a_norm, m_a_norm, v_a_norm),
        (gs[1], "cols", a_conv[0], m_a_conv[0], v_a_conv[0]),
        (gs[2], "cols", b_norm, m_b_norm, v_b_norm),
        (gs[3], "flat2", b_b_pw1, m_b_b_pw1, v_b_b_pw1),
        (gs[4], "cols", b_conv[0], m_b_conv[0], v_b_conv[0]),
        (gs[5], "cols", b_b_conv, m_b_b_conv, v_b_b_conv),
        (gs[6], "cols", b_ln_g, m_b_ln_g, v_b_ln_g),
        (gs[7], "cols", b_ln_b, m_b_ln_b, v_b_ln_b),
        (gs[8], "cols", b_b_pw2, m_b_b_pw2, v_b_b_pw2),
        (gs[9], "full", ffn_norm, m_ffn_norm, v_ffn_norm),
        (gs[10], "full", final_norm.reshape(1, D), m_final_norm.reshape(1, D), v_final_norm.reshape(1, D)),
    ]
    so = small_update(gall, chip, entries, "small_update")
    sm = [so[4 * e:4 * e + 4] for e in range(len(entries))]

    def shaped(e, like):
        return [t.reshape(like.shape) for t in sm[e]]

    r_a_norm, r_a_conv, r_b_norm, r_b_b_pw1 = shaped(0, a_norm), shaped(1, a_conv), shaped(2, b_norm), shaped(3, b_b_pw1)
    r_b_conv, r_b_b_conv, r_b_ln_g, r_b_ln_b = shaped(4, b_conv), shaped(5, b_b_conv), shaped(6, b_ln_g), shaped(7, b_ln_b)
    r_b_b_pw2, r_ffn_norm, r_final = shaped(8, b_b_pw2), shaped(9, ffn_norm), shaped(10, final_norm)

    loss = gall[gs[11], 0]
    order =[r_a_norm, r_in, r_a_conv, r_out, r_b_norm, r_pw1, r_b_b_pw1, r_b_conv, r_b_b_conv, r_b_ln_g, r_b_ln_b,
             r_pw2, r_b_b_pw2, r_ffn_norm, r_gate, r_up, r_down, r_final]
    outs = [loss, grad_x.reshape(x.shape)]
    for field in range(4):
        outs += [r[field] for r in order]
    return tuple(outs)
```

```python
import functools

import jax
import jax.numpy as jnp
from jax import lax
from jax.experimental import pallas as pl
from jax.experimental.pallas import tpu as pltpu

RMS_EPS = 1e-6
LN_EPS = 1e-5
ADAM_LR = 0.001
ADAM_B1 = 0.9
ADAM_B2 = 0.999
ADAM_EPS = 1e-08
ADAM_WD = 0.01
ADAM_STEP = 10

N_CHIPS = 4
N_DEV = 8
LANES = 128
SUBLANES = 8
HALO = 32
CONV_ROWS = 64
TOKEN_TILE = 512
WIDE_TOKEN_TILE = 1024
GRAD_TOKEN_TILE = 2048
FFN_ROW_CHUNKS = 2
ROW_TILE = 256
VMEM_LIMIT = 56 * 1024 * 1024
MESH = pl.DeviceIdType.MESH
BF16 = jnp.bfloat16
F32 = jnp.float32


def _tile(n, pref, mult=SUBLANES):
    t = min(n, pref) // mult * mult
    while n % t:
        t -= mult
    return t


def _params(sem):
    return pltpu.CompilerParams(dimension_semantics=sem, vmem_limit_bytes=VMEM_LIMIT)


def _sigmoid(x):
    return 0.5 * jnp.tanh(0.5 * x) + 0.5


class _Exchange:
    def __init__(self, ins, outs, aliases, n_sems, copies, then=None):
        self.ins, self.outs, self.aliases, self.n_sems, self.copies = list(ins), list(outs), dict(aliases), n_sems, copies
        self.then = then
        self.early = False

    def awaited_first(self):
        self.early = True
        return self

    def start(self, xi, xo, ssem, rsem):
        for cp in self.copies(xi, xo, ssem, rsem)[0]:
            cp.start()

    def finish(self, xi, xo, ssem, rsem):
        sends, recvs = self.copies(xi, xo, ssem, rsem)
        for cp in recvs:
            cp.wait_recv()
        if self.then is not None:
            sends2, recvs2 = self.then(xi, xo, ssem, rsem)
            for cp in sends2:
                cp.start()
            for cp in recvs2:
                cp.wait_recv()
            sends = sends + sends2
        for cp in sends:
            cp.wait_send()


def _call(body, name, grid, in_specs, out_specs, out_shape, args, sem, scratch_shapes=(), hosted=(), prefetch=(),
          own_aliases=None):
    in_specs, out_specs, out_shape = list(in_specs), list(out_specs), list(out_shape)
    scratch_shapes, hosted, prefetch = list(scratch_shapes), list(hosted), list(prefetch)
    n_pre, n_in, n_out, n_scr = len(prefetch), len(args), len(out_shape), len(scratch_shapes)
    x_in = [a for ex in hosted for a in ex.ins]
    x_out = [o for ex in hosted for o in ex.outs]
    aliases = {n_pre + i: o for i, o in (own_aliases or {}).items()}
    at_in, at_out = n_pre + n_in, n_out
    for ex in hosted:
        for i, o in ex.aliases.items():
            aliases[at_in + i] = at_out + o
        at_in += len(ex.ins)
        at_out += len(ex.outs)
    sems = [pltpu.SemaphoreType.DMA((ex.n_sems,)) for ex in hosted for _ in range(2)]

    def wrapped(*refs):
        pre, refs = refs[:n_pre], refs[n_pre:]
        ins, xi = refs[:n_in], refs[n_in:n_in + len(x_in)]
        refs = refs[n_in + len(x_in):]
        outs, xo = refs[:n_out], refs[n_out:n_out + len(x_out)]
        refs = refs[n_out + len(x_out):]
        scr, sm = refs[:n_scr], refs[n_scr:]
        views, a, b = [], 0, 0
        for e, ex in enumerate(hosted):
            views.append((xi[a:a + len(ex.ins)], xo[b:b + len(ex.outs)], sm[2 * e], sm[2 * e + 1]))
            a += len(ex.ins)
            b += len(ex.outs)
        first = last = None
        for ax, g in enumerate(grid):
            f, l = pl.program_id(ax) == 0, pl.program_id(ax) == g - 1
            first, last = (f, l) if first is None else (first & f, last & l)

        def begin():
            for ex, v in zip(hosted, views):
                ex.start(*v)
            for ex, v in zip(hosted, views):
                if ex.early:
                    ex.finish(*v)

        def end():
            for ex, v in zip(hosted, views):
                if not ex.early:
                    ex.finish(*v)

        if hosted and grid:
            pl.when(first)(begin)
        elif hosted:
            begin()
        early_refs = [r for ex, v in zip(hosted, views) if ex.early for r in v[1]]
        body(*pre, *ins, *outs, *scr, *early_refs)
        if hosted and grid:
            pl.when(last)(end)
        elif hosted:
            end()

    hbm = pl.BlockSpec(memory_space=pl.ANY)
    all_in, all_out = in_specs + [hbm] * len(x_in), out_specs + [hbm] * len(x_out)
    kw = dict(name=name, out_shape=out_shape + x_out, input_output_aliases=aliases,
              compiler_params=_params(tuple("arbitrary" for _ in grid) if hosted else sem))
    if prefetch:
        kw["grid_spec"] = pltpu.PrefetchScalarGridSpec(num_scalar_prefetch=n_pre, grid=grid, in_specs=all_in,
                                                       out_specs=all_out, scratch_shapes=scratch_shapes + sems)
    else:
        kw.update(grid=grid, in_specs=all_in, out_specs=all_out, scratch_shapes=scratch_shapes + sems)
    res = pl.pallas_call(wrapped, **kw)(*prefetch, *args, *x_in)
    return list(res[:n_out]), list(res[n_out:])


def rms_fwd(h, gain, name, hosted=()):
    T, D = h.shape
    tm = _tile(T, TOKEN_TILE)

    def body(h_ref, g_ref, o_ref):
        x = h_ref[...]
        r = lax.rsqrt(jnp.mean(x * x, axis=-1, keepdims=True) + RMS_EPS)
        o_ref[...] = (x * r * g_ref[...]).astype(o_ref.dtype)

    (n,), xo = _call(
        body, name, (T // tm,),
        [pl.BlockSpec((tm, D), lambda i: (i, 0)), pl.BlockSpec((1, D), lambda i: (0, 0))],
        [pl.BlockSpec((tm, D), lambda i: (i, 0))], [jax.ShapeDtypeStruct((T, D), BF16)],
        [h, gain], ("parallel",), hosted=hosted)
    return n, xo


def loss_head(h, gain, tgt, name):
    T, D = h.shape
    tm = _tile(T, TOKEN_TILE)

    def body(h_ref, g_ref, t_ref, loss_ref, dh_ref, dg_ref):
        i = pl.program_id(0)
        x = h_ref[...]
        g = g_ref[...]
        r = lax.rsqrt(jnp.mean(x * x, axis=-1, keepdims=True) + RMS_EPS)
        xhat = x * r
        diff = xhat * g - t_ref[...]
        part_loss = 0.5 * jnp.sum(jnp.mean(diff * diff, axis=-1, keepdims=True), axis=0, keepdims=True)
        dy = diff * (1.0 / D)
        dxhat = dy * g
        dh_ref[...] = r * (dxhat - xhat * jnp.mean(dxhat * xhat, axis=-1, keepdims=True))
        part = jnp.sum(dy * xhat, axis=0, keepdims=True)

        @pl.when(i == 0)
        def _():
            dg_ref[...] = part
            loss_ref[...] = part_loss

        @pl.when(i > 0)
        def _():
            dg_ref[...] += part
            loss_ref[...] += part_loss

    row = pl.BlockSpec((tm, D), lambda i: (i, 0))
    vec = pl.BlockSpec((1, D), lambda i: (0, 0))
    return pl.pallas_call(
        body, name=name, grid=(T // tm,),
        in_specs=[row, vec, row],
        out_specs=[pl.BlockSpec((1, 1), lambda i: (0, 0)), row, vec],
        out_shape=[jax.ShapeDtypeStruct((1, 1), F32), jax.ShapeDtypeStruct((T, D), F32),
                   jax.ShapeDtypeStruct((1, D), F32)],
        compiler_params=_params(("arbitrary",)),
    )(h, gain, tgt)


def _prev_halo_spec(tm, width):
    return pl.BlockSpec((HALO, width), lambda i: (jnp.maximum(i * (tm // HALO) - 1, 0), 0))


def _next_halo_spec(tm, width, T):
    return pl.BlockSpec((HALO, width), lambda i: (jnp.minimum((i + 1) * (tm // HALO), T // HALO - 1), 0))


def _shifted(win, off, rows):
    if off % SUBLANES == 0:
        return win[off:off + rows]
    n = win.shape[0]
    return pltpu.roll(win, (n - off) % n, 0)[:rows]


def _rowsum8(x):
    acc = x[0:SUBLANES]
    for q in range(1, x.shape[0] // SUBLANES):
        acc = acc + x[q * SUBLANES:(q + 1) * SUBLANES]
    return acc


def _conv_loops(tm, D, per_block):
    def chunk(r, carry):
        t0 = pl.multiple_of(r * CONV_ROWS, CONV_ROWS)
        for lb in range(D // LANES):
            per_block(t0, slice(lb * LANES, (lb + 1) * LANES))
        return carry

    lax.fori_loop(0, tm // CONV_ROWS, chunk, 0)


def gateconv_fwd(bcv, w, name, hosted=()):
    T, D3 = bcv.shape
    D = D3 // 3
    K = w.shape[0]
    tm = _tile(T, TOKEN_TILE)

    def body(x_ref, halo_ref, w_ref, y_ref, pad_ref):
        i = pl.program_id(0)
        pad_ref[HALO:, :] = x_ref[:, D:2 * D] * x_ref[:, 2 * D:]
        pad_ref[:HALO, :] = jnp.where(i > 0, halo_ref[:, D:2 * D] * halo_ref[:, 2 * D:], 0.0)

        def block(t0, ls):
            win = pad_ref[pl.ds(t0, CONV_ROWS + HALO), ls]
            acc = jnp.zeros((CONV_ROWS, LANES), F32)
            for k in range(K):
                acc = acc + w_ref[k:k + 1, ls] * _shifted(win, HALO - (K - 1) + k, CONV_ROWS)
            y_ref[pl.ds(t0, CONV_ROWS), ls] = (x_ref[pl.ds(t0, CONV_ROWS), ls] * acc).astype(y_ref.dtype)

        _conv_loops(tm, D, block)

    (y,), xo = _call(
        body, name, (T // tm,),
        [pl.BlockSpec((tm, D3), lambda i: (i, 0)), _prev_halo_spec(tm, D3), pl.BlockSpec((K, D), lambda i: (0, 0))],
        [pl.BlockSpec((tm, D), lambda i: (i, 0))], [jax.ShapeDtypeStruct((T, D), BF16)],
        [bcv, bcv, w], ("parallel",), [pltpu.VMEM((tm + HALO, D), F32)], hosted=hosted)
    return y, xo


def gateconv_bwd(dy, bcv, w, name, hosted=()):
    T, D3 = bcv.shape
    D = D3 // 3
    K = w.shape[0]
    tm = _tile(T, TOKEN_TILE)
    nt = T // tm

    def body(dy_ref, dyn_ref, x_ref, xp_ref, xn_ref, w_ref, o_ref, dw_ref, cv_ref, dc_ref, wacc_ref):
        i = pl.program_id(0)
        cv_ref[HALO:, :] = x_ref[:, D:2 * D] * x_ref[:, 2 * D:]
        cv_ref[:HALO, :] = jnp.where(i > 0, xp_ref[:, D:2 * D] * xp_ref[:, 2 * D:], 0.0)
        dc_ref[:tm, :] = dy_ref[...] * x_ref[:, :D]
        dc_ref[tm:, :] = jnp.where(i < nt - 1, dyn_ref[...] * xn_ref[:, :D], 0.0)

        @pl.when(i == 0)
        def _():
            wacc_ref[...] = jnp.zeros_like(wacc_ref)

        def block(t0, ls):
            cwin = cv_ref[pl.ds(t0, CONV_ROWS + HALO), ls]
            dwin = dc_ref[pl.ds(t0, CONV_ROWS + HALO), ls]
            dcon = dwin[:CONV_ROWS]
            conv = jnp.zeros((CONV_ROWS, LANES), F32)
            dcv = jnp.zeros((CONV_ROWS, LANES), F32)
            for k in range(K):
                wk = w_ref[k:k + 1, ls]
                cs = _shifted(cwin, HALO - (K - 1) + k, CONV_ROWS)
                conv = conv + wk * cs
                dcv = dcv + wk * _shifted(dwin, (K - 1) - k, CONV_ROWS)
                wacc_ref[k * SUBLANES:(k + 1) * SUBLANES, ls] += _rowsum8(dcon * cs)
            rows = pl.ds(t0, CONV_ROWS)
            o_ref[rows, ls] = (dy_ref[rows, ls] * conv).astype(o_ref.dtype)
            o_ref[rows, pl.ds(D + ls.start, LANES)] = (dcv * x_ref[rows, pl.ds(2 * D + ls.start, LANES)]).astype(o_ref.dtype)
            o_ref[rows, pl.ds(2 * D + ls.start, LANES)] = (dcv * x_ref[rows, pl.ds(D + ls.start, LANES)]).astype(o_ref.dtype)

        _conv_loops(tm, D, block)

        @pl.when(i == nt - 1)
        def _():
            for k in range(K):
                dw_ref[k:k + 1, :] = jnp.sum(wacc_ref[k * SUBLANES:(k + 1) * SUBLANES, :], axis=0, keepdims=True)

    (dx, dw), xo = _call(
        body, name, (nt,),
        [pl.BlockSpec((tm, D), lambda i: (i, 0)), _next_halo_spec(tm, D, T),
         pl.BlockSpec((tm, D3), lambda i: (i, 0)), _prev_halo_spec(tm, D3), _next_halo_spec(tm, D3, T),
         pl.BlockSpec((K, D), lambda i: (0, 0))],
        [pl.BlockSpec((tm, D3), lambda i: (i, 0)), pl.BlockSpec((K, D), lambda i: (0, 0))],
        [jax.ShapeDtypeStruct((T, D3), BF16), jax.ShapeDtypeStruct((K, D), F32)],
        [dy, dy, bcv, bcv, bcv, w], ("arbitrary",),
        [pltpu.VMEM((tm + HALO, D), F32), pltpu.VMEM((tm + HALO, D), F32), pltpu.VMEM((K * SUBLANES, D), F32)],
        hosted=hosted)
    return dx, dw, xo


def bconv_fwd(u, w, b_conv, ln_g, ln_b, name, hosted=()):
    T, D2 = u.shape
    D = D2 // 2
    K = w.shape[0]
    tm = _tile(T, TOKEN_TILE)

    def body(u_ref, halo_ref, w_ref, bc_ref, g_ref, b_ref, cu_ref, s_ref, pad_ref):
        i = pl.program_id(0)
        pad_ref[HALO:, :] = u_ref[:, :D] * _sigmoid(u_ref[:, D:])
        pad_ref[:HALO, :] = jnp.where(i > 0, halo_ref[:, :D] * _sigmoid(halo_ref[:, D:]), 0.0)

        def block(t0, ls):
            win = pad_ref[pl.ds(t0, CONV_ROWS + HALO), ls]
            acc = jnp.zeros((CONV_ROWS, LANES), F32)
            for k in range(K):
                acc = acc + w_ref[k:k + 1, ls] * _shifted(win, HALO - (K - 1) + k, CONV_ROWS)
            cu_ref[pl.ds(t0, CONV_ROWS), ls] = acc + bc_ref[:, ls]

        _conv_loops(tm, D, block)
        cu = cu_ref[...]
        mu = jnp.mean(cu, axis=-1, keepdims=True)
        xc = cu - mu
        rstd = lax.rsqrt(jnp.mean(xc * xc, axis=-1, keepdims=True) + LN_EPS)
        ln = xc * rstd * g_ref[...] + b_ref[...]
        s_ref[...] = (ln * _sigmoid(ln)).astype(s_ref.dtype)

    vec = pl.BlockSpec((1, D), lambda i: (0, 0))
    row = pl.BlockSpec((tm, D), lambda i: (i, 0))
    (cu, s), xo = _call(
        body, name, (T // tm,),
        [pl.BlockSpec((tm, D2), lambda i: (i, 0)), _prev_halo_spec(tm, D2), pl.BlockSpec((K, D), lambda i: (0, 0)), vec, vec, vec],
        [row, row], [jax.ShapeDtypeStruct((T, D), F32), jax.ShapeDtypeStruct((T, D), BF16)],
        [u, u, w, b_conv, ln_g, ln_b], ("parallel",), [pltpu.VMEM((tm + HALO, D), F32)], hosted=hosted)
    return cu, s, xo


def pw2_ln_bwd(dy, w, cu, ln_g, ln_b, name, hosted=()):
    T, D = cu.shape
    tm = _tile(T, TOKEN_TILE)

    def body(dy_ref, w_ref, cu_ref, g_ref, b_ref, dcu_ref, dg_ref, db_ref, dbc_ref, dbo_ref):
        i = pl.program_id(0)
        dy_ = dy_ref[...]
        ds = lax.dot_general(dy_.astype(BF16), w_ref[0], _NT, preferred_element_type=F32)
        cu_ = cu_ref[...]
        mu = jnp.mean(cu_, axis=-1, keepdims=True)
        xc = cu_ - mu
        rstd = lax.rsqrt(jnp.mean(xc * xc, axis=-1, keepdims=True) + LN_EPS)
        xh = xc * rstd
        ln = xh * g_ref[...] + b_ref[...]
        sg = _sigmoid(ln)
        dl = ds * (sg * (1.0 + ln * (1.0 - sg)))
        dxh = dl * g_ref[...]
        dcu = rstd * (dxh - jnp.mean(dxh, axis=-1, keepdims=True) - xh * jnp.mean(dxh * xh, axis=-1, keepdims=True))
        dcu_ref[...] = dcu
        pg = jnp.sum(dl * xh, axis=0, keepdims=True)
        pb = jnp.sum(dl, axis=0, keepdims=True)
        pc = jnp.sum(dcu, axis=0, keepdims=True)
        po = jnp.sum(dy_, axis=0, keepdims=True)

        @pl.when(i == 0)
        def _():
            dg_ref[...] = pg
            db_ref[...] = pb
            dbc_ref[...] = pc
            dbo_ref[...] = po

        @pl.when(i > 0)
        def _():
            dg_ref[...] += pg
            db_ref[...] += pb
            dbc_ref[...] += pc
            dbo_ref[...] += po

    vec = pl.BlockSpec((1, D), lambda i: (0, 0))
    row = pl.BlockSpec((tm, D), lambda i: (i, 0))
    vshape = jax.ShapeDtypeStruct((1, D), F32)
    outs, xo = _call(
        body, name, (T // tm,), [row, pl.BlockSpec((1, D, D), lambda i: (0, 0, 0)), row, vec, vec], [row, vec, vec, vec, vec],
        [jax.ShapeDtypeStruct((T, D), F32), vshape, vshape, vshape, vshape], [dy, w, cu, ln_g, ln_b], ("arbitrary",),
        hosted=hosted)
    return (*outs, xo)


def bconv_bwd(dcu, u, w, name, hosted=()):
    T, D2 = u.shape
    D = D2 // 2
    K = w.shape[0]
    tm = _tile(T, TOKEN_TILE)
    nt = T // tm

    def body(dc_ref, dcn_ref, u_ref, up_ref, w_ref, du_ref, dw_ref, db_ref, glu_ref, dpad_ref, dglu_ref, wacc_ref):
        i = pl.program_id(0)
        glu_ref[HALO:, :] = u_ref[:, :D] * _sigmoid(u_ref[:, D:])
        glu_ref[:HALO, :] = jnp.where(i > 0, up_ref[:, :D] * _sigmoid(up_ref[:, D:]), 0.0)
        dpad_ref[:tm, :] = dc_ref[...]
        dpad_ref[tm:, :] = jnp.where(i < nt - 1, dcn_ref[...], 0.0)

        @pl.when(i == 0)
        def _():
            wacc_ref[...] = jnp.zeros_like(wacc_ref)

        def block(t0, ls):
            gwin = glu_ref[pl.ds(t0, CONV_ROWS + HALO), ls]
            dwin = dpad_ref[pl.ds(t0, CONV_ROWS + HALO), ls]
            dcur = dwin[:CONV_ROWS]
            dglu = jnp.zeros((CONV_ROWS, LANES), F32)
            for k in range(K):
                dglu = dglu + w_ref[k:k + 1, ls] * _shifted(dwin, (K - 1) - k, CONV_ROWS)
                gs = _shifted(gwin, HALO - (K - 1) + k, CONV_ROWS)
                wacc_ref[k * SUBLANES:(k + 1) * SUBLANES, ls] += _rowsum8(dcur * gs)
            dglu_ref[pl.ds(t0, CONV_ROWS), ls] = dglu

        _conv_loops(tm, D, block)
        dglu = dglu_ref[...]
        a = u_ref[:, :D]
        sg = _sigmoid(u_ref[:, D:])
        da = dglu * sg
        dg = dglu * a * (sg * (1.0 - sg))
        du_ref[:, :D] = da.astype(du_ref.dtype)
        du_ref[:, D:] = dg.astype(du_ref.dtype)
        pa = jnp.sum(da, axis=0, keepdims=True)
        pg = jnp.sum(dg, axis=0, keepdims=True)

        @pl.when(i == 0)
        def _():
            db_ref[:, :D] = pa
            db_ref[:, D:] = pg

        @pl.when(i > 0)
        def _():
            db_ref[:, :D] += pa
            db_ref[:, D:] += pg

        @pl.when(i == nt - 1)
        def _():
            for k in range(K):
                dw_ref[k:k + 1, :] = jnp.sum(wacc_ref[k * SUBLANES:(k + 1) * SUBLANES, :], axis=0, keepdims=True)

    (du, dw, db), xo = _call(
        body, name, (nt,),
        [pl.BlockSpec((tm, D), lambda i: (i, 0)), _next_halo_spec(tm, D, T),
         pl.BlockSpec((tm, D2), lambda i: (i, 0)), _prev_halo_spec(tm, D2), pl.BlockSpec((K, D), lambda i: (0, 0))],
        [pl.BlockSpec((tm, D2), lambda i: (i, 0)), pl.BlockSpec((K, D), lambda i: (0, 0)), pl.BlockSpec((1, D2), lambda i: (0, 0))],
        [jax.ShapeDtypeStruct((T, D2), BF16), jax.ShapeDtypeStruct((K, D), F32), jax.ShapeDtypeStruct((1, D2), F32)],
        [dcu, dcu, u, u, w], ("arbitrary",),
        [pltpu.VMEM((tm + HALO, D), F32), pltpu.VMEM((tm + HALO, D), F32), pltpu.VMEM((tm, D), F32),
         pltpu.VMEM((K * SUBLANES, D), F32)], hosted=hosted)
    return du, dw, db, xo


def mm_cols(a, w, bias, name, hosted=()):
    T, K = a.shape
    S, _, n = w.shape
    tm = _tile(T, WIDE_TOKEN_TILE)

    def body(*refs):
        a_ref, w_ref = refs[:2]
        o_ref = refs[-1]
        acc = jnp.dot(a_ref[...], w_ref[...], preferred_element_type=F32)
        if bias is not None:
            acc = acc + refs[2][...]
        o_ref[...] = acc

    in_specs = [pl.BlockSpec((tm, K), lambda s, i: (i, 0)), pl.BlockSpec((None, K, n), lambda s, i: (s, 0, 0))]
    args = [a, w]
    if bias is not None:
        in_specs.append(pl.BlockSpec((1, n), lambda s, i: (0, s)))
        args.append(bias)
    (out,), xo = _call(body, name, (S, T // tm), in_specs, [pl.BlockSpec((tm, n), lambda s, i: (i, s))],
                       [jax.ShapeDtypeStruct((T, S * n), F32)], args, ("parallel", "parallel"), hosted=hosted)
    return out, xo


def _load_weights(pairs, sems, S, i, s):
    def copies(seg):
        return [pltpu.make_async_copy(src.at[seg], dst.at[seg], sems.at[k, seg]) for k, (src, dst) in enumerate(pairs)]

    @pl.when((i == 0) & (s == 0))
    def _():
        for seg in range(S):
            for cp in copies(seg):
                cp.start()

    @pl.when((i == 0) & (s < S))
    def _():
        for cp in copies(s):
            cp.wait()


def ffn_fwd(h, gain, wg, wu, wd, name, hosted=()):
    T, D = h.shape
    S, f, _ = wg.shape
    tm = _tile(T, TOKEN_TILE)
    rc = tm // FFN_ROW_CHUNKS
    chunks = [slice(r * rc, (r + 1) * rc) for r in range(FFN_ROW_CHUNKS)]
    wd_arrives = isinstance(wd, _Exchange)
    weights = [wg, wu] if wd_arrives else [wg, wu, wd]
    hosted = ([wd] if wd_arrives else []) + list(hosted)

    def body(h_ref, gain_ref, *refs):
        nw = len(weights)
        wg_hbm, wu_hbm = refs[:2]
        wd_hbm = refs[-1] if wd_arrives else refs[2]
        n_ref, g_ref, u_ref, gu_ref, o_ref, wg_v, wu_v, wd_v, sems = refs[nw:nw + 9]
        i, s = pl.program_id(0), pl.program_id(1)
        _load_weights([(wg_hbm, wg_v), (wu_hbm, wu_v), (wd_hbm, wd_v)], sems, S, i, s)

        @pl.when(s == 0)
        def _():
            x = h_ref[...]
            r = lax.rsqrt(jnp.mean(x * x, axis=-1, keepdims=True) + RMS_EPS)
            n_ref[...] = (x * r * gain_ref[...]).astype(n_ref.dtype)

        parts = []
        for rows in chunks:
            a = n_ref[rows, :]
            g = lax.dot_general(a, wg_v[s], _NT, preferred_element_type=F32)
            u = lax.dot_general(a, wu_v[s], _NT, preferred_element_type=F32)
            gu = (g * _sigmoid(g) * u).astype(gu_ref.dtype)
            g_ref[rows, :] = g.astype(g_ref.dtype)
            u_ref[rows, :] = u.astype(u_ref.dtype)
            gu_ref[rows, :] = gu
            parts.append(jnp.dot(gu, wd_v[s], preferred_element_type=F32))

        @pl.when(s == 0)
        def _():
            for rows, part in zip(chunks, parts):
                o_ref[rows, :] = h_ref[rows, :] + part

        @pl.when(s > 0)
        def _():
            for rows, part in zip(chunks, parts):
                o_ref[rows, :] += part

    row = pl.BlockSpec((tm, D), lambda i, s: (i, 0))
    seg = pl.BlockSpec((None, tm, f), lambda i, s: (s, i, 0))
    hbm = pl.BlockSpec(memory_space=pl.ANY)
    segs = jax.ShapeDtypeStruct((S, T, f), BF16)
    outs, xo = _call(
        body, name, (T // tm, S),
        [row, pl.BlockSpec((1, D), lambda i, s: (0, 0))] + [hbm] * len(weights), [row, seg, seg, seg, row],
        [jax.ShapeDtypeStruct((T, D), BF16), segs, segs, segs, jax.ShapeDtypeStruct((T, D), F32)],
        [h, gain] + weights, ("arbitrary", "arbitrary"),
        [pltpu.VMEM((S, f, D), BF16), pltpu.VMEM((S, f, D), BF16), pltpu.VMEM((S, f, D), BF16), pltpu.SemaphoreType.DMA((3, S))],
        hosted=hosted)
    return (*outs, xo)


def ffn_bwd(dy, h, gain, g, u, wd, wg, wu, name, hosted=()):
    T, D = h.shape
    S, f, _ = wg.shape
    tm = _tile(T, TOKEN_TILE)
    nt = T // tm
    rc = tm // FFN_ROW_CHUNKS
    chunks = [slice(r * rc, (r + 1) * rc) for r in range(FFN_ROW_CHUNKS)]

    def body(dy_ref, h_ref, gain_ref, g_ref, u_ref, wd_hbm, wg_hbm, wu_hbm, dg_ref, du_ref, dh_ref, dgain_ref,
             wd_v, wg_v, wu_v, dyb_ref, dgs_ref, dus_ref, sems):
        i, s = pl.program_id(0), pl.program_id(1)
        _load_weights([(wd_hbm, wd_v), (wg_hbm, wg_v), (wu_hbm, wu_v)], sems, S, i, s)

        def first_stage(seg, slot):
            for rows in chunks:
                dgu = lax.dot_general(dyb_ref[rows, :], wd_v[seg], _NT, preferred_element_type=F32)
                gv = g_ref[rows, :].astype(F32)
                sg = _sigmoid(gv)
                dg = (dgu * u_ref[rows, :].astype(F32) * (sg * (1.0 + gv * (1.0 - sg)))).astype(dg_ref.dtype)
                du = (dgu * (gv * sg)).astype(du_ref.dtype)
                dg_ref[rows, :] = dg
                du_ref[rows, :] = du
                dgs_ref[slot, rows, :] = dg
                dus_ref[slot, rows, :] = du

        def second_stage(seg, slot):
            for rows in chunks:
                dh_ref[rows, :] += (jnp.dot(dgs_ref[slot, rows, :], wg_v[seg], preferred_element_type=F32)
                                    + jnp.dot(dus_ref[slot, rows, :], wu_v[seg], preferred_element_type=F32))

        @pl.when(s == 0)
        def _():
            dyb_ref[...] = dy_ref[...].astype(dyb_ref.dtype)
            dh_ref[...] = jnp.zeros_like(dh_ref)
            first_stage(0, 0)

        @pl.when((s > 0) & (s < S))
        def _():
            second_stage(s - 1, (s - 1) % 2)
            first_stage(s, s % 2)

        @pl.when(s == S)
        def _():
            second_stage(S - 1, (S - 1) % 2)
            dn = dh_ref[...]
            x = h_ref[...]
            r = lax.rsqrt(jnp.mean(x * x, axis=-1, keepdims=True) + RMS_EPS)
            xhat = x * r
            dxhat = dn * gain_ref[...]
            dh_ref[...] = dy_ref[...] + r * (dxhat - xhat * jnp.mean(dxhat * xhat, axis=-1, keepdims=True))
            pg = jnp.sum(dn * xhat, axis=0, keepdims=True)

            @pl.when(i == 0)
            def _():
                dgain_ref[...] = pg

            @pl.when(i > 0)
            def _():
                dgain_ref[...] += pg

    row = pl.BlockSpec((tm, D), lambda i, s: (i, 0))
    vec = pl.BlockSpec((1, D), lambda i, s: (0, 0))
    seg = pl.BlockSpec((None, tm, f), lambda i, s: (jnp.minimum(s, S - 1), i, 0))
    hbm = pl.BlockSpec(memory_space=pl.ANY)
    segs = jax.ShapeDtypeStruct((S, T, f), BF16)
    outs, xo = _call(
        body, name, (nt, S + 1),
        [row, row, vec, seg, seg, hbm, hbm, hbm], [seg, seg, row, vec],
        [segs, segs, jax.ShapeDtypeStruct((T, D), F32), jax.ShapeDtypeStruct((1, D), F32)],
        [dy, h, gain, g, u, wd, wg, wu], ("arbitrary", "arbitrary"),
        [pltpu.VMEM((S, f, D), BF16), pltpu.VMEM((S, f, D), BF16), pltpu.VMEM((S, f, D), BF16),
         pltpu.VMEM((tm, D), BF16), pltpu.VMEM((2, tm, f), BF16), pltpu.VMEM((2, tm, f), BF16),
         pltpu.SemaphoreType.DMA((3, S))], hosted=hosted)
    return (*outs, xo)


def mm_rows(a, w, res, bias, name, hosted=()):
    S, T, k = a.shape
    N = w.shape[-1]
    tm = _tile(T, TOKEN_TILE)

    def body(*refs):
        a_ref, w_ref, r_ref = refs[:3]
        o_ref = refs[-1]
        s = pl.program_id(1)
        acc = jnp.dot(a_ref[...], w_ref[...], preferred_element_type=F32)

        @pl.when(s == 0)
        def _():
            base = r_ref[...]
            if bias is not None:
                base = base + refs[3][...]
            o_ref[...] = base + acc

        @pl.when(s > 0)
        def _():
            o_ref[...] += acc

    in_specs = [pl.BlockSpec((None, tm, k), lambda i, s: (s, i, 0)),
                pl.BlockSpec((None, k, N), lambda i, s: (s, 0, 0)),
                pl.BlockSpec((tm, N), lambda i, s: (i, 0))]
    args = [a, w, res]
    if bias is not None:
        in_specs.append(pl.BlockSpec((1, N), lambda i, s: (0, 0)))
        args.append(bias)
    (out,), xo = _call(body, name, (T // tm, S), in_specs, [pl.BlockSpec((tm, N), lambda i, s: (i, 0))],
                       [jax.ShapeDtypeStruct((T, N), F32)], args, ("parallel", "arbitrary"), hosted=hosted)
    return out, xo


_NT = (((1,), (1,)), ((), ()))
_TN = (((0,), (0,)), ((), ()))


def nt_rows(dy, w, name, hosted=()):
    T, N = dy.shape
    S, k, _ = w.shape
    tm = _tile(T, TOKEN_TILE)

    def body(dy_ref, w_ref, o_ref):
        o_ref[...] = lax.dot_general(dy_ref[...].astype(BF16), w_ref[...], _NT, preferred_element_type=F32)

    (out,), xo = _call(
        body, name, (T // tm, S),
        [pl.BlockSpec((tm, N), lambda i, s: (i, 0)), pl.BlockSpec((None, k, N), lambda i, s: (s, 0, 0))],
        [pl.BlockSpec((None, tm, k), lambda i, s: (s, i, 0))], [jax.ShapeDtypeStruct((S, T, k), F32)],
        [dy, w], ("parallel", "parallel"), hosted=hosted)
    return out, xo


def nt_cols_rms(dy, w, h, gain, dres, name, hosted=()):
    T, K = h.shape
    S, _, n = w.shape
    tm = _tile(T, TOKEN_TILE)

    def body(dy_ref, w_ref, h_ref, gain_ref, dres_ref, dh_ref, dgain_ref):
        i = pl.program_id(0)
        dn = None
        for s in range(S):
            part = lax.dot_general(dy_ref[:, s * n:(s + 1) * n], w_ref[s], _NT, preferred_element_type=F32)
            dn = part if dn is None else dn + part
        x = h_ref[...]
        r = lax.rsqrt(jnp.mean(x * x, axis=-1, keepdims=True) + RMS_EPS)
        xhat = x * r
        dxhat = dn * gain_ref[...]
        dh_ref[...] = dres_ref[...] + r * (dxhat - xhat * jnp.mean(dxhat * xhat, axis=-1, keepdims=True))
        pg = jnp.sum(dn * xhat, axis=0, keepdims=True)

        @pl.when(i == 0)
        def _():
            dgain_ref[...] = pg

        @pl.when(i > 0)
        def _():
            dgain_ref[...] += pg

    row = pl.BlockSpec((tm, K), lambda i: (i, 0))
    vec = pl.BlockSpec((1, K), lambda i: (0, 0))
    (dh, dgain), xo = _call(
        body, name, (T // tm,),
        [pl.BlockSpec((tm, S * n), lambda i: (i, 0)), pl.BlockSpec((S, K, n), lambda i: (0, 0, 0)), row, vec, row],
        [row, vec], [jax.ShapeDtypeStruct((T, K), F32), jax.ShapeDtypeStruct((1, K), F32)],
        [dy, w, h, gain, dres], ("arbitrary",), hosted=hosted)
    return dh, dgain, xo


def tn_grad(a, dy, S, a_by_seg, name, hosted=()):
    T = dy.shape[0] if dy.ndim == 2 else dy.shape[1]
    tt = _tile(T, GRAD_TOKEN_TILE)
    if a_by_seg:
        R = a.shape[1] // S if a.ndim == 2 else a.shape[2]
        C = dy.shape[1]
        a_spec = pl.BlockSpec((tt, R), lambda s, t: (t, s)) if a.ndim == 2 else pl.BlockSpec((None, tt, R), lambda s, t: (s, t, 0))
        b_spec = pl.BlockSpec((tt, C), lambda s, t: (t, 0))
    else:
        R = a.shape[1]
        C = dy.shape[1] // S if dy.ndim == 2 else dy.shape[2]
        a_spec = pl.BlockSpec((tt, R), lambda s, t: (t, 0))
        b_spec = pl.BlockSpec((tt, C), lambda s, t: (t, s)) if dy.ndim == 2 else pl.BlockSpec((None, tt, C), lambda s, t: (s, t, 0))
    Rh = R // 2
    nt = T // tt

    def body(a_ref, b_ref, o_ref, acc_ref):
        t = pl.program_id(1)
        part = lax.dot_general(a_ref[...], b_ref[...].astype(BF16), _TN, preferred_element_type=F32)

        @pl.when(t == 0)
        def _():
            acc_ref[...] = part

        @pl.when(t > 0)
        def _():
            acc_ref[...] += part

        @pl.when(t == nt - 1)
        def _():
            o_ref[0] = acc_ref[:Rh, :].astype(o_ref.dtype)
            o_ref[1] = acc_ref[Rh:, :].astype(o_ref.dtype)

    (gh,), xo = _call(
        body, name, (S, nt), [a_spec, b_spec], [pl.BlockSpec((2, None, Rh, C), lambda s, t: (0, s, 0, 0))],
        [jax.ShapeDtypeStruct((2, S, Rh, C), BF16)], [a, dy], ("parallel", "arbitrary"), [pltpu.VMEM((R, C), F32)],
        hosted=hosted)
    return gh, xo


def tn_grad_square(a, dy, S, name, hosted=()):
    T, K = a.shape
    N = dy.shape[1]
    tt = _tile(T, GRAD_TOKEN_TILE)
    nt = T // tt
    Rh = K // S // 2

    def body(a_ref, b_ref, o_ref, acc_ref):
        t = pl.program_id(0)
        part = lax.dot_general(a_ref[...], b_ref[...].astype(BF16), _TN, preferred_element_type=F32)

        @pl.when(t == 0)
        def _():
            acc_ref[...] = part

        @pl.when(t > 0)
        def _():
            acc_ref[...] += part

        @pl.when(t == nt - 1)
        def _():
            for s in range(S):
                for hf in range(2):
                    r0 = (2 * s + hf) * Rh
                    o_ref[hf, s] = acc_ref[r0:r0 + Rh, :].astype(o_ref.dtype)

    (gh,), xo = _call(
        body, name, (nt,), [pl.BlockSpec((tt, K), lambda t: (t, 0)), pl.BlockSpec((tt, N), lambda t: (t, 0))],
        [pl.BlockSpec((2, S, Rh, N), lambda t: (0, 0, 0, 0))], [jax.ShapeDtypeStruct((2, S, Rh, N), BF16)],
        [a, dy], ("arbitrary",), [pltpu.VMEM((K, N), F32)], hosted=hosted)
    return gh, xo


def _place():
    x, y, c = lax.axis_index("x"), lax.axis_index("y"), lax.axis_index("c")
    chips = [(1 - x, y), (x, 1 - y), (1 - x, 1 - y)]
    return x, y, c, chips


def _remote(src, dst, send_sem, recv_sem, dev):
    return pltpu.make_async_remote_copy(src_ref=src, dst_ref=dst, send_sem=send_sem, recv_sem=recv_sem,
                                        device_id=dev, device_id_type=MESH)


def small_allreduce(v, name, hosted=()):
    rows, W = v.shape

    def body(v_ref, o_ref, sib_ref, pair_ref, chips_ref, send_sems, recv_sems):
        x, y, c, chips = _place()
        me = 2 * x + y
        swap = _remote(v_ref, sib_ref, send_sems.at[3], recv_sems.at[3], (x, y, 1 - c))
        swap.start()
        swap.wait()
        mine, other = v_ref[...], sib_ref[...]
        pair_ref[...] = jnp.where(c == 0, mine, other) + jnp.where(c == 0, other, mine)
        sends = []
        for j, (px, py) in enumerate(chips):
            cp = _remote(pair_ref, chips_ref.at[me], send_sems.at[j], recv_sems.at[j], (px, py, c))
            cp.start()
            sends.append(cp)
        chips_ref[me] = pair_ref[...]
        for j, (px, py) in enumerate(chips):
            blk = chips_ref.at[2 * px + py]
            _remote(blk, blk, send_sems.at[j], recv_sems.at[j], (px, py, c)).wait_recv()
        for cp in sends:
            cp.wait_send()
        o_ref[...] = (chips_ref[0] + chips_ref[1]) + (chips_ref[2] + chips_ref[3])

    vm = pl.BlockSpec(memory_space=pltpu.VMEM)
    (out,), xo = _call(
        body, name, (), [vm], [vm], [jax.ShapeDtypeStruct((rows, W), F32)], [v], (),
        [pltpu.VMEM((rows, W), F32), pltpu.VMEM((rows, W), F32), pltpu.VMEM((N_CHIPS, rows, W), F32),
         pltpu.SemaphoreType.DMA((4,)), pltpu.SemaphoreType.DMA((4,))], hosted=hosted)
    return out, xo


def _gather_p1_copies(srcs, bufs, ssem, rsem, base):
    x, y, c, chips = _place()
    me, sib = 2 * x + y, (x, y, 1 - c)
    sends, recvs = [], []
    for k, (src, buf) in enumerate(zip(srcs, bufs)):
        rh = src.shape[0] // 2
        s0 = base + 4 * k
        sends.append(_remote(src, buf.at[me], ssem.at[s0 + 3], rsem.at[s0 + 3], sib))
        recvs.append(_remote(buf.at[me], buf.at[me], ssem.at[s0 + 3], rsem.at[s0 + 3], sib))
        for j, (px, py) in enumerate(chips):
            sends.append(_remote(src.at[pl.ds(c * rh, rh)], buf.at[me, pl.ds(c * rh, rh)], ssem.at[s0 + j], rsem.at[s0 + j], (px, py, c)))
            blk = buf.at[2 * px + py, pl.ds(c * rh, rh)]
            recvs.append(_remote(blk, blk, ssem.at[s0 + j], rsem.at[s0 + j], (px, py, c)))
    return sends, recvs


def _gather_p2_copies(bufs, ssem, rsem, base):
    x, y, c, chips = _place()
    sib = (x, y, 1 - c)
    sends, recvs = [], []
    for k, buf in enumerate(bufs):
        rh = buf.shape[1] // 2
        for j, (px, py) in enumerate(chips):
            s0 = base + 3 * k + j
            blk = buf.at[2 * px + py, pl.ds(c * rh, rh)]
            sends.append(_remote(blk, blk, ssem.at[s0], rsem.at[s0], sib))
            got = buf.at[2 * px + py, pl.ds((1 - c) * rh, rh)]
            recvs.append(_remote(got, got, ssem.at[s0], rsem.at[s0], sib))
    return sends, recvs


def _gathered_shape(s):
    return jax.ShapeDtypeStruct((N_CHIPS,) + s.shape, s.dtype)


def gather_p1(shards):
    return _Exchange(shards, [_gathered_shape(s) for s in shards], {}, 4 * len(shards),
                     lambda xi, xo, ss, rs: _gather_p1_copies(xi, xo, ss, rs, 0))


def gather_p2(bufs):
    return _Exchange(bufs, [jax.ShapeDtypeStruct(b.shape, b.dtype) for b in bufs], {k: k for k in range(len(bufs))},
                     3 * len(bufs), lambda xi, xo, ss, rs: _gather_p2_copies(xo, ss, rs, 0))


def gather_whole(whole, begun):
    nw, n = len(whole), len(whole) + len(begun)
    shards = list(whole) + list(begun)
    return _Exchange(shards, [_gathered_shape(s) for s in shards], {}, 4 * n + 3 * nw,
                     lambda xi, xo, ss, rs: _gather_p1_copies(xi, xo, ss, rs, 0),
                     then=lambda xi, xo, ss, rs: _gather_p2_copies(xo[:nw], ss, rs, 4 * n))


def gather_small(v):
    def copies(xi, xo, ssem, rsem):
        x, y, c, chips = _place()
        me, sib = 2 * x + y, (x, y, 1 - c)
        sends = [_remote(xi[0], xo[0].at[me], ssem.at[3], rsem.at[3], sib)]
        recvs = [_remote(xo[0].at[me], xo[0].at[me], ssem.at[3], rsem.at[3], sib)]
        for j, (px, py) in enumerate(chips):
            sends.append(_remote(xi[0], xo[0].at[me], ssem.at[j], rsem.at[j], (px, py, c)))
            blk = xo[0].at[2 * px + py]
            recvs.append(_remote(blk, blk, ssem.at[j], rsem.at[j], (px, py, c)))
        return sends, recvs

    return _Exchange([v], [_gathered_shape(v)], {}, 4, copies)


def run_exchanges(exchanges, name):
    return _call(lambda: None, name, (), [], [], [], [], (), hosted=exchanges)[1]


def sibling_halves(grads):
    def copies(xi, xo, ssem, rsem):
        x, y, c, _ = _place()
        sends = [_remote(xi[k].at[1 - c], xo[k], ssem.at[k], rsem.at[k], (x, y, 1 - c)) for k in range(len(grads))]
        return sends, sends

    return _Exchange(grads, [jax.ShapeDtypeStruct(g.shape[1:], g.dtype) for g in grads], {}, len(grads), copies)


def pair_sum(gh, recv, cidx, name):
    _, S, Rh, C = gh.shape

    def body(c_ref, a_ref, b_ref, o_ref):
        o_ref[...] = (a_ref[...].astype(F32) + b_ref[...].astype(F32)).astype(o_ref.dtype)

    return pl.pallas_call(
        body, name=name, out_shape=jax.ShapeDtypeStruct((S, Rh, C), BF16),
        grid_spec=pltpu.PrefetchScalarGridSpec(
            num_scalar_prefetch=1, grid=(S,),
            in_specs=[pl.BlockSpec((None, None, Rh, C), lambda s, c_ref: (c_ref[0], s, 0, 0)),
                      pl.BlockSpec((None, Rh, C), lambda s, c_ref: (s, 0, 0))],
            out_specs=pl.BlockSpec((None, Rh, C), lambda s, c_ref: (s, 0, 0))),
        compiler_params=_params(("parallel",)),
    )(cidx, gh, recv)


def scatter_p1(parts):
    def copies(xi, xo, ssem, rsem):
        x, y, c, chips = _place()
        me, sib = 2 * x + y, (x, y, 1 - c)
        sends, recvs = [], []
        for k in range(len(parts)):
            s0 = 4 * k
            sends.append(_remote(xi[k].at[me], xo[k].at[me, c], ssem.at[s0 + 3], rsem.at[s0 + 3], sib))
            own = xo[k].at[me, 1 - c]
            recvs.append(_remote(own, own, ssem.at[s0 + 3], rsem.at[s0 + 3], sib))
            for j, (px, py) in enumerate(chips):
                sends.append(_remote(xi[k].at[2 * px + py], xo[k].at[me, c], ssem.at[s0 + j], rsem.at[s0 + j], (px, py, c)))
                blk = xo[k].at[2 * px + py, c]
                recvs.append(_remote(blk, blk, ssem.at[s0 + j], rsem.at[s0 + j], (px, py, c)))
        return sends, recvs

    return _Exchange(parts, [jax.ShapeDtypeStruct((p.shape[0], 2) + p.shape[1:], p.dtype) for p in parts], {},
                     4 * len(parts), copies)


def scatter_p2(bufs):
    def copies(xi, xo, ssem, rsem):
        x, y, c, chips = _place()
        sib = (x, y, 1 - c)
        sends, recvs = [], []
        for k in range(len(bufs)):
            for j, (px, py) in enumerate(chips):
                s0 = 3 * k + j
                blk = xo[k].at[2 * px + py, c]
                sends.append(_remote(blk, blk, ssem.at[s0], rsem.at[s0], sib))
                got = xo[k].at[2 * px + py, 1 - c]
                recvs.append(_remote(got, got, ssem.at[s0], rsem.at[s0], sib))
        return sends, recvs

    return _Exchange(bufs, [jax.ShapeDtypeStruct(b.shape, b.dtype) for b in bufs], {k: k for k in range(len(bufs))},
                     3 * len(bufs), copies)


def _adamw_math(w, g, m, v):
    m = ADAM_B1 * m + (1.0 - ADAM_B1) * g
    v = ADAM_B2 * v + (1.0 - ADAM_B2) * (g * g)
    m_hat = m / (1.0 - ADAM_B1 ** ADAM_STEP)
    v_hat = v / (1.0 - ADAM_B2 ** ADAM_STEP)
    delta = -ADAM_LR * (m_hat / (jnp.sqrt(v_hat) + ADAM_EPS) + ADAM_WD * w)
    return delta, m, v


def adamw_reduce(w, m, v, buf, part, place, lyr, bases, name, hosted=()):
    L, R, C = w.shape
    Rh = R // 2
    rb = _tile(Rh, ROW_TILE, 2 * SUBLANES)
    nb = Rh // rb

    def body(place_ref, p_ref, b0, b1, b2, b3, w_ref, m_ref, v_ref, *rest):
        go_ref, d_ref, mo_ref, vo_ref = rest[-4:]
        mine = (place_ref[1] == pl.program_id(0))
        g = None
        for p, b in enumerate((b0, b1, b2, b3)):
            val = jnp.where(mine & (place_ref[0] == p), p_ref[...], b[...]).astype(F32)
            g = val if g is None else g + val
        d, mn, vn = _adamw_math(w_ref[...], g, m_ref[...], v_ref[...])
        go_ref[...] = g
        d_ref[...] = d
        mo_ref[...] = mn
        vo_ref[...] = vn

    def buf_spec(p):
        def idx(h, i, pr):
            own = (pr[0] == p) & (pr[1] == h)
            return (p, jnp.where(own, 1 - h, h), i, 0)
        return pl.BlockSpec((None, None, rb, C), idx)

    blk = pl.BlockSpec((None, rb, C), lambda h, i, pr: (lyr, h * nb + i, 0))
    in_specs = [pl.BlockSpec((None, rb, C), lambda h, i, pr: (pr[0], i, 0))] + [buf_spec(p) for p in range(N_CHIPS)] + [blk] * 3
    args = [part, buf, buf, buf, buf, w, m, v]
    aliases = {}
    if bases is not None:
        in_specs += [pl.BlockSpec(memory_space=pl.ANY)] * 4
        aliases = {len(args) + k: k for k in range(4)}
        args += list(bases)
    shp = jax.ShapeDtypeStruct((L, R, C), F32)
    return _call(body, name, (2, nb), in_specs, [blk] * 4, [shp] * 4, args, ("parallel", "parallel"),
                 hosted=hosted, prefetch=[place], own_aliases=aliases)


def small_update(gall, chip, entries, name):
    ne = len(entries)
    D = gall.shape[1]

    def body(chip_ref, gall_ref, *refs):
        ins, outs = refs[:3 * ne], refs[3 * ne:]
        ch = chip_ref[0]
        for e, (row0, kind, w, _, _) in enumerate(entries):
            r, width = w.shape

            def gsum(rs, cs):
                return gall_ref[rs, cs]

            if kind == "full":
                g = gsum(slice(row0, row0 + r), slice(0, D))
            elif kind == "cols":
                g = gsum(slice(row0, row0 + r), slice(0, width))
                for q in range(1, N_CHIPS):
                    g = jnp.where(ch == q, gsum(slice(row0, row0 + r), slice(q * width, (q + 1) * width)), g)
            else:
                per_row = D // width
                g = gsum(slice(row0, row0 + 1), slice(0, width))
                for q in range(1, N_CHIPS):
                    rr = row0 + q // per_row
                    cc = (q % per_row) * width
                    g = jnp.where(ch == q, gsum(slice(rr, rr + 1), slice(cc, cc + width)), g)
            d, mn, vn = _adamw_math(ins[3 * e][...], g, ins[3 * e + 1][...], ins[3 * e + 2][...])
            outs[4 * e][...] = g
            outs[4 * e + 1][...] = d
            outs[4 * e + 2][...] = mn
            outs[4 * e + 3][...] = vn

    vm = pl.BlockSpec(memory_space=pltpu.VMEM)
    args, out_shape = [], []
    for _, _, w, m, v in entries:
        args += [w, m, v]
        out_shape += [jax.ShapeDtypeStruct(w.shape, F32)] * 4
    return pl.pallas_call(
        body, name=name,
        in_specs=[pl.BlockSpec(memory_space=pltpu.SMEM), vm] + [vm] * (3 * ne),
        out_specs=[vm] * (4 * ne), out_shape=out_shape,
        compiler_params=pltpu.CompilerParams(vmem_limit_bytes=VMEM_LIMIT),
    )(chip, gall, *args)


def _pack_rows(items, width):
    rows, starts, at = [], [], 0
    for it in items:
        r = it.shape[0]
        pad = (-r) % SUBLANES
        starts.append(at)
        rows.append(it)
        if pad:
            rows.append(jnp.zeros((pad, width), F32))
        at += r + pad
    return jnp.concatenate(rows, axis=0), starts


def kernel(x, a_norm, a_w_in, a_conv, a_w_out, b_norm, b_w_pw1, b_b_pw1, b_conv, b_b_conv, b_ln_g, b_ln_b, b_w_pw2, b_b_pw2, ffn_norm, ffn_w_gate, ffn_w_up, ffn_w_down, final_norm, loss_target, m_a_norm, m_a_w_in, m_a_conv, m_a_w_out, m_b_norm, m_b_w_pw1, m_b_b_pw1, m_b_conv, m_b_b_conv, m_b_ln_g, m_b_ln_b, m_b_w_pw2, m_b_b_pw2, m_ffn_norm, m_ffn_w_gate, m_ffn_w_up, m_ffn_w_down, m_final_norm, v_a_norm, v_a_w_in, v_a_conv, v_a_w_out, v_b_norm, v_b_w_pw1, v_b_b_pw1, v_b_conv, v_b_b_conv, v_b_ln_g, v_b_ln_b, v_b_w_pw2, v_b_b_pw2, v_ffn_norm, v_ffn_w_gate, v_ffn_w_up, v_ffn_w_down, v_final_norm):
    T, D = x.shape[1], x.shape[2]
    Dq = D // N_CHIPS
    cx, cy, cc = lax.axis_index("x"), lax.axis_index("y"), lax.axis_index("c")
    chip = (2 * cx + cy).astype(jnp.int32).reshape(1)
    cidx = cc.astype(jnp.int32).reshape(1)
    h0 = x.reshape(T, D)
    tgt = loss_target.reshape(T, D)

    small_shards = [a_conv[0], b_norm, b_b_pw1.reshape(2, Dq), b_conv[0], b_b_conv, b_ln_g, b_ln_b, b_b_pw2]
    packed, st = _pack_rows(small_shards, Dq)

    tr = lambda t: jnp.swapaxes(t, 1, 2)
    w_gate, m_gate, v_gate = tr(ffn_w_gate), tr(m_ffn_w_gate), tr(v_ffn_w_gate)
    w_up, m_up, v_up = tr(ffn_w_up), tr(m_ffn_w_up), tr(v_ffn_w_up)
    bf = lambda t: t.astype(BF16)
    s_in, s_out, s_pw1, s_pw2 = bf(a_w_in[0]), bf(a_w_out[0]), bf(b_w_pw1[0]), bf(b_w_pw2[0])
    s_gate, s_up, s_down = [bf(w_gate[l]) for l in (0, 1)], [bf(w_up[l]) for l in (0, 1)], [bf(ffn_w_down[l]) for l in (0, 1)]

    n0, (g_in,) = rms_fwd(h0, a_norm, "rms_a", hosted=[gather_whole([s_in], [])])
    bcv, (g_out, gate0, sw) = mm_cols(n0, g_in, None, "mm_w_in", hosted=[gather_p1([s_out, s_gate[0]]), gather_small(packed)])

    def whole(k, r):
        return jnp.transpose(sw[:, st[k]:st[k] + r, :], (1, 0, 2)).reshape(r, D)

    a_conv_f, b_norm_f = whole(0, 3), whole(1, 1)
    b_b_pw1_f = sw[:, st[2]:st[2] + 2, :].reshape(1, 2 * D)
    b_conv_f, b_b_conv_f, b_ln_g_f, b_ln_b_f, b_b_pw2_f = whole(3, b_conv.shape[1]), whole(4, 1), whole(5, 1), whole(6, 1), whole(7, 1)
    ya, (up0, g_out, gate0) = gateconv_fwd(bcv, a_conv_f, "gateconv_fwd",
                                           hosted=[gather_p1([s_up[0]]), gather_p2([g_out, gate0])])
    g_out = g_out.reshape(1, D, D)
    h1, (down0, up0) = mm_rows(ya[None], g_out, h0, None, "mm_w_out", hosted=[gather_p1([s_down[0]]), gather_p2([up0])])
    n1, fg0, fu0, gu0, h2, (down0, *later) = ffn_fwd(h1, ffn_norm[0:1], gate0, up0, gather_p2([down0]).awaited_first(), "ffn_fwd0",
                                                     hosted=[gather_p1([s_pw1, s_pw2, s_gate[1], s_up[1]])])
    n2, (g_pw1, g_pw2, gate1, up1) = rms_fwd(h2, b_norm_f, "rms_b", hosted=[gather_p2(later)])
    g_pw2 = g_pw2.reshape(1, D, D)
    ub, (down1,) = mm_cols(n2, g_pw1, b_b_pw1_f, "mm_pw1", hosted=[gather_p1([s_down[1]])])
    cu, sb, (down1,) = bconv_fwd(ub, b_conv_f, b_b_conv_f, b_ln_g_f, b_ln_b_f, "bconv_fwd", hosted=[gather_p2([down1])])
    h3, _ = mm_rows(sb[None], g_pw2, h2, b_b_pw2_f, "mm_pw2")
    n3, fg1, fu1, gu1, h4, _ = ffn_fwd(h3, ffn_norm[1:2], gate1, up1, down1, "ffn_fwd1")
    loss_part, dh4, d_final = loss_head(h4, final_norm.reshape(1, D), tgt, "loss_head")

    place = jnp.concatenate([chip, cidx])

    def pair_sums(ghs, from_sib, tags):
        return [pair_sum(g, r, cidx, "pair_sum_" + t) for g, r, t in zip(ghs, from_sib, tags)]

    def upd(w, m, v, bufs, parts, tag, hosted=()):
        res, xo = None, []
        for lyr, (b, p) in enumerate(zip(bufs, parts)):
            res, xo_l = adamw_reduce(w, m, v, b, p, place, lyr, res, "adamw_%s%d" % (tag, lyr), hosted=hosted if lyr == 0 else ())
            xo += xo_l
        return res, xo

    dg1, du1, dh3, d_fn1, _ = ffn_bwd(dh4, h3, ffn_norm[1:2], fg1, fu1, down1, gate1, up1, "ffn_bwd1")
    gh_down1, _ = tn_grad(gu1, dh4, N_CHIPS, True, "tn_down1")
    gh_gate1, _ = tn_grad(dg1, n3, N_CHIPS, True, "tn_gate1")
    gh_up1, _ = tn_grad(du1, n3, N_CHIPS, True, "tn_up1")
    f1 = [gh_gate1, gh_up1, gh_down1]

    dcu, d_ln_g, d_ln_b, d_b_conv, d_b_pw2, sib_f1 = pw2_ln_bwd(dh3, g_pw2, cu, b_ln_g_f, b_ln_b_f, "pw2_ln_bwd",
                                                                hosted=[sibling_halves(f1)])
    p_f1 = pair_sums(f1, sib_f1, ["gate1", "up1", "down1"])
    gh_pw2, _ = tn_grad_square(sb, dh3, N_CHIPS, "tn_pw2")
    dub, d_bconv_w, d_b_pw1, buf_f1 = bconv_bwd(dcu, ub, b_conv_f, "bconv_bwd", hosted=[scatter_p1(p_f1)])
    gh_pw1, buf_f1 = tn_grad(n2, dub, N_CHIPS, False, "tn_pw1", hosted=[scatter_p2(buf_f1)])
    b_grp = [gh_pw1, gh_pw2]
    dh2, d_b_norm, sib_b = nt_cols_rms(dub, g_pw1, h2, b_norm_f, dh3, "nt_pw1", hosted=[sibling_halves(b_grp)])
    p_b = pair_sums(b_grp, sib_b, ["pw1", "pw2"])

    dg0, du0, dh1, d_fn0, buf_b = ffn_bwd(dh2, h1, ffn_norm[0:1], fg0, fu0, down0, gate0, up0, "ffn_bwd0", hosted=[scatter_p1(p_b)])
    gh_down0, buf_b = tn_grad(gu0, dh2, N_CHIPS, True, "tn_down0", hosted=[scatter_p2(buf_b)])
    gh_gate0, sib_down0 = tn_grad(dg0, n1, N_CHIPS, True, "tn_gate0", hosted=[sibling_halves([gh_down0])])
    p_down0 = pair_sums([gh_down0], sib_down0, ["down0"])
    gh_up0, (buf_down0, sib_gate0) = tn_grad(du0, n1, N_CHIPS, True, "tn_up0",
                                             hosted=[scatter_p1(p_down0), sibling_halves([gh_gate0])])
    p_gate0 = pair_sums([gh_gate0], [sib_gate0], ["gate0"])
    dya, (buf_down0, sib_up0) = nt_rows(dh1, g_out, "nt_w_out",
                                        hosted=[scatter_p2([buf_down0]), sibling_halves([gh_up0])])
    p_up0 = pair_sums([gh_up0], [sib_up0], ["up0"])
    gh_out, _ = tn_grad_square(ya, dh1, N_CHIPS, "tn_w_out")
    dbcv, d_aconv_w, (buf_gate0, sib_out) = gateconv_bwd(dya[0], bcv, a_conv_f, "gateconv_bwd",
                                                         hosted=[scatter_p1(p_gate0), sibling_halves([gh_out])])
    p_out = pair_sums([gh_out], [sib_out], ["out"])
    gh_in, (buf_up0, buf_gate0) = tn_grad(n0, dbcv, N_CHIPS, False, "tn_w_in",
                                          hosted=[scatter_p1(p_up0), scatter_p2([buf_gate0])])
    sib_in = run_exchanges([sibling_halves([gh_in])], "reduce_in_siblings")
    p_in = pair_sums([gh_in], sib_in, ["in"])
    grad_x, d_a_norm, (buf_in, buf_out, buf_up0) = nt_cols_rms(
        dbcv, g_in, h0, a_norm, dh1, "nt_w_in", hosted=[scatter_p1(p_in + p_out), scatter_p2([buf_up0])])
    p_f0 = [p_gate0[0], p_up0[0], p_down0[0]]

    d_ffn_norm = jnp.concatenate([d_fn0, d_fn1], axis=0)
    small_grads = [d_a_norm, d_aconv_w, d_b_norm, d_b_pw1.reshape(2, D), d_bconv_w, d_b_conv, d_ln_g, d_ln_b, d_b_pw2,
                   d_ffn_norm, d_final, jnp.broadcast_to(loss_part, (1, D))]
    gpacked, gs = _pack_rows(small_grads, D)
    gall, (buf_in, buf_out) = small_allreduce(gpacked, "allreduce_small_grads", hosted=[scatter_p2([buf_in, buf_out])])
    buf_a, p_a = [buf_in, buf_out], [p_in[0], p_out[0]]

    r_gate, _ = upd(w_gate, m_gate, v_gate, [buf_gate0, buf_f1[0]], [p_f0[0], p_f1[0]], "gate")
    r_up, _ = upd(w_up, m_up, v_up, [buf_up0, buf_f1[1]], [p_f0[1], p_f1[1]], "up")
    r_down, _ = upd(ffn_w_down, m_ffn_w_down, v_ffn_w_down, [buf_down0, buf_f1[2]], [p_f0[2], p_f1[2]], "down")
    r_gate, r_up = [tr(t) for t in r_gate], [tr(t) for t in r_up]
    r_pw1, _ = upd(b_w_pw1, m_b_w_pw1, v_b_w_pw1, [buf_b[0]], [p_b[0]], "pw1")
    r_pw2, _ = upd(b_w_pw2, m_b_w_pw2, v_b_w_pw2, [buf_b[1]], [p_b[1]], "pw2")
    r_in, _ = upd(a_w_in, m_a_w_in, v_a_w_in, [buf_a[0]], [p_a[0]], "w_in")
    r_out, _ = upd(a_w_out, m_a_w_out, v_a_w_out, [buf_a[1]], [p_a[1]], "w_out")
    entries = [
        (gs[0], "full", a_norm, m_a_norm, v_a_norm),
        (gs[1], "cols", a_conv[0], m_a_conv[0], v_a_conv[0]),
        (gs[2], "cols", b_norm, m_b_norm, v_b_norm),
        (gs[3], "flat2", b_b_pw1, m_b_b_pw1, v_b_b_pw1),
        (gs[4], "cols", b_conv[0], m_b_conv[0], v_b_conv[0]),
        (gs[5], "cols", b_b_conv, m_b_b_conv, v_b_b_conv),
        (gs[6], "cols", b_ln_g, m_b_ln_g, v_b_ln_g),
        (gs[7], "cols", b_ln_b, m_b_ln_b, v_b_ln_b),
        (gs[8], "cols", b_b_pw2, m_b_b_pw2, v_b_b_pw2),
        (gs[9], "full", ffn_norm, m_ffn_norm, v_ffn_norm),
        (gs[10], "full", final_norm.reshape(1, D), m_final_norm.reshape(1, D), v_final_norm.reshape(1, D)),
    ]
    so = small_update(gall, chip, entries, "small_update")
    sm = [so[4 * e:4 * e + 4] for e in range(len(entries))]

    def shaped(e, like):
        return [t.reshape(like.shape) for t in sm[e]]

    r_a_norm, r_a_conv, r_b_norm, r_b_b_pw1 = shaped(0, a_norm), shaped(1, a_conv), shaped(2, b_norm), shaped(3, b_b_pw1)
    r_b_conv, r_b_b_conv, r_b_ln_g, r_b_ln_b = shaped(4, b_conv), shaped(5, b_b_conv), shaped(6, b_ln_g), shaped(7, b_ln_b)
    r_b_b_pw2, r_ffn_norm, r_final = shaped(8, b_b_pw2), shaped(9, ffn_norm), shaped(10, final_norm)

    loss = gall[gs[11], 0]
    order =[r_a_norm, r_in, r_a_conv, r_out, r_b_norm, r_pw1, r_b_b_pw1, r_b_conv, r_b_b_conv, r_b_ln_g, r_b_ln_b,
             r_pw2, r_b_b_pw2, r_ffn_norm, r_gate, r_up, r_down, r_final]
    outs = [loss, grad_x.reshape(x.shape)]
    for field in range(4):
        outs += [r[field] for r in order]
    return tuple(outs)
```

```python
import functools

import jax
import jax.numpy as jnp
from jax import lax
from jax.experimental import pallas as pl
from jax.experimental.pallas import tpu as pltpu

RMS_EPS = 1e-6
LN_EPS = 1e-5
ADAM_LR = 0.001
ADAM_B1 = 0.9
ADAM_B2 = 0.999
ADAM_EPS = 1e-08
ADAM_WD = 0.01
ADAM_STEP = 10

N_CHIPS = 4
N_DEV = 8
LANES = 128
SUBLANES = 8
HALO = 32
CONV_ROWS = 64
TOKEN_TILE = 512
WIDE_TOKEN_TILE = 2048
GRAD_TOKEN_TILE = 2048
FFN_ROW_CHUNKS = 2
FFN_SEGS_PER_STEP = 2
ROW_TILE = 256
VMEM_LIMIT = 56 * 1024 * 1024
MESH = pl.DeviceIdType.MESH
BF16 = jnp.bfloat16
F32 = jnp.float32


def _tile(n, pref, mult=SUBLANES):
    t = min(n, pref) // mult * mult
    while n % t:
        t -= mult
    return t


def _params(sem):
    return pltpu.CompilerParams(dimension_semantics=sem, vmem_limit_bytes=VMEM_LIMIT)


def _sigmoid(x):
    return 0.5 * jnp.tanh(0.5 * x) + 0.5


class _Exchange:
    def __init__(self, ins, outs, aliases, n_sems, copies, then=None):
        self.ins, self.outs, self.aliases, self.n_sems, self.copies = list(ins), list(outs), dict(aliases), n_sems, copies
        self.then = then
        self.early = False

    def awaited_first(self):
        self.early = True
        return self

    def start(self, xi, xo, ssem, rsem):
        for cp in self.copies(xi, xo, ssem, rsem)[0]:
            cp.start()

    def finish(self, xi, xo, ssem, rsem):
        sends, recvs = self.copies(xi, xo, ssem, rsem)
        for cp in recvs:
            cp.wait_recv()
        if self.then is not None:
            sends2, recvs2 = self.then(xi, xo, ssem, rsem)
            for cp in sends2:
                cp.start()
            for cp in recvs2:
                cp.wait_recv()
            sends = sends + sends2
        for cp in sends:
            cp.wait_send()


def _call(body, name, grid, in_specs, out_specs, out_shape, args, sem, scratch_shapes=(), hosted=(), prefetch=(),
          own_aliases=None):
    in_specs, out_specs, out_shape = list(in_specs), list(out_specs), list(out_shape)
    scratch_shapes, hosted, prefetch = list(scratch_shapes), list(hosted), list(prefetch)
    n_pre, n_in, n_out, n_scr = len(prefetch), len(args), len(out_shape), len(scratch_shapes)
    x_in = [a for ex in hosted for a in ex.ins]
    x_out = [o for ex in hosted for o in ex.outs]
    aliases = {n_pre + i: o for i, o in (own_aliases or {}).items()}
    at_in, at_out = n_pre + n_in, n_out
    for ex in hosted:
        for i, o in ex.aliases.items():
            aliases[at_in + i] = at_out + o
        at_in += len(ex.ins)
        at_out += len(ex.outs)
    sems = [pltpu.SemaphoreType.DMA((ex.n_sems,)) for ex in hosted for _ in range(2)]

    def wrapped(*refs):
        pre, refs = refs[:n_pre], refs[n_pre:]
        ins, xi = refs[:n_in], refs[n_in:n_in + len(x_in)]
        refs = refs[n_in + len(x_in):]
        outs, xo = refs[:n_out], refs[n_out:n_out + len(x_out)]
        refs = refs[n_out + len(x_out):]
        scr, sm = refs[:n_scr], refs[n_scr:]
        views, a, b = [], 0, 0
        for e, ex in enumerate(hosted):
            views.append((xi[a:a + len(ex.ins)], xo[b:b + len(ex.outs)], sm[2 * e], sm[2 * e + 1]))
            a += len(ex.ins)
            b += len(ex.outs)
        first = last = None
        for ax, g in enumerate(grid):
            f, l = pl.program_id(ax) == 0, pl.program_id(ax) == g - 1
            first, last = (f, l) if first is None else (first & f, last & l)

        def begin():
            for ex, v in zip(hosted, views):
                ex.start(*v)
            for ex, v in zip(hosted, views):
                if ex.early:
                    ex.finish(*v)

        def end():
            for ex, v in zip(hosted, views):
                if not ex.early:
                    ex.finish(*v)

        if hosted and grid:
            pl.when(first)(begin)
        elif hosted:
            begin()
        early_refs = [r for ex, v in zip(hosted, views) if ex.early for r in v[1]]
        body(*pre, *ins, *outs, *scr, *early_refs)
        if hosted and grid:
            pl.when(last)(end)
        elif hosted:
            end()

    hbm = pl.BlockSpec(memory_space=pl.ANY)
    all_in, all_out = in_specs + [hbm] * len(x_in), out_specs + [hbm] * len(x_out)
    kw = dict(name=name, out_shape=out_shape + x_out, input_output_aliases=aliases,
              compiler_params=_params(tuple("arbitrary" for _ in grid) if hosted else sem))
    if prefetch:
        kw["grid_spec"] = pltpu.PrefetchScalarGridSpec(num_scalar_prefetch=n_pre, grid=grid, in_specs=all_in,
                                                       out_specs=all_out, scratch_shapes=scratch_shapes + sems)
    else:
        kw.update(grid=grid, in_specs=all_in, out_specs=all_out, scratch_shapes=scratch_shapes + sems)
    res = pl.pallas_call(wrapped, **kw)(*prefetch, *args, *x_in)
    return list(res[:n_out]), list(res[n_out:])


def rms_fwd(h, gain, name, hosted=()):
    T, D = h.shape
    tm = _tile(T, TOKEN_TILE)

    def body(h_ref, g_ref, o_ref):
        x = h_ref[...]
        r = lax.rsqrt(jnp.mean(x * x, axis=-1, keepdims=True) + RMS_EPS)
        o_ref[...] = (x * r * g_ref[...]).astype(o_ref.dtype)

    (n,), xo = _call(
        body, name, (T // tm,),
        [pl.BlockSpec((tm, D), lambda i: (i, 0)), pl.BlockSpec((1, D), lambda i: (0, 0))],
        [pl.BlockSpec((tm, D), lambda i: (i, 0))], [jax.ShapeDtypeStruct((T, D), BF16)],
        [h, gain], ("parallel",), hosted=hosted)
    return n, xo


def loss_head(h, gain, tgt, name):
    T, D = h.shape
    tm = _tile(T, TOKEN_TILE)

    def body(h_ref, g_ref, t_ref, loss_ref, dh_ref, dg_ref):
        i = pl.program_id(0)
        x = h_ref[...]
        g = g_ref[...]
        r = lax.rsqrt(jnp.mean(x * x, axis=-1, keepdims=True) + RMS_EPS)
        xhat = x * r
        diff = xhat * g - t_ref[...]
        part_loss = 0.5 * jnp.sum(jnp.mean(diff * diff, axis=-1, keepdims=True), axis=0, keepdims=True)
        dy = diff * (1.0 / D)
        dxhat = dy * g
        dh_ref[...] = r * (dxhat - xhat * jnp.mean(dxhat * xhat, axis=-1, keepdims=True))
        part = jnp.sum(dy * xhat, axis=0, keepdims=True)

        @pl.when(i == 0)
        def _():
            dg_ref[...] = part
            loss_ref[...] = part_loss

        @pl.when(i > 0)
        def _():
            dg_ref[...] += part
            loss_ref[...] += part_loss

    row = pl.BlockSpec((tm, D), lambda i: (i, 0))
    vec = pl.BlockSpec((1, D), lambda i: (0, 0))
    return pl.pallas_call(
        body, name=name, grid=(T // tm,),
        in_specs=[row, vec, row],
        out_specs=[pl.BlockSpec((1, 1), lambda i: (0, 0)), row, vec],
        out_shape=[jax.ShapeDtypeStruct((1, 1), F32), jax.ShapeDtypeStruct((T, D), F32),
                   jax.ShapeDtypeStruct((1, D), F32)],
        compiler_params=_params(("arbitrary",)),
    )(h, gain, tgt)


def _prev_halo_spec(tm, width):
    return pl.BlockSpec((HALO, width), lambda i: (jnp.maximum(i * (tm // HALO) - 1, 0), 0))


def _next_halo_spec(tm, width, T):
    return pl.BlockSpec((HALO, width), lambda i: (jnp.minimum((i + 1) * (tm // HALO), T // HALO - 1), 0))


def _shifted(win, off, rows):
    if off % SUBLANES == 0:
        return win[off:off + rows]
    n = win.shape[0]
    return pltpu.roll(win, (n - off) % n, 0)[:rows]


def _rowsum8(x):
    acc = x[0:SUBLANES]
    for q in range(1, x.shape[0] // SUBLANES):
        acc = acc + x[q * SUBLANES:(q + 1) * SUBLANES]
    return acc


def _conv_loops(tm, D, per_block):
    def chunk(r, carry):
        t0 = pl.multiple_of(r * CONV_ROWS, CONV_ROWS)
        for lb in range(D // LANES):
            per_block(t0, slice(lb * LANES, (lb + 1) * LANES))
        return carry

    lax.fori_loop(0, tm // CONV_ROWS, chunk, 0)


def gateconv_fwd(bcv, w, name, hosted=()):
    T, D3 = bcv.shape
    D = D3 // 3
    K = w.shape[0]
    tm = _tile(T, TOKEN_TILE)

    def body(x_ref, halo_ref, w_ref, y_ref, pad_ref):
        i = pl.program_id(0)
        pad_ref[HALO:, :] = x_ref[:, D:2 * D] * x_ref[:, 2 * D:]
        pad_ref[:HALO, :] = jnp.where(i > 0, halo_ref[:, D:2 * D] * halo_ref[:, 2 * D:], 0.0)

        def block(t0, ls):
            win = pad_ref[pl.ds(t0, CONV_ROWS + HALO), ls]
            acc = jnp.zeros((CONV_ROWS, LANES), F32)
            for k in range(K):
                acc = acc + w_ref[k:k + 1, ls] * _shifted(win, HALO - (K - 1) + k, CONV_ROWS)
            y_ref[pl.ds(t0, CONV_ROWS), ls] = (x_ref[pl.ds(t0, CONV_ROWS), ls] * acc).astype(y_ref.dtype)

        _conv_loops(tm, D, block)

    (y,), xo = _call(
        body, name, (T // tm,),
        [pl.BlockSpec((tm, D3), lambda i: (i, 0)), _prev_halo_spec(tm, D3), pl.BlockSpec((K, D), lambda i: (0, 0))],
        [pl.BlockSpec((tm, D), lambda i: (i, 0))], [jax.ShapeDtypeStruct((T, D), BF16)],
        [bcv, bcv, w], ("parallel",), [pltpu.VMEM((tm + HALO, D), F32)], hosted=hosted)
    return y, xo


def gateconv_bwd(dy, bcv, w, name, hosted=()):
    T, D3 = bcv.shape
    D = D3 // 3
    K = w.shape[0]
    tm = _tile(T, TOKEN_TILE)
    nt = T // tm

    def body(dy_ref, dyn_ref, x_ref, xp_ref, xn_ref, w_ref, o_ref, dw_ref, cv_ref, dc_ref, wacc_ref):
        i = pl.program_id(0)
        cv_ref[HALO:, :] = x_ref[:, D:2 * D] * x_ref[:, 2 * D:]
        cv_ref[:HALO, :] = jnp.where(i > 0, xp_ref[:, D:2 * D] * xp_ref[:, 2 * D:], 0.0)
        dc_ref[:tm, :] = dy_ref[...] * x_ref[:, :D]
        dc_ref[tm:, :] = jnp.where(i < nt - 1, dyn_ref[...] * xn_ref[:, :D], 0.0)

        @pl.when(i == 0)
        def _():
            wacc_ref[...] = jnp.zeros_like(wacc_ref)

        def block(t0, ls):
            cwin = cv_ref[pl.ds(t0, CONV_ROWS + HALO), ls]
            dwin = dc_ref[pl.ds(t0, CONV_ROWS + HALO), ls]
            dcon = dwin[:CONV_ROWS]
            conv = jnp.zeros((CONV_ROWS, LANES), F32)
            dcv = jnp.zeros((CONV_ROWS, LANES), F32)
            for k in range(K):
                wk = w_ref[k:k + 1, ls]
                cs = _shifted(cwin, HALO - (K - 1) + k, CONV_ROWS)
                conv = conv + wk * cs
                dcv = dcv + wk * _shifted(dwin, (K - 1) - k, CONV_ROWS)
                wacc_ref[k * SUBLANES:(k + 1) * SUBLANES, ls] += _rowsum8(dcon * cs)
            rows = pl.ds(t0, CONV_ROWS)
            o_ref[rows, ls] = (dy_ref[rows, ls] * conv).astype(o_ref.dtype)
            o_ref[rows, pl.ds(D + ls.start, LANES)] = (dcv * x_ref[rows, pl.ds(2 * D + ls.start, LANES)]).astype(o_ref.dtype)
            o_ref[rows, pl.ds(2 * D + ls.start, LANES)] = (dcv * x_ref[rows, pl.ds(D + ls.start, LANES)]).astype(o_ref.dtype)

        _conv_loops(tm, D, block)

        @pl.when(i == nt - 1)
        def _():
            for k in range(K):
                dw_ref[k:k + 1, :] = jnp.sum(wacc_ref[k * SUBLANES:(k + 1) * SUBLANES, :], axis=0, keepdims=True)

    (dx, dw), xo = _call(
        body, name, (nt,),
        [pl.BlockSpec((tm, D), lambda i: (i, 0)), _next_halo_spec(tm, D, T),
         pl.BlockSpec((tm, D3), lambda i: (i, 0)), _prev_halo_spec(tm, D3), _next_halo_spec(tm, D3, T),
         pl.BlockSpec((K, D), lambda i: (0, 0))],
        [pl.BlockSpec((tm, D3), lambda i: (i, 0)), pl.BlockSpec((K, D), lambda i: (0, 0))],
        [jax.ShapeDtypeStruct((T, D3), BF16), jax.ShapeDtypeStruct((K, D), F32)],
        [dy, dy, bcv, bcv, bcv, w], ("arbitrary",),
        [pltpu.VMEM((tm + HALO, D), F32), pltpu.VMEM((tm + HALO, D), F32), pltpu.VMEM((K * SUBLANES, D), F32)],
        hosted=hosted)
    return dx, dw, xo


def bconv_fwd(u, w, b_conv, ln_g, ln_b, name, hosted=()):
    T, D2 = u.shape
    D = D2 // 2
    K = w.shape[0]
    tm = _tile(T, TOKEN_TILE)

    def body(u_ref, halo_ref, w_ref, bc_ref, g_ref, b_ref, cu_ref, s_ref, pad_ref):
        i = pl.program_id(0)
        pad_ref[HALO:, :] = u_ref[:, :D] * _sigmoid(u_ref[:, D:])
        pad_ref[:HALO, :] = jnp.where(i > 0, halo_ref[:, :D] * _sigmoid(halo_ref[:, D:]), 0.0)

        def block(t0, ls):
            win = pad_ref[pl.ds(t0, CONV_ROWS + HALO), ls]
            acc = jnp.zeros((CONV_ROWS, LANES), F32)
            for k in range(K):
                acc = acc + w_ref[k:k + 1, ls] * _shifted(win, HALO - (K - 1) + k, CONV_ROWS)
            cu_ref[pl.ds(t0, CONV_ROWS), ls] = acc + bc_ref[:, ls]

        _conv_loops(tm, D, block)
        cu = cu_ref[...]
        mu = jnp.mean(cu, axis=-1, keepdims=True)
        xc = cu - mu
        rstd = lax.rsqrt(jnp.mean(xc * xc, axis=-1, keepdims=True) + LN_EPS)
        ln = xc * rstd * g_ref[...] + b_ref[...]
        s_ref[...] = (ln * _sigmoid(ln)).astype(s_ref.dtype)

    vec = pl.BlockSpec((1, D), lambda i: (0, 0))
    row = pl.BlockSpec((tm, D), lambda i: (i, 0))
    (cu, s), xo = _call(
        body, name, (T // tm,),
        [pl.BlockSpec((tm, D2), lambda i: (i, 0)), _prev_halo_spec(tm, D2), pl.BlockSpec((K, D), lambda i: (0, 0)), vec, vec, vec],
        [row, row], [jax.ShapeDtypeStruct((T, D), F32), jax.ShapeDtypeStruct((T, D), BF16)],
        [u, u, w, b_conv, ln_g, ln_b], ("parallel",), [pltpu.VMEM((tm + HALO, D), F32)], hosted=hosted)
    return cu, s, xo


def pw2_ln_bwd(dy, w, cu, ln_g, ln_b, name, hosted=()):
    T, D = cu.shape
    tm = _tile(T, TOKEN_TILE)

    def body(dy_ref, w_ref, cu_ref, g_ref, b_ref, dcu_ref, dg_ref, db_ref, dbc_ref, dbo_ref):
        i = pl.program_id(0)
        dy_ = dy_ref[...]
        ds = lax.dot_general(dy_.astype(BF16), w_ref[0], _NT, preferred_element_type=F32)
        cu_ = cu_ref[...]
        mu = jnp.mean(cu_, axis=-1, keepdims=True)
        xc = cu_ - mu
        rstd = lax.rsqrt(jnp.mean(xc * xc, axis=-1, keepdims=True) + LN_EPS)
        xh = xc * rstd
        ln = xh * g_ref[...] + b_ref[...]
        sg = _sigmoid(ln)
        dl = ds * (sg * (1.0 + ln * (1.0 - sg)))
        dxh = dl * g_ref[...]
        dcu = rstd * (dxh - jnp.mean(dxh, axis=-1, keepdims=True) - xh * jnp.mean(dxh * xh, axis=-1, keepdims=True))
        dcu_ref[...] = dcu
        pg = jnp.sum(dl * xh, axis=0, keepdims=True)
        pb = jnp.sum(dl, axis=0, keepdims=True)
        pc = jnp.sum(dcu, axis=0, keepdims=True)
        po = jnp.sum(dy_, axis=0, keepdims=True)

        @pl.when(i == 0)
        def _():
            dg_ref[...] = pg
            db_ref[...] = pb
            dbc_ref[...] = pc
            dbo_ref[...] = po

        @pl.when(i > 0)
        def _():
            dg_ref[...] += pg
            db_ref[...] += pb
            dbc_ref[...] += pc
            dbo_ref[...] += po

    vec = pl.BlockSpec((1, D), lambda i: (0, 0))
    row = pl.BlockSpec((tm, D), lambda i: (i, 0))
    vshape = jax.ShapeDtypeStruct((1, D), F32)
    outs, xo = _call(
        body, name, (T // tm,), [row, pl.BlockSpec((1, D, D), lambda i: (0, 0, 0)), row, vec, vec], [row, vec, vec, vec, vec],
        [jax.ShapeDtypeStruct((T, D), F32), vshape, vshape, vshape, vshape], [dy, w, cu, ln_g, ln_b], ("arbitrary",),
        hosted=hosted)
    return (*outs, xo)


def bconv_bwd(dcu, u, w, name, hosted=()):
    T, D2 = u.shape
    D = D2 // 2
    K = w.shape[0]
    tm = _tile(T, TOKEN_TILE)
    nt = T // tm

    def body(dc_ref, dcn_ref, u_ref, up_ref, w_ref, du_ref, dw_ref, db_ref, glu_ref, dpad_ref, dglu_ref, wacc_ref):
        i = pl.program_id(0)
        glu_ref[HALO:, :] = u_ref[:, :D] * _sigmoid(u_ref[:, D:])
        glu_ref[:HALO, :] = jnp.where(i > 0, up_ref[:, :D] * _sigmoid(up_ref[:, D:]), 0.0)
        dpad_ref[:tm, :] = dc_ref[...]
        dpad_ref[tm:, :] = jnp.where(i < nt - 1, dcn_ref[...], 0.0)

        @pl.when(i == 0)
        def _():
            wacc_ref[...] = jnp.zeros_like(wacc_ref)

        def block(t0, ls):
            gwin = glu_ref[pl.ds(t0, CONV_ROWS + HALO), ls]
            dwin = dpad_ref[pl.ds(t0, CONV_ROWS + HALO), ls]
            dcur = dwin[:CONV_ROWS]
            dglu = jnp.zeros((CONV_ROWS, LANES), F32)
            for k in range(K):
                dglu = dglu + w_ref[k:k + 1, ls] * _shifted(dwin, (K - 1) - k, CONV_ROWS)
                gs = _shifted(gwin, HALO - (K - 1) + k, CONV_ROWS)
                wacc_ref[k * SUBLANES:(k + 1) * SUBLANES, ls] += _rowsum8(dcur * gs)
            dglu_ref[pl.ds(t0, CONV_ROWS), ls] = dglu

        _conv_loops(tm, D, block)
        dglu = dglu_ref[...]
        a = u_ref[:, :D]
        sg = _sigmoid(u_ref[:, D:])
        da = dglu * sg
        dg = dglu * a * (sg * (1.0 - sg))
        du_ref[:, :D] = da.astype(du_ref.dtype)
        du_ref[:, D:] = dg.astype(du_ref.dtype)
        pa = jnp.sum(da, axis=0, keepdims=True)
        pg = jnp.sum(dg, axis=0, keepdims=True)

        @pl.when(i == 0)
        def _():
            db_ref[:, :D] = pa
            db_ref[:, D:] = pg

        @pl.when(i > 0)
        def _():
            db_ref[:, :D] += pa
            db_ref[:, D:] += pg

        @pl.when(i == nt - 1)
        def _():
            for k in range(K):
                dw_ref[k:k + 1, :] = jnp.sum(wacc_ref[k * SUBLANES:(k + 1) * SUBLANES, :], axis=0, keepdims=True)

    (du, dw, db), xo = _call(
        body, name, (nt,),
        [pl.BlockSpec((tm, D), lambda i: (i, 0)), _next_halo_spec(tm, D, T),
         pl.BlockSpec((tm, D2), lambda i: (i, 0)), _prev_halo_spec(tm, D2), pl.BlockSpec((K, D), lambda i: (0, 0))],
        [pl.BlockSpec((tm, D2), lambda i: (i, 0)), pl.BlockSpec((K, D), lambda i: (0, 0)), pl.BlockSpec((1, D2), lambda i: (0, 0))],
        [jax.ShapeDtypeStruct((T, D2), BF16), jax.ShapeDtypeStruct((K, D), F32), jax.ShapeDtypeStruct((1, D2), F32)],
        [dcu, dcu, u, u, w], ("arbitrary",),
        [pltpu.VMEM((tm + HALO, D), F32), pltpu.VMEM((tm + HALO, D), F32), pltpu.VMEM((tm, D), F32),
         pltpu.VMEM((K * SUBLANES, D), F32)], hosted=hosted)
    return du, dw, db, xo


def mm_cols(a, w, bias, name, hosted=()):
    T, K = a.shape
    S, _, n = w.shape
    tm = _tile(T, WIDE_TOKEN_TILE)

    def body(*refs):
        a_ref, w_ref = refs[:2]
        o_ref = refs[-1]
        acc = jnp.dot(a_ref[...], w_ref[...], preferred_element_type=F32)
        if bias is not None:
            acc = acc + refs[2][...]
        o_ref[...] = acc

    in_specs = [pl.BlockSpec((tm, K), lambda s, i: (i, 0)), pl.BlockSpec((None, K, n), lambda s, i: (s, 0, 0))]
    args = [a, w]
    if bias is not None:
        in_specs.append(pl.BlockSpec((1, n), lambda s, i: (0, s)))
        args.append(bias)
    (out,), xo = _call(body, name, (S, T // tm), in_specs, [pl.BlockSpec((tm, n), lambda s, i: (i, s))],
                       [jax.ShapeDtypeStruct((T, S * n), F32)], args, ("parallel", "parallel"), hosted=hosted)
    return out, xo


def _load_weights(pairs, sems, S, i, p):
    def copies(seg):
        return [pltpu.make_async_copy(src.at[seg], dst.at[seg], sems.at[k, seg]) for k, (src, dst) in enumerate(pairs)]

    @pl.when((i == 0) & (p == 0))
    def _():
        for seg in range(S):
            for cp in copies(seg):
                cp.start()

    @pl.when((i == 0) & (p < S // FFN_SEGS_PER_STEP))
    def _():
        for j in range(FFN_SEGS_PER_STEP):
            for cp in copies(FFN_SEGS_PER_STEP * p + j):
                cp.wait()


def ffn_fwd(h, gain, wg, wu, wd, name, hosted=()):
    T, D = h.shape
    S, f, _ = wg.shape
    tm = _tile(T, TOKEN_TILE)
    rc = tm // FFN_ROW_CHUNKS
    chunks = [slice(r * rc, (r + 1) * rc) for r in range(FFN_ROW_CHUNKS)]
    G = FFN_SEGS_PER_STEP
    wd_arrives = isinstance(wd, _Exchange)
    weights = [wg, wu] if wd_arrives else [wg, wu, wd]
    hosted = ([wd] if wd_arrives else []) + list(hosted)

    def body(h_ref, gain_ref, *refs):
        nw = len(weights)
        wg_hbm, wu_hbm = refs[:2]
        wd_hbm = refs[-1] if wd_arrives else refs[2]
        n_ref, g_ref, u_ref, gu_ref, o_ref, wg_v, wu_v, wd_v, sems = refs[nw:nw + 9]
        i, p = pl.program_id(0), pl.program_id(1)
        _load_weights([(wg_hbm, wg_v), (wu_hbm, wu_v), (wd_hbm, wd_v)], sems, S, i, p)

        @pl.when(p == 0)
        def _():
            x = h_ref[...]
            r = lax.rsqrt(jnp.mean(x * x, axis=-1, keepdims=True) + RMS_EPS)
            n_ref[...] = (x * r * gain_ref[...]).astype(n_ref.dtype)

        parts = []
        for rows in chunks:
            a = n_ref[rows, :]
            acc = None
            for j in range(G):
                seg = G * p + j
                g = lax.dot_general(a, wg_v[seg], _NT, preferred_element_type=F32)
                u = lax.dot_general(a, wu_v[seg], _NT, preferred_element_type=F32)
                gu = (g * _sigmoid(g) * u).astype(gu_ref.dtype)
                g_ref[j, rows, :] = g.astype(g_ref.dtype)
                u_ref[j, rows, :] = u.astype(u_ref.dtype)
                gu_ref[j, rows, :] = gu
                part = jnp.dot(gu, wd_v[seg], preferred_element_type=F32)
                acc = part if acc is None else acc + part
            parts.append(acc)

        @pl.when(p == 0)
        def _():
            for rows, part in zip(chunks, parts):
                o_ref[rows, :] = h_ref[rows, :] + part

        @pl.when(p > 0)
        def _():
            for rows, part in zip(chunks, parts):
                o_ref[rows, :] += part

    row = pl.BlockSpec((tm, D), lambda i, p: (i, 0))
    seg = pl.BlockSpec((G, tm, f), lambda i, p: (p, i, 0))
    hbm = pl.BlockSpec(memory_space=pl.ANY)
    segs = jax.ShapeDtypeStruct((S, T, f), BF16)
    outs, xo = _call(
        body, name, (T // tm, S // G),
        [row, pl.BlockSpec((1, D), lambda i, s: (0, 0))] + [hbm] * len(weights), [row, seg, seg, seg, row],
        [jax.ShapeDtypeStruct((T, D), BF16), segs, segs, segs, jax.ShapeDtypeStruct((T, D), F32)],
        [h, gain] + weights, ("arbitrary", "arbitrary"),
        [pltpu.VMEM((S, f, D), BF16), pltpu.VMEM((S, f, D), BF16), pltpu.VMEM((S, f, D), BF16), pltpu.SemaphoreType.DMA((3, S))],
        hosted=hosted)
    return (*outs, xo)


def ffn_bwd(dy, h, gain, g, u, wd, wg, wu, name, hosted=()):
    T, D = h.shape
    S, f, _ = wg.shape
    tm = _tile(T, TOKEN_TILE)
    nt = T // tm
    rc = tm // FFN_ROW_CHUNKS
    chunks = [slice(r * rc, (r + 1) * rc) for r in range(FFN_ROW_CHUNKS)]
    G = FFN_SEGS_PER_STEP
    P = S // G

    def body(dy_ref, h_ref, gain_ref, g_ref, u_ref, wd_hbm, wg_hbm, wu_hbm, dg_ref, du_ref, dh_ref, dgain_ref,
             wd_v, wg_v, wu_v, dyb_ref, dgs_ref, dus_ref, sems):
        i, p = pl.program_id(0), pl.program_id(1)
        _load_weights([(wd_hbm, wd_v), (wg_hbm, wg_v), (wu_hbm, wu_v)], sems, S, i, p)

        def first_stage(grp, slot):
            for rows in chunks:
                for j in range(G):
                    dgu = lax.dot_general(dyb_ref[rows, :], wd_v[G * grp + j], _NT, preferred_element_type=F32)
                    gv = g_ref[j, rows, :].astype(F32)
                    sg = _sigmoid(gv)
                    dg = (dgu * u_ref[j, rows, :].astype(F32) * (sg * (1.0 + gv * (1.0 - sg)))).astype(dg_ref.dtype)
                    du = (dgu * (gv * sg)).astype(du_ref.dtype)
                    dg_ref[j, rows, :] = dg
                    du_ref[j, rows, :] = du
                    dgs_ref[slot, j, rows, :] = dg
                    dus_ref[slot, j, rows, :] = du

        def second_stage(grp, slot):
            for rows in chunks:
                acc = None
                for j in range(G):
                    part = (jnp.dot(dgs_ref[slot, j, rows, :], wg_v[G * grp + j], preferred_element_type=F32)
                            + jnp.dot(dus_ref[slot, j, rows, :], wu_v[G * grp + j], preferred_element_type=F32))
                    acc = part if acc is None else acc + part
                dh_ref[rows, :] += acc

        @pl.when(p == 0)
        def _():
            dyb_ref[...] = dy_ref[...].astype(dyb_ref.dtype)
            dh_ref[...] = jnp.zeros_like(dh_ref)
            first_stage(0, 0)

        @pl.when((p > 0) & (p < P))
        def _():
            second_stage(p - 1, (p - 1) % 2)
            first_stage(p, p % 2)

        @pl.when(p == P)
        def _():
            second_stage(P - 1, (P - 1) % 2)
            dn = dh_ref[...]
            x = h_ref[...]
            r = lax.rsqrt(jnp.mean(x * x, axis=-1, keepdims=True) + RMS_EPS)
            xhat = x * r
            dxhat = dn * gain_ref[...]
            dh_ref[...] = dy_ref[...] + r * (dxhat - xhat * jnp.mean(dxhat * xhat, axis=-1, keepdims=True))
            pg = jnp.sum(dn * xhat, axis=0, keepdims=True)

            @pl.when(i == 0)
            def _():
                dgain_ref[...] = pg

            @pl.when(i > 0)
            def _():
                dgain_ref[...] += pg

    row = pl.BlockSpec((tm, D), lambda i, p: (i, 0))
    vec = pl.BlockSpec((1, D), lambda i, p: (0, 0))
    seg = pl.BlockSpec((G, tm, f), lambda i, p: (jnp.minimum(p, P - 1), i, 0))
    hbm = pl.BlockSpec(memory_space=pl.ANY)
    segs = jax.ShapeDtypeStruct((S, T, f), BF16)
    outs, xo = _call(
        body, name, (nt, P + 1),
        [row, row, vec, seg, seg, hbm, hbm, hbm], [seg, seg, row, vec],
        [segs, segs, jax.ShapeDtypeStruct((T, D), F32), jax.ShapeDtypeStruct((1, D), F32)],
        [dy, h, gain, g, u, wd, wg, wu], ("arbitrary", "arbitrary"),
        [pltpu.VMEM((S, f, D), BF16), pltpu.VMEM((S, f, D), BF16), pltpu.VMEM((S, f, D), BF16),
         pltpu.VMEM((tm, D), BF16), pltpu.VMEM((2, G, tm, f), BF16), pltpu.VMEM((2, G, tm, f), BF16),
         pltpu.SemaphoreType.DMA((3, S))], hosted=hosted)
    return (*outs, xo)


def mm_rows(a, w, res, bias, name, hosted=()):
    S, T, k = a.shape
    N = w.shape[-1]
    tm = _tile(T, TOKEN_TILE)

    def body(*refs):
        a_ref, w_ref, r_ref = refs[:3]
        o_ref = refs[-1]
        s = pl.program_id(1)
        acc = jnp.dot(a_ref[...], w_ref[...], preferred_element_type=F32)

        @pl.when(s == 0)
        def _():
            base = r_ref[...]
            if bias is not None:
                base = base + refs[3][...]
            o_ref[...] = base + acc

        @pl.when(s > 0)
        def _():
            o_ref[...] += acc

    in_specs = [pl.BlockSpec((None, tm, k), lambda i, s: (s, i, 0)),
                pl.BlockSpec((None, k, N), lambda i, s: (s, 0, 0)),
                pl.BlockSpec((tm, N), lambda i, s: (i, 0))]
    args = [a, w, res]
    if bias is not None:
        in_specs.append(pl.BlockSpec((1, N), lambda i, s: (0, 0)))
        args.append(bias)
    (out,), xo = _call(body, name, (T // tm, S), in_specs, [pl.BlockSpec((tm, N), lambda i, s: (i, 0))],
                       [jax.ShapeDtypeStruct((T, N), F32)], args, ("parallel", "arbitrary"), hosted=hosted)
    return out, xo


_NT = (((1,), (1,)), ((), ()))
_TN = (((0,), (0,)), ((), ()))


def nt_rows(dy, w, name, hosted=()):
    T, N = dy.shape
    S, k, _ = w.shape
    tm = _tile(T, TOKEN_TILE)

    def body(dy_ref, w_ref, o_ref):
        o_ref[...] = lax.dot_general(dy_ref[...].astype(BF16), w_ref[...], _NT, preferred_element_type=F32)

    (out,), xo = _call(
        body, name, (T // tm, S),
        [pl.BlockSpec((tm, N), lambda i, s: (i, 0)), pl.BlockSpec((None, k, N), lambda i, s: (s, 0, 0))],
        [pl.BlockSpec((None, tm, k), lambda i, s: (s, i, 0))], [jax.ShapeDtypeStruct((S, T, k), F32)],
        [dy, w], ("parallel", "parallel"), hosted=hosted)
    return out, xo


def nt_cols_rms(dy, w, h, gain, dres, name, hosted=()):
    T, K = h.shape
    S, _, n = w.shape
    tm = _tile(T, TOKEN_TILE)

    def body(dy_ref, w_ref, h_ref, gain_ref, dres_ref, dh_ref, dgain_ref):
        i = pl.program_id(0)
        dn = None
        for s in range(S):
            part = lax.dot_general(dy_ref[:, s * n:(s + 1) * n], w_ref[s], _NT, preferred_element_type=F32)
            dn = part if dn is None else dn + part
        x = h_ref[...]
        r = lax.rsqrt(jnp.mean(x * x, axis=-1, keepdims=True) + RMS_EPS)
        xhat = x * r
        dxhat = dn * gain_ref[...]
        dh_ref[...] = dres_ref[...] + r * (dxhat - xhat * jnp.mean(dxhat * xhat, axis=-1, keepdims=True))
        pg = jnp.sum(dn * xhat, axis=0, keepdims=True)

        @pl.when(i == 0)
        def _():
            dgain_ref[...] = pg

        @pl.when(i > 0)
        def _():
            dgain_ref[...] += pg

    row = pl.BlockSpec((tm, K), lambda i: (i, 0))
    vec = pl.BlockSpec((1, K), lambda i: (0, 0))
    (dh, dgain), xo = _call(
        body, name, (T // tm,),
        [pl.BlockSpec((tm, S * n), lambda i: (i, 0)), pl.BlockSpec((S, K, n), lambda i: (0, 0, 0)), row, vec, row],
        [row, vec], [jax.ShapeDtypeStruct((T, K), F32), jax.ShapeDtypeStruct((1, K), F32)],
        [dy, w, h, gain, dres], ("arbitrary",), hosted=hosted)
    return dh, dgain, xo


def tn_grad(a, dy, S, a_by_seg, name, hosted=()):
    T = dy.shape[0] if dy.ndim == 2 else dy.shape[1]
    tt = _tile(T, GRAD_TOKEN_TILE)
    if a_by_seg:
        R = a.shape[1] // S if a.ndim == 2 else a.shape[2]
        C = dy.shape[1]
        a_spec = pl.BlockSpec((tt, R), lambda s, t: (t, s)) if a.ndim == 2 else pl.BlockSpec((None, tt, R), lambda s, t: (s, t, 0))
        b_spec = pl.BlockSpec((tt, C), lambda s, t: (t, 0))
    else:
        R = a.shape[1]
        C = dy.shape[1] // S if dy.ndim == 2 else dy.shape[2]
        a_spec = pl.BlockSpec((tt, R), lambda s, t: (t, 0))
        b_spec = pl.BlockSpec((tt, C), lambda s, t: (t, s)) if dy.ndim == 2 else pl.BlockSpec((None, tt, C), lambda s, t: (s, t, 0))
    Rh = R // 2
    nt = T // tt

    def body(a_ref, b_ref, o_ref, acc_ref):
        t = pl.program_id(1)
        part = lax.dot_general(a_ref[...], b_ref[...].astype(BF16), _TN, preferred_element_type=F32)

        @pl.when(t == 0)
        def _():
            acc_ref[...] = part

        @pl.when(t > 0)
        def _():
            acc_ref[...] += part

        @pl.when(t == nt - 1)
        def _():
            o_ref[0] = acc_ref[:Rh, :].astype(o_ref.dtype)
            o_ref[1] = acc_ref[Rh:, :].astype(o_ref.dtype)

    (gh,), xo = _call(
        body, name, (S, nt), [a_spec, b_spec], [pl.BlockSpec((2, None, Rh, C), lambda s, t: (0, s, 0, 0))],
        [jax.ShapeDtypeStruct((2, S, Rh, C), BF16)], [a, dy], ("parallel", "arbitrary"), [pltpu.VMEM((R, C), F32)],
        hosted=hosted)
    return gh, xo


def tn_grad_square(a, dy, S, name, hosted=()):
    T, K = a.shape
    N = dy.shape[1]
    tt = _tile(T, GRAD_TOKEN_TILE)
    nt = T // tt
    Rh = K // S // 2

    def body(a_ref, b_ref, o_ref, acc_ref):
        t = pl.program_id(0)
        part = lax.dot_general(a_ref[...], b_ref[...].astype(BF16), _TN, preferred_element_type=F32)

        @pl.when(t == 0)
        def _():
            acc_ref[...] = part

        @pl.when(t > 0)
        def _():
            acc_ref[...] += part

        @pl.when(t == nt - 1)
        def _():
            for s in range(S):
                for hf in range(2):
                    r0 = (2 * s + hf) * Rh
                    o_ref[hf, s] = acc_ref[r0:r0 + Rh, :].astype(o_ref.dtype)

    (gh,), xo = _call(
        body, name, (nt,), [pl.BlockSpec((tt, K), lambda t: (t, 0)), pl.BlockSpec((tt, N), lambda t: (t, 0))],
        [pl.BlockSpec((2, S, Rh, N), lambda t: (0, 0, 0, 0))], [jax.ShapeDtypeStruct((2, S, Rh, N), BF16)],
        [a, dy], ("arbitrary",), [pltpu.VMEM((K, N), F32)], hosted=hosted)
    return gh, xo


def _place():
    x, y, c = lax.axis_index("x"), lax.axis_index("y"), lax.axis_index("c")
    chips = [(1 - x, y), (x, 1 - y), (1 - x, 1 - y)]
    return x, y, c, chips


def _remote(src, dst, send_sem, recv_sem, dev):
    return pltpu.make_async_remote_copy(src_ref=src, dst_ref=dst, send_sem=send_sem, recv_sem=recv_sem,
                                        device_id=dev, device_id_type=MESH)


def small_allreduce(v, name, hosted=()):
    rows, W = v.shape

    def body(v_ref, o_ref, sib_ref, pair_ref, chips_ref, send_sems, recv_sems):
        x, y, c, chips = _place()
        me = 2 * x + y
        swap = _remote(v_ref, sib_ref, send_sems.at[3], recv_sems.at[3], (x, y, 1 - c))
        swap.start()
        swap.wait()
        mine, other = v_ref[...], sib_ref[...]
        pair_ref[...] = jnp.where(c == 0, mine, other) + jnp.where(c == 0, other, mine)
        sends = []
        for j, (px, py) in enumerate(chips):
            cp = _remote(pair_ref, chips_ref.at[me], send_sems.at[j], recv_sems.at[j], (px, py, c))
            cp.start()
            sends.append(cp)
        chips_ref[me] = pair_ref[...]
        for j, (px, py) in enumerate(chips):
            blk = chips_ref.at[2 * px + py]
            _remote(blk, blk, send_sems.at[j], recv_sems.at[j], (px, py, c)).wait_recv()
        for cp in sends:
            cp.wait_send()
        o_ref[...] = (chips_ref[0] + chips_ref[1]) + (chips_ref[2] + chips_ref[3])

    vm = pl.BlockSpec(memory_space=pltpu.VMEM)
    (out,), xo = _call(
        body, name, (), [vm], [vm], [jax.ShapeDtypeStruct((rows, W), F32)], [v], (),
        [pltpu.VMEM((rows, W), F32), pltpu.VMEM((rows, W), F32), pltpu.VMEM((N_CHIPS, rows, W), F32),
         pltpu.SemaphoreType.DMA((4,)), pltpu.SemaphoreType.DMA((4,))], hosted=hosted)
    return out, xo


def _gather_p1_copies(srcs, bufs, ssem, rsem, base):
    x, y, c, chips = _place()
    me, sib = 2 * x + y, (x, y, 1 - c)
    sends, recvs = [], []
    for k, (src, buf) in enumerate(zip(srcs, bufs)):
        rh = src.shape[0] // 2
        s0 = base + 4 * k
        sends.append(_remote(src, buf.at[me], ssem.at[s0 + 3], rsem.at[s0 + 3], sib))
        recvs.append(_remote(buf.at[me], buf.at[me], ssem.at[s0 + 3], rsem.at[s0 + 3], sib))
        for j, (px, py) in enumerate(chips):
            sends.append(_remote(src.at[pl.ds(c * rh, rh)], buf.at[me, pl.ds(c * rh, rh)], ssem.at[s0 + j], rsem.at[s0 + j], (px, py, c)))
            blk = buf.at[2 * px + py, pl.ds(c * rh, rh)]
            recvs.append(_remote(blk, blk, ssem.at[s0 + j], rsem.at[s0 + j], (px, py, c)))
    return sends, recvs


def _gather_p2_copies(bufs, ssem, rsem, base):
    x, y, c, chips = _place()
    sib = (x, y, 1 - c)
    sends, recvs = [], []
    for k, buf in enumerate(bufs):
        rh = buf.shape[1] // 2
        for j, (px, py) in enumerate(chips):
            s0 = base + 3 * k + j
            blk = buf.at[2 * px + py, pl.ds(c * rh, rh)]
            sends.append(_remote(blk, blk, ssem.at[s0], rsem.at[s0], sib))
            got = buf.at[2 * px + py, pl.ds((1 - c) * rh, rh)]
            recvs.append(_remote(got, got, ssem.at[s0], rsem.at[s0], sib))
    return sends, recvs


def _gathered_shape(s):
    return jax.ShapeDtypeStruct((N_CHIPS,) + s.shape, s.dtype)


def gather_p1(shards):
    return _Exchange(shards, [_gathered_shape(s) for s in shards], {}, 4 * len(shards),
                     lambda xi, xo, ss, rs: _gather_p1_copies(xi, xo, ss, rs, 0))


def gather_p2(bufs):
    return _Exchange(bufs, [jax.ShapeDtypeStruct(b.shape, b.dtype) for b in bufs], {k: k for k in range(len(bufs))},
                     3 * len(bufs), lambda xi, xo, ss, rs: _gather_p2_copies(xo, ss, rs, 0))


def gather_whole(whole, begun):
    nw, n = len(whole), len(whole) + len(begun)
    shards = list(whole) + list(begun)
    return _Exchange(shards, [_gathered_shape(s) for s in shards], {}, 4 * n + 3 * nw,
                     lambda xi, xo, ss, rs: _gather_p1_copies(xi, xo, ss, rs, 0),
                     then=lambda xi, xo, ss, rs: _gather_p2_copies(xo[:nw], ss, rs, 4 * n))


def gather_small(v):
    def copies(xi, xo, ssem, rsem):
        x, y, c, chips = _place()
        me, sib = 2 * x + y, (x, y, 1 - c)
        sends = [_remote(xi[0], xo[0].at[me], ssem.at[3], rsem.at[3], sib)]
        recvs = [_remote(xo[0].at[me], xo[0].at[me], ssem.at[3], rsem.at[3], sib)]
        for j, (px, py) in enumerate(chips):
            sends.append(_remote(xi[0], xo[0].at[me], ssem.at[j], rsem.at[j], (px, py, c)))
            blk = xo[0].at[2 * px + py]
            recvs.append(_remote(blk, blk, ssem.at[j], rsem.at[j], (px, py, c)))
        return sends, recvs

    return _Exchange([v], [_gathered_shape(v)], {}, 4, copies)


def run_exchanges(exchanges, name):
    return _call(lambda: None, name, (), [], [], [], [], (), hosted=exchanges)[1]


def sibling_halves(grads):
    def copies(xi, xo, ssem, rsem):
        x, y, c, _ = _place()
        sends = [_remote(xi[k].at[1 - c], xo[k], ssem.at[k], rsem.at[k], (x, y, 1 - c)) for k in range(len(grads))]
        return sends, sends

    return _Exchange(grads, [jax.ShapeDtypeStruct(g.shape[1:], g.dtype) for g in grads], {}, len(grads), copies)


def pair_sum(gh, recv, cidx, name):
    _, S, Rh, C = gh.shape

    def body(c_ref, a_ref, b_ref, o_ref):
        o_ref[...] = (a_ref[...].astype(F32) + b_ref[...].astype(F32)).astype(o_ref.dtype)

    return pl.pallas_call(
        body, name=name, out_shape=jax.ShapeDtypeStruct((S, Rh, C), BF16),
        grid_spec=pltpu.PrefetchScalarGridSpec(
            num_scalar_prefetch=1, grid=(S,),
            in_specs=[pl.BlockSpec((None, None, Rh, C), lambda s, c_ref: (c_ref[0], s, 0, 0)),
                      pl.BlockSpec((None, Rh, C), lambda s, c_ref: (s, 0, 0))],
            out_specs=pl.BlockSpec((None, Rh, C), lambda s, c_ref: (s, 0, 0))),
        compiler_params=_params(("parallel",)),
    )(cidx, gh, recv)


def scatter_p1(parts):
    def copies(xi, xo, ssem, rsem):
        x, y, c, chips = _place()
        me, sib = 2 * x + y, (x, y, 1 - c)
        sends, recvs = [], []
        for k in range(len(parts)):
            s0 = 4 * k
            sends.append(_remote(xi[k].at[me], xo[k].at[me, c], ssem.at[s0 + 3], rsem.at[s0 + 3], sib))
            own = xo[k].at[me, 1 - c]
            recvs.append(_remote(own, own, ssem.at[s0 + 3], rsem.at[s0 + 3], sib))
            for j, (px, py) in enumerate(chips):
                sends.append(_remote(xi[k].at[2 * px + py], xo[k].at[me, c], ssem.at[s0 + j], rsem.at[s0 + j], (px, py, c)))
                blk = xo[k].at[2 * px + py, c]
                recvs.append(_remote(blk, blk, ssem.at[s0 + j], rsem.at[s0 + j], (px, py, c)))
        return sends, recvs

    return _Exchange(parts, [jax.ShapeDtypeStruct((p.shape[0], 2) + p.shape[1:], p.dtype) for p in parts], {},
                     4 * len(parts), copies)


def scatter_p2(bufs):
    def copies(xi, xo, ssem, rsem):
        x, y, c, chips = _place()
        sib = (x, y, 1 - c)
        sends, recvs = [], []
        for k in range(len(bufs)):
            for j, (px, py) in enumerate(chips):
                s0 = 3 * k + j
                blk = xo[k].at[2 * px + py, c]
                sends.append(_remote(blk, blk, ssem.at[s0], rsem.at[s0], sib))
                got = xo[k].at[2 * px + py, 1 - c]
                recvs.append(_remote(got, got, ssem.at[s0], rsem.at[s0], sib))
        return sends, recvs

    return _Exchange(bufs, [jax.ShapeDtypeStruct(b.shape, b.dtype) for b in bufs], {k: k for k in range(len(bufs))},
                     3 * len(bufs), copies)


def _adamw_math(w, g, m, v):
    m = ADAM_B1 * m + (1.0 - ADAM_B1) * g
    v = ADAM_B2 * v + (1.0 - ADAM_B2) * (g * g)
    m_hat = m / (1.0 - ADAM_B1 ** ADAM_STEP)
    v_hat = v / (1.0 - ADAM_B2 ** ADAM_STEP)
    delta = -ADAM_LR * (m_hat / (jnp.sqrt(v_hat) + ADAM_EPS) + ADAM_WD * w)
    return delta, m, v


def adamw_reduce(w, m, v, buf, part, place, lyr, bases, name, hosted=()):
    L, R, C = w.shape
    Rh = R // 2
    rb = _tile(Rh, ROW_TILE, 2 * SUBLANES)
    nb = Rh // rb

    def body(place_ref, p_ref, b0, b1, b2, b3, w_ref, m_ref, v_ref, *rest):
        go_ref, d_ref, mo_ref, vo_ref = rest[-4:]
        mine = (place_ref[1] == pl.program_id(0))
        g = None
        for p, b in enumerate((b0, b1, b2, b3)):
            val = jnp.where(mine & (place_ref[0] == p), p_ref[...], b[...]).astype(F32)
            g = val if g is None else g + val
        d, mn, vn = _adamw_math(w_ref[...], g, m_ref[...], v_ref[...])
        go_ref[...] = g
        d_ref[...] = d
        mo_ref[...] = mn
        vo_ref[...] = vn

    def buf_spec(p):
        def idx(h, i, pr):
            own = (pr[0] == p) & (pr[1] == h)
            return (p, jnp.where(own, 1 - h, h), i, 0)
        return pl.BlockSpec((None, None, rb, C), idx)

    blk = pl.BlockSpec((None, rb, C), lambda h, i, pr: (lyr, h * nb + i, 0))
    in_specs = [pl.BlockSpec((None, rb, C), lambda h, i, pr: (pr[0], i, 0))] + [buf_spec(p) for p in range(N_CHIPS)] + [blk] * 3
    args = [part, buf, buf, buf, buf, w, m, v]
    aliases = {}
    if bases is not None:
        in_specs += [pl.BlockSpec(memory_space=pl.ANY)] * 4
        aliases = {len(args) + k: k for k in range(4)}
        args += list(bases)
    shp = jax.ShapeDtypeStruct((L, R, C), F32)
    return _call(body, name, (2, nb), in_specs, [blk] * 4, [shp] * 4, args, ("parallel", "parallel"),
                 hosted=hosted, prefetch=[place], own_aliases=aliases)


def small_update(gall, chip, entries, name):
    ne = len(entries)
    D = gall.shape[1]

    def body(chip_ref, gall_ref, *refs):
        ins, outs = refs[:3 * ne], refs[3 * ne:]
        ch = chip_ref[0]
        for e, (row0, kind, w, _, _) in enumerate(entries):
            r, width = w.shape

            def gsum(rs, cs):
                return gall_ref[rs, cs]

            if kind == "full":
                g = gsum(slice(row0, row0 + r), slice(0, D))
            elif kind == "cols":
                g = gsum(slice(row0, row0 + r), slice(0, width))
                for q in range(1, N_CHIPS):
                    g = jnp.where(ch == q, gsum(slice(row0, row0 + r), slice(q * width, (q + 1) * width)), g)
            else:
                per_row = D // width
                g = gsum(slice(row0, row0 + 1), slice(0, width))
                for q in range(1, N_CHIPS):
                    rr = row0 + q // per_row
                    cc = (q % per_row) * width
                    g = jnp.where(ch == q, gsum(slice(rr, rr + 1), slice(cc, cc + width)), g)
            d, mn, vn = _adamw_math(ins[3 * e][...], g, ins[3 * e + 1][...], ins[3 * e + 2][...])
            outs[4 * e][...] = g
            outs[4 * e + 1][...] = d
            outs[4 * e + 2][...] = mn
            outs[4 * e + 3][...] = vn

    vm = pl.BlockSpec(memory_space=pltpu.VMEM)
    args, out_shape = [], []
    for _, _, w, m, v in entries:
        args += [w, m, v]
        out_shape += [jax.ShapeDtypeStruct(w.shape, F32)] * 4
    return pl.pallas_call(
        body, name=name,
        in_specs=[pl.BlockSpec(memory_space=pltpu.SMEM), vm] + [vm] * (3 * ne),
        out_specs=[vm] * (4 * ne), out_shape=out_shape,
        compiler_params=pltpu.CompilerParams(vmem_limit_bytes=VMEM_LIMIT),
    )(chip, gall, *args)


def _pack_rows(items, width):
    rows, starts, at = [], [], 0
    for it in items:
        r = it.shape[0]
        pad = (-r) % SUBLANES
        starts.append(at)
        rows.append(it)
        if pad:
            rows.append(jnp.zeros((pad, width), F32))
        at += r + pad
    return jnp.concatenate(rows, axis=0), starts


def kernel(x, a_norm, a_w_in, a_conv, a_w_out, b_norm, b_w_pw1, b_b_pw1, b_conv, b_b_conv, b_ln_g, b_ln_b, b_w_pw2, b_b_pw2, ffn_norm, ffn_w_gate, ffn_w_up, ffn_w_down, final_norm, loss_target, m_a_norm, m_a_w_in, m_a_conv, m_a_w_out, m_b_norm, m_b_w_pw1, m_b_b_pw1, m_b_conv, m_b_b_conv, m_b_ln_g, m_b_ln_b, m_b_w_pw2, m_b_b_pw2, m_ffn_norm, m_ffn_w_gate, m_ffn_w_up, m_ffn_w_down, m_final_norm, v_a_norm, v_a_w_in, v_a_conv, v_a_w_out, v_b_norm, v_b_w_pw1, v_b_b_pw1, v_b_conv, v_b_b_conv, v_b_ln_g, v_b_ln_b, v_b_w_pw2, v_b_b_pw2, v_ffn_norm, v_ffn_w_gate, v_ffn_w_up, v_ffn_w_down, v_final_norm):
    T, D = x.shape[1], x.shape[2]
    Dq = D // N_CHIPS
    cx, cy, cc = lax.axis_index("x"), lax.axis_index("y"), lax.axis_index("c")
    chip = (2 * cx + cy).astype(jnp.int32).reshape(1)
    cidx = cc.astype(jnp.int32).reshape(1)
    h0 = x.reshape(T, D)
    tgt = loss_target.reshape(T, D)

    small_shards = [a_conv[0], b_norm, b_b_pw1.reshape(2, Dq), b_conv[0], b_b_conv, b_ln_g, b_ln_b, b_b_pw2]
    packed, st = _pack_rows(small_shards, Dq)

    tr = lambda t: jnp.swapaxes(t, 1, 2)
    w_gate, m_gate, v_gate = tr(ffn_w_gate), tr(m_ffn_w_gate), tr(v_ffn_w_gate)
    w_up, m_up, v_up = tr(ffn_w_up), tr(m_ffn_w_up), tr(v_ffn_w_up)
    bf = lambda t: t.astype(BF16)
    s_in, s_out, s_pw1, s_pw2 = bf(a_w_in[0]), bf(a_w_out[0]), bf(b_w_pw1[0]), bf(b_w_pw2[0])
    s_gate, s_up, s_down = [bf(w_gate[l]) for l in (0, 1)], [bf(w_up[l]) for l in (0, 1)], [bf(ffn_w_down[l]) for l in (0, 1)]

    n0, (g_in,) = rms_fwd(h0, a_norm, "rms_a", hosted=[gather_whole([s_in], [])])
    bcv, (g_out, gate0, sw) = mm_cols(n0, g_in, None, "mm_w_in", hosted=[gather_p1([s_out, s_gate[0]]), gather_small(packed)])

    def whole(k, r):
        return jnp.transpose(sw[:, st[k]:st[k] + r, :], (1, 0, 2)).reshape(r, D)

    a_conv_f, b_norm_f = whole(0, 3), whole(1, 1)
    b_b_pw1_f = sw[:, st[2]:st[2] + 2, :].reshape(1, 2 * D)
    b_conv_f, b_b_conv_f, b_ln_g_f, b_ln_b_f, b_b_pw2_f = whole(3, b_conv.shape[1]), whole(4, 1), whole(5, 1), whole(6, 1), whole(7, 1)
    ya, (up0, g_out, gate0) = gateconv_fwd(bcv, a_conv_f, "gateconv_fwd",
                                           hosted=[gather_p1([s_up[0]]), gather_p2([g_out, gate0])])
    g_out = g_out.reshape(1, D, D)
    h1, (down0, up0) = mm_rows(ya[None], g_out, h0, None, "mm_w_out", hosted=[gather_p1([s_down[0]]), gather_p2([up0])])
    n1, fg0, fu0, gu0, h2, (down0, *later) = ffn_fwd(h1, ffn_norm[0:1], gate0, up0, gather_p2([down0]).awaited_first(), "ffn_fwd0",
                                                     hosted=[gather_p1([s_pw1, s_pw2, s_gate[1], s_up[1]])])
    n2, (g_pw1, g_pw2, gate1, up1) = rms_fwd(h2, b_norm_f, "rms_b", hosted=[gather_p2(later)])
    g_pw2 = g_pw2.reshape(1, D, D)
    ub, (down1,) = mm_cols(n2, g_pw1, b_b_pw1_f, "mm_pw1", hosted=[gather_p1([s_down[1]])])
    cu, sb, (down1,) = bconv_fwd(ub, b_conv_f, b_b_conv_f, b_ln_g_f, b_ln_b_f, "bconv_fwd", hosted=[gather_p2([down1])])
    h3, _ = mm_rows(sb[None], g_pw2, h2, b_b_pw2_f, "mm_pw2")
    n3, fg1, fu1, gu1, h4, _ = ffn_fwd(h3, ffn_norm[1:2], gate1, up1, down1, "ffn_fwd1")
    loss_part, dh4, d_final = loss_head(h4, final_norm.reshape(1, D), tgt, "loss_head")

    place = jnp.concatenate([chip, cidx])

    def pair_sums(ghs, from_sib, tags):
        return [pair_sum(g, r, cidx, "pair_sum_" + t) for g, r, t in zip(ghs, from_sib, tags)]

    def upd(w, m, v, bufs, parts, tag, hosted=()):
        res, xo = None, []
        for lyr, (b, p) in enumerate(zip(bufs, parts)):
            res, xo_l = adamw_reduce(w, m, v, b, p, place, lyr, res, "adamw_%s%d" % (tag, lyr), hosted=hosted if lyr == 0 else ())
            xo += xo_l
        return res, xo

    dg1, du1, dh3, d_fn1, _ = ffn_bwd(dh4, h3, ffn_norm[1:2], fg1, fu1, down1, gate1, up1, "ffn_bwd1")
    gh_down1, _ = tn_grad(gu1, dh4, N_CHIPS, True, "tn_down1")
    gh_gate1, _ = tn_grad(dg1, n3, N_CHIPS, True, "tn_gate1")
    gh_up1, _ = tn_grad(du1, n3, N_CHIPS, True, "tn_up1")
    f1 = [gh_gate1, gh_up1, gh_down1]

    dcu, d_ln_g, d_ln_b, d_b_conv, d_b_pw2, sib_f1 = pw2_ln_bwd(dh3, g_pw2, cu, b_ln_g_f, b_ln_b_f, "pw2_ln_bwd",
                                                                hosted=[sibling_halves(f1)])
    p_f1 = pair_sums(f1, sib_f1, ["gate1", "up1", "down1"])
    gh_pw2, _ = tn_grad_square(sb, dh3, N_CHIPS, "tn_pw2")
    dub, d_bconv_w, d_b_pw1, buf_f1 = bconv_bwd(dcu, ub, b_conv_f, "bconv_bwd", hosted=[scatter_p1(p_f1)])
    gh_pw1, buf_f1 = tn_grad(n2, dub, N_CHIPS, False, "tn_pw1", hosted=[scatter_p2(buf_f1)])
    b_grp = [gh_pw1, gh_pw2]
    dh2, d_b_norm, sib_b = nt_cols_rms(dub, g_pw1, h2, b_norm_f, dh3, "nt_pw1", hosted=[sibling_halves(b_grp)])
    p_b = pair_sums(b_grp, sib_b, ["pw1", "pw2"])

    dg0, du0, dh1, d_fn0, buf_b = ffn_bwd(dh2, h1, ffn_norm[0:1], fg0, fu0, down0, gate0, up0, "ffn_bwd0", hosted=[scatter_p1(p_b)])
    gh_down0, buf_b = tn_grad(gu0, dh2, N_CHIPS, True, "tn_down0", hosted=[scatter_p2(buf_b)])
    gh_gate0, sib_down0 = tn_grad(dg0, n1, N_CHIPS, True, "tn_gate0", hosted=[sibling_halves([gh_down0])])
    p_down0 = pair_sums([gh_down0], sib_down0, ["down0"])
    gh_up0, (buf_down0, sib_gate0) = tn_grad(du0, n1, N_CHIPS, True, "tn_up0",
                                             hosted=[scatter_p1(p_down0), sibling_halves([gh_gate0])])
    p_gate0 = pair_sums([gh_gate0], [sib_gate0], ["gate0"])
    dya, (buf_down0, sib_up0) = nt_rows(dh1, g_out, "nt_w_out",
                                        hosted=[scatter_p2([buf_down0]), sibling_halves([gh_up0])])
    p_up0 = pair_sums([gh_up0], [sib_up0], ["up0"])
    gh_out, _ = tn_grad_square(ya, dh1, N_CHIPS, "tn_w_out")
    dbcv, d_aconv_w, (buf_gate0, sib_out) = gateconv_bwd(dya[0], bcv, a_conv_f, "gateconv_bwd",
                                                         hosted=[scatter_p1(p_gate0), sibling_halves([gh_out])])
    p_out = pair_sums([gh_out], [sib_out], ["out"])
    gh_in, (buf_up0, buf_gate0) = tn_grad(n0, dbcv, N_CHIPS, False, "tn_w_in",
                                          hosted=[scatter_p1(p_up0), scatter_p2([buf_gate0])])
    sib_in = run_exchanges([sibling_halves([gh_in])], "reduce_in_siblings")
    p_in = pair_sums([gh_in], sib_in, ["in"])
    grad_x, d_a_norm, (buf_in, buf_out, buf_up0) = nt_cols_rms(
        dbcv, g_in, h0, a_norm, dh1, "nt_w_in", hosted=[scatter_p1(p_in + p_out), scatter_p2([buf_up0])])
    p_f0 = [p_gate0[0], p_up0[0], p_down0[0]]

    d_ffn_norm = jnp.concatenate([d_fn0, d_fn1], axis=0)
    small_grads = [d_a_norm, d_aconv_w, d_b_norm, d_b_pw1.reshape(2, D), d_bconv_w, d_b_conv, d_ln_g, d_ln_b, d_b_pw2,
                   d_ffn_norm, d_final, jnp.broadcast_to(loss_part, (1, D))]
    gpacked, gs = _pack_rows(small_grads, D)
    gall, (buf_in, buf_out) = small_allreduce(gpacked, "allreduce_small_grads", hosted=[scatter_p2([buf_in, buf_out])])
    buf_a, p_a = [buf_in, buf_out], [p_in[0], p_out[0]]

    r_gate, _ = upd(w_gate, m_gate, v_gate, [buf_gate0, buf_f1[0]], [p_f0[0], p_f1[0]], "gate")
    r_up, _ = upd(w_up, m_up, v_up, [buf_up0, buf_f1[1]], [p_f0[1], p_f1[1]], "up")
    r_down, _ = upd(ffn_w_down, m_ffn_w_down, v_ffn_w_down, [buf_down0, buf_f1[2]], [p_f0[2], p_f1[2]], "down")
    r_gate, r_up = [tr(t) for t in r_gate], [tr(t) for t in r_up]
    r_pw1, _ = upd(b_w_pw1, m_b_w_pw1, v_b_w_pw1, [buf_b[0]], [p_b[0]], "pw1")
    r_pw2, _ = upd(b_w_pw2, m_b_w_pw2, v_b_w_pw2, [buf_b[1]], [p_b[1]], "pw2")
    r_in, _ = upd(a_w_in, m_a_w_in, v_a_w_in, [buf_a[0]], [p_a[0]], "w_in")
    r_out, _ = upd(a_w_out, m_a_w_out, v_a_w_out, [buf_a[1]], [p_a[1]], "w_out")
    entries = [
        (gs[0], "full", a_norm, m_a_norm, v_a_norm),
        (gs[1], "cols", a_conv[0], m_a_conv[0], v_a_conv[0]),
        (gs[2], "cols", b_norm, m_b_norm, v_b_norm),
        (gs[3], "flat2", b_b_pw1, m_b_b_pw1, v_b_b_pw1),
        (gs[4], "cols", b_conv[0], m_b_conv[0], v_b_conv[0]),
        (gs[5], "cols", b_b_conv, m_b_b_conv, v_b_b_conv),
        (gs[6], "cols", b_ln_g, m_b_ln_g, v_b_ln_g),
        (gs[7], "cols", b_ln_b, m_b_ln_b, v_b_ln_b),
        (gs[8], "cols", b_b_pw2, m_b_b_pw2, v_b_b_pw2),
        (gs[9], "full", ffn_norm, m_ffn_norm, v_ffn_norm),
        (gs[10], "full", final_norm.reshape(1, D), m_final_norm.reshape(1, D), v_final_norm.reshape(1, D)),
    ]
    so = small_update(gall, chip, entries, "small_update")
    sm = [so[4 * e:4 * e + 4] for e in range(len(entries))]

    def shaped(e, like):
        return [t.reshape(like.shape) for t in sm[e]]

    r_a_norm, r_a_conv, r_b_norm, r_b_b_pw1 = shaped(0, a_norm), shaped(1, a_conv), shaped(2, b_norm), shaped(3, b_b_pw1)
    r_b_conv, r_b_b_conv, r_b_ln_g, r_b_ln_b = shaped(4, b_conv), shaped(5, b_b_conv), shaped(6, b_ln_g), shaped(7, b_ln_b)
    r_b_b_pw2, r_ffn_norm, r_final = shaped(8, b_b_pw2), shaped(9, ffn_norm), shaped(10, final_norm)

    loss = gall[gs[11], 0]
    order =[r_a_norm, r_in, r_a_conv, r_out, r_b_norm, r_pw1, r_b_b_pw1, r_b_conv, r_b_b_conv, r_b_ln_g, r_b_ln_b,
             r_pw2, r_b_b_pw2, r_ffn_norm, r_gate, r_up, r_down, r_final]
    outs = [loss, grad_x.reshape(x.shape)]
    for field in range(4):
        outs += [r[field] for r in order]
    return tuple(outs)
```

```python
import functools

import jax
import jax.numpy as jnp
from jax import lax
from jax.experimental import pallas as pl
from jax.experimental.pallas import tpu as pltpu

RMS_EPS = 1e-6
LN_EPS = 1e-5
ADAM_LR = 0.001
ADAM_B1 = 0.9
ADAM_B2 = 0.999
ADAM_EPS = 1e-08
ADAM_WD = 0.01
ADAM_STEP = 10

N_CHIPS = 4
N_DEV = 8
LANES = 128
SUBLANES = 8
HALO = 32
CONV_ROWS = 64
TOKEN_TILE = 512
WIDE_TOKEN_TILE = 1024
GRAD_TOKEN_TILE = 2048
FFN_ROW_CHUNKS = 2
FFN_SEGS_PER_STEP = 2
ROW_TILE = 256
VMEM_LIMIT = 56 * 1024 * 1024
MESH = pl.DeviceIdType.MESH
BF16 = jnp.bfloat16
F32 = jnp.float32


def _tile(n, pref, mult=SUBLANES):
    t = min(n, pref) // mult * mult
    while n % t:
        t -= mult
    return t


def _params(sem):
    return pltpu.CompilerParams(dimension_semantics=sem, vmem_limit_bytes=VMEM_LIMIT)


def _sigmoid(x):
    return 0.5 * jnp.tanh(0.5 * x) + 0.5


class _Exchange:
    def __init__(self, ins, outs, aliases, n_sems, copies, then=None):
        self.ins, self.outs, self.aliases, self.n_sems, self.copies = list(ins), list(outs), dict(aliases), n_sems, copies
        self.then = then
        self.early = False

    def awaited_first(self):
        self.early = True
        return self

    def start(self, xi, xo, ssem, rsem):
        for cp in self.copies(xi, xo, ssem, rsem)[0]:
            cp.start()

    def finish(self, xi, xo, ssem, rsem):
        sends, recvs = self.copies(xi, xo, ssem, rsem)
        for cp in recvs:
            cp.wait_recv()
        if self.then is not None:
            sends2, recvs2 = self.then(xi, xo, ssem, rsem)
            for cp in sends2:
                cp.start()
            for cp in recvs2:
                cp.wait_recv()
            sends = sends + sends2
        for cp in sends:
            cp.wait_send()


def _call(body, name, grid, in_specs, out_specs, out_shape, args, sem, scratch_shapes=(), hosted=(), prefetch=(),
          own_aliases=None):
    in_specs, out_specs, out_shape = list(in_specs), list(out_specs), list(out_shape)
    scratch_shapes, hosted, prefetch = list(scratch_shapes), list(hosted), list(prefetch)
    n_pre, n_in, n_out, n_scr = len(prefetch), len(args), len(out_shape), len(scratch_shapes)
    x_in = [a for ex in hosted for a in ex.ins]
    x_out = [o for ex in hosted for o in ex.outs]
    aliases = {n_pre + i: o for i, o in (own_aliases or {}).items()}
    at_in, at_out = n_pre + n_in, n_out
    for ex in hosted:
        for i, o in ex.aliases.items():
            aliases[at_in + i] = at_out + o
        at_in += len(ex.ins)
        at_out += len(ex.outs)
    sems = [pltpu.SemaphoreType.DMA((ex.n_sems,)) for ex in hosted for _ in range(2)]

    def wrapped(*refs):
        pre, refs = refs[:n_pre], refs[n_pre:]
        ins, xi = refs[:n_in], refs[n_in:n_in + len(x_in)]
        refs = refs[n_in + len(x_in):]
        outs, xo = refs[:n_out], refs[n_out:n_out + len(x_out)]
        refs = refs[n_out + len(x_out):]
        scr, sm = refs[:n_scr], refs[n_scr:]
        views, a, b = [], 0, 0
        for e, ex in enumerate(hosted):
            views.append((xi[a:a + len(ex.ins)], xo[b:b + len(ex.outs)], sm[2 * e], sm[2 * e + 1]))
            a += len(ex.ins)
            b += len(ex.outs)
        first = last = None
        for ax, g in enumerate(grid):
            f, l = pl.program_id(ax) == 0, pl.program_id(ax) == g - 1
            first, last = (f, l) if first is None else (first & f, last & l)

        def begin():
            for ex, v in zip(hosted, views):
                ex.start(*v)
            for ex, v in zip(hosted, views):
                if ex.early:
                    ex.finish(*v)

        def end():
            for ex, v in zip(hosted, views):
                if not ex.early:
                    ex.finish(*v)

        if hosted and grid:
            pl.when(first)(begin)
        elif hosted:
            begin()
        early_refs = [r for ex, v in zip(hosted, views) if ex.early for r in v[1]]
        body(*pre, *ins, *outs, *scr, *early_refs)
        if hosted and grid:
            pl.when(last)(end)
        elif hosted:
            end()

    hbm = pl.BlockSpec(memory_space=pl.ANY)
    all_in, all_out = in_specs + [hbm] * len(x_in), out_specs + [hbm] * len(x_out)
    kw = dict(name=name, out_shape=out_shape + x_out, input_output_aliases=aliases,
              compiler_params=_params(tuple("arbitrary" for _ in grid) if hosted else sem))
    if prefetch:
        kw["grid_spec"] = pltpu.PrefetchScalarGridSpec(num_scalar_prefetch=n_pre, grid=grid, in_specs=all_in,
                                                       out_specs=all_out, scratch_shapes=scratch_shapes + sems)
    else:
        kw.update(grid=grid, in_specs=all_in, out_specs=all_out, scratch_shapes=scratch_shapes + sems)
    res = pl.pallas_call(wrapped, **kw)(*prefetch, *args, *x_in)
    return list(res[:n_out]), list(res[n_out:])


def rms_fwd(h, gain, name, hosted=()):
    T, D = h.shape
    tm = _tile(T, TOKEN_TILE)

    def body(h_ref, g_ref, o_ref):
        x = h_ref[...]
        r = lax.rsqrt(jnp.mean(x * x, axis=-1, keepdims=True) + RMS_EPS)
        o_ref[...] = (x * r * g_ref[...]).astype(o_ref.dtype)

    (n,), xo = _call(
        body, name, (T // tm,),
        [pl.BlockSpec((tm, D), lambda i: (i, 0)), pl.BlockSpec((1, D), lambda i: (0, 0))],
        [pl.BlockSpec((tm, D), lambda i: (i, 0))], [jax.ShapeDtypeStruct((T, D), BF16)],
        [h, gain], ("parallel",), hosted=hosted)
    return n, xo


def loss_head(h, gain, tgt, name):
    T, D = h.shape
    tm = _tile(T, TOKEN_TILE)

    def body(h_ref, g_ref, t_ref, loss_ref, dh_ref, dg_ref):
        i = pl.program_id(0)
        x = h_ref[...]
        g = g_ref[...]
        r = lax.rsqrt(jnp.mean(x * x, axis=-1, keepdims=True) + RMS_EPS)
        xhat = x * r
        diff = xhat * g - t_ref[...]
        part_loss = 0.5 * jnp.sum(jnp.mean(diff * diff, axis=-1, keepdims=True), axis=0, keepdims=True)
        dy = diff * (1.0 / D)
        dxhat = dy * g
        dh_ref[...] = r * (dxhat - xhat * jnp.mean(dxhat * xhat, axis=-1, keepdims=True))
        part = jnp.sum(dy * xhat, axis=0, keepdims=True)

        @pl.when(i == 0)
        def _():
            dg_ref[...] = part
            loss_ref[...] = part_loss

        @pl.when(i > 0)
        def _():
            dg_ref[...] += part
            loss_ref[...] += part_loss

    row = pl.BlockSpec((tm, D), lambda i: (i, 0))
    vec = pl.BlockSpec((1, D), lambda i: (0, 0))
    return pl.pallas_call(
        body, name=name, grid=(T // tm,),
        in_specs=[row, vec, row],
        out_specs=[pl.BlockSpec((1, 1), lambda i: (0, 0)), row, vec],
        out_shape=[jax.ShapeDtypeStruct((1, 1), F32), jax.ShapeDtypeStruct((T, D), F32),
                   jax.ShapeDtypeStruct((1, D), F32)],
        compiler_params=_params(("arbitrary",)),
    )(h, gain, tgt)


def _prev_halo_spec(tm, width):
    return pl.BlockSpec((HALO, width), lambda i: (jnp.maximum(i * (tm // HALO) - 1, 0), 0))


def _next_halo_spec(tm, width, T):
    return pl.BlockSpec((HALO, width), lambda i: (jnp.minimum((i + 1) * (tm // HALO), T // HALO - 1), 0))


def _shifted(win, off, rows):
    if off % SUBLANES == 0:
        return win[off:off + rows]
    n = win.shape[0]
    return pltpu.roll(win, (n - off) % n, 0)[:rows]


def _rowsum8(x):
    acc = x[0:SUBLANES]
    for q in range(1, x.shape[0] // SUBLANES):
        acc = acc + x[q * SUBLANES:(q + 1) * SUBLANES]
    return acc


def _conv_loops(tm, D, per_block):
    def chunk(r, carry):
        t0 = pl.multiple_of(r * CONV_ROWS, CONV_ROWS)
        for lb in range(D // LANES):
            per_block(t0, slice(lb * LANES, (lb + 1) * LANES))
        return carry

    lax.fori_loop(0, tm // CONV_ROWS, chunk, 0)


def gateconv_fwd(bcv, w, name, hosted=()):
    T, D3 = bcv.shape
    D = D3 // 3
    K = w.shape[0]
    tm = _tile(T, TOKEN_TILE)

    def body(x_ref, halo_ref, w_ref, y_ref, pad_ref):
        i = pl.program_id(0)
        pad_ref[HALO:, :] = x_ref[:, D:2 * D] * x_ref[:, 2 * D:]
        pad_ref[:HALO, :] = jnp.where(i > 0, halo_ref[:, D:2 * D] * halo_ref[:, 2 * D:], 0.0)

        def block(t0, ls):
            win = pad_ref[pl.ds(t0, CONV_ROWS + HALO), ls]
            acc = jnp.zeros((CONV_ROWS, LANES), F32)
            for k in range(K):
                acc = acc + w_ref[k:k + 1, ls] * _shifted(win, HALO - (K - 1) + k, CONV_ROWS)
            y_ref[pl.ds(t0, CONV_ROWS), ls] = (x_ref[pl.ds(t0, CONV_ROWS), ls] * acc).astype(y_ref.dtype)

        _conv_loops(tm, D, block)

    (y,), xo = _call(
        body, name, (T // tm,),
        [pl.BlockSpec((tm, D3), lambda i: (i, 0)), _prev_halo_spec(tm, D3), pl.BlockSpec((K, D), lambda i: (0, 0))],
        [pl.BlockSpec((tm, D), lambda i: (i, 0))], [jax.ShapeDtypeStruct((T, D), BF16)],
        [bcv, bcv, w], ("parallel",), [pltpu.VMEM((tm + HALO, D), F32)], hosted=hosted)
    return y, xo


def gateconv_bwd(dy, bcv, w, name, hosted=()):
    T, D3 = bcv.shape
    D = D3 // 3
    K = w.shape[0]
    tm = _tile(T, TOKEN_TILE)
    nt = T // tm

    def body(dy_ref, dyn_ref, x_ref, xp_ref, xn_ref, w_ref, o_ref, dw_ref, cv_ref, dc_ref, wacc_ref):
        i = pl.program_id(0)
        cv_ref[HALO:, :] = x_ref[:, D:2 * D] * x_ref[:, 2 * D:]
        cv_ref[:HALO, :] = jnp.where(i > 0, xp_ref[:, D:2 * D] * xp_ref[:, 2 * D:], 0.0)
        dc_ref[:tm, :] = dy_ref[...] * x_ref[:, :D]
        dc_ref[tm:, :] = jnp.where(i < nt - 1, dyn_ref[...] * xn_ref[:, :D], 0.0)

        @pl.when(i == 0)
        def _():
            wacc_ref[...] = jnp.zeros_like(wacc_ref)

        def block(t0, ls):
            cwin = cv_ref[pl.ds(t0, CONV_ROWS + HALO), ls]
            dwin = dc_ref[pl.ds(t0, CONV_ROWS + HALO), ls]
            dcon = dwin[:CONV_ROWS]
            conv = jnp.zeros((CONV_ROWS, LANES), F32)
            dcv = jnp.zeros((CONV_ROWS, LANES), F32)
            for k in range(K):
                wk = w_ref[k:k + 1, ls]
                cs = _shifted(cwin, HALO - (K - 1) + k, CONV_ROWS)
                conv = conv + wk * cs
                dcv = dcv + wk * _shifted(dwin, (K - 1) - k, CONV_ROWS)
                wacc_ref[k * SUBLANES:(k + 1) * SUBLANES, ls] += _rowsum8(dcon * cs)
            rows = pl.ds(t0, CONV_ROWS)
            o_ref[rows, ls] = (dy_ref[rows, ls] * conv).astype(o_ref.dtype)
            o_ref[rows, pl.ds(D + ls.start, LANES)] = (dcv * x_ref[rows, pl.ds(2 * D + ls.start, LANES)]).astype(o_ref.dtype)
            o_ref[rows, pl.ds(2 * D + ls.start, LANES)] = (dcv * x_ref[rows, pl.ds(D + ls.start, LANES)]).astype(o_ref.dtype)

        _conv_loops(tm, D, block)

        @pl.when(i == nt - 1)
        def _():
            for k in range(K):
                dw_ref[k:k + 1, :] = jnp.sum(wacc_ref[k * SUBLANES:(k + 1) * SUBLANES, :], axis=0, keepdims=True)

    (dx, dw), xo = _call(
        body, name, (nt,),
        [pl.BlockSpec((tm, D), lambda i: (i, 0)), _next_halo_spec(tm, D, T),
         pl.BlockSpec((tm, D3), lambda i: (i, 0)), _prev_halo_spec(tm, D3), _next_halo_spec(tm, D3, T),
         pl.BlockSpec((K, D), lambda i: (0, 0))],
        [pl.BlockSpec((tm, D3), lambda i: (i, 0)), pl.BlockSpec((K, D), lambda i: (0, 0))],
        [jax.ShapeDtypeStruct((T, D3), BF16), jax.ShapeDtypeStruct((K, D), F32)],
        [dy, dy, bcv, bcv, bcv, w], ("arbitrary",),
        [pltpu.VMEM((tm + HALO, D), F32), pltpu.VMEM((tm + HALO, D), F32), pltpu.VMEM((K * SUBLANES, D), F32)],
        hosted=hosted)
    return dx, dw, xo


def bconv_fwd(u, w, b_conv, ln_g, ln_b, name, hosted=()):
    T, D2 = u.shape
    D = D2 // 2
    K = w.shape[0]
    tm = _tile(T, TOKEN_TILE)

    def body(u_ref, halo_ref, w_ref, bc_ref, g_ref, b_ref, cu_ref, s_ref, pad_ref):
        i = pl.program_id(0)
        pad_ref[HALO:, :] = u_ref[:, :D] * _sigmoid(u_ref[:, D:])
        pad_ref[:HALO, :] = jnp.where(i > 0, halo_ref[:, :D] * _sigmoid(halo_ref[:, D:]), 0.0)

        def block(t0, ls):
            win = pad_ref[pl.ds(t0, CONV_ROWS + HALO), ls]
            acc = jnp.zeros((CONV_ROWS, LANES), F32)
            for k in range(K):
                acc = acc + w_ref[k:k + 1, ls] * _shifted(win, HALO - (K - 1) + k, CONV_ROWS)
            cu_ref[pl.ds(t0, CONV_ROWS), ls] = acc + bc_ref[:, ls]

        _conv_loops(tm, D, block)
        cu = cu_ref[...]
        mu = jnp.mean(cu, axis=-1, keepdims=True)
        xc = cu - mu
        rstd = lax.rsqrt(jnp.mean(xc * xc, axis=-1, keepdims=True) + LN_EPS)
        ln = xc * rstd * g_ref[...] + b_ref[...]
        s_ref[...] = (ln * _sigmoid(ln)).astype(s_ref.dtype)

    vec = pl.BlockSpec((1, D), lambda i: (0, 0))
    row = pl.BlockSpec((tm, D), lambda i: (i, 0))
    (cu, s), xo = _call(
        body, name, (T // tm,),
        [pl.BlockSpec((tm, D2), lambda i: (i, 0)), _prev_halo_spec(tm, D2), pl.BlockSpec((K, D), lambda i: (0, 0)), vec, vec, vec],
        [row, row], [jax.ShapeDtypeStruct((T, D), F32), jax.ShapeDtypeStruct((T, D), BF16)],
        [u, u, w, b_conv, ln_g, ln_b], ("parallel",), [pltpu.VMEM((tm + HALO, D), F32)], hosted=hosted)
    return cu, s, xo


def pw2_ln_bwd(dy, w, cu, ln_g, ln_b, name, hosted=()):
    T, D = cu.shape
    tm = _tile(T, TOKEN_TILE)

    def body(dy_ref, w_ref, cu_ref, g_ref, b_ref, dcu_ref, dg_ref, db_ref, dbc_ref, dbo_ref):
        i = pl.program_id(0)
        dy_ = dy_ref[...]
        ds = lax.dot_general(dy_.astype(BF16), w_ref[0], _NT, preferred_element_type=F32)
        cu_ = cu_ref[...]
        mu = jnp.mean(cu_, axis=-1, keepdims=True)
        xc = cu_ - mu
        rstd = lax.rsqrt(jnp.mean(xc * xc, axis=-1, keepdims=True) + LN_EPS)
        xh = xc * rstd
        ln = xh * g_ref[...] + b_ref[...]
        sg = _sigmoid(ln)
        dl = ds * (sg * (1.0 + ln * (1.0 - sg)))
        dxh = dl * g_ref[...]
        dcu = rstd * (dxh - jnp.mean(dxh, axis=-1, keepdims=True) - xh * jnp.mean(dxh * xh, axis=-1, keepdims=True))
        dcu_ref[...] = dcu
        pg = jnp.sum(dl * xh, axis=0, keepdims=True)
        pb = jnp.sum(dl, axis=0, keepdims=True)
        pc = jnp.sum(dcu, axis=0, keepdims=True)
        po = jnp.sum(dy_, axis=0, keepdims=True)

        @pl.when(i == 0)
        def _():
            dg_ref[...] = pg
            db_ref[...] = pb
            dbc_ref[...] = pc
            dbo_ref[...] = po

        @pl.when(i > 0)
        def _():
            dg_ref[...] += pg
            db_ref[...] += pb
            dbc_ref[...] += pc
            dbo_ref[...] += po

    vec = pl.BlockSpec((1, D), lambda i: (0, 0))
    row = pl.BlockSpec((tm, D), lambda i: (i, 0))
    vshape = jax.ShapeDtypeStruct((1, D), F32)
    outs, xo = _call(
        body, name, (T // tm,), [row, pl.BlockSpec((1, D, D), lambda i: (0, 0, 0)), row, vec, vec], [row, vec, vec, vec, vec],
        [jax.ShapeDtypeStruct((T, D), F32), vshape, vshape, vshape, vshape], [dy, w, cu, ln_g, ln_b], ("arbitrary",),
        hosted=hosted)
    return (*outs, xo)


def bconv_bwd(dcu, u, w, name, hosted=()):
    T, D2 = u.shape
    D = D2 // 2
    K = w.shape[0]
    tm = _tile(T, TOKEN_TILE)
    nt = T // tm

    def body(dc_ref, dcn_ref, u_ref, up_ref, w_ref, du_ref, dw_ref, db_ref, glu_ref, dpad_ref, dglu_ref, wacc_ref):
        i = pl.program_id(0)
        glu_ref[HALO:, :] = u_ref[:, :D] * _sigmoid(u_ref[:, D:])
        glu_ref[:HALO, :] = jnp.where(i > 0, up_ref[:, :D] * _sigmoid(up_ref[:, D:]), 0.0)
        dpad_ref[:tm, :] = dc_ref[...]
        dpad_ref[tm:, :] = jnp.where(i < nt - 1, dcn_ref[...], 0.0)

        @pl.when(i == 0)
        def _():
            wacc_ref[...] = jnp.zeros_like(wacc_ref)

        def block(t0, ls):
            gwin = glu_ref[pl.ds(t0, CONV_ROWS + HALO), ls]
            dwin = dpad_ref[pl.ds(t0, CONV_ROWS + HALO), ls]
            dcur = dwin[:CONV_ROWS]
            dglu = jnp.zeros((CONV_ROWS, LANES), F32)
            for k in range(K):
                dglu = dglu + w_ref[k:k + 1, ls] * _shifted(dwin, (K - 1) - k, CONV_ROWS)
                gs = _shifted(gwin, HALO - (K - 1) + k, CONV_ROWS)
                wacc_ref[k * SUBLANES:(k + 1) * SUBLANES, ls] += _rowsum8(dcur * gs)
            dglu_ref[pl.ds(t0, CONV_ROWS), ls] = dglu

        _conv_loops(tm, D, block)
        dglu = dglu_ref[...]
        a = u_ref[:, :D]
        sg = _sigmoid(u_ref[:, D:])
        da = dglu * sg
        dg = dglu * a * (sg * (1.0 - sg))
        du_ref[:, :D] = da.astype(du_ref.dtype)
        du_ref[:, D:] = dg.astype(du_ref.dtype)
        pa = jnp.sum(da, axis=0, keepdims=True)
        pg = jnp.sum(dg, axis=0, keepdims=True)

        @pl.when(i == 0)
        def _():
            db_ref[:, :D] = pa
            db_ref[:, D:] = pg

        @pl.when(i > 0)
        def _():
            db_ref[:, :D] += pa
            db_ref[:, D:] += pg

        @pl.when(i == nt - 1)
        def _():
            for k in range(K):
                dw_ref[k:k + 1, :] = jnp.sum(wacc_ref[k * SUBLANES:(k + 1) * SUBLANES, :], axis=0, keepdims=True)

    (du, dw, db), xo = _call(
        body, name, (nt,),
        [pl.BlockSpec((tm, D), lambda i: (i, 0)), _next_halo_spec(tm, D, T),
         pl.BlockSpec((tm, D2), lambda i: (i, 0)), _prev_halo_spec(tm, D2), pl.BlockSpec((K, D), lambda i: (0, 0))],
        [pl.BlockSpec((tm, D2), lambda i: (i, 0)), pl.BlockSpec((K, D), lambda i: (0, 0)), pl.BlockSpec((1, D2), lambda i: (0, 0))],
        [jax.ShapeDtypeStruct((T, D2), BF16), jax.ShapeDtypeStruct((K, D), F32), jax.ShapeDtypeStruct((1, D2), F32)],
        [dcu, dcu, u, u, w], ("arbitrary",),
        [pltpu.VMEM((tm + HALO, D), F32), pltpu.VMEM((tm + HALO, D), F32), pltpu.VMEM((tm, D), F32),
         pltpu.VMEM((K * SUBLANES, D), F32)], hosted=hosted)
    return du, dw, db, xo


def mm_cols(a, w, bias, name, hosted=()):
    T, K = a.shape
    S, _, n = w.shape
    tm = _tile(T, WIDE_TOKEN_TILE)

    def body(*refs):
        a_ref, w_ref = refs[:2]
        o_ref = refs[-1]
        acc = jnp.dot(a_ref[...], w_ref[...], preferred_element_type=F32)
        if bias is not None:
            acc = acc + refs[2][...]
        o_ref[...] = acc

    in_specs = [pl.BlockSpec((tm, K), lambda s, i: (i, 0)), pl.BlockSpec((None, K, n), lambda s, i: (s, 0, 0))]
    args = [a, w]
    if bias is not None:
        in_specs.append(pl.BlockSpec((1, n), lambda s, i: (0, s)))
        args.append(bias)
    (out,), xo = _call(body, name, (S, T // tm), in_specs, [pl.BlockSpec((tm, n), lambda s, i: (i, s))],
                       [jax.ShapeDtypeStruct((T, S * n), F32)], args, ("parallel", "parallel"), hosted=hosted)
    return out, xo


def _load_weights(pairs, sems, S, i, p):
    def copies(seg):
        return [pltpu.make_async_copy(src.at[seg], dst.at[seg], sems.at[k, seg]) for k, (src, dst) in enumerate(pairs)]

    @pl.when((i == 0) & (p == 0))
    def _():
        for seg in range(S):
            for cp in copies(seg):
                cp.start()

    @pl.when((i == 0) & (p < S // FFN_SEGS_PER_STEP))
    def _():
        for j in range(FFN_SEGS_PER_STEP):
            for cp in copies(FFN_SEGS_PER_STEP * p + j):
                cp.wait()


def ffn_fwd(h, gain, wg, wu, wd, name, hosted=()):
    T, D = h.shape
    S, f, _ = wg.shape
    tm = _tile(T, TOKEN_TILE)
    rc = tm // FFN_ROW_CHUNKS
    chunks = [slice(r * rc, (r + 1) * rc) for r in range(FFN_ROW_CHUNKS)]
    G = FFN_SEGS_PER_STEP
    wd_arrives = isinstance(wd, _Exchange)
    weights = [wg, wu] if wd_arrives else [wg, wu, wd]
    hosted = ([wd] if wd_arrives else []) + list(hosted)

    def body(h_ref, gain_ref, *refs):
        nw = len(weights)
        wg_hbm, wu_hbm = refs[:2]
        wd_hbm = refs[-1] if wd_arrives else refs[2]
        n_ref, g_ref, u_ref, gu_ref, o_ref, wg_v, wu_v, wd_v, sems = refs[nw:nw + 9]
        i, p = pl.program_id(0), pl.program_id(1)
        _load_weights([(wg_hbm, wg_v), (wu_hbm, wu_v), (wd_hbm, wd_v)], sems, S, i, p)

        @pl.when(p == 0)
        def _():
            x = h_ref[...]
            r = lax.rsqrt(jnp.mean(x * x, axis=-1, keepdims=True) + RMS_EPS)
            n_ref[...] = (x * r * gain_ref[...]).astype(n_ref.dtype)

        parts = []
        for rows in chunks:
            a = n_ref[rows, :]
            acc = None
            for j in range(G):
                seg = G * p + j
                g = lax.dot_general(a, wg_v[seg], _NT, preferred_element_type=F32)
                u = lax.dot_general(a, wu_v[seg], _NT, preferred_element_type=F32)
                gu = (g * _sigmoid(g) * u).astype(gu_ref.dtype)
                g_ref[j, rows, :] = g.astype(g_ref.dtype)
                u_ref[j, rows, :] = u.astype(u_ref.dtype)
                gu_ref[j, rows, :] = gu
                part = jnp.dot(gu, wd_v[seg], preferred_element_type=F32)
                acc = part if acc is None else acc + part
            parts.append(acc)

        @pl.when(p == 0)
        def _():
            for rows, part in zip(chunks, parts):
                o_ref[rows, :] = h_ref[rows, :] + part

        @pl.when(p > 0)
        def _():
            for rows, part in zip(chunks, parts):
                o_ref[rows, :] += part

    row = pl.BlockSpec((tm, D), lambda i, p: (i, 0))
    seg = pl.BlockSpec((G, tm, f), lambda i, p: (p, i, 0))
    hbm = pl.BlockSpec(memory_space=pl.ANY)
    segs = jax.ShapeDtypeStruct((S, T, f), BF16)
    outs, xo = _call(
        body, name, (T // tm, S // G),
        [row, pl.BlockSpec((1, D), lambda i, s: (0, 0))] + [hbm] * len(weights), [row, seg, seg, seg, row],
        [jax.ShapeDtypeStruct((T, D), BF16), segs, segs, segs, jax.ShapeDtypeStruct((T, D), F32)],
        [h, gain] + weights, ("arbitrary", "arbitrary"),
        [pltpu.VMEM((S, f, D), BF16), pltpu.VMEM((S, f, D), BF16), pltpu.VMEM((S, f, D), BF16), pltpu.SemaphoreType.DMA((3, S))],
        hosted=hosted)
    return (*outs, xo)


def ffn_bwd(dy, h, gain, g, u, wd, wg, wu, name, hosted=()):
    T, D = h.shape
    S, f, _ = wg.shape
    tm = _tile(T, TOKEN_TILE)
    nt = T // tm
    rc = tm // FFN_ROW_CHUNKS
    chunks = [slice(r * rc, (r + 1) * rc) for r in range(FFN_ROW_CHUNKS)]
    G = FFN_SEGS_PER_STEP
    P = S // G

    def body(dy_ref, h_ref, gain_ref, g_ref, u_ref, wd_hbm, wg_hbm, wu_hbm, dg_ref, du_ref, dh_ref, dgain_ref,
             wd_v, wg_v, wu_v, dyb_ref, dgs_ref, dus_ref, sems):
        i, p = pl.program_id(0), pl.program_id(1)
        _load_weights([(wd_hbm, wd_v), (wg_hbm, wg_v), (wu_hbm, wu_v)], sems, S, i, p)

        def first_stage(grp, slot):
            for rows in chunks:
                for j in range(G):
                    dgu = lax.dot_general(dyb_ref[rows, :], wd_v[G * grp + j], _NT, preferred_element_type=F32)
                    gv = g_ref[j, rows, :].astype(F32)
                    sg = _sigmoid(gv)
                    dg = (dgu * u_ref[j, rows, :].astype(F32) * (sg * (1.0 + gv * (1.0 - sg)))).astype(dg_ref.dtype)
                    du = (dgu * (gv * sg)).astype(du_ref.dtype)
                    dg_ref[j, rows, :] = dg
                    du_ref[j, rows, :] = du
                    dgs_ref[slot, j, rows, :] = dg
                    dus_ref[slot, j, rows, :] = du

        def second_stage(grp, slot):
            for rows in chunks:
                acc = None
                for j in range(G):
                    part = (jnp.dot(dgs_ref[slot, j, rows, :], wg_v[G * grp + j], preferred_element_type=F32)
                            + jnp.dot(dus_ref[slot, j, rows, :], wu_v[G * grp + j], preferred_element_type=F32))
                    acc = part if acc is None else acc + part
                dh_ref[rows, :] += acc

        @pl.when(p == 0)
        def _():
            dyb_ref[...] = dy_ref[...].astype(dyb_ref.dtype)
            dh_ref[...] = jnp.zeros_like(dh_ref)
            first_stage(0, 0)

        @pl.when((p > 0) & (p < P))
        def _():
            second_stage(p - 1, (p - 1) % 2)
            first_stage(p, p % 2)

        @pl.when(p == P)
        def _():
            second_stage(P - 1, (P - 1) % 2)
            dn = dh_ref[...]
            x = h_ref[...]
            r = lax.rsqrt(jnp.mean(x * x, axis=-1, keepdims=True) + RMS_EPS)
            xhat = x * r
            dxhat = dn * gain_ref[...]
            dh_ref[...] = dy_ref[...] + r * (dxhat - xhat * jnp.mean(dxhat * xhat, axis=-1, keepdims=True))
            pg = jnp.sum(dn * xhat, axis=0, keepdims=True)

            @pl.when(i == 0)
            def _():
                dgain_ref[...] = pg

            @pl.when(i > 0)
            def _():
                dgain_ref[...] += pg

    row = pl.BlockSpec((tm, D), lambda i, p: (i, 0))
    vec = pl.BlockSpec((1, D), lambda i, p: (0, 0))
    seg = pl.BlockSpec((G, tm, f), lambda i, p: (jnp.minimum(p, P - 1), i, 0))
    hbm = pl.BlockSpec(memory_space=pl.ANY)
    segs = jax.ShapeDtypeStruct((S, T, f), BF16)
    outs, xo = _call(
        body, name, (nt, P + 1),
        [row, row, vec, seg, seg, hbm, hbm, hbm], [seg, seg, row, vec],
        [segs, segs, jax.ShapeDtypeStruct((T, D), F32), jax.ShapeDtypeStruct((1, D), F32)],
        [dy, h, gain, g, u, wd, wg, wu], ("arbitrary", "arbitrary"),
        [pltpu.VMEM((S, f, D), BF16), pltpu.VMEM((S, f, D), BF16), pltpu.VMEM((S, f, D), BF16),
         pltpu.VMEM((tm, D), BF16), pltpu.VMEM((2, G, tm, f), BF16), pltpu.VMEM((2, G, tm, f), BF16),
         pltpu.SemaphoreType.DMA((3, S))], hosted=hosted)
    return (*outs, xo)


def mm_rows(a, w, res, bias, name, hosted=()):
    S, T, k = a.shape
    N = w.shape[-1]
    tm = _tile(T, TOKEN_TILE)

    def body(*refs):
        a_ref, w_ref, r_ref = refs[:3]
        o_ref = refs[-1]
        s = pl.program_id(1)
        acc = jnp.dot(a_ref[...], w_ref[...], preferred_element_type=F32)

        @pl.when(s == 0)
        def _():
            base = r_ref[...]
            if bias is not None:
                base = base + refs[3][...]
            o_ref[...] = base + acc

        @pl.when(s > 0)
        def _():
            o_ref[...] += acc

    in_specs = [pl.BlockSpec((None, tm, k), lambda i, s: (s, i, 0)),
                pl.BlockSpec((None, k, N), lambda i, s: (s, 0, 0)),
                pl.BlockSpec((tm, N), lambda i, s: (i, 0))]
    args = [a, w, res]
    if bias is not None:
        in_specs.append(pl.BlockSpec((1, N), lambda i, s: (0, 0)))
        args.append(bias)
    (out,), xo = _call(body, name, (T // tm, S), in_specs, [pl.BlockSpec((tm, N), lambda i, s: (i, 0))],
                       [jax.ShapeDtypeStruct((T, N), F32)], args, ("parallel", "arbitrary"), hosted=hosted)
    return out, xo


_NT = (((1,), (1,)), ((), ()))
_TN = (((0,), (0,)), ((), ()))


def nt_rows(dy, w, name, hosted=()):
    T, N = dy.shape
    S, k, _ = w.shape
    tm = _tile(T, TOKEN_TILE)

    def body(dy_ref, w_ref, o_ref):
        o_ref[...] = lax.dot_general(dy_ref[...].astype(BF16), w_ref[...], _NT, preferred_element_type=F32)

    (out,), xo = _call(
        body, name, (T // tm, S),
        [pl.BlockSpec((tm, N), lambda i, s: (i, 0)), pl.BlockSpec((None, k, N), lambda i, s: (s, 0, 0))],
        [pl.BlockSpec((None, tm, k), lambda i, s: (s, i, 0))], [jax.ShapeDtypeStruct((S, T, k), F32)],
        [dy, w], ("parallel", "parallel"), hosted=hosted)
    return out, xo


def nt_cols_rms(dy, w, h, gain, dres, name, hosted=()):
    T, K = h.shape
    S, _, n = w.shape
    tm = _tile(T, TOKEN_TILE)

    def body(dy_ref, w_ref, h_ref, gain_ref, dres_ref, dh_ref, dgain_ref):
        i = pl.program_id(0)
        dn = None
        for s in range(S):
            part = lax.dot_general(dy_ref[:, s * n:(s + 1) * n], w_ref[s], _NT, preferred_element_type=F32)
            dn = part if dn is None else dn + part
        x = h_ref[...]
        r = lax.rsqrt(jnp.mean(x * x, axis=-1, keepdims=True) + RMS_EPS)
        xhat = x * r
        dxhat = dn * gain_ref[...]
        dh_ref[...] = dres_ref[...] + r * (dxhat - xhat * jnp.mean(dxhat * xhat, axis=-1, keepdims=True))
        pg = jnp.sum(dn * xhat, axis=0, keepdims=True)

        @pl.when(i == 0)
        def _():
            dgain_ref[...] = pg

        @pl.when(i > 0)
        def _():
            dgain_ref[...] += pg

    row = pl.BlockSpec((tm, K), lambda i: (i, 0))
    vec = pl.BlockSpec((1, K), lambda i: (0, 0))
    (dh, dgain), xo = _call(
        body, name, (T // tm,),
        [pl.BlockSpec((tm, S * n), lambda i: (i, 0)), pl.BlockSpec((S, K, n), lambda i: (0, 0, 0)), row, vec, row],
        [row, vec], [jax.ShapeDtypeStruct((T, K), F32), jax.ShapeDtypeStruct((1, K), F32)],
        [dy, w, h, gain, dres], ("arbitrary",), hosted=hosted)
    return dh, dgain, xo


def tn_grad(a, dy, S, a_by_seg, name, hosted=()):
    T = dy.shape[0] if dy.ndim == 2 else dy.shape[1]
    tt = _tile(T, GRAD_TOKEN_TILE)
    if a_by_seg:
        R = a.shape[1] // S if a.ndim == 2 else a.shape[2]
        C = dy.shape[1]
        a_spec = pl.BlockSpec((tt, R), lambda s, t: (t, s)) if a.ndim == 2 else pl.BlockSpec((None, tt, R), lambda s, t: (s, t, 0))
        b_spec = pl.BlockSpec((tt, C), lambda s, t: (t, 0))
    else:
        R = a.shape[1]
        C = dy.shape[1] // S if dy.ndim == 2 else dy.shape[2]
        a_spec = pl.BlockSpec((tt, R), lambda s, t: (t, 0))
        b_spec = pl.BlockSpec((tt, C), lambda s, t: (t, s)) if dy.ndim == 2 else pl.BlockSpec((None, tt, C), lambda s, t: (s, t, 0))
    Rh = R // 2
    nt = T // tt

    def body(a_ref, b_ref, o_ref, acc_ref):
        t = pl.program_id(1)
        part = lax.dot_general(a_ref[...], b_ref[...].astype(BF16), _TN, preferred_element_type=F32)

        @pl.when(t == 0)
        def _():
            acc_ref[...] = part

        @pl.when(t > 0)
        def _():
            acc_ref[...] += part

        @pl.when(t == nt - 1)
        def _():
            o_ref[0] = acc_ref[:Rh, :].astype(o_ref.dtype)
            o_ref[1] = acc_ref[Rh:, :].astype(o_ref.dtype)

    (gh,), xo = _call(
        body, name, (S, nt), [a_spec, b_spec], [pl.BlockSpec((2, None, Rh, C), lambda s, t: (0, s, 0, 0))],
        [jax.ShapeDtypeStruct((2, S, Rh, C), BF16)], [a, dy], ("parallel", "arbitrary"), [pltpu.VMEM((R, C), F32)],
        hosted=hosted)
    return gh, xo


def tn_grad_square(a, dy, S, name, hosted=()):
    T, K = a.shape
    N = dy.shape[1]
    tt = _tile(T, GRAD_TOKEN_TILE)
    nt = T // tt
    Rh = K // S // 2

    def body(a_ref, b_ref, o_ref, acc_ref):
        t = pl.program_id(0)
        part = lax.dot_general(a_ref[...], b_ref[...].astype(BF16), _TN, preferred_element_type=F32)

        @pl.when(t == 0)
        def _():
            acc_ref[...] = part

        @pl.when(t > 0)
        def _():
            acc_ref[...] += part

        @pl.when(t == nt - 1)
        def _():
            for s in range(S):
                for hf in range(2):
                    r0 = (2 * s + hf) * Rh
                    o_ref[hf, s] = acc_ref[r0:r0 + Rh, :].astype(o_ref.dtype)

    (gh,), xo = _call(
        body, name, (nt,), [pl.BlockSpec((tt, K), lambda t: (t, 0)), pl.BlockSpec((tt, N), lambda t: (t, 0))],
        [pl.BlockSpec((2, S, Rh, N), lambda t: (0, 0, 0, 0))], [jax.ShapeDtypeStruct((2, S, Rh, N), BF16)],
        [a, dy], ("arbitrary",), [pltpu.VMEM((K, N), F32)], hosted=hosted)
    return gh, xo


def _place():
    x, y, c = lax.axis_index("x"), lax.axis_index("y"), lax.axis_index("c")
    chips = [(1 - x, y), (x, 1 - y), (1 - x, 1 - y)]
    return x, y, c, chips


def _remote(src, dst, send_sem, recv_sem, dev):
    return pltpu.make_async_remote_copy(src_ref=src, dst_ref=dst, send_sem=send_sem, recv_sem=recv_sem,
                                        device_id=dev, device_id_type=MESH)


def small_allreduce(v, name, hosted=()):
    rows, W = v.shape

    def body(v_ref, o_ref, sib_ref, pair_ref, chips_ref, send_sems, recv_sems):
        x, y, c, chips = _place()
        me = 2 * x + y
        swap = _remote(v_ref, sib_ref, send_sems.at[3], recv_sems.at[3], (x, y, 1 - c))
        swap.start()
        swap.wait()
        mine, other = v_ref[...], sib_ref[...]
        pair_ref[...] = jnp.where(c == 0, mine, other) + jnp.where(c == 0, other, mine)
        sends = []
        for j, (px, py) in enumerate(chips):
            cp = _remote(pair_ref, chips_ref.at[me], send_sems.at[j], recv_sems.at[j], (px, py, c))
            cp.start()
            sends.append(cp)
        chips_ref[me] = pair_ref[...]
        for j, (px, py) in enumerate(chips):
            blk = chips_ref.at[2 * px + py]
            _remote(blk, blk, send_sems.at[j], recv_sems.at[j], (px, py, c)).wait_recv()
        for cp in sends:
            cp.wait_send()
        o_ref[...] = (chips_ref[0] + chips_ref[1]) + (chips_ref[2] + chips_ref[3])

    vm = pl.BlockSpec(memory_space=pltpu.VMEM)
    (out,), xo = _call(
        body, name, (), [vm], [vm], [jax.ShapeDtypeStruct((rows, W), F32)], [v], (),
        [pltpu.VMEM((rows, W), F32), pltpu.VMEM((rows, W), F32), pltpu.VMEM((N_CHIPS, rows, W), F32),
         pltpu.SemaphoreType.DMA((4,)), pltpu.SemaphoreType.DMA((4,))], hosted=hosted)
    return out, xo


def _gather_p1_copies(srcs, bufs, ssem, rsem, base):
    x, y, c, chips = _place()
    me, sib = 2 * x + y, (x, y, 1 - c)
    sends, recvs = [], []
    for k, (src, buf) in enumerate(zip(srcs, bufs)):
        rh = src.shape[0] // 2
        s0 = base + 4 * k
        sends.append(_remote(src, buf.at[me], ssem.at[s0 + 3], rsem.at[s0 + 3], sib))
        recvs.append(_remote(buf.at[me], buf.at[me], ssem.at[s0 + 3], rsem.at[s0 + 3], sib))
        for j, (px, py) in enumerate(chips):
            sends.append(_remote(src.at[pl.ds(c * rh, rh)], buf.at[me, pl.ds(c * rh, rh)], ssem.at[s0 + j], rsem.at[s0 + j], (px, py, c)))
            blk = buf.at[2 * px + py, pl.ds(c * rh, rh)]
            recvs.append(_remote(blk, blk, ssem.at[s0 + j], rsem.at[s0 + j], (px, py, c)))
    return sends, recvs


def _gather_p2_copies(bufs, ssem, rsem, base):
    x, y, c, chips = _place()
    sib = (x, y, 1 - c)
    sends, recvs = [], []
    for k, buf in enumerate(bufs):
        rh = buf.shape[1] // 2
        for j, (px, py) in enumerate(chips):
            s0 = base + 3 * k + j
            blk = buf.at[2 * px + py, pl.ds(c * rh, rh)]
            sends.append(_remote(blk, blk, ssem.at[s0], rsem.at[s0], sib))
            got = buf.at[2 * px + py, pl.ds((1 - c) * rh, rh)]
            recvs.append(_remote(got, got, ssem.at[s0], rsem.at[s0], sib))
    return sends, recvs


def _gathered_shape(s):
    return jax.ShapeDtypeStruct((N_CHIPS,) + s.shape, s.dtype)


def gather_p1(shards):
    return _Exchange(shards, [_gathered_shape(s) for s in shards], {}, 4 * len(shards),
                     lambda xi, xo, ss, rs: _gather_p1_copies(xi, xo, ss, rs, 0))


def gather_p2(bufs):
    return _Exchange(bufs, [jax.ShapeDtypeStruct(b.shape, b.dtype) for b in bufs], {k: k for k in range(len(bufs))},
                     3 * len(bufs), lambda xi, xo, ss, rs: _gather_p2_copies(xo, ss, rs, 0))


def gather_whole(whole, begun):
    nw, n = len(whole), len(whole) + len(begun)
    shards = list(whole) + list(begun)
    return _Exchange(shards, [_gathered_shape(s) for s in shards], {}, 4 * n + 3 * nw,
                     lambda xi, xo, ss, rs: _gather_p1_copies(xi, xo, ss, rs, 0),
                     then=lambda xi, xo, ss, rs: _gather_p2_copies(xo[:nw], ss, rs, 4 * n))


def gather_small(v):
    def copies(xi, xo, ssem, rsem):
        x, y, c, chips = _place()
        me, sib = 2 * x + y, (x, y, 1 - c)
        sends = [_remote(xi[0], xo[0].at[me], ssem.at[3], rsem.at[3], sib)]
        recvs = [_remote(xo[0].at[me], xo[0].at[me], ssem.at[3], rsem.at[3], sib)]
        for j, (px, py) in enumerate(chips):
            sends.append(_remote(xi[0], xo[0].at[me], ssem.at[j], rsem.at[j], (px, py, c)))
            blk = xo[0].at[2 * px + py]
            recvs.append(_remote(blk, blk, ssem.at[j], rsem.at[j], (px, py, c)))
        return sends, recvs

    return _Exchange([v], [_gathered_shape(v)], {}, 4, copies)


def run_exchanges(exchanges, name):
    return _call(lambda: None, name, (), [], [], [], [], (), hosted=exchanges)[1]


def sibling_halves(grads):
    def copies(xi, xo, ssem, rsem):
        x, y, c, _ = _place()
        sends = [_remote(xi[k].at[1 - c], xo[k], ssem.at[k], rsem.at[k], (x, y, 1 - c)) for k in range(len(grads))]
        return sends, sends

    return _Exchange(grads, [jax.ShapeDtypeStruct(g.shape[1:], g.dtype) for g in grads], {}, len(grads), copies)


def pair_sum(gh, recv, cidx, name):
    _, S, Rh, C = gh.shape

    def body(c_ref, a_ref, b_ref, o_ref):
        o_ref[...] = (a_ref[...].astype(F32) + b_ref[...].astype(F32)).astype(o_ref.dtype)

    return pl.pallas_call(
        body, name=name, out_shape=jax.ShapeDtypeStruct((S, Rh, C), BF16),
        grid_spec=pltpu.PrefetchScalarGridSpec(
            num_scalar_prefetch=1, grid=(S,),
            in_specs=[pl.BlockSpec((None, None, Rh, C), lambda s, c_ref: (c_ref[0], s, 0, 0)),
                      pl.BlockSpec((None, Rh, C), lambda s, c_ref: (s, 0, 0))],
            out_specs=pl.BlockSpec((None, Rh, C), lambda s, c_ref: (s, 0, 0))),
        compiler_params=_params(("parallel",)),
    )(cidx, gh, recv)


def scatter_p1(parts):
    def copies(xi, xo, ssem, rsem):
        x, y, c, chips = _place()
        me, sib = 2 * x + y, (x, y, 1 - c)
        sends, recvs = [], []
        for k in range(len(parts)):
            s0 = 4 * k
            sends.append(_remote(xi[k].at[me], xo[k].at[me, c], ssem.at[s0 + 3], rsem.at[s0 + 3], sib))
            own = xo[k].at[me, 1 - c]
            recvs.append(_remote(own, own, ssem.at[s0 + 3], rsem.at[s0 + 3], sib))
            for j, (px, py) in enumerate(chips):
                sends.append(_remote(xi[k].at[2 * px + py], xo[k].at[me, c], ssem.at[s0 + j], rsem.at[s0 + j], (px, py, c)))
                blk = xo[k].at[2 * px + py, c]
                recvs.append(_remote(blk, blk, ssem.at[s0 + j], rsem.at[s0 + j], (px, py, c)))
        return sends, recvs

    return _Exchange(parts, [jax.ShapeDtypeStruct((p.shape[0], 2) + p.shape[1:], p.dtype) for p in parts], {},
                     4 * len(parts), copies)


def scatter_p2(bufs):
    def copies(xi, xo, ssem, rsem):
        x, y, c, chips = _place()
        sib = (x, y, 1 - c)
        sends, recvs = [], []
        for k in range(len(bufs)):
            for j, (px, py) in enumerate(chips):
                s0 = 3 * k + j
                blk = xo[k].at[2 * px + py, c]
                sends.append(_remote(blk, blk, ssem.at[s0], rsem.at[s0], sib))
                got = xo[k].at[2 * px + py, 1 - c]
                recvs.append(_remote(got, got, ssem.at[s0], rsem.at[s0], sib))
        return sends, recvs

    return _Exchange(bufs, [jax.ShapeDtypeStruct(b.shape, b.dtype) for b in bufs], {k: k for k in range(len(bufs))},
                     3 * len(bufs), copies)


def _adamw_math(w, g, m, v):
    m = ADAM_B1 * m + (1.0 - ADAM_B1) * g
    v = ADAM_B2 * v + (1.0 - ADAM_B2) * (g * g)
    m_hat = m / (1.0 - ADAM_B1 ** ADAM_STEP)
    v_hat = v / (1.0 - ADAM_B2 ** ADAM_STEP)
    delta = -ADAM_LR * (m_hat / (jnp.sqrt(v_hat) + ADAM_EPS) + ADAM_WD * w)
    return delta, m, v


def adamw_reduce(w, m, v, buf, part, place, lyr, bases, name, hosted=()):
    L, R, C = w.shape
    Rh = R // 2
    rb = _tile(Rh, ROW_TILE, 2 * SUBLANES)
    nb = Rh // rb

    def body(place_ref, p_ref, b0, b1, b2, b3, w_ref, m_ref, v_ref, *rest):
        go_ref, d_ref, mo_ref, vo_ref = rest[-4:]
        mine = (place_ref[1] == pl.program_id(0))
        g = None
        for p, b in enumerate((b0, b1, b2, b3)):
            val = jnp.where(mine & (place_ref[0] == p), p_ref[...], b[...]).astype(F32)
            g = val if g is None else g + val
        d, mn, vn = _adamw_math(w_ref[...], g, m_ref[...], v_ref[...])
        go_ref[...] = g
        d_ref[...] = d
        mo_ref[...] = mn
        vo_ref[...] = vn

    def buf_spec(p):
        def idx(h, i, pr):
            own = (pr[0] == p) & (pr[1] == h)
            return (p, jnp.where(own, 1 - h, h), i, 0)
        return pl.BlockSpec((None, None, rb, C), idx)

    blk = pl.BlockSpec((None, rb, C), lambda h, i, pr: (lyr, h * nb + i, 0))
    in_specs = [pl.BlockSpec((None, rb, C), lambda h, i, pr: (pr[0], i, 0))] + [buf_spec(p) for p in range(N_CHIPS)] + [blk] * 3
    args = [part, buf, buf, buf, buf, w, m, v]
    aliases = {}
    if bases is not None:
        in_specs += [pl.BlockSpec(memory_space=pl.ANY)] * 4
        aliases = {len(args) + k: k for k in range(4)}
        args += list(bases)
    shp = jax.ShapeDtypeStruct((L, R, C), F32)
    return _call(body, name, (2, nb), in_specs, [blk] * 4, [shp] * 4, args, ("parallel", "parallel"),
                 hosted=hosted, prefetch=[place], own_aliases=aliases)


def small_update(gall, chip, entries, name):
    ne = len(entries)
    D = gall.shape[1]

    def body(chip_ref, gall_ref, *refs):
        ins, outs = refs[:3 * ne], refs[3 * ne:]
        ch = chip_ref[0]
        for e, (row0, kind, w, _, _) in enumerate(entries):
            r, width = w.shape

            def gsum(rs, cs):
                return gall_ref[rs, cs]

            if kind == "full":
                g = gsum(slice(row0, row0 + r), slice(0, D))
            elif kind == "cols":
                g = gsum(slice(row0, row0 + r), slice(0, width))
                for q in range(1, N_CHIPS):
                    g = jnp.where(ch == q, gsum(slice(row0, row0 + r), slice(q * width, (q + 1) * width)), g)
            else:
                per_row = D // width
                g = gsum(slice(row0, row0 + 1), slice(0, width))
                for q in range(1, N_CHIPS):
                    rr = row0 + q // per_row
                    cc = (q % per_row) * width
                    g = jnp.where(ch == q, gsum(slice(rr, rr + 1), slice(cc, cc + width)), g)
            d, mn, vn = _adamw_math(ins[3 * e][...], g, ins[3 * e + 1][...], ins[3 * e + 2][...])
            outs[4 * e][...] = g
            outs[4 * e + 1][...] = d
            outs[4 * e + 2][...] = mn
            outs[4 * e + 3][...] = vn

    vm = pl.BlockSpec(memory_space=pltpu.VMEM)
    args, out_shape = [], []
    for _, _, w, m, v in entries:
        args += [w, m, v]
        out_shape += [jax.ShapeDtypeStruct(w.shape, F32)] * 4
    return pl.pallas_call(
        body, name=name,
        in_specs=[pl.BlockSpec(memory_space=pltpu.SMEM), vm] + [vm] * (3 * ne),
        out_specs=[vm] * (4 * ne), out_shape=out_shape,
        compiler_params=pltpu.CompilerParams(vmem_limit_bytes=VMEM_LIMIT),
    )(chip, gall, *args)


def _pack_rows(items, width, name):
    starts, at = [], 0
    for it in items:
        starts.append(at)
        at += -(-it.shape[0] // SUBLANES) * SUBLANES
    total = at

    def body(*refs):
        o_ref = refs[-1]
        o_ref[...] = jnp.zeros_like(o_ref)
        for it_ref, r0 in zip(refs[:-1], starts):
            o_ref[r0:r0 + it_ref.shape[0], :] = it_ref[...]

    vm = pl.BlockSpec(memory_space=pltpu.VMEM)
    packed = pl.pallas_call(body, name=name, in_specs=[vm] * len(items), out_specs=vm,
                            out_shape=jax.ShapeDtypeStruct((total, width), F32))(*items)
    return packed, starts


def kernel(x, a_norm, a_w_in, a_conv, a_w_out, b_norm, b_w_pw1, b_b_pw1, b_conv, b_b_conv, b_ln_g, b_ln_b, b_w_pw2, b_b_pw2, ffn_norm, ffn_w_gate, ffn_w_up, ffn_w_down, final_norm, loss_target, m_a_norm, m_a_w_in, m_a_conv, m_a_w_out, m_b_norm, m_b_w_pw1, m_b_b_pw1, m_b_conv, m_b_b_conv, m_b_ln_g, m_b_ln_b, m_b_w_pw2, m_b_b_pw2, m_ffn_norm, m_ffn_w_gate, m_ffn_w_up, m_ffn_w_down, m_final_norm, v_a_norm, v_a_w_in, v_a_conv, v_a_w_out, v_b_norm, v_b_w_pw1, v_b_b_pw1, v_b_conv, v_b_b_conv, v_b_ln_g, v_b_ln_b, v_b_w_pw2, v_b_b_pw2, v_ffn_norm, v_ffn_w_gate, v_ffn_w_up, v_ffn_w_down, v_final_norm):
    T, D = x.shape[1], x.shape[2]
    Dq = D // N_CHIPS
    cx, cy, cc = lax.axis_index("x"), lax.axis_index("y"), lax.axis_index("c")
    chip = (2 * cx + cy).astype(jnp.int32).reshape(1)
    cidx = cc.astype(jnp.int32).reshape(1)
    h0 = x.reshape(T, D)
    tgt = loss_target.reshape(T, D)

    small_shards = [a_conv[0], b_norm, b_b_pw1.reshape(2, Dq), b_conv[0], b_b_conv, b_ln_g, b_ln_b, b_b_pw2]
    packed, st = _pack_rows(small_shards, Dq, "pack_small")

    tr = lambda t: jnp.swapaxes(t, 1, 2)
    w_gate, m_gate, v_gate = tr(ffn_w_gate), tr(m_ffn_w_gate), tr(v_ffn_w_gate)
    w_up, m_up, v_up = tr(ffn_w_up), tr(m_ffn_w_up), tr(v_ffn_w_up)
    bf = lambda t: t.astype(BF16)
    s_in, s_out, s_pw1, s_pw2 = bf(a_w_in[0]), bf(a_w_out[0]), bf(b_w_pw1[0]), bf(b_w_pw2[0])
    s_gate, s_up, s_down = [bf(w_gate[l]) for l in (0, 1)], [bf(w_up[l]) for l in (0, 1)], [bf(ffn_w_down[l]) for l in (0, 1)]

    n0, (g_in,) = rms_fwd(h0, a_norm, "rms_a", hosted=[gather_whole([s_in], [])])
    bcv, (g_out, gate0, sw) = mm_cols(n0, g_in, None, "mm_w_in", hosted=[gather_p1([s_out, s_gate[0]]), gather_small(packed)])

    def whole(k, r):
        return jnp.transpose(sw[:, st[k]:st[k] + r, :], (1, 0, 2)).reshape(r, D)

    a_conv_f, b_norm_f = whole(0, 3), whole(1, 1)
    b_b_pw1_f = sw[:, st[2]:st[2] + 2, :].reshape(1, 2 * D)
    b_conv_f, b_b_conv_f, b_ln_g_f, b_ln_b_f, b_b_pw2_f = whole(3, b_conv.shape[1]), whole(4, 1), whole(5, 1), whole(6, 1), whole(7, 1)
    ya, (up0, g_out, gate0) = gateconv_fwd(bcv, a_conv_f, "gateconv_fwd",
                                           hosted=[gather_p1([s_up[0]]), gather_p2([g_out, gate0])])
    g_out = g_out.reshape(1, D, D)
    h1, (down0, up0) = mm_rows(ya[None], g_out, h0, None, "mm_w_out", hosted=[gather_p1([s_down[0]]), gather_p2([up0])])
    n1, fg0, fu0, gu0, h2, (down0, *later) = ffn_fwd(h1, ffn_norm[0:1], gate0, up0, gather_p2([down0]).awaited_first(), "ffn_fwd0",
                                                     hosted=[gather_p1([s_pw1, s_pw2, s_gate[1], s_up[1]])])
    n2, (g_pw1, g_pw2, gate1, up1) = rms_fwd(h2, b_norm_f, "rms_b", hosted=[gather_p2(later)])
    g_pw2 = g_pw2.reshape(1, D, D)
    ub, (down1,) = mm_cols(n2, g_pw1, b_b_pw1_f, "mm_pw1", hosted=[gather_p1([s_down[1]])])
    cu, sb, (down1,) = bconv_fwd(ub, b_conv_f, b_b_conv_f, b_ln_g_f, b_ln_b_f, "bconv_fwd", hosted=[gather_p2([down1])])
    h3, _ = mm_rows(sb[None], g_pw2, h2, b_b_pw2_f, "mm_pw2")
    n3, fg1, fu1, gu1, h4, _ = ffn_fwd(h3, ffn_norm[1:2], gate1, up1, down1, "ffn_fwd1")
    loss_part, dh4, d_final = loss_head(h4, final_norm.reshape(1, D), tgt, "loss_head")

    place = jnp.concatenate([chip, cidx])

    def pair_sums(ghs, from_sib, tags):
        return [pair_sum(g, r, cidx, "pair_sum_" + t) for g, r, t in zip(ghs, from_sib, tags)]

    def upd(w, m, v, bufs, parts, tag, hosted=()):
        res, xo = None, []
        for lyr, (b, p) in enumerate(zip(bufs, parts)):
            res, xo_l = adamw_reduce(w, m, v, b, p, place, lyr, res, "adamw_%s%d" % (tag, lyr), hosted=hosted if lyr == 0 else ())
            xo += xo_l
        return res, xo

    dg1, du1, dh3, d_fn1, _ = ffn_bwd(dh4, h3, ffn_norm[1:2], fg1, fu1, down1, gate1, up1, "ffn_bwd1")
    gh_down1, _ = tn_grad(gu1, dh4, N_CHIPS, True, "tn_down1")
    gh_gate1, _ = tn_grad(dg1, n3, N_CHIPS, True, "tn_gate1")
    gh_up1, _ = tn_grad(du1, n3, N_CHIPS, True, "tn_up1")
    f1 = [gh_gate1, gh_up1, gh_down1]

    dcu, d_ln_g, d_ln_b, d_b_conv, d_b_pw2, sib_f1 = pw2_ln_bwd(dh3, g_pw2, cu, b_ln_g_f, b_ln_b_f, "pw2_ln_bwd",
                                                                hosted=[sibling_halves(f1)])
    p_f1 = pair_sums(f1, sib_f1, ["gate1", "up1", "down1"])
    gh_pw2, _ = tn_grad_square(sb, dh3, N_CHIPS, "tn_pw2")
    dub, d_bconv_w, d_b_pw1, buf_f1 = bconv_bwd(dcu, ub, b_conv_f, "bconv_bwd", hosted=[scatter_p1(p_f1)])
    gh_pw1, _ = tn_grad(n2, dub, N_CHIPS, False, "tn_pw1")
    b_grp = [gh_pw1, gh_pw2]
    dh2, d_b_norm, (*buf_f1, sib_pw1, sib_pw2) = nt_cols_rms(dub, g_pw1, h2, b_norm_f, dh3, "nt_pw1",
                                                             hosted=[scatter_p2(buf_f1), sibling_halves(b_grp)])
    sib_b = [sib_pw1, sib_pw2]
    p_b = pair_sums(b_grp, sib_b, ["pw1", "pw2"])

    dg0, du0, dh1, d_fn0, buf_b = ffn_bwd(dh2, h1, ffn_norm[0:1], fg0, fu0, down0, gate0, up0, "ffn_bwd0", hosted=[scatter_p1(p_b)])
    gh_down0, _ = tn_grad(gu0, dh2, N_CHIPS, True, "tn_down0")
    gh_gate0, (*buf_b, sib_down0) = tn_grad(dg0, n1, N_CHIPS, True, "tn_gate0",
                                            hosted=[scatter_p2(buf_b), sibling_halves([gh_down0])])
    p_down0 = pair_sums([gh_down0], [sib_down0], ["down0"])
    gh_up0, (buf_down0, sib_gate0) = tn_grad(du0, n1, N_CHIPS, True, "tn_up0",
                                             hosted=[scatter_p1(p_down0), sibling_halves([gh_gate0])])
    p_gate0 = pair_sums([gh_gate0], [sib_gate0], ["gate0"])
    dya, (buf_down0, sib_up0) = nt_rows(dh1, g_out, "nt_w_out",
                                        hosted=[scatter_p2([buf_down0]), sibling_halves([gh_up0])])
    p_up0 = pair_sums([gh_up0], [sib_up0], ["up0"])
    gh_out, _ = tn_grad_square(ya, dh1, N_CHIPS, "tn_w_out")
    dbcv, d_aconv_w, (buf_gate0, sib_out) = gateconv_bwd(dya[0], bcv, a_conv_f, "gateconv_bwd",
                                                         hosted=[scatter_p1(p_gate0), sibling_halves([gh_out])])
    p_out = pair_sums([gh_out], [sib_out], ["out"])
    gh_in, (buf_up0, buf_gate0) = tn_grad(n0, dbcv, N_CHIPS, False, "tn_w_in",
                                          hosted=[scatter_p1(p_up0), scatter_p2([buf_gate0])])
    sib_in = run_exchanges([sibling_halves([gh_in])], "reduce_in_siblings")
    p_in = pair_sums([gh_in], sib_in, ["in"])
    grad_x, d_a_norm, (buf_in, buf_out, buf_up0) = nt_cols_rms(
        dbcv, g_in, h0, a_norm, dh1, "nt_w_in", hosted=[scatter_p1(p_in + p_out), scatter_p2([buf_up0])])
    p_f0 = [p_gate0[0], p_up0[0], p_down0[0]]

    d_ffn_norm = jnp.concatenate([d_fn0, d_fn1], axis=0)
    small_grads = [d_a_norm, d_aconv_w, d_b_norm, d_b_pw1.reshape(2, D), d_bconv_w, d_b_conv, d_ln_g, d_ln_b, d_b_pw2,
                   d_ffn_norm, d_final, jnp.broadcast_to(loss_part, (1, D))]
    gpacked, gs = _pack_rows(small_grads, D, "pack_small_grads")
    gall, (buf_in, buf_out) = small_allreduce(gpacked, "allreduce_small_grads", hosted=[scatter_p2([buf_in, buf_out])])
    buf_a, p_a = [buf_in, buf_out], [p_in[0], p_out[0]]

    r_gate, _ = upd(w_gate, m_gate, v_gate, [buf_gate0, buf_f1[0]], [p_f0[0], p_f1[0]], "gate")
    r_up, _ = upd(w_up, m_up, v_up, [buf_up0, buf_f1[1]], [p_f0[1], p_f1[1]], "up")
    r_down, _ = upd(ffn_w_down, m_ffn_w_down, v_ffn_w_down, [buf_down0, buf_f1[2]], [p_f0[2], p_f1[2]], "down")
    r_gate, r_up = [tr(t) for t in r_gate], [tr(t) for t in r_up]
    r_pw1, _ = upd(b_w_pw1, m_b_w_pw1, v_b_w_pw1, [buf_b[0]], [p_b[0]], "pw1")
    r_pw2, _ = upd(b_w_pw2, m_b_w_pw2, v_b_w_pw2, [buf_b[1]], [p_b[1]], "pw2")
    r_in, _ = upd(a_w_in, m_a_w_in, v_a_w_in, [buf_a[0]], [p_a[0]], "w_in")
    r_out, _ = upd(a_w_out, m_a_w_out, v_a_w_out, [buf_a[1]], [p_a[1]], "w_out")
    entries = [
        (gs[0], "full", a_norm, m_a_norm, v_a_norm),
        (gs[1], "cols", a_conv[0], m_a_conv[0], v_a_conv[0]),
        (gs[2], "cols", b_norm, m_b_norm, v_b_norm),
        (gs[3], "flat2", b_b_pw1, m_b_b_pw1, v_b_b_pw1),
        (gs[4], "cols", b_conv[0], m_b_conv[0], v_b_conv[0]),
        (gs[5], "cols", b_b_conv, m_b_b_conv, v_b_b_conv),
        (gs[6], "cols", b_ln_g, m_b_ln_g, v_b_ln_g),
        (gs[7], "cols", b_ln_b, m_b_ln_b, v_b_ln_b),
        (gs[8], "cols", b_b_pw2, m_b_b_pw2, v_b_b_pw2),
        (gs[9], "full", ffn_norm, m_ffn_norm, v_ffn_norm),
        (gs[10], "full", final_norm.reshape(1, D), m_final_norm.reshape(1, D), v_final_norm.reshape(1, D)),
    ]
    so = small_update(gall, chip, entries, "small_update")
    sm = [so[4 * e:4 * e + 4] for e in range(len(entries))]

    def shaped(e, like):
        return [t.reshape(like.shape) for t in sm[e]]

    r_a_norm, r_a_conv, r_b_norm, r_b_b_pw1 = shaped(0, a_norm), shaped(1, a_conv), shaped(2, b_norm), shaped(3, b_b_pw1)
    r_b_conv, r_b_b_conv, r_b_ln_g, r_b_ln_b = shaped(4, b_conv), shaped(5, b_b_conv), shaped(6, b_ln_g), shaped(7, b_ln_b)
    r_b_b_pw2, r_ffn_norm, r_final = shaped(8, b_b_pw2), shaped(9, ffn_norm), shaped(10, final_norm)

    loss = gall[gs[11], 0]
    order =[r_a_norm, r_in, r_a_conv, r_out, r_b_norm, r_pw1, r_b_b_pw1, r_b_conv, r_b_b_conv, r_b_ln_g, r_b_ln_b,
             r_pw2, r_b_b_pw2, r_ffn_norm, r_gate, r_up, r_down, r_final]
    outs = [loss, grad_x.reshape(x.shape)]
    for field in range(4):
        outs += [r[field] for r in order]
    return tuple(outs)
```

```python
import functools

import jax
import jax.numpy as jnp
from jax import lax
from jax.experimental import pallas as pl
from jax.experimental.pallas import tpu as pltpu

RMS_EPS = 1e-6
LN_EPS = 1e-5
ADAM_LR = 0.001
ADAM_B1 = 0.9
ADAM_B2 = 0.999
ADAM_EPS = 1e-08
ADAM_WD = 0.01
ADAM_STEP = 10

N_CHIPS = 4
N_DEV = 8
LANES = 128
SUBLANES = 8
HALO = 32
CONV_ROWS = 64
TOKEN_TILE = 512
WIDE_TOKEN_TILE = 1024
GRAD_TOKEN_TILE = 2048
FFN_ROW_CHUNKS = 2
FFN_SEGS_PER_STEP = 2
ROW_TILE = 256
VMEM_LIMIT = 56 * 1024 * 1024
MESH = pl.DeviceIdType.MESH
BF16 = jnp.bfloat16
F32 = jnp.float32


def _tile(n, pref, mult=SUBLANES):
    t = min(n, pref) // mult * mult
    while n % t:
        t -= mult
    return t


def _params(sem):
    return pltpu.CompilerParams(dimension_semantics=sem, vmem_limit_bytes=VMEM_LIMIT)


def _sigmoid(x):
    return 0.5 * jnp.tanh(0.5 * x) + 0.5


class _Exchange:
    def __init__(self, ins, outs, aliases, n_sems, copies, then=None):
        self.ins, self.outs, self.aliases, self.n_sems, self.copies = list(ins), list(outs), dict(aliases), n_sems, copies
        self.then = then
        self.early = False

    def awaited_first(self):
        self.early = True
        return self

    def start(self, xi, xo, ssem, rsem):
        for cp in self.copies(xi, xo, ssem, rsem)[0]:
            cp.start()

    def finish(self, xi, xo, ssem, rsem):
        sends, recvs = self.copies(xi, xo, ssem, rsem)
        for cp in recvs:
            cp.wait_recv()
        if self.then is not None:
            sends2, recvs2 = self.then(xi, xo, ssem, rsem)
            for cp in sends2:
                cp.start()
            for cp in recvs2:
                cp.wait_recv()
            sends = sends + sends2
        for cp in sends:
            cp.wait_send()


def _call(body, name, grid, in_specs, out_specs, out_shape, args, sem, scratch_shapes=(), hosted=(), prefetch=(),
          own_aliases=None):
    in_specs, out_specs, out_shape = list(in_specs), list(out_specs), list(out_shape)
    scratch_shapes, hosted, prefetch = list(scratch_shapes), list(hosted), list(prefetch)
    n_pre, n_in, n_out, n_scr = len(prefetch), len(args), len(out_shape), len(scratch_shapes)
    x_in = [a for ex in hosted for a in ex.ins]
    x_out = [o for ex in hosted for o in ex.outs]
    aliases = {n_pre + i: o for i, o in (own_aliases or {}).items()}
    at_in, at_out = n_pre + n_in, n_out
    for ex in hosted:
        for i, o in ex.aliases.items():
            aliases[at_in + i] = at_out + o
        at_in += len(ex.ins)
        at_out += len(ex.outs)
    sems = [pltpu.SemaphoreType.DMA((ex.n_sems,)) for ex in hosted for _ in range(2)]

    def wrapped(*refs):
        pre, refs = refs[:n_pre], refs[n_pre:]
        ins, xi = refs[:n_in], refs[n_in:n_in + len(x_in)]
        refs = refs[n_in + len(x_in):]
        outs, xo = refs[:n_out], refs[n_out:n_out + len(x_out)]
        refs = refs[n_out + len(x_out):]
        scr, sm = refs[:n_scr], refs[n_scr:]
        views, a, b = [], 0, 0
        for e, ex in enumerate(hosted):
            views.append((xi[a:a + len(ex.ins)], xo[b:b + len(ex.outs)], sm[2 * e], sm[2 * e + 1]))
            a += len(ex.ins)
            b += len(ex.outs)
        first = last = None
        for ax, g in enumerate(grid):
            f, l = pl.program_id(ax) == 0, pl.program_id(ax) == g - 1
            first, last = (f, l) if first is None else (first & f, last & l)

        def begin():
            for ex, v in zip(hosted, views):
                ex.start(*v)
            for ex, v in zip(hosted, views):
                if ex.early:
                    ex.finish(*v)

        def end():
            for ex, v in zip(hosted, views):
                if not ex.early:
                    ex.finish(*v)

        if hosted and grid:
            pl.when(first)(begin)
        elif hosted:
            begin()
        early_refs = [r for ex, v in zip(hosted, views) if ex.early for r in v[1]]
        body(*pre, *ins, *outs, *scr, *early_refs)
        if hosted and grid:
            pl.when(last)(end)
        elif hosted:
            end()

    hbm = pl.BlockSpec(memory_space=pl.ANY)
    all_in, all_out = in_specs + [hbm] * len(x_in), out_specs + [hbm] * len(x_out)
    kw = dict(name=name, out_shape=out_shape + x_out, input_output_aliases=aliases,
              compiler_params=_params(tuple("arbitrary" for _ in grid) if hosted else sem))
    if prefetch:
        kw["grid_spec"] = pltpu.PrefetchScalarGridSpec(num_scalar_prefetch=n_pre, grid=grid, in_specs=all_in,
                                                       out_specs=all_out, scratch_shapes=scratch_shapes + sems)
    else:
        kw.update(grid=grid, in_specs=all_in, out_specs=all_out, scratch_shapes=scratch_shapes + sems)
    res = pl.pallas_call(wrapped, **kw)(*prefetch, *args, *x_in)
    return list(res[:n_out]), list(res[n_out:])


def rms_fwd(h, gain, name, hosted=()):
    T, D = h.shape
    tm = _tile(T, TOKEN_TILE)

    def body(h_ref, g_ref, o_ref):
        x = h_ref[...]
        r = lax.rsqrt(jnp.mean(x * x, axis=-1, keepdims=True) + RMS_EPS)
        o_ref[...] = (x * r * g_ref[...]).astype(o_ref.dtype)

    (n,), xo = _call(
        body, name, (T // tm,),
        [pl.BlockSpec((tm, D), lambda i: (i, 0)), pl.BlockSpec((1, D), lambda i: (0, 0))],
        [pl.BlockSpec((tm, D), lambda i: (i, 0))], [jax.ShapeDtypeStruct((T, D), BF16)],
        [h, gain], ("parallel",), hosted=hosted)
    return n, xo


def loss_head(h, gain, tgt, name):
    T, D = h.shape
    tm = _tile(T, TOKEN_TILE)

    def body(h_ref, g_ref, t_ref, loss_ref, dh_ref, dg_ref):
        i = pl.program_id(0)
        x = h_ref[...]
        g = g_ref[...]
        r = lax.rsqrt(jnp.mean(x * x, axis=-1, keepdims=True) + RMS_EPS)
        xhat = x * r
        diff = xhat * g - t_ref[...]
        part_loss = 0.5 * jnp.sum(jnp.mean(diff * diff, axis=-1, keepdims=True), axis=0, keepdims=True)
        dy = diff * (1.0 / D)
        dxhat = dy * g
        dh_ref[...] = r * (dxhat - xhat * jnp.mean(dxhat * xhat, axis=-1, keepdims=True))
        part = jnp.sum(dy * xhat, axis=0, keepdims=True)

        @pl.when(i == 0)
        def _():
            dg_ref[...] = part
            loss_ref[...] = part_loss

        @pl.when(i > 0)
        def _():
            dg_ref[...] += part
            loss_ref[...] += part_loss

    row = pl.BlockSpec((tm, D), lambda i: (i, 0))
    vec = pl.BlockSpec((1, D), lambda i: (0, 0))
    return pl.pallas_call(
        body, name=name, grid=(T // tm,),
        in_specs=[row, vec, row],
        out_specs=[pl.BlockSpec((1, 1), lambda i: (0, 0)), row, vec],
        out_shape=[jax.ShapeDtypeStruct((1, 1), F32), jax.ShapeDtypeStruct((T, D), F32),
                   jax.ShapeDtypeStruct((1, D), F32)],
        compiler_params=_params(("arbitrary",)),
    )(h, gain, tgt)


def _prev_halo_spec(tm, width):
    return pl.BlockSpec((HALO, width), lambda i: (jnp.maximum(i * (tm // HALO) - 1, 0), 0))


def _next_halo_spec(tm, width, T):
    return pl.BlockSpec((HALO, width), lambda i: (jnp.minimum((i + 1) * (tm // HALO), T // HALO - 1), 0))


def _shifted(win, off, rows):
    if off % SUBLANES == 0:
        return win[off:off + rows]
    n = win.shape[0]
    return pltpu.roll(win, (n - off) % n, 0)[:rows]


def _rowsum8(x):
    acc = x[0:SUBLANES]
    for q in range(1, x.shape[0] // SUBLANES):
        acc = acc + x[q * SUBLANES:(q + 1) * SUBLANES]
    return acc


def _conv_loops(tm, D, per_block):
    def chunk(r, carry):
        t0 = pl.multiple_of(r * CONV_ROWS, CONV_ROWS)
        for lb in range(D // LANES):
            per_block(t0, slice(lb * LANES, (lb + 1) * LANES))
        return carry

    lax.fori_loop(0, tm // CONV_ROWS, chunk, 0)


def gateconv_fwd(bcv, w, w_out, res, name, hosted=()):
    T, D3 = bcv.shape
    D = D3 // 3
    K = w.shape[0]
    tm = _tile(T, TOKEN_TILE)
    wo_shape = w_out.outs[0].shape

    def body(x_ref, halo_ref, w_ref, res_ref, y_ref, h_ref, pad_ref, wo_v, sem, wo_hbm):
        i = pl.program_id(0)

        @pl.when(i == 0)
        def _():
            cp = pltpu.make_async_copy(wo_hbm, wo_v, sem)
            cp.start()
            cp.wait()

        pad_ref[HALO:, :] = x_ref[:, D:2 * D] * x_ref[:, 2 * D:]
        pad_ref[:HALO, :] = jnp.where(i > 0, halo_ref[:, D:2 * D] * halo_ref[:, 2 * D:], 0.0)

        def block(t0, ls):
            win = pad_ref[pl.ds(t0, CONV_ROWS + HALO), ls]
            acc = jnp.zeros((CONV_ROWS, LANES), F32)
            for k in range(K):
                acc = acc + w_ref[k:k + 1, ls] * _shifted(win, HALO - (K - 1) + k, CONV_ROWS)
            y_ref[pl.ds(t0, CONV_ROWS), ls] = (x_ref[pl.ds(t0, CONV_ROWS), ls] * acc).astype(y_ref.dtype)

        _conv_loops(tm, D, block)
        h_ref[...] = res_ref[...] + jnp.dot(y_ref[...], wo_v[...].reshape(D, D), preferred_element_type=F32)

    row = pl.BlockSpec((tm, D), lambda i: (i, 0))
    (y, h), xo = _call(
        body, name, (T // tm,),
        [pl.BlockSpec((tm, D3), lambda i: (i, 0)), _prev_halo_spec(tm, D3), pl.BlockSpec((K, D), lambda i: (0, 0)), row],
        [row, row], [jax.ShapeDtypeStruct((T, D), BF16), jax.ShapeDtypeStruct((T, D), F32)],
        [bcv, bcv, w, res], ("arbitrary",),
        [pltpu.VMEM((tm + HALO, D), F32), pltpu.VMEM(wo_shape, BF16), pltpu.SemaphoreType.DMA],
        hosted=[w_out.awaited_first()] + list(hosted))
    return y, h, xo


def gateconv_bwd(dy, bcv, w, name, hosted=()):
    T, D3 = bcv.shape
    D = D3 // 3
    K = w.shape[0]
    tm = _tile(T, TOKEN_TILE)
    nt = T // tm

    def body(dy_ref, dyn_ref, x_ref, xp_ref, xn_ref, w_ref, o_ref, dw_ref, cv_ref, dc_ref, wacc_ref):
        i = pl.program_id(0)
        cv_ref[HALO:, :] = x_ref[:, D:2 * D] * x_ref[:, 2 * D:]
        cv_ref[:HALO, :] = jnp.where(i > 0, xp_ref[:, D:2 * D] * xp_ref[:, 2 * D:], 0.0)
        dc_ref[:tm, :] = dy_ref[...] * x_ref[:, :D]
        dc_ref[tm:, :] = jnp.where(i < nt - 1, dyn_ref[...] * xn_ref[:, :D], 0.0)

        @pl.when(i == 0)
        def _():
            wacc_ref[...] = jnp.zeros_like(wacc_ref)

        def block(t0, ls):
            cwin = cv_ref[pl.ds(t0, CONV_ROWS + HALO), ls]
            dwin = dc_ref[pl.ds(t0, CONV_ROWS + HALO), ls]
            dcon = dwin[:CONV_ROWS]
            conv = jnp.zeros((CONV_ROWS, LANES), F32)
            dcv = jnp.zeros((CONV_ROWS, LANES), F32)
            for k in range(K):
                wk = w_ref[k:k + 1, ls]
                cs = _shifted(cwin, HALO - (K - 1) + k, CONV_ROWS)
                conv = conv + wk * cs
                dcv = dcv + wk * _shifted(dwin, (K - 1) - k, CONV_ROWS)
                wacc_ref[k * SUBLANES:(k + 1) * SUBLANES, ls] += _rowsum8(dcon * cs)
            rows = pl.ds(t0, CONV_ROWS)
            o_ref[rows, ls] = (dy_ref[rows, ls] * conv).astype(o_ref.dtype)
            o_ref[rows, pl.ds(D + ls.start, LANES)] = (dcv * x_ref[rows, pl.ds(2 * D + ls.start, LANES)]).astype(o_ref.dtype)
            o_ref[rows, pl.ds(2 * D + ls.start, LANES)] = (dcv * x_ref[rows, pl.ds(D + ls.start, LANES)]).astype(o_ref.dtype)

        _conv_loops(tm, D, block)

        @pl.when(i == nt - 1)
        def _():
            for k in range(K):
                dw_ref[k:k + 1, :] = jnp.sum(wacc_ref[k * SUBLANES:(k + 1) * SUBLANES, :], axis=0, keepdims=True)

    (dx, dw), xo = _call(
        body, name, (nt,),
        [pl.BlockSpec((tm, D), lambda i: (i, 0)), _next_halo_spec(tm, D, T),
         pl.BlockSpec((tm, D3), lambda i: (i, 0)), _prev_halo_spec(tm, D3), _next_halo_spec(tm, D3, T),
         pl.BlockSpec((K, D), lambda i: (0, 0))],
        [pl.BlockSpec((tm, D3), lambda i: (i, 0)), pl.BlockSpec((K, D), lambda i: (0, 0))],
        [jax.ShapeDtypeStruct((T, D3), BF16), jax.ShapeDtypeStruct((K, D), F32)],
        [dy, dy, bcv, bcv, bcv, w], ("arbitrary",),
        [pltpu.VMEM((tm + HALO, D), F32), pltpu.VMEM((tm + HALO, D), F32), pltpu.VMEM((K * SUBLANES, D), F32)],
        hosted=hosted)
    return dx, dw, xo


def bconv_fwd(u, w, b_conv, ln_g, ln_b, w_out, b_out, res, name, hosted=()):
    T, D2 = u.shape
    D = D2 // 2
    K = w.shape[0]
    tm = _tile(T, TOKEN_TILE)

    def body(u_ref, halo_ref, w_ref, bc_ref, g_ref, b_ref, wo_ref, bo_ref, res_ref, cu_ref, s_ref, h_ref, pad_ref):
        i = pl.program_id(0)
        pad_ref[HALO:, :] = u_ref[:, :D] * _sigmoid(u_ref[:, D:])
        pad_ref[:HALO, :] = jnp.where(i > 0, halo_ref[:, :D] * _sigmoid(halo_ref[:, D:]), 0.0)

        def block(t0, ls):
            win = pad_ref[pl.ds(t0, CONV_ROWS + HALO), ls]
            acc = jnp.zeros((CONV_ROWS, LANES), F32)
            for k in range(K):
                acc = acc + w_ref[k:k + 1, ls] * _shifted(win, HALO - (K - 1) + k, CONV_ROWS)
            cu_ref[pl.ds(t0, CONV_ROWS), ls] = acc + bc_ref[:, ls]

        _conv_loops(tm, D, block)
        cu = cu_ref[...]
        mu = jnp.mean(cu, axis=-1, keepdims=True)
        xc = cu - mu
        rstd = lax.rsqrt(jnp.mean(xc * xc, axis=-1, keepdims=True) + LN_EPS)
        ln = xc * rstd * g_ref[...] + b_ref[...]
        s = (ln * _sigmoid(ln)).astype(s_ref.dtype)
        s_ref[...] = s
        h_ref[...] = res_ref[...] + bo_ref[...] + jnp.dot(s, wo_ref[0], preferred_element_type=F32)

    vec = pl.BlockSpec((1, D), lambda i: (0, 0))
    row = pl.BlockSpec((tm, D), lambda i: (i, 0))
    (cu, s, h), xo = _call(
        body, name, (T // tm,),
        [pl.BlockSpec((tm, D2), lambda i: (i, 0)), _prev_halo_spec(tm, D2), pl.BlockSpec((K, D), lambda i: (0, 0)), vec, vec, vec,
         pl.BlockSpec((1, D, D), lambda i: (0, 0, 0)), vec, row],
        [row, row, row], [jax.ShapeDtypeStruct((T, D), F32), jax.ShapeDtypeStruct((T, D), BF16), jax.ShapeDtypeStruct((T, D), F32)],
        [u, u, w, b_conv, ln_g, ln_b, w_out, b_out, res], ("parallel",), [pltpu.VMEM((tm + HALO, D), F32)], hosted=hosted)
    return cu, s, h, xo


def pw2_ln_bwd(dy, w, cu, ln_g, ln_b, name, hosted=()):
    T, D = cu.shape
    tm = _tile(T, TOKEN_TILE)

    def body(dy_ref, w_ref, cu_ref, g_ref, b_ref, dcu_ref, dg_ref, db_ref, dbc_ref, dbo_ref):
        i = pl.program_id(0)
        dy_ = dy_ref[...]
        ds = lax.dot_general(dy_.astype(BF16), w_ref[0], _NT, preferred_element_type=F32)
        cu_ = cu_ref[...]
        mu = jnp.mean(cu_, axis=-1, keepdims=True)
        xc = cu_ - mu
        rstd = lax.rsqrt(jnp.mean(xc * xc, axis=-1, keepdims=True) + LN_EPS)
        xh = xc * rstd
        ln = xh * g_ref[...] + b_ref[...]
        sg = _sigmoid(ln)
        dl = ds * (sg * (1.0 + ln * (1.0 - sg)))
        dxh = dl * g_ref[...]
        dcu = rstd * (dxh - jnp.mean(dxh, axis=-1, keepdims=True) - xh * jnp.mean(dxh * xh, axis=-1, keepdims=True))
        dcu_ref[...] = dcu
        pg = jnp.sum(dl * xh, axis=0, keepdims=True)
        pb = jnp.sum(dl, axis=0, keepdims=True)
        pc = jnp.sum(dcu, axis=0, keepdims=True)
        po = jnp.sum(dy_, axis=0, keepdims=True)

        @pl.when(i == 0)
        def _():
            dg_ref[...] = pg
            db_ref[...] = pb
            dbc_ref[...] = pc
            dbo_ref[...] = po

        @pl.when(i > 0)
        def _():
            dg_ref[...] += pg
            db_ref[...] += pb
            dbc_ref[...] += pc
            dbo_ref[...] += po

    vec = pl.BlockSpec((1, D), lambda i: (0, 0))
    row = pl.BlockSpec((tm, D), lambda i: (i, 0))
    vshape = jax.ShapeDtypeStruct((1, D), F32)
    outs, xo = _call(
        body, name, (T // tm,), [row, pl.BlockSpec((1, D, D), lambda i: (0, 0, 0)), row, vec, vec], [row, vec, vec, vec, vec],
        [jax.ShapeDtypeStruct((T, D), F32), vshape, vshape, vshape, vshape], [dy, w, cu, ln_g, ln_b], ("arbitrary",),
        hosted=hosted)
    return (*outs, xo)


def bconv_bwd(dcu, u, w, name, hosted=()):
    T, D2 = u.shape
    D = D2 // 2
    K = w.shape[0]
    tm = _tile(T, TOKEN_TILE)
    nt = T // tm

    def body(dc_ref, dcn_ref, u_ref, up_ref, w_ref, du_ref, dw_ref, db_ref, glu_ref, dpad_ref, dglu_ref, wacc_ref):
        i = pl.program_id(0)
        glu_ref[HALO:, :] = u_ref[:, :D] * _sigmoid(u_ref[:, D:])
        glu_ref[:HALO, :] = jnp.where(i > 0, up_ref[:, :D] * _sigmoid(up_ref[:, D:]), 0.0)
        dpad_ref[:tm, :] = dc_ref[...]
        dpad_ref[tm:, :] = jnp.where(i < nt - 1, dcn_ref[...], 0.0)

        @pl.when(i == 0)
        def _():
            wacc_ref[...] = jnp.zeros_like(wacc_ref)

        def block(t0, ls):
            gwin = glu_ref[pl.ds(t0, CONV_ROWS + HALO), ls]
            dwin = dpad_ref[pl.ds(t0, CONV_ROWS + HALO), ls]
            dcur = dwin[:CONV_ROWS]
            dglu = jnp.zeros((CONV_ROWS, LANES), F32)
            for k in range(K):
                dglu = dglu + w_ref[k:k + 1, ls] * _shifted(dwin, (K - 1) - k, CONV_ROWS)
                gs = _shifted(gwin, HALO - (K - 1) + k, CONV_ROWS)
                wacc_ref[k * SUBLANES:(k + 1) * SUBLANES, ls] += _rowsum8(dcur * gs)
            dglu_ref[pl.ds(t0, CONV_ROWS), ls] = dglu

        _conv_loops(tm, D, block)
        dglu = dglu_ref[...]
        a = u_ref[:, :D]
        sg = _sigmoid(u_ref[:, D:])
        da = dglu * sg
        dg = dglu * a * (sg * (1.0 - sg))
        du_ref[:, :D] = da.astype(du_ref.dtype)
        du_ref[:, D:] = dg.astype(du_ref.dtype)
        pa = jnp.sum(da, axis=0, keepdims=True)
        pg = jnp.sum(dg, axis=0, keepdims=True)

        @pl.when(i == 0)
        def _():
            db_ref[:, :D] = pa
            db_ref[:, D:] = pg

        @pl.when(i > 0)
        def _():
            db_ref[:, :D] += pa
            db_ref[:, D:] += pg

        @pl.when(i == nt - 1)
        def _():
            for k in range(K):
                dw_ref[k:k + 1, :] = jnp.sum(wacc_ref[k * SUBLANES:(k + 1) * SUBLANES, :], axis=0, keepdims=True)

    (du, dw, db), xo = _call(
        body, name, (nt,),
        [pl.BlockSpec((tm, D), lambda i: (i, 0)), _next_halo_spec(tm, D, T),
         pl.BlockSpec((tm, D2), lambda i: (i, 0)), _prev_halo_spec(tm, D2), pl.BlockSpec((K, D), lambda i: (0, 0))],
        [pl.BlockSpec((tm, D2), lambda i: (i, 0)), pl.BlockSpec((K, D), lambda i: (0, 0)), pl.BlockSpec((1, D2), lambda i: (0, 0))],
        [jax.ShapeDtypeStruct((T, D2), BF16), jax.ShapeDtypeStruct((K, D), F32), jax.ShapeDtypeStruct((1, D2), F32)],
        [dcu, dcu, u, u, w], ("arbitrary",),
        [pltpu.VMEM((tm + HALO, D), F32), pltpu.VMEM((tm + HALO, D), F32), pltpu.VMEM((tm, D), F32),
         pltpu.VMEM((K * SUBLANES, D), F32)], hosted=hosted)
    return du, dw, db, xo


def mm_cols(a, w, bias, name, hosted=()):
    T, K = a.shape
    S, _, n = w.shape
    tm = _tile(T, WIDE_TOKEN_TILE)

    def body(*refs):
        a_ref, w_ref = refs[:2]
        o_ref = refs[-1]
        acc = jnp.dot(a_ref[...], w_ref[...], preferred_element_type=F32)
        if bias is not None:
            acc = acc + refs[2][...]
        o_ref[...] = acc

    in_specs = [pl.BlockSpec((tm, K), lambda s, i: (i, 0)), pl.BlockSpec((None, K, n), lambda s, i: (s, 0, 0))]
    args = [a, w]
    if bias is not None:
        in_specs.append(pl.BlockSpec((1, n), lambda s, i: (0, s)))
        args.append(bias)
    (out,), xo = _call(body, name, (S, T // tm), in_specs, [pl.BlockSpec((tm, n), lambda s, i: (i, s))],
                       [jax.ShapeDtypeStruct((T, S * n), F32)], args, ("parallel", "parallel"), hosted=hosted)
    return out, xo


def _load_weights(pairs, sems, S, i, p):
    def copies(seg):
        return [pltpu.make_async_copy(src.at[seg], dst.at[seg], sems.at[k, seg]) for k, (src, dst) in enumerate(pairs)]

    @pl.when((i == 0) & (p == 0))
    def _():
        for seg in range(S):
            for cp in copies(seg):
                cp.start()

    @pl.when((i == 0) & (p < S // FFN_SEGS_PER_STEP))
    def _():
        for j in range(FFN_SEGS_PER_STEP):
            for cp in copies(FFN_SEGS_PER_STEP * p + j):
                cp.wait()


def ffn_fwd(h, gain, weights, name, hosted=(), arriving=None):
    T, D = h.shape
    S, f, _ = weights[0].shape
    tm = _tile(T, TOKEN_TILE)
    rc = tm // FFN_ROW_CHUNKS
    chunks = [slice(r * rc, (r + 1) * rc) for r in range(FFN_ROW_CHUNKS)]
    G = FFN_SEGS_PER_STEP
    weights = list(weights)
    hosted = ([arriving.awaited_first()] if arriving is not None else []) + list(hosted)

    def body(h_ref, gain_ref, *refs):
        nw = len(weights)
        wg_hbm, wu_hbm, wd_hbm = list(refs[:nw]) + list(refs[nw + 9:])
        n_ref, g_ref, u_ref, gu_ref, o_ref, wg_v, wu_v, wd_v, sems = refs[nw:nw + 9]
        i, p = pl.program_id(0), pl.program_id(1)
        _load_weights([(wg_hbm, wg_v), (wu_hbm, wu_v), (wd_hbm, wd_v)], sems, S, i, p)

        @pl.when(p == 0)
        def _():
            x = h_ref[...]
            r = lax.rsqrt(jnp.mean(x * x, axis=-1, keepdims=True) + RMS_EPS)
            n_ref[...] = (x * r * gain_ref[...]).astype(n_ref.dtype)

        parts = []
        for rows in chunks:
            a = n_ref[rows, :]
            acc = None
            for j in range(G):
                seg = G * p + j
                g = lax.dot_general(a, wg_v[seg], _NT, preferred_element_type=F32)
                u = lax.dot_general(a, wu_v[seg], _NT, preferred_element_type=F32)
                gu = (g * _sigmoid(g) * u).astype(gu_ref.dtype)
                g_ref[j, rows, :] = g.astype(g_ref.dtype)
                u_ref[j, rows, :] = u.astype(u_ref.dtype)
                gu_ref[j, rows, :] = gu
                part = jnp.dot(gu, wd_v[seg], preferred_element_type=F32)
                acc = part if acc is None else acc + part
            parts.append(acc)

        @pl.when(p == 0)
        def _():
            for rows, part in zip(chunks, parts):
                o_ref[rows, :] = h_ref[rows, :] + part

        @pl.when(p > 0)
        def _():
            for rows, part in zip(chunks, parts):
                o_ref[rows, :] += part

    row = pl.BlockSpec((tm, D), lambda i, p: (i, 0))
    seg = pl.BlockSpec((G, tm, f), lambda i, p: (p, i, 0))
    hbm = pl.BlockSpec(memory_space=pl.ANY)
    segs = jax.ShapeDtypeStruct((S, T, f), BF16)
    outs, xo = _call(
        body, name, (T // tm, S // G),
        [row, pl.BlockSpec((1, D), lambda i, s: (0, 0))] + [hbm] * len(weights), [row, seg, seg, seg, row],
        [jax.ShapeDtypeStruct((T, D), BF16), segs, segs, segs, jax.ShapeDtypeStruct((T, D), F32)],
        [h, gain] + weights, ("arbitrary", "arbitrary"),
        [pltpu.VMEM((S, f, D), BF16), pltpu.VMEM((S, f, D), BF16), pltpu.VMEM((S, f, D), BF16), pltpu.SemaphoreType.DMA((3, S))],
        hosted=hosted)
    return (*outs, xo)


def ffn_bwd(dy, h, gain, g, u, wd, wg, wu, name, hosted=()):
    T, D = h.shape
    S, f, _ = wg.shape
    tm = _tile(T, TOKEN_TILE)
    nt = T // tm
    rc = tm // FFN_ROW_CHUNKS
    chunks = [slice(r * rc, (r + 1) * rc) for r in range(FFN_ROW_CHUNKS)]
    G = FFN_SEGS_PER_STEP
    P = S // G

    def body(dy_ref, h_ref, gain_ref, g_ref, u_ref, wd_hbm, wg_hbm, wu_hbm, dg_ref, du_ref, dh_ref, dgain_ref,
             wd_v, wg_v, wu_v, dyb_ref, dgs_ref, dus_ref, sems):
        i, p = pl.program_id(0), pl.program_id(1)
        _load_weights([(wd_hbm, wd_v), (wg_hbm, wg_v), (wu_hbm, wu_v)], sems, S, i, p)

        def first_stage(grp, slot):
            for rows in chunks:
                for j in range(G):
                    dgu = lax.dot_general(dyb_ref[rows, :], wd_v[G * grp + j], _NT, preferred_element_type=F32)
                    gv = g_ref[j, rows, :].astype(F32)
                    sg = _sigmoid(gv)
                    dg = (dgu * u_ref[j, rows, :].astype(F32) * (sg * (1.0 + gv * (1.0 - sg)))).astype(dg_ref.dtype)
                    du = (dgu * (gv * sg)).astype(du_ref.dtype)
                    dg_ref[j, rows, :] = dg
                    du_ref[j, rows, :] = du
                    dgs_ref[slot, j, rows, :] = dg
                    dus_ref[slot, j, rows, :] = du

        def second_stage(grp, slot):
            for rows in chunks:
                acc = None
                for j in range(G):
                    part = (jnp.dot(dgs_ref[slot, j, rows, :], wg_v[G * grp + j], preferred_element_type=F32)
                            + jnp.dot(dus_ref[slot, j, rows, :], wu_v[G * grp + j], preferred_element_type=F32))
                    acc = part if acc is None else acc + part
                dh_ref[rows, :] += acc

        @pl.when(p == 0)
        def _():
            dyb_ref[...] = dy_ref[...].astype(dyb_ref.dtype)
            dh_ref[...] = jnp.zeros_like(dh_ref)
            first_stage(0, 0)

        @pl.when((p > 0) & (p < P))
        def _():
            second_stage(p - 1, (p - 1) % 2)
            first_stage(p, p % 2)

        @pl.when(p == P)
        def _():
            second_stage(P - 1, (P - 1) % 2)
            dn = dh_ref[...]
            x = h_ref[...]
            r = lax.rsqrt(jnp.mean(x * x, axis=-1, keepdims=True) + RMS_EPS)
            xhat = x * r
            dxhat = dn * gain_ref[...]
            dh_ref[...] = dy_ref[...] + r * (dxhat - xhat * jnp.mean(dxhat * xhat, axis=-1, keepdims=True))
            pg = jnp.sum(dn * xhat, axis=0, keepdims=True)

            @pl.when(i == 0)
            def _():
                dgain_ref[...] = pg

            @pl.when(i > 0)
            def _():
                dgain_ref[...] += pg

    row = pl.BlockSpec((tm, D), lambda i, p: (i, 0))
    vec = pl.BlockSpec((1, D), lambda i, p: (0, 0))
    seg = pl.BlockSpec((G, tm, f), lambda i, p: (jnp.minimum(p, P - 1), i, 0))
    hbm = pl.BlockSpec(memory_space=pl.ANY)
    segs = jax.ShapeDtypeStruct((S, T, f), BF16)
    outs, xo = _call(
        body, name, (nt, P + 1),
        [row, row, vec, seg, seg, hbm, hbm, hbm], [seg, seg, row, vec],
        [segs, segs, jax.ShapeDtypeStruct((T, D), F32), jax.ShapeDtypeStruct((1, D), F32)],
        [dy, h, gain, g, u, wd, wg, wu], ("arbitrary", "arbitrary"),
        [pltpu.VMEM((S, f, D), BF16), pltpu.VMEM((S, f, D), BF16), pltpu.VMEM((S, f, D), BF16),
         pltpu.VMEM((tm, D), BF16), pltpu.VMEM((2, G, tm, f), BF16), pltpu.VMEM((2, G, tm, f), BF16),
         pltpu.SemaphoreType.DMA((3, S))], hosted=hosted)
    return (*outs, xo)


_NT = (((1,), (1,)), ((), ()))
_TN = (((0,), (0,)), ((), ()))


def nt_rows(dy, w, name, hosted=()):
    T, N = dy.shape
    S, k, _ = w.shape
    tm = _tile(T, TOKEN_TILE)

    def body(dy_ref, w_ref, o_ref):
        o_ref[...] = lax.dot_general(dy_ref[...].astype(BF16), w_ref[...], _NT, preferred_element_type=F32)

    (out,), xo = _call(
        body, name, (T // tm, S),
        [pl.BlockSpec((tm, N), lambda i, s: (i, 0)), pl.BlockSpec((None, k, N), lambda i, s: (s, 0, 0))],
        [pl.BlockSpec((None, tm, k), lambda i, s: (s, i, 0))], [jax.ShapeDtypeStruct((S, T, k), F32)],
        [dy, w], ("parallel", "parallel"), hosted=hosted)
    return out, xo


def nt_cols_rms(dy, w, h, gain, dres, name, hosted=()):
    T, K = h.shape
    S, _, n = w.shape
    tm = _tile(T, TOKEN_TILE)

    def body(dy_ref, w_ref, h_ref, gain_ref, dres_ref, dh_ref, dgain_ref):
        i = pl.program_id(0)
        dn = None
        for s in range(S):
            part = lax.dot_general(dy_ref[:, s * n:(s + 1) * n], w_ref[s], _NT, preferred_element_type=F32)
            dn = part if dn is None else dn + part
        x = h_ref[...]
        r = lax.rsqrt(jnp.mean(x * x, axis=-1, keepdims=True) + RMS_EPS)
        xhat = x * r
        dxhat = dn * gain_ref[...]
        dh_ref[...] = dres_ref[...] + r * (dxhat - xhat * jnp.mean(dxhat * xhat, axis=-1, keepdims=True))
        pg = jnp.sum(dn * xhat, axis=0, keepdims=True)

        @pl.when(i == 0)
        def _():
            dgain_ref[...] = pg

        @pl.when(i > 0)
        def _():
            dgain_ref[...] += pg

    row = pl.BlockSpec((tm, K), lambda i: (i, 0))
    vec = pl.BlockSpec((1, K), lambda i: (0, 0))
    (dh, dgain), xo = _call(
        body, name, (T // tm,),
        [pl.BlockSpec((tm, S * n), lambda i: (i, 0)), pl.BlockSpec((S, K, n), lambda i: (0, 0, 0)), row, vec, row],
        [row, vec], [jax.ShapeDtypeStruct((T, K), F32), jax.ShapeDtypeStruct((1, K), F32)],
        [dy, w, h, gain, dres], ("arbitrary",), hosted=hosted)
    return dh, dgain, xo


def tn_grad(a, dy, S, a_by_seg, name, hosted=()):
    T = dy.shape[0] if dy.ndim == 2 else dy.shape[1]
    tt = _tile(T, GRAD_TOKEN_TILE)
    if a_by_seg:
        R = a.shape[1] // S if a.ndim == 2 else a.shape[2]
        C = dy.shape[1]
        a_spec = pl.BlockSpec((tt, R), lambda s, t: (t, s)) if a.ndim == 2 else pl.BlockSpec((None, tt, R), lambda s, t: (s, t, 0))
        b_spec = pl.BlockSpec((tt, C), lambda s, t: (t, 0))
    else:
        R = a.shape[1]
        C = dy.shape[1] // S if dy.ndim == 2 else dy.shape[2]
        a_spec = pl.BlockSpec((tt, R), lambda s, t: (t, 0))
        b_spec = pl.BlockSpec((tt, C), lambda s, t: (t, s)) if dy.ndim == 2 else pl.BlockSpec((None, tt, C), lambda s, t: (s, t, 0))
    Rh = R // 2
    nt = T // tt

    def body(a_ref, b_ref, o_ref, acc_ref):
        t = pl.program_id(1)
        part = lax.dot_general(a_ref[...], b_ref[...].astype(BF16), _TN, preferred_element_type=F32)

        @pl.when(t == 0)
        def _():
            acc_ref[...] = part

        @pl.when(t > 0)
        def _():
            acc_ref[...] += part

        @pl.when(t == nt - 1)
        def _():
            o_ref[0] = acc_ref[:Rh, :].astype(o_ref.dtype)
            o_ref[1] = acc_ref[Rh:, :].astype(o_ref.dtype)

    (gh,), xo = _call(
        body, name, (S, nt), [a_spec, b_spec], [pl.BlockSpec((2, None, Rh, C), lambda s, t: (0, s, 0, 0))],
        [jax.ShapeDtypeStruct((2, S, Rh, C), BF16)], [a, dy], ("parallel", "arbitrary"), [pltpu.VMEM((R, C), F32)],
        hosted=hosted)
    return gh, xo


def tn_grad_square(a, dy, S, name, hosted=()):
    T, K = a.shape
    N = dy.shape[1]
    tt = _tile(T, GRAD_TOKEN_TILE)
    nt = T // tt
    Rh = K // S // 2

    def body(a_ref, b_ref, o_ref, acc_ref):
        t = pl.program_id(0)
        part = lax.dot_general(a_ref[...], b_ref[...].astype(BF16), _TN, preferred_element_type=F32)

        @pl.when(t == 0)
        def _():
            acc_ref[...] = part

        @pl.when(t > 0)
        def _():
            acc_ref[...] += part

        @pl.when(t == nt - 1)
        def _():
            for s in range(S):
                for hf in range(2):
                    r0 = (2 * s + hf) * Rh
                    o_ref[hf, s] = acc_ref[r0:r0 + Rh, :].astype(o_ref.dtype)

    (gh,), xo = _call(
        body, name, (nt,), [pl.BlockSpec((tt, K), lambda t: (t, 0)), pl.BlockSpec((tt, N), lambda t: (t, 0))],
        [pl.BlockSpec((2, S, Rh, N), lambda t: (0, 0, 0, 0))], [jax.ShapeDtypeStruct((2, S, Rh, N), BF16)],
        [a, dy], ("arbitrary",), [pltpu.VMEM((K, N), F32)], hosted=hosted)
    return gh, xo


def _place():
    x, y, c = lax.axis_index("x"), lax.axis_index("y"), lax.axis_index("c")
    chips = [(1 - x, y), (x, 1 - y), (1 - x, 1 - y)]
    return x, y, c, chips


def _remote(src, dst, send_sem, recv_sem, dev):
    return pltpu.make_async_remote_copy(src_ref=src, dst_ref=dst, send_sem=send_sem, recv_sem=recv_sem,
                                        device_id=dev, device_id_type=MESH)


def small_allreduce(v, name, hosted=()):
    rows, W = v.shape

    def body(v_ref, o_ref, sib_ref, pair_ref, chips_ref, send_sems, recv_sems):
        x, y, c, chips = _place()
        me = 2 * x + y
        swap = _remote(v_ref, sib_ref, send_sems.at[3], recv_sems.at[3], (x, y, 1 - c))
        swap.start()
        swap.wait()
        mine, other = v_ref[...], sib_ref[...]
        pair_ref[...] = jnp.where(c == 0, mine, other) + jnp.where(c == 0, other, mine)
        sends = []
        for j, (px, py) in enumerate(chips):
            cp = _remote(pair_ref, chips_ref.at[me], send_sems.at[j], recv_sems.at[j], (px, py, c))
            cp.start()
            sends.append(cp)
        chips_ref[me] = pair_ref[...]
        for j, (px, py) in enumerate(chips):
            blk = chips_ref.at[2 * px + py]
            _remote(blk, blk, send_sems.at[j], recv_sems.at[j], (px, py, c)).wait_recv()
        for cp in sends:
            cp.wait_send()
        o_ref[...] = (chips_ref[0] + chips_ref[1]) + (chips_ref[2] + chips_ref[3])

    vm = pl.BlockSpec(memory_space=pltpu.VMEM)
    (out,), xo = _call(
        body, name, (), [vm], [vm], [jax.ShapeDtypeStruct((rows, W), F32)], [v], (),
        [pltpu.VMEM((rows, W), F32), pltpu.VMEM((rows, W), F32), pltpu.VMEM((N_CHIPS, rows, W), F32),
         pltpu.SemaphoreType.DMA((4,)), pltpu.SemaphoreType.DMA((4,))], hosted=hosted)
    return out, xo


def _gather_p1_copies(srcs, bufs, ssem, rsem, base):
    x, y, c, chips = _place()
    me, sib = 2 * x + y, (x, y, 1 - c)
    sends, recvs = [], []
    for k, (src, buf) in enumerate(zip(srcs, bufs)):
        rh = src.shape[0] // 2
        s0 = base + 4 * k
        sends.append(_remote(src, buf.at[me], ssem.at[s0 + 3], rsem.at[s0 + 3], sib))
        recvs.append(_remote(buf.at[me], buf.at[me], ssem.at[s0 + 3], rsem.at[s0 + 3], sib))
        for j, (px, py) in enumerate(chips):
            sends.append(_remote(src.at[pl.ds(c * rh, rh)], buf.at[me, pl.ds(c * rh, rh)], ssem.at[s0 + j], rsem.at[s0 + j], (px, py, c)))
            blk = buf.at[2 * px + py, pl.ds(c * rh, rh)]
            recvs.append(_remote(blk, blk, ssem.at[s0 + j], rsem.at[s0 + j], (px, py, c)))
    return sends, recvs


def _gather_p2_copies(bufs, ssem, rsem, base):
    x, y, c, chips = _place()
    sib = (x, y, 1 - c)
    sends, recvs = [], []
    for k, buf in enumerate(bufs):
        rh = buf.shape[1] // 2
        for j, (px, py) in enumerate(chips):
            s0 = base + 3 * k + j
            blk = buf.at[2 * px + py, pl.ds(c * rh, rh)]
            sends.append(_remote(blk, blk, ssem.at[s0], rsem.at[s0], sib))
            got = buf.at[2 * px + py, pl.ds((1 - c) * rh, rh)]
            recvs.append(_remote(got, got, ssem.at[s0], rsem.at[s0], sib))
    return sends, recvs


def _gathered_shape(s):
    return jax.ShapeDtypeStruct((N_CHIPS,) + s.shape, s.dtype)


def gather_p1(shards):
    return _Exchange(shards, [_gathered_shape(s) for s in shards], {}, 4 * len(shards),
                     lambda xi, xo, ss, rs: _gather_p1_copies(xi, xo, ss, rs, 0))


def gather_p2(bufs):
    return _Exchange(bufs, [jax.ShapeDtypeStruct(b.shape, b.dtype) for b in bufs], {k: k for k in range(len(bufs))},
                     3 * len(bufs), lambda xi, xo, ss, rs: _gather_p2_copies(xo, ss, rs, 0))


def gather_whole(whole, begun):
    nw, n = len(whole), len(whole) + len(begun)
    shards = list(whole) + list(begun)
    return _Exchange(shards, [_gathered_shape(s) for s in shards], {}, 4 * n + 3 * nw,
                     lambda xi, xo, ss, rs: _gather_p1_copies(xi, xo, ss, rs, 0),
                     then=lambda xi, xo, ss, rs: _gather_p2_copies(xo[:nw], ss, rs, 4 * n))


def gather_small(v):
    def copies(xi, xo, ssem, rsem):
        x, y, c, chips = _place()
        me, sib = 2 * x + y, (x, y, 1 - c)
        sends = [_remote(xi[0], xo[0].at[me], ssem.at[3], rsem.at[3], sib)]
        recvs = [_remote(xo[0].at[me], xo[0].at[me], ssem.at[3], rsem.at[3], sib)]
        for j, (px, py) in enumerate(chips):
            sends.append(_remote(xi[0], xo[0].at[me], ssem.at[j], rsem.at[j], (px, py, c)))
            blk = xo[0].at[2 * px + py]
            recvs.append(_remote(blk, blk, ssem.at[j], rsem.at[j], (px, py, c)))
        return sends, recvs

    return _Exchange([v], [_gathered_shape(v)], {}, 4, copies)


def run_exchanges(exchanges, name):
    return _call(lambda: None, name, (), [], [], [], [], (), hosted=exchanges)[1]


def sibling_halves(grads):
    def copies(xi, xo, ssem, rsem):
        x, y, c, _ = _place()
        sends = [_remote(xi[k].at[1 - c], xo[k], ssem.at[k], rsem.at[k], (x, y, 1 - c)) for k in range(len(grads))]
        return sends, sends

    return _Exchange(grads, [jax.ShapeDtypeStruct(g.shape[1:], g.dtype) for g in grads], {}, len(grads), copies)


def pair_sum(gh, recv, cidx, name):
    _, S, Rh, C = gh.shape

    def body(c_ref, a_ref, b_ref, o_ref):
        o_ref[...] = (a_ref[...].astype(F32) + b_ref[...].astype(F32)).astype(o_ref.dtype)

    return pl.pallas_call(
        body, name=name, out_shape=jax.ShapeDtypeStruct((S, Rh, C), BF16),
        grid_spec=pltpu.PrefetchScalarGridSpec(
            num_scalar_prefetch=1, grid=(S,),
            in_specs=[pl.BlockSpec((None, None, Rh, C), lambda s, c_ref: (c_ref[0], s, 0, 0)),
                      pl.BlockSpec((None, Rh, C), lambda s, c_ref: (s, 0, 0))],
            out_specs=pl.BlockSpec((None, Rh, C), lambda s, c_ref: (s, 0, 0))),
        compiler_params=_params(("parallel",)),
    )(cidx, gh, recv)


def scatter_p1(parts):
    def copies(xi, xo, ssem, rsem):
        x, y, c, chips = _place()
        me, sib = 2 * x + y, (x, y, 1 - c)
        sends, recvs = [], []
        for k in range(len(parts)):
            s0 = 4 * k
            sends.append(_remote(xi[k].at[me], xo[k].at[me, c], ssem.at[s0 + 3], rsem.at[s0 + 3], sib))
            own = xo[k].at[me, 1 - c]
            recvs.append(_remote(own, own, ssem.at[s0 + 3], rsem.at[s0 + 3], sib))
            for j, (px, py) in enumerate(chips):
                sends.append(_remote(xi[k].at[2 * px + py], xo[k].at[me, c], ssem.at[s0 + j], rsem.at[s0 + j], (px, py, c)))
                blk = xo[k].at[2 * px + py, c]
                recvs.append(_remote(blk, blk, ssem.at[s0 + j], rsem.at[s0 + j], (px, py, c)))
        return sends, recvs

    return _Exchange(parts, [jax.ShapeDtypeStruct((p.shape[0], 2) + p.shape[1:], p.dtype) for p in parts], {},
                     4 * len(parts), copies)


def scatter_p2(bufs):
    def copies(xi, xo, ssem, rsem):
        x, y, c, chips = _place()
        sib = (x, y, 1 - c)
        sends, recvs = [], []
        for k in range(len(bufs)):
            for j, (px, py) in enumerate(chips):
                s0 = 3 * k + j
                blk = xo[k].at[2 * px + py, c]
                sends.append(_remote(blk, blk, ssem.at[s0], rsem.at[s0], sib))
                got = xo[k].at[2 * px + py, 1 - c]
                recvs.append(_remote(got, got, ssem.at[s0], rsem.at[s0], sib))
        return sends, recvs

    return _Exchange(bufs, [jax.ShapeDtypeStruct(b.shape, b.dtype) for b in bufs], {k: k for k in range(len(bufs))},
                     3 * len(bufs), copies)


def _adamw_math(w, g, m, v):
    m = ADAM_B1 * m + (1.0 - ADAM_B1) * g
    v = ADAM_B2 * v + (1.0 - ADAM_B2) * (g * g)
    m_hat = m / (1.0 - ADAM_B1 ** ADAM_STEP)
    v_hat = v / (1.0 - ADAM_B2 ** ADAM_STEP)
    delta = -ADAM_LR * (m_hat / (jnp.sqrt(v_hat) + ADAM_EPS) + ADAM_WD * w)
    return delta, m, v


def adamw_reduce(w, m, v, buf, part, place, lyr, bases, name, hosted=()):
    L, R, C = w.shape
    Rh = R // 2
    rb = _tile(Rh, ROW_TILE, 2 * SUBLANES)
    nb = Rh // rb

    def body(place_ref, p_ref, b0, b1, b2, b3, w_ref, m_ref, v_ref, *rest):
        go_ref, d_ref, mo_ref, vo_ref = rest[-4:]
        mine = (place_ref[1] == pl.program_id(0))
        g = None
        for p, b in enumerate((b0, b1, b2, b3)):
            val = jnp.where(mine & (place_ref[0] == p), p_ref[...], b[...]).astype(F32)
            g = val if g is None else g + val
        d, mn, vn = _adamw_math(w_ref[...], g, m_ref[...], v_ref[...])
        go_ref[...] = g
        d_ref[...] = d
        mo_ref[...] = mn
        vo_ref[...] = vn

    def buf_spec(p):
        def idx(h, i, pr):
            own = (pr[0] == p) & (pr[1] == h)
            return (p, jnp.where(own, 1 - h, h), i, 0)
        return pl.BlockSpec((None, None, rb, C), idx)

    blk = pl.BlockSpec((None, rb, C), lambda h, i, pr: (lyr, h * nb + i, 0))
    in_specs = [pl.BlockSpec((None, rb, C), lambda h, i, pr: (pr[0], i, 0))] + [buf_spec(p) for p in range(N_CHIPS)] + [blk] * 3
    args = [part, buf, buf, buf, buf, w, m, v]
    aliases = {}
    if bases is not None:
        in_specs += [pl.BlockSpec(memory_space=pl.ANY)] * 4
        aliases = {len(args) + k: k for k in range(4)}
        args += list(bases)
    shp = jax.ShapeDtypeStruct((L, R, C), F32)
    return _call(body, name, (2, nb), in_specs, [blk] * 4, [shp] * 4, args, ("parallel", "parallel"),
                 hosted=hosted, prefetch=[place], own_aliases=aliases)


def small_update(gall, chip, entries, name):
    ne = len(entries)
    D = gall.shape[1]

    def body(chip_ref, gall_ref, *refs):
        ins, outs = refs[:3 * ne], refs[3 * ne:]
        ch = chip_ref[0]
        for e, (row0, kind, w, _, _) in enumerate(entries):
            r, width = w.shape

            def gsum(rs, cs):
                return gall_ref[rs, cs]

            if kind == "full":
                g = gsum(slice(row0, row0 + r), slice(0, D))
            elif kind == "cols":
                g = gsum(slice(row0, row0 + r), slice(0, width))
                for q in range(1, N_CHIPS):
                    g = jnp.where(ch == q, gsum(slice(row0, row0 + r), slice(q * width, (q + 1) * width)), g)
            else:
                per_row = D // width
                g = gsum(slice(row0, row0 + 1), slice(0, width))
                for q in range(1, N_CHIPS):
                    rr = row0 + q // per_row
                    cc = (q % per_row) * width
                    g = jnp.where(ch == q, gsum(slice(rr, rr + 1), slice(cc, cc + width)), g)
            d, mn, vn = _adamw_math(ins[3 * e][...], g, ins[3 * e + 1][...], ins[3 * e + 2][...])
            outs[4 * e][...] = g
            outs[4 * e + 1][...] = d
            outs[4 * e + 2][...] = mn
            outs[4 * e + 3][...] = vn

    vm = pl.BlockSpec(memory_space=pltpu.VMEM)
    args, out_shape = [], []
    for _, _, w, m, v in entries:
        args += [w, m, v]
        out_shape += [jax.ShapeDtypeStruct(w.shape, F32)] * 4
    return pl.pallas_call(
        body, name=name,
        in_specs=[pl.BlockSpec(memory_space=pltpu.SMEM), vm] + [vm] * (3 * ne),
        out_specs=[vm] * (4 * ne), out_shape=out_shape,
        compiler_params=pltpu.CompilerParams(vmem_limit_bytes=VMEM_LIMIT),
    )(chip, gall, *args)


def _pack_rows(items, width, name):
    starts, at = [], 0
    for it in items:
        starts.append(at)
        at += -(-it.shape[0] // SUBLANES) * SUBLANES
    total = at

    def body(*refs):
        o_ref = refs[-1]
        o_ref[...] = jnp.zeros_like(o_ref)
        for it_ref, r0 in zip(refs[:-1], starts):
            o_ref[r0:r0 + it_ref.shape[0], :] = it_ref[...]

    vm = pl.BlockSpec(memory_space=pltpu.VMEM)
    packed = pl.pallas_call(body, name=name, in_specs=[vm] * len(items), out_specs=vm,
                            out_shape=jax.ShapeDtypeStruct((total, width), F32))(*items)
    return packed, starts


def kernel(x, a_norm, a_w_in, a_conv, a_w_out, b_norm, b_w_pw1, b_b_pw1, b_conv, b_b_conv, b_ln_g, b_ln_b, b_w_pw2, b_b_pw2, ffn_norm, ffn_w_gate, ffn_w_up, ffn_w_down, final_norm, loss_target, m_a_norm, m_a_w_in, m_a_conv, m_a_w_out, m_b_norm, m_b_w_pw1, m_b_b_pw1, m_b_conv, m_b_b_conv, m_b_ln_g, m_b_ln_b, m_b_w_pw2, m_b_b_pw2, m_ffn_norm, m_ffn_w_gate, m_ffn_w_up, m_ffn_w_down, m_final_norm, v_a_norm, v_a_w_in, v_a_conv, v_a_w_out, v_b_norm, v_b_w_pw1, v_b_b_pw1, v_b_conv, v_b_b_conv, v_b_ln_g, v_b_ln_b, v_b_w_pw2, v_b_b_pw2, v_ffn_norm, v_ffn_w_gate, v_ffn_w_up, v_ffn_w_down, v_final_norm):
    T, D = x.shape[1], x.shape[2]
    Dq = D // N_CHIPS
    cx, cy, cc = lax.axis_index("x"), lax.axis_index("y"), lax.axis_index("c")
    chip = (2 * cx + cy).astype(jnp.int32).reshape(1)
    cidx = cc.astype(jnp.int32).reshape(1)
    h0 = x.reshape(T, D)
    tgt = loss_target.reshape(T, D)

    small_shards = [a_conv[0], b_norm, b_b_pw1.reshape(2, Dq), b_conv[0], b_b_conv, b_ln_g, b_ln_b, b_b_pw2]
    packed, st = _pack_rows(small_shards, Dq, "pack_small")

    tr = lambda t: jnp.swapaxes(t, 1, 2)
    w_gate, m_gate, v_gate = tr(ffn_w_gate), tr(m_ffn_w_gate), tr(v_ffn_w_gate)
    w_up, m_up, v_up = tr(ffn_w_up), tr(m_ffn_w_up), tr(v_ffn_w_up)
    bf = lambda t: t.astype(BF16)
    s_in, s_out, s_pw1, s_pw2 = bf(a_w_in[0]), bf(a_w_out[0]), bf(b_w_pw1[0]), bf(b_w_pw2[0])
    s_gate, s_up, s_down = [bf(w_gate[l]) for l in (0, 1)], [bf(w_up[l]) for l in (0, 1)], [bf(ffn_w_down[l]) for l in (0, 1)]

    n0, (g_in,) = rms_fwd(h0, a_norm, "rms_a", hosted=[gather_whole([s_in], [])])
    bcv, (g_out, gate0, sw) = mm_cols(n0, g_in, None, "mm_w_in", hosted=[gather_p1([s_out, s_gate[0]]), gather_small(packed)])

    def whole(k, r):
        return jnp.transpose(sw[:, st[k]:st[k] + r, :], (1, 0, 2)).reshape(r, D)

    a_conv_f, b_norm_f = whole(0, 3), whole(1, 1)
    b_b_pw1_f = sw[:, st[2]:st[2] + 2, :].reshape(1, 2 * D)
    b_conv_f, b_b_conv_f, b_ln_g_f, b_ln_b_f, b_b_pw2_f = whole(3, b_conv.shape[1]), whole(4, 1), whole(5, 1), whole(6, 1), whole(7, 1)
    ya, h1, (g_out, up0, down0, gate0) = gateconv_fwd(bcv, a_conv_f, gather_p2([g_out]), h0, "gateconv_fwd",
                                                      hosted=[gather_p1([s_up[0], s_down[0]]), gather_p2([gate0])])
    g_out = g_out.reshape(1, D, D)
    n1, fg0, fu0, gu0, h2, (up0, down0, *later) = ffn_fwd(h1, ffn_norm[0:1], [gate0], "ffn_fwd0", arriving=gather_p2([up0, down0]),
                                                          hosted=[gather_p1([s_pw1, s_pw2, s_gate[1], s_up[1]])])
    n2, (g_pw1, g_pw2, gate1, up1) = rms_fwd(h2, b_norm_f, "rms_b", hosted=[gather_p2(later)])
    g_pw2 = g_pw2.reshape(1, D, D)
    ub, (down1,) = mm_cols(n2, g_pw1, b_b_pw1_f, "mm_pw1", hosted=[gather_p1([s_down[1]])])
    cu, sb, h3, (down1,) = bconv_fwd(ub, b_conv_f, b_b_conv_f, b_ln_g_f, b_ln_b_f, g_pw2, b_b_pw2_f, h2, "bconv_fwd",
                                     hosted=[gather_p2([down1])])
    n3, fg1, fu1, gu1, h4, _ = ffn_fwd(h3, ffn_norm[1:2], [gate1, up1, down1], "ffn_fwd1")
    loss_part, dh4, d_final = loss_head(h4, final_norm.reshape(1, D), tgt, "loss_head")

    place = jnp.concatenate([chip, cidx])

    def pair_sums(ghs, from_sib, tags):
        return [pair_sum(g, r, cidx, "pair_sum_" + t) for g, r, t in zip(ghs, from_sib, tags)]

    def upd(w, m, v, bufs, parts, tag, hosted=()):
        res, xo = None, []
        for lyr, (b, p) in enumerate(zip(bufs, parts)):
            res, xo_l = adamw_reduce(w, m, v, b, p, place, lyr, res, "adamw_%s%d" % (tag, lyr), hosted=hosted if lyr == 0 else ())
            xo += xo_l
        return res, xo

    dg1, du1, dh3, d_fn1, _ = ffn_bwd(dh4, h3, ffn_norm[1:2], fg1, fu1, down1, gate1, up1, "ffn_bwd1")
    gh_down1, _ = tn_grad(gu1, dh4, N_CHIPS, True, "tn_down1")
    gh_gate1, _ = tn_grad(dg1, n3, N_CHIPS, True, "tn_gate1")
    gh_up1, _ = tn_grad(du1, n3, N_CHIPS, True, "tn_up1")
    f1 = [gh_gate1, gh_up1, gh_down1]

    dcu, d_ln_g, d_ln_b, d_b_conv, d_b_pw2, sib_f1 = pw2_ln_bwd(dh3, g_pw2, cu, b_ln_g_f, b_ln_b_f, "pw2_ln_bwd",
                                                                hosted=[sibling_halves(f1)])
    p_f1 = pair_sums(f1, sib_f1, ["gate1", "up1", "down1"])
    gh_pw2, _ = tn_grad_square(sb, dh3, N_CHIPS, "tn_pw2")
    dub, d_bconv_w, d_b_pw1, buf_f1 = bconv_bwd(dcu, ub, b_conv_f, "bconv_bwd", hosted=[scatter_p1(p_f1)])
    gh_pw1, _ = tn_grad(n2, dub, N_CHIPS, False, "tn_pw1")
    b_grp = [gh_pw1, gh_pw2]
    dh2, d_b_norm, (*buf_f1, sib_pw1, sib_pw2) = nt_cols_rms(dub, g_pw1, h2, b_norm_f, dh3, "nt_pw1",
                                                             hosted=[scatter_p2(buf_f1), sibling_halves(b_grp)])
    sib_b = [sib_pw1, sib_pw2]
    p_b = pair_sums(b_grp, sib_b, ["pw1", "pw2"])

    dg0, du0, dh1, d_fn0, buf_b = ffn_bwd(dh2, h1, ffn_norm[0:1], fg0, fu0, down0, gate0, up0, "ffn_bwd0", hosted=[scatter_p1(p_b)])
    gh_down0, _ = tn_grad(gu0, dh2, N_CHIPS, True, "tn_down0")
    gh_gate0, (*buf_b, sib_down0) = tn_grad(dg0, n1, N_CHIPS, True, "tn_gate0",
                                            hosted=[scatter_p2(buf_b), sibling_halves([gh_down0])])
    p_down0 = pair_sums([gh_down0], [sib_down0], ["down0"])
    gh_up0, (buf_down0, sib_gate0) = tn_grad(du0, n1, N_CHIPS, True, "tn_up0",
                                             hosted=[scatter_p1(p_down0), sibling_halves([gh_gate0])])
    p_gate0 = pair_sums([gh_gate0], [sib_gate0], ["gate0"])
    dya, (buf_down0, sib_up0) = nt_rows(dh1, g_out, "nt_w_out",
                                        hosted=[scatter_p2([buf_down0]), sibling_halves([gh_up0])])
    p_up0 = pair_sums([gh_up0], [sib_up0], ["up0"])
    gh_out, _ = tn_grad_square(ya, dh1, N_CHIPS, "tn_w_out")
    dbcv, d_aconv_w, (buf_gate0, sib_out) = gateconv_bwd(dya[0], bcv, a_conv_f, "gateconv_bwd",
                                                         hosted=[scatter_p1(p_gate0), sibling_halves([gh_out])])
    p_out = pair_sums([gh_out], [sib_out], ["out"])
    gh_in, (buf_up0, buf_gate0) = tn_grad(n0, dbcv, N_CHIPS, False, "tn_w_in",
                                          hosted=[scatter_p1(p_up0), scatter_p2([buf_gate0])])
    sib_in = run_exchanges([sibling_halves([gh_in])], "reduce_in_siblings")
    p_in = pair_sums([gh_in], sib_in, ["in"])
    grad_x, d_a_norm, (buf_in, buf_out, buf_up0) = nt_cols_rms(
        dbcv, g_in, h0, a_norm, dh1, "nt_w_in", hosted=[scatter_p1(p_in + p_out), scatter_p2([buf_up0])])
    p_f0 = [p_gate0[0], p_up0[0], p_down0[0]]

    d_ffn_norm = jnp.concatenate([d_fn0, d_fn1], axis=0)
    small_grads = [d_a_norm, d_aconv_w, d_b_norm, d_b_pw1.reshape(2, D), d_bconv_w, d_b_conv, d_ln_g, d_ln_b, d_b_pw2,
                   d_ffn_norm, d_final, jnp.broadcast_to(loss_part, (1, D))]
    gpacked, gs = _pack_rows(small_grads, D, "pack_small_grads")
    gall, (buf_in, buf_out) = small_allreduce(gpacked, "allreduce_small_grads", hosted=[scatter_p2([buf_in, buf_out])])
    buf_a, p_a = [buf_in, buf_out], [p_in[0], p_out[0]]

    r_gate, _ = upd(w_gate, m_gate, v_gate, [buf_gate0, buf_f1[0]], [p_f0[0], p_f1[0]], "gate")
    r_up, _ = upd(w_up, m_up, v_up, [buf_up0, buf_f1[1]], [p_f0[1], p_f1[1]], "up")
    r_down, _ = upd(ffn_w_down, m_ffn_w_down, v_ffn_w_down, [buf_down0, buf_f1[2]], [p_f0[2], p_f1[2]], "down")
    r_gate, r_up = [tr(t) for t in r_gate], [tr(t) for t in r_up]
    r_pw1, _ = upd(b_w_pw1, m_b_w_pw1, v_b_w_pw1, [buf_b[0]], [p_b[0]], "pw1")
    r_pw2, _ = upd(b_w_pw2, m_b_w_pw2, v_b_w_pw2, [buf_b[1]], [p_b[1]], "pw2")
    r_in, _ = upd(a_w_in, m_a_w_in, v_a_w_in, [buf_a[0]], [p_a[0]], "w_in")
    r_out, _ = upd(a_w_out, m_a_w_out, v_a_w_out, [buf_a[1]], [p_a[1]], "w_out")
    entries = [
        (gs[0], "full", a_norm, m_a_norm, v_a_norm),
        (gs[1], "cols", a_conv[0], m_a_conv[0], v_a_conv[0]),
        (gs[2], "cols", b_norm, m_b_norm, v_b_norm),
        (gs[3], "flat2", b_b_pw1, m_b_b_pw1, v_b_b_pw1),
        (gs[4], "cols", b_conv[0], m_b_conv[0], v_b_conv[0]),
        (gs[5], "cols", b_b_conv, m_b_b_conv, v_b_b_conv),
        (gs[6], "cols", b_ln_g, m_b_ln_g, v_b_ln_g),
        (gs[7], "cols", b_ln_b, m_b_ln_b, v_b_ln_b),
        (gs[8], "cols", b_b_pw2, m_b_b_pw2, v_b_b_pw2),
        (gs[9], "full", ffn_norm, m_ffn_norm, v_ffn_norm),
        (gs[10], "full", final_norm.reshape(1, D), m_final_norm.reshape(1, D), v_final_norm.reshape(1, D)),
    ]
    so = small_update(gall, chip, entries, "small_update")
    sm = [so[4 * e:4 * e + 4] for e in range(len(entries))]

    def shaped(e, like):
        return [t.reshape(like.shape) for t in sm[e]]

    r_a_norm, r_a_conv, r_b_norm, r_b_b_pw1 = shaped(0, a_norm), shaped(1, a_conv), shaped(2, b_norm), shaped(3, b_b_pw1)
    r_b_conv, r_b_b_conv, r_b_ln_g, r_b_ln_b = shaped(4, b_conv), shaped(5, b_b_conv), shaped(6, b_ln_g), shaped(7, b_ln_b)
    r_b_b_pw2, r_ffn_norm, r_final = shaped(8, b_b_pw2), shaped(9, ffn_norm), shaped(10, final_norm)

    loss = gall[gs[11], 0]
    order =[r_a_norm, r_in, r_a_conv, r_out, r_b_norm, r_pw1, r_b_b_pw1, r_b_conv, r_b_b_conv, r_b_ln_g, r_b_ln_b,
             r_pw2, r_b_b_pw2, r_ffn_norm, r_gate, r_up, r_down, r_final]
    outs = [loss, grad_x.reshape(x.shape)]
    for field in range(4):
        outs += [r[field] for r in order]
    return tuple(outs)
```

```python
import functools

import jax
import jax.numpy as jnp
from jax import lax
from jax.experimental import pallas as pl
from jax.experimental.pallas import tpu as pltpu

RMS_EPS = 1e-6
LN_EPS = 1e-5
ADAM_LR = 0.001
ADAM_B1 = 0.9
ADAM_B2 = 0.999
ADAM_EPS = 1e-08
ADAM_WD = 0.01
ADAM_STEP = 10

N_CHIPS = 4
N_DEV = 8
LANES = 128
SUBLANES = 8
HALO = 32
CONV_ROWS = 64
TOKEN_TILE = 512
WIDE_TOKEN_TILE = 1024
GRAD_TOKEN_TILE = 2048
FFN_ROW_CHUNKS = 2
FFN_FWD_SEGS_PER_STEP = 4
FFN_BWD_TOKEN_TILE = 256
ROW_TILE = 256
VMEM_LIMIT = 56 * 1024 * 1024
MESH = pl.DeviceIdType.MESH
BF16 = jnp.bfloat16
F32 = jnp.float32


def _tile(n, pref, mult=SUBLANES):
    t = min(n, pref) // mult * mult
    while n % t:
        t -= mult
    return t


def _params(sem):
    return pltpu.CompilerParams(dimension_semantics=sem, vmem_limit_bytes=VMEM_LIMIT)


def _sigmoid(x):
    return 0.5 * jnp.tanh(0.5 * x) + 0.5


class _Exchange:
    def __init__(self, ins, outs, aliases, n_sems, copies, then=None):
        self.ins, self.outs, self.aliases, self.n_sems, self.copies = list(ins), list(outs), dict(aliases), n_sems, copies
        self.then = then
        self.early = False

    def awaited_first(self):
        self.early = True
        return self

    def start(self, xi, xo, ssem, rsem):
        for cp in self.copies(xi, xo, ssem, rsem)[0]:
            cp.start()

    def finish(self, xi, xo, ssem, rsem):
        sends, recvs = self.copies(xi, xo, ssem, rsem)
        for cp in recvs:
            cp.wait_recv()
        if self.then is not None:
            sends2, recvs2 = self.then(xi, xo, ssem, rsem)
            for cp in sends2:
                cp.start()
            for cp in recvs2:
                cp.wait_recv()
            sends = sends + sends2
        for cp in sends:
            cp.wait_send()


def _call(body, name, grid, in_specs, out_specs, out_shape, args, sem, scratch_shapes=(), hosted=(), prefetch=(),
          own_aliases=None):
    in_specs, out_specs, out_shape = list(in_specs), list(out_specs), list(out_shape)
    scratch_shapes, hosted, prefetch = list(scratch_shapes), list(hosted), list(prefetch)
    n_pre, n_in, n_out, n_scr = len(prefetch), len(args), len(out_shape), len(scratch_shapes)
    x_in = [a for ex in hosted for a in ex.ins]
    x_out = [o for ex in hosted for o in ex.outs]
    aliases = {n_pre + i: o for i, o in (own_aliases or {}).items()}
    at_in, at_out = n_pre + n_in, n_out
    for ex in hosted:
        for i, o in ex.aliases.items():
            aliases[at_in + i] = at_out + o
        at_in += len(ex.ins)
        at_out += len(ex.outs)
    sems = [pltpu.SemaphoreType.DMA((ex.n_sems,)) for ex in hosted for _ in range(2)]

    def wrapped(*refs):
        pre, refs = refs[:n_pre], refs[n_pre:]
        ins, xi = refs[:n_in], refs[n_in:n_in + len(x_in)]
        refs = refs[n_in + len(x_in):]
        outs, xo = refs[:n_out], refs[n_out:n_out + len(x_out)]
        refs = refs[n_out + len(x_out):]
        scr, sm = refs[:n_scr], refs[n_scr:]
        views, a, b = [], 0, 0
        for e, ex in enumerate(hosted):
            views.append((xi[a:a + len(ex.ins)], xo[b:b + len(ex.outs)], sm[2 * e], sm[2 * e + 1]))
            a += len(ex.ins)
            b += len(ex.outs)
        first = last = None
        for ax, g in enumerate(grid):
            f, l = pl.program_id(ax) == 0, pl.program_id(ax) == g - 1
            first, last = (f, l) if first is None else (first & f, last & l)

        def begin():
            for ex, v in zip(hosted, views):
                ex.start(*v)
            for ex, v in zip(hosted, views):
                if ex.early:
                    ex.finish(*v)

        def end():
            for ex, v in zip(hosted, views):
                if not ex.early:
                    ex.finish(*v)

        if hosted and grid:
            pl.when(first)(begin)
        elif hosted:
            begin()
        early_refs = [r for ex, v in zip(hosted, views) if ex.early for r in v[1]]
        body(*pre, *ins, *outs, *scr, *early_refs)
        if hosted and grid:
            pl.when(last)(end)
        elif hosted:
            end()

    hbm = pl.BlockSpec(memory_space=pl.ANY)
    all_in, all_out = in_specs + [hbm] * len(x_in), out_specs + [hbm] * len(x_out)
    kw = dict(name=name, out_shape=out_shape + x_out, input_output_aliases=aliases,
              compiler_params=_params(tuple("arbitrary" for _ in grid) if hosted else sem))
    if prefetch:
        kw["grid_spec"] = pltpu.PrefetchScalarGridSpec(num_scalar_prefetch=n_pre, grid=grid, in_specs=all_in,
                                                       out_specs=all_out, scratch_shapes=scratch_shapes + sems)
    else:
        kw.update(grid=grid, in_specs=all_in, out_specs=all_out, scratch_shapes=scratch_shapes + sems)
    res = pl.pallas_call(wrapped, **kw)(*prefetch, *args, *x_in)
    return list(res[:n_out]), list(res[n_out:])


def rms_fwd(h, gain, name, hosted=()):
    T, D = h.shape
    tm = _tile(T, TOKEN_TILE)

    def body(h_ref, g_ref, o_ref):
        x = h_ref[...]
        r = lax.rsqrt(jnp.mean(x * x, axis=-1, keepdims=True) + RMS_EPS)
        o_ref[...] = (x * r * g_ref[...]).astype(o_ref.dtype)

    (n,), xo = _call(
        body, name, (T // tm,),
        [pl.BlockSpec((tm, D), lambda i: (i, 0)), pl.BlockSpec((1, D), lambda i: (0, 0))],
        [pl.BlockSpec((tm, D), lambda i: (i, 0))], [jax.ShapeDtypeStruct((T, D), BF16)],
        [h, gain], ("parallel",), hosted=hosted)
    return n, xo


def loss_head(h, gain, tgt, name):
    T, D = h.shape
    tm = _tile(T, TOKEN_TILE)

    def body(h_ref, g_ref, t_ref, loss_ref, dh_ref, dg_ref):
        i = pl.program_id(0)
        x = h_ref[...]
        g = g_ref[...]
        r = lax.rsqrt(jnp.mean(x * x, axis=-1, keepdims=True) + RMS_EPS)
        xhat = x * r
        diff = xhat * g - t_ref[...]
        part_loss = 0.5 * jnp.sum(jnp.mean(diff * diff, axis=-1, keepdims=True), axis=0, keepdims=True)
        dy = diff * (1.0 / D)
        dxhat = dy * g
        dh_ref[...] = r * (dxhat - xhat * jnp.mean(dxhat * xhat, axis=-1, keepdims=True))
        part = jnp.sum(dy * xhat, axis=0, keepdims=True)

        @pl.when(i == 0)
        def _():
            dg_ref[...] = part
            loss_ref[...] = part_loss

        @pl.when(i > 0)
        def _():
            dg_ref[...] += part
            loss_ref[...] += part_loss

    row = pl.BlockSpec((tm, D), lambda i: (i, 0))
    vec = pl.BlockSpec((1, D), lambda i: (0, 0))
    return pl.pallas_call(
        body, name=name, grid=(T // tm,),
        in_specs=[row, vec, row],
        out_specs=[pl.BlockSpec((1, 1), lambda i: (0, 0)), row, vec],
        out_shape=[jax.ShapeDtypeStruct((1, 1), F32), jax.ShapeDtypeStruct((T, D), F32),
                   jax.ShapeDtypeStruct((1, D), F32)],
        compiler_params=_params(("arbitrary",)),
    )(h, gain, tgt)


def _prev_halo_spec(tm, width):
    return pl.BlockSpec((HALO, width), lambda i: (jnp.maximum(i * (tm // HALO) - 1, 0), 0))


def _next_halo_spec(tm, width, T):
    return pl.BlockSpec((HALO, width), lambda i: (jnp.minimum((i + 1) * (tm // HALO), T // HALO - 1), 0))


def _shifted(win, off, rows):
    if off % SUBLANES == 0:
        return win[off:off + rows]
    n = win.shape[0]
    return pltpu.roll(win, (n - off) % n, 0)[:rows]


def _rowsum8(x):
    acc = x[0:SUBLANES]
    for q in range(1, x.shape[0] // SUBLANES):
        acc = acc + x[q * SUBLANES:(q + 1) * SUBLANES]
    return acc


def _conv_loops(tm, D, per_block):
    def chunk(r, carry):
        t0 = pl.multiple_of(r * CONV_ROWS, CONV_ROWS)
        for lb in range(D // LANES):
            per_block(t0, slice(lb * LANES, (lb + 1) * LANES))
        return carry

    lax.fori_loop(0, tm // CONV_ROWS, chunk, 0)


def gateconv_fwd(bcv, w, w_out, res, name, hosted=()):
    T, D3 = bcv.shape
    D = D3 // 3
    K = w.shape[0]
    tm = _tile(T, TOKEN_TILE)
    wo_shape = w_out.outs[0].shape

    def body(x_ref, halo_ref, w_ref, res_ref, y_ref, h_ref, pad_ref, wo_v, sem, wo_hbm):
        i = pl.program_id(0)

        @pl.when(i == 0)
        def _():
            cp = pltpu.make_async_copy(wo_hbm, wo_v, sem)
            cp.start()
            cp.wait()

        pad_ref[HALO:, :] = x_ref[:, D:2 * D] * x_ref[:, 2 * D:]
        pad_ref[:HALO, :] = jnp.where(i > 0, halo_ref[:, D:2 * D] * halo_ref[:, 2 * D:], 0.0)

        def block(t0, ls):
            win = pad_ref[pl.ds(t0, CONV_ROWS + HALO), ls]
            acc = jnp.zeros((CONV_ROWS, LANES), F32)
            for k in range(K):
                acc = acc + w_ref[k:k + 1, ls] * _shifted(win, HALO - (K - 1) + k, CONV_ROWS)
            y_ref[pl.ds(t0, CONV_ROWS), ls] = (x_ref[pl.ds(t0, CONV_ROWS), ls] * acc).astype(y_ref.dtype)

        _conv_loops(tm, D, block)
        h_ref[...] = res_ref[...] + jnp.dot(y_ref[...], wo_v[...].reshape(D, D), preferred_element_type=F32)

    row = pl.BlockSpec((tm, D), lambda i: (i, 0))
    (y, h), xo = _call(
        body, name, (T // tm,),
        [pl.BlockSpec((tm, D3), lambda i: (i, 0)), _prev_halo_spec(tm, D3), pl.BlockSpec((K, D), lambda i: (0, 0)), row],
        [row, row], [jax.ShapeDtypeStruct((T, D), BF16), jax.ShapeDtypeStruct((T, D), F32)],
        [bcv, bcv, w, res], ("arbitrary",),
        [pltpu.VMEM((tm + HALO, D), F32), pltpu.VMEM(wo_shape, BF16), pltpu.SemaphoreType.DMA],
        hosted=[w_out.awaited_first()] + list(hosted))
    return y, h, xo


def gateconv_bwd(dy, bcv, w, name, hosted=()):
    T, D3 = bcv.shape
    D = D3 // 3
    K = w.shape[0]
    tm = _tile(T, TOKEN_TILE)
    nt = T // tm

    def body(dy_ref, dyn_ref, x_ref, xp_ref, xn_ref, w_ref, o_ref, dw_ref, cv_ref, dc_ref, wacc_ref):
        i = pl.program_id(0)
        cv_ref[HALO:, :] = x_ref[:, D:2 * D] * x_ref[:, 2 * D:]
        cv_ref[:HALO, :] = jnp.where(i > 0, xp_ref[:, D:2 * D] * xp_ref[:, 2 * D:], 0.0)
        dc_ref[:tm, :] = dy_ref[...] * x_ref[:, :D]
        dc_ref[tm:, :] = jnp.where(i < nt - 1, dyn_ref[...] * xn_ref[:, :D], 0.0)

        @pl.when(i == 0)
        def _():
            wacc_ref[...] = jnp.zeros_like(wacc_ref)

        def block(t0, ls):
            cwin = cv_ref[pl.ds(t0, CONV_ROWS + HALO), ls]
            dwin = dc_ref[pl.ds(t0, CONV_ROWS + HALO), ls]
            dcon = dwin[:CONV_ROWS]
            conv = jnp.zeros((CONV_ROWS, LANES), F32)
            dcv = jnp.zeros((CONV_ROWS, LANES), F32)
            for k in range(K):
                wk = w_ref[k:k + 1, ls]
                cs = _shifted(cwin, HALO - (K - 1) + k, CONV_ROWS)
                conv = conv + wk * cs
                dcv = dcv + wk * _shifted(dwin, (K - 1) - k, CONV_ROWS)
                wacc_ref[k * SUBLANES:(k + 1) * SUBLANES, ls] += _rowsum8(dcon * cs)
            rows = pl.ds(t0, CONV_ROWS)
            o_ref[rows, ls] = (dy_ref[rows, ls] * conv).astype(o_ref.dtype)
            o_ref[rows, pl.ds(D + ls.start, LANES)] = (dcv * x_ref[rows, pl.ds(2 * D + ls.start, LANES)]).astype(o_ref.dtype)
            o_ref[rows, pl.ds(2 * D + ls.start, LANES)] = (dcv * x_ref[rows, pl.ds(D + ls.start, LANES)]).astype(o_ref.dtype)

        _conv_loops(tm, D, block)

        @pl.when(i == nt - 1)
        def _():
            for k in range(K):
                dw_ref[k:k + 1, :] = jnp.sum(wacc_ref[k * SUBLANES:(k + 1) * SUBLANES, :], axis=0, keepdims=True)

    (dx, dw), xo = _call(
        body, name, (nt,),
        [pl.BlockSpec((tm, D), lambda i: (i, 0)), _next_halo_spec(tm, D, T),
         pl.BlockSpec((tm, D3), lambda i: (i, 0)), _prev_halo_spec(tm, D3), _next_halo_spec(tm, D3, T),
         pl.BlockSpec((K, D), lambda i: (0, 0))],
        [pl.BlockSpec((tm, D3), lambda i: (i, 0)), pl.BlockSpec((K, D), lambda i: (0, 0))],
        [jax.ShapeDtypeStruct((T, D3), BF16), jax.ShapeDtypeStruct((K, D), F32)],
        [dy, dy, bcv, bcv, bcv, w], ("arbitrary",),
        [pltpu.VMEM((tm + HALO, D), F32), pltpu.VMEM((tm + HALO, D), F32), pltpu.VMEM((K * SUBLANES, D), F32)],
        hosted=hosted)
    return dx, dw, xo


def bconv_fwd(u, w, b_conv, ln_g, ln_b, w_out, b_out, res, name, hosted=()):
    T, D2 = u.shape
    D = D2 // 2
    K = w.shape[0]
    tm = _tile(T, TOKEN_TILE)

    def body(u_ref, halo_ref, w_ref, bc_ref, g_ref, b_ref, wo_ref, bo_ref, res_ref, cu_ref, s_ref, h_ref, pad_ref):
        i = pl.program_id(0)
        pad_ref[HALO:, :] = u_ref[:, :D] * _sigmoid(u_ref[:, D:])
        pad_ref[:HALO, :] = jnp.where(i > 0, halo_ref[:, :D] * _sigmoid(halo_ref[:, D:]), 0.0)

        def block(t0, ls):
            win = pad_ref[pl.ds(t0, CONV_ROWS + HALO), ls]
            acc = jnp.zeros((CONV_ROWS, LANES), F32)
            for k in range(K):
                acc = acc + w_ref[k:k + 1, ls] * _shifted(win, HALO - (K - 1) + k, CONV_ROWS)
            cu_ref[pl.ds(t0, CONV_ROWS), ls] = acc + bc_ref[:, ls]

        _conv_loops(tm, D, block)
        cu = cu_ref[...]
        mu = jnp.mean(cu, axis=-1, keepdims=True)
        xc = cu - mu
        rstd = lax.rsqrt(jnp.mean(xc * xc, axis=-1, keepdims=True) + LN_EPS)
        ln = xc * rstd * g_ref[...] + b_ref[...]
        s = (ln * _sigmoid(ln)).astype(s_ref.dtype)
        s_ref[...] = s
        h_ref[...] = res_ref[...] + bo_ref[...] + jnp.dot(s, wo_ref[0], preferred_element_type=F32)

    vec = pl.BlockSpec((1, D), lambda i: (0, 0))
    row = pl.BlockSpec((tm, D), lambda i: (i, 0))
    (cu, s, h), xo = _call(
        body, name, (T // tm,),
        [pl.BlockSpec((tm, D2), lambda i: (i, 0)), _prev_halo_spec(tm, D2), pl.BlockSpec((K, D), lambda i: (0, 0)), vec, vec, vec,
         pl.BlockSpec((1, D, D), lambda i: (0, 0, 0)), vec, row],
        [row, row, row], [jax.ShapeDtypeStruct((T, D), F32), jax.ShapeDtypeStruct((T, D), BF16), jax.ShapeDtypeStruct((T, D), F32)],
        [u, u, w, b_conv, ln_g, ln_b, w_out, b_out, res], ("parallel",), [pltpu.VMEM((tm + HALO, D), F32)], hosted=hosted)
    return cu, s, h, xo


def pw2_ln_bwd(dy, w, cu, ln_g, ln_b, name, hosted=()):
    T, D = cu.shape
    tm = _tile(T, TOKEN_TILE)

    def body(dy_ref, w_ref, cu_ref, g_ref, b_ref, dcu_ref, dg_ref, db_ref, dbc_ref, dbo_ref):
        i = pl.program_id(0)
        dy_ = dy_ref[...]
        ds = lax.dot_general(dy_.astype(BF16), w_ref[0], _NT, preferred_element_type=F32)
        cu_ = cu_ref[...]
        mu = jnp.mean(cu_, axis=-1, keepdims=True)
        xc = cu_ - mu
        rstd = lax.rsqrt(jnp.mean(xc * xc, axis=-1, keepdims=True) + LN_EPS)
        xh = xc * rstd
        ln = xh * g_ref[...] + b_ref[...]
        sg = _sigmoid(ln)
        dl = ds * (sg * (1.0 + ln * (1.0 - sg)))
        dxh = dl * g_ref[...]
        dcu = rstd * (dxh - jnp.mean(dxh, axis=-1, keepdims=True) - xh * jnp.mean(dxh * xh, axis=-1, keepdims=True))
        dcu_ref[...] = dcu
        pg = jnp.sum(dl * xh, axis=0, keepdims=True)
        pb = jnp.sum(dl, axis=0, keepdims=True)
        pc = jnp.sum(dcu, axis=0, keepdims=True)
        po = jnp.sum(dy_, axis=0, keepdims=True)

        @pl.when(i == 0)
        def _():
            dg_ref[...] = pg
            db_ref[...] = pb
            dbc_ref[...] = pc
            dbo_ref[...] = po

        @pl.when(i > 0)
        def _():
            dg_ref[...] += pg
            db_ref[...] += pb
            dbc_ref[...] += pc
            dbo_ref[...] += po

    vec = pl.BlockSpec((1, D), lambda i: (0, 0))
    row = pl.BlockSpec((tm, D), lambda i: (i, 0))
    vshape = jax.ShapeDtypeStruct((1, D), F32)
    outs, xo = _call(
        body, name, (T // tm,), [row, pl.BlockSpec((1, D, D), lambda i: (0, 0, 0)), row, vec, vec], [row, vec, vec, vec, vec],
        [jax.ShapeDtypeStruct((T, D), F32), vshape, vshape, vshape, vshape], [dy, w, cu, ln_g, ln_b], ("arbitrary",),
        hosted=hosted)
    return (*outs, xo)


def bconv_bwd(dcu, u, w, name, hosted=()):
    T, D2 = u.shape
    D = D2 // 2
    K = w.shape[0]
    tm = _tile(T, TOKEN_TILE)
    nt = T // tm

    def body(dc_ref, dcn_ref, u_ref, up_ref, w_ref, du_ref, dw_ref, db_ref, glu_ref, dpad_ref, dglu_ref, wacc_ref):
        i = pl.program_id(0)
        glu_ref[HALO:, :] = u_ref[:, :D] * _sigmoid(u_ref[:, D:])
        glu_ref[:HALO, :] = jnp.where(i > 0, up_ref[:, :D] * _sigmoid(up_ref[:, D:]), 0.0)
        dpad_ref[:tm, :] = dc_ref[...]
        dpad_ref[tm:, :] = jnp.where(i < nt - 1, dcn_ref[...], 0.0)

        @pl.when(i == 0)
        def _():
            wacc_ref[...] = jnp.zeros_like(wacc_ref)

        def block(t0, ls):
            gwin = glu_ref[pl.ds(t0, CONV_ROWS + HALO), ls]
            dwin = dpad_ref[pl.ds(t0, CONV_ROWS + HALO), ls]
            dcur = dwin[:CONV_ROWS]
            dglu = jnp.zeros((CONV_ROWS, LANES), F32)
            for k in range(K):
                dglu = dglu + w_ref[k:k + 1, ls] * _shifted(dwin, (K - 1) - k, CONV_ROWS)
                gs = _shifted(gwin, HALO - (K - 1) + k, CONV_ROWS)
                wacc_ref[k * SUBLANES:(k + 1) * SUBLANES, ls] += _rowsum8(dcur * gs)
            dglu_ref[pl.ds(t0, CONV_ROWS), ls] = dglu

        _conv_loops(tm, D, block)
        dglu = dglu_ref[...]
        a = u_ref[:, :D]
        sg = _sigmoid(u_ref[:, D:])
        da = dglu * sg
        dg = dglu * a * (sg * (1.0 - sg))
        du_ref[:, :D] = da.astype(du_ref.dtype)
        du_ref[:, D:] = dg.astype(du_ref.dtype)
        pa = jnp.sum(da, axis=0, keepdims=True)
        pg = jnp.sum(dg, axis=0, keepdims=True)

        @pl.when(i == 0)
        def _():
            db_ref[:, :D] = pa
            db_ref[:, D:] = pg

        @pl.when(i > 0)
        def _():
            db_ref[:, :D] += pa
            db_ref[:, D:] += pg

        @pl.when(i == nt - 1)
        def _():
            for k in range(K):
                dw_ref[k:k + 1, :] = jnp.sum(wacc_ref[k * SUBLANES:(k + 1) * SUBLANES, :], axis=0, keepdims=True)

    (du, dw, db), xo = _call(
        body, name, (nt,),
        [pl.BlockSpec((tm, D), lambda i: (i, 0)), _next_halo_spec(tm, D, T),
         pl.BlockSpec((tm, D2), lambda i: (i, 0)), _prev_halo_spec(tm, D2), pl.BlockSpec((K, D), lambda i: (0, 0))],
        [pl.BlockSpec((tm, D2), lambda i: (i, 0)), pl.BlockSpec((K, D), lambda i: (0, 0)), pl.BlockSpec((1, D2), lambda i: (0, 0))],
        [jax.ShapeDtypeStruct((T, D2), BF16), jax.ShapeDtypeStruct((K, D), F32), jax.ShapeDtypeStruct((1, D2), F32)],
        [dcu, dcu, u, u, w], ("arbitrary",),
        [pltpu.VMEM((tm + HALO, D), F32), pltpu.VMEM((tm + HALO, D), F32), pltpu.VMEM((tm, D), F32),
         pltpu.VMEM((K * SUBLANES, D), F32)], hosted=hosted)
    return du, dw, db, xo


def mm_cols(a, w, bias, name, hosted=()):
    T, K = a.shape
    S, _, n = w.shape
    tm = _tile(T, WIDE_TOKEN_TILE)

    def body(*refs):
        a_ref, w_ref = refs[:2]
        o_ref = refs[-1]
        acc = jnp.dot(a_ref[...], w_ref[...], preferred_element_type=F32)
        if bias is not None:
            acc = acc + refs[2][...]
        o_ref[...] = acc

    in_specs = [pl.BlockSpec((tm, K), lambda s, i: (i, 0)), pl.BlockSpec((None, K, n), lambda s, i: (s, 0, 0))]
    args = [a, w]
    if bias is not None:
        in_specs.append(pl.BlockSpec((1, n), lambda s, i: (0, s)))
        args.append(bias)
    (out,), xo = _call(body, name, (S, T // tm), in_specs, [pl.BlockSpec((tm, n), lambda s, i: (i, s))],
                       [jax.ShapeDtypeStruct((T, S * n), F32)], args, ("parallel", "parallel"), hosted=hosted)
    return out, xo


def _load_weights(pairs, sems, S, G, i, p):
    def copies(seg):
        return [pltpu.make_async_copy(src.at[seg], dst.at[seg], sems.at[k, seg]) for k, (src, dst) in enumerate(pairs)]

    @pl.when((i == 0) & (p == 0))
    def _():
        for seg in range(S):
            for cp in copies(seg):
                cp.start()

    @pl.when((i == 0) & (p < S // G))
    def _():
        for j in range(G):
            for cp in copies(G * p + j):
                cp.wait()


def ffn_fwd(h, gain, weights, name, hosted=(), arriving=None):
    T, D = h.shape
    S, f, _ = weights[0].shape
    tm = _tile(T, TOKEN_TILE)
    rc = tm // FFN_ROW_CHUNKS
    chunks = [slice(r * rc, (r + 1) * rc) for r in range(FFN_ROW_CHUNKS)]
    G = FFN_FWD_SEGS_PER_STEP
    weights = list(weights)
    hosted = ([arriving.awaited_first()] if arriving is not None else []) + list(hosted)

    def body(h_ref, gain_ref, *refs):
        nw = len(weights)
        wg_hbm, wu_hbm, wd_hbm = list(refs[:nw]) + list(refs[nw + 9:])
        n_ref, g_ref, u_ref, gu_ref, o_ref, wg_v, wu_v, wd_v, sems = refs[nw:nw + 9]
        i, p = pl.program_id(0), pl.program_id(1)
        _load_weights([(wg_hbm, wg_v), (wu_hbm, wu_v), (wd_hbm, wd_v)], sems, S, G, i, p)

        @pl.when(p == 0)
        def _():
            x = h_ref[...]
            r = lax.rsqrt(jnp.mean(x * x, axis=-1, keepdims=True) + RMS_EPS)
            n_ref[...] = (x * r * gain_ref[...]).astype(n_ref.dtype)

        parts = []
        for rows in chunks:
            a = n_ref[rows, :]
            acc = None
            for j in range(G):
                seg = G * p + j
                g = lax.dot_general(a, wg_v[seg], _NT, preferred_element_type=F32)
                u = lax.dot_general(a, wu_v[seg], _NT, preferred_element_type=F32)
                gu = (g * _sigmoid(g) * u).astype(gu_ref.dtype)
                g_ref[j, rows, :] = g.astype(g_ref.dtype)
                u_ref[j, rows, :] = u.astype(u_ref.dtype)
                gu_ref[j, rows, :] = gu
                part = jnp.dot(gu, wd_v[seg], preferred_element_type=F32)
                acc = part if acc is None else acc + part
            parts.append(acc)

        @pl.when(p == 0)
        def _():
            for rows, part in zip(chunks, parts):
                o_ref[rows, :] = h_ref[rows, :] + part

        @pl.when(p > 0)
        def _():
            for rows, part in zip(chunks, parts):
                o_ref[rows, :] += part

    row = pl.BlockSpec((tm, D), lambda i, p: (i, 0))
    seg = pl.BlockSpec((G, tm, f), lambda i, p: (p, i, 0))
    hbm = pl.BlockSpec(memory_space=pl.ANY)
    segs = jax.ShapeDtypeStruct((S, T, f), BF16)
    outs, xo = _call(
        body, name, (T // tm, S // G),
        [row, pl.BlockSpec((1, D), lambda i, s: (0, 0))] + [hbm] * len(weights), [row, seg, seg, seg, row],
        [jax.ShapeDtypeStruct((T, D), BF16), segs, segs, segs, jax.ShapeDtypeStruct((T, D), F32)],
        [h, gain] + weights, ("arbitrary", "arbitrary"),
        [pltpu.VMEM((S, f, D), BF16), pltpu.VMEM((S, f, D), BF16), pltpu.VMEM((S, f, D), BF16), pltpu.SemaphoreType.DMA((3, S))],
        hosted=hosted)
    return (*outs, xo)


def ffn_bwd(dy, h, gain, g, u, wd, wg, wu, name, hosted=()):
    T, D = h.shape
    S, f, _ = wg.shape
    tm = _tile(T, FFN_BWD_TOKEN_TILE)
    nt = T // tm

    def body(dy_ref, h_ref, gain_ref, g_ref, u_ref, wd_hbm, wg_hbm, wu_hbm, dg_ref, du_ref, dh_ref, dgain_ref,
             wd_v, wg_v, wu_v, sems):
        i = pl.program_id(0)
        _load_weights([(wd_hbm, wd_v), (wg_hbm, wg_v), (wu_hbm, wu_v)], sems, S, S, i, 0)
        dy_ = dy_ref[...]
        dyb = dy_.astype(BF16)
        dn = None
        for j in range(S):
            dgu = lax.dot_general(dyb, wd_v[j], _NT, preferred_element_type=F32)
            gv = g_ref[j].astype(F32)
            sg = _sigmoid(gv)
            dg = (dgu * u_ref[j].astype(F32) * (sg * (1.0 + gv * (1.0 - sg)))).astype(dg_ref.dtype)
            du = (dgu * (gv * sg)).astype(du_ref.dtype)
            dg_ref[j] = dg
            du_ref[j] = du
            part = jnp.dot(dg, wg_v[j], preferred_element_type=F32) + jnp.dot(du, wu_v[j], preferred_element_type=F32)
            dn = part if dn is None else dn + part
        x = h_ref[...]
        r = lax.rsqrt(jnp.mean(x * x, axis=-1, keepdims=True) + RMS_EPS)
        xhat = x * r
        dxhat = dn * gain_ref[...]
        dh_ref[...] = dy_ + r * (dxhat - xhat * jnp.mean(dxhat * xhat, axis=-1, keepdims=True))
        pg = jnp.sum(dn * xhat, axis=0, keepdims=True)

        @pl.when(i == 0)
        def _():
            dgain_ref[...] = pg

        @pl.when(i > 0)
        def _():
            dgain_ref[...] += pg

    row = pl.BlockSpec((tm, D), lambda i: (i, 0))
    vec = pl.BlockSpec((1, D), lambda i: (0, 0))
    seg = pl.BlockSpec((S, tm, f), lambda i: (0, i, 0))
    hbm = pl.BlockSpec(memory_space=pl.ANY)
    segs = jax.ShapeDtypeStruct((S, T, f), BF16)
    outs, xo = _call(
        body, name, (nt,),
        [row, row, vec, seg, seg, hbm, hbm, hbm], [seg, seg, row, vec],
        [segs, segs, jax.ShapeDtypeStruct((T, D), F32), jax.ShapeDtypeStruct((1, D), F32)],
        [dy, h, gain, g, u, wd, wg, wu], ("arbitrary",),
        [pltpu.VMEM((S, f, D), BF16), pltpu.VMEM((S, f, D), BF16), pltpu.VMEM((S, f, D), BF16),
         pltpu.SemaphoreType.DMA((3, S))], hosted=hosted)
    return (*outs, xo)


_NT = (((1,), (1,)), ((), ()))
_TN = (((0,), (0,)), ((), ()))


def nt_rows(dy, w, name, hosted=()):
    T, N = dy.shape
    S, k, _ = w.shape
    tm = _tile(T, TOKEN_TILE)

    def body(dy_ref, w_ref, o_ref):
        o_ref[...] = lax.dot_general(dy_ref[...].astype(BF16), w_ref[...], _NT, preferred_element_type=F32)

    (out,), xo = _call(
        body, name, (T // tm, S),
        [pl.BlockSpec((tm, N), lambda i, s: (i, 0)), pl.BlockSpec((None, k, N), lambda i, s: (s, 0, 0))],
        [pl.BlockSpec((None, tm, k), lambda i, s: (s, i, 0))], [jax.ShapeDtypeStruct((S, T, k), F32)],
        [dy, w], ("parallel", "parallel"), hosted=hosted)
    return out, xo


def nt_cols_rms(dy, w, h, gain, dres, name, hosted=()):
    T, K = h.shape
    S, _, n = w.shape
    tm = _tile(T, TOKEN_TILE)

    def body(dy_ref, w_ref, h_ref, gain_ref, dres_ref, dh_ref, dgain_ref):
        i = pl.program_id(0)
        dn = None
        for s in range(S):
            part = lax.dot_general(dy_ref[:, s * n:(s + 1) * n], w_ref[s], _NT, preferred_element_type=F32)
            dn = part if dn is None else dn + part
        x = h_ref[...]
        r = lax.rsqrt(jnp.mean(x * x, axis=-1, keepdims=True) + RMS_EPS)
        xhat = x * r
        dxhat = dn * gain_ref[...]
        dh_ref[...] = dres_ref[...] + r * (dxhat - xhat * jnp.mean(dxhat * xhat, axis=-1, keepdims=True))
        pg = jnp.sum(dn * xhat, axis=0, keepdims=True)

        @pl.when(i == 0)
        def _():
            dgain_ref[...] = pg

        @pl.when(i > 0)
        def _():
            dgain_ref[...] += pg

    row = pl.BlockSpec((tm, K), lambda i: (i, 0))
    vec = pl.BlockSpec((1, K), lambda i: (0, 0))
    (dh, dgain), xo = _call(
        body, name, (T // tm,),
        [pl.BlockSpec((tm, S * n), lambda i: (i, 0)), pl.BlockSpec((S, K, n), lambda i: (0, 0, 0)), row, vec, row],
        [row, vec], [jax.ShapeDtypeStruct((T, K), F32), jax.ShapeDtypeStruct((1, K), F32)],
        [dy, w, h, gain, dres], ("arbitrary",), hosted=hosted)
    return dh, dgain, xo


def tn_grad(a, dy, S, a_by_seg, name, hosted=()):
    T = dy.shape[0] if dy.ndim == 2 else dy.shape[1]
    tt = _tile(T, GRAD_TOKEN_TILE)
    if a_by_seg:
        R = a.shape[1] // S if a.ndim == 2 else a.shape[2]
        C = dy.shape[1]
        a_spec = pl.BlockSpec((tt, R), lambda s, t: (t, s)) if a.ndim == 2 else pl.BlockSpec((None, tt, R), lambda s, t: (s, t, 0))
        b_spec = pl.BlockSpec((tt, C), lambda s, t: (t, 0))
    else:
        R = a.shape[1]
        C = dy.shape[1] // S if dy.ndim == 2 else dy.shape[2]
        a_spec = pl.BlockSpec((tt, R), lambda s, t: (t, 0))
        b_spec = pl.BlockSpec((tt, C), lambda s, t: (t, s)) if dy.ndim == 2 else pl.BlockSpec((None, tt, C), lambda s, t: (s, t, 0))
    Rh = R // 2
    nt = T // tt

    def body(a_ref, b_ref, o_ref, acc_ref):
        t = pl.program_id(1)
        part = lax.dot_general(a_ref[...], b_ref[...].astype(BF16), _TN, preferred_element_type=F32)

        @pl.when(t == 0)
        def _():
            acc_ref[...] = part

        @pl.when(t > 0)
        def _():
            acc_ref[...] += part

        @pl.when(t == nt - 1)
        def _():
            o_ref[0] = acc_ref[:Rh, :].astype(o_ref.dtype)
            o_ref[1] = acc_ref[Rh:, :].astype(o_ref.dtype)

    (gh,), xo = _call(
        body, name, (S, nt), [a_spec, b_spec], [pl.BlockSpec((2, None, Rh, C), lambda s, t: (0, s, 0, 0))],
        [jax.ShapeDtypeStruct((2, S, Rh, C), BF16)], [a, dy], ("parallel", "arbitrary"), [pltpu.VMEM((R, C), F32)],
        hosted=hosted)
    return gh, xo


def tn_grad_square(a, dy, S, name, hosted=()):
    T, K = a.shape
    N = dy.shape[1]
    tt = _tile(T, GRAD_TOKEN_TILE)
    nt = T // tt
    Rh = K // S // 2

    def body(a_ref, b_ref, o_ref, acc_ref):
        t = pl.program_id(0)
        part = lax.dot_general(a_ref[...], b_ref[...].astype(BF16), _TN, preferred_element_type=F32)

        @pl.when(t == 0)
        def _():
            acc_ref[...] = part

        @pl.when(t > 0)
        def _():
            acc_ref[...] += part

        @pl.when(t == nt - 1)
        def _():
            for s in range(S):
                for hf in range(2):
                    r0 = (2 * s + hf) * Rh
                    o_ref[hf, s] = acc_ref[r0:r0 + Rh, :].astype(o_ref.dtype)

    (gh,), xo = _call(
        body, name, (nt,), [pl.BlockSpec((tt, K), lambda t: (t, 0)), pl.BlockSpec((tt, N), lambda t: (t, 0))],
        [pl.BlockSpec((2, S, Rh, N), lambda t: (0, 0, 0, 0))], [jax.ShapeDtypeStruct((2, S, Rh, N), BF16)],
        [a, dy], ("arbitrary",), [pltpu.VMEM((K, N), F32)], hosted=hosted)
    return gh, xo


def _place():
    x, y, c = lax.axis_index("x"), lax.axis_index("y"), lax.axis_index("c")
    chips = [(1 - x, y), (x, 1 - y), (1 - x, 1 - y)]
    return x, y, c, chips


def _remote(src, dst, send_sem, recv_sem, dev):
    return pltpu.make_async_remote_copy(src_ref=src, dst_ref=dst, send_sem=send_sem, recv_sem=recv_sem,
                                        device_id=dev, device_id_type=MESH)


def small_allreduce(v, name, hosted=()):
    rows, W = v.shape

    def body(v_ref, o_ref, sib_ref, pair_ref, chips_ref, send_sems, recv_sems):
        x, y, c, chips = _place()
        me = 2 * x + y
        swap = _remote(v_ref, sib_ref, send_sems.at[3], recv_sems.at[3], (x, y, 1 - c))
        swap.start()
        swap.wait()
        mine, other = v_ref[...], sib_ref[...]
        pair_ref[...] = jnp.where(c == 0, mine, other) + jnp.where(c == 0, other, mine)
        sends = []
        for j, (px, py) in enumerate(chips):
            cp = _remote(pair_ref, chips_ref.at[me], send_sems.at[j], recv_sems.at[j], (px, py, c))
            cp.start()
            sends.append(cp)
        chips_ref[me] = pair_ref[...]
        for j, (px, py) in enumerate(chips):
            blk = chips_ref.at[2 * px + py]
            _remote(blk, blk, send_sems.at[j], recv_sems.at[j], (px, py, c)).wait_recv()
        for cp in sends:
            cp.wait_send()
        o_ref[...] = (chips_ref[0] + chips_ref[1]) + (chips_ref[2] + chips_ref[3])

    vm = pl.BlockSpec(memory_space=pltpu.VMEM)
    (out,), xo = _call(
        body, name, (), [vm], [vm], [jax.ShapeDtypeStruct((rows, W), F32)], [v], (),
        [pltpu.VMEM((rows, W), F32), pltpu.VMEM((rows, W), F32), pltpu.VMEM((N_CHIPS, rows, W), F32),
         pltpu.SemaphoreType.DMA((4,)), pltpu.SemaphoreType.DMA((4,))], hosted=hosted)
    return out, xo


def _gather_p1_copies(srcs, bufs, ssem, rsem, base):
    x, y, c, chips = _place()
    me, sib = 2 * x + y, (x, y, 1 - c)
    sends, recvs = [], []
    for k, (src, buf) in enumerate(zip(srcs, bufs)):
        rh = src.shape[0] // 2
        s0 = base + 4 * k
        sends.append(_remote(src, buf.at[me], ssem.at[s0 + 3], rsem.at[s0 + 3], sib))
        recvs.append(_remote(buf.at[me], buf.at[me], ssem.at[s0 + 3], rsem.at[s0 + 3], sib))
        for j, (px, py) in enumerate(chips):
            sends.append(_remote(src.at[pl.ds(c * rh, rh)], buf.at[me, pl.ds(c * rh, rh)], ssem.at[s0 + j], rsem.at[s0 + j], (px, py, c)))
            blk = buf.at[2 * px + py, pl.ds(c * rh, rh)]
            recvs.append(_remote(blk, blk, ssem.at[s0 + j], rsem.at[s0 + j], (px, py, c)))
    return sends, recvs


def _gather_p2_copies(bufs, ssem, rsem, base):
    x, y, c, chips = _place()
    sib = (x, y, 1 - c)
    sends, recvs = [], []
    for k, buf in enumerate(bufs):
        rh = buf.shape[1] // 2
        for j, (px, py) in enumerate(chips):
            s0 = base + 3 * k + j
            blk = buf.at[2 * px + py, pl.ds(c * rh, rh)]
            sends.append(_remote(blk, blk, ssem.at[s0], rsem.at[s0], sib))
            got = buf.at[2 * px + py, pl.ds((1 - c) * rh, rh)]
            recvs.append(_remote(got, got, ssem.at[s0], rsem.at[s0], sib))
    return sends, recvs


def _gathered_shape(s):
    return jax.ShapeDtypeStruct((N_CHIPS,) + s.shape, s.dtype)


def gather_p1(shards):
    return _Exchange(shards, [_gathered_shape(s) for s in shards], {}, 4 * len(shards),
                     lambda xi, xo, ss, rs: _gather_p1_copies(xi, xo, ss, rs, 0))


def gather_p2(bufs):
    return _Exchange(bufs, [jax.ShapeDtypeStruct(b.shape, b.dtype) for b in bufs], {k: k for k in range(len(bufs))},
                     3 * len(bufs), lambda xi, xo, ss, rs: _gather_p2_copies(xo, ss, rs, 0))


def gather_whole(whole, begun):
    nw, n = len(whole), len(whole) + len(begun)
    shards = list(whole) + list(begun)
    return _Exchange(shards, [_gathered_shape(s) for s in shards], {}, 4 * n + 3 * nw,
                     lambda xi, xo, ss, rs: _gather_p1_copies(xi, xo, ss, rs, 0),
                     then=lambda xi, xo, ss, rs: _gather_p2_copies(xo[:nw], ss, rs, 4 * n))


def gather_small(v):
    def copies(xi, xo, ssem, rsem):
        x, y, c, chips = _place()
        me, sib = 2 * x + y, (x, y, 1 - c)
        sends = [_remote(xi[0], xo[0].at[me], ssem.at[3], rsem.at[3], sib)]
        recvs = [_remote(xo[0].at[me], xo[0].at[me], ssem.at[3], rsem.at[3], sib)]
        for j, (px, py) in enumerate(chips):
            sends.append(_remote(xi[0], xo[0].at[me], ssem.at[j], rsem.at[j], (px, py, c)))
            blk = xo[0].at[2 * px + py]
            recvs.append(_remote(blk, blk, ssem.at[j], rsem.at[j], (px, py, c)))
        return sends, recvs

    return _Exchange([v], [_gathered_shape(v)], {}, 4, copies)


def run_exchanges(exchanges, name):
    return _call(lambda: None, name, (), [], [], [], [], (), hosted=exchanges)[1]


def sibling_halves(grads):
    def copies(xi, xo, ssem, rsem):
        x, y, c, _ = _place()
        sends = [_remote(xi[k].at[1 - c], xo[k], ssem.at[k], rsem.at[k], (x, y, 1 - c)) for k in range(len(grads))]
        return sends, sends

    return _Exchange(grads, [jax.ShapeDtypeStruct(g.shape[1:], g.dtype) for g in grads], {}, len(grads), copies)


def pair_sum(gh, recv, cidx, name):
    _, S, Rh, C = gh.shape

    def body(c_ref, a_ref, b_ref, o_ref):
        o_ref[...] = (a_ref[...].astype(F32) + b_ref[...].astype(F32)).astype(o_ref.dtype)

    return pl.pallas_call(
        body, name=name, out_shape=jax.ShapeDtypeStruct((S, Rh, C), BF16),
        grid_spec=pltpu.PrefetchScalarGridSpec(
            num_scalar_prefetch=1, grid=(S,),
            in_specs=[pl.BlockSpec((None, None, Rh, C), lambda s, c_ref: (c_ref[0], s, 0, 0)),
                      pl.BlockSpec((None, Rh, C), lambda s, c_ref: (s, 0, 0))],
            out_specs=pl.BlockSpec((None, Rh, C), lambda s, c_ref: (s, 0, 0))),
        compiler_params=_params(("parallel",)),
    )(cidx, gh, recv)


def scatter_p1(parts):
    def copies(xi, xo, ssem, rsem):
        x, y, c, chips = _place()
        me, sib = 2 * x + y, (x, y, 1 - c)
        sends, recvs = [], []
        for k in range(len(parts)):
            s0 = 4 * k
            sends.append(_remote(xi[k].at[me], xo[k].at[me, c], ssem.at[s0 + 3], rsem.at[s0 + 3], sib))
            own = xo[k].at[me, 1 - c]
            recvs.append(_remote(own, own, ssem.at[s0 + 3], rsem.at[s0 + 3], sib))
            for j, (px, py) in enumerate(chips):
                sends.append(_remote(xi[k].at[2 * px + py], xo[k].at[me, c], ssem.at[s0 + j], rsem.at[s0 + j], (px, py, c)))
                blk = xo[k].at[2 * px + py, c]
                recvs.append(_remote(blk, blk, ssem.at[s0 + j], rsem.at[s0 + j], (px, py, c)))
        return sends, recvs

    return _Exchange(parts, [jax.ShapeDtypeStruct((p.shape[0], 2) + p.shape[1:], p.dtype) for p in parts], {},
                     4 * len(parts), copies)


def scatter_p2(bufs):
    def copies(xi, xo, ssem, rsem):
        x, y, c, chips = _place()
        sib = (x, y, 1 - c)
        sends, recvs = [], []
        for k in range(len(bufs)):
            for j, (px, py) in enumerate(chips):
                s0 = 3 * k + j
                blk = xo[k].at[2 * px + py, c]
                sends.append(_remote(blk, blk, ssem.at[s0], rsem.at[s0], sib))
                got = xo[k].at[2 * px + py, 1 - c]
                recvs.append(_remote(got, got, ssem.at[s0], rsem.at[s0], sib))
        return sends, recvs

    return _Exchange(bufs, [jax.ShapeDtypeStruct(b.shape, b.dtype) for b in bufs], {k: k for k in range(len(bufs))},
                     3 * len(bufs), copies)


def _adamw_math(w, g, m, v):
    m = ADAM_B1 * m + (1.0 - ADAM_B1) * g
    v = ADAM_B2 * v + (1.0 - ADAM_B2) * (g * g)
    m_hat = m / (1.0 - ADAM_B1 ** ADAM_STEP)
    v_hat = v / (1.0 - ADAM_B2 ** ADAM_STEP)
    delta = -ADAM_LR * (m_hat / (jnp.sqrt(v_hat) + ADAM_EPS) + ADAM_WD * w)
    return delta, m, v


def adamw_reduce(w, m, v, buf, part, place, lyr, bases, name, hosted=()):
    L, R, C = w.shape
    Rh = R // 2
    rb = _tile(Rh, ROW_TILE, 2 * SUBLANES)
    nb = Rh // rb

    def body(place_ref, p_ref, b0, b1, b2, b3, w_ref, m_ref, v_ref, *rest):
        go_ref, d_ref, mo_ref, vo_ref = rest[-4:]
        mine = (place_ref[1] == pl.program_id(0))
        g = None
        for p, b in enumerate((b0, b1, b2, b3)):
            val = jnp.where(mine & (place_ref[0] == p), p_ref[...], b[...]).astype(F32)
            g = val if g is None else g + val
        d, mn, vn = _adamw_math(w_ref[...], g, m_ref[...], v_ref[...])
        go_ref[...] = g
        d_ref[...] = d
        mo_ref[...] = mn
        vo_ref[...] = vn

    def buf_spec(p):
        def idx(h, i, pr):
            own = (pr[0] == p) & (pr[1] == h)
            return (p, jnp.where(own, 1 - h, h), i, 0)
        return pl.BlockSpec((None, None, rb, C), idx)

    blk = pl.BlockSpec((None, rb, C), lambda h, i, pr: (lyr, h * nb + i, 0))
    in_specs = [pl.BlockSpec((None, rb, C), lambda h, i, pr: (pr[0], i, 0))] + [buf_spec(p) for p in range(N_CHIPS)] + [blk] * 3
    args = [part, buf, buf, buf, buf, w, m, v]
    aliases = {}
    if bases is not None:
        in_specs += [pl.BlockSpec(memory_space=pl.ANY)] * 4
        aliases = {len(args) + k: k for k in range(4)}
        args += list(bases)
    shp = jax.ShapeDtypeStruct((L, R, C), F32)
    return _call(body, name, (2, nb), in_specs, [blk] * 4, [shp] * 4, args, ("parallel", "parallel"),
                 hosted=hosted, prefetch=[place], own_aliases=aliases)


def small_update(gall, chip, entries, name):
    ne = len(entries)
    D = gall.shape[1]

    def body(chip_ref, gall_ref, *refs):
        ins, outs = refs[:3 * ne], refs[3 * ne:]
        ch = chip_ref[0]
        for e, (row0, kind, w, _, _) in enumerate(entries):
            r, width = w.shape

            def gsum(rs, cs):
                return gall_ref[rs, cs]

            if kind == "full":
                g = gsum(slice(row0, row0 + r), slice(0, D))
            elif kind == "cols":
                g = gsum(slice(row0, row0 + r), slice(0, width))
                for q in range(1, N_CHIPS):
                    g = jnp.where(ch == q, gsum(slice(row0, row0 + r), slice(q * width, (q + 1) * width)), g)
            else:
                per_row = D // width
                g = gsum(slice(row0, row0 + 1), slice(0, width))
                for q in range(1, N_CHIPS):
                    rr = row0 + q // per_row
                    cc = (q % per_row) * width
                    g = jnp.where(ch == q, gsum(slice(rr, rr + 1), slice(cc, cc + width)), g)
            d, mn, vn = _adamw_math(ins[3 * e][...], g, ins[3 * e + 1][...], ins[3 * e + 2][...])
            outs[4 * e][...] = g
            outs[4 * e + 1][...] = d
            outs[4 * e + 2][...] = mn
            outs[4 * e + 3][...] = vn

    vm = pl.BlockSpec(memory_space=pltpu.VMEM)
    args, out_shape = [], []
    for _, _, w, m, v in entries:
        args += [w, m, v]
        out_shape += [jax.ShapeDtypeStruct(w.shape, F32)] * 4
    return pl.pallas_call(
        body, name=name,
        in_specs=[pl.BlockSpec(memory_space=pltpu.SMEM), vm] + [vm] * (3 * ne),
        out_specs=[vm] * (4 * ne), out_shape=out_shape,
        compiler_params=pltpu.CompilerParams(vmem_limit_bytes=VMEM_LIMIT),
    )(chip, gall, *args)


def _pack_rows(items, width, name):
    starts, at = [], 0
    for it in items:
        starts.append(at)
        at += -(-it.shape[0] // SUBLANES) * SUBLANES
    total = at

    def body(*refs):
        o_ref = refs[-1]
        o_ref[...] = jnp.zeros_like(o_ref)
        for it_ref, r0 in zip(refs[:-1], starts):
            o_ref[r0:r0 + it_ref.shape[0], :] = it_ref[...]

    vm = pl.BlockSpec(memory_space=pltpu.VMEM)
    packed = pl.pallas_call(body, name=name, in_specs=[vm] * len(items), out_specs=vm,
                            out_shape=jax.ShapeDtypeStruct((total, width), F32))(*items)
    return packed, starts


def kernel(x, a_norm, a_w_in, a_conv, a_w_out, b_norm, b_w_pw1, b_b_pw1, b_conv, b_b_conv, b_ln_g, b_ln_b, b_w_pw2, b_b_pw2, ffn_norm, ffn_w_gate, ffn_w_up, ffn_w_down, final_norm, loss_target, m_a_norm, m_a_w_in, m_a_conv, m_a_w_out, m_b_norm, m_b_w_pw1, m_b_b_pw1, m_b_conv, m_b_b_conv, m_b_ln_g, m_b_ln_b, m_b_w_pw2, m_b_b_pw2, m_ffn_norm, m_ffn_w_gate, m_ffn_w_up, m_ffn_w_down, m_final_norm, v_a_norm, v_a_w_in, v_a_conv, v_a_w_out, v_b_norm, v_b_w_pw1, v_b_b_pw1, v_b_conv, v_b_b_conv, v_b_ln_g, v_b_ln_b, v_b_w_pw2, v_b_b_pw2, v_ffn_norm, v_ffn_w_gate, v_ffn_w_up, v_ffn_w_down, v_final_norm):
    T, D = x.shape[1], x.shape[2]
    Dq = D // N_CHIPS
    cx, cy, cc = lax.axis_index("x"), lax.axis_index("y"), lax.axis_index("c")
    chip = (2 * cx + cy).astype(jnp.int32).reshape(1)
    cidx = cc.astype(jnp.int32).reshape(1)
    h0 = x.reshape(T, D)
    tgt = loss_target.reshape(T, D)

    small_shards = [a_conv[0], b_norm, b_b_pw1.reshape(2, Dq), b_conv[0], b_b_conv, b_ln_g, b_ln_b, b_b_pw2]
    packed, st = _pack_rows(small_shards, Dq, "pack_small")

    tr = lambda t: jnp.swapaxes(t, 1, 2)
    w_gate, m_gate, v_gate = tr(ffn_w_gate), tr(m_ffn_w_gate), tr(v_ffn_w_gate)
    w_up, m_up, v_up = tr(ffn_w_up), tr(m_ffn_w_up), tr(v_ffn_w_up)
    bf = lambda t: t.astype(BF16)
    s_in, s_out, s_pw1, s_pw2 = bf(a_w_in[0]), bf(a_w_out[0]), bf(b_w_pw1[0]), bf(b_w_pw2[0])
    s_gate, s_up, s_down = [bf(w_gate[l]) for l in (0, 1)], [bf(w_up[l]) for l in (0, 1)], [bf(ffn_w_down[l]) for l in (0, 1)]

    n0, (g_in,) = rms_fwd(h0, a_norm, "rms_a", hosted=[gather_whole([s_in], [])])
    bcv, (g_out, gate0, sw) = mm_cols(n0, g_in, None, "mm_w_in", hosted=[gather_p1([s_out, s_gate[0]]), gather_small(packed)])

    def whole(k, r):
        return jnp.transpose(sw[:, st[k]:st[k] + r, :], (1, 0, 2)).reshape(r, D)

    a_conv_f, b_norm_f = whole(0, 3), whole(1, 1)
    b_b_pw1_f = sw[:, st[2]:st[2] + 2, :].reshape(1, 2 * D)
    b_conv_f, b_b_conv_f, b_ln_g_f, b_ln_b_f, b_b_pw2_f = whole(3, b_conv.shape[1]), whole(4, 1), whole(5, 1), whole(6, 1), whole(7, 1)
    ya, h1, (g_out, up0, down0, gate0) = gateconv_fwd(bcv, a_conv_f, gather_p2([g_out]), h0, "gateconv_fwd",
                                                      hosted=[gather_p1([s_up[0], s_down[0]]), gather_p2([gate0])])
    g_out = g_out.reshape(1, D, D)
    n1, fg0, fu0, gu0, h2, (up0, down0, *later) = ffn_fwd(h1, ffn_norm[0:1], [gate0], "ffn_fwd0", arriving=gather_p2([up0, down0]),
                                                          hosted=[gather_p1([s_pw1, s_pw2, s_gate[1], s_up[1]])])
    n2, (g_pw1, g_pw2, gate1, up1) = rms_fwd(h2, b_norm_f, "rms_b", hosted=[gather_p2(later)])
    g_pw2 = g_pw2.reshape(1, D, D)
    ub, (down1,) = mm_cols(n2, g_pw1, b_b_pw1_f, "mm_pw1", hosted=[gather_p1([s_down[1]])])
    cu, sb, h3, (down1,) = bconv_fwd(ub, b_conv_f, b_b_conv_f, b_ln_g_f, b_ln_b_f, g_pw2, b_b_pw2_f, h2, "bconv_fwd",
                                     hosted=[gather_p2([down1])])
    n3, fg1, fu1, gu1, h4, _ = ffn_fwd(h3, ffn_norm[1:2], [gate1, up1, down1], "ffn_fwd1")
    loss_part, dh4, d_final = loss_head(h4, final_norm.reshape(1, D), tgt, "loss_head")

    place = jnp.concatenate([chip, cidx])

    def pair_sums(ghs, from_sib, tags):
        return [pair_sum(g, r, cidx, "pair_sum_" + t) for g, r, t in zip(ghs, from_sib, tags)]

    def upd(w, m, v, bufs, parts, tag, hosted=()):
        res, xo = None, []
        for lyr, (b, p) in enumerate(zip(bufs, parts)):
            res, xo_l = adamw_reduce(w, m, v, b, p, place, lyr, res, "adamw_%s%d" % (tag, lyr), hosted=hosted if lyr == 0 else ())
            xo += xo_l
        return res, xo

    dg1, du1, dh3, d_fn1, _ = ffn_bwd(dh4, h3, ffn_norm[1:2], fg1, fu1, down1, gate1, up1, "ffn_bwd1")
    gh_down1, _ = tn_grad(gu1, dh4, N_CHIPS, True, "tn_down1")
    gh_gate1, _ = tn_grad(dg1, n3, N_CHIPS, True, "tn_gate1")
    gh_up1, _ = tn_grad(du1, n3, N_CHIPS, True, "tn_up1")
    f1 = [gh_gate1, gh_up1, gh_down1]

    dcu, d_ln_g, d_ln_b, d_b_conv, d_b_pw2, sib_f1 = pw2_ln_bwd(dh3, g_pw2, cu, b_ln_g_f, b_ln_b_f, "pw2_ln_bwd",
                                                                hosted=[sibling_halves(f1)])
    p_f1 = pair_sums(f1, sib_f1, ["gate1", "up1", "down1"])
    gh_pw2, _ = tn_grad_square(sb, dh3, N_CHIPS, "tn_pw2")
    dub, d_bconv_w, d_b_pw1, buf_f1 = bconv_bwd(dcu, ub, b_conv_f, "bconv_bwd", hosted=[scatter_p1(p_f1)])
    gh_pw1, _ = tn_grad(n2, dub, N_CHIPS, False, "tn_pw1")
    b_grp = [gh_pw1, gh_pw2]
    dh2, d_b_norm, (*buf_f1, sib_pw1, sib_pw2) = nt_cols_rms(dub, g_pw1, h2, b_norm_f, dh3, "nt_pw1",
                                                             hosted=[scatter_p2(buf_f1), sibling_halves(b_grp)])
    sib_b = [sib_pw1, sib_pw2]
    p_b = pair_sums(b_grp, sib_b, ["pw1", "pw2"])

    dg0, du0, dh1, d_fn0, buf_b = ffn_bwd(dh2, h1, ffn_norm[0:1], fg0, fu0, down0, gate0, up0, "ffn_bwd0", hosted=[scatter_p1(p_b)])
    gh_down0, _ = tn_grad(gu0, dh2, N_CHIPS, True, "tn_down0")
    gh_gate0, (*buf_b, sib_down0) = tn_grad(dg0, n1, N_CHIPS, True, "tn_gate0",
                                            hosted=[scatter_p2(buf_b), sibling_halves([gh_down0])])
    p_down0 = pair_sums([gh_down0], [sib_down0], ["down0"])
    gh_up0, (buf_down0, sib_gate0) = tn_grad(du0, n1, N_CHIPS, True, "tn_up0",
                                             hosted=[scatter_p1(p_down0), sibling_halves([gh_gate0])])
    p_gate0 = pair_sums([gh_gate0], [sib_gate0], ["gate0"])
    dya, (buf_down0, sib_up0) = nt_rows(dh1, g_out, "nt_w_out",
                                        hosted=[scatter_p2([buf_down0]), sibling_halves([gh_up0])])
    p_up0 = pair_sums([gh_up0], [sib_up0], ["up0"])
    gh_out, _ = tn_grad_square(ya, dh1, N_CHIPS, "tn_w_out")
    dbcv, d_aconv_w, (buf_gate0, sib_out) = gateconv_bwd(dya[0], bcv, a_conv_f, "gateconv_bwd",
                                                         hosted=[scatter_p1(p_gate0), sibling_halves([gh_out])])
    p_out = pair_sums([gh_out], [sib_out], ["out"])
    gh_in, (buf_up0, buf_gate0) = tn_grad(n0, dbcv, N_CHIPS, False, "tn_w_in",
                                          hosted=[scatter_p1(p_up0), scatter_p2([buf_gate0])])
    sib_in = run_exchanges([sibling_halves([gh_in])], "reduce_in_siblings")
    p_in = pair_sums([gh_in], sib_in, ["in"])
    grad_x, d_a_norm, (buf_in, buf_out, buf_up0) = nt_cols_rms(
        dbcv, g_in, h0, a_norm, dh1, "nt_w_in", hosted=[scatter_p1(p_in + p_out), scatter_p2([buf_up0])])
    p_f0 = [p_gate0[0], p_up0[0], p_down0[0]]

    d_ffn_norm = jnp.concatenate([d_fn0, d_fn1], axis=0)
    small_grads = [d_a_norm, d_aconv_w, d_b_norm, d_b_pw1.reshape(2, D), d_bconv_w, d_b_conv, d_ln_g, d_ln_b, d_b_pw2,
                   d_ffn_norm, d_final, jnp.broadcast_to(loss_part, (1, D))]
    gpacked, gs = _pack_rows(small_grads, D, "pack_small_grads")
    gall, (buf_in, buf_out) = small_allreduce(gpacked, "allreduce_small_grads", hosted=[scatter_p2([buf_in, buf_out])])
    buf_a, p_a = [buf_in, buf_out], [p_in[0], p_out[0]]

    r_gate, _ = upd(w_gate, m_gate, v_gate, [buf_gate0, buf_f1[0]], [p_f0[0], p_f1[0]], "gate")
    r_up, _ = upd(w_up, m_up, v_up, [buf_up0, buf_f1[1]], [p_f0[1], p_f1[1]], "up")
    r_down, _ = upd(ffn_w_down, m_ffn_w_down, v_ffn_w_down, [buf_down0, buf_f1[2]], [p_f0[2], p_f1[2]], "down")
    r_gate, r_up = [tr(t) for t in r_gate], [tr(t) for t in r_up]
    r_pw1, _ = upd(b_w_pw1, m_b_w_pw1, v_b_w_pw1, [buf_b[0]], [p_b[0]], "pw1")
    r_pw2, _ = upd(b_w_pw2, m_b_w_pw2, v_b_w_pw2, [buf_b[1]], [p_b[1]], "pw2")
    r_in, _ = upd(a_w_in, m_a_w_in, v_a_w_in, [buf_a[0]], [p_a[0]], "w_in")
    r_out, _ = upd(a_w_out, m_a_w_out, v_a_w_out, [buf_a[1]], [p_a[1]], "w_out")
    entries = [
        (gs[0], "full", a_norm, m_a_norm, v_a_norm),
        (gs[1], "cols", a_conv[0], m_a_conv[0], v_a_conv[0]),
        (gs[2], "cols", b_norm, m_b_norm, v_b_norm),
        (gs[3], "flat2", b_b_pw1, m_b_b_pw1, v_b_b_pw1),
        (gs[4], "cols", b_conv[0], m_b_conv[0], v_b_conv[0]),
        (gs[5], "cols", b_b_conv, m_b_b_conv, v_b_b_conv),
        (gs[6], "cols", b_ln_g, m_b_ln_g, v_b_ln_g),
        (gs[7], "cols", b_ln_b, m_b_ln_b, v_b_ln_b),
        (gs[8], "cols", b_b_pw2, m_b_b_pw2, v_b_b_pw2),
        (gs[9], "full", ffn_norm, m_ffn_norm, v_ffn_norm),
        (gs[10], "full", final_norm.reshape(1, D), m_final_norm.reshape(1, D), v_final_norm.reshape(1, D)),
    ]
    so = small_update(gall, chip, entries, "small_update")
    sm = [so[4 * e:4 * e + 4] for e in range(len(entries))]

    def shaped(e, like):
        return [t.reshape(like.shape) for t in sm[e]]

    r_a_norm, r_a_conv, r_b_norm, r_b_b_pw1 = shaped(0, a_norm), shaped(1, a_conv), shaped(2, b_norm), shaped(3, b_b_pw1)
    r_b_conv, r_b_b_conv, r_b_ln_g, r_b_ln_b = shaped(4, b_conv), shaped(5, b_b_conv), shaped(6, b_ln_g), shaped(7, b_ln_b)
    r_b_b_pw2, r_ffn_norm, r_final = shaped(8, b_b_pw2), shaped(9, ffn_norm), shaped(10, final_norm)

    loss = gall[gs[11], 0]
    order =[r_a_norm, r_in, r_a_conv, r_out, r_b_norm, r_pw1, r_b_b_pw1, r_b_conv, r_b_b_conv, r_b_ln_g, r_b_ln_b,
             r_pw2, r_b_b_pw2, r_ffn_norm, r_gate, r_up, r_down, r_final]
    outs = [loss, grad_x.reshape(x.shape)]
    for field in range(4):
        outs += [r[field] for r in order]
    return tuple(outs)
```

```python
import functools

import jax
import jax.numpy as jnp
from jax import lax
from jax.experimental import pallas as pl
from jax.experimental.pallas import tpu as pltpu

RMS_EPS = 1e-6
LN_EPS = 1e-5
ADAM_LR = 0.001
ADAM_B1 = 0.9
ADAM_B2 = 0.999
ADAM_EPS = 1e-08
ADAM_WD = 0.01
ADAM_STEP = 10

N_CHIPS = 4
N_DEV = 8
LANES = 128
SUBLANES = 8
HALO = 32
CONV_ROWS = 64
TOKEN_TILE = 512
WIDE_TOKEN_TILE = 1024
GRAD_TOKEN_TILE = 2048
FFN_ROW_CHUNKS = 2
FFN_FWD_SEGS_PER_STEP = 4
FFN_BWD_TOKEN_TILE = 256
ROW_TILE = 256
VMEM_LIMIT = 56 * 1024 * 1024
MESH = pl.DeviceIdType.MESH
BF16 = jnp.bfloat16
F32 = jnp.float32


def _tile(n, pref, mult=SUBLANES):
    t = min(n, pref) // mult * mult
    while n % t:
        t -= mult
    return t


def _params(sem):
    return pltpu.CompilerParams(dimension_semantics=sem, vmem_limit_bytes=VMEM_LIMIT)


def _sigmoid(x):
    return 0.5 * jnp.tanh(0.5 * x) + 0.5


class _Exchange:
    def __init__(self, ins, outs, aliases, n_sems, copies, then=None):
        self.ins, self.outs, self.aliases, self.n_sems, self.copies = list(ins), list(outs), dict(aliases), n_sems, copies
        self.then = then
        self.early = False

    def awaited_first(self):
        self.early = True
        return self

    def start(self, xi, xo, ssem, rsem):
        for cp in self.copies(xi, xo, ssem, rsem)[0]:
            cp.start()

    def finish(self, xi, xo, ssem, rsem):
        sends, recvs = self.copies(xi, xo, ssem, rsem)
        for cp in recvs:
            cp.wait_recv()
        if self.then is not None:
            sends2, recvs2 = self.then(xi, xo, ssem, rsem)
            for cp in sends2:
                cp.start()
            for cp in recvs2:
                cp.wait_recv()
            sends = sends + sends2
        for cp in sends:
            cp.wait_send()


def _call(body, name, grid, in_specs, out_specs, out_shape, args, sem, scratch_shapes=(), hosted=(), prefetch=(),
          own_aliases=None):
    in_specs, out_specs, out_shape = list(in_specs), list(out_specs), list(out_shape)
    scratch_shapes, hosted, prefetch = list(scratch_shapes), list(hosted), list(prefetch)
    n_pre, n_in, n_out, n_scr = len(prefetch), len(args), len(out_shape), len(scratch_shapes)
    x_in = [a for ex in hosted for a in ex.ins]
    x_out = [o for ex in hosted for o in ex.outs]
    aliases = {n_pre + i: o for i, o in (own_aliases or {}).items()}
    at_in, at_out = n_pre + n_in, n_out
    for ex in hosted:
        for i, o in ex.aliases.items():
            aliases[at_in + i] = at_out + o
        at_in += len(ex.ins)
        at_out += len(ex.outs)
    sems = [pltpu.SemaphoreType.DMA((ex.n_sems,)) for ex in hosted for _ in range(2)]

    def wrapped(*refs):
        pre, refs = refs[:n_pre], refs[n_pre:]
        ins, xi = refs[:n_in], refs[n_in:n_in + len(x_in)]
        refs = refs[n_in + len(x_in):]
        outs, xo = refs[:n_out], refs[n_out:n_out + len(x_out)]
        refs = refs[n_out + len(x_out):]
        scr, sm = refs[:n_scr], refs[n_scr:]
        views, a, b = [], 0, 0
        for e, ex in enumerate(hosted):
            views.append((xi[a:a + len(ex.ins)], xo[b:b + len(ex.outs)], sm[2 * e], sm[2 * e + 1]))
            a += len(ex.ins)
            b += len(ex.outs)
        first = last = None
        for ax, g in enumerate(grid):
            f, l = pl.program_id(ax) == 0, pl.program_id(ax) == g - 1
            first, last = (f, l) if first is None else (first & f, last & l)

        def begin():
            for ex, v in zip(hosted, views):
                ex.start(*v)
            for ex, v in zip(hosted, views):
                if ex.early:
                    ex.finish(*v)

        def end():
            for ex, v in zip(hosted, views):
                if not ex.early:
                    ex.finish(*v)

        if hosted and grid:
            pl.when(first)(begin)
        elif hosted:
            begin()
        early_refs = [r for ex, v in zip(hosted, views) if ex.early for r in v[1]]
        body(*pre, *ins, *outs, *scr, *early_refs)
        if hosted and grid:
            pl.when(last)(end)
        elif hosted:
            end()

    hbm = pl.BlockSpec(memory_space=pl.ANY)
    all_in, all_out = in_specs + [hbm] * len(x_in), out_specs + [hbm] * len(x_out)
    kw = dict(name=name, out_shape=out_shape + x_out, input_output_aliases=aliases,
              compiler_params=_params(tuple("arbitrary" for _ in grid) if hosted else sem))
    if prefetch:
        kw["grid_spec"] = pltpu.PrefetchScalarGridSpec(num_scalar_prefetch=n_pre, grid=grid, in_specs=all_in,
                                                       out_specs=all_out, scratch_shapes=scratch_shapes + sems)
    else:
        kw.update(grid=grid, in_specs=all_in, out_specs=all_out, scratch_shapes=scratch_shapes + sems)
    res = pl.pallas_call(wrapped, **kw)(*prefetch, *args, *x_in)
    return list(res[:n_out]), list(res[n_out:])


def rms_fwd(h, gain, name, hosted=()):
    T, D = h.shape
    tm = _tile(T, TOKEN_TILE)

    def body(h_ref, g_ref, o_ref):
        x = h_ref[...]
        r = lax.rsqrt(jnp.mean(x * x, axis=-1, keepdims=True) + RMS_EPS)
        o_ref[...] = (x * r * g_ref[...]).astype(o_ref.dtype)

    (n,), xo = _call(
        body, name, (T // tm,),
        [pl.BlockSpec((tm, D), lambda i: (i, 0)), pl.BlockSpec((1, D), lambda i: (0, 0))],
        [pl.BlockSpec((tm, D), lambda i: (i, 0))], [jax.ShapeDtypeStruct((T, D), BF16)],
        [h, gain], ("parallel",), hosted=hosted)
    return n, xo


def loss_head(h, gain, tgt, name):
    T, D = h.shape
    tm = _tile(T, TOKEN_TILE)

    def body(h_ref, g_ref, t_ref, loss_ref, dh_ref, dhb_ref, dg_ref):
        i = pl.program_id(0)
        x = h_ref[...]
        g = g_ref[...]
        r = lax.rsqrt(jnp.mean(x * x, axis=-1, keepdims=True) + RMS_EPS)
        xhat = x * r
        diff = xhat * g - t_ref[...]
        part_loss = 0.5 * jnp.sum(jnp.mean(diff * diff, axis=-1, keepdims=True), axis=0, keepdims=True)
        dy = diff * (1.0 / D)
        dxhat = dy * g
        dh = r * (dxhat - xhat * jnp.mean(dxhat * xhat, axis=-1, keepdims=True))
        dh_ref[...] = dh
        dhb_ref[...] = dh.astype(dhb_ref.dtype)
        part = jnp.sum(dy * xhat, axis=0, keepdims=True)

        @pl.when(i == 0)
        def _():
            dg_ref[...] = part
            loss_ref[...] = part_loss

        @pl.when(i > 0)
        def _():
            dg_ref[...] += part
            loss_ref[...] += part_loss

    row = pl.BlockSpec((tm, D), lambda i: (i, 0))
    vec = pl.BlockSpec((1, D), lambda i: (0, 0))
    return pl.pallas_call(
        body, name=name, grid=(T // tm,),
        in_specs=[row, vec, row],
        out_specs=[pl.BlockSpec((1, 1), lambda i: (0, 0)), row, row, vec],
        out_shape=[jax.ShapeDtypeStruct((1, 1), F32), jax.ShapeDtypeStruct((T, D), F32),
                   jax.ShapeDtypeStruct((T, D), BF16), jax.ShapeDtypeStruct((1, D), F32)],
        compiler_params=_params(("arbitrary",)),
    )(h, gain, tgt)


def _prev_halo_spec(tm, width):
    return pl.BlockSpec((HALO, width), lambda i: (jnp.maximum(i * (tm // HALO) - 1, 0), 0))


def _next_halo_spec(tm, width, T):
    return pl.BlockSpec((HALO, width), lambda i: (jnp.minimum((i + 1) * (tm // HALO), T // HALO - 1), 0))


def _shifted(win, off, rows):
    if off % SUBLANES == 0:
        return win[off:off + rows]
    n = win.shape[0]
    return pltpu.roll(win, (n - off) % n, 0)[:rows]


def _rowsum8(x):
    acc = x[0:SUBLANES]
    for q in range(1, x.shape[0] // SUBLANES):
        acc = acc + x[q * SUBLANES:(q + 1) * SUBLANES]
    return acc


def _conv_loops(tm, D, per_block):
    def chunk(r, carry):
        t0 = pl.multiple_of(r * CONV_ROWS, CONV_ROWS)
        for lb in range(D // LANES):
            per_block(t0, slice(lb * LANES, (lb + 1) * LANES))
        return carry

    lax.fori_loop(0, tm // CONV_ROWS, chunk, 0)


def gateconv_fwd(bcv, w, w_out, res, name, hosted=()):
    T, D3 = bcv.shape
    D = D3 // 3
    K = w.shape[0]
    tm = _tile(T, TOKEN_TILE)
    wo_shape = w_out.outs[0].shape

    def body(x_ref, halo_ref, w_ref, res_ref, y_ref, h_ref, pad_ref, wo_v, sem, wo_hbm):
        i = pl.program_id(0)

        @pl.when(i == 0)
        def _():
            cp = pltpu.make_async_copy(wo_hbm, wo_v, sem)
            cp.start()
            cp.wait()

        pad_ref[HALO:, :] = x_ref[:, D:2 * D] * x_ref[:, 2 * D:]
        pad_ref[:HALO, :] = jnp.where(i > 0, halo_ref[:, D:2 * D] * halo_ref[:, 2 * D:], 0.0)

        def block(t0, ls):
            win = pad_ref[pl.ds(t0, CONV_ROWS + HALO), ls]
            acc = jnp.zeros((CONV_ROWS, LANES), F32)
            for k in range(K):
                acc = acc + w_ref[k:k + 1, ls] * _shifted(win, HALO - (K - 1) + k, CONV_ROWS)
            y_ref[pl.ds(t0, CONV_ROWS), ls] = (x_ref[pl.ds(t0, CONV_ROWS), ls] * acc).astype(y_ref.dtype)

        _conv_loops(tm, D, block)
        h_ref[...] = res_ref[...] + jnp.dot(y_ref[...], wo_v[...].reshape(D, D), preferred_element_type=F32)

    row = pl.BlockSpec((tm, D), lambda i: (i, 0))
    (y, h), xo = _call(
        body, name, (T // tm,),
        [pl.BlockSpec((tm, D3), lambda i: (i, 0)), _prev_halo_spec(tm, D3), pl.BlockSpec((K, D), lambda i: (0, 0)), row],
        [row, row], [jax.ShapeDtypeStruct((T, D), BF16), jax.ShapeDtypeStruct((T, D), F32)],
        [bcv, bcv, w, res], ("arbitrary",),
        [pltpu.VMEM((tm + HALO, D), F32), pltpu.VMEM(wo_shape, BF16), pltpu.SemaphoreType.DMA],
        hosted=[w_out.awaited_first()] + list(hosted))
    return y, h, xo


def gateconv_bwd(dy, bcv, w, name, hosted=()):
    T, D3 = bcv.shape
    D = D3 // 3
    K = w.shape[0]
    tm = _tile(T, TOKEN_TILE)
    nt = T // tm

    def body(dy_ref, dyn_ref, x_ref, xp_ref, xn_ref, w_ref, o_ref, dw_ref, cv_ref, dc_ref, wacc_ref):
        i = pl.program_id(0)
        cv_ref[HALO:, :] = x_ref[:, D:2 * D] * x_ref[:, 2 * D:]
        cv_ref[:HALO, :] = jnp.where(i > 0, xp_ref[:, D:2 * D] * xp_ref[:, 2 * D:], 0.0)
        dc_ref[:tm, :] = dy_ref[...] * x_ref[:, :D]
        dc_ref[tm:, :] = jnp.where(i < nt - 1, dyn_ref[...] * xn_ref[:, :D], 0.0)

        @pl.when(i == 0)
        def _():
            wacc_ref[...] = jnp.zeros_like(wacc_ref)

        def block(t0, ls):
            cwin = cv_ref[pl.ds(t0, CONV_ROWS + HALO), ls]
            dwin = dc_ref[pl.ds(t0, CONV_ROWS + HALO), ls]
            dcon = dwin[:CONV_ROWS]
            conv = jnp.zeros((CONV_ROWS, LANES), F32)
            dcv = jnp.zeros((CONV_ROWS, LANES), F32)
            for k in range(K):
                wk = w_ref[k:k + 1, ls]
                cs = _shifted(cwin, HALO - (K - 1) + k, CONV_ROWS)
                conv = conv + wk * cs
                dcv = dcv + wk * _shifted(dwin, (K - 1) - k, CONV_ROWS)
                wacc_ref[k * SUBLANES:(k + 1) * SUBLANES, ls] += _rowsum8(dcon * cs)
            rows = pl.ds(t0, CONV_ROWS)
            o_ref[rows, ls] = (dy_ref[rows, ls] * conv).astype(o_ref.dtype)
            o_ref[rows, pl.ds(D + ls.start, LANES)] = (dcv * x_ref[rows, pl.ds(2 * D + ls.start, LANES)]).astype(o_ref.dtype)
            o_ref[rows, pl.ds(2 * D + ls.start, LANES)] = (dcv * x_ref[rows, pl.ds(D + ls.start, LANES)]).astype(o_ref.dtype)

        _conv_loops(tm, D, block)

        @pl.when(i == nt - 1)
        def _():
            for k in range(K):
                dw_ref[k:k + 1, :] = jnp.sum(wacc_ref[k * SUBLANES:(k + 1) * SUBLANES, :], axis=0, keepdims=True)

    (dx, dw), xo = _call(
        body, name, (nt,),
        [pl.BlockSpec((tm, D), lambda i: (i, 0)), _next_halo_spec(tm, D, T),
         pl.BlockSpec((tm, D3), lambda i: (i, 0)), _prev_halo_spec(tm, D3), _next_halo_spec(tm, D3, T),
         pl.BlockSpec((K, D), lambda i: (0, 0))],
        [pl.BlockSpec((tm, D3), lambda i: (i, 0)), pl.BlockSpec((K, D), lambda i: (0, 0))],
        [jax.ShapeDtypeStruct((T, D3), BF16), jax.ShapeDtypeStruct((K, D), F32)],
        [dy, dy, bcv, bcv, bcv, w], ("arbitrary",),
        [pltpu.VMEM((tm + HALO, D), F32), pltpu.VMEM((tm + HALO, D), F32), pltpu.VMEM((K * SUBLANES, D), F32)],
        hosted=hosted)
    return dx, dw, xo


def bconv_fwd(u, w, b_conv, ln_g, ln_b, w_out, b_out, res, name, hosted=()):
    T, D2 = u.shape
    D = D2 // 2
    K = w.shape[0]
    tm = _tile(T, TOKEN_TILE)

    def body(u_ref, halo_ref, w_ref, bc_ref, g_ref, b_ref, wo_ref, bo_ref, res_ref, cu_ref, s_ref, h_ref, pad_ref):
        i = pl.program_id(0)
        pad_ref[HALO:, :] = u_ref[:, :D] * _sigmoid(u_ref[:, D:])
        pad_ref[:HALO, :] = jnp.where(i > 0, halo_ref[:, :D] * _sigmoid(halo_ref[:, D:]), 0.0)

        def block(t0, ls):
            win = pad_ref[pl.ds(t0, CONV_ROWS + HALO), ls]
            acc = jnp.zeros((CONV_ROWS, LANES), F32)
            for k in range(K):
                acc = acc + w_ref[k:k + 1, ls] * _shifted(win, HALO - (K - 1) + k, CONV_ROWS)
            cu_ref[pl.ds(t0, CONV_ROWS), ls] = acc + bc_ref[:, ls]

        _conv_loops(tm, D, block)
        cu = cu_ref[...]
        mu = jnp.mean(cu, axis=-1, keepdims=True)
        xc = cu - mu
        rstd = lax.rsqrt(jnp.mean(xc * xc, axis=-1, keepdims=True) + LN_EPS)
        ln = xc * rstd * g_ref[...] + b_ref[...]
        s = (ln * _sigmoid(ln)).astype(s_ref.dtype)
        s_ref[...] = s
        h_ref[...] = res_ref[...] + bo_ref[...] + jnp.dot(s, wo_ref[0], preferred_element_type=F32)

    vec = pl.BlockSpec((1, D), lambda i: (0, 0))
    row = pl.BlockSpec((tm, D), lambda i: (i, 0))
    (cu, s, h), xo = _call(
        body, name, (T // tm,),
        [pl.BlockSpec((tm, D2), lambda i: (i, 0)), _prev_halo_spec(tm, D2), pl.BlockSpec((K, D), lambda i: (0, 0)), vec, vec, vec,
         pl.BlockSpec((1, D, D), lambda i: (0, 0, 0)), vec, row],
        [row, row, row], [jax.ShapeDtypeStruct((T, D), F32), jax.ShapeDtypeStruct((T, D), BF16), jax.ShapeDtypeStruct((T, D), F32)],
        [u, u, w, b_conv, ln_g, ln_b, w_out, b_out, res], ("parallel",), [pltpu.VMEM((tm + HALO, D), F32)], hosted=hosted)
    return cu, s, h, xo


def pw2_ln_bwd(dy, w, cu, ln_g, ln_b, name, hosted=()):
    T, D = cu.shape
    tm = _tile(T, TOKEN_TILE)

    def body(dy_ref, w_ref, cu_ref, g_ref, b_ref, dcu_ref, dg_ref, db_ref, dbc_ref, dbo_ref):
        i = pl.program_id(0)
        dy_ = dy_ref[...]
        ds = lax.dot_general(dy_.astype(BF16), w_ref[0], _NT, preferred_element_type=F32)
        cu_ = cu_ref[...]
        mu = jnp.mean(cu_, axis=-1, keepdims=True)
        xc = cu_ - mu
        rstd = lax.rsqrt(jnp.mean(xc * xc, axis=-1, keepdims=True) + LN_EPS)
        xh = xc * rstd
        ln = xh * g_ref[...] + b_ref[...]
        sg = _sigmoid(ln)
        dl = ds * (sg * (1.0 + ln * (1.0 - sg)))
        dxh = dl * g_ref[...]
        dcu = rstd * (dxh - jnp.mean(dxh, axis=-1, keepdims=True) - xh * jnp.mean(dxh * xh, axis=-1, keepdims=True))
        dcu_ref[...] = dcu
        pg = jnp.sum(dl * xh, axis=0, keepdims=True)
        pb = jnp.sum(dl, axis=0, keepdims=True)
        pc = jnp.sum(dcu, axis=0, keepdims=True)
        po = jnp.sum(dy_, axis=0, keepdims=True)

        @pl.when(i == 0)
        def _():
            dg_ref[...] = pg
            db_ref[...] = pb
            dbc_ref[...] = pc
            dbo_ref[...] = po

        @pl.when(i > 0)
        def _():
            dg_ref[...] += pg
            db_ref[...] += pb
            dbc_ref[...] += pc
            dbo_ref[...] += po

    vec = pl.BlockSpec((1, D), lambda i: (0, 0))
    row = pl.BlockSpec((tm, D), lambda i: (i, 0))
    vshape = jax.ShapeDtypeStruct((1, D), F32)
    outs, xo = _call(
        body, name, (T // tm,), [row, pl.BlockSpec((1, D, D), lambda i: (0, 0, 0)), row, vec, vec], [row, vec, vec, vec, vec],
        [jax.ShapeDtypeStruct((T, D), F32), vshape, vshape, vshape, vshape], [dy, w, cu, ln_g, ln_b], ("arbitrary",),
        hosted=hosted)
    return (*outs, xo)


def bconv_bwd(dcu, u, w, name, hosted=()):
    T, D2 = u.shape
    D = D2 // 2
    K = w.shape[0]
    tm = _tile(T, TOKEN_TILE)
    nt = T // tm

    def body(dc_ref, dcn_ref, u_ref, up_ref, w_ref, du_ref, dw_ref, db_ref, glu_ref, dpad_ref, dglu_ref, wacc_ref):
        i = pl.program_id(0)
        glu_ref[HALO:, :] = u_ref[:, :D] * _sigmoid(u_ref[:, D:])
        glu_ref[:HALO, :] = jnp.where(i > 0, up_ref[:, :D] * _sigmoid(up_ref[:, D:]), 0.0)
        dpad_ref[:tm, :] = dc_ref[...]
        dpad_ref[tm:, :] = jnp.where(i < nt - 1, dcn_ref[...], 0.0)

        @pl.when(i == 0)
        def _():
            wacc_ref[...] = jnp.zeros_like(wacc_ref)

        def block(t0, ls):
            gwin = glu_ref[pl.ds(t0, CONV_ROWS + HALO), ls]
            dwin = dpad_ref[pl.ds(t0, CONV_ROWS + HALO), ls]
            dcur = dwin[:CONV_ROWS]
            dglu = jnp.zeros((CONV_ROWS, LANES), F32)
            for k in range(K):
                dglu = dglu + w_ref[k:k + 1, ls] * _shifted(dwin, (K - 1) - k, CONV_ROWS)
                gs = _shifted(gwin, HALO - (K - 1) + k, CONV_ROWS)
                wacc_ref[k * SUBLANES:(k + 1) * SUBLANES, ls] += _rowsum8(dcur * gs)
            dglu_ref[pl.ds(t0, CONV_ROWS), ls] = dglu

        _conv_loops(tm, D, block)
        dglu = dglu_ref[...]
        a = u_ref[:, :D]
        sg = _sigmoid(u_ref[:, D:])
        da = dglu * sg
        dg = dglu * a * (sg * (1.0 - sg))
        du_ref[:, :D] = da.astype(du_ref.dtype)
        du_ref[:, D:] = dg.astype(du_ref.dtype)
        pa = jnp.sum(da, axis=0, keepdims=True)
        pg = jnp.sum(dg, axis=0, keepdims=True)

        @pl.when(i == 0)
        def _():
            db_ref[:, :D] = pa
            db_ref[:, D:] = pg

        @pl.when(i > 0)
        def _():
            db_ref[:, :D] += pa
            db_ref[:, D:] += pg

        @pl.when(i == nt - 1)
        def _():
            for k in range(K):
                dw_ref[k:k + 1, :] = jnp.sum(wacc_ref[k * SUBLANES:(k + 1) * SUBLANES, :], axis=0, keepdims=True)

    (du, dw, db), xo = _call(
        body, name, (nt,),
        [pl.BlockSpec((tm, D), lambda i: (i, 0)), _next_halo_spec(tm, D, T),
         pl.BlockSpec((tm, D2), lambda i: (i, 0)), _prev_halo_spec(tm, D2), pl.BlockSpec((K, D), lambda i: (0, 0))],
        [pl.BlockSpec((tm, D2), lambda i: (i, 0)), pl.BlockSpec((K, D), lambda i: (0, 0)), pl.BlockSpec((1, D2), lambda i: (0, 0))],
        [jax.ShapeDtypeStruct((T, D2), BF16), jax.ShapeDtypeStruct((K, D), F32), jax.ShapeDtypeStruct((1, D2), F32)],
        [dcu, dcu, u, u, w], ("arbitrary",),
        [pltpu.VMEM((tm + HALO, D), F32), pltpu.VMEM((tm + HALO, D), F32), pltpu.VMEM((tm, D), F32),
         pltpu.VMEM((K * SUBLANES, D), F32)], hosted=hosted)
    return du, dw, db, xo


def mm_cols(a, w, bias, name, hosted=()):
    T, K = a.shape
    S, _, n = w.shape
    tm = _tile(T, WIDE_TOKEN_TILE)

    def body(*refs):
        a_ref, w_ref = refs[:2]
        o_ref = refs[-1]
        acc = jnp.dot(a_ref[...], w_ref[...], preferred_element_type=F32)
        if bias is not None:
            acc = acc + refs[2][...]
        o_ref[...] = acc

    in_specs = [pl.BlockSpec((tm, K), lambda s, i: (i, 0)), pl.BlockSpec((None, K, n), lambda s, i: (s, 0, 0))]
    args = [a, w]
    if bias is not None:
        in_specs.append(pl.BlockSpec((1, n), lambda s, i: (0, s)))
        args.append(bias)
    (out,), xo = _call(body, name, (S, T // tm), in_specs, [pl.BlockSpec((tm, n), lambda s, i: (i, s))],
                       [jax.ShapeDtypeStruct((T, S * n), F32)], args, ("parallel", "parallel"), hosted=hosted)
    return out, xo


def _load_weights(pairs, sems, S, G, i, p):
    def copies(seg):
        return [pltpu.make_async_copy(src.at[seg], dst.at[seg], sems.at[k, seg]) for k, (src, dst) in enumerate(pairs)]

    @pl.when((i == 0) & (p == 0))
    def _():
        for seg in range(S):
            for cp in copies(seg):
                cp.start()

    @pl.when((i == 0) & (p < S // G))
    def _():
        for j in range(G):
            for cp in copies(G * p + j):
                cp.wait()


def ffn_fwd(h, gain, weights, name, hosted=(), arriving=None):
    T, D = h.shape
    S, f, _ = weights[0].shape
    tm = _tile(T, TOKEN_TILE)
    rc = tm // FFN_ROW_CHUNKS
    chunks = [slice(r * rc, (r + 1) * rc) for r in range(FFN_ROW_CHUNKS)]
    G = FFN_FWD_SEGS_PER_STEP
    weights = list(weights)
    hosted = ([arriving.awaited_first()] if arriving is not None else []) + list(hosted)

    def body(h_ref, gain_ref, *refs):
        nw = len(weights)
        wg_hbm, wu_hbm, wd_hbm = list(refs[:nw]) + list(refs[nw + 9:])
        n_ref, g_ref, u_ref, gu_ref, o_ref, wg_v, wu_v, wd_v, sems = refs[nw:nw + 9]
        i, p = pl.program_id(0), pl.program_id(1)
        _load_weights([(wg_hbm, wg_v), (wu_hbm, wu_v), (wd_hbm, wd_v)], sems, S, G, i, p)

        @pl.when(p == 0)
        def _():
            x = h_ref[...]
            r = lax.rsqrt(jnp.mean(x * x, axis=-1, keepdims=True) + RMS_EPS)
            n_ref[...] = (x * r * gain_ref[...]).astype(n_ref.dtype)

        parts = []
        for rows in chunks:
            a = n_ref[rows, :]
            acc = None
            for j in range(G):
                seg = G * p + j
                g = lax.dot_general(a, wg_v[seg], _NT, preferred_element_type=F32)
                u = lax.dot_general(a, wu_v[seg], _NT, preferred_element_type=F32)
                gu = (g * _sigmoid(g) * u).astype(gu_ref.dtype)
                g_ref[j, rows, :] = g.astype(g_ref.dtype)
                u_ref[j, rows, :] = u.astype(u_ref.dtype)
                gu_ref[j, rows, :] = gu
                part = jnp.dot(gu, wd_v[seg], preferred_element_type=F32)
                acc = part if acc is None else acc + part
            parts.append(acc)

        @pl.when(p == 0)
        def _():
            for rows, part in zip(chunks, parts):
                o_ref[rows, :] = h_ref[rows, :] + part

        @pl.when(p > 0)
        def _():
            for rows, part in zip(chunks, parts):
                o_ref[rows, :] += part

    row = pl.BlockSpec((tm, D), lambda i, p: (i, 0))
    seg = pl.BlockSpec((G, tm, f), lambda i, p: (p, i, 0))
    hbm = pl.BlockSpec(memory_space=pl.ANY)
    segs = jax.ShapeDtypeStruct((S, T, f), BF16)
    outs, xo = _call(
        body, name, (T // tm, S // G),
        [row, pl.BlockSpec((1, D), lambda i, s: (0, 0))] + [hbm] * len(weights), [row, seg, seg, seg, row],
        [jax.ShapeDtypeStruct((T, D), BF16), segs, segs, segs, jax.ShapeDtypeStruct((T, D), F32)],
        [h, gain] + weights, ("arbitrary", "arbitrary"),
        [pltpu.VMEM((S, f, D), BF16), pltpu.VMEM((S, f, D), BF16), pltpu.VMEM((S, f, D), BF16), pltpu.SemaphoreType.DMA((3, S))],
        hosted=hosted)
    return (*outs, xo)


def ffn_bwd(dy, h, gain, g, u, wd, wg, wu, name, hosted=()):
    T, D = h.shape
    S, f, _ = wg.shape
    tm = _tile(T, FFN_BWD_TOKEN_TILE)
    nt = T // tm

    def body(dy_ref, h_ref, gain_ref, g_ref, u_ref, wd_hbm, wg_hbm, wu_hbm, dg_ref, du_ref, dh_ref, dhb_ref, dgain_ref,
             wd_v, wg_v, wu_v, sems):
        i = pl.program_id(0)
        _load_weights([(wd_hbm, wd_v), (wg_hbm, wg_v), (wu_hbm, wu_v)], sems, S, S, i, 0)
        dy_ = dy_ref[...]
        dyb = dy_.astype(BF16)
        dn = None
        for j in range(S):
            dgu = lax.dot_general(dyb, wd_v[j], _NT, preferred_element_type=F32)
            gv = g_ref[j].astype(F32)
            sg = _sigmoid(gv)
            dg = (dgu * u_ref[j].astype(F32) * (sg * (1.0 + gv * (1.0 - sg)))).astype(dg_ref.dtype)
            du = (dgu * (gv * sg)).astype(du_ref.dtype)
            dg_ref[j] = dg
            du_ref[j] = du
            part = jnp.dot(dg, wg_v[j], preferred_element_type=F32) + jnp.dot(du, wu_v[j], preferred_element_type=F32)
            dn = part if dn is None else dn + part
        x = h_ref[...]
        r = lax.rsqrt(jnp.mean(x * x, axis=-1, keepdims=True) + RMS_EPS)
        xhat = x * r
        dxhat = dn * gain_ref[...]
        dh = dy_ + r * (dxhat - xhat * jnp.mean(dxhat * xhat, axis=-1, keepdims=True))
        dh_ref[...] = dh
        dhb_ref[...] = dh.astype(dhb_ref.dtype)
        pg = jnp.sum(dn * xhat, axis=0, keepdims=True)

        @pl.when(i == 0)
        def _():
            dgain_ref[...] = pg

        @pl.when(i > 0)
        def _():
            dgain_ref[...] += pg

    row = pl.BlockSpec((tm, D), lambda i: (i, 0))
    vec = pl.BlockSpec((1, D), lambda i: (0, 0))
    seg = pl.BlockSpec((S, tm, f), lambda i: (0, i, 0))
    hbm = pl.BlockSpec(memory_space=pl.ANY)
    segs = jax.ShapeDtypeStruct((S, T, f), BF16)
    outs, xo = _call(
        body, name, (nt,),
        [row, row, vec, seg, seg, hbm, hbm, hbm], [seg, seg, row, row, vec],
        [segs, segs, jax.ShapeDtypeStruct((T, D), F32), jax.ShapeDtypeStruct((T, D), BF16), jax.ShapeDtypeStruct((1, D), F32)],
        [dy, h, gain, g, u, wd, wg, wu], ("arbitrary",),
        [pltpu.VMEM((S, f, D), BF16), pltpu.VMEM((S, f, D), BF16), pltpu.VMEM((S, f, D), BF16),
         pltpu.SemaphoreType.DMA((3, S))], hosted=hosted)
    return (*outs, xo)


_NT = (((1,), (1,)), ((), ()))
_TN = (((0,), (0,)), ((), ()))


def nt_rows(dy, w, name, hosted=()):
    T, N = dy.shape
    S, k, _ = w.shape
    tm = _tile(T, TOKEN_TILE)

    def body(dy_ref, w_ref, o_ref):
        o_ref[...] = lax.dot_general(dy_ref[...].astype(BF16), w_ref[...], _NT, preferred_element_type=F32)

    (out,), xo = _call(
        body, name, (T // tm, S),
        [pl.BlockSpec((tm, N), lambda i, s: (i, 0)), pl.BlockSpec((None, k, N), lambda i, s: (s, 0, 0))],
        [pl.BlockSpec((None, tm, k), lambda i, s: (s, i, 0))], [jax.ShapeDtypeStruct((S, T, k), F32)],
        [dy, w], ("parallel", "parallel"), hosted=hosted)
    return out, xo


def nt_cols_rms(dy, w, h, gain, dres, name, hosted=(), also_bf16=False):
    T, K = h.shape
    S, _, n = w.shape
    tm = _tile(T, TOKEN_TILE)

    def body(dy_ref, w_ref, h_ref, gain_ref, dres_ref, dh_ref, dgain_ref, *rest):
        i = pl.program_id(0)
        dn = None
        for s in range(S):
            part = lax.dot_general(dy_ref[:, s * n:(s + 1) * n], w_ref[s], _NT, preferred_element_type=F32)
            dn = part if dn is None else dn + part
        x = h_ref[...]
        r = lax.rsqrt(jnp.mean(x * x, axis=-1, keepdims=True) + RMS_EPS)
        xhat = x * r
        dxhat = dn * gain_ref[...]
        dh = dres_ref[...] + r * (dxhat - xhat * jnp.mean(dxhat * xhat, axis=-1, keepdims=True))
        dh_ref[...] = dh
        if also_bf16:
            rest[0][...] = dh.astype(BF16)
        pg = jnp.sum(dn * xhat, axis=0, keepdims=True)

        @pl.when(i == 0)
        def _():
            dgain_ref[...] = pg

        @pl.when(i > 0)
        def _():
            dgain_ref[...] += pg

    row = pl.BlockSpec((tm, K), lambda i: (i, 0))
    vec = pl.BlockSpec((1, K), lambda i: (0, 0))
    out_specs, out_shape = [row, vec], [jax.ShapeDtypeStruct((T, K), F32), jax.ShapeDtypeStruct((1, K), F32)]
    if also_bf16:
        out_specs, out_shape = out_specs + [row], out_shape + [jax.ShapeDtypeStruct((T, K), BF16)]
    outs, xo = _call(
        body, name, (T // tm,),
        [pl.BlockSpec((tm, S * n), lambda i: (i, 0)), pl.BlockSpec((S, K, n), lambda i: (0, 0, 0)), row, vec, row],
        out_specs, out_shape, [dy, w, h, gain, dres], ("arbitrary",), hosted=hosted)
    return (*outs, xo)


def tn_grad(a, dy, S, a_by_seg, name, hosted=()):
    T = dy.shape[0] if dy.ndim == 2 else dy.shape[1]
    tt = _tile(T, GRAD_TOKEN_TILE)
    if a_by_seg:
        R = a.shape[1] // S if a.ndim == 2 else a.shape[2]
        C = dy.shape[1]
        a_spec = pl.BlockSpec((tt, R), lambda s, t: (t, s)) if a.ndim == 2 else pl.BlockSpec((None, tt, R), lambda s, t: (s, t, 0))
        b_spec = pl.BlockSpec((tt, C), lambda s, t: (t, 0))
    else:
        R = a.shape[1]
        C = dy.shape[1] // S if dy.ndim == 2 else dy.shape[2]
        a_spec = pl.BlockSpec((tt, R), lambda s, t: (t, 0))
        b_spec = pl.BlockSpec((tt, C), lambda s, t: (t, s)) if dy.ndim == 2 else pl.BlockSpec((None, tt, C), lambda s, t: (s, t, 0))
    Rh = R // 2
    nt = T // tt

    def body(a_ref, b_ref, o_ref, acc_ref):
        t = pl.program_id(1)
        part = lax.dot_general(a_ref[...], b_ref[...].astype(BF16), _TN, preferred_element_type=F32)

        @pl.when(t == 0)
        def _():
            acc_ref[...] = part

        @pl.when(t > 0)
        def _():
            acc_ref[...] += part

        @pl.when(t == nt - 1)
        def _():
            o_ref[0] = acc_ref[:Rh, :].astype(o_ref.dtype)
            o_ref[1] = acc_ref[Rh:, :].astype(o_ref.dtype)

    (gh,), xo = _call(
        body, name, (S, nt), [a_spec, b_spec], [pl.BlockSpec((2, None, Rh, C), lambda s, t: (0, s, 0, 0))],
        [jax.ShapeDtypeStruct((2, S, Rh, C), BF16)], [a, dy], ("parallel", "arbitrary"), [pltpu.VMEM((R, C), F32)],
        hosted=hosted)
    return gh, xo


def tn_grad_square(a, dy, S, name, hosted=()):
    T, K = a.shape
    N = dy.shape[1]
    tt = _tile(T, GRAD_TOKEN_TILE)
    nt = T // tt
    Rh = K // S // 2

    def body(a_ref, b_ref, o_ref, acc_ref):
        t = pl.program_id(0)
        part = lax.dot_general(a_ref[...], b_ref[...].astype(BF16), _TN, preferred_element_type=F32)

        @pl.when(t == 0)
        def _():
            acc_ref[...] = part

        @pl.when(t > 0)
        def _():
            acc_ref[...] += part

        @pl.when(t == nt - 1)
        def _():
            for s in range(S):
                for hf in range(2):
                    r0 = (2 * s + hf) * Rh
                    o_ref[hf, s] = acc_ref[r0:r0 + Rh, :].astype(o_ref.dtype)

    (gh,), xo = _call(
        body, name, (nt,), [pl.BlockSpec((tt, K), lambda t: (t, 0)), pl.BlockSpec((tt, N), lambda t: (t, 0))],
        [pl.BlockSpec((2, S, Rh, N), lambda t: (0, 0, 0, 0))], [jax.ShapeDtypeStruct((2, S, Rh, N), BF16)],
        [a, dy], ("arbitrary",), [pltpu.VMEM((K, N), F32)], hosted=hosted)
    return gh, xo


def _place():
    x, y, c = lax.axis_index("x"), lax.axis_index("y"), lax.axis_index("c")
    chips = [(1 - x, y), (x, 1 - y), (1 - x, 1 - y)]
    return x, y, c, chips


def _remote(src, dst, send_sem, recv_sem, dev):
    return pltpu.make_async_remote_copy(src_ref=src, dst_ref=dst, send_sem=send_sem, recv_sem=recv_sem,
                                        device_id=dev, device_id_type=MESH)


def small_allreduce(v, name, hosted=()):
    rows, W = v.shape

    def body(v_ref, o_ref, sib_ref, pair_ref, chips_ref, send_sems, recv_sems):
        x, y, c, chips = _place()
        me = 2 * x + y
        swap = _remote(v_ref, sib_ref, send_sems.at[3], recv_sems.at[3], (x, y, 1 - c))
        swap.start()
        swap.wait()
        mine, other = v_ref[...], sib_ref[...]
        pair_ref[...] = jnp.where(c == 0, mine, other) + jnp.where(c == 0, other, mine)
        sends = []
        for j, (px, py) in enumerate(chips):
            cp = _remote(pair_ref, chips_ref.at[me], send_sems.at[j], recv_sems.at[j], (px, py, c))
            cp.start()
            sends.append(cp)
        chips_ref[me] = pair_ref[...]
        for j, (px, py) in enumerate(chips):
            blk = chips_ref.at[2 * px + py]
            _remote(blk, blk, send_sems.at[j], recv_sems.at[j], (px, py, c)).wait_recv()
        for cp in sends:
            cp.wait_send()
        o_ref[...] = (chips_ref[0] + chips_ref[1]) + (chips_ref[2] + chips_ref[3])

    vm = pl.BlockSpec(memory_space=pltpu.VMEM)
    (out,), xo = _call(
        body, name, (), [vm], [vm], [jax.ShapeDtypeStruct((rows, W), F32)], [v], (),
        [pltpu.VMEM((rows, W), F32), pltpu.VMEM((rows, W), F32), pltpu.VMEM((N_CHIPS, rows, W), F32),
         pltpu.SemaphoreType.DMA((4,)), pltpu.SemaphoreType.DMA((4,))], hosted=hosted)
    return out, xo


def _gather_p1_copies(srcs, bufs, ssem, rsem, base):
    x, y, c, chips = _place()
    me, sib = 2 * x + y, (x, y, 1 - c)
    sends, recvs = [], []
    for k, (src, buf) in enumerate(zip(srcs, bufs)):
        rh = src.shape[0] // 2
        s0 = base + 4 * k
        sends.append(_remote(src, buf.at[me], ssem.at[s0 + 3], rsem.at[s0 + 3], sib))
        recvs.append(_remote(buf.at[me], buf.at[me], ssem.at[s0 + 3], rsem.at[s0 + 3], sib))
        for j, (px, py) in enumerate(chips):
            sends.append(_remote(src.at[pl.ds(c * rh, rh)], buf.at[me, pl.ds(c * rh, rh)], ssem.at[s0 + j], rsem.at[s0 + j], (px, py, c)))
            blk = buf.at[2 * px + py, pl.ds(c * rh, rh)]
            recvs.append(_remote(blk, blk, ssem.at[s0 + j], rsem.at[s0 + j], (px, py, c)))
    return sends, recvs


def _gather_p2_copies(bufs, ssem, rsem, base):
    x, y, c, chips = _place()
    sib = (x, y, 1 - c)
    sends, recvs = [], []
    for k, buf in enumerate(bufs):
        rh = buf.shape[1] // 2
        for j, (px, py) in enumerate(chips):
            s0 = base + 3 * k + j
            blk = buf.at[2 * px + py, pl.ds(c * rh, rh)]
            sends.append(_remote(blk, blk, ssem.at[s0], rsem.at[s0], sib))
            got = buf.at[2 * px + py, pl.ds((1 - c) * rh, rh)]
            recvs.append(_remote(got, got, ssem.at[s0], rsem.at[s0], sib))
    return sends, recvs


def _gathered_shape(s):
    return jax.ShapeDtypeStruct((N_CHIPS,) + s.shape, s.dtype)


def gather_p1(shards):
    return _Exchange(shards, [_gathered_shape(s) for s in shards], {}, 4 * len(shards),
                     lambda xi, xo, ss, rs: _gather_p1_copies(xi, xo, ss, rs, 0))


def gather_p2(bufs):
    return _Exchange(bufs, [jax.ShapeDtypeStruct(b.shape, b.dtype) for b in bufs], {k: k for k in range(len(bufs))},
                     3 * len(bufs), lambda xi, xo, ss, rs: _gather_p2_copies(xo, ss, rs, 0))


def gather_whole(whole, begun):
    nw, n = len(whole), len(whole) + len(begun)
    shards = list(whole) + list(begun)
    return _Exchange(shards, [_gathered_shape(s) for s in shards], {}, 4 * n + 3 * nw,
                     lambda xi, xo, ss, rs: _gather_p1_copies(xi, xo, ss, rs, 0),
                     then=lambda xi, xo, ss, rs: _gather_p2_copies(xo[:nw], ss, rs, 4 * n))


def gather_small(v):
    def copies(xi, xo, ssem, rsem):
        x, y, c, chips = _place()
        me, sib = 2 * x + y, (x, y, 1 - c)
        sends = [_remote(xi[0], xo[0].at[me], ssem.at[3], rsem.at[3], sib)]
        recvs = [_remote(xo[0].at[me], xo[0].at[me], ssem.at[3], rsem.at[3], sib)]
        for j, (px, py) in enumerate(chips):
            sends.append(_remote(xi[0], xo[0].at[me], ssem.at[j], rsem.at[j], (px, py, c)))
            blk = xo[0].at[2 * px + py]
            recvs.append(_remote(blk, blk, ssem.at[j], rsem.at[j], (px, py, c)))
        return sends, recvs

    return _Exchange([v], [_gathered_shape(v)], {}, 4, copies)


def run_exchanges(exchanges, name):
    return _call(lambda: None, name, (), [], [], [], [], (), hosted=exchanges)[1]


def sibling_halves(grads):
    def copies(xi, xo, ssem, rsem):
        x, y, c, _ = _place()
        sends = [_remote(xi[k].at[1 - c], xo[k], ssem.at[k], rsem.at[k], (x, y, 1 - c)) for k in range(len(grads))]
        return sends, sends

    return _Exchange(grads, [jax.ShapeDtypeStruct(g.shape[1:], g.dtype) for g in grads], {}, len(grads), copies)


def pair_sum(gh, recv, cidx, name):
    _, S, Rh, C = gh.shape

    def body(c_ref, a_ref, b_ref, o_ref):
        o_ref[...] = (a_ref[...].astype(F32) + b_ref[...].astype(F32)).astype(o_ref.dtype)

    return pl.pallas_call(
        body, name=name, out_shape=jax.ShapeDtypeStruct((S, Rh, C), BF16),
        grid_spec=pltpu.PrefetchScalarGridSpec(
            num_scalar_prefetch=1, grid=(S,),
            in_specs=[pl.BlockSpec((None, None, Rh, C), lambda s, c_ref: (c_ref[0], s, 0, 0)),
                      pl.BlockSpec((None, Rh, C), lambda s, c_ref: (s, 0, 0))],
            out_specs=pl.BlockSpec((None, Rh, C), lambda s, c_ref: (s, 0, 0))),
        compiler_params=_params(("parallel",)),
    )(cidx, gh, recv)


def scatter_p1(parts):
    def copies(xi, xo, ssem, rsem):
        x, y, c, chips = _place()
        me, sib = 2 * x + y, (x, y, 1 - c)
        sends, recvs = [], []
        for k in range(len(parts)):
            s0 = 4 * k
            sends.append(_remote(xi[k].at[me], xo[k].at[me, c], ssem.at[s0 + 3], rsem.at[s0 + 3], sib))
            own = xo[k].at[me, 1 - c]
            recvs.append(_remote(own, own, ssem.at[s0 + 3], rsem.at[s0 + 3], sib))
            for j, (px, py) in enumerate(chips):
                sends.append(_remote(xi[k].at[2 * px + py], xo[k].at[me, c], ssem.at[s0 + j], rsem.at[s0 + j], (px, py, c)))
                blk = xo[k].at[2 * px + py, c]
                recvs.append(_remote(blk, blk, ssem.at[s0 + j], rsem.at[s0 + j], (px, py, c)))
        return sends, recvs

    return _Exchange(parts, [jax.ShapeDtypeStruct((p.shape[0], 2) + p.shape[1:], p.dtype) for p in parts], {},
                     4 * len(parts), copies)


def scatter_p2(bufs):
    def copies(xi, xo, ssem, rsem):
        x, y, c, chips = _place()
        sib = (x, y, 1 - c)
        sends, recvs = [], []
        for k in range(len(bufs)):
            for j, (px, py) in enumerate(chips):
                s0 = 3 * k + j
                blk = xo[k].at[2 * px + py, c]
                sends.append(_remote(blk, blk, ssem.at[s0], rsem.at[s0], sib))
                got = xo[k].at[2 * px + py, 1 - c]
                recvs.append(_remote(got, got, ssem.at[s0], rsem.at[s0], sib))
        return sends, recvs

    return _Exchange(bufs, [jax.ShapeDtypeStruct(b.shape, b.dtype) for b in bufs], {k: k for k in range(len(bufs))},
                     3 * len(bufs), copies)


def _adamw_math(w, g, m, v):
    m = ADAM_B1 * m + (1.0 - ADAM_B1) * g
    v = ADAM_B2 * v + (1.0 - ADAM_B2) * (g * g)
    m_hat = m / (1.0 - ADAM_B1 ** ADAM_STEP)
    v_hat = v / (1.0 - ADAM_B2 ** ADAM_STEP)
    delta = -ADAM_LR * (m_hat / (jnp.sqrt(v_hat) + ADAM_EPS) + ADAM_WD * w)
    return delta, m, v


def adamw_reduce(w, m, v, buf, part, place, lyr, bases, name, hosted=()):
    L, R, C = w.shape
    Rh = R // 2
    rb = _tile(Rh, ROW_TILE, 2 * SUBLANES)
    nb = Rh // rb

    def body(place_ref, p_ref, b0, b1, b2, b3, w_ref, m_ref, v_ref, *rest):
        go_ref, d_ref, mo_ref, vo_ref = rest[-4:]
        mine = (place_ref[1] == pl.program_id(0))
        g = None
        for p, b in enumerate((b0, b1, b2, b3)):
            val = jnp.where(mine & (place_ref[0] == p), p_ref[...], b[...]).astype(F32)
            g = val if g is None else g + val
        d, mn, vn = _adamw_math(w_ref[...], g, m_ref[...], v_ref[...])
        go_ref[...] = g
        d_ref[...] = d
        mo_ref[...] = mn
        vo_ref[...] = vn

    def buf_spec(p):
        def idx(h, i, pr):
            own = (pr[0] == p) & (pr[1] == h)
            return (p, jnp.where(own, 1 - h, h), i, 0)
        return pl.BlockSpec((None, None, rb, C), idx)

    blk = pl.BlockSpec((None, rb, C), lambda h, i, pr: (lyr, h * nb + i, 0))
    in_specs = [pl.BlockSpec((None, rb, C), lambda h, i, pr: (pr[0], i, 0))] + [buf_spec(p) for p in range(N_CHIPS)] + [blk] * 3
    args = [part, buf, buf, buf, buf, w, m, v]
    aliases = {}
    if bases is not None:
        in_specs += [pl.BlockSpec(memory_space=pl.ANY)] * 4
        aliases = {len(args) + k: k for k in range(4)}
        args += list(bases)
    shp = jax.ShapeDtypeStruct((L, R, C), F32)
    return _call(body, name, (2, nb), in_specs, [blk] * 4, [shp] * 4, args, ("parallel", "parallel"),
                 hosted=hosted, prefetch=[place], own_aliases=aliases)


def small_update(gall, chip, entries, name):
    ne = len(entries)
    D = gall.shape[1]

    def body(chip_ref, gall_ref, *refs):
        ins, outs = refs[:3 * ne], refs[3 * ne:]
        ch = chip_ref[0]
        for e, (row0, kind, w, _, _) in enumerate(entries):
            r, width = w.shape

            def gsum(rs, cs):
                return gall_ref[rs, cs]

            if kind == "full":
                g = gsum(slice(row0, row0 + r), slice(0, D))
            elif kind == "cols":
                g = gsum(slice(row0, row0 + r), slice(0, width))
                for q in range(1, N_CHIPS):
                    g = jnp.where(ch == q, gsum(slice(row0, row0 + r), slice(q * width, (q + 1) * width)), g)
            else:
                per_row = D // width
                g = gsum(slice(row0, row0 + 1), slice(0, width))
                for q in range(1, N_CHIPS):
                    rr = row0 + q // per_row
                    cc = (q % per_row) * width
                    g = jnp.where(ch == q, gsum(slice(rr, rr + 1), slice(cc, cc + width)), g)
            d, mn, vn = _adamw_math(ins[3 * e][...], g, ins[3 * e + 1][...], ins[3 * e + 2][...])
            outs[4 * e][...] = g
            outs[4 * e + 1][...] = d
            outs[4 * e + 2][...] = mn
            outs[4 * e + 3][...] = vn

    vm = pl.BlockSpec(memory_space=pltpu.VMEM)
    args, out_shape = [], []
    for _, _, w, m, v in entries:
        args += [w, m, v]
        out_shape += [jax.ShapeDtypeStruct(w.shape, F32)] * 4
    return pl.pallas_call(
        body, name=name,
        in_specs=[pl.BlockSpec(memory_space=pltpu.SMEM), vm] + [vm] * (3 * ne),
        out_specs=[vm] * (4 * ne), out_shape=out_shape,
        compiler_params=pltpu.CompilerParams(vmem_limit_bytes=VMEM_LIMIT),
    )(chip, gall, *args)


def _pack_rows(items, width, name):
    starts, at = [], 0
    for it in items:
        starts.append(at)
        at += -(-it.shape[0] // SUBLANES) * SUBLANES
    total = at

    def body(*refs):
        o_ref = refs[-1]
        o_ref[...] = jnp.zeros_like(o_ref)
        for it_ref, r0 in zip(refs[:-1], starts):
            o_ref[r0:r0 + it_ref.shape[0], :] = it_ref[...]

    vm = pl.BlockSpec(memory_space=pltpu.VMEM)
    packed = pl.pallas_call(body, name=name, in_specs=[vm] * len(items), out_specs=vm,
                            out_shape=jax.ShapeDtypeStruct((total, width), F32))(*items)
    return packed, starts


def kernel(x, a_norm, a_w_in, a_conv, a_w_out, b_norm, b_w_pw1, b_b_pw1, b_conv, b_b_conv, b_ln_g, b_ln_b, b_w_pw2, b_b_pw2, ffn_norm, ffn_w_gate, ffn_w_up, ffn_w_down, final_norm, loss_target, m_a_norm, m_a_w_in, m_a_conv, m_a_w_out, m_b_norm, m_b_w_pw1, m_b_b_pw1, m_b_conv, m_b_b_conv, m_b_ln_g, m_b_ln_b, m_b_w_pw2, m_b_b_pw2, m_ffn_norm, m_ffn_w_gate, m_ffn_w_up, m_ffn_w_down, m_final_norm, v_a_norm, v_a_w_in, v_a_conv, v_a_w_out, v_b_norm, v_b_w_pw1, v_b_b_pw1, v_b_conv, v_b_b_conv, v_b_ln_g, v_b_ln_b, v_b_w_pw2, v_b_b_pw2, v_ffn_norm, v_ffn_w_gate, v_ffn_w_up, v_ffn_w_down, v_final_norm):
    T, D = x.shape[1], x.shape[2]
    Dq = D // N_CHIPS
    cx, cy, cc = lax.axis_index("x"), lax.axis_index("y"), lax.axis_index("c")
    chip = (2 * cx + cy).astype(jnp.int32).reshape(1)
    cidx = cc.astype(jnp.int32).reshape(1)
    h0 = x.reshape(T, D)
    tgt = loss_target.reshape(T, D)

    small_shards = [a_conv[0], b_norm, b_b_pw1.reshape(2, Dq), b_conv[0], b_b_conv, b_ln_g, b_ln_b, b_b_pw2]
    packed, st = _pack_rows(small_shards, Dq, "pack_small")

    tr = lambda t: jnp.swapaxes(t, 1, 2)
    w_gate, m_gate, v_gate = tr(ffn_w_gate), tr(m_ffn_w_gate), tr(v_ffn_w_gate)
    w_up, m_up, v_up = tr(ffn_w_up), tr(m_ffn_w_up), tr(v_ffn_w_up)
    bf = lambda t: t.astype(BF16)
    s_in, s_out, s_pw1, s_pw2 = bf(a_w_in[0]), bf(a_w_out[0]), bf(b_w_pw1[0]), bf(b_w_pw2[0])
    s_gate, s_up, s_down = [bf(w_gate[l]) for l in (0, 1)], [bf(w_up[l]) for l in (0, 1)], [bf(ffn_w_down[l]) for l in (0, 1)]

    n0, (g_in,) = rms_fwd(h0, a_norm, "rms_a", hosted=[gather_whole([s_in], [])])
    bcv, (g_out, gate0, sw) = mm_cols(n0, g_in, None, "mm_w_in", hosted=[gather_p1([s_out, s_gate[0]]), gather_small(packed)])

    def whole(k, r):
        return jnp.transpose(sw[:, st[k]:st[k] + r, :], (1, 0, 2)).reshape(r, D)

    a_conv_f, b_norm_f = whole(0, 3), whole(1, 1)
    b_b_pw1_f = sw[:, st[2]:st[2] + 2, :].reshape(1, 2 * D)
    b_conv_f, b_b_conv_f, b_ln_g_f, b_ln_b_f, b_b_pw2_f = whole(3, b_conv.shape[1]), whole(4, 1), whole(5, 1), whole(6, 1), whole(7, 1)
    ya, h1, (g_out, up0, down0, gate0) = gateconv_fwd(bcv, a_conv_f, gather_p2([g_out]), h0, "gateconv_fwd",
                                                      hosted=[gather_p1([s_up[0], s_down[0]]), gather_p2([gate0])])
    g_out = g_out.reshape(1, D, D)
    n1, fg0, fu0, gu0, h2, (up0, down0, *later) = ffn_fwd(h1, ffn_norm[0:1], [gate0], "ffn_fwd0", arriving=gather_p2([up0, down0]),
                                                          hosted=[gather_p1([s_pw1, s_pw2, s_gate[1], s_up[1]])])
    n2, (g_pw1, g_pw2, gate1, up1) = rms_fwd(h2, b_norm_f, "rms_b", hosted=[gather_p2(later)])
    g_pw2 = g_pw2.reshape(1, D, D)
    ub, (down1,) = mm_cols(n2, g_pw1, b_b_pw1_f, "mm_pw1", hosted=[gather_p1([s_down[1]])])
    cu, sb, h3, (down1,) = bconv_fwd(ub, b_conv_f, b_b_conv_f, b_ln_g_f, b_ln_b_f, g_pw2, b_b_pw2_f, h2, "bconv_fwd",
                                     hosted=[gather_p2([down1])])
    n3, fg1, fu1, gu1, h4, _ = ffn_fwd(h3, ffn_norm[1:2], [gate1, up1, down1], "ffn_fwd1")
    loss_part, dh4, dh4_b, d_final = loss_head(h4, final_norm.reshape(1, D), tgt, "loss_head")

    place = jnp.concatenate([chip, cidx])

    def pair_sums(ghs, from_sib, tags):
        return [pair_sum(g, r, cidx, "pair_sum_" + t) for g, r, t in zip(ghs, from_sib, tags)]

    def upd(w, m, v, bufs, parts, tag, hosted=()):
        res, xo = None, []
        for lyr, (b, p) in enumerate(zip(bufs, parts)):
            res, xo_l = adamw_reduce(w, m, v, b, p, place, lyr, res, "adamw_%s%d" % (tag, lyr), hosted=hosted if lyr == 0 else ())
            xo += xo_l
        return res, xo

    dg1, du1, dh3, dh3_b, d_fn1, _ = ffn_bwd(dh4, h3, ffn_norm[1:2], fg1, fu1, down1, gate1, up1, "ffn_bwd1")
    gh_down1, _ = tn_grad(gu1, dh4_b, N_CHIPS, True, "tn_down1")
    gh_gate1, _ = tn_grad(dg1, n3, N_CHIPS, True, "tn_gate1")
    gh_up1, _ = tn_grad(du1, n3, N_CHIPS, True, "tn_up1")
    f1 = [gh_gate1, gh_up1, gh_down1]

    dcu, d_ln_g, d_ln_b, d_b_conv, d_b_pw2, sib_f1 = pw2_ln_bwd(dh3, g_pw2, cu, b_ln_g_f, b_ln_b_f, "pw2_ln_bwd",
                                                                hosted=[sibling_halves(f1)])
    p_f1 = pair_sums(f1, sib_f1, ["gate1", "up1", "down1"])
    gh_pw2, _ = tn_grad_square(sb, dh3_b, N_CHIPS, "tn_pw2")
    dub, d_bconv_w, d_b_pw1, buf_f1 = bconv_bwd(dcu, ub, b_conv_f, "bconv_bwd", hosted=[scatter_p1(p_f1)])
    gh_pw1, _ = tn_grad(n2, dub, N_CHIPS, False, "tn_pw1")
    b_grp = [gh_pw1, gh_pw2]
    dh2, d_b_norm, dh2_b, (*buf_f1, sib_pw1, sib_pw2) = nt_cols_rms(
        dub, g_pw1, h2, b_norm_f, dh3, "nt_pw1", hosted=[scatter_p2(buf_f1), sibling_halves(b_grp)], also_bf16=True)
    sib_b = [sib_pw1, sib_pw2]
    p_b = pair_sums(b_grp, sib_b, ["pw1", "pw2"])

    dg0, du0, dh1, dh1_b, d_fn0, buf_b = ffn_bwd(dh2, h1, ffn_norm[0:1], fg0, fu0, down0, gate0, up0, "ffn_bwd0",
                                                 hosted=[scatter_p1(p_b)])
    gh_down0, _ = tn_grad(gu0, dh2_b, N_CHIPS, True, "tn_down0")
    gh_gate0, (*buf_b, sib_down0) = tn_grad(dg0, n1, N_CHIPS, True, "tn_gate0",
                                            hosted=[scatter_p2(buf_b), sibling_halves([gh_down0])])
    p_down0 = pair_sums([gh_down0], [sib_down0], ["down0"])
    gh_up0, (buf_down0, sib_gate0) = tn_grad(du0, n1, N_CHIPS, True, "tn_up0",
                                             hosted=[scatter_p1(p_down0), sibling_halves([gh_gate0])])
    p_gate0 = pair_sums([gh_gate0], [sib_gate0], ["gate0"])
    dya, (buf_down0, sib_up0) = nt_rows(dh1, g_out, "nt_w_out",
                                        hosted=[scatter_p2([buf_down0]), sibling_halves([gh_up0])])
    p_up0 = pair_sums([gh_up0], [sib_up0], ["up0"])
    gh_out, _ = tn_grad_square(ya, dh1_b, N_CHIPS, "tn_w_out")
    dbcv, d_aconv_w, (buf_gate0, sib_out) = gateconv_bwd(dya[0], bcv, a_conv_f, "gateconv_bwd",
                                                         hosted=[scatter_p1(p_gate0), sibling_halves([gh_out])])
    p_out = pair_sums([gh_out], [sib_out], ["out"])
    gh_in, (buf_up0, buf_gate0) = tn_grad(n0, dbcv, N_CHIPS, False, "tn_w_in",
                                          hosted=[scatter_p1(p_up0), scatter_p2([buf_gate0])])
    sib_in = run_exchanges([sibling_halves([gh_in])], "reduce_in_siblings")
    p_in = pair_sums([gh_in], sib_in, ["in"])
    grad_x, d_a_norm, (buf_in, buf_out, buf_up0) = nt_cols_rms(
        dbcv, g_in, h0, a_norm, dh1, "nt_w_in", hosted=[scatter_p1(p_in + p_out), scatter_p2([buf_up0])])
    p_f0 = [p_gate0[0], p_up0[0], p_down0[0]]

    d_ffn_norm = jnp.concatenate([d_fn0, d_fn1], axis=0)
    small_grads = [d_a_norm, d_aconv_w, d_b_norm, d_b_pw1.reshape(2, D), d_bconv_w, d_b_conv, d_ln_g, d_ln_b, d_b_pw2,
                   d_ffn_norm, d_final, jnp.broadcast_to(loss_part, (1, D))]
    gpacked, gs = _pack_rows(small_grads, D, "pack_small_grads")
    gall, (buf_in, buf_out) = small_allreduce(gpacked, "allreduce_small_grads", hosted=[scatter_p2([buf_in, buf_out])])
    buf_a, p_a = [buf_in, buf_out], [p_in[0], p_out[0]]

    r_gate, _ = upd(w_gate, m_gate, v_gate, [buf_gate0, buf_f1[0]], [p_f0[0], p_f1[0]], "gate")
    r_up, _ = upd(w_up, m_up, v_up, [buf_up0, buf_f1[1]], [p_f0[1], p_f1[1]], "up")
    r_down, _ = upd(ffn_w_down, m_ffn_w_down, v_ffn_w_down, [buf_down0, buf_f1[2]], [p_f0[2], p_f1[2]], "down")
    r_gate, r_up = [tr(t) for t in r_gate], [tr(t) for t in r_up]
    r_pw1, _ = upd(b_w_pw1, m_b_w_pw1, v_b_w_pw1, [buf_b[0]], [p_b[0]], "pw1")
    r_pw2, _ = upd(b_w_pw2, m_b_w_pw2, v_b_w_pw2, [buf_b[1]], [p_b[1]], "pw2")
    r_in, _ = upd(a_w_in, m_a_w_in, v_a_w_in, [buf_a[0]], [p_a[0]], "w_in")
    r_out, _ = upd(a_w_out, m_a_w_out, v_a_w_out, [buf_a[1]], [p_a[1]], "w_out")
    entries = [
        (gs[0], "full", a_norm, m_a_norm, v_a_norm),
        (gs[1], "cols", a_conv[0], m_a_conv[0], v_a_conv[0]),
        (gs[2], "cols", b_norm, m_b_norm, v_b_norm),
        (gs[3], "flat2", b_b_pw1, m_b_b_pw1, v_b_b_pw1),
        (gs[4], "cols", b_conv[0], m_b_conv[0], v_b_conv[0]),
        (gs[5], "cols", b_b_conv, m_b_b_conv, v_b_b_conv),
        (gs[6], "cols", b_ln_g, m_b_ln_g, v_b_ln_g),
        (gs[7], "cols", b_ln_b, m_b_ln_b, v_b_ln_b),
        (gs[8], "cols", b_b_pw2, m_b_b_pw2, v_b_b_pw2),
        (gs[9], "full", ffn_norm, m_ffn_norm, v_ffn_norm),
        (gs[10], "full", final_norm.reshape(1, D), m_final_norm.reshape(1, D), v_final_norm.reshape(1, D)),
    ]
    so = small_update(gall, chip, entries, "small_update")
    sm = [so[4 * e:4 * e + 4] for e in range(len(entries))]

    def shaped(e, like):
        return [t.reshape(like.shape) for t in sm[e]]

    r_a_norm, r_a_conv, r_b_norm, r_b_b_pw1 = shaped(0, a_norm), shaped(1, a_conv), shaped(2, b_norm), shaped(3, b_b_pw1)
    r_b_conv, r_b_b_conv, r_b_ln_g, r_b_ln_b = shaped(4, b_conv), shaped(5, b_b_conv), shaped(6, b_ln_g), shaped(7, b_ln_b)
    r_b_b_pw2, r_ffn_norm, r_final = shaped(8, b_b_pw2), shaped(9, ffn_norm), shaped(10, final_norm)

    loss = gall[gs[11], 0]
    order =[r_a_norm, r_in, r_a_conv, r_out, r_b_norm, r_pw1, r_b_b_pw1, r_b_conv, r_b_b_conv, r_b_ln_g, r_b_ln_b,
             r_pw2, r_b_b_pw2, r_ffn_norm, r_gate, r_up, r_down, r_final]
    outs = [loss, grad_x.reshape(x.shape)]
    for field in range(4):
        outs += [r[field] for r in order]
    return tuple(outs)
```

```python
import functools

import jax
import jax.numpy as jnp
from jax import lax
from jax.experimental import pallas as pl
from jax.experimental.pallas import tpu as pltpu

RMS_EPS = 1e-6
LN_EPS = 1e-5
ADAM_LR = 0.001
ADAM_B1 = 0.9
ADAM_B2 = 0.999
ADAM_EPS = 1e-08
ADAM_WD = 0.01
ADAM_STEP = 10

N_CHIPS = 4
N_DEV = 8
LANES = 128
SUBLANES = 8
HALO = 32
CONV_ROWS = 64
TOKEN_TILE = 512
WIDE_TOKEN_TILE = 1024
GRAD_TOKEN_TILE = 2048
FFN_ROW_CHUNKS = 2
FFN_FWD_SEGS_PER_STEP = 4
FFN_BWD_TOKEN_TILE = 256
ROW_TILE = 256
VMEM_LIMIT = 56 * 1024 * 1024
MESH = pl.DeviceIdType.MESH
BF16 = jnp.bfloat16
F32 = jnp.float32


def _tile(n, pref, mult=SUBLANES):
    t = min(n, pref) // mult * mult
    while n % t:
        t -= mult
    return t


def _params(sem):
    return pltpu.CompilerParams(dimension_semantics=sem, vmem_limit_bytes=VMEM_LIMIT)


def _sigmoid(x):
    return 0.5 * jnp.tanh(0.5 * x) + 0.5


class _Exchange:
    def __init__(self, ins, outs, aliases, n_sems, copies, then=None):
        self.ins, self.outs, self.aliases, self.n_sems, self.copies = list(ins), list(outs), dict(aliases), n_sems, copies
        self.then = then
        self.early = False

    def awaited_first(self):
        self.early = True
        return self

    def start(self, xi, xo, ssem, rsem):
        for cp in self.copies(xi, xo, ssem, rsem)[0]:
            cp.start()

    def finish(self, xi, xo, ssem, rsem):
        sends, recvs = self.copies(xi, xo, ssem, rsem)
        for cp in recvs:
            cp.wait_recv()
        if self.then is not None:
            sends2, recvs2 = self.then(xi, xo, ssem, rsem)
            for cp in sends2:
                cp.start()
            for cp in recvs2:
                cp.wait_recv()
            sends = sends + sends2
        for cp in sends:
            cp.wait_send()


def _call(body, name, grid, in_specs, out_specs, out_shape, args, sem, scratch_shapes=(), hosted=(), prefetch=(),
          own_aliases=None):
    in_specs, out_specs, out_shape = list(in_specs), list(out_specs), list(out_shape)
    scratch_shapes, hosted, prefetch = list(scratch_shapes), list(hosted), list(prefetch)
    n_pre, n_in, n_out, n_scr = len(prefetch), len(args), len(out_shape), len(scratch_shapes)
    x_in = [a for ex in hosted for a in ex.ins]
    x_out = [o for ex in hosted for o in ex.outs]
    aliases = {n_pre + i: o for i, o in (own_aliases or {}).items()}
    at_in, at_out = n_pre + n_in, n_out
    for ex in hosted:
        for i, o in ex.aliases.items():
            aliases[at_in + i] = at_out + o
        at_in += len(ex.ins)
        at_out += len(ex.outs)
    sems = [pltpu.SemaphoreType.DMA((ex.n_sems,)) for ex in hosted for _ in range(2)]

    def wrapped(*refs):
        pre, refs = refs[:n_pre], refs[n_pre:]
        ins, xi = refs[:n_in], refs[n_in:n_in + len(x_in)]
        refs = refs[n_in + len(x_in):]
        outs, xo = refs[:n_out], refs[n_out:n_out + len(x_out)]
        refs = refs[n_out + len(x_out):]
        scr, sm = refs[:n_scr], refs[n_scr:]
        views, a, b = [], 0, 0
        for e, ex in enumerate(hosted):
            views.append((xi[a:a + len(ex.ins)], xo[b:b + len(ex.outs)], sm[2 * e], sm[2 * e + 1]))
            a += len(ex.ins)
            b += len(ex.outs)
        first = last = None
        for ax, g in enumerate(grid):
            f, l = pl.program_id(ax) == 0, pl.program_id(ax) == g - 1
            first, last = (f, l) if first is None else (first & f, last & l)

        def begin():
            for ex, v in zip(hosted, views):
                ex.start(*v)
            for ex, v in zip(hosted, views):
                if ex.early:
                    ex.finish(*v)

        def end():
            for ex, v in zip(hosted, views):
                if not ex.early:
                    ex.finish(*v)

        if hosted and grid:
            pl.when(first)(begin)
        elif hosted:
            begin()
        early_refs = [r for ex, v in zip(hosted, views) if ex.early for r in v[1]]
        body(*pre, *ins, *outs, *scr, *early_refs)
        if hosted and grid:
            pl.when(last)(end)
        elif hosted:
            end()

    hbm = pl.BlockSpec(memory_space=pl.ANY)
    all_in, all_out = in_specs + [hbm] * len(x_in), out_specs + [hbm] * len(x_out)
    kw = dict(name=name, out_shape=out_shape + x_out, input_output_aliases=aliases,
              compiler_params=_params(tuple("arbitrary" for _ in grid) if hosted else sem))
    if prefetch:
        kw["grid_spec"] = pltpu.PrefetchScalarGridSpec(num_scalar_prefetch=n_pre, grid=grid, in_specs=all_in,
                                                       out_specs=all_out, scratch_shapes=scratch_shapes + sems)
    else:
        kw.update(grid=grid, in_specs=all_in, out_specs=all_out, scratch_shapes=scratch_shapes + sems)
    res = pl.pallas_call(wrapped, **kw)(*prefetch, *args, *x_in)
    return list(res[:n_out]), list(res[n_out:])


def rms_fwd(h, gain, name, hosted=()):
    T, D = h.shape
    tm = _tile(T, TOKEN_TILE)

    def body(h_ref, g_ref, o_ref):
        x = h_ref[...]
        r = lax.rsqrt(jnp.mean(x * x, axis=-1, keepdims=True) + RMS_EPS)
        o_ref[...] = (x * r * g_ref[...]).astype(o_ref.dtype)

    (n,), xo = _call(
        body, name, (T // tm,),
        [pl.BlockSpec((tm, D), lambda i: (i, 0)), pl.BlockSpec((1, D), lambda i: (0, 0))],
        [pl.BlockSpec((tm, D), lambda i: (i, 0))], [jax.ShapeDtypeStruct((T, D), BF16)],
        [h, gain], ("parallel",), hosted=hosted)
    return n, xo


def loss_head(h, gain, tgt, name):
    T, D = h.shape
    tm = _tile(T, TOKEN_TILE)

    def body(h_ref, g_ref, t_ref, loss_ref, dh_ref, dhb_ref, dg_ref):
        i = pl.program_id(0)
        x = h_ref[...]
        g = g_ref[...]
        r = lax.rsqrt(jnp.mean(x * x, axis=-1, keepdims=True) + RMS_EPS)
        xhat = x * r
        diff = xhat * g - t_ref[...]
        part_loss = 0.5 * jnp.sum(jnp.mean(diff * diff, axis=-1, keepdims=True), axis=0, keepdims=True)
        dy = diff * (1.0 / D)
        dxhat = dy * g
        dh = r * (dxhat - xhat * jnp.mean(dxhat * xhat, axis=-1, keepdims=True))
        dh_ref[...] = dh
        dhb_ref[...] = dh.astype(dhb_ref.dtype)
        part = jnp.sum(dy * xhat, axis=0, keepdims=True)

        @pl.when(i == 0)
        def _():
            dg_ref[...] = part
            loss_ref[...] = part_loss

        @pl.when(i > 0)
        def _():
            dg_ref[...] += part
            loss_ref[...] += part_loss

    row = pl.BlockSpec((tm, D), lambda i: (i, 0))
    vec = pl.BlockSpec((1, D), lambda i: (0, 0))
    return pl.pallas_call(
        body, name=name, grid=(T // tm,),
        in_specs=[row, vec, row],
        out_specs=[pl.BlockSpec((1, 1), lambda i: (0, 0)), row, row, vec],
        out_shape=[jax.ShapeDtypeStruct((1, 1), F32), jax.ShapeDtypeStruct((T, D), F32),
                   jax.ShapeDtypeStruct((T, D), BF16), jax.ShapeDtypeStruct((1, D), F32)],
        compiler_params=_params(("arbitrary",)),
    )(h, gain, tgt)


def _prev_halo_spec(tm, width):
    return pl.BlockSpec((HALO, width), lambda i: (jnp.maximum(i * (tm // HALO) - 1, 0), 0))


def _next_halo_spec(tm, width, T):
    return pl.BlockSpec((HALO, width), lambda i: (jnp.minimum((i + 1) * (tm // HALO), T // HALO - 1), 0))


def _shifted(win, off, rows):
    if off % SUBLANES == 0:
        return win[off:off + rows]
    n = win.shape[0]
    return pltpu.roll(win, (n - off) % n, 0)[:rows]


def _rowsum8(x):
    acc = x[0:SUBLANES]
    for q in range(1, x.shape[0] // SUBLANES):
        acc = acc + x[q * SUBLANES:(q + 1) * SUBLANES]
    return acc


def _conv_loops(tm, D, per_block):
    def chunk(r, carry):
        t0 = pl.multiple_of(r * CONV_ROWS, CONV_ROWS)
        for lb in range(D // LANES):
            per_block(t0, slice(lb * LANES, (lb + 1) * LANES))
        return carry

    lax.fori_loop(0, tm // CONV_ROWS, chunk, 0)


def gateconv_fwd(bcv, w, w_out, res, name, hosted=()):
    T, D3 = bcv.shape
    D = D3 // 3
    K = w.shape[0]
    tm = _tile(T, TOKEN_TILE)
    wo_shape = w_out.outs[0].shape

    def body(x_ref, halo_ref, w_ref, res_ref, y_ref, h_ref, pad_ref, wo_v, sem, wo_hbm):
        i = pl.program_id(0)

        @pl.when(i == 0)
        def _():
            cp = pltpu.make_async_copy(wo_hbm, wo_v, sem)
            cp.start()
            cp.wait()

        pad_ref[HALO:, :] = x_ref[:, D:2 * D] * x_ref[:, 2 * D:]
        pad_ref[:HALO, :] = jnp.where(i > 0, halo_ref[:, D:2 * D] * halo_ref[:, 2 * D:], 0.0)

        def block(t0, ls):
            win = pad_ref[pl.ds(t0, CONV_ROWS + HALO), ls]
            acc = jnp.zeros((CONV_ROWS, LANES), F32)
            for k in range(K):
                acc = acc + w_ref[k:k + 1, ls] * _shifted(win, HALO - (K - 1) + k, CONV_ROWS)
            y_ref[pl.ds(t0, CONV_ROWS), ls] = (x_ref[pl.ds(t0, CONV_ROWS), ls] * acc).astype(y_ref.dtype)

        _conv_loops(tm, D, block)
        h_ref[...] = res_ref[...] + jnp.dot(y_ref[...], wo_v[...].reshape(D, D), preferred_element_type=F32)

    row = pl.BlockSpec((tm, D), lambda i: (i, 0))
    (y, h), xo = _call(
        body, name, (T // tm,),
        [pl.BlockSpec((tm, D3), lambda i: (i, 0)), _prev_halo_spec(tm, D3), pl.BlockSpec((K, D), lambda i: (0, 0)), row],
        [row, row], [jax.ShapeDtypeStruct((T, D), BF16), jax.ShapeDtypeStruct((T, D), F32)],
        [bcv, bcv, w, res], ("arbitrary",),
        [pltpu.VMEM((tm + HALO, D), F32), pltpu.VMEM(wo_shape, BF16), pltpu.SemaphoreType.DMA],
        hosted=[w_out.awaited_first()] + list(hosted))
    return y, h, xo


def gateconv_bwd(dy, bcv, w, name, hosted=()):
    T, D3 = bcv.shape
    D = D3 // 3
    K = w.shape[0]
    tm = _tile(T, TOKEN_TILE)
    nt = T // tm

    def body(dy_ref, dyn_ref, x_ref, xp_ref, xn_ref, w_ref, o_ref, dw_ref, cv_ref, dc_ref, wacc_ref):
        i = pl.program_id(0)
        cv_ref[HALO:, :] = x_ref[:, D:2 * D] * x_ref[:, 2 * D:]
        cv_ref[:HALO, :] = jnp.where(i > 0, xp_ref[:, D:2 * D] * xp_ref[:, 2 * D:], 0.0)
        dc_ref[:tm, :] = dy_ref[...] * x_ref[:, :D]
        dc_ref[tm:, :] = jnp.where(i < nt - 1, dyn_ref[...] * xn_ref[:, :D], 0.0)

        @pl.when(i == 0)
        def _():
            wacc_ref[...] = jnp.zeros_like(wacc_ref)

        def block(t0, ls):
            cwin = cv_ref[pl.ds(t0, CONV_ROWS + HALO), ls]
            dwin = dc_ref[pl.ds(t0, CONV_ROWS + HALO), ls]
            dcon = dwin[:CONV_ROWS]
            conv = jnp.zeros((CONV_ROWS, LANES), F32)
            dcv = jnp.zeros((CONV_ROWS, LANES), F32)
            for k in range(K):
                wk = w_ref[k:k + 1, ls]
                cs = _shifted(cwin, HALO - (K - 1) + k, CONV_ROWS)
                conv = conv + wk * cs
                dcv = dcv + wk * _shifted(dwin, (K - 1) - k, CONV_ROWS)
                wacc_ref[k * SUBLANES:(k + 1) * SUBLANES, ls] += _rowsum8(dcon * cs)
            rows = pl.ds(t0, CONV_ROWS)
            o_ref[rows, ls] = (dy_ref[rows, ls] * conv).astype(o_ref.dtype)
            o_ref[rows, pl.ds(D + ls.start, LANES)] = (dcv * x_ref[rows, pl.ds(2 * D + ls.start, LANES)]).astype(o_ref.dtype)
            o_ref[rows, pl.ds(2 * D + ls.start, LANES)] = (dcv * x_ref[rows, pl.ds(D + ls.start, LANES)]).astype(o_ref.dtype)

        _conv_loops(tm, D, block)

        @pl.when(i == nt - 1)
        def _():
            for k in range(K):
                dw_ref[k:k + 1, :] = jnp.sum(wacc_ref[k * SUBLANES:(k + 1) * SUBLANES, :], axis=0, keepdims=True)

    (dx, dw), xo = _call(
        body, name, (nt,),
        [pl.BlockSpec((tm, D), lambda i: (i, 0)), _next_halo_spec(tm, D, T),
         pl.BlockSpec((tm, D3), lambda i: (i, 0)), _prev_halo_spec(tm, D3), _next_halo_spec(tm, D3, T),
         pl.BlockSpec((K, D), lambda i: (0, 0))],
        [pl.BlockSpec((tm, D3), lambda i: (i, 0)), pl.BlockSpec((K, D), lambda i: (0, 0))],
        [jax.ShapeDtypeStruct((T, D3), BF16), jax.ShapeDtypeStruct((K, D), F32)],
        [dy, dy, bcv, bcv, bcv, w], ("arbitrary",),
        [pltpu.VMEM((tm + HALO, D), F32), pltpu.VMEM((tm + HALO, D), F32), pltpu.VMEM((K * SUBLANES, D), F32)],
        hosted=hosted)
    return dx, dw, xo


def bconv_fwd(u, w, b_conv, ln_g, ln_b, w_out, b_out, res, name, hosted=()):
    T, D2 = u.shape
    D = D2 // 2
    K = w.shape[0]
    tm = _tile(T, TOKEN_TILE)

    def body(u_ref, halo_ref, w_ref, bc_ref, g_ref, b_ref, wo_ref, bo_ref, res_ref, cu_ref, s_ref, h_ref, pad_ref):
        i = pl.program_id(0)
        pad_ref[HALO:, :] = u_ref[:, :D] * _sigmoid(u_ref[:, D:])
        pad_ref[:HALO, :] = jnp.where(i > 0, halo_ref[:, :D] * _sigmoid(halo_ref[:, D:]), 0.0)

        def block(t0, ls):
            win = pad_ref[pl.ds(t0, CONV_ROWS + HALO), ls]
            acc = jnp.zeros((CONV_ROWS, LANES), F32)
            for k in range(K):
                acc = acc + w_ref[k:k + 1, ls] * _shifted(win, HALO - (K - 1) + k, CONV_ROWS)
            cu_ref[pl.ds(t0, CONV_ROWS), ls] = acc + bc_ref[:, ls]

        _conv_loops(tm, D, block)
        cu = cu_ref[...]
        mu = jnp.mean(cu, axis=-1, keepdims=True)
        xc = cu - mu
        rstd = lax.rsqrt(jnp.mean(xc * xc, axis=-1, keepdims=True) + LN_EPS)
        ln = xc * rstd * g_ref[...] + b_ref[...]
        s = (ln * _sigmoid(ln)).astype(s_ref.dtype)
        s_ref[...] = s
        h_ref[...] = res_ref[...] + bo_ref[...] + jnp.dot(s, wo_ref[0], preferred_element_type=F32)

    vec = pl.BlockSpec((1, D), lambda i: (0, 0))
    row = pl.BlockSpec((tm, D), lambda i: (i, 0))
    (cu, s, h), xo = _call(
        body, name, (T // tm,),
        [pl.BlockSpec((tm, D2), lambda i: (i, 0)), _prev_halo_spec(tm, D2), pl.BlockSpec((K, D), lambda i: (0, 0)), vec, vec, vec,
         pl.BlockSpec((1, D, D), lambda i: (0, 0, 0)), vec, row],
        [row, row, row], [jax.ShapeDtypeStruct((T, D), F32), jax.ShapeDtypeStruct((T, D), BF16), jax.ShapeDtypeStruct((T, D), F32)],
        [u, u, w, b_conv, ln_g, ln_b, w_out, b_out, res], ("parallel",), [pltpu.VMEM((tm + HALO, D), F32)], hosted=hosted)
    return cu, s, h, xo


def pw2_ln_bwd(dy, w, cu, ln_g, ln_b, name, hosted=()):
    T, D = cu.shape
    tm = _tile(T, TOKEN_TILE)

    def body(dy_ref, w_ref, cu_ref, g_ref, b_ref, dcu_ref, dg_ref, db_ref, dbc_ref, dbo_ref):
        i = pl.program_id(0)
        dy_ = dy_ref[...]
        ds = lax.dot_general(dy_.astype(BF16), w_ref[0], _NT, preferred_element_type=F32)
        cu_ = cu_ref[...]
        mu = jnp.mean(cu_, axis=-1, keepdims=True)
        xc = cu_ - mu
        rstd = lax.rsqrt(jnp.mean(xc * xc, axis=-1, keepdims=True) + LN_EPS)
        xh = xc * rstd
        ln = xh * g_ref[...] + b_ref[...]
        sg = _sigmoid(ln)
        dl = ds * (sg * (1.0 + ln * (1.0 - sg)))
        dxh = dl * g_ref[...]
        dcu = rstd * (dxh - jnp.mean(dxh, axis=-1, keepdims=True) - xh * jnp.mean(dxh * xh, axis=-1, keepdims=True))
        dcu_ref[...] = dcu
        pg = jnp.sum(dl * xh, axis=0, keepdims=True)
        pb = jnp.sum(dl, axis=0, keepdims=True)
        pc = jnp.sum(dcu, axis=0, keepdims=True)
        po = jnp.sum(dy_, axis=0, keepdims=True)

        @pl.when(i == 0)
        def _():
            dg_ref[...] = pg
            db_ref[...] = pb
            dbc_ref[...] = pc
            dbo_ref[...] = po

        @pl.when(i > 0)
        def _():
            dg_ref[...] += pg
            db_ref[...] += pb
            dbc_ref[...] += pc
            dbo_ref[...] += po

    vec = pl.BlockSpec((1, D), lambda i: (0, 0))
    row = pl.BlockSpec((tm, D), lambda i: (i, 0))
    vshape = jax.ShapeDtypeStruct((1, D), F32)
    outs, xo = _call(
        body, name, (T // tm,), [row, pl.BlockSpec((1, D, D), lambda i: (0, 0, 0)), row, vec, vec], [row, vec, vec, vec, vec],
        [jax.ShapeDtypeStruct((T, D), F32), vshape, vshape, vshape, vshape], [dy, w, cu, ln_g, ln_b], ("arbitrary",),
        hosted=hosted)
    return (*outs, xo)


def bconv_bwd(dcu, u, w, name, hosted=()):
    T, D2 = u.shape
    D = D2 // 2
    K = w.shape[0]
    tm = _tile(T, TOKEN_TILE)
    nt = T // tm

    def body(dc_ref, dcn_ref, u_ref, up_ref, w_ref, du_ref, dw_ref, db_ref, glu_ref, dpad_ref, dglu_ref, wacc_ref):
        i = pl.program_id(0)
        glu_ref[HALO:, :] = u_ref[:, :D] * _sigmoid(u_ref[:, D:])
        glu_ref[:HALO, :] = jnp.where(i > 0, up_ref[:, :D] * _sigmoid(up_ref[:, D:]), 0.0)
        dpad_ref[:tm, :] = dc_ref[...]
        dpad_ref[tm:, :] = jnp.where(i < nt - 1, dcn_ref[...], 0.0)

        @pl.when(i == 0)
        def _():
            wacc_ref[...] = jnp.zeros_like(wacc_ref)

        def block(t0, ls):
            gwin = glu_ref[pl.ds(t0, CONV_ROWS + HALO), ls]
            dwin = dpad_ref[pl.ds(t0, CONV_ROWS + HALO), ls]
            dcur = dwin[:CONV_ROWS]
            dglu = jnp.zeros((CONV_ROWS, LANES), F32)
            for k in range(K):
                dglu = dglu + w_ref[k:k + 1, ls] * _shifted(dwin, (K - 1) - k, CONV_ROWS)
                gs = _shifted(gwin, HALO - (K - 1) + k, CONV_ROWS)
                wacc_ref[k * SUBLANES:(k + 1) * SUBLANES, ls] += _rowsum8(dcur * gs)
            dglu_ref[pl.ds(t0, CONV_ROWS), ls] = dglu

        _conv_loops(tm, D, block)
        dglu = dglu_ref[...]
        a = u_ref[:, :D]
        sg = _sigmoid(u_ref[:, D:])
        da = dglu * sg
        dg = dglu * a * (sg * (1.0 - sg))
        du_ref[:, :D] = da.astype(du_ref.dtype)
        du_ref[:, D:] = dg.astype(du_ref.dtype)
        pa = jnp.sum(da, axis=0, keepdims=True)
        pg = jnp.sum(dg, axis=0, keepdims=True)

        @pl.when(i == 0)
        def _():
            db_ref[:, :D] = pa
            db_ref[:, D:] = pg

        @pl.when(i > 0)
        def _():
            db_ref[:, :D] += pa
            db_ref[:, D:] += pg

        @pl.when(i == nt - 1)
        def _():
            for k in range(K):
                dw_ref[k:k + 1, :] = jnp.sum(wacc_ref[k * SUBLANES:(k + 1) * SUBLANES, :], axis=0, keepdims=True)

    (du, dw, db), xo = _call(
        body, name, (nt,),
        [pl.BlockSpec((tm, D), lambda i: (i, 0)), _next_halo_spec(tm, D, T),
         pl.BlockSpec((tm, D2), lambda i: (i, 0)), _prev_halo_spec(tm, D2), pl.BlockSpec((K, D), lambda i: (0, 0))],
        [pl.BlockSpec((tm, D2), lambda i: (i, 0)), pl.BlockSpec((K, D), lambda i: (0, 0)), pl.BlockSpec((1, D2), lambda i: (0, 0))],
        [jax.ShapeDtypeStruct((T, D2), BF16), jax.ShapeDtypeStruct((K, D), F32), jax.ShapeDtypeStruct((1, D2), F32)],
        [dcu, dcu, u, u, w], ("arbitrary",),
        [pltpu.VMEM((tm + HALO, D), F32), pltpu.VMEM((tm + HALO, D), F32), pltpu.VMEM((tm, D), F32),
         pltpu.VMEM((K * SUBLANES, D), F32)], hosted=hosted)
    return du, dw, db, xo


def mm_cols(a, w, bias, name, hosted=()):
    T, K = a.shape
    S, _, n = w.shape
    tm = _tile(T, WIDE_TOKEN_TILE)

    def body(*refs):
        a_ref, w_ref = refs[:2]
        o_ref = refs[-1]
        acc = jnp.dot(a_ref[...], w_ref[...], preferred_element_type=F32)
        if bias is not None:
            acc = acc + refs[2][...]
        o_ref[...] = acc

    in_specs = [pl.BlockSpec((tm, K), lambda s, i: (i, 0)), pl.BlockSpec((None, K, n), lambda s, i: (s, 0, 0))]
    args = [a, w]
    if bias is not None:
        in_specs.append(pl.BlockSpec((1, n), lambda s, i: (0, s)))
        args.append(bias)
    (out,), xo = _call(body, name, (S, T // tm), in_specs, [pl.BlockSpec((tm, n), lambda s, i: (i, s))],
                       [jax.ShapeDtypeStruct((T, S * n), F32)], args, ("parallel", "parallel"), hosted=hosted)
    return out, xo


def _load_weights(pairs, sems, S, G, i, p):
    def copies(seg):
        return [pltpu.make_async_copy(src.at[seg], dst.at[seg], sems.at[k, seg]) for k, (src, dst) in enumerate(pairs)]

    @pl.when((i == 0) & (p == 0))
    def _():
        for seg in range(S):
            for cp in copies(seg):
                cp.start()

    @pl.when((i == 0) & (p < S // G))
    def _():
        for j in range(G):
            for cp in copies(G * p + j):
                cp.wait()


def ffn_fwd(h, gain, weights, name, hosted=(), arriving=None):
    T, D = h.shape
    S, f, _ = weights[0].shape
    tm = _tile(T, TOKEN_TILE)
    rc = tm // FFN_ROW_CHUNKS
    chunks = [slice(r * rc, (r + 1) * rc) for r in range(FFN_ROW_CHUNKS)]
    G = FFN_FWD_SEGS_PER_STEP
    weights = list(weights)
    hosted = ([arriving.awaited_first()] if arriving is not None else []) + list(hosted)

    def body(h_ref, gain_ref, *refs):
        nw = len(weights)
        wg_hbm, wu_hbm, wd_hbm = list(refs[:nw]) + list(refs[nw + 9:])
        n_ref, g_ref, u_ref, gu_ref, o_ref, wg_v, wu_v, wd_v, sems = refs[nw:nw + 9]
        i, p = pl.program_id(0), pl.program_id(1)
        _load_weights([(wg_hbm, wg_v), (wu_hbm, wu_v), (wd_hbm, wd_v)], sems, S, G, i, p)

        @pl.when(p == 0)
        def _():
            x = h_ref[...]
            r = lax.rsqrt(jnp.mean(x * x, axis=-1, keepdims=True) + RMS_EPS)
            n_ref[...] = (x * r * gain_ref[...]).astype(n_ref.dtype)

        parts = []
        for rows in chunks:
            a = n_ref[rows, :]
            acc = None
            for j in range(G):
                seg = G * p + j
                g = lax.dot_general(a, wg_v[seg], _NT, preferred_element_type=F32)
                u = lax.dot_general(a, wu_v[seg], _NT, preferred_element_type=F32)
                gu = (g * _sigmoid(g) * u).astype(gu_ref.dtype)
                g_ref[j, rows, :] = g.astype(g_ref.dtype)
                u_ref[j, rows, :] = u.astype(u_ref.dtype)
                gu_ref[j, rows, :] = gu
                part = jnp.dot(gu, wd_v[seg], preferred_element_type=F32)
                acc = part if acc is None else acc + part
            parts.append(acc)

        @pl.when(p == 0)
        def _():
            for rows, part in zip(chunks, parts):
                o_ref[rows, :] = h_ref[rows, :] + part

        @pl.when(p > 0)
        def _():
            for rows, part in zip(chunks, parts):
                o_ref[rows, :] += part

    row = pl.BlockSpec((tm, D), lambda i, p: (i, 0))
    seg = pl.BlockSpec((G, tm, f), lambda i, p: (p, i, 0))
    hbm = pl.BlockSpec(memory_space=pl.ANY)
    segs = jax.ShapeDtypeStruct((S, T, f), BF16)
    outs, xo = _call(
        body, name, (T // tm, S // G),
        [row, pl.BlockSpec((1, D), lambda i, s: (0, 0))] + [hbm] * len(weights), [row, seg, seg, seg, row],
        [jax.ShapeDtypeStruct((T, D), BF16), segs, segs, segs, jax.ShapeDtypeStruct((T, D), F32)],
        [h, gain] + weights, ("arbitrary", "arbitrary"),
        [pltpu.VMEM((S, f, D), BF16), pltpu.VMEM((S, f, D), BF16), pltpu.VMEM((S, f, D), BF16), pltpu.SemaphoreType.DMA((3, S))],
        hosted=hosted)
    return (*outs, xo)


def ffn_bwd(dy, h, gain, g, u, wd, wg, wu, name, hosted=()):
    T, D = h.shape
    S, f, _ = wg.shape
    tm = _tile(T, FFN_BWD_TOKEN_TILE)
    nt = T // tm

    def body(dy_ref, h_ref, gain_ref, g_ref, u_ref, wd_hbm, wg_hbm, wu_hbm, dg_ref, du_ref, dh_ref, dhb_ref, dgain_ref,
             wd_v, wg_v, wu_v, sems):
        i = pl.program_id(0)
        _load_weights([(wd_hbm, wd_v), (wg_hbm, wg_v), (wu_hbm, wu_v)], sems, S, S, i, 0)
        dy_ = dy_ref[...]
        dyb = dy_.astype(BF16)
        dn = None
        for j in range(S):
            dgu = lax.dot_general(dyb, wd_v[j], _NT, preferred_element_type=F32)
            gv = g_ref[j].astype(F32)
            sg = _sigmoid(gv)
            dg = (dgu * u_ref[j].astype(F32) * (sg * (1.0 + gv * (1.0 - sg)))).astype(dg_ref.dtype)
            du = (dgu * (gv * sg)).astype(du_ref.dtype)
            dg_ref[j] = dg
            du_ref[j] = du
            part = jnp.dot(dg, wg_v[j], preferred_element_type=F32) + jnp.dot(du, wu_v[j], preferred_element_type=F32)
            dn = part if dn is None else dn + part
        x = h_ref[...]
        r = lax.rsqrt(jnp.mean(x * x, axis=-1, keepdims=True) + RMS_EPS)
        xhat = x * r
        dxhat = dn * gain_ref[...]
        dh = dy_ + r * (dxhat - xhat * jnp.mean(dxhat * xhat, axis=-1, keepdims=True))
        dh_ref[...] = dh
        dhb_ref[...] = dh.astype(dhb_ref.dtype)
        pg = jnp.sum(dn * xhat, axis=0, keepdims=True)

        @pl.when(i == 0)
        def _():
            dgain_ref[...] = pg

        @pl.when(i > 0)
        def _():
            dgain_ref[...] += pg

    row = pl.BlockSpec((tm, D), lambda i: (i, 0))
    vec = pl.BlockSpec((1, D), lambda i: (0, 0))
    seg = pl.BlockSpec((S, tm, f), lambda i: (0, i, 0))
    hbm = pl.BlockSpec(memory_space=pl.ANY)
    segs = jax.ShapeDtypeStruct((S, T, f), BF16)
    outs, xo = _call(
        body, name, (nt,),
        [row, row, vec, seg, seg, hbm, hbm, hbm], [seg, seg, row, row, vec],
        [segs, segs, jax.ShapeDtypeStruct((T, D), F32), jax.ShapeDtypeStruct((T, D), BF16), jax.ShapeDtypeStruct((1, D), F32)],
        [dy, h, gain, g, u, wd, wg, wu], ("arbitrary",),
        [pltpu.VMEM((S, f, D), BF16), pltpu.VMEM((S, f, D), BF16), pltpu.VMEM((S, f, D), BF16),
         pltpu.SemaphoreType.DMA((3, S))], hosted=hosted)
    return (*outs, xo)


_NT = (((1,), (1,)), ((), ()))
_TN = (((0,), (0,)), ((), ()))


def nt_rows(dy, w, name, hosted=()):
    T, N = dy.shape
    S, k, _ = w.shape
    tm = _tile(T, TOKEN_TILE)

    def body(dy_ref, w_ref, o_ref):
        o_ref[...] = lax.dot_general(dy_ref[...].astype(BF16), w_ref[...], _NT, preferred_element_type=F32)

    (out,), xo = _call(
        body, name, (T // tm, S),
        [pl.BlockSpec((tm, N), lambda i, s: (i, 0)), pl.BlockSpec((None, k, N), lambda i, s: (s, 0, 0))],
        [pl.BlockSpec((None, tm, k), lambda i, s: (s, i, 0))], [jax.ShapeDtypeStruct((S, T, k), F32)],
        [dy, w], ("parallel", "parallel"), hosted=hosted)
    return out, xo


def nt_cols_rms(dy, w, h, gain, dres, name, hosted=(), also_bf16=False):
    T, K = h.shape
    S, _, n = w.shape
    tm = _tile(T, TOKEN_TILE)

    def body(dy_ref, w_ref, h_ref, gain_ref, dres_ref, dh_ref, dgain_ref, *rest):
        i = pl.program_id(0)
        dn = None
        for s in range(S):
            part = lax.dot_general(dy_ref[:, s * n:(s + 1) * n], w_ref[s], _NT, preferred_element_type=F32)
            dn = part if dn is None else dn + part
        x = h_ref[...]
        r = lax.rsqrt(jnp.mean(x * x, axis=-1, keepdims=True) + RMS_EPS)
        xhat = x * r
        dxhat = dn * gain_ref[...]
        dh = dres_ref[...] + r * (dxhat - xhat * jnp.mean(dxhat * xhat, axis=-1, keepdims=True))
        dh_ref[...] = dh
        if also_bf16:
            rest[0][...] = dh.astype(BF16)
        pg = jnp.sum(dn * xhat, axis=0, keepdims=True)

        @pl.when(i == 0)
        def _():
            dgain_ref[...] = pg

        @pl.when(i > 0)
        def _():
            dgain_ref[...] += pg

    row = pl.BlockSpec((tm, K), lambda i: (i, 0))
    vec = pl.BlockSpec((1, K), lambda i: (0, 0))
    out_specs, out_shape = [row, vec], [jax.ShapeDtypeStruct((T, K), F32), jax.ShapeDtypeStruct((1, K), F32)]
    if also_bf16:
        out_specs, out_shape = out_specs + [row], out_shape + [jax.ShapeDtypeStruct((T, K), BF16)]
    outs, xo = _call(
        body, name, (T // tm,),
        [pl.BlockSpec((tm, S * n), lambda i: (i, 0)), pl.BlockSpec((S, K, n), lambda i: (0, 0, 0)), row, vec, row],
        out_specs, out_shape, [dy, w, h, gain, dres], ("arbitrary",), hosted=hosted)
    return (*outs, xo)


def tn_grad(a, dy, S, a_by_seg, name, hosted=()):
    T = dy.shape[0] if dy.ndim == 2 else dy.shape[1]
    tt = _tile(T, GRAD_TOKEN_TILE)
    if a_by_seg:
        R = a.shape[1] // S if a.ndim == 2 else a.shape[2]
        C = dy.shape[1]
        a_spec = pl.BlockSpec((tt, R), lambda s, t: (t, s)) if a.ndim == 2 else pl.BlockSpec((None, tt, R), lambda s, t: (s, t, 0))
        b_spec = pl.BlockSpec((tt, C), lambda s, t: (t, 0))
    else:
        R = a.shape[1]
        C = dy.shape[1] // S if dy.ndim == 2 else dy.shape[2]
        a_spec = pl.BlockSpec((tt, R), lambda s, t: (t, 0))
        b_spec = pl.BlockSpec((tt, C), lambda s, t: (t, s)) if dy.ndim == 2 else pl.BlockSpec((None, tt, C), lambda s, t: (s, t, 0))
    Rh = R // 2
    nt = T // tt

    def body(a_ref, b_ref, o_ref, acc_ref):
        t = pl.program_id(1)
        part = lax.dot_general(a_ref[...], b_ref[...].astype(BF16), _TN, preferred_element_type=F32)

        @pl.when(t == 0)
        def _():
            acc_ref[...] = part

        @pl.when(t > 0)
        def _():
            acc_ref[...] += part

        @pl.when(t == nt - 1)
        def _():
            o_ref[0] = acc_ref[:Rh, :].astype(o_ref.dtype)
            o_ref[1] = acc_ref[Rh:, :].astype(o_ref.dtype)

    (gh,), xo = _call(
        body, name, (S, nt), [a_spec, b_spec], [pl.BlockSpec((2, None, Rh, C), lambda s, t: (0, s, 0, 0))],
        [jax.ShapeDtypeStruct((2, S, Rh, C), BF16)], [a, dy], ("parallel", "arbitrary"), [pltpu.VMEM((R, C), F32)],
        hosted=hosted)
    return gh, xo


def tn_grad_square(a, dy, S, name, hosted=()):
    T, K = a.shape
    N = dy.shape[1]
    tt = _tile(T, GRAD_TOKEN_TILE)
    nt = T // tt
    Rh = K // S // 2

    def body(a_ref, b_ref, o_ref, acc_ref):
        t = pl.program_id(0)
        part = lax.dot_general(a_ref[...], b_ref[...].astype(BF16), _TN, preferred_element_type=F32)

        @pl.when(t == 0)
        def _():
            acc_ref[...] = part

        @pl.when(t > 0)
        def _():
            acc_ref[...] += part

        @pl.when(t == nt - 1)
        def _():
            for s in range(S):
                for hf in range(2):
                    r0 = (2 * s + hf) * Rh
                    o_ref[hf, s] = acc_ref[r0:r0 + Rh, :].astype(o_ref.dtype)

    (gh,), xo = _call(
        body, name, (nt,), [pl.BlockSpec((tt, K), lambda t: (t, 0)), pl.BlockSpec((tt, N), lambda t: (t, 0))],
        [pl.BlockSpec((2, S, Rh, N), lambda t: (0, 0, 0, 0))], [jax.ShapeDtypeStruct((2, S, Rh, N), BF16)],
        [a, dy], ("arbitrary",), [pltpu.VMEM((K, N), F32)], hosted=hosted)
    return gh, xo


def _place():
    x, y, c = lax.axis_index("x"), lax.axis_index("y"), lax.axis_index("c")
    chips = [(1 - x, y), (x, 1 - y), (1 - x, 1 - y)]
    return x, y, c, chips


def _remote(src, dst, send_sem, recv_sem, dev):
    return pltpu.make_async_remote_copy(src_ref=src, dst_ref=dst, send_sem=send_sem, recv_sem=recv_sem,
                                        device_id=dev, device_id_type=MESH)


def small_allreduce(v, name, hosted=()):
    rows, W = v.shape

    def body(v_ref, o_ref, sib_ref, pair_ref, chips_ref, send_sems, recv_sems):
        x, y, c, chips = _place()
        me = 2 * x + y
        swap = _remote(v_ref, sib_ref, send_sems.at[3], recv_sems.at[3], (x, y, 1 - c))
        swap.start()
        swap.wait()
        mine, other = v_ref[...], sib_ref[...]
        pair_ref[...] = jnp.where(c == 0, mine, other) + jnp.where(c == 0, other, mine)
        sends = []
        for j, (px, py) in enumerate(chips):
            cp = _remote(pair_ref, chips_ref.at[me], send_sems.at[j], recv_sems.at[j], (px, py, c))
            cp.start()
            sends.append(cp)
        chips_ref[me] = pair_ref[...]
        for j, (px, py) in enumerate(chips):
            blk = chips_ref.at[2 * px + py]
            _remote(blk, blk, send_sems.at[j], recv_sems.at[j], (px, py, c)).wait_recv()
        for cp in sends:
            cp.wait_send()
        o_ref[...] = (chips_ref[0] + chips_ref[1]) + (chips_ref[2] + chips_ref[3])

    vm = pl.BlockSpec(memory_space=pltpu.VMEM)
    (out,), xo = _call(
        body, name, (), [vm], [vm], [jax.ShapeDtypeStruct((rows, W), F32)], [v], (),
        [pltpu.VMEM((rows, W), F32), pltpu.VMEM((rows, W), F32), pltpu.VMEM((N_CHIPS, rows, W), F32),
         pltpu.SemaphoreType.DMA((4,)), pltpu.SemaphoreType.DMA((4,))], hosted=hosted)
    return out, xo


def _gather_p1_copies(srcs, bufs, ssem, rsem, base):
    x, y, c, chips = _place()
    me, sib = 2 * x + y, (x, y, 1 - c)
    sends, recvs = [], []
    for k, (src, buf) in enumerate(zip(srcs, bufs)):
        rh = src.shape[0] // 2
        s0 = base + 4 * k
        sends.append(_remote(src, buf.at[me], ssem.at[s0 + 3], rsem.at[s0 + 3], sib))
        recvs.append(_remote(buf.at[me], buf.at[me], ssem.at[s0 + 3], rsem.at[s0 + 3], sib))
        for j, (px, py) in enumerate(chips):
            sends.append(_remote(src.at[pl.ds(c * rh, rh)], buf.at[me, pl.ds(c * rh, rh)], ssem.at[s0 + j], rsem.at[s0 + j], (px, py, c)))
            blk = buf.at[2 * px + py, pl.ds(c * rh, rh)]
            recvs.append(_remote(blk, blk, ssem.at[s0 + j], rsem.at[s0 + j], (px, py, c)))
    return sends, recvs


def _gather_p2_copies(bufs, ssem, rsem, base):
    x, y, c, chips = _place()
    sib = (x, y, 1 - c)
    sends, recvs = [], []
    for k, buf in enumerate(bufs):
        rh = buf.shape[1] // 2
        for j, (px, py) in enumerate(chips):
            s0 = base + 3 * k + j
            blk = buf.at[2 * px + py, pl.ds(c * rh, rh)]
            sends.append(_remote(blk, blk, ssem.at[s0], rsem.at[s0], sib))
            got = buf.at[2 * px + py, pl.ds((1 - c) * rh, rh)]
            recvs.append(_remote(got, got, ssem.at[s0], rsem.at[s0], sib))
    return sends, recvs


def _gathered_shape(s):
    return jax.ShapeDtypeStruct((N_CHIPS,) + s.shape, s.dtype)


def gather_p1(shards):
    return _Exchange(shards, [_gathered_shape(s) for s in shards], {}, 4 * len(shards),
                     lambda xi, xo, ss, rs: _gather_p1_copies(xi, xo, ss, rs, 0))


def gather_p2(bufs):
    return _Exchange(bufs, [jax.ShapeDtypeStruct(b.shape, b.dtype) for b in bufs], {k: k for k in range(len(bufs))},
                     3 * len(bufs), lambda xi, xo, ss, rs: _gather_p2_copies(xo, ss, rs, 0))


def gather_whole(whole, begun):
    nw, n = len(whole), len(whole) + len(begun)
    shards = list(whole) + list(begun)
    return _Exchange(shards, [_gathered_shape(s) for s in shards], {}, 4 * n + 3 * nw,
                     lambda xi, xo, ss, rs: _gather_p1_copies(xi, xo, ss, rs, 0),
                     then=lambda xi, xo, ss, rs: _gather_p2_copies(xo[:nw], ss, rs, 4 * n))


def gather_small(v):
    def copies(xi, xo, ssem, rsem):
        x, y, c, chips = _place()
        me, sib = 2 * x + y, (x, y, 1 - c)
        sends = [_remote(xi[0], xo[0].at[me], ssem.at[3], rsem.at[3], sib)]
        recvs = [_remote(xo[0].at[me], xo[0].at[me], ssem.at[3], rsem.at[3], sib)]
        for j, (px, py) in enumerate(chips):
            sends.append(_remote(xi[0], xo[0].at[me], ssem.at[j], rsem.at[j], (px, py, c)))
            blk = xo[0].at[2 * px + py]
            recvs.append(_remote(blk, blk, ssem.at[j], rsem.at[j], (px, py, c)))
        return sends, recvs

    return _Exchange([v], [_gathered_shape(v)], {}, 4, copies)


def run_exchanges(exchanges, name):
    return _call(lambda: None, name, (), [], [], [], [], (), hosted=exchanges)[1]


def sibling_halves(grads):
    def copies(xi, xo, ssem, rsem):
        x, y, c, _ = _place()
        sends = [_remote(xi[k].at[1 - c], xo[k], ssem.at[k], rsem.at[k], (x, y, 1 - c)) for k in range(len(grads))]
        return sends, sends

    return _Exchange(grads, [jax.ShapeDtypeStruct(g.shape[1:], g.dtype) for g in grads], {}, len(grads), copies)


def pair_sum(ghs, recvs, cidx, name):
    n = len(ghs)
    S = ghs[0].shape[1]

    def body(c_ref, *refs):
        for k in range(n):
            a_ref, b_ref, o_ref = refs[2 * k], refs[2 * k + 1], refs[2 * n + k]
            o_ref[...] = (a_ref[...].astype(F32) + b_ref[...].astype(F32)).astype(o_ref.dtype)

    in_specs, out_specs, out_shape, args = [], [], [], []
    for gh, recv in zip(ghs, recvs):
        _, _, Rh, C = gh.shape
        in_specs += [pl.BlockSpec((None, None, Rh, C), lambda s, c_ref: (c_ref[0], s, 0, 0)),
                     pl.BlockSpec((None, Rh, C), lambda s, c_ref: (s, 0, 0))]
        out_specs.append(pl.BlockSpec((None, Rh, C), lambda s, c_ref: (s, 0, 0)))
        out_shape.append(jax.ShapeDtypeStruct((S, Rh, C), BF16))
        args += [gh, recv]
    return pl.pallas_call(
        body, name=name, out_shape=out_shape,
        grid_spec=pltpu.PrefetchScalarGridSpec(num_scalar_prefetch=1, grid=(S,), in_specs=in_specs, out_specs=out_specs),
        compiler_params=_params(("parallel",)),
    )(cidx, *args)


def scatter_p1(parts):
    def copies(xi, xo, ssem, rsem):
        x, y, c, chips = _place()
        me, sib = 2 * x + y, (x, y, 1 - c)
        sends, recvs = [], []
        for k in range(len(parts)):
            s0 = 4 * k
            sends.append(_remote(xi[k].at[me], xo[k].at[me, c], ssem.at[s0 + 3], rsem.at[s0 + 3], sib))
            own = xo[k].at[me, 1 - c]
            recvs.append(_remote(own, own, ssem.at[s0 + 3], rsem.at[s0 + 3], sib))
            for j, (px, py) in enumerate(chips):
                sends.append(_remote(xi[k].at[2 * px + py], xo[k].at[me, c], ssem.at[s0 + j], rsem.at[s0 + j], (px, py, c)))
                blk = xo[k].at[2 * px + py, c]
                recvs.append(_remote(blk, blk, ssem.at[s0 + j], rsem.at[s0 + j], (px, py, c)))
        return sends, recvs

    return _Exchange(parts, [jax.ShapeDtypeStruct((p.shape[0], 2) + p.shape[1:], p.dtype) for p in parts], {},
                     4 * len(parts), copies)


def scatter_p2(bufs):
    def copies(xi, xo, ssem, rsem):
        x, y, c, chips = _place()
        sib = (x, y, 1 - c)
        sends, recvs = [], []
        for k in range(len(bufs)):
            for j, (px, py) in enumerate(chips):
                s0 = 3 * k + j
                blk = xo[k].at[2 * px + py, c]
                sends.append(_remote(blk, blk, ssem.at[s0], rsem.at[s0], sib))
                got = xo[k].at[2 * px + py, 1 - c]
                recvs.append(_remote(got, got, ssem.at[s0], rsem.at[s0], sib))
        return sends, recvs

    return _Exchange(bufs, [jax.ShapeDtypeStruct(b.shape, b.dtype) for b in bufs], {k: k for k in range(len(bufs))},
                     3 * len(bufs), copies)


def _adamw_math(w, g, m, v):
    m = ADAM_B1 * m + (1.0 - ADAM_B1) * g
    v = ADAM_B2 * v + (1.0 - ADAM_B2) * (g * g)
    m_hat = m / (1.0 - ADAM_B1 ** ADAM_STEP)
    v_hat = v / (1.0 - ADAM_B2 ** ADAM_STEP)
    delta = -ADAM_LR * (m_hat / (jnp.sqrt(v_hat) + ADAM_EPS) + ADAM_WD * w)
    return delta, m, v


def adamw_reduce(w, m, v, buf, part, place, lyr, bases, name, hosted=()):
    L, R, C = w.shape
    Rh = R // 2
    rb = _tile(Rh, ROW_TILE, 2 * SUBLANES)
    nb = Rh // rb

    def body(place_ref, p_ref, b0, b1, b2, b3, w_ref, m_ref, v_ref, *rest):
        go_ref, d_ref, mo_ref, vo_ref = rest[-4:]
        mine = (place_ref[1] == pl.program_id(0))
        g = None
        for p, b in enumerate((b0, b1, b2, b3)):
            val = jnp.where(mine & (place_ref[0] == p), p_ref[...], b[...]).astype(F32)
            g = val if g is None else g + val
        d, mn, vn = _adamw_math(w_ref[...], g, m_ref[...], v_ref[...])
        go_ref[...] = g
        d_ref[...] = d
        mo_ref[...] = mn
        vo_ref[...] = vn

    def buf_spec(p):
        def idx(h, i, pr):
            own = (pr[0] == p) & (pr[1] == h)
            return (p, jnp.where(own, 1 - h, h), i, 0)
        return pl.BlockSpec((None, None, rb, C), idx)

    blk = pl.BlockSpec((None, rb, C), lambda h, i, pr: (lyr, h * nb + i, 0))
    in_specs = [pl.BlockSpec((None, rb, C), lambda h, i, pr: (pr[0], i, 0))] + [buf_spec(p) for p in range(N_CHIPS)] + [blk] * 3
    args = [part, buf, buf, buf, buf, w, m, v]
    aliases = {}
    if bases is not None:
        in_specs += [pl.BlockSpec(memory_space=pl.ANY)] * 4
        aliases = {len(args) + k: k for k in range(4)}
        args += list(bases)
    shp = jax.ShapeDtypeStruct((L, R, C), F32)
    return _call(body, name, (2, nb), in_specs, [blk] * 4, [shp] * 4, args, ("parallel", "parallel"),
                 hosted=hosted, prefetch=[place], own_aliases=aliases)


def small_update(gall, chip, entries, name):
    ne = len(entries)
    D = gall.shape[1]

    def body(chip_ref, gall_ref, *refs):
        ins, outs = refs[:3 * ne], refs[3 * ne:]
        ch = chip_ref[0]
        for e, (row0, kind, w, _, _) in enumerate(entries):
            r, width = w.shape

            def gsum(rs, cs):
                return gall_ref[rs, cs]

            if kind == "full":
                g = gsum(slice(row0, row0 + r), slice(0, D))
            elif kind == "cols":
                g = gsum(slice(row0, row0 + r), slice(0, width))
                for q in range(1, N_CHIPS):
                    g = jnp.where(ch == q, gsum(slice(row0, row0 + r), slice(q * width, (q + 1) * width)), g)
            else:
                per_row = D // width
                g = gsum(slice(row0, row0 + 1), slice(0, width))
                for q in range(1, N_CHIPS):
                    rr = row0 + q // per_row
                    cc = (q % per_row) * width
                    g = jnp.where(ch == q, gsum(slice(rr, rr + 1), slice(cc, cc + width)), g)
            d, mn, vn = _adamw_math(ins[3 * e][...], g, ins[3 * e + 1][...], ins[3 * e + 2][...])
            outs[4 * e][...] = g
            outs[4 * e + 1][...] = d
            outs[4 * e + 2][...] = mn
            outs[4 * e + 3][...] = vn

    vm = pl.BlockSpec(memory_space=pltpu.VMEM)
    args, out_shape = [], []
    for _, _, w, m, v in entries:
        args += [w, m, v]
        out_shape += [jax.ShapeDtypeStruct(w.shape, F32)] * 4
    return pl.pallas_call(
        body, name=name,
        in_specs=[pl.BlockSpec(memory_space=pltpu.SMEM), vm] + [vm] * (3 * ne),
        out_specs=[vm] * (4 * ne), out_shape=out_shape,
        compiler_params=pltpu.CompilerParams(vmem_limit_bytes=VMEM_LIMIT),
    )(chip, gall, *args)


def _pack_rows(items, width, name):
    starts, at = [], 0
    for it in items:
        starts.append(at)
        at += -(-it.shape[0] // SUBLANES) * SUBLANES
    total = at

    def body(*refs):
        o_ref = refs[-1]
        o_ref[...] = jnp.zeros_like(o_ref)
        for it_ref, r0 in zip(refs[:-1], starts):
            o_ref[r0:r0 + it_ref.shape[0], :] = it_ref[...]

    vm = pl.BlockSpec(memory_space=pltpu.VMEM)
    packed = pl.pallas_call(body, name=name, in_specs=[vm] * len(items), out_specs=vm,
                            out_shape=jax.ShapeDtypeStruct((total, width), F32))(*items)
    return packed, starts


def kernel(x, a_norm, a_w_in, a_conv, a_w_out, b_norm, b_w_pw1, b_b_pw1, b_conv, b_b_conv, b_ln_g, b_ln_b, b_w_pw2, b_b_pw2, ffn_norm, ffn_w_gate, ffn_w_up, ffn_w_down, final_norm, loss_target, m_a_norm, m_a_w_in, m_a_conv, m_a_w_out, m_b_norm, m_b_w_pw1, m_b_b_pw1, m_b_conv, m_b_b_conv, m_b_ln_g, m_b_ln_b, m_b_w_pw2, m_b_b_pw2, m_ffn_norm, m_ffn_w_gate, m_ffn_w_up, m_ffn_w_down, m_final_norm, v_a_norm, v_a_w_in, v_a_conv, v_a_w_out, v_b_norm, v_b_w_pw1, v_b_b_pw1, v_b_conv, v_b_b_conv, v_b_ln_g, v_b_ln_b, v_b_w_pw2, v_b_b_pw2, v_ffn_norm, v_ffn_w_gate, v_ffn_w_up, v_ffn_w_down, v_final_norm):
    T, D = x.shape[1], x.shape[2]
    Dq = D // N_CHIPS
    cx, cy, cc = lax.axis_index("x"), lax.axis_index("y"), lax.axis_index("c")
    chip = (2 * cx + cy).astype(jnp.int32).reshape(1)
    cidx = cc.astype(jnp.int32).reshape(1)
    h0 = x.reshape(T, D)
    tgt = loss_target.reshape(T, D)

    small_shards = [a_conv[0], b_norm, b_b_pw1.reshape(2, Dq), b_conv[0], b_b_conv, b_ln_g, b_ln_b, b_b_pw2]
    packed, st = _pack_rows(small_shards, Dq, "pack_small")

    tr = lambda t: jnp.swapaxes(t, 1, 2)
    w_gate, m_gate, v_gate = tr(ffn_w_gate), tr(m_ffn_w_gate), tr(v_ffn_w_gate)
    w_up, m_up, v_up = tr(ffn_w_up), tr(m_ffn_w_up), tr(v_ffn_w_up)
    bf = lambda t: t.astype(BF16)
    s_in, s_out, s_pw1, s_pw2 = bf(a_w_in[0]), bf(a_w_out[0]), bf(b_w_pw1[0]), bf(b_w_pw2[0])
    s_gate, s_up, s_down = [bf(w_gate[l]) for l in (0, 1)], [bf(w_up[l]) for l in (0, 1)], [bf(ffn_w_down[l]) for l in (0, 1)]

    n0, (g_in,) = rms_fwd(h0, a_norm, "rms_a", hosted=[gather_whole([s_in], [])])
    bcv, (g_out, gate0, sw) = mm_cols(n0, g_in, None, "mm_w_in", hosted=[gather_p1([s_out, s_gate[0]]), gather_small(packed)])

    def whole(k, r):
        return jnp.transpose(sw[:, st[k]:st[k] + r, :], (1, 0, 2)).reshape(r, D)

    a_conv_f, b_norm_f = whole(0, 3), whole(1, 1)
    b_b_pw1_f = sw[:, st[2]:st[2] + 2, :].reshape(1, 2 * D)
    b_conv_f, b_b_conv_f, b_ln_g_f, b_ln_b_f, b_b_pw2_f = whole(3, b_conv.shape[1]), whole(4, 1), whole(5, 1), whole(6, 1), whole(7, 1)
    ya, h1, (g_out, up0, down0, gate0) = gateconv_fwd(bcv, a_conv_f, gather_p2([g_out]), h0, "gateconv_fwd",
                                                      hosted=[gather_p1([s_up[0], s_down[0]]), gather_p2([gate0])])
    g_out = g_out.reshape(1, D, D)
    n1, fg0, fu0, gu0, h2, (up0, down0, *later) = ffn_fwd(h1, ffn_norm[0:1], [gate0], "ffn_fwd0", arriving=gather_p2([up0, down0]),
                                                          hosted=[gather_p1([s_pw1, s_pw2, s_gate[1], s_up[1]])])
    n2, (g_pw1, g_pw2, gate1, up1) = rms_fwd(h2, b_norm_f, "rms_b", hosted=[gather_p2(later)])
    g_pw2 = g_pw2.reshape(1, D, D)
    ub, (down1,) = mm_cols(n2, g_pw1, b_b_pw1_f, "mm_pw1", hosted=[gather_p1([s_down[1]])])
    cu, sb, h3, (down1,) = bconv_fwd(ub, b_conv_f, b_b_conv_f, b_ln_g_f, b_ln_b_f, g_pw2, b_b_pw2_f, h2, "bconv_fwd",
                                     hosted=[gather_p2([down1])])
    n3, fg1, fu1, gu1, h4, _ = ffn_fwd(h3, ffn_norm[1:2], [gate1, up1, down1], "ffn_fwd1")
    loss_part, dh4, dh4_b, d_final = loss_head(h4, final_norm.reshape(1, D), tgt, "loss_head")

    place = jnp.concatenate([chip, cidx])

    def pair_sums(ghs, from_sib, tags):
        return pair_sum(ghs, from_sib, cidx, "pair_sum_" + "_".join(tags))

    def upd(w, m, v, bufs, parts, tag, hosted=()):
        res, xo = None, []
        for lyr, (b, p) in enumerate(zip(bufs, parts)):
            res, xo_l = adamw_reduce(w, m, v, b, p, place, lyr, res, "adamw_%s%d" % (tag, lyr), hosted=hosted if lyr == 0 else ())
            xo += xo_l
        return res, xo

    dg1, du1, dh3, dh3_b, d_fn1, _ = ffn_bwd(dh4, h3, ffn_norm[1:2], fg1, fu1, down1, gate1, up1, "ffn_bwd1")
    gh_down1, _ = tn_grad(gu1, dh4_b, N_CHIPS, True, "tn_down1")
    gh_gate1, _ = tn_grad(dg1, n3, N_CHIPS, True, "tn_gate1")
    gh_up1, _ = tn_grad(du1, n3, N_CHIPS, True, "tn_up1")
    f1 = [gh_gate1, gh_up1, gh_down1]

    dcu, d_ln_g, d_ln_b, d_b_conv, d_b_pw2, sib_f1 = pw2_ln_bwd(dh3, g_pw2, cu, b_ln_g_f, b_ln_b_f, "pw2_ln_bwd",
                                                                hosted=[sibling_halves(f1)])
    p_f1 = pair_sums(f1, sib_f1, ["gate1", "up1", "down1"])
    gh_pw2, _ = tn_grad_square(sb, dh3_b, N_CHIPS, "tn_pw2")
    dub, d_bconv_w, d_b_pw1, buf_f1 = bconv_bwd(dcu, ub, b_conv_f, "bconv_bwd", hosted=[scatter_p1(p_f1)])
    gh_pw1, _ = tn_grad(n2, dub, N_CHIPS, False, "tn_pw1")
    b_grp = [gh_pw1, gh_pw2]
    dh2, d_b_norm, dh2_b, (*buf_f1, sib_pw1, sib_pw2) = nt_cols_rms(
        dub, g_pw1, h2, b_norm_f, dh3, "nt_pw1", hosted=[scatter_p2(buf_f1), sibling_halves(b_grp)], also_bf16=True)
    sib_b = [sib_pw1, sib_pw2]
    p_b = pair_sums(b_grp, sib_b, ["pw1", "pw2"])

    dg0, du0, dh1, dh1_b, d_fn0, buf_b = ffn_bwd(dh2, h1, ffn_norm[0:1], fg0, fu0, down0, gate0, up0, "ffn_bwd0",
                                                 hosted=[scatter_p1(p_b)])
    gh_down0, _ = tn_grad(gu0, dh2_b, N_CHIPS, True, "tn_down0")
    gh_gate0, (*buf_b, sib_down0) = tn_grad(dg0, n1, N_CHIPS, True, "tn_gate0",
                                            hosted=[scatter_p2(buf_b), sibling_halves([gh_down0])])
    p_down0 = pair_sums([gh_down0], [sib_down0], ["down0"])
    gh_up0, (buf_down0, sib_gate0) = tn_grad(du0, n1, N_CHIPS, True, "tn_up0",
                                             hosted=[scatter_p1(p_down0), sibling_halves([gh_gate0])])
    p_gate0 = pair_sums([gh_gate0], [sib_gate0], ["gate0"])
    dya, (buf_down0, sib_up0) = nt_rows(dh1, g_out, "nt_w_out",
                                        hosted=[scatter_p2([buf_down0]), sibling_halves([gh_up0])])
    p_up0 = pair_sums([gh_up0], [sib_up0], ["up0"])
    gh_out, _ = tn_grad_square(ya, dh1_b, N_CHIPS, "tn_w_out")
    dbcv, d_aconv_w, (buf_gate0, sib_out) = gateconv_bwd(dya[0], bcv, a_conv_f, "gateconv_bwd",
                                                         hosted=[scatter_p1(p_gate0), sibling_halves([gh_out])])
    p_out = pair_sums([gh_out], [sib_out], ["out"])
    gh_in, (buf_up0, buf_out, buf_gate0) = tn_grad(n0, dbcv, N_CHIPS, False, "tn_w_in",
                                                   hosted=[scatter_p1(p_up0 + p_out), scatter_p2([buf_gate0])])
    sib_in = run_exchanges([sibling_halves([gh_in])], "reduce_in_siblings")
    p_in = pair_sums([gh_in], sib_in, ["in"])
    grad_x, d_a_norm, (buf_in, buf_up0, buf_out) = nt_cols_rms(
        dbcv, g_in, h0, a_norm, dh1, "nt_w_in", hosted=[scatter_p1(p_in), scatter_p2([buf_up0, buf_out])])
    p_f0 = [p_gate0[0], p_up0[0], p_down0[0]]

    d_ffn_norm = jnp.concatenate([d_fn0, d_fn1], axis=0)
    small_grads = [d_a_norm, d_aconv_w, d_b_norm, d_b_pw1.reshape(2, D), d_bconv_w, d_b_conv, d_ln_g, d_ln_b, d_b_pw2,
                   d_ffn_norm, d_final, jnp.broadcast_to(loss_part, (1, D))]
    gpacked, gs = _pack_rows(small_grads, D, "pack_small_grads")
    gall, (buf_in,) = small_allreduce(gpacked, "allreduce_small_grads", hosted=[scatter_p2([buf_in])])
    buf_a, p_a = [buf_in, buf_out], [p_in[0], p_out[0]]

    r_gate, _ = upd(w_gate, m_gate, v_gate, [buf_gate0, buf_f1[0]], [p_f0[0], p_f1[0]], "gate")
    r_up, _ = upd(w_up, m_up, v_up, [buf_up0, buf_f1[1]], [p_f0[1], p_f1[1]], "up")
    r_down, _ = upd(ffn_w_down, m_ffn_w_down, v_ffn_w_down, [buf_down0, buf_f1[2]], [p_f0[2], p_f1[2]], "down")
    r_gate, r_up = [tr(t) for t in r_gate], [tr(t) for t in r_up]
    r_pw1, _ = upd(b_w_pw1, m_b_w_pw1, v_b_w_pw1, [buf_b[0]], [p_b[0]], "pw1")
    r_pw2, _ = upd(b_w_pw2, m_b_w_pw2, v_b_w_pw2, [buf_b[1]], [p_b[1]], "pw2")
    r_in, _ = upd(a_w_in, m_a_w_in, v_a_w_in, [buf_a[0]], [p_a[0]], "w_in")
    r_out, _ = upd(a_w_out, m_a_w_out, v_a_w_out, [buf_a[1]], [p_a[1]], "w_out")
    entries = [
        (gs[0], "full", a_norm, m_a_norm, v_a_norm),
        (gs[1], "cols", a_conv[0], m_a_conv[0], v_a_conv[0]),
        (gs[2], "cols", b_norm, m_b_norm, v_b_norm),
        (gs[3], "flat2", b_b_pw1, m_b_b_pw1, v_b_b_pw1),
        (gs[4], "cols", b_conv[0], m_b_conv[0], v_b_conv[0]),
        (gs[5], "cols", b_b_conv, m_b_b_conv, v_b_b_conv),
        (gs[6], "cols", b_ln_g, m_b_ln_g, v_b_ln_g),
        (gs[7], "cols", b_ln_b, m_b_ln_b, v_b_ln_b),
        (gs[8], "cols", b_b_pw2, m_b_b_pw2, v_b_b_pw2),
        (gs[9], "full", ffn_norm, m_ffn_norm, v_ffn_norm),
        (gs[10], "full", final_norm.reshape(1, D), m_final_norm.reshape(1, D), v_final_norm.reshape(1, D)),
    ]
    so = small_update(gall, chip, entries, "small_update")
    sm = [so[4 * e:4 * e + 4] for e in range(len(entries))]

    def shaped(e, like):
        return [t.reshape(like.shape) for t in sm[e]]

    r_a_norm, r_a_conv, r_b_norm, r_b_b_pw1 = shaped(0, a_norm), shaped(1, a_conv), shaped(2, b_norm), shaped(3, b_b_pw1)
    r_b_conv, r_b_b_conv, r_b_ln_g, r_b_ln_b = shaped(4, b_conv), shaped(5, b_b_conv), shaped(6, b_ln_g), shaped(7, b_ln_b)
    r_b_b_pw2, r_ffn_norm, r_final = shaped(8, b_b_pw2), shaped(9, ffn_norm), shaped(10, final_norm)

    loss = gall[gs[11], 0]
    order =[r_a_norm, r_in, r_a_conv, r_out, r_b_norm, r_pw1, r_b_b_pw1, r_b_conv, r_b_b_conv, r_b_ln_g, r_b_ln_b,
             r_pw2, r_b_b_pw2, r_ffn_norm, r_gate, r_up, r_down, r_final]
    outs = [loss, grad_x.reshape(x.shape)]
    for field in range(4):
        outs += [r[field] for r in order]
    return tuple(outs)
```

```python
import functools

import jax
import jax.numpy as jnp
from jax import lax
from jax.experimental import pallas as pl
from jax.experimental.pallas import tpu as pltpu

RMS_EPS = 1e-6
LN_EPS = 1e-5
ADAM_LR = 0.001
ADAM_B1 = 0.9
ADAM_B2 = 0.999
ADAM_EPS = 1e-08
ADAM_WD = 0.01
ADAM_STEP = 10

N_CHIPS = 4
N_DEV = 8
LANES = 128
SUBLANES = 8
HALO = 32
CONV_ROWS = 64
TOKEN_TILE = 512
WIDE_TOKEN_TILE = 1024
GRAD_TOKEN_TILE = 2048
FFN_ROW_CHUNKS = 2
FFN_FWD_SEGS_PER_STEP = 4
FFN_BWD_TOKEN_TILE = 256
ROW_TILE = 256
VMEM_LIMIT = 56 * 1024 * 1024
MESH = pl.DeviceIdType.MESH
BF16 = jnp.bfloat16
F32 = jnp.float32


def _tile(n, pref, mult=SUBLANES):
    t = min(n, pref) // mult * mult
    while n % t:
        t -= mult
    return t


def _params(sem):
    return pltpu.CompilerParams(dimension_semantics=sem, vmem_limit_bytes=VMEM_LIMIT)


def _sigmoid(x):
    return 0.5 * jnp.tanh(0.5 * x) + 0.5


class _Exchange:
    def __init__(self, ins, outs, aliases, n_sems, copies, then=None):
        self.ins, self.outs, self.aliases, self.n_sems, self.copies = list(ins), list(outs), dict(aliases), n_sems, copies
        self.then = then
        self.early = False

    def awaited_first(self):
        self.early = True
        return self

    def start(self, xi, xo, ssem, rsem):
        for cp in self.copies(xi, xo, ssem, rsem)[0]:
            cp.start()

    def finish(self, xi, xo, ssem, rsem):
        sends, recvs = self.copies(xi, xo, ssem, rsem)
        for cp in recvs:
            cp.wait_recv()
        if self.then is not None:
            sends2, recvs2 = self.then(xi, xo, ssem, rsem)
            for cp in sends2:
                cp.start()
            for cp in recvs2:
                cp.wait_recv()
            sends = sends + sends2
        for cp in sends:
            cp.wait_send()


def _call(body, name, grid, in_specs, out_specs, out_shape, args, sem, scratch_shapes=(), hosted=(), prefetch=(),
          own_aliases=None):
    in_specs, out_specs, out_shape = list(in_specs), list(out_specs), list(out_shape)
    scratch_shapes, hosted, prefetch = list(scratch_shapes), list(hosted), list(prefetch)
    n_pre, n_in, n_out, n_scr = len(prefetch), len(args), len(out_shape), len(scratch_shapes)
    x_in = [a for ex in hosted for a in ex.ins]
    x_out = [o for ex in hosted for o in ex.outs]
    aliases = {n_pre + i: o for i, o in (own_aliases or {}).items()}
    at_in, at_out = n_pre + n_in, n_out
    for ex in hosted:
        for i, o in ex.aliases.items():
            aliases[at_in + i] = at_out + o
        at_in += len(ex.ins)
        at_out += len(ex.outs)
    sems = [pltpu.SemaphoreType.DMA((ex.n_sems,)) for ex in hosted for _ in range(2)]

    def wrapped(*refs):
        pre, refs = refs[:n_pre], refs[n_pre:]
        ins, xi = refs[:n_in], refs[n_in:n_in + len(x_in)]
        refs = refs[n_in + len(x_in):]
        outs, xo = refs[:n_out], refs[n_out:n_out + len(x_out)]
        refs = refs[n_out + len(x_out):]
        scr, sm = refs[:n_scr], refs[n_scr:]
        views, a, b = [], 0, 0
        for e, ex in enumerate(hosted):
            views.append((xi[a:a + len(ex.ins)], xo[b:b + len(ex.outs)], sm[2 * e], sm[2 * e + 1]))
            a += len(ex.ins)
            b += len(ex.outs)
        first = last = None
        for ax, g in enumerate(grid):
            f, l = pl.program_id(ax) == 0, pl.program_id(ax) == g - 1
            first, last = (f, l) if first is None else (first & f, last & l)

        def begin():
            for ex, v in zip(hosted, views):
                ex.start(*v)
            for ex, v in zip(hosted, views):
                if ex.early:
                    ex.finish(*v)

        def end():
            for ex, v in zip(hosted, views):
                if not ex.early:
                    ex.finish(*v)

        if hosted and grid:
            pl.when(first)(begin)
        elif hosted:
            begin()
        early_refs = [r for ex, v in zip(hosted, views) if ex.early for r in v[1]]
        body(*pre, *ins, *outs, *scr, *early_refs)
        if hosted and grid:
            pl.when(last)(end)
        elif hosted:
            end()

    hbm = pl.BlockSpec(memory_space=pl.ANY)
    all_in, all_out = in_specs + [hbm] * len(x_in), out_specs + [hbm] * len(x_out)
    kw = dict(name=name, out_shape=out_shape + x_out, input_output_aliases=aliases,
              compiler_params=_params(tuple("arbitrary" for _ in grid) if hosted else sem))
    if prefetch:
        kw["grid_spec"] = pltpu.PrefetchScalarGridSpec(num_scalar_prefetch=n_pre, grid=grid, in_specs=all_in,
                                                       out_specs=all_out, scratch_shapes=scratch_shapes + sems)
    else:
        kw.update(grid=grid, in_specs=all_in, out_specs=all_out, scratch_shapes=scratch_shapes + sems)
    res = pl.pallas_call(wrapped, **kw)(*prefetch, *args, *x_in)
    return list(res[:n_out]), list(res[n_out:])


def rms_fwd(h, gain, name, hosted=()):
    T, D = h.shape
    tm = _tile(T, TOKEN_TILE)

    def body(h_ref, g_ref, o_ref):
        x = h_ref[...]
        r = lax.rsqrt(jnp.mean(x * x, axis=-1, keepdims=True) + RMS_EPS)
        o_ref[...] = (x * r * g_ref[...]).astype(o_ref.dtype)

    (n,), xo = _call(
        body, name, (T // tm,),
        [pl.BlockSpec((tm, D), lambda i: (i, 0)), pl.BlockSpec((1, D), lambda i: (0, 0))],
        [pl.BlockSpec((tm, D), lambda i: (i, 0))], [jax.ShapeDtypeStruct((T, D), BF16)],
        [h, gain], ("parallel",), hosted=hosted)
    return n, xo


def loss_head(h, gain, tgt, name):
    T, D = h.shape
    tm = _tile(T, TOKEN_TILE)

    def body(h_ref, g_ref, t_ref, loss_ref, dh_ref, dhb_ref, dg_ref):
        i = pl.program_id(0)
        x = h_ref[...]
        g = g_ref[...]
        r = lax.rsqrt(jnp.mean(x * x, axis=-1, keepdims=True) + RMS_EPS)
        xhat = x * r
        diff = xhat * g - t_ref[...]
        part_loss = 0.5 * jnp.sum(jnp.mean(diff * diff, axis=-1, keepdims=True), axis=0, keepdims=True)
        dy = diff * (1.0 / D)
        dxhat = dy * g
        dh = r * (dxhat - xhat * jnp.mean(dxhat * xhat, axis=-1, keepdims=True))
        dh_ref[...] = dh
        dhb_ref[...] = dh.astype(dhb_ref.dtype)
        part = jnp.sum(dy * xhat, axis=0, keepdims=True)

        @pl.when(i == 0)
        def _():
            dg_ref[...] = part
            loss_ref[...] = part_loss

        @pl.when(i > 0)
        def _():
            dg_ref[...] += part
            loss_ref[...] += part_loss

    row = pl.BlockSpec((tm, D), lambda i: (i, 0))
    vec = pl.BlockSpec((1, D), lambda i: (0, 0))
    return pl.pallas_call(
        body, name=name, grid=(T // tm,),
        in_specs=[row, vec, row],
        out_specs=[pl.BlockSpec((1, 1), lambda i: (0, 0)), row, row, vec],
        out_shape=[jax.ShapeDtypeStruct((1, 1), F32), jax.ShapeDtypeStruct((T, D), F32),
                   jax.ShapeDtypeStruct((T, D), BF16), jax.ShapeDtypeStruct((1, D), F32)],
        compiler_params=_params(("arbitrary",)),
    )(h, gain, tgt)


def _prev_halo_spec(tm, width):
    return pl.BlockSpec((HALO, width), lambda i: (jnp.maximum(i * (tm // HALO) - 1, 0), 0))


def _next_halo_spec(tm, width, T):
    return pl.BlockSpec((HALO, width), lambda i: (jnp.minimum((i + 1) * (tm // HALO), T // HALO - 1), 0))


def _shifted(win, off, rows):
    if off % SUBLANES == 0:
        return win[off:off + rows]
    n = win.shape[0]
    return pltpu.roll(win, (n - off) % n, 0)[:rows]


def _rowsum8(x):
    acc = x[0:SUBLANES]
    for q in range(1, x.shape[0] // SUBLANES):
        acc = acc + x[q * SUBLANES:(q + 1) * SUBLANES]
    return acc


def _conv_loops(tm, D, per_block):
    def chunk(r, carry):
        t0 = pl.multiple_of(r * CONV_ROWS, CONV_ROWS)
        for lb in range(D // LANES):
            per_block(t0, slice(lb * LANES, (lb + 1) * LANES))
        return carry

    lax.fori_loop(0, tm // CONV_ROWS, chunk, 0)


def gateconv_fwd(bcv, w, w_out, res, name, hosted=()):
    T, D3 = bcv.shape
    D = D3 // 3
    K = w.shape[0]
    tm = _tile(T, TOKEN_TILE)
    wo_shape = w_out.outs[0].shape

    def body(x_ref, halo_ref, w_ref, res_ref, y_ref, h_ref, pad_ref, wo_v, sem, wo_hbm):
        i = pl.program_id(0)

        @pl.when(i == 0)
        def _():
            cp = pltpu.make_async_copy(wo_hbm, wo_v, sem)
            cp.start()
            cp.wait()

        pad_ref[HALO:, :] = x_ref[:, D:2 * D] * x_ref[:, 2 * D:]
        pad_ref[:HALO, :] = jnp.where(i > 0, halo_ref[:, D:2 * D] * halo_ref[:, 2 * D:], 0.0)

        def block(t0, ls):
            win = pad_ref[pl.ds(t0, CONV_ROWS + HALO), ls]
            acc = jnp.zeros((CONV_ROWS, LANES), F32)
            for k in range(K):
                acc = acc + w_ref[k:k + 1, ls] * _shifted(win, HALO - (K - 1) + k, CONV_ROWS)
            y_ref[pl.ds(t0, CONV_ROWS), ls] = (x_ref[pl.ds(t0, CONV_ROWS), ls] * acc).astype(y_ref.dtype)

        _conv_loops(tm, D, block)
        h_ref[...] = res_ref[...] + jnp.dot(y_ref[...], wo_v[...].reshape(D, D), preferred_element_type=F32)

    row = pl.BlockSpec((tm, D), lambda i: (i, 0))
    (y, h), xo = _call(
        body, name, (T // tm,),
        [pl.BlockSpec((tm, D3), lambda i: (i, 0)), _prev_halo_spec(tm, D3), pl.BlockSpec((K, D), lambda i: (0, 0)), row],
        [row, row], [jax.ShapeDtypeStruct((T, D), BF16), jax.ShapeDtypeStruct((T, D), F32)],
        [bcv, bcv, w, res], ("arbitrary",),
        [pltpu.VMEM((tm + HALO, D), F32), pltpu.VMEM(wo_shape, BF16), pltpu.SemaphoreType.DMA],
        hosted=[w_out.awaited_first()] + list(hosted))
    return y, h, xo


def gateconv_bwd(dy, bcv, w, name, hosted=()):
    T, D3 = bcv.shape
    D = D3 // 3
    K = w.shape[0]
    tm = _tile(T, TOKEN_TILE)
    nt = T // tm

    def body(dy_ref, dyn_ref, x_ref, xp_ref, xn_ref, w_ref, o_ref, dw_ref, cv_ref, dc_ref, wacc_ref):
        i = pl.program_id(0)
        cv_ref[HALO:, :] = x_ref[:, D:2 * D] * x_ref[:, 2 * D:]
        cv_ref[:HALO, :] = jnp.where(i > 0, xp_ref[:, D:2 * D] * xp_ref[:, 2 * D:], 0.0)
        dc_ref[:tm, :] = dy_ref[...] * x_ref[:, :D]
        dc_ref[tm:, :] = jnp.where(i < nt - 1, dyn_ref[...] * xn_ref[:, :D], 0.0)

        @pl.when(i == 0)
        def _():
            wacc_ref[...] = jnp.zeros_like(wacc_ref)

        def block(t0, ls):
            cwin = cv_ref[pl.ds(t0, CONV_ROWS + HALO), ls]
            dwin = dc_ref[pl.ds(t0, CONV_ROWS + HALO), ls]
            dcon = dwin[:CONV_ROWS]
            conv = jnp.zeros((CONV_ROWS, LANES), F32)
            dcv = jnp.zeros((CONV_ROWS, LANES), F32)
            for k in range(K):
                wk = w_ref[k:k + 1, ls]
                cs = _shifted(cwin, HALO - (K - 1) + k, CONV_ROWS)
                conv = conv + wk * cs
                dcv = dcv + wk * _shifted(dwin, (K - 1) - k, CONV_ROWS)
                wacc_ref[k * SUBLANES:(k + 1) * SUBLANES, ls] += _rowsum8(dcon * cs)
            rows = pl.ds(t0, CONV_ROWS)
            o_ref[rows, ls] = (dy_ref[rows, ls] * conv).astype(o_ref.dtype)
            o_ref[rows, pl.ds(D + ls.start, LANES)] = (dcv * x_ref[rows, pl.ds(2 * D + ls.start, LANES)]).astype(o_ref.dtype)
            o_ref[rows, pl.ds(2 * D + ls.start, LANES)] = (dcv * x_ref[rows, pl.ds(D + ls.start, LANES)]).astype(o_ref.dtype)

        _conv_loops(tm, D, block)

        @pl.when(i == nt - 1)
        def _():
            for k in range(K):
                dw_ref[k:k + 1, :] = jnp.sum(wacc_ref[k * SUBLANES:(k + 1) * SUBLANES, :], axis=0, keepdims=True)

    (dx, dw), xo = _call(
        body, name, (nt,),
        [pl.BlockSpec((tm, D), lambda i: (i, 0)), _next_halo_spec(tm, D, T),
         pl.BlockSpec((tm, D3), lambda i: (i, 0)), _prev_halo_spec(tm, D3), _next_halo_spec(tm, D3, T),
         pl.BlockSpec((K, D), lambda i: (0, 0))],
        [pl.BlockSpec((tm, D3), lambda i: (i, 0)), pl.BlockSpec((K, D), lambda i: (0, 0))],
        [jax.ShapeDtypeStruct((T, D3), BF16), jax.ShapeDtypeStruct((K, D), F32)],
        [dy, dy, bcv, bcv, bcv, w], ("arbitrary",),
        [pltpu.VMEM((tm + HALO, D), F32), pltpu.VMEM((tm + HALO, D), F32), pltpu.VMEM((K * SUBLANES, D), F32)],
        hosted=hosted)
    return dx, dw, xo


def bconv_fwd(u, w, b_conv, ln_g, ln_b, w_out, b_out, res, name, hosted=()):
    T, D2 = u.shape
    D = D2 // 2
    K = w.shape[0]
    tm = _tile(T, TOKEN_TILE)

    def body(u_ref, halo_ref, w_ref, bc_ref, g_ref, b_ref, wo_ref, bo_ref, res_ref, cu_ref, s_ref, h_ref, pad_ref):
        i = pl.program_id(0)
        pad_ref[HALO:, :] = u_ref[:, :D] * _sigmoid(u_ref[:, D:])
        pad_ref[:HALO, :] = jnp.where(i > 0, halo_ref[:, :D] * _sigmoid(halo_ref[:, D:]), 0.0)

        def block(t0, ls):
            win = pad_ref[pl.ds(t0, CONV_ROWS + HALO), ls]
            acc = jnp.zeros((CONV_ROWS, LANES), F32)
            for k in range(K):
                acc = acc + w_ref[k:k + 1, ls] * _shifted(win, HALO - (K - 1) + k, CONV_ROWS)
            cu_ref[pl.ds(t0, CONV_ROWS), ls] = acc + bc_ref[:, ls]

        _conv_loops(tm, D, block)
        cu = cu_ref[...]
        mu = jnp.mean(cu, axis=-1, keepdims=True)
        xc = cu - mu
        rstd = lax.rsqrt(jnp.mean(xc * xc, axis=-1, keepdims=True) + LN_EPS)
        ln = xc * rstd * g_ref[...] + b_ref[...]
        s = (ln * _sigmoid(ln)).astype(s_ref.dtype)
        s_ref[...] = s
        h_ref[...] = res_ref[...] + bo_ref[...] + jnp.dot(s, wo_ref[0], preferred_element_type=F32)

    vec = pl.BlockSpec((1, D), lambda i: (0, 0))
    row = pl.BlockSpec((tm, D), lambda i: (i, 0))
    (cu, s, h), xo = _call(
        body, name, (T // tm,),
        [pl.BlockSpec((tm, D2), lambda i: (i, 0)), _prev_halo_spec(tm, D2), pl.BlockSpec((K, D), lambda i: (0, 0)), vec, vec, vec,
         pl.BlockSpec((1, D, D), lambda i: (0, 0, 0)), vec, row],
        [row, row, row], [jax.ShapeDtypeStruct((T, D), F32), jax.ShapeDtypeStruct((T, D), BF16), jax.ShapeDtypeStruct((T, D), F32)],
        [u, u, w, b_conv, ln_g, ln_b, w_out, b_out, res], ("parallel",), [pltpu.VMEM((tm + HALO, D), F32)], hosted=hosted)
    return cu, s, h, xo


def pw2_ln_bwd(dy, w, cu, ln_g, ln_b, name, hosted=()):
    T, D = cu.shape
    tm = _tile(T, TOKEN_TILE)

    def body(dy_ref, w_ref, cu_ref, g_ref, b_ref, dcu_ref, dg_ref, db_ref, dbc_ref, dbo_ref):
        i = pl.program_id(0)
        dy_ = dy_ref[...]
        ds = lax.dot_general(dy_.astype(BF16), w_ref[0], _NT, preferred_element_type=F32)
        cu_ = cu_ref[...]
        mu = jnp.mean(cu_, axis=-1, keepdims=True)
        xc = cu_ - mu
        rstd = lax.rsqrt(jnp.mean(xc * xc, axis=-1, keepdims=True) + LN_EPS)
        xh = xc * rstd
        ln = xh * g_ref[...] + b_ref[...]
        sg = _sigmoid(ln)
        dl = ds * (sg * (1.0 + ln * (1.0 - sg)))
        dxh = dl * g_ref[...]
        dcu = rstd * (dxh - jnp.mean(dxh, axis=-1, keepdims=True) - xh * jnp.mean(dxh * xh, axis=-1, keepdims=True))
        dcu_ref[...] = dcu
        pg = jnp.sum(dl * xh, axis=0, keepdims=True)
        pb = jnp.sum(dl, axis=0, keepdims=True)
        pc = jnp.sum(dcu, axis=0, keepdims=True)
        po = jnp.sum(dy_, axis=0, keepdims=True)

        @pl.when(i == 0)
        def _():
            dg_ref[...] = pg
            db_ref[...] = pb
            dbc_ref[...] = pc
            dbo_ref[...] = po

        @pl.when(i > 0)
        def _():
            dg_ref[...] += pg
            db_ref[...] += pb
            dbc_ref[...] += pc
            dbo_ref[...] += po

    vec = pl.BlockSpec((1, D), lambda i: (0, 0))
    row = pl.BlockSpec((tm, D), lambda i: (i, 0))
    vshape = jax.ShapeDtypeStruct((1, D), F32)
    outs, xo = _call(
        body, name, (T // tm,), [row, pl.BlockSpec((1, D, D), lambda i: (0, 0, 0)), row, vec, vec], [row, vec, vec, vec, vec],
        [jax.ShapeDtypeStruct((T, D), F32), vshape, vshape, vshape, vshape], [dy, w, cu, ln_g, ln_b], ("arbitrary",),
        hosted=hosted)
    return (*outs, xo)


def bconv_bwd(dcu, u, w, name, hosted=()):
    T, D2 = u.shape
    D = D2 // 2
    K = w.shape[0]
    tm = _tile(T, TOKEN_TILE)
    nt = T // tm

    def body(dc_ref, dcn_ref, u_ref, up_ref, w_ref, du_ref, dw_ref, db_ref, glu_ref, dpad_ref, dglu_ref, wacc_ref):
        i = pl.program_id(0)
        glu_ref[HALO:, :] = u_ref[:, :D] * _sigmoid(u_ref[:, D:])
        glu_ref[:HALO, :] = jnp.where(i > 0, up_ref[:, :D] * _sigmoid(up_ref[:, D:]), 0.0)
        dpad_ref[:tm, :] = dc_ref[...]
        dpad_ref[tm:, :] = jnp.where(i < nt - 1, dcn_ref[...], 0.0)

        @pl.when(i == 0)
        def _():
            wacc_ref[...] = jnp.zeros_like(wacc_ref)

        def block(t0, ls):
            gwin = glu_ref[pl.ds(t0, CONV_ROWS + HALO), ls]
            dwin = dpad_ref[pl.ds(t0, CONV_ROWS + HALO), ls]
            dcur = dwin[:CONV_ROWS]
            dglu = jnp.zeros((CONV_ROWS, LANES), F32)
            for k in range(K):
                dglu = dglu + w_ref[k:k + 1, ls] * _shifted(dwin, (K - 1) - k, CONV_ROWS)
                gs = _shifted(gwin, HALO - (K - 1) + k, CONV_ROWS)
                wacc_ref[k * SUBLANES:(k + 1) * SUBLANES, ls] += _rowsum8(dcur * gs)
            dglu_ref[pl.ds(t0, CONV_ROWS), ls] = dglu

        _conv_loops(tm, D, block)
        dglu = dglu_ref[...]
        a = u_ref[:, :D]
        sg = _sigmoid(u_ref[:, D:])
        da = dglu * sg
        dg = dglu * a * (sg * (1.0 - sg))
        du_ref[:, :D] = da.astype(du_ref.dtype)
        du_ref[:, D:] = dg.astype(du_ref.dtype)
        pa = jnp.sum(da, axis=0, keepdims=True)
        pg = jnp.sum(dg, axis=0, keepdims=True)

        @pl.when(i == 0)
        def _():
            db_ref[:, :D] = pa
            db_ref[:, D:] = pg

        @pl.when(i > 0)
        def _():
            db_ref[:, :D] += pa
            db_ref[:, D:] += pg

        @pl.when(i == nt - 1)
        def _():
            for k in range(K):
                dw_ref[k:k + 1, :] = jnp.sum(wacc_ref[k * SUBLANES:(k + 1) * SUBLANES, :], axis=0, keepdims=True)

    (du, dw, db), xo = _call(
        body, name, (nt,),
        [pl.BlockSpec((tm, D), lambda i: (i, 0)), _next_halo_spec(tm, D, T),
         pl.BlockSpec((tm, D2), lambda i: (i, 0)), _prev_halo_spec(tm, D2), pl.BlockSpec((K, D), lambda i: (0, 0))],
        [pl.BlockSpec((tm, D2), lambda i: (i, 0)), pl.BlockSpec((K, D), lambda i: (0, 0)), pl.BlockSpec((1, D2), lambda i: (0, 0))],
        [jax.ShapeDtypeStruct((T, D2), BF16), jax.ShapeDtypeStruct((K, D), F32), jax.ShapeDtypeStruct((1, D2), F32)],
        [dcu, dcu, u, u, w], ("arbitrary",),
        [pltpu.VMEM((tm + HALO, D), F32), pltpu.VMEM((tm + HALO, D), F32), pltpu.VMEM((tm, D), F32),
         pltpu.VMEM((K * SUBLANES, D), F32)], hosted=hosted)
    return du, dw, db, xo


def mm_cols(a, w, bias, name, hosted=()):
    T, K = a.shape
    S, _, n = w.shape
    tm = _tile(T, WIDE_TOKEN_TILE)

    def body(*refs):
        a_ref, w_ref = refs[:2]
        o_ref = refs[-1]
        acc = jnp.dot(a_ref[...], w_ref[...], preferred_element_type=F32)
        if bias is not None:
            acc = acc + refs[2][...]
        o_ref[...] = acc

    in_specs = [pl.BlockSpec((tm, K), lambda s, i: (i, 0)), pl.BlockSpec((None, K, n), lambda s, i: (s, 0, 0))]
    args = [a, w]
    if bias is not None:
        in_specs.append(pl.BlockSpec((1, n), lambda s, i: (0, s)))
        args.append(bias)
    (out,), xo = _call(body, name, (S, T // tm), in_specs, [pl.BlockSpec((tm, n), lambda s, i: (i, s))],
                       [jax.ShapeDtypeStruct((T, S * n), F32)], args, ("parallel", "parallel"), hosted=hosted)
    return out, xo


def _load_weights(pairs, sems, S, G, i, p):
    def copies(seg):
        return [pltpu.make_async_copy(src.at[seg], dst.at[seg], sems.at[k, seg]) for k, (src, dst) in enumerate(pairs)]

    @pl.when((i == 0) & (p == 0))
    def _():
        for seg in range(S):
            for cp in copies(seg):
                cp.start()

    @pl.when((i == 0) & (p < S // G))
    def _():
        for j in range(G):
            for cp in copies(G * p + j):
                cp.wait()


def ffn_fwd(h, gain, weights, name, hosted=(), arriving=None):
    T, D = h.shape
    S, f, _ = weights[0].shape
    tm = _tile(T, TOKEN_TILE)
    rc = tm // FFN_ROW_CHUNKS
    chunks = [slice(r * rc, (r + 1) * rc) for r in range(FFN_ROW_CHUNKS)]
    G = FFN_FWD_SEGS_PER_STEP
    weights = list(weights)
    hosted = ([arriving.awaited_first()] if arriving is not None else []) + list(hosted)

    def body(h_ref, gain_ref, *refs):
        nw = len(weights)
        wg_hbm, wu_hbm, wd_hbm = list(refs[:nw]) + list(refs[nw + 9:])
        n_ref, g_ref, u_ref, gu_ref, o_ref, wg_v, wu_v, wd_v, sems = refs[nw:nw + 9]
        i, p = pl.program_id(0), pl.program_id(1)
        _load_weights([(wg_hbm, wg_v), (wu_hbm, wu_v), (wd_hbm, wd_v)], sems, S, G, i, p)

        @pl.when(p == 0)
        def _():
            x = h_ref[...]
            r = lax.rsqrt(jnp.mean(x * x, axis=-1, keepdims=True) + RMS_EPS)
            n_ref[...] = (x * r * gain_ref[...]).astype(n_ref.dtype)

        parts = []
        for rows in chunks:
            a = n_ref[rows, :]
            acc = None
            for j in range(G):
                seg = G * p + j
                g = lax.dot_general(a, wg_v[seg], _NT, preferred_element_type=F32)
                u = lax.dot_general(a, wu_v[seg], _NT, preferred_element_type=F32)
                gu = (g * _sigmoid(g) * u).astype(gu_ref.dtype)
                g_ref[j, rows, :] = g.astype(g_ref.dtype)
                u_ref[j, rows, :] = u.astype(u_ref.dtype)
                gu_ref[j, rows, :] = gu
                part = jnp.dot(gu, wd_v[seg], preferred_element_type=F32)
                acc = part if acc is None else acc + part
            parts.append(acc)

        @pl.when(p == 0)
        def _():
            for rows, part in zip(chunks, parts):
                o_ref[rows, :] = h_ref[rows, :] + part

        @pl.when(p > 0)
        def _():
            for rows, part in zip(chunks, parts):
                o_ref[rows, :] += part

    row = pl.BlockSpec((tm, D), lambda i, p: (i, 0))
    seg = pl.BlockSpec((G, tm, f), lambda i, p: (p, i, 0))
    hbm = pl.BlockSpec(memory_space=pl.ANY)
    segs = jax.ShapeDtypeStruct((S, T, f), BF16)
    outs, xo = _call(
        body, name, (T // tm, S // G),
        [row, pl.BlockSpec((1, D), lambda i, s: (0, 0))] + [hbm] * len(weights), [row, seg, seg, seg, row],
        [jax.ShapeDtypeStruct((T, D), BF16), segs, segs, segs, jax.ShapeDtypeStruct((T, D), F32)],
        [h, gain] + weights, ("arbitrary", "arbitrary"),
        [pltpu.VMEM((S, f, D), BF16), pltpu.VMEM((S, f, D), BF16), pltpu.VMEM((S, f, D), BF16), pltpu.SemaphoreType.DMA((3, S))],
        hosted=hosted)
    return (*outs, xo)


def ffn_bwd(dy, h, gain, g, u, wd, wg, wu, name, hosted=()):
    T, D = h.shape
    S, f, _ = wg.shape
    tm = _tile(T, FFN_BWD_TOKEN_TILE)
    nt = T // tm

    def body(dy_ref, h_ref, gain_ref, g_ref, u_ref, wd_hbm, wg_hbm, wu_hbm, dg_ref, du_ref, dh_ref, dhb_ref, dgain_ref,
             wd_v, wg_v, wu_v, sems):
        i = pl.program_id(0)
        _load_weights([(wd_hbm, wd_v), (wg_hbm, wg_v), (wu_hbm, wu_v)], sems, S, S, i, 0)
        dy_ = dy_ref[...]
        dyb = dy_.astype(BF16)
        dn = None
        for j in range(S):
            dgu = lax.dot_general(dyb, wd_v[j], _NT, preferred_element_type=F32)
            gv = g_ref[j].astype(F32)
            sg = _sigmoid(gv)
            dg = (dgu * u_ref[j].astype(F32) * (sg * (1.0 + gv * (1.0 - sg)))).astype(dg_ref.dtype)
            du = (dgu * (gv * sg)).astype(du_ref.dtype)
            dg_ref[j] = dg
            du_ref[j] = du
            part = jnp.dot(dg, wg_v[j], preferred_element_type=F32) + jnp.dot(du, wu_v[j], preferred_element_type=F32)
            dn = part if dn is None else dn + part
        x = h_ref[...]
        r = lax.rsqrt(jnp.mean(x * x, axis=-1, keepdims=True) + RMS_EPS)
        xhat = x * r
        dxhat = dn * gain_ref[...]
        dh = dy_ + r * (dxhat - xhat * jnp.mean(dxhat * xhat, axis=-1, keepdims=True))
        dh_ref[...] = dh
        dhb_ref[...] = dh.astype(dhb_ref.dtype)
        pg = jnp.sum(dn * xhat, axis=0, keepdims=True)

        @pl.when(i == 0)
        def _():
            dgain_ref[...] = pg

        @pl.when(i > 0)
        def _():
            dgain_ref[...] += pg

    row = pl.BlockSpec((tm, D), lambda i: (i, 0))
    vec = pl.BlockSpec((1, D), lambda i: (0, 0))
    seg = pl.BlockSpec((S, tm, f), lambda i: (0, i, 0))
    hbm = pl.BlockSpec(memory_space=pl.ANY)
    segs = jax.ShapeDtypeStruct((S, T, f), BF16)
    outs, xo = _call(
        body, name, (nt,),
        [row, row, vec, seg, seg, hbm, hbm, hbm], [seg, seg, row, row, vec],
        [segs, segs, jax.ShapeDtypeStruct((T, D), F32), jax.ShapeDtypeStruct((T, D), BF16), jax.ShapeDtypeStruct((1, D), F32)],
        [dy, h, gain, g, u, wd, wg, wu], ("arbitrary",),
        [pltpu.VMEM((S, f, D), BF16), pltpu.VMEM((S, f, D), BF16), pltpu.VMEM((S, f, D), BF16),
         pltpu.SemaphoreType.DMA((3, S))], hosted=hosted)
    return (*outs, xo)


_NT = (((1,), (1,)), ((), ()))
_TN = (((0,), (0,)), ((), ()))


def nt_rows(dy, w, name, hosted=()):
    T, N = dy.shape
    S, k, _ = w.shape
    tm = _tile(T, TOKEN_TILE)

    def body(dy_ref, w_ref, o_ref):
        o_ref[...] = lax.dot_general(dy_ref[...].astype(BF16), w_ref[...], _NT, preferred_element_type=F32)

    (out,), xo = _call(
        body, name, (T // tm, S),
        [pl.BlockSpec((tm, N), lambda i, s: (i, 0)), pl.BlockSpec((None, k, N), lambda i, s: (s, 0, 0))],
        [pl.BlockSpec((None, tm, k), lambda i, s: (s, i, 0))], [jax.ShapeDtypeStruct((S, T, k), F32)],
        [dy, w], ("parallel", "parallel"), hosted=hosted)
    return out, xo


def nt_cols_rms(dy, w, h, gain, dres, name, hosted=(), also_bf16=False):
    T, K = h.shape
    S, _, n = w.shape
    tm = _tile(T, TOKEN_TILE)

    def body(dy_ref, w_ref, h_ref, gain_ref, dres_ref, dh_ref, dgain_ref, *rest):
        i = pl.program_id(0)
        dn = None
        for s in range(S):
            part = lax.dot_general(dy_ref[:, s * n:(s + 1) * n], w_ref[s], _NT, preferred_element_type=F32)
            dn = part if dn is None else dn + part
        x = h_ref[...]
        r = lax.rsqrt(jnp.mean(x * x, axis=-1, keepdims=True) + RMS_EPS)
        xhat = x * r
        dxhat = dn * gain_ref[...]
        dh = dres_ref[...] + r * (dxhat - xhat * jnp.mean(dxhat * xhat, axis=-1, keepdims=True))
        dh_ref[...] = dh
        if also_bf16:
            rest[0][...] = dh.astype(BF16)
        pg = jnp.sum(dn * xhat, axis=0, keepdims=True)

        @pl.when(i == 0)
        def _():
            dgain_ref[...] = pg

        @pl.when(i > 0)
        def _():
            dgain_ref[...] += pg

    row = pl.BlockSpec((tm, K), lambda i: (i, 0))
    vec = pl.BlockSpec((1, K), lambda i: (0, 0))
    out_specs, out_shape = [row, vec], [jax.ShapeDtypeStruct((T, K), F32), jax.ShapeDtypeStruct((1, K), F32)]
    if also_bf16:
        out_specs, out_shape = out_specs + [row], out_shape + [jax.ShapeDtypeStruct((T, K), BF16)]
    outs, xo = _call(
        body, name, (T // tm,),
        [pl.BlockSpec((tm, S * n), lambda i: (i, 0)), pl.BlockSpec((S, K, n), lambda i: (0, 0, 0)), row, vec, row],
        out_specs, out_shape, [dy, w, h, gain, dres], ("arbitrary",), hosted=hosted)
    return (*outs, xo)


def tn_grad(a, dy, S, a_by_seg, name, hosted=()):
    T = dy.shape[0] if dy.ndim == 2 else dy.shape[1]
    tt = _tile(T, GRAD_TOKEN_TILE)
    if a_by_seg:
        R = a.shape[1] // S if a.ndim == 2 else a.shape[2]
        C = dy.shape[1]
        a_spec = pl.BlockSpec((tt, R), lambda s, t: (t, s)) if a.ndim == 2 else pl.BlockSpec((None, tt, R), lambda s, t: (s, t, 0))
        b_spec = pl.BlockSpec((tt, C), lambda s, t: (t, 0))
    else:
        R = a.shape[1]
        C = dy.shape[1] // S if dy.ndim == 2 else dy.shape[2]
        a_spec = pl.BlockSpec((tt, R), lambda s, t: (t, 0))
        b_spec = pl.BlockSpec((tt, C), lambda s, t: (t, s)) if dy.ndim == 2 else pl.BlockSpec((None, tt, C), lambda s, t: (s, t, 0))
    Rh = R // 2
    nt = T // tt

    def body(a_ref, b_ref, o_ref, acc_ref):
        t = pl.program_id(1)
        part = lax.dot_general(a_ref[...], b_ref[...].astype(BF16), _TN, preferred_element_type=F32)

        @pl.when(t == 0)
        def _():
            acc_ref[...] = part

        @pl.when(t > 0)
        def _():
            acc_ref[...] += part

        @pl.when(t == nt - 1)
        def _():
            o_ref[0] = acc_ref[:Rh, :].astype(o_ref.dtype)
            o_ref[1] = acc_ref[Rh:, :].astype(o_ref.dtype)

    (gh,), xo = _call(
        body, name, (S, nt), [a_spec, b_spec], [pl.BlockSpec((2, None, Rh, C), lambda s, t: (0, s, 0, 0))],
        [jax.ShapeDtypeStruct((2, S, Rh, C), BF16)], [a, dy], ("parallel", "arbitrary"), [pltpu.VMEM((R, C), F32)],
        hosted=hosted)
    return gh, xo


def tn_grad_square(a, dy, S, name, hosted=()):
    T, K = a.shape
    N = dy.shape[1]
    tt = _tile(T, GRAD_TOKEN_TILE)
    nt = T // tt
    Rh = K // S // 2

    def body(a_ref, b_ref, o_ref, acc_ref):
        t = pl.program_id(0)
        part = lax.dot_general(a_ref[...], b_ref[...].astype(BF16), _TN, preferred_element_type=F32)

        @pl.when(t == 0)
        def _():
            acc_ref[...] = part

        @pl.when(t > 0)
        def _():
            acc_ref[...] += part

        @pl.when(t == nt - 1)
        def _():
            for s in range(S):
                for hf in range(2):
                    r0 = (2 * s + hf) * Rh
                    o_ref[hf, s] = acc_ref[r0:r0 + Rh, :].astype(o_ref.dtype)

    (gh,), xo = _call(
        body, name, (nt,), [pl.BlockSpec((tt, K), lambda t: (t, 0)), pl.BlockSpec((tt, N), lambda t: (t, 0))],
        [pl.BlockSpec((2, S, Rh, N), lambda t: (0, 0, 0, 0))], [jax.ShapeDtypeStruct((2, S, Rh, N), BF16)],
        [a, dy], ("arbitrary",), [pltpu.VMEM((K, N), F32)], hosted=hosted)
    return gh, xo


def _place():
    x, y, c = lax.axis_index("x"), lax.axis_index("y"), lax.axis_index("c")
    chips = [(1 - x, y), (x, 1 - y), (1 - x, 1 - y)]
    return x, y, c, chips


def _remote(src, dst, send_sem, recv_sem, dev):
    return pltpu.make_async_remote_copy(src_ref=src, dst_ref=dst, send_sem=send_sem, recv_sem=recv_sem,
                                        device_id=dev, device_id_type=MESH)


def small_allreduce(v, name, hosted=()):
    rows, W = v.shape

    def body(v_ref, o_ref, sib_ref, pair_ref, chips_ref, send_sems, recv_sems):
        x, y, c, chips = _place()
        me = 2 * x + y
        swap = _remote(v_ref, sib_ref, send_sems.at[3], recv_sems.at[3], (x, y, 1 - c))
        swap.start()
        swap.wait()
        mine, other = v_ref[...], sib_ref[...]
        pair_ref[...] = jnp.where(c == 0, mine, other) + jnp.where(c == 0, other, mine)
        sends = []
        for j, (px, py) in enumerate(chips):
            cp = _remote(pair_ref, chips_ref.at[me], send_sems.at[j], recv_sems.at[j], (px, py, c))
            cp.start()
            sends.append(cp)
        chips_ref[me] = pair_ref[...]
        for j, (px, py) in enumerate(chips):
            blk = chips_ref.at[2 * px + py]
            _remote(blk, blk, send_sems.at[j], recv_sems.at[j], (px, py, c)).wait_recv()
        for cp in sends:
            cp.wait_send()
        o_ref[...] = (chips_ref[0] + chips_ref[1]) + (chips_ref[2] + chips_ref[3])

    vm = pl.BlockSpec(memory_space=pltpu.VMEM)
    (out,), xo = _call(
        body, name, (), [vm], [vm], [jax.ShapeDtypeStruct((rows, W), F32)], [v], (),
        [pltpu.VMEM((rows, W), F32), pltpu.VMEM((rows, W), F32), pltpu.VMEM((N_CHIPS, rows, W), F32),
         pltpu.SemaphoreType.DMA((4,)), pltpu.SemaphoreType.DMA((4,))], hosted=hosted)
    return out, xo


def _gather_p1_copies(srcs, bufs, ssem, rsem, base):
    x, y, c, chips = _place()
    me, sib = 2 * x + y, (x, y, 1 - c)
    sends, recvs = [], []
    for k, (src, buf) in enumerate(zip(srcs, bufs)):
        rh = src.shape[0] // 2
        s0 = base + 4 * k
        sends.append(_remote(src, buf.at[me], ssem.at[s0 + 3], rsem.at[s0 + 3], sib))
        recvs.append(_remote(buf.at[me], buf.at[me], ssem.at[s0 + 3], rsem.at[s0 + 3], sib))
        for j, (px, py) in enumerate(chips):
            sends.append(_remote(src.at[pl.ds(c * rh, rh)], buf.at[me, pl.ds(c * rh, rh)], ssem.at[s0 + j], rsem.at[s0 + j], (px, py, c)))
            blk = buf.at[2 * px + py, pl.ds(c * rh, rh)]
            recvs.append(_remote(blk, blk, ssem.at[s0 + j], rsem.at[s0 + j], (px, py, c)))
    return sends, recvs


def _gather_p2_copies(bufs, ssem, rsem, base):
    x, y, c, chips = _place()
    sib = (x, y, 1 - c)
    sends, recvs = [], []
    for k, buf in enumerate(bufs):
        rh = buf.shape[1] // 2
        for j, (px, py) in enumerate(chips):
            s0 = base + 3 * k + j
            blk = buf.at[2 * px + py, pl.ds(c * rh, rh)]
            sends.append(_remote(blk, blk, ssem.at[s0], rsem.at[s0], sib))
            got = buf.at[2 * px + py, pl.ds((1 - c) * rh, rh)]
            recvs.append(_remote(got, got, ssem.at[s0], rsem.at[s0], sib))
    return sends, recvs


def _gathered_shape(s):
    return jax.ShapeDtypeStruct((N_CHIPS,) + s.shape, s.dtype)


def gather_p1(shards):
    return _Exchange(shards, [_gathered_shape(s) for s in shards], {}, 4 * len(shards),
                     lambda xi, xo, ss, rs: _gather_p1_copies(xi, xo, ss, rs, 0))


def gather_p2(bufs):
    return _Exchange(bufs, [jax.ShapeDtypeStruct(b.shape, b.dtype) for b in bufs], {k: k for k in range(len(bufs))},
                     3 * len(bufs), lambda xi, xo, ss, rs: _gather_p2_copies(xo, ss, rs, 0))


def gather_whole(whole, begun):
    nw, n = len(whole), len(whole) + len(begun)
    shards = list(whole) + list(begun)
    return _Exchange(shards, [_gathered_shape(s) for s in shards], {}, 4 * n + 3 * nw,
                     lambda xi, xo, ss, rs: _gather_p1_copies(xi, xo, ss, rs, 0),
                     then=lambda xi, xo, ss, rs: _gather_p2_copies(xo[:nw], ss, rs, 4 * n))


def gather_small(v):
    def copies(xi, xo, ssem, rsem):
        x, y, c, chips = _place()
        me, sib = 2 * x + y, (x, y, 1 - c)
        sends = [_remote(xi[0], xo[0].at[me], ssem.at[3], rsem.at[3], sib)]
        recvs = [_remote(xo[0].at[me], xo[0].at[me], ssem.at[3], rsem.at[3], sib)]
        for j, (px, py) in enumerate(chips):
            sends.append(_remote(xi[0], xo[0].at[me], ssem.at[j], rsem.at[j], (px, py, c)))
            blk = xo[0].at[2 * px + py]
            recvs.append(_remote(blk, blk, ssem.at[j], rsem.at[j], (px, py, c)))
        return sends, recvs

    return _Exchange([v], [_gathered_shape(v)], {}, 4, copies)


def gather_all(v):
    def copies(xi, xo, ssem, rsem):
        x, y, c, _ = _place()
        sends, recvs = [], []
        for m in range(1, N_DEV):
            px, py, pc = (1 - x) if m & 4 else x, (1 - y) if m & 2 else y, (1 - c) if m & 1 else c
            sends.append(_remote(xi[0], xo[0].at[4 * x + 2 * y + c], ssem.at[m - 1], rsem.at[m - 1], (px, py, pc)))
            blk = xo[0].at[4 * px + 2 * py + pc]
            recvs.append(_remote(blk, blk, ssem.at[m - 1], rsem.at[m - 1], (px, py, pc)))
        return sends, recvs

    return _Exchange([v], [jax.ShapeDtypeStruct((N_DEV,) + v.shape, v.dtype)], {}, N_DEV - 1, copies)


def run_exchanges(exchanges, name):
    return _call(lambda: None, name, (), [], [], [], [], (), hosted=exchanges)[1]


def sibling_halves(grads):
    def copies(xi, xo, ssem, rsem):
        x, y, c, _ = _place()
        sends = [_remote(xi[k].at[1 - c], xo[k], ssem.at[k], rsem.at[k], (x, y, 1 - c)) for k in range(len(grads))]
        return sends, sends

    return _Exchange(grads, [jax.ShapeDtypeStruct(g.shape[1:], g.dtype) for g in grads], {}, len(grads), copies)


def pair_sum(ghs, recvs, cidx, name):
    n = len(ghs)
    S = ghs[0].shape[1]

    def body(c_ref, *refs):
        for k in range(n):
            a_ref, b_ref, o_ref = refs[2 * k], refs[2 * k + 1], refs[2 * n + k]
            o_ref[...] = (a_ref[...].astype(F32) + b_ref[...].astype(F32)).astype(o_ref.dtype)

    in_specs, out_specs, out_shape, args = [], [], [], []
    for gh, recv in zip(ghs, recvs):
        _, _, Rh, C = gh.shape
        in_specs += [pl.BlockSpec((None, None, Rh, C), lambda s, c_ref: (c_ref[0], s, 0, 0)),
                     pl.BlockSpec((None, Rh, C), lambda s, c_ref: (s, 0, 0))]
        out_specs.append(pl.BlockSpec((None, Rh, C), lambda s, c_ref: (s, 0, 0)))
        out_shape.append(jax.ShapeDtypeStruct((S, Rh, C), BF16))
        args += [gh, recv]
    return pl.pallas_call(
        body, name=name, out_shape=out_shape,
        grid_spec=pltpu.PrefetchScalarGridSpec(num_scalar_prefetch=1, grid=(S,), in_specs=in_specs, out_specs=out_specs),
        compiler_params=_params(("parallel",)),
    )(cidx, *args)


def scatter_p1(parts):
    def copies(xi, xo, ssem, rsem):
        x, y, c, chips = _place()
        me, sib = 2 * x + y, (x, y, 1 - c)
        sends, recvs = [], []
        for k in range(len(parts)):
            s0 = 4 * k
            sends.append(_remote(xi[k].at[me], xo[k].at[me, c], ssem.at[s0 + 3], rsem.at[s0 + 3], sib))
            own = xo[k].at[me, 1 - c]
            recvs.append(_remote(own, own, ssem.at[s0 + 3], rsem.at[s0 + 3], sib))
            for j, (px, py) in enumerate(chips):
                sends.append(_remote(xi[k].at[2 * px + py], xo[k].at[me, c], ssem.at[s0 + j], rsem.at[s0 + j], (px, py, c)))
                blk = xo[k].at[2 * px + py, c]
                recvs.append(_remote(blk, blk, ssem.at[s0 + j], rsem.at[s0 + j], (px, py, c)))
        return sends, recvs

    return _Exchange(parts, [jax.ShapeDtypeStruct((p.shape[0], 2) + p.shape[1:], p.dtype) for p in parts], {},
                     4 * len(parts), copies)


def scatter_p2(bufs):
    def copies(xi, xo, ssem, rsem):
        x, y, c, chips = _place()
        sib = (x, y, 1 - c)
        sends, recvs = [], []
        for k in range(len(bufs)):
            for j, (px, py) in enumerate(chips):
                s0 = 3 * k + j
                blk = xo[k].at[2 * px + py, c]
                sends.append(_remote(blk, blk, ssem.at[s0], rsem.at[s0], sib))
                got = xo[k].at[2 * px + py, 1 - c]
                recvs.append(_remote(got, got, ssem.at[s0], rsem.at[s0], sib))
        return sends, recvs

    return _Exchange(bufs, [jax.ShapeDtypeStruct(b.shape, b.dtype) for b in bufs], {k: k for k in range(len(bufs))},
                     3 * len(bufs), copies)


def _adamw_math(w, g, m, v):
    m = ADAM_B1 * m + (1.0 - ADAM_B1) * g
    v = ADAM_B2 * v + (1.0 - ADAM_B2) * (g * g)
    m_hat = m / (1.0 - ADAM_B1 ** ADAM_STEP)
    v_hat = v / (1.0 - ADAM_B2 ** ADAM_STEP)
    delta = -ADAM_LR * (m_hat / (jnp.sqrt(v_hat) + ADAM_EPS) + ADAM_WD * w)
    return delta, m, v


def adamw_reduce(w, m, v, buf, part, place, lyr, bases, name, hosted=()):
    L, R, C = w.shape
    Rh = R // 2
    rb = _tile(Rh, ROW_TILE, 2 * SUBLANES)
    nb = Rh // rb

    def body(place_ref, p_ref, b0, b1, b2, b3, w_ref, m_ref, v_ref, *rest):
        go_ref, d_ref, mo_ref, vo_ref = rest[-4:]
        mine = (place_ref[1] == pl.program_id(0))
        g = None
        for p, b in enumerate((b0, b1, b2, b3)):
            val = jnp.where(mine & (place_ref[0] == p), p_ref[...], b[...]).astype(F32)
            g = val if g is None else g + val
        d, mn, vn = _adamw_math(w_ref[...], g, m_ref[...], v_ref[...])
        go_ref[...] = g
        d_ref[...] = d
        mo_ref[...] = mn
        vo_ref[...] = vn

    def buf_spec(p):
        def idx(h, i, pr):
            own = (pr[0] == p) & (pr[1] == h)
            return (p, jnp.where(own, 1 - h, h), i, 0)
        return pl.BlockSpec((None, None, rb, C), idx)

    blk = pl.BlockSpec((None, rb, C), lambda h, i, pr: (lyr, h * nb + i, 0))
    in_specs = [pl.BlockSpec((None, rb, C), lambda h, i, pr: (pr[0], i, 0))] + [buf_spec(p) for p in range(N_CHIPS)] + [blk] * 3
    args = [part, buf, buf, buf, buf, w, m, v]
    aliases = {}
    if bases is not None:
        in_specs += [pl.BlockSpec(memory_space=pl.ANY)] * 4
        aliases = {len(args) + k: k for k in range(4)}
        args += list(bases)
    shp = jax.ShapeDtypeStruct((L, R, C), F32)
    return _call(body, name, (2, nb), in_specs, [blk] * 4, [shp] * 4, args, ("parallel", "parallel"),
                 hosted=hosted, prefetch=[place], own_aliases=aliases)


def small_update(late, early, own, place, entries, loss_row, name):
    ne = len(entries)
    D = late.shape[1]

    def body(place_ref, late_ref, early_ref, own_ref, *refs):
        ins, outs = refs[:3 * ne], refs[3 * ne:]
        ch = place_ref[0]
        me = 2 * place_ref[0] + place_ref[1]

        def early_sum(rs, cs):
            acc = None
            for d in range(N_DEV):
                val = jnp.where(me == d, own_ref[rs, cs], early_ref[d, rs, cs])
                acc = val if acc is None else acc + val
            return acc

        outs[4 * ne][...] = early_sum(slice(loss_row, loss_row + 1), slice(0, D))
        for e, (source, row0, kind, w, _, _) in enumerate(entries):
            r, width = w.shape
            gsum = early_sum if source == "early" else (lambda rs, cs: late_ref[rs, cs])

            if kind == "full":
                g = gsum(slice(row0, row0 + r), slice(0, D))
            elif kind == "cols":
                g = gsum(slice(row0, row0 + r), slice(0, width))
                for q in range(1, N_CHIPS):
                    g = jnp.where(ch == q, gsum(slice(row0, row0 + r), slice(q * width, (q + 1) * width)), g)
            else:
                per_row = D // width
                g = gsum(slice(row0, row0 + 1), slice(0, width))
                for q in range(1, N_CHIPS):
                    rr = row0 + q // per_row
                    cc = (q % per_row) * width
                    g = jnp.where(ch == q, gsum(slice(rr, rr + 1), slice(cc, cc + width)), g)
            d, mn, vn = _adamw_math(ins[3 * e][...], g, ins[3 * e + 1][...], ins[3 * e + 2][...])
            outs[4 * e][...] = g
            outs[4 * e + 1][...] = d
            outs[4 * e + 2][...] = mn
            outs[4 * e + 3][...] = vn

    vm = pl.BlockSpec(memory_space=pltpu.VMEM)
    args, out_shape = [], []
    for _, _, _, w, m, v in entries:
        args += [w, m, v]
        out_shape += [jax.ShapeDtypeStruct(w.shape, F32)] * 4
    out_shape.append(jax.ShapeDtypeStruct((1, D), F32))
    return pl.pallas_call(
        body, name=name,
        in_specs=[pl.BlockSpec(memory_space=pltpu.SMEM), vm, vm, vm] + [vm] * (3 * ne),
        out_specs=[vm] * (4 * ne + 1), out_shape=out_shape,
        compiler_params=pltpu.CompilerParams(vmem_limit_bytes=VMEM_LIMIT),
    )(place, late, early, own, *args)


def _pack_rows(items, width, name):
    starts, at = [], 0
    for it in items:
        starts.append(at)
        at += -(-it.shape[0] // SUBLANES) * SUBLANES
    total = at

    def body(*refs):
        o_ref = refs[-1]
        o_ref[...] = jnp.zeros_like(o_ref)
        for it_ref, r0 in zip(refs[:-1], starts):
            o_ref[r0:r0 + it_ref.shape[0], :] = it_ref[...]

    vm = pl.BlockSpec(memory_space=pltpu.VMEM)
    packed = pl.pallas_call(body, name=name, in_specs=[vm] * len(items), out_specs=vm,
                            out_shape=jax.ShapeDtypeStruct((total, width), F32))(*items)
    return packed, starts


def kernel(x, a_norm, a_w_in, a_conv, a_w_out, b_norm, b_w_pw1, b_b_pw1, b_conv, b_b_conv, b_ln_g, b_ln_b, b_w_pw2, b_b_pw2, ffn_norm, ffn_w_gate, ffn_w_up, ffn_w_down, final_norm, loss_target, m_a_norm, m_a_w_in, m_a_conv, m_a_w_out, m_b_norm, m_b_w_pw1, m_b_b_pw1, m_b_conv, m_b_b_conv, m_b_ln_g, m_b_ln_b, m_b_w_pw2, m_b_b_pw2, m_ffn_norm, m_ffn_w_gate, m_ffn_w_up, m_ffn_w_down, m_final_norm, v_a_norm, v_a_w_in, v_a_conv, v_a_w_out, v_b_norm, v_b_w_pw1, v_b_b_pw1, v_b_conv, v_b_b_conv, v_b_ln_g, v_b_ln_b, v_b_w_pw2, v_b_b_pw2, v_ffn_norm, v_ffn_w_gate, v_ffn_w_up, v_ffn_w_down, v_final_norm):
    T, D = x.shape[1], x.shape[2]
    Dq = D // N_CHIPS
    cx, cy, cc = lax.axis_index("x"), lax.axis_index("y"), lax.axis_index("c")
    chip = (2 * cx + cy).astype(jnp.int32).reshape(1)
    cidx = cc.astype(jnp.int32).reshape(1)
    h0 = x.reshape(T, D)
    tgt = loss_target.reshape(T, D)

    small_shards = [a_conv[0], b_norm, b_b_pw1.reshape(2, Dq), b_conv[0], b_b_conv, b_ln_g, b_ln_b, b_b_pw2]
    packed, st = _pack_rows(small_shards, Dq, "pack_small")

    tr = lambda t: jnp.swapaxes(t, 1, 2)
    w_gate, m_gate, v_gate = tr(ffn_w_gate), tr(m_ffn_w_gate), tr(v_ffn_w_gate)
    w_up, m_up, v_up = tr(ffn_w_up), tr(m_ffn_w_up), tr(v_ffn_w_up)
    bf = lambda t: t.astype(BF16)
    s_in, s_out, s_pw1, s_pw2 = bf(a_w_in[0]), bf(a_w_out[0]), bf(b_w_pw1[0]), bf(b_w_pw2[0])
    s_gate, s_up, s_down = [bf(w_gate[l]) for l in (0, 1)], [bf(w_up[l]) for l in (0, 1)], [bf(ffn_w_down[l]) for l in (0, 1)]

    n0, (g_in,) = rms_fwd(h0, a_norm, "rms_a", hosted=[gather_whole([s_in], [])])
    bcv, (g_out, gate0, sw) = mm_cols(n0, g_in, None, "mm_w_in", hosted=[gather_p1([s_out, s_gate[0]]), gather_small(packed)])

    def whole(k, r):
        return jnp.transpose(sw[:, st[k]:st[k] + r, :], (1, 0, 2)).reshape(r, D)

    a_conv_f, b_norm_f = whole(0, 3), whole(1, 1)
    b_b_pw1_f = sw[:, st[2]:st[2] + 2, :].reshape(1, 2 * D)
    b_conv_f, b_b_conv_f, b_ln_g_f, b_ln_b_f, b_b_pw2_f = whole(3, b_conv.shape[1]), whole(4, 1), whole(5, 1), whole(6, 1), whole(7, 1)
    ya, h1, (g_out, up0, down0, gate0) = gateconv_fwd(bcv, a_conv_f, gather_p2([g_out]), h0, "gateconv_fwd",
                                                      hosted=[gather_p1([s_up[0], s_down[0]]), gather_p2([gate0])])
    g_out = g_out.reshape(1, D, D)
    n1, fg0, fu0, gu0, h2, (up0, down0, *later) = ffn_fwd(h1, ffn_norm[0:1], [gate0], "ffn_fwd0", arriving=gather_p2([up0, down0]),
                                                          hosted=[gather_p1([s_pw1, s_pw2, s_gate[1], s_up[1]])])
    n2, (g_pw1, g_pw2) = rms_fwd(h2, b_norm_f, "rms_b", hosted=[gather_p2(later[:2])])
    g_pw2 = g_pw2.reshape(1, D, D)
    ub, (down1, gate1, up1) = mm_cols(n2, g_pw1, b_b_pw1_f, "mm_pw1", hosted=[gather_p1([s_down[1]]), gather_p2(later[2:])])
    cu, sb, h3, (down1,) = bconv_fwd(ub, b_conv_f, b_b_conv_f, b_ln_g_f, b_ln_b_f, g_pw2, b_b_pw2_f, h2, "bconv_fwd",
                                     hosted=[gather_p2([down1])])
    n3, fg1, fu1, gu1, h4, _ = ffn_fwd(h3, ffn_norm[1:2], [gate1, up1, down1], "ffn_fwd1")
    loss_part, dh4, dh4_b, d_final = loss_head(h4, final_norm.reshape(1, D), tgt, "loss_head")

    place = jnp.concatenate([chip, cidx])

    def pair_sums(ghs, from_sib, tags):
        return pair_sum(ghs, from_sib, cidx, "pair_sum_" + "_".join(tags))

    def upd(w, m, v, bufs, parts, tag, hosted=()):
        res, xo = None, []
        for lyr, (b, p) in enumerate(zip(bufs, parts)):
            res, xo_l = adamw_reduce(w, m, v, b, p, place, lyr, res, "adamw_%s%d" % (tag, lyr), hosted=hosted if lyr == 0 else ())
            xo += xo_l
        return res, xo

    dg1, du1, dh3, dh3_b, d_fn1, _ = ffn_bwd(dh4, h3, ffn_norm[1:2], fg1, fu1, down1, gate1, up1, "ffn_bwd1")
    gh_down1, _ = tn_grad(gu1, dh4_b, N_CHIPS, True, "tn_down1")
    gh_gate1, _ = tn_grad(dg1, n3, N_CHIPS, True, "tn_gate1")
    gh_up1, _ = tn_grad(du1, n3, N_CHIPS, True, "tn_up1")
    f1 = [gh_gate1, gh_up1, gh_down1]

    dcu, d_ln_g, d_ln_b, d_b_conv, d_b_pw2, sib_f1 = pw2_ln_bwd(dh3, g_pw2, cu, b_ln_g_f, b_ln_b_f, "pw2_ln_bwd",
                                                                hosted=[sibling_halves(f1)])
    p_f1 = pair_sums(f1, sib_f1, ["gate1", "up1", "down1"])
    gh_pw2, _ = tn_grad_square(sb, dh3_b, N_CHIPS, "tn_pw2")
    dub, d_bconv_w, d_b_pw1, buf_f1 = bconv_bwd(dcu, ub, b_conv_f, "bconv_bwd", hosted=[scatter_p1(p_f1)])
    gh_pw1, _ = tn_grad(n2, dub, N_CHIPS, False, "tn_pw1")
    b_grp = [gh_pw1, gh_pw2]
    dh2, d_b_norm, dh2_b, (*buf_f1, sib_pw1, sib_pw2) = nt_cols_rms(
        dub, g_pw1, h2, b_norm_f, dh3, "nt_pw1", hosted=[scatter_p2(buf_f1), sibling_halves(b_grp)], also_bf16=True)
    sib_b = [sib_pw1, sib_pw2]
    p_b = pair_sums(b_grp, sib_b, ["pw1", "pw2"])

    early_grads = [d_b_norm, d_b_pw1.reshape(2, D), d_bconv_w, d_b_conv, d_ln_g, d_ln_b, d_b_pw2, d_fn1, d_final,
                   jnp.broadcast_to(loss_part, (1, D))]
    epacked, es = _pack_rows(early_grads, D, "pack_small_grads_early")
    dg0, du0, dh1, dh1_b, d_fn0, (*buf_b, eall) = ffn_bwd(dh2, h1, ffn_norm[0:1], fg0, fu0, down0, gate0, up0, "ffn_bwd0",
                                                         hosted=[scatter_p1(p_b), gather_all(epacked)])
    gh_down0, _ = tn_grad(gu0, dh2_b, N_CHIPS, True, "tn_down0")
    gh_gate0, (*buf_b, sib_down0) = tn_grad(dg0, n1, N_CHIPS, True, "tn_gate0",
                                            hosted=[scatter_p2(buf_b), sibling_halves([gh_down0])])
    p_down0 = pair_sums([gh_down0], [sib_down0], ["down0"])
    gh_up0, (buf_down0, sib_gate0) = tn_grad(du0, n1, N_CHIPS, True, "tn_up0",
                                             hosted=[scatter_p1(p_down0), sibling_halves([gh_gate0])])
    p_gate0 = pair_sums([gh_gate0], [sib_gate0], ["gate0"])
    dya, (buf_down0, sib_up0) = nt_rows(dh1, g_out, "nt_w_out",
                                        hosted=[scatter_p2([buf_down0]), sibling_halves([gh_up0])])
    p_up0 = pair_sums([gh_up0], [sib_up0], ["up0"])
    gh_out, _ = tn_grad_square(ya, dh1_b, N_CHIPS, "tn_w_out")
    dbcv, d_aconv_w, (buf_gate0, sib_out) = gateconv_bwd(dya[0], bcv, a_conv_f, "gateconv_bwd",
                                                         hosted=[scatter_p1(p_gate0), sibling_halves([gh_out])])
    p_out = pair_sums([gh_out], [sib_out], ["out"])
    gh_in, (buf_up0, buf_out, buf_gate0) = tn_grad(n0, dbcv, N_CHIPS, False, "tn_w_in",
                                                   hosted=[scatter_p1(p_up0 + p_out), scatter_p2([buf_gate0])])
    sib_in = run_exchanges([sibling_halves([gh_in])], "reduce_in_siblings")
    p_in = pair_sums([gh_in], sib_in, ["in"])
    grad_x, d_a_norm, (buf_in, buf_up0, buf_out) = nt_cols_rms(
        dbcv, g_in, h0, a_norm, dh1, "nt_w_in", hosted=[scatter_p1(p_in), scatter_p2([buf_up0, buf_out])])
    p_f0 = [p_gate0[0], p_up0[0], p_down0[0]]

    lpacked, ls = _pack_rows([d_a_norm, d_aconv_w, d_fn0], D, "pack_small_grads_late")
    lall, (buf_in,) = small_allreduce(lpacked, "allreduce_small_grads", hosted=[scatter_p2([buf_in])])
    buf_a, p_a = [buf_in, buf_out], [p_in[0], p_out[0]]

    r_gate, _ = upd(w_gate, m_gate, v_gate, [buf_gate0, buf_f1[0]], [p_f0[0], p_f1[0]], "gate")
    r_up, _ = upd(w_up, m_up, v_up, [buf_up0, buf_f1[1]], [p_f0[1], p_f1[1]], "up")
    r_down, _ = upd(ffn_w_down, m_ffn_w_down, v_ffn_w_down, [buf_down0, buf_f1[2]], [p_f0[2], p_f1[2]], "down")
    r_gate, r_up = [tr(t) for t in r_gate], [tr(t) for t in r_up]
    r_pw1, _ = upd(b_w_pw1, m_b_w_pw1, v_b_w_pw1, [buf_b[0]], [p_b[0]], "pw1")
    r_pw2, _ = upd(b_w_pw2, m_b_w_pw2, v_b_w_pw2, [buf_b[1]], [p_b[1]], "pw2")
    r_in, _ = upd(a_w_in, m_a_w_in, v_a_w_in, [buf_a[0]], [p_a[0]], "w_in")
    r_out, _ = upd(a_w_out, m_a_w_out, v_a_w_out, [buf_a[1]], [p_a[1]], "w_out")
    entries = [
        ("late", ls[0], "full", a_norm, m_a_norm, v_a_norm),
        ("late", ls[1], "cols", a_conv[0], m_a_conv[0], v_a_conv[0]),
        ("early", es[0], "cols", b_norm, m_b_norm, v_b_norm),
        ("early", es[1], "flat2", b_b_pw1, m_b_b_pw1, v_b_b_pw1),
        ("early", es[2], "cols", b_conv[0], m_b_conv[0], v_b_conv[0]),
        ("early", es[3], "cols", b_b_conv, m_b_b_conv, v_b_b_conv),
        ("early", es[4], "cols", b_ln_g, m_b_ln_g, v_b_ln_g),
        ("early", es[5], "cols", b_ln_b, m_b_ln_b, v_b_ln_b),
        ("early", es[6], "cols", b_b_pw2, m_b_b_pw2, v_b_b_pw2),
        ("late", ls[2], "full", ffn_norm[0:1], m_ffn_norm[0:1], v_ffn_norm[0:1]),
        ("early", es[7], "full", ffn_norm[1:2], m_ffn_norm[1:2], v_ffn_norm[1:2]),
        ("early", es[8], "full", final_norm.reshape(1, D), m_final_norm.reshape(1, D), v_final_norm.reshape(1, D)),
    ]
    so = small_update(lall, eall, epacked, place, entries, es[9], "small_update")
    sm = [so[4 * e:4 * e + 4] for e in range(len(entries))]

    def shaped(e, like):
        return [t.reshape(like.shape) for t in sm[e]]

    r_a_norm, r_a_conv, r_b_norm, r_b_b_pw1 = shaped(0, a_norm), shaped(1, a_conv), shaped(2, b_norm), shaped(3, b_b_pw1)
    r_b_conv, r_b_b_conv, r_b_ln_g, r_b_ln_b = shaped(4, b_conv), shaped(5, b_b_conv), shaped(6, b_ln_g), shaped(7, b_ln_b)
    r_b_b_pw2, r_final = shaped(8, b_b_pw2), shaped(11, final_norm)
    r_ffn_norm = [jnp.concatenate([l0, l1], axis=0) for l0, l1 in zip(sm[9], sm[10])]

    loss = so[4 * len(entries)][0, 0]
    order =[r_a_norm, r_in, r_a_conv, r_out, r_b_norm, r_pw1, r_b_b_pw1, r_b_conv, r_b_b_conv, r_b_ln_g, r_b_ln_b,
             r_pw2, r_b_b_pw2, r_ffn_norm, r_gate, r_up, r_down, r_final]
    outs = [loss, grad_x.reshape(x.shape)]
    for field in range(4):
        outs += [r[field] for r in order]
    return tuple(outs)
```

```python
import functools

import jax
import jax.numpy as jnp
from jax import lax
from jax.experimental import pallas as pl
from jax.experimental.pallas import tpu as pltpu

RMS_EPS = 1e-6
LN_EPS = 1e-5
ADAM_LR = 0.001
ADAM_B1 = 0.9
ADAM_B2 = 0.999
ADAM_EPS = 1e-08
ADAM_WD = 0.01
ADAM_STEP = 10

N_CHIPS = 4
N_DEV = 8
LANES = 128
SUBLANES = 8
HALO = 32
CONV_ROWS = 64
TOKEN_TILE = 512
WIDE_TOKEN_TILE = 1024
GRAD_TOKEN_TILE = 2048
FFN_ROW_CHUNKS = 2
FFN_FWD_SEGS_PER_STEP = 4
FFN_BWD_TOKEN_TILE = 256
ROW_TILE = 256
VMEM_LIMIT = 56 * 1024 * 1024
MESH = pl.DeviceIdType.MESH
BF16 = jnp.bfloat16
F32 = jnp.float32


def _tile(n, pref, mult=SUBLANES):
    t = min(n, pref) // mult * mult
    while n % t:
        t -= mult
    return t


def _params(sem):
    return pltpu.CompilerParams(dimension_semantics=sem, vmem_limit_bytes=VMEM_LIMIT)


def _sigmoid(x):
    return 0.5 * jnp.tanh(0.5 * x) + 0.5


class _Exchange:
    def __init__(self, ins, outs, aliases, n_sems, copies, then=None):
        self.ins, self.outs, self.aliases, self.n_sems, self.copies = list(ins), list(outs), dict(aliases), n_sems, copies
        self.then = then
        self.early = False

    def awaited_first(self):
        self.early = True
        return self

    def start(self, xi, xo, ssem, rsem):
        for cp in self.copies(xi, xo, ssem, rsem)[0]:
            cp.start()

    def finish(self, xi, xo, ssem, rsem):
        sends, recvs = self.copies(xi, xo, ssem, rsem)
        for cp in recvs:
            cp.wait_recv()
        if self.then is not None:
            sends2, recvs2 = self.then(xi, xo, ssem, rsem)
            for cp in sends2:
                cp.start()
            for cp in recvs2:
                cp.wait_recv()
            sends = sends + sends2
        for cp in sends:
            cp.wait_send()


def _call(body, name, grid, in_specs, out_specs, out_shape, args, sem, scratch_shapes=(), hosted=(), prefetch=(),
          own_aliases=None):
    in_specs, out_specs, out_shape = list(in_specs), list(out_specs), list(out_shape)
    scratch_shapes, hosted, prefetch = list(scratch_shapes), list(hosted), list(prefetch)
    n_pre, n_in, n_out, n_scr = len(prefetch), len(args), len(out_shape), len(scratch_shapes)
    x_in = [a for ex in hosted for a in ex.ins]
    x_out = [o for ex in hosted for o in ex.outs]
    aliases = {n_pre + i: o for i, o in (own_aliases or {}).items()}
    at_in, at_out = n_pre + n_in, n_out
    for ex in hosted:
        for i, o in ex.aliases.items():
            aliases[at_in + i] = at_out + o
        at_in += len(ex.ins)
        at_out += len(ex.outs)
    sems = [pltpu.SemaphoreType.DMA((ex.n_sems,)) for ex in hosted for _ in range(2)]

    def wrapped(*refs):
        pre, refs = refs[:n_pre], refs[n_pre:]
        ins, xi = refs[:n_in], refs[n_in:n_in + len(x_in)]
        refs = refs[n_in + len(x_in):]
        outs, xo = refs[:n_out], refs[n_out:n_out + len(x_out)]
        refs = refs[n_out + len(x_out):]
        scr, sm = refs[:n_scr], refs[n_scr:]
        views, a, b = [], 0, 0
        for e, ex in enumerate(hosted):
            views.append((xi[a:a + len(ex.ins)], xo[b:b + len(ex.outs)], sm[2 * e], sm[2 * e + 1]))
            a += len(ex.ins)
            b += len(ex.outs)
        first = last = None
        for ax, g in enumerate(grid):
            f, l = pl.program_id(ax) == 0, pl.program_id(ax) == g - 1
            first, last = (f, l) if first is None else (first & f, last & l)

        def begin():
            for ex, v in zip(hosted, views):
                ex.start(*v)
            for ex, v in zip(hosted, views):
                if ex.early:
                    ex.finish(*v)

        def end():
            for ex, v in zip(hosted, views):
                if not ex.early:
                    ex.finish(*v)

        if hosted and grid:
            pl.when(first)(begin)
        elif hosted:
            begin()
        early_refs = [r for ex, v in zip(hosted, views) if ex.early for r in v[1]]
        body(*pre, *ins, *outs, *scr, *early_refs)
        if hosted and grid:
            pl.when(last)(end)
        elif hosted:
            end()

    hbm = pl.BlockSpec(memory_space=pl.ANY)
    all_in, all_out = in_specs + [hbm] * len(x_in), out_specs + [hbm] * len(x_out)
    kw = dict(name=name, out_shape=out_shape + x_out, input_output_aliases=aliases,
              compiler_params=_params(tuple("arbitrary" for _ in grid) if hosted else sem))
    if prefetch:
        kw["grid_spec"] = pltpu.PrefetchScalarGridSpec(num_scalar_prefetch=n_pre, grid=grid, in_specs=all_in,
                                                       out_specs=all_out, scratch_shapes=scratch_shapes + sems)
    else:
        kw.update(grid=grid, in_specs=all_in, out_specs=all_out, scratch_shapes=scratch_shapes + sems)
    res = pl.pallas_call(wrapped, **kw)(*prefetch, *args, *x_in)
    return list(res[:n_out]), list(res[n_out:])


def rms_fwd(h, gain, name, hosted=()):
    T, D = h.shape
    tm = _tile(T, TOKEN_TILE)

    def body(h_ref, g_ref, o_ref):
        x = h_ref[...]
        r = lax.rsqrt(jnp.mean(x * x, axis=-1, keepdims=True) + RMS_EPS)
        o_ref[...] = (x * r * g_ref[...]).astype(o_ref.dtype)

    (n,), xo = _call(
        body, name, (T // tm,),
        [pl.BlockSpec((tm, D), lambda i: (i, 0)), pl.BlockSpec((1, D), lambda i: (0, 0))],
        [pl.BlockSpec((tm, D), lambda i: (i, 0))], [jax.ShapeDtypeStruct((T, D), BF16)],
        [h, gain], ("parallel",), hosted=hosted)
    return n, xo


def loss_head(h, gain, tgt, name):
    T, D = h.shape
    tm = _tile(T, TOKEN_TILE)

    def body(h_ref, g_ref, t_ref, loss_ref, dh_ref, dhb_ref, dg_ref):
        i = pl.program_id(0)
        x = h_ref[...]
        g = g_ref[...]
        r = lax.rsqrt(jnp.mean(x * x, axis=-1, keepdims=True) + RMS_EPS)
        xhat = x * r
        diff = xhat * g - t_ref[...]
        part_loss = 0.5 * jnp.sum(jnp.mean(diff * diff, axis=-1, keepdims=True), axis=0, keepdims=True)
        dy = diff * (1.0 / D)
        dxhat = dy * g
        dh = r * (dxhat - xhat * jnp.mean(dxhat * xhat, axis=-1, keepdims=True))
        dh_ref[...] = dh
        dhb_ref[...] = dh.astype(dhb_ref.dtype)
        part = jnp.sum(dy * xhat, axis=0, keepdims=True)

        @pl.when(i == 0)
        def _():
            dg_ref[...] = part
            loss_ref[...] = part_loss

        @pl.when(i > 0)
        def _():
            dg_ref[...] += part
            loss_ref[...] += part_loss

    row = pl.BlockSpec((tm, D), lambda i: (i, 0))
    vec = pl.BlockSpec((1, D), lambda i: (0, 0))
    return pl.pallas_call(
        body, name=name, grid=(T // tm,),
        in_specs=[row, vec, row],
        out_specs=[pl.BlockSpec((1, 1), lambda i: (0, 0)), row, row, vec],
        out_shape=[jax.ShapeDtypeStruct((1, 1), F32), jax.ShapeDtypeStruct((T, D), F32),
                   jax.ShapeDtypeStruct((T, D), BF16), jax.ShapeDtypeStruct((1, D), F32)],
        compiler_params=_params(("arbitrary",)),
    )(h, gain, tgt)


def _prev_halo_spec(tm, width):
    return pl.BlockSpec((HALO, width), lambda i: (jnp.maximum(i * (tm // HALO) - 1, 0), 0))


def _next_halo_spec(tm, width, T):
    return pl.BlockSpec((HALO, width), lambda i: (jnp.minimum((i + 1) * (tm // HALO), T // HALO - 1), 0))


def _shifted(win, off, rows):
    if off % SUBLANES == 0:
        return win[off:off + rows]
    n = win.shape[0]
    return pltpu.roll(win, (n - off) % n, 0)[:rows]


def _rowsum8(x):
    acc = x[0:SUBLANES]
    for q in range(1, x.shape[0] // SUBLANES):
        acc = acc + x[q * SUBLANES:(q + 1) * SUBLANES]
    return acc


def _conv_loops(tm, D, per_block):
    def chunk(r, carry):
        t0 = pl.multiple_of(r * CONV_ROWS, CONV_ROWS)
        for lb in range(D // LANES):
            per_block(t0, slice(lb * LANES, (lb + 1) * LANES))
        return carry

    lax.fori_loop(0, tm // CONV_ROWS, chunk, 0)


def gateconv_fwd(bcv, w, w_out, res, name, hosted=()):
    T, D3 = bcv.shape
    D = D3 // 3
    K = w.shape[0]
    tm = _tile(T, TOKEN_TILE)
    wo_shape = w_out.outs[0].shape

    def body(x_ref, halo_ref, w_ref, res_ref, y_ref, h_ref, pad_ref, wo_v, sem, wo_hbm):
        i = pl.program_id(0)

        @pl.when(i == 0)
        def _():
            cp = pltpu.make_async_copy(wo_hbm, wo_v, sem)
            cp.start()
            cp.wait()

        pad_ref[HALO:, :] = x_ref[:, D:2 * D] * x_ref[:, 2 * D:]
        pad_ref[:HALO, :] = jnp.where(i > 0, halo_ref[:, D:2 * D] * halo_ref[:, 2 * D:], 0.0)

        def block(t0, ls):
            win = pad_ref[pl.ds(t0, CONV_ROWS + HALO), ls]
            acc = jnp.zeros((CONV_ROWS, LANES), F32)
            for k in range(K):
                acc = acc + w_ref[k:k + 1, ls] * _shifted(win, HALO - (K - 1) + k, CONV_ROWS)
            y_ref[pl.ds(t0, CONV_ROWS), ls] = (x_ref[pl.ds(t0, CONV_ROWS), ls] * acc).astype(y_ref.dtype)

        _conv_loops(tm, D, block)
        h_ref[...] = res_ref[...] + jnp.dot(y_ref[...], wo_v[...].reshape(D, D), preferred_element_type=F32)

    row = pl.BlockSpec((tm, D), lambda i: (i, 0))
    (y, h), xo = _call(
        body, name, (T // tm,),
        [pl.BlockSpec((tm, D3), lambda i: (i, 0)), _prev_halo_spec(tm, D3), pl.BlockSpec((K, D), lambda i: (0, 0)), row],
        [row, row], [jax.ShapeDtypeStruct((T, D), BF16), jax.ShapeDtypeStruct((T, D), F32)],
        [bcv, bcv, w, res], ("arbitrary",),
        [pltpu.VMEM((tm + HALO, D), F32), pltpu.VMEM(wo_shape, BF16), pltpu.SemaphoreType.DMA],
        hosted=[w_out.awaited_first()] + list(hosted))
    return y, h, xo


def gateconv_bwd(dy, bcv, w, name, hosted=()):
    T, D3 = bcv.shape
    D = D3 // 3
    K = w.shape[0]
    tm = _tile(T, TOKEN_TILE)
    nt = T // tm

    def body(dy_ref, dyn_ref, x_ref, xp_ref, xn_ref, w_ref, o_ref, dw_ref, cv_ref, dc_ref, wacc_ref):
        i = pl.program_id(0)
        cv_ref[HALO:, :] = x_ref[:, D:2 * D] * x_ref[:, 2 * D:]
        cv_ref[:HALO, :] = jnp.where(i > 0, xp_ref[:, D:2 * D] * xp_ref[:, 2 * D:], 0.0)
        dc_ref[:tm, :] = dy_ref[...] * x_ref[:, :D]
        dc_ref[tm:, :] = jnp.where(i < nt - 1, dyn_ref[...] * xn_ref[:, :D], 0.0)

        @pl.when(i == 0)
        def _():
            wacc_ref[...] = jnp.zeros_like(wacc_ref)

        def block(t0, ls):
            cwin = cv_ref[pl.ds(t0, CONV_ROWS + HALO), ls]
            dwin = dc_ref[pl.ds(t0, CONV_ROWS + HALO), ls]
            dcon = dwin[:CONV_ROWS]
            conv = jnp.zeros((CONV_ROWS, LANES), F32)
            dcv = jnp.zeros((CONV_ROWS, LANES), F32)
            for k in range(K):
                wk = w_ref[k:k + 1, ls]
                cs = _shifted(cwin, HALO - (K - 1) + k, CONV_ROWS)
                conv = conv + wk * cs
                dcv = dcv + wk * _shifted(dwin, (K - 1) - k, CONV_ROWS)
                wacc_ref[k * SUBLANES:(k + 1) * SUBLANES, ls] += _rowsum8(dcon * cs)
            rows = pl.ds(t0, CONV_ROWS)
            o_ref[rows, ls] = (dy_ref[rows, ls] * conv).astype(o_ref.dtype)
            o_ref[rows, pl.ds(D + ls.start, LANES)] = (dcv * x_ref[rows, pl.ds(2 * D + ls.start, LANES)]).astype(o_ref.dtype)
            o_ref[rows, pl.ds(2 * D + ls.start, LANES)] = (dcv * x_ref[rows, pl.ds(D + ls.start, LANES)]).astype(o_ref.dtype)

        _conv_loops(tm, D, block)

        @pl.when(i == nt - 1)
        def _():
            for k in range(K):
                dw_ref[k:k + 1, :] = jnp.sum(wacc_ref[k * SUBLANES:(k + 1) * SUBLANES, :], axis=0, keepdims=True)

    (dx, dw), xo = _call(
        body, name, (nt,),
        [pl.BlockSpec((tm, D), lambda i: (i, 0)), _next_halo_spec(tm, D, T),
         pl.BlockSpec((tm, D3), lambda i: (i, 0)), _prev_halo_spec(tm, D3), _next_halo_spec(tm, D3, T),
         pl.BlockSpec((K, D), lambda i: (0, 0))],
        [pl.BlockSpec((tm, D3), lambda i: (i, 0)), pl.BlockSpec((K, D), lambda i: (0, 0))],
        [jax.ShapeDtypeStruct((T, D3), BF16), jax.ShapeDtypeStruct((K, D), F32)],
        [dy, dy, bcv, bcv, bcv, w], ("arbitrary",),
        [pltpu.VMEM((tm + HALO, D), F32), pltpu.VMEM((tm + HALO, D), F32), pltpu.VMEM((K * SUBLANES, D), F32)],
        hosted=hosted)
    return dx, dw, xo


def bconv_fwd(u, w, b_conv, ln_g, ln_b, w_out, b_out, res, name, hosted=()):
    T, D2 = u.shape
    D = D2 // 2
    K = w.shape[0]
    tm = _tile(T, TOKEN_TILE)

    def body(u_ref, halo_ref, w_ref, bc_ref, g_ref, b_ref, wo_ref, bo_ref, res_ref, cu_ref, s_ref, h_ref, pad_ref):
        i = pl.program_id(0)
        pad_ref[HALO:, :] = u_ref[:, :D] * _sigmoid(u_ref[:, D:])
        pad_ref[:HALO, :] = jnp.where(i > 0, halo_ref[:, :D] * _sigmoid(halo_ref[:, D:]), 0.0)

        def block(t0, ls):
            win = pad_ref[pl.ds(t0, CONV_ROWS + HALO), ls]
            acc = jnp.zeros((CONV_ROWS, LANES), F32)
            for k in range(K):
                acc = acc + w_ref[k:k + 1, ls] * _shifted(win, HALO - (K - 1) + k, CONV_ROWS)
            cu_ref[pl.ds(t0, CONV_ROWS), ls] = acc + bc_ref[:, ls]

        _conv_loops(tm, D, block)
        cu = cu_ref[...]
        mu = jnp.mean(cu, axis=-1, keepdims=True)
        xc = cu - mu
        rstd = lax.rsqrt(jnp.mean(xc * xc, axis=-1, keepdims=True) + LN_EPS)
        ln = xc * rstd * g_ref[...] + b_ref[...]
        s = (ln * _sigmoid(ln)).astype(s_ref.dtype)
        s_ref[...] = s
        h_ref[...] = res_ref[...] + bo_ref[...] + jnp.dot(s, wo_ref[0], preferred_element_type=F32)

    vec = pl.BlockSpec((1, D), lambda i: (0, 0))
    row = pl.BlockSpec((tm, D), lambda i: (i, 0))
    (cu, s, h), xo = _call(
        body, name, (T // tm,),
        [pl.BlockSpec((tm, D2), lambda i: (i, 0)), _prev_halo_spec(tm, D2), pl.BlockSpec((K, D), lambda i: (0, 0)), vec, vec, vec,
         pl.BlockSpec((1, D, D), lambda i: (0, 0, 0)), vec, row],
        [row, row, row], [jax.ShapeDtypeStruct((T, D), F32), jax.ShapeDtypeStruct((T, D), BF16), jax.ShapeDtypeStruct((T, D), F32)],
        [u, u, w, b_conv, ln_g, ln_b, w_out, b_out, res], ("parallel",), [pltpu.VMEM((tm + HALO, D), F32)], hosted=hosted)
    return cu, s, h, xo


def pw2_ln_bwd(dy, w, cu, ln_g, ln_b, name, hosted=()):
    T, D = cu.shape
    tm = _tile(T, TOKEN_TILE)

    def body(dy_ref, w_ref, cu_ref, g_ref, b_ref, dcu_ref, dg_ref, db_ref, dbc_ref, dbo_ref):
        i = pl.program_id(0)
        dy_ = dy_ref[...]
        ds = lax.dot_general(dy_.astype(BF16), w_ref[0], _NT, preferred_element_type=F32)
        cu_ = cu_ref[...]
        mu = jnp.mean(cu_, axis=-1, keepdims=True)
        xc = cu_ - mu
        rstd = lax.rsqrt(jnp.mean(xc * xc, axis=-1, keepdims=True) + LN_EPS)
        xh = xc * rstd
        ln = xh * g_ref[...] + b_ref[...]
        sg = _sigmoid(ln)
        dl = ds * (sg * (1.0 + ln * (1.0 - sg)))
        dxh = dl * g_ref[...]
        dcu = rstd * (dxh - jnp.mean(dxh, axis=-1, keepdims=True) - xh * jnp.mean(dxh * xh, axis=-1, keepdims=True))
        dcu_ref[...] = dcu
        pg = jnp.sum(dl * xh, axis=0, keepdims=True)
        pb = jnp.sum(dl, axis=0, keepdims=True)
        pc = jnp.sum(dcu, axis=0, keepdims=True)
        po = jnp.sum(dy_, axis=0, keepdims=True)

        @pl.when(i == 0)
        def _():
            dg_ref[...] = pg
            db_ref[...] = pb
            dbc_ref[...] = pc
            dbo_ref[...] = po

        @pl.when(i > 0)
        def _():
            dg_ref[...] += pg
            db_ref[...] += pb
            dbc_ref[...] += pc
            dbo_ref[...] += po

    vec = pl.BlockSpec((1, D), lambda i: (0, 0))
    row = pl.BlockSpec((tm, D), lambda i: (i, 0))
    vshape = jax.ShapeDtypeStruct((1, D), F32)
    outs, xo = _call(
        body, name, (T // tm,), [row, pl.BlockSpec((1, D, D), lambda i: (0, 0, 0)), row, vec, vec], [row, vec, vec, vec, vec],
        [jax.ShapeDtypeStruct((T, D), F32), vshape, vshape, vshape, vshape], [dy, w, cu, ln_g, ln_b], ("arbitrary",),
        hosted=hosted)
    return (*outs, xo)


def bconv_bwd(dcu, u, w, name, hosted=()):
    T, D2 = u.shape
    D = D2 // 2
    K = w.shape[0]
    tm = _tile(T, TOKEN_TILE)
    nt = T // tm

    def body(dc_ref, dcn_ref, u_ref, up_ref, w_ref, du_ref, dw_ref, db_ref, glu_ref, dpad_ref, dglu_ref, wacc_ref):
        i = pl.program_id(0)
        glu_ref[HALO:, :] = u_ref[:, :D] * _sigmoid(u_ref[:, D:])
        glu_ref[:HALO, :] = jnp.where(i > 0, up_ref[:, :D] * _sigmoid(up_ref[:, D:]), 0.0)
        dpad_ref[:tm, :] = dc_ref[...]
        dpad_ref[tm:, :] = jnp.where(i < nt - 1, dcn_ref[...], 0.0)

        @pl.when(i == 0)
        def _():
            wacc_ref[...] = jnp.zeros_like(wacc_ref)

        def block(t0, ls):
            gwin = glu_ref[pl.ds(t0, CONV_ROWS + HALO), ls]
            dwin = dpad_ref[pl.ds(t0, CONV_ROWS + HALO), ls]
            dcur = dwin[:CONV_ROWS]
            dglu = jnp.zeros((CONV_ROWS, LANES), F32)
            for k in range(K):
                dglu = dglu + w_ref[k:k + 1, ls] * _shifted(dwin, (K - 1) - k, CONV_ROWS)
                gs = _shifted(gwin, HALO - (K - 1) + k, CONV_ROWS)
                wacc_ref[k * SUBLANES:(k + 1) * SUBLANES, ls] += _rowsum8(dcur * gs)
            dglu_ref[pl.ds(t0, CONV_ROWS), ls] = dglu

        _conv_loops(tm, D, block)
        dglu = dglu_ref[...]
        a = u_ref[:, :D]
        sg = _sigmoid(u_ref[:, D:])
        da = dglu * sg
        dg = dglu * a * (sg * (1.0 - sg))
        du_ref[:, :D] = da.astype(du_ref.dtype)
        du_ref[:, D:] = dg.astype(du_ref.dtype)
        pa = jnp.sum(da, axis=0, keepdims=True)
        pg = jnp.sum(dg, axis=0, keepdims=True)

        @pl.when(i == 0)
        def _():
            db_ref[:, :D] = pa
            db_ref[:, D:] = pg

        @pl.when(i > 0)
        def _():
            db_ref[:, :D] += pa
            db_ref[:, D:] += pg

        @pl.when(i == nt - 1)
        def _():
            for k in range(K):
                dw_ref[k:k + 1, :] = jnp.sum(wacc_ref[k * SUBLANES:(k + 1) * SUBLANES, :], axis=0, keepdims=True)

    (du, dw, db), xo = _call(
        body, name, (nt,),
        [pl.BlockSpec((tm, D), lambda i: (i, 0)), _next_halo_spec(tm, D, T),
         pl.BlockSpec((tm, D2), lambda i: (i, 0)), _prev_halo_spec(tm, D2), pl.BlockSpec((K, D), lambda i: (0, 0))],
        [pl.BlockSpec((tm, D2), lambda i: (i, 0)), pl.BlockSpec((K, D), lambda i: (0, 0)), pl.BlockSpec((1, D2), lambda i: (0, 0))],
        [jax.ShapeDtypeStruct((T, D2), BF16), jax.ShapeDtypeStruct((K, D), F32), jax.ShapeDtypeStruct((1, D2), F32)],
        [dcu, dcu, u, u, w], ("arbitrary",),
        [pltpu.VMEM((tm + HALO, D), F32), pltpu.VMEM((tm + HALO, D), F32), pltpu.VMEM((tm, D), F32),
         pltpu.VMEM((K * SUBLANES, D), F32)], hosted=hosted)
    return du, dw, db, xo


def mm_cols(a, w, name, hosted=()):
    T, K = a.shape
    S, _, n = w.shape
    tm = _tile(T, WIDE_TOKEN_TILE)

    def body(a_ref, w_ref, o_ref):
        o_ref[...] = jnp.dot(a_ref[...], w_ref[...], preferred_element_type=F32)

    in_specs = [pl.BlockSpec((tm, K), lambda s, i: (i, 0)), pl.BlockSpec((None, K, n), lambda s, i: (s, 0, 0))]
    (out,), xo = _call(body, name, (S, T // tm), in_specs, [pl.BlockSpec((tm, n), lambda s, i: (i, s))],
                       [jax.ShapeDtypeStruct((T, S * n), F32)], [a, w], ("parallel", "parallel"), hosted=hosted)
    return out, xo


def rms_mm_cols(h, gain, w, bias, name, hosted=()):
    T, K = h.shape
    S, _, n = w.shape
    tm = _tile(T, TOKEN_TILE)

    def body(h_ref, gain_ref, w_ref, b_ref, n_ref, o_ref):
        x = h_ref[...]
        r = lax.rsqrt(jnp.mean(x * x, axis=-1, keepdims=True) + RMS_EPS)
        a = (x * r * gain_ref[...]).astype(n_ref.dtype)
        n_ref[...] = a
        for s in range(S):
            cols = slice(s * n, (s + 1) * n)
            o_ref[:, cols] = jnp.dot(a, w_ref[s], preferred_element_type=F32) + b_ref[:, cols]

    row = pl.BlockSpec((tm, K), lambda i: (i, 0))
    (n_out, out), xo = _call(
        body, name, (T // tm,),
        [row, pl.BlockSpec((1, K), lambda i: (0, 0)), pl.BlockSpec((S, K, n), lambda i: (0, 0, 0)),
         pl.BlockSpec((1, S * n), lambda i: (0, 0))],
        [row, pl.BlockSpec((tm, S * n), lambda i: (i, 0))],
        [jax.ShapeDtypeStruct((T, K), BF16), jax.ShapeDtypeStruct((T, S * n), F32)],
        [h, gain, w, bias], ("parallel",), hosted=hosted)
    return n_out, out, xo


def _load_weights(pairs, sems, S, G, i, p):
    def copies(seg):
        return [pltpu.make_async_copy(src.at[seg], dst.at[seg], sems.at[k, seg]) for k, (src, dst) in enumerate(pairs)]

    @pl.when((i == 0) & (p == 0))
    def _():
        for seg in range(S):
            for cp in copies(seg):
                cp.start()

    @pl.when((i == 0) & (p < S // G))
    def _():
        for j in range(G):
            for cp in copies(G * p + j):
                cp.wait()


def ffn_fwd(h, gain, weights, name, hosted=(), arriving=None):
    T, D = h.shape
    S, f, _ = weights[0].shape
    tm = _tile(T, TOKEN_TILE)
    rc = tm // FFN_ROW_CHUNKS
    chunks = [slice(r * rc, (r + 1) * rc) for r in range(FFN_ROW_CHUNKS)]
    G = FFN_FWD_SEGS_PER_STEP
    weights = list(weights)
    hosted = ([arriving.awaited_first()] if arriving is not None else []) + list(hosted)

    def body(h_ref, gain_ref, *refs):
        nw = len(weights)
        wg_hbm, wu_hbm, wd_hbm = list(refs[:nw]) + list(refs[nw + 9:])
        n_ref, g_ref, u_ref, gu_ref, o_ref, wg_v, wu_v, wd_v, sems = refs[nw:nw + 9]
        i, p = pl.program_id(0), pl.program_id(1)
        _load_weights([(wg_hbm, wg_v), (wu_hbm, wu_v), (wd_hbm, wd_v)], sems, S, G, i, p)

        @pl.when(p == 0)
        def _():
            x = h_ref[...]
            r = lax.rsqrt(jnp.mean(x * x, axis=-1, keepdims=True) + RMS_EPS)
            n_ref[...] = (x * r * gain_ref[...]).astype(n_ref.dtype)

        parts = []
        for rows in chunks:
            a = n_ref[rows, :]
            acc = None
            for j in range(G):
                seg = G * p + j
                g = lax.dot_general(a, wg_v[seg], _NT, preferred_element_type=F32)
                u = lax.dot_general(a, wu_v[seg], _NT, preferred_element_type=F32)
                gu = (g * _sigmoid(g) * u).astype(gu_ref.dtype)
                g_ref[j, rows, :] = g.astype(g_ref.dtype)
                u_ref[j, rows, :] = u.astype(u_ref.dtype)
                gu_ref[j, rows, :] = gu
                part = jnp.dot(gu, wd_v[seg], preferred_element_type=F32)
                acc = part if acc is None else acc + part
            parts.append(acc)

        @pl.when(p == 0)
        def _():
            for rows, part in zip(chunks, parts):
                o_ref[rows, :] = h_ref[rows, :] + part

        @pl.when(p > 0)
        def _():
            for rows, part in zip(chunks, parts):
                o_ref[rows, :] += part

    row = pl.BlockSpec((tm, D), lambda i, p: (i, 0))
    seg = pl.BlockSpec((G, tm, f), lambda i, p: (p, i, 0))
    hbm = pl.BlockSpec(memory_space=pl.ANY)
    segs = jax.ShapeDtypeStruct((S, T, f), BF16)
    outs, xo = _call(
        body, name, (T // tm, S // G),
        [row, pl.BlockSpec((1, D), lambda i, s: (0, 0))] + [hbm] * len(weights), [row, seg, seg, seg, row],
        [jax.ShapeDtypeStruct((T, D), BF16), segs, segs, segs, jax.ShapeDtypeStruct((T, D), F32)],
        [h, gain] + weights, ("arbitrary", "arbitrary"),
        [pltpu.VMEM((S, f, D), BF16), pltpu.VMEM((S, f, D), BF16), pltpu.VMEM((S, f, D), BF16), pltpu.SemaphoreType.DMA((3, S))],
        hosted=hosted)
    return (*outs, xo)


def ffn_bwd(dy, h, gain, g, u, wd, wg, wu, name, hosted=()):
    T, D = h.shape
    S, f, _ = wg.shape
    tm = _tile(T, FFN_BWD_TOKEN_TILE)
    nt = T // tm

    def body(dy_ref, h_ref, gain_ref, g_ref, u_ref, wd_hbm, wg_hbm, wu_hbm, dg_ref, du_ref, dh_ref, dhb_ref, dgain_ref,
             wd_v, wg_v, wu_v, sems):
        i = pl.program_id(0)
        _load_weights([(wd_hbm, wd_v), (wg_hbm, wg_v), (wu_hbm, wu_v)], sems, S, S, i, 0)
        dy_ = dy_ref[...]
        dyb = dy_.astype(BF16)
        dn = None
        for j in range(S):
            dgu = lax.dot_general(dyb, wd_v[j], _NT, preferred_element_type=F32)
            gv = g_ref[j].astype(F32)
            sg = _sigmoid(gv)
            dg = (dgu * u_ref[j].astype(F32) * (sg * (1.0 + gv * (1.0 - sg)))).astype(dg_ref.dtype)
            du = (dgu * (gv * sg)).astype(du_ref.dtype)
            dg_ref[j] = dg
            du_ref[j] = du
            part = jnp.dot(dg, wg_v[j], preferred_element_type=F32) + jnp.dot(du, wu_v[j], preferred_element_type=F32)
            dn = part if dn is None else dn + part
        x = h_ref[...]
        r = lax.rsqrt(jnp.mean(x * x, axis=-1, keepdims=True) + RMS_EPS)
        xhat = x * r
        dxhat = dn * gain_ref[...]
        dh = dy_ + r * (dxhat - xhat * jnp.mean(dxhat * xhat, axis=-1, keepdims=True))
        dh_ref[...] = dh
        dhb_ref[...] = dh.astype(dhb_ref.dtype)
        pg = jnp.sum(dn * xhat, axis=0, keepdims=True)

        @pl.when(i == 0)
        def _():
            dgain_ref[...] = pg

        @pl.when(i > 0)
        def _():
            dgain_ref[...] += pg

    row = pl.BlockSpec((tm, D), lambda i: (i, 0))
    vec = pl.BlockSpec((1, D), lambda i: (0, 0))
    seg = pl.BlockSpec((S, tm, f), lambda i: (0, i, 0))
    hbm = pl.BlockSpec(memory_space=pl.ANY)
    segs = jax.ShapeDtypeStruct((S, T, f), BF16)
    outs, xo = _call(
        body, name, (nt,),
        [row, row, vec, seg, seg, hbm, hbm, hbm], [seg, seg, row, row, vec],
        [segs, segs, jax.ShapeDtypeStruct((T, D), F32), jax.ShapeDtypeStruct((T, D), BF16), jax.ShapeDtypeStruct((1, D), F32)],
        [dy, h, gain, g, u, wd, wg, wu], ("arbitrary",),
        [pltpu.VMEM((S, f, D), BF16), pltpu.VMEM((S, f, D), BF16), pltpu.VMEM((S, f, D), BF16),
         pltpu.SemaphoreType.DMA((3, S))], hosted=hosted)
    return (*outs, xo)


_NT = (((1,), (1,)), ((), ()))
_TN = (((0,), (0,)), ((), ()))


def nt_rows(dy, w, name, hosted=()):
    T, N = dy.shape
    S, k, _ = w.shape
    tm = _tile(T, TOKEN_TILE)

    def body(dy_ref, w_ref, o_ref):
        o_ref[...] = lax.dot_general(dy_ref[...].astype(BF16), w_ref[...], _NT, preferred_element_type=F32)

    (out,), xo = _call(
        body, name, (T // tm, S),
        [pl.BlockSpec((tm, N), lambda i, s: (i, 0)), pl.BlockSpec((None, k, N), lambda i, s: (s, 0, 0))],
        [pl.BlockSpec((None, tm, k), lambda i, s: (s, i, 0))], [jax.ShapeDtypeStruct((S, T, k), F32)],
        [dy, w], ("parallel", "parallel"), hosted=hosted)
    return out, xo


def nt_cols_rms(dy, w, h, gain, dres, name, hosted=(), also_bf16=False):
    T, K = h.shape
    S, _, n = w.shape
    tm = _tile(T, TOKEN_TILE)

    def body(dy_ref, w_ref, h_ref, gain_ref, dres_ref, dh_ref, dgain_ref, *rest):
        i = pl.program_id(0)
        dn = None
        for s in range(S):
            part = lax.dot_general(dy_ref[:, s * n:(s + 1) * n], w_ref[s], _NT, preferred_element_type=F32)
            dn = part if dn is None else dn + part
        x = h_ref[...]
        r = lax.rsqrt(jnp.mean(x * x, axis=-1, keepdims=True) + RMS_EPS)
        xhat = x * r
        dxhat = dn * gain_ref[...]
        dh = dres_ref[...] + r * (dxhat - xhat * jnp.mean(dxhat * xhat, axis=-1, keepdims=True))
        dh_ref[...] = dh
        if also_bf16:
            rest[0][...] = dh.astype(BF16)
        pg = jnp.sum(dn * xhat, axis=0, keepdims=True)

        @pl.when(i == 0)
        def _():
            dgain_ref[...] = pg

        @pl.when(i > 0)
        def _():
            dgain_ref[...] += pg

    row = pl.BlockSpec((tm, K), lambda i: (i, 0))
    vec = pl.BlockSpec((1, K), lambda i: (0, 0))
    out_specs, out_shape = [row, vec], [jax.ShapeDtypeStruct((T, K), F32), jax.ShapeDtypeStruct((1, K), F32)]
    if also_bf16:
        out_specs, out_shape = out_specs + [row], out_shape + [jax.ShapeDtypeStruct((T, K), BF16)]
    outs, xo = _call(
        body, name, (T // tm,),
        [pl.BlockSpec((tm, S * n), lambda i: (i, 0)), pl.BlockSpec((S, K, n), lambda i: (0, 0, 0)), row, vec, row],
        out_specs, out_shape, [dy, w, h, gain, dres], ("arbitrary",), hosted=hosted)
    return (*outs, xo)


def tn_grad(a, dy, S, a_by_seg, name, hosted=()):
    T = dy.shape[0] if dy.ndim == 2 else dy.shape[1]
    tt = _tile(T, GRAD_TOKEN_TILE)
    if a_by_seg:
        R = a.shape[1] // S if a.ndim == 2 else a.shape[2]
        C = dy.shape[1]
        a_spec = pl.BlockSpec((tt, R), lambda s, t: (t, s)) if a.ndim == 2 else pl.BlockSpec((None, tt, R), lambda s, t: (s, t, 0))
        b_spec = pl.BlockSpec((tt, C), lambda s, t: (t, 0))
    else:
        R = a.shape[1]
        C = dy.shape[1] // S if dy.ndim == 2 else dy.shape[2]
        a_spec = pl.BlockSpec((tt, R), lambda s, t: (t, 0))
        b_spec = pl.BlockSpec((tt, C), lambda s, t: (t, s)) if dy.ndim == 2 else pl.BlockSpec((None, tt, C), lambda s, t: (s, t, 0))
    Rh = R // 2
    nt = T // tt

    def body(a_ref, b_ref, o_ref, acc_ref):
        t = pl.program_id(1)
        part = lax.dot_general(a_ref[...], b_ref[...].astype(BF16), _TN, preferred_element_type=F32)

        @pl.when(t == 0)
        def _():
            acc_ref[...] = part

        @pl.when(t > 0)
        def _():
            acc_ref[...] += part

        @pl.when(t == nt - 1)
        def _():
            o_ref[0] = acc_ref[:Rh, :].astype(o_ref.dtype)
            o_ref[1] = acc_ref[Rh:, :].astype(o_ref.dtype)

    (gh,), xo = _call(
        body, name, (S, nt), [a_spec, b_spec], [pl.BlockSpec((2, None, Rh, C), lambda s, t: (0, s, 0, 0))],
        [jax.ShapeDtypeStruct((2, S, Rh, C), BF16)], [a, dy], ("parallel", "arbitrary"), [pltpu.VMEM((R, C), F32)],
        hosted=hosted)
    return gh, xo


def tn_grad_square(a, dy, S, name, hosted=()):
    T, K = a.shape
    N = dy.shape[1]
    tt = _tile(T, GRAD_TOKEN_TILE)
    nt = T // tt
    Rh = K // S // 2

    def body(a_ref, b_ref, o_ref, acc_ref):
        t = pl.program_id(0)
        part = lax.dot_general(a_ref[...], b_ref[...].astype(BF16), _TN, preferred_element_type=F32)

        @pl.when(t == 0)
        def _():
            acc_ref[...] = part

        @pl.when(t > 0)
        def _():
            acc_ref[...] += part

        @pl.when(t == nt - 1)
        def _():
            for s in range(S):
                for hf in range(2):
                    r0 = (2 * s + hf) * Rh
                    o_ref[hf, s] = acc_ref[r0:r0 + Rh, :].astype(o_ref.dtype)

    (gh,), xo = _call(
        body, name, (nt,), [pl.BlockSpec((tt, K), lambda t: (t, 0)), pl.BlockSpec((tt, N), lambda t: (t, 0))],
        [pl.BlockSpec((2, S, Rh, N), lambda t: (0, 0, 0, 0))], [jax.ShapeDtypeStruct((2, S, Rh, N), BF16)],
        [a, dy], ("arbitrary",), [pltpu.VMEM((K, N), F32)], hosted=hosted)
    return gh, xo


def _place():
    x, y, c = lax.axis_index("x"), lax.axis_index("y"), lax.axis_index("c")
    chips = [(1 - x, y), (x, 1 - y), (1 - x, 1 - y)]
    return x, y, c, chips


def _remote(src, dst, send_sem, recv_sem, dev):
    return pltpu.make_async_remote_copy(src_ref=src, dst_ref=dst, send_sem=send_sem, recv_sem=recv_sem,
                                        device_id=dev, device_id_type=MESH)


def small_allreduce(v, name, hosted=()):
    rows, W = v.shape

    def body(v_ref, o_ref, sib_ref, pair_ref, chips_ref, send_sems, recv_sems):
        x, y, c, chips = _place()
        me = 2 * x + y
        swap = _remote(v_ref, sib_ref, send_sems.at[3], recv_sems.at[3], (x, y, 1 - c))
        swap.start()
        swap.wait()
        mine, other = v_ref[...], sib_ref[...]
        pair_ref[...] = jnp.where(c == 0, mine, other) + jnp.where(c == 0, other, mine)
        sends = []
        for j, (px, py) in enumerate(chips):
            cp = _remote(pair_ref, chips_ref.at[me], send_sems.at[j], recv_sems.at[j], (px, py, c))
            cp.start()
            sends.append(cp)
        chips_ref[me] = pair_ref[...]
        for j, (px, py) in enumerate(chips):
            blk = chips_ref.at[2 * px + py]
            _remote(blk, blk, send_sems.at[j], recv_sems.at[j], (px, py, c)).wait_recv()
        for cp in sends:
            cp.wait_send()
        o_ref[...] = (chips_ref[0] + chips_ref[1]) + (chips_ref[2] + chips_ref[3])

    vm = pl.BlockSpec(memory_space=pltpu.VMEM)
    (out,), xo = _call(
        body, name, (), [vm], [vm], [jax.ShapeDtypeStruct((rows, W), F32)], [v], (),
        [pltpu.VMEM((rows, W), F32), pltpu.VMEM((rows, W), F32), pltpu.VMEM((N_CHIPS, rows, W), F32),
         pltpu.SemaphoreType.DMA((4,)), pltpu.SemaphoreType.DMA((4,))], hosted=hosted)
    return out, xo


def _gather_p1_copies(srcs, bufs, ssem, rsem, base):
    x, y, c, chips = _place()
    me, sib = 2 * x + y, (x, y, 1 - c)
    sends, recvs = [], []
    for k, (src, buf) in enumerate(zip(srcs, bufs)):
        rh = src.shape[0] // 2
        s0 = base + 4 * k
        sends.append(_remote(src, buf.at[me], ssem.at[s0 + 3], rsem.at[s0 + 3], sib))
        recvs.append(_remote(buf.at[me], buf.at[me], ssem.at[s0 + 3], rsem.at[s0 + 3], sib))
        for j, (px, py) in enumerate(chips):
            sends.append(_remote(src.at[pl.ds(c * rh, rh)], buf.at[me, pl.ds(c * rh, rh)], ssem.at[s0 + j], rsem.at[s0 + j], (px, py, c)))
            blk = buf.at[2 * px + py, pl.ds(c * rh, rh)]
            recvs.append(_remote(blk, blk, ssem.at[s0 + j], rsem.at[s0 + j], (px, py, c)))
    return sends, recvs


def _gather_p2_copies(bufs, ssem, rsem, base):
    x, y, c, chips = _place()
    sib = (x, y, 1 - c)
    sends, recvs = [], []
    for k, buf in enumerate(bufs):
        rh = buf.shape[1] // 2
        for j, (px, py) in enumerate(chips):
            s0 = base + 3 * k + j
            blk = buf.at[2 * px + py, pl.ds(c * rh, rh)]
            sends.append(_remote(blk, blk, ssem.at[s0], rsem.at[s0], sib))
            got = buf.at[2 * px + py, pl.ds((1 - c) * rh, rh)]
            recvs.append(_remote(got, got, ssem.at[s0], rsem.at[s0], sib))
    return sends, recvs


def _gathered_shape(s):
    return jax.ShapeDtypeStruct((N_CHIPS,) + s.shape, s.dtype)


def gather_p1(shards):
    return _Exchange(shards, [_gathered_shape(s) for s in shards], {}, 4 * len(shards),
                     lambda xi, xo, ss, rs: _gather_p1_copies(xi, xo, ss, rs, 0))


def gather_p2(bufs):
    return _Exchange(bufs, [jax.ShapeDtypeStruct(b.shape, b.dtype) for b in bufs], {k: k for k in range(len(bufs))},
                     3 * len(bufs), lambda xi, xo, ss, rs: _gather_p2_copies(xo, ss, rs, 0))


def gather_whole(whole, begun):
    nw, n = len(whole), len(whole) + len(begun)
    shards = list(whole) + list(begun)
    return _Exchange(shards, [_gathered_shape(s) for s in shards], {}, 4 * n + 3 * nw,
                     lambda xi, xo, ss, rs: _gather_p1_copies(xi, xo, ss, rs, 0),
                     then=lambda xi, xo, ss, rs: _gather_p2_copies(xo[:nw], ss, rs, 4 * n))


def gather_small(v):
    def copies(xi, xo, ssem, rsem):
        x, y, c, chips = _place()
        me, sib = 2 * x + y, (x, y, 1 - c)
        sends = [_remote(xi[0], xo[0].at[me], ssem.at[3], rsem.at[3], sib)]
        recvs = [_remote(xo[0].at[me], xo[0].at[me], ssem.at[3], rsem.at[3], sib)]
        for j, (px, py) in enumerate(chips):
            sends.append(_remote(xi[0], xo[0].at[me], ssem.at[j], rsem.at[j], (px, py, c)))
            blk = xo[0].at[2 * px + py]
            recvs.append(_remote(blk, blk, ssem.at[j], rsem.at[j], (px, py, c)))
        return sends, recvs

    return _Exchange([v], [_gathered_shape(v)], {}, 4, copies)


def gather_all(v):
    def copies(xi, xo, ssem, rsem):
        x, y, c, _ = _place()
        sends, recvs = [], []
        for m in range(1, N_DEV):
            px, py, pc = (1 - x) if m & 4 else x, (1 - y) if m & 2 else y, (1 - c) if m & 1 else c
            sends.append(_remote(xi[0], xo[0].at[4 * x + 2 * y + c], ssem.at[m - 1], rsem.at[m - 1], (px, py, pc)))
            blk = xo[0].at[4 * px + 2 * py + pc]
            recvs.append(_remote(blk, blk, ssem.at[m - 1], rsem.at[m - 1], (px, py, pc)))
        return sends, recvs

    return _Exchange([v], [jax.ShapeDtypeStruct((N_DEV,) + v.shape, v.dtype)], {}, N_DEV - 1, copies)


def run_exchanges(exchanges, name):
    return _call(lambda: None, name, (), [], [], [], [], (), hosted=exchanges)[1]


def sibling_halves(grads):
    def copies(xi, xo, ssem, rsem):
        x, y, c, _ = _place()
        sends = [_remote(xi[k].at[1 - c], xo[k], ssem.at[k], rsem.at[k], (x, y, 1 - c)) for k in range(len(grads))]
        return sends, sends

    return _Exchange(grads, [jax.ShapeDtypeStruct(g.shape[1:], g.dtype) for g in grads], {}, len(grads), copies)


def pair_sum(ghs, recvs, cidx, name):
    n = len(ghs)
    S = ghs[0].shape[1]

    def body(c_ref, *refs):
        for k in range(n):
            a_ref, b_ref, o_ref = refs[2 * k], refs[2 * k + 1], refs[2 * n + k]
            o_ref[...] = (a_ref[...].astype(F32) + b_ref[...].astype(F32)).astype(o_ref.dtype)

    in_specs, out_specs, out_shape, args = [], [], [], []
    for gh, recv in zip(ghs, recvs):
        _, _, Rh, C = gh.shape
        in_specs += [pl.BlockSpec((None, None, Rh, C), lambda s, c_ref: (c_ref[0], s, 0, 0)),
                     pl.BlockSpec((None, Rh, C), lambda s, c_ref: (s, 0, 0))]
        out_specs.append(pl.BlockSpec((None, Rh, C), lambda s, c_ref: (s, 0, 0)))
        out_shape.append(jax.ShapeDtypeStruct((S, Rh, C), BF16))
        args += [gh, recv]
    return pl.pallas_call(
        body, name=name, out_shape=out_shape,
        grid_spec=pltpu.PrefetchScalarGridSpec(num_scalar_prefetch=1, grid=(S,), in_specs=in_specs, out_specs=out_specs),
        compiler_params=_params(("parallel",)),
    )(cidx, *args)


def scatter_p1(parts):
    def copies(xi, xo, ssem, rsem):
        x, y, c, chips = _place()
        me, sib = 2 * x + y, (x, y, 1 - c)
        sends, recvs = [], []
        for k in range(len(parts)):
            s0 = 4 * k
            sends.append(_remote(xi[k].at[me], xo[k].at[me, c], ssem.at[s0 + 3], rsem.at[s0 + 3], sib))
            own = xo[k].at[me, 1 - c]
            recvs.append(_remote(own, own, ssem.at[s0 + 3], rsem.at[s0 + 3], sib))
            for j, (px, py) in enumerate(chips):
                sends.append(_remote(xi[k].at[2 * px + py], xo[k].at[me, c], ssem.at[s0 + j], rsem.at[s0 + j], (px, py, c)))
                blk = xo[k].at[2 * px + py, c]
                recvs.append(_remote(blk, blk, ssem.at[s0 + j], rsem.at[s0 + j], (px, py, c)))
        return sends, recvs

    return _Exchange(parts, [jax.ShapeDtypeStruct((p.shape[0], 2) + p.shape[1:], p.dtype) for p in parts], {},
                     4 * len(parts), copies)


def scatter_p2(bufs):
    def copies(xi, xo, ssem, rsem):
        x, y, c, chips = _place()
        sib = (x, y, 1 - c)
        sends, recvs = [], []
        for k in range(len(bufs)):
            for j, (px, py) in enumerate(chips):
                s0 = 3 * k + j
                blk = xo[k].at[2 * px + py, c]
                sends.append(_remote(blk, blk, ssem.at[s0], rsem.at[s0], sib))
                got = xo[k].at[2 * px + py, 1 - c]
                recvs.append(_remote(got, got, ssem.at[s0], rsem.at[s0], sib))
        return sends, recvs

    return _Exchange(bufs, [jax.ShapeDtypeStruct(b.shape, b.dtype) for b in bufs], {k: k for k in range(len(bufs))},
                     3 * len(bufs), copies)


def _adamw_math(w, g, m, v):
    m = ADAM_B1 * m + (1.0 - ADAM_B1) * g
    v = ADAM_B2 * v + (1.0 - ADAM_B2) * (g * g)
    m_hat = m / (1.0 - ADAM_B1 ** ADAM_STEP)
    v_hat = v / (1.0 - ADAM_B2 ** ADAM_STEP)
    delta = -ADAM_LR * (m_hat / (jnp.sqrt(v_hat) + ADAM_EPS) + ADAM_WD * w)
    return delta, m, v


def adamw_reduce(w, m, v, buf, part, place, lyr, bases, name, hosted=()):
    L, R, C = w.shape
    Rh = R // 2
    rb = _tile(Rh, ROW_TILE, 2 * SUBLANES)
    nb = Rh // rb

    def body(place_ref, p_ref, b0, b1, b2, b3, w_ref, m_ref, v_ref, *rest):
        go_ref, d_ref, mo_ref, vo_ref = rest[-4:]
        mine = (place_ref[1] == pl.program_id(0))
        g = None
        for p, b in enumerate((b0, b1, b2, b3)):
            val = jnp.where(mine & (place_ref[0] == p), p_ref[...], b[...]).astype(F32)
            g = val if g is None else g + val
        d, mn, vn = _adamw_math(w_ref[...], g, m_ref[...], v_ref[...])
        go_ref[...] = g
        d_ref[...] = d
        mo_ref[...] = mn
        vo_ref[...] = vn

    def buf_spec(p):
        def idx(h, i, pr):
            own = (pr[0] == p) & (pr[1] == h)
            return (p, jnp.where(own, 1 - h, h), i, 0)
        return pl.BlockSpec((None, None, rb, C), idx)

    blk = pl.BlockSpec((None, rb, C), lambda h, i, pr: (lyr, h * nb + i, 0))
    in_specs = [pl.BlockSpec((None, rb, C), lambda h, i, pr: (pr[0], i, 0))] + [buf_spec(p) for p in range(N_CHIPS)] + [blk] * 3
    args = [part, buf, buf, buf, buf, w, m, v]
    aliases = {}
    if bases is not None:
        in_specs += [pl.BlockSpec(memory_space=pl.ANY)] * 4
        aliases = {len(args) + k: k for k in range(4)}
        args += list(bases)
    shp = jax.ShapeDtypeStruct((L, R, C), F32)
    return _call(body, name, (2, nb), in_specs, [blk] * 4, [shp] * 4, args, ("parallel", "parallel"),
                 hosted=hosted, prefetch=[place], own_aliases=aliases)


def small_update(late, early, own, place, entries, loss_row, name):
    ne = len(entries)
    D = late.shape[1]

    def body(place_ref, late_ref, early_ref, own_ref, *refs):
        ins, outs = refs[:3 * ne], refs[3 * ne:]
        ch = place_ref[0]
        me = 2 * place_ref[0] + place_ref[1]

        def early_sum(rs, cs):
            acc = None
            for d in range(N_DEV):
                val = jnp.where(me == d, own_ref[rs, cs], early_ref[d, rs, cs])
                acc = val if acc is None else acc + val
            return acc

        outs[4 * ne][...] = early_sum(slice(loss_row, loss_row + 1), slice(0, D))
        for e, (source, row0, kind, w, _, _) in enumerate(entries):
            r, width = w.shape
            gsum = early_sum if source == "early" else (lambda rs, cs: late_ref[rs, cs])

            if kind == "full":
                g = gsum(slice(row0, row0 + r), slice(0, D))
            elif kind == "cols":
                g = gsum(slice(row0, row0 + r), slice(0, width))
                for q in range(1, N_CHIPS):
                    g = jnp.where(ch == q, gsum(slice(row0, row0 + r), slice(q * width, (q + 1) * width)), g)
            else:
                per_row = D // width
                g = gsum(slice(row0, row0 + 1), slice(0, width))
                for q in range(1, N_CHIPS):
                    rr = row0 + q // per_row
                    cc = (q % per_row) * width
                    g = jnp.where(ch == q, gsum(slice(rr, rr + 1), slice(cc, cc + width)), g)
            d, mn, vn = _adamw_math(ins[3 * e][...], g, ins[3 * e + 1][...], ins[3 * e + 2][...])
            outs[4 * e][...] = g
            outs[4 * e + 1][...] = d
            outs[4 * e + 2][...] = mn
            outs[4 * e + 3][...] = vn

    vm = pl.BlockSpec(memory_space=pltpu.VMEM)
    args, out_shape = [], []
    for _, _, _, w, m, v in entries:
        args += [w, m, v]
        out_shape += [jax.ShapeDtypeStruct(w.shape, F32)] * 4
    out_shape.append(jax.ShapeDtypeStruct((1, D), F32))
    return pl.pallas_call(
        body, name=name,
        in_specs=[pl.BlockSpec(memory_space=pltpu.SMEM), vm, vm, vm] + [vm] * (3 * ne),
        out_specs=[vm] * (4 * ne + 1), out_shape=out_shape,
        compiler_params=pltpu.CompilerParams(vmem_limit_bytes=VMEM_LIMIT),
    )(place, late, early, own, *args)


def _pack_rows(items, width, name):
    starts, at = [], 0
    for it in items:
        starts.append(at)
        at += -(-it.shape[0] // SUBLANES) * SUBLANES
    total = at

    def body(*refs):
        o_ref = refs[-1]
        o_ref[...] = jnp.zeros_like(o_ref)
        for it_ref, r0 in zip(refs[:-1], starts):
            o_ref[r0:r0 + it_ref.shape[0], :] = it_ref[...]

    vm = pl.BlockSpec(memory_space=pltpu.VMEM)
    packed = pl.pallas_call(body, name=name, in_specs=[vm] * len(items), out_specs=vm,
                            out_shape=jax.ShapeDtypeStruct((total, width), F32))(*items)
    return packed, starts


def kernel(x, a_norm, a_w_in, a_conv, a_w_out, b_norm, b_w_pw1, b_b_pw1, b_conv, b_b_conv, b_ln_g, b_ln_b, b_w_pw2, b_b_pw2, ffn_norm, ffn_w_gate, ffn_w_up, ffn_w_down, final_norm, loss_target, m_a_norm, m_a_w_in, m_a_conv, m_a_w_out, m_b_norm, m_b_w_pw1, m_b_b_pw1, m_b_conv, m_b_b_conv, m_b_ln_g, m_b_ln_b, m_b_w_pw2, m_b_b_pw2, m_ffn_norm, m_ffn_w_gate, m_ffn_w_up, m_ffn_w_down, m_final_norm, v_a_norm, v_a_w_in, v_a_conv, v_a_w_out, v_b_norm, v_b_w_pw1, v_b_b_pw1, v_b_conv, v_b_b_conv, v_b_ln_g, v_b_ln_b, v_b_w_pw2, v_b_b_pw2, v_ffn_norm, v_ffn_w_gate, v_ffn_w_up, v_ffn_w_down, v_final_norm):
    T, D = x.shape[1], x.shape[2]
    Dq = D // N_CHIPS
    cx, cy, cc = lax.axis_index("x"), lax.axis_index("y"), lax.axis_index("c")
    chip = (2 * cx + cy).astype(jnp.int32).reshape(1)
    cidx = cc.astype(jnp.int32).reshape(1)
    h0 = x.reshape(T, D)
    tgt = loss_target.reshape(T, D)

    small_shards = [a_conv[0], b_norm, b_b_pw1.reshape(2, Dq), b_conv[0], b_b_conv, b_ln_g, b_ln_b, b_b_pw2]
    packed, st = _pack_rows(small_shards, Dq, "pack_small")

    tr = lambda t: jnp.swapaxes(t, 1, 2)
    w_gate, m_gate, v_gate = tr(ffn_w_gate), tr(m_ffn_w_gate), tr(v_ffn_w_gate)
    w_up, m_up, v_up = tr(ffn_w_up), tr(m_ffn_w_up), tr(v_ffn_w_up)
    bf = lambda t: t.astype(BF16)
    s_in, s_out, s_pw1, s_pw2 = bf(a_w_in[0]), bf(a_w_out[0]), bf(b_w_pw1[0]), bf(b_w_pw2[0])
    s_gate, s_up, s_down = [bf(w_gate[l]) for l in (0, 1)], [bf(w_up[l]) for l in (0, 1)], [bf(ffn_w_down[l]) for l in (0, 1)]

    n0, (g_in,) = rms_fwd(h0, a_norm, "rms_a", hosted=[gather_whole([s_in], [])])
    bcv, (g_out, gate0, sw) = mm_cols(n0, g_in, "mm_w_in", hosted=[gather_p1([s_out, s_gate[0]]), gather_small(packed)])

    def whole(k, r):
        return jnp.transpose(sw[:, st[k]:st[k] + r, :], (1, 0, 2)).reshape(r, D)

    a_conv_f, b_norm_f = whole(0, 3), whole(1, 1)
    b_b_pw1_f = sw[:, st[2]:st[2] + 2, :].reshape(1, 2 * D)
    b_conv_f, b_b_conv_f, b_ln_g_f, b_ln_b_f, b_b_pw2_f = whole(3, b_conv.shape[1]), whole(4, 1), whole(5, 1), whole(6, 1), whole(7, 1)
    ya, h1, (g_out, up0, down0, gate0) = gateconv_fwd(bcv, a_conv_f, gather_p2([g_out]), h0, "gateconv_fwd",
                                                      hosted=[gather_p1([s_up[0], s_down[0]]), gather_p2([gate0])])
    g_out = g_out.reshape(1, D, D)
    n1, fg0, fu0, gu0, h2, (up0, down0, g_pw1, g_pw2, gate1, up1) = ffn_fwd(
        h1, ffn_norm[0:1], [gate0], "ffn_fwd0", arriving=gather_p2([up0, down0]),
        hosted=[gather_whole([s_pw1, s_pw2], [s_gate[1], s_up[1]])])
    g_pw2 = g_pw2.reshape(1, D, D)
    n2, ub, (down1, gate1, up1) = rms_mm_cols(h2, b_norm_f, g_pw1, b_b_pw1_f, "mm_pw1",
                                              hosted=[gather_p1([s_down[1]]), gather_p2([gate1, up1])])
    cu, sb, h3, (down1,) = bconv_fwd(ub, b_conv_f, b_b_conv_f, b_ln_g_f, b_ln_b_f, g_pw2, b_b_pw2_f, h2, "bconv_fwd",
                                     hosted=[gather_p2([down1])])
    n3, fg1, fu1, gu1, h4, _ = ffn_fwd(h3, ffn_norm[1:2], [gate1, up1, down1], "ffn_fwd1")
    loss_part, dh4, dh4_b, d_final = loss_head(h4, final_norm.reshape(1, D), tgt, "loss_head")

    place = jnp.concatenate([chip, cidx])

    def pair_sums(ghs, from_sib, tags):
        return pair_sum(ghs, from_sib, cidx, "pair_sum_" + "_".join(tags))

    def upd(w, m, v, bufs, parts, tag, hosted=()):
        res, xo = None, []
        for lyr, (b, p) in enumerate(zip(bufs, parts)):
            res, xo_l = adamw_reduce(w, m, v, b, p, place, lyr, res, "adamw_%s%d" % (tag, lyr), hosted=hosted if lyr == 0 else ())
            xo += xo_l
        return res, xo

    dg1, du1, dh3, dh3_b, d_fn1, _ = ffn_bwd(dh4, h3, ffn_norm[1:2], fg1, fu1, down1, gate1, up1, "ffn_bwd1")
    gh_down1, _ = tn_grad(gu1, dh4_b, N_CHIPS, True, "tn_down1")
    gh_gate1, _ = tn_grad(dg1, n3, N_CHIPS, True, "tn_gate1")
    gh_up1, _ = tn_grad(du1, n3, N_CHIPS, True, "tn_up1")
    f1 = [gh_gate1, gh_up1, gh_down1]

    dcu, d_ln_g, d_ln_b, d_b_conv, d_b_pw2, sib_f1 = pw2_ln_bwd(dh3, g_pw2, cu, b_ln_g_f, b_ln_b_f, "pw2_ln_bwd",
                                                                hosted=[sibling_halves(f1)])
    p_f1 = pair_sums(f1, sib_f1, ["gate1", "up1", "down1"])
    gh_pw2, _ = tn_grad_square(sb, dh3_b, N_CHIPS, "tn_pw2")
    dub, d_bconv_w, d_b_pw1, buf_f1 = bconv_bwd(dcu, ub, b_conv_f, "bconv_bwd", hosted=[scatter_p1(p_f1)])
    gh_pw1, _ = tn_grad(n2, dub, N_CHIPS, False, "tn_pw1")
    b_grp = [gh_pw1, gh_pw2]
    dh2, d_b_norm, dh2_b, (*buf_f1, sib_pw1, sib_pw2) = nt_cols_rms(
        dub, g_pw1, h2, b_norm_f, dh3, "nt_pw1", hosted=[scatter_p2(buf_f1), sibling_halves(b_grp)], also_bf16=True)
    sib_b = [sib_pw1, sib_pw2]
    p_b = pair_sums(b_grp, sib_b, ["pw1", "pw2"])

    early_grads = [d_b_norm, d_b_pw1.reshape(2, D), d_bconv_w, d_b_conv, d_ln_g, d_ln_b, d_b_pw2, d_fn1, d_final,
                   jnp.broadcast_to(loss_part, (1, D))]
    epacked, es = _pack_rows(early_grads, D, "pack_small_grads_early")
    dg0, du0, dh1, dh1_b, d_fn0, (*buf_b, eall) = ffn_bwd(dh2, h1, ffn_norm[0:1], fg0, fu0, down0, gate0, up0, "ffn_bwd0",
                                                         hosted=[scatter_p1(p_b), gather_all(epacked)])
    gh_down0, _ = tn_grad(gu0, dh2_b, N_CHIPS, True, "tn_down0")
    gh_gate0, (*buf_b, sib_down0) = tn_grad(dg0, n1, N_CHIPS, True, "tn_gate0",
                                            hosted=[scatter_p2(buf_b), sibling_halves([gh_down0])])
    p_down0 = pair_sums([gh_down0], [sib_down0], ["down0"])
    gh_up0, (buf_down0, sib_gate0) = tn_grad(du0, n1, N_CHIPS, True, "tn_up0",
                                             hosted=[scatter_p1(p_down0), sibling_halves([gh_gate0])])
    p_gate0 = pair_sums([gh_gate0], [sib_gate0], ["gate0"])
    dya, (buf_down0, sib_up0) = nt_rows(dh1, g_out, "nt_w_out",
                                        hosted=[scatter_p2([buf_down0]), sibling_halves([gh_up0])])
    p_up0 = pair_sums([gh_up0], [sib_up0], ["up0"])
    gh_out, _ = tn_grad_square(ya, dh1_b, N_CHIPS, "tn_w_out")
    dbcv, d_aconv_w, (buf_gate0, sib_out) = gateconv_bwd(dya[0], bcv, a_conv_f, "gateconv_bwd",
                                                         hosted=[scatter_p1(p_gate0), sibling_halves([gh_out])])
    p_out = pair_sums([gh_out], [sib_out], ["out"])
    gh_in, (buf_up0, buf_out, buf_gate0) = tn_grad(n0, dbcv, N_CHIPS, False, "tn_w_in",
                                                   hosted=[scatter_p1(p_up0 + p_out), scatter_p2([buf_gate0])])
    sib_in = run_exchanges([sibling_halves([gh_in])], "reduce_in_siblings")
    p_in = pair_sums([gh_in], sib_in, ["in"])
    grad_x, d_a_norm, (buf_in, buf_up0, buf_out) = nt_cols_rms(
        dbcv, g_in, h0, a_norm, dh1, "nt_w_in", hosted=[scatter_p1(p_in), scatter_p2([buf_up0, buf_out])])
    p_f0 = [p_gate0[0], p_up0[0], p_down0[0]]

    lpacked, ls = _pack_rows([d_a_norm, d_aconv_w, d_fn0], D, "pack_small_grads_late")
    lall, (buf_in,) = small_allreduce(lpacked, "allreduce_small_grads", hosted=[scatter_p2([buf_in])])
    buf_a, p_a = [buf_in, buf_out], [p_in[0], p_out[0]]

    r_gate, _ = upd(w_gate, m_gate, v_gate, [buf_gate0, buf_f1[0]], [p_f0[0], p_f1[0]], "gate")
    r_up, _ = upd(w_up, m_up, v_up, [buf_up0, buf_f1[1]], [p_f0[1], p_f1[1]], "up")
    r_down, _ = upd(ffn_w_down, m_ffn_w_down, v_ffn_w_down, [buf_down0, buf_f1[2]], [p_f0[2], p_f1[2]], "down")
    r_gate, r_up = [tr(t) for t in r_gate], [tr(t) for t in r_up]
    r_pw1, _ = upd(b_w_pw1, m_b_w_pw1, v_b_w_pw1, [buf_b[0]], [p_b[0]], "pw1")
    r_pw2, _ = upd(b_w_pw2, m_b_w_pw2, v_b_w_pw2, [buf_b[1]], [p_b[1]], "pw2")
    r_in, _ = upd(a_w_in, m_a_w_in, v_a_w_in, [buf_a[0]], [p_a[0]], "w_in")
    r_out, _ = upd(a_w_out, m_a_w_out, v_a_w_out, [buf_a[1]], [p_a[1]], "w_out")
    entries = [
        ("late", ls[0], "full", a_norm, m_a_norm, v_a_norm),
        ("late", ls[1], "cols", a_conv[0], m_a_conv[0], v_a_conv[0]),
        ("early", es[0], "cols", b_norm, m_b_norm, v_b_norm),
        ("early", es[1], "flat2", b_b_pw1, m_b_b_pw1, v_b_b_pw1),
        ("early", es[2], "cols", b_conv[0], m_b_conv[0], v_b_conv[0]),
        ("early", es[3], "cols", b_b_conv, m_b_b_conv, v_b_b_conv),
        ("early", es[4], "cols", b_ln_g, m_b_ln_g, v_b_ln_g),
        ("early", es[5], "cols", b_ln_b, m_b_ln_b, v_b_ln_b),
        ("early", es[6], "cols", b_b_pw2, m_b_b_pw2, v_b_b_pw2),
        ("late", ls[2], "full", ffn_norm[0:1], m_ffn_norm[0:1], v_ffn_norm[0:1]),
        ("early", es[7], "full", ffn_norm[1:2], m_ffn_norm[1:2], v_ffn_norm[1:2]),
        ("early", es[8], "full", final_norm.reshape(1, D), m_final_norm.reshape(1, D), v_final_norm.reshape(1, D)),
    ]
    so = small_update(lall, eall, epacked, place, entries, es[9], "small_update")
    sm = [so[4 * e:4 * e + 4] for e in range(len(entries))]

    def shaped(e, like):
        return [t.reshape(like.shape) for t in sm[e]]

    r_a_norm, r_a_conv, r_b_norm, r_b_b_pw1 = shaped(0, a_norm), shaped(1, a_conv), shaped(2, b_norm), shaped(3, b_b_pw1)
    r_b_conv, r_b_b_conv, r_b_ln_g, r_b_ln_b = shaped(4, b_conv), shaped(5, b_b_conv), shaped(6, b_ln_g), shaped(7, b_ln_b)
    r_b_b_pw2, r_final = shaped(8, b_b_pw2), shaped(11, final_norm)
    r_ffn_norm = [jnp.concatenate([l0, l1], axis=0) for l0, l1 in zip(sm[9], sm[10])]

    loss = so[4 * len(entries)][0, 0]
    order =[r_a_norm, r_in, r_a_conv, r_out, r_b_norm, r_pw1, r_b_b_pw1, r_b_conv, r_b_b_conv, r_b_ln_g, r_b_ln_b,
             r_pw2, r_b_b_pw2, r_ffn_norm, r_gate, r_up, r_down, r_final]
    outs = [loss, grad_x.reshape(x.shape)]
    for field in range(4):
        outs += [r[field] for r in order]
    return tuple(outs)
```

```python
import functools

import jax
import jax.numpy as jnp
from jax import lax
from jax.experimental import pallas as pl
from jax.experimental.pallas import tpu as pltpu

RMS_EPS = 1e-6
LN_EPS = 1e-5
ADAM_LR = 0.001
ADAM_B1 = 0.9
ADAM_B2 = 0.999
ADAM_EPS = 1e-08
ADAM_WD = 0.01
ADAM_STEP = 10

N_CHIPS = 4
N_DEV = 8
LANES = 128
SUBLANES = 8
HALO = 32
CONV_ROWS = 64
TOKEN_TILE = 512
WIDE_TOKEN_TILE = 1024
GRAD_TOKEN_TILE = 2048
GRAD_SEGS_PER_STEP = 2
FFN_ROW_CHUNKS = 2
FFN_FWD_SEGS_PER_STEP = 4
FFN_BWD_TOKEN_TILE = 256
ROW_TILE = 256
VMEM_LIMIT = 56 * 1024 * 1024
MESH = pl.DeviceIdType.MESH
BF16 = jnp.bfloat16
F32 = jnp.float32


def _tile(n, pref, mult=SUBLANES):
    t = min(n, pref) // mult * mult
    while n % t:
        t -= mult
    return t


def _params(sem):
    return pltpu.CompilerParams(dimension_semantics=sem, vmem_limit_bytes=VMEM_LIMIT)


def _sigmoid(x):
    return 0.5 * jnp.tanh(0.5 * x) + 0.5


class _Exchange:
    def __init__(self, ins, outs, aliases, n_sems, copies, then=None):
        self.ins, self.outs, self.aliases, self.n_sems, self.copies = list(ins), list(outs), dict(aliases), n_sems, copies
        self.then = then
        self.early = False

    def awaited_first(self):
        self.early = True
        return self

    def start(self, xi, xo, ssem, rsem):
        for cp in self.copies(xi, xo, ssem, rsem)[0]:
            cp.start()

    def finish(self, xi, xo, ssem, rsem):
        sends, recvs = self.copies(xi, xo, ssem, rsem)
        for cp in recvs:
            cp.wait_recv()
        if self.then is not None:
            sends2, recvs2 = self.then(xi, xo, ssem, rsem)
            for cp in sends2:
                cp.start()
            for cp in recvs2:
                cp.wait_recv()
            sends = sends + sends2
        for cp in sends:
            cp.wait_send()


def _call(body, name, grid, in_specs, out_specs, out_shape, args, sem, scratch_shapes=(), hosted=(), prefetch=(),
          own_aliases=None):
    in_specs, out_specs, out_shape = list(in_specs), list(out_specs), list(out_shape)
    scratch_shapes, hosted, prefetch = list(scratch_shapes), list(hosted), list(prefetch)
    n_pre, n_in, n_out, n_scr = len(prefetch), len(args), len(out_shape), len(scratch_shapes)
    x_in = [a for ex in hosted for a in ex.ins]
    x_out = [o for ex in hosted for o in ex.outs]
    aliases = {n_pre + i: o for i, o in (own_aliases or {}).items()}
    at_in, at_out = n_pre + n_in, n_out
    for ex in hosted:
        for i, o in ex.aliases.items():
            aliases[at_in + i] = at_out + o
        at_in += len(ex.ins)
        at_out += len(ex.outs)
    sems = [pltpu.SemaphoreType.DMA((ex.n_sems,)) for ex in hosted for _ in range(2)]

    def wrapped(*refs):
        pre, refs = refs[:n_pre], refs[n_pre:]
        ins, xi = refs[:n_in], refs[n_in:n_in + len(x_in)]
        refs = refs[n_in + len(x_in):]
        outs, xo = refs[:n_out], refs[n_out:n_out + len(x_out)]
        refs = refs[n_out + len(x_out):]
        scr, sm = refs[:n_scr], refs[n_scr:]
        views, a, b = [], 0, 0
        for e, ex in enumerate(hosted):
            views.append((xi[a:a + len(ex.ins)], xo[b:b + len(ex.outs)], sm[2 * e], sm[2 * e + 1]))
            a += len(ex.ins)
            b += len(ex.outs)
        first = last = None
        for ax, g in enumerate(grid):
            f, l = pl.program_id(ax) == 0, pl.program_id(ax) == g - 1
            first, last = (f, l) if first is None else (first & f, last & l)

        def begin():
            for ex, v in zip(hosted, views):
                ex.start(*v)
            for ex, v in zip(hosted, views):
                if ex.early:
                    ex.finish(*v)

        def end():
            for ex, v in zip(hosted, views):
                if not ex.early:
                    ex.finish(*v)

        if hosted and grid:
            pl.when(first)(begin)
        elif hosted:
            begin()
        early_refs = [r for ex, v in zip(hosted, views) if ex.early for r in v[1]]
        body(*pre, *ins, *outs, *scr, *early_refs)
        if hosted and grid:
            pl.when(last)(end)
        elif hosted:
            end()

    hbm = pl.BlockSpec(memory_space=pl.ANY)
    all_in, all_out = in_specs + [hbm] * len(x_in), out_specs + [hbm] * len(x_out)
    kw = dict(name=name, out_shape=out_shape + x_out, input_output_aliases=aliases,
              compiler_params=_params(tuple("arbitrary" for _ in grid) if hosted else sem))
    if prefetch:
        kw["grid_spec"] = pltpu.PrefetchScalarGridSpec(num_scalar_prefetch=n_pre, grid=grid, in_specs=all_in,
                                                       out_specs=all_out, scratch_shapes=scratch_shapes + sems)
    else:
        kw.update(grid=grid, in_specs=all_in, out_specs=all_out, scratch_shapes=scratch_shapes + sems)
    res = pl.pallas_call(wrapped, **kw)(*prefetch, *args, *x_in)
    return list(res[:n_out]), list(res[n_out:])


def rms_fwd(h, gain, name, hosted=()):
    T, D = h.shape
    tm = _tile(T, TOKEN_TILE)

    def body(h_ref, g_ref, o_ref):
        x = h_ref[...]
        r = lax.rsqrt(jnp.mean(x * x, axis=-1, keepdims=True) + RMS_EPS)
        o_ref[...] = (x * r * g_ref[...]).astype(o_ref.dtype)

    (n,), xo = _call(
        body, name, (T // tm,),
        [pl.BlockSpec((tm, D), lambda i: (i, 0)), pl.BlockSpec((1, D), lambda i: (0, 0))],
        [pl.BlockSpec((tm, D), lambda i: (i, 0))], [jax.ShapeDtypeStruct((T, D), BF16)],
        [h, gain], ("parallel",), hosted=hosted)
    return n, xo


def loss_head(h, gain, tgt, name):
    T, D = h.shape
    tm = _tile(T, TOKEN_TILE)

    def body(h_ref, g_ref, t_ref, loss_ref, dh_ref, dhb_ref, dg_ref):
        i = pl.program_id(0)
        x = h_ref[...]
        g = g_ref[...]
        r = lax.rsqrt(jnp.mean(x * x, axis=-1, keepdims=True) + RMS_EPS)
        xhat = x * r
        diff = xhat * g - t_ref[...]
        part_loss = 0.5 * jnp.sum(jnp.mean(diff * diff, axis=-1, keepdims=True), axis=0, keepdims=True)
        dy = diff * (1.0 / D)
        dxhat = dy * g
        dh = r * (dxhat - xhat * jnp.mean(dxhat * xhat, axis=-1, keepdims=True))
        dh_ref[...] = dh
        dhb_ref[...] = dh.astype(dhb_ref.dtype)
        part = jnp.sum(dy * xhat, axis=0, keepdims=True)

        @pl.when(i == 0)
        def _():
            dg_ref[...] = part
            loss_ref[...] = part_loss

        @pl.when(i > 0)
        def _():
            dg_ref[...] += part
            loss_ref[...] += part_loss

    row = pl.BlockSpec((tm, D), lambda i: (i, 0))
    vec = pl.BlockSpec((1, D), lambda i: (0, 0))
    return pl.pallas_call(
        body, name=name, grid=(T // tm,),
        in_specs=[row, vec, row],
        out_specs=[pl.BlockSpec((1, 1), lambda i: (0, 0)), row, row, vec],
        out_shape=[jax.ShapeDtypeStruct((1, 1), F32), jax.ShapeDtypeStruct((T, D), F32),
                   jax.ShapeDtypeStruct((T, D), BF16), jax.ShapeDtypeStruct((1, D), F32)],
        compiler_params=_params(("arbitrary",)),
    )(h, gain, tgt)


def _prev_halo_spec(tm, width):
    return pl.BlockSpec((HALO, width), lambda i: (jnp.maximum(i * (tm // HALO) - 1, 0), 0))


def _next_halo_spec(tm, width, T):
    return pl.BlockSpec((HALO, width), lambda i: (jnp.minimum((i + 1) * (tm // HALO), T // HALO - 1), 0))


def _shifted(win, off, rows):
    if off % SUBLANES == 0:
        return win[off:off + rows]
    n = win.shape[0]
    return pltpu.roll(win, (n - off) % n, 0)[:rows]


def _rowsum8(x):
    acc = x[0:SUBLANES]
    for q in range(1, x.shape[0] // SUBLANES):
        acc = acc + x[q * SUBLANES:(q + 1) * SUBLANES]
    return acc


def _conv_loops(tm, D, per_block):
    def chunk(r, carry):
        t0 = pl.multiple_of(r * CONV_ROWS, CONV_ROWS)
        for lb in range(D // LANES):
            per_block(t0, slice(lb * LANES, (lb + 1) * LANES))
        return carry

    lax.fori_loop(0, tm // CONV_ROWS, chunk, 0)


def gateconv_fwd(bcv, w, w_out, res, name, hosted=()):
    T, D3 = bcv.shape
    D = D3 // 3
    K = w.shape[0]
    tm = _tile(T, TOKEN_TILE)
    wo_shape = w_out.outs[0].shape

    def body(x_ref, halo_ref, w_ref, res_ref, y_ref, h_ref, pad_ref, wo_v, sem, wo_hbm):
        i = pl.program_id(0)

        @pl.when(i == 0)
        def _():
            cp = pltpu.make_async_copy(wo_hbm, wo_v, sem)
            cp.start()
            cp.wait()

        pad_ref[HALO:, :] = x_ref[:, D:2 * D] * x_ref[:, 2 * D:]
        pad_ref[:HALO, :] = jnp.where(i > 0, halo_ref[:, D:2 * D] * halo_ref[:, 2 * D:], 0.0)

        def block(t0, ls):
            win = pad_ref[pl.ds(t0, CONV_ROWS + HALO), ls]
            acc = jnp.zeros((CONV_ROWS, LANES), F32)
            for k in range(K):
                acc = acc + w_ref[k:k + 1, ls] * _shifted(win, HALO - (K - 1) + k, CONV_ROWS)
            y_ref[pl.ds(t0, CONV_ROWS), ls] = (x_ref[pl.ds(t0, CONV_ROWS), ls] * acc).astype(y_ref.dtype)

        _conv_loops(tm, D, block)
        h_ref[...] = res_ref[...] + jnp.dot(y_ref[...], wo_v[...].reshape(D, D), preferred_element_type=F32)

    row = pl.BlockSpec((tm, D), lambda i: (i, 0))
    (y, h), xo = _call(
        body, name, (T // tm,),
        [pl.BlockSpec((tm, D3), lambda i: (i, 0)), _prev_halo_spec(tm, D3), pl.BlockSpec((K, D), lambda i: (0, 0)), row],
        [row, row], [jax.ShapeDtypeStruct((T, D), BF16), jax.ShapeDtypeStruct((T, D), F32)],
        [bcv, bcv, w, res], ("arbitrary",),
        [pltpu.VMEM((tm + HALO, D), F32), pltpu.VMEM(wo_shape, BF16), pltpu.SemaphoreType.DMA],
        hosted=[w_out.awaited_first()] + list(hosted))
    return y, h, xo


def gateconv_bwd(dy, bcv, w, name, hosted=()):
    T, D3 = bcv.shape
    D = D3 // 3
    K = w.shape[0]
    tm = _tile(T, TOKEN_TILE)
    nt = T // tm

    def body(dy_ref, dyn_ref, x_ref, xp_ref, xn_ref, w_ref, o_ref, dw_ref, cv_ref, dc_ref, wacc_ref):
        i = pl.program_id(0)
        cv_ref[HALO:, :] = x_ref[:, D:2 * D] * x_ref[:, 2 * D:]
        cv_ref[:HALO, :] = jnp.where(i > 0, xp_ref[:, D:2 * D] * xp_ref[:, 2 * D:], 0.0)
        dc_ref[:tm, :] = dy_ref[...] * x_ref[:, :D]
        dc_ref[tm:, :] = jnp.where(i < nt - 1, dyn_ref[...] * xn_ref[:, :D], 0.0)

        @pl.when(i == 0)
        def _():
            wacc_ref[...] = jnp.zeros_like(wacc_ref)

        def block(t0, ls):
            cwin = cv_ref[pl.ds(t0, CONV_ROWS + HALO), ls]
            dwin = dc_ref[pl.ds(t0, CONV_ROWS + HALO), ls]
            dcon = dwin[:CONV_ROWS]
            conv = jnp.zeros((CONV_ROWS, LANES), F32)
            dcv = jnp.zeros((CONV_ROWS, LANES), F32)
            for k in range(K):
                wk = w_ref[k:k + 1, ls]
                cs = _shifted(cwin, HALO - (K - 1) + k, CONV_ROWS)
                conv = conv + wk * cs
                dcv = dcv + wk * _shifted(dwin, (K - 1) - k, CONV_ROWS)
                wacc_ref[k * SUBLANES:(k + 1) * SUBLANES, ls] += _rowsum8(dcon * cs)
            rows = pl.ds(t0, CONV_ROWS)
            o_ref[rows, ls] = (dy_ref[rows, ls] * conv).astype(o_ref.dtype)
            o_ref[rows, pl.ds(D + ls.start, LANES)] = (dcv * x_ref[rows, pl.ds(2 * D + ls.start, LANES)]).astype(o_ref.dtype)
            o_ref[rows, pl.ds(2 * D + ls.start, LANES)] = (dcv * x_ref[rows, pl.ds(D + ls.start, LANES)]).astype(o_ref.dtype)

        _conv_loops(tm, D, block)

        @pl.when(i == nt - 1)
        def _():
            for k in range(K):
                dw_ref[k:k + 1, :] = jnp.sum(wacc_ref[k * SUBLANES:(k + 1) * SUBLANES, :], axis=0, keepdims=True)

    (dx, dw), xo = _call(
        body, name, (nt,),
        [pl.BlockSpec((tm, D), lambda i: (i, 0)), _next_halo_spec(tm, D, T),
         pl.BlockSpec((tm, D3), lambda i: (i, 0)), _prev_halo_spec(tm, D3), _next_halo_spec(tm, D3, T),
         pl.BlockSpec((K, D), lambda i: (0, 0))],
        [pl.BlockSpec((tm, D3), lambda i: (i, 0)), pl.BlockSpec((K, D), lambda i: (0, 0))],
        [jax.ShapeDtypeStruct((T, D3), BF16), jax.ShapeDtypeStruct((K, D), F32)],
        [dy, dy, bcv, bcv, bcv, w], ("arbitrary",),
        [pltpu.VMEM((tm + HALO, D), F32), pltpu.VMEM((tm + HALO, D), F32), pltpu.VMEM((K * SUBLANES, D), F32)],
        hosted=hosted)
    return dx, dw, xo


def bconv_fwd(u, w, b_conv, ln_g, ln_b, w_out, b_out, res, name, hosted=()):
    T, D2 = u.shape
    D = D2 // 2
    K = w.shape[0]
    tm = _tile(T, TOKEN_TILE)

    def body(u_ref, halo_ref, w_ref, bc_ref, g_ref, b_ref, wo_ref, bo_ref, res_ref, cu_ref, s_ref, h_ref, pad_ref):
        i = pl.program_id(0)
        pad_ref[HALO:, :] = u_ref[:, :D] * _sigmoid(u_ref[:, D:])
        pad_ref[:HALO, :] = jnp.where(i > 0, halo_ref[:, :D] * _sigmoid(halo_ref[:, D:]), 0.0)

        def block(t0, ls):
            win = pad_ref[pl.ds(t0, CONV_ROWS + HALO), ls]
            acc = jnp.zeros((CONV_ROWS, LANES), F32)
            for k in range(K):
                acc = acc + w_ref[k:k + 1, ls] * _shifted(win, HALO - (K - 1) + k, CONV_ROWS)
            cu_ref[pl.ds(t0, CONV_ROWS), ls] = acc + bc_ref[:, ls]

        _conv_loops(tm, D, block)
        cu = cu_ref[...]
        mu = jnp.mean(cu, axis=-1, keepdims=True)
        xc = cu - mu
        rstd = lax.rsqrt(jnp.mean(xc * xc, axis=-1, keepdims=True) + LN_EPS)
        ln = xc * rstd * g_ref[...] + b_ref[...]
        s = (ln * _sigmoid(ln)).astype(s_ref.dtype)
        s_ref[...] = s
        h_ref[...] = res_ref[...] + bo_ref[...] + jnp.dot(s, wo_ref[0], preferred_element_type=F32)

    vec = pl.BlockSpec((1, D), lambda i: (0, 0))
    row = pl.BlockSpec((tm, D), lambda i: (i, 0))
    (cu, s, h), xo = _call(
        body, name, (T // tm,),
        [pl.BlockSpec((tm, D2), lambda i: (i, 0)), _prev_halo_spec(tm, D2), pl.BlockSpec((K, D), lambda i: (0, 0)), vec, vec, vec,
         pl.BlockSpec((1, D, D), lambda i: (0, 0, 0)), vec, row],
        [row, row, row], [jax.ShapeDtypeStruct((T, D), F32), jax.ShapeDtypeStruct((T, D), BF16), jax.ShapeDtypeStruct((T, D), F32)],
        [u, u, w, b_conv, ln_g, ln_b, w_out, b_out, res], ("parallel",), [pltpu.VMEM((tm + HALO, D), F32)], hosted=hosted)
    return cu, s, h, xo


def pw2_ln_bwd(dy, w, cu, ln_g, ln_b, name, hosted=()):
    T, D = cu.shape
    tm = _tile(T, TOKEN_TILE)

    def body(dy_ref, w_ref, cu_ref, g_ref, b_ref, dcu_ref, dg_ref, db_ref, dbc_ref, dbo_ref):
        i = pl.program_id(0)
        dy_ = dy_ref[...]
        ds = lax.dot_general(dy_.astype(BF16), w_ref[0], _NT, preferred_element_type=F32)
        cu_ = cu_ref[...]
        mu = jnp.mean(cu_, axis=-1, keepdims=True)
        xc = cu_ - mu
        rstd = lax.rsqrt(jnp.mean(xc * xc, axis=-1, keepdims=True) + LN_EPS)
        xh = xc * rstd
        ln = xh * g_ref[...] + b_ref[...]
        sg = _sigmoid(ln)
        dl = ds * (sg * (1.0 + ln * (1.0 - sg)))
        dxh = dl * g_ref[...]
        dcu = rstd * (dxh - jnp.mean(dxh, axis=-1, keepdims=True) - xh * jnp.mean(dxh * xh, axis=-1, keepdims=True))
        dcu_ref[...] = dcu
        pg = jnp.sum(dl * xh, axis=0, keepdims=True)
        pb = jnp.sum(dl, axis=0, keepdims=True)
        pc = jnp.sum(dcu, axis=0, keepdims=True)
        po = jnp.sum(dy_, axis=0, keepdims=True)

        @pl.when(i == 0)
        def _():
            dg_ref[...] = pg
            db_ref[...] = pb
            dbc_ref[...] = pc
            dbo_ref[...] = po

        @pl.when(i > 0)
        def _():
            dg_ref[...] += pg
            db_ref[...] += pb
            dbc_ref[...] += pc
            dbo_ref[...] += po

    vec = pl.BlockSpec((1, D), lambda i: (0, 0))
    row = pl.BlockSpec((tm, D), lambda i: (i, 0))
    vshape = jax.ShapeDtypeStruct((1, D), F32)
    outs, xo = _call(
        body, name, (T // tm,), [row, pl.BlockSpec((1, D, D), lambda i: (0, 0, 0)), row, vec, vec], [row, vec, vec, vec, vec],
        [jax.ShapeDtypeStruct((T, D), F32), vshape, vshape, vshape, vshape], [dy, w, cu, ln_g, ln_b], ("arbitrary",),
        hosted=hosted)
    return (*outs, xo)


def bconv_bwd(dcu, u, w, name, hosted=()):
    T, D2 = u.shape
    D = D2 // 2
    K = w.shape[0]
    tm = _tile(T, TOKEN_TILE)
    nt = T // tm

    def body(dc_ref, dcn_ref, u_ref, up_ref, w_ref, du_ref, dw_ref, db_ref, glu_ref, dpad_ref, dglu_ref, wacc_ref):
        i = pl.program_id(0)
        glu_ref[HALO:, :] = u_ref[:, :D] * _sigmoid(u_ref[:, D:])
        glu_ref[:HALO, :] = jnp.where(i > 0, up_ref[:, :D] * _sigmoid(up_ref[:, D:]), 0.0)
        dpad_ref[:tm, :] = dc_ref[...]
        dpad_ref[tm:, :] = jnp.where(i < nt - 1, dcn_ref[...], 0.0)

        @pl.when(i == 0)
        def _():
            wacc_ref[...] = jnp.zeros_like(wacc_ref)

        def block(t0, ls):
            gwin = glu_ref[pl.ds(t0, CONV_ROWS + HALO), ls]
            dwin = dpad_ref[pl.ds(t0, CONV_ROWS + HALO), ls]
            dcur = dwin[:CONV_ROWS]
            dglu = jnp.zeros((CONV_ROWS, LANES), F32)
            for k in range(K):
                dglu = dglu + w_ref[k:k + 1, ls] * _shifted(dwin, (K - 1) - k, CONV_ROWS)
                gs = _shifted(gwin, HALO - (K - 1) + k, CONV_ROWS)
                wacc_ref[k * SUBLANES:(k + 1) * SUBLANES, ls] += _rowsum8(dcur * gs)
            dglu_ref[pl.ds(t0, CONV_ROWS), ls] = dglu

        _conv_loops(tm, D, block)
        dglu = dglu_ref[...]
        a = u_ref[:, :D]
        sg = _sigmoid(u_ref[:, D:])
        da = dglu * sg
        dg = dglu * a * (sg * (1.0 - sg))
        du_ref[:, :D] = da.astype(du_ref.dtype)
        du_ref[:, D:] = dg.astype(du_ref.dtype)
        pa = jnp.sum(da, axis=0, keepdims=True)
        pg = jnp.sum(dg, axis=0, keepdims=True)

        @pl.when(i == 0)
        def _():
            db_ref[:, :D] = pa
            db_ref[:, D:] = pg

        @pl.when(i > 0)
        def _():
            db_ref[:, :D] += pa
            db_ref[:, D:] += pg

        @pl.when(i == nt - 1)
        def _():
            for k in range(K):
                dw_ref[k:k + 1, :] = jnp.sum(wacc_ref[k * SUBLANES:(k + 1) * SUBLANES, :], axis=0, keepdims=True)

    (du, dw, db), xo = _call(
        body, name, (nt,),
        [pl.BlockSpec((tm, D), lambda i: (i, 0)), _next_halo_spec(tm, D, T),
         pl.BlockSpec((tm, D2), lambda i: (i, 0)), _prev_halo_spec(tm, D2), pl.BlockSpec((K, D), lambda i: (0, 0))],
        [pl.BlockSpec((tm, D2), lambda i: (i, 0)), pl.BlockSpec((K, D), lambda i: (0, 0)), pl.BlockSpec((1, D2), lambda i: (0, 0))],
        [jax.ShapeDtypeStruct((T, D2), BF16), jax.ShapeDtypeStruct((K, D), F32), jax.ShapeDtypeStruct((1, D2), F32)],
        [dcu, dcu, u, u, w], ("arbitrary",),
        [pltpu.VMEM((tm + HALO, D), F32), pltpu.VMEM((tm + HALO, D), F32), pltpu.VMEM((tm, D), F32),
         pltpu.VMEM((K * SUBLANES, D), F32)], hosted=hosted)
    return du, dw, db, xo


def mm_cols(a, w, name, hosted=()):
    T, K = a.shape
    S, _, n = w.shape
    tm = _tile(T, WIDE_TOKEN_TILE)

    def body(a_ref, w_ref, o_ref):
        o_ref[...] = jnp.dot(a_ref[...], w_ref[...], preferred_element_type=F32)

    in_specs = [pl.BlockSpec((tm, K), lambda s, i: (i, 0)), pl.BlockSpec((None, K, n), lambda s, i: (s, 0, 0))]
    (out,), xo = _call(body, name, (S, T // tm), in_specs, [pl.BlockSpec((tm, n), lambda s, i: (i, s))],
                       [jax.ShapeDtypeStruct((T, S * n), F32)], [a, w], ("parallel", "parallel"), hosted=hosted)
    return out, xo


def rms_mm_cols(h, gain, w, bias, name, hosted=()):
    T, K = h.shape
    S, _, n = w.shape
    tm = _tile(T, TOKEN_TILE)

    def body(h_ref, gain_ref, w_ref, b_ref, n_ref, o_ref):
        x = h_ref[...]
        r = lax.rsqrt(jnp.mean(x * x, axis=-1, keepdims=True) + RMS_EPS)
        a = (x * r * gain_ref[...]).astype(n_ref.dtype)
        n_ref[...] = a
        for s in range(S):
            cols = slice(s * n, (s + 1) * n)
            o_ref[:, cols] = jnp.dot(a, w_ref[s], preferred_element_type=F32) + b_ref[:, cols]

    row = pl.BlockSpec((tm, K), lambda i: (i, 0))
    (n_out, out), xo = _call(
        body, name, (T // tm,),
        [row, pl.BlockSpec((1, K), lambda i: (0, 0)), pl.BlockSpec((S, K, n), lambda i: (0, 0, 0)),
         pl.BlockSpec((1, S * n), lambda i: (0, 0))],
        [row, pl.BlockSpec((tm, S * n), lambda i: (i, 0))],
        [jax.ShapeDtypeStruct((T, K), BF16), jax.ShapeDtypeStruct((T, S * n), F32)],
        [h, gain, w, bias], ("parallel",), hosted=hosted)
    return n_out, out, xo


def _load_weights(pairs, sems, S, G, i, p):
    def copies(seg):
        return [pltpu.make_async_copy(src.at[seg], dst.at[seg], sems.at[k, seg]) for k, (src, dst) in enumerate(pairs)]

    @pl.when((i == 0) & (p == 0))
    def _():
        for seg in range(S):
            for cp in copies(seg):
                cp.start()

    @pl.when((i == 0) & (p < S // G))
    def _():
        for j in range(G):
            for cp in copies(G * p + j):
                cp.wait()


def ffn_fwd(h, gain, weights, name, hosted=(), arriving=None):
    T, D = h.shape
    S, f, _ = weights[0].shape
    tm = _tile(T, TOKEN_TILE)
    rc = tm // FFN_ROW_CHUNKS
    chunks = [slice(r * rc, (r + 1) * rc) for r in range(FFN_ROW_CHUNKS)]
    G = FFN_FWD_SEGS_PER_STEP
    weights = list(weights)
    hosted = ([arriving.awaited_first()] if arriving is not None else []) + list(hosted)

    def body(h_ref, gain_ref, *refs):
        nw = len(weights)
        wg_hbm, wu_hbm, wd_hbm = list(refs[:nw]) + list(refs[nw + 9:])
        n_ref, g_ref, u_ref, gu_ref, o_ref, wg_v, wu_v, wd_v, sems = refs[nw:nw + 9]
        i, p = pl.program_id(0), pl.program_id(1)
        _load_weights([(wg_hbm, wg_v), (wu_hbm, wu_v), (wd_hbm, wd_v)], sems, S, G, i, p)

        @pl.when(p == 0)
        def _():
            x = h_ref[...]
            r = lax.rsqrt(jnp.mean(x * x, axis=-1, keepdims=True) + RMS_EPS)
            n_ref[...] = (x * r * gain_ref[...]).astype(n_ref.dtype)

        parts = []
        for rows in chunks:
            a = n_ref[rows, :]
            acc = None
            for j in range(G):
                seg = G * p + j
                g = lax.dot_general(a, wg_v[seg], _NT, preferred_element_type=F32)
                u = lax.dot_general(a, wu_v[seg], _NT, preferred_element_type=F32)
                gu = (g * _sigmoid(g) * u).astype(gu_ref.dtype)
                g_ref[j, rows, :] = g.astype(g_ref.dtype)
                u_ref[j, rows, :] = u.astype(u_ref.dtype)
                gu_ref[j, rows, :] = gu
                part = jnp.dot(gu, wd_v[seg], preferred_element_type=F32)
                acc = part if acc is None else acc + part
            parts.append(acc)

        @pl.when(p == 0)
        def _():
            for rows, part in zip(chunks, parts):
                o_ref[rows, :] = h_ref[rows, :] + part

        @pl.when(p > 0)
        def _():
            for rows, part in zip(chunks, parts):
                o_ref[rows, :] += part

    row = pl.BlockSpec((tm, D), lambda i, p: (i, 0))
    seg = pl.BlockSpec((G, tm, f), lambda i, p: (p, i, 0))
    hbm = pl.BlockSpec(memory_space=pl.ANY)
    segs = jax.ShapeDtypeStruct((S, T, f), BF16)
    outs, xo = _call(
        body, name, (T // tm, S // G),
        [row, pl.BlockSpec((1, D), lambda i, s: (0, 0))] + [hbm] * len(weights), [row, seg, seg, seg, row],
        [jax.ShapeDtypeStruct((T, D), BF16), segs, segs, segs, jax.ShapeDtypeStruct((T, D), F32)],
        [h, gain] + weights, ("arbitrary", "arbitrary"),
        [pltpu.VMEM((S, f, D), BF16), pltpu.VMEM((S, f, D), BF16), pltpu.VMEM((S, f, D), BF16), pltpu.SemaphoreType.DMA((3, S))],
        hosted=hosted)
    return (*outs, xo)


def ffn_bwd(dy, h, gain, g, u, wd, wg, wu, name, hosted=()):
    T, D = h.shape
    S, f, _ = wg.shape
    tm = _tile(T, FFN_BWD_TOKEN_TILE)
    nt = T // tm

    def body(dy_ref, h_ref, gain_ref, g_ref, u_ref, wd_hbm, wg_hbm, wu_hbm, dg_ref, du_ref, dh_ref, dhb_ref, dgain_ref,
             wd_v, wg_v, wu_v, sems):
        i = pl.program_id(0)
        _load_weights([(wd_hbm, wd_v), (wg_hbm, wg_v), (wu_hbm, wu_v)], sems, S, S, i, 0)
        dy_ = dy_ref[...]
        dyb = dy_.astype(BF16)
        dn = None
        for j in range(S):
            dgu = lax.dot_general(dyb, wd_v[j], _NT, preferred_element_type=F32)
            gv = g_ref[j].astype(F32)
            sg = _sigmoid(gv)
            dg = (dgu * u_ref[j].astype(F32) * (sg * (1.0 + gv * (1.0 - sg)))).astype(dg_ref.dtype)
            du = (dgu * (gv * sg)).astype(du_ref.dtype)
            dg_ref[j] = dg
            du_ref[j] = du
            part = jnp.dot(dg, wg_v[j], preferred_element_type=F32) + jnp.dot(du, wu_v[j], preferred_element_type=F32)
            dn = part if dn is None else dn + part
        x = h_ref[...]
        r = lax.rsqrt(jnp.mean(x * x, axis=-1, keepdims=True) + RMS_EPS)
        xhat = x * r
        dxhat = dn * gain_ref[...]
        dh = dy_ + r * (dxhat - xhat * jnp.mean(dxhat * xhat, axis=-1, keepdims=True))
        dh_ref[...] = dh
        dhb_ref[...] = dh.astype(dhb_ref.dtype)
        pg = jnp.sum(dn * xhat, axis=0, keepdims=True)

        @pl.when(i == 0)
        def _():
            dgain_ref[...] = pg

        @pl.when(i > 0)
        def _():
            dgain_ref[...] += pg

    row = pl.BlockSpec((tm, D), lambda i: (i, 0))
    vec = pl.BlockSpec((1, D), lambda i: (0, 0))
    seg = pl.BlockSpec((S, tm, f), lambda i: (0, i, 0))
    hbm = pl.BlockSpec(memory_space=pl.ANY)
    segs = jax.ShapeDtypeStruct((S, T, f), BF16)
    outs, xo = _call(
        body, name, (nt,),
        [row, row, vec, seg, seg, hbm, hbm, hbm], [seg, seg, row, row, vec],
        [segs, segs, jax.ShapeDtypeStruct((T, D), F32), jax.ShapeDtypeStruct((T, D), BF16), jax.ShapeDtypeStruct((1, D), F32)],
        [dy, h, gain, g, u, wd, wg, wu], ("arbitrary",),
        [pltpu.VMEM((S, f, D), BF16), pltpu.VMEM((S, f, D), BF16), pltpu.VMEM((S, f, D), BF16),
         pltpu.SemaphoreType.DMA((3, S))], hosted=hosted)
    return (*outs, xo)


_NT = (((1,), (1,)), ((), ()))
_TN = (((0,), (0,)), ((), ()))


def nt_rows(dy, w, name, hosted=()):
    T, N = dy.shape
    S, k, _ = w.shape
    tm = _tile(T, TOKEN_TILE)

    def body(dy_ref, w_ref, o_ref):
        o_ref[...] = lax.dot_general(dy_ref[...].astype(BF16), w_ref[...], _NT, preferred_element_type=F32)

    (out,), xo = _call(
        body, name, (T // tm, S),
        [pl.BlockSpec((tm, N), lambda i, s: (i, 0)), pl.BlockSpec((None, k, N), lambda i, s: (s, 0, 0))],
        [pl.BlockSpec((None, tm, k), lambda i, s: (s, i, 0))], [jax.ShapeDtypeStruct((S, T, k), F32)],
        [dy, w], ("parallel", "parallel"), hosted=hosted)
    return out, xo


def nt_cols_rms(dy, w, h, gain, dres, name, hosted=(), also_bf16=False):
    T, K = h.shape
    S, _, n = w.shape
    tm = _tile(T, TOKEN_TILE)

    def body(dy_ref, w_ref, h_ref, gain_ref, dres_ref, dh_ref, dgain_ref, *rest):
        i = pl.program_id(0)
        dn = None
        for s in range(S):
            part = lax.dot_general(dy_ref[:, s * n:(s + 1) * n], w_ref[s], _NT, preferred_element_type=F32)
            dn = part if dn is None else dn + part
        x = h_ref[...]
        r = lax.rsqrt(jnp.mean(x * x, axis=-1, keepdims=True) + RMS_EPS)
        xhat = x * r
        dxhat = dn * gain_ref[...]
        dh = dres_ref[...] + r * (dxhat - xhat * jnp.mean(dxhat * xhat, axis=-1, keepdims=True))
        dh_ref[...] = dh
        if also_bf16:
            rest[0][...] = dh.astype(BF16)
        pg = jnp.sum(dn * xhat, axis=0, keepdims=True)

        @pl.when(i == 0)
        def _():
            dgain_ref[...] = pg

        @pl.when(i > 0)
        def _():
            dgain_ref[...] += pg

    row = pl.BlockSpec((tm, K), lambda i: (i, 0))
    vec = pl.BlockSpec((1, K), lambda i: (0, 0))
    out_specs, out_shape = [row, vec], [jax.ShapeDtypeStruct((T, K), F32), jax.ShapeDtypeStruct((1, K), F32)]
    if also_bf16:
        out_specs, out_shape = out_specs + [row], out_shape + [jax.ShapeDtypeStruct((T, K), BF16)]
    outs, xo = _call(
        body, name, (T // tm,),
        [pl.BlockSpec((tm, S * n), lambda i: (i, 0)), pl.BlockSpec((S, K, n), lambda i: (0, 0, 0)), row, vec, row],
        out_specs, out_shape, [dy, w, h, gain, dres], ("arbitrary",), hosted=hosted)
    return (*outs, xo)


def tn_grad(a, dy, S, a_by_seg, name, hosted=()):
    T = dy.shape[0]
    tt = _tile(T, GRAD_TOKEN_TILE)
    G = GRAD_SEGS_PER_STEP
    if a_by_seg:
        R, C = a.shape[2], dy.shape[1]
        a_spec = pl.BlockSpec((G, tt, R), lambda p, t: (p, t, 0))
        b_spec = pl.BlockSpec((tt, C), lambda p, t: (t, 0))
    else:
        R, C = a.shape[1], dy.shape[1] // S
        a_spec = pl.BlockSpec((tt, R), lambda p, t: (t, 0))
        b_spec = pl.BlockSpec((tt, G * C), lambda p, t: (t, p))
    Rh = R // 2
    nt = T // tt

    def body(a_ref, b_ref, o_ref, acc_ref):
        t = pl.program_id(1)
        parts = []
        for j in range(G):
            a_j = a_ref[j] if a_by_seg else a_ref[...]
            b_j = b_ref[...] if a_by_seg else b_ref[:, j * C:(j + 1) * C]
            parts.append(lax.dot_general(a_j, b_j.astype(BF16), _TN, preferred_element_type=F32))

        @pl.when(t == 0)
        def _():
            for j in range(G):
                acc_ref[j] = parts[j]

        @pl.when(t > 0)
        def _():
            for j in range(G):
                acc_ref[j] += parts[j]

        @pl.when(t == nt - 1)
        def _():
            for j in range(G):
                o_ref[0, j] = acc_ref[j, :Rh, :].astype(o_ref.dtype)
                o_ref[1, j] = acc_ref[j, Rh:, :].astype(o_ref.dtype)

    (gh,), xo = _call(
        body, name, (S // G, nt), [a_spec, b_spec], [pl.BlockSpec((2, G, Rh, C), lambda p, t: (0, p, 0, 0))],
        [jax.ShapeDtypeStruct((2, S, Rh, C), BF16)], [a, dy], ("parallel", "arbitrary"), [pltpu.VMEM((G, R, C), F32)],
        hosted=hosted)
    return gh, xo


def tn_grad_square(a, dy, S, name, hosted=()):
    T, K = a.shape
    N = dy.shape[1]
    tt = _tile(T, GRAD_TOKEN_TILE)
    nt = T // tt
    Rh = K // S // 2

    def body(a_ref, b_ref, o_ref, acc_ref):
        t = pl.program_id(0)
        part = lax.dot_general(a_ref[...], b_ref[...].astype(BF16), _TN, preferred_element_type=F32)

        @pl.when(t == 0)
        def _():
            acc_ref[...] = part

        @pl.when(t > 0)
        def _():
            acc_ref[...] += part

        @pl.when(t == nt - 1)
        def _():
            for s in range(S):
                for hf in range(2):
                    r0 = (2 * s + hf) * Rh
                    o_ref[hf, s] = acc_ref[r0:r0 + Rh, :].astype(o_ref.dtype)

    (gh,), xo = _call(
        body, name, (nt,), [pl.BlockSpec((tt, K), lambda t: (t, 0)), pl.BlockSpec((tt, N), lambda t: (t, 0))],
        [pl.BlockSpec((2, S, Rh, N), lambda t: (0, 0, 0, 0))], [jax.ShapeDtypeStruct((2, S, Rh, N), BF16)],
        [a, dy], ("arbitrary",), [pltpu.VMEM((K, N), F32)], hosted=hosted)
    return gh, xo


def _place():
    x, y, c = lax.axis_index("x"), lax.axis_index("y"), lax.axis_index("c")
    chips = [(1 - x, y), (x, 1 - y), (1 - x, 1 - y)]
    return x, y, c, chips


def _remote(src, dst, send_sem, recv_sem, dev):
    return pltpu.make_async_remote_copy(src_ref=src, dst_ref=dst, send_sem=send_sem, recv_sem=recv_sem,
                                        device_id=dev, device_id_type=MESH)


def small_allreduce(v, name, hosted=()):
    rows, W = v.shape

    def body(v_ref, o_ref, sib_ref, pair_ref, chips_ref, send_sems, recv_sems):
        x, y, c, chips = _place()
        me = 2 * x + y
        swap = _remote(v_ref, sib_ref, send_sems.at[3], recv_sems.at[3], (x, y, 1 - c))
        swap.start()
        swap.wait()
        mine, other = v_ref[...], sib_ref[...]
        pair_ref[...] = jnp.where(c == 0, mine, other) + jnp.where(c == 0, other, mine)
        sends = []
        for j, (px, py) in enumerate(chips):
            cp = _remote(pair_ref, chips_ref.at[me], send_sems.at[j], recv_sems.at[j], (px, py, c))
            cp.start()
            sends.append(cp)
        chips_ref[me] = pair_ref[...]
        for j, (px, py) in enumerate(chips):
            blk = chips_ref.at[2 * px + py]
            _remote(blk, blk, send_sems.at[j], recv_sems.at[j], (px, py, c)).wait_recv()
        for cp in sends:
            cp.wait_send()
        o_ref[...] = (chips_ref[0] + chips_ref[1]) + (chips_ref[2] + chips_ref[3])

    vm = pl.BlockSpec(memory_space=pltpu.VMEM)
    (out,), xo = _call(
        body, name, (), [vm], [vm], [jax.ShapeDtypeStruct((rows, W), F32)], [v], (),
        [pltpu.VMEM((rows, W), F32), pltpu.VMEM((rows, W), F32), pltpu.VMEM((N_CHIPS, rows, W), F32),
         pltpu.SemaphoreType.DMA((4,)), pltpu.SemaphoreType.DMA((4,))], hosted=hosted)
    return out, xo


def _gather_p1_copies(srcs, bufs, ssem, rsem, base):
    x, y, c, chips = _place()
    me, sib = 2 * x + y, (x, y, 1 - c)
    sends, recvs = [], []
    for k, (src, buf) in enumerate(zip(srcs, bufs)):
        rh = src.shape[0] // 2
        s0 = base + 4 * k
        sends.append(_remote(src, buf.at[me], ssem.at[s0 + 3], rsem.at[s0 + 3], sib))
        recvs.append(_remote(buf.at[me], buf.at[me], ssem.at[s0 + 3], rsem.at[s0 + 3], sib))
        for j, (px, py) in enumerate(chips):
            sends.append(_remote(src.at[pl.ds(c * rh, rh)], buf.at[me, pl.ds(c * rh, rh)], ssem.at[s0 + j], rsem.at[s0 + j], (px, py, c)))
            blk = buf.at[2 * px + py, pl.ds(c * rh, rh)]
            recvs.append(_remote(blk, blk, ssem.at[s0 + j], rsem.at[s0 + j], (px, py, c)))
    return sends, recvs


def _gather_p2_copies(bufs, ssem, rsem, base):
    x, y, c, chips = _place()
    sib = (x, y, 1 - c)
    sends, recvs = [], []
    for k, buf in enumerate(bufs):
        rh = buf.shape[1] // 2
        for j, (px, py) in enumerate(chips):
            s0 = base + 3 * k + j
            blk = buf.at[2 * px + py, pl.ds(c * rh, rh)]
            sends.append(_remote(blk, blk, ssem.at[s0], rsem.at[s0], sib))
            got = buf.at[2 * px + py, pl.ds((1 - c) * rh, rh)]
            recvs.append(_remote(got, got, ssem.at[s0], rsem.at[s0], sib))
    return sends, recvs


def _gathered_shape(s):
    return jax.ShapeDtypeStruct((N_CHIPS,) + s.shape, s.dtype)


def gather_p1(shards):
    return _Exchange(shards, [_gathered_shape(s) for s in shards], {}, 4 * len(shards),
                     lambda xi, xo, ss, rs: _gather_p1_copies(xi, xo, ss, rs, 0))


def gather_p2(bufs):
    return _Exchange(bufs, [jax.ShapeDtypeStruct(b.shape, b.dtype) for b in bufs], {k: k for k in range(len(bufs))},
                     3 * len(bufs), lambda xi, xo, ss, rs: _gather_p2_copies(xo, ss, rs, 0))


def gather_whole(whole, begun):
    nw, n = len(whole), len(whole) + len(begun)
    shards = list(whole) + list(begun)
    return _Exchange(shards, [_gathered_shape(s) for s in shards], {}, 4 * n + 3 * nw,
                     lambda xi, xo, ss, rs: _gather_p1_copies(xi, xo, ss, rs, 0),
                     then=lambda xi, xo, ss, rs: _gather_p2_copies(xo[:nw], ss, rs, 4 * n))


def gather_small(v):
    def copies(xi, xo, ssem, rsem):
        x, y, c, chips = _place()
        me, sib = 2 * x + y, (x, y, 1 - c)
        sends = [_remote(xi[0], xo[0].at[me], ssem.at[3], rsem.at[3], sib)]
        recvs = [_remote(xo[0].at[me], xo[0].at[me], ssem.at[3], rsem.at[3], sib)]
        for j, (px, py) in enumerate(chips):
            sends.append(_remote(xi[0], xo[0].at[me], ssem.at[j], rsem.at[j], (px, py, c)))
            blk = xo[0].at[2 * px + py]
            recvs.append(_remote(blk, blk, ssem.at[j], rsem.at[j], (px, py, c)))
        return sends, recvs

    return _Exchange([v], [_gathered_shape(v)], {}, 4, copies)


def gather_all(v):
    def copies(xi, xo, ssem, rsem):
        x, y, c, _ = _place()
        sends, recvs = [], []
        for m in range(1, N_DEV):
            px, py, pc = (1 - x) if m & 4 else x, (1 - y) if m & 2 else y, (1 - c) if m & 1 else c
            sends.append(_remote(xi[0], xo[0].at[4 * x + 2 * y + c], ssem.at[m - 1], rsem.at[m - 1], (px, py, pc)))
            blk = xo[0].at[4 * px + 2 * py + pc]
            recvs.append(_remote(blk, blk, ssem.at[m - 1], rsem.at[m - 1], (px, py, pc)))
        return sends, recvs

    return _Exchange([v], [jax.ShapeDtypeStruct((N_DEV,) + v.shape, v.dtype)], {}, N_DEV - 1, copies)


def run_exchanges(exchanges, name):
    return _call(lambda: None, name, (), [], [], [], [], (), hosted=exchanges)[1]


def sibling_halves(grads):
    def copies(xi, xo, ssem, rsem):
        x, y, c, _ = _place()
        sends = [_remote(xi[k].at[1 - c], xo[k], ssem.at[k], rsem.at[k], (x, y, 1 - c)) for k in range(len(grads))]
        return sends, sends

    return _Exchange(grads, [jax.ShapeDtypeStruct(g.shape[1:], g.dtype) for g in grads], {}, len(grads), copies)


def pair_sum(ghs, recvs, cidx, name):
    n = len(ghs)
    S = ghs[0].shape[1]

    def body(c_ref, *refs):
        for k in range(n):
            a_ref, b_ref, o_ref = refs[2 * k], refs[2 * k + 1], refs[2 * n + k]
            o_ref[...] = (a_ref[...].astype(F32) + b_ref[...].astype(F32)).astype(o_ref.dtype)

    in_specs, out_specs, out_shape, args = [], [], [], []
    for gh, recv in zip(ghs, recvs):
        _, _, Rh, C = gh.shape
        in_specs += [pl.BlockSpec((None, None, Rh, C), lambda s, c_ref: (c_ref[0], s, 0, 0)),
                     pl.BlockSpec((None, Rh, C), lambda s, c_ref: (s, 0, 0))]
        out_specs.append(pl.BlockSpec((None, Rh, C), lambda s, c_ref: (s, 0, 0)))
        out_shape.append(jax.ShapeDtypeStruct((S, Rh, C), BF16))
        args += [gh, recv]
    return pl.pallas_call(
        body, name=name, out_shape=out_shape,
        grid_spec=pltpu.PrefetchScalarGridSpec(num_scalar_prefetch=1, grid=(S,), in_specs=in_specs, out_specs=out_specs),
        compiler_params=_params(("parallel",)),
    )(cidx, *args)


def scatter_p1(parts):
    def copies(xi, xo, ssem, rsem):
        x, y, c, chips = _place()
        me, sib = 2 * x + y, (x, y, 1 - c)
        sends, recvs = [], []
        for k in range(len(parts)):
            s0 = 4 * k
            sends.append(_remote(xi[k].at[me], xo[k].at[me, c], ssem.at[s0 + 3], rsem.at[s0 + 3], sib))
            own = xo[k].at[me, 1 - c]
            recvs.append(_remote(own, own, ssem.at[s0 + 3], rsem.at[s0 + 3], sib))
            for j, (px, py) in enumerate(chips):
                sends.append(_remote(xi[k].at[2 * px + py], xo[k].at[me, c], ssem.at[s0 + j], rsem.at[s0 + j], (px, py, c)))
                blk = xo[k].at[2 * px + py, c]
                recvs.append(_remote(blk, blk, ssem.at[s0 + j], rsem.at[s0 + j], (px, py, c)))
        return sends, recvs

    return _Exchange(parts, [jax.ShapeDtypeStruct((p.shape[0], 2) + p.shape[1:], p.dtype) for p in parts], {},
                     4 * len(parts), copies)


def scatter_p2(bufs):
    def copies(xi, xo, ssem, rsem):
        x, y, c, chips = _place()
        sib = (x, y, 1 - c)
        sends, recvs = [], []
        for k in range(len(bufs)):
            for j, (px, py) in enumerate(chips):
                s0 = 3 * k + j
                blk = xo[k].at[2 * px + py, c]
                sends.append(_remote(blk, blk, ssem.at[s0], rsem.at[s0], sib))
                got = xo[k].at[2 * px + py, 1 - c]
                recvs.append(_remote(got, got, ssem.at[s0], rsem.at[s0], sib))
        return sends, recvs

    return _Exchange(bufs, [jax.ShapeDtypeStruct(b.shape, b.dtype) for b in bufs], {k: k for k in range(len(bufs))},
                     3 * len(bufs), copies)


def _adamw_math(w, g, m, v):
    m = ADAM_B1 * m + (1.0 - ADAM_B1) * g
    v = ADAM_B2 * v + (1.0 - ADAM_B2) * (g * g)
    m_hat = m / (1.0 - ADAM_B1 ** ADAM_STEP)
    v_hat = v / (1.0 - ADAM_B2 ** ADAM_STEP)
    delta = -ADAM_LR * (m_hat / (jnp.sqrt(v_hat) + ADAM_EPS) + ADAM_WD * w)
    return delta, m, v


def adamw_reduce(tensors, place, lyr, bases, name):
    n = len(tensors)
    L, R, C = tensors[0][0].shape
    Rh = R // 2
    rb = _tile(Rh, ROW_TILE, 2 * SUBLANES)
    nb = Rh // rb

    def body(place_ref, *refs):
        mine = (place_ref[1] == pl.program_id(0))
        for k in range(n):
            p_ref, b0, b1, b2, b3, w_ref, m_ref, v_ref = refs[8 * k:8 * k + 8]
            go_ref, d_ref, mo_ref, vo_ref = refs[len(refs) - 4 * n + 4 * k:len(refs) - 4 * n + 4 * k + 4]
            g = None
            for p, b in enumerate((b0, b1, b2, b3)):
                val = jnp.where(mine & (place_ref[0] == p), p_ref[...], b[...]).astype(F32)
                g = val if g is None else g + val
            d, mn, vn = _adamw_math(w_ref[...], g, m_ref[...], v_ref[...])
            go_ref[...] = g
            d_ref[...] = d
            mo_ref[...] = mn
            vo_ref[...] = vn

    def buf_spec(p):
        def idx(h, i, pr):
            own = (pr[0] == p) & (pr[1] == h)
            return (p, jnp.where(own, 1 - h, h), i, 0)
        return pl.BlockSpec((None, None, rb, C), idx)

    blk = pl.BlockSpec((None, rb, C), lambda h, i, pr: (lyr, h * nb + i, 0))
    in_specs, args = [], []
    for w, m, v, buf, part in tensors:
        in_specs += [pl.BlockSpec((None, rb, C), lambda h, i, pr: (pr[0], i, 0))] + [buf_spec(p) for p in range(N_CHIPS)] + [blk] * 3
        args += [part, buf, buf, buf, buf, w, m, v]
    aliases = {}
    if bases is not None:
        in_specs += [pl.BlockSpec(memory_space=pl.ANY)] * (4 * n)
        aliases = {len(args) + k: k for k in range(4 * n)}
        args += list(bases)
    shp = jax.ShapeDtypeStruct((L, R, C), F32)
    flat = _call(body, name, (2, nb), in_specs, [blk] * (4 * n), [shp] * (4 * n), args, ("parallel", "parallel"),
                 prefetch=[place], own_aliases=aliases)[0]
    return flat


def small_update(late, early, own, place, entries, loss_row, name):
    ne = len(entries)
    D = late.shape[1]

    def body(place_ref, late_ref, early_ref, own_ref, *refs):
        ins, outs = refs[:3 * ne], refs[3 * ne:]
        ch = place_ref[0]
        me = 2 * place_ref[0] + place_ref[1]

        def early_sum(rs, cs):
            acc = None
            for d in range(N_DEV):
                val = jnp.where(me == d, own_ref[rs, cs], early_ref[d, rs, cs])
                acc = val if acc is None else acc + val
            return acc

        outs[4 * ne][...] = early_sum(slice(loss_row, loss_row + 1), slice(0, D))
        for e, (source, row0, kind, w, _, _) in enumerate(entries):
            r, width = w.shape
            gsum = early_sum if source == "early" else (lambda rs, cs: late_ref[rs, cs])

            if kind == "full":
                g = gsum(slice(row0, row0 + r), slice(0, D))
            elif kind == "cols":
                g = gsum(slice(row0, row0 + r), slice(0, width))
                for q in range(1, N_CHIPS):
                    g = jnp.where(ch == q, gsum(slice(row0, row0 + r), slice(q * width, (q + 1) * width)), g)
            else:
                per_row = D // width
                g = gsum(slice(row0, row0 + 1), slice(0, width))
                for q in range(1, N_CHIPS):
                    rr = row0 + q // per_row
                    cc = (q % per_row) * width
                    g = jnp.where(ch == q, gsum(slice(rr, rr + 1), slice(cc, cc + width)), g)
            d, mn, vn = _adamw_math(ins[3 * e][...], g, ins[3 * e + 1][...], ins[3 * e + 2][...])
            outs[4 * e][...] = g
            outs[4 * e + 1][...] = d
            outs[4 * e + 2][...] = mn
            outs[4 * e + 3][...] = vn

    vm = pl.BlockSpec(memory_space=pltpu.VMEM)
    args, out_shape = [], []
    for _, _, _, w, m, v in entries:
        args += [w, m, v]
        out_shape += [jax.ShapeDtypeStruct(w.shape, F32)] * 4
    out_shape.append(jax.ShapeDtypeStruct((1, D), F32))
    return pl.pallas_call(
        body, name=name,
        in_specs=[pl.BlockSpec(memory_space=pltpu.SMEM), vm, vm, vm] + [vm] * (3 * ne),
        out_specs=[vm] * (4 * ne + 1), out_shape=out_shape,
        compiler_params=pltpu.CompilerParams(vmem_limit_bytes=VMEM_LIMIT),
    )(place, late, early, own, *args)


def _pack_rows(items, width, name):
    starts, at = [], 0
    for it in items:
        starts.append(at)
        at += -(-it.shape[0] // SUBLANES) * SUBLANES
    total = at

    def body(*refs):
        o_ref = refs[-1]
        o_ref[...] = jnp.zeros_like(o_ref)
        for it_ref, r0 in zip(refs[:-1], starts):
            o_ref[r0:r0 + it_ref.shape[0], :] = it_ref[...]

    vm = pl.BlockSpec(memory_space=pltpu.VMEM)
    packed = pl.pallas_call(body, name=name, in_specs=[vm] * len(items), out_specs=vm,
                            out_shape=jax.ShapeDtypeStruct((total, width), F32))(*items)
    return packed, starts


def kernel(x, a_norm, a_w_in, a_conv, a_w_out, b_norm, b_w_pw1, b_b_pw1, b_conv, b_b_conv, b_ln_g, b_ln_b, b_w_pw2, b_b_pw2, ffn_norm, ffn_w_gate, ffn_w_up, ffn_w_down, final_norm, loss_target, m_a_norm, m_a_w_in, m_a_conv, m_a_w_out, m_b_norm, m_b_w_pw1, m_b_b_pw1, m_b_conv, m_b_b_conv, m_b_ln_g, m_b_ln_b, m_b_w_pw2, m_b_b_pw2, m_ffn_norm, m_ffn_w_gate, m_ffn_w_up, m_ffn_w_down, m_final_norm, v_a_norm, v_a_w_in, v_a_conv, v_a_w_out, v_b_norm, v_b_w_pw1, v_b_b_pw1, v_b_conv, v_b_b_conv, v_b_ln_g, v_b_ln_b, v_b_w_pw2, v_b_b_pw2, v_ffn_norm, v_ffn_w_gate, v_ffn_w_up, v_ffn_w_down, v_final_norm):
    T, D = x.shape[1], x.shape[2]
    Dq = D // N_CHIPS
    cx, cy, cc = lax.axis_index("x"), lax.axis_index("y"), lax.axis_index("c")
    chip = (2 * cx + cy).astype(jnp.int32).reshape(1)
    cidx = cc.astype(jnp.int32).reshape(1)
    h0 = x.reshape(T, D)
    tgt = loss_target.reshape(T, D)

    small_shards = [a_conv[0], b_norm, b_b_pw1.reshape(2, Dq), b_conv[0], b_b_conv, b_ln_g, b_ln_b, b_b_pw2]
    packed, st = _pack_rows(small_shards, Dq, "pack_small")

    tr = lambda t: jnp.swapaxes(t, 1, 2)
    w_gate, m_gate, v_gate = tr(ffn_w_gate), tr(m_ffn_w_gate), tr(v_ffn_w_gate)
    w_up, m_up, v_up = tr(ffn_w_up), tr(m_ffn_w_up), tr(v_ffn_w_up)
    bf = lambda t: t.astype(BF16)
    s_in, s_out, s_pw1, s_pw2 = bf(a_w_in[0]), bf(a_w_out[0]), bf(b_w_pw1[0]), bf(b_w_pw2[0])
    s_gate, s_up, s_down = [bf(w_gate[l]) for l in (0, 1)], [bf(w_up[l]) for l in (0, 1)], [bf(ffn_w_down[l]) for l in (0, 1)]

    n0, (g_in,) = rms_fwd(h0, a_norm, "rms_a", hosted=[gather_whole([s_in], [])])
    bcv, (g_out, gate0, sw) = mm_cols(n0, g_in, "mm_w_in", hosted=[gather_p1([s_out, s_gate[0]]), gather_small(packed)])

    def whole(k, r):
        return jnp.transpose(sw[:, st[k]:st[k] + r, :], (1, 0, 2)).reshape(r, D)

    a_conv_f, b_norm_f = whole(0, 3), whole(1, 1)
    b_b_pw1_f = sw[:, st[2]:st[2] + 2, :].reshape(1, 2 * D)
    b_conv_f, b_b_conv_f, b_ln_g_f, b_ln_b_f, b_b_pw2_f = whole(3, b_conv.shape[1]), whole(4, 1), whole(5, 1), whole(6, 1), whole(7, 1)
    ya, h1, (g_out, up0, down0, gate0) = gateconv_fwd(bcv, a_conv_f, gather_p2([g_out]), h0, "gateconv_fwd",
                                                      hosted=[gather_p1([s_up[0], s_down[0]]), gather_p2([gate0])])
    g_out = g_out.reshape(1, D, D)
    n1, fg0, fu0, gu0, h2, (up0, down0, g_pw1, g_pw2, gate1, up1) = ffn_fwd(
        h1, ffn_norm[0:1], [gate0], "ffn_fwd0", arriving=gather_p2([up0, down0]),
        hosted=[gather_whole([s_pw1, s_pw2], [s_gate[1], s_up[1]])])
    g_pw2 = g_pw2.reshape(1, D, D)
    n2, ub, (down1, gate1, up1) = rms_mm_cols(h2, b_norm_f, g_pw1, b_b_pw1_f, "mm_pw1",
                                              hosted=[gather_p1([s_down[1]]), gather_p2([gate1, up1])])
    cu, sb, h3, (down1,) = bconv_fwd(ub, b_conv_f, b_b_conv_f, b_ln_g_f, b_ln_b_f, g_pw2, b_b_pw2_f, h2, "bconv_fwd",
                                     hosted=[gather_p2([down1])])
    n3, fg1, fu1, gu1, h4, _ = ffn_fwd(h3, ffn_norm[1:2], [gate1, up1, down1], "ffn_fwd1")
    loss_part, dh4, dh4_b, d_final = loss_head(h4, final_norm.reshape(1, D), tgt, "loss_head")

    place = jnp.concatenate([chip, cidx])

    def pair_sums(ghs, from_sib, tags):
        return pair_sum(ghs, from_sib, cidx, "pair_sum_" + "_".join(tags))

    def upd(wmvs, bufs, parts, tag):
        flat = None
        for lyr in range(len(bufs[0])):
            tensors = [(w, m, v, b[lyr], p[lyr]) for (w, m, v), b, p in zip(wmvs, bufs, parts)]
            flat = adamw_reduce(tensors, place, lyr, flat, "adamw_%s%d" % (tag, lyr))
        return [flat[4 * k:4 * k + 4] for k in range(len(wmvs))]

    dg1, du1, dh3, dh3_b, d_fn1, _ = ffn_bwd(dh4, h3, ffn_norm[1:2], fg1, fu1, down1, gate1, up1, "ffn_bwd1")
    gh_down1, _ = tn_grad(gu1, dh4_b, N_CHIPS, True, "tn_down1")
    gh_gate1, _ = tn_grad(dg1, n3, N_CHIPS, True, "tn_gate1")
    gh_up1, _ = tn_grad(du1, n3, N_CHIPS, True, "tn_up1")
    f1 = [gh_gate1, gh_up1, gh_down1]

    dcu, d_ln_g, d_ln_b, d_b_conv, d_b_pw2, sib_f1 = pw2_ln_bwd(dh3, g_pw2, cu, b_ln_g_f, b_ln_b_f, "pw2_ln_bwd",
                                                                hosted=[sibling_halves(f1)])
    p_f1 = pair_sums(f1, sib_f1, ["gate1", "up1", "down1"])
    gh_pw2, _ = tn_grad_square(sb, dh3_b, N_CHIPS, "tn_pw2")
    dub, d_bconv_w, d_b_pw1, buf_f1 = bconv_bwd(dcu, ub, b_conv_f, "bconv_bwd", hosted=[scatter_p1(p_f1)])
    gh_pw1, _ = tn_grad(n2, dub, N_CHIPS, False, "tn_pw1")
    b_grp = [gh_pw1, gh_pw2]
    dh2, d_b_norm, dh2_b, (*buf_f1, sib_pw1, sib_pw2) = nt_cols_rms(
        dub, g_pw1, h2, b_norm_f, dh3, "nt_pw1", hosted=[scatter_p2(buf_f1), sibling_halves(b_grp)], also_bf16=True)
    sib_b = [sib_pw1, sib_pw2]
    p_b = pair_sums(b_grp, sib_b, ["pw1", "pw2"])

    early_grads = [d_b_norm, d_b_pw1.reshape(2, D), d_bconv_w, d_b_conv, d_ln_g, d_ln_b, d_b_pw2, d_fn1, d_final,
                   jnp.broadcast_to(loss_part, (1, D))]
    epacked, es = _pack_rows(early_grads, D, "pack_small_grads_early")
    dg0, du0, dh1, dh1_b, d_fn0, (*buf_b, eall) = ffn_bwd(dh2, h1, ffn_norm[0:1], fg0, fu0, down0, gate0, up0, "ffn_bwd0",
                                                         hosted=[scatter_p1(p_b), gather_all(epacked)])
    gh_down0, _ = tn_grad(gu0, dh2_b, N_CHIPS, True, "tn_down0")
    gh_gate0, (*buf_b, sib_down0) = tn_grad(dg0, n1, N_CHIPS, True, "tn_gate0",
                                            hosted=[scatter_p2(buf_b), sibling_halves([gh_down0])])
    p_down0 = pair_sums([gh_down0], [sib_down0], ["down0"])
    gh_up0, (buf_down0, sib_gate0) = tn_grad(du0, n1, N_CHIPS, True, "tn_up0",
                                             hosted=[scatter_p1(p_down0), sibling_halves([gh_gate0])])
    p_gate0 = pair_sums([gh_gate0], [sib_gate0], ["gate0"])
    dya, (buf_down0, sib_up0) = nt_rows(dh1, g_out, "nt_w_out",
                                        hosted=[scatter_p2([buf_down0]), sibling_halves([gh_up0])])
    p_up0 = pair_sums([gh_up0], [sib_up0], ["up0"])
    gh_out, _ = tn_grad_square(ya, dh1_b, N_CHIPS, "tn_w_out")
    dbcv, d_aconv_w, (buf_gate0, sib_out) = gateconv_bwd(dya[0], bcv, a_conv_f, "gateconv_bwd",
                                                         hosted=[scatter_p1(p_gate0), sibling_halves([gh_out])])
    p_out = pair_sums([gh_out], [sib_out], ["out"])
    gh_in, (buf_up0, buf_out, buf_gate0) = tn_grad(n0, dbcv, N_CHIPS, False, "tn_w_in",
                                                   hosted=[scatter_p1(p_up0 + p_out), scatter_p2([buf_gate0])])
    sib_in = run_exchanges([sibling_halves([gh_in])], "reduce_in_siblings")
    p_in = pair_sums([gh_in], sib_in, ["in"])
    grad_x, d_a_norm, (buf_in, buf_up0, buf_out) = nt_cols_rms(
        dbcv, g_in, h0, a_norm, dh1, "nt_w_in", hosted=[scatter_p1(p_in), scatter_p2([buf_up0, buf_out])])
    p_f0 = [p_gate0[0], p_up0[0], p_down0[0]]

    lpacked, ls = _pack_rows([d_a_norm, d_aconv_w, d_fn0], D, "pack_small_grads_late")
    lall, (buf_in,) = small_allreduce(lpacked, "allreduce_small_grads", hosted=[scatter_p2([buf_in])])
    buf_a, p_a = [buf_in, buf_out], [p_in[0], p_out[0]]

    r_gate, r_up = upd([(w_gate, m_gate, v_gate), (w_up, m_up, v_up)],
                       [[buf_gate0, buf_f1[0]], [buf_up0, buf_f1[1]]], [[p_f0[0], p_f1[0]], [p_f0[1], p_f1[1]]], "gate_up")
    (r_down,) = upd([(ffn_w_down, m_ffn_w_down, v_ffn_w_down)], [[buf_down0, buf_f1[2]]], [[p_f0[2], p_f1[2]]], "down")
    r_gate, r_up = [tr(t) for t in r_gate], [tr(t) for t in r_up]
    (r_pw1,) = upd([(b_w_pw1, m_b_w_pw1, v_b_w_pw1)], [[buf_b[0]]], [[p_b[0]]], "pw1")
    r_pw2, r_out = upd([(b_w_pw2, m_b_w_pw2, v_b_w_pw2), (a_w_out, m_a_w_out, v_a_w_out)],
                       [[buf_b[1]], [buf_a[1]]], [[p_b[1]], [p_a[1]]], "pw2_out")
    (r_in,) = upd([(a_w_in, m_a_w_in, v_a_w_in)], [[buf_a[0]]], [[p_a[0]]], "w_in")
    entries = [
        ("late", ls[0], "full", a_norm, m_a_norm, v_a_norm),
        ("late", ls[1], "cols", a_conv[0], m_a_conv[0], v_a_conv[0]),
        ("early", es[0], "cols", b_norm, m_b_norm, v_b_norm),
        ("early", es[1], "flat2", b_b_pw1, m_b_b_pw1, v_b_b_pw1),
        ("early", es[2], "cols", b_conv[0], m_b_conv[0], v_b_conv[0]),
        ("early", es[3], "cols", b_b_conv, m_b_b_conv, v_b_b_conv),
        ("early", es[4], "cols", b_ln_g, m_b_ln_g, v_b_ln_g),
        ("early", es[5], "cols", b_ln_b, m_b_ln_b, v_b_ln_b),
        ("early", es[6], "cols", b_b_pw2, m_b_b_pw2, v_b_b_pw2),
        ("late", ls[2], "full", ffn_norm[0:1], m_ffn_norm[0:1], v_ffn_norm[0:1]),
        ("early", es[7], "full", ffn_norm[1:2], m_ffn_norm[1:2], v_ffn_norm[1:2]),
        ("early", es[8], "full", final_norm.reshape(1, D), m_final_norm.reshape(1, D), v_final_norm.reshape(1, D)),
    ]
    so = small_update(lall, eall, epacked, place, entries, es[9], "small_update")
    sm = [so[4 * e:4 * e + 4] for e in range(len(entries))]

    def shaped(e, like):
        return [t.reshape(like.shape) for t in sm[e]]

    r_a_norm, r_a_conv, r_b_norm, r_b_b_pw1 = shaped(0, a_norm), shaped(1, a_conv), shaped(2, b_norm), shaped(3, b_b_pw1)
    r_b_conv, r_b_b_conv, r_b_ln_g, r_b_ln_b = shaped(4, b_conv), shaped(5, b_b_conv), shaped(6, b_ln_g), shaped(7, b_ln_b)
    r_b_b_pw2, r_final = shaped(8, b_b_pw2), shaped(11, final_norm)
    r_ffn_norm = [jnp.concatenate([l0, l1], axis=0) for l0, l1 in zip(sm[9], sm[10])]

    loss = so[4 * len(entries)][0, 0]
    order =[r_a_norm, r_in, r_a_conv, r_out, r_b_norm, r_pw1, r_b_b_pw1, r_b_conv, r_b_b_conv, r_b_ln_g, r_b_ln_b,
             r_pw2, r_b_b_pw2, r_ffn_norm, r_gate, r_up, r_down, r_final]
    outs = [loss, grad_x.reshape(x.shape)]
    for field in range(4):
        outs += [r[field] for r in order]
    return tuple(outs)
```

```python
import functools

import jax
import jax.numpy as jnp
from jax import lax
from jax.experimental import pallas as pl
from jax.experimental.pallas import tpu as pltpu

RMS_EPS = 1e-6
LN_EPS = 1e-5
ADAM_LR = 0.001
ADAM_B1 = 0.9
ADAM_B2 = 0.999
ADAM_EPS = 1e-08
ADAM_WD = 0.01
ADAM_STEP = 10

N_CHIPS = 4
N_DEV = 8
LANES = 128
SUBLANES = 8
HALO = 32
CONV_ROWS = 64
TOKEN_TILE = 512
WIDE_TOKEN_TILE = 1024
GRAD_TOKEN_TILE = 2048
GRAD_SEGS_PER_STEP = 2
FFN_ROW_CHUNKS = 2
FFN_FWD_SEGS_PER_STEP = 4
FFN_BWD_TOKEN_TILE = 256
ROW_TILE = 256
VMEM_LIMIT = 56 * 1024 * 1024
MESH = pl.DeviceIdType.MESH
BF16 = jnp.bfloat16
F32 = jnp.float32


def _tile(n, pref, mult=SUBLANES):
    t = min(n, pref) // mult * mult
    while n % t:
        t -= mult
    return t


def _params(sem):
    return pltpu.CompilerParams(dimension_semantics=sem, vmem_limit_bytes=VMEM_LIMIT)


def _sigmoid(x):
    return 0.5 * jnp.tanh(0.5 * x) + 0.5


class _Exchange:
    def __init__(self, ins, outs, aliases, n_sems, copies, then=None):
        self.ins, self.outs, self.aliases, self.n_sems, self.copies = list(ins), list(outs), dict(aliases), n_sems, copies
        self.then = then
        self.early = False

    def awaited_first(self):
        self.early = True
        return self

    def start(self, xi, xo, ssem, rsem):
        for cp in self.copies(xi, xo, ssem, rsem)[0]:
            cp.start()

    def finish(self, xi, xo, ssem, rsem):
        sends, recvs = self.copies(xi, xo, ssem, rsem)
        for cp in recvs:
            cp.wait_recv()
        if self.then is not None:
            sends2, recvs2 = self.then(xi, xo, ssem, rsem)
            for cp in sends2:
                cp.start()
            for cp in recvs2:
                cp.wait_recv()
            sends = sends + sends2
        for cp in sends:
            cp.wait_send()


def _call(body, name, grid, in_specs, out_specs, out_shape, args, sem, scratch_shapes=(), hosted=(), prefetch=(),
          own_aliases=None):
    in_specs, out_specs, out_shape = list(in_specs), list(out_specs), list(out_shape)
    scratch_shapes, hosted, prefetch = list(scratch_shapes), list(hosted), list(prefetch)
    n_pre, n_in, n_out, n_scr = len(prefetch), len(args), len(out_shape), len(scratch_shapes)
    x_in = [a for ex in hosted for a in ex.ins]
    x_out = [o for ex in hosted for o in ex.outs]
    aliases = {n_pre + i: o for i, o in (own_aliases or {}).items()}
    at_in, at_out = n_pre + n_in, n_out
    for ex in hosted:
        for i, o in ex.aliases.items():
            aliases[at_in + i] = at_out + o
        at_in += len(ex.ins)
        at_out += len(ex.outs)
    sems = [pltpu.SemaphoreType.DMA((ex.n_sems,)) for ex in hosted for _ in range(2)]

    def wrapped(*refs):
        pre, refs = refs[:n_pre], refs[n_pre:]
        ins, xi = refs[:n_in], refs[n_in:n_in + len(x_in)]
        refs = refs[n_in + len(x_in):]
        outs, xo = refs[:n_out], refs[n_out:n_out + len(x_out)]
        refs = refs[n_out + len(x_out):]
        scr, sm = refs[:n_scr], refs[n_scr:]
        views, a, b = [], 0, 0
        for e, ex in enumerate(hosted):
            views.append((xi[a:a + len(ex.ins)], xo[b:b + len(ex.outs)], sm[2 * e], sm[2 * e + 1]))
            a += len(ex.ins)
            b += len(ex.outs)
        first = last = None
        for ax, g in enumerate(grid):
            f, l = pl.program_id(ax) == 0, pl.program_id(ax) == g - 1
            first, last = (f, l) if first is None else (first & f, last & l)

        def begin():
            for ex, v in zip(hosted, views):
                ex.start(*v)
            for ex, v in zip(hosted, views):
                if ex.early:
                    ex.finish(*v)

        def end():
            for ex, v in zip(hosted, views):
                if not ex.early:
                    ex.finish(*v)

        if hosted and grid:
            pl.when(first)(begin)
        elif hosted:
            begin()
        early_refs = [r for ex, v in zip(hosted, views) if ex.early for r in v[1]]
        body(*pre, *ins, *outs, *scr, *early_refs)
        if hosted and grid:
            pl.when(last)(end)
        elif hosted:
            end()

    hbm = pl.BlockSpec(memory_space=pl.ANY)
    all_in, all_out = in_specs + [hbm] * len(x_in), out_specs + [hbm] * len(x_out)
    kw = dict(name=name, out_shape=out_shape + x_out, input_output_aliases=aliases,
              compiler_params=_params(tuple("arbitrary" for _ in grid) if hosted else sem))
    if prefetch:
        kw["grid_spec"] = pltpu.PrefetchScalarGridSpec(num_scalar_prefetch=n_pre, grid=grid, in_specs=all_in,
                                                       out_specs=all_out, scratch_shapes=scratch_shapes + sems)
    else:
        kw.update(grid=grid, in_specs=all_in, out_specs=all_out, scratch_shapes=scratch_shapes + sems)
    res = pl.pallas_call(wrapped, **kw)(*prefetch, *args, *x_in)
    return list(res[:n_out]), list(res[n_out:])


def rms_fwd(h, gain, name, hosted=()):
    T, D = h.shape
    tm = _tile(T, TOKEN_TILE)

    def body(h_ref, g_ref, o_ref):
        x = h_ref[...]
        r = lax.rsqrt(jnp.mean(x * x, axis=-1, keepdims=True) + RMS_EPS)
        o_ref[...] = (x * r * g_ref[...]).astype(o_ref.dtype)

    (n,), xo = _call(
        body, name, (T // tm,),
        [pl.BlockSpec((tm, D), lambda i: (i, 0)), pl.BlockSpec((1, D), lambda i: (0, 0))],
        [pl.BlockSpec((tm, D), lambda i: (i, 0))], [jax.ShapeDtypeStruct((T, D), BF16)],
        [h, gain], ("parallel",), hosted=hosted)
    return n, xo


def loss_head(h, gain, tgt, name):
    T, D = h.shape
    tm = _tile(T, TOKEN_TILE)

    def body(h_ref, g_ref, t_ref, loss_ref, dh_ref, dhb_ref, dg_ref):
        i = pl.program_id(0)
        x = h_ref[...]
        g = g_ref[...]
        r = lax.rsqrt(jnp.mean(x * x, axis=-1, keepdims=True) + RMS_EPS)
        xhat = x * r
        diff = xhat * g - t_ref[...]
        part_loss = 0.5 * jnp.sum(jnp.mean(diff * diff, axis=-1, keepdims=True), axis=0, keepdims=True)
        dy = diff * (1.0 / D)
        dxhat = dy * g
        dh = r * (dxhat - xhat * jnp.mean(dxhat * xhat, axis=-1, keepdims=True))
        dh_ref[...] = dh
        dhb_ref[...] = dh.astype(dhb_ref.dtype)
        part = jnp.sum(dy * xhat, axis=0, keepdims=True)

        @pl.when(i == 0)
        def _():
            dg_ref[...] = part
            loss_ref[...] = part_loss

        @pl.when(i > 0)
        def _():
            dg_ref[...] += part
            loss_ref[...] += part_loss

    row = pl.BlockSpec((tm, D), lambda i: (i, 0))
    vec = pl.BlockSpec((1, D), lambda i: (0, 0))
    return pl.pallas_call(
        body, name=name, grid=(T // tm,),
        in_specs=[row, vec, row],
        out_specs=[pl.BlockSpec((1, 1), lambda i: (0, 0)), row, row, vec],
        out_shape=[jax.ShapeDtypeStruct((1, 1), F32), jax.ShapeDtypeStruct((T, D), F32),
                   jax.ShapeDtypeStruct((T, D), BF16), jax.ShapeDtypeStruct((1, D), F32)],
        compiler_params=_params(("arbitrary",)),
    )(h, gain, tgt)


def _prev_halo_spec(tm, width):
    return pl.BlockSpec((HALO, width), lambda i: (jnp.maximum(i * (tm // HALO) - 1, 0), 0))


def _next_halo_spec(tm, width, T):
    return pl.BlockSpec((HALO, width), lambda i: (jnp.minimum((i + 1) * (tm // HALO), T // HALO - 1), 0))


def _shifted(win, off, rows):
    if off % SUBLANES == 0:
        return win[off:off + rows]
    n = win.shape[0]
    return pltpu.roll(win, (n - off) % n, 0)[:rows]


def _rowsum8(x):
    acc = x[0:SUBLANES]
    for q in range(1, x.shape[0] // SUBLANES):
        acc = acc + x[q * SUBLANES:(q + 1) * SUBLANES]
    return acc


def _conv_loops(tm, D, per_block):
    def chunk(r, carry):
        t0 = pl.multiple_of(r * CONV_ROWS, CONV_ROWS)
        for lb in range(D // LANES):
            per_block(t0, slice(lb * LANES, (lb + 1) * LANES))
        return carry

    lax.fori_loop(0, tm // CONV_ROWS, chunk, 0)


def gateconv_fwd(bcv, w, w_out, res, name, hosted=()):
    T, D3 = bcv.shape
    D = D3 // 3
    K = w.shape[0]
    tm = _tile(T, TOKEN_TILE)
    wo_shape = w_out.outs[0].shape

    def body(x_ref, halo_ref, w_ref, res_ref, y_ref, h_ref, pad_ref, wo_v, sem, wo_hbm):
        i = pl.program_id(0)

        @pl.when(i == 0)
        def _():
            cp = pltpu.make_async_copy(wo_hbm, wo_v, sem)
            cp.start()
            cp.wait()

        pad_ref[HALO:, :] = x_ref[:, D:2 * D] * x_ref[:, 2 * D:]
        pad_ref[:HALO, :] = jnp.where(i > 0, halo_ref[:, D:2 * D] * halo_ref[:, 2 * D:], 0.0)

        def block(t0, ls):
            win = pad_ref[pl.ds(t0, CONV_ROWS + HALO), ls]
            acc = jnp.zeros((CONV_ROWS, LANES), F32)
            for k in range(K):
                acc = acc + w_ref[k:k + 1, ls] * _shifted(win, HALO - (K - 1) + k, CONV_ROWS)
            y_ref[pl.ds(t0, CONV_ROWS), ls] = (x_ref[pl.ds(t0, CONV_ROWS), ls] * acc).astype(y_ref.dtype)

        _conv_loops(tm, D, block)
        h_ref[...] = res_ref[...] + jnp.dot(y_ref[...], wo_v[...].reshape(D, D), preferred_element_type=F32)

    row = pl.BlockSpec((tm, D), lambda i: (i, 0))
    (y, h), xo = _call(
        body, name, (T // tm,),
        [pl.BlockSpec((tm, D3), lambda i: (i, 0)), _prev_halo_spec(tm, D3), pl.BlockSpec((K, D), lambda i: (0, 0)), row],
        [row, row], [jax.ShapeDtypeStruct((T, D), BF16), jax.ShapeDtypeStruct((T, D), F32)],
        [bcv, bcv, w, res], ("arbitrary",),
        [pltpu.VMEM((tm + HALO, D), F32), pltpu.VMEM(wo_shape, BF16), pltpu.SemaphoreType.DMA],
        hosted=[w_out.awaited_first()] + list(hosted))
    return y, h, xo


def gateconv_bwd(dh, w_out, bcv, w, name, hosted=()):
    T, D3 = bcv.shape
    D = D3 // 3
    K = w.shape[0]
    tm = _tile(T, TOKEN_TILE)
    nt = T // tm

    def body(dh_ref, dhn_ref, wo_ref, x_ref, xp_ref, xn_ref, w_ref, o_ref, dw_ref, cv_ref, dc_ref, wacc_ref, dy_ref):
        i = pl.program_id(0)
        dy_ref[...] = lax.dot_general(dh_ref[...], wo_ref[0], _NT, preferred_element_type=F32)
        dyn = lax.dot_general(dhn_ref[...], wo_ref[0], _NT, preferred_element_type=F32)
        cv_ref[HALO:, :] = x_ref[:, D:2 * D] * x_ref[:, 2 * D:]
        cv_ref[:HALO, :] = jnp.where(i > 0, xp_ref[:, D:2 * D] * xp_ref[:, 2 * D:], 0.0)
        dc_ref[:tm, :] = dy_ref[...] * x_ref[:, :D]
        dc_ref[tm:, :] = jnp.where(i < nt - 1, dyn * xn_ref[:, :D], 0.0)

        @pl.when(i == 0)
        def _():
            wacc_ref[...] = jnp.zeros_like(wacc_ref)

        def block(t0, ls):
            cwin = cv_ref[pl.ds(t0, CONV_ROWS + HALO), ls]
            dwin = dc_ref[pl.ds(t0, CONV_ROWS + HALO), ls]
            dcon = dwin[:CONV_ROWS]
            conv = jnp.zeros((CONV_ROWS, LANES), F32)
            dcv = jnp.zeros((CONV_ROWS, LANES), F32)
            for k in range(K):
                wk = w_ref[k:k + 1, ls]
                cs = _shifted(cwin, HALO - (K - 1) + k, CONV_ROWS)
                conv = conv + wk * cs
                dcv = dcv + wk * _shifted(dwin, (K - 1) - k, CONV_ROWS)
                wacc_ref[k * SUBLANES:(k + 1) * SUBLANES, ls] += _rowsum8(dcon * cs)
            rows = pl.ds(t0, CONV_ROWS)
            o_ref[rows, ls] = (dy_ref[rows, ls] * conv).astype(o_ref.dtype)
            o_ref[rows, pl.ds(D + ls.start, LANES)] = (dcv * x_ref[rows, pl.ds(2 * D + ls.start, LANES)]).astype(o_ref.dtype)
            o_ref[rows, pl.ds(2 * D + ls.start, LANES)] = (dcv * x_ref[rows, pl.ds(D + ls.start, LANES)]).astype(o_ref.dtype)

        _conv_loops(tm, D, block)

        @pl.when(i == nt - 1)
        def _():
            for k in range(K):
                dw_ref[k:k + 1, :] = jnp.sum(wacc_ref[k * SUBLANES:(k + 1) * SUBLANES, :], axis=0, keepdims=True)

    (dx, dw), xo = _call(
        body, name, (nt,),
        [pl.BlockSpec((tm, D), lambda i: (i, 0)), _next_halo_spec(tm, D, T), pl.BlockSpec((1, D, D), lambda i: (0, 0, 0)),
         pl.BlockSpec((tm, D3), lambda i: (i, 0)), _prev_halo_spec(tm, D3), _next_halo_spec(tm, D3, T),
         pl.BlockSpec((K, D), lambda i: (0, 0))],
        [pl.BlockSpec((tm, D3), lambda i: (i, 0)), pl.BlockSpec((K, D), lambda i: (0, 0))],
        [jax.ShapeDtypeStruct((T, D3), BF16), jax.ShapeDtypeStruct((K, D), F32)],
        [dh, dh, w_out, bcv, bcv, bcv, w], ("arbitrary",),
        [pltpu.VMEM((tm + HALO, D), F32), pltpu.VMEM((tm + HALO, D), F32), pltpu.VMEM((K * SUBLANES, D), F32),
         pltpu.VMEM((tm, D), F32)], hosted=hosted)
    return dx, dw, xo


def bconv_fwd(u, w, b_conv, ln_g, ln_b, w_out, b_out, res, name, hosted=()):
    T, D2 = u.shape
    D = D2 // 2
    K = w.shape[0]
    tm = _tile(T, TOKEN_TILE)

    def body(u_ref, halo_ref, w_ref, bc_ref, g_ref, b_ref, wo_ref, bo_ref, res_ref, cu_ref, s_ref, h_ref, pad_ref):
        i = pl.program_id(0)
        pad_ref[HALO:, :] = u_ref[:, :D] * _sigmoid(u_ref[:, D:])
        pad_ref[:HALO, :] = jnp.where(i > 0, halo_ref[:, :D] * _sigmoid(halo_ref[:, D:]), 0.0)

        def block(t0, ls):
            win = pad_ref[pl.ds(t0, CONV_ROWS + HALO), ls]
            acc = jnp.zeros((CONV_ROWS, LANES), F32)
            for k in range(K):
                acc = acc + w_ref[k:k + 1, ls] * _shifted(win, HALO - (K - 1) + k, CONV_ROWS)
            cu_ref[pl.ds(t0, CONV_ROWS), ls] = acc + bc_ref[:, ls]

        _conv_loops(tm, D, block)
        cu = cu_ref[...]
        mu = jnp.mean(cu, axis=-1, keepdims=True)
        xc = cu - mu
        rstd = lax.rsqrt(jnp.mean(xc * xc, axis=-1, keepdims=True) + LN_EPS)
        ln = xc * rstd * g_ref[...] + b_ref[...]
        s = (ln * _sigmoid(ln)).astype(s_ref.dtype)
        s_ref[...] = s
        h_ref[...] = res_ref[...] + bo_ref[...] + jnp.dot(s, wo_ref[0], preferred_element_type=F32)

    vec = pl.BlockSpec((1, D), lambda i: (0, 0))
    row = pl.BlockSpec((tm, D), lambda i: (i, 0))
    (cu, s, h), xo = _call(
        body, name, (T // tm,),
        [pl.BlockSpec((tm, D2), lambda i: (i, 0)), _prev_halo_spec(tm, D2), pl.BlockSpec((K, D), lambda i: (0, 0)), vec, vec, vec,
         pl.BlockSpec((1, D, D), lambda i: (0, 0, 0)), vec, row],
        [row, row, row], [jax.ShapeDtypeStruct((T, D), F32), jax.ShapeDtypeStruct((T, D), BF16), jax.ShapeDtypeStruct((T, D), F32)],
        [u, u, w, b_conv, ln_g, ln_b, w_out, b_out, res], ("parallel",), [pltpu.VMEM((tm + HALO, D), F32)], hosted=hosted)
    return cu, s, h, xo


def pw2_ln_bwd(dy, w, cu, ln_g, ln_b, name, hosted=()):
    T, D = cu.shape
    tm = _tile(T, TOKEN_TILE)

    def body(dy_ref, w_ref, cu_ref, g_ref, b_ref, dcu_ref, dg_ref, db_ref, dbc_ref, dbo_ref):
        i = pl.program_id(0)
        dy_ = dy_ref[...]
        ds = lax.dot_general(dy_.astype(BF16), w_ref[0], _NT, preferred_element_type=F32)
        cu_ = cu_ref[...]
        mu = jnp.mean(cu_, axis=-1, keepdims=True)
        xc = cu_ - mu
        rstd = lax.rsqrt(jnp.mean(xc * xc, axis=-1, keepdims=True) + LN_EPS)
        xh = xc * rstd
        ln = xh * g_ref[...] + b_ref[...]
        sg = _sigmoid(ln)
        dl = ds * (sg * (1.0 + ln * (1.0 - sg)))
        dxh = dl * g_ref[...]
        dcu = rstd * (dxh - jnp.mean(dxh, axis=-1, keepdims=True) - xh * jnp.mean(dxh * xh, axis=-1, keepdims=True))
        dcu_ref[...] = dcu
        pg = jnp.sum(dl * xh, axis=0, keepdims=True)
        pb = jnp.sum(dl, axis=0, keepdims=True)
        pc = jnp.sum(dcu, axis=0, keepdims=True)
        po = jnp.sum(dy_, axis=0, keepdims=True)

        @pl.when(i == 0)
        def _():
            dg_ref[...] = pg
            db_ref[...] = pb
            dbc_ref[...] = pc
            dbo_ref[...] = po

        @pl.when(i > 0)
        def _():
            dg_ref[...] += pg
            db_ref[...] += pb
            dbc_ref[...] += pc
            dbo_ref[...] += po

    vec = pl.BlockSpec((1, D), lambda i: (0, 0))
    row = pl.BlockSpec((tm, D), lambda i: (i, 0))
    vshape = jax.ShapeDtypeStruct((1, D), F32)
    outs, xo = _call(
        body, name, (T // tm,), [row, pl.BlockSpec((1, D, D), lambda i: (0, 0, 0)), row, vec, vec], [row, vec, vec, vec, vec],
        [jax.ShapeDtypeStruct((T, D), F32), vshape, vshape, vshape, vshape], [dy, w, cu, ln_g, ln_b], ("arbitrary",),
        hosted=hosted)
    return (*outs, xo)


def bconv_bwd(dcu, u, w, name, hosted=()):
    T, D2 = u.shape
    D = D2 // 2
    K = w.shape[0]
    tm = _tile(T, TOKEN_TILE)
    nt = T // tm

    def body(dc_ref, dcn_ref, u_ref, up_ref, w_ref, du_ref, dw_ref, db_ref, glu_ref, dpad_ref, dglu_ref, wacc_ref):
        i = pl.program_id(0)
        glu_ref[HALO:, :] = u_ref[:, :D] * _sigmoid(u_ref[:, D:])
        glu_ref[:HALO, :] = jnp.where(i > 0, up_ref[:, :D] * _sigmoid(up_ref[:, D:]), 0.0)
        dpad_ref[:tm, :] = dc_ref[...]
        dpad_ref[tm:, :] = jnp.where(i < nt - 1, dcn_ref[...], 0.0)

        @pl.when(i == 0)
        def _():
            wacc_ref[...] = jnp.zeros_like(wacc_ref)

        def block(t0, ls):
            gwin = glu_ref[pl.ds(t0, CONV_ROWS + HALO), ls]
            dwin = dpad_ref[pl.ds(t0, CONV_ROWS + HALO), ls]
            dcur = dwin[:CONV_ROWS]
            dglu = jnp.zeros((CONV_ROWS, LANES), F32)
            for k in range(K):
                dglu = dglu + w_ref[k:k + 1, ls] * _shifted(dwin, (K - 1) - k, CONV_ROWS)
                gs = _shifted(gwin, HALO - (K - 1) + k, CONV_ROWS)
                wacc_ref[k * SUBLANES:(k + 1) * SUBLANES, ls] += _rowsum8(dcur * gs)
            dglu_ref[pl.ds(t0, CONV_ROWS), ls] = dglu

        _conv_loops(tm, D, block)
        dglu = dglu_ref[...]
        a = u_ref[:, :D]
        sg = _sigmoid(u_ref[:, D:])
        da = dglu * sg
        dg = dglu * a * (sg * (1.0 - sg))
        du_ref[:, :D] = da.astype(du_ref.dtype)
        du_ref[:, D:] = dg.astype(du_ref.dtype)
        pa = jnp.sum(da, axis=0, keepdims=True)
        pg = jnp.sum(dg, axis=0, keepdims=True)

        @pl.when(i == 0)
        def _():
            db_ref[:, :D] = pa
            db_ref[:, D:] = pg

        @pl.when(i > 0)
        def _():
            db_ref[:, :D] += pa
            db_ref[:, D:] += pg

        @pl.when(i == nt - 1)
        def _():
            for k in range(K):
                dw_ref[k:k + 1, :] = jnp.sum(wacc_ref[k * SUBLANES:(k + 1) * SUBLANES, :], axis=0, keepdims=True)

    (du, dw, db), xo = _call(
        body, name, (nt,),
        [pl.BlockSpec((tm, D), lambda i: (i, 0)), _next_halo_spec(tm, D, T),
         pl.BlockSpec((tm, D2), lambda i: (i, 0)), _prev_halo_spec(tm, D2), pl.BlockSpec((K, D), lambda i: (0, 0))],
        [pl.BlockSpec((tm, D2), lambda i: (i, 0)), pl.BlockSpec((K, D), lambda i: (0, 0)), pl.BlockSpec((1, D2), lambda i: (0, 0))],
        [jax.ShapeDtypeStruct((T, D2), BF16), jax.ShapeDtypeStruct((K, D), F32), jax.ShapeDtypeStruct((1, D2), F32)],
        [dcu, dcu, u, u, w], ("arbitrary",),
        [pltpu.VMEM((tm + HALO, D), F32), pltpu.VMEM((tm + HALO, D), F32), pltpu.VMEM((tm, D), F32),
         pltpu.VMEM((K * SUBLANES, D), F32)], hosted=hosted)
    return du, dw, db, xo


def mm_cols(a, w, name, hosted=()):
    T, K = a.shape
    S, _, n = w.shape
    tm = _tile(T, WIDE_TOKEN_TILE)

    def body(a_ref, w_ref, o_ref):
        o_ref[...] = jnp.dot(a_ref[...], w_ref[...], preferred_element_type=F32)

    in_specs = [pl.BlockSpec((tm, K), lambda s, i: (i, 0)), pl.BlockSpec((None, K, n), lambda s, i: (s, 0, 0))]
    (out,), xo = _call(body, name, (S, T // tm), in_specs, [pl.BlockSpec((tm, n), lambda s, i: (i, s))],
                       [jax.ShapeDtypeStruct((T, S * n), F32)], [a, w], ("parallel", "parallel"), hosted=hosted)
    return out, xo


def rms_mm_cols(h, gain, w, bias, name, hosted=()):
    T, K = h.shape
    S, _, n = w.shape
    tm = _tile(T, TOKEN_TILE)

    def body(h_ref, gain_ref, w_ref, b_ref, n_ref, o_ref):
        x = h_ref[...]
        r = lax.rsqrt(jnp.mean(x * x, axis=-1, keepdims=True) + RMS_EPS)
        a = (x * r * gain_ref[...]).astype(n_ref.dtype)
        n_ref[...] = a
        for s in range(S):
            cols = slice(s * n, (s + 1) * n)
            o_ref[:, cols] = jnp.dot(a, w_ref[s], preferred_element_type=F32) + b_ref[:, cols]

    row = pl.BlockSpec((tm, K), lambda i: (i, 0))
    (n_out, out), xo = _call(
        body, name, (T // tm,),
        [row, pl.BlockSpec((1, K), lambda i: (0, 0)), pl.BlockSpec((S, K, n), lambda i: (0, 0, 0)),
         pl.BlockSpec((1, S * n), lambda i: (0, 0))],
        [row, pl.BlockSpec((tm, S * n), lambda i: (i, 0))],
        [jax.ShapeDtypeStruct((T, K), BF16), jax.ShapeDtypeStruct((T, S * n), F32)],
        [h, gain, w, bias], ("parallel",), hosted=hosted)
    return n_out, out, xo


def _load_weights(pairs, sems, S, G, i, p):
    def copies(seg):
        return [pltpu.make_async_copy(src.at[seg], dst.at[seg], sems.at[k, seg]) for k, (src, dst) in enumerate(pairs)]

    @pl.when((i == 0) & (p == 0))
    def _():
        for seg in range(S):
            for cp in copies(seg):
                cp.start()

    @pl.when((i == 0) & (p < S // G))
    def _():
        for j in range(G):
            for cp in copies(G * p + j):
                cp.wait()


def ffn_fwd(h, gain, weights, name, hosted=(), arriving=None):
    T, D = h.shape
    S, f, _ = weights[0].shape
    tm = _tile(T, TOKEN_TILE)
    rc = tm // FFN_ROW_CHUNKS
    chunks = [slice(r * rc, (r + 1) * rc) for r in range(FFN_ROW_CHUNKS)]
    G = FFN_FWD_SEGS_PER_STEP
    weights = list(weights)
    hosted = ([arriving.awaited_first()] if arriving is not None else []) + list(hosted)

    def body(h_ref, gain_ref, *refs):
        nw = len(weights)
        wg_hbm, wu_hbm, wd_hbm = list(refs[:nw]) + list(refs[nw + 9:])
        n_ref, g_ref, u_ref, gu_ref, o_ref, wg_v, wu_v, wd_v, sems = refs[nw:nw + 9]
        i, p = pl.program_id(0), pl.program_id(1)
        _load_weights([(wg_hbm, wg_v), (wu_hbm, wu_v), (wd_hbm, wd_v)], sems, S, G, i, p)

        @pl.when(p == 0)
        def _():
            x = h_ref[...]
            r = lax.rsqrt(jnp.mean(x * x, axis=-1, keepdims=True) + RMS_EPS)
            n_ref[...] = (x * r * gain_ref[...]).astype(n_ref.dtype)

        parts = []
        for rows in chunks:
            a = n_ref[rows, :]
            acc = None
            for j in range(G):
                seg = G * p + j
                g = lax.dot_general(a, wg_v[seg], _NT, preferred_element_type=F32)
                u = lax.dot_general(a, wu_v[seg], _NT, preferred_element_type=F32)
                gu = (g * _sigmoid(g) * u).astype(gu_ref.dtype)
                g_ref[j, rows, :] = g.astype(g_ref.dtype)
                u_ref[j, rows, :] = u.astype(u_ref.dtype)
                gu_ref[j, rows, :] = gu
                part = jnp.dot(gu, wd_v[seg], preferred_element_type=F32)
                acc = part if acc is None else acc + part
            parts.append(acc)

        @pl.when(p == 0)
        def _():
            for rows, part in zip(chunks, parts):
                o_ref[rows, :] = h_ref[rows, :] + part

        @pl.when(p > 0)
        def _():
            for rows, part in zip(chunks, parts):
                o_ref[rows, :] += part

    row = pl.BlockSpec((tm, D), lambda i, p: (i, 0))
    seg = pl.BlockSpec((G, tm, f), lambda i, p: (p, i, 0))
    hbm = pl.BlockSpec(memory_space=pl.ANY)
    segs = jax.ShapeDtypeStruct((S, T, f), BF16)
    outs, xo = _call(
        body, name, (T // tm, S // G),
        [row, pl.BlockSpec((1, D), lambda i, s: (0, 0))] + [hbm] * len(weights), [row, seg, seg, seg, row],
        [jax.ShapeDtypeStruct((T, D), BF16), segs, segs, segs, jax.ShapeDtypeStruct((T, D), F32)],
        [h, gain] + weights, ("arbitrary", "arbitrary"),
        [pltpu.VMEM((S, f, D), BF16), pltpu.VMEM((S, f, D), BF16), pltpu.VMEM((S, f, D), BF16), pltpu.SemaphoreType.DMA((3, S))],
        hosted=hosted)
    return (*outs, xo)


def ffn_bwd(dy, h, gain, g, u, wd, wg, wu, name, hosted=()):
    T, D = h.shape
    S, f, _ = wg.shape
    tm = _tile(T, FFN_BWD_TOKEN_TILE)
    nt = T // tm

    def body(dy_ref, h_ref, gain_ref, g_ref, u_ref, wd_hbm, wg_hbm, wu_hbm, dg_ref, du_ref, dh_ref, dhb_ref, dgain_ref,
             wd_v, wg_v, wu_v, sems):
        i = pl.program_id(0)
        _load_weights([(wd_hbm, wd_v), (wg_hbm, wg_v), (wu_hbm, wu_v)], sems, S, S, i, 0)
        dy_ = dy_ref[...]
        dyb = dy_.astype(BF16)
        dn = None
        for j in range(S):
            dgu = lax.dot_general(dyb, wd_v[j], _NT, preferred_element_type=F32)
            gv = g_ref[j].astype(F32)
            sg = _sigmoid(gv)
            dg = (dgu * u_ref[j].astype(F32) * (sg * (1.0 + gv * (1.0 - sg)))).astype(dg_ref.dtype)
            du = (dgu * (gv * sg)).astype(du_ref.dtype)
            dg_ref[j] = dg
            du_ref[j] = du
            part = jnp.dot(dg, wg_v[j], preferred_element_type=F32) + jnp.dot(du, wu_v[j], preferred_element_type=F32)
            dn = part if dn is None else dn + part
        x = h_ref[...]
        r = lax.rsqrt(jnp.mean(x * x, axis=-1, keepdims=True) + RMS_EPS)
        xhat = x * r
        dxhat = dn * gain_ref[...]
        dh = dy_ + r * (dxhat - xhat * jnp.mean(dxhat * xhat, axis=-1, keepdims=True))
        dh_ref[...] = dh
        dhb_ref[...] = dh.astype(dhb_ref.dtype)
        pg = jnp.sum(dn * xhat, axis=0, keepdims=True)

        @pl.when(i == 0)
        def _():
            dgain_ref[...] = pg

        @pl.when(i > 0)
        def _():
            dgain_ref[...] += pg

    row = pl.BlockSpec((tm, D), lambda i: (i, 0))
    vec = pl.BlockSpec((1, D), lambda i: (0, 0))
    seg = pl.BlockSpec((S, tm, f), lambda i: (0, i, 0))
    hbm = pl.BlockSpec(memory_space=pl.ANY)
    segs = jax.ShapeDtypeStruct((S, T, f), BF16)
    outs, xo = _call(
        body, name, (nt,),
        [row, row, vec, seg, seg, hbm, hbm, hbm], [seg, seg, row, row, vec],
        [segs, segs, jax.ShapeDtypeStruct((T, D), F32), jax.ShapeDtypeStruct((T, D), BF16), jax.ShapeDtypeStruct((1, D), F32)],
        [dy, h, gain, g, u, wd, wg, wu], ("arbitrary",),
        [pltpu.VMEM((S, f, D), BF16), pltpu.VMEM((S, f, D), BF16), pltpu.VMEM((S, f, D), BF16),
         pltpu.SemaphoreType.DMA((3, S))], hosted=hosted)
    return (*outs, xo)


_NT = (((1,), (1,)), ((), ()))
_TN = (((0,), (0,)), ((), ()))


def nt_cols_rms(dy, w, h, gain, dres, name, hosted=(), also_bf16=False):
    T, K = h.shape
    S, _, n = w.shape
    tm = _tile(T, TOKEN_TILE)

    def body(dy_ref, w_ref, h_ref, gain_ref, dres_ref, dh_ref, dgain_ref, *rest):
        i = pl.program_id(0)
        dn = None
        for s in range(S):
            part = lax.dot_general(dy_ref[:, s * n:(s + 1) * n], w_ref[s], _NT, preferred_element_type=F32)
            dn = part if dn is None else dn + part
        x = h_ref[...]
        r = lax.rsqrt(jnp.mean(x * x, axis=-1, keepdims=True) + RMS_EPS)
        xhat = x * r
        dxhat = dn * gain_ref[...]
        dh = dres_ref[...] + r * (dxhat - xhat * jnp.mean(dxhat * xhat, axis=-1, keepdims=True))
        dh_ref[...] = dh
        if also_bf16:
            rest[0][...] = dh.astype(BF16)
        pg = jnp.sum(dn * xhat, axis=0, keepdims=True)

        @pl.when(i == 0)
        def _():
            dgain_ref[...] = pg

        @pl.when(i > 0)
        def _():
            dgain_ref[...] += pg

    row = pl.BlockSpec((tm, K), lambda i: (i, 0))
    vec = pl.BlockSpec((1, K), lambda i: (0, 0))
    out_specs, out_shape = [row, vec], [jax.ShapeDtypeStruct((T, K), F32), jax.ShapeDtypeStruct((1, K), F32)]
    if also_bf16:
        out_specs, out_shape = out_specs + [row], out_shape + [jax.ShapeDtypeStruct((T, K), BF16)]
    outs, xo = _call(
        body, name, (T // tm,),
        [pl.BlockSpec((tm, S * n), lambda i: (i, 0)), pl.BlockSpec((S, K, n), lambda i: (0, 0, 0)), row, vec, row],
        out_specs, out_shape, [dy, w, h, gain, dres], ("arbitrary",), hosted=hosted)
    return (*outs, xo)


def tn_grad(a, dy, S, a_by_seg, name, hosted=()):
    T = dy.shape[0]
    tt = _tile(T, GRAD_TOKEN_TILE)
    G = GRAD_SEGS_PER_STEP
    if a_by_seg:
        R, C = a.shape[2], dy.shape[1]
        a_spec = pl.BlockSpec((G, tt, R), lambda p, t: (p, t, 0))
        b_spec = pl.BlockSpec((tt, C), lambda p, t: (t, 0))
    else:
        R, C = a.shape[1], dy.shape[1] // S
        a_spec = pl.BlockSpec((tt, R), lambda p, t: (t, 0))
        b_spec = pl.BlockSpec((tt, G * C), lambda p, t: (t, p))
    Rh = R // 2
    nt = T // tt

    def body(a_ref, b_ref, o_ref, acc_ref):
        t = pl.program_id(1)
        parts = []
        for j in range(G):
            a_j = a_ref[j] if a_by_seg else a_ref[...]
            b_j = b_ref[...] if a_by_seg else b_ref[:, j * C:(j + 1) * C]
            parts.append(lax.dot_general(a_j, b_j.astype(BF16), _TN, preferred_element_type=F32))

        @pl.when(t == 0)
        def _():
            for j in range(G):
                acc_ref[j] = parts[j]

        @pl.when(t > 0)
        def _():
            for j in range(G):
                acc_ref[j] += parts[j]

        @pl.when(t == nt - 1)
        def _():
            for j in range(G):
                o_ref[0, j] = acc_ref[j, :Rh, :].astype(o_ref.dtype)
                o_ref[1, j] = acc_ref[j, Rh:, :].astype(o_ref.dtype)

    (gh,), xo = _call(
        body, name, (S // G, nt), [a_spec, b_spec], [pl.BlockSpec((2, G, Rh, C), lambda p, t: (0, p, 0, 0))],
        [jax.ShapeDtypeStruct((2, S, Rh, C), BF16)], [a, dy], ("parallel", "arbitrary"), [pltpu.VMEM((G, R, C), F32)],
        hosted=hosted)
    return gh, xo


def tn_grad_square(a, dy, S, name, hosted=()):
    T, K = a.shape
    N = dy.shape[1]
    tt = _tile(T, GRAD_TOKEN_TILE)
    nt = T // tt
    Rh = K // S // 2

    def body(a_ref, b_ref, o_ref, acc_ref):
        t = pl.program_id(0)
        part = lax.dot_general(a_ref[...], b_ref[...].astype(BF16), _TN, preferred_element_type=F32)

        @pl.when(t == 0)
        def _():
            acc_ref[...] = part

        @pl.when(t > 0)
        def _():
            acc_ref[...] += part

        @pl.when(t == nt - 1)
        def _():
            for s in range(S):
                for hf in range(2):
                    r0 = (2 * s + hf) * Rh
                    o_ref[hf, s] = acc_ref[r0:r0 + Rh, :].astype(o_ref.dtype)

    (gh,), xo = _call(
        body, name, (nt,), [pl.BlockSpec((tt, K), lambda t: (t, 0)), pl.BlockSpec((tt, N), lambda t: (t, 0))],
        [pl.BlockSpec((2, S, Rh, N), lambda t: (0, 0, 0, 0))], [jax.ShapeDtypeStruct((2, S, Rh, N), BF16)],
        [a, dy], ("arbitrary",), [pltpu.VMEM((K, N), F32)], hosted=hosted)
    return gh, xo


def _place():
    x, y, c = lax.axis_index("x"), lax.axis_index("y"), lax.axis_index("c")
    chips = [(1 - x, y), (x, 1 - y), (1 - x, 1 - y)]
    return x, y, c, chips


def _remote(src, dst, send_sem, recv_sem, dev):
    return pltpu.make_async_remote_copy(src_ref=src, dst_ref=dst, send_sem=send_sem, recv_sem=recv_sem,
                                        device_id=dev, device_id_type=MESH)


def small_allreduce(v, name, hosted=()):
    rows, W = v.shape

    def body(v_ref, o_ref, sib_ref, pair_ref, chips_ref, send_sems, recv_sems):
        x, y, c, chips = _place()
        me = 2 * x + y
        swap = _remote(v_ref, sib_ref, send_sems.at[3], recv_sems.at[3], (x, y, 1 - c))
        swap.start()
        swap.wait()
        mine, other = v_ref[...], sib_ref[...]
        pair_ref[...] = jnp.where(c == 0, mine, other) + jnp.where(c == 0, other, mine)
        sends = []
        for j, (px, py) in enumerate(chips):
            cp = _remote(pair_ref, chips_ref.at[me], send_sems.at[j], recv_sems.at[j], (px, py, c))
            cp.start()
            sends.append(cp)
        chips_ref[me] = pair_ref[...]
        for j, (px, py) in enumerate(chips):
            blk = chips_ref.at[2 * px + py]
            _remote(blk, blk, send_sems.at[j], recv_sems.at[j], (px, py, c)).wait_recv()
        for cp in sends:
            cp.wait_send()
        o_ref[...] = (chips_ref[0] + chips_ref[1]) + (chips_ref[2] + chips_ref[3])

    vm = pl.BlockSpec(memory_space=pltpu.VMEM)
    (out,), xo = _call(
        body, name, (), [vm], [vm], [jax.ShapeDtypeStruct((rows, W), F32)], [v], (),
        [pltpu.VMEM((rows, W), F32), pltpu.VMEM((rows, W), F32), pltpu.VMEM((N_CHIPS, rows, W), F32),
         pltpu.SemaphoreType.DMA((4,)), pltpu.SemaphoreType.DMA((4,))], hosted=hosted)
    return out, xo


def _gather_p1_copies(srcs, bufs, ssem, rsem, base):
    x, y, c, chips = _place()
    me, sib = 2 * x + y, (x, y, 1 - c)
    sends, recvs = [], []
    for k, (src, buf) in enumerate(zip(srcs, bufs)):
        rh = src.shape[0] // 2
        s0 = base + 4 * k
        sends.append(_remote(src, buf.at[me], ssem.at[s0 + 3], rsem.at[s0 + 3], sib))
        recvs.append(_remote(buf.at[me], buf.at[me], ssem.at[s0 + 3], rsem.at[s0 + 3], sib))
        for j, (px, py) in enumerate(chips):
            sends.append(_remote(src.at[pl.ds(c * rh, rh)], buf.at[me, pl.ds(c * rh, rh)], ssem.at[s0 + j], rsem.at[s0 + j], (px, py, c)))
            blk = buf.at[2 * px + py, pl.ds(c * rh, rh)]
            recvs.append(_remote(blk, blk, ssem.at[s0 + j], rsem.at[s0 + j], (px, py, c)))
    return sends, recvs


def _gather_p2_copies(bufs, ssem, rsem, base):
    x, y, c, chips = _place()
    sib = (x, y, 1 - c)
    sends, recvs = [], []
    for k, buf in enumerate(bufs):
        rh = buf.shape[1] // 2
        for j, (px, py) in enumerate(chips):
            s0 = base + 3 * k + j
            blk = buf.at[2 * px + py, pl.ds(c * rh, rh)]
            sends.append(_remote(blk, blk, ssem.at[s0], rsem.at[s0], sib))
            got = buf.at[2 * px + py, pl.ds((1 - c) * rh, rh)]
            recvs.append(_remote(got, got, ssem.at[s0], rsem.at[s0], sib))
    return sends, recvs


def _gathered_shape(s):
    return jax.ShapeDtypeStruct((N_CHIPS,) + s.shape, s.dtype)


def gather_p1(shards):
    return _Exchange(shards, [_gathered_shape(s) for s in shards], {}, 4 * len(shards),
                     lambda xi, xo, ss, rs: _gather_p1_copies(xi, xo, ss, rs, 0))


def gather_p2(bufs):
    return _Exchange(bufs, [jax.ShapeDtypeStruct(b.shape, b.dtype) for b in bufs], {k: k for k in range(len(bufs))},
                     3 * len(bufs), lambda xi, xo, ss, rs: _gather_p2_copies(xo, ss, rs, 0))


def gather_whole(whole, begun):
    nw, n = len(whole), len(whole) + len(begun)
    shards = list(whole) + list(begun)
    return _Exchange(shards, [_gathered_shape(s) for s in shards], {}, 4 * n + 3 * nw,
                     lambda xi, xo, ss, rs: _gather_p1_copies(xi, xo, ss, rs, 0),
                     then=lambda xi, xo, ss, rs: _gather_p2_copies(xo[:nw], ss, rs, 4 * n))


def gather_small(v):
    def copies(xi, xo, ssem, rsem):
        x, y, c, chips = _place()
        me, sib = 2 * x + y, (x, y, 1 - c)
        sends = [_remote(xi[0], xo[0].at[me], ssem.at[3], rsem.at[3], sib)]
        recvs = [_remote(xo[0].at[me], xo[0].at[me], ssem.at[3], rsem.at[3], sib)]
        for j, (px, py) in enumerate(chips):
            sends.append(_remote(xi[0], xo[0].at[me], ssem.at[j], rsem.at[j], (px, py, c)))
            blk = xo[0].at[2 * px + py]
            recvs.append(_remote(blk, blk, ssem.at[j], rsem.at[j], (px, py, c)))
        return sends, recvs

    return _Exchange([v], [_gathered_shape(v)], {}, 4, copies)


def gather_all(v):
    def copies(xi, xo, ssem, rsem):
        x, y, c, _ = _place()
        sends, recvs = [], []
        for m in range(1, N_DEV):
            px, py, pc = (1 - x) if m & 4 else x, (1 - y) if m & 2 else y, (1 - c) if m & 1 else c
            sends.append(_remote(xi[0], xo[0].at[4 * x + 2 * y + c], ssem.at[m - 1], rsem.at[m - 1], (px, py, pc)))
            blk = xo[0].at[4 * px + 2 * py + pc]
            recvs.append(_remote(blk, blk, ssem.at[m - 1], rsem.at[m - 1], (px, py, pc)))
        return sends, recvs

    return _Exchange([v], [jax.ShapeDtypeStruct((N_DEV,) + v.shape, v.dtype)], {}, N_DEV - 1, copies)


def run_exchanges(exchanges, name):
    return _call(lambda: None, name, (), [], [], [], [], (), hosted=exchanges)[1]


def sibling_halves(grads):
    def copies(xi, xo, ssem, rsem):
        x, y, c, _ = _place()
        sends = [_remote(xi[k].at[1 - c], xo[k], ssem.at[k], rsem.at[k], (x, y, 1 - c)) for k in range(len(grads))]
        return sends, sends

    return _Exchange(grads, [jax.ShapeDtypeStruct(g.shape[1:], g.dtype) for g in grads], {}, len(grads), copies)


def pair_sum(ghs, recvs, cidx, name):
    n = len(ghs)
    S = ghs[0].shape[1]

    def body(c_ref, *refs):
        for k in range(n):
            a_ref, b_ref, o_ref = refs[2 * k], refs[2 * k + 1], refs[2 * n + k]
            o_ref[...] = (a_ref[...].astype(F32) + b_ref[...].astype(F32)).astype(o_ref.dtype)

    in_specs, out_specs, out_shape, args = [], [], [], []
    for gh, recv in zip(ghs, recvs):
        _, _, Rh, C = gh.shape
        in_specs += [pl.BlockSpec((None, None, Rh, C), lambda s, c_ref: (c_ref[0], s, 0, 0)),
                     pl.BlockSpec((None, Rh, C), lambda s, c_ref: (s, 0, 0))]
        out_specs.append(pl.BlockSpec((None, Rh, C), lambda s, c_ref: (s, 0, 0)))
        out_shape.append(jax.ShapeDtypeStruct((S, Rh, C), BF16))
        args += [gh, recv]
    return pl.pallas_call(
        body, name=name, out_shape=out_shape,
        grid_spec=pltpu.PrefetchScalarGridSpec(num_scalar_prefetch=1, grid=(S,), in_specs=in_specs, out_specs=out_specs),
        compiler_params=_params(("parallel",)),
    )(cidx, *args)


def scatter_p1(parts):
    def copies(xi, xo, ssem, rsem):
        x, y, c, chips = _place()
        me, sib = 2 * x + y, (x, y, 1 - c)
        sends, recvs = [], []
        for k in range(len(parts)):
            s0 = 4 * k
            sends.append(_remote(xi[k].at[me], xo[k].at[me, c], ssem.at[s0 + 3], rsem.at[s0 + 3], sib))
            own = xo[k].at[me, 1 - c]
            recvs.append(_remote(own, own, ssem.at[s0 + 3], rsem.at[s0 + 3], sib))
            for j, (px, py) in enumerate(chips):
                sends.append(_remote(xi[k].at[2 * px + py], xo[k].at[me, c], ssem.at[s0 + j], rsem.at[s0 + j], (px, py, c)))
                blk = xo[k].at[2 * px + py, c]
                recvs.append(_remote(blk, blk, ssem.at[s0 + j], rsem.at[s0 + j], (px, py, c)))
        return sends, recvs

    return _Exchange(parts, [jax.ShapeDtypeStruct((p.shape[0], 2) + p.shape[1:], p.dtype) for p in parts], {},
                     4 * len(parts), copies)


def scatter_p2(bufs):
    def copies(xi, xo, ssem, rsem):
        x, y, c, chips = _place()
        sib = (x, y, 1 - c)
        sends, recvs = [], []
        for k in range(len(bufs)):
            for j, (px, py) in enumerate(chips):
                s0 = 3 * k + j
                blk = xo[k].at[2 * px + py, c]
                sends.append(_remote(blk, blk, ssem.at[s0], rsem.at[s0], sib))
                got = xo[k].at[2 * px + py, 1 - c]
                recvs.append(_remote(got, got, ssem.at[s0], rsem.at[s0], sib))
        return sends, recvs

    return _Exchange(bufs, [jax.ShapeDtypeStruct(b.shape, b.dtype) for b in bufs], {k: k for k in range(len(bufs))},
                     3 * len(bufs), copies)


def _adamw_math(w, g, m, v):
    m = ADAM_B1 * m + (1.0 - ADAM_B1) * g
    v = ADAM_B2 * v + (1.0 - ADAM_B2) * (g * g)
    m_hat = m / (1.0 - ADAM_B1 ** ADAM_STEP)
    v_hat = v / (1.0 - ADAM_B2 ** ADAM_STEP)
    delta = -ADAM_LR * (m_hat / (jnp.sqrt(v_hat) + ADAM_EPS) + ADAM_WD * w)
    return delta, m, v


def adamw_reduce(tensors, place, lyr, bases, name):
    n = len(tensors)
    L, R, C = tensors[0][0].shape
    Rh = R // 2
    rb = _tile(Rh, ROW_TILE, 2 * SUBLANES)
    nb = Rh // rb

    def body(place_ref, *refs):
        mine = (place_ref[1] == pl.program_id(0))
        for k in range(n):
            p_ref, b0, b1, b2, b3, w_ref, m_ref, v_ref = refs[8 * k:8 * k + 8]
            go_ref, d_ref, mo_ref, vo_ref = refs[len(refs) - 4 * n + 4 * k:len(refs) - 4 * n + 4 * k + 4]
            g = None
            for p, b in enumerate((b0, b1, b2, b3)):
                val = jnp.where(mine & (place_ref[0] == p), p_ref[...], b[...]).astype(F32)
                g = val if g is None else g + val
            d, mn, vn = _adamw_math(w_ref[...], g, m_ref[...], v_ref[...])
            go_ref[...] = g
            d_ref[...] = d
            mo_ref[...] = mn
            vo_ref[...] = vn

    def buf_spec(p):
        def idx(h, i, pr):
            own = (pr[0] == p) & (pr[1] == h)
            return (p, jnp.where(own, 1 - h, h), i, 0)
        return pl.BlockSpec((None, None, rb, C), idx)

    blk = pl.BlockSpec((None, rb, C), lambda h, i, pr: (lyr, h * nb + i, 0))
    in_specs, args = [], []
    for w, m, v, buf, part in tensors:
        in_specs += [pl.BlockSpec((None, rb, C), lambda h, i, pr: (pr[0], i, 0))] + [buf_spec(p) for p in range(N_CHIPS)] + [blk] * 3
        args += [part, buf, buf, buf, buf, w, m, v]
    aliases = {}
    if bases is not None:
        in_specs += [pl.BlockSpec(memory_space=pl.ANY)] * (4 * n)
        aliases = {len(args) + k: k for k in range(4 * n)}
        args += list(bases)
    shp = jax.ShapeDtypeStruct((L, R, C), F32)
    flat = _call(body, name, (2, nb), in_specs, [blk] * (4 * n), [shp] * (4 * n), args, ("parallel", "parallel"),
                 prefetch=[place], own_aliases=aliases)[0]
    return flat


def small_update(late, early, own, place, entries, loss_row, name):
    ne = len(entries)
    D = late.shape[1]

    def body(place_ref, late_ref, early_ref, own_ref, *refs):
        ins, outs = refs[:3 * ne], refs[3 * ne:]
        ch = place_ref[0]
        me = 2 * place_ref[0] + place_ref[1]

        def early_sum(rs, cs):
            acc = None
            for d in range(N_DEV):
                val = jnp.where(me == d, own_ref[rs, cs], early_ref[d, rs, cs])
                acc = val if acc is None else acc + val
            return acc

        outs[4 * ne][...] = early_sum(slice(loss_row, loss_row + 1), slice(0, D))
        for e, (source, row0, kind, w, _, _) in enumerate(entries):
            r, width = w.shape
            gsum = early_sum if source == "early" else (lambda rs, cs: late_ref[rs, cs])

            if kind == "full":
                g = gsum(slice(row0, row0 + r), slice(0, D))
            elif kind == "cols":
                g = gsum(slice(row0, row0 + r), slice(0, width))
                for q in range(1, N_CHIPS):
                    g = jnp.where(ch == q, gsum(slice(row0, row0 + r), slice(q * width, (q + 1) * width)), g)
            else:
                per_row = D // width
                g = gsum(slice(row0, row0 + 1), slice(0, width))
                for q in range(1, N_CHIPS):
                    rr = row0 + q // per_row
                    cc = (q % per_row) * width
                    g = jnp.where(ch == q, gsum(slice(rr, rr + 1), slice(cc, cc + width)), g)
            d, mn, vn = _adamw_math(ins[3 * e][...], g, ins[3 * e + 1][...], ins[3 * e + 2][...])
            outs[4 * e][...] = g
            outs[4 * e + 1][...] = d
            outs[4 * e + 2][...] = mn
            outs[4 * e + 3][...] = vn

    vm = pl.BlockSpec(memory_space=pltpu.VMEM)
    args, out_shape = [], []
    for _, _, _, w, m, v in entries:
        args += [w, m, v]
        out_shape += [jax.ShapeDtypeStruct(w.shape, F32)] * 4
    out_shape.append(jax.ShapeDtypeStruct((1, D), F32))
    return pl.pallas_call(
        body, name=name,
        in_specs=[pl.BlockSpec(memory_space=pltpu.SMEM), vm, vm, vm] + [vm] * (3 * ne),
        out_specs=[vm] * (4 * ne + 1), out_shape=out_shape,
        compiler_params=pltpu.CompilerParams(vmem_limit_bytes=VMEM_LIMIT),
    )(place, late, early, own, *args)


def _pack_rows(items, width, name):
    starts, at = [], 0
    for it in items:
        starts.append(at)
        at += -(-it.shape[0] // SUBLANES) * SUBLANES
    total = at

    def body(*refs):
        o_ref = refs[-1]
        o_ref[...] = jnp.zeros_like(o_ref)
        for it_ref, r0 in zip(refs[:-1], starts):
            o_ref[r0:r0 + it_ref.shape[0], :] = it_ref[...]

    vm = pl.BlockSpec(memory_space=pltpu.VMEM)
    packed = pl.pallas_call(body, name=name, in_specs=[vm] * len(items), out_specs=vm,
                            out_shape=jax.ShapeDtypeStruct((total, width), F32))(*items)
    return packed, starts


def kernel(x, a_norm, a_w_in, a_conv, a_w_out, b_norm, b_w_pw1, b_b_pw1, b_conv, b_b_conv, b_ln_g, b_ln_b, b_w_pw2, b_b_pw2, ffn_norm, ffn_w_gate, ffn_w_up, ffn_w_down, final_norm, loss_target, m_a_norm, m_a_w_in, m_a_conv, m_a_w_out, m_b_norm, m_b_w_pw1, m_b_b_pw1, m_b_conv, m_b_b_conv, m_b_ln_g, m_b_ln_b, m_b_w_pw2, m_b_b_pw2, m_ffn_norm, m_ffn_w_gate, m_ffn_w_up, m_ffn_w_down, m_final_norm, v_a_norm, v_a_w_in, v_a_conv, v_a_w_out, v_b_norm, v_b_w_pw1, v_b_b_pw1, v_b_conv, v_b_b_conv, v_b_ln_g, v_b_ln_b, v_b_w_pw2, v_b_b_pw2, v_ffn_norm, v_ffn_w_gate, v_ffn_w_up, v_ffn_w_down, v_final_norm):
    T, D = x.shape[1], x.shape[2]
    Dq = D // N_CHIPS
    cx, cy, cc = lax.axis_index("x"), lax.axis_index("y"), lax.axis_index("c")
    chip = (2 * cx + cy).astype(jnp.int32).reshape(1)
    cidx = cc.astype(jnp.int32).reshape(1)
    h0 = x.reshape(T, D)
    tgt = loss_target.reshape(T, D)

    small_shards = [a_conv[0], b_norm, b_b_pw1.reshape(2, Dq), b_conv[0], b_b_conv, b_ln_g, b_ln_b, b_b_pw2]
    packed, st = _pack_rows(small_shards, Dq, "pack_small")

    tr = lambda t: jnp.swapaxes(t, 1, 2)
    w_gate, m_gate, v_gate = tr(ffn_w_gate), tr(m_ffn_w_gate), tr(v_ffn_w_gate)
    w_up, m_up, v_up = tr(ffn_w_up), tr(m_ffn_w_up), tr(v_ffn_w_up)
    bf = lambda t: t.astype(BF16)
    s_in, s_out, s_pw1, s_pw2 = bf(a_w_in[0]), bf(a_w_out[0]), bf(b_w_pw1[0]), bf(b_w_pw2[0])
    s_gate, s_up, s_down = [bf(w_gate[l]) for l in (0, 1)], [bf(w_up[l]) for l in (0, 1)], [bf(ffn_w_down[l]) for l in (0, 1)]

    n0, (g_in,) = rms_fwd(h0, a_norm, "rms_a", hosted=[gather_whole([s_in], [])])
    bcv, (g_out, gate0, sw) = mm_cols(n0, g_in, "mm_w_in", hosted=[gather_p1([s_out, s_gate[0]]), gather_small(packed)])

    def whole(k, r):
        return jnp.transpose(sw[:, st[k]:st[k] + r, :], (1, 0, 2)).reshape(r, D)

    a_conv_f, b_norm_f = whole(0, 3), whole(1, 1)
    b_b_pw1_f = sw[:, st[2]:st[2] + 2, :].reshape(1, 2 * D)
    b_conv_f, b_b_conv_f, b_ln_g_f, b_ln_b_f, b_b_pw2_f = whole(3, b_conv.shape[1]), whole(4, 1), whole(5, 1), whole(6, 1), whole(7, 1)
    ya, h1, (g_out, up0, down0, gate0) = gateconv_fwd(bcv, a_conv_f, gather_p2([g_out]), h0, "gateconv_fwd",
                                                      hosted=[gather_p1([s_up[0], s_down[0]]), gather_p2([gate0])])
    g_out = g_out.reshape(1, D, D)
    n1, fg0, fu0, gu0, h2, (up0, down0, g_pw1, g_pw2, gate1, up1) = ffn_fwd(
        h1, ffn_norm[0:1], [gate0], "ffn_fwd0", arriving=gather_p2([up0, down0]),
        hosted=[gather_whole([s_pw1, s_pw2], [s_gate[1], s_up[1]])])
    g_pw2 = g_pw2.reshape(1, D, D)
    n2, ub, (down1, gate1, up1) = rms_mm_cols(h2, b_norm_f, g_pw1, b_b_pw1_f, "mm_pw1",
                                              hosted=[gather_p1([s_down[1]]), gather_p2([gate1, up1])])
    cu, sb, h3, (down1,) = bconv_fwd(ub, b_conv_f, b_b_conv_f, b_ln_g_f, b_ln_b_f, g_pw2, b_b_pw2_f, h2, "bconv_fwd",
                                     hosted=[gather_p2([down1])])
    n3, fg1, fu1, gu1, h4, _ = ffn_fwd(h3, ffn_norm[1:2], [gate1, up1, down1], "ffn_fwd1")
    loss_part, dh4, dh4_b, d_final = loss_head(h4, final_norm.reshape(1, D), tgt, "loss_head")

    place = jnp.concatenate([chip, cidx])

    def pair_sums(ghs, from_sib, tags):
        return pair_sum(ghs, from_sib, cidx, "pair_sum_" + "_".join(tags))

    def upd(wmvs, bufs, parts, tag):
        flat = None
        for lyr in range(len(bufs[0])):
            tensors = [(w, m, v, b[lyr], p[lyr]) for (w, m, v), b, p in zip(wmvs, bufs, parts)]
            flat = adamw_reduce(tensors, place, lyr, flat, "adamw_%s%d" % (tag, lyr))
        return [flat[4 * k:4 * k + 4] for k in range(len(wmvs))]

    dg1, du1, dh3, dh3_b, d_fn1, _ = ffn_bwd(dh4, h3, ffn_norm[1:2], fg1, fu1, down1, gate1, up1, "ffn_bwd1")
    gh_down1, _ = tn_grad(gu1, dh4_b, N_CHIPS, True, "tn_down1")
    gh_gate1, _ = tn_grad(dg1, n3, N_CHIPS, True, "tn_gate1")
    gh_up1, _ = tn_grad(du1, n3, N_CHIPS, True, "tn_up1")
    f1 = [gh_gate1, gh_up1, gh_down1]

    dcu, d_ln_g, d_ln_b, d_b_conv, d_b_pw2, sib_f1 = pw2_ln_bwd(dh3, g_pw2, cu, b_ln_g_f, b_ln_b_f, "pw2_ln_bwd",
                                                                hosted=[sibling_halves(f1)])
    p_f1 = pair_sums(f1, sib_f1, ["gate1", "up1", "down1"])
    gh_pw2, _ = tn_grad_square(sb, dh3_b, N_CHIPS, "tn_pw2")
    dub, d_bconv_w, d_b_pw1, buf_f1 = bconv_bwd(dcu, ub, b_conv_f, "bconv_bwd", hosted=[scatter_p1(p_f1)])
    gh_pw1, _ = tn_grad(n2, dub, N_CHIPS, False, "tn_pw1")
    b_grp = [gh_pw1, gh_pw2]
    dh2, d_b_norm, dh2_b, (*buf_f1, sib_pw1, sib_pw2) = nt_cols_rms(
        dub, g_pw1, h2, b_norm_f, dh3, "nt_pw1", hosted=[scatter_p2(buf_f1), sibling_halves(b_grp)], also_bf16=True)
    sib_b = [sib_pw1, sib_pw2]
    p_b = pair_sums(b_grp, sib_b, ["pw1", "pw2"])

    early_grads = [d_b_norm, d_b_pw1.reshape(2, D), d_bconv_w, d_b_conv, d_ln_g, d_ln_b, d_b_pw2, d_fn1, d_final,
                   jnp.broadcast_to(loss_part, (1, D))]
    epacked, es = _pack_rows(early_grads, D, "pack_small_grads_early")
    dg0, du0, dh1, dh1_b, d_fn0, (*buf_b, eall) = ffn_bwd(dh2, h1, ffn_norm[0:1], fg0, fu0, down0, gate0, up0, "ffn_bwd0",
                                                         hosted=[scatter_p1(p_b), gather_all(epacked)])
    gh_down0, _ = tn_grad(gu0, dh2_b, N_CHIPS, True, "tn_down0")
    gh_gate0, (*buf_b, sib_down0) = tn_grad(dg0, n1, N_CHIPS, True, "tn_gate0",
                                            hosted=[scatter_p2(buf_b), sibling_halves([gh_down0])])
    p_down0 = pair_sums([gh_down0], [sib_down0], ["down0"])
    gh_up0, (buf_down0, sib_gate0) = tn_grad(du0, n1, N_CHIPS, True, "tn_up0",
                                             hosted=[scatter_p1(p_down0), sibling_halves([gh_gate0])])
    p_gate0 = pair_sums([gh_gate0], [sib_gate0], ["gate0"])
    gh_out, (buf_down0, sib_up0) = tn_grad_square(ya, dh1_b, N_CHIPS, "tn_w_out",
                                                  hosted=[scatter_p2([buf_down0]), sibling_halves([gh_up0])])
    p_up0 = pair_sums([gh_up0], [sib_up0], ["up0"])
    dbcv, d_aconv_w, (buf_gate0, sib_out) = gateconv_bwd(dh1_b, g_out, bcv, a_conv_f, "gateconv_bwd",
                                                         hosted=[scatter_p1(p_gate0), sibling_halves([gh_out])])
    p_out = pair_sums([gh_out], [sib_out], ["out"])
    gh_in, (buf_up0, buf_out, buf_gate0) = tn_grad(n0, dbcv, N_CHIPS, False, "tn_w_in",
                                                   hosted=[scatter_p1(p_up0 + p_out), scatter_p2([buf_gate0])])
    sib_in = run_exchanges([sibling_halves([gh_in])], "reduce_in_siblings")
    p_in = pair_sums([gh_in], sib_in, ["in"])
    grad_x, d_a_norm, (buf_in, buf_up0, buf_out) = nt_cols_rms(
        dbcv, g_in, h0, a_norm, dh1, "nt_w_in", hosted=[scatter_p1(p_in), scatter_p2([buf_up0, buf_out])])
    p_f0 = [p_gate0[0], p_up0[0], p_down0[0]]

    lpacked, ls = _pack_rows([d_a_norm, d_aconv_w, d_fn0], D, "pack_small_grads_late")
    lall, (buf_in,) = small_allreduce(lpacked, "allreduce_small_grads", hosted=[scatter_p2([buf_in])])
    buf_a, p_a = [buf_in, buf_out], [p_in[0], p_out[0]]

    r_gate, r_up = upd([(w_gate, m_gate, v_gate), (w_up, m_up, v_up)],
                       [[buf_gate0, buf_f1[0]], [buf_up0, buf_f1[1]]], [[p_f0[0], p_f1[0]], [p_f0[1], p_f1[1]]], "gate_up")
    (r_down,) = upd([(ffn_w_down, m_ffn_w_down, v_ffn_w_down)], [[buf_down0, buf_f1[2]]], [[p_f0[2], p_f1[2]]], "down")
    r_gate, r_up = [tr(t) for t in r_gate], [tr(t) for t in r_up]
    (r_pw1,) = upd([(b_w_pw1, m_b_w_pw1, v_b_w_pw1)], [[buf_b[0]]], [[p_b[0]]], "pw1")
    r_pw2, r_out = upd([(b_w_pw2, m_b_w_pw2, v_b_w_pw2), (a_w_out, m_a_w_out, v_a_w_out)],
                       [[buf_b[1]], [buf_a[1]]], [[p_b[1]], [p_a[1]]], "pw2_out")
    (r_in,) = upd([(a_w_in, m_a_w_in, v_a_w_in)], [[buf_a[0]]], [[p_a[0]]], "w_in")
    entries = [
        ("late", ls[0], "full", a_norm, m_a_norm, v_a_norm),
        ("late", ls[1], "cols", a_conv[0], m_a_conv[0], v_a_conv[0]),
        ("early", es[0], "cols", b_norm, m_b_norm, v_b_norm),
        ("early", es[1], "flat2", b_b_pw1, m_b_b_pw1, v_b_b_pw1),
        ("early", es[2], "cols", b_conv[0], m_b_conv[0], v_b_conv[0]),
        ("early", es[3], "cols", b_b_conv, m_b_b_conv, v_b_b_conv),
        ("early", es[4], "cols", b_ln_g, m_b_ln_g, v_b_ln_g),
        ("early", es[5], "cols", b_ln_b, m_b_ln_b, v_b_ln_b),
        ("early", es[6], "cols", b_b_pw2, m_b_b_pw2, v_b_b_pw2),
        ("late", ls[2], "full", ffn_norm[0:1], m_ffn_norm[0:1], v_ffn_norm[0:1]),
        ("early", es[7], "full", ffn_norm[1:2], m_ffn_norm[1:2], v_ffn_norm[1:2]),
        ("early", es[8], "full", final_norm.reshape(1, D), m_final_norm.reshape(1, D), v_final_norm.reshape(1, D)),
    ]
    so = small_update(lall, eall, epacked, place, entries, es[9], "small_update")
    sm = [so[4 * e:4 * e + 4] for e in range(len(entries))]

    def shaped(e, like):
        return [t.reshape(like.shape) for t in sm[e]]

    r_a_norm, r_a_conv, r_b_norm, r_b_b_pw1 = shaped(0, a_norm), shaped(1, a_conv), shaped(2, b_norm), shaped(3, b_b_pw1)
    r_b_conv, r_b_b_conv, r_b_ln_g, r_b_ln_b = shaped(4, b_conv), shaped(5, b_b_conv), shaped(6, b_ln_g), shaped(7, b_ln_b)
    r_b_b_pw2, r_final = shaped(8, b_b_pw2), shaped(11, final_norm)
    r_ffn_norm = [jnp.concatenate([l0, l1], axis=0) for l0, l1 in zip(sm[9], sm[10])]

    loss = so[4 * len(entries)][0, 0]
    order =[r_a_norm, r_in, r_a_conv, r_out, r_b_norm, r_pw1, r_b_b_pw1, r_b_conv, r_b_b_conv, r_b_ln_g, r_b_ln_b,
             r_pw2, r_b_b_pw2, r_ffn_norm, r_gate, r_up, r_down, r_final]
    outs = [loss, grad_x.reshape(x.shape)]
    for field in range(4):
        outs += [r[field] for r in order]
    return tuple(outs)
```

```python
import functools

import jax
import jax.numpy as jnp
from jax import lax
from jax.experimental import pallas as pl
from jax.experimental.pallas import tpu as pltpu

RMS_EPS = 1e-6
LN_EPS = 1e-5
ADAM_LR = 0.001
ADAM_B1 = 0.9
ADAM_B2 = 0.999
ADAM_EPS = 1e-08
ADAM_WD = 0.01
ADAM_STEP = 10

N_CHIPS = 4
N_DEV = 8
LANES = 128
SUBLANES = 8
HALO = 32
CONV_ROWS = 64
TOKEN_TILE = 512
WIDE_TOKEN_TILE = 1024
GRAD_TOKEN_TILE = 2048
GRAD_SEGS_PER_STEP = 2
FFN_ROW_CHUNKS = 2
FFN_FWD_SEGS_PER_STEP = 4
FFN_BWD_TOKEN_TILE = 256
ROW_TILE = 256
VMEM_LIMIT = 56 * 1024 * 1024
MESH = pl.DeviceIdType.MESH
BF16 = jnp.bfloat16
F32 = jnp.float32


def _tile(n, pref, mult=SUBLANES):
    t = min(n, pref) // mult * mult
    while n % t:
        t -= mult
    return t


def _params(sem):
    return pltpu.CompilerParams(dimension_semantics=sem, vmem_limit_bytes=VMEM_LIMIT)


def _sigmoid(x):
    return 0.5 * jnp.tanh(0.5 * x) + 0.5


class _Exchange:
    def __init__(self, ins, outs, aliases, n_sems, copies, then=None):
        self.ins, self.outs, self.aliases, self.n_sems, self.copies = list(ins), list(outs), dict(aliases), n_sems, copies
        self.then = then
        self.early = False

    def awaited_first(self):
        self.early = True
        return self

    def start(self, xi, xo, ssem, rsem):
        for cp in self.copies(xi, xo, ssem, rsem)[0]:
            cp.start()

    def finish(self, xi, xo, ssem, rsem):
        sends, recvs = self.copies(xi, xo, ssem, rsem)
        for cp in recvs:
            cp.wait_recv()
        if self.then is not None:
            sends2, recvs2 = self.then(xi, xo, ssem, rsem)
            for cp in sends2:
                cp.start()
            for cp in recvs2:
                cp.wait_recv()
            sends = sends + sends2
        for cp in sends:
            cp.wait_send()


def _call(body, name, grid, in_specs, out_specs, out_shape, args, sem, scratch_shapes=(), hosted=(), prefetch=(),
          own_aliases=None):
    in_specs, out_specs, out_shape = list(in_specs), list(out_specs), list(out_shape)
    scratch_shapes, hosted, prefetch = list(scratch_shapes), list(hosted), list(prefetch)
    n_pre, n_in, n_out, n_scr = len(prefetch), len(args), len(out_shape), len(scratch_shapes)
    x_in = [a for ex in hosted for a in ex.ins]
    x_out = [o for ex in hosted for o in ex.outs]
    aliases = {n_pre + i: o for i, o in (own_aliases or {}).items()}
    at_in, at_out = n_pre + n_in, n_out
    for ex in hosted:
        for i, o in ex.aliases.items():
            aliases[at_in + i] = at_out + o
        at_in += len(ex.ins)
        at_out += len(ex.outs)
    sems = [pltpu.SemaphoreType.DMA((ex.n_sems,)) for ex in hosted for _ in range(2)]

    def wrapped(*refs):
        pre, refs = refs[:n_pre], refs[n_pre:]
        ins, xi = refs[:n_in], refs[n_in:n_in + len(x_in)]
        refs = refs[n_in + len(x_in):]
        outs, xo = refs[:n_out], refs[n_out:n_out + len(x_out)]
        refs = refs[n_out + len(x_out):]
        scr, sm = refs[:n_scr], refs[n_scr:]
        views, a, b = [], 0, 0
        for e, ex in enumerate(hosted):
            views.append((xi[a:a + len(ex.ins)], xo[b:b + len(ex.outs)], sm[2 * e], sm[2 * e + 1]))
            a += len(ex.ins)
            b += len(ex.outs)
        first = last = None
        for ax, g in enumerate(grid):
            f, l = pl.program_id(ax) == 0, pl.program_id(ax) == g - 1
            first, last = (f, l) if first is None else (first & f, last & l)

        def begin():
            for ex, v in zip(hosted, views):
                ex.start(*v)
            for ex, v in zip(hosted, views):
                if ex.early:
                    ex.finish(*v)

        def end():
            for ex, v in zip(hosted, views):
                if not ex.early:
                    ex.finish(*v)

        if hosted and grid:
            pl.when(first)(begin)
        elif hosted:
            begin()
        early_refs = [r for ex, v in zip(hosted, views) if ex.early for r in v[1]]
        body(*pre, *ins, *outs, *scr, *early_refs)
        if hosted and grid:
            pl.when(last)(end)
        elif hosted:
            end()

    hbm = pl.BlockSpec(memory_space=pl.ANY)
    all_in, all_out = in_specs + [hbm] * len(x_in), out_specs + [hbm] * len(x_out)
    kw = dict(name=name, out_shape=out_shape + x_out, input_output_aliases=aliases,
              compiler_params=_params(tuple("arbitrary" for _ in grid) if hosted else sem))
    if prefetch:
        kw["grid_spec"] = pltpu.PrefetchScalarGridSpec(num_scalar_prefetch=n_pre, grid=grid, in_specs=all_in,
                                                       out_specs=all_out, scratch_shapes=scratch_shapes + sems)
    else:
        kw.update(grid=grid, in_specs=all_in, out_specs=all_out, scratch_shapes=scratch_shapes + sems)
    res = pl.pallas_call(wrapped, **kw)(*prefetch, *args, *x_in)
    return list(res[:n_out]), list(res[n_out:])


def rms_fwd(h, gain, name, hosted=()):
    T, D = h.shape
    tm = _tile(T, TOKEN_TILE)

    def body(h_ref, g_ref, o_ref):
        x = h_ref[...]
        r = lax.rsqrt(jnp.mean(x * x, axis=-1, keepdims=True) + RMS_EPS)
        o_ref[...] = (x * r * g_ref[...]).astype(o_ref.dtype)

    (n,), xo = _call(
        body, name, (T // tm,),
        [pl.BlockSpec((tm, D), lambda i: (i, 0)), pl.BlockSpec((1, D), lambda i: (0, 0))],
        [pl.BlockSpec((tm, D), lambda i: (i, 0))], [jax.ShapeDtypeStruct((T, D), BF16)],
        [h, gain], ("parallel",), hosted=hosted)
    return n, xo


def loss_head(h, gain, tgt, name):
    T, D = h.shape
    tm = _tile(T, TOKEN_TILE)

    def body(h_ref, g_ref, t_ref, loss_ref, dh_ref, dhb_ref, dg_ref):
        i = pl.program_id(0)
        x = h_ref[...]
        g = g_ref[...]
        r = lax.rsqrt(jnp.mean(x * x, axis=-1, keepdims=True) + RMS_EPS)
        xhat = x * r
        diff = xhat * g - t_ref[...]
        part_loss = 0.5 * jnp.sum(jnp.mean(diff * diff, axis=-1, keepdims=True), axis=0, keepdims=True)
        dy = diff * (1.0 / D)
        dxhat = dy * g
        dh = r * (dxhat - xhat * jnp.mean(dxhat * xhat, axis=-1, keepdims=True))
        dh_ref[...] = dh
        dhb_ref[...] = dh.astype(dhb_ref.dtype)
        part = jnp.sum(dy * xhat, axis=0, keepdims=True)

        @pl.when(i == 0)
        def _():
            dg_ref[...] = part
            loss_ref[...] = part_loss

        @pl.when(i > 0)
        def _():
            dg_ref[...] += part
            loss_ref[...] += part_loss

    row = pl.BlockSpec((tm, D), lambda i: (i, 0))
    vec = pl.BlockSpec((1, D), lambda i: (0, 0))
    return pl.pallas_call(
        body, name=name, grid=(T // tm,),
        in_specs=[row, vec, row],
        out_specs=[pl.BlockSpec((1, 1), lambda i: (0, 0)), row, row, vec],
        out_shape=[jax.ShapeDtypeStruct((1, 1), F32), jax.ShapeDtypeStruct((T, D), F32),
                   jax.ShapeDtypeStruct((T, D), BF16), jax.ShapeDtypeStruct((1, D), F32)],
        compiler_params=_params(("arbitrary",)),
    )(h, gain, tgt)


def _prev_halo_spec(tm, width):
    return pl.BlockSpec((HALO, width), lambda i: (jnp.maximum(i * (tm // HALO) - 1, 0), 0))


def _next_halo_spec(tm, width, T):
    return pl.BlockSpec((HALO, width), lambda i: (jnp.minimum((i + 1) * (tm // HALO), T // HALO - 1), 0))


def _shifted(win, off, rows):
    if off % SUBLANES == 0:
        return win[off:off + rows]
    n = win.shape[0]
    return pltpu.roll(win, (n - off) % n, 0)[:rows]


def _rowsum8(x):
    acc = x[0:SUBLANES]
    for q in range(1, x.shape[0] // SUBLANES):
        acc = acc + x[q * SUBLANES:(q + 1) * SUBLANES]
    return acc


def _conv_loops(tm, D, per_block):
    def chunk(r, carry):
        t0 = pl.multiple_of(r * CONV_ROWS, CONV_ROWS)
        for lb in range(D // LANES):
            per_block(t0, slice(lb * LANES, (lb + 1) * LANES))
        return carry

    lax.fori_loop(0, tm // CONV_ROWS, chunk, 0)


def gateconv_fwd(bcv, w, w_out, res, name, hosted=()):
    T, D3 = bcv.shape
    D = D3 // 3
    K = w.shape[0]
    tm = _tile(T, TOKEN_TILE)
    wo_shape = w_out.outs[0].shape

    def body(x_ref, halo_ref, w_ref, res_ref, y_ref, h_ref, pad_ref, wo_v, sem, wo_hbm):
        i = pl.program_id(0)

        @pl.when(i == 0)
        def _():
            cp = pltpu.make_async_copy(wo_hbm, wo_v, sem)
            cp.start()
            cp.wait()

        pad_ref[HALO:, :] = x_ref[:, D:2 * D] * x_ref[:, 2 * D:]
        pad_ref[:HALO, :] = jnp.where(i > 0, halo_ref[:, D:2 * D] * halo_ref[:, 2 * D:], 0.0)

        def block(t0, ls):
            win = pad_ref[pl.ds(t0, CONV_ROWS + HALO), ls]
            acc = jnp.zeros((CONV_ROWS, LANES), F32)
            for k in range(K):
                acc = acc + w_ref[k:k + 1, ls] * _shifted(win, HALO - (K - 1) + k, CONV_ROWS)
            y_ref[pl.ds(t0, CONV_ROWS), ls] = (x_ref[pl.ds(t0, CONV_ROWS), ls] * acc).astype(y_ref.dtype)

        _conv_loops(tm, D, block)
        h_ref[...] = res_ref[...] + jnp.dot(y_ref[...], wo_v[...].reshape(D, D), preferred_element_type=F32)

    row = pl.BlockSpec((tm, D), lambda i: (i, 0))
    (y, h), xo = _call(
        body, name, (T // tm,),
        [pl.BlockSpec((tm, D3), lambda i: (i, 0)), _prev_halo_spec(tm, D3), pl.BlockSpec((K, D), lambda i: (0, 0)), row],
        [row, row], [jax.ShapeDtypeStruct((T, D), BF16), jax.ShapeDtypeStruct((T, D), F32)],
        [bcv, bcv, w, res], ("arbitrary",),
        [pltpu.VMEM((tm + HALO, D), F32), pltpu.VMEM(wo_shape, BF16), pltpu.SemaphoreType.DMA],
        hosted=[w_out.awaited_first()] + list(hosted))
    return y, h, xo


def gateconv_bwd(dh, w_out, bcv, w, name, hosted=()):
    T, D3 = bcv.shape
    D = D3 // 3
    K = w.shape[0]
    tm = _tile(T, TOKEN_TILE)
    nt = T // tm

    def body(dh_ref, dhn_ref, wo_ref, x_ref, xp_ref, xn_ref, w_ref, o_ref, dw_ref, cv_ref, dc_ref, wacc_ref, dy_ref):
        i = pl.program_id(0)
        dy_ref[...] = lax.dot_general(dh_ref[...], wo_ref[0], _NT, preferred_element_type=F32)
        dyn = lax.dot_general(dhn_ref[...], wo_ref[0], _NT, preferred_element_type=F32)
        cv_ref[HALO:, :] = x_ref[:, D:2 * D] * x_ref[:, 2 * D:]
        cv_ref[:HALO, :] = jnp.where(i > 0, xp_ref[:, D:2 * D] * xp_ref[:, 2 * D:], 0.0)
        dc_ref[:tm, :] = dy_ref[...] * x_ref[:, :D]
        dc_ref[tm:, :] = jnp.where(i < nt - 1, dyn * xn_ref[:, :D], 0.0)

        @pl.when(i == 0)
        def _():
            wacc_ref[...] = jnp.zeros_like(wacc_ref)

        def block(t0, ls):
            cwin = cv_ref[pl.ds(t0, CONV_ROWS + HALO), ls]
            dwin = dc_ref[pl.ds(t0, CONV_ROWS + HALO), ls]
            dcon = dwin[:CONV_ROWS]
            conv = jnp.zeros((CONV_ROWS, LANES), F32)
            dcv = jnp.zeros((CONV_ROWS, LANES), F32)
            for k in range(K):
                wk = w_ref[k:k + 1, ls]
                cs = _shifted(cwin, HALO - (K - 1) + k, CONV_ROWS)
                conv = conv + wk * cs
                dcv = dcv + wk * _shifted(dwin, (K - 1) - k, CONV_ROWS)
                wacc_ref[k * SUBLANES:(k + 1) * SUBLANES, ls] += _rowsum8(dcon * cs)
            rows = pl.ds(t0, CONV_ROWS)
            o_ref[rows, ls] = (dy_ref[rows, ls] * conv).astype(o_ref.dtype)
            o_ref[rows, pl.ds(D + ls.start, LANES)] = (dcv * x_ref[rows, pl.ds(2 * D + ls.start, LANES)]).astype(o_ref.dtype)
            o_ref[rows, pl.ds(2 * D + ls.start, LANES)] = (dcv * x_ref[rows, pl.ds(D + ls.start, LANES)]).astype(o_ref.dtype)

        _conv_loops(tm, D, block)

        @pl.when(i == nt - 1)
        def _():
            for k in range(K):
                dw_ref[k:k + 1, :] = jnp.sum(wacc_ref[k * SUBLANES:(k + 1) * SUBLANES, :], axis=0, keepdims=True)

    (dx, dw), xo = _call(
        body, name, (nt,),
        [pl.BlockSpec((tm, D), lambda i: (i, 0)), _next_halo_spec(tm, D, T), pl.BlockSpec((1, D, D), lambda i: (0, 0, 0)),
         pl.BlockSpec((tm, D3), lambda i: (i, 0)), _prev_halo_spec(tm, D3), _next_halo_spec(tm, D3, T),
         pl.BlockSpec((K, D), lambda i: (0, 0))],
        [pl.BlockSpec((tm, D3), lambda i: (i, 0)), pl.BlockSpec((K, D), lambda i: (0, 0))],
        [jax.ShapeDtypeStruct((T, D3), BF16), jax.ShapeDtypeStruct((K, D), F32)],
        [dh, dh, w_out, bcv, bcv, bcv, w], ("arbitrary",),
        [pltpu.VMEM((tm + HALO, D), F32), pltpu.VMEM((tm + HALO, D), F32), pltpu.VMEM((K * SUBLANES, D), F32),
         pltpu.VMEM((tm, D), F32)], hosted=hosted)
    return dx, dw, xo


def bconv_fwd(u, w, b_conv, ln_g, ln_b, w_out, b_out, res, name, hosted=()):
    T, D2 = u.shape
    D = D2 // 2
    K = w.shape[0]
    tm = _tile(T, TOKEN_TILE)

    def body(u_ref, halo_ref, w_ref, bc_ref, g_ref, b_ref, wo_ref, bo_ref, res_ref, cu_ref, s_ref, h_ref, pad_ref):
        i = pl.program_id(0)
        pad_ref[HALO:, :] = u_ref[:, :D] * _sigmoid(u_ref[:, D:])
        pad_ref[:HALO, :] = jnp.where(i > 0, halo_ref[:, :D] * _sigmoid(halo_ref[:, D:]), 0.0)

        def block(t0, ls):
            win = pad_ref[pl.ds(t0, CONV_ROWS + HALO), ls]
            acc = jnp.zeros((CONV_ROWS, LANES), F32)
            for k in range(K):
                acc = acc + w_ref[k:k + 1, ls] * _shifted(win, HALO - (K - 1) + k, CONV_ROWS)
            cu_ref[pl.ds(t0, CONV_ROWS), ls] = acc + bc_ref[:, ls]

        _conv_loops(tm, D, block)
        cu = cu_ref[...]
        mu = jnp.mean(cu, axis=-1, keepdims=True)
        xc = cu - mu
        rstd = lax.rsqrt(jnp.mean(xc * xc, axis=-1, keepdims=True) + LN_EPS)
        ln = xc * rstd * g_ref[...] + b_ref[...]
        s = (ln * _sigmoid(ln)).astype(s_ref.dtype)
        s_ref[...] = s
        h_ref[...] = res_ref[...] + bo_ref[...] + jnp.dot(s, wo_ref[0], preferred_element_type=F32)

    vec = pl.BlockSpec((1, D), lambda i: (0, 0))
    row = pl.BlockSpec((tm, D), lambda i: (i, 0))
    (cu, s, h), xo = _call(
        body, name, (T // tm,),
        [pl.BlockSpec((tm, D2), lambda i: (i, 0)), _prev_halo_spec(tm, D2), pl.BlockSpec((K, D), lambda i: (0, 0)), vec, vec, vec,
         pl.BlockSpec((1, D, D), lambda i: (0, 0, 0)), vec, row],
        [row, row, row], [jax.ShapeDtypeStruct((T, D), F32), jax.ShapeDtypeStruct((T, D), BF16), jax.ShapeDtypeStruct((T, D), F32)],
        [u, u, w, b_conv, ln_g, ln_b, w_out, b_out, res], ("parallel",), [pltpu.VMEM((tm + HALO, D), F32)], hosted=hosted)
    return cu, s, h, xo


def pw2_ln_bwd(dy, w, cu, ln_g, ln_b, name, hosted=()):
    T, D = cu.shape
    tm = _tile(T, TOKEN_TILE)

    def body(dy_ref, w_ref, cu_ref, g_ref, b_ref, dcu_ref, dg_ref, db_ref, dbc_ref, dbo_ref):
        i = pl.program_id(0)
        dy_ = dy_ref[...]
        ds = lax.dot_general(dy_.astype(BF16), w_ref[0], _NT, preferred_element_type=F32)
        cu_ = cu_ref[...]
        mu = jnp.mean(cu_, axis=-1, keepdims=True)
        xc = cu_ - mu
        rstd = lax.rsqrt(jnp.mean(xc * xc, axis=-1, keepdims=True) + LN_EPS)
        xh = xc * rstd
        ln = xh * g_ref[...] + b_ref[...]
        sg = _sigmoid(ln)
        dl = ds * (sg * (1.0 + ln * (1.0 - sg)))
        dxh = dl * g_ref[...]
        dcu = rstd * (dxh - jnp.mean(dxh, axis=-1, keepdims=True) - xh * jnp.mean(dxh * xh, axis=-1, keepdims=True))
        dcu_ref[...] = dcu
        pg = jnp.sum(dl * xh, axis=0, keepdims=True)
        pb = jnp.sum(dl, axis=0, keepdims=True)
        pc = jnp.sum(dcu, axis=0, keepdims=True)
        po = jnp.sum(dy_, axis=0, keepdims=True)

        @pl.when(i == 0)
        def _():
            dg_ref[...] = pg
            db_ref[...] = pb
            dbc_ref[...] = pc
            dbo_ref[...] = po

        @pl.when(i > 0)
        def _():
            dg_ref[...] += pg
            db_ref[...] += pb
            dbc_ref[...] += pc
            dbo_ref[...] += po

    vec = pl.BlockSpec((1, D), lambda i: (0, 0))
    row = pl.BlockSpec((tm, D), lambda i: (i, 0))
    vshape = jax.ShapeDtypeStruct((1, D), F32)
    outs, xo = _call(
        body, name, (T // tm,), [row, pl.BlockSpec((1, D, D), lambda i: (0, 0, 0)), row, vec, vec], [row, vec, vec, vec, vec],
        [jax.ShapeDtypeStruct((T, D), F32), vshape, vshape, vshape, vshape], [dy, w, cu, ln_g, ln_b], ("arbitrary",),
        hosted=hosted)
    return (*outs, xo)


def bconv_bwd(dcu, u, w, name, hosted=()):
    T, D2 = u.shape
    D = D2 // 2
    K = w.shape[0]
    tm = _tile(T, TOKEN_TILE)
    nt = T // tm

    def body(dc_ref, dcn_ref, u_ref, up_ref, w_ref, du_ref, dw_ref, db_ref, glu_ref, dpad_ref, dglu_ref, wacc_ref):
        i = pl.program_id(0)
        glu_ref[HALO:, :] = u_ref[:, :D] * _sigmoid(u_ref[:, D:])
        glu_ref[:HALO, :] = jnp.where(i > 0, up_ref[:, :D] * _sigmoid(up_ref[:, D:]), 0.0)
        dpad_ref[:tm, :] = dc_ref[...]
        dpad_ref[tm:, :] = jnp.where(i < nt - 1, dcn_ref[...], 0.0)

        @pl.when(i == 0)
        def _():
            wacc_ref[...] = jnp.zeros_like(wacc_ref)

        def block(t0, ls):
            gwin = glu_ref[pl.ds(t0, CONV_ROWS + HALO), ls]
            dwin = dpad_ref[pl.ds(t0, CONV_ROWS + HALO), ls]
            dcur = dwin[:CONV_ROWS]
            dglu = jnp.zeros((CONV_ROWS, LANES), F32)
            for k in range(K):
                dglu = dglu + w_ref[k:k + 1, ls] * _shifted(dwin, (K - 1) - k, CONV_ROWS)
                gs = _shifted(gwin, HALO - (K - 1) + k, CONV_ROWS)
                wacc_ref[k * SUBLANES:(k + 1) * SUBLANES, ls] += _rowsum8(dcur * gs)
            dglu_ref[pl.ds(t0, CONV_ROWS), ls] = dglu

        _conv_loops(tm, D, block)
        dglu = dglu_ref[...]
        a = u_ref[:, :D]
        sg = _sigmoid(u_ref[:, D:])
        da = dglu * sg
        dg = dglu * a * (sg * (1.0 - sg))
        du_ref[:, :D] = da.astype(du_ref.dtype)
        du_ref[:, D:] = dg.astype(du_ref.dtype)
        pa = jnp.sum(da, axis=0, keepdims=True)
        pg = jnp.sum(dg, axis=0, keepdims=True)

        @pl.when(i == 0)
        def _():
            db_ref[:, :D] = pa
            db_ref[:, D:] = pg

        @pl.when(i > 0)
        def _():
            db_ref[:, :D] += pa
            db_ref[:, D:] += pg

        @pl.when(i == nt - 1)
        def _():
            for k in range(K):
                dw_ref[k:k + 1, :] = jnp.sum(wacc_ref[k * SUBLANES:(k + 1) * SUBLANES, :], axis=0, keepdims=True)

    (du, dw, db), xo = _call(
        body, name, (nt,),
        [pl.BlockSpec((tm, D), lambda i: (i, 0)), _next_halo_spec(tm, D, T),
         pl.BlockSpec((tm, D2), lambda i: (i, 0)), _prev_halo_spec(tm, D2), pl.BlockSpec((K, D), lambda i: (0, 0))],
        [pl.BlockSpec((tm, D2), lambda i: (i, 0)), pl.BlockSpec((K, D), lambda i: (0, 0)), pl.BlockSpec((1, D2), lambda i: (0, 0))],
        [jax.ShapeDtypeStruct((T, D2), BF16), jax.ShapeDtypeStruct((K, D), F32), jax.ShapeDtypeStruct((1, D2), F32)],
        [dcu, dcu, u, u, w], ("arbitrary",),
        [pltpu.VMEM((tm + HALO, D), F32), pltpu.VMEM((tm + HALO, D), F32), pltpu.VMEM((tm, D), F32),
         pltpu.VMEM((K * SUBLANES, D), F32)], hosted=hosted)
    return du, dw, db, xo


def mm_cols(a, w, name, hosted=()):
    T, K = a.shape
    S, _, n = w.shape
    tm = _tile(T, WIDE_TOKEN_TILE)

    def body(a_ref, w_ref, o_ref):
        o_ref[...] = jnp.dot(a_ref[...], w_ref[...], preferred_element_type=F32)

    in_specs = [pl.BlockSpec((tm, K), lambda s, i: (i, 0)), pl.BlockSpec((None, K, n), lambda s, i: (s, 0, 0))]
    (out,), xo = _call(body, name, (S, T // tm), in_specs, [pl.BlockSpec((tm, n), lambda s, i: (i, s))],
                       [jax.ShapeDtypeStruct((T, S * n), F32)], [a, w], ("parallel", "parallel"), hosted=hosted)
    return out, xo


def rms_mm_cols(h, gain, w, bias, name, hosted=()):
    T, K = h.shape
    S, _, n = w.shape
    tm = _tile(T, TOKEN_TILE)

    def body(h_ref, gain_ref, w_ref, b_ref, n_ref, o_ref):
        x = h_ref[...]
        r = lax.rsqrt(jnp.mean(x * x, axis=-1, keepdims=True) + RMS_EPS)
        a = (x * r * gain_ref[...]).astype(n_ref.dtype)
        n_ref[...] = a
        for s in range(S):
            cols = slice(s * n, (s + 1) * n)
            o_ref[:, cols] = jnp.dot(a, w_ref[s], preferred_element_type=F32) + b_ref[:, cols]

    row = pl.BlockSpec((tm, K), lambda i: (i, 0))
    (n_out, out), xo = _call(
        body, name, (T // tm,),
        [row, pl.BlockSpec((1, K), lambda i: (0, 0)), pl.BlockSpec((S, K, n), lambda i: (0, 0, 0)),
         pl.BlockSpec((1, S * n), lambda i: (0, 0))],
        [row, pl.BlockSpec((tm, S * n), lambda i: (i, 0))],
        [jax.ShapeDtypeStruct((T, K), BF16), jax.ShapeDtypeStruct((T, S * n), F32)],
        [h, gain, w, bias], ("parallel",), hosted=hosted)
    return n_out, out, xo


def _load_weights(pairs, sems, S, G, i, p):
    def copies(seg):
        return [pltpu.make_async_copy(src.at[seg], dst.at[seg], sems.at[k, seg]) for k, (src, dst) in enumerate(pairs)]

    @pl.when((i == 0) & (p == 0))
    def _():
        for seg in range(S):
            for cp in copies(seg):
                cp.start()

    @pl.when((i == 0) & (p < S // G))
    def _():
        for j in range(G):
            for cp in copies(G * p + j):
                cp.wait()


def ffn_fwd(h, gain, weights, name, hosted=(), arriving=None):
    T, D = h.shape
    S, f, _ = weights[0].shape
    tm = _tile(T, TOKEN_TILE)
    rc = tm // FFN_ROW_CHUNKS
    chunks = [slice(r * rc, (r + 1) * rc) for r in range(FFN_ROW_CHUNKS)]
    G = FFN_FWD_SEGS_PER_STEP
    weights = list(weights)
    hosted = ([arriving.awaited_first()] if arriving is not None else []) + list(hosted)

    def body(h_ref, gain_ref, *refs):
        nw = len(weights)
        wg_hbm, wu_hbm, wd_hbm = list(refs[:nw]) + list(refs[nw + 9:])
        n_ref, g_ref, u_ref, gu_ref, o_ref, wg_v, wu_v, wd_v, sems = refs[nw:nw + 9]
        i, p = pl.program_id(0), pl.program_id(1)
        _load_weights([(wg_hbm, wg_v), (wu_hbm, wu_v), (wd_hbm, wd_v)], sems, S, G, i, p)

        @pl.when(p == 0)
        def _():
            x = h_ref[...]
            r = lax.rsqrt(jnp.mean(x * x, axis=-1, keepdims=True) + RMS_EPS)
            n_ref[...] = (x * r * gain_ref[...]).astype(n_ref.dtype)

        parts = []
        for rows in chunks:
            a = n_ref[rows, :]
            acc = None
            for j in range(G):
                seg = G * p + j
                g = lax.dot_general(a, wg_v[seg], _NT, preferred_element_type=F32)
                u = lax.dot_general(a, wu_v[seg], _NT, preferred_element_type=F32)
                gu = (g * _sigmoid(g) * u).astype(gu_ref.dtype)
                g_ref[j, rows, :] = g.astype(g_ref.dtype)
                u_ref[j, rows, :] = u.astype(u_ref.dtype)
                gu_ref[j, rows, :] = gu
                part = jnp.dot(gu, wd_v[seg], preferred_element_type=F32)
                acc = part if acc is None else acc + part
            parts.append(acc)

        @pl.when(p == 0)
        def _():
            for rows, part in zip(chunks, parts):
                o_ref[rows, :] = h_ref[rows, :] + part

        @pl.when(p > 0)
        def _():
            for rows, part in zip(chunks, parts):
                o_ref[rows, :] += part

    row = pl.BlockSpec((tm, D), lambda i, p: (i, 0))
    seg = pl.BlockSpec((G, tm, f), lambda i, p: (p, i, 0))
    hbm = pl.BlockSpec(memory_space=pl.ANY)
    segs = jax.ShapeDtypeStruct((S, T, f), BF16)
    outs, xo = _call(
        body, name, (T // tm, S // G),
        [row, pl.BlockSpec((1, D), lambda i, s: (0, 0))] + [hbm] * len(weights), [row, seg, seg, seg, row],
        [jax.ShapeDtypeStruct((T, D), BF16), segs, segs, segs, jax.ShapeDtypeStruct((T, D), F32)],
        [h, gain] + weights, ("arbitrary", "arbitrary"),
        [pltpu.VMEM((S, f, D), BF16), pltpu.VMEM((S, f, D), BF16), pltpu.VMEM((S, f, D), BF16), pltpu.SemaphoreType.DMA((3, S))],
        hosted=hosted)
    return (*outs, xo)


def ffn_bwd(dy, h, gain, g, u, wd, wg, wu, name, hosted=()):
    T, D = h.shape
    S, f, _ = wg.shape
    tm = _tile(T, FFN_BWD_TOKEN_TILE)
    nt = T // tm

    def body(dy_ref, h_ref, gain_ref, g_ref, u_ref, wd_hbm, wg_hbm, wu_hbm, dg_ref, du_ref, dh_ref, dhb_ref, dgain_ref,
             wd_v, wg_v, wu_v, sems):
        i = pl.program_id(0)
        _load_weights([(wd_hbm, wd_v), (wg_hbm, wg_v), (wu_hbm, wu_v)], sems, S, S, i, 0)
        dy_ = dy_ref[...]
        dyb = dy_.astype(BF16)
        dn = None
        for j in range(S):
            dgu = lax.dot_general(dyb, wd_v[j], _NT, preferred_element_type=F32)
            gv = g_ref[j].astype(F32)
            sg = _sigmoid(gv)
            dg = (dgu * u_ref[j].astype(F32) * (sg * (1.0 + gv * (1.0 - sg)))).astype(dg_ref.dtype)
            du = (dgu * (gv * sg)).astype(du_ref.dtype)
            dg_ref[j] = dg
            du_ref[j] = du
            part = jnp.dot(dg, wg_v[j], preferred_element_type=F32) + jnp.dot(du, wu_v[j], preferred_element_type=F32)
            dn = part if dn is None else dn + part
        x = h_ref[...]
        r = lax.rsqrt(jnp.mean(x * x, axis=-1, keepdims=True) + RMS_EPS)
        xhat = x * r
        dxhat = dn * gain_ref[...]
        dh = dy_ + r * (dxhat - xhat * jnp.mean(dxhat * xhat, axis=-1, keepdims=True))
        dh_ref[...] = dh
        dhb_ref[...] = dh.astype(dhb_ref.dtype)
        pg = jnp.sum(dn * xhat, axis=0, keepdims=True)

        @pl.when(i == 0)
        def _():
            dgain_ref[...] = pg

        @pl.when(i > 0)
        def _():
            dgain_ref[...] += pg

    row = pl.BlockSpec((tm, D), lambda i: (i, 0))
    vec = pl.BlockSpec((1, D), lambda i: (0, 0))
    seg = pl.BlockSpec((S, tm, f), lambda i: (0, i, 0))
    hbm = pl.BlockSpec(memory_space=pl.ANY)
    segs = jax.ShapeDtypeStruct((S, T, f), BF16)
    outs, xo = _call(
        body, name, (nt,),
        [row, row, vec, seg, seg, hbm, hbm, hbm], [seg, seg, row, row, vec],
        [segs, segs, jax.ShapeDtypeStruct((T, D), F32), jax.ShapeDtypeStruct((T, D), BF16), jax.ShapeDtypeStruct((1, D), F32)],
        [dy, h, gain, g, u, wd, wg, wu], ("arbitrary",),
        [pltpu.VMEM((S, f, D), BF16), pltpu.VMEM((S, f, D), BF16), pltpu.VMEM((S, f, D), BF16),
         pltpu.SemaphoreType.DMA((3, S))], hosted=hosted)
    return (*outs, xo)


_NT = (((1,), (1,)), ((), ()))
_TN = (((0,), (0,)), ((), ()))


def nt_cols_rms(dy, w, h, gain, dres, name, hosted=(), also_bf16=False):
    T, K = h.shape
    S, _, n = w.shape
    tm = _tile(T, TOKEN_TILE)

    def body(dy_ref, w_ref, h_ref, gain_ref, dres_ref, dh_ref, dgain_ref, *rest):
        i = pl.program_id(0)
        dn = None
        for s in range(S):
            part = lax.dot_general(dy_ref[:, s * n:(s + 1) * n], w_ref[s], _NT, preferred_element_type=F32)
            dn = part if dn is None else dn + part
        x = h_ref[...]
        r = lax.rsqrt(jnp.mean(x * x, axis=-1, keepdims=True) + RMS_EPS)
        xhat = x * r
        dxhat = dn * gain_ref[...]
        dh = dres_ref[...] + r * (dxhat - xhat * jnp.mean(dxhat * xhat, axis=-1, keepdims=True))
        dh_ref[...] = dh
        if also_bf16:
            rest[0][...] = dh.astype(BF16)
        pg = jnp.sum(dn * xhat, axis=0, keepdims=True)

        @pl.when(i == 0)
        def _():
            dgain_ref[...] = pg

        @pl.when(i > 0)
        def _():
            dgain_ref[...] += pg

    row = pl.BlockSpec((tm, K), lambda i: (i, 0))
    vec = pl.BlockSpec((1, K), lambda i: (0, 0))
    out_specs, out_shape = [row, vec], [jax.ShapeDtypeStruct((T, K), F32), jax.ShapeDtypeStruct((1, K), F32)]
    if also_bf16:
        out_specs, out_shape = out_specs + [row], out_shape + [jax.ShapeDtypeStruct((T, K), BF16)]
    outs, xo = _call(
        body, name, (T // tm,),
        [pl.BlockSpec((tm, S * n), lambda i: (i, 0)), pl.BlockSpec((S, K, n), lambda i: (0, 0, 0)), row, vec, row],
        out_specs, out_shape, [dy, w, h, gain, dres], ("arbitrary",), hosted=hosted)
    return (*outs, xo)


def tn_grad(a, dy, S, a_by_seg, name, hosted=()):
    T = dy.shape[0]
    tt = _tile(T, GRAD_TOKEN_TILE)
    G = GRAD_SEGS_PER_STEP
    if a_by_seg:
        R, C = a.shape[2], dy.shape[1]
        a_spec = pl.BlockSpec((G, tt, R), lambda p, t: (p, t, 0))
        b_spec = pl.BlockSpec((tt, C), lambda p, t: (t, 0))
    else:
        R, C = a.shape[1], dy.shape[1] // S
        a_spec = pl.BlockSpec((tt, R), lambda p, t: (t, 0))
        b_spec = pl.BlockSpec((tt, G * C), lambda p, t: (t, p))
    Rh = R // 2
    nt = T // tt

    def body(a_ref, b_ref, o_ref, acc_ref):
        t = pl.program_id(1)
        parts = []
        for j in range(G):
            a_j = a_ref[j] if a_by_seg else a_ref[...]
            b_j = b_ref[...] if a_by_seg else b_ref[:, j * C:(j + 1) * C]
            parts.append(lax.dot_general(a_j, b_j.astype(BF16), _TN, preferred_element_type=F32))

        @pl.when(t == 0)
        def _():
            for j in range(G):
                acc_ref[j] = parts[j]

        @pl.when(t > 0)
        def _():
            for j in range(G):
                acc_ref[j] += parts[j]

        @pl.when(t == nt - 1)
        def _():
            for j in range(G):
                o_ref[0, j] = acc_ref[j, :Rh, :].astype(o_ref.dtype)
                o_ref[1, j] = acc_ref[j, Rh:, :].astype(o_ref.dtype)

    (gh,), xo = _call(
        body, name, (S // G, nt), [a_spec, b_spec], [pl.BlockSpec((2, G, Rh, C), lambda p, t: (0, p, 0, 0))],
        [jax.ShapeDtypeStruct((2, S, Rh, C), BF16)], [a, dy], ("parallel", "arbitrary"), [pltpu.VMEM((G, R, C), F32)],
        hosted=hosted)
    return gh, xo


def tn_grad_square(a, dy, S, name, hosted=()):
    T, K = a.shape
    N = dy.shape[1]
    tt = _tile(T, GRAD_TOKEN_TILE)
    nt = T // tt
    Rh = K // S // 2

    def body(a_ref, b_ref, o_ref, acc_ref):
        t = pl.program_id(0)
        part = lax.dot_general(a_ref[...], b_ref[...].astype(BF16), _TN, preferred_element_type=F32)

        @pl.when(t == 0)
        def _():
            acc_ref[...] = part

        @pl.when(t > 0)
        def _():
            acc_ref[...] += part

        @pl.when(t == nt - 1)
        def _():
            for s in range(S):
                for hf in range(2):
                    r0 = (2 * s + hf) * Rh
                    o_ref[hf, s] = acc_ref[r0:r0 + Rh, :].astype(o_ref.dtype)

    (gh,), xo = _call(
        body, name, (nt,), [pl.BlockSpec((tt, K), lambda t: (t, 0)), pl.BlockSpec((tt, N), lambda t: (t, 0))],
        [pl.BlockSpec((2, S, Rh, N), lambda t: (0, 0, 0, 0))], [jax.ShapeDtypeStruct((2, S, Rh, N), BF16)],
        [a, dy], ("arbitrary",), [pltpu.VMEM((K, N), F32)], hosted=hosted)
    return gh, xo


def _place():
    x, y, c = lax.axis_index("x"), lax.axis_index("y"), lax.axis_index("c")
    chips = [(1 - x, y), (x, 1 - y), (1 - x, 1 - y)]
    return x, y, c, chips


def _remote(src, dst, send_sem, recv_sem, dev):
    return pltpu.make_async_remote_copy(src_ref=src, dst_ref=dst, send_sem=send_sem, recv_sem=recv_sem,
                                        device_id=dev, device_id_type=MESH)


def small_allreduce(v, name, hosted=()):
    rows, W = v.shape

    def body(v_ref, o_ref, sib_ref, pair_ref, chips_ref, send_sems, recv_sems):
        x, y, c, chips = _place()
        me = 2 * x + y
        swap = _remote(v_ref, sib_ref, send_sems.at[3], recv_sems.at[3], (x, y, 1 - c))
        swap.start()
        swap.wait()
        mine, other = v_ref[...], sib_ref[...]
        pair_ref[...] = jnp.where(c == 0, mine, other) + jnp.where(c == 0, other, mine)
        sends = []
        for j, (px, py) in enumerate(chips):
            cp = _remote(pair_ref, chips_ref.at[me], send_sems.at[j], recv_sems.at[j], (px, py, c))
            cp.start()
            sends.append(cp)
        chips_ref[me] = pair_ref[...]
        for j, (px, py) in enumerate(chips):
            blk = chips_ref.at[2 * px + py]
            _remote(blk, blk, send_sems.at[j], recv_sems.at[j], (px, py, c)).wait_recv()
        for cp in sends:
            cp.wait_send()
        o_ref[...] = (chips_ref[0] + chips_ref[1]) + (chips_ref[2] + chips_ref[3])

    vm = pl.BlockSpec(memory_space=pltpu.VMEM)
    (out,), xo = _call(
        body, name, (), [vm], [vm], [jax.ShapeDtypeStruct((rows, W), F32)], [v], (),
        [pltpu.VMEM((rows, W), F32), pltpu.VMEM((rows, W), F32), pltpu.VMEM((N_CHIPS, rows, W), F32),
         pltpu.SemaphoreType.DMA((4,)), pltpu.SemaphoreType.DMA((4,))], hosted=hosted)
    return out, xo


def _gather_p1_copies(srcs, bufs, ssem, rsem, base):
    x, y, c, chips = _place()
    me, sib = 2 * x + y, (x, y, 1 - c)
    sends, recvs = [], []
    for k, (src, buf) in enumerate(zip(srcs, bufs)):
        rh = src.shape[0] // 2
        s0 = base + 4 * k
        sends.append(_remote(src, buf.at[me], ssem.at[s0 + 3], rsem.at[s0 + 3], sib))
        recvs.append(_remote(buf.at[me], buf.at[me], ssem.at[s0 + 3], rsem.at[s0 + 3], sib))
        for j, (px, py) in enumerate(chips):
            sends.append(_remote(src.at[pl.ds(c * rh, rh)], buf.at[me, pl.ds(c * rh, rh)], ssem.at[s0 + j], rsem.at[s0 + j], (px, py, c)))
            blk = buf.at[2 * px + py, pl.ds(c * rh, rh)]
            recvs.append(_remote(blk, blk, ssem.at[s0 + j], rsem.at[s0 + j], (px, py, c)))
    return sends, recvs


def _gather_p2_copies(bufs, ssem, rsem, base):
    x, y, c, chips = _place()
    sib = (x, y, 1 - c)
    sends, recvs = [], []
    for k, buf in enumerate(bufs):
        rh = buf.shape[1] // 2
        for j, (px, py) in enumerate(chips):
            s0 = base + 3 * k + j
            blk = buf.at[2 * px + py, pl.ds(c * rh, rh)]
            sends.append(_remote(blk, blk, ssem.at[s0], rsem.at[s0], sib))
            got = buf.at[2 * px + py, pl.ds((1 - c) * rh, rh)]
            recvs.append(_remote(got, got, ssem.at[s0], rsem.at[s0], sib))
    return sends, recvs


def _gathered_shape(s):
    return jax.ShapeDtypeStruct((N_CHIPS,) + s.shape, s.dtype)


def gather_p1(shards):
    return _Exchange(shards, [_gathered_shape(s) for s in shards], {}, 4 * len(shards),
                     lambda xi, xo, ss, rs: _gather_p1_copies(xi, xo, ss, rs, 0))


def gather_p2(bufs):
    return _Exchange(bufs, [jax.ShapeDtypeStruct(b.shape, b.dtype) for b in bufs], {k: k for k in range(len(bufs))},
                     3 * len(bufs), lambda xi, xo, ss, rs: _gather_p2_copies(xo, ss, rs, 0))


def gather_whole(whole, begun):
    nw, n = len(whole), len(whole) + len(begun)
    shards = list(whole) + list(begun)
    return _Exchange(shards, [_gathered_shape(s) for s in shards], {}, 4 * n + 3 * nw,
                     lambda xi, xo, ss, rs: _gather_p1_copies(xi, xo, ss, rs, 0),
                     then=lambda xi, xo, ss, rs: _gather_p2_copies(xo[:nw], ss, rs, 4 * n))


def gather_small(v):
    def copies(xi, xo, ssem, rsem):
        x, y, c, chips = _place()
        me, sib = 2 * x + y, (x, y, 1 - c)
        sends = [_remote(xi[0], xo[0].at[me], ssem.at[3], rsem.at[3], sib)]
        recvs = [_remote(xo[0].at[me], xo[0].at[me], ssem.at[3], rsem.at[3], sib)]
        for j, (px, py) in enumerate(chips):
            sends.append(_remote(xi[0], xo[0].at[me], ssem.at[j], rsem.at[j], (px, py, c)))
            blk = xo[0].at[2 * px + py]
            recvs.append(_remote(blk, blk, ssem.at[j], rsem.at[j], (px, py, c)))
        return sends, recvs

    return _Exchange([v], [_gathered_shape(v)], {}, 4, copies)


def gather_all(v):
    def copies(xi, xo, ssem, rsem):
        x, y, c, _ = _place()
        sends, recvs = [], []
        for m in range(1, N_DEV):
            px, py, pc = (1 - x) if m & 4 else x, (1 - y) if m & 2 else y, (1 - c) if m & 1 else c
            sends.append(_remote(xi[0], xo[0].at[4 * x + 2 * y + c], ssem.at[m - 1], rsem.at[m - 1], (px, py, pc)))
            blk = xo[0].at[4 * px + 2 * py + pc]
            recvs.append(_remote(blk, blk, ssem.at[m - 1], rsem.at[m - 1], (px, py, pc)))
        return sends, recvs

    return _Exchange([v], [jax.ShapeDtypeStruct((N_DEV,) + v.shape, v.dtype)], {}, N_DEV - 1, copies)


def run_exchanges(exchanges, name):
    return _call(lambda: None, name, (), [], [], [], [], (), hosted=exchanges)[1]


def sibling_halves(grads):
    def copies(xi, xo, ssem, rsem):
        x, y, c, _ = _place()
        sends = [_remote(xi[k].at[1 - c], xo[k], ssem.at[k], rsem.at[k], (x, y, 1 - c)) for k in range(len(grads))]
        return sends, sends

    return _Exchange(grads, [jax.ShapeDtypeStruct(g.shape[1:], g.dtype) for g in grads], {}, len(grads), copies)


def pair_sum(ghs, recvs, cidx, name):
    n = len(ghs)
    S = ghs[0].shape[1]

    def body(c_ref, *refs):
        for k in range(n):
            a_ref, b_ref, o_ref = refs[2 * k], refs[2 * k + 1], refs[2 * n + k]
            o_ref[...] = (a_ref[...].astype(F32) + b_ref[...].astype(F32)).astype(o_ref.dtype)

    in_specs, out_specs, out_shape, args = [], [], [], []
    for gh, recv in zip(ghs, recvs):
        _, _, Rh, C = gh.shape
        in_specs += [pl.BlockSpec((None, None, Rh, C), lambda s, c_ref: (c_ref[0], s, 0, 0)),
                     pl.BlockSpec((None, Rh, C), lambda s, c_ref: (s, 0, 0))]
        out_specs.append(pl.BlockSpec((None, Rh, C), lambda s, c_ref: (s, 0, 0)))
        out_shape.append(jax.ShapeDtypeStruct((S, Rh, C), BF16))
        args += [gh, recv]
    return pl.pallas_call(
        body, name=name, out_shape=out_shape,
        grid_spec=pltpu.PrefetchScalarGridSpec(num_scalar_prefetch=1, grid=(S,), in_specs=in_specs, out_specs=out_specs),
        compiler_params=_params(("parallel",)),
    )(cidx, *args)


def scatter_p1(parts):
    def copies(xi, xo, ssem, rsem):
        x, y, c, chips = _place()
        me, sib = 2 * x + y, (x, y, 1 - c)
        sends, recvs = [], []
        for k in range(len(parts)):
            s0 = 4 * k
            sends.append(_remote(xi[k].at[me], xo[k].at[me, c], ssem.at[s0 + 3], rsem.at[s0 + 3], sib))
            own = xo[k].at[me, 1 - c]
            recvs.append(_remote(own, own, ssem.at[s0 + 3], rsem.at[s0 + 3], sib))
            for j, (px, py) in enumerate(chips):
                sends.append(_remote(xi[k].at[2 * px + py], xo[k].at[me, c], ssem.at[s0 + j], rsem.at[s0 + j], (px, py, c)))
                blk = xo[k].at[2 * px + py, c]
                recvs.append(_remote(blk, blk, ssem.at[s0 + j], rsem.at[s0 + j], (px, py, c)))
        return sends, recvs

    return _Exchange(parts, [jax.ShapeDtypeStruct((p.shape[0], 2) + p.shape[1:], p.dtype) for p in parts], {},
                     4 * len(parts), copies)


def scatter_p2(bufs):
    def copies(xi, xo, ssem, rsem):
        x, y, c, chips = _place()
        sib = (x, y, 1 - c)
        sends, recvs = [], []
        for k in range(len(bufs)):
            for j, (px, py) in enumerate(chips):
                s0 = 3 * k + j
                blk = xo[k].at[2 * px + py, c]
                sends.append(_remote(blk, blk, ssem.at[s0], rsem.at[s0], sib))
                got = xo[k].at[2 * px + py, 1 - c]
                recvs.append(_remote(got, got, ssem.at[s0], rsem.at[s0], sib))
        return sends, recvs

    return _Exchange(bufs, [jax.ShapeDtypeStruct(b.shape, b.dtype) for b in bufs], {k: k for k in range(len(bufs))},
                     3 * len(bufs), copies)


def _adamw_math(w, g, m, v):
    m = ADAM_B1 * m + (1.0 - ADAM_B1) * g
    v = ADAM_B2 * v + (1.0 - ADAM_B2) * (g * g)
    m_hat = m / (1.0 - ADAM_B1 ** ADAM_STEP)
    v_hat = v / (1.0 - ADAM_B2 ** ADAM_STEP)
    delta = -ADAM_LR * (m_hat / (jnp.sqrt(v_hat) + ADAM_EPS) + ADAM_WD * w)
    return delta, m, v


def adamw_reduce(tensors, place, lyr, bases, name):
    n = len(tensors)
    L, R, C = tensors[0][0].shape
    Rh = R // 2
    rb = _tile(Rh, ROW_TILE, 2 * SUBLANES)
    nb = Rh // rb

    def body(place_ref, *refs):
        mine = (place_ref[1] == pl.program_id(0))
        for k in range(n):
            p_ref, b0, b1, b2, b3, w_ref, m_ref, v_ref = refs[8 * k:8 * k + 8]
            go_ref, d_ref, mo_ref, vo_ref = refs[len(refs) - 4 * n + 4 * k:len(refs) - 4 * n + 4 * k + 4]
            g = None
            for p, b in enumerate((b0, b1, b2, b3)):
                val = jnp.where(mine & (place_ref[0] == p), p_ref[...], b[...]).astype(F32)
                g = val if g is None else g + val
            d, mn, vn = _adamw_math(w_ref[...], g, m_ref[...], v_ref[...])
            go_ref[...] = g
            d_ref[...] = d
            mo_ref[...] = mn
            vo_ref[...] = vn

    def buf_spec(p):
        def idx(h, i, pr):
            own = (pr[0] == p) & (pr[1] == h)
            return (p, jnp.where(own, 1 - h, h), i, 0)
        return pl.BlockSpec((None, None, rb, C), idx)

    blk = pl.BlockSpec((None, rb, C), lambda h, i, pr: (lyr, h * nb + i, 0))
    in_specs, args = [], []
    for w, m, v, buf, part in tensors:
        in_specs += [pl.BlockSpec((None, rb, C), lambda h, i, pr: (pr[0], i, 0))] + [buf_spec(p) for p in range(N_CHIPS)] + [blk] * 3
        args += [part, buf, buf, buf, buf, w, m, v]
    aliases = {}
    if bases is not None:
        in_specs += [pl.BlockSpec(memory_space=pl.ANY)] * (4 * n)
        aliases = {len(args) + k: k for k in range(4 * n)}
        args += list(bases)
    shp = jax.ShapeDtypeStruct((L, R, C), F32)
    flat = _call(body, name, (2, nb), in_specs, [blk] * (4 * n), [shp] * (4 * n), args, ("parallel", "parallel"),
                 prefetch=[place], own_aliases=aliases)[0]
    return flat


def small_update(late, early, own, place, entries, loss_row, name):
    ne = len(entries)
    D = late.shape[1]

    def body(place_ref, late_ref, early_ref, own_ref, *refs):
        ins, outs = refs[:3 * ne], refs[3 * ne:]
        ch = place_ref[0]
        me = 2 * place_ref[0] + place_ref[1]

        def early_sum(rs, cs):
            acc = None
            for d in range(N_DEV):
                val = jnp.where(me == d, own_ref[rs, cs], early_ref[d, rs, cs])
                acc = val if acc is None else acc + val
            return acc

        outs[4 * ne][...] = early_sum(slice(loss_row, loss_row + 1), slice(0, D))
        for e, (source, row0, kind, w, _, _) in enumerate(entries):
            r, width = w.shape[0], w.shape[-1]
            gsum = early_sum if source == "early" else (lambda rs, cs: late_ref[rs, cs])

            if kind == "full":
                g = gsum(slice(row0, row0 + r), slice(0, D))
            elif kind in ("cols", "rows"):
                g = gsum(slice(row0, row0 + r), slice(0, width))
                for q in range(1, N_CHIPS):
                    g = jnp.where(ch == q, gsum(slice(row0, row0 + r), slice(q * width, (q + 1) * width)), g)
            else:
                per_row = D // width
                g = gsum(slice(row0, row0 + 1), slice(0, width))
                for q in range(1, N_CHIPS):
                    rr = row0 + q // per_row
                    cc = (q % per_row) * width
                    g = jnp.where(ch == q, gsum(slice(rr, rr + 1), slice(cc, cc + width)), g)
            for j in ([slice(None)] if kind != "rows" else range(r)):
                gj = g if kind != "rows" else g[j:j + 1, :]
                d, mn, vn = _adamw_math(ins[3 * e][j], gj, ins[3 * e + 1][j], ins[3 * e + 2][j])
                outs[4 * e][j] = gj
                outs[4 * e + 1][j] = d
                outs[4 * e + 2][j] = mn
                outs[4 * e + 3][j] = vn

    vm = pl.BlockSpec(memory_space=pltpu.VMEM)
    args, out_shape = [], []
    for _, _, _, w, m, v in entries:
        args += [w, m, v]
        out_shape += [jax.ShapeDtypeStruct(w.shape, F32)] * 4
    out_shape.append(jax.ShapeDtypeStruct((1, D), F32))
    return pl.pallas_call(
        body, name=name,
        in_specs=[pl.BlockSpec(memory_space=pltpu.SMEM), vm, vm, vm] + [vm] * (3 * ne),
        out_specs=[vm] * (4 * ne + 1), out_shape=out_shape,
        compiler_params=pltpu.CompilerParams(vmem_limit_bytes=VMEM_LIMIT),
    )(place, late, early, own, *args)


def _pack_rows(items, width, name):
    starts, at = [], 0
    for it in items:
        starts.append(at)
        at += -(-it.shape[0] // SUBLANES) * SUBLANES
    total = at

    def body(*refs):
        o_ref = refs[-1]
        o_ref[...] = jnp.zeros_like(o_ref)
        for it_ref, r0 in zip(refs[:-1], starts):
            if len(it_ref.shape) == 3:
                for j in range(it_ref.shape[0]):
                    o_ref[r0 + j:r0 + j + 1, :] = it_ref[j]
            else:
                o_ref[r0:r0 + it_ref.shape[0], :] = it_ref[...]

    vm = pl.BlockSpec(memory_space=pltpu.VMEM)
    packed = pl.pallas_call(body, name=name, in_specs=[vm] * len(items), out_specs=vm,
                            out_shape=jax.ShapeDtypeStruct((total, width), F32))(*items)
    return packed, starts


def kernel(x, a_norm, a_w_in, a_conv, a_w_out, b_norm, b_w_pw1, b_b_pw1, b_conv, b_b_conv, b_ln_g, b_ln_b, b_w_pw2, b_b_pw2, ffn_norm, ffn_w_gate, ffn_w_up, ffn_w_down, final_norm, loss_target, m_a_norm, m_a_w_in, m_a_conv, m_a_w_out, m_b_norm, m_b_w_pw1, m_b_b_pw1, m_b_conv, m_b_b_conv, m_b_ln_g, m_b_ln_b, m_b_w_pw2, m_b_b_pw2, m_ffn_norm, m_ffn_w_gate, m_ffn_w_up, m_ffn_w_down, m_final_norm, v_a_norm, v_a_w_in, v_a_conv, v_a_w_out, v_b_norm, v_b_w_pw1, v_b_b_pw1, v_b_conv, v_b_b_conv, v_b_ln_g, v_b_ln_b, v_b_w_pw2, v_b_b_pw2, v_ffn_norm, v_ffn_w_gate, v_ffn_w_up, v_ffn_w_down, v_final_norm):
    T, D = x.shape[1], x.shape[2]
    Dq = D // N_CHIPS
    cx, cy, cc = lax.axis_index("x"), lax.axis_index("y"), lax.axis_index("c")
    chip = (2 * cx + cy).astype(jnp.int32).reshape(1)
    cidx = cc.astype(jnp.int32).reshape(1)
    h0 = x.reshape(T, D)
    tgt = loss_target.reshape(T, D)

    rows3 = lambda t: jnp.swapaxes(t, 0, 1)
    small_shards = [rows3(a_conv), b_norm, b_b_pw1.reshape(2, Dq), rows3(b_conv), b_b_conv, b_ln_g, b_ln_b, b_b_pw2]
    packed, st = _pack_rows(small_shards, Dq, "pack_small")

    tr = lambda t: jnp.swapaxes(t, 1, 2)
    w_gate, m_gate, v_gate = tr(ffn_w_gate), tr(m_ffn_w_gate), tr(v_ffn_w_gate)
    w_up, m_up, v_up = tr(ffn_w_up), tr(m_ffn_w_up), tr(v_ffn_w_up)
    bf = lambda t: t.astype(BF16)
    s_in, s_out, s_pw1, s_pw2 = bf(a_w_in[0]), bf(a_w_out[0]), bf(b_w_pw1[0]), bf(b_w_pw2[0])
    s_gate, s_up, s_down = [bf(w_gate[l]) for l in (0, 1)], [bf(w_up[l]) for l in (0, 1)], [bf(ffn_w_down[l]) for l in (0, 1)]

    n0, (g_in,) = rms_fwd(h0, a_norm, "rms_a", hosted=[gather_whole([s_in], [])])
    bcv, (g_out, gate0, sw) = mm_cols(n0, g_in, "mm_w_in", hosted=[gather_p1([s_out, s_gate[0]]), gather_small(packed)])

    def whole(k, r):
        return jnp.transpose(sw[:, st[k]:st[k] + r, :], (1, 0, 2)).reshape(r, D)

    a_conv_f, b_norm_f = whole(0, 3), whole(1, 1)
    b_b_pw1_f = sw[:, st[2]:st[2] + 2, :].reshape(1, 2 * D)
    b_conv_f, b_b_conv_f, b_ln_g_f, b_ln_b_f, b_b_pw2_f = whole(3, b_conv.shape[1]), whole(4, 1), whole(5, 1), whole(6, 1), whole(7, 1)
    ya, h1, (g_out, up0, down0, gate0) = gateconv_fwd(bcv, a_conv_f, gather_p2([g_out]), h0, "gateconv_fwd",
                                                      hosted=[gather_p1([s_up[0], s_down[0]]), gather_p2([gate0])])
    g_out = g_out.reshape(1, D, D)
    n1, fg0, fu0, gu0, h2, (up0, down0, g_pw1, g_pw2, gate1, up1) = ffn_fwd(
        h1, ffn_norm[0:1], [gate0], "ffn_fwd0", arriving=gather_p2([up0, down0]),
        hosted=[gather_whole([s_pw1, s_pw2], [s_gate[1], s_up[1]])])
    g_pw2 = g_pw2.reshape(1, D, D)
    n2, ub, (down1, gate1, up1) = rms_mm_cols(h2, b_norm_f, g_pw1, b_b_pw1_f, "mm_pw1",
                                              hosted=[gather_p1([s_down[1]]), gather_p2([gate1, up1])])
    cu, sb, h3, (down1,) = bconv_fwd(ub, b_conv_f, b_b_conv_f, b_ln_g_f, b_ln_b_f, g_pw2, b_b_pw2_f, h2, "bconv_fwd",
                                     hosted=[gather_p2([down1])])
    n3, fg1, fu1, gu1, h4, _ = ffn_fwd(h3, ffn_norm[1:2], [gate1, up1, down1], "ffn_fwd1")
    loss_part, dh4, dh4_b, d_final = loss_head(h4, final_norm.reshape(1, D), tgt, "loss_head")

    place = jnp.concatenate([chip, cidx])

    def pair_sums(ghs, from_sib, tags):
        return pair_sum(ghs, from_sib, cidx, "pair_sum_" + "_".join(tags))

    def upd(wmvs, bufs, parts, tag):
        flat = None
        for lyr in range(len(bufs[0])):
            tensors = [(w, m, v, b[lyr], p[lyr]) for (w, m, v), b, p in zip(wmvs, bufs, parts)]
            flat = adamw_reduce(tensors, place, lyr, flat, "adamw_%s%d" % (tag, lyr))
        return [flat[4 * k:4 * k + 4] for k in range(len(wmvs))]

    dg1, du1, dh3, dh3_b, d_fn1, _ = ffn_bwd(dh4, h3, ffn_norm[1:2], fg1, fu1, down1, gate1, up1, "ffn_bwd1")
    gh_down1, _ = tn_grad(gu1, dh4_b, N_CHIPS, True, "tn_down1")
    gh_gate1, _ = tn_grad(dg1, n3, N_CHIPS, True, "tn_gate1")
    gh_up1, _ = tn_grad(du1, n3, N_CHIPS, True, "tn_up1")
    f1 = [gh_gate1, gh_up1, gh_down1]

    dcu, d_ln_g, d_ln_b, d_b_conv, d_b_pw2, sib_f1 = pw2_ln_bwd(dh3, g_pw2, cu, b_ln_g_f, b_ln_b_f, "pw2_ln_bwd",
                                                                hosted=[sibling_halves(f1)])
    p_f1 = pair_sums(f1, sib_f1, ["gate1", "up1", "down1"])
    gh_pw2, _ = tn_grad_square(sb, dh3_b, N_CHIPS, "tn_pw2")
    dub, d_bconv_w, d_b_pw1, buf_f1 = bconv_bwd(dcu, ub, b_conv_f, "bconv_bwd", hosted=[scatter_p1(p_f1)])
    gh_pw1, _ = tn_grad(n2, dub, N_CHIPS, False, "tn_pw1")
    b_grp = [gh_pw1, gh_pw2]
    dh2, d_b_norm, dh2_b, (*buf_f1, sib_pw1, sib_pw2) = nt_cols_rms(
        dub, g_pw1, h2, b_norm_f, dh3, "nt_pw1", hosted=[scatter_p2(buf_f1), sibling_halves(b_grp)], also_bf16=True)
    sib_b = [sib_pw1, sib_pw2]
    p_b = pair_sums(b_grp, sib_b, ["pw1", "pw2"])

    early_grads = [d_b_norm, d_b_pw1.reshape(2, D), d_bconv_w, d_b_conv, d_ln_g, d_ln_b, d_b_pw2, d_fn1, d_final,
                   jnp.broadcast_to(loss_part, (1, D))]
    epacked, es = _pack_rows(early_grads, D, "pack_small_grads_early")
    dg0, du0, dh1, dh1_b, d_fn0, (*buf_b, eall) = ffn_bwd(dh2, h1, ffn_norm[0:1], fg0, fu0, down0, gate0, up0, "ffn_bwd0",
                                                         hosted=[scatter_p1(p_b), gather_all(epacked)])
    gh_down0, _ = tn_grad(gu0, dh2_b, N_CHIPS, True, "tn_down0")
    gh_gate0, (*buf_b, sib_down0) = tn_grad(dg0, n1, N_CHIPS, True, "tn_gate0",
                                            hosted=[scatter_p2(buf_b), sibling_halves([gh_down0])])
    p_down0 = pair_sums([gh_down0], [sib_down0], ["down0"])
    gh_up0, (buf_down0, sib_gate0) = tn_grad(du0, n1, N_CHIPS, True, "tn_up0",
                                             hosted=[scatter_p1(p_down0), sibling_halves([gh_gate0])])
    p_gate0 = pair_sums([gh_gate0], [sib_gate0], ["gate0"])
    gh_out, (buf_down0, sib_up0) = tn_grad_square(ya, dh1_b, N_CHIPS, "tn_w_out",
                                                  hosted=[scatter_p2([buf_down0]), sibling_halves([gh_up0])])
    p_up0 = pair_sums([gh_up0], [sib_up0], ["up0"])
    dbcv, d_aconv_w, (buf_gate0, sib_out) = gateconv_bwd(dh1_b, g_out, bcv, a_conv_f, "gateconv_bwd",
                                                         hosted=[scatter_p1(p_gate0), sibling_halves([gh_out])])
    p_out = pair_sums([gh_out], [sib_out], ["out"])
    gh_in, (buf_up0, buf_out, buf_gate0) = tn_grad(n0, dbcv, N_CHIPS, False, "tn_w_in",
                                                   hosted=[scatter_p1(p_up0 + p_out), scatter_p2([buf_gate0])])
    sib_in = run_exchanges([sibling_halves([gh_in])], "reduce_in_siblings")
    p_in = pair_sums([gh_in], sib_in, ["in"])
    grad_x, d_a_norm, (buf_in, buf_up0, buf_out) = nt_cols_rms(
        dbcv, g_in, h0, a_norm, dh1, "nt_w_in", hosted=[scatter_p1(p_in), scatter_p2([buf_up0, buf_out])])
    p_f0 = [p_gate0[0], p_up0[0], p_down0[0]]

    lpacked, ls = _pack_rows([d_a_norm, d_aconv_w, d_fn0], D, "pack_small_grads_late")
    lall, (buf_in,) = small_allreduce(lpacked, "allreduce_small_grads", hosted=[scatter_p2([buf_in])])
    buf_a, p_a = [buf_in, buf_out], [p_in[0], p_out[0]]

    r_gate, r_up = upd([(w_gate, m_gate, v_gate), (w_up, m_up, v_up)],
                       [[buf_gate0, buf_f1[0]], [buf_up0, buf_f1[1]]], [[p_f0[0], p_f1[0]], [p_f0[1], p_f1[1]]], "gate_up")
    (r_down,) = upd([(ffn_w_down, m_ffn_w_down, v_ffn_w_down)], [[buf_down0, buf_f1[2]]], [[p_f0[2], p_f1[2]]], "down")
    r_gate, r_up = [tr(t) for t in r_gate], [tr(t) for t in r_up]
    (r_pw1,) = upd([(b_w_pw1, m_b_w_pw1, v_b_w_pw1)], [[buf_b[0]]], [[p_b[0]]], "pw1")
    r_pw2, r_out = upd([(b_w_pw2, m_b_w_pw2, v_b_w_pw2), (a_w_out, m_a_w_out, v_a_w_out)],
                       [[buf_b[1]], [buf_a[1]]], [[p_b[1]], [p_a[1]]], "pw2_out")
    (r_in,) = upd([(a_w_in, m_a_w_in, v_a_w_in)], [[buf_a[0]]], [[p_a[0]]], "w_in")
    entries = [
        ("late", ls[0], "full", a_norm, m_a_norm, v_a_norm),
        ("late", ls[1], "rows", rows3(a_conv), rows3(m_a_conv), rows3(v_a_conv)),
        ("early", es[0], "cols", b_norm, m_b_norm, v_b_norm),
        ("early", es[1], "flat2", b_b_pw1, m_b_b_pw1, v_b_b_pw1),
        ("early", es[2], "rows", rows3(b_conv), rows3(m_b_conv), rows3(v_b_conv)),
        ("early", es[3], "cols", b_b_conv, m_b_b_conv, v_b_b_conv),
        ("early", es[4], "cols", b_ln_g, m_b_ln_g, v_b_ln_g),
        ("early", es[5], "cols", b_ln_b, m_b_ln_b, v_b_ln_b),
        ("early", es[6], "cols", b_b_pw2, m_b_b_pw2, v_b_b_pw2),
        ("late", ls[2], "full", ffn_norm[0:1], m_ffn_norm[0:1], v_ffn_norm[0:1]),
        ("early", es[7], "full", ffn_norm[1:2], m_ffn_norm[1:2], v_ffn_norm[1:2]),
        ("early", es[8], "full", final_norm.reshape(1, D), m_final_norm.reshape(1, D), v_final_norm.reshape(1, D)),
    ]
    so = small_update(lall, eall, epacked, place, entries, es[9], "small_update")
    sm = [so[4 * e:4 * e + 4] for e in range(len(entries))]

    def shaped(e, like):
        return [t.reshape(like.shape) for t in sm[e]]

    r_a_norm, r_a_conv, r_b_norm, r_b_b_pw1 = shaped(0, a_norm), shaped(1, a_conv), shaped(2, b_norm), shaped(3, b_b_pw1)
    r_b_conv, r_b_b_conv, r_b_ln_g, r_b_ln_b = shaped(4, b_conv), shaped(5, b_b_conv), shaped(6, b_ln_g), shaped(7, b_ln_b)
    r_b_b_pw2, r_final = shaped(8, b_b_pw2), shaped(11, final_norm)
    r_ffn_norm = [jnp.concatenate([l0, l1], axis=0) for l0, l1 in zip(sm[9], sm[10])]

    loss = so[4 * len(entries)][0, 0]
    order =[r_a_norm, r_in, r_a_conv, r_out, r_b_norm, r_pw1, r_b_b_pw1, r_b_conv, r_b_b_conv, r_b_ln_g, r_b_ln_b,
             r_pw2, r_b_b_pw2, r_ffn_norm, r_gate, r_up, r_down, r_final]
    outs = [loss, grad_x.reshape(x.shape)]
    for field in range(4):
        outs += [r[field] for r in order]
    return tuple(outs)
```

```python
import functools

import jax
import jax.numpy as jnp
from jax import lax
from jax.experimental import pallas as pl
from jax.experimental.pallas import tpu as pltpu

RMS_EPS = 1e-6
LN_EPS = 1e-5
ADAM_LR = 0.001
ADAM_B1 = 0.9
ADAM_B2 = 0.999
ADAM_EPS = 1e-08
ADAM_WD = 0.01
ADAM_STEP = 10

N_CHIPS = 4
N_DEV = 8
LANES = 128
SUBLANES = 8
HALO = 32
CONV_ROWS = 64
TOKEN_TILE = 512
WIDE_TOKEN_TILE = 1024
GRAD_TOKEN_TILE = 2048
GRAD_SEGS_PER_STEP = 2
FFN_ROW_CHUNKS = 2
FFN_FWD_SEGS_PER_STEP = 4
FFN_BWD_TOKEN_TILE = 256
ROW_TILE = 256
VMEM_LIMIT = 56 * 1024 * 1024
MESH = pl.DeviceIdType.MESH
BF16 = jnp.bfloat16
F32 = jnp.float32


def _tile(n, pref, mult=SUBLANES):
    t = min(n, pref) // mult * mult
    while n % t:
        t -= mult
    return t


def _params(sem):
    return pltpu.CompilerParams(dimension_semantics=sem, vmem_limit_bytes=VMEM_LIMIT)


def _sigmoid(x):
    return 0.5 * jnp.tanh(0.5 * x) + 0.5


class _Exchange:
    def __init__(self, ins, outs, aliases, n_sems, copies, then=None):
        self.ins, self.outs, self.aliases, self.n_sems, self.copies = list(ins), list(outs), dict(aliases), n_sems, copies
        self.then = then
        self.early = False

    def awaited_first(self):
        self.early = True
        return self

    def start(self, xi, xo, ssem, rsem):
        for cp in self.copies(xi, xo, ssem, rsem)[0]:
            cp.start()

    def finish(self, xi, xo, ssem, rsem):
        sends, recvs = self.copies(xi, xo, ssem, rsem)
        for cp in recvs:
            cp.wait_recv()
        if self.then is not None:
            sends2, recvs2 = self.then(xi, xo, ssem, rsem)
            for cp in sends2:
                cp.start()
            for cp in recvs2:
                cp.wait_recv()
            sends = sends + sends2
        for cp in sends:
            cp.wait_send()


def _call(body, name, grid, in_specs, out_specs, out_shape, args, sem, scratch_shapes=(), hosted=(), prefetch=(),
          own_aliases=None):
    in_specs, out_specs, out_shape = list(in_specs), list(out_specs), list(out_shape)
    scratch_shapes, hosted, prefetch = list(scratch_shapes), list(hosted), list(prefetch)
    n_pre, n_in, n_out, n_scr = len(prefetch), len(args), len(out_shape), len(scratch_shapes)
    x_in = [a for ex in hosted for a in ex.ins]
    x_out = [o for ex in hosted for o in ex.outs]
    aliases = {n_pre + i: o for i, o in (own_aliases or {}).items()}
    at_in, at_out = n_pre + n_in, n_out
    for ex in hosted:
        for i, o in ex.aliases.items():
            aliases[at_in + i] = at_out + o
        at_in += len(ex.ins)
        at_out += len(ex.outs)
    sems = [pltpu.SemaphoreType.DMA((ex.n_sems,)) for ex in hosted for _ in range(2)]

    def wrapped(*refs):
        pre, refs = refs[:n_pre], refs[n_pre:]
        ins, xi = refs[:n_in], refs[n_in:n_in + len(x_in)]
        refs = refs[n_in + len(x_in):]
        outs, xo = refs[:n_out], refs[n_out:n_out + len(x_out)]
        refs = refs[n_out + len(x_out):]
        scr, sm = refs[:n_scr], refs[n_scr:]
        views, a, b = [], 0, 0
        for e, ex in enumerate(hosted):
            views.append((xi[a:a + len(ex.ins)], xo[b:b + len(ex.outs)], sm[2 * e], sm[2 * e + 1]))
            a += len(ex.ins)
            b += len(ex.outs)
        first = last = None
        for ax, g in enumerate(grid):
            f, l = pl.program_id(ax) == 0, pl.program_id(ax) == g - 1
            first, last = (f, l) if first is None else (first & f, last & l)

        def begin():
            for ex, v in zip(hosted, views):
                ex.start(*v)
            for ex, v in zip(hosted, views):
                if ex.early:
                    ex.finish(*v)

        def end():
            for ex, v in zip(hosted, views):
                if not ex.early:
                    ex.finish(*v)

        if hosted and grid:
            pl.when(first)(begin)
        elif hosted:
            begin()
        early_refs = [r for ex, v in zip(hosted, views) if ex.early for r in v[1]]
        body(*pre, *ins, *outs, *scr, *early_refs)
        if hosted and grid:
            pl.when(last)(end)
        elif hosted:
            end()

    hbm = pl.BlockSpec(memory_space=pl.ANY)
    all_in, all_out = in_specs + [hbm] * len(x_in), out_specs + [hbm] * len(x_out)
    kw = dict(name=name, out_shape=out_shape + x_out, input_output_aliases=aliases,
              compiler_params=_params(tuple("arbitrary" for _ in grid) if hosted else sem))
    if prefetch:
        kw["grid_spec"] = pltpu.PrefetchScalarGridSpec(num_scalar_prefetch=n_pre, grid=grid, in_specs=all_in,
                                                       out_specs=all_out, scratch_shapes=scratch_shapes + sems)
    else:
        kw.update(grid=grid, in_specs=all_in, out_specs=all_out, scratch_shapes=scratch_shapes + sems)
    res = pl.pallas_call(wrapped, **kw)(*prefetch, *args, *x_in)
    return list(res[:n_out]), list(res[n_out:])


def rms_fwd(h, gain, name, hosted=()):
    T, D = h.shape
    tm = _tile(T, TOKEN_TILE)

    def body(h_ref, g_ref, o_ref):
        x = h_ref[...]
        r = lax.rsqrt(jnp.mean(x * x, axis=-1, keepdims=True) + RMS_EPS)
        o_ref[...] = (x * r * g_ref[...]).astype(o_ref.dtype)

    (n,), xo = _call(
        body, name, (T // tm,),
        [pl.BlockSpec((tm, D), lambda i: (i, 0)), pl.BlockSpec((1, D), lambda i: (0, 0))],
        [pl.BlockSpec((tm, D), lambda i: (i, 0))], [jax.ShapeDtypeStruct((T, D), BF16)],
        [h, gain], ("parallel",), hosted=hosted)
    return n, xo


def loss_head(h, gain, tgt, name):
    T, D = h.shape
    tm = _tile(T, TOKEN_TILE)

    def body(h_ref, g_ref, t_ref, loss_ref, dh_ref, dhb_ref, dg_ref):
        i = pl.program_id(0)
        x = h_ref[...]
        g = g_ref[...]
        r = lax.rsqrt(jnp.mean(x * x, axis=-1, keepdims=True) + RMS_EPS)
        xhat = x * r
        diff = xhat * g - t_ref[...]
        part_loss = 0.5 * jnp.sum(jnp.mean(diff * diff, axis=-1, keepdims=True), axis=0, keepdims=True)
        dy = diff * (1.0 / D)
        dxhat = dy * g
        dh = r * (dxhat - xhat * jnp.mean(dxhat * xhat, axis=-1, keepdims=True))
        dh_ref[...] = dh
        dhb_ref[...] = dh.astype(dhb_ref.dtype)
        part = jnp.sum(dy * xhat, axis=0, keepdims=True)

        @pl.when(i == 0)
        def _():
            dg_ref[...] = part
            loss_ref[...] = part_loss

        @pl.when(i > 0)
        def _():
            dg_ref[...] += part
            loss_ref[...] += part_loss

    row = pl.BlockSpec((tm, D), lambda i: (i, 0))
    vec = pl.BlockSpec((1, D), lambda i: (0, 0))
    return pl.pallas_call(
        body, name=name, grid=(T // tm,),
        in_specs=[row, vec, row],
        out_specs=[pl.BlockSpec((1, 1), lambda i: (0, 0)), row, row, vec],
        out_shape=[jax.ShapeDtypeStruct((1, 1), F32), jax.ShapeDtypeStruct((T, D), F32),
                   jax.ShapeDtypeStruct((T, D), BF16), jax.ShapeDtypeStruct((1, D), F32)],
        compiler_params=_params(("arbitrary",)),
    )(h, gain, tgt)


def _prev_halo_spec(tm, width):
    return pl.BlockSpec((HALO, width), lambda i: (jnp.maximum(i * (tm // HALO) - 1, 0), 0))


def _next_halo_spec(tm, width, T):
    return pl.BlockSpec((HALO, width), lambda i: (jnp.minimum((i + 1) * (tm // HALO), T // HALO - 1), 0))


def _shifted(win, off, rows):
    if off % SUBLANES == 0:
        return win[off:off + rows]
    n = win.shape[0]
    return pltpu.roll(win, (n - off) % n, 0)[:rows]


def _rowsum8(x):
    acc = x[0:SUBLANES]
    for q in range(1, x.shape[0] // SUBLANES):
        acc = acc + x[q * SUBLANES:(q + 1) * SUBLANES]
    return acc


def _conv_loops(tm, D, per_block):
    def chunk(r, carry):
        t0 = pl.multiple_of(r * CONV_ROWS, CONV_ROWS)
        for lb in range(D // LANES):
            per_block(t0, slice(lb * LANES, (lb + 1) * LANES))
        return carry

    lax.fori_loop(0, tm // CONV_ROWS, chunk, 0)


def gateconv_fwd(bcv, w, w_out, res, name, hosted=()):
    T, D3 = bcv.shape
    D = D3 // 3
    K = w.shape[0]
    tm = _tile(T, TOKEN_TILE)
    wo_shape = w_out.outs[0].shape

    def body(x_ref, halo_ref, w_ref, res_ref, y_ref, h_ref, pad_ref, wo_v, sem, wo_hbm):
        i = pl.program_id(0)

        @pl.when(i == 0)
        def _():
            cp = pltpu.make_async_copy(wo_hbm, wo_v, sem)
            cp.start()
            cp.wait()

        pad_ref[HALO:, :] = x_ref[:, D:2 * D] * x_ref[:, 2 * D:]
        pad_ref[:HALO, :] = jnp.where(i > 0, halo_ref[:, D:2 * D] * halo_ref[:, 2 * D:], 0.0)

        def block(t0, ls):
            win = pad_ref[pl.ds(t0, CONV_ROWS + HALO), ls]
            acc = jnp.zeros((CONV_ROWS, LANES), F32)
            for k in range(K):
                acc = acc + w_ref[k:k + 1, ls] * _shifted(win, HALO - (K - 1) + k, CONV_ROWS)
            y_ref[pl.ds(t0, CONV_ROWS), ls] = (x_ref[pl.ds(t0, CONV_ROWS), ls] * acc).astype(y_ref.dtype)

        _conv_loops(tm, D, block)
        h_ref[...] = res_ref[...] + jnp.dot(y_ref[...], wo_v[...].reshape(D, D), preferred_element_type=F32)

    row = pl.BlockSpec((tm, D), lambda i: (i, 0))
    (y, h), xo = _call(
        body, name, (T // tm,),
        [pl.BlockSpec((tm, D3), lambda i: (i, 0)), _prev_halo_spec(tm, D3), pl.BlockSpec((K, D), lambda i: (0, 0)), row],
        [row, row], [jax.ShapeDtypeStruct((T, D), BF16), jax.ShapeDtypeStruct((T, D), F32)],
        [bcv, bcv, w, res], ("arbitrary",),
        [pltpu.VMEM((tm + HALO, D), F32), pltpu.VMEM(wo_shape, BF16), pltpu.SemaphoreType.DMA],
        hosted=[w_out.awaited_first()] + list(hosted))
    return y, h, xo


def gateconv_bwd(dh, w_out, bcv, w, name, hosted=()):
    T, D3 = bcv.shape
    D = D3 // 3
    K = w.shape[0]
    tm = _tile(T, TOKEN_TILE)
    nt = T // tm

    def body(dh_ref, dhn_ref, wo_ref, x_ref, xp_ref, xn_ref, w_ref, o_ref, dw_ref, cv_ref, dc_ref, wacc_ref, dy_ref):
        i = pl.program_id(0)
        dy_ref[...] = lax.dot_general(dh_ref[...], wo_ref[0], _NT, preferred_element_type=F32)
        dyn = lax.dot_general(dhn_ref[...], wo_ref[0], _NT, preferred_element_type=F32)
        cv_ref[HALO:, :] = x_ref[:, D:2 * D] * x_ref[:, 2 * D:]
        cv_ref[:HALO, :] = jnp.where(i > 0, xp_ref[:, D:2 * D] * xp_ref[:, 2 * D:], 0.0)
        dc_ref[:tm, :] = dy_ref[...] * x_ref[:, :D]
        dc_ref[tm:, :] = jnp.where(i < nt - 1, dyn * xn_ref[:, :D], 0.0)

        @pl.when(i == 0)
        def _():
            wacc_ref[...] = jnp.zeros_like(wacc_ref)

        def block(t0, ls):
            cwin = cv_ref[pl.ds(t0, CONV_ROWS + HALO), ls]
            dwin = dc_ref[pl.ds(t0, CONV_ROWS + HALO), ls]
            dcon = dwin[:CONV_ROWS]
            conv = jnp.zeros((CONV_ROWS, LANES), F32)
            dcv = jnp.zeros((CONV_ROWS, LANES), F32)
            for k in range(K):
                wk = w_ref[k:k + 1, ls]
                cs = _shifted(cwin, HALO - (K - 1) + k, CONV_ROWS)
                conv = conv + wk * cs
                dcv = dcv + wk * _shifted(dwin, (K - 1) - k, CONV_ROWS)
                wacc_ref[k * SUBLANES:(k + 1) * SUBLANES, ls] += _rowsum8(dcon * cs)
            rows = pl.ds(t0, CONV_ROWS)
            o_ref[rows, ls] = (dy_ref[rows, ls] * conv).astype(o_ref.dtype)
            o_ref[rows, pl.ds(D + ls.start, LANES)] = (dcv * x_ref[rows, pl.ds(2 * D + ls.start, LANES)]).astype(o_ref.dtype)
            o_ref[rows, pl.ds(2 * D + ls.start, LANES)] = (dcv * x_ref[rows, pl.ds(D + ls.start, LANES)]).astype(o_ref.dtype)

        _conv_loops(tm, D, block)

        @pl.when(i == nt - 1)
        def _():
            for k in range(K):
                dw_ref[k:k + 1, :] = jnp.sum(wacc_ref[k * SUBLANES:(k + 1) * SUBLANES, :], axis=0, keepdims=True)

    (dx, dw), xo = _call(
        body, name, (nt,),
        [pl.BlockSpec((tm, D), lambda i: (i, 0)), _next_halo_spec(tm, D, T), pl.BlockSpec((1, D, D), lambda i: (0, 0, 0)),
         pl.BlockSpec((tm, D3), lambda i: (i, 0)), _prev_halo_spec(tm, D3), _next_halo_spec(tm, D3, T),
         pl.BlockSpec((K, D), lambda i: (0, 0))],
        [pl.BlockSpec((tm, D3), lambda i: (i, 0)), pl.BlockSpec((K, D), lambda i: (0, 0))],
        [jax.ShapeDtypeStruct((T, D3), BF16), jax.ShapeDtypeStruct((K, D), F32)],
        [dh, dh, w_out, bcv, bcv, bcv, w], ("arbitrary",),
        [pltpu.VMEM((tm + HALO, D), F32), pltpu.VMEM((tm + HALO, D), F32), pltpu.VMEM((K * SUBLANES, D), F32),
         pltpu.VMEM((tm, D), F32)], hosted=hosted)
    return dx, dw, xo


def bconv_fwd(u, w, b_conv, ln_g, ln_b, w_out, b_out, res, name, hosted=()):
    T, D2 = u.shape
    D = D2 // 2
    K = w.shape[0]
    tm = _tile(T, TOKEN_TILE)

    def body(u_ref, halo_ref, w_ref, bc_ref, g_ref, b_ref, wo_ref, bo_ref, res_ref, cu_ref, s_ref, h_ref, pad_ref):
        i = pl.program_id(0)
        pad_ref[HALO:, :] = u_ref[:, :D] * _sigmoid(u_ref[:, D:])
        pad_ref[:HALO, :] = jnp.where(i > 0, halo_ref[:, :D] * _sigmoid(halo_ref[:, D:]), 0.0)

        def block(t0, ls):
            win = pad_ref[pl.ds(t0, CONV_ROWS + HALO), ls]
            acc = jnp.zeros((CONV_ROWS, LANES), F32)
            for k in range(K):
                acc = acc + w_ref[k:k + 1, ls] * _shifted(win, HALO - (K - 1) + k, CONV_ROWS)
            cu_ref[pl.ds(t0, CONV_ROWS), ls] = acc + bc_ref[:, ls]

        _conv_loops(tm, D, block)
        cu = cu_ref[...]
        mu = jnp.mean(cu, axis=-1, keepdims=True)
        xc = cu - mu
        rstd = lax.rsqrt(jnp.mean(xc * xc, axis=-1, keepdims=True) + LN_EPS)
        ln = xc * rstd * g_ref[...] + b_ref[...]
        s = (ln * _sigmoid(ln)).astype(s_ref.dtype)
        s_ref[...] = s
        h_ref[...] = res_ref[...] + bo_ref[...] + jnp.dot(s, wo_ref[0], preferred_element_type=F32)

    vec = pl.BlockSpec((1, D), lambda i: (0, 0))
    row = pl.BlockSpec((tm, D), lambda i: (i, 0))
    (cu, s, h), xo = _call(
        body, name, (T // tm,),
        [pl.BlockSpec((tm, D2), lambda i: (i, 0)), _prev_halo_spec(tm, D2), pl.BlockSpec((K, D), lambda i: (0, 0)), vec, vec, vec,
         pl.BlockSpec((1, D, D), lambda i: (0, 0, 0)), vec, row],
        [row, row, row], [jax.ShapeDtypeStruct((T, D), F32), jax.ShapeDtypeStruct((T, D), BF16), jax.ShapeDtypeStruct((T, D), F32)],
        [u, u, w, b_conv, ln_g, ln_b, w_out, b_out, res], ("parallel",), [pltpu.VMEM((tm + HALO, D), F32)], hosted=hosted)
    return cu, s, h, xo


def pw2_ln_bwd(dy, w, cu, ln_g, ln_b, name, hosted=()):
    T, D = cu.shape
    tm = _tile(T, TOKEN_TILE)

    def body(dy_ref, w_ref, cu_ref, g_ref, b_ref, dcu_ref, dg_ref, db_ref, dbc_ref, dbo_ref):
        i = pl.program_id(0)
        dy_ = dy_ref[...]
        ds = lax.dot_general(dy_.astype(BF16), w_ref[0], _NT, preferred_element_type=F32)
        cu_ = cu_ref[...]
        mu = jnp.mean(cu_, axis=-1, keepdims=True)
        xc = cu_ - mu
        rstd = lax.rsqrt(jnp.mean(xc * xc, axis=-1, keepdims=True) + LN_EPS)
        xh = xc * rstd
        ln = xh * g_ref[...] + b_ref[...]
        sg = _sigmoid(ln)
        dl = ds * (sg * (1.0 + ln * (1.0 - sg)))
        dxh = dl * g_ref[...]
        dcu = rstd * (dxh - jnp.mean(dxh, axis=-1, keepdims=True) - xh * jnp.mean(dxh * xh, axis=-1, keepdims=True))
        dcu_ref[...] = dcu
        pg = jnp.sum(dl * xh, axis=0, keepdims=True)
        pb = jnp.sum(dl, axis=0, keepdims=True)
        pc = jnp.sum(dcu, axis=0, keepdims=True)
        po = jnp.sum(dy_, axis=0, keepdims=True)

        @pl.when(i == 0)
        def _():
            dg_ref[...] = pg
            db_ref[...] = pb
            dbc_ref[...] = pc
            dbo_ref[...] = po

        @pl.when(i > 0)
        def _():
            dg_ref[...] += pg
            db_ref[...] += pb
            dbc_ref[...] += pc
            dbo_ref[...] += po

    vec = pl.BlockSpec((1, D), lambda i: (0, 0))
    row = pl.BlockSpec((tm, D), lambda i: (i, 0))
    vshape = jax.ShapeDtypeStruct((1, D), F32)
    outs, xo = _call(
        body, name, (T // tm,), [row, pl.BlockSpec((1, D, D), lambda i: (0, 0, 0)), row, vec, vec], [row, vec, vec, vec, vec],
        [jax.ShapeDtypeStruct((T, D), F32), vshape, vshape, vshape, vshape], [dy, w, cu, ln_g, ln_b], ("arbitrary",),
        hosted=hosted)
    return (*outs, xo)


def bconv_bwd(dcu, u, w, name, hosted=()):
    T, D2 = u.shape
    D = D2 // 2
    K = w.shape[0]
    tm = _tile(T, TOKEN_TILE)
    nt = T // tm

    def body(dc_ref, dcn_ref, u_ref, up_ref, w_ref, du_ref, dw_ref, db_ref, glu_ref, dpad_ref, dglu_ref, wacc_ref):
        i = pl.program_id(0)
        glu_ref[HALO:, :] = u_ref[:, :D] * _sigmoid(u_ref[:, D:])
        glu_ref[:HALO, :] = jnp.where(i > 0, up_ref[:, :D] * _sigmoid(up_ref[:, D:]), 0.0)
        dpad_ref[:tm, :] = dc_ref[...]
        dpad_ref[tm:, :] = jnp.where(i < nt - 1, dcn_ref[...], 0.0)

        @pl.when(i == 0)
        def _():
            wacc_ref[...] = jnp.zeros_like(wacc_ref)

        def block(t0, ls):
            gwin = glu_ref[pl.ds(t0, CONV_ROWS + HALO), ls]
            dwin = dpad_ref[pl.ds(t0, CONV_ROWS + HALO), ls]
            dcur = dwin[:CONV_ROWS]
            dglu = jnp.zeros((CONV_ROWS, LANES), F32)
            for k in range(K):
                dglu = dglu + w_ref[k:k + 1, ls] * _shifted(dwin, (K - 1) - k, CONV_ROWS)
                gs = _shifted(gwin, HALO - (K - 1) + k, CONV_ROWS)
                wacc_ref[k * SUBLANES:(k + 1) * SUBLANES, ls] += _rowsum8(dcur * gs)
            dglu_ref[pl.ds(t0, CONV_ROWS), ls] = dglu

        _conv_loops(tm, D, block)
        dglu = dglu_ref[...]
        a = u_ref[:, :D]
        sg = _sigmoid(u_ref[:, D:])
        da = dglu * sg
        dg = dglu * a * (sg * (1.0 - sg))
        du_ref[:, :D] = da.astype(du_ref.dtype)
        du_ref[:, D:] = dg.astype(du_ref.dtype)
        pa = jnp.sum(da, axis=0, keepdims=True)
        pg = jnp.sum(dg, axis=0, keepdims=True)

        @pl.when(i == 0)
        def _():
            db_ref[:, :D] = pa
            db_ref[:, D:] = pg

        @pl.when(i > 0)
        def _():
            db_ref[:, :D] += pa
            db_ref[:, D:] += pg

        @pl.when(i == nt - 1)
        def _():
            for k in range(K):
                dw_ref[k:k + 1, :] = jnp.sum(wacc_ref[k * SUBLANES:(k + 1) * SUBLANES, :], axis=0, keepdims=True)

    (du, dw, db), xo = _call(
        body, name, (nt,),
        [pl.BlockSpec((tm, D), lambda i: (i, 0)), _next_halo_spec(tm, D, T),
         pl.BlockSpec((tm, D2), lambda i: (i, 0)), _prev_halo_spec(tm, D2), pl.BlockSpec((K, D), lambda i: (0, 0))],
        [pl.BlockSpec((tm, D2), lambda i: (i, 0)), pl.BlockSpec((K, D), lambda i: (0, 0)), pl.BlockSpec((1, D2), lambda i: (0, 0))],
        [jax.ShapeDtypeStruct((T, D2), BF16), jax.ShapeDtypeStruct((K, D), F32), jax.ShapeDtypeStruct((1, D2), F32)],
        [dcu, dcu, u, u, w], ("arbitrary",),
        [pltpu.VMEM((tm + HALO, D), F32), pltpu.VMEM((tm + HALO, D), F32), pltpu.VMEM((tm, D), F32),
         pltpu.VMEM((K * SUBLANES, D), F32)], hosted=hosted)
    return du, dw, db, xo


def mm_cols(a, w, name, hosted=()):
    T, K = a.shape
    S, _, n = w.shape
    tm = _tile(T, WIDE_TOKEN_TILE)

    def body(a_ref, w_ref, o_ref):
        o_ref[...] = jnp.dot(a_ref[...], w_ref[...], preferred_element_type=F32)

    in_specs = [pl.BlockSpec((tm, K), lambda s, i: (i, 0)), pl.BlockSpec((None, K, n), lambda s, i: (s, 0, 0))]
    (out,), xo = _call(body, name, (S, T // tm), in_specs, [pl.BlockSpec((tm, n), lambda s, i: (i, s))],
                       [jax.ShapeDtypeStruct((T, S * n), F32)], [a, w], ("parallel", "parallel"), hosted=hosted)
    return out, xo


def rms_mm_cols(h, gain, w, bias, name, hosted=()):
    T, K = h.shape
    S, _, n = w.shape
    tm = _tile(T, TOKEN_TILE)

    def body(h_ref, gain_ref, w_ref, b_ref, n_ref, o_ref):
        x = h_ref[...]
        r = lax.rsqrt(jnp.mean(x * x, axis=-1, keepdims=True) + RMS_EPS)
        a = (x * r * gain_ref[...]).astype(n_ref.dtype)
        n_ref[...] = a
        for s in range(S):
            cols = slice(s * n, (s + 1) * n)
            o_ref[:, cols] = jnp.dot(a, w_ref[s], preferred_element_type=F32) + b_ref[:, cols]

    row = pl.BlockSpec((tm, K), lambda i: (i, 0))
    (n_out, out), xo = _call(
        body, name, (T // tm,),
        [row, pl.BlockSpec((1, K), lambda i: (0, 0)), pl.BlockSpec((S, K, n), lambda i: (0, 0, 0)),
         pl.BlockSpec((1, S * n), lambda i: (0, 0))],
        [row, pl.BlockSpec((tm, S * n), lambda i: (i, 0))],
        [jax.ShapeDtypeStruct((T, K), BF16), jax.ShapeDtypeStruct((T, S * n), F32)],
        [h, gain, w, bias], ("parallel",), hosted=hosted)
    return n_out, out, xo


def _load_weights(pairs, sems, S, G, i, p):
    def copies(seg):
        return [pltpu.make_async_copy(src.at[seg], dst.at[seg], sems.at[k, seg]) for k, (src, dst) in enumerate(pairs)]

    @pl.when((i == 0) & (p == 0))
    def _():
        for seg in range(S):
            for cp in copies(seg):
                cp.start()

    @pl.when((i == 0) & (p < S // G))
    def _():
        for j in range(G):
            for cp in copies(G * p + j):
                cp.wait()


def ffn_fwd(h, gain, weights, name, hosted=(), arriving=None):
    T, D = h.shape
    S, f, _ = weights[0].shape
    tm = _tile(T, TOKEN_TILE)
    rc = tm // FFN_ROW_CHUNKS
    chunks = [slice(r * rc, (r + 1) * rc) for r in range(FFN_ROW_CHUNKS)]
    G = FFN_FWD_SEGS_PER_STEP
    weights = list(weights)
    hosted = ([arriving.awaited_first()] if arriving is not None else []) + list(hosted)

    def body(h_ref, gain_ref, *refs):
        nw = len(weights)
        wg_hbm, wu_hbm, wd_hbm = list(refs[:nw]) + list(refs[nw + 9:])
        n_ref, g_ref, u_ref, gu_ref, o_ref, wg_v, wu_v, wd_v, sems = refs[nw:nw + 9]
        i, p = pl.program_id(0), pl.program_id(1)
        _load_weights([(wg_hbm, wg_v), (wu_hbm, wu_v), (wd_hbm, wd_v)], sems, S, G, i, p)

        @pl.when(p == 0)
        def _():
            x = h_ref[...]
            r = lax.rsqrt(jnp.mean(x * x, axis=-1, keepdims=True) + RMS_EPS)
            n_ref[...] = (x * r * gain_ref[...]).astype(n_ref.dtype)

        parts = []
        for rows in chunks:
            a = n_ref[rows, :]
            acc = None
            for j in range(G):
                seg = G * p + j
                g = lax.dot_general(a, wg_v[seg], _NT, preferred_element_type=F32)
                u = lax.dot_general(a, wu_v[seg], _NT, preferred_element_type=F32)
                gu = (g * _sigmoid(g) * u).astype(gu_ref.dtype)
                g_ref[j, rows, :] = g.astype(g_ref.dtype)
                u_ref[j, rows, :] = u.astype(u_ref.dtype)
                gu_ref[j, rows, :] = gu
                part = jnp.dot(gu, wd_v[seg], preferred_element_type=F32)
                acc = part if acc is None else acc + part
            parts.append(acc)

        @pl.when(p == 0)
        def _():
            for rows, part in zip(chunks, parts):
                o_ref[rows, :] = h_ref[rows, :] + part

        @pl.when(p > 0)
        def _():
            for rows, part in zip(chunks, parts):
                o_ref[rows, :] += part

    row = pl.BlockSpec((tm, D), lambda i, p: (i, 0))
    seg = pl.BlockSpec((G, tm, f), lambda i, p: (p, i, 0))
    hbm = pl.BlockSpec(memory_space=pl.ANY)
    segs = jax.ShapeDtypeStruct((S, T, f), BF16)
    outs, xo = _call(
        body, name, (T // tm, S // G),
        [row, pl.BlockSpec((1, D), lambda i, s: (0, 0))] + [hbm] * len(weights), [row, seg, seg, seg, row],
        [jax.ShapeDtypeStruct((T, D), BF16), segs, segs, segs, jax.ShapeDtypeStruct((T, D), F32)],
        [h, gain] + weights, ("arbitrary", "arbitrary"),
        [pltpu.VMEM((S, f, D), BF16), pltpu.VMEM((S, f, D), BF16), pltpu.VMEM((S, f, D), BF16), pltpu.SemaphoreType.DMA((3, S))],
        hosted=hosted)
    return (*outs, xo)


def ffn_bwd(dy, h, gain, g, u, wd, wg, wu, name, hosted=()):
    T, D = h.shape
    S, f, _ = wg.shape
    tm = _tile(T, FFN_BWD_TOKEN_TILE)
    nt = T // tm

    def body(dy_ref, h_ref, gain_ref, g_ref, u_ref, wd_hbm, wg_hbm, wu_hbm, dg_ref, du_ref, dh_ref, dhb_ref, dgain_ref,
             wd_v, wg_v, wu_v, sems):
        i = pl.program_id(0)
        _load_weights([(wd_hbm, wd_v), (wg_hbm, wg_v), (wu_hbm, wu_v)], sems, S, S, i, 0)
        dy_ = dy_ref[...]
        dyb = dy_.astype(BF16)
        dn = None
        for j in range(S):
            dgu = lax.dot_general(dyb, wd_v[j], _NT, preferred_element_type=F32)
            gv = g_ref[j].astype(F32)
            sg = _sigmoid(gv)
            dg = (dgu * u_ref[j].astype(F32) * (sg * (1.0 + gv * (1.0 - sg)))).astype(dg_ref.dtype)
            du = (dgu * (gv * sg)).astype(du_ref.dtype)
            dg_ref[j] = dg
            du_ref[j] = du
            part = jnp.dot(dg, wg_v[j], preferred_element_type=F32) + jnp.dot(du, wu_v[j], preferred_element_type=F32)
            dn = part if dn is None else dn + part
        x = h_ref[...]
        r = lax.rsqrt(jnp.mean(x * x, axis=-1, keepdims=True) + RMS_EPS)
        xhat = x * r
        dxhat = dn * gain_ref[...]
        dh = dy_ + r * (dxhat - xhat * jnp.mean(dxhat * xhat, axis=-1, keepdims=True))
        dh_ref[...] = dh
        dhb_ref[...] = dh.astype(dhb_ref.dtype)
        pg = jnp.sum(dn * xhat, axis=0, keepdims=True)

        @pl.when(i == 0)
        def _():
            dgain_ref[...] = pg

        @pl.when(i > 0)
        def _():
            dgain_ref[...] += pg

    row = pl.BlockSpec((tm, D), lambda i: (i, 0))
    vec = pl.BlockSpec((1, D), lambda i: (0, 0))
    seg = pl.BlockSpec((S, tm, f), lambda i: (0, i, 0))
    hbm = pl.BlockSpec(memory_space=pl.ANY)
    segs = jax.ShapeDtypeStruct((S, T, f), BF16)
    outs, xo = _call(
        body, name, (nt,),
        [row, row, vec, seg, seg, hbm, hbm, hbm], [seg, seg, row, row, vec],
        [segs, segs, jax.ShapeDtypeStruct((T, D), F32), jax.ShapeDtypeStruct((T, D), BF16), jax.ShapeDtypeStruct((1, D), F32)],
        [dy, h, gain, g, u, wd, wg, wu], ("arbitrary",),
        [pltpu.VMEM((S, f, D), BF16), pltpu.VMEM((S, f, D), BF16), pltpu.VMEM((S, f, D), BF16),
         pltpu.SemaphoreType.DMA((3, S))], hosted=hosted)
    return (*outs, xo)


_NT = (((1,), (1,)), ((), ()))
_TN = (((0,), (0,)), ((), ()))


def nt_cols_rms(dy, w, h, gain, dres, name, hosted=(), also_bf16=False):
    T, K = h.shape
    S, _, n = w.shape
    tm = _tile(T, TOKEN_TILE)

    def body(dy_ref, w_ref, h_ref, gain_ref, dres_ref, dh_ref, dgain_ref, *rest):
        i = pl.program_id(0)
        dn = None
        for s in range(S):
            part = lax.dot_general(dy_ref[:, s * n:(s + 1) * n], w_ref[s], _NT, preferred_element_type=F32)
            dn = part if dn is None else dn + part
        x = h_ref[...]
        r = lax.rsqrt(jnp.mean(x * x, axis=-1, keepdims=True) + RMS_EPS)
        xhat = x * r
        dxhat = dn * gain_ref[...]
        dh = dres_ref[...] + r * (dxhat - xhat * jnp.mean(dxhat * xhat, axis=-1, keepdims=True))
        dh_ref[...] = dh
        if also_bf16:
            rest[0][...] = dh.astype(BF16)
        pg = jnp.sum(dn * xhat, axis=0, keepdims=True)

        @pl.when(i == 0)
        def _():
            dgain_ref[...] = pg

        @pl.when(i > 0)
        def _():
            dgain_ref[...] += pg

    row = pl.BlockSpec((tm, K), lambda i: (i, 0))
    vec = pl.BlockSpec((1, K), lambda i: (0, 0))
    out_specs, out_shape = [row, vec], [jax.ShapeDtypeStruct((T, K), F32), jax.ShapeDtypeStruct((1, K), F32)]
    if also_bf16:
        out_specs, out_shape = out_specs + [row], out_shape + [jax.ShapeDtypeStruct((T, K), BF16)]
    outs, xo = _call(
        body, name, (T // tm,),
        [pl.BlockSpec((tm, S * n), lambda i: (i, 0)), pl.BlockSpec((S, K, n), lambda i: (0, 0, 0)), row, vec, row],
        out_specs, out_shape, [dy, w, h, gain, dres], ("arbitrary",), hosted=hosted)
    return (*outs, xo)


def tn_grad(a, dy, S, a_by_seg, name, hosted=()):
    T = dy.shape[0]
    tt = _tile(T, GRAD_TOKEN_TILE)
    G = GRAD_SEGS_PER_STEP
    if a_by_seg:
        R, C = a.shape[2], dy.shape[1]
        a_spec = pl.BlockSpec((G, tt, R), lambda p, t: (p, t, 0))
        b_spec = pl.BlockSpec((tt, C), lambda p, t: (t, 0))
    else:
        R, C = a.shape[1], dy.shape[1] // S
        a_spec = pl.BlockSpec((tt, R), lambda p, t: (t, 0))
        b_spec = pl.BlockSpec((tt, G * C), lambda p, t: (t, p))
    Rh = R // 2
    nt = T // tt

    def body(a_ref, b_ref, o_ref, acc_ref):
        t = pl.program_id(1)
        parts = []
        for j in range(G):
            a_j = a_ref[j] if a_by_seg else a_ref[...]
            b_j = b_ref[...] if a_by_seg else b_ref[:, j * C:(j + 1) * C]
            parts.append(lax.dot_general(a_j, b_j.astype(BF16), _TN, preferred_element_type=F32))

        @pl.when(t == 0)
        def _():
            for j in range(G):
                acc_ref[j] = parts[j]

        @pl.when(t > 0)
        def _():
            for j in range(G):
                acc_ref[j] += parts[j]

        @pl.when(t == nt - 1)
        def _():
            for j in range(G):
                o_ref[0, j] = acc_ref[j, :Rh, :].astype(o_ref.dtype)
                o_ref[1, j] = acc_ref[j, Rh:, :].astype(o_ref.dtype)

    (gh,), xo = _call(
        body, name, (S // G, nt), [a_spec, b_spec], [pl.BlockSpec((2, G, Rh, C), lambda p, t: (0, p, 0, 0))],
        [jax.ShapeDtypeStruct((2, S, Rh, C), BF16)], [a, dy], ("parallel", "arbitrary"), [pltpu.VMEM((G, R, C), F32)],
        hosted=hosted)
    return gh, xo


def tn_grad_square(a, dy, S, name, hosted=()):
    T, K = a.shape
    N = dy.shape[1]
    tt = _tile(T, GRAD_TOKEN_TILE)
    nt = T // tt
    Rh = K // S // 2

    def body(a_ref, b_ref, o_ref, acc_ref):
        t = pl.program_id(0)
        part = lax.dot_general(a_ref[...], b_ref[...].astype(BF16), _TN, preferred_element_type=F32)

        @pl.when(t == 0)
        def _():
            acc_ref[...] = part

        @pl.when(t > 0)
        def _():
            acc_ref[...] += part

        @pl.when(t == nt - 1)
        def _():
            for s in range(S):
                for hf in range(2):
                    r0 = (2 * s + hf) * Rh
                    o_ref[hf, s] = acc_ref[r0:r0 + Rh, :].astype(o_ref.dtype)

    (gh,), xo = _call(
        body, name, (nt,), [pl.BlockSpec((tt, K), lambda t: (t, 0)), pl.BlockSpec((tt, N), lambda t: (t, 0))],
        [pl.BlockSpec((2, S, Rh, N), lambda t: (0, 0, 0, 0))], [jax.ShapeDtypeStruct((2, S, Rh, N), BF16)],
        [a, dy], ("arbitrary",), [pltpu.VMEM((K, N), F32)], hosted=hosted)
    return gh, xo


def _place():
    x, y, c = lax.axis_index("x"), lax.axis_index("y"), lax.axis_index("c")
    chips = [(1 - x, y), (x, 1 - y), (1 - x, 1 - y)]
    return x, y, c, chips


def _remote(src, dst, send_sem, recv_sem, dev):
    return pltpu.make_async_remote_copy(src_ref=src, dst_ref=dst, send_sem=send_sem, recv_sem=recv_sem,
                                        device_id=dev, device_id_type=MESH)


def small_allreduce(v, name, hosted=()):
    rows, W = v.shape

    def body(v_ref, o_ref, sib_ref, pair_ref, chips_ref, send_sems, recv_sems):
        x, y, c, chips = _place()
        me = 2 * x + y
        swap = _remote(v_ref, sib_ref, send_sems.at[3], recv_sems.at[3], (x, y, 1 - c))
        swap.start()
        swap.wait()
        mine, other = v_ref[...], sib_ref[...]
        pair_ref[...] = jnp.where(c == 0, mine, other) + jnp.where(c == 0, other, mine)
        sends = []
        for j, (px, py) in enumerate(chips):
            cp = _remote(pair_ref, chips_ref.at[me], send_sems.at[j], recv_sems.at[j], (px, py, c))
            cp.start()
            sends.append(cp)
        chips_ref[me] = pair_ref[...]
        for j, (px, py) in enumerate(chips):
            blk = chips_ref.at[2 * px + py]
            _remote(blk, blk, send_sems.at[j], recv_sems.at[j], (px, py, c)).wait_recv()
        for cp in sends:
            cp.wait_send()
        o_ref[...] = (chips_ref[0] + chips_ref[1]) + (chips_ref[2] + chips_ref[3])

    vm = pl.BlockSpec(memory_space=pltpu.VMEM)
    (out,), xo = _call(
        body, name, (), [vm], [vm], [jax.ShapeDtypeStruct((rows, W), F32)], [v], (),
        [pltpu.VMEM((rows, W), F32), pltpu.VMEM((rows, W), F32), pltpu.VMEM((N_CHIPS, rows, W), F32),
         pltpu.SemaphoreType.DMA((4,)), pltpu.SemaphoreType.DMA((4,))], hosted=hosted)
    return out, xo


def _gather_p1_copies(srcs, bufs, ssem, rsem, base):
    x, y, c, chips = _place()
    me, sib = 2 * x + y, (x, y, 1 - c)
    sends, recvs = [], []
    for k, (src, buf) in enumerate(zip(srcs, bufs)):
        rh = src.shape[0] // 2
        s0 = base + 4 * k
        sends.append(_remote(src, buf.at[me], ssem.at[s0 + 3], rsem.at[s0 + 3], sib))
        recvs.append(_remote(buf.at[me], buf.at[me], ssem.at[s0 + 3], rsem.at[s0 + 3], sib))
        for j, (px, py) in enumerate(chips):
            sends.append(_remote(src.at[pl.ds(c * rh, rh)], buf.at[me, pl.ds(c * rh, rh)], ssem.at[s0 + j], rsem.at[s0 + j], (px, py, c)))
            blk = buf.at[2 * px + py, pl.ds(c * rh, rh)]
            recvs.append(_remote(blk, blk, ssem.at[s0 + j], rsem.at[s0 + j], (px, py, c)))
    return sends, recvs


def _gather_p2_copies(bufs, ssem, rsem, base):
    x, y, c, chips = _place()
    sib = (x, y, 1 - c)
    sends, recvs = [], []
    for k, buf in enumerate(bufs):
        rh = buf.shape[1] // 2
        for j, (px, py) in enumerate(chips):
            s0 = base + 3 * k + j
            blk = buf.at[2 * px + py, pl.ds(c * rh, rh)]
            sends.append(_remote(blk, blk, ssem.at[s0], rsem.at[s0], sib))
            got = buf.at[2 * px + py, pl.ds((1 - c) * rh, rh)]
            recvs.append(_remote(got, got, ssem.at[s0], rsem.at[s0], sib))
    return sends, recvs


def _gathered_shape(s):
    return jax.ShapeDtypeStruct((N_CHIPS,) + s.shape, s.dtype)


def gather_p1(shards):
    return _Exchange(shards, [_gathered_shape(s) for s in shards], {}, 4 * len(shards),
                     lambda xi, xo, ss, rs: _gather_p1_copies(xi, xo, ss, rs, 0))


def gather_p2(bufs):
    return _Exchange(bufs, [jax.ShapeDtypeStruct(b.shape, b.dtype) for b in bufs], {k: k for k in range(len(bufs))},
                     3 * len(bufs), lambda xi, xo, ss, rs: _gather_p2_copies(xo, ss, rs, 0))


def gather_whole(whole, begun):
    nw, n = len(whole), len(whole) + len(begun)
    shards = list(whole) + list(begun)
    return _Exchange(shards, [_gathered_shape(s) for s in shards], {}, 4 * n + 3 * nw,
                     lambda xi, xo, ss, rs: _gather_p1_copies(xi, xo, ss, rs, 0),
                     then=lambda xi, xo, ss, rs: _gather_p2_copies(xo[:nw], ss, rs, 4 * n))


def gather_small(v):
    def copies(xi, xo, ssem, rsem):
        x, y, c, chips = _place()
        me, sib = 2 * x + y, (x, y, 1 - c)
        sends = [_remote(xi[0], xo[0].at[me], ssem.at[3], rsem.at[3], sib)]
        recvs = [_remote(xo[0].at[me], xo[0].at[me], ssem.at[3], rsem.at[3], sib)]
        for j, (px, py) in enumerate(chips):
            sends.append(_remote(xi[0], xo[0].at[me], ssem.at[j], rsem.at[j], (px, py, c)))
            blk = xo[0].at[2 * px + py]
            recvs.append(_remote(blk, blk, ssem.at[j], rsem.at[j], (px, py, c)))
        return sends, recvs

    return _Exchange([v], [_gathered_shape(v)], {}, 4, copies)


def gather_all(v):
    def copies(xi, xo, ssem, rsem):
        x, y, c, _ = _place()
        sends, recvs = [], []
        for m in range(1, N_DEV):
            px, py, pc = (1 - x) if m & 4 else x, (1 - y) if m & 2 else y, (1 - c) if m & 1 else c
            sends.append(_remote(xi[0], xo[0].at[4 * x + 2 * y + c], ssem.at[m - 1], rsem.at[m - 1], (px, py, pc)))
            blk = xo[0].at[4 * px + 2 * py + pc]
            recvs.append(_remote(blk, blk, ssem.at[m - 1], rsem.at[m - 1], (px, py, pc)))
        return sends, recvs

    return _Exchange([v], [jax.ShapeDtypeStruct((N_DEV,) + v.shape, v.dtype)], {}, N_DEV - 1, copies)


def run_exchanges(exchanges, name):
    return _call(lambda: None, name, (), [], [], [], [], (), hosted=exchanges)[1]


def sibling_halves(grads):
    def copies(xi, xo, ssem, rsem):
        x, y, c, _ = _place()
        sends = [_remote(xi[k].at[1 - c], xo[k], ssem.at[k], rsem.at[k], (x, y, 1 - c)) for k in range(len(grads))]
        return sends, sends

    return _Exchange(grads, [jax.ShapeDtypeStruct(g.shape[1:], g.dtype) for g in grads], {}, len(grads), copies)


def pair_sum(ghs, recvs, cidx, name):
    n = len(ghs)
    S = ghs[0].shape[1]

    def body(c_ref, *refs):
        for k in range(n):
            a_ref, b_ref, o_ref = refs[2 * k], refs[2 * k + 1], refs[2 * n + k]
            o_ref[...] = (a_ref[...].astype(F32) + b_ref[...].astype(F32)).astype(o_ref.dtype)

    in_specs, out_specs, out_shape, args = [], [], [], []
    for gh, recv in zip(ghs, recvs):
        _, _, Rh, C = gh.shape
        in_specs += [pl.BlockSpec((None, None, Rh, C), lambda s, c_ref: (c_ref[0], s, 0, 0)),
                     pl.BlockSpec((None, Rh, C), lambda s, c_ref: (s, 0, 0))]
        out_specs.append(pl.BlockSpec((None, Rh, C), lambda s, c_ref: (s, 0, 0)))
        out_shape.append(jax.ShapeDtypeStruct((S, Rh, C), BF16))
        args += [gh, recv]
    return pl.pallas_call(
        body, name=name, out_shape=out_shape,
        grid_spec=pltpu.PrefetchScalarGridSpec(num_scalar_prefetch=1, grid=(S,), in_specs=in_specs, out_specs=out_specs),
        compiler_params=_params(("parallel",)),
    )(cidx, *args)


def scatter_p1(parts):
    def copies(xi, xo, ssem, rsem):
        x, y, c, chips = _place()
        me, sib = 2 * x + y, (x, y, 1 - c)
        sends, recvs = [], []
        for k in range(len(parts)):
            s0 = 4 * k
            sends.append(_remote(xi[k].at[me], xo[k].at[me, c], ssem.at[s0 + 3], rsem.at[s0 + 3], sib))
            own = xo[k].at[me, 1 - c]
            recvs.append(_remote(own, own, ssem.at[s0 + 3], rsem.at[s0 + 3], sib))
            for j, (px, py) in enumerate(chips):
                sends.append(_remote(xi[k].at[2 * px + py], xo[k].at[me, c], ssem.at[s0 + j], rsem.at[s0 + j], (px, py, c)))
                blk = xo[k].at[2 * px + py, c]
                recvs.append(_remote(blk, blk, ssem.at[s0 + j], rsem.at[s0 + j], (px, py, c)))
        return sends, recvs

    return _Exchange(parts, [jax.ShapeDtypeStruct((p.shape[0], 2) + p.shape[1:], p.dtype) for p in parts], {},
                     4 * len(parts), copies)


def scatter_p2(bufs):
    def copies(xi, xo, ssem, rsem):
        x, y, c, chips = _place()
        sib = (x, y, 1 - c)
        sends, recvs = [], []
        for k in range(len(bufs)):
            for j, (px, py) in enumerate(chips):
                s0 = 3 * k + j
                blk = xo[k].at[2 * px + py, c]
                sends.append(_remote(blk, blk, ssem.at[s0], rsem.at[s0], sib))
                got = xo[k].at[2 * px + py, 1 - c]
                recvs.append(_remote(got, got, ssem.at[s0], rsem.at[s0], sib))
        return sends, recvs

    return _Exchange(bufs, [jax.ShapeDtypeStruct(b.shape, b.dtype) for b in bufs], {k: k for k in range(len(bufs))},
                     3 * len(bufs), copies)


def _adamw_math(w, g, m, v):
    m = ADAM_B1 * m + (1.0 - ADAM_B1) * g
    v = ADAM_B2 * v + (1.0 - ADAM_B2) * (g * g)
    m_hat = m / (1.0 - ADAM_B1 ** ADAM_STEP)
    v_hat = v / (1.0 - ADAM_B2 ** ADAM_STEP)
    delta = -ADAM_LR * (m_hat / (jnp.sqrt(v_hat) + ADAM_EPS) + ADAM_WD * w)
    return delta, m, v


def adamw_reduce(tensors, place, lyr, bases, name):
    n = len(tensors)
    L, R, C = tensors[0][0].shape
    Rh = R // 2
    rb = _tile(Rh, ROW_TILE, 2 * SUBLANES)
    nb = Rh // rb

    def body(place_ref, *refs):
        mine = (place_ref[1] == pl.program_id(0))
        for k in range(n):
            p_ref, b0, b1, b2, b3, w_ref, m_ref, v_ref = refs[8 * k:8 * k + 8]
            go_ref, d_ref, mo_ref, vo_ref = refs[len(refs) - 4 * n + 4 * k:len(refs) - 4 * n + 4 * k + 4]
            g = None
            for p, b in enumerate((b0, b1, b2, b3)):
                val = jnp.where(mine & (place_ref[0] == p), p_ref[...], b[...]).astype(F32)
                g = val if g is None else g + val
            d, mn, vn = _adamw_math(w_ref[...], g, m_ref[...], v_ref[...])
            go_ref[...] = g
            d_ref[...] = d
            mo_ref[...] = mn
            vo_ref[...] = vn

    def buf_spec(p):
        def idx(h, i, pr):
            own = (pr[0] == p) & (pr[1] == h)
            return (p, jnp.where(own, 1 - h, h), i, 0)
        return pl.BlockSpec((None, None, rb, C), idx)

    blk = pl.BlockSpec((None, rb, C), lambda h, i, pr: (lyr, h * nb + i, 0))
    in_specs, args = [], []
    for w, m, v, buf, part in tensors:
        in_specs += [pl.BlockSpec((None, rb, C), lambda h, i, pr: (pr[0], i, 0))] + [buf_spec(p) for p in range(N_CHIPS)] + [blk] * 3
        args += [part, buf, buf, buf, buf, w, m, v]
    aliases = {}
    if bases is not None:
        in_specs += [pl.BlockSpec(memory_space=pl.ANY)] * (4 * n)
        aliases = {len(args) + k: k for k in range(4 * n)}
        args += list(bases)
    shp = jax.ShapeDtypeStruct((L, R, C), F32)
    flat = _call(body, name, (2, nb), in_specs, [blk] * (4 * n), [shp] * (4 * n), args, ("parallel", "parallel"),
                 prefetch=[place], own_aliases=aliases)[0]
    return flat


def small_update(late, early, own, place, entries, loss_row, name):
    ne = len(entries)
    D = late.shape[1]

    def body(place_ref, late_ref, early_ref, own_ref, *refs):
        ins, outs = refs[:3 * ne], refs[3 * ne:]
        ch = place_ref[0]
        me = 2 * place_ref[0] + place_ref[1]

        def early_sum(rs, cs):
            acc = None
            for d in range(N_DEV):
                val = jnp.where(me == d, own_ref[rs, cs], early_ref[d, rs, cs])
                acc = val if acc is None else acc + val
            return acc

        outs[4 * ne][...] = early_sum(slice(loss_row, loss_row + 1), slice(0, LANES))[:, 0:1]
        for e, (source, row0, kind, w, _, _) in enumerate(entries):
            r, width = w.shape[0], w.shape[-1]
            from_late = lambda rs, cs: late_ref[rs, cs]
            gsum = early_sum if source == "early" else from_late

            if kind == "layers":
                for j, (src, rw) in enumerate(row0):
                    gj = (early_sum if src == "early" else from_late)(slice(rw, rw + 1), slice(0, D))
                    at = (slice(j, j + 1), slice(None))
                    d, mn, vn = _adamw_math(ins[3 * e][at], gj, ins[3 * e + 1][at], ins[3 * e + 2][at])
                    outs[4 * e][at] = gj
                    outs[4 * e + 1][at] = d
                    outs[4 * e + 2][at] = mn
                    outs[4 * e + 3][at] = vn
                continue
            if kind == "full":
                g = gsum(slice(row0, row0 + r), slice(0, D))
            elif kind in ("cols", "rows"):
                g = gsum(slice(row0, row0 + r), slice(0, width))
                for q in range(1, N_CHIPS):
                    g = jnp.where(ch == q, gsum(slice(row0, row0 + r), slice(q * width, (q + 1) * width)), g)
            else:
                per_row = D // width
                g = gsum(slice(row0, row0 + 1), slice(0, width))
                for q in range(1, N_CHIPS):
                    rr = row0 + q // per_row
                    cc = (q % per_row) * width
                    g = jnp.where(ch == q, gsum(slice(rr, rr + 1), slice(cc, cc + width)), g)
            for j in ([slice(None)] if kind != "rows" else range(r)):
                gj = g if kind != "rows" else g[j:j + 1, :]
                d, mn, vn = _adamw_math(ins[3 * e][j], gj, ins[3 * e + 1][j], ins[3 * e + 2][j])
                outs[4 * e][j] = gj
                outs[4 * e + 1][j] = d
                outs[4 * e + 2][j] = mn
                outs[4 * e + 3][j] = vn

    vm = pl.BlockSpec(memory_space=pltpu.VMEM)
    args, out_shape = [], []
    for _, _, _, w, m, v in entries:
        args += [w, m, v]
        out_shape += [jax.ShapeDtypeStruct(w.shape, F32)] * 4
    out_shape.append(jax.ShapeDtypeStruct((1, 1), F32))
    return pl.pallas_call(
        body, name=name,
        in_specs=[pl.BlockSpec(memory_space=pltpu.SMEM), vm, vm, vm] + [vm] * (3 * ne),
        out_specs=[vm] * (4 * ne + 1), out_shape=out_shape,
        compiler_params=pltpu.CompilerParams(vmem_limit_bytes=VMEM_LIMIT),
    )(place, late, early, own, *args)


def _pack_rows(items, width, name):
    starts, at = [], 0
    for it in items:
        starts.append(at)
        at += -(-it.shape[0] // SUBLANES) * SUBLANES
    total = at

    def body(*refs):
        o_ref = refs[-1]
        o_ref[...] = jnp.zeros_like(o_ref)
        for it_ref, r0 in zip(refs[:-1], starts):
            if len(it_ref.shape) == 3:
                for j in range(it_ref.shape[0]):
                    o_ref[r0 + j:r0 + j + 1, :] = it_ref[j]
            elif it_ref.shape == (1, 1):
                o_ref[r0:r0 + 1, :] = jnp.broadcast_to(it_ref[...], (1, width))
            else:
                o_ref[r0:r0 + it_ref.shape[0], :] = it_ref[...]

    vm = pl.BlockSpec(memory_space=pltpu.VMEM)
    packed = pl.pallas_call(body, name=name, in_specs=[vm] * len(items), out_specs=vm,
                            out_shape=jax.ShapeDtypeStruct((total, width), F32))(*items)
    return packed, starts


def kernel(x, a_norm, a_w_in, a_conv, a_w_out, b_norm, b_w_pw1, b_b_pw1, b_conv, b_b_conv, b_ln_g, b_ln_b, b_w_pw2, b_b_pw2, ffn_norm, ffn_w_gate, ffn_w_up, ffn_w_down, final_norm, loss_target, m_a_norm, m_a_w_in, m_a_conv, m_a_w_out, m_b_norm, m_b_w_pw1, m_b_b_pw1, m_b_conv, m_b_b_conv, m_b_ln_g, m_b_ln_b, m_b_w_pw2, m_b_b_pw2, m_ffn_norm, m_ffn_w_gate, m_ffn_w_up, m_ffn_w_down, m_final_norm, v_a_norm, v_a_w_in, v_a_conv, v_a_w_out, v_b_norm, v_b_w_pw1, v_b_b_pw1, v_b_conv, v_b_b_conv, v_b_ln_g, v_b_ln_b, v_b_w_pw2, v_b_b_pw2, v_ffn_norm, v_ffn_w_gate, v_ffn_w_up, v_ffn_w_down, v_final_norm):
    T, D = x.shape[1], x.shape[2]
    Dq = D // N_CHIPS
    cx, cy, cc = lax.axis_index("x"), lax.axis_index("y"), lax.axis_index("c")
    chip = (2 * cx + cy).astype(jnp.int32).reshape(1)
    cidx = cc.astype(jnp.int32).reshape(1)
    h0 = x.reshape(T, D)
    tgt = loss_target.reshape(T, D)

    rows3 = lambda t: jnp.swapaxes(t, 0, 1)
    small_shards = [rows3(a_conv), b_norm, b_b_pw1.reshape(2, Dq), rows3(b_conv), b_b_conv, b_ln_g, b_ln_b, b_b_pw2]
    packed, st = _pack_rows(small_shards, Dq, "pack_small")

    tr = lambda t: jnp.swapaxes(t, 1, 2)
    w_gate, m_gate, v_gate = tr(ffn_w_gate), tr(m_ffn_w_gate), tr(v_ffn_w_gate)
    w_up, m_up, v_up = tr(ffn_w_up), tr(m_ffn_w_up), tr(v_ffn_w_up)
    bf = lambda t: t.astype(BF16)
    s_in, s_out, s_pw1, s_pw2 = bf(a_w_in[0]), bf(a_w_out[0]), bf(b_w_pw1[0]), bf(b_w_pw2[0])
    s_gate, s_up, s_down = [bf(w_gate[l]) for l in (0, 1)], [bf(w_up[l]) for l in (0, 1)], [bf(ffn_w_down[l]) for l in (0, 1)]

    n0, (g_in,) = rms_fwd(h0, a_norm, "rms_a", hosted=[gather_whole([s_in], [])])
    bcv, (g_out, gate0, sw) = mm_cols(n0, g_in, "mm_w_in", hosted=[gather_p1([s_out, s_gate[0]]), gather_small(packed)])

    def whole(k, r):
        return jnp.transpose(sw[:, st[k]:st[k] + r, :], (1, 0, 2)).reshape(r, D)

    a_conv_f, b_norm_f = whole(0, 3), whole(1, 1)
    b_b_pw1_f = sw[:, st[2]:st[2] + 2, :].reshape(1, 2 * D)
    b_conv_f, b_b_conv_f, b_ln_g_f, b_ln_b_f, b_b_pw2_f = whole(3, b_conv.shape[1]), whole(4, 1), whole(5, 1), whole(6, 1), whole(7, 1)
    ya, h1, (g_out, up0, down0, gate0) = gateconv_fwd(bcv, a_conv_f, gather_p2([g_out]), h0, "gateconv_fwd",
                                                      hosted=[gather_p1([s_up[0], s_down[0]]), gather_p2([gate0])])
    g_out = g_out.reshape(1, D, D)
    n1, fg0, fu0, gu0, h2, (up0, down0, g_pw1, g_pw2, gate1, up1) = ffn_fwd(
        h1, ffn_norm[0:1], [gate0], "ffn_fwd0", arriving=gather_p2([up0, down0]),
        hosted=[gather_whole([s_pw1, s_pw2], [s_gate[1], s_up[1]])])
    g_pw2 = g_pw2.reshape(1, D, D)
    n2, ub, (down1, gate1, up1) = rms_mm_cols(h2, b_norm_f, g_pw1, b_b_pw1_f, "mm_pw1",
                                              hosted=[gather_p1([s_down[1]]), gather_p2([gate1, up1])])
    cu, sb, h3, (down1,) = bconv_fwd(ub, b_conv_f, b_b_conv_f, b_ln_g_f, b_ln_b_f, g_pw2, b_b_pw2_f, h2, "bconv_fwd",
                                     hosted=[gather_p2([down1])])
    n3, fg1, fu1, gu1, h4, _ = ffn_fwd(h3, ffn_norm[1:2], [gate1, up1, down1], "ffn_fwd1")
    loss_part, dh4, dh4_b, d_final = loss_head(h4, final_norm.reshape(1, D), tgt, "loss_head")

    place = jnp.concatenate([chip, cidx])

    def pair_sums(ghs, from_sib, tags):
        return pair_sum(ghs, from_sib, cidx, "pair_sum_" + "_".join(tags))

    def upd(wmvs, bufs, parts, tag):
        flat = None
        for lyr in range(len(bufs[0])):
            tensors = [(w, m, v, b[lyr], p[lyr]) for (w, m, v), b, p in zip(wmvs, bufs, parts)]
            flat = adamw_reduce(tensors, place, lyr, flat, "adamw_%s%d" % (tag, lyr))
        return [flat[4 * k:4 * k + 4] for k in range(len(wmvs))]

    dg1, du1, dh3, dh3_b, d_fn1, _ = ffn_bwd(dh4, h3, ffn_norm[1:2], fg1, fu1, down1, gate1, up1, "ffn_bwd1")
    gh_down1, _ = tn_grad(gu1, dh4_b, N_CHIPS, True, "tn_down1")
    gh_gate1, _ = tn_grad(dg1, n3, N_CHIPS, True, "tn_gate1")
    gh_up1, _ = tn_grad(du1, n3, N_CHIPS, True, "tn_up1")
    f1 = [gh_gate1, gh_up1, gh_down1]

    dcu, d_ln_g, d_ln_b, d_b_conv, d_b_pw2, sib_f1 = pw2_ln_bwd(dh3, g_pw2, cu, b_ln_g_f, b_ln_b_f, "pw2_ln_bwd",
                                                                hosted=[sibling_halves(f1)])
    p_f1 = pair_sums(f1, sib_f1, ["gate1", "up1", "down1"])
    gh_pw2, _ = tn_grad_square(sb, dh3_b, N_CHIPS, "tn_pw2")
    dub, d_bconv_w, d_b_pw1, buf_f1 = bconv_bwd(dcu, ub, b_conv_f, "bconv_bwd", hosted=[scatter_p1(p_f1)])
    gh_pw1, _ = tn_grad(n2, dub, N_CHIPS, False, "tn_pw1")
    b_grp = [gh_pw1, gh_pw2]
    dh2, d_b_norm, dh2_b, (*buf_f1, sib_pw1, sib_pw2) = nt_cols_rms(
        dub, g_pw1, h2, b_norm_f, dh3, "nt_pw1", hosted=[scatter_p2(buf_f1), sibling_halves(b_grp)], also_bf16=True)
    sib_b = [sib_pw1, sib_pw2]
    p_b = pair_sums(b_grp, sib_b, ["pw1", "pw2"])

    early_grads = [d_b_norm, d_b_pw1.reshape(2, D), d_bconv_w, d_b_conv, d_ln_g, d_ln_b, d_b_pw2, d_fn1, d_final, loss_part]
    epacked, es = _pack_rows(early_grads, D, "pack_small_grads_early")
    dg0, du0, dh1, dh1_b, d_fn0, (*buf_b, eall) = ffn_bwd(dh2, h1, ffn_norm[0:1], fg0, fu0, down0, gate0, up0, "ffn_bwd0",
                                                         hosted=[scatter_p1(p_b), gather_all(epacked)])
    gh_down0, _ = tn_grad(gu0, dh2_b, N_CHIPS, True, "tn_down0")
    gh_gate0, (*buf_b, sib_down0) = tn_grad(dg0, n1, N_CHIPS, True, "tn_gate0",
                                            hosted=[scatter_p2(buf_b), sibling_halves([gh_down0])])
    p_down0 = pair_sums([gh_down0], [sib_down0], ["down0"])
    gh_up0, (buf_down0, sib_gate0) = tn_grad(du0, n1, N_CHIPS, True, "tn_up0",
                                             hosted=[scatter_p1(p_down0), sibling_halves([gh_gate0])])
    p_gate0 = pair_sums([gh_gate0], [sib_gate0], ["gate0"])
    gh_out, (buf_down0, sib_up0) = tn_grad_square(ya, dh1_b, N_CHIPS, "tn_w_out",
                                                  hosted=[scatter_p2([buf_down0]), sibling_halves([gh_up0])])
    p_up0 = pair_sums([gh_up0], [sib_up0], ["up0"])
    dbcv, d_aconv_w, (buf_gate0, sib_out) = gateconv_bwd(dh1_b, g_out, bcv, a_conv_f, "gateconv_bwd",
                                                         hosted=[scatter_p1(p_gate0), sibling_halves([gh_out])])
    p_out = pair_sums([gh_out], [sib_out], ["out"])
    gh_in, (buf_up0, buf_out, buf_gate0) = tn_grad(n0, dbcv, N_CHIPS, False, "tn_w_in",
                                                   hosted=[scatter_p1(p_up0 + p_out), scatter_p2([buf_gate0])])
    sib_in = run_exchanges([sibling_halves([gh_in])], "reduce_in_siblings")
    p_in = pair_sums([gh_in], sib_in, ["in"])
    grad_x, d_a_norm, (buf_in, buf_up0, buf_out) = nt_cols_rms(
        dbcv, g_in, h0, a_norm, dh1, "nt_w_in", hosted=[scatter_p1(p_in), scatter_p2([buf_up0, buf_out])])
    p_f0 = [p_gate0[0], p_up0[0], p_down0[0]]

    lpacked, ls = _pack_rows([d_a_norm, d_aconv_w, d_fn0], D, "pack_small_grads_late")
    lall, (buf_in,) = small_allreduce(lpacked, "allreduce_small_grads", hosted=[scatter_p2([buf_in])])
    buf_a, p_a = [buf_in, buf_out], [p_in[0], p_out[0]]

    r_gate, r_up = upd([(w_gate, m_gate, v_gate), (w_up, m_up, v_up)],
                       [[buf_gate0, buf_f1[0]], [buf_up0, buf_f1[1]]], [[p_f0[0], p_f1[0]], [p_f0[1], p_f1[1]]], "gate_up")
    (r_down,) = upd([(ffn_w_down, m_ffn_w_down, v_ffn_w_down)], [[buf_down0, buf_f1[2]]], [[p_f0[2], p_f1[2]]], "down")
    r_gate, r_up = [tr(t) for t in r_gate], [tr(t) for t in r_up]
    (r_pw1,) = upd([(b_w_pw1, m_b_w_pw1, v_b_w_pw1)], [[buf_b[0]]], [[p_b[0]]], "pw1")
    r_pw2, r_out = upd([(b_w_pw2, m_b_w_pw2, v_b_w_pw2), (a_w_out, m_a_w_out, v_a_w_out)],
                       [[buf_b[1]], [buf_a[1]]], [[p_b[1]], [p_a[1]]], "pw2_out")
    (r_in,) = upd([(a_w_in, m_a_w_in, v_a_w_in)], [[buf_a[0]]], [[p_a[0]]], "w_in")
    entries = [
        ("late", ls[0], "full", a_norm, m_a_norm, v_a_norm),
        ("late", ls[1], "rows", rows3(a_conv), rows3(m_a_conv), rows3(v_a_conv)),
        ("early", es[0], "cols", b_norm, m_b_norm, v_b_norm),
        ("early", es[1], "flat2", b_b_pw1, m_b_b_pw1, v_b_b_pw1),
        ("early", es[2], "rows", rows3(b_conv), rows3(m_b_conv), rows3(v_b_conv)),
        ("early", es[3], "cols", b_b_conv, m_b_b_conv, v_b_b_conv),
        ("early", es[4], "cols", b_ln_g, m_b_ln_g, v_b_ln_g),
        ("early", es[5], "cols", b_ln_b, m_b_ln_b, v_b_ln_b),
        ("early", es[6], "cols", b_b_pw2, m_b_b_pw2, v_b_b_pw2),
        (None, [("late", ls[2]), ("early", es[7])], "layers", ffn_norm, m_ffn_norm, v_ffn_norm),
        ("early", es[8], "full", final_norm.reshape(1, D), m_final_norm.reshape(1, D), v_final_norm.reshape(1, D)),
    ]
    so = small_update(lall, eall, epacked, place, entries, es[9], "small_update")
    sm = [so[4 * e:4 * e + 4] for e in range(len(entries))]

    def shaped(e, like):
        return [t.reshape(like.shape) for t in sm[e]]

    r_a_norm, r_a_conv, r_b_norm, r_b_b_pw1 = shaped(0, a_norm), shaped(1, a_conv), shaped(2, b_norm), shaped(3, b_b_pw1)
    r_b_conv, r_b_b_conv, r_b_ln_g, r_b_ln_b = shaped(4, b_conv), shaped(5, b_b_conv), shaped(6, b_ln_g), shaped(7, b_ln_b)
    r_b_b_pw2, r_ffn_norm, r_final = shaped(8, b_b_pw2), sm[9], shaped(10, final_norm)

    loss = so[4 * len(entries)].reshape(())
    order =[r_a_norm, r_in, r_a_conv, r_out, r_b_norm, r_pw1, r_b_b_pw1, r_b_conv, r_b_b_conv, r_b_ln_g, r_b_ln_b,
             r_pw2, r_b_b_pw2, r_ffn_norm, r_gate, r_up, r_down, r_final]
    outs = [loss, grad_x.reshape(x.shape)]
    for field in range(4):
        outs += [r[field] for r in order]
    return tuple(outs)
```

```python
import functools

import jax
import jax.numpy as jnp
from jax import lax
from jax.experimental import pallas as pl
from jax.experimental.pallas import tpu as pltpu

RMS_EPS = 1e-6
LN_EPS = 1e-5
ADAM_LR = 0.001
ADAM_B1 = 0.9
ADAM_B2 = 0.999
ADAM_EPS = 1e-08
ADAM_WD = 0.01
ADAM_STEP = 10

N_CHIPS = 4
N_DEV = 8
LANES = 128
SUBLANES = 8
HALO = 32
CONV_ROWS = 64
TOKEN_TILE = 512
WIDE_TOKEN_TILE = 1024
GRAD_TOKEN_TILE = 2048
GRAD_SEGS_PER_STEP = 2
FFN_ROW_CHUNKS = 2
FFN_FWD_SEGS_PER_STEP = 4
FFN_BWD_TOKEN_TILE = 256
ROW_TILE = 256
VMEM_LIMIT = 56 * 1024 * 1024
MESH = pl.DeviceIdType.MESH
BF16 = jnp.bfloat16
F32 = jnp.float32


def _tile(n, pref, mult=SUBLANES):
    t = min(n, pref) // mult * mult
    while n % t:
        t -= mult
    return t


def _params(sem):
    return pltpu.CompilerParams(dimension_semantics=sem, vmem_limit_bytes=VMEM_LIMIT)


def _sigmoid(x):
    return 0.5 * jnp.tanh(0.5 * x) + 0.5


class _Exchange:
    def __init__(self, ins, outs, aliases, n_sems, copies, then=None):
        self.ins, self.outs, self.aliases, self.n_sems, self.copies = list(ins), list(outs), dict(aliases), n_sems, copies
        self.then = then
        self.early = False

    def awaited_first(self):
        self.early = True
        return self

    def start(self, xi, xo, ssem, rsem):
        for cp in self.copies(xi, xo, ssem, rsem)[0]:
            cp.start()

    def finish(self, xi, xo, ssem, rsem):
        sends, recvs = self.copies(xi, xo, ssem, rsem)
        for cp in recvs:
            cp.wait_recv()
        if self.then is not None:
            sends2, recvs2 = self.then(xi, xo, ssem, rsem)
            for cp in sends2:
                cp.start()
            for cp in recvs2:
                cp.wait_recv()
            sends = sends + sends2
        for cp in sends:
            cp.wait_send()


def _call(body, name, grid, in_specs, out_specs, out_shape, args, sem, scratch_shapes=(), hosted=(), prefetch=(),
          own_aliases=None):
    in_specs, out_specs, out_shape = list(in_specs), list(out_specs), list(out_shape)
    scratch_shapes, hosted, prefetch = list(scratch_shapes), list(hosted), list(prefetch)
    n_pre, n_in, n_out, n_scr = len(prefetch), len(args), len(out_shape), len(scratch_shapes)
    x_in = [a for ex in hosted for a in ex.ins]
    x_out = [o for ex in hosted for o in ex.outs]
    aliases = {n_pre + i: o for i, o in (own_aliases or {}).items()}
    at_in, at_out = n_pre + n_in, n_out
    for ex in hosted:
        for i, o in ex.aliases.items():
            aliases[at_in + i] = at_out + o
        at_in += len(ex.ins)
        at_out += len(ex.outs)
    sems = [pltpu.SemaphoreType.DMA((ex.n_sems,)) for ex in hosted for _ in range(2)]

    def wrapped(*refs):
        pre, refs = refs[:n_pre], refs[n_pre:]
        ins, xi = refs[:n_in], refs[n_in:n_in + len(x_in)]
        refs = refs[n_in + len(x_in):]
        outs, xo = refs[:n_out], refs[n_out:n_out + len(x_out)]
        refs = refs[n_out + len(x_out):]
        scr, sm = refs[:n_scr], refs[n_scr:]
        views, a, b = [], 0, 0
        for e, ex in enumerate(hosted):
            views.append((xi[a:a + len(ex.ins)], xo[b:b + len(ex.outs)], sm[2 * e], sm[2 * e + 1]))
            a += len(ex.ins)
            b += len(ex.outs)
        first = last = None
        for ax, g in enumerate(grid):
            f, l = pl.program_id(ax) == 0, pl.program_id(ax) == g - 1
            first, last = (f, l) if first is None else (first & f, last & l)

        def begin():
            for ex, v in zip(hosted, views):
                ex.start(*v)
            for ex, v in zip(hosted, views):
                if ex.early:
                    ex.finish(*v)

        def end():
            for ex, v in zip(hosted, views):
                if not ex.early:
                    ex.finish(*v)

        if hosted and grid:
            pl.when(first)(begin)
        elif hosted:
            begin()
        early_refs = [r for ex, v in zip(hosted, views) if ex.early for r in v[1]]
        body(*pre, *ins, *outs, *scr, *early_refs)
        if hosted and grid:
            pl.when(last)(end)
        elif hosted:
            end()

    hbm = pl.BlockSpec(memory_space=pl.ANY)
    all_in, all_out = in_specs + [hbm] * len(x_in), out_specs + [hbm] * len(x_out)
    kw = dict(name=name, out_shape=out_shape + x_out, input_output_aliases=aliases,
              compiler_params=_params(tuple("arbitrary" for _ in grid) if hosted else sem))
    if prefetch:
        kw["grid_spec"] = pltpu.PrefetchScalarGridSpec(num_scalar_prefetch=n_pre, grid=grid, in_specs=all_in,
                                                       out_specs=all_out, scratch_shapes=scratch_shapes + sems)
    else:
        kw.update(grid=grid, in_specs=all_in, out_specs=all_out, scratch_shapes=scratch_shapes + sems)
    res = pl.pallas_call(wrapped, **kw)(*prefetch, *args, *x_in)
    return list(res[:n_out]), list(res[n_out:])


def rms_fwd(h, gain, name, hosted=()):
    T, D = h.shape
    tm = _tile(T, TOKEN_TILE)

    def body(h_ref, g_ref, o_ref):
        x = h_ref[...]
        r = lax.rsqrt(jnp.mean(x * x, axis=-1, keepdims=True) + RMS_EPS)
        o_ref[...] = (x * r * g_ref[...]).astype(o_ref.dtype)

    (n,), xo = _call(
        body, name, (T // tm,),
        [pl.BlockSpec((tm, D), lambda i: (i, 0)), pl.BlockSpec((1, D), lambda i: (0, 0))],
        [pl.BlockSpec((tm, D), lambda i: (i, 0))], [jax.ShapeDtypeStruct((T, D), BF16)],
        [h, gain], ("parallel",), hosted=hosted)
    return n, xo


def loss_head(h, gain, tgt, name):
    T, D = h.shape
    tm = _tile(T, TOKEN_TILE)

    def body(h_ref, g_ref, t_ref, loss_ref, dh_ref, dhb_ref, dg_ref):
        i = pl.program_id(0)
        x = h_ref[...]
        g = g_ref[...]
        r = lax.rsqrt(jnp.mean(x * x, axis=-1, keepdims=True) + RMS_EPS)
        xhat = x * r
        diff = xhat * g - t_ref[...]
        part_loss = 0.5 * jnp.sum(jnp.mean(diff * diff, axis=-1, keepdims=True), axis=0, keepdims=True)
        dy = diff * (1.0 / D)
        dxhat = dy * g
        dh = r * (dxhat - xhat * jnp.mean(dxhat * xhat, axis=-1, keepdims=True))
        dh_ref[...] = dh
        dhb_ref[...] = dh.astype(dhb_ref.dtype)
        part = jnp.sum(dy * xhat, axis=0, keepdims=True)

        @pl.when(i == 0)
        def _():
            dg_ref[...] = part
            loss_ref[...] = part_loss

        @pl.when(i > 0)
        def _():
            dg_ref[...] += part
            loss_ref[...] += part_loss

    row = pl.BlockSpec((tm, D), lambda i: (i, 0))
    vec = pl.BlockSpec((1, D), lambda i: (0, 0))
    return pl.pallas_call(
        body, name=name, grid=(T // tm,),
        in_specs=[row, vec, row],
        out_specs=[pl.BlockSpec((1, 1), lambda i: (0, 0)), row, row, vec],
        out_shape=[jax.ShapeDtypeStruct((1, 1), F32), jax.ShapeDtypeStruct((T, D), F32),
                   jax.ShapeDtypeStruct((T, D), BF16), jax.ShapeDtypeStruct((1, D), F32)],
        compiler_params=_params(("arbitrary",)),
    )(h, gain, tgt)


def _prev_halo_spec(tm, width):
    return pl.BlockSpec((HALO, width), lambda i: (jnp.maximum(i * (tm // HALO) - 1, 0), 0))


def _next_halo_spec(tm, width, T):
    return pl.BlockSpec((HALO, width), lambda i: (jnp.minimum((i + 1) * (tm // HALO), T // HALO - 1), 0))


def _shifted(win, off, rows):
    if off % SUBLANES == 0:
        return win[off:off + rows]
    n = win.shape[0]
    return pltpu.roll(win, (n - off) % n, 0)[:rows]


def _rowsum8(x):
    acc = x[0:SUBLANES]
    for q in range(1, x.shape[0] // SUBLANES):
        acc = acc + x[q * SUBLANES:(q + 1) * SUBLANES]
    return acc


def _conv_loops(tm, D, per_block):
    def chunk(r, carry):
        t0 = pl.multiple_of(r * CONV_ROWS, CONV_ROWS)
        for lb in range(D // LANES):
            per_block(t0, slice(lb * LANES, (lb + 1) * LANES))
        return carry

    lax.fori_loop(0, tm // CONV_ROWS, chunk, 0)


def gateconv_fwd(bcv, w, w_out, res, name, hosted=()):
    T, D3 = bcv.shape
    D = D3 // 3
    K = w.shape[0]
    tm = _tile(T, TOKEN_TILE)
    wo_shape = w_out.outs[0].shape

    def body(x_ref, halo_ref, w_ref, res_ref, y_ref, h_ref, pad_ref, wo_v, sem, wo_hbm):
        i = pl.program_id(0)

        @pl.when(i == 0)
        def _():
            cp = pltpu.make_async_copy(wo_hbm, wo_v, sem)
            cp.start()
            cp.wait()

        pad_ref[HALO:, :] = x_ref[:, D:2 * D] * x_ref[:, 2 * D:]
        pad_ref[:HALO, :] = jnp.where(i > 0, halo_ref[:, D:2 * D] * halo_ref[:, 2 * D:], 0.0)

        def block(t0, ls):
            win = pad_ref[pl.ds(t0, CONV_ROWS + HALO), ls]
            acc = jnp.zeros((CONV_ROWS, LANES), F32)
            for k in range(K):
                acc = acc + w_ref[k:k + 1, ls] * _shifted(win, HALO - (K - 1) + k, CONV_ROWS)
            y_ref[pl.ds(t0, CONV_ROWS), ls] = (x_ref[pl.ds(t0, CONV_ROWS), ls] * acc).astype(y_ref.dtype)

        _conv_loops(tm, D, block)
        h_ref[...] = res_ref[...] + jnp.dot(y_ref[...], wo_v[...].reshape(D, D), preferred_element_type=F32)

    row = pl.BlockSpec((tm, D), lambda i: (i, 0))
    (y, h), xo = _call(
        body, name, (T // tm,),
        [pl.BlockSpec((tm, D3), lambda i: (i, 0)), _prev_halo_spec(tm, D3), pl.BlockSpec((K, D), lambda i: (0, 0)), row],
        [row, row], [jax.ShapeDtypeStruct((T, D), BF16), jax.ShapeDtypeStruct((T, D), F32)],
        [bcv, bcv, w, res], ("arbitrary",),
        [pltpu.VMEM((tm + HALO, D), F32), pltpu.VMEM(wo_shape, BF16), pltpu.SemaphoreType.DMA],
        hosted=[w_out.awaited_first()] + list(hosted))
    return y, h, xo


def gateconv_bwd(dh, w_out, bcv, w, name, hosted=()):
    T, D3 = bcv.shape
    D = D3 // 3
    K = w.shape[0]
    tm = _tile(T, TOKEN_TILE)
    nt = T // tm

    def body(dh_ref, dhn_ref, wo_ref, x_ref, xp_ref, xn_ref, w_ref, o_ref, dw_ref, cv_ref, dc_ref, wacc_ref, dy_ref):
        i = pl.program_id(0)
        dy_ref[...] = lax.dot_general(dh_ref[...], wo_ref[0], _NT, preferred_element_type=F32)
        dyn = lax.dot_general(dhn_ref[...], wo_ref[0], _NT, preferred_element_type=F32)
        cv_ref[HALO:, :] = x_ref[:, D:2 * D] * x_ref[:, 2 * D:]
        cv_ref[:HALO, :] = jnp.where(i > 0, xp_ref[:, D:2 * D] * xp_ref[:, 2 * D:], 0.0)
        dc_ref[:tm, :] = dy_ref[...] * x_ref[:, :D]
        dc_ref[tm:, :] = jnp.where(i < nt - 1, dyn * xn_ref[:, :D], 0.0)

        @pl.when(i == 0)
        def _():
            wacc_ref[...] = jnp.zeros_like(wacc_ref)

        def block(t0, ls):
            cwin = cv_ref[pl.ds(t0, CONV_ROWS + HALO), ls]
            dwin = dc_ref[pl.ds(t0, CONV_ROWS + HALO), ls]
            dcon = dwin[:CONV_ROWS]
            conv = jnp.zeros((CONV_ROWS, LANES), F32)
            dcv = jnp.zeros((CONV_ROWS, LANES), F32)
            for k in range(K):
                wk = w_ref[k:k + 1, ls]
                cs = _shifted(cwin, HALO - (K - 1) + k, CONV_ROWS)
                conv = conv + wk * cs
                dcv = dcv + wk * _shifted(dwin, (K - 1) - k, CONV_ROWS)
                wacc_ref[k * SUBLANES:(k + 1) * SUBLANES, ls] += _rowsum8(dcon * cs)
            rows = pl.ds(t0, CONV_ROWS)
            o_ref[rows, ls] = (dy_ref[rows, ls] * conv).astype(o_ref.dtype)
            o_ref[rows, pl.ds(D + ls.start, LANES)] = (dcv * x_ref[rows, pl.ds(2 * D + ls.start, LANES)]).astype(o_ref.dtype)
            o_ref[rows, pl.ds(2 * D + ls.start, LANES)] = (dcv * x_ref[rows, pl.ds(D + ls.start, LANES)]).astype(o_ref.dtype)

        _conv_loops(tm, D, block)

        @pl.when(i == nt - 1)
        def _():
            for k in range(K):
                dw_ref[k:k + 1, :] = jnp.sum(wacc_ref[k * SUBLANES:(k + 1) * SUBLANES, :], axis=0, keepdims=True)

    (dx, dw), xo = _call(
        body, name, (nt,),
        [pl.BlockSpec((tm, D), lambda i: (i, 0)), _next_halo_spec(tm, D, T), pl.BlockSpec((1, D, D), lambda i: (0, 0, 0)),
         pl.BlockSpec((tm, D3), lambda i: (i, 0)), _prev_halo_spec(tm, D3), _next_halo_spec(tm, D3, T),
         pl.BlockSpec((K, D), lambda i: (0, 0))],
        [pl.BlockSpec((tm, D3), lambda i: (i, 0)), pl.BlockSpec((K, D), lambda i: (0, 0))],
        [jax.ShapeDtypeStruct((T, D3), BF16), jax.ShapeDtypeStruct((K, D), F32)],
        [dh, dh, w_out, bcv, bcv, bcv, w], ("arbitrary",),
        [pltpu.VMEM((tm + HALO, D), F32), pltpu.VMEM((tm + HALO, D), F32), pltpu.VMEM((K * SUBLANES, D), F32),
         pltpu.VMEM((tm, D), F32)], hosted=hosted)
    return dx, dw, xo


def bconv_fwd(u, w, b_conv, ln_g, ln_b, w_out, b_out, res, name, hosted=()):
    T, D2 = u.shape
    D = D2 // 2
    K = w.shape[0]
    tm = _tile(T, TOKEN_TILE)

    def body(u_ref, halo_ref, w_ref, bc_ref, g_ref, b_ref, wo_ref, bo_ref, res_ref, cu_ref, s_ref, h_ref, pad_ref):
        i = pl.program_id(0)
        pad_ref[HALO:, :] = u_ref[:, :D] * _sigmoid(u_ref[:, D:])
        pad_ref[:HALO, :] = jnp.where(i > 0, halo_ref[:, :D] * _sigmoid(halo_ref[:, D:]), 0.0)

        def block(t0, ls):
            win = pad_ref[pl.ds(t0, CONV_ROWS + HALO), ls]
            acc = jnp.zeros((CONV_ROWS, LANES), F32)
            for k in range(K):
                acc = acc + w_ref[k:k + 1, ls] * _shifted(win, HALO - (K - 1) + k, CONV_ROWS)
            cu_ref[pl.ds(t0, CONV_ROWS), ls] = acc + bc_ref[:, ls]

        _conv_loops(tm, D, block)
        cu = cu_ref[...]
        mu = jnp.mean(cu, axis=-1, keepdims=True)
        xc = cu - mu
        rstd = lax.rsqrt(jnp.mean(xc * xc, axis=-1, keepdims=True) + LN_EPS)
        ln = xc * rstd * g_ref[...] + b_ref[...]
        s = (ln * _sigmoid(ln)).astype(s_ref.dtype)
        s_ref[...] = s
        h_ref[...] = res_ref[...] + bo_ref[...] + jnp.dot(s, wo_ref[0], preferred_element_type=F32)

    vec = pl.BlockSpec((1, D), lambda i: (0, 0))
    row = pl.BlockSpec((tm, D), lambda i: (i, 0))
    (cu, s, h), xo = _call(
        body, name, (T // tm,),
        [pl.BlockSpec((tm, D2), lambda i: (i, 0)), _prev_halo_spec(tm, D2), pl.BlockSpec((K, D), lambda i: (0, 0)), vec, vec, vec,
         pl.BlockSpec((1, D, D), lambda i: (0, 0, 0)), vec, row],
        [row, row, row], [jax.ShapeDtypeStruct((T, D), F32), jax.ShapeDtypeStruct((T, D), BF16), jax.ShapeDtypeStruct((T, D), F32)],
        [u, u, w, b_conv, ln_g, ln_b, w_out, b_out, res], ("parallel",), [pltpu.VMEM((tm + HALO, D), F32)], hosted=hosted)
    return cu, s, h, xo


def pw2_ln_bwd(dy, w, cu, ln_g, ln_b, name, hosted=()):
    T, D = cu.shape
    tm = _tile(T, TOKEN_TILE)

    def body(dy_ref, w_ref, cu_ref, g_ref, b_ref, dcu_ref, dg_ref, db_ref, dbc_ref, dbo_ref):
        i = pl.program_id(0)
        dy_ = dy_ref[...]
        ds = lax.dot_general(dy_.astype(BF16), w_ref[0], _NT, preferred_element_type=F32)
        cu_ = cu_ref[...]
        mu = jnp.mean(cu_, axis=-1, keepdims=True)
        xc = cu_ - mu
        rstd = lax.rsqrt(jnp.mean(xc * xc, axis=-1, keepdims=True) + LN_EPS)
        xh = xc * rstd
        ln = xh * g_ref[...] + b_ref[...]
        sg = _sigmoid(ln)
        dl = ds * (sg * (1.0 + ln * (1.0 - sg)))
        dxh = dl * g_ref[...]
        dcu = rstd * (dxh - jnp.mean(dxh, axis=-1, keepdims=True) - xh * jnp.mean(dxh * xh, axis=-1, keepdims=True))
        dcu_ref[...] = dcu
        pg = jnp.sum(dl * xh, axis=0, keepdims=True)
        pb = jnp.sum(dl, axis=0, keepdims=True)
        pc = jnp.sum(dcu, axis=0, keepdims=True)
        po = jnp.sum(dy_, axis=0, keepdims=True)

        @pl.when(i == 0)
        def _():
            dg_ref[...] = pg
            db_ref[...] = pb
            dbc_ref[...] = pc
            dbo_ref[...] = po

        @pl.when(i > 0)
        def _():
            dg_ref[...] += pg
            db_ref[...] += pb
            dbc_ref[...] += pc
            dbo_ref[...] += po

    vec = pl.BlockSpec((1, D), lambda i: (0, 0))
    row = pl.BlockSpec((tm, D), lambda i: (i, 0))
    vshape = jax.ShapeDtypeStruct((1, D), F32)
    outs, xo = _call(
        body, name, (T // tm,), [row, pl.BlockSpec((1, D, D), lambda i: (0, 0, 0)), row, vec, vec], [row, vec, vec, vec, vec],
        [jax.ShapeDtypeStruct((T, D), F32), vshape, vshape, vshape, vshape], [dy, w, cu, ln_g, ln_b], ("arbitrary",),
        hosted=hosted)
    return (*outs, xo)


def bconv_bwd(dcu, u, w, name, hosted=()):
    T, D2 = u.shape
    D = D2 // 2
    K = w.shape[0]
    tm = _tile(T, TOKEN_TILE)
    nt = T // tm

    def body(dc_ref, dcn_ref, u_ref, up_ref, w_ref, du_ref, dw_ref, db_ref, glu_ref, dpad_ref, dglu_ref, wacc_ref):
        i = pl.program_id(0)
        glu_ref[HALO:, :] = u_ref[:, :D] * _sigmoid(u_ref[:, D:])
        glu_ref[:HALO, :] = jnp.where(i > 0, up_ref[:, :D] * _sigmoid(up_ref[:, D:]), 0.0)
        dpad_ref[:tm, :] = dc_ref[...]
        dpad_ref[tm:, :] = jnp.where(i < nt - 1, dcn_ref[...], 0.0)

        @pl.when(i == 0)
        def _():
            wacc_ref[...] = jnp.zeros_like(wacc_ref)

        def block(t0, ls):
            gwin = glu_ref[pl.ds(t0, CONV_ROWS + HALO), ls]
            dwin = dpad_ref[pl.ds(t0, CONV_ROWS + HALO), ls]
            dcur = dwin[:CONV_ROWS]
            dglu = jnp.zeros((CONV_ROWS, LANES), F32)
            for k in range(K):
                dglu = dglu + w_ref[k:k + 1, ls] * _shifted(dwin, (K - 1) - k, CONV_ROWS)
                gs = _shifted(gwin, HALO - (K - 1) + k, CONV_ROWS)
                wacc_ref[k * SUBLANES:(k + 1) * SUBLANES, ls] += _rowsum8(dcur * gs)
            dglu_ref[pl.ds(t0, CONV_ROWS), ls] = dglu

        _conv_loops(tm, D, block)
        dglu = dglu_ref[...]
        a = u_ref[:, :D]
        sg = _sigmoid(u_ref[:, D:])
        da = dglu * sg
        dg = dglu * a * (sg * (1.0 - sg))
        du_ref[:, :D] = da.astype(du_ref.dtype)
        du_ref[:, D:] = dg.astype(du_ref.dtype)
        pa = jnp.sum(da, axis=0, keepdims=True)
        pg = jnp.sum(dg, axis=0, keepdims=True)

        @pl.when(i == 0)
        def _():
            db_ref[:, :D] = pa
            db_ref[:, D:] = pg

        @pl.when(i > 0)
        def _():
            db_ref[:, :D] += pa
            db_ref[:, D:] += pg

        @pl.when(i == nt - 1)
        def _():
            for k in range(K):
                dw_ref[k:k + 1, :] = jnp.sum(wacc_ref[k * SUBLANES:(k + 1) * SUBLANES, :], axis=0, keepdims=True)

    (du, dw, db), xo = _call(
        body, name, (nt,),
        [pl.BlockSpec((tm, D), lambda i: (i, 0)), _next_halo_spec(tm, D, T),
         pl.BlockSpec((tm, D2), lambda i: (i, 0)), _prev_halo_spec(tm, D2), pl.BlockSpec((K, D), lambda i: (0, 0))],
        [pl.BlockSpec((tm, D2), lambda i: (i, 0)), pl.BlockSpec((K, D), lambda i: (0, 0)), pl.BlockSpec((1, D2), lambda i: (0, 0))],
        [jax.ShapeDtypeStruct((T, D2), BF16), jax.ShapeDtypeStruct((K, D), F32), jax.ShapeDtypeStruct((1, D2), F32)],
        [dcu, dcu, u, u, w], ("arbitrary",),
        [pltpu.VMEM((tm + HALO, D), F32), pltpu.VMEM((tm + HALO, D), F32), pltpu.VMEM((tm, D), F32),
         pltpu.VMEM((K * SUBLANES, D), F32)], hosted=hosted)
    return du, dw, db, xo


def mm_cols(a, w, name, hosted=()):
    T, K = a.shape
    S, _, n = w.shape
    tm = _tile(T, WIDE_TOKEN_TILE)

    def body(a_ref, w_ref, o_ref):
        o_ref[...] = jnp.dot(a_ref[...], w_ref[...], preferred_element_type=F32)

    in_specs = [pl.BlockSpec((tm, K), lambda s, i: (i, 0)), pl.BlockSpec((None, K, n), lambda s, i: (s, 0, 0))]
    (out,), xo = _call(body, name, (S, T // tm), in_specs, [pl.BlockSpec((tm, n), lambda s, i: (i, s))],
                       [jax.ShapeDtypeStruct((T, S * n), F32)], [a, w], ("parallel", "parallel"), hosted=hosted)
    return out, xo


def rms_mm_cols(h, gain, w, bias, name, hosted=()):
    T, K = h.shape
    S, _, n = w.shape
    tm = _tile(T, TOKEN_TILE)

    def body(h_ref, gain_ref, w_ref, b_ref, n_ref, o_ref):
        x = h_ref[...]
        r = lax.rsqrt(jnp.mean(x * x, axis=-1, keepdims=True) + RMS_EPS)
        a = (x * r * gain_ref[...]).astype(n_ref.dtype)
        n_ref[...] = a
        for s in range(S):
            cols = slice(s * n, (s + 1) * n)
            o_ref[:, cols] = jnp.dot(a, w_ref[s], preferred_element_type=F32) + b_ref[:, cols]

    row = pl.BlockSpec((tm, K), lambda i: (i, 0))
    (n_out, out), xo = _call(
        body, name, (T // tm,),
        [row, pl.BlockSpec((1, K), lambda i: (0, 0)), pl.BlockSpec((S, K, n), lambda i: (0, 0, 0)),
         pl.BlockSpec((1, S * n), lambda i: (0, 0))],
        [row, pl.BlockSpec((tm, S * n), lambda i: (i, 0))],
        [jax.ShapeDtypeStruct((T, K), BF16), jax.ShapeDtypeStruct((T, S * n), F32)],
        [h, gain, w, bias], ("parallel",), hosted=hosted)
    return n_out, out, xo


def _load_weights(pairs, sems, S, G, i, p):
    def copies(seg):
        return [pltpu.make_async_copy(src.at[seg], dst.at[seg], sems.at[k, seg]) for k, (src, dst) in enumerate(pairs)]

    @pl.when((i == 0) & (p == 0))
    def _():
        for seg in range(S):
            for cp in copies(seg):
                cp.start()

    @pl.when((i == 0) & (p < S // G))
    def _():
        for j in range(G):
            for cp in copies(G * p + j):
                cp.wait()


def ffn_fwd(h, gain, weights, name, hosted=(), arriving=None):
    T, D = h.shape
    S, f, _ = weights[0].shape
    tm = _tile(T, TOKEN_TILE)
    rc = tm // FFN_ROW_CHUNKS
    chunks = [slice(r * rc, (r + 1) * rc) for r in range(FFN_ROW_CHUNKS)]
    G = FFN_FWD_SEGS_PER_STEP
    weights = list(weights)
    hosted = ([arriving.awaited_first()] if arriving is not None else []) + list(hosted)

    def body(h_ref, gain_ref, *refs):
        nw = len(weights)
        wg_hbm, wu_hbm, wd_hbm = list(refs[:nw]) + list(refs[nw + 9:])
        n_ref, g_ref, u_ref, gu_ref, o_ref, wg_v, wu_v, wd_v, sems = refs[nw:nw + 9]
        i, p = pl.program_id(0), pl.program_id(1)
        _load_weights([(wg_hbm, wg_v), (wu_hbm, wu_v), (wd_hbm, wd_v)], sems, S, G, i, p)

        @pl.when(p == 0)
        def _():
            x = h_ref[...]
            r = lax.rsqrt(jnp.mean(x * x, axis=-1, keepdims=True) + RMS_EPS)
            n_ref[...] = (x * r * gain_ref[...]).astype(n_ref.dtype)

        parts = []
        for rows in chunks:
            a = n_ref[rows, :]
            acc = None
            for j in range(G):
                seg = G * p + j
                g = lax.dot_general(a, wg_v[seg], _NT, preferred_element_type=F32)
                u = lax.dot_general(a, wu_v[seg], _NT, preferred_element_type=F32)
                gu = (g * _sigmoid(g) * u).astype(gu_ref.dtype)
                g_ref[j, rows, :] = g.astype(g_ref.dtype)
                u_ref[j, rows, :] = u.astype(u_ref.dtype)
                gu_ref[j, rows, :] = gu
                part = jnp.dot(gu, wd_v[seg], preferred_element_type=F32)
                acc = part if acc is None else acc + part
            parts.append(acc)

        @pl.when(p == 0)
        def _():
            for rows, part in zip(chunks, parts):
                o_ref[rows, :] = h_ref[rows, :] + part

        @pl.when(p > 0)
        def _():
            for rows, part in zip(chunks, parts):
                o_ref[rows, :] += part

    row = pl.BlockSpec((tm, D), lambda i, p: (i, 0))
    seg = pl.BlockSpec((G, tm, f), lambda i, p: (p, i, 0))
    hbm = pl.BlockSpec(memory_space=pl.ANY)
    segs = jax.ShapeDtypeStruct((S, T, f), BF16)
    outs, xo = _call(
        body, name, (T // tm, S // G),
        [row, pl.BlockSpec((1, D), lambda i, s: (0, 0))] + [hbm] * len(weights), [row, seg, seg, seg, row],
        [jax.ShapeDtypeStruct((T, D), BF16), segs, segs, segs, jax.ShapeDtypeStruct((T, D), F32)],
        [h, gain] + weights, ("arbitrary", "arbitrary"),
        [pltpu.VMEM((S, f, D), BF16), pltpu.VMEM((S, f, D), BF16), pltpu.VMEM((S, f, D), BF16), pltpu.SemaphoreType.DMA((3, S))],
        hosted=hosted)
    return (*outs, xo)


def ffn_bwd(dy, h, gain, g, u, wd, wg, wu, name, hosted=()):
    T, D = h.shape
    S, f, _ = wg.shape
    tm = _tile(T, FFN_BWD_TOKEN_TILE)
    nt = T // tm

    def body(dy_ref, h_ref, gain_ref, g_ref, u_ref, wd_hbm, wg_hbm, wu_hbm, dg_ref, du_ref, dh_ref, dhb_ref, dgain_ref,
             wd_v, wg_v, wu_v, sems):
        i = pl.program_id(0)
        _load_weights([(wd_hbm, wd_v), (wg_hbm, wg_v), (wu_hbm, wu_v)], sems, S, S, i, 0)
        dy_ = dy_ref[...]
        dyb = dy_.astype(BF16)
        dn = None
        for j in range(S):
            dgu = lax.dot_general(dyb, wd_v[j], _NT, preferred_element_type=F32)
            gv = g_ref[j].astype(F32)
            sg = _sigmoid(gv)
            dg = (dgu * u_ref[j].astype(F32) * (sg * (1.0 + gv * (1.0 - sg)))).astype(dg_ref.dtype)
            du = (dgu * (gv * sg)).astype(du_ref.dtype)
            dg_ref[j] = dg
            du_ref[j] = du
            part = jnp.dot(dg, wg_v[j], preferred_element_type=F32) + jnp.dot(du, wu_v[j], preferred_element_type=F32)
            dn = part if dn is None else dn + part
        x = h_ref[...]
        r = lax.rsqrt(jnp.mean(x * x, axis=-1, keepdims=True) + RMS_EPS)
        xhat = x * r
        dxhat = dn * gain_ref[...]
        dh = dy_ + r * (dxhat - xhat * jnp.mean(dxhat * xhat, axis=-1, keepdims=True))
        dh_ref[...] = dh
        dhb_ref[...] = dh.astype(dhb_ref.dtype)
        pg = jnp.sum(dn * xhat, axis=0, keepdims=True)

        @pl.when(i == 0)
        def _():
            dgain_ref[...] = pg

        @pl.when(i > 0)
        def _():
            dgain_ref[...] += pg

    row = pl.BlockSpec((tm, D), lambda i: (i, 0))
    vec = pl.BlockSpec((1, D), lambda i: (0, 0))
    seg = pl.BlockSpec((S, tm, f), lambda i: (0, i, 0))
    hbm = pl.BlockSpec(memory_space=pl.ANY)
    segs = jax.ShapeDtypeStruct((S, T, f), BF16)
    outs, xo = _call(
        body, name, (nt,),
        [row, row, vec, seg, seg, hbm, hbm, hbm], [seg, seg, row, row, vec],
        [segs, segs, jax.ShapeDtypeStruct((T, D), F32), jax.ShapeDtypeStruct((T, D), BF16), jax.ShapeDtypeStruct((1, D), F32)],
        [dy, h, gain, g, u, wd, wg, wu], ("arbitrary",),
        [pltpu.VMEM((S, f, D), BF16), pltpu.VMEM((S, f, D), BF16), pltpu.VMEM((S, f, D), BF16),
         pltpu.SemaphoreType.DMA((3, S))], hosted=hosted)
    return (*outs, xo)


_NT = (((1,), (1,)), ((), ()))
_TN = (((0,), (0,)), ((), ()))


def nt_cols_rms(dy, w, h, gain, dres, name, hosted=(), also_bf16=False):
    T, K = h.shape
    S, _, n = w.shape
    tm = _tile(T, TOKEN_TILE)

    def body(dy_ref, w_ref, h_ref, gain_ref, dres_ref, dh_ref, dgain_ref, *rest):
        i = pl.program_id(0)
        dn = None
        for s in range(S):
            part = lax.dot_general(dy_ref[:, s * n:(s + 1) * n], w_ref[s], _NT, preferred_element_type=F32)
            dn = part if dn is None else dn + part
        x = h_ref[...]
        r = lax.rsqrt(jnp.mean(x * x, axis=-1, keepdims=True) + RMS_EPS)
        xhat = x * r
        dxhat = dn * gain_ref[...]
        dh = dres_ref[...] + r * (dxhat - xhat * jnp.mean(dxhat * xhat, axis=-1, keepdims=True))
        dh_ref[...] = dh
        if also_bf16:
            rest[0][...] = dh.astype(BF16)
        pg = jnp.sum(dn * xhat, axis=0, keepdims=True)

        @pl.when(i == 0)
        def _():
            dgain_ref[...] = pg

        @pl.when(i > 0)
        def _():
            dgain_ref[...] += pg

    row = pl.BlockSpec((tm, K), lambda i: (i, 0))
    vec = pl.BlockSpec((1, K), lambda i: (0, 0))
    out_specs, out_shape = [row, vec], [jax.ShapeDtypeStruct((T, K), F32), jax.ShapeDtypeStruct((1, K), F32)]
    if also_bf16:
        out_specs, out_shape = out_specs + [row], out_shape + [jax.ShapeDtypeStruct((T, K), BF16)]
    outs, xo = _call(
        body, name, (T // tm,),
        [pl.BlockSpec((tm, S * n), lambda i: (i, 0)), pl.BlockSpec((S, K, n), lambda i: (0, 0, 0)), row, vec, row],
        out_specs, out_shape, [dy, w, h, gain, dres], ("arbitrary",), hosted=hosted)
    return (*outs, xo)


def tn_grad(a, dy, S, a_by_seg, name, hosted=()):
    T = dy.shape[0]
    tt = _tile(T, GRAD_TOKEN_TILE)
    G = GRAD_SEGS_PER_STEP
    if a_by_seg:
        R, C = a.shape[2], dy.shape[1]
        a_spec = pl.BlockSpec((G, tt, R), lambda p, t: (p, t, 0))
        b_spec = pl.BlockSpec((tt, C), lambda p, t: (t, 0))
    else:
        R, C = a.shape[1], dy.shape[1] // S
        a_spec = pl.BlockSpec((tt, R), lambda p, t: (t, 0))
        b_spec = pl.BlockSpec((tt, G * C), lambda p, t: (t, p))
    Rh = R // 2
    nt = T // tt

    def body(a_ref, b_ref, o_ref, acc_ref):
        t = pl.program_id(1)
        parts = []
        for j in range(G):
            a_j = a_ref[j] if a_by_seg else a_ref[...]
            b_j = b_ref[...] if a_by_seg else b_ref[:, j * C:(j + 1) * C]
            parts.append(lax.dot_general(a_j, b_j.astype(BF16), _TN, preferred_element_type=F32))

        @pl.when(t == 0)
        def _():
            for j in range(G):
                acc_ref[j] = parts[j]

        @pl.when(t > 0)
        def _():
            for j in range(G):
                acc_ref[j] += parts[j]

        @pl.when(t == nt - 1)
        def _():
            for j in range(G):
                o_ref[0, j] = acc_ref[j, :Rh, :].astype(o_ref.dtype)
                o_ref[1, j] = acc_ref[j, Rh:, :].astype(o_ref.dtype)

    (gh,), xo = _call(
        body, name, (S // G, nt), [a_spec, b_spec], [pl.BlockSpec((2, G, Rh, C), lambda p, t: (0, p, 0, 0))],
        [jax.ShapeDtypeStruct((2, S, Rh, C), BF16)], [a, dy], ("parallel", "arbitrary"), [pltpu.VMEM((G, R, C), F32)],
        hosted=hosted)
    return gh, xo


def tn_grad_square(a, dy, S, name, hosted=()):
    T, K = a.shape
    N = dy.shape[1]
    tt = _tile(T, GRAD_TOKEN_TILE)
    nt = T // tt
    Rh = K // S // 2

    def body(a_ref, b_ref, o_ref, acc_ref):
        t = pl.program_id(0)
        part = lax.dot_general(a_ref[...], b_ref[...].astype(BF16), _TN, preferred_element_type=F32)

        @pl.when(t == 0)
        def _():
            acc_ref[...] = part

        @pl.when(t > 0)
        def _():
            acc_ref[...] += part

        @pl.when(t == nt - 1)
        def _():
            for s in range(S):
                for hf in range(2):
                    r0 = (2 * s + hf) * Rh
                    o_ref[hf, s] = acc_ref[r0:r0 + Rh, :].astype(o_ref.dtype)

    (gh,), xo = _call(
        body, name, (nt,), [pl.BlockSpec((tt, K), lambda t: (t, 0)), pl.BlockSpec((tt, N), lambda t: (t, 0))],
        [pl.BlockSpec((2, S, Rh, N), lambda t: (0, 0, 0, 0))], [jax.ShapeDtypeStruct((2, S, Rh, N), BF16)],
        [a, dy], ("arbitrary",), [pltpu.VMEM((K, N), F32)], hosted=hosted)
    return gh, xo


def _place():
    x, y, c = lax.axis_index("x"), lax.axis_index("y"), lax.axis_index("c")
    chips = [(1 - x, y), (x, 1 - y), (1 - x, 1 - y)]
    return x, y, c, chips


def _remote(src, dst, send_sem, recv_sem, dev):
    return pltpu.make_async_remote_copy(src_ref=src, dst_ref=dst, send_sem=send_sem, recv_sem=recv_sem,
                                        device_id=dev, device_id_type=MESH)


def small_allreduce(v, name, hosted=()):
    rows, W = v.shape

    def body(v_ref, o_ref, sib_ref, pair_ref, chips_ref, send_sems, recv_sems):
        x, y, c, chips = _place()
        me = 2 * x + y
        swap = _remote(v_ref, sib_ref, send_sems.at[3], recv_sems.at[3], (x, y, 1 - c))
        swap.start()
        swap.wait()
        mine, other = v_ref[...], sib_ref[...]
        pair_ref[...] = jnp.where(c == 0, mine, other) + jnp.where(c == 0, other, mine)
        sends = []
        for j, (px, py) in enumerate(chips):
            cp = _remote(pair_ref, chips_ref.at[me], send_sems.at[j], recv_sems.at[j], (px, py, c))
            cp.start()
            sends.append(cp)
        chips_ref[me] = pair_ref[...]
        for j, (px, py) in enumerate(chips):
            blk = chips_ref.at[2 * px + py]
            _remote(blk, blk, send_sems.at[j], recv_sems.at[j], (px, py, c)).wait_recv()
        for cp in sends:
            cp.wait_send()
        o_ref[...] = (chips_ref[0] + chips_ref[1]) + (chips_ref[2] + chips_ref[3])

    vm = pl.BlockSpec(memory_space=pltpu.VMEM)
    (out,), xo = _call(
        body, name, (), [vm], [vm], [jax.ShapeDtypeStruct((rows, W), F32)], [v], (),
        [pltpu.VMEM((rows, W), F32), pltpu.VMEM((rows, W), F32), pltpu.VMEM((N_CHIPS, rows, W), F32),
         pltpu.SemaphoreType.DMA((4,)), pltpu.SemaphoreType.DMA((4,))], hosted=hosted)
    return out, xo


def _gather_p1_copies(srcs, bufs, ssem, rsem, base):
    x, y, c, chips = _place()
    me, sib = 2 * x + y, (x, y, 1 - c)
    sends, recvs = [], []
    for k, (src, buf) in enumerate(zip(srcs, bufs)):
        rh = src.shape[0] // 2
        s0 = base + 4 * k
        sends.append(_remote(src, buf.at[me], ssem.at[s0 + 3], rsem.at[s0 + 3], sib))
        recvs.append(_remote(buf.at[me], buf.at[me], ssem.at[s0 + 3], rsem.at[s0 + 3], sib))
        for j, (px, py) in enumerate(chips):
            sends.append(_remote(src.at[pl.ds(c * rh, rh)], buf.at[me, pl.ds(c * rh, rh)], ssem.at[s0 + j], rsem.at[s0 + j], (px, py, c)))
            blk = buf.at[2 * px + py, pl.ds(c * rh, rh)]
            recvs.append(_remote(blk, blk, ssem.at[s0 + j], rsem.at[s0 + j], (px, py, c)))
    return sends, recvs


def _gather_p2_copies(bufs, ssem, rsem, base):
    x, y, c, chips = _place()
    sib = (x, y, 1 - c)
    sends, recvs = [], []
    for k, buf in enumerate(bufs):
        rh = buf.shape[1] // 2
        for j, (px, py) in enumerate(chips):
            s0 = base + 3 * k + j
            blk = buf.at[2 * px + py, pl.ds(c * rh, rh)]
            sends.append(_remote(blk, blk, ssem.at[s0], rsem.at[s0], sib))
            got = buf.at[2 * px + py, pl.ds((1 - c) * rh, rh)]
            recvs.append(_remote(got, got, ssem.at[s0], rsem.at[s0], sib))
    return sends, recvs


def _gathered_shape(s):
    return jax.ShapeDtypeStruct((N_CHIPS,) + s.shape, s.dtype)


def gather_p1(shards):
    return _Exchange(shards, [_gathered_shape(s) for s in shards], {}, 4 * len(shards),
                     lambda xi, xo, ss, rs: _gather_p1_copies(xi, xo, ss, rs, 0))


def gather_p2(bufs):
    return _Exchange(bufs, [jax.ShapeDtypeStruct(b.shape, b.dtype) for b in bufs], {k: k for k in range(len(bufs))},
                     3 * len(bufs), lambda xi, xo, ss, rs: _gather_p2_copies(xo, ss, rs, 0))


def gather_whole(whole, begun):
    nw, n = len(whole), len(whole) + len(begun)
    shards = list(whole) + list(begun)
    return _Exchange(shards, [_gathered_shape(s) for s in shards], {}, 4 * n + 3 * nw,
                     lambda xi, xo, ss, rs: _gather_p1_copies(xi, xo, ss, rs, 0),
                     then=lambda xi, xo, ss, rs: _gather_p2_copies(xo[:nw], ss, rs, 4 * n))


def gather_small(v):
    def copies(xi, xo, ssem, rsem):
        x, y, c, chips = _place()
        me, sib = 2 * x + y, (x, y, 1 - c)
        sends = [_remote(xi[0], xo[0].at[me], ssem.at[3], rsem.at[3], sib)]
        recvs = [_remote(xo[0].at[me], xo[0].at[me], ssem.at[3], rsem.at[3], sib)]
        for j, (px, py) in enumerate(chips):
            sends.append(_remote(xi[0], xo[0].at[me], ssem.at[j], rsem.at[j], (px, py, c)))
            blk = xo[0].at[2 * px + py]
            recvs.append(_remote(blk, blk, ssem.at[j], rsem.at[j], (px, py, c)))
        return sends, recvs

    return _Exchange([v], [_gathered_shape(v)], {}, 4, copies)


def gather_all(v):
    def copies(xi, xo, ssem, rsem):
        x, y, c, _ = _place()
        sends, recvs = [], []
        for m in range(1, N_DEV):
            px, py, pc = (1 - x) if m & 4 else x, (1 - y) if m & 2 else y, (1 - c) if m & 1 else c
            sends.append(_remote(xi[0], xo[0].at[4 * x + 2 * y + c], ssem.at[m - 1], rsem.at[m - 1], (px, py, pc)))
            blk = xo[0].at[4 * px + 2 * py + pc]
            recvs.append(_remote(blk, blk, ssem.at[m - 1], rsem.at[m - 1], (px, py, pc)))
        return sends, recvs

    return _Exchange([v], [jax.ShapeDtypeStruct((N_DEV,) + v.shape, v.dtype)], {}, N_DEV - 1, copies)


def run_exchanges(exchanges, name):
    return _call(lambda: None, name, (), [], [], [], [], (), hosted=exchanges)[1]


def sibling_halves(grads):
    def copies(xi, xo, ssem, rsem):
        x, y, c, _ = _place()
        sends = [_remote(xi[k].at[1 - c], xo[k], ssem.at[k], rsem.at[k], (x, y, 1 - c)) for k in range(len(grads))]
        return sends, sends

    return _Exchange(grads, [jax.ShapeDtypeStruct(g.shape[1:], g.dtype) for g in grads], {}, len(grads), copies)


def pair_sum(ghs, recvs, cidx, name):
    n = len(ghs)
    S = ghs[0].shape[1]

    def body(c_ref, *refs):
        for k in range(n):
            a_ref, b_ref, o_ref = refs[2 * k], refs[2 * k + 1], refs[2 * n + k]
            o_ref[...] = (a_ref[...].astype(F32) + b_ref[...].astype(F32)).astype(o_ref.dtype)

    in_specs, out_specs, out_shape, args = [], [], [], []
    for gh, recv in zip(ghs, recvs):
        _, _, Rh, C = gh.shape
        in_specs += [pl.BlockSpec((None, None, Rh, C), lambda s, c_ref: (c_ref[0], s, 0, 0)),
                     pl.BlockSpec((None, Rh, C), lambda s, c_ref: (s, 0, 0))]
        out_specs.append(pl.BlockSpec((None, Rh, C), lambda s, c_ref: (s, 0, 0)))
        out_shape.append(jax.ShapeDtypeStruct((S, Rh, C), BF16))
        args += [gh, recv]
    return pl.pallas_call(
        body, name=name, out_shape=out_shape,
        grid_spec=pltpu.PrefetchScalarGridSpec(num_scalar_prefetch=1, grid=(S,), in_specs=in_specs, out_specs=out_specs),
        compiler_params=_params(("parallel",)),
    )(cidx, *args)


def scatter_p1(parts):
    def copies(xi, xo, ssem, rsem):
        x, y, c, chips = _place()
        me, sib = 2 * x + y, (x, y, 1 - c)
        sends, recvs = [], []
        for k in range(len(parts)):
            s0 = 4 * k
            sends.append(_remote(xi[k].at[me], xo[k].at[me, c], ssem.at[s0 + 3], rsem.at[s0 + 3], sib))
            own = xo[k].at[me, 1 - c]
            recvs.append(_remote(own, own, ssem.at[s0 + 3], rsem.at[s0 + 3], sib))
            for j, (px, py) in enumerate(chips):
                sends.append(_remote(xi[k].at[2 * px + py], xo[k].at[me, c], ssem.at[s0 + j], rsem.at[s0 + j], (px, py, c)))
                blk = xo[k].at[2 * px + py, c]
                recvs.append(_remote(blk, blk, ssem.at[s0 + j], rsem.at[s0 + j], (px, py, c)))
        return sends, recvs

    return _Exchange(parts, [jax.ShapeDtypeStruct((p.shape[0], 2) + p.shape[1:], p.dtype) for p in parts], {},
                     4 * len(parts), copies)


def scatter_p2(bufs):
    def copies(xi, xo, ssem, rsem):
        x, y, c, chips = _place()
        sib = (x, y, 1 - c)
        sends, recvs = [], []
        for k in range(len(bufs)):
            for j, (px, py) in enumerate(chips):
                s0 = 3 * k + j
                blk = xo[k].at[2 * px + py, c]
                sends.append(_remote(blk, blk, ssem.at[s0], rsem.at[s0], sib))
                got = xo[k].at[2 * px + py, 1 - c]
                recvs.append(_remote(got, got, ssem.at[s0], rsem.at[s0], sib))
        return sends, recvs

    return _Exchange(bufs, [jax.ShapeDtypeStruct(b.shape, b.dtype) for b in bufs], {k: k for k in range(len(bufs))},
                     3 * len(bufs), copies)


def _adamw_math(w, g, m, v):
    m = ADAM_B1 * m + (1.0 - ADAM_B1) * g
    v = ADAM_B2 * v + (1.0 - ADAM_B2) * (g * g)
    m_hat = m / (1.0 - ADAM_B1 ** ADAM_STEP)
    v_hat = v / (1.0 - ADAM_B2 ** ADAM_STEP)
    delta = -ADAM_LR * (m_hat / (jnp.sqrt(v_hat) + ADAM_EPS) + ADAM_WD * w)
    return delta, m, v


def adamw_reduce(tensors, place, lyr, bases, name):
    n = len(tensors)
    L, R, C = tensors[0][0].shape
    Rh = R // 2
    rb = _tile(Rh, ROW_TILE, 2 * SUBLANES)
    nb = Rh // rb

    def body(place_ref, *refs):
        mine = (place_ref[1] == pl.program_id(0))
        for k in range(n):
            p_ref, b0, b1, b2, b3, w_ref, m_ref, v_ref = refs[8 * k:8 * k + 8]
            go_ref, d_ref, mo_ref, vo_ref = refs[len(refs) - 4 * n + 4 * k:len(refs) - 4 * n + 4 * k + 4]
            g = None
            for p, b in enumerate((b0, b1, b2, b3)):
                val = jnp.where(mine & (place_ref[0] == p), p_ref[...], b[...]).astype(F32)
                g = val if g is None else g + val
            d, mn, vn = _adamw_math(w_ref[...], g, m_ref[...], v_ref[...])
            go_ref[...] = g
            d_ref[...] = d
            mo_ref[...] = mn
            vo_ref[...] = vn

    def buf_spec(p):
        def idx(h, i, pr):
            own = (pr[0] == p) & (pr[1] == h)
            return (p, jnp.where(own, 1 - h, h), i, 0)
        return pl.BlockSpec((None, None, rb, C), idx)

    blk = pl.BlockSpec((None, rb, C), lambda h, i, pr: (lyr, h * nb + i, 0))
    in_specs, args = [], []
    for w, m, v, buf, part in tensors:
        in_specs += [pl.BlockSpec((None, rb, C), lambda h, i, pr: (pr[0], i, 0))] + [buf_spec(p) for p in range(N_CHIPS)] + [blk] * 3
        args += [part, buf, buf, buf, buf, w, m, v]
    aliases = {}
    if bases is not None:
        in_specs += [pl.BlockSpec(memory_space=pl.ANY)] * (4 * n)
        aliases = {len(args) + k: k for k in range(4 * n)}
        args += list(bases)
    shp = jax.ShapeDtypeStruct((L, R, C), F32)
    flat = _call(body, name, (2, nb), in_specs, [blk] * (4 * n), [shp] * (4 * n), args, ("parallel", "parallel"),
                 prefetch=[place], own_aliases=aliases)[0]
    return flat


def small_update(late, early, own, place, entries, loss_row, name):
    ne = len(entries)
    D = late.shape[1]

    def body(place_ref, late_ref, early_ref, own_ref, *refs):
        ins, outs = refs[:3 * ne], refs[3 * ne:]
        ch = place_ref[0]
        me = 2 * place_ref[0] + place_ref[1]

        def early_sum(rs, cs):
            acc = None
            for d in range(N_DEV):
                val = jnp.where(me == d, own_ref[rs, cs], early_ref[d, rs, cs])
                acc = val if acc is None else acc + val
            return acc

        outs[4 * ne][...] = early_sum(slice(loss_row, loss_row + 1), slice(0, LANES))[:, 0:1]
        for e, (source, row0, kind, w, _, _) in enumerate(entries):
            r, width = w.shape[0], w.shape[-1]
            from_late = lambda rs, cs: late_ref[rs, cs]
            gsum = early_sum if source == "early" else from_late

            if kind == "layers":
                for j, (src, rw) in enumerate(row0):
                    gj = (early_sum if src == "early" else from_late)(slice(rw, rw + 1), slice(0, D))
                    at = (slice(j, j + 1), slice(None))
                    d, mn, vn = _adamw_math(ins[3 * e][at], gj, ins[3 * e + 1][at], ins[3 * e + 2][at])
                    outs[4 * e][at] = gj
                    outs[4 * e + 1][at] = d
                    outs[4 * e + 2][at] = mn
                    outs[4 * e + 3][at] = vn
                continue
            if kind == "full":
                g = gsum(slice(row0, row0 + r), slice(0, D))
            elif kind in ("cols", "rows"):
                g = gsum(slice(row0, row0 + r), slice(0, width))
                for q in range(1, N_CHIPS):
                    g = jnp.where(ch == q, gsum(slice(row0, row0 + r), slice(q * width, (q + 1) * width)), g)
            else:
                per_row = D // width
                g = gsum(slice(row0, row0 + 1), slice(0, width))
                for q in range(1, N_CHIPS):
                    rr = row0 + q // per_row
                    cc = (q % per_row) * width
                    g = jnp.where(ch == q, gsum(slice(rr, rr + 1), slice(cc, cc + width)), g)
            for j in ([slice(None)] if kind != "rows" else range(r)):
                gj = g if kind != "rows" else g[j:j + 1, :]
                d, mn, vn = _adamw_math(ins[3 * e][j], gj, ins[3 * e + 1][j], ins[3 * e + 2][j])
                outs[4 * e][j] = gj
                outs[4 * e + 1][j] = d
                outs[4 * e + 2][j] = mn
                outs[4 * e + 3][j] = vn

    vm = pl.BlockSpec(memory_space=pltpu.VMEM)
    args, out_shape = [], []
    for _, _, _, w, m, v in entries:
        args += [w, m, v]
        out_shape += [jax.ShapeDtypeStruct(w.shape, F32)] * 4
    out_shape.append(jax.ShapeDtypeStruct((1, 1), F32))
    return pl.pallas_call(
        body, name=name,
        in_specs=[pl.BlockSpec(memory_space=pltpu.SMEM), vm, vm, vm] + [vm] * (3 * ne),
        out_specs=[vm] * (4 * ne + 1), out_shape=out_shape,
        compiler_params=pltpu.CompilerParams(vmem_limit_bytes=VMEM_LIMIT),
    )(place, late, early, own, *args)


def _pack_rows(items, width, name):
    starts, at = [], 0
    for it in items:
        starts.append(at)
        at += -(-it.shape[0] // SUBLANES) * SUBLANES
    total = at

    def body(*refs):
        o_ref = refs[-1]
        o_ref[...] = jnp.zeros_like(o_ref)
        for it_ref, r0 in zip(refs[:-1], starts):
            if len(it_ref.shape) == 3:
                for j in range(it_ref.shape[0]):
                    o_ref[r0 + j:r0 + j + 1, :] = it_ref[j]
            elif it_ref.shape == (1, 1):
                o_ref[r0:r0 + 1, :] = jnp.broadcast_to(it_ref[...], (1, width))
            else:
                o_ref[r0:r0 + it_ref.shape[0], :] = it_ref[...]

    vm = pl.BlockSpec(memory_space=pltpu.VMEM)
    packed = pl.pallas_call(body, name=name, in_specs=[vm] * len(items), out_specs=vm,
                            out_shape=jax.ShapeDtypeStruct((total, width), F32))(*items)
    return packed, starts


def kernel(x, a_norm, a_w_in, a_conv, a_w_out, b_norm, b_w_pw1, b_b_pw1, b_conv, b_b_conv, b_ln_g, b_ln_b, b_w_pw2, b_b_pw2, ffn_norm, ffn_w_gate, ffn_w_up, ffn_w_down, final_norm, loss_target, m_a_norm, m_a_w_in, m_a_conv, m_a_w_out, m_b_norm, m_b_w_pw1, m_b_b_pw1, m_b_conv, m_b_b_conv, m_b_ln_g, m_b_ln_b, m_b_w_pw2, m_b_b_pw2, m_ffn_norm, m_ffn_w_gate, m_ffn_w_up, m_ffn_w_down, m_final_norm, v_a_norm, v_a_w_in, v_a_conv, v_a_w_out, v_b_norm, v_b_w_pw1, v_b_b_pw1, v_b_conv, v_b_b_conv, v_b_ln_g, v_b_ln_b, v_b_w_pw2, v_b_b_pw2, v_ffn_norm, v_ffn_w_gate, v_ffn_w_up, v_ffn_w_down, v_final_norm):
    T, D = x.shape[1], x.shape[2]
    Dq = D // N_CHIPS
    cx, cy, cc = lax.axis_index("x"), lax.axis_index("y"), lax.axis_index("c")
    chip = (2 * cx + cy).astype(jnp.int32).reshape(1)
    cidx = cc.astype(jnp.int32).reshape(1)
    h0 = x.reshape(T, D)
    tgt = loss_target.reshape(T, D)

    rows3 = lambda t: jnp.swapaxes(t, 0, 1)
    small_shards = [rows3(a_conv), b_norm, b_b_pw1.reshape(2, Dq), rows3(b_conv), b_b_conv, b_ln_g, b_ln_b, b_b_pw2]
    packed, st = _pack_rows(small_shards, Dq, "pack_small")

    tr = lambda t: jnp.swapaxes(t, 1, 2)
    w_gate, m_gate, v_gate = tr(ffn_w_gate), tr(m_ffn_w_gate), tr(v_ffn_w_gate)
    w_up, m_up, v_up = tr(ffn_w_up), tr(m_ffn_w_up), tr(v_ffn_w_up)
    bf = lambda t: t.astype(BF16)
    s_in, s_out, s_pw1, s_pw2 = bf(a_w_in[0]), bf(a_w_out[0]), bf(b_w_pw1[0]), bf(b_w_pw2[0])
    s_gate, s_up, s_down = [bf(w_gate[l]) for l in (0, 1)], [bf(w_up[l]) for l in (0, 1)], [bf(ffn_w_down[l]) for l in (0, 1)]

    n0, (g_in,) = rms_fwd(h0, a_norm, "rms_a", hosted=[gather_whole([s_in], [])])
    bcv, (g_out, gate0, sw) = mm_cols(n0, g_in, "mm_w_in", hosted=[gather_p1([s_out, s_gate[0]]), gather_small(packed)])

    def whole(k, r):
        return jnp.transpose(sw[:, st[k]:st[k] + r, :], (1, 0, 2)).reshape(r, D)

    a_conv_f, b_norm_f = whole(0, 3), whole(1, 1)
    b_b_pw1_f = sw[:, st[2]:st[2] + 2, :].reshape(1, 2 * D)
    b_conv_f, b_b_conv_f, b_ln_g_f, b_ln_b_f, b_b_pw2_f = whole(3, b_conv.shape[1]), whole(4, 1), whole(5, 1), whole(6, 1), whole(7, 1)
    ya, h1, (g_out, up0, down0, gate0) = gateconv_fwd(bcv, a_conv_f, gather_p2([g_out]), h0, "gateconv_fwd",
                                                      hosted=[gather_p1([s_up[0], s_down[0]]), gather_p2([gate0])])
    g_out = g_out.reshape(1, D, D)
    n1, fg0, fu0, gu0, h2, (up0, down0, g_pw1, g_pw2, gate1, up1) = ffn_fwd(
        h1, ffn_norm[0:1], [gate0], "ffn_fwd0", arriving=gather_p2([up0, down0]),
        hosted=[gather_whole([s_pw1, s_pw2], [s_gate[1], s_up[1]])])
    g_pw2 = g_pw2.reshape(1, D, D)
    n2, ub, (down1, gate1, up1) = rms_mm_cols(h2, b_norm_f, g_pw1, b_b_pw1_f, "mm_pw1",
                                              hosted=[gather_p1([s_down[1]]), gather_p2([gate1, up1])])
    cu, sb, h3, (down1,) = bconv_fwd(ub, b_conv_f, b_b_conv_f, b_ln_g_f, b_ln_b_f, g_pw2, b_b_pw2_f, h2, "bconv_fwd",
                                     hosted=[gather_p2([down1])])
    n3, fg1, fu1, gu1, h4, _ = ffn_fwd(h3, ffn_norm[1:2], [gate1, up1, down1], "ffn_fwd1")
    loss_part, dh4, dh4_b, d_final = loss_head(h4, final_norm.reshape(1, D), tgt, "loss_head")

    place = jnp.concatenate([chip, cidx])

    def pair_sums(ghs, from_sib, tags):
        return pair_sum(ghs, from_sib, cidx, "pair_sum_" + "_".join(tags))

    def upd(wmvs, bufs, parts, tag):
        flat = None
        for lyr in range(len(bufs[0])):
            tensors = [(w, m, v, b[lyr], p[lyr]) for (w, m, v), b, p in zip(wmvs, bufs, parts)]
            flat = adamw_reduce(tensors, place, lyr, flat, "adamw_%s%d" % (tag, lyr))
        return [flat[4 * k:4 * k + 4] for k in range(len(wmvs))]

    dg1, du1, dh3, dh3_b, d_fn1, _ = ffn_bwd(dh4, h3, ffn_norm[1:2], fg1, fu1, down1, gate1, up1, "ffn_bwd1")
    gh_down1, _ = tn_grad(gu1, dh4_b, N_CHIPS, True, "tn_down1")
    gh_gate1, _ = tn_grad(dg1, n3, N_CHIPS, True, "tn_gate1")
    gh_up1, _ = tn_grad(du1, n3, N_CHIPS, True, "tn_up1")
    f1 = [gh_gate1, gh_up1, gh_down1]

    dcu, d_ln_g, d_ln_b, d_b_conv, d_b_pw2, sib_f1 = pw2_ln_bwd(dh3, g_pw2, cu, b_ln_g_f, b_ln_b_f, "pw2_ln_bwd",
                                                                hosted=[sibling_halves(f1)])
    p_f1 = pair_sums(f1, sib_f1, ["gate1", "up1", "down1"])
    gh_pw2, _ = tn_grad_square(sb, dh3_b, N_CHIPS, "tn_pw2")
    dub, d_bconv_w, d_b_pw1, buf_f1 = bconv_bwd(dcu, ub, b_conv_f, "bconv_bwd", hosted=[scatter_p1(p_f1)])
    gh_pw1, _ = tn_grad(n2, dub, N_CHIPS, False, "tn_pw1")
    b_grp = [gh_pw1, gh_pw2]
    dh2, d_b_norm, dh2_b, (*buf_f1, sib_pw1, sib_pw2) = nt_cols_rms(
        dub, g_pw1, h2, b_norm_f, dh3, "nt_pw1", hosted=[scatter_p2(buf_f1), sibling_halves(b_grp)], also_bf16=True)
    sib_b = [sib_pw1, sib_pw2]
    p_b = pair_sums(b_grp, sib_b, ["pw1", "pw2"])

    early_grads = [d_b_norm, d_b_pw1.reshape(2, D), d_bconv_w, d_b_conv, d_ln_g, d_ln_b, d_b_pw2, d_fn1, d_final, loss_part]
    epacked, es = _pack_rows(early_grads, D, "pack_small_grads_early")
    dg0, du0, dh1, dh1_b, d_fn0, (*buf_b, eall) = ffn_bwd(dh2, h1, ffn_norm[0:1], fg0, fu0, down0, gate0, up0, "ffn_bwd0",
                                                         hosted=[scatter_p1(p_b), gather_all(epacked)])
    gh_down0, _ = tn_grad(gu0, dh2_b, N_CHIPS, True, "tn_down0")
    gh_gate0, (*buf_b, sib_down0) = tn_grad(dg0, n1, N_CHIPS, True, "tn_gate0",
                                            hosted=[scatter_p2(buf_b), sibling_halves([gh_down0])])
    p_down0 = pair_sums([gh_down0], [sib_down0], ["down0"])
    gh_up0, (buf_down0, sib_gate0) = tn_grad(du0, n1, N_CHIPS, True, "tn_up0",
                                             hosted=[scatter_p1(p_down0), sibling_halves([gh_gate0])])
    p_gate0 = pair_sums([gh_gate0], [sib_gate0], ["gate0"])
    gh_out, (buf_down0, sib_up0) = tn_grad_square(ya, dh1_b, N_CHIPS, "tn_w_out",
                                                  hosted=[scatter_p2([buf_down0]), sibling_halves([gh_up0])])
    p_up0 = pair_sums([gh_up0], [sib_up0], ["up0"])
    dbcv, d_aconv_w, (buf_gate0, sib_out) = gateconv_bwd(dh1_b, g_out, bcv, a_conv_f, "gateconv_bwd",
                                                         hosted=[scatter_p1(p_gate0), sibling_halves([gh_out])])
    p_out = pair_sums([gh_out], [sib_out], ["out"])
    gh_in, (buf_up0, buf_out, buf_gate0) = tn_grad(n0, dbcv, N_CHIPS, False, "tn_w_in",
                                                   hosted=[scatter_p1(p_up0 + p_out), scatter_p2([buf_gate0])])
    sib_in = run_exchanges([sibling_halves([gh_in])], "reduce_in_siblings")
    p_in = pair_sums([gh_in], sib_in, ["in"])
    grad_x, d_a_norm, (buf_in, buf_up0, buf_out) = nt_cols_rms(
        dbcv, g_in, h0, a_norm, dh1, "nt_w_in", hosted=[scatter_p1(p_in), scatter_p2([buf_up0, buf_out])])
    p_f0 = [p_gate0[0], p_up0[0], p_down0[0]]

    lpacked, ls = _pack_rows([d_a_norm, d_aconv_w, d_fn0], D, "pack_small_grads_late")
    lall, (buf_in,) = small_allreduce(lpacked, "allreduce_small_grads", hosted=[scatter_p2([buf_in])])
    buf_a, p_a = [buf_in, buf_out], [p_in[0], p_out[0]]

    r_gate, r_up, r_down = upd([(w_gate, m_gate, v_gate), (w_up, m_up, v_up), (ffn_w_down, m_ffn_w_down, v_ffn_w_down)],
                               [[buf_gate0, buf_f1[0]], [buf_up0, buf_f1[1]], [buf_down0, buf_f1[2]]],
                               [[p_f0[0], p_f1[0]], [p_f0[1], p_f1[1]], [p_f0[2], p_f1[2]]], "ffn")
    r_gate, r_up = [tr(t) for t in r_gate], [tr(t) for t in r_up]
    (r_pw1,) = upd([(b_w_pw1, m_b_w_pw1, v_b_w_pw1)], [[buf_b[0]]], [[p_b[0]]], "pw1")
    r_pw2, r_out = upd([(b_w_pw2, m_b_w_pw2, v_b_w_pw2), (a_w_out, m_a_w_out, v_a_w_out)],
                       [[buf_b[1]], [buf_a[1]]], [[p_b[1]], [p_a[1]]], "pw2_out")
    (r_in,) = upd([(a_w_in, m_a_w_in, v_a_w_in)], [[buf_a[0]]], [[p_a[0]]], "w_in")
    entries = [
        ("late", ls[0], "full", a_norm, m_a_norm, v_a_norm),
        ("late", ls[1], "rows", rows3(a_conv), rows3(m_a_conv), rows3(v_a_conv)),
        ("early", es[0], "cols", b_norm, m_b_norm, v_b_norm),
        ("early", es[1], "flat2", b_b_pw1, m_b_b_pw1, v_b_b_pw1),
        ("early", es[2], "rows", rows3(b_conv), rows3(m_b_conv), rows3(v_b_conv)),
        ("early", es[3], "cols", b_b_conv, m_b_b_conv, v_b_b_conv),
        ("early", es[4], "cols", b_ln_g, m_b_ln_g, v_b_ln_g),
        ("early", es[5], "cols", b_ln_b, m_b_ln_b, v_b_ln_b),
        ("early", es[6], "cols", b_b_pw2, m_b_b_pw2, v_b_b_pw2),
        (None, [("late", ls[2]), ("early", es[7])], "layers", ffn_norm, m_ffn_norm, v_ffn_norm),
        ("early", es[8], "full", final_norm.reshape(1, D), m_final_norm.reshape(1, D), v_final_norm.reshape(1, D)),
    ]
    so = small_update(lall, eall, epacked, place, entries, es[9], "small_update")
    sm = [so[4 * e:4 * e + 4] for e in range(len(entries))]

    def shaped(e, like):
        return [t.reshape(like.shape) for t in sm[e]]

    r_a_norm, r_a_conv, r_b_norm, r_b_b_pw1 = shaped(0, a_norm), shaped(1, a_conv), shaped(2, b_norm), shaped(3, b_b_pw1)
    r_b_conv, r_b_b_conv, r_b_ln_g, r_b_ln_b = shaped(4, b_conv), shaped(5, b_b_conv), shaped(6, b_ln_g), shaped(7, b_ln_b)
    r_b_b_pw2, r_ffn_norm, r_final = shaped(8, b_b_pw2), sm[9], shaped(10, final_norm)

    loss = so[4 * len(entries)].reshape(())
    order =[r_a_norm, r_in, r_a_conv, r_out, r_b_norm, r_pw1, r_b_b_pw1, r_b_conv, r_b_b_conv, r_b_ln_g, r_b_ln_b,
             r_pw2, r_b_b_pw2, r_ffn_norm, r_gate, r_up, r_down, r_final]
    outs = [loss, grad_x.reshape(x.shape)]
    for field in range(4):
        outs += [r[field] for r in order]
    return tuple(outs)
```

```python
import functools

import jax
import jax.numpy as jnp
from jax import lax
from jax.experimental import pallas as pl
from jax.experimental.pallas import tpu as pltpu

RMS_EPS = 1e-6
LN_EPS = 1e-5
ADAM_LR = 0.001
ADAM_B1 = 0.9
ADAM_B2 = 0.999
ADAM_EPS = 1e-08
ADAM_WD = 0.01
ADAM_STEP = 10

N_CHIPS = 4
N_DEV = 8
LANES = 128
SUBLANES = 8
HALO = 32
CONV_ROWS = 64
TOKEN_TILE = 512
WIDE_TOKEN_TILE = 1024
GRAD_TOKEN_TILE = 2048
GRAD_SEGS_PER_STEP = 2
FFN_ROW_CHUNKS = 2
FFN_FWD_SEGS_PER_STEP = 4
FFN_BWD_TOKEN_TILE = 256
ROW_TILE = 256
CAST_STEPS = 4
VMEM_LIMIT = 56 * 1024 * 1024
MESH = pl.DeviceIdType.MESH
BF16 = jnp.bfloat16
F32 = jnp.float32


def _tile(n, pref, mult=SUBLANES):
    t = min(n, pref) // mult * mult
    while n % t:
        t -= mult
    return t


def _params(sem):
    return pltpu.CompilerParams(dimension_semantics=sem, vmem_limit_bytes=VMEM_LIMIT)


def _sigmoid(x):
    return 0.5 * jnp.tanh(0.5 * x) + 0.5


class _Exchange:
    def __init__(self, ins, outs, aliases, n_sems, copies, then=None):
        self.ins, self.outs, self.aliases, self.n_sems, self.copies = list(ins), list(outs), dict(aliases), n_sems, copies
        self.then = then
        self.early = False

    def awaited_first(self):
        self.early = True
        return self

    def start(self, xi, xo, ssem, rsem):
        for cp in self.copies(xi, xo, ssem, rsem)[0]:
            cp.start()

    def finish(self, xi, xo, ssem, rsem):
        sends, recvs = self.copies(xi, xo, ssem, rsem)
        for cp in recvs:
            cp.wait_recv()
        if self.then is not None:
            sends2, recvs2 = self.then(xi, xo, ssem, rsem)
            for cp in sends2:
                cp.start()
            for cp in recvs2:
                cp.wait_recv()
            sends = sends + sends2
        for cp in sends:
            cp.wait_send()


def _call(body, name, grid, in_specs, out_specs, out_shape, args, sem, scratch_shapes=(), hosted=(), prefetch=(),
          own_aliases=None):
    in_specs, out_specs, out_shape = list(in_specs), list(out_specs), list(out_shape)
    scratch_shapes, hosted, prefetch = list(scratch_shapes), list(hosted), list(prefetch)
    n_pre, n_in, n_out, n_scr = len(prefetch), len(args), len(out_shape), len(scratch_shapes)
    x_in = [a for ex in hosted for a in ex.ins]
    x_out = [o for ex in hosted for o in ex.outs]
    aliases = {n_pre + i: o for i, o in (own_aliases or {}).items()}
    at_in, at_out = n_pre + n_in, n_out
    for ex in hosted:
        for i, o in ex.aliases.items():
            aliases[at_in + i] = at_out + o
        at_in += len(ex.ins)
        at_out += len(ex.outs)
    sems = [pltpu.SemaphoreType.DMA((ex.n_sems,)) for ex in hosted for _ in range(2)]

    def wrapped(*refs):
        pre, refs = refs[:n_pre], refs[n_pre:]
        ins, xi = refs[:n_in], refs[n_in:n_in + len(x_in)]
        refs = refs[n_in + len(x_in):]
        outs, xo = refs[:n_out], refs[n_out:n_out + len(x_out)]
        refs = refs[n_out + len(x_out):]
        scr, sm = refs[:n_scr], refs[n_scr:]
        views, a, b = [], 0, 0
        for e, ex in enumerate(hosted):
            views.append((xi[a:a + len(ex.ins)], xo[b:b + len(ex.outs)], sm[2 * e], sm[2 * e + 1]))
            a += len(ex.ins)
            b += len(ex.outs)
        first = last = None
        for ax, g in enumerate(grid):
            f, l = pl.program_id(ax) == 0, pl.program_id(ax) == g - 1
            first, last = (f, l) if first is None else (first & f, last & l)

        def begin():
            for ex, v in zip(hosted, views):
                ex.start(*v)
            for ex, v in zip(hosted, views):
                if ex.early:
                    ex.finish(*v)

        def end():
            for ex, v in zip(hosted, views):
                if not ex.early:
                    ex.finish(*v)

        if hosted and grid:
            pl.when(first)(begin)
        elif hosted:
            begin()
        early_refs = [r for ex, v in zip(hosted, views) if ex.early for r in v[1]]
        body(*pre, *ins, *outs, *scr, *early_refs)
        if hosted and grid:
            pl.when(last)(end)
        elif hosted:
            end()

    hbm = pl.BlockSpec(memory_space=pl.ANY)
    all_in, all_out = in_specs + [hbm] * len(x_in), out_specs + [hbm] * len(x_out)
    kw = dict(name=name, out_shape=out_shape + x_out, input_output_aliases=aliases,
              compiler_params=_params(tuple("arbitrary" for _ in grid) if hosted else sem))
    if prefetch:
        kw["grid_spec"] = pltpu.PrefetchScalarGridSpec(num_scalar_prefetch=n_pre, grid=grid, in_specs=all_in,
                                                       out_specs=all_out, scratch_shapes=scratch_shapes + sems)
    else:
        kw.update(grid=grid, in_specs=all_in, out_specs=all_out, scratch_shapes=scratch_shapes + sems)
    res = pl.pallas_call(wrapped, **kw)(*prefetch, *args, *x_in)
    return list(res[:n_out]), list(res[n_out:])


def rms_fwd(h, gain, name, hosted=()):
    T, D = h.shape
    tm = _tile(T, TOKEN_TILE)

    def body(h_ref, g_ref, o_ref):
        x = h_ref[...]
        r = lax.rsqrt(jnp.mean(x * x, axis=-1, keepdims=True) + RMS_EPS)
        o_ref[...] = (x * r * g_ref[...]).astype(o_ref.dtype)

    (n,), xo = _call(
        body, name, (T // tm,),
        [pl.BlockSpec((tm, D), lambda i: (i, 0)), pl.BlockSpec((1, D), lambda i: (0, 0))],
        [pl.BlockSpec((tm, D), lambda i: (i, 0))], [jax.ShapeDtypeStruct((T, D), BF16)],
        [h, gain], ("parallel",), hosted=hosted)
    return n, xo


def cast_weights(tensors, name, hosted=()):
    n = len(tensors)

    def body(*refs):
        for w_ref, o_ref in zip(refs[:n], refs[n:]):
            o_ref[...] = w_ref[...].astype(o_ref.dtype)

    in_specs, out_specs, out_shape = [], [], []
    for w, lyr in tensors:
        R, C = w.shape[-2:]
        rb = R // CAST_STEPS
        assert rb * CAST_STEPS == R and rb % (2 * SUBLANES) == 0
        if lyr is None:
            in_specs.append(pl.BlockSpec((rb, C), lambda i: (i, 0)))
        else:
            in_specs.append(pl.BlockSpec((None, rb, C), lambda i, lyr=lyr: (lyr, i, 0)))
        out_specs.append(pl.BlockSpec((rb, C), lambda i: (i, 0)))
        out_shape.append(jax.ShapeDtypeStruct((R, C), BF16))
    return _call(body, name, (CAST_STEPS,), in_specs, out_specs, out_shape, [w for w, _ in tensors], ("parallel",),
                 hosted=hosted)


def loss_head(h, gain, tgt, name):
    T, D = h.shape
    tm = _tile(T, TOKEN_TILE)

    def body(h_ref, g_ref, t_ref, loss_ref, dh_ref, dhb_ref, dg_ref):
        i = pl.program_id(0)
        x = h_ref[...]
        g = g_ref[...]
        r = lax.rsqrt(jnp.mean(x * x, axis=-1, keepdims=True) + RMS_EPS)
        xhat = x * r
        diff = xhat * g - t_ref[...]
        part_loss = 0.5 * jnp.sum(jnp.mean(diff * diff, axis=-1, keepdims=True), axis=0, keepdims=True)
        dy = diff * (1.0 / D)
        dxhat = dy * g
        dh = r * (dxhat - xhat * jnp.mean(dxhat * xhat, axis=-1, keepdims=True))
        dh_ref[...] = dh
        dhb_ref[...] = dh.astype(dhb_ref.dtype)
        part = jnp.sum(dy * xhat, axis=0, keepdims=True)

        @pl.when(i == 0)
        def _():
            dg_ref[...] = part
            loss_ref[...] = part_loss

        @pl.when(i > 0)
        def _():
            dg_ref[...] += part
            loss_ref[...] += part_loss

    row = pl.BlockSpec((tm, D), lambda i: (i, 0))
    vec = pl.BlockSpec((1, D), lambda i: (0, 0))
    return pl.pallas_call(
        body, name=name, grid=(T // tm,),
        in_specs=[row, vec, row],
        out_specs=[pl.BlockSpec((1, 1), lambda i: (0, 0)), row, row, vec],
        out_shape=[jax.ShapeDtypeStruct((1, 1), F32), jax.ShapeDtypeStruct((T, D), F32),
                   jax.ShapeDtypeStruct((T, D), BF16), jax.ShapeDtypeStruct((1, D), F32)],
        compiler_params=_params(("arbitrary",)),
    )(h, gain, tgt)


def _prev_halo_spec(tm, width):
    return pl.BlockSpec((HALO, width), lambda i: (jnp.maximum(i * (tm // HALO) - 1, 0), 0))


def _next_halo_spec(tm, width, T):
    return pl.BlockSpec((HALO, width), lambda i: (jnp.minimum((i + 1) * (tm // HALO), T // HALO - 1), 0))


def _shifted(win, off, rows):
    if off % SUBLANES == 0:
        return win[off:off + rows]
    n = win.shape[0]
    return pltpu.roll(win, (n - off) % n, 0)[:rows]


def _rowsum8(x):
    acc = x[0:SUBLANES]
    for q in range(1, x.shape[0] // SUBLANES):
        acc = acc + x[q * SUBLANES:(q + 1) * SUBLANES]
    return acc


def _conv_loops(tm, D, per_block):
    def chunk(r, carry):
        t0 = pl.multiple_of(r * CONV_ROWS, CONV_ROWS)
        for lb in range(D // LANES):
            per_block(t0, slice(lb * LANES, (lb + 1) * LANES))
        return carry

    lax.fori_loop(0, tm // CONV_ROWS, chunk, 0)


def gateconv_fwd(bcv, w, w_out, res, name, hosted=()):
    T, D3 = bcv.shape
    D = D3 // 3
    K = w.shape[0]
    tm = _tile(T, TOKEN_TILE)
    wo_shape = w_out.outs[0].shape

    def body(x_ref, halo_ref, w_ref, res_ref, y_ref, h_ref, pad_ref, wo_v, sem, wo_hbm):
        i = pl.program_id(0)

        @pl.when(i == 0)
        def _():
            cp = pltpu.make_async_copy(wo_hbm, wo_v, sem)
            cp.start()
            cp.wait()

        pad_ref[HALO:, :] = x_ref[:, D:2 * D] * x_ref[:, 2 * D:]
        pad_ref[:HALO, :] = jnp.where(i > 0, halo_ref[:, D:2 * D] * halo_ref[:, 2 * D:], 0.0)

        def block(t0, ls):
            win = pad_ref[pl.ds(t0, CONV_ROWS + HALO), ls]
            acc = jnp.zeros((CONV_ROWS, LANES), F32)
            for k in range(K):
                acc = acc + w_ref[k:k + 1, ls] * _shifted(win, HALO - (K - 1) + k, CONV_ROWS)
            y_ref[pl.ds(t0, CONV_ROWS), ls] = (x_ref[pl.ds(t0, CONV_ROWS), ls] * acc).astype(y_ref.dtype)

        _conv_loops(tm, D, block)
        h_ref[...] = res_ref[...] + jnp.dot(y_ref[...], wo_v[...].reshape(D, D), preferred_element_type=F32)

    row = pl.BlockSpec((tm, D), lambda i: (i, 0))
    (y, h), xo = _call(
        body, name, (T // tm,),
        [pl.BlockSpec((tm, D3), lambda i: (i, 0)), _prev_halo_spec(tm, D3), pl.BlockSpec((K, D), lambda i: (0, 0)), row],
        [row, row], [jax.ShapeDtypeStruct((T, D), BF16), jax.ShapeDtypeStruct((T, D), F32)],
        [bcv, bcv, w, res], ("arbitrary",),
        [pltpu.VMEM((tm + HALO, D), F32), pltpu.VMEM(wo_shape, BF16), pltpu.SemaphoreType.DMA],
        hosted=[w_out.awaited_first()] + list(hosted))
    return y, h, xo


def gateconv_bwd(dh, w_out, bcv, w, name, hosted=()):
    T, D3 = bcv.shape
    D = D3 // 3
    K = w.shape[0]
    tm = _tile(T, TOKEN_TILE)
    nt = T // tm

    def body(dh_ref, dhn_ref, wo_ref, x_ref, xp_ref, xn_ref, w_ref, o_ref, dw_ref, cv_ref, dc_ref, wacc_ref, dy_ref):
        i = pl.program_id(0)
        dy_ref[...] = lax.dot_general(dh_ref[...], wo_ref[0], _NT, preferred_element_type=F32)
        dyn = lax.dot_general(dhn_ref[...], wo_ref[0], _NT, preferred_element_type=F32)
        cv_ref[HALO:, :] = x_ref[:, D:2 * D] * x_ref[:, 2 * D:]
        cv_ref[:HALO, :] = jnp.where(i > 0, xp_ref[:, D:2 * D] * xp_ref[:, 2 * D:], 0.0)
        dc_ref[:tm, :] = dy_ref[...] * x_ref[:, :D]
        dc_ref[tm:, :] = jnp.where(i < nt - 1, dyn * xn_ref[:, :D], 0.0)

        @pl.when(i == 0)
        def _():
            wacc_ref[...] = jnp.zeros_like(wacc_ref)

        def block(t0, ls):
            cwin = cv_ref[pl.ds(t0, CONV_ROWS + HALO), ls]
            dwin = dc_ref[pl.ds(t0, CONV_ROWS + HALO), ls]
            dcon = dwin[:CONV_ROWS]
            conv = jnp.zeros((CONV_ROWS, LANES), F32)
            dcv = jnp.zeros((CONV_ROWS, LANES), F32)
            for k in range(K):
                wk = w_ref[k:k + 1, ls]
                cs = _shifted(cwin, HALO - (K - 1) + k, CONV_ROWS)
                conv = conv + wk * cs
                dcv = dcv + wk * _shifted(dwin, (K - 1) - k, CONV_ROWS)
                wacc_ref[k * SUBLANES:(k + 1) * SUBLANES, ls] += _rowsum8(dcon * cs)
            rows = pl.ds(t0, CONV_ROWS)
            o_ref[rows, ls] = (dy_ref[rows, ls] * conv).astype(o_ref.dtype)
            o_ref[rows, pl.ds(D + ls.start, LANES)] = (dcv * x_ref[rows, pl.ds(2 * D + ls.start, LANES)]).astype(o_ref.dtype)
            o_ref[rows, pl.ds(2 * D + ls.start, LANES)] = (dcv * x_ref[rows, pl.ds(D + ls.start, LANES)]).astype(o_ref.dtype)

        _conv_loops(tm, D, block)

        @pl.when(i == nt - 1)
        def _():
            for k in range(K):
                dw_ref[k:k + 1, :] = jnp.sum(wacc_ref[k * SUBLANES:(k + 1) * SUBLANES, :], axis=0, keepdims=True)

    (dx, dw), xo = _call(
        body, name, (nt,),
        [pl.BlockSpec((tm, D), lambda i: (i, 0)), _next_halo_spec(tm, D, T), pl.BlockSpec((1, D, D), lambda i: (0, 0, 0)),
         pl.BlockSpec((tm, D3), lambda i: (i, 0)), _prev_halo_spec(tm, D3), _next_halo_spec(tm, D3, T),
         pl.BlockSpec((K, D), lambda i: (0, 0))],
        [pl.BlockSpec((tm, D3), lambda i: (i, 0)), pl.BlockSpec((K, D), lambda i: (0, 0))],
        [jax.ShapeDtypeStruct((T, D3), BF16), jax.ShapeDtypeStruct((K, D), F32)],
        [dh, dh, w_out, bcv, bcv, bcv, w], ("arbitrary",),
        [pltpu.VMEM((tm + HALO, D), F32), pltpu.VMEM((tm + HALO, D), F32), pltpu.VMEM((K * SUBLANES, D), F32),
         pltpu.VMEM((tm, D), F32)], hosted=hosted)
    return dx, dw, xo


def bconv_fwd(u, w, b_conv, ln_g, ln_b, w_out, b_out, res, name, hosted=()):
    T, D2 = u.shape
    D = D2 // 2
    K = w.shape[0]
    tm = _tile(T, TOKEN_TILE)

    def body(u_ref, halo_ref, w_ref, bc_ref, g_ref, b_ref, wo_ref, bo_ref, res_ref, cu_ref, s_ref, h_ref, pad_ref):
        i = pl.program_id(0)
        pad_ref[HALO:, :] = u_ref[:, :D] * _sigmoid(u_ref[:, D:])
        pad_ref[:HALO, :] = jnp.where(i > 0, halo_ref[:, :D] * _sigmoid(halo_ref[:, D:]), 0.0)

        def block(t0, ls):
            win = pad_ref[pl.ds(t0, CONV_ROWS + HALO), ls]
            acc = jnp.zeros((CONV_ROWS, LANES), F32)
            for k in range(K):
                acc = acc + w_ref[k:k + 1, ls] * _shifted(win, HALO - (K - 1) + k, CONV_ROWS)
            cu_ref[pl.ds(t0, CONV_ROWS), ls] = acc + bc_ref[:, ls]

        _conv_loops(tm, D, block)
        cu = cu_ref[...]
        mu = jnp.mean(cu, axis=-1, keepdims=True)
        xc = cu - mu
        rstd = lax.rsqrt(jnp.mean(xc * xc, axis=-1, keepdims=True) + LN_EPS)
        ln = xc * rstd * g_ref[...] + b_ref[...]
        s = (ln * _sigmoid(ln)).astype(s_ref.dtype)
        s_ref[...] = s
        h_ref[...] = res_ref[...] + bo_ref[...] + jnp.dot(s, wo_ref[0], preferred_element_type=F32)

    vec = pl.BlockSpec((1, D), lambda i: (0, 0))
    row = pl.BlockSpec((tm, D), lambda i: (i, 0))
    (cu, s, h), xo = _call(
        body, name, (T // tm,),
        [pl.BlockSpec((tm, D2), lambda i: (i, 0)), _prev_halo_spec(tm, D2), pl.BlockSpec((K, D), lambda i: (0, 0)), vec, vec, vec,
         pl.BlockSpec((1, D, D), lambda i: (0, 0, 0)), vec, row],
        [row, row, row], [jax.ShapeDtypeStruct((T, D), F32), jax.ShapeDtypeStruct((T, D), BF16), jax.ShapeDtypeStruct((T, D), F32)],
        [u, u, w, b_conv, ln_g, ln_b, w_out, b_out, res], ("parallel",), [pltpu.VMEM((tm + HALO, D), F32)], hosted=hosted)
    return cu, s, h, xo


def pw2_ln_bwd(dy, w, cu, ln_g, ln_b, name, hosted=()):
    T, D = cu.shape
    tm = _tile(T, TOKEN_TILE)

    def body(dy_ref, w_ref, cu_ref, g_ref, b_ref, dcu_ref, dg_ref, db_ref, dbc_ref, dbo_ref):
        i = pl.program_id(0)
        dy_ = dy_ref[...]
        ds = lax.dot_general(dy_.astype(BF16), w_ref[0], _NT, preferred_element_type=F32)
        cu_ = cu_ref[...]
        mu = jnp.mean(cu_, axis=-1, keepdims=True)
        xc = cu_ - mu
        rstd = lax.rsqrt(jnp.mean(xc * xc, axis=-1, keepdims=True) + LN_EPS)
        xh = xc * rstd
        ln = xh * g_ref[...] + b_ref[...]
        sg = _sigmoid(ln)
        dl = ds * (sg * (1.0 + ln * (1.0 - sg)))
        dxh = dl * g_ref[...]
        dcu = rstd * (dxh - jnp.mean(dxh, axis=-1, keepdims=True) - xh * jnp.mean(dxh * xh, axis=-1, keepdims=True))
        dcu_ref[...] = dcu
        pg = jnp.sum(dl * xh, axis=0, keepdims=True)
        pb = jnp.sum(dl, axis=0, keepdims=True)
        pc = jnp.sum(dcu, axis=0, keepdims=True)
        po = jnp.sum(dy_, axis=0, keepdims=True)

        @pl.when(i == 0)
        def _():
            dg_ref[...] = pg
            db_ref[...] = pb
            dbc_ref[...] = pc
            dbo_ref[...] = po

        @pl.when(i > 0)
        def _():
            dg_ref[...] += pg
            db_ref[...] += pb
            dbc_ref[...] += pc
            dbo_ref[...] += po

    vec = pl.BlockSpec((1, D), lambda i: (0, 0))
    row = pl.BlockSpec((tm, D), lambda i: (i, 0))
    vshape = jax.ShapeDtypeStruct((1, D), F32)
    outs, xo = _call(
        body, name, (T // tm,), [row, pl.BlockSpec((1, D, D), lambda i: (0, 0, 0)), row, vec, vec], [row, vec, vec, vec, vec],
        [jax.ShapeDtypeStruct((T, D), F32), vshape, vshape, vshape, vshape], [dy, w, cu, ln_g, ln_b], ("arbitrary",),
        hosted=hosted)
    return (*outs, xo)


def bconv_bwd(dcu, u, w, name, hosted=()):
    T, D2 = u.shape
    D = D2 // 2
    K = w.shape[0]
    tm = _tile(T, TOKEN_TILE)
    nt = T // tm

    def body(dc_ref, dcn_ref, u_ref, up_ref, w_ref, du_ref, dw_ref, db_ref, glu_ref, dpad_ref, dglu_ref, wacc_ref):
        i = pl.program_id(0)
        glu_ref[HALO:, :] = u_ref[:, :D] * _sigmoid(u_ref[:, D:])
        glu_ref[:HALO, :] = jnp.where(i > 0, up_ref[:, :D] * _sigmoid(up_ref[:, D:]), 0.0)
        dpad_ref[:tm, :] = dc_ref[...]
        dpad_ref[tm:, :] = jnp.where(i < nt - 1, dcn_ref[...], 0.0)

        @pl.when(i == 0)
        def _():
            wacc_ref[...] = jnp.zeros_like(wacc_ref)

        def block(t0, ls):
            gwin = glu_ref[pl.ds(t0, CONV_ROWS + HALO), ls]
            dwin = dpad_ref[pl.ds(t0, CONV_ROWS + HALO), ls]
            dcur = dwin[:CONV_ROWS]
            dglu = jnp.zeros((CONV_ROWS, LANES), F32)
            for k in range(K):
                dglu = dglu + w_ref[k:k + 1, ls] * _shifted(dwin, (K - 1) - k, CONV_ROWS)
                gs = _shifted(gwin, HALO - (K - 1) + k, CONV_ROWS)
                wacc_ref[k * SUBLANES:(k + 1) * SUBLANES, ls] += _rowsum8(dcur * gs)
            dglu_ref[pl.ds(t0, CONV_ROWS), ls] = dglu

        _conv_loops(tm, D, block)
        dglu = dglu_ref[...]
        a = u_ref[:, :D]
        sg = _sigmoid(u_ref[:, D:])
        da = dglu * sg
        dg = dglu * a * (sg * (1.0 - sg))
        du_ref[:, :D] = da.astype(du_ref.dtype)
        du_ref[:, D:] = dg.astype(du_ref.dtype)
        pa = jnp.sum(da, axis=0, keepdims=True)
        pg = jnp.sum(dg, axis=0, keepdims=True)

        @pl.when(i == 0)
        def _():
            db_ref[:, :D] = pa
            db_ref[:, D:] = pg

        @pl.when(i > 0)
        def _():
            db_ref[:, :D] += pa
            db_ref[:, D:] += pg

        @pl.when(i == nt - 1)
        def _():
            for k in range(K):
                dw_ref[k:k + 1, :] = jnp.sum(wacc_ref[k * SUBLANES:(k + 1) * SUBLANES, :], axis=0, keepdims=True)

    (du, dw, db), xo = _call(
        body, name, (nt,),
        [pl.BlockSpec((tm, D), lambda i: (i, 0)), _next_halo_spec(tm, D, T),
         pl.BlockSpec((tm, D2), lambda i: (i, 0)), _prev_halo_spec(tm, D2), pl.BlockSpec((K, D), lambda i: (0, 0))],
        [pl.BlockSpec((tm, D2), lambda i: (i, 0)), pl.BlockSpec((K, D), lambda i: (0, 0)), pl.BlockSpec((1, D2), lambda i: (0, 0))],
        [jax.ShapeDtypeStruct((T, D2), BF16), jax.ShapeDtypeStruct((K, D), F32), jax.ShapeDtypeStruct((1, D2), F32)],
        [dcu, dcu, u, u, w], ("arbitrary",),
        [pltpu.VMEM((tm + HALO, D), F32), pltpu.VMEM((tm + HALO, D), F32), pltpu.VMEM((tm, D), F32),
         pltpu.VMEM((K * SUBLANES, D), F32)], hosted=hosted)
    return du, dw, db, xo


def mm_cols(a, w, name, hosted=()):
    T, K = a.shape
    S, _, n = w.shape
    tm = _tile(T, WIDE_TOKEN_TILE)

    def body(a_ref, w_ref, o_ref):
        o_ref[...] = jnp.dot(a_ref[...], w_ref[...], preferred_element_type=F32)

    in_specs = [pl.BlockSpec((tm, K), lambda s, i: (i, 0)), pl.BlockSpec((None, K, n), lambda s, i: (s, 0, 0))]
    (out,), xo = _call(body, name, (S, T // tm), in_specs, [pl.BlockSpec((tm, n), lambda s, i: (i, s))],
                       [jax.ShapeDtypeStruct((T, S * n), F32)], [a, w], ("parallel", "parallel"), hosted=hosted)
    return out, xo


def rms_mm_cols(h, gain, w, bias, name, hosted=()):
    T, K = h.shape
    S, _, n = w.shape
    tm = _tile(T, TOKEN_TILE)

    def body(h_ref, gain_ref, w_ref, b_ref, n_ref, o_ref):
        x = h_ref[...]
        r = lax.rsqrt(jnp.mean(x * x, axis=-1, keepdims=True) + RMS_EPS)
        a = (x * r * gain_ref[...]).astype(n_ref.dtype)
        n_ref[...] = a
        for s in range(S):
            cols = slice(s * n, (s + 1) * n)
            o_ref[:, cols] = jnp.dot(a, w_ref[s], preferred_element_type=F32) + b_ref[:, cols]

    row = pl.BlockSpec((tm, K), lambda i: (i, 0))
    (n_out, out), xo = _call(
        body, name, (T // tm,),
        [row, pl.BlockSpec((1, K), lambda i: (0, 0)), pl.BlockSpec((S, K, n), lambda i: (0, 0, 0)),
         pl.BlockSpec((1, S * n), lambda i: (0, 0))],
        [row, pl.BlockSpec((tm, S * n), lambda i: (i, 0))],
        [jax.ShapeDtypeStruct((T, K), BF16), jax.ShapeDtypeStruct((T, S * n), F32)],
        [h, gain, w, bias], ("parallel",), hosted=hosted)
    return n_out, out, xo


def _load_weights(pairs, sems, S, G, i, p):
    def copies(seg):
        return [pltpu.make_async_copy(src.at[seg], dst.at[seg], sems.at[k, seg]) for k, (src, dst) in enumerate(pairs)]

    @pl.when((i == 0) & (p == 0))
    def _():
        for seg in range(S):
            for cp in copies(seg):
                cp.start()

    @pl.when((i == 0) & (p < S // G))
    def _():
        for j in range(G):
            for cp in copies(G * p + j):
                cp.wait()


def ffn_fwd(h, gain, weights, name, hosted=(), arriving=None):
    T, D = h.shape
    S, f, _ = weights[0].shape
    tm = _tile(T, TOKEN_TILE)
    rc = tm // FFN_ROW_CHUNKS
    chunks = [slice(r * rc, (r + 1) * rc) for r in range(FFN_ROW_CHUNKS)]
    G = FFN_FWD_SEGS_PER_STEP
    weights = list(weights)
    hosted = ([arriving.awaited_first()] if arriving is not None else []) + list(hosted)

    def body(h_ref, gain_ref, *refs):
        nw = len(weights)
        wg_hbm, wu_hbm, wd_hbm = list(refs[:nw]) + list(refs[nw + 9:])
        n_ref, g_ref, u_ref, gu_ref, o_ref, wg_v, wu_v, wd_v, sems = refs[nw:nw + 9]
        i, p = pl.program_id(0), pl.program_id(1)
        _load_weights([(wg_hbm, wg_v), (wu_hbm, wu_v), (wd_hbm, wd_v)], sems, S, G, i, p)

        @pl.when(p == 0)
        def _():
            x = h_ref[...]
            r = lax.rsqrt(jnp.mean(x * x, axis=-1, keepdims=True) + RMS_EPS)
            n_ref[...] = (x * r * gain_ref[...]).astype(n_ref.dtype)

        parts = []
        for rows in chunks:
            a = n_ref[rows, :]
            acc = None
            for j in range(G):
                seg = G * p + j
                g = lax.dot_general(a, wg_v[seg], _NT, preferred_element_type=F32)
                u = lax.dot_general(a, wu_v[seg], _NT, preferred_element_type=F32)
                gu = (g * _sigmoid(g) * u).astype(gu_ref.dtype)
                g_ref[j, rows, :] = g.astype(g_ref.dtype)
                u_ref[j, rows, :] = u.astype(u_ref.dtype)
                gu_ref[j, rows, :] = gu
                part = jnp.dot(gu, wd_v[seg], preferred_element_type=F32)
                acc = part if acc is None else acc + part
            parts.append(acc)

        @pl.when(p == 0)
        def _():
            for rows, part in zip(chunks, parts):
                o_ref[rows, :] = h_ref[rows, :] + part

        @pl.when(p > 0)
        def _():
            for rows, part in zip(chunks, parts):
                o_ref[rows, :] += part

    row = pl.BlockSpec((tm, D), lambda i, p: (i, 0))
    seg = pl.BlockSpec((G, tm, f), lambda i, p: (p, i, 0))
    hbm = pl.BlockSpec(memory_space=pl.ANY)
    segs = jax.ShapeDtypeStruct((S, T, f), BF16)
    outs, xo = _call(
        body, name, (T // tm, S // G),
        [row, pl.BlockSpec((1, D), lambda i, s: (0, 0))] + [hbm] * len(weights), [row, seg, seg, seg, row],
        [jax.ShapeDtypeStruct((T, D), BF16), segs, segs, segs, jax.ShapeDtypeStruct((T, D), F32)],
        [h, gain] + weights, ("arbitrary", "arbitrary"),
        [pltpu.VMEM((S, f, D), BF16), pltpu.VMEM((S, f, D), BF16), pltpu.VMEM((S, f, D), BF16), pltpu.SemaphoreType.DMA((3, S))],
        hosted=hosted)
    return (*outs, xo)


def ffn_bwd(dy, h, gain, g, u, wd, wg, wu, name, hosted=()):
    T, D = h.shape
    S, f, _ = wg.shape
    tm = _tile(T, FFN_BWD_TOKEN_TILE)
    nt = T // tm

    def body(dy_ref, h_ref, gain_ref, g_ref, u_ref, wd_hbm, wg_hbm, wu_hbm, dg_ref, du_ref, dh_ref, dhb_ref, dgain_ref,
             wd_v, wg_v, wu_v, sems):
        i = pl.program_id(0)
        _load_weights([(wd_hbm, wd_v), (wg_hbm, wg_v), (wu_hbm, wu_v)], sems, S, S, i, 0)
        dy_ = dy_ref[...]
        dyb = dy_.astype(BF16)
        dn = None
        for j in range(S):
            dgu = lax.dot_general(dyb, wd_v[j], _NT, preferred_element_type=F32)
            gv = g_ref[j].astype(F32)
            sg = _sigmoid(gv)
            dg = (dgu * u_ref[j].astype(F32) * (sg * (1.0 + gv * (1.0 - sg)))).astype(dg_ref.dtype)
            du = (dgu * (gv * sg)).astype(du_ref.dtype)
            dg_ref[j] = dg
            du_ref[j] = du
            part = jnp.dot(dg, wg_v[j], preferred_element_type=F32) + jnp.dot(du, wu_v[j], preferred_element_type=F32)
            dn = part if dn is None else dn + part
        x = h_ref[...]
        r = lax.rsqrt(jnp.mean(x * x, axis=-1, keepdims=True) + RMS_EPS)
        xhat = x * r
        dxhat = dn * gain_ref[...]
        dh = dy_ + r * (dxhat - xhat * jnp.mean(dxhat * xhat, axis=-1, keepdims=True))
        dh_ref[...] = dh
        dhb_ref[...] = dh.astype(dhb_ref.dtype)
        pg = jnp.sum(dn * xhat, axis=0, keepdims=True)

        @pl.when(i == 0)
        def _():
            dgain_ref[...] = pg

        @pl.when(i > 0)
        def _():
            dgain_ref[...] += pg

    row = pl.BlockSpec((tm, D), lambda i: (i, 0))
    vec = pl.BlockSpec((1, D), lambda i: (0, 0))
    seg = pl.BlockSpec((S, tm, f), lambda i: (0, i, 0))
    hbm = pl.BlockSpec(memory_space=pl.ANY)
    segs = jax.ShapeDtypeStruct((S, T, f), BF16)
    outs, xo = _call(
        body, name, (nt,),
        [row, row, vec, seg, seg, hbm, hbm, hbm], [seg, seg, row, row, vec],
        [segs, segs, jax.ShapeDtypeStruct((T, D), F32), jax.ShapeDtypeStruct((T, D), BF16), jax.ShapeDtypeStruct((1, D), F32)],
        [dy, h, gain, g, u, wd, wg, wu], ("arbitrary",),
        [pltpu.VMEM((S, f, D), BF16), pltpu.VMEM((S, f, D), BF16), pltpu.VMEM((S, f, D), BF16),
         pltpu.SemaphoreType.DMA((3, S))], hosted=hosted)
    return (*outs, xo)


_NT = (((1,), (1,)), ((), ()))
_TN = (((0,), (0,)), ((), ()))


def nt_cols_rms(dy, w, h, gain, dres, name, hosted=(), also_bf16=False):
    T, K = h.shape
    S, _, n = w.shape
    tm = _tile(T, TOKEN_TILE)

    def body(dy_ref, w_ref, h_ref, gain_ref, dres_ref, dh_ref, dgain_ref, *rest):
        i = pl.program_id(0)
        dn = None
        for s in range(S):
            part = lax.dot_general(dy_ref[:, s * n:(s + 1) * n], w_ref[s], _NT, preferred_element_type=F32)
            dn = part if dn is None else dn + part
        x = h_ref[...]
        r = lax.rsqrt(jnp.mean(x * x, axis=-1, keepdims=True) + RMS_EPS)
        xhat = x * r
        dxhat = dn * gain_ref[...]
        dh = dres_ref[...] + r * (dxhat - xhat * jnp.mean(dxhat * xhat, axis=-1, keepdims=True))
        dh_ref[...] = dh
        if also_bf16:
            rest[0][...] = dh.astype(BF16)
        pg = jnp.sum(dn * xhat, axis=0, keepdims=True)

        @pl.when(i == 0)
        def _():
            dgain_ref[...] = pg

        @pl.when(i > 0)
        def _():
            dgain_ref[...] += pg

    row = pl.BlockSpec((tm, K), lambda i: (i, 0))
    vec = pl.BlockSpec((1, K), lambda i: (0, 0))
    out_specs, out_shape = [row, vec], [jax.ShapeDtypeStruct((T, K), F32), jax.ShapeDtypeStruct((1, K), F32)]
    if also_bf16:
        out_specs, out_shape = out_specs + [row], out_shape + [jax.ShapeDtypeStruct((T, K), BF16)]
    outs, xo = _call(
        body, name, (T // tm,),
        [pl.BlockSpec((tm, S * n), lambda i: (i, 0)), pl.BlockSpec((S, K, n), lambda i: (0, 0, 0)), row, vec, row],
        out_specs, out_shape, [dy, w, h, gain, dres], ("arbitrary",), hosted=hosted)
    return (*outs, xo)


def tn_grad(a, dy, S, a_by_seg, name, hosted=()):
    T = dy.shape[0]
    tt = _tile(T, GRAD_TOKEN_TILE)
    G = GRAD_SEGS_PER_STEP
    if a_by_seg:
        R, C = a.shape[2], dy.shape[1]
        a_spec = pl.BlockSpec((G, tt, R), lambda p, t: (p, t, 0))
        b_spec = pl.BlockSpec((tt, C), lambda p, t: (t, 0))
    else:
        R, C = a.shape[1], dy.shape[1] // S
        a_spec = pl.BlockSpec((tt, R), lambda p, t: (t, 0))
        b_spec = pl.BlockSpec((tt, G * C), lambda p, t: (t, p))
    Rh = R // 2
    nt = T // tt

    def body(a_ref, b_ref, o_ref, acc_ref):
        t = pl.program_id(1)
        parts = []
        for j in range(G):
            a_j = a_ref[j] if a_by_seg else a_ref[...]
            b_j = b_ref[...] if a_by_seg else b_ref[:, j * C:(j + 1) * C]
            parts.append(lax.dot_general(a_j, b_j.astype(BF16), _TN, preferred_element_type=F32))

        @pl.when(t == 0)
        def _():
            for j in range(G):
                acc_ref[j] = parts[j]

        @pl.when(t > 0)
        def _():
            for j in range(G):
                acc_ref[j] += parts[j]

        @pl.when(t == nt - 1)
        def _():
            for j in range(G):
                o_ref[0, j] = acc_ref[j, :Rh, :].astype(o_ref.dtype)
                o_ref[1, j] = acc_ref[j, Rh:, :].astype(o_ref.dtype)

    (gh,), xo = _call(
        body, name, (S // G, nt), [a_spec, b_spec], [pl.BlockSpec((2, G, Rh, C), lambda p, t: (0, p, 0, 0))],
        [jax.ShapeDtypeStruct((2, S, Rh, C), BF16)], [a, dy], ("parallel", "arbitrary"), [pltpu.VMEM((G, R, C), F32)],
        hosted=hosted)
    return gh, xo


def tn_grad_square(a, dy, S, name, hosted=()):
    T, K = a.shape
    N = dy.shape[1]
    tt = _tile(T, GRAD_TOKEN_TILE)
    nt = T // tt
    Rh = K // S // 2

    def body(a_ref, b_ref, o_ref, acc_ref):
        t = pl.program_id(0)
        part = lax.dot_general(a_ref[...], b_ref[...].astype(BF16), _TN, preferred_element_type=F32)

        @pl.when(t == 0)
        def _():
            acc_ref[...] = part

        @pl.when(t > 0)
        def _():
            acc_ref[...] += part

        @pl.when(t == nt - 1)
        def _():
            for s in range(S):
                for hf in range(2):
                    r0 = (2 * s + hf) * Rh
                    o_ref[hf, s] = acc_ref[r0:r0 + Rh, :].astype(o_ref.dtype)

    (gh,), xo = _call(
        body, name, (nt,), [pl.BlockSpec((tt, K), lambda t: (t, 0)), pl.BlockSpec((tt, N), lambda t: (t, 0))],
        [pl.BlockSpec((2, S, Rh, N), lambda t: (0, 0, 0, 0))], [jax.ShapeDtypeStruct((2, S, Rh, N), BF16)],
        [a, dy], ("arbitrary",), [pltpu.VMEM((K, N), F32)], hosted=hosted)
    return gh, xo


def _place():
    x, y, c = lax.axis_index("x"), lax.axis_index("y"), lax.axis_index("c")
    chips = [(1 - x, y), (x, 1 - y), (1 - x, 1 - y)]
    return x, y, c, chips


def _remote(src, dst, send_sem, recv_sem, dev):
    return pltpu.make_async_remote_copy(src_ref=src, dst_ref=dst, send_sem=send_sem, recv_sem=recv_sem,
                                        device_id=dev, device_id_type=MESH)


def small_allreduce(v, name, hosted=()):
    rows, W = v.shape

    def body(v_ref, o_ref, sib_ref, pair_ref, chips_ref, send_sems, recv_sems):
        x, y, c, chips = _place()
        me = 2 * x + y
        swap = _remote(v_ref, sib_ref, send_sems.at[3], recv_sems.at[3], (x, y, 1 - c))
        swap.start()
        swap.wait()
        mine, other = v_ref[...], sib_ref[...]
        pair_ref[...] = jnp.where(c == 0, mine, other) + jnp.where(c == 0, other, mine)
        sends = []
        for j, (px, py) in enumerate(chips):
            cp = _remote(pair_ref, chips_ref.at[me], send_sems.at[j], recv_sems.at[j], (px, py, c))
            cp.start()
            sends.append(cp)
        chips_ref[me] = pair_ref[...]
        for j, (px, py) in enumerate(chips):
            blk = chips_ref.at[2 * px + py]
            _remote(blk, blk, send_sems.at[j], recv_sems.at[j], (px, py, c)).wait_recv()
        for cp in sends:
            cp.wait_send()
        o_ref[...] = (chips_ref[0] + chips_ref[1]) + (chips_ref[2] + chips_ref[3])

    vm = pl.BlockSpec(memory_space=pltpu.VMEM)
    (out,), xo = _call(
        body, name, (), [vm], [vm], [jax.ShapeDtypeStruct((rows, W), F32)], [v], (),
        [pltpu.VMEM((rows, W), F32), pltpu.VMEM((rows, W), F32), pltpu.VMEM((N_CHIPS, rows, W), F32),
         pltpu.SemaphoreType.DMA((4,)), pltpu.SemaphoreType.DMA((4,))], hosted=hosted)
    return out, xo


def _gather_p1_copies(srcs, bufs, ssem, rsem, base):
    x, y, c, chips = _place()
    me, sib = 2 * x + y, (x, y, 1 - c)
    sends, recvs = [], []
    for k, (src, buf) in enumerate(zip(srcs, bufs)):
        rh = src.shape[0] // 2
        s0 = base + 4 * k
        sends.append(_remote(src, buf.at[me], ssem.at[s0 + 3], rsem.at[s0 + 3], sib))
        recvs.append(_remote(buf.at[me], buf.at[me], ssem.at[s0 + 3], rsem.at[s0 + 3], sib))
        for j, (px, py) in enumerate(chips):
            sends.append(_remote(src.at[pl.ds(c * rh, rh)], buf.at[me, pl.ds(c * rh, rh)], ssem.at[s0 + j], rsem.at[s0 + j], (px, py, c)))
            blk = buf.at[2 * px + py, pl.ds(c * rh, rh)]
            recvs.append(_remote(blk, blk, ssem.at[s0 + j], rsem.at[s0 + j], (px, py, c)))
    return sends, recvs


def _gather_p2_copies(bufs, ssem, rsem, base):
    x, y, c, chips = _place()
    sib = (x, y, 1 - c)
    sends, recvs = [], []
    for k, buf in enumerate(bufs):
        rh = buf.shape[1] // 2
        for j, (px, py) in enumerate(chips):
            s0 = base + 3 * k + j
            blk = buf.at[2 * px + py, pl.ds(c * rh, rh)]
            sends.append(_remote(blk, blk, ssem.at[s0], rsem.at[s0], sib))
            got = buf.at[2 * px + py, pl.ds((1 - c) * rh, rh)]
            recvs.append(_remote(got, got, ssem.at[s0], rsem.at[s0], sib))
    return sends, recvs


def _gathered_shape(s):
    return jax.ShapeDtypeStruct((N_CHIPS,) + s.shape, s.dtype)


def gather_p1(shards):
    return _Exchange(shards, [_gathered_shape(s) for s in shards], {}, 4 * len(shards),
                     lambda xi, xo, ss, rs: _gather_p1_copies(xi, xo, ss, rs, 0))


def gather_p2(bufs):
    return _Exchange(bufs, [jax.ShapeDtypeStruct(b.shape, b.dtype) for b in bufs], {k: k for k in range(len(bufs))},
                     3 * len(bufs), lambda xi, xo, ss, rs: _gather_p2_copies(xo, ss, rs, 0))


def gather_whole(whole, begun):
    nw, n = len(whole), len(whole) + len(begun)
    shards = list(whole) + list(begun)
    return _Exchange(shards, [_gathered_shape(s) for s in shards], {}, 4 * n + 3 * nw,
                     lambda xi, xo, ss, rs: _gather_p1_copies(xi, xo, ss, rs, 0),
                     then=lambda xi, xo, ss, rs: _gather_p2_copies(xo[:nw], ss, rs, 4 * n))


def gather_small(v):
    def copies(xi, xo, ssem, rsem):
        x, y, c, chips = _place()
        me, sib = 2 * x + y, (x, y, 1 - c)
        sends = [_remote(xi[0], xo[0].at[me], ssem.at[3], rsem.at[3], sib)]
        recvs = [_remote(xo[0].at[me], xo[0].at[me], ssem.at[3], rsem.at[3], sib)]
        for j, (px, py) in enumerate(chips):
            sends.append(_remote(xi[0], xo[0].at[me], ssem.at[j], rsem.at[j], (px, py, c)))
            blk = xo[0].at[2 * px + py]
            recvs.append(_remote(blk, blk, ssem.at[j], rsem.at[j], (px, py, c)))
        return sends, recvs

    return _Exchange([v], [_gathered_shape(v)], {}, 4, copies)


def gather_all(v):
    def copies(xi, xo, ssem, rsem):
        x, y, c, _ = _place()
        sends, recvs = [], []
        for m in range(1, N_DEV):
            px, py, pc = (1 - x) if m & 4 else x, (1 - y) if m & 2 else y, (1 - c) if m & 1 else c
            sends.append(_remote(xi[0], xo[0].at[4 * x + 2 * y + c], ssem.at[m - 1], rsem.at[m - 1], (px, py, pc)))
            blk = xo[0].at[4 * px + 2 * py + pc]
            recvs.append(_remote(blk, blk, ssem.at[m - 1], rsem.at[m - 1], (px, py, pc)))
        return sends, recvs

    return _Exchange([v], [jax.ShapeDtypeStruct((N_DEV,) + v.shape, v.dtype)], {}, N_DEV - 1, copies)


def run_exchanges(exchanges, name):
    return _call(lambda: None, name, (), [], [], [], [], (), hosted=exchanges)[1]


def sibling_halves(grads):
    def copies(xi, xo, ssem, rsem):
        x, y, c, _ = _place()
        sends = [_remote(xi[k].at[1 - c], xo[k], ssem.at[k], rsem.at[k], (x, y, 1 - c)) for k in range(len(grads))]
        return sends, sends

    return _Exchange(grads, [jax.ShapeDtypeStruct(g.shape[1:], g.dtype) for g in grads], {}, len(grads), copies)


def pair_sum(ghs, recvs, cidx, name):
    n = len(ghs)
    S = ghs[0].shape[1]

    def body(c_ref, *refs):
        for k in range(n):
            a_ref, b_ref, o_ref = refs[2 * k], refs[2 * k + 1], refs[2 * n + k]
            o_ref[...] = (a_ref[...].astype(F32) + b_ref[...].astype(F32)).astype(o_ref.dtype)

    in_specs, out_specs, out_shape, args = [], [], [], []
    for gh, recv in zip(ghs, recvs):
        _, _, Rh, C = gh.shape
        in_specs += [pl.BlockSpec((None, None, Rh, C), lambda s, c_ref: (c_ref[0], s, 0, 0)),
                     pl.BlockSpec((None, Rh, C), lambda s, c_ref: (s, 0, 0))]
        out_specs.append(pl.BlockSpec((None, Rh, C), lambda s, c_ref: (s, 0, 0)))
        out_shape.append(jax.ShapeDtypeStruct((S, Rh, C), BF16))
        args += [gh, recv]
    return pl.pallas_call(
        body, name=name, out_shape=out_shape,
        grid_spec=pltpu.PrefetchScalarGridSpec(num_scalar_prefetch=1, grid=(S,), in_specs=in_specs, out_specs=out_specs),
        compiler_params=_params(("parallel",)),
    )(cidx, *args)


def scatter_p1(parts):
    def copies(xi, xo, ssem, rsem):
        x, y, c, chips = _place()
        me, sib = 2 * x + y, (x, y, 1 - c)
        sends, recvs = [], []
        for k in range(len(parts)):
            s0 = 4 * k
            sends.append(_remote(xi[k].at[me], xo[k].at[me, c], ssem.at[s0 + 3], rsem.at[s0 + 3], sib))
            own = xo[k].at[me, 1 - c]
            recvs.append(_remote(own, own, ssem.at[s0 + 3], rsem.at[s0 + 3], sib))
            for j, (px, py) in enumerate(chips):
                sends.append(_remote(xi[k].at[2 * px + py], xo[k].at[me, c], ssem.at[s0 + j], rsem.at[s0 + j], (px, py, c)))
                blk = xo[k].at[2 * px + py, c]
                recvs.append(_remote(blk, blk, ssem.at[s0 + j], rsem.at[s0 + j], (px, py, c)))
        return sends, recvs

    return _Exchange(parts, [jax.ShapeDtypeStruct((p.shape[0], 2) + p.shape[1:], p.dtype) for p in parts], {},
                     4 * len(parts), copies)


def scatter_p2(bufs):
    def copies(xi, xo, ssem, rsem):
        x, y, c, chips = _place()
        sib = (x, y, 1 - c)
        sends, recvs = [], []
        for k in range(len(bufs)):
            for j, (px, py) in enumerate(chips):
                s0 = 3 * k + j
                blk = xo[k].at[2 * px + py, c]
                sends.append(_remote(blk, blk, ssem.at[s0], rsem.at[s0], sib))
                got = xo[k].at[2 * px + py, 1 - c]
                recvs.append(_remote(got, got, ssem.at[s0], rsem.at[s0], sib))
        return sends, recvs

    return _Exchange(bufs, [jax.ShapeDtypeStruct(b.shape, b.dtype) for b in bufs], {k: k for k in range(len(bufs))},
                     3 * len(bufs), copies)


def _adamw_math(w, g, m, v):
    m = ADAM_B1 * m + (1.0 - ADAM_B1) * g
    v = ADAM_B2 * v + (1.0 - ADAM_B2) * (g * g)
    m_hat = m / (1.0 - ADAM_B1 ** ADAM_STEP)
    v_hat = v / (1.0 - ADAM_B2 ** ADAM_STEP)
    delta = -ADAM_LR * (m_hat / (jnp.sqrt(v_hat) + ADAM_EPS) + ADAM_WD * w)
    return delta, m, v


def adamw_reduce(tensors, place, lyr, bases, name):
    n = len(tensors)
    L, R, C = tensors[0][0].shape
    Rh = R // 2
    rb = _tile(Rh, ROW_TILE, 2 * SUBLANES)
    nb = Rh // rb

    def body(place_ref, *refs):
        mine = (place_ref[1] == pl.program_id(0))
        for k in range(n):
            p_ref, b0, b1, b2, b3, w_ref, m_ref, v_ref = refs[8 * k:8 * k + 8]
            go_ref, d_ref, mo_ref, vo_ref = refs[len(refs) - 4 * n + 4 * k:len(refs) - 4 * n + 4 * k + 4]
            g = None
            for p, b in enumerate((b0, b1, b2, b3)):
                val = jnp.where(mine & (place_ref[0] == p), p_ref[...], b[...]).astype(F32)
                g = val if g is None else g + val
            d, mn, vn = _adamw_math(w_ref[...], g, m_ref[...], v_ref[...])
            go_ref[...] = g
            d_ref[...] = d
            mo_ref[...] = mn
            vo_ref[...] = vn

    def buf_spec(p):
        def idx(h, i, pr):
            own = (pr[0] == p) & (pr[1] == h)
            return (p, jnp.where(own, 1 - h, h), i, 0)
        return pl.BlockSpec((None, None, rb, C), idx)

    blk = pl.BlockSpec((None, rb, C), lambda h, i, pr: (lyr, h * nb + i, 0))
    in_specs, args = [], []
    for w, m, v, buf, part in tensors:
        in_specs += [pl.BlockSpec((None, rb, C), lambda h, i, pr: (pr[0], i, 0))] + [buf_spec(p) for p in range(N_CHIPS)] + [blk] * 3
        args += [part, buf, buf, buf, buf, w, m, v]
    aliases = {}
    if bases is not None:
        in_specs += [pl.BlockSpec(memory_space=pl.ANY)] * (4 * n)
        aliases = {len(args) + k: k for k in range(4 * n)}
        args += list(bases)
    shp = jax.ShapeDtypeStruct((L, R, C), F32)
    flat = _call(body, name, (2, nb), in_specs, [blk] * (4 * n), [shp] * (4 * n), args, ("parallel", "parallel"),
                 prefetch=[place], own_aliases=aliases)[0]
    return flat


def small_update(late, early, own, place, entries, loss_row, name):
    ne = len(entries)
    D = late.shape[1]

    def body(place_ref, late_ref, early_ref, own_ref, *refs):
        ins, outs = refs[:3 * ne], refs[3 * ne:]
        ch = place_ref[0]
        me = 2 * place_ref[0] + place_ref[1]

        def early_sum(rs, cs):
            acc = None
            for d in range(N_DEV):
                val = jnp.where(me == d, own_ref[rs, cs], early_ref[d, rs, cs])
                acc = val if acc is None else acc + val
            return acc

        outs[4 * ne][...] = early_sum(slice(loss_row, loss_row + 1), slice(0, LANES))[:, 0:1]
        for e, (source, row0, kind, w, _, _) in enumerate(entries):
            r, width = w.shape[0], w.shape[-1]
            from_late = lambda rs, cs: late_ref[rs, cs]
            gsum = early_sum if source == "early" else from_late

            if kind == "layers":
                for j, (src, rw) in enumerate(row0):
                    gj = (early_sum if src == "early" else from_late)(slice(rw, rw + 1), slice(0, D))
                    at = (slice(j, j + 1), slice(None))
                    d, mn, vn = _adamw_math(ins[3 * e][at], gj, ins[3 * e + 1][at], ins[3 * e + 2][at])
                    outs[4 * e][at] = gj
                    outs[4 * e + 1][at] = d
                    outs[4 * e + 2][at] = mn
                    outs[4 * e + 3][at] = vn
                continue
            if kind == "full":
                g = gsum(slice(row0, row0 + r), slice(0, D))
            elif kind in ("cols", "rows"):
                g = gsum(slice(row0, row0 + r), slice(0, width))
                for q in range(1, N_CHIPS):
                    g = jnp.where(ch == q, gsum(slice(row0, row0 + r), slice(q * width, (q + 1) * width)), g)
            else:
                per_row = D // width
                g = gsum(slice(row0, row0 + 1), slice(0, width))
                for q in range(1, N_CHIPS):
                    rr = row0 + q // per_row
                    cc = (q % per_row) * width
                    g = jnp.where(ch == q, gsum(slice(rr, rr + 1), slice(cc, cc + width)), g)
            for j in ([slice(None)] if kind != "rows" else range(r)):
                gj = g if kind != "rows" else g[j:j + 1, :]
                d, mn, vn = _adamw_math(ins[3 * e][j], gj, ins[3 * e + 1][j], ins[3 * e + 2][j])
                outs[4 * e][j] = gj
                outs[4 * e + 1][j] = d
                outs[4 * e + 2][j] = mn
                outs[4 * e + 3][j] = vn

    vm = pl.BlockSpec(memory_space=pltpu.VMEM)
    args, out_shape = [], []
    for _, _, _, w, m, v in entries:
        args += [w, m, v]
        out_shape += [jax.ShapeDtypeStruct(w.shape, F32)] * 4
    out_shape.append(jax.ShapeDtypeStruct((1, 1), F32))
    return pl.pallas_call(
        body, name=name,
        in_specs=[pl.BlockSpec(memory_space=pltpu.SMEM), vm, vm, vm] + [vm] * (3 * ne),
        out_specs=[vm] * (4 * ne + 1), out_shape=out_shape,
        compiler_params=pltpu.CompilerParams(vmem_limit_bytes=VMEM_LIMIT),
    )(place, late, early, own, *args)


def _pack_rows(items, width, name):
    starts, at = [], 0
    for it in items:
        starts.append(at)
        at += -(-it.shape[0] // SUBLANES) * SUBLANES
    total = at

    def body(*refs):
        o_ref = refs[-1]
        o_ref[...] = jnp.zeros_like(o_ref)
        for it_ref, r0 in zip(refs[:-1], starts):
            if len(it_ref.shape) == 3:
                for j in range(it_ref.shape[0]):
                    o_ref[r0 + j:r0 + j + 1, :] = it_ref[j]
            elif it_ref.shape == (1, 1):
                o_ref[r0:r0 + 1, :] = jnp.broadcast_to(it_ref[...], (1, width))
            else:
                o_ref[r0:r0 + it_ref.shape[0], :] = it_ref[...]

    vm = pl.BlockSpec(memory_space=pltpu.VMEM)
    packed = pl.pallas_call(body, name=name, in_specs=[vm] * len(items), out_specs=vm,
                            out_shape=jax.ShapeDtypeStruct((total, width), F32))(*items)
    return packed, starts


def kernel(x, a_norm, a_w_in, a_conv, a_w_out, b_norm, b_w_pw1, b_b_pw1, b_conv, b_b_conv, b_ln_g, b_ln_b, b_w_pw2, b_b_pw2, ffn_norm, ffn_w_gate, ffn_w_up, ffn_w_down, final_norm, loss_target, m_a_norm, m_a_w_in, m_a_conv, m_a_w_out, m_b_norm, m_b_w_pw1, m_b_b_pw1, m_b_conv, m_b_b_conv, m_b_ln_g, m_b_ln_b, m_b_w_pw2, m_b_b_pw2, m_ffn_norm, m_ffn_w_gate, m_ffn_w_up, m_ffn_w_down, m_final_norm, v_a_norm, v_a_w_in, v_a_conv, v_a_w_out, v_b_norm, v_b_w_pw1, v_b_b_pw1, v_b_conv, v_b_b_conv, v_b_ln_g, v_b_ln_b, v_b_w_pw2, v_b_b_pw2, v_ffn_norm, v_ffn_w_gate, v_ffn_w_up, v_ffn_w_down, v_final_norm):
    T, D = x.shape[1], x.shape[2]
    Dq = D // N_CHIPS
    cx, cy, cc = lax.axis_index("x"), lax.axis_index("y"), lax.axis_index("c")
    chip = (2 * cx + cy).astype(jnp.int32).reshape(1)
    cidx = cc.astype(jnp.int32).reshape(1)
    h0 = x.reshape(T, D)
    tgt = loss_target.reshape(T, D)

    rows3 = lambda t: jnp.swapaxes(t, 0, 1)
    small_shards = [rows3(a_conv), b_norm, b_b_pw1.reshape(2, Dq), rows3(b_conv), b_b_conv, b_ln_g, b_ln_b, b_b_pw2]
    packed, st = _pack_rows(small_shards, Dq, "pack_small")

    tr = lambda t: jnp.swapaxes(t, 1, 2)
    w_gate, m_gate, v_gate = tr(ffn_w_gate), tr(m_ffn_w_gate), tr(v_ffn_w_gate)
    w_up, m_up, v_up = tr(ffn_w_up), tr(m_ffn_w_up), tr(v_ffn_w_up)
    s_in = a_w_in[0].astype(BF16)
    (s_out, s_pw1, s_pw2, *s_ffn), (g_in,) = cast_weights(
        [(a_w_out, 0), (b_w_pw1, 0), (b_w_pw2, 0)] + [(t, l) for t in (w_gate, w_up, ffn_w_down) for l in (0, 1)],
        "cast_weights", hosted=[gather_p1([s_in])])
    s_gate, s_up, s_down = s_ffn[0:2], s_ffn[2:4], s_ffn[4:6]

    n0, (g_in,) = rms_fwd(h0, a_norm, "rms_a", hosted=[gather_p2([g_in])])
    bcv, (g_out, gate0, sw) = mm_cols(n0, g_in, "mm_w_in", hosted=[gather_p1([s_out, s_gate[0]]), gather_small(packed)])

    def whole(k, r):
        return jnp.transpose(sw[:, st[k]:st[k] + r, :], (1, 0, 2)).reshape(r, D)

    a_conv_f, b_norm_f = whole(0, 3), whole(1, 1)
    b_b_pw1_f = sw[:, st[2]:st[2] + 2, :].reshape(1, 2 * D)
    b_conv_f, b_b_conv_f, b_ln_g_f, b_ln_b_f, b_b_pw2_f = whole(3, b_conv.shape[1]), whole(4, 1), whole(5, 1), whole(6, 1), whole(7, 1)
    ya, h1, (g_out, up0, down0, gate0) = gateconv_fwd(bcv, a_conv_f, gather_p2([g_out]), h0, "gateconv_fwd",
                                                      hosted=[gather_p1([s_up[0], s_down[0]]), gather_p2([gate0])])
    g_out = g_out.reshape(1, D, D)
    n1, fg0, fu0, gu0, h2, (up0, down0, g_pw1, g_pw2, gate1, up1) = ffn_fwd(
        h1, ffn_norm[0:1], [gate0], "ffn_fwd0", arriving=gather_p2([up0, down0]),
        hosted=[gather_whole([s_pw1, s_pw2], [s_gate[1], s_up[1]])])
    g_pw2 = g_pw2.reshape(1, D, D)
    n2, ub, (down1, gate1, up1) = rms_mm_cols(h2, b_norm_f, g_pw1, b_b_pw1_f, "mm_pw1",
                                              hosted=[gather_p1([s_down[1]]), gather_p2([gate1, up1])])
    cu, sb, h3, (down1,) = bconv_fwd(ub, b_conv_f, b_b_conv_f, b_ln_g_f, b_ln_b_f, g_pw2, b_b_pw2_f, h2, "bconv_fwd",
                                     hosted=[gather_p2([down1])])
    n3, fg1, fu1, gu1, h4, _ = ffn_fwd(h3, ffn_norm[1:2], [gate1, up1, down1], "ffn_fwd1")
    loss_part, dh4, dh4_b, d_final = loss_head(h4, final_norm.reshape(1, D), tgt, "loss_head")

    place = jnp.concatenate([chip, cidx])

    def pair_sums(ghs, from_sib, tags):
        return pair_sum(ghs, from_sib, cidx, "pair_sum_" + "_".join(tags))

    def upd(wmvs, bufs, parts, tag):
        flat = None
        for lyr in range(len(bufs[0])):
            tensors = [(w, m, v, b[lyr], p[lyr]) for (w, m, v), b, p in zip(wmvs, bufs, parts)]
            flat = adamw_reduce(tensors, place, lyr, flat, "adamw_%s%d" % (tag, lyr))
        return [flat[4 * k:4 * k + 4] for k in range(len(wmvs))]

    dg1, du1, dh3, dh3_b, d_fn1, _ = ffn_bwd(dh4, h3, ffn_norm[1:2], fg1, fu1, down1, gate1, up1, "ffn_bwd1")
    gh_down1, _ = tn_grad(gu1, dh4_b, N_CHIPS, True, "tn_down1")
    gh_gate1, _ = tn_grad(dg1, n3, N_CHIPS, True, "tn_gate1")
    gh_up1, _ = tn_grad(du1, n3, N_CHIPS, True, "tn_up1")
    f1 = [gh_gate1, gh_up1, gh_down1]

    dcu, d_ln_g, d_ln_b, d_b_conv, d_b_pw2, sib_f1 = pw2_ln_bwd(dh3, g_pw2, cu, b_ln_g_f, b_ln_b_f, "pw2_ln_bwd",
                                                                hosted=[sibling_halves(f1)])
    p_f1 = pair_sums(f1, sib_f1, ["gate1", "up1", "down1"])
    gh_pw2, _ = tn_grad_square(sb, dh3_b, N_CHIPS, "tn_pw2")
    dub, d_bconv_w, d_b_pw1, buf_f1 = bconv_bwd(dcu, ub, b_conv_f, "bconv_bwd", hosted=[scatter_p1(p_f1)])
    gh_pw1, _ = tn_grad(n2, dub, N_CHIPS, False, "tn_pw1")
    b_grp = [gh_pw1, gh_pw2]
    dh2, d_b_norm, dh2_b, (*buf_f1, sib_pw1, sib_pw2) = nt_cols_rms(
        dub, g_pw1, h2, b_norm_f, dh3, "nt_pw1", hosted=[scatter_p2(buf_f1), sibling_halves(b_grp)], also_bf16=True)
    sib_b = [sib_pw1, sib_pw2]
    p_b = pair_sums(b_grp, sib_b, ["pw1", "pw2"])

    early_grads = [d_b_norm, d_b_pw1.reshape(2, D), d_bconv_w, d_b_conv, d_ln_g, d_ln_b, d_b_pw2, d_fn1, d_final, loss_part]
    epacked, es = _pack_rows(early_grads, D, "pack_small_grads_early")
    dg0, du0, dh1, dh1_b, d_fn0, (*buf_b, eall) = ffn_bwd(dh2, h1, ffn_norm[0:1], fg0, fu0, down0, gate0, up0, "ffn_bwd0",
                                                         hosted=[scatter_p1(p_b), gather_all(epacked)])
    gh_down0, _ = tn_grad(gu0, dh2_b, N_CHIPS, True, "tn_down0")
    gh_gate0, (*buf_b, sib_down0) = tn_grad(dg0, n1, N_CHIPS, True, "tn_gate0",
                                            hosted=[scatter_p2(buf_b), sibling_halves([gh_down0])])
    p_down0 = pair_sums([gh_down0], [sib_down0], ["down0"])
    gh_up0, (buf_down0, sib_gate0) = tn_grad(du0, n1, N_CHIPS, True, "tn_up0",
                                             hosted=[scatter_p1(p_down0), sibling_halves([gh_gate0])])
    p_gate0 = pair_sums([gh_gate0], [sib_gate0], ["gate0"])
    gh_out, (buf_down0, sib_up0) = tn_grad_square(ya, dh1_b, N_CHIPS, "tn_w_out",
                                                  hosted=[scatter_p2([buf_down0]), sibling_halves([gh_up0])])
    p_up0 = pair_sums([gh_up0], [sib_up0], ["up0"])
    dbcv, d_aconv_w, (buf_gate0, sib_out) = gateconv_bwd(dh1_b, g_out, bcv, a_conv_f, "gateconv_bwd",
                                                         hosted=[scatter_p1(p_gate0), sibling_halves([gh_out])])
    p_out = pair_sums([gh_out], [sib_out], ["out"])
    gh_in, (buf_up0, buf_out, buf_gate0) = tn_grad(n0, dbcv, N_CHIPS, False, "tn_w_in",
                                                   hosted=[scatter_p1(p_up0 + p_out), scatter_p2([buf_gate0])])
    sib_in = run_exchanges([sibling_halves([gh_in])], "reduce_in_siblings")
    p_in = pair_sums([gh_in], sib_in, ["in"])
    grad_x, d_a_norm, (buf_in, buf_up0, buf_out) = nt_cols_rms(
        dbcv, g_in, h0, a_norm, dh1, "nt_w_in", hosted=[scatter_p1(p_in), scatter_p2([buf_up0, buf_out])])
    p_f0 = [p_gate0[0], p_up0[0], p_down0[0]]

    lpacked, ls = _pack_rows([d_a_norm, d_aconv_w, d_fn0], D, "pack_small_grads_late")
    lall, (buf_in,) = small_allreduce(lpacked, "allreduce_small_grads", hosted=[scatter_p2([buf_in])])
    buf_a, p_a = [buf_in, buf_out], [p_in[0], p_out[0]]

    r_gate, r_up, r_down = upd([(w_gate, m_gate, v_gate), (w_up, m_up, v_up), (ffn_w_down, m_ffn_w_down, v_ffn_w_down)],
                               [[buf_gate0, buf_f1[0]], [buf_up0, buf_f1[1]], [buf_down0, buf_f1[2]]],
                               [[p_f0[0], p_f1[0]], [p_f0[1], p_f1[1]], [p_f0[2], p_f1[2]]], "ffn")
    r_gate, r_up = [tr(t) for t in r_gate], [tr(t) for t in r_up]
    (r_pw1,) = upd([(b_w_pw1, m_b_w_pw1, v_b_w_pw1)], [[buf_b[0]]], [[p_b[0]]], "pw1")
    r_pw2, r_out = upd([(b_w_pw2, m_b_w_pw2, v_b_w_pw2), (a_w_out, m_a_w_out, v_a_w_out)],
                       [[buf_b[1]], [buf_a[1]]], [[p_b[1]], [p_a[1]]], "pw2_out")
    (r_in,) = upd([(a_w_in, m_a_w_in, v_a_w_in)], [[buf_a[0]]], [[p_a[0]]], "w_in")
    entries = [
        ("late", ls[0], "full", a_norm, m_a_norm, v_a_norm),
        ("late", ls[1], "rows", rows3(a_conv), rows3(m_a_conv), rows3(v_a_conv)),
        ("early", es[0], "cols", b_norm, m_b_norm, v_b_norm),
        ("early", es[1], "flat2", b_b_pw1, m_b_b_pw1, v_b_b_pw1),
        ("early", es[2], "rows", rows3(b_conv), rows3(m_b_conv), rows3(v_b_conv)),
        ("early", es[3], "cols", b_b_conv, m_b_b_conv, v_b_b_conv),
        ("early", es[4], "cols", b_ln_g, m_b_ln_g, v_b_ln_g),
        ("early", es[5], "cols", b_ln_b, m_b_ln_b, v_b_ln_b),
        ("early", es[6], "cols", b_b_pw2, m_b_b_pw2, v_b_b_pw2),
        (None, [("late", ls[2]), ("early", es[7])], "layers", ffn_norm, m_ffn_norm, v_ffn_norm),
        ("early", es[8], "full", final_norm.reshape(1, D), m_final_norm.reshape(1, D), v_final_norm.reshape(1, D)),
    ]
    so = small_update(lall, eall, epacked, place, entries, es[9], "small_update")
    sm = [so[4 * e:4 * e + 4] for e in range(len(entries))]

    def shaped(e, like):
        return [t.reshape(like.shape) for t in sm[e]]

    r_a_norm, r_a_conv, r_b_norm, r_b_b_pw1 = shaped(0, a_norm), shaped(1, a_conv), shaped(2, b_norm), shaped(3, b_b_pw1)
    r_b_conv, r_b_b_conv, r_b_ln_g, r_b_ln_b = shaped(4, b_conv), shaped(5, b_b_conv), shaped(6, b_ln_g), shaped(7, b_ln_b)
    r_b_b_pw2, r_ffn_norm, r_final = shaped(8, b_b_pw2), sm[9], shaped(10, final_norm)

    loss = so[4 * len(entries)].reshape(())
    order =[r_a_norm, r_in, r_a_conv, r_out, r_b_norm, r_pw1, r_b_b_pw1, r_b_conv, r_b_b_conv, r_b_ln_g, r_b_ln_b,
             r_pw2, r_b_b_pw2, r_ffn_norm, r_gate, r_up, r_down, r_final]
    outs = [loss, grad_x.reshape(x.shape)]
    for field in range(4):
        outs += [r[field] for r in order]
    return tuple(outs)
```

```python
import functools

import jax
import jax.numpy as jnp
from jax import lax
from jax.experimental import pallas as pl
from jax.experimental.pallas import tpu as pltpu

RMS_EPS = 1e-6
LN_EPS = 1e-5
ADAM_LR = 0.001
ADAM_B1 = 0.9
ADAM_B2 = 0.999
ADAM_EPS = 1e-08
ADAM_WD = 0.01
ADAM_STEP = 10

N_CHIPS = 4
N_DEV = 8
LANES = 128
SUBLANES = 8
HALO = 32
CONV_ROWS = 64
TOKEN_TILE = 512
WIDE_TOKEN_TILE = 1024
GRAD_TOKEN_TILE = 2048
GRAD_SEGS_PER_STEP = 2
FFN_ROW_CHUNKS = 2
FFN_FWD_SEGS_PER_STEP = 4
FFN_BWD_TOKEN_TILE = 256
ROW_TILE = 256
CAST_STEPS = 4
VMEM_LIMIT = 56 * 1024 * 1024
MESH = pl.DeviceIdType.MESH
BF16 = jnp.bfloat16
F32 = jnp.float32


def _tile(n, pref, mult=SUBLANES):
    t = min(n, pref) // mult * mult
    while n % t:
        t -= mult
    return t


def _params(sem):
    return pltpu.CompilerParams(dimension_semantics=sem, vmem_limit_bytes=VMEM_LIMIT)


def _sigmoid(x):
    return 0.5 * jnp.tanh(0.5 * x) + 0.5


class _Exchange:
    def __init__(self, ins, outs, aliases, n_sems, copies, then=None):
        self.ins, self.outs, self.aliases, self.n_sems, self.copies = list(ins), list(outs), dict(aliases), n_sems, copies
        self.then = then
        self.early = False

    def awaited_first(self):
        self.early = True
        return self

    def start(self, xi, xo, ssem, rsem):
        for cp in self.copies(xi, xo, ssem, rsem)[0]:
            cp.start()

    def finish(self, xi, xo, ssem, rsem):
        sends, recvs = self.copies(xi, xo, ssem, rsem)
        for cp in recvs:
            cp.wait_recv()
        if self.then is not None:
            sends2, recvs2 = self.then(xi, xo, ssem, rsem)
            for cp in sends2:
                cp.start()
            for cp in recvs2:
                cp.wait_recv()
            sends = sends + sends2
        for cp in sends:
            cp.wait_send()


def _call(body, name, grid, in_specs, out_specs, out_shape, args, sem, scratch_shapes=(), hosted=(), prefetch=(),
          own_aliases=None):
    in_specs, out_specs, out_shape = list(in_specs), list(out_specs), list(out_shape)
    scratch_shapes, hosted, prefetch = list(scratch_shapes), list(hosted), list(prefetch)
    n_pre, n_in, n_out, n_scr = len(prefetch), len(args), len(out_shape), len(scratch_shapes)
    x_in = [a for ex in hosted for a in ex.ins]
    x_out = [o for ex in hosted for o in ex.outs]
    aliases = {n_pre + i: o for i, o in (own_aliases or {}).items()}
    at_in, at_out = n_pre + n_in, n_out
    for ex in hosted:
        for i, o in ex.aliases.items():
            aliases[at_in + i] = at_out + o
        at_in += len(ex.ins)
        at_out += len(ex.outs)
    sems = [pltpu.SemaphoreType.DMA((ex.n_sems,)) for ex in hosted for _ in range(2)]

    def wrapped(*refs):
        pre, refs = refs[:n_pre], refs[n_pre:]
        ins, xi = refs[:n_in], refs[n_in:n_in + len(x_in)]
        refs = refs[n_in + len(x_in):]
        outs, xo = refs[:n_out], refs[n_out:n_out + len(x_out)]
        refs = refs[n_out + len(x_out):]
        scr, sm = refs[:n_scr], refs[n_scr:]
        views, a, b = [], 0, 0
        for e, ex in enumerate(hosted):
            views.append((xi[a:a + len(ex.ins)], xo[b:b + len(ex.outs)], sm[2 * e], sm[2 * e + 1]))
            a += len(ex.ins)
            b += len(ex.outs)
        first = last = None
        for ax, g in enumerate(grid):
            f, l = pl.program_id(ax) == 0, pl.program_id(ax) == g - 1
            first, last = (f, l) if first is None else (first & f, last & l)

        def begin():
            for ex, v in zip(hosted, views):
                ex.start(*v)
            for ex, v in zip(hosted, views):
                if ex.early:
                    ex.finish(*v)

        def end():
            for ex, v in zip(hosted, views):
                if not ex.early:
                    ex.finish(*v)

        if hosted and grid:
            pl.when(first)(begin)
        elif hosted:
            begin()
        early_refs = [r for ex, v in zip(hosted, views) if ex.early for r in v[1]]
        body(*pre, *ins, *outs, *scr, *early_refs)
        if hosted and grid:
            pl.when(last)(end)
        elif hosted:
            end()

    hbm = pl.BlockSpec(memory_space=pl.ANY)
    all_in, all_out = in_specs + [hbm] * len(x_in), out_specs + [hbm] * len(x_out)
    kw = dict(name=name, out_shape=out_shape + x_out, input_output_aliases=aliases,
              compiler_params=_params(tuple("arbitrary" for _ in grid) if hosted else sem))
    if prefetch:
        kw["grid_spec"] = pltpu.PrefetchScalarGridSpec(num_scalar_prefetch=n_pre, grid=grid, in_specs=all_in,
                                                       out_specs=all_out, scratch_shapes=scratch_shapes + sems)
    else:
        kw.update(grid=grid, in_specs=all_in, out_specs=all_out, scratch_shapes=scratch_shapes + sems)
    res = pl.pallas_call(wrapped, **kw)(*prefetch, *args, *x_in)
    return list(res[:n_out]), list(res[n_out:])


def rms_cast_weights(h, gain, tensors, name, hosted=()):
    T, D = h.shape
    tm = T // CAST_STEPS
    assert tm * CAST_STEPS == T and tm % (2 * SUBLANES) == 0
    n = len(tensors)

    def body(h_ref, g_ref, *refs):
        x = h_ref[...]
        r = lax.rsqrt(jnp.mean(x * x, axis=-1, keepdims=True) + RMS_EPS)
        refs[n][...] = (x * r * g_ref[...]).astype(refs[n].dtype)
        for w_ref, o_ref in zip(refs[:n], refs[n + 1:]):
            o_ref[...] = w_ref[...].astype(o_ref.dtype)

    in_specs = [pl.BlockSpec((tm, D), lambda i: (i, 0)), pl.BlockSpec((1, D), lambda i: (0, 0))]
    out_specs, out_shape = [pl.BlockSpec((tm, D), lambda i: (i, 0))], [jax.ShapeDtypeStruct((T, D), BF16)]
    for w, lyr in tensors:
        R, C = w.shape[-2:]
        rb = R // CAST_STEPS
        assert rb * CAST_STEPS == R and rb % (2 * SUBLANES) == 0
        in_specs.append(pl.BlockSpec((None, rb, C), lambda i, lyr=lyr: (lyr, i, 0)))
        out_specs.append(pl.BlockSpec((rb, C), lambda i: (i, 0)))
        out_shape.append(jax.ShapeDtypeStruct((R, C), BF16))
    return _call(body, name, (CAST_STEPS,), in_specs, out_specs, out_shape, [h, gain] + [w for w, _ in tensors],
                 ("parallel",), hosted=hosted)


def loss_head(h, gain, tgt, name):
    T, D = h.shape
    tm = _tile(T, TOKEN_TILE)

    def body(h_ref, g_ref, t_ref, loss_ref, dh_ref, dhb_ref, dg_ref):
        i = pl.program_id(0)
        x = h_ref[...]
        g = g_ref[...]
        r = lax.rsqrt(jnp.mean(x * x, axis=-1, keepdims=True) + RMS_EPS)
        xhat = x * r
        diff = xhat * g - t_ref[...]
        part_loss = 0.5 * jnp.sum(jnp.mean(diff * diff, axis=-1, keepdims=True), axis=0, keepdims=True)
        dy = diff * (1.0 / D)
        dxhat = dy * g
        dh = r * (dxhat - xhat * jnp.mean(dxhat * xhat, axis=-1, keepdims=True))
        dh_ref[...] = dh
        dhb_ref[...] = dh.astype(dhb_ref.dtype)
        part = jnp.sum(dy * xhat, axis=0, keepdims=True)

        @pl.when(i == 0)
        def _():
            dg_ref[...] = part
            loss_ref[...] = part_loss

        @pl.when(i > 0)
        def _():
            dg_ref[...] += part
            loss_ref[...] += part_loss

    row = pl.BlockSpec((tm, D), lambda i: (i, 0))
    vec = pl.BlockSpec((1, D), lambda i: (0, 0))
    return pl.pallas_call(
        body, name=name, grid=(T // tm,),
        in_specs=[row, vec, row],
        out_specs=[pl.BlockSpec((1, 1), lambda i: (0, 0)), row, row, vec],
        out_shape=[jax.ShapeDtypeStruct((1, 1), F32), jax.ShapeDtypeStruct((T, D), F32),
                   jax.ShapeDtypeStruct((T, D), BF16), jax.ShapeDtypeStruct((1, D), F32)],
        compiler_params=_params(("arbitrary",)),
    )(h, gain, tgt)


def _prev_halo_spec(tm, width):
    return pl.BlockSpec((HALO, width), lambda i: (jnp.maximum(i * (tm // HALO) - 1, 0), 0))


def _next_halo_spec(tm, width, T):
    return pl.BlockSpec((HALO, width), lambda i: (jnp.minimum((i + 1) * (tm // HALO), T // HALO - 1), 0))


def _shifted(win, off, rows):
    if off % SUBLANES == 0:
        return win[off:off + rows]
    n = win.shape[0]
    return pltpu.roll(win, (n - off) % n, 0)[:rows]


def _rowsum8(x):
    acc = x[0:SUBLANES]
    for q in range(1, x.shape[0] // SUBLANES):
        acc = acc + x[q * SUBLANES:(q + 1) * SUBLANES]
    return acc


def _conv_loops(tm, D, per_block):
    def chunk(r, carry):
        t0 = pl.multiple_of(r * CONV_ROWS, CONV_ROWS)
        for lb in range(D // LANES):
            per_block(t0, slice(lb * LANES, (lb + 1) * LANES))
        return carry

    lax.fori_loop(0, tm // CONV_ROWS, chunk, 0)


def gateconv_fwd(bcv, w, w_out, res, name, hosted=()):
    T, D3 = bcv.shape
    D = D3 // 3
    K = w.shape[0]
    tm = _tile(T, TOKEN_TILE)
    wo_shape = w_out.outs[0].shape

    def body(x_ref, halo_ref, w_ref, res_ref, y_ref, h_ref, pad_ref, wo_v, sem, wo_hbm):
        i = pl.program_id(0)

        @pl.when(i == 0)
        def _():
            cp = pltpu.make_async_copy(wo_hbm, wo_v, sem)
            cp.start()
            cp.wait()

        pad_ref[HALO:, :] = x_ref[:, D:2 * D] * x_ref[:, 2 * D:]
        pad_ref[:HALO, :] = jnp.where(i > 0, halo_ref[:, D:2 * D] * halo_ref[:, 2 * D:], 0.0)

        def block(t0, ls):
            win = pad_ref[pl.ds(t0, CONV_ROWS + HALO), ls]
            acc = jnp.zeros((CONV_ROWS, LANES), F32)
            for k in range(K):
                acc = acc + w_ref[k:k + 1, ls] * _shifted(win, HALO - (K - 1) + k, CONV_ROWS)
            y_ref[pl.ds(t0, CONV_ROWS), ls] = (x_ref[pl.ds(t0, CONV_ROWS), ls] * acc).astype(y_ref.dtype)

        _conv_loops(tm, D, block)
        h_ref[...] = res_ref[...] + jnp.dot(y_ref[...], wo_v[...].reshape(D, D), preferred_element_type=F32)

    row = pl.BlockSpec((tm, D), lambda i: (i, 0))
    (y, h), xo = _call(
        body, name, (T // tm,),
        [pl.BlockSpec((tm, D3), lambda i: (i, 0)), _prev_halo_spec(tm, D3), pl.BlockSpec((K, D), lambda i: (0, 0)), row],
        [row, row], [jax.ShapeDtypeStruct((T, D), BF16), jax.ShapeDtypeStruct((T, D), F32)],
        [bcv, bcv, w, res], ("arbitrary",),
        [pltpu.VMEM((tm + HALO, D), F32), pltpu.VMEM(wo_shape, BF16), pltpu.SemaphoreType.DMA],
        hosted=[w_out.awaited_first()] + list(hosted))
    return y, h, xo


def gateconv_bwd(dh, w_out, bcv, w, name, hosted=()):
    T, D3 = bcv.shape
    D = D3 // 3
    K = w.shape[0]
    tm = _tile(T, TOKEN_TILE)
    nt = T // tm

    def body(dh_ref, dhn_ref, wo_ref, x_ref, xp_ref, xn_ref, w_ref, o_ref, dw_ref, cv_ref, dc_ref, wacc_ref, dy_ref):
        i = pl.program_id(0)
        dy_ref[...] = lax.dot_general(dh_ref[...], wo_ref[0], _NT, preferred_element_type=F32)
        dyn = lax.dot_general(dhn_ref[...], wo_ref[0], _NT, preferred_element_type=F32)
        cv_ref[HALO:, :] = x_ref[:, D:2 * D] * x_ref[:, 2 * D:]
        cv_ref[:HALO, :] = jnp.where(i > 0, xp_ref[:, D:2 * D] * xp_ref[:, 2 * D:], 0.0)
        dc_ref[:tm, :] = dy_ref[...] * x_ref[:, :D]
        dc_ref[tm:, :] = jnp.where(i < nt - 1, dyn * xn_ref[:, :D], 0.0)

        @pl.when(i == 0)
        def _():
            wacc_ref[...] = jnp.zeros_like(wacc_ref)

        def block(t0, ls):
            cwin = cv_ref[pl.ds(t0, CONV_ROWS + HALO), ls]
            dwin = dc_ref[pl.ds(t0, CONV_ROWS + HALO), ls]
            dcon = dwin[:CONV_ROWS]
            conv = jnp.zeros((CONV_ROWS, LANES), F32)
            dcv = jnp.zeros((CONV_ROWS, LANES), F32)
            for k in range(K):
                wk = w_ref[k:k + 1, ls]
                cs = _shifted(cwin, HALO - (K - 1) + k, CONV_ROWS)
                conv = conv + wk * cs
                dcv = dcv + wk * _shifted(dwin, (K - 1) - k, CONV_ROWS)
                wacc_ref[k * SUBLANES:(k + 1) * SUBLANES, ls] += _rowsum8(dcon * cs)
            rows = pl.ds(t0, CONV_ROWS)
            o_ref[rows, ls] = (dy_ref[rows, ls] * conv).astype(o_ref.dtype)
            o_ref[rows, pl.ds(D + ls.start, LANES)] = (dcv * x_ref[rows, pl.ds(2 * D + ls.start, LANES)]).astype(o_ref.dtype)
            o_ref[rows, pl.ds(2 * D + ls.start, LANES)] = (dcv * x_ref[rows, pl.ds(D + ls.start, LANES)]).astype(o_ref.dtype)

        _conv_loops(tm, D, block)

        @pl.when(i == nt - 1)
        def _():
            for k in range(K):
                dw_ref[k:k + 1, :] = jnp.sum(wacc_ref[k * SUBLANES:(k + 1) * SUBLANES, :], axis=0, keepdims=True)

    (dx, dw), xo = _call(
        body, name, (nt,),
        [pl.BlockSpec((tm, D), lambda i: (i, 0)), _next_halo_spec(tm, D, T), pl.BlockSpec((1, D, D), lambda i: (0, 0, 0)),
         pl.BlockSpec((tm, D3), lambda i: (i, 0)), _prev_halo_spec(tm, D3), _next_halo_spec(tm, D3, T),
         pl.BlockSpec((K, D), lambda i: (0, 0))],
        [pl.BlockSpec((tm, D3), lambda i: (i, 0)), pl.BlockSpec((K, D), lambda i: (0, 0))],
        [jax.ShapeDtypeStruct((T, D3), BF16), jax.ShapeDtypeStruct((K, D), F32)],
        [dh, dh, w_out, bcv, bcv, bcv, w], ("arbitrary",),
        [pltpu.VMEM((tm + HALO, D), F32), pltpu.VMEM((tm + HALO, D), F32), pltpu.VMEM((K * SUBLANES, D), F32),
         pltpu.VMEM((tm, D), F32)], hosted=hosted)
    return dx, dw, xo


def bconv_fwd(u, w, b_conv, ln_g, ln_b, w_out, b_out, res, name, hosted=()):
    T, D2 = u.shape
    D = D2 // 2
    K = w.shape[0]
    tm = _tile(T, TOKEN_TILE)

    def body(u_ref, halo_ref, w_ref, bc_ref, g_ref, b_ref, wo_ref, bo_ref, res_ref, cu_ref, s_ref, h_ref, pad_ref):
        i = pl.program_id(0)
        pad_ref[HALO:, :] = u_ref[:, :D] * _sigmoid(u_ref[:, D:])
        pad_ref[:HALO, :] = jnp.where(i > 0, halo_ref[:, :D] * _sigmoid(halo_ref[:, D:]), 0.0)

        def block(t0, ls):
            win = pad_ref[pl.ds(t0, CONV_ROWS + HALO), ls]
            acc = jnp.zeros((CONV_ROWS, LANES), F32)
            for k in range(K):
                acc = acc + w_ref[k:k + 1, ls] * _shifted(win, HALO - (K - 1) + k, CONV_ROWS)
            cu_ref[pl.ds(t0, CONV_ROWS), ls] = acc + bc_ref[:, ls]

        _conv_loops(tm, D, block)
        cu = cu_ref[...]
        mu = jnp.mean(cu, axis=-1, keepdims=True)
        xc = cu - mu
        rstd = lax.rsqrt(jnp.mean(xc * xc, axis=-1, keepdims=True) + LN_EPS)
        ln = xc * rstd * g_ref[...] + b_ref[...]
        s = (ln * _sigmoid(ln)).astype(s_ref.dtype)
        s_ref[...] = s
        h_ref[...] = res_ref[...] + bo_ref[...] + jnp.dot(s, wo_ref[0], preferred_element_type=F32)

    vec = pl.BlockSpec((1, D), lambda i: (0, 0))
    row = pl.BlockSpec((tm, D), lambda i: (i, 0))
    (cu, s, h), xo = _call(
        body, name, (T // tm,),
        [pl.BlockSpec((tm, D2), lambda i: (i, 0)), _prev_halo_spec(tm, D2), pl.BlockSpec((K, D), lambda i: (0, 0)), vec, vec, vec,
         pl.BlockSpec((1, D, D), lambda i: (0, 0, 0)), vec, row],
        [row, row, row], [jax.ShapeDtypeStruct((T, D), F32), jax.ShapeDtypeStruct((T, D), BF16), jax.ShapeDtypeStruct((T, D), F32)],
        [u, u, w, b_conv, ln_g, ln_b, w_out, b_out, res], ("parallel",), [pltpu.VMEM((tm + HALO, D), F32)], hosted=hosted)
    return cu, s, h, xo


def pw2_ln_bwd(dy, w, cu, ln_g, ln_b, name, hosted=()):
    T, D = cu.shape
    tm = _tile(T, TOKEN_TILE)

    def body(dy_ref, w_ref, cu_ref, g_ref, b_ref, dcu_ref, dg_ref, db_ref, dbc_ref, dbo_ref):
        i = pl.program_id(0)
        dy_ = dy_ref[...]
        ds = lax.dot_general(dy_.astype(BF16), w_ref[0], _NT, preferred_element_type=F32)
        cu_ = cu_ref[...]
        mu = jnp.mean(cu_, axis=-1, keepdims=True)
        xc = cu_ - mu
        rstd = lax.rsqrt(jnp.mean(xc * xc, axis=-1, keepdims=True) + LN_EPS)
        xh = xc * rstd
        ln = xh * g_ref[...] + b_ref[...]
        sg = _sigmoid(ln)
        dl = ds * (sg * (1.0 + ln * (1.0 - sg)))
        dxh = dl * g_ref[...]
        dcu = rstd * (dxh - jnp.mean(dxh, axis=-1, keepdims=True) - xh * jnp.mean(dxh * xh, axis=-1, keepdims=True))
        dcu_ref[...] = dcu
        pg = jnp.sum(dl * xh, axis=0, keepdims=True)
        pb = jnp.sum(dl, axis=0, keepdims=True)
        pc = jnp.sum(dcu, axis=0, keepdims=True)
        po = jnp.sum(dy_, axis=0, keepdims=True)

        @pl.when(i == 0)
        def _():
            dg_ref[...] = pg
            db_ref[...] = pb
            dbc_ref[...] = pc
            dbo_ref[...] = po

        @pl.when(i > 0)
        def _():
            dg_ref[...] += pg
            db_ref[...] += pb
            dbc_ref[...] += pc
            dbo_ref[...] += po

    vec = pl.BlockSpec((1, D), lambda i: (0, 0))
    row = pl.BlockSpec((tm, D), lambda i: (i, 0))
    vshape = jax.ShapeDtypeStruct((1, D), F32)
    outs, xo = _call(
        body, name, (T // tm,), [row, pl.BlockSpec((1, D, D), lambda i: (0, 0, 0)), row, vec, vec], [row, vec, vec, vec, vec],
        [jax.ShapeDtypeStruct((T, D), F32), vshape, vshape, vshape, vshape], [dy, w, cu, ln_g, ln_b], ("arbitrary",),
        hosted=hosted)
    return (*outs, xo)


def bconv_bwd(dcu, u, w, name, hosted=()):
    T, D2 = u.shape
    D = D2 // 2
    K = w.shape[0]
    tm = _tile(T, TOKEN_TILE)
    nt = T // tm

    def body(dc_ref, dcn_ref, u_ref, up_ref, w_ref, du_ref, dw_ref, db_ref, glu_ref, dpad_ref, dglu_ref, wacc_ref):
        i = pl.program_id(0)
        glu_ref[HALO:, :] = u_ref[:, :D] * _sigmoid(u_ref[:, D:])
        glu_ref[:HALO, :] = jnp.where(i > 0, up_ref[:, :D] * _sigmoid(up_ref[:, D:]), 0.0)
        dpad_ref[:tm, :] = dc_ref[...]
        dpad_ref[tm:, :] = jnp.where(i < nt - 1, dcn_ref[...], 0.0)

        @pl.when(i == 0)
        def _():
            wacc_ref[...] = jnp.zeros_like(wacc_ref)

        def block(t0, ls):
            gwin = glu_ref[pl.ds(t0, CONV_ROWS + HALO), ls]
            dwin = dpad_ref[pl.ds(t0, CONV_ROWS + HALO), ls]
            dcur = dwin[:CONV_ROWS]
            dglu = jnp.zeros((CONV_ROWS, LANES), F32)
            for k in range(K):
                dglu = dglu + w_ref[k:k + 1, ls] * _shifted(dwin, (K - 1) - k, CONV_ROWS)
                gs = _shifted(gwin, HALO - (K - 1) + k, CONV_ROWS)
                wacc_ref[k * SUBLANES:(k + 1) * SUBLANES, ls] += _rowsum8(dcur * gs)
            dglu_ref[pl.ds(t0, CONV_ROWS), ls] = dglu

        _conv_loops(tm, D, block)
        dglu = dglu_ref[...]
        a = u_ref[:, :D]
        sg = _sigmoid(u_ref[:, D:])
        da = dglu * sg
        dg = dglu * a * (sg * (1.0 - sg))
        du_ref[:, :D] = da.astype(du_ref.dtype)
        du_ref[:, D:] = dg.astype(du_ref.dtype)
        pa = jnp.sum(da, axis=0, keepdims=True)
        pg = jnp.sum(dg, axis=0, keepdims=True)

        @pl.when(i == 0)
        def _():
            db_ref[:, :D] = pa
            db_ref[:, D:] = pg

        @pl.when(i > 0)
        def _():
            db_ref[:, :D] += pa
            db_ref[:, D:] += pg

        @pl.when(i == nt - 1)
        def _():
            for k in range(K):
                dw_ref[k:k + 1, :] = jnp.sum(wacc_ref[k * SUBLANES:(k + 1) * SUBLANES, :], axis=0, keepdims=True)

    (du, dw, db), xo = _call(
        body, name, (nt,),
        [pl.BlockSpec((tm, D), lambda i: (i, 0)), _next_halo_spec(tm, D, T),
         pl.BlockSpec((tm, D2), lambda i: (i, 0)), _prev_halo_spec(tm, D2), pl.BlockSpec((K, D), lambda i: (0, 0))],
        [pl.BlockSpec((tm, D2), lambda i: (i, 0)), pl.BlockSpec((K, D), lambda i: (0, 0)), pl.BlockSpec((1, D2), lambda i: (0, 0))],
        [jax.ShapeDtypeStruct((T, D2), BF16), jax.ShapeDtypeStruct((K, D), F32), jax.ShapeDtypeStruct((1, D2), F32)],
        [dcu, dcu, u, u, w], ("arbitrary",),
        [pltpu.VMEM((tm + HALO, D), F32), pltpu.VMEM((tm + HALO, D), F32), pltpu.VMEM((tm, D), F32),
         pltpu.VMEM((K * SUBLANES, D), F32)], hosted=hosted)
    return du, dw, db, xo


def mm_cols(a, w, name, hosted=()):
    T, K = a.shape
    S, _, n = w.shape
    tm = _tile(T, WIDE_TOKEN_TILE)

    def body(a_ref, w_ref, o_ref):
        o_ref[...] = jnp.dot(a_ref[...], w_ref[...], preferred_element_type=F32)

    in_specs = [pl.BlockSpec((tm, K), lambda s, i: (i, 0)), pl.BlockSpec((None, K, n), lambda s, i: (s, 0, 0))]
    (out,), xo = _call(body, name, (S, T // tm), in_specs, [pl.BlockSpec((tm, n), lambda s, i: (i, s))],
                       [jax.ShapeDtypeStruct((T, S * n), F32)], [a, w], ("parallel", "parallel"), hosted=hosted)
    return out, xo


def rms_mm_cols(h, gain, w, bias, name, hosted=()):
    T, K = h.shape
    S, _, n = w.shape
    tm = _tile(T, TOKEN_TILE)

    def body(h_ref, gain_ref, w_ref, b_ref, n_ref, o_ref):
        x = h_ref[...]
        r = lax.rsqrt(jnp.mean(x * x, axis=-1, keepdims=True) + RMS_EPS)
        a = (x * r * gain_ref[...]).astype(n_ref.dtype)
        n_ref[...] = a
        for s in range(S):
            cols = slice(s * n, (s + 1) * n)
            o_ref[:, cols] = jnp.dot(a, w_ref[s], preferred_element_type=F32) + b_ref[:, cols]

    row = pl.BlockSpec((tm, K), lambda i: (i, 0))
    (n_out, out), xo = _call(
        body, name, (T // tm,),
        [row, pl.BlockSpec((1, K), lambda i: (0, 0)), pl.BlockSpec((S, K, n), lambda i: (0, 0, 0)),
         pl.BlockSpec((1, S * n), lambda i: (0, 0))],
        [row, pl.BlockSpec((tm, S * n), lambda i: (i, 0))],
        [jax.ShapeDtypeStruct((T, K), BF16), jax.ShapeDtypeStruct((T, S * n), F32)],
        [h, gain, w, bias], ("parallel",), hosted=hosted)
    return n_out, out, xo


def _load_weights(pairs, sems, S, G, i, p):
    def copies(seg):
        return [pltpu.make_async_copy(src.at[seg], dst.at[seg], sems.at[k, seg]) for k, (src, dst) in enumerate(pairs)]

    @pl.when((i == 0) & (p == 0))
    def _():
        for seg in range(S):
            for cp in copies(seg):
                cp.start()

    @pl.when((i == 0) & (p < S // G))
    def _():
        for j in range(G):
            for cp in copies(G * p + j):
                cp.wait()


def ffn_fwd(h, gain, weights, name, hosted=(), arriving=None):
    T, D = h.shape
    S, f, _ = weights[0].shape
    tm = _tile(T, TOKEN_TILE)
    rc = tm // FFN_ROW_CHUNKS
    chunks = [slice(r * rc, (r + 1) * rc) for r in range(FFN_ROW_CHUNKS)]
    G = FFN_FWD_SEGS_PER_STEP
    weights = list(weights)
    hosted = ([arriving.awaited_first()] if arriving is not None else []) + list(hosted)

    def body(h_ref, gain_ref, *refs):
        nw = len(weights)
        wg_hbm, wu_hbm, wd_hbm = list(refs[:nw]) + list(refs[nw + 9:])
        n_ref, g_ref, u_ref, gu_ref, o_ref, wg_v, wu_v, wd_v, sems = refs[nw:nw + 9]
        i, p = pl.program_id(0), pl.program_id(1)
        _load_weights([(wg_hbm, wg_v), (wu_hbm, wu_v), (wd_hbm, wd_v)], sems, S, G, i, p)

        @pl.when(p == 0)
        def _():
            x = h_ref[...]
            r = lax.rsqrt(jnp.mean(x * x, axis=-1, keepdims=True) + RMS_EPS)
            n_ref[...] = (x * r * gain_ref[...]).astype(n_ref.dtype)

        parts = []
        for rows in chunks:
            a = n_ref[rows, :]
            acc = None
            for j in range(G):
                seg = G * p + j
                g = lax.dot_general(a, wg_v[seg], _NT, preferred_element_type=F32)
                u = lax.dot_general(a, wu_v[seg], _NT, preferred_element_type=F32)
                gu = (g * _sigmoid(g) * u).astype(gu_ref.dtype)
                g_ref[j, rows, :] = g.astype(g_ref.dtype)
                u_ref[j, rows, :] = u.astype(u_ref.dtype)
                gu_ref[j, rows, :] = gu
                part = jnp.dot(gu, wd_v[seg], preferred_element_type=F32)
                acc = part if acc is None else acc + part
            parts.append(acc)

        @pl.when(p == 0)
        def _():
            for rows, part in zip(chunks, parts):
                o_ref[rows, :] = h_ref[rows, :] + part

        @pl.when(p > 0)
        def _():
            for rows, part in zip(chunks, parts):
                o_ref[rows, :] += part

    row = pl.BlockSpec((tm, D), lambda i, p: (i, 0))
    seg = pl.BlockSpec((G, tm, f), lambda i, p: (p, i, 0))
    hbm = pl.BlockSpec(memory_space=pl.ANY)
    segs = jax.ShapeDtypeStruct((S, T, f), BF16)
    outs, xo = _call(
        body, name, (T // tm, S // G),
        [row, pl.BlockSpec((1, D), lambda i, s: (0, 0))] + [hbm] * len(weights), [row, seg, seg, seg, row],
        [jax.ShapeDtypeStruct((T, D), BF16), segs, segs, segs, jax.ShapeDtypeStruct((T, D), F32)],
        [h, gain] + weights, ("arbitrary", "arbitrary"),
        [pltpu.VMEM((S, f, D), BF16), pltpu.VMEM((S, f, D), BF16), pltpu.VMEM((S, f, D), BF16), pltpu.SemaphoreType.DMA((3, S))],
        hosted=hosted)
    return (*outs, xo)


def ffn_bwd(dy, h, gain, g, u, wd, wg, wu, name, hosted=()):
    T, D = h.shape
    S, f, _ = wg.shape
    tm = _tile(T, FFN_BWD_TOKEN_TILE)
    nt = T // tm

    def body(dy_ref, h_ref, gain_ref, g_ref, u_ref, wd_hbm, wg_hbm, wu_hbm, dg_ref, du_ref, dh_ref, dhb_ref, dgain_ref,
             wd_v, wg_v, wu_v, sems):
        i = pl.program_id(0)
        _load_weights([(wd_hbm, wd_v), (wg_hbm, wg_v), (wu_hbm, wu_v)], sems, S, S, i, 0)
        dy_ = dy_ref[...]
        dyb = dy_.astype(BF16)
        dn = None
        for j in range(S):
            dgu = lax.dot_general(dyb, wd_v[j], _NT, preferred_element_type=F32)
            gv = g_ref[j].astype(F32)
            sg = _sigmoid(gv)
            dg = (dgu * u_ref[j].astype(F32) * (sg * (1.0 + gv * (1.0 - sg)))).astype(dg_ref.dtype)
            du = (dgu * (gv * sg)).astype(du_ref.dtype)
            dg_ref[j] = dg
            du_ref[j] = du
            part = jnp.dot(dg, wg_v[j], preferred_element_type=F32) + jnp.dot(du, wu_v[j], preferred_element_type=F32)
            dn = part if dn is None else dn + part
        x = h_ref[...]
        r = lax.rsqrt(jnp.mean(x * x, axis=-1, keepdims=True) + RMS_EPS)
        xhat = x * r
        dxhat = dn * gain_ref[...]
        dh = dy_ + r * (dxhat - xhat * jnp.mean(dxhat * xhat, axis=-1, keepdims=True))
        dh_ref[...] = dh
        dhb_ref[...] = dh.astype(dhb_ref.dtype)
        pg = jnp.sum(dn * xhat, axis=0, keepdims=True)

        @pl.when(i == 0)
        def _():
            dgain_ref[...] = pg

        @pl.when(i > 0)
        def _():
            dgain_ref[...] += pg

    row = pl.BlockSpec((tm, D), lambda i: (i, 0))
    vec = pl.BlockSpec((1, D), lambda i: (0, 0))
    seg = pl.BlockSpec((S, tm, f), lambda i: (0, i, 0))
    hbm = pl.BlockSpec(memory_space=pl.ANY)
    segs = jax.ShapeDtypeStruct((S, T, f), BF16)
    outs, xo = _call(
        body, name, (nt,),
        [row, row, vec, seg, seg, hbm, hbm, hbm], [seg, seg, row, row, vec],
        [segs, segs, jax.ShapeDtypeStruct((T, D), F32), jax.ShapeDtypeStruct((T, D), BF16), jax.ShapeDtypeStruct((1, D), F32)],
        [dy, h, gain, g, u, wd, wg, wu], ("arbitrary",),
        [pltpu.VMEM((S, f, D), BF16), pltpu.VMEM((S, f, D), BF16), pltpu.VMEM((S, f, D), BF16),
         pltpu.SemaphoreType.DMA((3, S))], hosted=hosted)
    return (*outs, xo)


_NT = (((1,), (1,)), ((), ()))
_TN = (((0,), (0,)), ((), ()))


def nt_cols_rms(dy, w, h, gain, dres, name, hosted=(), also_bf16=False):
    T, K = h.shape
    S, _, n = w.shape
    tm = _tile(T, TOKEN_TILE)

    def body(dy_ref, w_ref, h_ref, gain_ref, dres_ref, dh_ref, dgain_ref, *rest):
        i = pl.program_id(0)
        dn = None
        for s in range(S):
            part = lax.dot_general(dy_ref[:, s * n:(s + 1) * n], w_ref[s], _NT, preferred_element_type=F32)
            dn = part if dn is None else dn + part
        x = h_ref[...]
        r = lax.rsqrt(jnp.mean(x * x, axis=-1, keepdims=True) + RMS_EPS)
        xhat = x * r
        dxhat = dn * gain_ref[...]
        dh = dres_ref[...] + r * (dxhat - xhat * jnp.mean(dxhat * xhat, axis=-1, keepdims=True))
        dh_ref[...] = dh
        if also_bf16:
            rest[0][...] = dh.astype(BF16)
        pg = jnp.sum(dn * xhat, axis=0, keepdims=True)

        @pl.when(i == 0)
        def _():
            dgain_ref[...] = pg

        @pl.when(i > 0)
        def _():
            dgain_ref[...] += pg

    row = pl.BlockSpec((tm, K), lambda i: (i, 0))
    vec = pl.BlockSpec((1, K), lambda i: (0, 0))
    out_specs, out_shape = [row, vec], [jax.ShapeDtypeStruct((T, K), F32), jax.ShapeDtypeStruct((1, K), F32)]
    if also_bf16:
        out_specs, out_shape = out_specs + [row], out_shape + [jax.ShapeDtypeStruct((T, K), BF16)]
    outs, xo = _call(
        body, name, (T // tm,),
        [pl.BlockSpec((tm, S * n), lambda i: (i, 0)), pl.BlockSpec((S, K, n), lambda i: (0, 0, 0)), row, vec, row],
        out_specs, out_shape, [dy, w, h, gain, dres], ("arbitrary",), hosted=hosted)
    return (*outs, xo)


def tn_grad(a, dy, S, a_by_seg, name, hosted=()):
    T = dy.shape[0]
    tt = _tile(T, GRAD_TOKEN_TILE)
    G = GRAD_SEGS_PER_STEP
    if a_by_seg:
        R, C = a.shape[2], dy.shape[1]
        a_spec = pl.BlockSpec((G, tt, R), lambda p, t: (p, t, 0))
        b_spec = pl.BlockSpec((tt, C), lambda p, t: (t, 0))
    else:
        R, C = a.shape[1], dy.shape[1] // S
        a_spec = pl.BlockSpec((tt, R), lambda p, t: (t, 0))
        b_spec = pl.BlockSpec((tt, G * C), lambda p, t: (t, p))
    Rh = R // 2
    nt = T // tt

    def body(a_ref, b_ref, o_ref, acc_ref):
        t = pl.program_id(1)
        parts = []
        for j in range(G):
            a_j = a_ref[j] if a_by_seg else a_ref[...]
            b_j = b_ref[...] if a_by_seg else b_ref[:, j * C:(j + 1) * C]
            parts.append(lax.dot_general(a_j, b_j.astype(BF16), _TN, preferred_element_type=F32))

        @pl.when(t == 0)
        def _():
            for j in range(G):
                acc_ref[j] = parts[j]

        @pl.when(t > 0)
        def _():
            for j in range(G):
                acc_ref[j] += parts[j]

        @pl.when(t == nt - 1)
        def _():
            for j in range(G):
                o_ref[0, j] = acc_ref[j, :Rh, :].astype(o_ref.dtype)
                o_ref[1, j] = acc_ref[j, Rh:, :].astype(o_ref.dtype)

    (gh,), xo = _call(
        body, name, (S // G, nt), [a_spec, b_spec], [pl.BlockSpec((2, G, Rh, C), lambda p, t: (0, p, 0, 0))],
        [jax.ShapeDtypeStruct((2, S, Rh, C), BF16)], [a, dy], ("parallel", "arbitrary"), [pltpu.VMEM((G, R, C), F32)],
        hosted=hosted)
    return gh, xo


def tn_grad_square(a, dy, S, name, hosted=()):
    T, K = a.shape
    N = dy.shape[1]
    tt = _tile(T, GRAD_TOKEN_TILE)
    nt = T // tt
    Rh = K // S // 2

    def body(a_ref, b_ref, o_ref, acc_ref):
        t = pl.program_id(0)
        part = lax.dot_general(a_ref[...], b_ref[...].astype(BF16), _TN, preferred_element_type=F32)

        @pl.when(t == 0)
        def _():
            acc_ref[...] = part

        @pl.when(t > 0)
        def _():
            acc_ref[...] += part

        @pl.when(t == nt - 1)
        def _():
            for s in range(S):
                for hf in range(2):
                    r0 = (2 * s + hf) * Rh
                    o_ref[hf, s] = acc_ref[r0:r0 + Rh, :].astype(o_ref.dtype)

    (gh,), xo = _call(
        body, name, (nt,), [pl.BlockSpec((tt, K), lambda t: (t, 0)), pl.BlockSpec((tt, N), lambda t: (t, 0))],
        [pl.BlockSpec((2, S, Rh, N), lambda t: (0, 0, 0, 0))], [jax.ShapeDtypeStruct((2, S, Rh, N), BF16)],
        [a, dy], ("arbitrary",), [pltpu.VMEM((K, N), F32)], hosted=hosted)
    return gh, xo


def _place():
    x, y, c = lax.axis_index("x"), lax.axis_index("y"), lax.axis_index("c")
    chips = [(1 - x, y), (x, 1 - y), (1 - x, 1 - y)]
    return x, y, c, chips


def _remote(src, dst, send_sem, recv_sem, dev):
    return pltpu.make_async_remote_copy(src_ref=src, dst_ref=dst, send_sem=send_sem, recv_sem=recv_sem,
                                        device_id=dev, device_id_type=MESH)


def small_allreduce(v, name, hosted=()):
    rows, W = v.shape

    def body(v_ref, o_ref, sib_ref, pair_ref, chips_ref, send_sems, recv_sems):
        x, y, c, chips = _place()
        me = 2 * x + y
        swap = _remote(v_ref, sib_ref, send_sems.at[3], recv_sems.at[3], (x, y, 1 - c))
        swap.start()
        swap.wait()
        mine, other = v_ref[...], sib_ref[...]
        pair_ref[...] = jnp.where(c == 0, mine, other) + jnp.where(c == 0, other, mine)
        sends = []
        for j, (px, py) in enumerate(chips):
            cp = _remote(pair_ref, chips_ref.at[me], send_sems.at[j], recv_sems.at[j], (px, py, c))
            cp.start()
            sends.append(cp)
        chips_ref[me] = pair_ref[...]
        for j, (px, py) in enumerate(chips):
            blk = chips_ref.at[2 * px + py]
            _remote(blk, blk, send_sems.at[j], recv_sems.at[j], (px, py, c)).wait_recv()
        for cp in sends:
            cp.wait_send()
        o_ref[...] = (chips_ref[0] + chips_ref[1]) + (chips_ref[2] + chips_ref[3])

    vm = pl.BlockSpec(memory_space=pltpu.VMEM)
    (out,), xo = _call(
        body, name, (), [vm], [vm], [jax.ShapeDtypeStruct((rows, W), F32)], [v], (),
        [pltpu.VMEM((rows, W), F32), pltpu.VMEM((rows, W), F32), pltpu.VMEM((N_CHIPS, rows, W), F32),
         pltpu.SemaphoreType.DMA((4,)), pltpu.SemaphoreType.DMA((4,))], hosted=hosted)
    return out, xo


def _gather_p1_copies(srcs, bufs, ssem, rsem, base):
    x, y, c, chips = _place()
    me, sib = 2 * x + y, (x, y, 1 - c)
    sends, recvs = [], []
    for k, (src, buf) in enumerate(zip(srcs, bufs)):
        rh = src.shape[0] // 2
        s0 = base + 4 * k
        sends.append(_remote(src, buf.at[me], ssem.at[s0 + 3], rsem.at[s0 + 3], sib))
        recvs.append(_remote(buf.at[me], buf.at[me], ssem.at[s0 + 3], rsem.at[s0 + 3], sib))
        for j, (px, py) in enumerate(chips):
            sends.append(_remote(src.at[pl.ds(c * rh, rh)], buf.at[me, pl.ds(c * rh, rh)], ssem.at[s0 + j], rsem.at[s0 + j], (px, py, c)))
            blk = buf.at[2 * px + py, pl.ds(c * rh, rh)]
            recvs.append(_remote(blk, blk, ssem.at[s0 + j], rsem.at[s0 + j], (px, py, c)))
    return sends, recvs


def _gather_p2_copies(bufs, ssem, rsem, base):
    x, y, c, chips = _place()
    sib = (x, y, 1 - c)
    sends, recvs = [], []
    for k, buf in enumerate(bufs):
        rh = buf.shape[1] // 2
        for j, (px, py) in enumerate(chips):
            s0 = base + 3 * k + j
            blk = buf.at[2 * px + py, pl.ds(c * rh, rh)]
            sends.append(_remote(blk, blk, ssem.at[s0], rsem.at[s0], sib))
            got = buf.at[2 * px + py, pl.ds((1 - c) * rh, rh)]
            recvs.append(_remote(got, got, ssem.at[s0], rsem.at[s0], sib))
    return sends, recvs


def _gathered_shape(s):
    return jax.ShapeDtypeStruct((N_CHIPS,) + s.shape, s.dtype)


def gather_p1(shards):
    return _Exchange(shards, [_gathered_shape(s) for s in shards], {}, 4 * len(shards),
                     lambda xi, xo, ss, rs: _gather_p1_copies(xi, xo, ss, rs, 0))


def gather_p2(bufs):
    return _Exchange(bufs, [jax.ShapeDtypeStruct(b.shape, b.dtype) for b in bufs], {k: k for k in range(len(bufs))},
                     3 * len(bufs), lambda xi, xo, ss, rs: _gather_p2_copies(xo, ss, rs, 0))


def gather_whole(whole, begun):
    nw, n = len(whole), len(whole) + len(begun)
    shards = list(whole) + list(begun)
    return _Exchange(shards, [_gathered_shape(s) for s in shards], {}, 4 * n + 3 * nw,
                     lambda xi, xo, ss, rs: _gather_p1_copies(xi, xo, ss, rs, 0),
                     then=lambda xi, xo, ss, rs: _gather_p2_copies(xo[:nw], ss, rs, 4 * n))


def gather_small(v):
    def copies(xi, xo, ssem, rsem):
        x, y, c, chips = _place()
        me, sib = 2 * x + y, (x, y, 1 - c)
        sends = [_remote(xi[0], xo[0].at[me], ssem.at[3], rsem.at[3], sib)]
        recvs = [_remote(xo[0].at[me], xo[0].at[me], ssem.at[3], rsem.at[3], sib)]
        for j, (px, py) in enumerate(chips):
            sends.append(_remote(xi[0], xo[0].at[me], ssem.at[j], rsem.at[j], (px, py, c)))
            blk = xo[0].at[2 * px + py]
            recvs.append(_remote(blk, blk, ssem.at[j], rsem.at[j], (px, py, c)))
        return sends, recvs

    return _Exchange([v], [_gathered_shape(v)], {}, 4, copies)


def gather_all(v):
    def copies(xi, xo, ssem, rsem):
        x, y, c, _ = _place()
        sends, recvs = [], []
        for m in range(1, N_DEV):
            px, py, pc = (1 - x) if m & 4 else x, (1 - y) if m & 2 else y, (1 - c) if m & 1 else c
            sends.append(_remote(xi[0], xo[0].at[4 * x + 2 * y + c], ssem.at[m - 1], rsem.at[m - 1], (px, py, pc)))
            blk = xo[0].at[4 * px + 2 * py + pc]
            recvs.append(_remote(blk, blk, ssem.at[m - 1], rsem.at[m - 1], (px, py, pc)))
        return sends, recvs

    return _Exchange([v], [jax.ShapeDtypeStruct((N_DEV,) + v.shape, v.dtype)], {}, N_DEV - 1, copies)


def run_exchanges(exchanges, name):
    return _call(lambda: None, name, (), [], [], [], [], (), hosted=exchanges)[1]


def sibling_halves(grads):
    def copies(xi, xo, ssem, rsem):
        x, y, c, _ = _place()
        sends = [_remote(xi[k].at[1 - c], xo[k], ssem.at[k], rsem.at[k], (x, y, 1 - c)) for k in range(len(grads))]
        return sends, sends

    return _Exchange(grads, [jax.ShapeDtypeStruct(g.shape[1:], g.dtype) for g in grads], {}, len(grads), copies)


def pair_sum(ghs, recvs, cidx, name):
    n = len(ghs)
    S = ghs[0].shape[1]

    def body(c_ref, *refs):
        for k in range(n):
            a_ref, b_ref, o_ref = refs[2 * k], refs[2 * k + 1], refs[2 * n + k]
            o_ref[...] = (a_ref[...].astype(F32) + b_ref[...].astype(F32)).astype(o_ref.dtype)

    in_specs, out_specs, out_shape, args = [], [], [], []
    for gh, recv in zip(ghs, recvs):
        _, _, Rh, C = gh.shape
        in_specs += [pl.BlockSpec((None, None, Rh, C), lambda s, c_ref: (c_ref[0], s, 0, 0)),
                     pl.BlockSpec((None, Rh, C), lambda s, c_ref: (s, 0, 0))]
        out_specs.append(pl.BlockSpec((None, Rh, C), lambda s, c_ref: (s, 0, 0)))
        out_shape.append(jax.ShapeDtypeStruct((S, Rh, C), BF16))
        args += [gh, recv]
    return pl.pallas_call(
        body, name=name, out_shape=out_shape,
        grid_spec=pltpu.PrefetchScalarGridSpec(num_scalar_prefetch=1, grid=(S,), in_specs=in_specs, out_specs=out_specs),
        compiler_params=_params(("parallel",)),
    )(cidx, *args)


def scatter_p1(parts):
    def copies(xi, xo, ssem, rsem):
        x, y, c, chips = _place()
        me, sib = 2 * x + y, (x, y, 1 - c)
        sends, recvs = [], []
        for k in range(len(parts)):
            s0 = 4 * k
            sends.append(_remote(xi[k].at[me], xo[k].at[me, c], ssem.at[s0 + 3], rsem.at[s0 + 3], sib))
            own = xo[k].at[me, 1 - c]
            recvs.append(_remote(own, own, ssem.at[s0 + 3], rsem.at[s0 + 3], sib))
            for j, (px, py) in enumerate(chips):
                sends.append(_remote(xi[k].at[2 * px + py], xo[k].at[me, c], ssem.at[s0 + j], rsem.at[s0 + j], (px, py, c)))
                blk = xo[k].at[2 * px + py, c]
                recvs.append(_remote(blk, blk, ssem.at[s0 + j], rsem.at[s0 + j], (px, py, c)))
        return sends, recvs

    return _Exchange(parts, [jax.ShapeDtypeStruct((p.shape[0], 2) + p.shape[1:], p.dtype) for p in parts], {},
                     4 * len(parts), copies)


def scatter_p2(bufs):
    def copies(xi, xo, ssem, rsem):
        x, y, c, chips = _place()
        sib = (x, y, 1 - c)
        sends, recvs = [], []
        for k in range(len(bufs)):
            for j, (px, py) in enumerate(chips):
                s0 = 3 * k + j
                blk = xo[k].at[2 * px + py, c]
                sends.append(_remote(blk, blk, ssem.at[s0], rsem.at[s0], sib))
                got = xo[k].at[2 * px + py, 1 - c]
                recvs.append(_remote(got, got, ssem.at[s0], rsem.at[s0], sib))
        return sends, recvs

    return _Exchange(bufs, [jax.ShapeDtypeStruct(b.shape, b.dtype) for b in bufs], {k: k for k in range(len(bufs))},
                     3 * len(bufs), copies)


def _adamw_math(w, g, m, v):
    m = ADAM_B1 * m + (1.0 - ADAM_B1) * g
    v = ADAM_B2 * v + (1.0 - ADAM_B2) * (g * g)
    m_hat = m / (1.0 - ADAM_B1 ** ADAM_STEP)
    v_hat = v / (1.0 - ADAM_B2 ** ADAM_STEP)
    delta = -ADAM_LR * (m_hat / (jnp.sqrt(v_hat) + ADAM_EPS) + ADAM_WD * w)
    return delta, m, v


def adamw_reduce(tensors, place, lyr, bases, name):
    n = len(tensors)
    L, R, C = tensors[0][0].shape
    Rh = R // 2
    rb = _tile(Rh, ROW_TILE, 2 * SUBLANES)
    nb = Rh // rb

    def body(place_ref, *refs):
        mine = (place_ref[1] == pl.program_id(0))
        for k in range(n):
            p_ref, b0, b1, b2, b3, w_ref, m_ref, v_ref = refs[8 * k:8 * k + 8]
            go_ref, d_ref, mo_ref, vo_ref = refs[len(refs) - 4 * n + 4 * k:len(refs) - 4 * n + 4 * k + 4]
            g = None
            for p, b in enumerate((b0, b1, b2, b3)):
                val = jnp.where(mine & (place_ref[0] == p), p_ref[...], b[...]).astype(F32)
                g = val if g is None else g + val
            d, mn, vn = _adamw_math(w_ref[...], g, m_ref[...], v_ref[...])
            go_ref[...] = g
            d_ref[...] = d
            mo_ref[...] = mn
            vo_ref[...] = vn

    def buf_spec(p):
        def idx(h, i, pr):
            own = (pr[0] == p) & (pr[1] == h)
            return (p, jnp.where(own, 1 - h, h), i, 0)
        return pl.BlockSpec((None, None, rb, C), idx)

    blk = pl.BlockSpec((None, rb, C), lambda h, i, pr: (lyr, h * nb + i, 0))
    in_specs, args = [], []
    for w, m, v, buf, part in tensors:
        in_specs += [pl.BlockSpec((None, rb, C), lambda h, i, pr: (pr[0], i, 0))] + [buf_spec(p) for p in range(N_CHIPS)] + [blk] * 3
        args += [part, buf, buf, buf, buf, w, m, v]
    aliases = {}
    if bases is not None:
        in_specs += [pl.BlockSpec(memory_space=pl.ANY)] * (4 * n)
        aliases = {len(args) + k: k for k in range(4 * n)}
        args += list(bases)
    shp = jax.ShapeDtypeStruct((L, R, C), F32)
    flat = _call(body, name, (2, nb), in_specs, [blk] * (4 * n), [shp] * (4 * n), args, ("parallel", "parallel"),
                 prefetch=[place], own_aliases=aliases)[0]
    return flat


def small_update(late, early, own, place, entries, loss_row, name):
    ne = len(entries)
    D = late.shape[1]

    def body(place_ref, late_ref, early_ref, own_ref, *refs):
        ins, outs = refs[:3 * ne], refs[3 * ne:]
        ch = place_ref[0]
        me = 2 * place_ref[0] + place_ref[1]

        def early_sum(rs, cs):
            acc = None
            for d in range(N_DEV):
                val = jnp.where(me == d, own_ref[rs, cs], early_ref[d, rs, cs])
                acc = val if acc is None else acc + val
            return acc

        outs[4 * ne][...] = early_sum(slice(loss_row, loss_row + 1), slice(0, LANES))[:, 0:1]
        for e, (source, row0, kind, w, _, _) in enumerate(entries):
            r, width = w.shape[0], w.shape[-1]
            from_late = lambda rs, cs: late_ref[rs, cs]
            gsum = early_sum if source == "early" else from_late

            if kind == "layers":
                for j, (src, rw) in enumerate(row0):
                    gj = (early_sum if src == "early" else from_late)(slice(rw, rw + 1), slice(0, D))
                    at = (slice(j, j + 1), slice(None))
                    d, mn, vn = _adamw_math(ins[3 * e][at], gj, ins[3 * e + 1][at], ins[3 * e + 2][at])
                    outs[4 * e][at] = gj
                    outs[4 * e + 1][at] = d
                    outs[4 * e + 2][at] = mn
                    outs[4 * e + 3][at] = vn
                continue
            if kind == "full":
                g = gsum(slice(row0, row0 + r), slice(0, D))
            elif kind in ("cols", "rows"):
                g = gsum(slice(row0, row0 + r), slice(0, width))
                for q in range(1, N_CHIPS):
                    g = jnp.where(ch == q, gsum(slice(row0, row0 + r), slice(q * width, (q + 1) * width)), g)
            else:
                per_row = D // width
                g = gsum(slice(row0, row0 + 1), slice(0, width))
                for q in range(1, N_CHIPS):
                    rr = row0 + q // per_row
                    cc = (q % per_row) * width
                    g = jnp.where(ch == q, gsum(slice(rr, rr + 1), slice(cc, cc + width)), g)
            for j in ([slice(None)] if kind != "rows" else range(r)):
                gj = g if kind != "rows" else g[j:j + 1, :]
                d, mn, vn = _adamw_math(ins[3 * e][j], gj, ins[3 * e + 1][j], ins[3 * e + 2][j])
                outs[4 * e][j] = gj
                outs[4 * e + 1][j] = d
                outs[4 * e + 2][j] = mn
                outs[4 * e + 3][j] = vn

    vm = pl.BlockSpec(memory_space=pltpu.VMEM)
    args, out_shape = [], []
    for _, _, _, w, m, v in entries:
        args += [w, m, v]
        out_shape += [jax.ShapeDtypeStruct(w.shape, F32)] * 4
    out_shape.append(jax.ShapeDtypeStruct((1, 1), F32))
    return pl.pallas_call(
        body, name=name,
        in_specs=[pl.BlockSpec(memory_space=pltpu.SMEM), vm, vm, vm] + [vm] * (3 * ne),
        out_specs=[vm] * (4 * ne + 1), out_shape=out_shape,
        compiler_params=pltpu.CompilerParams(vmem_limit_bytes=VMEM_LIMIT),
    )(place, late, early, own, *args)


def _pack_rows(items, width, name):
    starts, at = [], 0
    for it in items:
        starts.append(at)
        at += -(-it.shape[0] // SUBLANES) * SUBLANES
    total = at

    def body(*refs):
        o_ref = refs[-1]
        o_ref[...] = jnp.zeros_like(o_ref)
        for it_ref, r0 in zip(refs[:-1], starts):
            if len(it_ref.shape) == 3:
                for j in range(it_ref.shape[0]):
                    o_ref[r0 + j:r0 + j + 1, :] = it_ref[j]
            elif it_ref.shape == (1, 1):
                o_ref[r0:r0 + 1, :] = jnp.broadcast_to(it_ref[...], (1, width))
            else:
                o_ref[r0:r0 + it_ref.shape[0], :] = it_ref[...]

    vm = pl.BlockSpec(memory_space=pltpu.VMEM)
    packed = pl.pallas_call(body, name=name, in_specs=[vm] * len(items), out_specs=vm,
                            out_shape=jax.ShapeDtypeStruct((total, width), F32))(*items)
    return packed, starts


def kernel(x, a_norm, a_w_in, a_conv, a_w_out, b_norm, b_w_pw1, b_b_pw1, b_conv, b_b_conv, b_ln_g, b_ln_b, b_w_pw2, b_b_pw2, ffn_norm, ffn_w_gate, ffn_w_up, ffn_w_down, final_norm, loss_target, m_a_norm, m_a_w_in, m_a_conv, m_a_w_out, m_b_norm, m_b_w_pw1, m_b_b_pw1, m_b_conv, m_b_b_conv, m_b_ln_g, m_b_ln_b, m_b_w_pw2, m_b_b_pw2, m_ffn_norm, m_ffn_w_gate, m_ffn_w_up, m_ffn_w_down, m_final_norm, v_a_norm, v_a_w_in, v_a_conv, v_a_w_out, v_b_norm, v_b_w_pw1, v_b_b_pw1, v_b_conv, v_b_b_conv, v_b_ln_g, v_b_ln_b, v_b_w_pw2, v_b_b_pw2, v_ffn_norm, v_ffn_w_gate, v_ffn_w_up, v_ffn_w_down, v_final_norm):
    T, D = x.shape[1], x.shape[2]
    Dq = D // N_CHIPS
    cx, cy, cc = lax.axis_index("x"), lax.axis_index("y"), lax.axis_index("c")
    chip = (2 * cx + cy).astype(jnp.int32).reshape(1)
    cidx = cc.astype(jnp.int32).reshape(1)
    h0 = x.reshape(T, D)
    tgt = loss_target.reshape(T, D)

    rows3 = lambda t: jnp.swapaxes(t, 0, 1)
    small_shards = [rows3(a_conv), b_norm, b_b_pw1.reshape(2, Dq), rows3(b_conv), b_b_conv, b_ln_g, b_ln_b, b_b_pw2]
    packed, st = _pack_rows(small_shards, Dq, "pack_small")

    tr = lambda t: jnp.swapaxes(t, 1, 2)
    w_gate, m_gate, v_gate = tr(ffn_w_gate), tr(m_ffn_w_gate), tr(v_ffn_w_gate)
    w_up, m_up, v_up = tr(ffn_w_up), tr(m_ffn_w_up), tr(v_ffn_w_up)
    s_in = a_w_in[0].astype(BF16)
    (n0, s_out, s_pw1, s_pw2, *s_ffn), (g_in,) = rms_cast_weights(
        h0, a_norm, [(a_w_out, 0), (b_w_pw1, 0), (b_w_pw2, 0)] + [(t, l) for t in (w_gate, w_up, ffn_w_down) for l in (0, 1)],
        "rms_a_cast_weights", hosted=[gather_whole([s_in], [])])
    s_gate, s_up, s_down = s_ffn[0:2], s_ffn[2:4], s_ffn[4:6]
    bcv, (g_out, gate0, sw) = mm_cols(n0, g_in, "mm_w_in", hosted=[gather_p1([s_out, s_gate[0]]), gather_small(packed)])

    def whole(k, r):
        return jnp.transpose(sw[:, st[k]:st[k] + r, :], (1, 0, 2)).reshape(r, D)

    a_conv_f, b_norm_f = whole(0, 3), whole(1, 1)
    b_b_pw1_f = sw[:, st[2]:st[2] + 2, :].reshape(1, 2 * D)
    b_conv_f, b_b_conv_f, b_ln_g_f, b_ln_b_f, b_b_pw2_f = whole(3, b_conv.shape[1]), whole(4, 1), whole(5, 1), whole(6, 1), whole(7, 1)
    ya, h1, (g_out, up0, down0, gate0) = gateconv_fwd(bcv, a_conv_f, gather_p2([g_out]), h0, "gateconv_fwd",
                                                      hosted=[gather_p1([s_up[0], s_down[0]]), gather_p2([gate0])])
    g_out = g_out.reshape(1, D, D)
    n1, fg0, fu0, gu0, h2, (up0, down0, g_pw1, g_pw2, gate1, up1) = ffn_fwd(
        h1, ffn_norm[0:1], [gate0], "ffn_fwd0", arriving=gather_p2([up0, down0]),
        hosted=[gather_whole([s_pw1, s_pw2], [s_gate[1], s_up[1]])])
    g_pw2 = g_pw2.reshape(1, D, D)
    n2, ub, (down1, gate1, up1) = rms_mm_cols(h2, b_norm_f, g_pw1, b_b_pw1_f, "mm_pw1",
                                              hosted=[gather_p1([s_down[1]]), gather_p2([gate1, up1])])
    cu, sb, h3, (down1,) = bconv_fwd(ub, b_conv_f, b_b_conv_f, b_ln_g_f, b_ln_b_f, g_pw2, b_b_pw2_f, h2, "bconv_fwd",
                                     hosted=[gather_p2([down1])])
    n3, fg1, fu1, gu1, h4, _ = ffn_fwd(h3, ffn_norm[1:2], [gate1, up1, down1], "ffn_fwd1")
    loss_part, dh4, dh4_b, d_final = loss_head(h4, final_norm.reshape(1, D), tgt, "loss_head")

    place = jnp.concatenate([chip, cidx])

    def pair_sums(ghs, from_sib, tags):
        return pair_sum(ghs, from_sib, cidx, "pair_sum_" + "_".join(tags))

    def upd(wmvs, bufs, parts, tag):
        flat = None
        for lyr in range(len(bufs[0])):
            tensors = [(w, m, v, b[lyr], p[lyr]) for (w, m, v), b, p in zip(wmvs, bufs, parts)]
            flat = adamw_reduce(tensors, place, lyr, flat, "adamw_%s%d" % (tag, lyr))
        return [flat[4 * k:4 * k + 4] for k in range(len(wmvs))]

    dg1, du1, dh3, dh3_b, d_fn1, _ = ffn_bwd(dh4, h3, ffn_norm[1:2], fg1, fu1, down1, gate1, up1, "ffn_bwd1")
    gh_down1, _ = tn_grad(gu1, dh4_b, N_CHIPS, True, "tn_down1")
    gh_gate1, _ = tn_grad(dg1, n3, N_CHIPS, True, "tn_gate1")
    gh_up1, _ = tn_grad(du1, n3, N_CHIPS, True, "tn_up1")
    f1 = [gh_gate1, gh_up1, gh_down1]

    dcu, d_ln_g, d_ln_b, d_b_conv, d_b_pw2, sib_f1 = pw2_ln_bwd(dh3, g_pw2, cu, b_ln_g_f, b_ln_b_f, "pw2_ln_bwd",
                                                                hosted=[sibling_halves(f1)])
    p_f1 = pair_sums(f1, sib_f1, ["gate1", "up1", "down1"])
    gh_pw2, _ = tn_grad_square(sb, dh3_b, N_CHIPS, "tn_pw2")
    dub, d_bconv_w, d_b_pw1, buf_f1 = bconv_bwd(dcu, ub, b_conv_f, "bconv_bwd", hosted=[scatter_p1(p_f1)])
    gh_pw1, _ = tn_grad(n2, dub, N_CHIPS, False, "tn_pw1")
    b_grp = [gh_pw1, gh_pw2]
    dh2, d_b_norm, dh2_b, (*buf_f1, sib_pw1, sib_pw2) = nt_cols_rms(
        dub, g_pw1, h2, b_norm_f, dh3, "nt_pw1", hosted=[scatter_p2(buf_f1), sibling_halves(b_grp)], also_bf16=True)
    sib_b = [sib_pw1, sib_pw2]
    p_b = pair_sums(b_grp, sib_b, ["pw1", "pw2"])

    early_grads = [d_b_norm, d_b_pw1.reshape(2, D), d_bconv_w, d_b_conv, d_ln_g, d_ln_b, d_b_pw2, d_fn1, d_final, loss_part]
    epacked, es = _pack_rows(early_grads, D, "pack_small_grads_early")
    dg0, du0, dh1, dh1_b, d_fn0, (*buf_b, eall) = ffn_bwd(dh2, h1, ffn_norm[0:1], fg0, fu0, down0, gate0, up0, "ffn_bwd0",
                                                         hosted=[scatter_p1(p_b), gather_all(epacked)])
    gh_down0, _ = tn_grad(gu0, dh2_b, N_CHIPS, True, "tn_down0")
    gh_gate0, (*buf_b, sib_down0) = tn_grad(dg0, n1, N_CHIPS, True, "tn_gate0",
                                            hosted=[scatter_p2(buf_b), sibling_halves([gh_down0])])
    p_down0 = pair_sums([gh_down0], [sib_down0], ["down0"])
    gh_up0, (buf_down0, sib_gate0) = tn_grad(du0, n1, N_CHIPS, True, "tn_up0",
                                             hosted=[scatter_p1(p_down0), sibling_halves([gh_gate0])])
    p_gate0 = pair_sums([gh_gate0], [sib_gate0], ["gate0"])
    gh_out, (buf_down0, sib_up0) = tn_grad_square(ya, dh1_b, N_CHIPS, "tn_w_out",
                                                  hosted=[scatter_p2([buf_down0]), sibling_halves([gh_up0])])
    p_up0 = pair_sums([gh_up0], [sib_up0], ["up0"])
    dbcv, d_aconv_w, (buf_gate0, sib_out) = gateconv_bwd(dh1_b, g_out, bcv, a_conv_f, "gateconv_bwd",
                                                         hosted=[scatter_p1(p_gate0), sibling_halves([gh_out])])
    p_out = pair_sums([gh_out], [sib_out], ["out"])
    gh_in, (buf_up0, buf_out, buf_gate0) = tn_grad(n0, dbcv, N_CHIPS, False, "tn_w_in",
                                                   hosted=[scatter_p1(p_up0 + p_out), scatter_p2([buf_gate0])])
    sib_in = run_exchanges([sibling_halves([gh_in])], "reduce_in_siblings")
    p_in = pair_sums([gh_in], sib_in, ["in"])
    grad_x, d_a_norm, (buf_in, buf_up0, buf_out) = nt_cols_rms(
        dbcv, g_in, h0, a_norm, dh1, "nt_w_in", hosted=[scatter_p1(p_in), scatter_p2([buf_up0, buf_out])])
    p_f0 = [p_gate0[0], p_up0[0], p_down0[0]]

    lpacked, ls = _pack_rows([d_a_norm, d_aconv_w, d_fn0], D, "pack_small_grads_late")
    lall, (buf_in,) = small_allreduce(lpacked, "allreduce_small_grads", hosted=[scatter_p2([buf_in])])
    buf_a, p_a = [buf_in, buf_out], [p_in[0], p_out[0]]

    r_gate, r_up, r_down = upd([(w_gate, m_gate, v_gate), (w_up, m_up, v_up), (ffn_w_down, m_ffn_w_down, v_ffn_w_down)],
                               [[buf_gate0, buf_f1[0]], [buf_up0, buf_f1[1]], [buf_down0, buf_f1[2]]],
                               [[p_f0[0], p_f1[0]], [p_f0[1], p_f1[1]], [p_f0[2], p_f1[2]]], "ffn")
    r_gate, r_up = [tr(t) for t in r_gate], [tr(t) for t in r_up]
    (r_pw1,) = upd([(b_w_pw1, m_b_w_pw1, v_b_w_pw1)], [[buf_b[0]]], [[p_b[0]]], "pw1")
    r_pw2, r_out = upd([(b_w_pw2, m_b_w_pw2, v_b_w_pw2), (a_w_out, m_a_w_out, v_a_w_out)],
                       [[buf_b[1]], [buf_a[1]]], [[p_b[1]], [p_a[1]]], "pw2_out")
    (r_in,) = upd([(a_w_in, m_a_w_in, v_a_w_in)], [[buf_a[0]]], [[p_a[0]]], "w_in")
    entries = [
        ("late", ls[0], "full", a_norm, m_a_norm, v_a_norm),
        ("late", ls[1], "rows", rows3(a_conv), rows3(m_a_conv), rows3(v_a_conv)),
        ("early", es[0], "cols", b_norm, m_b_norm, v_b_norm),
        ("early", es[1], "flat2", b_b_pw1, m_b_b_pw1, v_b_b_pw1),
        ("early", es[2], "rows", rows3(b_conv), rows3(m_b_conv), rows3(v_b_conv)),
        ("early", es[3], "cols", b_b_conv, m_b_b_conv, v_b_b_conv),
        ("early", es[4], "cols", b_ln_g, m_b_ln_g, v_b_ln_g),
        ("early", es[5], "cols", b_ln_b, m_b_ln_b, v_b_ln_b),
        ("early", es[6], "cols", b_b_pw2, m_b_b_pw2, v_b_b_pw2),
        (None, [("late", ls[2]), ("early", es[7])], "layers", ffn_norm, m_ffn_norm, v_ffn_norm),
        ("early", es[8], "full", final_norm.reshape(1, D), m_final_norm.reshape(1, D), v_final_norm.reshape(1, D)),
    ]
    so = small_update(lall, eall, epacked, place, entries, es[9], "small_update")
    sm = [so[4 * e:4 * e + 4] for e in range(len(entries))]

    def shaped(e, like):
        return [t.reshape(like.shape) for t in sm[e]]

    r_a_norm, r_a_conv, r_b_norm, r_b_b_pw1 = shaped(0, a_norm), shaped(1, a_conv), shaped(2, b_norm), shaped(3, b_b_pw1)
    r_b_conv, r_b_b_conv, r_b_ln_g, r_b_ln_b = shaped(4, b_conv), shaped(5, b_b_conv), shaped(6, b_ln_g), shaped(7, b_ln_b)
    r_b_b_pw2, r_ffn_norm, r_final = shaped(8, b_b_pw2), sm[9], shaped(10, final_norm)

    loss = so[4 * len(entries)].reshape(())
    order =[r_a_norm, r_in, r_a_conv, r_out, r_b_norm, r_pw1, r_b_b_pw1, r_b_conv, r_b_b_conv, r_b_ln_g, r_b_ln_b,
             r_pw2, r_b_b_pw2, r_ffn_norm, r_gate, r_up, r_down, r_final]
    outs = [loss, grad_x.reshape(x.shape)]
    for field in range(4):
        outs += [r[field] for r in order]
    return tuple(outs)
```

```python
import functools

import jax
import jax.numpy as jnp
from jax import lax
from jax.experimental import pallas as pl
from jax.experimental.pallas import tpu as pltpu

RMS_EPS = 1e-6
LN_EPS = 1e-5
ADAM_LR = 0.001
ADAM_B1 = 0.9
ADAM_B2 = 0.999
ADAM_EPS = 1e-08
ADAM_WD = 0.01
ADAM_STEP = 10

N_CHIPS = 4
N_DEV = 8
LANES = 128
SUBLANES = 8
HALO = 32
CONV_ROWS = 64
TOKEN_TILE = 512
WIDE_TOKEN_TILE = 1024
GRAD_TOKEN_TILE = 2048
GRAD_SEGS_PER_STEP = 2
FFN_ROW_CHUNKS = 2
FFN_FWD_SEGS_PER_STEP = 4
FFN_BWD_TOKEN_TILE = 256
ROW_TILE = 256
CAST_STEPS = 4
VMEM_LIMIT = 56 * 1024 * 1024
MESH = pl.DeviceIdType.MESH
BF16 = jnp.bfloat16
F32 = jnp.float32


def _tile(n, pref, mult=SUBLANES):
    t = min(n, pref) // mult * mult
    while n % t:
        t -= mult
    return t


def _params(sem):
    return pltpu.CompilerParams(dimension_semantics=sem, vmem_limit_bytes=VMEM_LIMIT)


def _sigmoid(x):
    return 0.5 * jnp.tanh(0.5 * x) + 0.5


class _Exchange:
    def __init__(self, ins, outs, aliases, n_sems, copies, then=None):
        self.ins, self.outs, self.aliases, self.n_sems, self.copies = list(ins), list(outs), dict(aliases), n_sems, copies
        self.then = then
        self.early = False

    def awaited_first(self):
        self.early = True
        return self

    def start(self, xi, xo, ssem, rsem):
        for cp in self.copies(xi, xo, ssem, rsem)[0]:
            cp.start()

    def finish(self, xi, xo, ssem, rsem):
        sends, recvs = self.copies(xi, xo, ssem, rsem)
        for cp in recvs:
            cp.wait_recv()
        if self.then is not None:
            sends2, recvs2 = self.then(xi, xo, ssem, rsem)
            for cp in sends2:
                cp.start()
            for cp in recvs2:
                cp.wait_recv()
            sends = sends + sends2
        for cp in sends:
            cp.wait_send()


def _call(body, name, grid, in_specs, out_specs, out_shape, args, sem, scratch_shapes=(), hosted=(), prefetch=(),
          own_aliases=None):
    in_specs, out_specs, out_shape = list(in_specs), list(out_specs), list(out_shape)
    scratch_shapes, hosted, prefetch = list(scratch_shapes), list(hosted), list(prefetch)
    n_pre, n_in, n_out, n_scr = len(prefetch), len(args), len(out_shape), len(scratch_shapes)
    x_in = [a for ex in hosted for a in ex.ins]
    x_out = [o for ex in hosted for o in ex.outs]
    aliases = {n_pre + i: o for i, o in (own_aliases or {}).items()}
    at_in, at_out = n_pre + n_in, n_out
    for ex in hosted:
        for i, o in ex.aliases.items():
            aliases[at_in + i] = at_out + o
        at_in += len(ex.ins)
        at_out += len(ex.outs)
    sems = [pltpu.SemaphoreType.DMA((ex.n_sems,)) for ex in hosted for _ in range(2)]

    def wrapped(*refs):
        pre, refs = refs[:n_pre], refs[n_pre:]
        ins, xi = refs[:n_in], refs[n_in:n_in + len(x_in)]
        refs = refs[n_in + len(x_in):]
        outs, xo = refs[:n_out], refs[n_out:n_out + len(x_out)]
        refs = refs[n_out + len(x_out):]
        scr, sm = refs[:n_scr], refs[n_scr:]
        views, a, b = [], 0, 0
        for e, ex in enumerate(hosted):
            views.append((xi[a:a + len(ex.ins)], xo[b:b + len(ex.outs)], sm[2 * e], sm[2 * e + 1]))
            a += len(ex.ins)
            b += len(ex.outs)
        first = last = None
        for ax, g in enumerate(grid):
            f, l = pl.program_id(ax) == 0, pl.program_id(ax) == g - 1
            first, last = (f, l) if first is None else (first & f, last & l)

        def begin():
            for ex, v in zip(hosted, views):
                ex.start(*v)
            for ex, v in zip(hosted, views):
                if ex.early:
                    ex.finish(*v)

        def end():
            for ex, v in zip(hosted, views):
                if not ex.early:
                    ex.finish(*v)

        if hosted and grid:
            pl.when(first)(begin)
        elif hosted:
            begin()
        early_refs = [r for ex, v in zip(hosted, views) if ex.early for r in v[1]]
        body(*pre, *ins, *outs, *scr, *early_refs)
        if hosted and grid:
            pl.when(last)(end)
        elif hosted:
            end()

    hbm = pl.BlockSpec(memory_space=pl.ANY)
    all_in, all_out = in_specs + [hbm] * len(x_in), out_specs + [hbm] * len(x_out)
    kw = dict(name=name, out_shape=out_shape + x_out, input_output_aliases=aliases,
              compiler_params=_params(tuple("arbitrary" for _ in grid) if hosted else sem))
    if prefetch:
        kw["grid_spec"] = pltpu.PrefetchScalarGridSpec(num_scalar_prefetch=n_pre, grid=grid, in_specs=all_in,
                                                       out_specs=all_out, scratch_shapes=scratch_shapes + sems)
    else:
        kw.update(grid=grid, in_specs=all_in, out_specs=all_out, scratch_shapes=scratch_shapes + sems)
    res = pl.pallas_call(wrapped, **kw)(*prefetch, *args, *x_in)
    return list(res[:n_out]), list(res[n_out:])


def rms_cast_weights(h, gain, tensors, name, hosted=()):
    T, D = h.shape
    tm = T // CAST_STEPS
    assert tm * CAST_STEPS == T and tm % (2 * SUBLANES) == 0
    n = len(tensors)

    def body(h_ref, g_ref, *refs):
        x = h_ref[...]
        r = lax.rsqrt(jnp.mean(x * x, axis=-1, keepdims=True) + RMS_EPS)
        refs[n][...] = (x * r * g_ref[...]).astype(refs[n].dtype)
        for w_ref, o_ref in zip(refs[:n], refs[n + 1:]):
            o_ref[...] = w_ref[...].astype(o_ref.dtype)

    in_specs = [pl.BlockSpec((tm, D), lambda i: (i, 0)), pl.BlockSpec((1, D), lambda i: (0, 0))]
    out_specs, out_shape = [pl.BlockSpec((tm, D), lambda i: (i, 0))], [jax.ShapeDtypeStruct((T, D), BF16)]
    for w, lyr in tensors:
        R, C = w.shape[-2:]
        rb = R // CAST_STEPS
        assert rb * CAST_STEPS == R and rb % (2 * SUBLANES) == 0
        in_specs.append(pl.BlockSpec((None, rb, C), lambda i, lyr=lyr: (lyr, i, 0)))
        out_specs.append(pl.BlockSpec((rb, C), lambda i: (i, 0)))
        out_shape.append(jax.ShapeDtypeStruct((R, C), BF16))
    return _call(body, name, (CAST_STEPS,), in_specs, out_specs, out_shape, [h, gain] + [w for w, _ in tensors],
                 ("parallel",), hosted=hosted)


def loss_head(h, gain, tgt, name):
    T, D = h.shape
    tm = _tile(T, TOKEN_TILE)

    def body(h_ref, g_ref, t_ref, loss_ref, dh_ref, dhb_ref, dg_ref):
        i = pl.program_id(0)
        x = h_ref[...]
        g = g_ref[...]
        r = lax.rsqrt(jnp.mean(x * x, axis=-1, keepdims=True) + RMS_EPS)
        xhat = x * r
        diff = xhat * g - t_ref[...]
        part_loss = 0.5 * jnp.sum(jnp.mean(diff * diff, axis=-1, keepdims=True), axis=0, keepdims=True)
        dy = diff * (1.0 / D)
        dxhat = dy * g
        dh = r * (dxhat - xhat * jnp.mean(dxhat * xhat, axis=-1, keepdims=True))
        dh_ref[...] = dh
        dhb_ref[...] = dh.astype(dhb_ref.dtype)
        part = jnp.sum(dy * xhat, axis=0, keepdims=True)

        @pl.when(i == 0)
        def _():
            dg_ref[...] = part
            loss_ref[...] = part_loss

        @pl.when(i > 0)
        def _():
            dg_ref[...] += part
            loss_ref[...] += part_loss

    row = pl.BlockSpec((tm, D), lambda i: (i, 0))
    vec = pl.BlockSpec((1, D), lambda i: (0, 0))
    return pl.pallas_call(
        body, name=name, grid=(T // tm,),
        in_specs=[row, vec, row],
        out_specs=[pl.BlockSpec((1, 1), lambda i: (0, 0)), row, row, vec],
        out_shape=[jax.ShapeDtypeStruct((1, 1), F32), jax.ShapeDtypeStruct((T, D), F32),
                   jax.ShapeDtypeStruct((T, D), BF16), jax.ShapeDtypeStruct((1, D), F32)],
        compiler_params=_params(("arbitrary",)),
    )(h, gain, tgt)


def _prev_halo_spec(tm, width):
    return pl.BlockSpec((HALO, width), lambda i: (jnp.maximum(i * (tm // HALO) - 1, 0), 0))


def _next_halo_spec(tm, width, T):
    return pl.BlockSpec((HALO, width), lambda i: (jnp.minimum((i + 1) * (tm // HALO), T // HALO - 1), 0))


def _shifted(win, off, rows):
    if off % SUBLANES == 0:
        return win[off:off + rows]
    n = win.shape[0]
    return pltpu.roll(win, (n - off) % n, 0)[:rows]


def _rowsum8(x):
    acc = x[0:SUBLANES]
    for q in range(1, x.shape[0] // SUBLANES):
        acc = acc + x[q * SUBLANES:(q + 1) * SUBLANES]
    return acc


def _conv_loops(tm, D, per_block):
    def chunk(r, carry):
        t0 = pl.multiple_of(r * CONV_ROWS, CONV_ROWS)
        for lb in range(D // LANES):
            per_block(t0, slice(lb * LANES, (lb + 1) * LANES))
        return carry

    lax.fori_loop(0, tm // CONV_ROWS, chunk, 0)


def gateconv_fwd(bcv, w, w_out, res, name, hosted=()):
    T, D3 = bcv.shape
    D = D3 // 3
    K = w.shape[0]
    tm = _tile(T, TOKEN_TILE)
    wo_shape = w_out.outs[0].shape

    def body(x_ref, halo_ref, w_ref, res_ref, y_ref, h_ref, pad_ref, wo_v, sem, wo_hbm):
        i = pl.program_id(0)

        @pl.when(i == 0)
        def _():
            cp = pltpu.make_async_copy(wo_hbm, wo_v, sem)
            cp.start()
            cp.wait()

        pad_ref[HALO:, :] = x_ref[:, D:2 * D] * x_ref[:, 2 * D:]
        pad_ref[:HALO, :] = jnp.where(i > 0, halo_ref[:, D:2 * D] * halo_ref[:, 2 * D:], 0.0)

        def block(t0, ls):
            win = pad_ref[pl.ds(t0, CONV_ROWS + HALO), ls]
            acc = jnp.zeros((CONV_ROWS, LANES), F32)
            for k in range(K):
                acc = acc + w_ref[k:k + 1, ls] * _shifted(win, HALO - (K - 1) + k, CONV_ROWS)
            y_ref[pl.ds(t0, CONV_ROWS), ls] = (x_ref[pl.ds(t0, CONV_ROWS), ls] * acc).astype(y_ref.dtype)

        _conv_loops(tm, D, block)
        h_ref[...] = res_ref[...] + jnp.dot(y_ref[...], wo_v[...].reshape(D, D), preferred_element_type=F32)

    row = pl.BlockSpec((tm, D), lambda i: (i, 0))
    (y, h), xo = _call(
        body, name, (T // tm,),
        [pl.BlockSpec((tm, D3), lambda i: (i, 0)), _prev_halo_spec(tm, D3), pl.BlockSpec((K, D), lambda i: (0, 0)), row],
        [row, row], [jax.ShapeDtypeStruct((T, D), BF16), jax.ShapeDtypeStruct((T, D), F32)],
        [bcv, bcv, w, res], ("arbitrary",),
        [pltpu.VMEM((tm + HALO, D), F32), pltpu.VMEM(wo_shape, BF16), pltpu.SemaphoreType.DMA],
        hosted=[w_out.awaited_first()] + list(hosted))
    return y, h, xo


def gateconv_bwd(dh, w_out, bcv, w, name, hosted=()):
    T, D3 = bcv.shape
    D = D3 // 3
    K = w.shape[0]
    tm = _tile(T, TOKEN_TILE)
    nt = T // tm

    def body(dh_ref, dhn_ref, wo_ref, x_ref, xp_ref, xn_ref, w_ref, o_ref, dw_ref, cv_ref, dc_ref, wacc_ref, dy_ref):
        i = pl.program_id(0)
        dy_ref[...] = lax.dot_general(dh_ref[...], wo_ref[0], _NT, preferred_element_type=F32)
        dyn = lax.dot_general(dhn_ref[...], wo_ref[0], _NT, preferred_element_type=F32)
        cv_ref[HALO:, :] = x_ref[:, D:2 * D] * x_ref[:, 2 * D:]
        cv_ref[:HALO, :] = jnp.where(i > 0, xp_ref[:, D:2 * D] * xp_ref[:, 2 * D:], 0.0)
        dc_ref[:tm, :] = dy_ref[...] * x_ref[:, :D]
        dc_ref[tm:, :] = jnp.where(i < nt - 1, dyn * xn_ref[:, :D], 0.0)

        @pl.when(i == 0)
        def _():
            wacc_ref[...] = jnp.zeros_like(wacc_ref)

        def block(t0, ls):
            cwin = cv_ref[pl.ds(t0, CONV_ROWS + HALO), ls]
            dwin = dc_ref[pl.ds(t0, CONV_ROWS + HALO), ls]
            dcon = dwin[:CONV_ROWS]
            conv = jnp.zeros((CONV_ROWS, LANES), F32)
            dcv = jnp.zeros((CONV_ROWS, LANES), F32)
            for k in range(K):
                wk = w_ref[k:k + 1, ls]
                cs = _shifted(cwin, HALO - (K - 1) + k, CONV_ROWS)
                conv = conv + wk * cs
                dcv = dcv + wk * _shifted(dwin, (K - 1) - k, CONV_ROWS)
                wacc_ref[k * SUBLANES:(k + 1) * SUBLANES, ls] += _rowsum8(dcon * cs)
            rows = pl.ds(t0, CONV_ROWS)
            o_ref[rows, ls] = (dy_ref[rows, ls] * conv).astype(o_ref.dtype)
            o_ref[rows, pl.ds(D + ls.start, LANES)] = (dcv * x_ref[rows, pl.ds(2 * D + ls.start, LANES)]).astype(o_ref.dtype)
            o_ref[rows, pl.ds(2 * D + ls.start, LANES)] = (dcv * x_ref[rows, pl.ds(D + ls.start, LANES)]).astype(o_ref.dtype)

        _conv_loops(tm, D, block)

        @pl.when(i == nt - 1)
        def _():
            for k in range(K):
                dw_ref[k:k + 1, :] = jnp.sum(wacc_ref[k * SUBLANES:(k + 1) * SUBLANES, :], axis=0, keepdims=True)

    (dx, dw), xo = _call(
        body, name, (nt,),
        [pl.BlockSpec((tm, D), lambda i: (i, 0)), _next_halo_spec(tm, D, T), pl.BlockSpec((1, D, D), lambda i: (0, 0, 0)),
         pl.BlockSpec((tm, D3), lambda i: (i, 0)), _prev_halo_spec(tm, D3), _next_halo_spec(tm, D3, T),
         pl.BlockSpec((K, D), lambda i: (0, 0))],
        [pl.BlockSpec((tm, D3), lambda i: (i, 0)), pl.BlockSpec((K, D), lambda i: (0, 0))],
        [jax.ShapeDtypeStruct((T, D3), BF16), jax.ShapeDtypeStruct((K, D), F32)],
        [dh, dh, w_out, bcv, bcv, bcv, w], ("arbitrary",),
        [pltpu.VMEM((tm + HALO, D), F32), pltpu.VMEM((tm + HALO, D), F32), pltpu.VMEM((K * SUBLANES, D), F32),
         pltpu.VMEM((tm, D), F32)], hosted=hosted)
    return dx, dw, xo


def bconv_fwd(u, w, b_conv, ln_g, ln_b, w_out, b_out, res, name, hosted=()):
    T, D2 = u.shape
    D = D2 // 2
    K = w.shape[0]
    tm = _tile(T, TOKEN_TILE)

    def body(u_ref, halo_ref, w_ref, bc_ref, g_ref, b_ref, wo_ref, bo_ref, res_ref, cu_ref, s_ref, h_ref, pad_ref):
        i = pl.program_id(0)
        pad_ref[HALO:, :] = u_ref[:, :D] * _sigmoid(u_ref[:, D:])
        pad_ref[:HALO, :] = jnp.where(i > 0, halo_ref[:, :D] * _sigmoid(halo_ref[:, D:]), 0.0)

        def block(t0, ls):
            win = pad_ref[pl.ds(t0, CONV_ROWS + HALO), ls]
            acc = jnp.zeros((CONV_ROWS, LANES), F32)
            for k in range(K):
                acc = acc + w_ref[k:k + 1, ls] * _shifted(win, HALO - (K - 1) + k, CONV_ROWS)
            cu_ref[pl.ds(t0, CONV_ROWS), ls] = acc + bc_ref[:, ls]

        _conv_loops(tm, D, block)
        cu = cu_ref[...]
        mu = jnp.mean(cu, axis=-1, keepdims=True)
        xc = cu - mu
        rstd = lax.rsqrt(jnp.mean(xc * xc, axis=-1, keepdims=True) + LN_EPS)
        ln = xc * rstd * g_ref[...] + b_ref[...]
        s = (ln * _sigmoid(ln)).astype(s_ref.dtype)
        s_ref[...] = s
        h_ref[...] = res_ref[...] + bo_ref[...] + jnp.dot(s, wo_ref[0], preferred_element_type=F32)

    vec = pl.BlockSpec((1, D), lambda i: (0, 0))
    row = pl.BlockSpec((tm, D), lambda i: (i, 0))
    (cu, s, h), xo = _call(
        body, name, (T // tm,),
        [pl.BlockSpec((tm, D2), lambda i: (i, 0)), _prev_halo_spec(tm, D2), pl.BlockSpec((K, D), lambda i: (0, 0)), vec, vec, vec,
         pl.BlockSpec((1, D, D), lambda i: (0, 0, 0)), vec, row],
        [row, row, row], [jax.ShapeDtypeStruct((T, D), F32), jax.ShapeDtypeStruct((T, D), BF16), jax.ShapeDtypeStruct((T, D), F32)],
        [u, u, w, b_conv, ln_g, ln_b, w_out, b_out, res], ("parallel",), [pltpu.VMEM((tm + HALO, D), F32)], hosted=hosted)
    return cu, s, h, xo


def pw2_ln_bwd(dy, w, cu, ln_g, ln_b, name, hosted=()):
    T, D = cu.shape
    tm = _tile(T, TOKEN_TILE)

    def body(dy_ref, w_ref, cu_ref, g_ref, b_ref, dcu_ref, dg_ref, db_ref, dbc_ref, dbo_ref):
        i = pl.program_id(0)
        dy_ = dy_ref[...]
        ds = lax.dot_general(dy_.astype(BF16), w_ref[0], _NT, preferred_element_type=F32)
        cu_ = cu_ref[...]
        mu = jnp.mean(cu_, axis=-1, keepdims=True)
        xc = cu_ - mu
        rstd = lax.rsqrt(jnp.mean(xc * xc, axis=-1, keepdims=True) + LN_EPS)
        xh = xc * rstd
        ln = xh * g_ref[...] + b_ref[...]
        sg = _sigmoid(ln)
        dl = ds * (sg * (1.0 + ln * (1.0 - sg)))
        dxh = dl * g_ref[...]
        dcu = rstd * (dxh - jnp.mean(dxh, axis=-1, keepdims=True) - xh * jnp.mean(dxh * xh, axis=-1, keepdims=True))
        dcu_ref[...] = dcu
        pg = jnp.sum(dl * xh, axis=0, keepdims=True)
        pb = jnp.sum(dl, axis=0, keepdims=True)
        pc = jnp.sum(dcu, axis=0, keepdims=True)
        po = jnp.sum(dy_, axis=0, keepdims=True)

        @pl.when(i == 0)
        def _():
            dg_ref[...] = pg
            db_ref[...] = pb
            dbc_ref[...] = pc
            dbo_ref[...] = po

        @pl.when(i > 0)
        def _():
            dg_ref[...] += pg
            db_ref[...] += pb
            dbc_ref[...] += pc
            dbo_ref[...] += po

    vec = pl.BlockSpec((1, D), lambda i: (0, 0))
    row = pl.BlockSpec((tm, D), lambda i: (i, 0))
    vshape = jax.ShapeDtypeStruct((1, D), F32)
    outs, xo = _call(
        body, name, (T // tm,), [row, pl.BlockSpec((1, D, D), lambda i: (0, 0, 0)), row, vec, vec], [row, vec, vec, vec, vec],
        [jax.ShapeDtypeStruct((T, D), F32), vshape, vshape, vshape, vshape], [dy, w, cu, ln_g, ln_b], ("arbitrary",),
        hosted=hosted)
    return (*outs, xo)


def bconv_bwd(dcu, u, w, name, hosted=()):
    T, D2 = u.shape
    D = D2 // 2
    K = w.shape[0]
    tm = _tile(T, TOKEN_TILE)
    nt = T // tm

    def body(dc_ref, dcn_ref, u_ref, up_ref, w_ref, du_ref, dw_ref, db_ref, glu_ref, dpad_ref, dglu_ref, wacc_ref):
        i = pl.program_id(0)
        glu_ref[HALO:, :] = u_ref[:, :D] * _sigmoid(u_ref[:, D:])
        glu_ref[:HALO, :] = jnp.where(i > 0, up_ref[:, :D] * _sigmoid(up_ref[:, D:]), 0.0)
        dpad_ref[:tm, :] = dc_ref[...]
        dpad_ref[tm:, :] = jnp.where(i < nt - 1, dcn_ref[...], 0.0)

        @pl.when(i == 0)
        def _():
            wacc_ref[...] = jnp.zeros_like(wacc_ref)

        def block(t0, ls):
            gwin = glu_ref[pl.ds(t0, CONV_ROWS + HALO), ls]
            dwin = dpad_ref[pl.ds(t0, CONV_ROWS + HALO), ls]
            dcur = dwin[:CONV_ROWS]
            dglu = jnp.zeros((CONV_ROWS, LANES), F32)
            for k in range(K):
                dglu = dglu + w_ref[k:k + 1, ls] * _shifted(dwin, (K - 1) - k, CONV_ROWS)
                gs = _shifted(gwin, HALO - (K - 1) + k, CONV_ROWS)
                wacc_ref[k * SUBLANES:(k + 1) * SUBLANES, ls] += _rowsum8(dcur * gs)
            dglu_ref[pl.ds(t0, CONV_ROWS), ls] = dglu

        _conv_loops(tm, D, block)
        dglu = dglu_ref[...]
        a = u_ref[:, :D]
        sg = _sigmoid(u_ref[:, D:])
        da = dglu * sg
        dg = dglu * a * (sg * (1.0 - sg))
        du_ref[:, :D] = da.astype(du_ref.dtype)
        du_ref[:, D:] = dg.astype(du_ref.dtype)
        pa = jnp.sum(da, axis=0, keepdims=True)
        pg = jnp.sum(dg, axis=0, keepdims=True)

        @pl.when(i == 0)
        def _():
            db_ref[:, :D] = pa
            db_ref[:, D:] = pg

        @pl.when(i > 0)
        def _():
            db_ref[:, :D] += pa
            db_ref[:, D:] += pg

        @pl.when(i == nt - 1)
        def _():
            for k in range(K):
                dw_ref[k:k + 1, :] = jnp.sum(wacc_ref[k * SUBLANES:(k + 1) * SUBLANES, :], axis=0, keepdims=True)

    (du, dw, db), xo = _call(
        body, name, (nt,),
        [pl.BlockSpec((tm, D), lambda i: (i, 0)), _next_halo_spec(tm, D, T),
         pl.BlockSpec((tm, D2), lambda i: (i, 0)), _prev_halo_spec(tm, D2), pl.BlockSpec((K, D), lambda i: (0, 0))],
        [pl.BlockSpec((tm, D2), lambda i: (i, 0)), pl.BlockSpec((K, D), lambda i: (0, 0)), pl.BlockSpec((1, D2), lambda i: (0, 0))],
        [jax.ShapeDtypeStruct((T, D2), BF16), jax.ShapeDtypeStruct((K, D), F32), jax.ShapeDtypeStruct((1, D2), F32)],
        [dcu, dcu, u, u, w], ("arbitrary",),
        [pltpu.VMEM((tm + HALO, D), F32), pltpu.VMEM((tm + HALO, D), F32), pltpu.VMEM((tm, D), F32),
         pltpu.VMEM((K * SUBLANES, D), F32)], hosted=hosted)
    return du, dw, db, xo


def mm_cols(a, w, name, hosted=()):
    T, K = a.shape
    S, _, n = w.shape
    tm = _tile(T, WIDE_TOKEN_TILE)

    def body(a_ref, w_ref, o_ref):
        o_ref[...] = jnp.dot(a_ref[...], w_ref[...], preferred_element_type=F32)

    in_specs = [pl.BlockSpec((tm, K), lambda s, i: (i, 0)), pl.BlockSpec((None, K, n), lambda s, i: (s, 0, 0))]
    (out,), xo = _call(body, name, (S, T // tm), in_specs, [pl.BlockSpec((tm, n), lambda s, i: (i, s))],
                       [jax.ShapeDtypeStruct((T, S * n), F32)], [a, w], ("parallel", "parallel"), hosted=hosted)
    return out, xo


def rms_mm_cols(h, gain, w, bias, name, hosted=()):
    T, K = h.shape
    S, _, n = w.shape
    tm = _tile(T, TOKEN_TILE)

    def body(h_ref, gain_ref, w_ref, b_ref, n_ref, o_ref):
        x = h_ref[...]
        r = lax.rsqrt(jnp.mean(x * x, axis=-1, keepdims=True) + RMS_EPS)
        a = (x * r * gain_ref[...]).astype(n_ref.dtype)
        n_ref[...] = a
        for s in range(S):
            cols = slice(s * n, (s + 1) * n)
            o_ref[:, cols] = jnp.dot(a, w_ref[s], preferred_element_type=F32) + b_ref[:, cols]

    row = pl.BlockSpec((tm, K), lambda i: (i, 0))
    (n_out, out), xo = _call(
        body, name, (T // tm,),
        [row, pl.BlockSpec((1, K), lambda i: (0, 0)), pl.BlockSpec((S, K, n), lambda i: (0, 0, 0)),
         pl.BlockSpec((1, S * n), lambda i: (0, 0))],
        [row, pl.BlockSpec((tm, S * n), lambda i: (i, 0))],
        [jax.ShapeDtypeStruct((T, K), BF16), jax.ShapeDtypeStruct((T, S * n), F32)],
        [h, gain, w, bias], ("parallel",), hosted=hosted)
    return n_out, out, xo


def _load_weights(pairs, sems, S, G, i, p):
    def copies(seg):
        return [pltpu.make_async_copy(src.at[seg], dst.at[seg], sems.at[k, seg]) for k, (src, dst) in enumerate(pairs)]

    @pl.when((i == 0) & (p == 0))
    def _():
        for seg in range(S):
            for cp in copies(seg):
                cp.start()

    @pl.when((i == 0) & (p < S // G))
    def _():
        for j in range(G):
            for cp in copies(G * p + j):
                cp.wait()


def ffn_fwd(h, gain, weights, name, hosted=(), arriving=None):
    T, D = h.shape
    S, f, _ = weights[0].shape
    tm = _tile(T, TOKEN_TILE)
    rc = tm // FFN_ROW_CHUNKS
    chunks = [slice(r * rc, (r + 1) * rc) for r in range(FFN_ROW_CHUNKS)]
    G = FFN_FWD_SEGS_PER_STEP
    weights = list(weights)
    hosted = ([arriving.awaited_first()] if arriving is not None else []) + list(hosted)

    def body(h_ref, gain_ref, *refs):
        nw = len(weights)
        wg_hbm, wu_hbm, wd_hbm = list(refs[:nw]) + list(refs[nw + 9:])
        n_ref, g_ref, u_ref, gu_ref, o_ref, wg_v, wu_v, wd_v, sems = refs[nw:nw + 9]
        i, p = pl.program_id(0), pl.program_id(1)
        _load_weights([(wg_hbm, wg_v), (wu_hbm, wu_v), (wd_hbm, wd_v)], sems, S, G, i, p)

        @pl.when(p == 0)
        def _():
            x = h_ref[...]
            r = lax.rsqrt(jnp.mean(x * x, axis=-1, keepdims=True) + RMS_EPS)
            n_ref[...] = (x * r * gain_ref[...]).astype(n_ref.dtype)

        parts = []
        for rows in chunks:
            a = n_ref[rows, :]
            acc = None
            for j in range(G):
                seg = G * p + j
                g = lax.dot_general(a, wg_v[seg], _NT, preferred_element_type=F32)
                u = lax.dot_general(a, wu_v[seg], _NT, preferred_element_type=F32)
                gu = (g * _sigmoid(g) * u).astype(gu_ref.dtype)
                g_ref[j, rows, :] = g.astype(g_ref.dtype)
                u_ref[j, rows, :] = u.astype(u_ref.dtype)
                gu_ref[j, rows, :] = gu
                part = jnp.dot(gu, wd_v[seg], preferred_element_type=F32)
                acc = part if acc is None else acc + part
            parts.append(acc)

        @pl.when(p == 0)
        def _():
            for rows, part in zip(chunks, parts):
                o_ref[rows, :] = h_ref[rows, :] + part

        @pl.when(p > 0)
        def _():
            for rows, part in zip(chunks, parts):
                o_ref[rows, :] += part

    row = pl.BlockSpec((tm, D), lambda i, p: (i, 0))
    seg = pl.BlockSpec((G, tm, f), lambda i, p: (p, i, 0))
    hbm = pl.BlockSpec(memory_space=pl.ANY)
    segs = jax.ShapeDtypeStruct((S, T, f), BF16)
    outs, xo = _call(
        body, name, (T // tm, S // G),
        [row, pl.BlockSpec((1, D), lambda i, s: (0, 0))] + [hbm] * len(weights), [row, seg, seg, seg, row],
        [jax.ShapeDtypeStruct((T, D), BF16), segs, segs, segs, jax.ShapeDtypeStruct((T, D), F32)],
        [h, gain] + weights, ("arbitrary", "arbitrary"),
        [pltpu.VMEM((S, f, D), BF16), pltpu.VMEM((S, f, D), BF16), pltpu.VMEM((S, f, D), BF16), pltpu.SemaphoreType.DMA((3, S))],
        hosted=hosted)
    return (*outs, xo)


def ffn_bwd(dy, h, gain, g, u, wd, wg, wu, name, hosted=()):
    T, D = h.shape
    S, f, _ = wg.shape
    tm = _tile(T, FFN_BWD_TOKEN_TILE)
    nt = T // tm

    def body(dy_ref, h_ref, gain_ref, g_ref, u_ref, wd_hbm, wg_hbm, wu_hbm, dg_ref, du_ref, dh_ref, dhb_ref, dgain_ref,
             wd_v, wg_v, wu_v, sems):
        i = pl.program_id(0)
        _load_weights([(wd_hbm, wd_v), (wg_hbm, wg_v), (wu_hbm, wu_v)], sems, S, S, i, 0)
        dy_ = dy_ref[...]
        dyb = dy_.astype(BF16)
        dn = None
        for j in range(S):
            dgu = lax.dot_general(dyb, wd_v[j], _NT, preferred_element_type=F32)
            gv = g_ref[j].astype(F32)
            sg = _sigmoid(gv)
            dg = (dgu * u_ref[j].astype(F32) * (sg * (1.0 + gv * (1.0 - sg)))).astype(dg_ref.dtype)
            du = (dgu * (gv * sg)).astype(du_ref.dtype)
            dg_ref[j] = dg
            du_ref[j] = du
            part = jnp.dot(dg, wg_v[j], preferred_element_type=F32) + jnp.dot(du, wu_v[j], preferred_element_type=F32)
            dn = part if dn is None else dn + part
        x = h_ref[...]
        r = lax.rsqrt(jnp.mean(x * x, axis=-1, keepdims=True) + RMS_EPS)
        xhat = x * r
        dxhat = dn * gain_ref[...]
        dh = dy_ + r * (dxhat - xhat * jnp.mean(dxhat * xhat, axis=-1, keepdims=True))
        dh_ref[...] = dh
        dhb_ref[...] = dh.astype(dhb_ref.dtype)
        pg = jnp.sum(dn * xhat, axis=0, keepdims=True)

        @pl.when(i == 0)
        def _():
            dgain_ref[...] = pg

        @pl.when(i > 0)
        def _():
            dgain_ref[...] += pg

    row = pl.BlockSpec((tm, D), lambda i: (i, 0))
    vec = pl.BlockSpec((1, D), lambda i: (0, 0))
    seg = pl.BlockSpec((S, tm, f), lambda i: (0, i, 0))
    hbm = pl.BlockSpec(memory_space=pl.ANY)
    segs = jax.ShapeDtypeStruct((S, T, f), BF16)
    outs, xo = _call(
        body, name, (nt,),
        [row, row, vec, seg, seg, hbm, hbm, hbm], [seg, seg, row, row, vec],
        [segs, segs, jax.ShapeDtypeStruct((T, D), F32), jax.ShapeDtypeStruct((T, D), BF16), jax.ShapeDtypeStruct((1, D), F32)],
        [dy, h, gain, g, u, wd, wg, wu], ("arbitrary",),
        [pltpu.VMEM((S, f, D), BF16), pltpu.VMEM((S, f, D), BF16), pltpu.VMEM((S, f, D), BF16),
         pltpu.SemaphoreType.DMA((3, S))], hosted=hosted)
    return (*outs, xo)


_NT = (((1,), (1,)), ((), ()))
_TN = (((0,), (0,)), ((), ()))


def nt_cols_rms(dy, w, h, gain, dres, name, hosted=(), also_bf16=False, tokens=None, base=None):
    T, K = h.shape
    S, _, n = w.shape
    tm = _tile(T, TOKEN_TILE)
    t0, t1 = tokens or (0, T)
    assert t0 % tm == 0 and (t1 - t0) % tm == 0 and not (also_bf16 and base is not None)
    i0 = t0 // tm

    def body(dy_ref, w_ref, h_ref, gain_ref, dres_ref, *rest):
        base_gain_ref = None
        if base is not None:
            _, base_gain_ref, *rest = rest
        dh_ref, dgain_ref, *rest = rest
        i = pl.program_id(0)
        dn = None
        for s in range(S):
            part = lax.dot_general(dy_ref[:, s * n:(s + 1) * n], w_ref[s], _NT, preferred_element_type=F32)
            dn = part if dn is None else dn + part
        x = h_ref[...]
        r = lax.rsqrt(jnp.mean(x * x, axis=-1, keepdims=True) + RMS_EPS)
        xhat = x * r
        dxhat = dn * gain_ref[...]
        dh = dres_ref[...] + r * (dxhat - xhat * jnp.mean(dxhat * xhat, axis=-1, keepdims=True))
        dh_ref[...] = dh
        if also_bf16:
            rest[0][...] = dh.astype(BF16)
        pg = jnp.sum(dn * xhat, axis=0, keepdims=True)

        @pl.when(i == 0)
        def _():
            dgain_ref[...] = pg if base is None else base_gain_ref[...] + pg

        @pl.when(i > 0)
        def _():
            dgain_ref[...] += pg

    row = pl.BlockSpec((tm, K), lambda i: (i + i0, 0))
    vec = pl.BlockSpec((1, K), lambda i: (0, 0))
    out_specs, out_shape = [row, vec], [jax.ShapeDtypeStruct((T, K), F32), jax.ShapeDtypeStruct((1, K), F32)]
    if also_bf16:
        out_specs, out_shape = out_specs + [row], out_shape + [jax.ShapeDtypeStruct((T, K), BF16)]
    in_specs = [pl.BlockSpec((tm, S * n), lambda i: (i + i0, 0)), pl.BlockSpec((S, K, n), lambda i: (0, 0, 0)), row, vec, row]
    args, aliases = [dy, w, h, gain, dres], None
    if base is not None:
        in_specs += [pl.BlockSpec(memory_space=pl.ANY), vec]
        aliases = {len(args): 0}
        args += list(base)
    outs, xo = _call(body, name, ((t1 - t0) // tm,), in_specs, out_specs, out_shape, args, ("arbitrary",),
                     hosted=hosted, own_aliases=aliases)
    return (*outs, xo)


def tn_grad(a, dy, S, a_by_seg, name, hosted=()):
    T = dy.shape[0]
    tt = _tile(T, GRAD_TOKEN_TILE)
    G = GRAD_SEGS_PER_STEP
    if a_by_seg:
        R, C = a.shape[2], dy.shape[1]
        a_spec = pl.BlockSpec((G, tt, R), lambda p, t: (p, t, 0))
        b_spec = pl.BlockSpec((tt, C), lambda p, t: (t, 0))
    else:
        R, C = a.shape[1], dy.shape[1] // S
        a_spec = pl.BlockSpec((tt, R), lambda p, t: (t, 0))
        b_spec = pl.BlockSpec((tt, G * C), lambda p, t: (t, p))
    Rh = R // 2
    nt = T // tt

    def body(a_ref, b_ref, o_ref, acc_ref):
        t = pl.program_id(1)
        parts = []
        for j in range(G):
            a_j = a_ref[j] if a_by_seg else a_ref[...]
            b_j = b_ref[...] if a_by_seg else b_ref[:, j * C:(j + 1) * C]
            parts.append(lax.dot_general(a_j, b_j.astype(BF16), _TN, preferred_element_type=F32))

        @pl.when(t == 0)
        def _():
            for j in range(G):
                acc_ref[j] = parts[j]

        @pl.when(t > 0)
        def _():
            for j in range(G):
                acc_ref[j] += parts[j]

        @pl.when(t == nt - 1)
        def _():
            for j in range(G):
                o_ref[0, j] = acc_ref[j, :Rh, :].astype(o_ref.dtype)
                o_ref[1, j] = acc_ref[j, Rh:, :].astype(o_ref.dtype)

    (gh,), xo = _call(
        body, name, (S // G, nt), [a_spec, b_spec], [pl.BlockSpec((2, G, Rh, C), lambda p, t: (0, p, 0, 0))],
        [jax.ShapeDtypeStruct((2, S, Rh, C), BF16)], [a, dy], ("parallel", "arbitrary"), [pltpu.VMEM((G, R, C), F32)],
        hosted=hosted)
    return gh, xo


def tn_grad_square(a, dy, S, name, hosted=()):
    T, K = a.shape
    N = dy.shape[1]
    tt = _tile(T, GRAD_TOKEN_TILE)
    nt = T // tt
    Rh = K // S // 2

    def body(a_ref, b_ref, o_ref, acc_ref):
        t = pl.program_id(0)
        part = lax.dot_general(a_ref[...], b_ref[...].astype(BF16), _TN, preferred_element_type=F32)

        @pl.when(t == 0)
        def _():
            acc_ref[...] = part

        @pl.when(t > 0)
        def _():
            acc_ref[...] += part

        @pl.when(t == nt - 1)
        def _():
            for s in range(S):
                for hf in range(2):
                    r0 = (2 * s + hf) * Rh
                    o_ref[hf, s] = acc_ref[r0:r0 + Rh, :].astype(o_ref.dtype)

    (gh,), xo = _call(
        body, name, (nt,), [pl.BlockSpec((tt, K), lambda t: (t, 0)), pl.BlockSpec((tt, N), lambda t: (t, 0))],
        [pl.BlockSpec((2, S, Rh, N), lambda t: (0, 0, 0, 0))], [jax.ShapeDtypeStruct((2, S, Rh, N), BF16)],
        [a, dy], ("arbitrary",), [pltpu.VMEM((K, N), F32)], hosted=hosted)
    return gh, xo


def _place():
    x, y, c = lax.axis_index("x"), lax.axis_index("y"), lax.axis_index("c")
    chips = [(1 - x, y), (x, 1 - y), (1 - x, 1 - y)]
    return x, y, c, chips


def _remote(src, dst, send_sem, recv_sem, dev):
    return pltpu.make_async_remote_copy(src_ref=src, dst_ref=dst, send_sem=send_sem, recv_sem=recv_sem,
                                        device_id=dev, device_id_type=MESH)


def small_allreduce(v, name, hosted=()):
    rows, W = v.shape

    def body(v_ref, o_ref, sib_ref, pair_ref, chips_ref, send_sems, recv_sems):
        x, y, c, chips = _place()
        me = 2 * x + y
        swap = _remote(v_ref, sib_ref, send_sems.at[3], recv_sems.at[3], (x, y, 1 - c))
        swap.start()
        swap.wait()
        mine, other = v_ref[...], sib_ref[...]
        pair_ref[...] = jnp.where(c == 0, mine, other) + jnp.where(c == 0, other, mine)
        sends = []
        for j, (px, py) in enumerate(chips):
            cp = _remote(pair_ref, chips_ref.at[me], send_sems.at[j], recv_sems.at[j], (px, py, c))
            cp.start()
            sends.append(cp)
        chips_ref[me] = pair_ref[...]
        for j, (px, py) in enumerate(chips):
            blk = chips_ref.at[2 * px + py]
            _remote(blk, blk, send_sems.at[j], recv_sems.at[j], (px, py, c)).wait_recv()
        for cp in sends:
            cp.wait_send()
        o_ref[...] = (chips_ref[0] + chips_ref[1]) + (chips_ref[2] + chips_ref[3])

    vm = pl.BlockSpec(memory_space=pltpu.VMEM)
    (out,), xo = _call(
        body, name, (), [vm], [vm], [jax.ShapeDtypeStruct((rows, W), F32)], [v], (),
        [pltpu.VMEM((rows, W), F32), pltpu.VMEM((rows, W), F32), pltpu.VMEM((N_CHIPS, rows, W), F32),
         pltpu.SemaphoreType.DMA((4,)), pltpu.SemaphoreType.DMA((4,))], hosted=hosted)
    return out, xo


def _gather_p1_copies(srcs, bufs, ssem, rsem, base):
    x, y, c, chips = _place()
    me, sib = 2 * x + y, (x, y, 1 - c)
    sends, recvs = [], []
    for k, (src, buf) in enumerate(zip(srcs, bufs)):
        rh = src.shape[0] // 2
        s0 = base + 4 * k
        sends.append(_remote(src, buf.at[me], ssem.at[s0 + 3], rsem.at[s0 + 3], sib))
        recvs.append(_remote(buf.at[me], buf.at[me], ssem.at[s0 + 3], rsem.at[s0 + 3], sib))
        for j, (px, py) in enumerate(chips):
            sends.append(_remote(src.at[pl.ds(c * rh, rh)], buf.at[me, pl.ds(c * rh, rh)], ssem.at[s0 + j], rsem.at[s0 + j], (px, py, c)))
            blk = buf.at[2 * px + py, pl.ds(c * rh, rh)]
            recvs.append(_remote(blk, blk, ssem.at[s0 + j], rsem.at[s0 + j], (px, py, c)))
    return sends, recvs


def _gather_p2_copies(bufs, ssem, rsem, base):
    x, y, c, chips = _place()
    sib = (x, y, 1 - c)
    sends, recvs = [], []
    for k, buf in enumerate(bufs):
        rh = buf.shape[1] // 2
        for j, (px, py) in enumerate(chips):
            s0 = base + 3 * k + j
            blk = buf.at[2 * px + py, pl.ds(c * rh, rh)]
            sends.append(_remote(blk, blk, ssem.at[s0], rsem.at[s0], sib))
            got = buf.at[2 * px + py, pl.ds((1 - c) * rh, rh)]
            recvs.append(_remote(got, got, ssem.at[s0], rsem.at[s0], sib))
    return sends, recvs


def _gathered_shape(s):
    return jax.ShapeDtypeStruct((N_CHIPS,) + s.shape, s.dtype)


def gather_p1(shards):
    return _Exchange(shards, [_gathered_shape(s) for s in shards], {}, 4 * len(shards),
                     lambda xi, xo, ss, rs: _gather_p1_copies(xi, xo, ss, rs, 0))


def gather_p2(bufs):
    return _Exchange(bufs, [jax.ShapeDtypeStruct(b.shape, b.dtype) for b in bufs], {k: k for k in range(len(bufs))},
                     3 * len(bufs), lambda xi, xo, ss, rs: _gather_p2_copies(xo, ss, rs, 0))


def gather_whole(whole, begun):
    nw, n = len(whole), len(whole) + len(begun)
    shards = list(whole) + list(begun)
    return _Exchange(shards, [_gathered_shape(s) for s in shards], {}, 4 * n + 3 * nw,
                     lambda xi, xo, ss, rs: _gather_p1_copies(xi, xo, ss, rs, 0),
                     then=lambda xi, xo, ss, rs: _gather_p2_copies(xo[:nw], ss, rs, 4 * n))


def gather_small(v):
    def copies(xi, xo, ssem, rsem):
        x, y, c, chips = _place()
        me, sib = 2 * x + y, (x, y, 1 - c)
        sends = [_remote(xi[0], xo[0].at[me], ssem.at[3], rsem.at[3], sib)]
        recvs = [_remote(xo[0].at[me], xo[0].at[me], ssem.at[3], rsem.at[3], sib)]
        for j, (px, py) in enumerate(chips):
            sends.append(_remote(xi[0], xo[0].at[me], ssem.at[j], rsem.at[j], (px, py, c)))
            blk = xo[0].at[2 * px + py]
            recvs.append(_remote(blk, blk, ssem.at[j], rsem.at[j], (px, py, c)))
        return sends, recvs

    return _Exchange([v], [_gathered_shape(v)], {}, 4, copies)


def gather_all(v):
    def copies(xi, xo, ssem, rsem):
        x, y, c, _ = _place()
        sends, recvs = [], []
        for m in range(1, N_DEV):
            px, py, pc = (1 - x) if m & 4 else x, (1 - y) if m & 2 else y, (1 - c) if m & 1 else c
            sends.append(_remote(xi[0], xo[0].at[4 * x + 2 * y + c], ssem.at[m - 1], rsem.at[m - 1], (px, py, pc)))
            blk = xo[0].at[4 * px + 2 * py + pc]
            recvs.append(_remote(blk, blk, ssem.at[m - 1], rsem.at[m - 1], (px, py, pc)))
        return sends, recvs

    return _Exchange([v], [jax.ShapeDtypeStruct((N_DEV,) + v.shape, v.dtype)], {}, N_DEV - 1, copies)


def run_exchanges(exchanges, name):
    return _call(lambda: None, name, (), [], [], [], [], (), hosted=exchanges)[1]


def sibling_halves(grads):
    def copies(xi, xo, ssem, rsem):
        x, y, c, _ = _place()
        sends = [_remote(xi[k].at[1 - c], xo[k], ssem.at[k], rsem.at[k], (x, y, 1 - c)) for k in range(len(grads))]
        return sends, sends

    return _Exchange(grads, [jax.ShapeDtypeStruct(g.shape[1:], g.dtype) for g in grads], {}, len(grads), copies)


def pair_sum(ghs, recvs, cidx, name):
    n = len(ghs)
    S = ghs[0].shape[1]

    def body(c_ref, *refs):
        for k in range(n):
            a_ref, b_ref, o_ref = refs[2 * k], refs[2 * k + 1], refs[2 * n + k]
            o_ref[...] = (a_ref[...].astype(F32) + b_ref[...].astype(F32)).astype(o_ref.dtype)

    in_specs, out_specs, out_shape, args = [], [], [], []
    for gh, recv in zip(ghs, recvs):
        _, _, Rh, C = gh.shape
        in_specs += [pl.BlockSpec((None, None, Rh, C), lambda s, c_ref: (c_ref[0], s, 0, 0)),
                     pl.BlockSpec((None, Rh, C), lambda s, c_ref: (s, 0, 0))]
        out_specs.append(pl.BlockSpec((None, Rh, C), lambda s, c_ref: (s, 0, 0)))
        out_shape.append(jax.ShapeDtypeStruct((S, Rh, C), BF16))
        args += [gh, recv]
    return pl.pallas_call(
        body, name=name, out_shape=out_shape,
        grid_spec=pltpu.PrefetchScalarGridSpec(num_scalar_prefetch=1, grid=(S,), in_specs=in_specs, out_specs=out_specs),
        compiler_params=_params(("parallel",)),
    )(cidx, *args)


def scatter_p1(parts):
    def copies(xi, xo, ssem, rsem):
        x, y, c, chips = _place()
        me, sib = 2 * x + y, (x, y, 1 - c)
        sends, recvs = [], []
        for k in range(len(parts)):
            s0 = 4 * k
            sends.append(_remote(xi[k].at[me], xo[k].at[me, c], ssem.at[s0 + 3], rsem.at[s0 + 3], sib))
            own = xo[k].at[me, 1 - c]
            recvs.append(_remote(own, own, ssem.at[s0 + 3], rsem.at[s0 + 3], sib))
            for j, (px, py) in enumerate(chips):
                sends.append(_remote(xi[k].at[2 * px + py], xo[k].at[me, c], ssem.at[s0 + j], rsem.at[s0 + j], (px, py, c)))
                blk = xo[k].at[2 * px + py, c]
                recvs.append(_remote(blk, blk, ssem.at[s0 + j], rsem.at[s0 + j], (px, py, c)))
        return sends, recvs

    return _Exchange(parts, [jax.ShapeDtypeStruct((p.shape[0], 2) + p.shape[1:], p.dtype) for p in parts], {},
                     4 * len(parts), copies)


def scatter_p2(bufs):
    def copies(xi, xo, ssem, rsem):
        x, y, c, chips = _place()
        sib = (x, y, 1 - c)
        sends, recvs = [], []
        for k in range(len(bufs)):
            for j, (px, py) in enumerate(chips):
                s0 = 3 * k + j
                blk = xo[k].at[2 * px + py, c]
                sends.append(_remote(blk, blk, ssem.at[s0], rsem.at[s0], sib))
                got = xo[k].at[2 * px + py, 1 - c]
                recvs.append(_remote(got, got, ssem.at[s0], rsem.at[s0], sib))
        return sends, recvs

    return _Exchange(bufs, [jax.ShapeDtypeStruct(b.shape, b.dtype) for b in bufs], {k: k for k in range(len(bufs))},
                     3 * len(bufs), copies)


def _adamw_math(w, g, m, v):
    m = ADAM_B1 * m + (1.0 - ADAM_B1) * g
    v = ADAM_B2 * v + (1.0 - ADAM_B2) * (g * g)
    m_hat = m / (1.0 - ADAM_B1 ** ADAM_STEP)
    v_hat = v / (1.0 - ADAM_B2 ** ADAM_STEP)
    delta = -ADAM_LR * (m_hat / (jnp.sqrt(v_hat) + ADAM_EPS) + ADAM_WD * w)
    return delta, m, v


def adamw_reduce(tensors, place, lyr, bases, name):
    n = len(tensors)
    L, R, C = tensors[0][0].shape
    Rh = R // 2
    rb = _tile(Rh, ROW_TILE, 2 * SUBLANES)
    nb = Rh // rb

    def body(place_ref, *refs):
        mine = (place_ref[1] == pl.program_id(0))
        for k in range(n):
            p_ref, b0, b1, b2, b3, w_ref, m_ref, v_ref = refs[8 * k:8 * k + 8]
            go_ref, d_ref, mo_ref, vo_ref = refs[len(refs) - 4 * n + 4 * k:len(refs) - 4 * n + 4 * k + 4]
            g = None
            for p, b in enumerate((b0, b1, b2, b3)):
                val = jnp.where(mine & (place_ref[0] == p), p_ref[...], b[...]).astype(F32)
                g = val if g is None else g + val
            d, mn, vn = _adamw_math(w_ref[...], g, m_ref[...], v_ref[...])
            go_ref[...] = g
            d_ref[...] = d
            mo_ref[...] = mn
            vo_ref[...] = vn

    def buf_spec(p):
        def idx(h, i, pr):
            own = (pr[0] == p) & (pr[1] == h)
            return (p, jnp.where(own, 1 - h, h), i, 0)
        return pl.BlockSpec((None, None, rb, C), idx)

    blk = pl.BlockSpec((None, rb, C), lambda h, i, pr: (lyr, h * nb + i, 0))
    in_specs, args = [], []
    for w, m, v, buf, part in tensors:
        in_specs += [pl.BlockSpec((None, rb, C), lambda h, i, pr: (pr[0], i, 0))] + [buf_spec(p) for p in range(N_CHIPS)] + [blk] * 3
        args += [part, buf, buf, buf, buf, w, m, v]
    aliases = {}
    if bases is not None:
        in_specs += [pl.BlockSpec(memory_space=pl.ANY)] * (4 * n)
        aliases = {len(args) + k: k for k in range(4 * n)}
        args += list(bases)
    shp = jax.ShapeDtypeStruct((L, R, C), F32)
    flat = _call(body, name, (2, nb), in_specs, [blk] * (4 * n), [shp] * (4 * n), args, ("parallel", "parallel"),
                 prefetch=[place], own_aliases=aliases)[0]
    return flat


def small_update(late, early, own, place, entries, loss_row, name):
    ne = len(entries)
    D = late.shape[1]

    def body(place_ref, late_ref, early_ref, own_ref, *refs):
        ins, outs = refs[:3 * ne], refs[3 * ne:]
        ch = place_ref[0]
        me = 2 * place_ref[0] + place_ref[1]

        def early_sum(rs, cs):
            acc = None
            for d in range(N_DEV):
                val = jnp.where(me == d, own_ref[rs, cs], early_ref[d, rs, cs])
                acc = val if acc is None else acc + val
            return acc

        outs[4 * ne][...] = early_sum(slice(loss_row, loss_row + 1), slice(0, LANES))[:, 0:1]
        for e, (source, row0, kind, w, _, _) in enumerate(entries):
            r, width = w.shape[0], w.shape[-1]
            from_late = lambda rs, cs: late_ref[rs, cs]
            gsum = early_sum if source == "early" else from_late

            if kind == "layers":
                for j, (src, rw) in enumerate(row0):
                    gj = (early_sum if src == "early" else from_late)(slice(rw, rw + 1), slice(0, D))
                    at = (slice(j, j + 1), slice(None))
                    d, mn, vn = _adamw_math(ins[3 * e][at], gj, ins[3 * e + 1][at], ins[3 * e + 2][at])
                    outs[4 * e][at] = gj
                    outs[4 * e + 1][at] = d
                    outs[4 * e + 2][at] = mn
                    outs[4 * e + 3][at] = vn
                continue
            if kind == "full":
                g = gsum(slice(row0, row0 + r), slice(0, D))
            elif kind in ("cols", "rows"):
                g = gsum(slice(row0, row0 + r), slice(0, width))
                for q in range(1, N_CHIPS):
                    g = jnp.where(ch == q, gsum(slice(row0, row0 + r), slice(q * width, (q + 1) * width)), g)
            else:
                per_row = D // width
                g = gsum(slice(row0, row0 + 1), slice(0, width))
                for q in range(1, N_CHIPS):
                    rr = row0 + q // per_row
                    cc = (q % per_row) * width
                    g = jnp.where(ch == q, gsum(slice(rr, rr + 1), slice(cc, cc + width)), g)
            for j in ([slice(None)] if kind != "rows" else range(r)):
                gj = g if kind != "rows" else g[j:j + 1, :]
                d, mn, vn = _adamw_math(ins[3 * e][j], gj, ins[3 * e + 1][j], ins[3 * e + 2][j])
                outs[4 * e][j] = gj
                outs[4 * e + 1][j] = d
                outs[4 * e + 2][j] = mn
                outs[4 * e + 3][j] = vn

    vm = pl.BlockSpec(memory_space=pltpu.VMEM)
    args, out_shape = [], []
    for _, _, _, w, m, v in entries:
        args += [w, m, v]
        out_shape += [jax.ShapeDtypeStruct(w.shape, F32)] * 4
    out_shape.append(jax.ShapeDtypeStruct((1, 1), F32))
    return pl.pallas_call(
        body, name=name,
        in_specs=[pl.BlockSpec(memory_space=pltpu.SMEM), vm, vm, vm] + [vm] * (3 * ne),
        out_specs=[vm] * (4 * ne + 1), out_shape=out_shape,
        compiler_params=pltpu.CompilerParams(vmem_limit_bytes=VMEM_LIMIT),
    )(place, late, early, own, *args)


def _pack_rows(items, width, name):
    starts, at = [], 0
    for it in items:
        starts.append(at)
        at += -(-it.shape[0] // SUBLANES) * SUBLANES
    total = at

    def body(*refs):
        o_ref = refs[-1]
        o_ref[...] = jnp.zeros_like(o_ref)
        for it_ref, r0 in zip(refs[:-1], starts):
            if len(it_ref.shape) == 3:
                for j in range(it_ref.shape[0]):
                    o_ref[r0 + j:r0 + j + 1, :] = it_ref[j]
            elif it_ref.shape == (1, 1):
                o_ref[r0:r0 + 1, :] = jnp.broadcast_to(it_ref[...], (1, width))
            else:
                o_ref[r0:r0 + it_ref.shape[0], :] = it_ref[...]

    vm = pl.BlockSpec(memory_space=pltpu.VMEM)
    packed = pl.pallas_call(body, name=name, in_specs=[vm] * len(items), out_specs=vm,
                            out_shape=jax.ShapeDtypeStruct((total, width), F32))(*items)
    return packed, starts


def kernel(x, a_norm, a_w_in, a_conv, a_w_out, b_norm, b_w_pw1, b_b_pw1, b_conv, b_b_conv, b_ln_g, b_ln_b, b_w_pw2, b_b_pw2, ffn_norm, ffn_w_gate, ffn_w_up, ffn_w_down, final_norm, loss_target, m_a_norm, m_a_w_in, m_a_conv, m_a_w_out, m_b_norm, m_b_w_pw1, m_b_b_pw1, m_b_conv, m_b_b_conv, m_b_ln_g, m_b_ln_b, m_b_w_pw2, m_b_b_pw2, m_ffn_norm, m_ffn_w_gate, m_ffn_w_up, m_ffn_w_down, m_final_norm, v_a_norm, v_a_w_in, v_a_conv, v_a_w_out, v_b_norm, v_b_w_pw1, v_b_b_pw1, v_b_conv, v_b_b_conv, v_b_ln_g, v_b_ln_b, v_b_w_pw2, v_b_b_pw2, v_ffn_norm, v_ffn_w_gate, v_ffn_w_up, v_ffn_w_down, v_final_norm):
    T, D = x.shape[1], x.shape[2]
    Dq = D // N_CHIPS
    cx, cy, cc = lax.axis_index("x"), lax.axis_index("y"), lax.axis_index("c")
    chip = (2 * cx + cy).astype(jnp.int32).reshape(1)
    cidx = cc.astype(jnp.int32).reshape(1)
    h0 = x.reshape(T, D)
    tgt = loss_target.reshape(T, D)

    rows3 = lambda t: jnp.swapaxes(t, 0, 1)
    small_shards = [rows3(a_conv), b_norm, b_b_pw1.reshape(2, Dq), rows3(b_conv), b_b_conv, b_ln_g, b_ln_b, b_b_pw2]
    packed, st = _pack_rows(small_shards, Dq, "pack_small")

    tr = lambda t: jnp.swapaxes(t, 1, 2)
    w_gate, m_gate, v_gate = tr(ffn_w_gate), tr(m_ffn_w_gate), tr(v_ffn_w_gate)
    w_up, m_up, v_up = tr(ffn_w_up), tr(m_ffn_w_up), tr(v_ffn_w_up)
    s_in = a_w_in[0].astype(BF16)
    (n0, s_out, s_pw1, s_pw2, *s_ffn), (g_in,) = rms_cast_weights(
        h0, a_norm, [(a_w_out, 0), (b_w_pw1, 0), (b_w_pw2, 0)] + [(t, l) for t in (w_gate, w_up, ffn_w_down) for l in (0, 1)],
        "rms_a_cast_weights", hosted=[gather_whole([s_in], [])])
    s_gate, s_up, s_down = s_ffn[0:2], s_ffn[2:4], s_ffn[4:6]
    bcv, (g_out, gate0, sw) = mm_cols(n0, g_in, "mm_w_in", hosted=[gather_p1([s_out, s_gate[0]]), gather_small(packed)])

    def whole(k, r):
        return jnp.transpose(sw[:, st[k]:st[k] + r, :], (1, 0, 2)).reshape(r, D)

    a_conv_f, b_norm_f = whole(0, 3), whole(1, 1)
    b_b_pw1_f = sw[:, st[2]:st[2] + 2, :].reshape(1, 2 * D)
    b_conv_f, b_b_conv_f, b_ln_g_f, b_ln_b_f, b_b_pw2_f = whole(3, b_conv.shape[1]), whole(4, 1), whole(5, 1), whole(6, 1), whole(7, 1)
    ya, h1, (g_out, up0, down0, gate0) = gateconv_fwd(bcv, a_conv_f, gather_p2([g_out]), h0, "gateconv_fwd",
                                                      hosted=[gather_p1([s_up[0], s_down[0]]), gather_p2([gate0])])
    g_out = g_out.reshape(1, D, D)
    n1, fg0, fu0, gu0, h2, (up0, down0, g_pw1, g_pw2, gate1, up1) = ffn_fwd(
        h1, ffn_norm[0:1], [gate0], "ffn_fwd0", arriving=gather_p2([up0, down0]),
        hosted=[gather_whole([s_pw1, s_pw2], [s_gate[1], s_up[1]])])
    g_pw2 = g_pw2.reshape(1, D, D)
    n2, ub, (down1, gate1, up1) = rms_mm_cols(h2, b_norm_f, g_pw1, b_b_pw1_f, "mm_pw1",
                                              hosted=[gather_p1([s_down[1]]), gather_p2([gate1, up1])])
    cu, sb, h3, (down1,) = bconv_fwd(ub, b_conv_f, b_b_conv_f, b_ln_g_f, b_ln_b_f, g_pw2, b_b_pw2_f, h2, "bconv_fwd",
                                     hosted=[gather_p2([down1])])
    n3, fg1, fu1, gu1, h4, _ = ffn_fwd(h3, ffn_norm[1:2], [gate1, up1, down1], "ffn_fwd1")
    loss_part, dh4, dh4_b, d_final = loss_head(h4, final_norm.reshape(1, D), tgt, "loss_head")

    place = jnp.concatenate([chip, cidx])

    def pair_sums(ghs, from_sib, tags):
        return pair_sum(ghs, from_sib, cidx, "pair_sum_" + "_".join(tags))

    def upd(wmvs, bufs, parts, tag):
        flat = None
        for lyr in range(len(bufs[0])):
            tensors = [(w, m, v, b[lyr], p[lyr]) for (w, m, v), b, p in zip(wmvs, bufs, parts)]
            flat = adamw_reduce(tensors, place, lyr, flat, "adamw_%s%d" % (tag, lyr))
        return [flat[4 * k:4 * k + 4] for k in range(len(wmvs))]

    dg1, du1, dh3, dh3_b, d_fn1, _ = ffn_bwd(dh4, h3, ffn_norm[1:2], fg1, fu1, down1, gate1, up1, "ffn_bwd1")
    gh_down1, _ = tn_grad(gu1, dh4_b, N_CHIPS, True, "tn_down1")
    gh_gate1, _ = tn_grad(dg1, n3, N_CHIPS, True, "tn_gate1")
    gh_up1, _ = tn_grad(du1, n3, N_CHIPS, True, "tn_up1")
    f1 = [gh_gate1, gh_up1, gh_down1]

    dcu, d_ln_g, d_ln_b, d_b_conv, d_b_pw2, sib_f1 = pw2_ln_bwd(dh3, g_pw2, cu, b_ln_g_f, b_ln_b_f, "pw2_ln_bwd",
                                                                hosted=[sibling_halves(f1)])
    p_f1 = pair_sums(f1, sib_f1, ["gate1", "up1", "down1"])
    gh_pw2, _ = tn_grad_square(sb, dh3_b, N_CHIPS, "tn_pw2")
    dub, d_bconv_w, d_b_pw1, buf_f1 = bconv_bwd(dcu, ub, b_conv_f, "bconv_bwd", hosted=[scatter_p1(p_f1)])
    gh_pw1, _ = tn_grad(n2, dub, N_CHIPS, False, "tn_pw1")
    b_grp = [gh_pw1, gh_pw2]
    dh2, d_b_norm, dh2_b, (*buf_f1, sib_pw1, sib_pw2) = nt_cols_rms(
        dub, g_pw1, h2, b_norm_f, dh3, "nt_pw1", hosted=[scatter_p2(buf_f1), sibling_halves(b_grp)], also_bf16=True)
    sib_b = [sib_pw1, sib_pw2]
    p_b = pair_sums(b_grp, sib_b, ["pw1", "pw2"])

    early_grads = [d_b_norm, d_b_pw1.reshape(2, D), d_bconv_w, d_b_conv, d_ln_g, d_ln_b, d_b_pw2, d_fn1, d_final, loss_part]
    epacked, es = _pack_rows(early_grads, D, "pack_small_grads_early")
    dg0, du0, dh1, dh1_b, d_fn0, (*buf_b, eall) = ffn_bwd(dh2, h1, ffn_norm[0:1], fg0, fu0, down0, gate0, up0, "ffn_bwd0",
                                                         hosted=[scatter_p1(p_b), gather_all(epacked)])
    gh_down0, _ = tn_grad(gu0, dh2_b, N_CHIPS, True, "tn_down0")
    gh_gate0, (*buf_b, sib_down0) = tn_grad(dg0, n1, N_CHIPS, True, "tn_gate0",
                                            hosted=[scatter_p2(buf_b), sibling_halves([gh_down0])])
    p_down0 = pair_sums([gh_down0], [sib_down0], ["down0"])
    gh_up0, (buf_down0, sib_gate0) = tn_grad(du0, n1, N_CHIPS, True, "tn_up0",
                                             hosted=[scatter_p1(p_down0), sibling_halves([gh_gate0])])
    p_gate0 = pair_sums([gh_gate0], [sib_gate0], ["gate0"])
    gh_out, (buf_down0, sib_up0) = tn_grad_square(ya, dh1_b, N_CHIPS, "tn_w_out",
                                                  hosted=[scatter_p2([buf_down0]), sibling_halves([gh_up0])])
    p_up0 = pair_sums([gh_up0], [sib_up0], ["up0"])
    dbcv, d_aconv_w, (buf_gate0, sib_out) = gateconv_bwd(dh1_b, g_out, bcv, a_conv_f, "gateconv_bwd",
                                                         hosted=[scatter_p1(p_gate0), sibling_halves([gh_out])])
    p_out = pair_sums([gh_out], [sib_out], ["out"])
    gh_in, (buf_up0, buf_out, buf_gate0) = tn_grad(n0, dbcv, N_CHIPS, False, "tn_w_in",
                                                   hosted=[scatter_p1(p_up0 + p_out), scatter_p2([buf_gate0])])
    sib_in = run_exchanges([sibling_halves([gh_in])], "reduce_in_siblings")
    p_in = pair_sums([gh_in], sib_in, ["in"])
    tail = T - _tile(T, TOKEN_TILE)
    grad_x, d_a_norm, (buf_in, buf_up0, buf_out) = nt_cols_rms(
        dbcv, g_in, h0, a_norm, dh1, "nt_w_in", hosted=[scatter_p1(p_in), scatter_p2([buf_up0, buf_out])], tokens=(0, tail))
    grad_x, d_a_norm, _ = nt_cols_rms(dbcv, g_in, h0, a_norm, dh1, "nt_w_in_tail", tokens=(tail, T), base=(grad_x, d_a_norm))
    p_f0 = [p_gate0[0], p_up0[0], p_down0[0]]

    lpacked, ls = _pack_rows([d_a_norm, d_aconv_w, d_fn0], D, "pack_small_grads_late")
    lall, (buf_in,) = small_allreduce(lpacked, "allreduce_small_grads", hosted=[scatter_p2([buf_in])])
    buf_a, p_a = [buf_in, buf_out], [p_in[0], p_out[0]]

    r_gate, r_up, r_down = upd([(w_gate, m_gate, v_gate), (w_up, m_up, v_up), (ffn_w_down, m_ffn_w_down, v_ffn_w_down)],
                               [[buf_gate0, buf_f1[0]], [buf_up0, buf_f1[1]], [buf_down0, buf_f1[2]]],
                               [[p_f0[0], p_f1[0]], [p_f0[1], p_f1[1]], [p_f0[2], p_f1[2]]], "ffn")
    r_gate, r_up = [tr(t) for t in r_gate], [tr(t) for t in r_up]
    (r_pw1,) = upd([(b_w_pw1, m_b_w_pw1, v_b_w_pw1)], [[buf_b[0]]], [[p_b[0]]], "pw1")
    r_pw2, r_out = upd([(b_w_pw2, m_b_w_pw2, v_b_w_pw2), (a_w_out, m_a_w_out, v_a_w_out)],
                       [[buf_b[1]], [buf_a[1]]], [[p_b[1]], [p_a[1]]], "pw2_out")
    (r_in,) = upd([(a_w_in, m_a_w_in, v_a_w_in)], [[buf_a[0]]], [[p_a[0]]], "w_in")
    entries = [
        ("late", ls[0], "full", a_norm, m_a_norm, v_a_norm),
        ("late", ls[1], "rows", rows3(a_conv), rows3(m_a_conv), rows3(v_a_conv)),
        ("early", es[0], "cols", b_norm, m_b_norm, v_b_norm),
        ("early", es[1], "flat2", b_b_pw1, m_b_b_pw1, v_b_b_pw1),
        ("early", es[2], "rows", rows3(b_conv), rows3(m_b_conv), rows3(v_b_conv)),
        ("early", es[3], "cols", b_b_conv, m_b_b_conv, v_b_b_conv),
        ("early", es[4], "cols", b_ln_g, m_b_ln_g, v_b_ln_g),
        ("early", es[5], "cols", b_ln_b, m_b_ln_b, v_b_ln_b),
        ("early", es[6], "cols", b_b_pw2, m_b_b_pw2, v_b_b_pw2),
        (None, [("late", ls[2]), ("early", es[7])], "layers", ffn_norm, m_ffn_norm, v_ffn_norm),
        ("early", es[8], "full", final_norm.reshape(1, D), m_final_norm.reshape(1, D), v_final_norm.reshape(1, D)),
    ]
    so = small_update(lall, eall, epacked, place, entries, es[9], "small_update")
    sm = [so[4 * e:4 * e + 4] for e in range(len(entries))]

    def shaped(e, like):
        return [t.reshape(like.shape) for t in sm[e]]

    r_a_norm, r_a_conv, r_b_norm, r_b_b_pw1 = shaped(0, a_norm), shaped(1, a_conv), shaped(2, b_norm), shaped(3, b_b_pw1)
    r_b_conv, r_b_b_conv, r_b_ln_g, r_b_ln_b = shaped(4, b_conv), shaped(5, b_b_conv), shaped(6, b_ln_g), shaped(7, b_ln_b)
    r_b_b_pw2, r_ffn_norm, r_final = shaped(8, b_b_pw2), sm[9], shaped(10, final_norm)

    loss = so[4 * len(entries)].reshape(())
    order =[r_a_norm, r_in, r_a_conv, r_out, r_b_norm, r_pw1, r_b_b_pw1, r_b_conv, r_b_b_conv, r_b_ln_g, r_b_ln_b,
             r_pw2, r_b_b_pw2, r_ffn_norm, r_gate, r_up, r_down, r_final]
    outs = [loss, grad_x.reshape(x.shape)]
    for field in range(4):
        outs += [r[field] for r in order]
    return tuple(outs)
```

```python
import functools

import jax
import jax.numpy as jnp
from jax import lax
from jax.experimental import pallas as pl
from jax.experimental.pallas import tpu as pltpu

RMS_EPS = 1e-6
LN_EPS = 1e-5
ADAM_LR = 0.001
ADAM_B1 = 0.9
ADAM_B2 = 0.999
ADAM_EPS = 1e-08
ADAM_WD = 0.01
ADAM_STEP = 10

N_CHIPS = 4
N_DEV = 8
LANES = 128
SUBLANES = 8
HALO = 32
CONV_ROWS = 64
TOKEN_TILE = 512
WIDE_TOKEN_TILE = 1024
GRAD_TOKEN_TILE = 2048
GRAD_SEGS_PER_STEP = 2
FFN_ROW_CHUNKS = 2
FFN_FWD_SEGS_PER_STEP = 4
FFN_BWD_TOKEN_TILE = 256
ROW_TILE = 256
CAST_STEPS = 4
VMEM_LIMIT = 56 * 1024 * 1024
MESH = pl.DeviceIdType.MESH
BF16 = jnp.bfloat16
F32 = jnp.float32


def _tile(n, pref, mult=SUBLANES):
    t = min(n, pref) // mult * mult
    while n % t:
        t -= mult
    return t


def _params(sem):
    return pltpu.CompilerParams(dimension_semantics=sem, vmem_limit_bytes=VMEM_LIMIT)


def _sigmoid(x):
    return 0.5 * jnp.tanh(0.5 * x) + 0.5


class _Exchange:
    def __init__(self, ins, outs, aliases, n_sems, copies, then=None):
        self.ins, self.outs, self.aliases, self.n_sems, self.copies = list(ins), list(outs), dict(aliases), n_sems, copies
        self.then = then
        self.early = False

    def awaited_first(self):
        self.early = True
        return self

    def start(self, xi, xo, ssem, rsem):
        for cp in self.copies(xi, xo, ssem, rsem)[0]:
            cp.start()

    def finish(self, xi, xo, ssem, rsem):
        sends, recvs = self.copies(xi, xo, ssem, rsem)
        for cp in recvs:
            cp.wait_recv()
        if self.then is not None:
            sends2, recvs2 = self.then(xi, xo, ssem, rsem)
            for cp in sends2:
                cp.start()
            for cp in recvs2:
                cp.wait_recv()
            sends = sends + sends2
        for cp in sends:
            cp.wait_send()


def _call(body, name, grid, in_specs, out_specs, out_shape, args, sem, scratch_shapes=(), hosted=(), prefetch=(),
          own_aliases=None):
    in_specs, out_specs, out_shape = list(in_specs), list(out_specs), list(out_shape)
    scratch_shapes, hosted, prefetch = list(scratch_shapes), list(hosted), list(prefetch)
    n_pre, n_in, n_out, n_scr = len(prefetch), len(args), len(out_shape), len(scratch_shapes)
    x_in = [a for ex in hosted for a in ex.ins]
    x_out = [o for ex in hosted for o in ex.outs]
    aliases = {n_pre + i: o for i, o in (own_aliases or {}).items()}
    at_in, at_out = n_pre + n_in, n_out
    for ex in hosted:
        for i, o in ex.aliases.items():
            aliases[at_in + i] = at_out + o
        at_in += len(ex.ins)
        at_out += len(ex.outs)
    sems = [pltpu.SemaphoreType.DMA((ex.n_sems,)) for ex in hosted for _ in range(2)]

    def wrapped(*refs):
        pre, refs = refs[:n_pre], refs[n_pre:]
        ins, xi = refs[:n_in], refs[n_in:n_in + len(x_in)]
        refs = refs[n_in + len(x_in):]
        outs, xo = refs[:n_out], refs[n_out:n_out + len(x_out)]
        refs = refs[n_out + len(x_out):]
        scr, sm = refs[:n_scr], refs[n_scr:]
        views, a, b = [], 0, 0
        for e, ex in enumerate(hosted):
            views.append((xi[a:a + len(ex.ins)], xo[b:b + len(ex.outs)], sm[2 * e], sm[2 * e + 1]))
            a += len(ex.ins)
            b += len(ex.outs)
        first = last = None
        for ax, g in enumerate(grid):
            f, l = pl.program_id(ax) == 0, pl.program_id(ax) == g - 1
            first, last = (f, l) if first is None else (first & f, last & l)

        def begin():
            for ex, v in zip(hosted, views):
                ex.start(*v)
            for ex, v in zip(hosted, views):
                if ex.early:
                    ex.finish(*v)

        def end():
            for ex, v in zip(hosted, views):
                if not ex.early:
                    ex.finish(*v)

        if hosted and grid:
            pl.when(first)(begin)
        elif hosted:
            begin()
        early_refs = [r for ex, v in zip(hosted, views) if ex.early for r in v[1]]
        body(*pre, *ins, *outs, *scr, *early_refs)
        if hosted and grid:
            pl.when(last)(end)
        elif hosted:
            end()

    hbm = pl.BlockSpec(memory_space=pl.ANY)
    all_in, all_out = in_specs + [hbm] * len(x_in), out_specs + [hbm] * len(x_out)
    kw = dict(name=name, out_shape=out_shape + x_out, input_output_aliases=aliases,
              compiler_params=_params(tuple("arbitrary" for _ in grid) if hosted else sem))
    if prefetch:
        kw["grid_spec"] = pltpu.PrefetchScalarGridSpec(num_scalar_prefetch=n_pre, grid=grid, in_specs=all_in,
                                                       out_specs=all_out, scratch_shapes=scratch_shapes + sems)
    else:
        kw.update(grid=grid, in_specs=all_in, out_specs=all_out, scratch_shapes=scratch_shapes + sems)
    res = pl.pallas_call(wrapped, **kw)(*prefetch, *args, *x_in)
    return list(res[:n_out]), list(res[n_out:])


def rms_cast_weights(h, gain, tensors, name, hosted=()):
    T, D = h.shape
    tm = T // CAST_STEPS
    assert tm * CAST_STEPS == T and tm % (2 * SUBLANES) == 0
    n = len(tensors)

    def body(h_ref, g_ref, *refs):
        x = h_ref[...]
        r = lax.rsqrt(jnp.mean(x * x, axis=-1, keepdims=True) + RMS_EPS)
        refs[n][...] = (x * r * g_ref[...]).astype(refs[n].dtype)
        for w_ref, o_ref in zip(refs[:n], refs[n + 1:]):
            o_ref[...] = w_ref[...].astype(o_ref.dtype)

    in_specs = [pl.BlockSpec((tm, D), lambda i: (i, 0)), pl.BlockSpec((1, D), lambda i: (0, 0))]
    out_specs, out_shape = [pl.BlockSpec((tm, D), lambda i: (i, 0))], [jax.ShapeDtypeStruct((T, D), BF16)]
    for w, lyr in tensors:
        R, C = w.shape[-2:]
        rb = R // CAST_STEPS
        assert rb * CAST_STEPS == R and rb % (2 * SUBLANES) == 0
        in_specs.append(pl.BlockSpec((None, rb, C), lambda i, lyr=lyr: (lyr, i, 0)))
        out_specs.append(pl.BlockSpec((rb, C), lambda i: (i, 0)))
        out_shape.append(jax.ShapeDtypeStruct((R, C), BF16))
    return _call(body, name, (CAST_STEPS,), in_specs, out_specs, out_shape, [h, gain] + [w for w, _ in tensors],
                 ("parallel",), hosted=hosted)


def loss_head(h, gain, tgt, name):
    T, D = h.shape
    tm = _tile(T, TOKEN_TILE)

    def body(h_ref, g_ref, t_ref, loss_ref, dh_ref, dhb_ref, dg_ref):
        i = pl.program_id(0)
        x = h_ref[...]
        g = g_ref[...]
        r = lax.rsqrt(jnp.mean(x * x, axis=-1, keepdims=True) + RMS_EPS)
        xhat = x * r
        diff = xhat * g - t_ref[...]
        part_loss = 0.5 * jnp.sum(jnp.mean(diff * diff, axis=-1, keepdims=True), axis=0, keepdims=True)
        dy = diff * (1.0 / D)
        dxhat = dy * g
        dh = r * (dxhat - xhat * jnp.mean(dxhat * xhat, axis=-1, keepdims=True))
        dh_ref[...] = dh
        dhb_ref[...] = dh.astype(dhb_ref.dtype)
        part = jnp.sum(dy * xhat, axis=0, keepdims=True)

        @pl.when(i == 0)
        def _():
            dg_ref[...] = part
            loss_ref[...] = part_loss

        @pl.when(i > 0)
        def _():
            dg_ref[...] += part
            loss_ref[...] += part_loss

    row = pl.BlockSpec((tm, D), lambda i: (i, 0))
    vec = pl.BlockSpec((1, D), lambda i: (0, 0))
    return pl.pallas_call(
        body, name=name, grid=(T // tm,),
        in_specs=[row, vec, row],
        out_specs=[pl.BlockSpec((1, 1), lambda i: (0, 0)), row, row, vec],
        out_shape=[jax.ShapeDtypeStruct((1, 1), F32), jax.ShapeDtypeStruct((T, D), F32),
                   jax.ShapeDtypeStruct((T, D), BF16), jax.ShapeDtypeStruct((1, D), F32)],
        compiler_params=_params(("arbitrary",)),
    )(h, gain, tgt)


def _prev_halo_spec(tm, width):
    return pl.BlockSpec((HALO, width), lambda i: (jnp.maximum(i * (tm // HALO) - 1, 0), 0))


def _next_halo_spec(tm, width, T):
    return pl.BlockSpec((HALO, width), lambda i: (jnp.minimum((i + 1) * (tm // HALO), T // HALO - 1), 0))


def _shifted(win, off, rows):
    if off % SUBLANES == 0:
        return win[off:off + rows]
    n = win.shape[0]
    return pltpu.roll(win, (n - off) % n, 0)[:rows]


def _rowsum8(x):
    acc = x[0:SUBLANES]
    for q in range(1, x.shape[0] // SUBLANES):
        acc = acc + x[q * SUBLANES:(q + 1) * SUBLANES]
    return acc


def _conv_loops(tm, D, per_block):
    def chunk(r, carry):
        t0 = pl.multiple_of(r * CONV_ROWS, CONV_ROWS)
        for lb in range(D // LANES):
            per_block(t0, slice(lb * LANES, (lb + 1) * LANES))
        return carry

    lax.fori_loop(0, tm // CONV_ROWS, chunk, 0)


def gateconv_fwd(bcv, w, w_out, res, name, hosted=()):
    T, D3 = bcv.shape
    D = D3 // 3
    K = w.shape[0]
    tm = _tile(T, TOKEN_TILE)
    wo_shape = w_out.outs[0].shape

    def body(x_ref, halo_ref, w_ref, res_ref, y_ref, h_ref, pad_ref, wo_v, sem, wo_hbm):
        i = pl.program_id(0)

        @pl.when(i == 0)
        def _():
            cp = pltpu.make_async_copy(wo_hbm, wo_v, sem)
            cp.start()
            cp.wait()

        pad_ref[HALO:, :] = x_ref[:, D:2 * D] * x_ref[:, 2 * D:]
        pad_ref[:HALO, :] = jnp.where(i > 0, halo_ref[:, D:2 * D] * halo_ref[:, 2 * D:], 0.0)

        def block(t0, ls):
            win = pad_ref[pl.ds(t0, CONV_ROWS + HALO), ls]
            acc = jnp.zeros((CONV_ROWS, LANES), F32)
            for k in range(K):
                acc = acc + w_ref[k:k + 1, ls] * _shifted(win, HALO - (K - 1) + k, CONV_ROWS)
            y_ref[pl.ds(t0, CONV_ROWS), ls] = (x_ref[pl.ds(t0, CONV_ROWS), ls] * acc).astype(y_ref.dtype)

        _conv_loops(tm, D, block)
        h_ref[...] = res_ref[...] + jnp.dot(y_ref[...], wo_v[...].reshape(D, D), preferred_element_type=F32)

    row = pl.BlockSpec((tm, D), lambda i: (i, 0))
    (y, h), xo = _call(
        body, name, (T // tm,),
        [pl.BlockSpec((tm, D3), lambda i: (i, 0)), _prev_halo_spec(tm, D3), pl.BlockSpec((K, D), lambda i: (0, 0)), row],
        [row, row], [jax.ShapeDtypeStruct((T, D), BF16), jax.ShapeDtypeStruct((T, D), F32)],
        [bcv, bcv, w, res], ("arbitrary",),
        [pltpu.VMEM((tm + HALO, D), F32), pltpu.VMEM(wo_shape, BF16), pltpu.SemaphoreType.DMA],
        hosted=[w_out.awaited_first()] + list(hosted))
    return y, h, xo


def gateconv_bwd(dh, w_out, bcv, w, name, hosted=()):
    T, D3 = bcv.shape
    D = D3 // 3
    K = w.shape[0]
    tm = _tile(T, TOKEN_TILE)
    nt = T // tm

    def body(dh_ref, dhn_ref, wo_ref, x_ref, xp_ref, xn_ref, w_ref, o_ref, dw_ref, cv_ref, dc_ref, wacc_ref, dy_ref):
        i = pl.program_id(0)
        dy_ref[...] = lax.dot_general(dh_ref[...], wo_ref[0], _NT, preferred_element_type=F32)
        dyn = lax.dot_general(dhn_ref[...], wo_ref[0], _NT, preferred_element_type=F32)
        cv_ref[HALO:, :] = x_ref[:, D:2 * D] * x_ref[:, 2 * D:]
        cv_ref[:HALO, :] = jnp.where(i > 0, xp_ref[:, D:2 * D] * xp_ref[:, 2 * D:], 0.0)
        dc_ref[:tm, :] = dy_ref[...] * x_ref[:, :D]
        dc_ref[tm:, :] = jnp.where(i < nt - 1, dyn * xn_ref[:, :D], 0.0)

        @pl.when(i == 0)
        def _():
            wacc_ref[...] = jnp.zeros_like(wacc_ref)

        def block(t0, ls):
            cwin = cv_ref[pl.ds(t0, CONV_ROWS + HALO), ls]
            dwin = dc_ref[pl.ds(t0, CONV_ROWS + HALO), ls]
            dcon = dwin[:CONV_ROWS]
            conv = jnp.zeros((CONV_ROWS, LANES), F32)
            dcv = jnp.zeros((CONV_ROWS, LANES), F32)
            for k in range(K):
                wk = w_ref[k:k + 1, ls]
                cs = _shifted(cwin, HALO - (K - 1) + k, CONV_ROWS)
                conv = conv + wk * cs
                dcv = dcv + wk * _shifted(dwin, (K - 1) - k, CONV_ROWS)
                wacc_ref[k * SUBLANES:(k + 1) * SUBLANES, ls] += _rowsum8(dcon * cs)
            rows = pl.ds(t0, CONV_ROWS)
            o_ref[rows, ls] = (dy_ref[rows, ls] * conv).astype(o_ref.dtype)
            o_ref[rows, pl.ds(D + ls.start, LANES)] = (dcv * x_ref[rows, pl.ds(2 * D + ls.start, LANES)]).astype(o_ref.dtype)
            o_ref[rows, pl.ds(2 * D + ls.start, LANES)] = (dcv * x_ref[rows, pl.ds(D + ls.start, LANES)]).astype(o_ref.dtype)

        _conv_loops(tm, D, block)

        @pl.when(i == nt - 1)
        def _():
            for k in range(K):
                dw_ref[k:k + 1, :] = jnp.sum(wacc_ref[k * SUBLANES:(k + 1) * SUBLANES, :], axis=0, keepdims=True)

    (dx, dw), xo = _call(
        body, name, (nt,),
        [pl.BlockSpec((tm, D), lambda i: (i, 0)), _next_halo_spec(tm, D, T), pl.BlockSpec((1, D, D), lambda i: (0, 0, 0)),
         pl.BlockSpec((tm, D3), lambda i: (i, 0)), _prev_halo_spec(tm, D3), _next_halo_spec(tm, D3, T),
         pl.BlockSpec((K, D), lambda i: (0, 0))],
        [pl.BlockSpec((tm, D3), lambda i: (i, 0)), pl.BlockSpec((K, D), lambda i: (0, 0))],
        [jax.ShapeDtypeStruct((T, D3), BF16), jax.ShapeDtypeStruct((K, D), F32)],
        [dh, dh, w_out, bcv, bcv, bcv, w], ("arbitrary",),
        [pltpu.VMEM((tm + HALO, D), F32), pltpu.VMEM((tm + HALO, D), F32), pltpu.VMEM((K * SUBLANES, D), F32),
         pltpu.VMEM((tm, D), F32)], hosted=hosted)
    return dx, dw, xo


def bconv_fwd(u, w, b_conv, ln_g, ln_b, w_out, b_out, res, name, hosted=()):
    T, D2 = u.shape
    D = D2 // 2
    K = w.shape[0]
    tm = _tile(T, TOKEN_TILE)

    def body(u_ref, halo_ref, w_ref, bc_ref, g_ref, b_ref, wo_ref, bo_ref, res_ref, cu_ref, s_ref, h_ref, pad_ref):
        i = pl.program_id(0)
        pad_ref[HALO:, :] = u_ref[:, :D] * _sigmoid(u_ref[:, D:])
        pad_ref[:HALO, :] = jnp.where(i > 0, halo_ref[:, :D] * _sigmoid(halo_ref[:, D:]), 0.0)

        def block(t0, ls):
            win = pad_ref[pl.ds(t0, CONV_ROWS + HALO), ls]
            acc = jnp.zeros((CONV_ROWS, LANES), F32)
            for k in range(K):
                acc = acc + w_ref[k:k + 1, ls] * _shifted(win, HALO - (K - 1) + k, CONV_ROWS)
            cu_ref[pl.ds(t0, CONV_ROWS), ls] = acc + bc_ref[:, ls]

        _conv_loops(tm, D, block)
        cu = cu_ref[...]
        mu = jnp.mean(cu, axis=-1, keepdims=True)
        xc = cu - mu
        rstd = lax.rsqrt(jnp.mean(xc * xc, axis=-1, keepdims=True) + LN_EPS)
        ln = xc * rstd * g_ref[...] + b_ref[...]
        s = (ln * _sigmoid(ln)).astype(s_ref.dtype)
        s_ref[...] = s
        h_ref[...] = res_ref[...] + bo_ref[...] + jnp.dot(s, wo_ref[0], preferred_element_type=F32)

    vec = pl.BlockSpec((1, D), lambda i: (0, 0))
    row = pl.BlockSpec((tm, D), lambda i: (i, 0))
    (cu, s, h), xo = _call(
        body, name, (T // tm,),
        [pl.BlockSpec((tm, D2), lambda i: (i, 0)), _prev_halo_spec(tm, D2), pl.BlockSpec((K, D), lambda i: (0, 0)), vec, vec, vec,
         pl.BlockSpec((1, D, D), lambda i: (0, 0, 0)), vec, row],
        [row, row, row], [jax.ShapeDtypeStruct((T, D), F32), jax.ShapeDtypeStruct((T, D), BF16), jax.ShapeDtypeStruct((T, D), F32)],
        [u, u, w, b_conv, ln_g, ln_b, w_out, b_out, res], ("parallel",), [pltpu.VMEM((tm + HALO, D), F32)], hosted=hosted)
    return cu, s, h, xo


def pw2_ln_bwd(dy, w, cu, ln_g, ln_b, name, hosted=()):
    T, D = cu.shape
    tm = _tile(T, TOKEN_TILE)

    def body(dy_ref, w_ref, cu_ref, g_ref, b_ref, dcu_ref, dg_ref, db_ref, dbc_ref, dbo_ref):
        i = pl.program_id(0)
        dy_ = dy_ref[...]
        ds = lax.dot_general(dy_.astype(BF16), w_ref[0], _NT, preferred_element_type=F32)
        cu_ = cu_ref[...]
        mu = jnp.mean(cu_, axis=-1, keepdims=True)
        xc = cu_ - mu
        rstd = lax.rsqrt(jnp.mean(xc * xc, axis=-1, keepdims=True) + LN_EPS)
        xh = xc * rstd
        ln = xh * g_ref[...] + b_ref[...]
        sg = _sigmoid(ln)
        dl = ds * (sg * (1.0 + ln * (1.0 - sg)))
        dxh = dl * g_ref[...]
        dcu = rstd * (dxh - jnp.mean(dxh, axis=-1, keepdims=True) - xh * jnp.mean(dxh * xh, axis=-1, keepdims=True))
        dcu_ref[...] = dcu
        pg = jnp.sum(dl * xh, axis=0, keepdims=True)
        pb = jnp.sum(dl, axis=0, keepdims=True)
        pc = jnp.sum(dcu, axis=0, keepdims=True)
        po = jnp.sum(dy_, axis=0, keepdims=True)

        @pl.when(i == 0)
        def _():
            dg_ref[...] = pg
            db_ref[...] = pb
            dbc_ref[...] = pc
            dbo_ref[...] = po

        @pl.when(i > 0)
        def _():
            dg_ref[...] += pg
            db_ref[...] += pb
            dbc_ref[...] += pc
            dbo_ref[...] += po

    vec = pl.BlockSpec((1, D), lambda i: (0, 0))
    row = pl.BlockSpec((tm, D), lambda i: (i, 0))
    vshape = jax.ShapeDtypeStruct((1, D), F32)
    outs, xo = _call(
        body, name, (T // tm,), [row, pl.BlockSpec((1, D, D), lambda i: (0, 0, 0)), row, vec, vec], [row, vec, vec, vec, vec],
        [jax.ShapeDtypeStruct((T, D), F32), vshape, vshape, vshape, vshape], [dy, w, cu, ln_g, ln_b], ("arbitrary",),
        hosted=hosted)
    return (*outs, xo)


def bconv_bwd(dcu, u, w, name, hosted=()):
    T, D2 = u.shape
    D = D2 // 2
    K = w.shape[0]
    tm = _tile(T, TOKEN_TILE)
    nt = T // tm

    def body(dc_ref, dcn_ref, u_ref, up_ref, w_ref, du_ref, dw_ref, db_ref, glu_ref, dpad_ref, dglu_ref, wacc_ref):
        i = pl.program_id(0)
        glu_ref[HALO:, :] = u_ref[:, :D] * _sigmoid(u_ref[:, D:])
        glu_ref[:HALO, :] = jnp.where(i > 0, up_ref[:, :D] * _sigmoid(up_ref[:, D:]), 0.0)
        dpad_ref[:tm, :] = dc_ref[...]
        dpad_ref[tm:, :] = jnp.where(i < nt - 1, dcn_ref[...], 0.0)

        @pl.when(i == 0)
        def _():
            wacc_ref[...] = jnp.zeros_like(wacc_ref)

        def block(t0, ls):
            gwin = glu_ref[pl.ds(t0, CONV_ROWS + HALO), ls]
            dwin = dpad_ref[pl.ds(t0, CONV_ROWS + HALO), ls]
            dcur = dwin[:CONV_ROWS]
            dglu = jnp.zeros((CONV_ROWS, LANES), F32)
            for k in range(K):
                dglu = dglu + w_ref[k:k + 1, ls] * _shifted(dwin, (K - 1) - k, CONV_ROWS)
                gs = _shifted(gwin, HALO - (K - 1) + k, CONV_ROWS)
                wacc_ref[k * SUBLANES:(k + 1) * SUBLANES, ls] += _rowsum8(dcur * gs)
            dglu_ref[pl.ds(t0, CONV_ROWS), ls] = dglu

        _conv_loops(tm, D, block)
        dglu = dglu_ref[...]
        a = u_ref[:, :D]
        sg = _sigmoid(u_ref[:, D:])
        da = dglu * sg
        dg = dglu * a * (sg * (1.0 - sg))
        du_ref[:, :D] = da.astype(du_ref.dtype)
        du_ref[:, D:] = dg.astype(du_ref.dtype)
        pa = jnp.sum(da, axis=0, keepdims=True)
        pg = jnp.sum(dg, axis=0, keepdims=True)

        @pl.when(i == 0)
        def _():
            db_ref[:, :D] = pa
            db_ref[:, D:] = pg

        @pl.when(i > 0)
        def _():
            db_ref[:, :D] += pa
            db_ref[:, D:] += pg

        @pl.when(i == nt - 1)
        def _():
            for k in range(K):
                dw_ref[k:k + 1, :] = jnp.sum(wacc_ref[k * SUBLANES:(k + 1) * SUBLANES, :], axis=0, keepdims=True)

    (du, dw, db), xo = _call(
        body, name, (nt,),
        [pl.BlockSpec((tm, D), lambda i: (i, 0)), _next_halo_spec(tm, D, T),
         pl.BlockSpec((tm, D2), lambda i: (i, 0)), _prev_halo_spec(tm, D2), pl.BlockSpec((K, D), lambda i: (0, 0))],
        [pl.BlockSpec((tm, D2), lambda i: (i, 0)), pl.BlockSpec((K, D), lambda i: (0, 0)), pl.BlockSpec((1, D2), lambda i: (0, 0))],
        [jax.ShapeDtypeStruct((T, D2), BF16), jax.ShapeDtypeStruct((K, D), F32), jax.ShapeDtypeStruct((1, D2), F32)],
        [dcu, dcu, u, u, w], ("arbitrary",),
        [pltpu.VMEM((tm + HALO, D), F32), pltpu.VMEM((tm + HALO, D), F32), pltpu.VMEM((tm, D), F32),
         pltpu.VMEM((K * SUBLANES, D), F32)], hosted=hosted)
    return du, dw, db, xo


def mm_cols(a, w, name, hosted=()):
    T, K = a.shape
    S, _, n = w.shape
    tm = _tile(T, WIDE_TOKEN_TILE)

    def body(a_ref, w_ref, o_ref):
        o_ref[...] = jnp.dot(a_ref[...], w_ref[...], preferred_element_type=F32)

    in_specs = [pl.BlockSpec((tm, K), lambda s, i: (i, 0)), pl.BlockSpec((None, K, n), lambda s, i: (s, 0, 0))]
    (out,), xo = _call(body, name, (S, T // tm), in_specs, [pl.BlockSpec((tm, n), lambda s, i: (i, s))],
                       [jax.ShapeDtypeStruct((T, S * n), F32)], [a, w], ("parallel", "parallel"), hosted=hosted)
    return out, xo


def rms_mm_cols(h, gain, w, bias, name, hosted=()):
    T, K = h.shape
    S, _, n = w.shape
    tm = _tile(T, TOKEN_TILE)

    def body(h_ref, gain_ref, w_ref, b_ref, n_ref, o_ref):
        x = h_ref[...]
        r = lax.rsqrt(jnp.mean(x * x, axis=-1, keepdims=True) + RMS_EPS)
        a = (x * r * gain_ref[...]).astype(n_ref.dtype)
        n_ref[...] = a
        for s in range(S):
            cols = slice(s * n, (s + 1) * n)
            o_ref[:, cols] = jnp.dot(a, w_ref[s], preferred_element_type=F32) + b_ref[:, cols]

    row = pl.BlockSpec((tm, K), lambda i: (i, 0))
    (n_out, out), xo = _call(
        body, name, (T // tm,),
        [row, pl.BlockSpec((1, K), lambda i: (0, 0)), pl.BlockSpec((S, K, n), lambda i: (0, 0, 0)),
         pl.BlockSpec((1, S * n), lambda i: (0, 0))],
        [row, pl.BlockSpec((tm, S * n), lambda i: (i, 0))],
        [jax.ShapeDtypeStruct((T, K), BF16), jax.ShapeDtypeStruct((T, S * n), F32)],
        [h, gain, w, bias], ("parallel",), hosted=hosted)
    return n_out, out, xo


def _load_weights(pairs, sems, S, G, i, p):
    def copies(seg):
        return [pltpu.make_async_copy(src.at[seg], dst.at[seg], sems.at[k, seg]) for k, (src, dst) in enumerate(pairs)]

    @pl.when((i == 0) & (p == 0))
    def _():
        for seg in range(S):
            for cp in copies(seg):
                cp.start()

    @pl.when((i == 0) & (p < S // G))
    def _():
        for j in range(G):
            for cp in copies(G * p + j):
                cp.wait()


def ffn_fwd(h, gain, weights, name, hosted=(), arriving=None):
    T, D = h.shape
    S, f, _ = weights[0].shape
    tm = _tile(T, TOKEN_TILE)
    rc = tm // FFN_ROW_CHUNKS
    chunks = [slice(r * rc, (r + 1) * rc) for r in range(FFN_ROW_CHUNKS)]
    G = FFN_FWD_SEGS_PER_STEP
    weights = list(weights)
    hosted = ([arriving.awaited_first()] if arriving is not None else []) + list(hosted)

    def body(h_ref, gain_ref, *refs):
        nw = len(weights)
        wg_hbm, wu_hbm, wd_hbm = list(refs[:nw]) + list(refs[nw + 9:])
        n_ref, g_ref, u_ref, gu_ref, o_ref, wg_v, wu_v, wd_v, sems = refs[nw:nw + 9]
        i, p = pl.program_id(0), pl.program_id(1)
        _load_weights([(wg_hbm, wg_v), (wu_hbm, wu_v), (wd_hbm, wd_v)], sems, S, G, i, p)

        @pl.when(p == 0)
        def _():
            x = h_ref[...]
            r = lax.rsqrt(jnp.mean(x * x, axis=-1, keepdims=True) + RMS_EPS)
            n_ref[...] = (x * r * gain_ref[...]).astype(n_ref.dtype)

        parts = []
        for rows in chunks:
            a = n_ref[rows, :]
            acc = None
            for j in range(G):
                seg = G * p + j
                g = lax.dot_general(a, wg_v[seg], _NT, preferred_element_type=F32)
                u = lax.dot_general(a, wu_v[seg], _NT, preferred_element_type=F32)
                gu = (g * _sigmoid(g) * u).astype(gu_ref.dtype)
                g_ref[j, rows, :] = g.astype(g_ref.dtype)
                u_ref[j, rows, :] = u.astype(u_ref.dtype)
                gu_ref[j, rows, :] = gu
                part = jnp.dot(gu, wd_v[seg], preferred_element_type=F32)
                acc = part if acc is None else acc + part
            parts.append(acc)

        @pl.when(p == 0)
        def _():
            for rows, part in zip(chunks, parts):
                o_ref[rows, :] = h_ref[rows, :] + part

        @pl.when(p > 0)
        def _():
            for rows, part in zip(chunks, parts):
                o_ref[rows, :] += part

    row = pl.BlockSpec((tm, D), lambda i, p: (i, 0))
    seg = pl.BlockSpec((G, tm, f), lambda i, p: (p, i, 0))
    hbm = pl.BlockSpec(memory_space=pl.ANY)
    segs = jax.ShapeDtypeStruct((S, T, f), BF16)
    outs, xo = _call(
        body, name, (T // tm, S // G),
        [row, pl.BlockSpec((1, D), lambda i, s: (0, 0))] + [hbm] * len(weights), [row, seg, seg, seg, row],
        [jax.ShapeDtypeStruct((T, D), BF16), segs, segs, segs, jax.ShapeDtypeStruct((T, D), F32)],
        [h, gain] + weights, ("arbitrary", "arbitrary"),
        [pltpu.VMEM((S, f, D), BF16), pltpu.VMEM((S, f, D), BF16), pltpu.VMEM((S, f, D), BF16), pltpu.SemaphoreType.DMA((3, S))],
        hosted=hosted)
    return (*outs, xo)


def ffn_bwd(dy, h, gain, g, u, wd, wg, wu, name, hosted=()):
    T, D = h.shape
    S, f, _ = wg.shape
    tm = _tile(T, FFN_BWD_TOKEN_TILE)
    nt = T // tm

    def body(dy_ref, h_ref, gain_ref, g_ref, u_ref, wd_hbm, wg_hbm, wu_hbm, dg_ref, du_ref, dh_ref, dhb_ref, dgain_ref,
             wd_v, wg_v, wu_v, sems):
        i = pl.program_id(0)
        _load_weights([(wd_hbm, wd_v), (wg_hbm, wg_v), (wu_hbm, wu_v)], sems, S, S, i, 0)
        dy_ = dy_ref[...]
        dyb = dy_.astype(BF16)
        dn = None
        for j in range(S):
            dgu = lax.dot_general(dyb, wd_v[j], _NT, preferred_element_type=F32)
            gv = g_ref[j].astype(F32)
            sg = _sigmoid(gv)
            dg = (dgu * u_ref[j].astype(F32) * (sg * (1.0 + gv * (1.0 - sg)))).astype(dg_ref.dtype)
            du = (dgu * (gv * sg)).astype(du_ref.dtype)
            dg_ref[j] = dg
            du_ref[j] = du
            part = jnp.dot(dg, wg_v[j], preferred_element_type=F32) + jnp.dot(du, wu_v[j], preferred_element_type=F32)
            dn = part if dn is None else dn + part
        x = h_ref[...]
        r = lax.rsqrt(jnp.mean(x * x, axis=-1, keepdims=True) + RMS_EPS)
        xhat = x * r
        dxhat = dn * gain_ref[...]
        dh = dy_ + r * (dxhat - xhat * jnp.mean(dxhat * xhat, axis=-1, keepdims=True))
        dh_ref[...] = dh
        dhb_ref[...] = dh.astype(dhb_ref.dtype)
        pg = jnp.sum(dn * xhat, axis=0, keepdims=True)

        @pl.when(i == 0)
        def _():
            dgain_ref[...] = pg

        @pl.when(i > 0)
        def _():
            dgain_ref[...] += pg

    row = pl.BlockSpec((tm, D), lambda i: (i, 0))
    vec = pl.BlockSpec((1, D), lambda i: (0, 0))
    seg = pl.BlockSpec((S, tm, f), lambda i: (0, i, 0))
    hbm = pl.BlockSpec(memory_space=pl.ANY)
    segs = jax.ShapeDtypeStruct((S, T, f), BF16)
    outs, xo = _call(
        body, name, (nt,),
        [row, row, vec, seg, seg, hbm, hbm, hbm], [seg, seg, row, row, vec],
        [segs, segs, jax.ShapeDtypeStruct((T, D), F32), jax.ShapeDtypeStruct((T, D), BF16), jax.ShapeDtypeStruct((1, D), F32)],
        [dy, h, gain, g, u, wd, wg, wu], ("arbitrary",),
        [pltpu.VMEM((S, f, D), BF16), pltpu.VMEM((S, f, D), BF16), pltpu.VMEM((S, f, D), BF16),
         pltpu.SemaphoreType.DMA((3, S))], hosted=hosted)
    return (*outs, xo)


_NT = (((1,), (1,)), ((), ()))
_TN = (((0,), (0,)), ((), ()))


def nt_cols_rms(dy, w, h, gain, dres, name, hosted=(), also_bf16=False, batched=False):
    T, K = h.shape
    S, _, n = w.shape
    tm = _tile(T, TOKEN_TILE)

    def body(dy_ref, w_ref, h_ref, gain_ref, dres_ref, dh_ref, dgain_ref, *rest):
        i = pl.program_id(0)
        dn = None
        for s in range(S):
            part = lax.dot_general(dy_ref[:, s * n:(s + 1) * n], w_ref[s], _NT, preferred_element_type=F32)
            dn = part if dn is None else dn + part
        x = h_ref[...]
        r = lax.rsqrt(jnp.mean(x * x, axis=-1, keepdims=True) + RMS_EPS)
        xhat = x * r
        dxhat = dn * gain_ref[...]
        dh = dres_ref[...] + r * (dxhat - xhat * jnp.mean(dxhat * xhat, axis=-1, keepdims=True))
        dh_ref[...] = dh
        if also_bf16:
            rest[0][...] = dh.astype(BF16)
        pg = jnp.sum(dn * xhat, axis=0, keepdims=True)

        @pl.when(i == 0)
        def _():
            dgain_ref[...] = pg

        @pl.when(i > 0)
        def _():
            dgain_ref[...] += pg

    row = pl.BlockSpec((tm, K), lambda i: (i, 0))
    vec = pl.BlockSpec((1, K), lambda i: (0, 0))
    out_specs, out_shape = [row, vec], [jax.ShapeDtypeStruct((T, K), F32), jax.ShapeDtypeStruct((1, K), F32)]
    if batched:
        out_specs[0], out_shape[0] = pl.BlockSpec((None, tm, K), lambda i: (0, i, 0)), jax.ShapeDtypeStruct((1, T, K), F32)
    if also_bf16:
        out_specs, out_shape = out_specs + [row], out_shape + [jax.ShapeDtypeStruct((T, K), BF16)]
    outs, xo = _call(
        body, name, (T // tm,),
        [pl.BlockSpec((tm, S * n), lambda i: (i, 0)), pl.BlockSpec((S, K, n), lambda i: (0, 0, 0)), row, vec, row],
        out_specs, out_shape, [dy, w, h, gain, dres], ("arbitrary",), hosted=hosted)
    return (*outs, xo)


def tn_grad(a, dy, S, a_by_seg, name, hosted=()):
    T = dy.shape[0]
    tt = _tile(T, GRAD_TOKEN_TILE)
    G = GRAD_SEGS_PER_STEP
    if a_by_seg:
        R, C = a.shape[2], dy.shape[1]
        a_spec = pl.BlockSpec((G, tt, R), lambda p, t: (p, t, 0))
        b_spec = pl.BlockSpec((tt, C), lambda p, t: (t, 0))
    else:
        R, C = a.shape[1], dy.shape[1] // S
        a_spec = pl.BlockSpec((tt, R), lambda p, t: (t, 0))
        b_spec = pl.BlockSpec((tt, G * C), lambda p, t: (t, p))
    Rh = R // 2
    nt = T // tt

    def body(a_ref, b_ref, o_ref, acc_ref):
        t = pl.program_id(1)
        parts = []
        for j in range(G):
            a_j = a_ref[j] if a_by_seg else a_ref[...]
            b_j = b_ref[...] if a_by_seg else b_ref[:, j * C:(j + 1) * C]
            parts.append(lax.dot_general(a_j, b_j.astype(BF16), _TN, preferred_element_type=F32))

        @pl.when(t == 0)
        def _():
            for j in range(G):
                acc_ref[j] = parts[j]

        @pl.when(t > 0)
        def _():
            for j in range(G):
                acc_ref[j] += parts[j]

        @pl.when(t == nt - 1)
        def _():
            for j in range(G):
                o_ref[0, j] = acc_ref[j, :Rh, :].astype(o_ref.dtype)
                o_ref[1, j] = acc_ref[j, Rh:, :].astype(o_ref.dtype)

    (gh,), xo = _call(
        body, name, (S // G, nt), [a_spec, b_spec], [pl.BlockSpec((2, G, Rh, C), lambda p, t: (0, p, 0, 0))],
        [jax.ShapeDtypeStruct((2, S, Rh, C), BF16)], [a, dy], ("parallel", "arbitrary"), [pltpu.VMEM((G, R, C), F32)],
        hosted=hosted)
    return gh, xo


def tn_grad_square(a, dy, S, name, hosted=()):
    T, K = a.shape
    N = dy.shape[1]
    tt = _tile(T, GRAD_TOKEN_TILE)
    nt = T // tt
    Rh = K // S // 2

    def body(a_ref, b_ref, o_ref, acc_ref):
        t = pl.program_id(0)
        part = lax.dot_general(a_ref[...], b_ref[...].astype(BF16), _TN, preferred_element_type=F32)

        @pl.when(t == 0)
        def _():
            acc_ref[...] = part

        @pl.when(t > 0)
        def _():
            acc_ref[...] += part

        @pl.when(t == nt - 1)
        def _():
            for s in range(S):
                for hf in range(2):
                    r0 = (2 * s + hf) * Rh
                    o_ref[hf, s] = acc_ref[r0:r0 + Rh, :].astype(o_ref.dtype)

    (gh,), xo = _call(
        body, name, (nt,), [pl.BlockSpec((tt, K), lambda t: (t, 0)), pl.BlockSpec((tt, N), lambda t: (t, 0))],
        [pl.BlockSpec((2, S, Rh, N), lambda t: (0, 0, 0, 0))], [jax.ShapeDtypeStruct((2, S, Rh, N), BF16)],
        [a, dy], ("arbitrary",), [pltpu.VMEM((K, N), F32)], hosted=hosted)
    return gh, xo


def _place():
    x, y, c = lax.axis_index("x"), lax.axis_index("y"), lax.axis_index("c")
    chips = [(1 - x, y), (x, 1 - y), (1 - x, 1 - y)]
    return x, y, c, chips


def _remote(src, dst, send_sem, recv_sem, dev):
    return pltpu.make_async_remote_copy(src_ref=src, dst_ref=dst, send_sem=send_sem, recv_sem=recv_sem,
                                        device_id=dev, device_id_type=MESH)


def small_allreduce(v, name, hosted=()):
    rows, W = v.shape

    def body(v_ref, o_ref, sib_ref, pair_ref, chips_ref, send_sems, recv_sems):
        x, y, c, chips = _place()
        me = 2 * x + y
        swap = _remote(v_ref, sib_ref, send_sems.at[3], recv_sems.at[3], (x, y, 1 - c))
        swap.start()
        swap.wait()
        mine, other = v_ref[...], sib_ref[...]
        pair_ref[...] = jnp.where(c == 0, mine, other) + jnp.where(c == 0, other, mine)
        sends = []
        for j, (px, py) in enumerate(chips):
            cp = _remote(pair_ref, chips_ref.at[me], send_sems.at[j], recv_sems.at[j], (px, py, c))
            cp.start()
            sends.append(cp)
        chips_ref[me] = pair_ref[...]
        for j, (px, py) in enumerate(chips):
            blk = chips_ref.at[2 * px + py]
            _remote(blk, blk, send_sems.at[j], recv_sems.at[j], (px, py, c)).wait_recv()
        for cp in sends:
            cp.wait_send()
        o_ref[...] = (chips_ref[0] + chips_ref[1]) + (chips_ref[2] + chips_ref[3])

    vm = pl.BlockSpec(memory_space=pltpu.VMEM)
    (out,), xo = _call(
        body, name, (), [vm], [vm], [jax.ShapeDtypeStruct((rows, W), F32)], [v], (),
        [pltpu.VMEM((rows, W), F32), pltpu.VMEM((rows, W), F32), pltpu.VMEM((N_CHIPS, rows, W), F32),
         pltpu.SemaphoreType.DMA((4,)), pltpu.SemaphoreType.DMA((4,))], hosted=hosted)
    return out, xo


def _gather_p1_copies(srcs, bufs, ssem, rsem, base):
    x, y, c, chips = _place()
    me, sib = 2 * x + y, (x, y, 1 - c)
    sends, recvs = [], []
    for k, (src, buf) in enumerate(zip(srcs, bufs)):
        rh = src.shape[0] // 2
        s0 = base + 4 * k
        sends.append(_remote(src, buf.at[me], ssem.at[s0 + 3], rsem.at[s0 + 3], sib))
        recvs.append(_remote(buf.at[me], buf.at[me], ssem.at[s0 + 3], rsem.at[s0 + 3], sib))
        for j, (px, py) in enumerate(chips):
            sends.append(_remote(src.at[pl.ds(c * rh, rh)], buf.at[me, pl.ds(c * rh, rh)], ssem.at[s0 + j], rsem.at[s0 + j], (px, py, c)))
            blk = buf.at[2 * px + py, pl.ds(c * rh, rh)]
            recvs.append(_remote(blk, blk, ssem.at[s0 + j], rsem.at[s0 + j], (px, py, c)))
    return sends, recvs


def _gather_p2_copies(bufs, ssem, rsem, base):
    x, y, c, chips = _place()
    sib = (x, y, 1 - c)
    sends, recvs = [], []
    for k, buf in enumerate(bufs):
        rh = buf.shape[1] // 2
        for j, (px, py) in enumerate(chips):
            s0 = base + 3 * k + j
            blk = buf.at[2 * px + py, pl.ds(c * rh, rh)]
            sends.append(_remote(blk, blk, ssem.at[s0], rsem.at[s0], sib))
            got = buf.at[2 * px + py, pl.ds((1 - c) * rh, rh)]
            recvs.append(_remote(got, got, ssem.at[s0], rsem.at[s0], sib))
    return sends, recvs


def _gathered_shape(s):
    return jax.ShapeDtypeStruct((N_CHIPS,) + s.shape, s.dtype)


def gather_p1(shards):
    return _Exchange(shards, [_gathered_shape(s) for s in shards], {}, 4 * len(shards),
                     lambda xi, xo, ss, rs: _gather_p1_copies(xi, xo, ss, rs, 0))


def gather_p2(bufs):
    return _Exchange(bufs, [jax.ShapeDtypeStruct(b.shape, b.dtype) for b in bufs], {k: k for k in range(len(bufs))},
                     3 * len(bufs), lambda xi, xo, ss, rs: _gather_p2_copies(xo, ss, rs, 0))


def gather_whole(whole, begun):
    nw, n = len(whole), len(whole) + len(begun)
    shards = list(whole) + list(begun)
    return _Exchange(shards, [_gathered_shape(s) for s in shards], {}, 4 * n + 3 * nw,
                     lambda xi, xo, ss, rs: _gather_p1_copies(xi, xo, ss, rs, 0),
                     then=lambda xi, xo, ss, rs: _gather_p2_copies(xo[:nw], ss, rs, 4 * n))


def gather_small(v):
    def copies(xi, xo, ssem, rsem):
        x, y, c, chips = _place()
        me, sib = 2 * x + y, (x, y, 1 - c)
        sends = [_remote(xi[0], xo[0].at[me], ssem.at[3], rsem.at[3], sib)]
        recvs = [_remote(xo[0].at[me], xo[0].at[me], ssem.at[3], rsem.at[3], sib)]
        for j, (px, py) in enumerate(chips):
            sends.append(_remote(xi[0], xo[0].at[me], ssem.at[j], rsem.at[j], (px, py, c)))
            blk = xo[0].at[2 * px + py]
            recvs.append(_remote(blk, blk, ssem.at[j], rsem.at[j], (px, py, c)))
        return sends, recvs

    return _Exchange([v], [_gathered_shape(v)], {}, 4, copies)


def gather_all(v):
    def copies(xi, xo, ssem, rsem):
        x, y, c, _ = _place()
        sends, recvs = [], []
        for m in range(1, N_DEV):
            px, py, pc = (1 - x) if m & 4 else x, (1 - y) if m & 2 else y, (1 - c) if m & 1 else c
            sends.append(_remote(xi[0], xo[0].at[4 * x + 2 * y + c], ssem.at[m - 1], rsem.at[m - 1], (px, py, pc)))
            blk = xo[0].at[4 * px + 2 * py + pc]
            recvs.append(_remote(blk, blk, ssem.at[m - 1], rsem.at[m - 1], (px, py, pc)))
        return sends, recvs

    return _Exchange([v], [jax.ShapeDtypeStruct((N_DEV,) + v.shape, v.dtype)], {}, N_DEV - 1, copies)


def run_exchanges(exchanges, name):
    return _call(lambda: None, name, (), [], [], [], [], (), hosted=exchanges)[1]


def sibling_halves(grads):
    def copies(xi, xo, ssem, rsem):
        x, y, c, _ = _place()
        sends = [_remote(xi[k].at[1 - c], xo[k], ssem.at[k], rsem.at[k], (x, y, 1 - c)) for k in range(len(grads))]
        return sends, sends

    return _Exchange(grads, [jax.ShapeDtypeStruct(g.shape[1:], g.dtype) for g in grads], {}, len(grads), copies)


def pair_sum(ghs, recvs, cidx, name):
    n = len(ghs)
    S = ghs[0].shape[1]

    def body(c_ref, *refs):
        for k in range(n):
            a_ref, b_ref, o_ref = refs[2 * k], refs[2 * k + 1], refs[2 * n + k]
            o_ref[...] = (a_ref[...].astype(F32) + b_ref[...].astype(F32)).astype(o_ref.dtype)

    in_specs, out_specs, out_shape, args = [], [], [], []
    for gh, recv in zip(ghs, recvs):
        _, _, Rh, C = gh.shape
        in_specs += [pl.BlockSpec((None, None, Rh, C), lambda s, c_ref: (c_ref[0], s, 0, 0)),
                     pl.BlockSpec((None, Rh, C), lambda s, c_ref: (s, 0, 0))]
        out_specs.append(pl.BlockSpec((None, Rh, C), lambda s, c_ref: (s, 0, 0)))
        out_shape.append(jax.ShapeDtypeStruct((S, Rh, C), BF16))
        args += [gh, recv]
    return pl.pallas_call(
        body, name=name, out_shape=out_shape,
        grid_spec=pltpu.PrefetchScalarGridSpec(num_scalar_prefetch=1, grid=(S,), in_specs=in_specs, out_specs=out_specs),
        compiler_params=_params(("parallel",)),
    )(cidx, *args)


def scatter_p1(parts):
    def copies(xi, xo, ssem, rsem):
        x, y, c, chips = _place()
        me, sib = 2 * x + y, (x, y, 1 - c)
        sends, recvs = [], []
        for k in range(len(parts)):
            s0 = 4 * k
            sends.append(_remote(xi[k].at[me], xo[k].at[me, c], ssem.at[s0 + 3], rsem.at[s0 + 3], sib))
            own = xo[k].at[me, 1 - c]
            recvs.append(_remote(own, own, ssem.at[s0 + 3], rsem.at[s0 + 3], sib))
            for j, (px, py) in enumerate(chips):
                sends.append(_remote(xi[k].at[2 * px + py], xo[k].at[me, c], ssem.at[s0 + j], rsem.at[s0 + j], (px, py, c)))
                blk = xo[k].at[2 * px + py, c]
                recvs.append(_remote(blk, blk, ssem.at[s0 + j], rsem.at[s0 + j], (px, py, c)))
        return sends, recvs

    return _Exchange(parts, [jax.ShapeDtypeStruct((p.shape[0], 2) + p.shape[1:], p.dtype) for p in parts], {},
                     4 * len(parts), copies)


def scatter_p2(bufs):
    def copies(xi, xo, ssem, rsem):
        x, y, c, chips = _place()
        sib = (x, y, 1 - c)
        sends, recvs = [], []
        for k in range(len(bufs)):
            for j, (px, py) in enumerate(chips):
                s0 = 3 * k + j
                blk = xo[k].at[2 * px + py, c]
                sends.append(_remote(blk, blk, ssem.at[s0], rsem.at[s0], sib))
                got = xo[k].at[2 * px + py, 1 - c]
                recvs.append(_remote(got, got, ssem.at[s0], rsem.at[s0], sib))
        return sends, recvs

    return _Exchange(bufs, [jax.ShapeDtypeStruct(b.shape, b.dtype) for b in bufs], {k: k for k in range(len(bufs))},
                     3 * len(bufs), copies)


def _adamw_math(w, g, m, v):
    m = ADAM_B1 * m + (1.0 - ADAM_B1) * g
    v = ADAM_B2 * v + (1.0 - ADAM_B2) * (g * g)
    m_hat = m / (1.0 - ADAM_B1 ** ADAM_STEP)
    v_hat = v / (1.0 - ADAM_B2 ** ADAM_STEP)
    delta = -ADAM_LR * (m_hat / (jnp.sqrt(v_hat) + ADAM_EPS) + ADAM_WD * w)
    return delta, m, v


def adamw_reduce(tensors, place, lyr, bases, name):
    n = len(tensors)
    L, R, C = tensors[0][0].shape
    Rh = R // 2
    rb = _tile(Rh, ROW_TILE, 2 * SUBLANES)
    nb = Rh // rb

    def body(place_ref, *refs):
        mine = (place_ref[1] == pl.program_id(0))
        for k in range(n):
            p_ref, b0, b1, b2, b3, w_ref, m_ref, v_ref = refs[8 * k:8 * k + 8]
            go_ref, d_ref, mo_ref, vo_ref = refs[len(refs) - 4 * n + 4 * k:len(refs) - 4 * n + 4 * k + 4]
            g = None
            for p, b in enumerate((b0, b1, b2, b3)):
                val = jnp.where(mine & (place_ref[0] == p), p_ref[...], b[...]).astype(F32)
                g = val if g is None else g + val
            d, mn, vn = _adamw_math(w_ref[...], g, m_ref[...], v_ref[...])
            go_ref[...] = g
            d_ref[...] = d
            mo_ref[...] = mn
            vo_ref[...] = vn

    def buf_spec(p):
        def idx(h, i, pr):
            own = (pr[0] == p) & (pr[1] == h)
            return (p, jnp.where(own, 1 - h, h), i, 0)
        return pl.BlockSpec((None, None, rb, C), idx)

    blk = pl.BlockSpec((None, rb, C), lambda h, i, pr: (lyr, h * nb + i, 0))
    in_specs, args = [], []
    for w, m, v, buf, part in tensors:
        in_specs += [pl.BlockSpec((None, rb, C), lambda h, i, pr: (pr[0], i, 0))] + [buf_spec(p) for p in range(N_CHIPS)] + [blk] * 3
        args += [part, buf, buf, buf, buf, w, m, v]
    aliases = {}
    if bases is not None:
        in_specs += [pl.BlockSpec(memory_space=pl.ANY)] * (4 * n)
        aliases = {len(args) + k: k for k in range(4 * n)}
        args += list(bases)
    shp = jax.ShapeDtypeStruct((L, R, C), F32)
    flat = _call(body, name, (2, nb), in_specs, [blk] * (4 * n), [shp] * (4 * n), args, ("parallel", "parallel"),
                 prefetch=[place], own_aliases=aliases)[0]
    return flat


def small_update(late, early, own, place, entries, loss_row, name):
    ne = len(entries)
    D = late.shape[1]

    def body(place_ref, late_ref, early_ref, own_ref, *refs):
        ins, outs = refs[:3 * ne], refs[3 * ne:]
        ch = place_ref[0]
        me = 2 * place_ref[0] + place_ref[1]

        def early_sum(rs, cs):
            acc = None
            for d in range(N_DEV):
                val = jnp.where(me == d, own_ref[rs, cs], early_ref[d, rs, cs])
                acc = val if acc is None else acc + val
            return acc

        outs[4 * ne][...] = early_sum(slice(loss_row, loss_row + 1), slice(0, LANES))[:, 0:1]
        for e, (source, row0, kind, w, _, _) in enumerate(entries):
            r, width = w.shape[0], w.shape[-1]
            from_late = lambda rs, cs: late_ref[rs, cs]
            gsum = early_sum if source == "early" else from_late

            if kind == "layers":
                for j, (src, rw) in enumerate(row0):
                    gj = (early_sum if src == "early" else from_late)(slice(rw, rw + 1), slice(0, D))
                    at = (slice(j, j + 1), slice(None))
                    d, mn, vn = _adamw_math(ins[3 * e][at], gj, ins[3 * e + 1][at], ins[3 * e + 2][at])
                    outs[4 * e][at] = gj
                    outs[4 * e + 1][at] = d
                    outs[4 * e + 2][at] = mn
                    outs[4 * e + 3][at] = vn
                continue
            if kind == "full":
                g = gsum(slice(row0, row0 + r), slice(0, D))
            elif kind in ("cols", "rows"):
                g = gsum(slice(row0, row0 + r), slice(0, width))
                for q in range(1, N_CHIPS):
                    g = jnp.where(ch == q, gsum(slice(row0, row0 + r), slice(q * width, (q + 1) * width)), g)
            else:
                per_row = D // width
                g = gsum(slice(row0, row0 + 1), slice(0, width))
                for q in range(1, N_CHIPS):
                    rr = row0 + q // per_row
                    cc = (q % per_row) * width
                    g = jnp.where(ch == q, gsum(slice(rr, rr + 1), slice(cc, cc + width)), g)
            for j in ([slice(None)] if kind != "rows" else range(r)):
                gj = g if kind != "rows" else g[j:j + 1, :]
                d, mn, vn = _adamw_math(ins[3 * e][j], gj, ins[3 * e + 1][j], ins[3 * e + 2][j])
                outs[4 * e][j] = gj
                outs[4 * e + 1][j] = d
                outs[4 * e + 2][j] = mn
                outs[4 * e + 3][j] = vn

    vm = pl.BlockSpec(memory_space=pltpu.VMEM)
    args, out_shape = [], []
    for _, _, _, w, m, v in entries:
        args += [w, m, v]
        out_shape += [jax.ShapeDtypeStruct(w.shape, F32)] * 4
    out_shape.append(jax.ShapeDtypeStruct((1, 1), F32))
    return pl.pallas_call(
        body, name=name,
        in_specs=[pl.BlockSpec(memory_space=pltpu.SMEM), vm, vm, vm] + [vm] * (3 * ne),
        out_specs=[vm] * (4 * ne + 1), out_shape=out_shape,
        compiler_params=pltpu.CompilerParams(vmem_limit_bytes=VMEM_LIMIT),
    )(place, late, early, own, *args)


def _pack_rows(items, width, name):
    starts, at = [], 0
    for it in items:
        starts.append(at)
        at += -(-it.shape[0] // SUBLANES) * SUBLANES
    total = at

    def body(*refs):
        o_ref = refs[-1]
        o_ref[...] = jnp.zeros_like(o_ref)
        for it_ref, r0 in zip(refs[:-1], starts):
            if len(it_ref.shape) == 3:
                for j in range(it_ref.shape[0]):
                    o_ref[r0 + j:r0 + j + 1, :] = it_ref[j]
            elif it_ref.shape == (1, 1):
                o_ref[r0:r0 + 1, :] = jnp.broadcast_to(it_ref[...], (1, width))
            else:
                o_ref[r0:r0 + it_ref.shape[0], :] = it_ref[...]

    vm = pl.BlockSpec(memory_space=pltpu.VMEM)
    packed = pl.pallas_call(body, name=name, in_specs=[vm] * len(items), out_specs=vm,
                            out_shape=jax.ShapeDtypeStruct((total, width), F32))(*items)
    return packed, starts


def kernel(x, a_norm, a_w_in, a_conv, a_w_out, b_norm, b_w_pw1, b_b_pw1, b_conv, b_b_conv, b_ln_g, b_ln_b, b_w_pw2, b_b_pw2, ffn_norm, ffn_w_gate, ffn_w_up, ffn_w_down, final_norm, loss_target, m_a_norm, m_a_w_in, m_a_conv, m_a_w_out, m_b_norm, m_b_w_pw1, m_b_b_pw1, m_b_conv, m_b_b_conv, m_b_ln_g, m_b_ln_b, m_b_w_pw2, m_b_b_pw2, m_ffn_norm, m_ffn_w_gate, m_ffn_w_up, m_ffn_w_down, m_final_norm, v_a_norm, v_a_w_in, v_a_conv, v_a_w_out, v_b_norm, v_b_w_pw1, v_b_b_pw1, v_b_conv, v_b_b_conv, v_b_ln_g, v_b_ln_b, v_b_w_pw2, v_b_b_pw2, v_ffn_norm, v_ffn_w_gate, v_ffn_w_up, v_ffn_w_down, v_final_norm):
    T, D = x.shape[1], x.shape[2]
    Dq = D // N_CHIPS
    cx, cy, cc = lax.axis_index("x"), lax.axis_index("y"), lax.axis_index("c")
    chip = (2 * cx + cy).astype(jnp.int32).reshape(1)
    cidx = cc.astype(jnp.int32).reshape(1)
    h0 = x.reshape(T, D)
    tgt = loss_target.reshape(T, D)

    rows3 = lambda t: jnp.swapaxes(t, 0, 1)
    small_shards = [rows3(a_conv), b_norm, b_b_pw1.reshape(2, Dq), rows3(b_conv), b_b_conv, b_ln_g, b_ln_b, b_b_pw2]
    packed, st = _pack_rows(small_shards, Dq, "pack_small")

    tr = lambda t: jnp.swapaxes(t, 1, 2)
    w_gate, m_gate, v_gate = tr(ffn_w_gate), tr(m_ffn_w_gate), tr(v_ffn_w_gate)
    w_up, m_up, v_up = tr(ffn_w_up), tr(m_ffn_w_up), tr(v_ffn_w_up)
    s_in = a_w_in[0].astype(BF16)
    (n0, s_out, s_pw1, s_pw2, *s_ffn), (g_in,) = rms_cast_weights(
        h0, a_norm, [(a_w_out, 0), (b_w_pw1, 0), (b_w_pw2, 0)] + [(t, l) for t in (w_gate, w_up, ffn_w_down) for l in (0, 1)],
        "rms_a_cast_weights", hosted=[gather_whole([s_in], [])])
    s_gate, s_up, s_down = s_ffn[0:2], s_ffn[2:4], s_ffn[4:6]
    bcv, (g_out, gate0, sw) = mm_cols(n0, g_in, "mm_w_in", hosted=[gather_p1([s_out, s_gate[0]]), gather_small(packed)])

    def whole(k, r):
        return jnp.transpose(sw[:, st[k]:st[k] + r, :], (1, 0, 2)).reshape(r, D)

    a_conv_f, b_norm_f = whole(0, 3), whole(1, 1)
    b_b_pw1_f = sw[:, st[2]:st[2] + 2, :].reshape(1, 2 * D)
    b_conv_f, b_b_conv_f, b_ln_g_f, b_ln_b_f, b_b_pw2_f = whole(3, b_conv.shape[1]), whole(4, 1), whole(5, 1), whole(6, 1), whole(7, 1)
    ya, h1, (g_out, up0, down0, gate0) = gateconv_fwd(bcv, a_conv_f, gather_p2([g_out]), h0, "gateconv_fwd",
                                                      hosted=[gather_p1([s_up[0], s_down[0]]), gather_p2([gate0])])
    g_out = g_out.reshape(1, D, D)
    n1, fg0, fu0, gu0, h2, (up0, down0, g_pw1, g_pw2, gate1, up1) = ffn_fwd(
        h1, ffn_norm[0:1], [gate0], "ffn_fwd0", arriving=gather_p2([up0, down0]),
        hosted=[gather_whole([s_pw1, s_pw2], [s_gate[1], s_up[1]])])
    g_pw2 = g_pw2.reshape(1, D, D)
    n2, ub, (down1, gate1, up1) = rms_mm_cols(h2, b_norm_f, g_pw1, b_b_pw1_f, "mm_pw1",
                                              hosted=[gather_p1([s_down[1]]), gather_p2([gate1, up1])])
    cu, sb, h3, (down1,) = bconv_fwd(ub, b_conv_f, b_b_conv_f, b_ln_g_f, b_ln_b_f, g_pw2, b_b_pw2_f, h2, "bconv_fwd",
                                     hosted=[gather_p2([down1])])
    n3, fg1, fu1, gu1, h4, _ = ffn_fwd(h3, ffn_norm[1:2], [gate1, up1, down1], "ffn_fwd1")
    loss_part, dh4, dh4_b, d_final = loss_head(h4, final_norm.reshape(1, D), tgt, "loss_head")

    place = jnp.concatenate([chip, cidx])

    def pair_sums(ghs, from_sib, tags):
        return pair_sum(ghs, from_sib, cidx, "pair_sum_" + "_".join(tags))

    def upd(wmvs, bufs, parts, tag):
        flat = None
        for lyr in range(len(bufs[0])):
            tensors = [(w, m, v, b[lyr], p[lyr]) for (w, m, v), b, p in zip(wmvs, bufs, parts)]
            flat = adamw_reduce(tensors, place, lyr, flat, "adamw_%s%d" % (tag, lyr))
        return [flat[4 * k:4 * k + 4] for k in range(len(wmvs))]

    dg1, du1, dh3, dh3_b, d_fn1, _ = ffn_bwd(dh4, h3, ffn_norm[1:2], fg1, fu1, down1, gate1, up1, "ffn_bwd1")
    gh_down1, _ = tn_grad(gu1, dh4_b, N_CHIPS, True, "tn_down1")
    gh_gate1, _ = tn_grad(dg1, n3, N_CHIPS, True, "tn_gate1")
    gh_up1, _ = tn_grad(du1, n3, N_CHIPS, True, "tn_up1")
    f1 = [gh_gate1, gh_up1, gh_down1]

    dcu, d_ln_g, d_ln_b, d_b_conv, d_b_pw2, sib_f1 = pw2_ln_bwd(dh3, g_pw2, cu, b_ln_g_f, b_ln_b_f, "pw2_ln_bwd",
                                                                hosted=[sibling_halves(f1)])
    p_f1 = pair_sums(f1, sib_f1, ["gate1", "up1", "down1"])
    gh_pw2, _ = tn_grad_square(sb, dh3_b, N_CHIPS, "tn_pw2")
    dub, d_bconv_w, d_b_pw1, buf_f1 = bconv_bwd(dcu, ub, b_conv_f, "bconv_bwd", hosted=[scatter_p1(p_f1)])
    gh_pw1, _ = tn_grad(n2, dub, N_CHIPS, False, "tn_pw1")
    b_grp = [gh_pw1, gh_pw2]
    dh2, d_b_norm, dh2_b, (*buf_f1, sib_pw1, sib_pw2) = nt_cols_rms(
        dub, g_pw1, h2, b_norm_f, dh3, "nt_pw1", hosted=[scatter_p2(buf_f1), sibling_halves(b_grp)], also_bf16=True)
    sib_b = [sib_pw1, sib_pw2]
    p_b = pair_sums(b_grp, sib_b, ["pw1", "pw2"])

    early_grads = [d_b_norm, d_b_pw1.reshape(2, D), d_bconv_w, d_b_conv, d_ln_g, d_ln_b, d_b_pw2, d_fn1, d_final, loss_part]
    epacked, es = _pack_rows(early_grads, D, "pack_small_grads_early")
    dg0, du0, dh1, dh1_b, d_fn0, (*buf_b, eall) = ffn_bwd(dh2, h1, ffn_norm[0:1], fg0, fu0, down0, gate0, up0, "ffn_bwd0",
                                                         hosted=[scatter_p1(p_b), gather_all(epacked)])
    gh_down0, _ = tn_grad(gu0, dh2_b, N_CHIPS, True, "tn_down0")
    gh_gate0, (*buf_b, sib_down0) = tn_grad(dg0, n1, N_CHIPS, True, "tn_gate0",
                                            hosted=[scatter_p2(buf_b), sibling_halves([gh_down0])])
    p_down0 = pair_sums([gh_down0], [sib_down0], ["down0"])
    gh_up0, (buf_down0, sib_gate0) = tn_grad(du0, n1, N_CHIPS, True, "tn_up0",
                                             hosted=[scatter_p1(p_down0), sibling_halves([gh_gate0])])
    p_gate0 = pair_sums([gh_gate0], [sib_gate0], ["gate0"])
    gh_out, (buf_down0, sib_up0) = tn_grad_square(ya, dh1_b, N_CHIPS, "tn_w_out",
                                                  hosted=[scatter_p2([buf_down0]), sibling_halves([gh_up0])])
    p_up0 = pair_sums([gh_up0], [sib_up0], ["up0"])
    dbcv, d_aconv_w, (buf_gate0, sib_out) = gateconv_bwd(dh1_b, g_out, bcv, a_conv_f, "gateconv_bwd",
                                                         hosted=[scatter_p1(p_gate0), sibling_halves([gh_out])])
    p_out = pair_sums([gh_out], [sib_out], ["out"])
    gh_in, (buf_up0, buf_out, buf_gate0) = tn_grad(n0, dbcv, N_CHIPS, False, "tn_w_in",
                                                   hosted=[scatter_p1(p_up0 + p_out), scatter_p2([buf_gate0])])
    sib_in = run_exchanges([sibling_halves([gh_in])], "reduce_in_siblings")
    p_in = pair_sums([gh_in], sib_in, ["in"])
    grad_x, d_a_norm, (buf_in, buf_up0, buf_out) = nt_cols_rms(
        dbcv, g_in, h0, a_norm, dh1, "nt_w_in", hosted=[scatter_p1(p_in), scatter_p2([buf_up0, buf_out])], batched=True)
    p_f0 = [p_gate0[0], p_up0[0], p_down0[0]]

    lpacked, ls = _pack_rows([d_a_norm, d_aconv_w, d_fn0], D, "pack_small_grads_late")
    lall, (buf_in,) = small_allreduce(lpacked, "allreduce_small_grads", hosted=[scatter_p2([buf_in])])
    buf_a, p_a = [buf_in, buf_out], [p_in[0], p_out[0]]

    r_gate, r_up, r_down = upd([(w_gate, m_gate, v_gate), (w_up, m_up, v_up), (ffn_w_down, m_ffn_w_down, v_ffn_w_down)],
                               [[buf_gate0, buf_f1[0]], [buf_up0, buf_f1[1]], [buf_down0, buf_f1[2]]],
                               [[p_f0[0], p_f1[0]], [p_f0[1], p_f1[1]], [p_f0[2], p_f1[2]]], "ffn")
    r_gate, r_up = [tr(t) for t in r_gate], [tr(t) for t in r_up]
    (r_pw1,) = upd([(b_w_pw1, m_b_w_pw1, v_b_w_pw1)], [[buf_b[0]]], [[p_b[0]]], "pw1")
    r_pw2, r_out = upd([(b_w_pw2, m_b_w_pw2, v_b_w_pw2), (a_w_out, m_a_w_out, v_a_w_out)],
                       [[buf_b[1]], [buf_a[1]]], [[p_b[1]], [p_a[1]]], "pw2_out")
    (r_in,) = upd([(a_w_in, m_a_w_in, v_a_w_in)], [[buf_a[0]]], [[p_a[0]]], "w_in")
    entries = [
        ("late", ls[0], "full", a_norm, m_a_norm, v_a_norm),
        ("late", ls[1], "rows", rows3(a_conv), rows3(m_a_conv), rows3(v_a_conv)),
        ("early", es[0], "cols", b_norm, m_b_norm, v_b_norm),
        ("early", es[1], "flat2", b_b_pw1, m_b_b_pw1, v_b_b_pw1),
        ("early", es[2], "rows", rows3(b_conv), rows3(m_b_conv), rows3(v_b_conv)),
        ("early", es[3], "cols", b_b_conv, m_b_b_conv, v_b_b_conv),
        ("early", es[4], "cols", b_ln_g, m_b_ln_g, v_b_ln_g),
        ("early", es[5], "cols", b_ln_b, m_b_ln_b, v_b_ln_b),
        ("early", es[6], "cols", b_b_pw2, m_b_b_pw2, v_b_b_pw2),
        (None, [("late", ls[2]), ("early", es[7])], "layers", ffn_norm, m_ffn_norm, v_ffn_norm),
        ("early", es[8], "full", final_norm.reshape(1, D), m_final_norm.reshape(1, D), v_final_norm.reshape(1, D)),
    ]
    so = small_update(lall, eall, epacked, place, entries, es[9], "small_update")
    sm = [so[4 * e:4 * e + 4] for e in range(len(entries))]

    def shaped(e, like):
        return [t.reshape(like.shape) for t in sm[e]]

    r_a_norm, r_a_conv, r_b_norm, r_b_b_pw1 = shaped(0, a_norm), shaped(1, a_conv), shaped(2, b_norm), shaped(3, b_b_pw1)
    r_b_conv, r_b_b_conv, r_b_ln_g, r_b_ln_b = shaped(4, b_conv), shaped(5, b_b_conv), shaped(6, b_ln_g), shaped(7, b_ln_b)
    r_b_b_pw2, r_ffn_norm, r_final = shaped(8, b_b_pw2), sm[9], shaped(10, final_norm)

    loss = so[4 * len(entries)].reshape(())
    order =[r_a_norm, r_in, r_a_conv, r_out, r_b_norm, r_pw1, r_b_b_pw1, r_b_conv, r_b_b_conv, r_b_ln_g, r_b_ln_b,
             r_pw2, r_b_b_pw2, r_ffn_norm, r_gate, r_up, r_down, r_final]
    outs = [loss, grad_x.reshape(x.shape)]
    for field in range(4):
        outs += [r[field] for r in order]
    return tuple(outs)
```
